```python
import math
import jax, jax.numpy as jnp
from jax import lax
import numpy as np

D_MODEL = 1024
BATCH = 8
SEQ = 2048
DEPTH = 2
DEC_BATCH = 128
DEC_SEQ = 4
PAST_LEN = 16384
PAGE_SIZE = 128

MIX_WIDTH = D_MODEL
GROUP_WIDTH = MIX_WIDTH // 4
SSD_HEAD_DIM = 64
SSD_HEADS = GROUP_WIDTH // SSD_HEAD_DIM
SSD_GROUPS = 2
SSD_STATE = 128
SSD_CONV = 4
SSD_CHUNK = 128
SSD_CONV_DIM = GROUP_WIDTH + 2 * SSD_GROUPS * SSD_STATE
SSD_PROJ = GROUP_WIDTH + SSD_CONV_DIM + SSD_HEADS
RWKV_HEAD = 64
RWKV_HEADS = GROUP_WIDTH // RWKV_HEAD
RWKV_DECAY_LORA = 64
RWKV_ICLR_LORA = 64
RWKV_GATE_LORA = 128
RWKV_PROJ = 3 * GROUP_WIDTH + RWKV_DECAY_LORA + RWKV_ICLR_LORA + RWKV_GATE_LORA
RWKV_LN_EPS = 64e-5
S5_GROUP_CH = 16
S5_GROUPS = GROUP_WIDTH // S5_GROUP_CH
S5_STATE = 64
POOL_WINDOWS = (2, 4, 8, 16)
POOL_GROUPS = len(POOL_WINDOWS)
POOL_CH = GROUP_WIDTH // POOL_GROUPS
POOL_BUF = max(POOL_WINDOWS) - 1
IN_PROJ = SSD_PROJ + RWKV_PROJ + 2 * GROUP_WIDTH
D_FF = 2816
RMS_EPS = 1e-6

kernel_name = 'hymba_ssd_rwkv7_s5_pool_macaron_step'

F32 = jnp.float32


def _split_at(x, sizes):
    idx = [int(s) for s in np.cumsum(sizes)[:-1]]
    return jnp.split(x, idx, axis=-1)


def _rmsnorm(x, g):
    xf = x.astype(F32)
    y = xf * lax.rsqrt(jnp.mean(xf * xf, axis=-1, keepdims=True) + RMS_EPS)
    return (y * g.astype(F32)).astype(x.dtype)


def _swiglu(x, w_in, w_out):
    gate, up = jnp.split(x @ w_in, 2, axis=-1)
    return (jax.nn.silu(gate) * up) @ w_out


def _causal_dwconv(u, buf, w, b):
    k_w = w.shape[0]
    t = u.shape[1]
    full = jnp.concatenate([buf.astype(u.dtype), u], axis=1)
    out = b + full[:, 0:t] * w[0]
    for j in range(1, k_w):
        out = out + full[:, j:j + t] * w[j]
    return out, full[:, -(k_w - 1):]


def _ssd_scan(x, dt, a_neg, bm, cm, h0):
    bsz, t, nh, hp = x.shape
    n = bm.shape[-1]
    L = min(SSD_CHUNK, t)
    nc = -(-t // L)
    pad = nc * L - t
    if pad:
        padf = lambda z: jnp.pad(z, [(0, 0), (0, pad)] + [(0, 0)] * (z.ndim - 2))
        x, dt, bm, cm = padf(x), padf(dt), padf(bm), padf(cm)
    x = x.reshape(bsz, nc, L, nh, hp)
    dt = dt.reshape(bsz, nc, L, nh)
    bm = bm.reshape(bsz, nc, L, nh, n)
    cm = cm.reshape(bsz, nc, L, nh, n)
    acs = jnp.cumsum(dt * a_neg, axis=2)
    seg = acs[:, :, :, None, :] - acs[:, :, None, :, :]
    causal = jnp.tril(jnp.ones((L, L), dtype=bool))[None, None, :, :, None]
    decay = jnp.exp(jnp.where(causal, seg, -jnp.inf))
    xdt = x * dt[..., None]
    scores = jnp.einsum('bcihn,bcjhn->bcijh', cm, bm) * decay
    y_diag = jnp.einsum('bcijh,bcjhp->bcihp', scores, xdt)
    decay_end = jnp.exp(acs[:, :, -1:, :] - acs)
    chunk_states = jnp.einsum('bclhn,bclh,bclhp->bchpn', bm, decay_end, xdt)
    chunk_decay = jnp.exp(acs[:, :, -1, :])

    def step(h, inp):
        s_c, d_c = inp
        return h * d_c[:, :, None, None] + s_c, h

    h_last, h_prev = lax.scan(step, h0, (jnp.moveaxis(chunk_states, 1, 0), jnp.moveaxis(chunk_decay, 1, 0)))
    h_prev = jnp.moveaxis(h_prev, 0, 1)
    y_off = jnp.einsum('bclhn,bchpn->bclhp', cm, h_prev) * jnp.exp(acs)[..., None]
    y = (y_diag + y_off).reshape(bsz, nc * L, nh, hp)[:, :t]
    return y, h_last


def _ssd_mixer(u, conv_buf, h0, conv_w, conv_b, dt_bias, a_log, d_skip, norm_g):
    bsz, t, _ = u.shape
    z, xbc, dt_raw = _split_at(u, (GROUP_WIDTH, SSD_CONV_DIM, SSD_HEADS))
    xbc, new_buf = _causal_dwconv(xbc, conv_buf, conv_w, conv_b)
    xbc = jax.nn.silu(xbc.astype(F32))
    xs, bm, cm = _split_at(xbc, (GROUP_WIDTH, SSD_GROUPS * SSD_STATE, SSD_GROUPS * SSD_STATE))
    rep = SSD_HEADS // SSD_GROUPS
    xs = xs.reshape(bsz, t, SSD_HEADS, SSD_HEAD_DIM)
    bm = jnp.repeat(bm.reshape(bsz, t, SSD_GROUPS, SSD_STATE), rep, axis=2)
    cm = jnp.repeat(cm.reshape(bsz, t, SSD_GROUPS, SSD_STATE), rep, axis=2)
    dt = jax.nn.softplus(dt_raw.astype(F32) + dt_bias.astype(F32))
    a_neg = -jnp.exp(a_log.astype(F32))
    y, h_last = _ssd_scan(xs, dt, a_neg, bm, cm, h0.astype(F32))
    y = (y + xs * d_skip.astype(F32)[:, None]).reshape(bsz, t, GROUP_WIDTH)
    y = _rmsnorm(y * jax.nn.silu(z.astype(F32)), norm_g)
    return y.astype(u.dtype), new_buf.astype(conv_buf.dtype), h_last.astype(h0.dtype)


def _rwkv_mixer(u, shift_prev, s0, mu, w0, w2, a0, a2, g2, k_k, k_a, r_k, ln_g, ln_b):
    bsz, t, _ = u.shape
    uf = u.astype(F32)
    prev = jnp.concatenate([shift_prev[:, None, :].astype(F32), uf[:, :-1]], axis=1)
    xs = uf + (prev - uf) * mu.astype(F32)
    r, k, v, wd, ad, gd = _split_at(xs, (GROUP_WIDTH, GROUP_WIDTH, GROUP_WIDTH, RWKV_DECAY_LORA, RWKV_ICLR_LORA, RWKV_GATE_LORA))
    w_raw = -jax.nn.softplus(-(w0.astype(F32) + jnp.tanh(wd) @ w2.astype(F32))) - 0.5
    decay = jnp.exp(-jnp.exp(w_raw))
    a = jax.nn.sigmoid(a0.astype(F32) + ad @ a2.astype(F32))
    g = jax.nn.sigmoid(gd) @ g2.astype(F32)
    hs = lambda z: z.reshape(bsz, t, RWKV_HEADS, RWKV_HEAD)
    r, k, v, decay, a = hs(r), hs(k), hs(v), hs(decay), hs(a)
    kk = k * k_k.astype(F32).reshape(RWKV_HEADS, RWKV_HEAD)
    kk = kk / jnp.maximum(jnp.sqrt(jnp.sum(kk * kk, axis=-1, keepdims=True)), 1e-12)
    k = k * (1.0 + (a - 1.0) * k_a.astype(F32).reshape(RWKV_HEADS, RWKV_HEAD))

    def step(s, inp):
        r_t, w_t, k_t, v_t, kk_t, b_t = inp
        sa = jnp.einsum('bhij,bhj->bhi', s, -kk_t)
        s = s * w_t[:, :, None, :] + v_t[..., None] * k_t[:, :, None, :] + sa[..., None] * b_t[:, :, None, :]
        return s, jnp.einsum('bhij,bhj->bhi', s, r_t)

    tm = lambda z: jnp.moveaxis(z, 1, 0)
    s_last, y = lax.scan(step, s0.astype(F32), (tm(r), tm(decay), tm(k), tm(v), tm(kk), tm(kk * a)))
    y = jnp.moveaxis(y, 0, 1)
    mean = jnp.mean(y, axis=-1, keepdims=True)
    var = jnp.mean(jnp.square(y - mean), axis=-1, keepdims=True)
    y = ((y - mean) * lax.rsqrt(var + RWKV_LN_EPS)).reshape(bsz, t, GROUP_WIDTH)
    y = y * ln_g.astype(F32) + ln_b.astype(F32)
    bonus = jnp.sum(r * k * r_k.astype(F32), axis=-1, keepdims=True) * v
    y = (y + bonus.reshape(bsz, t, GROUP_WIDTH)) * g
    return y.astype(u.dtype), u[:, -1].astype(shift_prev.dtype), s_last.astype(s0.dtype)


def _s5_mixer(u, h0_re, h0_im, lam_re, lam_im, log_step, b_re, b_im, c_re, c_im, d_skip, glu_w, glu_b):
    bsz, t, _ = u.shape
    uf = u.astype(F32).reshape(bsz, t, S5_GROUPS, S5_GROUP_CH)
    lam = lax.complex(lam_re.astype(F32), lam_im.astype(F32))
    delta = jnp.exp(log_step.astype(F32))[:, None]
    a_bar = jnp.exp(lam * delta)
    b_bar = ((a_bar - 1.0) / lam)[..., None] * lax.complex(b_re.astype(F32), b_im.astype(F32))
    bu = jnp.einsum('gnc,btgc->btgn', b_bar, uf.astype(jnp.complex64))
    h0 = lax.complex(h0_re.astype(F32), h0_im.astype(F32))
    bu = bu.at[:, 0].add(a_bar * h0)
    a_seq = jnp.broadcast_to(a_bar, bu.shape)

    def combine(e1, e2):
        a1, b1 = e1
        a2, b2 = e2
        return a1 * a2, a2 * b1 + b2

    _, h = lax.associative_scan(combine, (a_seq, bu), axis=1)
    c_mat = lax.complex(c_re.astype(F32), c_im.astype(F32))
    y = jnp.real(jnp.einsum('gcn,btgn->btgc', c_mat, h)) + uf * d_skip.astype(F32).reshape(S5_GROUPS, S5_GROUP_CH)
    y = jax.nn.gelu(y.reshape(bsz, t, GROUP_WIDTH)).astype(u.dtype)
    ya, yb = jnp.split(y @ glu_w + glu_b, 2, axis=-1)
    out = ya * jax.nn.sigmoid(yb)
    h_last = h[:, -1]
    return out.astype(u.dtype), jnp.real(h_last).astype(h0_re.dtype), jnp.imag(h_last).astype(h0_im.dtype)


def _pool_mixer(u, buf, pos0, pool_w, pool_scale):
    bsz, t, _ = u.shape
    full = jnp.concatenate([buf.astype(u.dtype), u], axis=1)
    cs = jnp.pad(jnp.cumsum(full.astype(F32), axis=1), ((0, 0), (1, 0), (0, 0)))
    pos = (pos0 + jnp.arange(t)).astype(F32)
    outs = []
    for gi, w in enumerate(POOL_WINDOWS):
        sl = slice(gi * POOL_CH, (gi + 1) * POOL_CH)
        end = cs[:, POOL_BUF + 1:POOL_BUF + 1 + t, sl]
        start = cs[:, POOL_BUF + 1 - w:POOL_BUF + 1 - w + t, sl]
        cnt = jnp.minimum(pos + 1.0, float(w))[None, :, None]
        outs.append((end - start) / cnt)
    pooled = jnp.concatenate(outs, axis=-1) - u.astype(F32)
    pooled = pooled.reshape(bsz, t, POOL_GROUPS, POOL_CH)
    y = jnp.einsum('btgc,gcd->btgd', pooled, pool_w.astype(F32)).reshape(bsz, t, GROUP_WIDTH)
    y = y * pool_scale.astype(F32)
    return y.astype(u.dtype), full[:, -POOL_BUF:].astype(buf.dtype)


def _layer(x, pos0, st, p):
    conv_buf, ssd_h, rwkv_shift, rwkv_s, s5_re, s5_im, pool_buf = st
    x = x + 0.5 * _swiglu(_rmsnorm(x, p['norm_ffn1']), p['ffn1_in'], p['ffn1_out'])
    u = _rmsnorm(x, p['norm_mix']) @ p['w_in']
    u_ssd, u_rwkv, u_s5, u_pool = _split_at(u, (SSD_PROJ, RWKV_PROJ, GROUP_WIDTH, GROUP_WIDTH))
    y_ssd, conv_buf, ssd_h = _ssd_mixer(u_ssd, conv_buf, ssd_h, p['ssd_conv_w'], p['ssd_conv_b'], p['ssd_dt_bias'], p['ssd_a_log'], p['ssd_d'], p['ssd_norm'])
    y_rwkv, rwkv_shift, rwkv_s = _rwkv_mixer(u_rwkv, rwkv_shift, rwkv_s, p['rwkv_mu'], p['rwkv_w0'], p['rwkv_w2'], p['rwkv_a0'], p['rwkv_a2'], p['rwkv_g2'], p['rwkv_k_k'], p['rwkv_k_a'], p['rwkv_r_k'], p['rwkv_ln_g'], p['rwkv_ln_b'])
    y_s5, s5_re, s5_im = _s5_mixer(u_s5, s5_re, s5_im, p['s5_lam_re'], p['s5_lam_im'], p['s5_log_step'], p['s5_b_re'], p['s5_b_im'], p['s5_c_re'], p['s5_c_im'], p['s5_d'], p['s5_glu_w'], p['s5_glu_b'])
    y_pool, pool_buf = _pool_mixer(u_pool, pool_buf, pos0, p['pool_w'], p['pool_scale'])
    y = jnp.concatenate([y_ssd, y_rwkv, y_s5, y_pool], axis=-1).astype(x.dtype)
    x = x + y @ p['w_out']
    x = x + 0.5 * _swiglu(_rmsnorm(x, p['norm_ffn2']), p['ffn2_in'], p['ffn2_out'])
    return x, (conv_buf, ssd_h, rwkv_shift, rwkv_s, s5_re, s5_im, pool_buf)


def _trunk(x, pos0, states, params, norm_final):
    per_layer = []
    for l in range(DEPTH):
        pl = {name: arr[l] for name, arr in params.items()}
        x, st = _layer(x, pos0, [s[l] for s in states], pl)
        per_layer.append(st)
    new_states = [jnp.stack([st[i] for st in per_layer]) for i in range(len(states))]
    return _rmsnorm(x, norm_final), new_states


def _normal(k, shape, scale):
    return scale * jax.random.normal(k, shape, F32)


def setup_inputs(seed: int = 0) -> dict:
    key = jax.random.key(seed)
    keys = list(jax.random.split(key, 80))
    nk = keys.pop
    D, GW, L = D_MODEL, GROUP_WIDTH, DEPTH
    ones = lambda shape, s=0.02: 1.0 + _normal(nk(), shape, s)
    dt0 = jnp.exp(jax.random.uniform(nk(), (L, SSD_HEADS), F32, math.log(1e-3), math.log(1e-1)))
    w0_base = -5.5 + 5.0 * jnp.linspace(0.0, 1.0, GW, dtype=F32)
    inp = {
        'x_prompt': _normal(nk(), (BATCH, SEQ, D), 1.0),
        'x_sample': _normal(nk(), (DEC_BATCH, DEC_SEQ, D), 1.0),
        'state_ssd_conv': _normal(nk(), (L, DEC_BATCH, SSD_CONV - 1, SSD_CONV_DIM), 1.0),
        'state_ssd': _normal(nk(), (L, DEC_BATCH, SSD_HEADS, SSD_HEAD_DIM, SSD_STATE), 0.1),
        'state_rwkv_shift': _normal(nk(), (L, DEC_BATCH, RWKV_PROJ), 1.0),
        'state_rwkv': _normal(nk(), (L, DEC_BATCH, RWKV_HEADS, RWKV_HEAD, RWKV_HEAD), 0.3),
        'state_s5_re': _normal(nk(), (L, DEC_BATCH, S5_GROUPS, S5_STATE), 0.3),
        'state_s5_im': _normal(nk(), (L, DEC_BATCH, S5_GROUPS, S5_STATE), 0.3),
        'state_pool': _normal(nk(), (L, DEC_BATCH, POOL_BUF, GW), 1.0),
        'norm_ffn1': ones((L, D)),
        'ffn1_in': _normal(nk(), (L, D, 2 * D_FF), D ** -0.5),
        'ffn1_out': _normal(nk(), (L, D_FF, D), D_FF ** -0.5),
        'norm_mix': ones((L, D)),
        'w_in': _normal(nk(), (L, D, IN_PROJ), D ** -0.5),
        'ssd_conv_w': _normal(nk(), (L, SSD_CONV, SSD_CONV_DIM), SSD_CONV ** -0.5),
        'ssd_conv_b': _normal(nk(), (L, SSD_CONV_DIM), 0.02),
        'ssd_dt_bias': dt0 + jnp.log(-jnp.expm1(-dt0)),
        'ssd_a_log': jnp.log(jax.random.uniform(nk(), (L, SSD_HEADS), F32, 1.0, 16.0)),
        'ssd_d': ones((L, SSD_HEADS)),
        'ssd_norm': ones((L, GW)),
        'rwkv_mu': jax.random.uniform(nk(), (L, RWKV_PROJ), F32),
        'rwkv_w0': w0_base + _normal(nk(), (L, GW), 0.1),
        'rwkv_w2': _normal(nk(), (L, RWKV_DECAY_LORA, GW), 0.1 * RWKV_DECAY_LORA ** -0.5),
        'rwkv_a0': _normal(nk(), (L, GW), 0.1),
        'rwkv_a2': _normal(nk(), (L, RWKV_ICLR_LORA, GW), 0.1 * RWKV_ICLR_LORA ** -0.5),
        'rwkv_g2': _normal(nk(), (L, RWKV_GATE_LORA, GW), RWKV_GATE_LORA ** -0.5),
        'rwkv_k_k': 0.85 + _normal(nk(), (L, GW), 0.02),
        'rwkv_k_a': ones((L, GW)),
        'rwkv_r_k': -0.04 + _normal(nk(), (L, RWKV_HEADS, RWKV_HEAD), 0.1),
        'rwkv_ln_g': ones((L, GW)),
        'rwkv_ln_b': _normal(nk(), (L, GW), 0.02),
        's5_lam_re': -0.5 + _normal(nk(), (L, S5_GROUPS, S5_STATE), 0.01),
        's5_lam_im': math.pi * jnp.arange(S5_STATE, dtype=F32) + _normal(nk(), (L, S5_GROUPS, S5_STATE), 0.01),
        's5_log_step': jax.random.uniform(nk(), (L, S5_GROUPS), F32, math.log(1e-3), math.log(1e-1)),
        's5_b_re': _normal(nk(), (L, S5_GROUPS, S5_STATE, S5_GROUP_CH), (2 * S5_GROUP_CH) ** -0.5),
        's5_b_im': _normal(nk(), (L, S5_GROUPS, S5_STATE, S5_GROUP_CH), (2 * S5_GROUP_CH) ** -0.5),
        's5_c_re': _normal(nk(), (L, S5_GROUPS, S5_GROUP_CH, S5_STATE), (2 * S5_STATE) ** -0.5),
        's5_c_im': _normal(nk(), (L, S5_GROUPS, S5_GROUP_CH, S5_STATE), (2 * S5_STATE) ** -0.5),
        's5_d': _normal(nk(), (L, GW), 1.0),
        's5_glu_w': _normal(nk(), (L, GW, 2 * GW), GW ** -0.5),
        's5_glu_b': _normal(nk(), (L, 2 * GW), 0.02),
        'pool_w': _normal(nk(), (L, POOL_GROUPS, POOL_CH, POOL_CH), POOL_CH ** -0.5),
        'pool_scale': ones((L, GW)),
        'w_out': _normal(nk(), (L, MIX_WIDTH, D), MIX_WIDTH ** -0.5),
        'norm_ffn2': ones((L, D)),
        'ffn2_in': _normal(nk(), (L, D, 2 * D_FF), D ** -0.5),
        'ffn2_out': _normal(nk(), (L, D_FF, D), D_FF ** -0.5),
        'norm_final': ones((D,)),
    }
    return inp


def reference(x_prompt, x_sample, state_ssd_conv, state_ssd, state_rwkv_shift, state_rwkv, state_s5_re, state_s5_im, state_pool,
              norm_ffn1, ffn1_in, ffn1_out, norm_mix, w_in, ssd_conv_w, ssd_conv_b, ssd_dt_bias, ssd_a_log, ssd_d, ssd_norm,
              rwkv_mu, rwkv_w0, rwkv_w2, rwkv_a0, rwkv_a2, rwkv_g2, rwkv_k_k, rwkv_k_a, rwkv_r_k, rwkv_ln_g, rwkv_ln_b,
              s5_lam_re, s5_lam_im, s5_log_step, s5_b_re, s5_b_im, s5_c_re, s5_c_im, s5_d, s5_glu_w, s5_glu_b,
              pool_w, pool_scale, w_out, norm_ffn2, ffn2_in, ffn2_out, norm_final):
    params = {
        'norm_ffn1': norm_ffn1, 'ffn1_in': ffn1_in, 'ffn1_out': ffn1_out, 'norm_mix': norm_mix, 'w_in': w_in,
        'ssd_conv_w': ssd_conv_w, 'ssd_conv_b': ssd_conv_b, 'ssd_dt_bias': ssd_dt_bias, 'ssd_a_log': ssd_a_log,
        'ssd_d': ssd_d, 'ssd_norm': ssd_norm,
        'rwkv_mu': rwkv_mu, 'rwkv_w0': rwkv_w0, 'rwkv_w2': rwkv_w2, 'rwkv_a0': rwkv_a0, 'rwkv_a2': rwkv_a2,
        'rwkv_g2': rwkv_g2, 'rwkv_k_k': rwkv_k_k, 'rwkv_k_a': rwkv_k_a, 'rwkv_r_k': rwkv_r_k,
        'rwkv_ln_g': rwkv_ln_g, 'rwkv_ln_b': rwkv_ln_b,
        's5_lam_re': s5_lam_re, 's5_lam_im': s5_lam_im, 's5_log_step': s5_log_step, 's5_b_re': s5_b_re,
        's5_b_im': s5_b_im, 's5_c_re': s5_c_re, 's5_c_im': s5_c_im, 's5_d': s5_d, 's5_glu_w': s5_glu_w,
        's5_glu_b': s5_glu_b, 'pool_w': pool_w, 'pool_scale': pool_scale, 'w_out': w_out,
        'norm_ffn2': norm_ffn2, 'ffn2_in': ffn2_in, 'ffn2_out': ffn2_out,
    }
    b = x_prompt.shape[0]
    sd = state_ssd.dtype
    prompt_states = [
        jnp.zeros((DEPTH, b, SSD_CONV - 1, SSD_CONV_DIM), sd),
        jnp.zeros((DEPTH, b, SSD_HEADS, SSD_HEAD_DIM, SSD_STATE), sd),
        jnp.zeros((DEPTH, b, RWKV_PROJ), sd),
        jnp.zeros((DEPTH, b, RWKV_HEADS, RWKV_HEAD, RWKV_HEAD), sd),
        jnp.zeros((DEPTH, b, S5_GROUPS, S5_STATE), sd),
        jnp.zeros((DEPTH, b, S5_GROUPS, S5_STATE), sd),
        jnp.zeros((DEPTH, b, POOL_BUF, GROUP_WIDTH), sd),
    ]
    sample_states = [state_ssd_conv, state_ssd, state_rwkv_shift, state_rwkv, state_s5_re, state_s5_im, state_pool]
    y_prompt, (conv_p, ssd_p, shift_p, rwkv_p, s5re_p, s5im_p, pool_p) = _trunk(x_prompt, 0, prompt_states, params, norm_final)
    y_sample, (conv_s, ssd_s, shift_s, rwkv_s, s5re_s, s5im_s, pool_s) = _trunk(x_sample, PAST_LEN, sample_states, params, norm_final)
    return (y_prompt, y_sample, conv_p, conv_s, ssd_p, ssd_s, shift_p, shift_s, rwkv_p, rwkv_s, s5re_p, s5re_s, s5im_p, s5im_s, pool_p, pool_s)
```

```python
import functools
import math

import jax
import jax.numpy as jnp
from jax import lax
from jax.experimental import pallas as pl
from jax.experimental.pallas import tpu as pltpu

F32 = jnp.float32
BF16 = jnp.bfloat16
HIGHEST = lax.Precision.HIGHEST

SUBLANES = 8
LANES = 128
VMEM_LIMIT_BYTES = 56 * 1024 * 1024

GROUP_WIDTH = 256
SSD_HEAD_DIM = 64
SSD_HEADS = 4
SSD_GROUPS = 2
SSD_STATE = 128
SSD_CONV = 4
SSD_CONV_DIM = GROUP_WIDTH + 2 * SSD_GROUPS * SSD_STATE
SSD_CHUNK = 128
RWKV_HEAD = 64
RWKV_HEADS = 4
RWKV_PROJ = 1024
RWKV_LN_EPS = 64e-5
RWKV_CHUNK = 64
S5_GROUP_CH = 16
S5_GROUPS = 16
S5_STATE = 64
POOL_WINDOWS = (2, 4, 8, 16)
POOL_CH = 64
POOL_BUF = 15
RMS_EPS = 1e-6
PAST_LEN = 16384

ROW_TILE = 512
FFN_CHUNK = 256
TM_CHUNK = 64


def _cparams(*sem):
    return pltpu.CompilerParams(dimension_semantics=sem, vmem_limit_bytes=VMEM_LIMIT_BYTES)


def _dot(a, b, **kw):
    return jnp.dot(a, b, preferred_element_type=F32, **kw)


def _dot_nt(a, b):
    return lax.dot_general(a, b, (((1,), (1,)), ((), ())), preferred_element_type=F32)


def _dot_tn(a, b):
    return lax.dot_general(a, b, (((0,), (0,)), ((), ())), preferred_element_type=F32)


def _sigmoid(x):
    return 1.0 / (1.0 + jnp.exp(-x))


def _silu(x):
    return x * _sigmoid(x)


def _softplus(x):
    return jnp.maximum(x, 0.0) + jnp.log(1.0 + jnp.exp(-jnp.abs(x)))


def _gelu_tanh(x):
    c = math.sqrt(2.0 / math.pi)
    return x * (0.5 * (1.0 + jnp.tanh(c * (x + 0.044715 * (x * x * x)))))


def _rms(x, g):
    return x * lax.rsqrt(jnp.mean(x * x, axis=-1, keepdims=True) + RMS_EPS) * g


def _full_spec(shape):
    n = len(shape)
    return pl.BlockSpec(shape, lambda *_: (0,) * n)


def _ffn_body(*refs, has_mix, final_norm):
    it = iter(refs)
    x_ref = next(it)
    x = x_ref[...]
    if has_mix:
        y_refs = [next(it) for _ in range(4)]
        wmix_ref = next(it)
        for i, y_ref in enumerate(y_refs):
            x = x + _dot(y_ref[...].astype(BF16), wmix_ref[i * GROUP_WIDTH:(i + 1) * GROUP_WIDTH, :])
    g_ref, wi_ref, wo_ref = next(it), next(it), next(it)
    gf_ref = next(it) if final_norm else None
    o_ref = next(it)
    h = _rms(x, g_ref[...]).astype(BF16)
    d_ff = wo_ref.shape[0]
    acc = jnp.zeros_like(x)
    for c in range(d_ff // FFN_CHUNK):
        lo = c * FFN_CHUNK
        gate = _dot(h, wi_ref[:, lo:lo + FFN_CHUNK])
        up = _dot(h, wi_ref[:, d_ff + lo:d_ff + lo + FFN_CHUNK])
        act = (_silu(gate) * up).astype(BF16)
        acc = acc + _dot(act, wo_ref[lo:lo + FFN_CHUNK, :])
    x = x + 0.5 * acc
    if final_norm:
        x = _rms(x, gf_ref[...])
    o_ref[...] = x


def _ffn(x, g, wi, wo, mix=None, wmix=None, gf=None):
    rows, d = x.shape
    row_spec = lambda w: pl.BlockSpec((ROW_TILE, w), lambda i: (i, 0))
    args, specs = [x], [row_spec(d)]
    if mix is not None:
        for y in mix:
            args.append(y)
            specs.append(row_spec(y.shape[1]))
        args.append(wmix)
        specs.append(_full_spec(wmix.shape))
    for a in (g, wi, wo) + ((gf,) if gf is not None else ()):
        args.append(a)
        specs.append(_full_spec(a.shape))
    return pl.pallas_call(
        functools.partial(_ffn_body, has_mix=mix is not None, final_norm=gf is not None),
        grid=(rows // ROW_TILE,),
        in_specs=specs,
        out_specs=row_spec(d),
        out_shape=jax.ShapeDtypeStruct((rows, d), F32),
        compiler_params=_cparams("parallel"),
        name="ffn",
    )(*args)


def _inproj_body(x_ref, g_ref, w_ref, *o_refs):
    h = _rms(x_ref[...], g_ref[...]).astype(BF16)
    off = 0
    for o_ref in o_refs:
        n = o_ref.shape[-1]
        o_ref[...] = _dot(h, w_ref[:, off:off + n])
        off += n


def _inproj(x, g, w, widths):
    rows, d = x.shape
    row_spec = lambda w_: pl.BlockSpec((ROW_TILE, w_), lambda i: (i, 0))
    return pl.pallas_call(
        _inproj_body,
        grid=(rows // ROW_TILE,),
        in_specs=[row_spec(d), _full_spec(g.shape), _full_spec(w.shape)],
        out_specs=[row_spec(n) for n in widths],
        out_shape=[jax.ShapeDtypeStruct((rows, n), F32) for n in widths],
        compiler_params=_cparams("parallel"),
        name="inproj",
    )(x, g, w)


def _ssd_body(z_ref, xbc_ref, dt_ref, conv0_ref, h0_ref, cw_ref, cb_ref, dtb_ref, alog_ref, dsk_ref, ng_ref,
              y_ref, hout_ref, xpad_scr, h_scr, *, chunk, t_valid):
    L = chunk
    c = pl.program_id(1)
    pad = SUBLANES

    @pl.when(c == 0)
    def _():
        xpad_scr[pad - (SSD_CONV - 1):pad, :] = conv0_ref[0]
        h_scr[...] = h0_ref[0]

    xpad_scr[pad:pad + L, :] = xbc_ref[...]
    conv = cb_ref[...] + cw_ref[0:1, :] * xpad_scr[pad - 3:pad - 3 + L, :]
    for j in range(1, SSD_CONV):
        conv = conv + cw_ref[j:j + 1, :] * xpad_scr[pad - 3 + j:pad - 3 + j + L, :]
    xpad_scr[pad - (SSD_CONV - 1):pad, :] = xpad_scr[pad + L - (SSD_CONV - 1):pad + L, :]
    conv = _silu(conv)
    xs = conv[:, 0:GROUP_WIDTH]
    bm = conv[:, GROUP_WIDTH:2 * GROUP_WIDTH].astype(BF16)
    cm = conv[:, 2 * GROUP_WIDTH:3 * GROUP_WIDTH].astype(BF16)

    row = lax.broadcasted_iota(jnp.int32, (L, L), 0)
    col = lax.broadcasted_iota(jnp.int32, (L, L), 1)
    causal = row >= col
    dt = _softplus(dt_ref[...] + dtb_ref[...])
    if t_valid < L:
        dt = jnp.where(lax.broadcasted_iota(jnp.int32, dt.shape, 0) < t_valid, dt, 0.0)
    da = dt * (-jnp.exp(alog_ref[...]))
    acs = _dot(jnp.where(causal, 1.0, 0.0).astype(F32), da, precision=HIGHEST)
    acs_t = acs.T
    e_acs = jnp.exp(acs)
    acs_last = acs[L - 1:L, :]
    e_end = jnp.exp(acs_last - acs)
    e_last = jnp.exp(acs_last)

    ys = []
    for h in range(SSD_HEADS):
        g = h // (SSD_HEADS // SSD_GROUPS)
        bg = bm[:, g * SSD_STATE:(g + 1) * SSD_STATE]
        cg = cm[:, g * SSD_STATE:(g + 1) * SSD_STATE]
        x_h = xs[:, h * SSD_HEAD_DIM:(h + 1) * SSD_HEAD_DIM]
        xdt = x_h * dt[:, h:h + 1]
        seg = acs[:, h:h + 1] - acs_t[h:h + 1, :]
        decay = jnp.exp(jnp.where(causal, seg, -jnp.inf))
        scores = _dot_nt(cg, bg) * decay
        y_h = _dot(scores.astype(BF16), xdt.astype(BF16))
        h_prev = h_scr[h]
        y_h = y_h + _dot_nt(cg, h_prev.astype(BF16)) * e_acs[:, h:h + 1]
        st = _dot_tn((xdt * e_end[:, h:h + 1]).astype(BF16), bg)
        h_scr[h] = h_prev * e_last[:, h:h + 1] + st
        ys.append(y_h)
    y = jnp.concatenate(ys, axis=-1) + xs * dsk_ref[...]
    y = y * _silu(z_ref[...])
    y_ref[...] = _rms(y, ng_ref[...])

    @pl.when(c == pl.num_programs(1) - 1)
    def _():
        hout_ref[0] = h_scr[...]


def _ssd(z, xbc, dtr, conv0, h0, cw, cb, dtb, alog, dsk, ng, *, batch, seq, chunk, t_valid, row_off):
    nc = seq // chunk
    blk0 = row_off // chunk
    rspec = lambda w: pl.BlockSpec((chunk, w), lambda b, c: (blk0 + b * nc + c, 0))
    y, h_last = pl.pallas_call(
        functools.partial(_ssd_body, chunk=chunk, t_valid=t_valid),
        grid=(batch, nc),
        in_specs=[rspec(GROUP_WIDTH), rspec(SSD_CONV_DIM), rspec(LANES),
                  pl.BlockSpec((1, SSD_CONV - 1, SSD_CONV_DIM), lambda b, c: (b, 0, 0)),
                  pl.BlockSpec((1, SSD_HEADS, SSD_HEAD_DIM, SSD_STATE), lambda b, c: (b, 0, 0, 0)),
                  _full_spec(cw.shape), _full_spec(cb.shape), _full_spec(dtb.shape), _full_spec(alog.shape),
                  _full_spec(dsk.shape), _full_spec(ng.shape)],
        out_specs=[pl.BlockSpec((chunk, GROUP_WIDTH), lambda b, c: (b * nc + c, 0)),
                   pl.BlockSpec((1, SSD_HEADS, SSD_HEAD_DIM, SSD_STATE), lambda b, c: (b, 0, 0, 0))],
        out_shape=[jax.ShapeDtypeStruct((batch * seq, GROUP_WIDTH), F32),
                   jax.ShapeDtypeStruct((batch, SSD_HEADS, SSD_HEAD_DIM, SSD_STATE), F32)],
        scratch_shapes=[pltpu.VMEM((SUBLANES + chunk, SSD_CONV_DIM), F32),
                        pltpu.VMEM((SSD_HEADS, SSD_HEAD_DIM, SSD_STATE), F32)],
        compiler_params=_cparams("parallel", "arbitrary"),
        name="ssd",
    )(z, xbc, dtr, conv0, h0, cw, cb, dtb, alog, dsk, ng)
    return y, h_last


def _rwkv_body(u_ref, shift0_ref, s0_ref, mu_ref, w0_ref, w2_ref, a0_ref, a2_ref, g2_ref, kk_ref, ka_ref, rk_ref,
               lng_ref, lnb_ref, y_ref, sout_ref, upad_scr, s_scr, *, chunk, t_valid):
    L = chunk
    GW = GROUP_WIDTH
    c = pl.program_id(1)
    pad = SUBLANES

    @pl.when(c == 0)
    def _():
        upad_scr[pad - 1:pad, :] = shift0_ref[0]
        s_scr[...] = s0_ref[0]

    u = u_ref[...]
    upad_scr[pad:pad + L, :] = u
    prev = upad_scr[pad - 1:pad - 1 + L, :]
    upad_scr[pad - 1:pad, :] = u[L - 1:L, :]
    xs = u + (prev - u) * mu_ref[...]
    r = xs[:, 0:GW]
    k = xs[:, GW:2 * GW]
    v = xs[:, 2 * GW:3 * GW]
    wd = xs[:, 3 * GW:3 * GW + 64]
    ad = xs[:, 3 * GW + 64:3 * GW + 128]
    gd = xs[:, 3 * GW + 128:3 * GW + 256]
    w_lin = w0_ref[...] + _dot(jnp.tanh(wd).astype(BF16), w2_ref[...])
    logdecay = -jnp.exp(-_softplus(-w_lin) - 0.5)
    a = _sigmoid(a0_ref[...] + _dot(ad.astype(BF16), a2_ref[...]))
    g = _dot(_sigmoid(gd).astype(BF16), g2_ref[...])
    kk = k * kk_ref[...]
    kk_parts = []
    for h in range(RWKV_HEADS):
        kh = kk[:, h * RWKV_HEAD:(h + 1) * RWKV_HEAD]
        nrm = jnp.sqrt(jnp.sum(kh * kh, axis=-1, keepdims=True))
        kk_parts.append(kh / jnp.maximum(nrm, 1e-12))
    kk = jnp.concatenate(kk_parts, axis=-1)
    k = k * (1.0 + (a - 1.0) * ka_ref[...])
    if t_valid < L:
        live = lax.broadcasted_iota(jnp.int32, (L, GW), 0) < t_valid
        logdecay = jnp.where(live, logdecay, 0.0)
        k = jnp.where(live, k, 0.0)
        v = jnp.where(live, v, 0.0)
        kk = jnp.where(live, kk, 0.0)

    row = lax.broadcasted_iota(jnp.int32, (L, L), 0)
    col = lax.broadcasted_iota(jnp.int32, (L, L), 1)
    incl = row >= col
    strict = row > col
    eye = jnp.where(row == col, 1.0, 0.0).astype(F32)
    cl = _dot(jnp.where(incl, 1.0, 0.0).astype(F32), logdecay, precision=HIGHEST)
    e_in = jnp.exp(cl)
    e_inv = jnp.exp(-cl)
    r_t = (r * e_in).astype(BF16)
    a_t = (-kk * jnp.exp(cl - logdecay)).astype(BF16)
    b_t = (kk * a * e_inv).astype(BF16)
    k_t = (k * e_inv).astype(BF16)
    p_end = e_in[L - 1:L, :]
    vb = v.astype(BF16)

    ys = []
    for h in range(RWKV_HEADS):
        sl = slice(h * RWKV_HEAD, (h + 1) * RWKV_HEAD)
        s0 = s_scr[h]
        s0b = s0.astype(BF16)
        lhs = jnp.concatenate([a_t[:, sl], r_t[:, sl]], axis=0)
        rhs = jnp.concatenate([b_t[:, sl], k_t[:, sl]], axis=0)
        amat = _dot_nt(lhs, rhs)
        n_ab = jnp.where(strict, amat[0:L, 0:L], 0.0)
        n_ak = jnp.where(strict, amat[0:L, L:2 * L], 0.0)
        m_rb = jnp.where(incl, amat[L:2 * L, 0:L], 0.0)
        m_rk = jnp.where(incl, amat[L:2 * L, L:2 * L], 0.0)
        tinv = eye + n_ab
        pw = n_ab
        steps = int(math.log2(L)) - 1
        for _ in range(steps):
            pw = _dot(pw.astype(BF16), pw.astype(BF16))
            tinv = tinv + _dot(tinv.astype(BF16), pw.astype(BF16))
        v_h = vb[:, sl]
        rhs_u = _dot_nt(a_t[:, sl], s0b) + _dot(n_ak.astype(BF16), v_h)
        u_h = _dot(tinv.astype(BF16), rhs_u.astype(BF16)).astype(BF16)
        y_h = _dot_nt(r_t[:, sl], s0b) + _dot(m_rb.astype(BF16), u_h) + _dot(m_rk.astype(BF16), v_h)
        upd = _dot_tn(jnp.concatenate([u_h, v_h], axis=0), rhs)
        s_scr[h] = (s0 + upd) * p_end[:, sl]
        mean = jnp.mean(y_h, axis=-1, keepdims=True)
        var = jnp.mean(jnp.square(y_h - mean), axis=-1, keepdims=True)
        y_h = (y_h - mean) * lax.rsqrt(var + RWKV_LN_EPS)
        ys.append(y_h)
    y = jnp.concatenate(ys, axis=-1) * lng_ref[...] + lnb_ref[...]
    rkr = r * k * rk_ref[...]
    bonus = []
    for h in range(RWKV_HEADS):
        sl = slice(h * RWKV_HEAD, (h + 1) * RWKV_HEAD)
        bonus.append(jnp.sum(rkr[:, sl], axis=-1, keepdims=True) * v[:, sl])
    y_ref[...] = (y + jnp.concatenate(bonus, axis=-1)) * g

    @pl.when(c == pl.num_programs(1) - 1)
    def _():
        sout_ref[0] = s_scr[...]


def _rwkv(u, shift0, s0, p, *, batch, seq, chunk, t_valid, row_off):
    nc = seq // chunk
    blk0 = row_off // chunk
    names = ("mu", "w0", "w2", "a0", "a2", "g2", "k_k", "k_a", "r_k", "ln_g", "ln_b")
    params = [p[n] for n in names]
    y, s_last = pl.pallas_call(
        functools.partial(_rwkv_body, chunk=chunk, t_valid=t_valid),
        grid=(batch, nc),
        in_specs=[pl.BlockSpec((chunk, RWKV_PROJ), lambda b, c: (blk0 + b * nc + c, 0)),
                  pl.BlockSpec((1, 1, RWKV_PROJ), lambda b, c: (b, 0, 0)),
                  pl.BlockSpec((1, RWKV_HEADS, RWKV_HEAD, RWKV_HEAD), lambda b, c: (b, 0, 0, 0))]
                 + [_full_spec(a.shape) for a in params],
        out_specs=[pl.BlockSpec((chunk, GROUP_WIDTH), lambda b, c: (b * nc + c, 0)),
                   pl.BlockSpec((1, RWKV_HEADS, RWKV_HEAD, RWKV_HEAD), lambda b, c: (b, 0, 0, 0))],
        out_shape=[jax.ShapeDtypeStruct((batch * seq, GROUP_WIDTH), F32),
                   jax.ShapeDtypeStruct((batch, RWKV_HEADS, RWKV_HEAD, RWKV_HEAD), F32)],
        scratch_shapes=[pltpu.VMEM((SUBLANES + chunk, RWKV_PROJ), F32),
                        pltpu.VMEM((RWKV_HEADS, RWKV_HEAD, RWKV_HEAD), F32)],
        compiler_params=_cparams("parallel", "arbitrary"),
        name="rwkv",
    )(u, shift0, s0, *params)
    return y, s_last


def _s5_body(u_ref, hre0_ref, him0_ref, are_ref, aim_ref, bmat_ref, cmat_ref, d_ref, gw_ref, gb_ref,
             y_ref, hre_ref, him_ref, hs_scr, *, steps):
    c = pl.program_id(1)
    ns = S5_GROUPS * S5_STATE
    bsub = u_ref.shape[1]

    @pl.when(c == 0)
    def _():
        hre_ref[...] = hre0_ref[...]
        him_ref[...] = him0_ref[...]

    u = u_ref[...].reshape(steps * bsub, GROUP_WIDTH)
    hs_scr[...] = _dot(u.astype(BF16), bmat_ref[...])
    are = jnp.broadcast_to(are_ref[...], (bsub, ns))
    aim = jnp.broadcast_to(aim_ref[...], (bsub, ns))

    def step(t, carry):
        hre, him = carry
        r0 = pl.multiple_of(t * bsub, bsub)
        nre = are * hre - aim * him + hs_scr[pl.ds(r0, bsub), 0:ns]
        nim = are * him + aim * hre + hs_scr[pl.ds(r0, bsub), ns:2 * ns]
        hs_scr[pl.ds(r0, bsub), 0:ns] = nre
        hs_scr[pl.ds(r0, bsub), ns:2 * ns] = nim
        return nre, nim

    hre, him = lax.fori_loop(0, steps, step, (hre_ref[...], him_ref[...]))
    hre_ref[...] = hre
    him_ref[...] = him
    y = _dot(hs_scr[...].astype(BF16), cmat_ref[...]) + u * d_ref[...]
    y = _gelu_tanh(y)
    yy = _dot(y.astype(BF16), gw_ref[...]) + gb_ref[...]
    out = yy[:, 0:GROUP_WIDTH] * _sigmoid(yy[:, GROUP_WIDTH:2 * GROUP_WIDTH])
    y_ref[...] = out.reshape(steps, bsub, GROUP_WIDTH)


def _s5(u_tm, hre0, him0, are, aim, bmat, cmat, d, gw, gb):
    seq, batch, _ = u_tm.shape
    steps = min(TM_CHUNK, seq)
    ns = S5_GROUPS * S5_STATE
    bsub = SUBLANES
    hspec = pl.BlockSpec((bsub, ns), lambda b, c: (b, 0))
    tspec = pl.BlockSpec((steps, bsub, GROUP_WIDTH), lambda b, c: (c, b, 0))
    consts = (are, aim, bmat, cmat, d, gw, gb)
    return pl.pallas_call(
        functools.partial(_s5_body, steps=steps),
        grid=(batch // bsub, seq // steps),
        in_specs=[tspec, hspec, hspec] + [_full_spec(a.shape) for a in consts],
        out_specs=[tspec, hspec, hspec],
        out_shape=[jax.ShapeDtypeStruct((seq, batch, GROUP_WIDTH), F32),
                   jax.ShapeDtypeStruct((batch, ns), F32),
                   jax.ShapeDtypeStruct((batch, ns), F32)],
        scratch_shapes=[pltpu.VMEM((steps * bsub, 2 * ns), F32)],
        compiler_params=_cparams("parallel", "arbitrary"),
        name="s5",
    )(u_tm, hre0, him0, *consts)


def _pool_body(u_ref, buf0_ref, pw_ref, sc_ref, y_ref, f_scr, *, steps, pos0):
    c = pl.program_id(1)
    bsub = u_ref.shape[1]
    halo = POOL_BUF + 1

    @pl.when(c == 0)
    def _():
        f_scr[0:halo] = buf0_ref[...]

    u = u_ref[...]
    f_scr[halo:halo + steps] = u
    f = f_scr[...]
    s2 = f[1:] + f[:-1]
    s4 = s2[2:] + s2[:-2]
    s8 = s4[4:] + s4[:-4]
    s16 = s8[8:] + s8[:-8]
    f_scr[0:halo] = f[steps:steps + halo]
    lane = lax.broadcasted_iota(jnp.int32, (steps, bsub, GROUP_WIDTH), 2)
    tpos = lax.broadcasted_iota(jnp.int32, (steps, bsub, GROUP_WIDTH), 0) + (pos0 + 1) + c * steps
    win = jnp.where(lane < POOL_CH, s2[halo - 1:halo - 1 + steps],
                    jnp.where(lane < 2 * POOL_CH, s4[halo - 3:halo - 3 + steps],
                              jnp.where(lane < 3 * POOL_CH, s8[halo - 7:halo - 7 + steps],
                                        s16[halo - 15:halo - 15 + steps])))
    wlen = jnp.where(lane < POOL_CH, POOL_WINDOWS[0],
                     jnp.where(lane < 2 * POOL_CH, POOL_WINDOWS[1],
                               jnp.where(lane < 3 * POOL_CH, POOL_WINDOWS[2], POOL_WINDOWS[3])))
    cnt = jnp.minimum(tpos, wlen).astype(F32)
    pooled = (win / cnt - u).reshape(steps * bsub, GROUP_WIDTH)
    y = _dot(pooled.astype(BF16), pw_ref[...]) * sc_ref[...]
    y_ref[...] = y.reshape(steps, bsub, GROUP_WIDTH)


def _pool(u_tm, buf0_tm, pw, sc, *, pos0):
    seq, batch, _ = u_tm.shape
    steps = min(TM_CHUNK, seq)
    bsub = SUBLANES
    tspec = pl.BlockSpec((steps, bsub, GROUP_WIDTH), lambda b, c: (c, b, 0))
    return pl.pallas_call(
        functools.partial(_pool_body, steps=steps, pos0=pos0),
        grid=(batch // bsub, seq // steps),
        in_specs=[tspec, pl.BlockSpec((POOL_BUF + 1, bsub, GROUP_WIDTH), lambda b, c: (0, b, 0)),
                  _full_spec(pw.shape), _full_spec(sc.shape)],
        out_specs=tspec,
        out_shape=jax.ShapeDtypeStruct((seq, batch, GROUP_WIDTH), F32),
        scratch_shapes=[pltpu.VMEM((POOL_BUF + 1 + steps, bsub, GROUP_WIDTH), F32)],
        compiler_params=_cparams("parallel", "arbitrary"),
        name="pool",
    )(u_tm, buf0_tm, pw, sc)


def _block_diag(blocks):
    g, r, c = blocks.shape
    eye = jnp.eye(g, dtype=blocks.dtype)
    return (eye[:, None, :, None] * blocks[:, :, None, :]).reshape(g * r, g * c)


def _pad_lanes(v, n=LANES):
    return jnp.pad(v, (0, n - v.shape[0])).reshape(1, n)


def _layer_params(l, P):
    row = lambda a: a.reshape(1, -1)
    w_in = P["w_in"][l]
    z_w, xbc_w, dt_w, rw_w, s5_w, pool_w = jnp.split(
        w_in, [256, 256 + 768, 256 + 768 + 4, 1028 + 1024, 1028 + 1024 + 256], axis=1)
    w_all = jnp.concatenate([z_w, xbc_w, rw_w, s5_w, pool_w, jnp.pad(dt_w, ((0, 0), (0, LANES - SSD_HEADS)))], axis=1)

    lam = lax.complex(P["s5_lam_re"][l], P["s5_lam_im"][l])
    a_bar = jnp.exp(lam * jnp.exp(P["s5_log_step"][l])[:, None])
    b_bar = ((a_bar - 1.0) / lam)[..., None] * lax.complex(P["s5_b_re"][l], P["s5_b_im"][l])
    b_t = jnp.swapaxes(b_bar, 1, 2)
    bmat = jnp.concatenate([_block_diag(jnp.real(b_t)), _block_diag(jnp.imag(b_t))], axis=1)
    c_t = jnp.swapaxes(lax.complex(P["s5_c_re"][l], P["s5_c_im"][l]), 1, 2)
    cmat = jnp.concatenate([_block_diag(jnp.real(c_t)), -_block_diag(jnp.imag(c_t))], axis=0)

    return dict(
        norm_ffn1=row(P["norm_ffn1"][l]), ffn1_in=P["ffn1_in"][l].astype(BF16), ffn1_out=P["ffn1_out"][l].astype(BF16),
        norm_mix=row(P["norm_mix"][l]), w_all=w_all.astype(BF16),
        conv_w=P["ssd_conv_w"][l], conv_b=row(P["ssd_conv_b"][l]),
        dt_bias=_pad_lanes(P["ssd_dt_bias"][l]), a_log=_pad_lanes(P["ssd_a_log"][l]),
        d_skip=row(jnp.repeat(P["ssd_d"][l], SSD_HEAD_DIM)), ssd_norm=row(P["ssd_norm"][l]),
        rwkv=dict(mu=row(P["rwkv_mu"][l]), w0=row(P["rwkv_w0"][l]), w2=P["rwkv_w2"][l].astype(BF16),
                  a0=row(P["rwkv_a0"][l]), a2=P["rwkv_a2"][l].astype(BF16), g2=P["rwkv_g2"][l].astype(BF16),
                  k_k=row(P["rwkv_k_k"][l]), k_a=row(P["rwkv_k_a"][l]), r_k=row(P["rwkv_r_k"][l]),
                  ln_g=row(P["rwkv_ln_g"][l]), ln_b=row(P["rwkv_ln_b"][l])),
        s5_are=row(jnp.real(a_bar)), s5_aim=row(jnp.imag(a_bar)), s5_bmat=bmat.astype(BF16), s5_cmat=cmat.astype(BF16),
        s5_d=row(P["s5_d"][l]), s5_gw=P["s5_glu_w"][l].astype(BF16), s5_gb=row(P["s5_glu_b"][l]),
        pool_w=_block_diag(P["pool_w"][l]).astype(BF16), pool_scale=row(P["pool_scale"][l]),
        w_out=P["w_out"][l].astype(BF16),
        norm_ffn2=row(P["norm_ffn2"][l]), ffn2_in=P["ffn2_in"][l].astype(BF16), ffn2_out=P["ffn2_out"][l].astype(BF16),
    )


def _to_tm(rows, batch, seq):
    return jnp.swapaxes(rows.reshape(batch, seq, rows.shape[-1]), 0, 1)


def _from_tm(x_tm):
    seq, batch, w = x_tm.shape
    return jnp.swapaxes(x_tm, 0, 1).reshape(batch * seq, w)


def _pad_time(rows, batch, seq, seq_pad):
    w = rows.shape[-1]
    return jnp.pad(rows.reshape(batch, seq, w), ((0, 0), (0, seq_pad - seq), (0, 0))).reshape(batch * seq_pad, w)


def _unpad_time(rows, batch, seq, seq_pad):
    w = rows.shape[-1]
    return rows.reshape(batch, seq_pad, w)[:, :seq].reshape(batch * seq, w)


def _mixers(l, lp, proj, states, *, batch, seq, row_off, pos0):
    z, xbc, ur, us5, upool, dtr = proj
    conv0, ssd0, shift0, rwkv0, s5re0, s5im0, pool0 = states
    n = batch * seq
    sl = lambda a: a[row_off:row_off + n]

    if seq % SSD_CHUNK == 0:
        y_ssd, ssd_new = _ssd(z, xbc, dtr, conv0, ssd0, lp["conv_w"], lp["conv_b"], lp["dt_bias"], lp["a_log"],
                              lp["d_skip"], lp["ssd_norm"], batch=batch, seq=seq, chunk=SSD_CHUNK, t_valid=SSD_CHUNK,
                              row_off=row_off)
        y_rwkv, rwkv_new = _rwkv(ur, shift0[:, None, :], rwkv0, lp["rwkv"], batch=batch, seq=seq, chunk=RWKV_CHUNK,
                                 t_valid=RWKV_CHUNK, row_off=row_off)
    else:
        pt = lambda a, L: _pad_time(sl(a), batch, seq, L)
        y_ssd, ssd_new = _ssd(pt(z, SSD_CHUNK), pt(xbc, SSD_CHUNK), pt(dtr, SSD_CHUNK), conv0, ssd0, lp["conv_w"],
                              lp["conv_b"], lp["dt_bias"], lp["a_log"], lp["d_skip"], lp["ssd_norm"],
                              batch=batch, seq=SSD_CHUNK, chunk=SSD_CHUNK, t_valid=seq, row_off=0)
        y_ssd = _unpad_time(y_ssd, batch, seq, SSD_CHUNK)
        y_rwkv, rwkv_new = _rwkv(pt(ur, RWKV_CHUNK), shift0[:, None, :], rwkv0, lp["rwkv"], batch=batch,
                                 seq=RWKV_CHUNK, chunk=RWKV_CHUNK, t_valid=seq, row_off=0)
        y_rwkv = _unpad_time(y_rwkv, batch, seq, RWKV_CHUNK)

    ns = S5_GROUPS * S5_STATE
    y_s5_tm, s5re_new, s5im_new = _s5(_to_tm(sl(us5), batch, seq), s5re0.reshape(batch, ns), s5im0.reshape(batch, ns),
                                      lp["s5_are"], lp["s5_aim"], lp["s5_bmat"], lp["s5_cmat"], lp["s5_d"],
                                      lp["s5_gw"], lp["s5_gb"])
    upool_tm = _to_tm(sl(upool), batch, seq)
    pool0_tm = jnp.pad(jnp.swapaxes(pool0, 0, 1), ((1, 0), (0, 0), (0, 0)))
    y_pool_tm = _pool(upool_tm, pool0_tm, lp["pool_w"], lp["pool_scale"], pos0=pos0)

    tail = lambda old, new, k: jnp.concatenate([old, new.reshape(batch, seq, -1)], axis=1)[:, -k:]
    conv_new = tail(conv0, sl(xbc), SSD_CONV - 1)
    shift_new = sl(ur).reshape(batch, seq, -1)[:, -1]
    pool_new = tail(pool0, sl(upool), POOL_BUF)
    ys = (y_ssd, y_rwkv, _from_tm(y_s5_tm), _from_tm(y_pool_tm))
    new_states = (conv_new, ssd_new, shift_new, rwkv_new,
                  s5re_new.reshape(batch, S5_GROUPS, S5_STATE), s5im_new.reshape(batch, S5_GROUPS, S5_STATE), pool_new)
    return ys, new_states


def kernel(x_prompt, x_sample, state_ssd_conv, state_ssd, state_rwkv_shift, state_rwkv, state_s5_re, state_s5_im, state_pool, norm_ffn1, ffn1_in, ffn1_out, norm_mix, w_in, ssd_conv_w, ssd_conv_b, ssd_dt_bias, ssd_a_log, ssd_d, ssd_norm, rwkv_mu, rwkv_w0, rwkv_w2, rwkv_a0, rwkv_a2, rwkv_g2, rwkv_k_k, rwkv_k_a, rwkv_r_k, rwkv_ln_g, rwkv_ln_b, s5_lam_re, s5_lam_im, s5_log_step, s5_b_re, s5_b_im, s5_c_re, s5_c_im, s5_d, s5_glu_w, s5_glu_b, pool_w, pool_scale, w_out, norm_ffn2, ffn2_in, ffn2_out, norm_final):
    P = dict(norm_ffn1=norm_ffn1, ffn1_in=ffn1_in, ffn1_out=ffn1_out, norm_mix=norm_mix, w_in=w_in,
             ssd_conv_w=ssd_conv_w, ssd_conv_b=ssd_conv_b, ssd_dt_bias=ssd_dt_bias, ssd_a_log=ssd_a_log,
             ssd_d=ssd_d, ssd_norm=ssd_norm, rwkv_mu=rwkv_mu, rwkv_w0=rwkv_w0, rwkv_w2=rwkv_w2, rwkv_a0=rwkv_a0,
             rwkv_a2=rwkv_a2, rwkv_g2=rwkv_g2, rwkv_k_k=rwkv_k_k, rwkv_k_a=rwkv_k_a,
             rwkv_r_k=rwkv_r_k.reshape(rwkv_r_k.shape[0], -1), rwkv_ln_g=rwkv_ln_g, rwkv_ln_b=rwkv_ln_b,
             s5_lam_re=s5_lam_re, s5_lam_im=s5_lam_im, s5_log_step=s5_log_step, s5_b_re=s5_b_re, s5_b_im=s5_b_im,
             s5_c_re=s5_c_re, s5_c_im=s5_c_im, s5_d=s5_d, s5_glu_w=s5_glu_w, s5_glu_b=s5_glu_b, pool_w=pool_w,
             pool_scale=pool_scale, w_out=w_out, norm_ffn2=norm_ffn2, ffn2_in=ffn2_in, ffn2_out=ffn2_out)
    depth = norm_ffn1.shape[0]
    bp, tp, d = x_prompt.shape
    bs, ts, _ = x_sample.shape
    n_p, n_s = bp * tp, bs * ts
    x = jnp.concatenate([x_prompt.reshape(n_p, d), x_sample.reshape(n_s, d)], axis=0)
    sample_states = (state_ssd_conv, state_ssd, state_rwkv_shift, state_rwkv, state_s5_re, state_s5_im, state_pool)
    widths = (GROUP_WIDTH, SSD_CONV_DIM, RWKV_PROJ, GROUP_WIDTH, GROUP_WIDTH, LANES)

    new_p, new_s = [], []
    mix, wmix = None, None
    lp = None
    for l in range(depth):
        if l > 0:
            x = _ffn(x, lp["norm_ffn2"], lp["ffn2_in"], lp["ffn2_out"], mix=mix, wmix=wmix)
        lp = _layer_params(l, P)
        x = _ffn(x, lp["norm_ffn1"], lp["ffn1_in"], lp["ffn1_out"])
        proj = _inproj(x, lp["norm_mix"], lp["w_all"], widths)
        zeros_p = tuple(jnp.zeros((bp,) + s.shape[2:], s.dtype) for s in sample_states)
        ys_p, st_p = _mixers(l, lp, proj, zeros_p, batch=bp, seq=tp, row_off=0, pos0=0)
        ys_s, st_s = _mixers(l, lp, proj, tuple(s[l] for s in sample_states), batch=bs, seq=ts, row_off=n_p,
                             pos0=PAST_LEN)
        mix = tuple(jnp.concatenate([a, b], axis=0) for a, b in zip(ys_p, ys_s))
        wmix = lp["w_out"]
        new_p.append(st_p)
        new_s.append(st_s)
    x = _ffn(x, lp["norm_ffn2"], lp["ffn2_in"], lp["ffn2_out"], mix=mix, wmix=wmix, gf=norm_final.reshape(1, -1))
    y_prompt = x[:n_p].reshape(bp, tp, d)
    y_sample = x[n_p:].reshape(bs, ts, d)
    outs = [y_prompt, y_sample]
    for i in range(7):
        outs.append(jnp.stack([st[i] for st in new_p]))
        outs.append(jnp.stack([st[i] for st in new_s]))
    return tuple(outs)
```

```python
import functools
import math

import jax
import jax.numpy as jnp
from jax import lax
from jax.experimental import pallas as pl
from jax.experimental.pallas import tpu as pltpu

F32 = jnp.float32
BF16 = jnp.bfloat16
HIGHEST = lax.Precision.HIGHEST

SUBLANES = 8
LANES = 128
VMEM_LIMIT_BYTES = 56 * 1024 * 1024

GROUP_WIDTH = 256
SSD_HEAD_DIM = 64
SSD_HEADS = 4
SSD_GROUPS = 2
SSD_STATE = 128
SSD_CONV = 4
SSD_CONV_DIM = GROUP_WIDTH + 2 * SSD_GROUPS * SSD_STATE
SSD_CHUNK = 128
RWKV_HEAD = 64
RWKV_HEADS = 4
RWKV_PROJ = 1024
RWKV_LN_EPS = 64e-5
RWKV_CHUNK = 64
RWKV_GROUP = 4
S5_GROUP_CH = 16
S5_GROUPS = 16
S5_STATE = 64
POOL_WINDOWS = (2, 4, 8, 16)
POOL_CH = 64
POOL_BUF = 15
RMS_EPS = 1e-6
PAST_LEN = 16384

ROW_TILE = 512
FFN_CHUNK = 256
TM_CHUNK = 64


def _cparams(*sem):
    return pltpu.CompilerParams(dimension_semantics=sem, vmem_limit_bytes=VMEM_LIMIT_BYTES)


def _dot(a, b, **kw):
    return jnp.dot(a, b, preferred_element_type=F32, **kw)


def _dot_nt(a, b):
    return lax.dot_general(a, b, (((1,), (1,)), ((), ())), preferred_element_type=F32)


def _dot_tn(a, b):
    return lax.dot_general(a, b, (((0,), (0,)), ((), ())), preferred_element_type=F32)


def _sigmoid(x):
    return 1.0 / (1.0 + jnp.exp(-x))


def _silu(x):
    return x * _sigmoid(x)


def _softplus(x):
    return jnp.maximum(x, 0.0) + jnp.log(1.0 + jnp.exp(-jnp.abs(x)))


def _gelu_tanh(x):
    c = math.sqrt(2.0 / math.pi)
    return x * (0.5 * (1.0 + jnp.tanh(c * (x + 0.044715 * (x * x * x)))))


def _rms(x, g):
    return x * lax.rsqrt(jnp.mean(x * x, axis=-1, keepdims=True) + RMS_EPS) * g


def _full_spec(shape):
    n = len(shape)
    return pl.BlockSpec(shape, lambda *_: (0,) * n)


def _ffn_body(*refs, has_mix, final_norm):
    it = iter(refs)
    x_ref = next(it)
    x = x_ref[...]
    if has_mix:
        y_refs = [next(it) for _ in range(4)]
        wmix_ref = next(it)
        for i, y_ref in enumerate(y_refs):
            x = x + _dot(y_ref[...].astype(BF16), wmix_ref[i * GROUP_WIDTH:(i + 1) * GROUP_WIDTH, :])
    g_ref, wi_ref, wo_ref = next(it), next(it), next(it)
    gf_ref = next(it) if final_norm else None
    o_ref = next(it)
    h = _rms(x, g_ref[...]).astype(BF16)
    d_ff = wo_ref.shape[0]
    acc = jnp.zeros_like(x)
    for c in range(d_ff // FFN_CHUNK):
        lo = c * FFN_CHUNK
        gate = _dot(h, wi_ref[:, lo:lo + FFN_CHUNK])
        up = _dot(h, wi_ref[:, d_ff + lo:d_ff + lo + FFN_CHUNK])
        act = (_silu(gate) * up).astype(BF16)
        acc = acc + _dot(act, wo_ref[lo:lo + FFN_CHUNK, :])
    x = x + 0.5 * acc
    if final_norm:
        x = _rms(x, gf_ref[...])
    o_ref[...] = x


def _ffn(x, g, wi, wo, mix=None, wmix=None, gf=None):
    rows, d = x.shape
    row_spec = lambda w: pl.BlockSpec((ROW_TILE, w), lambda i: (i, 0))
    args, specs = [x], [row_spec(d)]
    if mix is not None:
        for y in mix:
            args.append(y)
            specs.append(row_spec(y.shape[1]))
        args.append(wmix)
        specs.append(_full_spec(wmix.shape))
    for a in (g, wi, wo) + ((gf,) if gf is not None else ()):
        args.append(a)
        specs.append(_full_spec(a.shape))
    return pl.pallas_call(
        functools.partial(_ffn_body, has_mix=mix is not None, final_norm=gf is not None),
        grid=(rows // ROW_TILE,),
        in_specs=specs,
        out_specs=row_spec(d),
        out_shape=jax.ShapeDtypeStruct((rows, d), F32),
        compiler_params=_cparams("parallel"),
        name="ffn",
    )(*args)


def _inproj_body(x_ref, g_ref, w_ref, *o_refs):
    h = _rms(x_ref[...], g_ref[...]).astype(BF16)
    off = 0
    for o_ref in o_refs:
        n = o_ref.shape[-1]
        o_ref[...] = _dot(h, w_ref[:, off:off + n])
        off += n


def _inproj(x, g, w, widths):
    rows, d = x.shape
    row_spec = lambda w_: pl.BlockSpec((ROW_TILE, w_), lambda i: (i, 0))
    return pl.pallas_call(
        _inproj_body,
        grid=(rows // ROW_TILE,),
        in_specs=[row_spec(d), _full_spec(g.shape), _full_spec(w.shape)],
        out_specs=[row_spec(n) for n in widths],
        out_shape=[jax.ShapeDtypeStruct((rows, n), F32) for n in widths],
        compiler_params=_cparams("parallel"),
        name="inproj",
    )(x, g, w)


def _ssd_body(z_ref, xbc_ref, dt_ref, conv0_ref, h0_ref, cw_ref, cb_ref, dtb_ref, alog_ref, dsk_ref, ng_ref,
              y_ref, hout_ref, xpad_scr, h_scr, *, chunk, t_valid):
    L = chunk
    c = pl.program_id(1)
    pad = SUBLANES

    @pl.when(c == 0)
    def _():
        xpad_scr[pad - (SSD_CONV - 1):pad, :] = conv0_ref[0]
        h_scr[...] = h0_ref[0]

    xpad_scr[pad:pad + L, :] = xbc_ref[...]
    conv = cb_ref[...] + cw_ref[0:1, :] * xpad_scr[pad - 3:pad - 3 + L, :]
    for j in range(1, SSD_CONV):
        conv = conv + cw_ref[j:j + 1, :] * xpad_scr[pad - 3 + j:pad - 3 + j + L, :]
    xpad_scr[pad - (SSD_CONV - 1):pad, :] = xpad_scr[pad + L - (SSD_CONV - 1):pad + L, :]
    conv = _silu(conv)
    xs = conv[:, 0:GROUP_WIDTH]
    bm = conv[:, GROUP_WIDTH:2 * GROUP_WIDTH].astype(BF16)
    cm = conv[:, 2 * GROUP_WIDTH:3 * GROUP_WIDTH].astype(BF16)

    row = lax.broadcasted_iota(jnp.int32, (L, L), 0)
    col = lax.broadcasted_iota(jnp.int32, (L, L), 1)
    causal = row >= col
    dt = _softplus(dt_ref[...] + dtb_ref[...])
    if t_valid < L:
        dt = jnp.where(lax.broadcasted_iota(jnp.int32, dt.shape, 0) < t_valid, dt, 0.0)
    da = dt * (-jnp.exp(alog_ref[...]))
    acs = _dot(jnp.where(causal, 1.0, 0.0).astype(F32), da, precision=HIGHEST)
    acs_t = acs.T
    e_acs = jnp.exp(acs)
    acs_last = acs[L - 1:L, :]
    e_end = jnp.exp(acs_last - acs)
    e_last = jnp.exp(acs_last)

    ys = []
    for h in range(SSD_HEADS):
        g = h // (SSD_HEADS // SSD_GROUPS)
        bg = bm[:, g * SSD_STATE:(g + 1) * SSD_STATE]
        cg = cm[:, g * SSD_STATE:(g + 1) * SSD_STATE]
        x_h = xs[:, h * SSD_HEAD_DIM:(h + 1) * SSD_HEAD_DIM]
        xdt = x_h * dt[:, h:h + 1]
        seg = acs[:, h:h + 1] - acs_t[h:h + 1, :]
        decay = jnp.exp(jnp.where(causal, seg, -jnp.inf))
        scores = _dot_nt(cg, bg) * decay
        y_h = _dot(scores.astype(BF16), xdt.astype(BF16))
        h_prev = h_scr[h]
        y_h = y_h + _dot_nt(cg, h_prev.astype(BF16)) * e_acs[:, h:h + 1]
        st = _dot_tn((xdt * e_end[:, h:h + 1]).astype(BF16), bg)
        h_scr[h] = h_prev * e_last[:, h:h + 1] + st
        ys.append(y_h)
    y = jnp.concatenate(ys, axis=-1) + xs * dsk_ref[...]
    y = y * _silu(z_ref[...])
    y_ref[...] = _rms(y, ng_ref[...])

    @pl.when(c == pl.num_programs(1) - 1)
    def _():
        hout_ref[0] = h_scr[...]


def _ssd(z, xbc, dtr, conv0, h0, cw, cb, dtb, alog, dsk, ng, *, batch, seq, chunk, t_valid, row_off):
    nc = seq // chunk
    blk0 = row_off // chunk
    rspec = lambda w: pl.BlockSpec((chunk, w), lambda b, c: (blk0 + b * nc + c, 0))
    y, h_last = pl.pallas_call(
        functools.partial(_ssd_body, chunk=chunk, t_valid=t_valid),
        grid=(batch, nc),
        in_specs=[rspec(GROUP_WIDTH), rspec(SSD_CONV_DIM), rspec(LANES),
                  pl.BlockSpec((1, SSD_CONV - 1, SSD_CONV_DIM), lambda b, c: (b, 0, 0)),
                  pl.BlockSpec((1, SSD_HEADS, SSD_HEAD_DIM, SSD_STATE), lambda b, c: (b, 0, 0, 0)),
                  _full_spec(cw.shape), _full_spec(cb.shape), _full_spec(dtb.shape), _full_spec(alog.shape),
                  _full_spec(dsk.shape), _full_spec(ng.shape)],
        out_specs=[pl.BlockSpec((chunk, GROUP_WIDTH), lambda b, c: (b * nc + c, 0)),
                   pl.BlockSpec((1, SSD_HEADS, SSD_HEAD_DIM, SSD_STATE), lambda b, c: (b, 0, 0, 0))],
        out_shape=[jax.ShapeDtypeStruct((batch * seq, GROUP_WIDTH), F32),
                   jax.ShapeDtypeStruct((batch, SSD_HEADS, SSD_HEAD_DIM, SSD_STATE), F32)],
        scratch_shapes=[pltpu.VMEM((SUBLANES + chunk, SSD_CONV_DIM), F32),
                        pltpu.VMEM((SSD_HEADS, SSD_HEAD_DIM, SSD_STATE), F32)],
        compiler_params=_cparams("parallel", "arbitrary"),
        name="ssd",
    )(z, xbc, dtr, conv0, h0, cw, cb, dtb, alog, dsk, ng)
    return y, h_last


PAIR = 2 * RWKV_HEAD
RWKV_PAIRS = RWKV_HEADS // 2


def _bd(x):
    lane = lax.broadcasted_iota(jnp.int32, x.shape, 1)
    zero = jnp.zeros_like(x)
    return jnp.concatenate([jnp.where(lane < RWKV_HEAD, x, zero), jnp.where(lane >= RWKV_HEAD, x, zero)], axis=0)


def _head_sum(x):
    lo = lax.broadcasted_iota(jnp.int32, (x.shape[0], PAIR), 1) < RWKV_HEAD
    outs = []
    for p in range(RWKV_PAIRS):
        xp = x[:, p * PAIR:(p + 1) * PAIR]
        s_lo = jnp.sum(jnp.where(lo, xp, 0.0), axis=-1, keepdims=True)
        s_hi = jnp.sum(jnp.where(lo, 0.0, xp), axis=-1, keepdims=True)
        outs.append(jnp.where(lo, s_lo, s_hi))
    return jnp.concatenate(outs, axis=-1)


def _rwkv_body(u_ref, shift0_ref, s0_ref, mu_ref, w0_ref, w2_ref, a0_ref, a2_ref, g2_ref, kk_ref, ka_ref, rk_ref,
               lng_ref, lnb_ref, y_ref, sout_ref, upad_scr, s_scr, *, chunk, group, t_valid):
    L, G = chunk, group
    GL = G * L
    GW = GROUP_WIDTH
    c = pl.program_id(1)
    pad = SUBLANES

    @pl.when(c == 0)
    def _():
        upad_scr[pad - 1:pad, :] = shift0_ref[0]
        s_scr[...] = s0_ref[0]

    u = u_ref[...]
    upad_scr[pad:pad + GL, :] = u
    prev = upad_scr[pad - 1:pad - 1 + GL, :]
    upad_scr[pad - 1:pad, :] = u[GL - 1:GL, :]
    xs = u + (prev - u) * mu_ref[...]
    r = xs[:, 0:GW]
    k = xs[:, GW:2 * GW]
    v = xs[:, 2 * GW:3 * GW]
    wd = xs[:, 3 * GW:3 * GW + 64]
    ad = xs[:, 3 * GW + 64:3 * GW + 128]
    gd = xs[:, 3 * GW + 128:3 * GW + 256]
    w_lin = w0_ref[...] + _dot(jnp.tanh(wd).astype(BF16), w2_ref[...])
    logdecay = -jnp.exp(-_softplus(-w_lin) - 0.5)
    a = _sigmoid(a0_ref[...] + _dot(ad.astype(BF16), a2_ref[...]))
    g = _dot(_sigmoid(gd).astype(BF16), g2_ref[...])
    kk = k * kk_ref[...]
    kk = kk / jnp.maximum(jnp.sqrt(_head_sum(kk * kk)), 1e-12)
    k = k * (1.0 + (a - 1.0) * ka_ref[...])
    if t_valid < GL:
        live = lax.broadcasted_iota(jnp.int32, (GL, GW), 0) < t_valid
        logdecay = jnp.where(live, logdecay, 0.0)
        k = jnp.where(live, k, 0.0)
        v = jnp.where(live, v, 0.0)
        kk = jnp.where(live, kk, 0.0)

    tril = jnp.where(lax.broadcasted_iota(jnp.int32, (L, L), 0) >= lax.broadcasted_iota(jnp.int32, (L, L), 1),
                     1.0, 0.0).astype(F32)
    cl = jnp.concatenate([_dot(tril, logdecay[i * L:(i + 1) * L, :], precision=HIGHEST) for i in range(G)], axis=0)
    e_in = jnp.exp(cl)
    e_inv = jnp.exp(-cl)
    r_t = r * e_in
    r_tb = r_t.astype(BF16)
    a_tb = (-kk * jnp.exp(cl - logdecay)).astype(BF16)
    b_tb = (kk * a * e_inv).astype(BF16)
    k_tb = (k * e_inv).astype(BF16)
    vb = v.astype(BF16)

    row = lax.broadcasted_iota(jnp.int32, (L, PAIR), 0)
    colh = lax.broadcasted_iota(jnp.int32, (L, PAIR), 1) & (RWKV_HEAD - 1)
    strict = row > colh
    incl = row >= colh
    eye_pair = jnp.where(row == colh, 1.0, 0.0).astype(F32)
    lane_lo = lax.broadcasted_iota(jnp.int32, (RWKV_HEAD, PAIR), 1) < RWKV_HEAD
    same_head = (lax.broadcasted_iota(jnp.int32, (PAIR, PAIR), 0) < RWKV_HEAD) == \
                (lax.broadcasted_iota(jnp.int32, (PAIR, PAIR), 1) < RWKV_HEAD)

    streams = [(i, p) for i in range(G) for p in range(RWKV_PAIRS)]
    ns = len(streams)
    blk = lambda x, i, p: x[i * L:(i + 1) * L, p * PAIR:(p + 1) * PAIR]
    lhs = [jnp.concatenate([blk(a_tb, i, p), blk(r_tb, i, p)], axis=0) for i, p in streams]
    m_ab = [_dot_nt(lhs[s], _bd(blk(b_tb, i, p))) for s, (i, p) in enumerate(streams)]
    m_ak = [_dot_nt(lhs[s], _bd(blk(k_tb, i, p))) for s, (i, p) in enumerate(streams)]
    n_ab = [jnp.where(strict, m[0:L], 0.0) for m in m_ab]
    m_rb = [jnp.where(incl, m[L:2 * L], 0.0).astype(BF16) for m in m_ab]
    n_ak = [jnp.where(strict, m[0:L], 0.0).astype(BF16) for m in m_ak]
    m_rk = [jnp.where(incl, m[L:2 * L], 0.0).astype(BF16) for m in m_ak]
    tinv = [eye_pair + n for n in n_ab]
    pwb = [n.astype(BF16) for n in n_ab]
    pw = [_dot(x, _bd(x)) for x in pwb]
    for _ in range(int(math.log2(L)) - 2):
        pwb = [x.astype(BF16) for x in pw]
        both = [_dot(jnp.concatenate([pwb[s], tinv[s].astype(BF16)], axis=0), _bd(pwb[s])) for s in range(ns)]
        pw = [x[0:L] for x in both]
        tinv = [tinv[s] + both[s][L:2 * L] for s in range(ns)]
    pwb = [x.astype(BF16) for x in pw]
    tinv = [tinv[s] + _dot(tinv[s].astype(BF16), _bd(pwb[s])) for s in range(ns)]
    tinvb = [x.astype(BF16) for x in tinv]
    nv_mv = [_dot(jnp.concatenate([n_ak[s], m_rk[s]], axis=0), _bd(blk(vb, i, p))) for s, (i, p) in enumerate(streams)]
    wu = [_dot(tinvb[s], jnp.concatenate([_bd(blk(a_tb, i, p)), _bd(nv_mv[s][0:L].astype(BF16))], axis=1))
          for s, (i, p) in enumerate(streams)]
    wub = [x.astype(BF16) for x in wu]
    qy = [_dot(m_rb[s], jnp.concatenate([_bd(wub[s][:, 0:PAIR]), _bd(wub[s][:, PAIR:2 * PAIR])], axis=1))
          for s in range(ns)]
    q = [(blk(r_t, i, p) + qy[s][:, 0:PAIR]).astype(BF16) for s, (i, p) in enumerate(streams)]
    y_loc = [qy[s][:, PAIR:2 * PAIR] + nv_mv[s][L:2 * L] for s in range(ns)]
    zeros_b = jnp.zeros((L, PAIR), BF16)
    mg = [_dot_tn(jnp.concatenate([wub[s], jnp.concatenate([zeros_b, blk(vb, i, p)], axis=1)], axis=0),
                  jnp.concatenate([blk(b_tb, i, p), blk(k_tb, i, p)], axis=0))
          for s, (i, p) in enumerate(streams)]
    p_end = [e_in[(i + 1) * L - 1:(i + 1) * L, p * PAIR:(p + 1) * PAIR] for i, p in streams]
    m_t = [(jnp.where(same_head, mg[s][0:PAIR], 0.0) * p_end[s]).astype(BF16) for s in range(ns)]
    g_t = [jnp.where(lane_lo, mg[s][PAIR:PAIR + RWKV_HEAD], mg[s][PAIR + RWKV_HEAD:2 * PAIR]) * p_end[s]
           for s in range(ns)]

    y_rows = []
    for i in range(G):
        y_pairs = []
        for p in range(RWKV_PAIRS):
            s = i * RWKV_PAIRS + p
            s0 = s_scr[p]
            s0b = s0.astype(BF16)
            y_pairs.append(_dot_nt(q[s], _bd(s0b)) + y_loc[s])
            s_scr[p] = s0 * p_end[s] + _dot(s0b, m_t[s]) + g_t[s]
        y_rows.append(jnp.concatenate(y_pairs, axis=-1))
    y = jnp.concatenate(y_rows, axis=0)

    mean = _head_sum(y) * (1.0 / RWKV_HEAD)
    yc = y - mean
    var = _head_sum(yc * yc) * (1.0 / RWKV_HEAD)
    y = yc * lax.rsqrt(var + RWKV_LN_EPS) * lng_ref[...] + lnb_ref[...]
    bonus = _head_sum(r * k * rk_ref[...]) * v
    y_ref[...] = (y + bonus) * g

    @pl.when(c == pl.num_programs(1) - 1)
    def _():
        sout_ref[0] = s_scr[...]


def _rwkv(u, shift0, s0, p, *, batch, seq, chunk, group, t_valid, row_off):
    rows = chunk * group
    nc = seq // rows
    blk0 = row_off // rows
    names = ("mu", "w0", "w2", "a0", "a2", "g2", "k_k", "k_a", "r_k", "ln_g", "ln_b")
    params = [p[n] for n in names]
    s0p = s0.reshape(batch, RWKV_PAIRS, 2, RWKV_HEAD, RWKV_HEAD).transpose(0, 1, 3, 2, 4).reshape(
        batch, RWKV_PAIRS, RWKV_HEAD, PAIR)
    sspec = pl.BlockSpec((1, RWKV_PAIRS, RWKV_HEAD, PAIR), lambda b, c: (b, 0, 0, 0))
    y, s_last = pl.pallas_call(
        functools.partial(_rwkv_body, chunk=chunk, group=group, t_valid=t_valid),
        grid=(batch, nc),
        in_specs=[pl.BlockSpec((rows, RWKV_PROJ), lambda b, c: (blk0 + b * nc + c, 0)),
                  pl.BlockSpec((1, 1, RWKV_PROJ), lambda b, c: (b, 0, 0)), sspec]
                 + [_full_spec(a.shape) for a in params],
        out_specs=[pl.BlockSpec((rows, GROUP_WIDTH), lambda b, c: (b * nc + c, 0)), sspec],
        out_shape=[jax.ShapeDtypeStruct((batch * seq, GROUP_WIDTH), F32),
                   jax.ShapeDtypeStruct((batch, RWKV_PAIRS, RWKV_HEAD, PAIR), F32)],
        scratch_shapes=[pltpu.VMEM((SUBLANES + rows, RWKV_PROJ), F32),
                        pltpu.VMEM((RWKV_PAIRS, RWKV_HEAD, PAIR), F32)],
        compiler_params=_cparams("parallel", "arbitrary"),
        name="rwkv",
    )(u, shift0, s0p, *params)
    s_last = s_last.reshape(batch, RWKV_PAIRS, RWKV_HEAD, 2, RWKV_HEAD).transpose(0, 1, 3, 2, 4).reshape(
        batch, RWKV_HEADS, RWKV_HEAD, RWKV_HEAD)
    return y, s_last


def _s5_body(u_ref, hre0_ref, him0_ref, are_ref, aim_ref, bmat_ref, cmat_ref, d_ref, gw_ref, gb_ref,
             y_ref, hre_ref, him_ref, hs_scr, *, steps):
    c = pl.program_id(1)
    ns = S5_GROUPS * S5_STATE
    bsub = u_ref.shape[1]

    @pl.when(c == 0)
    def _():
        hre_ref[...] = hre0_ref[...]
        him_ref[...] = him0_ref[...]

    u = u_ref[...].reshape(steps * bsub, GROUP_WIDTH)
    hs_scr[...] = _dot(u.astype(BF16), bmat_ref[...])
    are = jnp.broadcast_to(are_ref[...], (bsub, ns))
    aim = jnp.broadcast_to(aim_ref[...], (bsub, ns))

    def step(t, carry):
        hre, him = carry
        r0 = pl.multiple_of(t * bsub, bsub)
        nre = are * hre - aim * him + hs_scr[pl.ds(r0, bsub), 0:ns]
        nim = are * him + aim * hre + hs_scr[pl.ds(r0, bsub), ns:2 * ns]
        hs_scr[pl.ds(r0, bsub), 0:ns] = nre
        hs_scr[pl.ds(r0, bsub), ns:2 * ns] = nim
        return nre, nim

    hre, him = lax.fori_loop(0, steps, step, (hre_ref[...], him_ref[...]))
    hre_ref[...] = hre
    him_ref[...] = him
    y = _dot(hs_scr[...].astype(BF16), cmat_ref[...]) + u * d_ref[...]
    y = _gelu_tanh(y)
    yy = _dot(y.astype(BF16), gw_ref[...]) + gb_ref[...]
    out = yy[:, 0:GROUP_WIDTH] * _sigmoid(yy[:, GROUP_WIDTH:2 * GROUP_WIDTH])
    y_ref[...] = out.reshape(steps, bsub, GROUP_WIDTH)


def _s5(u_tm, hre0, him0, are, aim, bmat, cmat, d, gw, gb):
    seq, batch, _ = u_tm.shape
    steps = min(TM_CHUNK, seq)
    ns = S5_GROUPS * S5_STATE
    bsub = SUBLANES
    hspec = pl.BlockSpec((bsub, ns), lambda b, c: (b, 0))
    tspec = pl.BlockSpec((steps, bsub, GROUP_WIDTH), lambda b, c: (c, b, 0))
    consts = (are, aim, bmat, cmat, d, gw, gb)
    return pl.pallas_call(
        functools.partial(_s5_body, steps=steps),
        grid=(batch // bsub, seq // steps),
        in_specs=[tspec, hspec, hspec] + [_full_spec(a.shape) for a in consts],
        out_specs=[tspec, hspec, hspec],
        out_shape=[jax.ShapeDtypeStruct((seq, batch, GROUP_WIDTH), F32),
                   jax.ShapeDtypeStruct((batch, ns), F32),
                   jax.ShapeDtypeStruct((batch, ns), F32)],
        scratch_shapes=[pltpu.VMEM((steps * bsub, 2 * ns), F32)],
        compiler_params=_cparams("parallel", "arbitrary"),
        name="s5",
    )(u_tm, hre0, him0, *consts)


def _pool_body(u_ref, buf0_ref, pw_ref, sc_ref, y_ref, f_scr, *, steps, pos0):
    c = pl.program_id(1)
    bsub = u_ref.shape[1]
    halo = POOL_BUF + 1

    @pl.when(c == 0)
    def _():
        f_scr[0:halo] = buf0_ref[...]

    u = u_ref[...]
    f_scr[halo:halo + steps] = u
    f = f_scr[...]
    s2 = f[1:] + f[:-1]
    s4 = s2[2:] + s2[:-2]
    s8 = s4[4:] + s4[:-4]
    s16 = s8[8:] + s8[:-8]
    f_scr[0:halo] = f[steps:steps + halo]
    lane = lax.broadcasted_iota(jnp.int32, (steps, bsub, GROUP_WIDTH), 2)
    tpos = lax.broadcasted_iota(jnp.int32, (steps, bsub, GROUP_WIDTH), 0) + (pos0 + 1) + c * steps
    win = jnp.where(lane < POOL_CH, s2[halo - 1:halo - 1 + steps],
                    jnp.where(lane < 2 * POOL_CH, s4[halo - 3:halo - 3 + steps],
                              jnp.where(lane < 3 * POOL_CH, s8[halo - 7:halo - 7 + steps],
                                        s16[halo - 15:halo - 15 + steps])))
    wlen = jnp.where(lane < POOL_CH, POOL_WINDOWS[0],
                     jnp.where(lane < 2 * POOL_CH, POOL_WINDOWS[1],
                               jnp.where(lane < 3 * POOL_CH, POOL_WINDOWS[2], POOL_WINDOWS[3])))
    cnt = jnp.minimum(tpos, wlen).astype(F32)
    pooled = (win / cnt - u).reshape(steps * bsub, GROUP_WIDTH)
    y = _dot(pooled.astype(BF16), pw_ref[...]) * sc_ref[...]
    y_ref[...] = y.reshape(steps, bsub, GROUP_WIDTH)


def _pool(u_tm, buf0_tm, pw, sc, *, pos0):
    seq, batch, _ = u_tm.shape
    steps = min(TM_CHUNK, seq)
    bsub = SUBLANES
    tspec = pl.BlockSpec((steps, bsub, GROUP_WIDTH), lambda b, c: (c, b, 0))
    return pl.pallas_call(
        functools.partial(_pool_body, steps=steps, pos0=pos0),
        grid=(batch // bsub, seq // steps),
        in_specs=[tspec, pl.BlockSpec((POOL_BUF + 1, bsub, GROUP_WIDTH), lambda b, c: (0, b, 0)),
                  _full_spec(pw.shape), _full_spec(sc.shape)],
        out_specs=tspec,
        out_shape=jax.ShapeDtypeStruct((seq, batch, GROUP_WIDTH), F32),
        scratch_shapes=[pltpu.VMEM((POOL_BUF + 1 + steps, bsub, GROUP_WIDTH), F32)],
        compiler_params=_cparams("parallel", "arbitrary"),
        name="pool",
    )(u_tm, buf0_tm, pw, sc)


def _block_diag(blocks):
    g, r, c = blocks.shape
    eye = jnp.eye(g, dtype=blocks.dtype)
    return (eye[:, None, :, None] * blocks[:, :, None, :]).reshape(g * r, g * c)


def _pad_lanes(v, n=LANES):
    return jnp.pad(v, (0, n - v.shape[0])).reshape(1, n)


def _layer_params(l, P):
    row = lambda a: a.reshape(1, -1)
    w_in = P["w_in"][l]
    z_w, xbc_w, dt_w, rw_w, s5_w, pool_w = jnp.split(
        w_in, [256, 256 + 768, 256 + 768 + 4, 1028 + 1024, 1028 + 1024 + 256], axis=1)
    w_all = jnp.concatenate([z_w, xbc_w, rw_w, s5_w, pool_w, jnp.pad(dt_w, ((0, 0), (0, LANES - SSD_HEADS)))], axis=1)

    lam = lax.complex(P["s5_lam_re"][l], P["s5_lam_im"][l])
    a_bar = jnp.exp(lam * jnp.exp(P["s5_log_step"][l])[:, None])
    b_bar = ((a_bar - 1.0) / lam)[..., None] * lax.complex(P["s5_b_re"][l], P["s5_b_im"][l])
    b_t = jnp.swapaxes(b_bar, 1, 2)
    bmat = jnp.concatenate([_block_diag(jnp.real(b_t)), _block_diag(jnp.imag(b_t))], axis=1)
    c_t = jnp.swapaxes(lax.complex(P["s5_c_re"][l], P["s5_c_im"][l]), 1, 2)
    cmat = jnp.concatenate([_block_diag(jnp.real(c_t)), -_block_diag(jnp.imag(c_t))], axis=0)

    return dict(
        norm_ffn1=row(P["norm_ffn1"][l]), ffn1_in=P["ffn1_in"][l].astype(BF16), ffn1_out=P["ffn1_out"][l].astype(BF16),
        norm_mix=row(P["norm_mix"][l]), w_all=w_all.astype(BF16),
        conv_w=P["ssd_conv_w"][l], conv_b=row(P["ssd_conv_b"][l]),
        dt_bias=_pad_lanes(P["ssd_dt_bias"][l]), a_log=_pad_lanes(P["ssd_a_log"][l]),
        d_skip=row(jnp.repeat(P["ssd_d"][l], SSD_HEAD_DIM)), ssd_norm=row(P["ssd_norm"][l]),
        rwkv=dict(mu=row(P["rwkv_mu"][l]), w0=row(P["rwkv_w0"][l]), w2=P["rwkv_w2"][l].astype(BF16),
                  a0=row(P["rwkv_a0"][l]), a2=P["rwkv_a2"][l].astype(BF16), g2=P["rwkv_g2"][l].astype(BF16),
                  k_k=row(P["rwkv_k_k"][l]), k_a=row(P["rwkv_k_a"][l]), r_k=row(P["rwkv_r_k"][l]),
                  ln_g=row(P["rwkv_ln_g"][l]), ln_b=row(P["rwkv_ln_b"][l])),
        s5_are=row(jnp.real(a_bar)), s5_aim=row(jnp.imag(a_bar)), s5_bmat=bmat.astype(BF16), s5_cmat=cmat.astype(BF16),
        s5_d=row(P["s5_d"][l]), s5_gw=P["s5_glu_w"][l].astype(BF16), s5_gb=row(P["s5_glu_b"][l]),
        pool_w=_block_diag(P["pool_w"][l]).astype(BF16), pool_scale=row(P["pool_scale"][l]),
        w_out=P["w_out"][l].astype(BF16),
        norm_ffn2=row(P["norm_ffn2"][l]), ffn2_in=P["ffn2_in"][l].astype(BF16), ffn2_out=P["ffn2_out"][l].astype(BF16),
    )


def _to_tm(rows, batch, seq):
    return jnp.swapaxes(rows.reshape(batch, seq, rows.shape[-1]), 0, 1)


def _from_tm(x_tm):
    seq, batch, w = x_tm.shape
    return jnp.swapaxes(x_tm, 0, 1).reshape(batch * seq, w)


def _pad_time(rows, batch, seq, seq_pad):
    w = rows.shape[-1]
    return jnp.pad(rows.reshape(batch, seq, w), ((0, 0), (0, seq_pad - seq), (0, 0))).reshape(batch * seq_pad, w)


def _unpad_time(rows, batch, seq, seq_pad):
    w = rows.shape[-1]
    return rows.reshape(batch, seq_pad, w)[:, :seq].reshape(batch * seq, w)


def _mixers(l, lp, proj, states, *, batch, seq, row_off, pos0):
    z, xbc, ur, us5, upool, dtr = proj
    conv0, ssd0, shift0, rwkv0, s5re0, s5im0, pool0 = states
    n = batch * seq
    sl = lambda a: a[row_off:row_off + n]

    if seq % SSD_CHUNK == 0:
        y_ssd, ssd_new = _ssd(z, xbc, dtr, conv0, ssd0, lp["conv_w"], lp["conv_b"], lp["dt_bias"], lp["a_log"],
                              lp["d_skip"], lp["ssd_norm"], batch=batch, seq=seq, chunk=SSD_CHUNK, t_valid=SSD_CHUNK,
                              row_off=row_off)
        y_rwkv, rwkv_new = _rwkv(ur, shift0[:, None, :], rwkv0, lp["rwkv"], batch=batch, seq=seq, chunk=RWKV_CHUNK,
                                 group=RWKV_GROUP, t_valid=RWKV_CHUNK * RWKV_GROUP, row_off=row_off)
    else:
        pt = lambda a, L: _pad_time(sl(a), batch, seq, L)
        y_ssd, ssd_new = _ssd(pt(z, SSD_CHUNK), pt(xbc, SSD_CHUNK), pt(dtr, SSD_CHUNK), conv0, ssd0, lp["conv_w"],
                              lp["conv_b"], lp["dt_bias"], lp["a_log"], lp["d_skip"], lp["ssd_norm"],
                              batch=batch, seq=SSD_CHUNK, chunk=SSD_CHUNK, t_valid=seq, row_off=0)
        y_ssd = _unpad_time(y_ssd, batch, seq, SSD_CHUNK)
        y_rwkv, rwkv_new = _rwkv(pt(ur, RWKV_CHUNK), shift0[:, None, :], rwkv0, lp["rwkv"], batch=batch,
                                 seq=RWKV_CHUNK, chunk=RWKV_CHUNK, group=1, t_valid=seq, row_off=0)
        y_rwkv = _unpad_time(y_rwkv, batch, seq, RWKV_CHUNK)

    ns = S5_GROUPS * S5_STATE
    y_s5_tm, s5re_new, s5im_new = _s5(_to_tm(sl(us5), batch, seq), s5re0.reshape(batch, ns), s5im0.reshape(batch, ns),
                                      lp["s5_are"], lp["s5_aim"], lp["s5_bmat"], lp["s5_cmat"], lp["s5_d"],
                                      lp["s5_gw"], lp["s5_gb"])
    upool_tm = _to_tm(sl(upool), batch, seq)
    pool0_tm = jnp.pad(jnp.swapaxes(pool0, 0, 1), ((1, 0), (0, 0), (0, 0)))
    y_pool_tm = _pool(upool_tm, pool0_tm, lp["pool_w"], lp["pool_scale"], pos0=pos0)

    tail = lambda old, new, k: jnp.concatenate([old, new.reshape(batch, seq, -1)], axis=1)[:, -k:]
    conv_new = tail(conv0, sl(xbc), SSD_CONV - 1)
    shift_new = sl(ur).reshape(batch, seq, -1)[:, -1]
    pool_new = tail(pool0, sl(upool), POOL_BUF)
    ys = (y_ssd, y_rwkv, _from_tm(y_s5_tm), _from_tm(y_pool_tm))
    new_states = (conv_new, ssd_new, shift_new, rwkv_new,
                  s5re_new.reshape(batch, S5_GROUPS, S5_STATE), s5im_new.reshape(batch, S5_GROUPS, S5_STATE), pool_new)
    return ys, new_states


def kernel(x_prompt, x_sample, state_ssd_conv, state_ssd, state_rwkv_shift, state_rwkv, state_s5_re, state_s5_im, state_pool, norm_ffn1, ffn1_in, ffn1_out, norm_mix, w_in, ssd_conv_w, ssd_conv_b, ssd_dt_bias, ssd_a_log, ssd_d, ssd_norm, rwkv_mu, rwkv_w0, rwkv_w2, rwkv_a0, rwkv_a2, rwkv_g2, rwkv_k_k, rwkv_k_a, rwkv_r_k, rwkv_ln_g, rwkv_ln_b, s5_lam_re, s5_lam_im, s5_log_step, s5_b_re, s5_b_im, s5_c_re, s5_c_im, s5_d, s5_glu_w, s5_glu_b, pool_w, pool_scale, w_out, norm_ffn2, ffn2_in, ffn2_out, norm_final):
    P = dict(norm_ffn1=norm_ffn1, ffn1_in=ffn1_in, ffn1_out=ffn1_out, norm_mix=norm_mix, w_in=w_in,
             ssd_conv_w=ssd_conv_w, ssd_conv_b=ssd_conv_b, ssd_dt_bias=ssd_dt_bias, ssd_a_log=ssd_a_log,
             ssd_d=ssd_d, ssd_norm=ssd_norm, rwkv_mu=rwkv_mu, rwkv_w0=rwkv_w0, rwkv_w2=rwkv_w2, rwkv_a0=rwkv_a0,
             rwkv_a2=rwkv_a2, rwkv_g2=rwkv_g2, rwkv_k_k=rwkv_k_k, rwkv_k_a=rwkv_k_a,
             rwkv_r_k=rwkv_r_k.reshape(rwkv_r_k.shape[0], -1), rwkv_ln_g=rwkv_ln_g, rwkv_ln_b=rwkv_ln_b,
             s5_lam_re=s5_lam_re, s5_lam_im=s5_lam_im, s5_log_step=s5_log_step, s5_b_re=s5_b_re, s5_b_im=s5_b_im,
             s5_c_re=s5_c_re, s5_c_im=s5_c_im, s5_d=s5_d, s5_glu_w=s5_glu_w, s5_glu_b=s5_glu_b, pool_w=pool_w,
             pool_scale=pool_scale, w_out=w_out, norm_ffn2=norm_ffn2, ffn2_in=ffn2_in, ffn2_out=ffn2_out)
    depth = norm_ffn1.shape[0]
    bp, tp, d = x_prompt.shape
    bs, ts, _ = x_sample.shape
    n_p, n_s = bp * tp, bs * ts
    x = jnp.concatenate([x_prompt.reshape(n_p, d), x_sample.reshape(n_s, d)], axis=0)
    sample_states = (state_ssd_conv, state_ssd, state_rwkv_shift, state_rwkv, state_s5_re, state_s5_im, state_pool)
    widths = (GROUP_WIDTH, SSD_CONV_DIM, RWKV_PROJ, GROUP_WIDTH, GROUP_WIDTH, LANES)

    new_p, new_s = [], []
    mix, wmix = None, None
    lp = None
    for l in range(depth):
        if l > 0:
            x = _ffn(x, lp["norm_ffn2"], lp["ffn2_in"], lp["ffn2_out"], mix=mix, wmix=wmix)
        lp = _layer_params(l, P)
        x = _ffn(x, lp["norm_ffn1"], lp["ffn1_in"], lp["ffn1_out"])
        proj = _inproj(x, lp["norm_mix"], lp["w_all"], widths)
        zeros_p = tuple(jnp.zeros((bp,) + s.shape[2:], s.dtype) for s in sample_states)
        ys_p, st_p = _mixers(l, lp, proj, zeros_p, batch=bp, seq=tp, row_off=0, pos0=0)
        ys_s, st_s = _mixers(l, lp, proj, tuple(s[l] for s in sample_states), batch=bs, seq=ts, row_off=n_p,
                             pos0=PAST_LEN)
        mix = tuple(jnp.concatenate([a, b], axis=0) for a, b in zip(ys_p, ys_s))
        wmix = lp["w_out"]
        new_p.append(st_p)
        new_s.append(st_s)
    x = _ffn(x, lp["norm_ffn2"], lp["ffn2_in"], lp["ffn2_out"], mix=mix, wmix=wmix, gf=norm_final.reshape(1, -1))
    y_prompt = x[:n_p].reshape(bp, tp, d)
    y_sample = x[n_p:].reshape(bs, ts, d)
    outs = [y_prompt, y_sample]
    for i in range(7):
        outs.append(jnp.stack([st[i] for st in new_p]))
        outs.append(jnp.stack([st[i] for st in new_s]))
    return tuple(outs)
```

```python
import functools
import math

import jax
import jax.numpy as jnp
from jax import lax
from jax.experimental import pallas as pl
from jax.experimental.pallas import tpu as pltpu

F32 = jnp.float32
BF16 = jnp.bfloat16
HIGHEST = lax.Precision.HIGHEST

SUBLANES = 8
LANES = 128
VMEM_LIMIT_BYTES = 56 * 1024 * 1024

GROUP_WIDTH = 256
SSD_HEAD_DIM = 64
SSD_HEADS = 4
SSD_GROUPS = 2
SSD_STATE = 128
SSD_CONV = 4
SSD_CONV_DIM = GROUP_WIDTH + 2 * SSD_GROUPS * SSD_STATE
SSD_CHUNK = 128
RWKV_HEAD = 64
RWKV_HEADS = 4
RWKV_PROJ = 1024
RWKV_LN_EPS = 64e-5
RWKV_CHUNK = 64
RWKV_GROUP = 4
S5_GROUP_CH = 16
S5_GROUPS = 16
S5_STATE = 64
S5_WIDTH = S5_GROUPS * S5_STATE
POOL_WINDOWS = (2, 4, 8, 16)
POOL_CH = 64
POOL_BUF = 15
RMS_EPS = 1e-6
PAST_LEN = 16384

ROW_TILE = 512
FFN_CHUNK = 256
TM_CHUNK = 64
SSD_STEP_TILES = 16
RWKV_STEP_TILES = 8


def _cparams(*sem):
    return pltpu.CompilerParams(dimension_semantics=sem, vmem_limit_bytes=VMEM_LIMIT_BYTES)


def _dot(a, b, **kw):
    return jnp.dot(a, b, preferred_element_type=F32, **kw)


def _dot_nt(a, b):
    return lax.dot_general(a, b, (((1,), (1,)), ((), ())), preferred_element_type=F32)


def _dot_tn(a, b):
    return lax.dot_general(a, b, (((0,), (0,)), ((), ())), preferred_element_type=F32)


def _sigmoid(x):
    return 1.0 / (1.0 + jnp.exp(-x))


def _silu(x):
    return x * _sigmoid(x)


def _softplus(x):
    return jnp.maximum(x, 0.0) + jnp.log(1.0 + jnp.exp(-jnp.abs(x)))


def _gelu_tanh(x):
    c = math.sqrt(2.0 / math.pi)
    return x * (0.5 * (1.0 + jnp.tanh(c * (x + 0.044715 * (x * x * x)))))


def _rms(x, g):
    return x * lax.rsqrt(jnp.mean(x * x, axis=-1, keepdims=True) + RMS_EPS) * g


def _full_spec(shape):
    n = len(shape)
    return pl.BlockSpec(shape, lambda *_: (0,) * n)


def _roll_lanes(x, shift):
    return pltpu.roll(x, shift, axis=x.ndim - 1)


def _ffn_body(*refs, has_mix, final_norm):
    it = iter(refs)
    x_ref = next(it)
    x = x_ref[...]
    if has_mix:
        y_refs = [next(it) for _ in range(4)]
        wmix_ref = next(it)
        for i, y_ref in enumerate(y_refs):
            x = x + _dot(y_ref[...].astype(BF16), wmix_ref[i * GROUP_WIDTH:(i + 1) * GROUP_WIDTH, :])
    g_ref, wi_ref, wo_ref = next(it), next(it), next(it)
    gf_ref = next(it) if final_norm else None
    o_ref = next(it)
    h = _rms(x, g_ref[...]).astype(BF16)
    d_ff = wo_ref.shape[0]
    acc = jnp.zeros_like(x)
    for c in range(d_ff // FFN_CHUNK):
        lo = c * FFN_CHUNK
        gate = _dot(h, wi_ref[:, lo:lo + FFN_CHUNK])
        up = _dot(h, wi_ref[:, d_ff + lo:d_ff + lo + FFN_CHUNK])
        act = (_silu(gate) * up).astype(BF16)
        acc = acc + _dot(act, wo_ref[lo:lo + FFN_CHUNK, :])
    x = x + 0.5 * acc
    if final_norm:
        x = _rms(x, gf_ref[...])
    o_ref[...] = x


def _ffn(x, g, wi, wo, mix=None, wmix=None, gf=None):
    rows, d = x.shape
    row_spec = lambda w: pl.BlockSpec((ROW_TILE, w), lambda i: (i, 0))
    args, specs = [x], [row_spec(d)]
    if mix is not None:
        for y in mix:
            args.append(y)
            specs.append(row_spec(y.shape[1]))
        args.append(wmix)
        specs.append(_full_spec(wmix.shape))
    for a in (g, wi, wo) + ((gf,) if gf is not None else ()):
        args.append(a)
        specs.append(_full_spec(a.shape))
    return pl.pallas_call(
        functools.partial(_ffn_body, has_mix=mix is not None, final_norm=gf is not None),
        grid=(rows // ROW_TILE,),
        in_specs=specs,
        out_specs=row_spec(d),
        out_shape=jax.ShapeDtypeStruct((rows, d), F32),
        compiler_params=_cparams("parallel"),
        name="ffn",
    )(*args)


def _inproj_body(x_ref, g_ref, w_ref, *o_refs):
    h = _rms(x_ref[...], g_ref[...]).astype(BF16)
    off = 0
    for o_ref in o_refs:
        n = o_ref.shape[-1]
        o_ref[...] = _dot(h, w_ref[:, off:off + n])
        off += n


def _inproj(x, g, w, widths):
    rows, d = x.shape
    row_spec = lambda w_: pl.BlockSpec((ROW_TILE, w_), lambda i: (i, 0))
    return pl.pallas_call(
        _inproj_body,
        grid=(rows // ROW_TILE,),
        in_specs=[row_spec(d), _full_spec(g.shape), _full_spec(w.shape)],
        out_specs=[row_spec(n) for n in widths],
        out_shape=[jax.ShapeDtypeStruct((rows, n), F32) for n in widths],
        compiler_params=_cparams("parallel"),
        name="inproj",
    )(x, g, w)


def _ssd_body(z_ref, xbc_ref, dt_ref, cw_ref, cb_ref, dtb_ref, alog_ref, dsk_ref, ng_ref,
              y_ref, conv_ref, hout_ref, xpad_scr, h_scr, *, chunk):
    L = chunk
    c = pl.program_id(1)
    pad = SUBLANES
    halo = SSD_CONV - 1

    @pl.when(c == 0)
    def _():
        xpad_scr[0:pad, :] = jnp.zeros((pad, SSD_CONV_DIM), F32)
        h_scr[...] = jnp.zeros(h_scr.shape, F32)

    xpad_scr[pad:pad + L, :] = xbc_ref[...]
    conv = cb_ref[...] + cw_ref[0:1, :] * xpad_scr[pad - halo:pad - halo + L, :]
    for j in range(1, SSD_CONV):
        conv = conv + cw_ref[j:j + 1, :] * xpad_scr[pad - halo + j:pad - halo + j + L, :]
    xpad_scr[pad - halo:pad, :] = xpad_scr[pad + L - halo:pad + L, :]
    conv = _silu(conv)
    xs = conv[:, 0:GROUP_WIDTH]
    bm = conv[:, GROUP_WIDTH:2 * GROUP_WIDTH].astype(BF16)
    cm = conv[:, 2 * GROUP_WIDTH:3 * GROUP_WIDTH].astype(BF16)

    row = lax.broadcasted_iota(jnp.int32, (L, L), 0)
    col = lax.broadcasted_iota(jnp.int32, (L, L), 1)
    causal = row >= col
    dt = _softplus(dt_ref[...] + dtb_ref[...])
    da = dt * (-jnp.exp(alog_ref[...]))
    acs = _dot(jnp.where(causal, 1.0, 0.0).astype(F32), da, precision=HIGHEST)
    acs_t = acs.T
    e_acs = jnp.exp(acs)
    acs_last = acs[L - 1:L, :]
    e_end = jnp.exp(acs_last - acs)
    e_last = jnp.exp(acs_last)

    ys = []
    for h in range(SSD_HEADS):
        g = h // (SSD_HEADS // SSD_GROUPS)
        bg = bm[:, g * SSD_STATE:(g + 1) * SSD_STATE]
        cg = cm[:, g * SSD_STATE:(g + 1) * SSD_STATE]
        x_h = xs[:, h * SSD_HEAD_DIM:(h + 1) * SSD_HEAD_DIM]
        xdt = x_h * dt[:, h:h + 1]
        seg = acs[:, h:h + 1] - acs_t[h:h + 1, :]
        decay = jnp.exp(jnp.where(causal, seg, -jnp.inf))
        scores = _dot_nt(cg, bg) * decay
        y_h = _dot(scores.astype(BF16), xdt.astype(BF16))
        h_prev = h_scr[h]
        y_h = y_h + _dot_nt(cg, h_prev.astype(BF16)) * e_acs[:, h:h + 1]
        st = _dot_tn((xdt * e_end[:, h:h + 1]).astype(BF16), bg)
        h_scr[h] = h_prev * e_last[:, h:h + 1] + st
        ys.append(y_h)
    y = jnp.concatenate(ys, axis=-1) + xs * dsk_ref[...]
    y = y * _silu(z_ref[...])
    y_ref[...] = _rms(y, ng_ref[...])

    @pl.when(c == pl.num_programs(1) - 1)
    def _():
        hout_ref[0] = h_scr[...]
        conv_ref[0] = xpad_scr[pad - halo:pad, :]


def _ssd(z, xbc, dtr, lp, *, batch, seq):
    chunk = SSD_CHUNK
    nc = seq // chunk
    rspec = lambda w: pl.BlockSpec((chunk, w), lambda b, c: (b * nc + c, 0))
    consts = (lp["conv_w"], lp["conv_b"], lp["dt_bias"], lp["a_log"], lp["d_skip"], lp["ssd_norm"])
    return pl.pallas_call(
        functools.partial(_ssd_body, chunk=chunk),
        grid=(batch, nc),
        in_specs=[rspec(GROUP_WIDTH), rspec(SSD_CONV_DIM), rspec(LANES)] + [_full_spec(a.shape) for a in consts],
        out_specs=[rspec(GROUP_WIDTH),
                   pl.BlockSpec((1, SSD_CONV - 1, SSD_CONV_DIM), lambda b, c: (b, 0, 0)),
                   pl.BlockSpec((1, SSD_HEADS, SSD_HEAD_DIM, SSD_STATE), lambda b, c: (b, 0, 0, 0))],
        out_shape=[jax.ShapeDtypeStruct((batch * seq, GROUP_WIDTH), F32),
                   jax.ShapeDtypeStruct((batch, SSD_CONV - 1, SSD_CONV_DIM), F32),
                   jax.ShapeDtypeStruct((batch, SSD_HEADS, SSD_HEAD_DIM, SSD_STATE), F32)],
        scratch_shapes=[pltpu.VMEM((SUBLANES + chunk, SSD_CONV_DIM), F32),
                        pltpu.VMEM((SSD_HEADS, SSD_HEAD_DIM, SSD_STATE), F32)],
        compiler_params=_cparams("parallel", "arbitrary"),
        name="ssd",
    )(z, xbc, dtr, *consts)


def _ssd_step_body(z_ref, xbc_ref, dt_ref, conv0_ref, h0_ref, cw_ref, cb_ref, dtb_ref, aneg_ref, dsk_ref, ng_ref,
                   hexp_ref, y_ref, conv_ref, hout_ref, xs_scr, bm_scr, cm_scr, xdt_scr, dec_scr, y_scr,
                   *, seq, batch):
    T, B = seq, batch
    GW = GROUP_WIDTH
    j = pl.program_id(0)
    tiles = SSD_STEP_TILES

    @pl.when(j == 0)
    def _():
        rows = [conv0_ref[:, i * SSD_CONV_DIM:(i + 1) * SSD_CONV_DIM] for i in range(SSD_CONV - 1)]
        rows += [xbc_ref[t * B:(t + 1) * B, :] for t in range(T)]
        for t in range(T):
            conv = cb_ref[...] + cw_ref[0:1, :] * rows[t]
            for i in range(1, SSD_CONV):
                conv = conv + cw_ref[i:i + 1, :] * rows[t + i]
            conv = _silu(conv)
            xs = conv[:, 0:GW]
            xs_scr[t] = xs
            for g in range(SSD_GROUPS):
                bm_scr[t, g] = conv[:, GW + g * SSD_STATE:GW + (g + 1) * SSD_STATE]
                cm_scr[t, g] = conv[:, 2 * GW + g * SSD_STATE:2 * GW + (g + 1) * SSD_STATE]
            dt = _softplus(dt_ref[t * B:(t + 1) * B, :] + dtb_ref[...])
            dte = _dot(dt, hexp_ref[...], precision=HIGHEST)
            xdt_scr[t] = xs * dte
            dec_scr[t] = jnp.exp(dte * aneg_ref[...])
        for i in range(SSD_CONV - 1):
            conv_ref[:, i * SSD_CONV_DIM:(i + 1) * SSD_CONV_DIM] = rows[T + i]
        y_scr[...] = jnp.zeros(y_scr.shape, F32)

    hp0 = j * tiles
    grp = hp0 // (SSD_HEAD_DIM * (SSD_HEADS // SSD_GROUPS))
    back = lax.rem(GW - hp0, GW)
    xdt_r = [_roll_lanes(xdt_scr[t], back) for t in range(T)]
    dcol = [_roll_lanes(dec_scr[t], back)[:, 0:1] for t in range(T)]
    bm_t = [bm_scr[t, grp] for t in range(T)]
    cm_t = [cm_scr[t, grp] for t in range(T)]
    lane = lax.broadcasted_iota(jnp.int32, (B, GW), 1)
    y_loc = [jnp.zeros((B, GW), F32) for _ in range(T)]
    for q in range(tiles):
        h = h0_ref[:, q * SSD_STATE:(q + 1) * SSD_STATE]
        for t in range(T):
            h = h * dcol[t] + xdt_r[t][:, q:q + 1] * bm_t[t]
            y_col = jnp.sum(h * cm_t[t], axis=-1, keepdims=True)
            y_loc[t] = jnp.where(lane == q, y_col, y_loc[t])
        hout_ref[:, q * SSD_STATE:(q + 1) * SSD_STATE] = h
    for t in range(T):
        y_scr[t] = y_scr[t] + _roll_lanes(y_loc[t], hp0)

    @pl.when(j == pl.num_programs(0) - 1)
    def _():
        for t in range(T):
            y = y_scr[t] + xs_scr[t] * dsk_ref[...]
            y = y * _silu(z_ref[t * B:(t + 1) * B, :])
            y_ref[t * B:(t + 1) * B, :] = _rms(y, ng_ref[...])


def _ssd_step(z, xbc, dtr, conv0, h0, lp, *, batch, seq):
    n = batch * seq
    nstate = SSD_HEADS * SSD_HEAD_DIM * SSD_STATE
    tile_w = SSD_STEP_TILES * SSD_STATE
    consts = (lp["conv_w"], lp["conv_b"], lp["dt_bias"], lp["a_neg_exp"], lp["d_skip"], lp["ssd_norm"], lp["head_expand"])
    hspec = pl.BlockSpec((batch, tile_w), lambda j: (0, j))
    cflat = (SSD_CONV - 1) * SSD_CONV_DIM
    y, conv_new, h_new = pl.pallas_call(
        functools.partial(_ssd_step_body, seq=seq, batch=batch),
        grid=(nstate // tile_w,),
        in_specs=[_full_spec((n, GROUP_WIDTH)), _full_spec((n, SSD_CONV_DIM)), _full_spec((n, LANES)),
                  _full_spec((batch, cflat)), hspec] + [_full_spec(a.shape) for a in consts],
        out_specs=[_full_spec((n, GROUP_WIDTH)), _full_spec((batch, cflat)), hspec],
        out_shape=[jax.ShapeDtypeStruct((n, GROUP_WIDTH), F32),
                   jax.ShapeDtypeStruct((batch, cflat), F32),
                   jax.ShapeDtypeStruct((batch, nstate), F32)],
        scratch_shapes=[pltpu.VMEM((seq, batch, GROUP_WIDTH), F32),
                        pltpu.VMEM((seq, SSD_GROUPS, batch, SSD_STATE), F32),
                        pltpu.VMEM((seq, SSD_GROUPS, batch, SSD_STATE), F32),
                        pltpu.VMEM((seq, batch, GROUP_WIDTH), F32),
                        pltpu.VMEM((seq, batch, GROUP_WIDTH), F32),
                        pltpu.VMEM((seq, batch, GROUP_WIDTH), F32)],
        compiler_params=_cparams("arbitrary"),
        name="ssd_step",
    )(z, xbc, dtr, conv0.reshape(batch, cflat), h0.reshape(batch, nstate), *consts)
    return (y, conv_new.reshape(batch, SSD_CONV - 1, SSD_CONV_DIM),
            h_new.reshape(batch, SSD_HEADS, SSD_HEAD_DIM, SSD_STATE))


PAIR = 2 * RWKV_HEAD
RWKV_PAIRS = RWKV_HEADS // 2


def _bd(x):
    lane = lax.broadcasted_iota(jnp.int32, x.shape, 1)
    zero = jnp.zeros_like(x)
    return jnp.concatenate([jnp.where(lane < RWKV_HEAD, x, zero), jnp.where(lane >= RWKV_HEAD, x, zero)], axis=0)


def _half_sums(x, lo):
    s_lo = jnp.sum(jnp.where(lo, x, 0.0), axis=-1, keepdims=True)
    s_hi = jnp.sum(jnp.where(lo, 0.0, x), axis=-1, keepdims=True)
    return jnp.where(lo, s_lo, s_hi)


def _head_sum(x):
    lo = lax.broadcasted_iota(jnp.int32, (x.shape[0], PAIR), 1) < RWKV_HEAD
    return jnp.concatenate([_half_sums(x[:, p * PAIR:(p + 1) * PAIR], lo) for p in range(RWKV_PAIRS)], axis=-1)


def _rwkv_pointwise(u, prev, mu_ref, w0_ref, w2_ref, a0_ref, a2_ref, g2_ref, kk_ref, ka_ref):
    GW = GROUP_WIDTH
    xs = u + (prev - u) * mu_ref[...]
    r = xs[:, 0:GW]
    k = xs[:, GW:2 * GW]
    v = xs[:, 2 * GW:3 * GW]
    wd = xs[:, 3 * GW:3 * GW + 64]
    ad = xs[:, 3 * GW + 64:3 * GW + 128]
    gd = xs[:, 3 * GW + 128:3 * GW + 256]
    w_lin = w0_ref[...] + _dot(jnp.tanh(wd).astype(BF16), w2_ref[...])
    logdecay = -jnp.exp(-_softplus(-w_lin) - 0.5)
    a = _sigmoid(a0_ref[...] + _dot(ad.astype(BF16), a2_ref[...]))
    g = _dot(_sigmoid(gd).astype(BF16), g2_ref[...])
    kk = k * kk_ref[...]
    kk = kk / jnp.maximum(jnp.sqrt(_head_sum(kk * kk)), 1e-12)
    k = k * (1.0 + (a - 1.0) * ka_ref[...])
    return r, k, v, logdecay, a, g, kk


def _rwkv_finish(y, r, k, v, g, rk_ref, lng_ref, lnb_ref):
    mean = _head_sum(y) * (1.0 / RWKV_HEAD)
    yc = y - mean
    var = _head_sum(yc * yc) * (1.0 / RWKV_HEAD)
    y = yc * lax.rsqrt(var + RWKV_LN_EPS) * lng_ref[...] + lnb_ref[...]
    bonus = _head_sum(r * k * rk_ref[...]) * v
    return (y + bonus) * g


def _rwkv_body(u_ref, mu_ref, w0_ref, w2_ref, a0_ref, a2_ref, g2_ref, kk_ref, ka_ref, rk_ref,
               lng_ref, lnb_ref, y_ref, shift_ref, sout_ref, upad_scr, s_scr, *, chunk, group):
    L, G = chunk, group
    GL = G * L
    c = pl.program_id(1)
    pad = SUBLANES

    @pl.when(c == 0)
    def _():
        upad_scr[0:pad, :] = jnp.zeros((pad, RWKV_PROJ), F32)
        s_scr[...] = jnp.zeros(s_scr.shape, F32)

    u = u_ref[...]
    upad_scr[pad:pad + GL, :] = u
    prev = upad_scr[pad - 1:pad - 1 + GL, :]
    upad_scr[pad - 1:pad, :] = u[GL - 1:GL, :]
    r, k, v, logdecay, a, g, kk = _rwkv_pointwise(u, prev, mu_ref, w0_ref, w2_ref, a0_ref, a2_ref, g2_ref,
                                                  kk_ref, ka_ref)

    tril = jnp.where(lax.broadcasted_iota(jnp.int32, (L, L), 0) >= lax.broadcasted_iota(jnp.int32, (L, L), 1),
                     1.0, 0.0).astype(F32)
    cl = jnp.concatenate([_dot(tril, logdecay[i * L:(i + 1) * L, :], precision=HIGHEST) for i in range(G)], axis=0)
    e_in = jnp.exp(cl)
    e_inv = jnp.exp(-cl)
    r_t = r * e_in
    r_tb = r_t.astype(BF16)
    a_tb = (-kk * jnp.exp(cl - logdecay)).astype(BF16)
    b_tb = (kk * a * e_inv).astype(BF16)
    k_tb = (k * e_inv).astype(BF16)
    vb = v.astype(BF16)

    row = lax.broadcasted_iota(jnp.int32, (L, PAIR), 0)
    colh = lax.broadcasted_iota(jnp.int32, (L, PAIR), 1) & (RWKV_HEAD - 1)
    strict = row > colh
    incl = row >= colh
    eye_pair = jnp.where(row == colh, 1.0, 0.0).astype(F32)
    lane_lo = lax.broadcasted_iota(jnp.int32, (RWKV_HEAD, PAIR), 1) < RWKV_HEAD
    same_head = (lax.broadcasted_iota(jnp.int32, (PAIR, PAIR), 0) < RWKV_HEAD) == \
                (lax.broadcasted_iota(jnp.int32, (PAIR, PAIR), 1) < RWKV_HEAD)

    streams = [(i, p) for i in range(G) for p in range(RWKV_PAIRS)]
    ns = len(streams)
    blk = lambda x, i, p: x[i * L:(i + 1) * L, p * PAIR:(p + 1) * PAIR]
    lhs = [jnp.concatenate([blk(a_tb, i, p), blk(r_tb, i, p)], axis=0) for i, p in streams]
    m_ab = [_dot_nt(lhs[s], _bd(blk(b_tb, i, p))) for s, (i, p) in enumerate(streams)]
    m_ak = [_dot_nt(lhs[s], _bd(blk(k_tb, i, p))) for s, (i, p) in enumerate(streams)]
    n_ab = [jnp.where(strict, m[0:L], 0.0) for m in m_ab]
    m_rb = [jnp.where(incl, m[L:2 * L], 0.0).astype(BF16) for m in m_ab]
    n_ak = [jnp.where(strict, m[0:L], 0.0).astype(BF16) for m in m_ak]
    m_rk = [jnp.where(incl, m[L:2 * L], 0.0).astype(BF16) for m in m_ak]
    tinv = [eye_pair + n for n in n_ab]
    pwb = [n.astype(BF16) for n in n_ab]
    pw = [_dot(x, _bd(x)) for x in pwb]
    for _ in range(int(math.log2(L)) - 2):
        pwb = [x.astype(BF16) for x in pw]
        both = [_dot(jnp.concatenate([pwb[s], tinv[s].astype(BF16)], axis=0), _bd(pwb[s])) for s in range(ns)]
        pw = [x[0:L] for x in both]
        tinv = [tinv[s] + both[s][L:2 * L] for s in range(ns)]
    pwb = [x.astype(BF16) for x in pw]
    tinv = [tinv[s] + _dot(tinv[s].astype(BF16), _bd(pwb[s])) for s in range(ns)]
    tinvb = [x.astype(BF16) for x in tinv]
    nv_mv = [_dot(jnp.concatenate([n_ak[s], m_rk[s]], axis=0), _bd(blk(vb, i, p))) for s, (i, p) in enumerate(streams)]
    wu = [_dot(tinvb[s], jnp.concatenate([_bd(blk(a_tb, i, p)), _bd(nv_mv[s][0:L].astype(BF16))], axis=1))
          for s, (i, p) in enumerate(streams)]
    wub = [x.astype(BF16) for x in wu]
    qy = [_dot(m_rb[s], jnp.concatenate([_bd(wub[s][:, 0:PAIR]), _bd(wub[s][:, PAIR:2 * PAIR])], axis=1))
          for s in range(ns)]
    q = [(blk(r_t, i, p) + qy[s][:, 0:PAIR]).astype(BF16) for s, (i, p) in enumerate(streams)]
    y_loc = [qy[s][:, PAIR:2 * PAIR] + nv_mv[s][L:2 * L] for s in range(ns)]
    zeros_b = jnp.zeros((L, PAIR), BF16)
    mg = [_dot_tn(jnp.concatenate([wub[s], jnp.concatenate([zeros_b, blk(vb, i, p)], axis=1)], axis=0),
                  jnp.concatenate([blk(b_tb, i, p), blk(k_tb, i, p)], axis=0))
          for s, (i, p) in enumerate(streams)]
    p_end = [e_in[(i + 1) * L - 1:(i + 1) * L, p * PAIR:(p + 1) * PAIR] for i, p in streams]
    m_t = [(jnp.where(same_head, mg[s][0:PAIR], 0.0) * p_end[s]).astype(BF16) for s in range(ns)]
    g_t = [jnp.where(lane_lo, mg[s][PAIR:PAIR + RWKV_HEAD], mg[s][PAIR + RWKV_HEAD:2 * PAIR]) * p_end[s]
           for s in range(ns)]

    y_rows = []
    for i in range(G):
        y_pairs = []
        for p in range(RWKV_PAIRS):
            s = i * RWKV_PAIRS + p
            s0 = s_scr[p]
            s0b = s0.astype(BF16)
            y_pairs.append(_dot_nt(q[s], _bd(s0b)) + y_loc[s])
            s_scr[p] = s0 * p_end[s] + _dot(s0b, m_t[s]) + g_t[s]
        y_rows.append(jnp.concatenate(y_pairs, axis=-1))
    y = jnp.concatenate(y_rows, axis=0)
    y_ref[...] = _rwkv_finish(y, r, k, v, g, rk_ref, lng_ref, lnb_ref)

    @pl.when(c == pl.num_programs(1) - 1)
    def _():
        sout_ref[0] = s_scr[...]
        shift_ref[0] = upad_scr[pad - 1:pad, :]


_RWKV_PARAM_NAMES = ("mu", "w0", "w2", "a0", "a2", "g2", "k_k", "k_a", "r_k", "ln_g", "ln_b")


def _rwkv(u, p, *, batch, seq):
    rows = RWKV_CHUNK * RWKV_GROUP
    nc = seq // rows
    params = [p[n] for n in _RWKV_PARAM_NAMES]
    sspec = pl.BlockSpec((1, RWKV_PAIRS, RWKV_HEAD, PAIR), lambda b, c: (b, 0, 0, 0))
    y, shift, s_last = pl.pallas_call(
        functools.partial(_rwkv_body, chunk=RWKV_CHUNK, group=RWKV_GROUP),
        grid=(batch, nc),
        in_specs=[pl.BlockSpec((rows, RWKV_PROJ), lambda b, c: (b * nc + c, 0))] + [_full_spec(a.shape) for a in params],
        out_specs=[pl.BlockSpec((rows, GROUP_WIDTH), lambda b, c: (b * nc + c, 0)),
                   pl.BlockSpec((1, 1, RWKV_PROJ), lambda b, c: (b, 0, 0)), sspec],
        out_shape=[jax.ShapeDtypeStruct((batch * seq, GROUP_WIDTH), F32),
                   jax.ShapeDtypeStruct((batch, 1, RWKV_PROJ), F32),
                   jax.ShapeDtypeStruct((batch, RWKV_PAIRS, RWKV_HEAD, PAIR), F32)],
        scratch_shapes=[pltpu.VMEM((SUBLANES + rows, RWKV_PROJ), F32),
                        pltpu.VMEM((RWKV_PAIRS, RWKV_HEAD, PAIR), F32)],
        compiler_params=_cparams("parallel", "arbitrary"),
        name="rwkv",
    )(u, *params)
    s_last = s_last.reshape(batch, RWKV_PAIRS, RWKV_HEAD, 2, RWKV_HEAD).transpose(0, 1, 3, 2, 4).reshape(
        batch, RWKV_HEADS, RWKV_HEAD, RWKV_HEAD)
    return y, shift.reshape(batch, RWKV_PROJ), s_last


def _rwkv_step_body(u_ref, shift0_ref, s0_ref, mu_ref, w0_ref, w2_ref, a0_ref, a2_ref, g2_ref, kk_ref, ka_ref,
                    rk_ref, lng_ref, lnb_ref, y_ref, sout_ref, r_scr, w_scr, k_scr, b_scr, nkk_scr, v_scr, y_scr,
                    *, seq, batch):
    T, B = seq, batch
    GW = GROUP_WIDTH
    j = pl.program_id(0)
    tiles = RWKV_STEP_TILES

    def pointwise(t):
        u = u_ref[t * B:(t + 1) * B, :]
        prev = shift0_ref[...] if t == 0 else u_ref[(t - 1) * B:t * B, :]
        return _rwkv_pointwise(u, prev, mu_ref, w0_ref, w2_ref, a0_ref, a2_ref, g2_ref, kk_ref, ka_ref)

    @pl.when(j == 0)
    def _():
        for t in range(T):
            r, k, v, logdecay, a, _, kk = pointwise(t)
            w = jnp.exp(logdecay)
            v_scr[t] = v
            for h in range(RWKV_HEADS):
                twice = lambda x: jnp.concatenate([x[:, h * RWKV_HEAD:(h + 1) * RWKV_HEAD]] * 2, axis=1)
                r_scr[t, h] = twice(r)
                w_scr[t, h] = twice(w)
                k_scr[t, h] = twice(k)
                b_scr[t, h] = twice(kk * a)
                nkk_scr[t, h] = twice(-kk)
        y_scr[...] = jnp.zeros(y_scr.shape, F32)

    i0 = j * (2 * tiles)
    head = i0 // RWKV_HEAD
    back = lax.rem(GW - i0, GW)
    v_r = [_roll_lanes(v_scr[t], back) for t in range(T)]
    r_t = [r_scr[t, head] for t in range(T)]
    w_t = [w_scr[t, head] for t in range(T)]
    k_t = [k_scr[t, head] for t in range(T)]
    b_t = [b_scr[t, head] for t in range(T)]
    nkk_t = [nkk_scr[t, head] for t in range(T)]
    lo = lax.broadcasted_iota(jnp.int32, (B, PAIR), 1) < RWKV_HEAD
    lane = lax.broadcasted_iota(jnp.int32, (B, GW), 1)
    y_loc = [jnp.zeros((B, GW), F32) for _ in range(T)]
    for q in range(tiles):
        s = s0_ref[:, q * PAIR:(q + 1) * PAIR]
        for t in range(T):
            sa = _half_sums(s * nkk_t[t], lo)
            v_col = jnp.where(lo, v_r[t][:, 2 * q:2 * q + 1], v_r[t][:, 2 * q + 1:2 * q + 2])
            s = s * w_t[t] + v_col * k_t[t] + sa * b_t[t]
            sr = s * r_t[t]
            y_lo = jnp.sum(jnp.where(lo, sr, 0.0), axis=-1, keepdims=True)
            y_hi = jnp.sum(jnp.where(lo, 0.0, sr), axis=-1, keepdims=True)
            y_loc[t] = jnp.where(lane == 2 * q, y_lo, jnp.where(lane == 2 * q + 1, y_hi, y_loc[t]))
        sout_ref[:, q * PAIR:(q + 1) * PAIR] = s
    for t in range(T):
        y_scr[t] = y_scr[t] + _roll_lanes(y_loc[t], i0)

    @pl.when(j == pl.num_programs(0) - 1)
    def _():
        for t in range(T):
            r, k, v, _, _, g, _ = pointwise(t)
            y_ref[t * B:(t + 1) * B, :] = _rwkv_finish(y_scr[t], r, k, v, g, rk_ref, lng_ref, lnb_ref)


def _rwkv_step(u, shift0, s0, p, *, batch, seq):
    n = batch * seq
    nstate = RWKV_HEADS * RWKV_HEAD * RWKV_HEAD
    tile_w = RWKV_STEP_TILES * PAIR
    params = [p[nm] for nm in _RWKV_PARAM_NAMES]
    sspec = pl.BlockSpec((batch, tile_w), lambda j: (0, j))
    dup = pltpu.VMEM((seq, RWKV_HEADS, batch, PAIR), F32)
    y, s_new = pl.pallas_call(
        functools.partial(_rwkv_step_body, seq=seq, batch=batch),
        grid=(nstate // tile_w,),
        in_specs=[_full_spec((n, RWKV_PROJ)), _full_spec((batch, RWKV_PROJ)), sspec]
                 + [_full_spec(a.shape) for a in params],
        out_specs=[_full_spec((n, GROUP_WIDTH)), sspec],
        out_shape=[jax.ShapeDtypeStruct((n, GROUP_WIDTH), F32), jax.ShapeDtypeStruct((batch, nstate), F32)],
        scratch_shapes=[dup, dup, dup, dup, dup,
                        pltpu.VMEM((seq, batch, GROUP_WIDTH), F32), pltpu.VMEM((seq, batch, GROUP_WIDTH), F32)],
        compiler_params=_cparams("arbitrary"),
        name="rwkv_step",
    )(u, shift0, s0.reshape(batch, nstate), *params)
    return y, s_new.reshape(batch, RWKV_HEADS, RWKV_HEAD, RWKV_HEAD)


def _s5_body(u_ref, hre0_ref, him0_ref, are_ref, aim_ref, bmat_ref, cmat_ref, d_ref, gw_ref, gb_ref,
             y_ref, hre_ref, him_ref, hs_scr, *, steps):
    c = pl.program_id(1)
    ns = S5_WIDTH
    bsub = u_ref.shape[1]

    @pl.when(c == 0)
    def _():
        hre_ref[...] = hre0_ref[...]
        him_ref[...] = him0_ref[...]

    u = u_ref[...].reshape(steps * bsub, GROUP_WIDTH)
    hs_scr[...] = _dot(u.astype(BF16), bmat_ref[...])
    are = jnp.broadcast_to(are_ref[...], (bsub, ns))
    aim = jnp.broadcast_to(aim_ref[...], (bsub, ns))

    def step(t, carry):
        hre, him = carry
        r0 = pl.multiple_of(t * bsub, bsub)
        nre = are * hre - aim * him + hs_scr[pl.ds(r0, bsub), 0:ns]
        nim = are * him + aim * hre + hs_scr[pl.ds(r0, bsub), ns:2 * ns]
        hs_scr[pl.ds(r0, bsub), 0:ns] = nre
        hs_scr[pl.ds(r0, bsub), ns:2 * ns] = nim
        return nre, nim

    hre, him = lax.fori_loop(0, steps, step, (hre_ref[...], him_ref[...]))
    hre_ref[...] = hre
    him_ref[...] = him
    y = _dot(hs_scr[...].astype(BF16), cmat_ref[...]) + u * d_ref[...]
    y = _gelu_tanh(y)
    yy = _dot(y.astype(BF16), gw_ref[...]) + gb_ref[...]
    out = yy[:, 0:GROUP_WIDTH] * _sigmoid(yy[:, GROUP_WIDTH:2 * GROUP_WIDTH])
    y_ref[...] = out.reshape(steps, bsub, GROUP_WIDTH)


def _s5(u_tm, hre0, him0, lp):
    seq, batch, _ = u_tm.shape
    steps = min(TM_CHUNK, seq)
    bsub = SUBLANES
    hspec = pl.BlockSpec((bsub, S5_WIDTH), lambda b, c: (b, 0))
    tspec = pl.BlockSpec((steps, bsub, GROUP_WIDTH), lambda b, c: (c, b, 0))
    consts = (lp["s5_are"], lp["s5_aim"], lp["s5_bmat"], lp["s5_cmat"], lp["s5_d"], lp["s5_gw"], lp["s5_gb"])
    return pl.pallas_call(
        functools.partial(_s5_body, steps=steps),
        grid=(batch // bsub, seq // steps),
        in_specs=[tspec, hspec, hspec] + [_full_spec(a.shape) for a in consts],
        out_specs=[tspec, hspec, hspec],
        out_shape=[jax.ShapeDtypeStruct((seq, batch, GROUP_WIDTH), F32),
                   jax.ShapeDtypeStruct((batch, S5_WIDTH), F32),
                   jax.ShapeDtypeStruct((batch, S5_WIDTH), F32)],
        scratch_shapes=[pltpu.VMEM((steps * bsub, 2 * S5_WIDTH), F32)],
        compiler_params=_cparams("parallel", "arbitrary"),
        name="s5",
    )(u_tm, hre0, him0, *consts)


def _pool_body(u_ref, buf0_ref, pw_ref, sc_ref, y_ref, buf_ref, f_scr, *, steps, pos0):
    c = pl.program_id(1)
    bsub = u_ref.shape[1]
    GW = GROUP_WIDTH
    halo = POOL_BUF + 1

    @pl.when(c == 0)
    def _():
        f_scr[0] = jnp.zeros((bsub, GW), F32)
        for i in range(POOL_BUF):
            f_scr[1 + i] = buf0_ref[:, i * GW:(i + 1) * GW]

    u = u_ref[...]
    f_scr[halo:halo + steps] = u
    f = f_scr[...]
    s2 = f[1:] + f[:-1]
    s4 = s2[2:] + s2[:-2]
    s8 = s4[4:] + s4[:-4]
    s16 = s8[8:] + s8[:-8]
    f_scr[0:halo] = f[steps:steps + halo]
    lane = lax.broadcasted_iota(jnp.int32, (steps, bsub, GW), 2)
    tpos = lax.broadcasted_iota(jnp.int32, (steps, bsub, GW), 0) + (pos0 + 1) + c * steps
    win = jnp.where(lane < POOL_CH, s2[halo - 1:halo - 1 + steps],
                    jnp.where(lane < 2 * POOL_CH, s4[halo - 3:halo - 3 + steps],
                              jnp.where(lane < 3 * POOL_CH, s8[halo - 7:halo - 7 + steps],
                                        s16[halo - 15:halo - 15 + steps])))
    wlen = jnp.where(lane < POOL_CH, POOL_WINDOWS[0],
                     jnp.where(lane < 2 * POOL_CH, POOL_WINDOWS[1],
                               jnp.where(lane < 3 * POOL_CH, POOL_WINDOWS[2], POOL_WINDOWS[3])))
    cnt = jnp.minimum(tpos, wlen).astype(F32)
    pooled = (win / cnt - u).reshape(steps * bsub, GW)
    y = _dot(pooled.astype(BF16), pw_ref[...]) * sc_ref[...]
    y_ref[...] = y.reshape(steps, bsub, GW)

    @pl.when(c == pl.num_programs(1) - 1)
    def _():
        for i in range(POOL_BUF):
            buf_ref[:, i * GW:(i + 1) * GW] = f_scr[1 + i]


def _pool(u_tm, buf0, lp, *, pos0):
    seq, batch, _ = u_tm.shape
    steps = min(TM_CHUNK, seq)
    bsub = SUBLANES
    flat = POOL_BUF * GROUP_WIDTH
    tspec = pl.BlockSpec((steps, bsub, GROUP_WIDTH), lambda b, c: (c, b, 0))
    bspec = pl.BlockSpec((bsub, flat), lambda b, c: (b, 0))
    y, buf = pl.pallas_call(
        functools.partial(_pool_body, steps=steps, pos0=pos0),
        grid=(batch // bsub, seq // steps),
        in_specs=[tspec, bspec, _full_spec(lp["pool_w"].shape), _full_spec(lp["pool_scale"].shape)],
        out_specs=[tspec, bspec],
        out_shape=[jax.ShapeDtypeStruct((seq, batch, GROUP_WIDTH), F32), jax.ShapeDtypeStruct((batch, flat), F32)],
        scratch_shapes=[pltpu.VMEM((POOL_BUF + 1 + steps, bsub, GROUP_WIDTH), F32)],
        compiler_params=_cparams("parallel", "arbitrary"),
        name="pool",
    )(u_tm, buf0.reshape(batch, flat), lp["pool_w"], lp["pool_scale"])
    return y, buf.reshape(batch, POOL_BUF, GROUP_WIDTH)


def _block_diag(blocks):
    g, r, c = blocks.shape
    eye = jnp.eye(g, dtype=blocks.dtype)
    return (eye[:, None, :, None] * blocks[:, :, None, :]).reshape(g * r, g * c)


def _pad_lanes(v, n=LANES):
    return jnp.pad(v, (0, n - v.shape[0])).reshape(1, n)


def _layer_params(l, P):
    row = lambda a: a.reshape(1, -1)
    w_in = P["w_in"][l]
    z_w, xbc_w, dt_w, rw_w, s5_w, pool_w = jnp.split(
        w_in, [256, 256 + 768, 256 + 768 + 4, 1028 + 1024, 1028 + 1024 + 256], axis=1)
    w_all = jnp.concatenate([z_w, xbc_w, rw_w, s5_w, pool_w, jnp.pad(dt_w, ((0, 0), (0, LANES - SSD_HEADS)))], axis=1)

    lam = lax.complex(P["s5_lam_re"][l], P["s5_lam_im"][l])
    a_bar = jnp.exp(lam * jnp.exp(P["s5_log_step"][l])[:, None])
    b_bar = ((a_bar - 1.0) / lam)[..., None] * lax.complex(P["s5_b_re"][l], P["s5_b_im"][l])
    b_t = jnp.swapaxes(b_bar, 1, 2)
    bmat = jnp.concatenate([_block_diag(jnp.real(b_t)), _block_diag(jnp.imag(b_t))], axis=1)
    c_t = jnp.swapaxes(lax.complex(P["s5_c_re"][l], P["s5_c_im"][l]), 1, 2)
    cmat = jnp.concatenate([_block_diag(jnp.real(c_t)), -_block_diag(jnp.imag(c_t))], axis=0)
    head_expand = jnp.pad(jnp.repeat(jnp.eye(SSD_HEADS, dtype=F32), SSD_HEAD_DIM, axis=1),
                          ((0, LANES - SSD_HEADS), (0, 0)))

    return dict(
        norm_ffn1=row(P["norm_ffn1"][l]), ffn1_in=P["ffn1_in"][l].astype(BF16), ffn1_out=P["ffn1_out"][l].astype(BF16),
        norm_mix=row(P["norm_mix"][l]), w_all=w_all.astype(BF16),
        conv_w=P["ssd_conv_w"][l], conv_b=row(P["ssd_conv_b"][l]),
        dt_bias=_pad_lanes(P["ssd_dt_bias"][l]), a_log=_pad_lanes(P["ssd_a_log"][l]),
        a_neg_exp=row(jnp.repeat(-jnp.exp(P["ssd_a_log"][l]), SSD_HEAD_DIM)), head_expand=head_expand,
        d_skip=row(jnp.repeat(P["ssd_d"][l], SSD_HEAD_DIM)), ssd_norm=row(P["ssd_norm"][l]),
        rwkv=dict(mu=row(P["rwkv_mu"][l]), w0=row(P["rwkv_w0"][l]), w2=P["rwkv_w2"][l].astype(BF16),
                  a0=row(P["rwkv_a0"][l]), a2=P["rwkv_a2"][l].astype(BF16), g2=P["rwkv_g2"][l].astype(BF16),
                  k_k=row(P["rwkv_k_k"][l]), k_a=row(P["rwkv_k_a"][l]), r_k=row(P["rwkv_r_k"][l]),
                  ln_g=row(P["rwkv_ln_g"][l]), ln_b=row(P["rwkv_ln_b"][l])),
        s5_are=row(jnp.real(a_bar)), s5_aim=row(jnp.imag(a_bar)), s5_bmat=bmat.astype(BF16), s5_cmat=cmat.astype(BF16),
        s5_d=row(P["s5_d"][l]), s5_gw=P["s5_glu_w"][l].astype(BF16), s5_gb=row(P["s5_glu_b"][l]),
        pool_w=_block_diag(P["pool_w"][l]).astype(BF16), pool_scale=row(P["pool_scale"][l]),
        w_out=P["w_out"][l].astype(BF16),
        norm_ffn2=row(P["norm_ffn2"][l]), ffn2_in=P["ffn2_in"][l].astype(BF16), ffn2_out=P["ffn2_out"][l].astype(BF16),
    )


def _to_tm(rows, batch, seq):
    return jnp.swapaxes(rows.reshape(batch, seq, rows.shape[-1]), 0, 1)


def _from_tm(x_tm):
    seq, batch, w = x_tm.shape
    return jnp.swapaxes(x_tm, 0, 1).reshape(batch * seq, w)


def _mixers_prompt(lp, proj, *, batch, seq):
    z, xbc, ur, us5, upool, dtr = proj
    y_ssd, conv_new, ssd_new = _ssd(z, xbc, dtr, lp, batch=batch, seq=seq)
    y_rwkv, shift_new, rwkv_new = _rwkv(ur, lp["rwkv"], batch=batch, seq=seq)
    zeros = jnp.zeros((batch, S5_WIDTH), F32)
    y_s5, s5re, s5im = _s5(_to_tm(us5, batch, seq), zeros, zeros, lp)
    y_pool, pool_new = _pool(_to_tm(upool, batch, seq), jnp.zeros((batch, POOL_BUF, GROUP_WIDTH), F32), lp, pos0=0)
    ys = (y_ssd, y_rwkv, _from_tm(y_s5), _from_tm(y_pool))
    states = (conv_new, ssd_new, shift_new, rwkv_new, s5re.reshape(batch, S5_GROUPS, S5_STATE),
              s5im.reshape(batch, S5_GROUPS, S5_STATE), pool_new)
    return ys, states


def _mixers_decode(lp, proj, states, *, batch, seq):
    z, xbc, ur, us5, upool, dtr = proj
    conv0, ssd0, shift0, rwkv0, s5re0, s5im0, pool0 = states
    y_ssd, conv_new, ssd_new = _ssd_step(z, xbc, dtr, conv0, ssd0, lp, batch=batch, seq=seq)
    y_rwkv, rwkv_new = _rwkv_step(ur, shift0, rwkv0, lp["rwkv"], batch=batch, seq=seq)
    shift_new = ur[(seq - 1) * batch:, :]
    tm = lambda a: a.reshape(seq, batch, a.shape[-1])
    y_s5, s5re, s5im = _s5(tm(us5), s5re0.reshape(batch, S5_WIDTH), s5im0.reshape(batch, S5_WIDTH), lp)
    y_pool, pool_new = _pool(tm(upool), pool0, lp, pos0=PAST_LEN)
    rows = lambda a: a.reshape(seq * batch, a.shape[-1])
    ys = (y_ssd, y_rwkv, rows(y_s5), rows(y_pool))
    new_states = (conv_new, ssd_new, shift_new, rwkv_new, s5re.reshape(batch, S5_GROUPS, S5_STATE),
                  s5im.reshape(batch, S5_GROUPS, S5_STATE), pool_new)
    return ys, new_states


_WIDTHS = (GROUP_WIDTH, SSD_CONV_DIM, RWKV_PROJ, GROUP_WIDTH, GROUP_WIDTH, LANES)


def _trunk(x, layer_params, norm_final, mixers):
    states = []
    mix, lp = None, None
    for l, lp_next in enumerate(layer_params):
        if l > 0:
            x = _ffn(x, lp["norm_ffn2"], lp["ffn2_in"], lp["ffn2_out"], mix=mix, wmix=lp["w_out"])
        lp = lp_next
        x = _ffn(x, lp["norm_ffn1"], lp["ffn1_in"], lp["ffn1_out"])
        proj = _inproj(x, lp["norm_mix"], lp["w_all"], _WIDTHS)
        mix, st = mixers(l, lp, proj)
        states.append(st)
    x = _ffn(x, lp["norm_ffn2"], lp["ffn2_in"], lp["ffn2_out"], mix=mix, wmix=lp["w_out"], gf=norm_final)
    return x, [jnp.stack([st[i] for st in states]) for i in range(7)]


def kernel(x_prompt, x_sample, state_ssd_conv, state_ssd, state_rwkv_shift, state_rwkv, state_s5_re, state_s5_im, state_pool, norm_ffn1, ffn1_in, ffn1_out, norm_mix, w_in, ssd_conv_w, ssd_conv_b, ssd_dt_bias, ssd_a_log, ssd_d, ssd_norm, rwkv_mu, rwkv_w0, rwkv_w2, rwkv_a0, rwkv_a2, rwkv_g2, rwkv_k_k, rwkv_k_a, rwkv_r_k, rwkv_ln_g, rwkv_ln_b, s5_lam_re, s5_lam_im, s5_log_step, s5_b_re, s5_b_im, s5_c_re, s5_c_im, s5_d, s5_glu_w, s5_glu_b, pool_w, pool_scale, w_out, norm_ffn2, ffn2_in, ffn2_out, norm_final):
    P = dict(norm_ffn1=norm_ffn1, ffn1_in=ffn1_in, ffn1_out=ffn1_out, norm_mix=norm_mix, w_in=w_in,
             ssd_conv_w=ssd_conv_w, ssd_conv_b=ssd_conv_b, ssd_dt_bias=ssd_dt_bias, ssd_a_log=ssd_a_log,
             ssd_d=ssd_d, ssd_norm=ssd_norm, rwkv_mu=rwkv_mu, rwkv_w0=rwkv_w0, rwkv_w2=rwkv_w2, rwkv_a0=rwkv_a0,
             rwkv_a2=rwkv_a2, rwkv_g2=rwkv_g2, rwkv_k_k=rwkv_k_k, rwkv_k_a=rwkv_k_a,
             rwkv_r_k=rwkv_r_k.reshape(rwkv_r_k.shape[0], -1), rwkv_ln_g=rwkv_ln_g, rwkv_ln_b=rwkv_ln_b,
             s5_lam_re=s5_lam_re, s5_lam_im=s5_lam_im, s5_log_step=s5_log_step, s5_b_re=s5_b_re, s5_b_im=s5_b_im,
             s5_c_re=s5_c_re, s5_c_im=s5_c_im, s5_d=s5_d, s5_glu_w=s5_glu_w, s5_glu_b=s5_glu_b, pool_w=pool_w,
             pool_scale=pool_scale, w_out=w_out, norm_ffn2=norm_ffn2, ffn2_in=ffn2_in, ffn2_out=ffn2_out)
    depth = norm_ffn1.shape[0]
    bp, tp, d = x_prompt.shape
    bs, ts, _ = x_sample.shape
    layer_params = [_layer_params(l, P) for l in range(depth)]
    gf = norm_final.reshape(1, -1)
    sample_states = (state_ssd_conv, state_ssd, state_rwkv_shift, state_rwkv, state_s5_re, state_s5_im, state_pool)

    y_p, st_p = _trunk(x_prompt.reshape(bp * tp, d), layer_params, gf,
                       lambda l, lp, proj: _mixers_prompt(lp, proj, batch=bp, seq=tp))
    x_s = jnp.swapaxes(x_sample, 0, 1).reshape(ts * bs, d)
    y_s, st_s = _trunk(x_s, layer_params, gf,
                       lambda l, lp, proj: _mixers_decode(lp, proj, tuple(s[l] for s in sample_states),
                                                          batch=bs, seq=ts))
    outs = [y_p.reshape(bp, tp, d), jnp.swapaxes(y_s.reshape(ts, bs, d), 0, 1)]
    for a, b in zip(st_p, st_s):
        outs += [a, b]
    return tuple(outs)
```

```python
import functools
import math

import jax
import jax.numpy as jnp
from jax import lax
from jax.experimental import pallas as pl
from jax.experimental.pallas import tpu as pltpu

F32 = jnp.float32
BF16 = jnp.bfloat16
HIGHEST = lax.Precision.HIGHEST

SUBLANES = 8
LANES = 128
VMEM_LIMIT_BYTES = 56 * 1024 * 1024

GROUP_WIDTH = 256
SSD_HEAD_DIM = 64
SSD_HEADS = 4
SSD_GROUPS = 2
SSD_STATE = 128
SSD_CONV = 4
SSD_CONV_DIM = GROUP_WIDTH + 2 * SSD_GROUPS * SSD_STATE
SSD_CHUNK = 128
RWKV_HEAD = 64
RWKV_HEADS = 4
RWKV_PROJ = 1024
RWKV_LN_EPS = 64e-5
RWKV_CHUNK = 64
RWKV_GROUP = 4
S5_GROUP_CH = 16
S5_GROUPS = 16
S5_STATE = 64
S5_WIDTH = S5_GROUPS * S5_STATE
POOL_WINDOWS = (2, 4, 8, 16)
POOL_CH = 64
POOL_BUF = 15
RMS_EPS = 1e-6
PAST_LEN = 16384

ROW_TILE = 512
FFN_CHUNK = 256
TM_CHUNK = 64
SSD_STEP_TILES = 16
RWKV_STEP_TILES = 8


def _cparams(*sem):
    return pltpu.CompilerParams(dimension_semantics=sem, vmem_limit_bytes=VMEM_LIMIT_BYTES)


def _dot(a, b, **kw):
    return jnp.dot(a, b, preferred_element_type=F32, **kw)


def _dot_nt(a, b):
    return lax.dot_general(a, b, (((1,), (1,)), ((), ())), preferred_element_type=F32)


def _dot_tn(a, b):
    return lax.dot_general(a, b, (((0,), (0,)), ((), ())), preferred_element_type=F32)


def _sigmoid(x):
    return 1.0 / (1.0 + jnp.exp(-x))


def _silu(x):
    return x * _sigmoid(x)


def _softplus(x):
    return jnp.maximum(x, 0.0) + jnp.log(1.0 + jnp.exp(-jnp.abs(x)))


def _gelu_tanh(x):
    c = math.sqrt(2.0 / math.pi)
    return x * (0.5 * (1.0 + jnp.tanh(c * (x + 0.044715 * (x * x * x)))))


def _rms(x, g):
    return x * lax.rsqrt(jnp.mean(x * x, axis=-1, keepdims=True) + RMS_EPS) * g


def _full_spec(shape):
    n = len(shape)
    return pl.BlockSpec(shape, lambda *_: (0,) * n)


def _roll_lanes(x, shift):
    return pltpu.roll(x, shift, axis=x.ndim - 1)


def _ffn_body(*refs, has_mix, final_norm):
    it = iter(refs)
    x_ref = next(it)
    x = x_ref[...]
    if has_mix:
        y_refs = [next(it) for _ in range(4)]
        wmix_ref = next(it)
        for i, y_ref in enumerate(y_refs):
            x = x + _dot(y_ref[...].astype(BF16), wmix_ref[i * GROUP_WIDTH:(i + 1) * GROUP_WIDTH, :])
    g_ref, wi_ref, wo_ref = next(it), next(it), next(it)
    gf_ref = next(it) if final_norm else None
    o_ref = next(it)
    h = _rms(x, g_ref[...]).astype(BF16)
    d_ff = wo_ref.shape[0]
    acc = jnp.zeros_like(x)
    for c in range(d_ff // FFN_CHUNK):
        lo = c * FFN_CHUNK
        gate = _dot(h, wi_ref[:, lo:lo + FFN_CHUNK])
        up = _dot(h, wi_ref[:, d_ff + lo:d_ff + lo + FFN_CHUNK])
        act = (_silu(gate) * up).astype(BF16)
        acc = acc + _dot(act, wo_ref[lo:lo + FFN_CHUNK, :])
    x = x + 0.5 * acc
    if final_norm:
        x = _rms(x, gf_ref[...])
    o_ref[...] = x


def _ffn(x, g, wi, wo, mix=None, wmix=None, gf=None):
    rows, d = x.shape
    row_spec = lambda w: pl.BlockSpec((ROW_TILE, w), lambda i: (i, 0))
    args, specs = [x], [row_spec(d)]
    if mix is not None:
        for y in mix:
            args.append(y)
            specs.append(row_spec(y.shape[1]))
        args.append(wmix)
        specs.append(_full_spec(wmix.shape))
    for a in (g, wi, wo) + ((gf,) if gf is not None else ()):
        args.append(a)
        specs.append(_full_spec(a.shape))
    return pl.pallas_call(
        functools.partial(_ffn_body, has_mix=mix is not None, final_norm=gf is not None),
        grid=(rows // ROW_TILE,),
        in_specs=specs,
        out_specs=row_spec(d),
        out_shape=jax.ShapeDtypeStruct((rows, d), F32),
        compiler_params=_cparams("parallel"),
        name="ffn",
    )(*args)


def _inproj_body(x_ref, g_ref, w_ref, *o_refs):
    h = _rms(x_ref[...], g_ref[...]).astype(BF16)
    off = 0
    for o_ref in o_refs:
        n = o_ref.shape[-1]
        o_ref[...] = _dot(h, w_ref[:, off:off + n])
        off += n


def _inproj(x, g, w, widths):
    rows, d = x.shape
    row_spec = lambda w_: pl.BlockSpec((ROW_TILE, w_), lambda i: (i, 0))
    return pl.pallas_call(
        _inproj_body,
        grid=(rows // ROW_TILE,),
        in_specs=[row_spec(d), _full_spec(g.shape), _full_spec(w.shape)],
        out_specs=[row_spec(n) for n in widths],
        out_shape=[jax.ShapeDtypeStruct((rows, n), F32) for n in widths],
        compiler_params=_cparams("parallel"),
        name="inproj",
    )(x, g, w)


def _ssd_body(z_ref, xbc_ref, dt_ref, cw_ref, cb_ref, dtb_ref, alog_ref, dsk_ref, ng_ref,
              y_ref, conv_ref, hout_ref, xpad_scr, h_scr, *, chunk):
    L = chunk
    c = pl.program_id(1)
    pad = SUBLANES
    halo = SSD_CONV - 1

    @pl.when(c == 0)
    def _():
        xpad_scr[0:pad, :] = jnp.zeros((pad, SSD_CONV_DIM), F32)
        h_scr[...] = jnp.zeros(h_scr.shape, F32)

    xpad_scr[pad:pad + L, :] = xbc_ref[...]
    conv = cb_ref[...] + cw_ref[0:1, :] * xpad_scr[pad - halo:pad - halo + L, :]
    for j in range(1, SSD_CONV):
        conv = conv + cw_ref[j:j + 1, :] * xpad_scr[pad - halo + j:pad - halo + j + L, :]
    xpad_scr[pad - halo:pad, :] = xpad_scr[pad + L - halo:pad + L, :]
    conv = _silu(conv)
    xs = conv[:, 0:GROUP_WIDTH]
    bm = conv[:, GROUP_WIDTH:2 * GROUP_WIDTH].astype(BF16)
    cm = conv[:, 2 * GROUP_WIDTH:3 * GROUP_WIDTH].astype(BF16)

    row = lax.broadcasted_iota(jnp.int32, (L, L), 0)
    col = lax.broadcasted_iota(jnp.int32, (L, L), 1)
    causal = row >= col
    dt = _softplus(dt_ref[...] + dtb_ref[...])
    da = dt * (-jnp.exp(alog_ref[...]))
    acs = _dot(jnp.where(causal, 1.0, 0.0).astype(F32), da, precision=HIGHEST)
    acs_t = acs.T
    e_acs = jnp.exp(acs)
    acs_last = acs[L - 1:L, :]
    e_end = jnp.exp(acs_last - acs)
    e_last = jnp.exp(acs_last)

    ys = []
    for h in range(SSD_HEADS):
        g = h // (SSD_HEADS // SSD_GROUPS)
        bg = bm[:, g * SSD_STATE:(g + 1) * SSD_STATE]
        cg = cm[:, g * SSD_STATE:(g + 1) * SSD_STATE]
        x_h = xs[:, h * SSD_HEAD_DIM:(h + 1) * SSD_HEAD_DIM]
        xdt = x_h * dt[:, h:h + 1]
        seg = acs[:, h:h + 1] - acs_t[h:h + 1, :]
        decay = jnp.exp(jnp.where(causal, seg, -jnp.inf))
        scores = _dot_nt(cg, bg) * decay
        y_h = _dot(scores.astype(BF16), xdt.astype(BF16))
        h_prev = h_scr[h]
        y_h = y_h + _dot_nt(cg, h_prev.astype(BF16)) * e_acs[:, h:h + 1]
        st = _dot_tn((xdt * e_end[:, h:h + 1]).astype(BF16), bg)
        h_scr[h] = h_prev * e_last[:, h:h + 1] + st
        ys.append(y_h)
    y = jnp.concatenate(ys, axis=-1) + xs * dsk_ref[...]
    y = y * _silu(z_ref[...])
    y_ref[...] = _rms(y, ng_ref[...])

    @pl.when(c == pl.num_programs(1) - 1)
    def _():
        hout_ref[0] = h_scr[...]
        conv_ref[0] = xpad_scr[pad - halo:pad, :]


def _ssd(z, xbc, dtr, lp, *, batch, seq):
    chunk = SSD_CHUNK
    nc = seq // chunk
    rspec = lambda w: pl.BlockSpec((chunk, w), lambda b, c: (b * nc + c, 0))
    consts = (lp["conv_w"], lp["conv_b"], lp["dt_bias"], lp["a_log"], lp["d_skip"], lp["ssd_norm"])
    return pl.pallas_call(
        functools.partial(_ssd_body, chunk=chunk),
        grid=(batch, nc),
        in_specs=[rspec(GROUP_WIDTH), rspec(SSD_CONV_DIM), rspec(LANES)] + [_full_spec(a.shape) for a in consts],
        out_specs=[rspec(GROUP_WIDTH),
                   pl.BlockSpec((1, SSD_CONV - 1, SSD_CONV_DIM), lambda b, c: (b, 0, 0)),
                   pl.BlockSpec((1, SSD_HEADS, SSD_HEAD_DIM, SSD_STATE), lambda b, c: (b, 0, 0, 0))],
        out_shape=[jax.ShapeDtypeStruct((batch * seq, GROUP_WIDTH), F32),
                   jax.ShapeDtypeStruct((batch, SSD_CONV - 1, SSD_CONV_DIM), F32),
                   jax.ShapeDtypeStruct((batch, SSD_HEADS, SSD_HEAD_DIM, SSD_STATE), F32)],
        scratch_shapes=[pltpu.VMEM((SUBLANES + chunk, SSD_CONV_DIM), F32),
                        pltpu.VMEM((SSD_HEADS, SSD_HEAD_DIM, SSD_STATE), F32)],
        compiler_params=_cparams("parallel", "arbitrary"),
        name="ssd",
    )(z, xbc, dtr, *consts)


def _ssd_step_body(z_ref, xbc_ref, dt_ref, conv0_ref, h0_ref, cw_ref, cb_ref, dtb_ref, aneg_ref, dsk_ref, ng_ref,
                   hexp_ref, y_ref, conv_ref, hout_ref, xs_scr, bm_scr, cm_scr, xdt_scr, dec_scr, y_scr,
                   *, seq, batch):
    T, B = seq, batch
    GW = GROUP_WIDTH
    j = pl.program_id(0)
    tiles = SSD_STEP_TILES

    @pl.when(j == 0)
    def _():
        rows = [conv0_ref[:, i * SSD_CONV_DIM:(i + 1) * SSD_CONV_DIM] for i in range(SSD_CONV - 1)]
        rows += [xbc_ref[t * B:(t + 1) * B, :] for t in range(T)]
        for t in range(T):
            conv = cb_ref[...] + cw_ref[0:1, :] * rows[t]
            for i in range(1, SSD_CONV):
                conv = conv + cw_ref[i:i + 1, :] * rows[t + i]
            conv = _silu(conv)
            xs = conv[:, 0:GW]
            xs_scr[t] = xs
            for g in range(SSD_GROUPS):
                bm_scr[t, g] = conv[:, GW + g * SSD_STATE:GW + (g + 1) * SSD_STATE].T
                cm_scr[t, g] = conv[:, 2 * GW + g * SSD_STATE:2 * GW + (g + 1) * SSD_STATE].T
            dt = _softplus(dt_ref[t * B:(t + 1) * B, :] + dtb_ref[...])
            dte = _dot(dt, hexp_ref[...], precision=HIGHEST)
            xdt_scr[t] = (xs * dte).T
            dec_scr[t] = jnp.exp(dte * aneg_ref[...]).T
        for i in range(SSD_CONV - 1):
            conv_ref[:, i * SSD_CONV_DIM:(i + 1) * SSD_CONV_DIM] = rows[T + i]

    hp0 = j * tiles
    grp = hp0 // (SSD_HEAD_DIM * (SSD_HEADS // SSD_GROUPS))
    for q in range(tiles):
        hp = pl.ds(hp0 + q, 1)
        h = h0_ref[:, q * SSD_STATE:(q + 1) * SSD_STATE].T
        for t in range(T):
            h = h * dec_scr[t, hp, :] + bm_scr[t, grp] * xdt_scr[t, hp, :]
            y_scr[t, hp, :] = jnp.sum(h * cm_scr[t, grp], axis=0, keepdims=True)
        hout_ref[:, q * SSD_STATE:(q + 1) * SSD_STATE] = h.T

    @pl.when(j == pl.num_programs(0) - 1)
    def _():
        for t in range(T):
            y = y_scr[t].T + xs_scr[t] * dsk_ref[...]
            y = y * _silu(z_ref[t * B:(t + 1) * B, :])
            y_ref[t * B:(t + 1) * B, :] = _rms(y, ng_ref[...])


def _ssd_step(z, xbc, dtr, conv0, h0, lp, *, batch, seq):
    n = batch * seq
    nstate = SSD_HEADS * SSD_HEAD_DIM * SSD_STATE
    tile_w = SSD_STEP_TILES * SSD_STATE
    consts = (lp["conv_w"], lp["conv_b"], lp["dt_bias"], lp["a_neg_exp"], lp["d_skip"], lp["ssd_norm"], lp["head_expand"])
    hspec = pl.BlockSpec((batch, tile_w), lambda j: (0, j))
    cflat = (SSD_CONV - 1) * SSD_CONV_DIM
    y, conv_new, h_new = pl.pallas_call(
        functools.partial(_ssd_step_body, seq=seq, batch=batch),
        grid=(nstate // tile_w,),
        in_specs=[_full_spec((n, GROUP_WIDTH)), _full_spec((n, SSD_CONV_DIM)), _full_spec((n, LANES)),
                  _full_spec((batch, cflat)), hspec] + [_full_spec(a.shape) for a in consts],
        out_specs=[_full_spec((n, GROUP_WIDTH)), _full_spec((batch, cflat)), hspec],
        out_shape=[jax.ShapeDtypeStruct((n, GROUP_WIDTH), F32),
                   jax.ShapeDtypeStruct((batch, cflat), F32),
                   jax.ShapeDtypeStruct((batch, nstate), F32)],
        scratch_shapes=[pltpu.VMEM((seq, batch, GROUP_WIDTH), F32),
                        pltpu.VMEM((seq, SSD_GROUPS, SSD_STATE, batch), F32),
                        pltpu.VMEM((seq, SSD_GROUPS, SSD_STATE, batch), F32),
                        pltpu.VMEM((seq, GROUP_WIDTH, batch), F32),
                        pltpu.VMEM((seq, GROUP_WIDTH, batch), F32),
                        pltpu.VMEM((seq, GROUP_WIDTH, batch), F32)],
        compiler_params=_cparams("arbitrary"),
        name="ssd_step",
    )(z, xbc, dtr, conv0.reshape(batch, cflat), h0.reshape(batch, nstate), *consts)
    return (y, conv_new.reshape(batch, SSD_CONV - 1, SSD_CONV_DIM),
            h_new.reshape(batch, SSD_HEADS, SSD_HEAD_DIM, SSD_STATE))


PAIR = 2 * RWKV_HEAD
RWKV_PAIRS = RWKV_HEADS // 2


def _bd(x):
    lane = lax.broadcasted_iota(jnp.int32, x.shape, 1)
    zero = jnp.zeros_like(x)
    return jnp.concatenate([jnp.where(lane < RWKV_HEAD, x, zero), jnp.where(lane >= RWKV_HEAD, x, zero)], axis=0)


def _half_sums(x, lo):
    s_lo = jnp.sum(jnp.where(lo, x, 0.0), axis=-1, keepdims=True)
    s_hi = jnp.sum(jnp.where(lo, 0.0, x), axis=-1, keepdims=True)
    return jnp.where(lo, s_lo, s_hi)


def _head_sum(x):
    lo = lax.broadcasted_iota(jnp.int32, (x.shape[0], PAIR), 1) < RWKV_HEAD
    return jnp.concatenate([_half_sums(x[:, p * PAIR:(p + 1) * PAIR], lo) for p in range(RWKV_PAIRS)], axis=-1)


def _rwkv_pointwise(u, prev, mu_ref, w0_ref, w2_ref, a0_ref, a2_ref, g2_ref, kk_ref, ka_ref):
    GW = GROUP_WIDTH
    xs = u + (prev - u) * mu_ref[...]
    r = xs[:, 0:GW]
    k = xs[:, GW:2 * GW]
    v = xs[:, 2 * GW:3 * GW]
    wd = xs[:, 3 * GW:3 * GW + 64]
    ad = xs[:, 3 * GW + 64:3 * GW + 128]
    gd = xs[:, 3 * GW + 128:3 * GW + 256]
    w_lin = w0_ref[...] + _dot(jnp.tanh(wd).astype(BF16), w2_ref[...])
    logdecay = -jnp.exp(-_softplus(-w_lin) - 0.5)
    a = _sigmoid(a0_ref[...] + _dot(ad.astype(BF16), a2_ref[...]))
    g = _dot(_sigmoid(gd).astype(BF16), g2_ref[...])
    kk = k * kk_ref[...]
    kk = kk / jnp.maximum(jnp.sqrt(_head_sum(kk * kk)), 1e-12)
    k = k * (1.0 + (a - 1.0) * ka_ref[...])
    return r, k, v, logdecay, a, g, kk


def _rwkv_finish(y, r, k, v, g, rk_ref, lng_ref, lnb_ref):
    mean = _head_sum(y) * (1.0 / RWKV_HEAD)
    yc = y - mean
    var = _head_sum(yc * yc) * (1.0 / RWKV_HEAD)
    y = yc * lax.rsqrt(var + RWKV_LN_EPS) * lng_ref[...] + lnb_ref[...]
    bonus = _head_sum(r * k * rk_ref[...]) * v
    return (y + bonus) * g


def _rwkv_body(u_ref, mu_ref, w0_ref, w2_ref, a0_ref, a2_ref, g2_ref, kk_ref, ka_ref, rk_ref,
               lng_ref, lnb_ref, y_ref, shift_ref, sout_ref, upad_scr, s_scr, *, chunk, group):
    L, G = chunk, group
    GL = G * L
    c = pl.program_id(1)
    pad = SUBLANES

    @pl.when(c == 0)
    def _():
        upad_scr[0:pad, :] = jnp.zeros((pad, RWKV_PROJ), F32)
        s_scr[...] = jnp.zeros(s_scr.shape, F32)

    u = u_ref[...]
    upad_scr[pad:pad + GL, :] = u
    prev = upad_scr[pad - 1:pad - 1 + GL, :]
    upad_scr[pad - 1:pad, :] = u[GL - 1:GL, :]
    r, k, v, logdecay, a, g, kk = _rwkv_pointwise(u, prev, mu_ref, w0_ref, w2_ref, a0_ref, a2_ref, g2_ref,
                                                  kk_ref, ka_ref)

    tril = jnp.where(lax.broadcasted_iota(jnp.int32, (L, L), 0) >= lax.broadcasted_iota(jnp.int32, (L, L), 1),
                     1.0, 0.0).astype(F32)
    cl = jnp.concatenate([_dot(tril, logdecay[i * L:(i + 1) * L, :], precision=HIGHEST) for i in range(G)], axis=0)
    e_in = jnp.exp(cl)
    e_inv = jnp.exp(-cl)
    r_t = r * e_in
    r_tb = r_t.astype(BF16)
    a_tb = (-kk * jnp.exp(cl - logdecay)).astype(BF16)
    b_tb = (kk * a * e_inv).astype(BF16)
    k_tb = (k * e_inv).astype(BF16)
    vb = v.astype(BF16)

    row = lax.broadcasted_iota(jnp.int32, (L, PAIR), 0)
    colh = lax.broadcasted_iota(jnp.int32, (L, PAIR), 1) & (RWKV_HEAD - 1)
    strict = row > colh
    incl = row >= colh
    eye_pair = jnp.where(row == colh, 1.0, 0.0).astype(F32)
    lane_lo = lax.broadcasted_iota(jnp.int32, (RWKV_HEAD, PAIR), 1) < RWKV_HEAD
    same_head = (lax.broadcasted_iota(jnp.int32, (PAIR, PAIR), 0) < RWKV_HEAD) == \
                (lax.broadcasted_iota(jnp.int32, (PAIR, PAIR), 1) < RWKV_HEAD)

    streams = [(i, p) for i in range(G) for p in range(RWKV_PAIRS)]
    ns = len(streams)
    blk = lambda x, i, p: x[i * L:(i + 1) * L, p * PAIR:(p + 1) * PAIR]
    lhs = [jnp.concatenate([blk(a_tb, i, p), blk(r_tb, i, p)], axis=0) for i, p in streams]
    m_ab = [_dot_nt(lhs[s], _bd(blk(b_tb, i, p))) for s, (i, p) in enumerate(streams)]
    m_ak = [_dot_nt(lhs[s], _bd(blk(k_tb, i, p))) for s, (i, p) in enumerate(streams)]
    n_ab = [jnp.where(strict, m[0:L], 0.0) for m in m_ab]
    m_rb = [jnp.where(incl, m[L:2 * L], 0.0).astype(BF16) for m in m_ab]
    n_ak = [jnp.where(strict, m[0:L], 0.0).astype(BF16) for m in m_ak]
    m_rk = [jnp.where(incl, m[L:2 * L], 0.0).astype(BF16) for m in m_ak]
    tinv = [eye_pair + n for n in n_ab]
    pwb = [n.astype(BF16) for n in n_ab]
    pw = [_dot(x, _bd(x)) for x in pwb]
    for _ in range(int(math.log2(L)) - 2):
        pwb = [x.astype(BF16) for x in pw]
        both = [_dot(jnp.concatenate([pwb[s], tinv[s].astype(BF16)], axis=0), _bd(pwb[s])) for s in range(ns)]
        pw = [x[0:L] for x in both]
        tinv = [tinv[s] + both[s][L:2 * L] for s in range(ns)]
    pwb = [x.astype(BF16) for x in pw]
    tinv = [tinv[s] + _dot(tinv[s].astype(BF16), _bd(pwb[s])) for s in range(ns)]
    tinvb = [x.astype(BF16) for x in tinv]
    nv_mv = [_dot(jnp.concatenate([n_ak[s], m_rk[s]], axis=0), _bd(blk(vb, i, p))) for s, (i, p) in enumerate(streams)]
    wu = [_dot(tinvb[s], jnp.concatenate([_bd(blk(a_tb, i, p)), _bd(nv_mv[s][0:L].astype(BF16))], axis=1))
          for s, (i, p) in enumerate(streams)]
    wub = [x.astype(BF16) for x in wu]
    qy = [_dot(m_rb[s], jnp.concatenate([_bd(wub[s][:, 0:PAIR]), _bd(wub[s][:, PAIR:2 * PAIR])], axis=1))
          for s in range(ns)]
    q = [(blk(r_t, i, p) + qy[s][:, 0:PAIR]).astype(BF16) for s, (i, p) in enumerate(streams)]
    y_loc = [qy[s][:, PAIR:2 * PAIR] + nv_mv[s][L:2 * L] for s in range(ns)]
    zeros_b = jnp.zeros((L, PAIR), BF16)
    mg = [_dot_tn(jnp.concatenate([wub[s], jnp.concatenate([zeros_b, blk(vb, i, p)], axis=1)], axis=0),
                  jnp.concatenate([blk(b_tb, i, p), blk(k_tb, i, p)], axis=0))
          for s, (i, p) in enumerate(streams)]
    p_end = [e_in[(i + 1) * L - 1:(i + 1) * L, p * PAIR:(p + 1) * PAIR] for i, p in streams]
    m_t = [(jnp.where(same_head, mg[s][0:PAIR], 0.0) * p_end[s]).astype(BF16) for s in range(ns)]
    g_t = [jnp.where(lane_lo, mg[s][PAIR:PAIR + RWKV_HEAD], mg[s][PAIR + RWKV_HEAD:2 * PAIR]) * p_end[s]
           for s in range(ns)]

    y_rows = []
    for i in range(G):
        y_pairs = []
        for p in range(RWKV_PAIRS):
            s = i * RWKV_PAIRS + p
            s0 = s_scr[p]
            s0b = s0.astype(BF16)
            y_pairs.append(_dot_nt(q[s], _bd(s0b)) + y_loc[s])
            s_scr[p] = s0 * p_end[s] + _dot(s0b, m_t[s]) + g_t[s]
        y_rows.append(jnp.concatenate(y_pairs, axis=-1))
    y = jnp.concatenate(y_rows, axis=0)
    y_ref[...] = _rwkv_finish(y, r, k, v, g, rk_ref, lng_ref, lnb_ref)

    @pl.when(c == pl.num_programs(1) - 1)
    def _():
        sout_ref[0] = s_scr[...]
        shift_ref[0] = upad_scr[pad - 1:pad, :]


_RWKV_PARAM_NAMES = ("mu", "w0", "w2", "a0", "a2", "g2", "k_k", "k_a", "r_k", "ln_g", "ln_b")


def _rwkv(u, p, *, batch, seq):
    rows = RWKV_CHUNK * RWKV_GROUP
    nc = seq // rows
    params = [p[n] for n in _RWKV_PARAM_NAMES]
    sspec = pl.BlockSpec((1, RWKV_PAIRS, RWKV_HEAD, PAIR), lambda b, c: (b, 0, 0, 0))
    y, shift, s_last = pl.pallas_call(
        functools.partial(_rwkv_body, chunk=RWKV_CHUNK, group=RWKV_GROUP),
        grid=(batch, nc),
        in_specs=[pl.BlockSpec((rows, RWKV_PROJ), lambda b, c: (b * nc + c, 0))] + [_full_spec(a.shape) for a in params],
        out_specs=[pl.BlockSpec((rows, GROUP_WIDTH), lambda b, c: (b * nc + c, 0)),
                   pl.BlockSpec((1, 1, RWKV_PROJ), lambda b, c: (b, 0, 0)), sspec],
        out_shape=[jax.ShapeDtypeStruct((batch * seq, GROUP_WIDTH), F32),
                   jax.ShapeDtypeStruct((batch, 1, RWKV_PROJ), F32),
                   jax.ShapeDtypeStruct((batch, RWKV_PAIRS, RWKV_HEAD, PAIR), F32)],
        scratch_shapes=[pltpu.VMEM((SUBLANES + rows, RWKV_PROJ), F32),
                        pltpu.VMEM((RWKV_PAIRS, RWKV_HEAD, PAIR), F32)],
        compiler_params=_cparams("parallel", "arbitrary"),
        name="rwkv",
    )(u, *params)
    s_last = s_last.reshape(batch, RWKV_PAIRS, RWKV_HEAD, 2, RWKV_HEAD).transpose(0, 1, 3, 2, 4).reshape(
        batch, RWKV_HEADS, RWKV_HEAD, RWKV_HEAD)
    return y, shift.reshape(batch, RWKV_PROJ), s_last


def _rwkv_step_body(u_ref, shift0_ref, s0_ref, mu_ref, w0_ref, w2_ref, a0_ref, a2_ref, g2_ref, kk_ref, ka_ref,
                    rk_ref, lng_ref, lnb_ref, y_ref, sout_ref, r_scr, w_scr, k_scr, b_scr, nkk_scr, v_scr, y_scr,
                    *, seq, batch):
    T, B = seq, batch
    GW = GROUP_WIDTH
    j = pl.program_id(0)
    tiles = RWKV_STEP_TILES

    def pointwise(t):
        u = u_ref[t * B:(t + 1) * B, :]
        prev = shift0_ref[...] if t == 0 else u_ref[(t - 1) * B:t * B, :]
        return _rwkv_pointwise(u, prev, mu_ref, w0_ref, w2_ref, a0_ref, a2_ref, g2_ref, kk_ref, ka_ref)

    @pl.when(j == 0)
    def _():
        for t in range(T):
            r, k, v, logdecay, a, _, kk = pointwise(t)
            r_scr[t] = r.T
            w_scr[t] = jnp.exp(logdecay).T
            k_scr[t] = k.T
            b_scr[t] = (kk * a).T
            nkk_scr[t] = (-kk).T
            v_scr[t] = v.T

    i0 = j * (2 * tiles)
    keys = pl.ds(pl.multiple_of((i0 // RWKV_HEAD) * RWKV_HEAD, RWKV_HEAD), RWKV_HEAD)
    for q in range(tiles):
        s_pair = s0_ref[:, q * PAIR:(q + 1) * PAIR].T
        halves = []
        for half in range(2):
            vi = pl.ds(i0 + 2 * q + half, 1)
            s = s_pair[half * RWKV_HEAD:(half + 1) * RWKV_HEAD]
            for t in range(T):
                sa = jnp.sum(s * nkk_scr[t, keys, :], axis=0, keepdims=True)
                s = s * w_scr[t, keys, :] + k_scr[t, keys, :] * v_scr[t, vi, :] + b_scr[t, keys, :] * sa
                y_scr[t, vi, :] = jnp.sum(s * r_scr[t, keys, :], axis=0, keepdims=True)
            halves.append(s)
        sout_ref[:, q * PAIR:(q + 1) * PAIR] = jnp.concatenate(halves, axis=0).T

    @pl.when(j == pl.num_programs(0) - 1)
    def _():
        for t in range(T):
            r, k, v, _, _, g, _ = pointwise(t)
            y_ref[t * B:(t + 1) * B, :] = _rwkv_finish(y_scr[t].T, r, k, v, g, rk_ref, lng_ref, lnb_ref)


def _rwkv_step(u, shift0, s0, p, *, batch, seq):
    n = batch * seq
    nstate = RWKV_HEADS * RWKV_HEAD * RWKV_HEAD
    tile_w = RWKV_STEP_TILES * PAIR
    params = [p[nm] for nm in _RWKV_PARAM_NAMES]
    sspec = pl.BlockSpec((batch, tile_w), lambda j: (0, j))
    tposed = pltpu.VMEM((seq, GROUP_WIDTH, batch), F32)
    y, s_new = pl.pallas_call(
        functools.partial(_rwkv_step_body, seq=seq, batch=batch),
        grid=(nstate // tile_w,),
        in_specs=[_full_spec((n, RWKV_PROJ)), _full_spec((batch, RWKV_PROJ)), sspec]
                 + [_full_spec(a.shape) for a in params],
        out_specs=[_full_spec((n, GROUP_WIDTH)), sspec],
        out_shape=[jax.ShapeDtypeStruct((n, GROUP_WIDTH), F32), jax.ShapeDtypeStruct((batch, nstate), F32)],
        scratch_shapes=[tposed] * 7,
        compiler_params=_cparams("arbitrary"),
        name="rwkv_step",
    )(u, shift0, s0.reshape(batch, nstate), *params)
    return y, s_new.reshape(batch, RWKV_HEADS, RWKV_HEAD, RWKV_HEAD)


def _s5_body(u_ref, hre0_ref, him0_ref, are_ref, aim_ref, bmat_ref, cmat_ref, d_ref, gw_ref, gb_ref,
             y_ref, hre_ref, him_ref, hs_scr, *, steps):
    c = pl.program_id(1)
    ns = S5_WIDTH
    bsub = u_ref.shape[1]

    @pl.when(c == 0)
    def _():
        hre_ref[...] = hre0_ref[...]
        him_ref[...] = him0_ref[...]

    u = u_ref[...].reshape(steps * bsub, GROUP_WIDTH)
    hs_scr[...] = _dot(u.astype(BF16), bmat_ref[...])
    are = jnp.broadcast_to(are_ref[...], (bsub, ns))
    aim = jnp.broadcast_to(aim_ref[...], (bsub, ns))

    def step(t, carry):
        hre, him = carry
        r0 = pl.multiple_of(t * bsub, bsub)
        nre = are * hre - aim * him + hs_scr[pl.ds(r0, bsub), 0:ns]
        nim = are * him + aim * hre + hs_scr[pl.ds(r0, bsub), ns:2 * ns]
        hs_scr[pl.ds(r0, bsub), 0:ns] = nre
        hs_scr[pl.ds(r0, bsub), ns:2 * ns] = nim
        return nre, nim

    hre, him = lax.fori_loop(0, steps, step, (hre_ref[...], him_ref[...]))
    hre_ref[...] = hre
    him_ref[...] = him
    y = _dot(hs_scr[...].astype(BF16), cmat_ref[...]) + u * d_ref[...]
    y = _gelu_tanh(y)
    yy = _dot(y.astype(BF16), gw_ref[...]) + gb_ref[...]
    out = yy[:, 0:GROUP_WIDTH] * _sigmoid(yy[:, GROUP_WIDTH:2 * GROUP_WIDTH])
    y_ref[...] = out.reshape(steps, bsub, GROUP_WIDTH)


def _s5(u_tm, hre0, him0, lp):
    seq, batch, _ = u_tm.shape
    steps = min(TM_CHUNK, seq)
    bsub = SUBLANES
    hspec = pl.BlockSpec((bsub, S5_WIDTH), lambda b, c: (b, 0))
    tspec = pl.BlockSpec((steps, bsub, GROUP_WIDTH), lambda b, c: (c, b, 0))
    consts = (lp["s5_are"], lp["s5_aim"], lp["s5_bmat"], lp["s5_cmat"], lp["s5_d"], lp["s5_gw"], lp["s5_gb"])
    return pl.pallas_call(
        functools.partial(_s5_body, steps=steps),
        grid=(batch // bsub, seq // steps),
        in_specs=[tspec, hspec, hspec] + [_full_spec(a.shape) for a in consts],
        out_specs=[tspec, hspec, hspec],
        out_shape=[jax.ShapeDtypeStruct((seq, batch, GROUP_WIDTH), F32),
                   jax.ShapeDtypeStruct((batch, S5_WIDTH), F32),
                   jax.ShapeDtypeStruct((batch, S5_WIDTH), F32)],
        scratch_shapes=[pltpu.VMEM((steps * bsub, 2 * S5_WIDTH), F32)],
        compiler_params=_cparams("parallel", "arbitrary"),
        name="s5",
    )(u_tm, hre0, him0, *consts)


def _pool_body(u_ref, buf0_ref, pw_ref, sc_ref, y_ref, buf_ref, f_scr, *, steps, pos0):
    c = pl.program_id(1)
    bsub = u_ref.shape[1]
    GW = GROUP_WIDTH
    halo = POOL_BUF + 1

    @pl.when(c == 0)
    def _():
        f_scr[0] = jnp.zeros((bsub, GW), F32)
        for i in range(POOL_BUF):
            f_scr[1 + i] = buf0_ref[:, i * GW:(i + 1) * GW]

    u = u_ref[...]
    f_scr[halo:halo + steps] = u
    f = f_scr[...]
    s2 = f[1:] + f[:-1]
    s4 = s2[2:] + s2[:-2]
    s8 = s4[4:] + s4[:-4]
    s16 = s8[8:] + s8[:-8]
    f_scr[0:halo] = f[steps:steps + halo]
    lane = lax.broadcasted_iota(jnp.int32, (steps, bsub, GW), 2)
    tpos = lax.broadcasted_iota(jnp.int32, (steps, bsub, GW), 0) + (pos0 + 1) + c * steps
    win = jnp.where(lane < POOL_CH, s2[halo - 1:halo - 1 + steps],
                    jnp.where(lane < 2 * POOL_CH, s4[halo - 3:halo - 3 + steps],
                              jnp.where(lane < 3 * POOL_CH, s8[halo - 7:halo - 7 + steps],
                                        s16[halo - 15:halo - 15 + steps])))
    wlen = jnp.where(lane < POOL_CH, POOL_WINDOWS[0],
                     jnp.where(lane < 2 * POOL_CH, POOL_WINDOWS[1],
                               jnp.where(lane < 3 * POOL_CH, POOL_WINDOWS[2], POOL_WINDOWS[3])))
    cnt = jnp.minimum(tpos, wlen).astype(F32)
    pooled = (win / cnt - u).reshape(steps * bsub, GW)
    y = _dot(pooled.astype(BF16), pw_ref[...]) * sc_ref[...]
    y_ref[...] = y.reshape(steps, bsub, GW)

    @pl.when(c == pl.num_programs(1) - 1)
    def _():
        for i in range(POOL_BUF):
            buf_ref[:, i * GW:(i + 1) * GW] = f_scr[1 + i]


def _pool(u_tm, buf0, lp, *, pos0):
    seq, batch, _ = u_tm.shape
    steps = min(TM_CHUNK, seq)
    bsub = SUBLANES
    flat = POOL_BUF * GROUP_WIDTH
    tspec = pl.BlockSpec((steps, bsub, GROUP_WIDTH), lambda b, c: (c, b, 0))
    bspec = pl.BlockSpec((bsub, flat), lambda b, c: (b, 0))
    y, buf = pl.pallas_call(
        functools.partial(_pool_body, steps=steps, pos0=pos0),
        grid=(batch // bsub, seq // steps),
        in_specs=[tspec, bspec, _full_spec(lp["pool_w"].shape), _full_spec(lp["pool_scale"].shape)],
        out_specs=[tspec, bspec],
        out_shape=[jax.ShapeDtypeStruct((seq, batch, GROUP_WIDTH), F32), jax.ShapeDtypeStruct((batch, flat), F32)],
        scratch_shapes=[pltpu.VMEM((POOL_BUF + 1 + steps, bsub, GROUP_WIDTH), F32)],
        compiler_params=_cparams("parallel", "arbitrary"),
        name="pool",
    )(u_tm, buf0.reshape(batch, flat), lp["pool_w"], lp["pool_scale"])
    return y, buf.reshape(batch, POOL_BUF, GROUP_WIDTH)


def _block_diag(blocks):
    g, r, c = blocks.shape
    eye = jnp.eye(g, dtype=blocks.dtype)
    return (eye[:, None, :, None] * blocks[:, :, None, :]).reshape(g * r, g * c)


def _pad_lanes(v, n=LANES):
    return jnp.pad(v, (0, n - v.shape[0])).reshape(1, n)


def _layer_params(l, P):
    row = lambda a: a.reshape(1, -1)
    w_in = P["w_in"][l]
    z_w, xbc_w, dt_w, rw_w, s5_w, pool_w = jnp.split(
        w_in, [256, 256 + 768, 256 + 768 + 4, 1028 + 1024, 1028 + 1024 + 256], axis=1)
    w_all = jnp.concatenate([z_w, xbc_w, rw_w, s5_w, pool_w, jnp.pad(dt_w, ((0, 0), (0, LANES - SSD_HEADS)))], axis=1)

    lam = lax.complex(P["s5_lam_re"][l], P["s5_lam_im"][l])
    a_bar = jnp.exp(lam * jnp.exp(P["s5_log_step"][l])[:, None])
    b_bar = ((a_bar - 1.0) / lam)[..., None] * lax.complex(P["s5_b_re"][l], P["s5_b_im"][l])
    b_t = jnp.swapaxes(b_bar, 1, 2)
    bmat = jnp.concatenate([_block_diag(jnp.real(b_t)), _block_diag(jnp.imag(b_t))], axis=1)
    c_t = jnp.swapaxes(lax.complex(P["s5_c_re"][l], P["s5_c_im"][l]), 1, 2)
    cmat = jnp.concatenate([_block_diag(jnp.real(c_t)), -_block_diag(jnp.imag(c_t))], axis=0)
    head_expand = jnp.pad(jnp.repeat(jnp.eye(SSD_HEADS, dtype=F32), SSD_HEAD_DIM, axis=1),
                          ((0, LANES - SSD_HEADS), (0, 0)))

    return dict(
        norm_ffn1=row(P["norm_ffn1"][l]), ffn1_in=P["ffn1_in"][l].astype(BF16), ffn1_out=P["ffn1_out"][l].astype(BF16),
        norm_mix=row(P["norm_mix"][l]), w_all=w_all.astype(BF16),
        conv_w=P["ssd_conv_w"][l], conv_b=row(P["ssd_conv_b"][l]),
        dt_bias=_pad_lanes(P["ssd_dt_bias"][l]), a_log=_pad_lanes(P["ssd_a_log"][l]),
        a_neg_exp=row(jnp.repeat(-jnp.exp(P["ssd_a_log"][l]), SSD_HEAD_DIM)), head_expand=head_expand,
        d_skip=row(jnp.repeat(P["ssd_d"][l], SSD_HEAD_DIM)), ssd_norm=row(P["ssd_norm"][l]),
        rwkv=dict(mu=row(P["rwkv_mu"][l]), w0=row(P["rwkv_w0"][l]), w2=P["rwkv_w2"][l].astype(BF16),
                  a0=row(P["rwkv_a0"][l]), a2=P["rwkv_a2"][l].astype(BF16), g2=P["rwkv_g2"][l].astype(BF16),
                  k_k=row(P["rwkv_k_k"][l]), k_a=row(P["rwkv_k_a"][l]), r_k=row(P["rwkv_r_k"][l]),
                  ln_g=row(P["rwkv_ln_g"][l]), ln_b=row(P["rwkv_ln_b"][l])),
        s5_are=row(jnp.real(a_bar)), s5_aim=row(jnp.imag(a_bar)), s5_bmat=bmat.astype(BF16), s5_cmat=cmat.astype(BF16),
        s5_d=row(P["s5_d"][l]), s5_gw=P["s5_glu_w"][l].astype(BF16), s5_gb=row(P["s5_glu_b"][l]),
        pool_w=_block_diag(P["pool_w"][l]).astype(BF16), pool_scale=row(P["pool_scale"][l]),
        w_out=P["w_out"][l].astype(BF16),
        norm_ffn2=row(P["norm_ffn2"][l]), ffn2_in=P["ffn2_in"][l].astype(BF16), ffn2_out=P["ffn2_out"][l].astype(BF16),
    )


def _to_tm(rows, batch, seq):
    return jnp.swapaxes(rows.reshape(batch, seq, rows.shape[-1]), 0, 1)


def _from_tm(x_tm):
    seq, batch, w = x_tm.shape
    return jnp.swapaxes(x_tm, 0, 1).reshape(batch * seq, w)


def _mixers_prompt(lp, proj, *, batch, seq):
    z, xbc, ur, us5, upool, dtr = proj
    y_ssd, conv_new, ssd_new = _ssd(z, xbc, dtr, lp, batch=batch, seq=seq)
    y_rwkv, shift_new, rwkv_new = _rwkv(ur, lp["rwkv"], batch=batch, seq=seq)
    zeros = jnp.zeros((batch, S5_WIDTH), F32)
    y_s5, s5re, s5im = _s5(_to_tm(us5, batch, seq), zeros, zeros, lp)
    y_pool, pool_new = _pool(_to_tm(upool, batch, seq), jnp.zeros((batch, POOL_BUF, GROUP_WIDTH), F32), lp, pos0=0)
    ys = (y_ssd, y_rwkv, _from_tm(y_s5), _from_tm(y_pool))
    states = (conv_new, ssd_new, shift_new, rwkv_new, s5re.reshape(batch, S5_GROUPS, S5_STATE),
              s5im.reshape(batch, S5_GROUPS, S5_STATE), pool_new)
    return ys, states


def _mixers_decode(lp, proj, states, *, batch, seq):
    z, xbc, ur, us5, upool, dtr = proj
    conv0, ssd0, shift0, rwkv0, s5re0, s5im0, pool0 = states
    y_ssd, conv_new, ssd_new = _ssd_step(z, xbc, dtr, conv0, ssd0, lp, batch=batch, seq=seq)
    y_rwkv, rwkv_new = _rwkv_step(ur, shift0, rwkv0, lp["rwkv"], batch=batch, seq=seq)
    shift_new = ur[(seq - 1) * batch:, :]
    tm = lambda a: a.reshape(seq, batch, a.shape[-1])
    y_s5, s5re, s5im = _s5(tm(us5), s5re0.reshape(batch, S5_WIDTH), s5im0.reshape(batch, S5_WIDTH), lp)
    y_pool, pool_new = _pool(tm(upool), pool0, lp, pos0=PAST_LEN)
    rows = lambda a: a.reshape(seq * batch, a.shape[-1])
    ys = (y_ssd, y_rwkv, rows(y_s5), rows(y_pool))
    new_states = (conv_new, ssd_new, shift_new, rwkv_new, s5re.reshape(batch, S5_GROUPS, S5_STATE),
                  s5im.reshape(batch, S5_GROUPS, S5_STATE), pool_new)
    return ys, new_states


_WIDTHS = (GROUP_WIDTH, SSD_CONV_DIM, RWKV_PROJ, GROUP_WIDTH, GROUP_WIDTH, LANES)


def _trunk(x, layer_params, norm_final, mixers):
    states = []
    mix, lp = None, None
    for l, lp_next in enumerate(layer_params):
        if l > 0:
            x = _ffn(x, lp["norm_ffn2"], lp["ffn2_in"], lp["ffn2_out"], mix=mix, wmix=lp["w_out"])
        lp = lp_next
        x = _ffn(x, lp["norm_ffn1"], lp["ffn1_in"], lp["ffn1_out"])
        proj = _inproj(x, lp["norm_mix"], lp["w_all"], _WIDTHS)
        mix, st = mixers(l, lp, proj)
        states.append(st)
    x = _ffn(x, lp["norm_ffn2"], lp["ffn2_in"], lp["ffn2_out"], mix=mix, wmix=lp["w_out"], gf=norm_final)
    return x, [jnp.stack([st[i] for st in states]) for i in range(7)]


def kernel(x_prompt, x_sample, state_ssd_conv, state_ssd, state_rwkv_shift, state_rwkv, state_s5_re, state_s5_im, state_pool, norm_ffn1, ffn1_in, ffn1_out, norm_mix, w_in, ssd_conv_w, ssd_conv_b, ssd_dt_bias, ssd_a_log, ssd_d, ssd_norm, rwkv_mu, rwkv_w0, rwkv_w2, rwkv_a0, rwkv_a2, rwkv_g2, rwkv_k_k, rwkv_k_a, rwkv_r_k, rwkv_ln_g, rwkv_ln_b, s5_lam_re, s5_lam_im, s5_log_step, s5_b_re, s5_b_im, s5_c_re, s5_c_im, s5_d, s5_glu_w, s5_glu_b, pool_w, pool_scale, w_out, norm_ffn2, ffn2_in, ffn2_out, norm_final):
    P = dict(norm_ffn1=norm_ffn1, ffn1_in=ffn1_in, ffn1_out=ffn1_out, norm_mix=norm_mix, w_in=w_in,
             ssd_conv_w=ssd_conv_w, ssd_conv_b=ssd_conv_b, ssd_dt_bias=ssd_dt_bias, ssd_a_log=ssd_a_log,
             ssd_d=ssd_d, ssd_norm=ssd_norm, rwkv_mu=rwkv_mu, rwkv_w0=rwkv_w0, rwkv_w2=rwkv_w2, rwkv_a0=rwkv_a0,
             rwkv_a2=rwkv_a2, rwkv_g2=rwkv_g2, rwkv_k_k=rwkv_k_k, rwkv_k_a=rwkv_k_a,
             rwkv_r_k=rwkv_r_k.reshape(rwkv_r_k.shape[0], -1), rwkv_ln_g=rwkv_ln_g, rwkv_ln_b=rwkv_ln_b,
             s5_lam_re=s5_lam_re, s5_lam_im=s5_lam_im, s5_log_step=s5_log_step, s5_b_re=s5_b_re, s5_b_im=s5_b_im,
             s5_c_re=s5_c_re, s5_c_im=s5_c_im, s5_d=s5_d, s5_glu_w=s5_glu_w, s5_glu_b=s5_glu_b, pool_w=pool_w,
             pool_scale=pool_scale, w_out=w_out, norm_ffn2=norm_ffn2, ffn2_in=ffn2_in, ffn2_out=ffn2_out)
    depth = norm_ffn1.shape[0]
    bp, tp, d = x_prompt.shape
    bs, ts, _ = x_sample.shape
    layer_params = [_layer_params(l, P) for l in range(depth)]
    gf = norm_final.reshape(1, -1)
    sample_states = (state_ssd_conv, state_ssd, state_rwkv_shift, state_rwkv, state_s5_re, state_s5_im, state_pool)

    y_p, st_p = _trunk(x_prompt.reshape(bp * tp, d), layer_params, gf,
                       lambda l, lp, proj: _mixers_prompt(lp, proj, batch=bp, seq=tp))
    x_s = jnp.swapaxes(x_sample, 0, 1).reshape(ts * bs, d)
    y_s, st_s = _trunk(x_s, layer_params, gf,
                       lambda l, lp, proj: _mixers_decode(lp, proj, tuple(s[l] for s in sample_states),
                                                          batch=bs, seq=ts))
    outs = [y_p.reshape(bp, tp, d), jnp.swapaxes(y_s.reshape(ts, bs, d), 0, 1)]
    for a, b in zip(st_p, st_s):
        outs += [a, b]
    return tuple(outs)
```

```python
import functools
import math

import jax
import jax.numpy as jnp
from jax import lax
from jax.experimental import pallas as pl
from jax.experimental.pallas import tpu as pltpu

F32 = jnp.float32
BF16 = jnp.bfloat16
HIGHEST = lax.Precision.HIGHEST

SUBLANES = 8
LANES = 128
VMEM_LIMIT_BYTES = 56 * 1024 * 1024

GROUP_WIDTH = 256
SSD_HEAD_DIM = 64
SSD_HEADS = 4
SSD_GROUPS = 2
SSD_STATE = 128
SSD_CONV = 4
SSD_CONV_DIM = GROUP_WIDTH + 2 * SSD_GROUPS * SSD_STATE
SSD_CHUNK = 128
RWKV_HEAD = 64
RWKV_HEADS = 4
RWKV_PROJ = 1024
RWKV_LN_EPS = 64e-5
RWKV_CHUNK = 64
RWKV_GROUP = 8
S5_GROUP_CH = 16
S5_GROUPS = 16
S5_STATE = 64
S5_WIDTH = S5_GROUPS * S5_STATE
POOL_WINDOWS = (2, 4, 8, 16)
POOL_CH = 64
POOL_BUF = 15
RMS_EPS = 1e-6
PAST_LEN = 16384

ROW_TILE = 512
FFN_CHUNK = 256
TM_CHUNK = 64
SSD_STEP_TILES = 16
RWKV_STEP_TILES = 8


def _cparams(*sem):
    return pltpu.CompilerParams(dimension_semantics=sem, vmem_limit_bytes=VMEM_LIMIT_BYTES)


def _dot(a, b, **kw):
    return jnp.dot(a, b, preferred_element_type=F32, **kw)


def _dot_nt(a, b):
    return lax.dot_general(a, b, (((1,), (1,)), ((), ())), preferred_element_type=F32)


def _dot_tn(a, b):
    return lax.dot_general(a, b, (((0,), (0,)), ((), ())), preferred_element_type=F32)


def _sigmoid(x):
    return 1.0 / (1.0 + jnp.exp(-x))


def _silu(x):
    return x * _sigmoid(x)


def _softplus(x):
    return jnp.maximum(x, 0.0) + jnp.log(1.0 + jnp.exp(-jnp.abs(x)))


def _gelu_tanh(x):
    c = math.sqrt(2.0 / math.pi)
    return x * (0.5 * (1.0 + jnp.tanh(c * (x + 0.044715 * (x * x * x)))))


def _rms(x, g):
    return x * lax.rsqrt(jnp.mean(x * x, axis=-1, keepdims=True) + RMS_EPS) * g


def _full_spec(shape):
    n = len(shape)
    return pl.BlockSpec(shape, lambda *_: (0,) * n)


def _ffn_body(*refs, has_mix, final_norm):
    it = iter(refs)
    x_ref = next(it)
    x = x_ref[...]
    if has_mix:
        y_refs = [next(it) for _ in range(4)]
        wmix_ref = next(it)
        for i, y_ref in enumerate(y_refs):
            x = x + _dot(y_ref[...].astype(BF16), wmix_ref[i * GROUP_WIDTH:(i + 1) * GROUP_WIDTH, :])
    g_ref, wi_ref, wo_ref = next(it), next(it), next(it)
    gf_ref = next(it) if final_norm else None
    o_ref = next(it)
    h = _rms(x, g_ref[...]).astype(BF16)
    d_ff = wo_ref.shape[0]
    acc = jnp.zeros_like(x)
    for c in range(d_ff // FFN_CHUNK):
        lo = c * FFN_CHUNK
        gate = _dot(h, wi_ref[:, lo:lo + FFN_CHUNK])
        up = _dot(h, wi_ref[:, d_ff + lo:d_ff + lo + FFN_CHUNK])
        act = (_silu(gate) * up).astype(BF16)
        acc = acc + _dot(act, wo_ref[lo:lo + FFN_CHUNK, :])
    x = x + 0.5 * acc
    if final_norm:
        x = _rms(x, gf_ref[...])
    o_ref[...] = x


def _ffn(x, g, wi, wo, mix=None, wmix=None, gf=None):
    rows, d = x.shape
    row_spec = lambda w: pl.BlockSpec((ROW_TILE, w), lambda i: (i, 0))
    args, specs = [x], [row_spec(d)]
    if mix is not None:
        for y in mix:
            args.append(y)
            specs.append(row_spec(y.shape[1]))
        args.append(wmix)
        specs.append(_full_spec(wmix.shape))
    for a in (g, wi, wo) + ((gf,) if gf is not None else ()):
        args.append(a)
        specs.append(_full_spec(a.shape))
    return pl.pallas_call(
        functools.partial(_ffn_body, has_mix=mix is not None, final_norm=gf is not None),
        grid=(rows // ROW_TILE,),
        in_specs=specs,
        out_specs=row_spec(d),
        out_shape=jax.ShapeDtypeStruct((rows, d), F32),
        compiler_params=_cparams("parallel"),
        name="ffn",
    )(*args)


def _inproj_body(x_ref, g_ref, w_ref, *o_refs):
    h = _rms(x_ref[...], g_ref[...]).astype(BF16)
    off = 0
    for o_ref in o_refs:
        n = o_ref.shape[-1]
        o_ref[...] = _dot(h, w_ref[:, off:off + n])
        off += n


def _inproj(x, g, w, widths):
    rows, d = x.shape
    row_spec = lambda w_: pl.BlockSpec((ROW_TILE, w_), lambda i: (i, 0))
    return pl.pallas_call(
        _inproj_body,
        grid=(rows // ROW_TILE,),
        in_specs=[row_spec(d), _full_spec(g.shape), _full_spec(w.shape)],
        out_specs=[row_spec(n) for n in widths],
        out_shape=[jax.ShapeDtypeStruct((rows, n), F32) for n in widths],
        compiler_params=_cparams("parallel"),
        name="inproj",
    )(x, g, w)


def _ssd_body(z_ref, xbc_ref, dt_ref, cw_ref, cb_ref, dtb_ref, alog_ref, dsk_ref, ng_ref,
              y_ref, conv_ref, hout_ref, xpad_scr, h_scr, *, chunk):
    L = chunk
    c = pl.program_id(1)
    pad = SUBLANES
    halo = SSD_CONV - 1

    @pl.when(c == 0)
    def _():
        xpad_scr[0:pad, :] = jnp.zeros((pad, SSD_CONV_DIM), F32)
        h_scr[...] = jnp.zeros(h_scr.shape, F32)

    xpad_scr[pad:pad + L, :] = xbc_ref[...]
    conv = cb_ref[...] + cw_ref[0:1, :] * xpad_scr[pad - halo:pad - halo + L, :]
    for j in range(1, SSD_CONV):
        conv = conv + cw_ref[j:j + 1, :] * xpad_scr[pad - halo + j:pad - halo + j + L, :]
    xpad_scr[pad - halo:pad, :] = xpad_scr[pad + L - halo:pad + L, :]
    conv = _silu(conv)
    xs = conv[:, 0:GROUP_WIDTH]
    bm = conv[:, GROUP_WIDTH:2 * GROUP_WIDTH].astype(BF16)
    cm = conv[:, 2 * GROUP_WIDTH:3 * GROUP_WIDTH].astype(BF16)

    row = lax.broadcasted_iota(jnp.int32, (L, L), 0)
    col = lax.broadcasted_iota(jnp.int32, (L, L), 1)
    causal = row >= col
    dt = _softplus(dt_ref[...] + dtb_ref[...])
    da = dt * (-jnp.exp(alog_ref[...]))
    acs = _dot(jnp.where(causal, 1.0, 0.0).astype(F32), da, precision=HIGHEST)
    acs_t = acs.T
    e_acs = jnp.exp(acs)
    acs_last = acs[L - 1:L, :]
    e_end = jnp.exp(acs_last - acs)
    e_last = jnp.exp(acs_last)

    ys = []
    for h in range(SSD_HEADS):
        g = h // (SSD_HEADS // SSD_GROUPS)
        bg = bm[:, g * SSD_STATE:(g + 1) * SSD_STATE]
        cg = cm[:, g * SSD_STATE:(g + 1) * SSD_STATE]
        x_h = xs[:, h * SSD_HEAD_DIM:(h + 1) * SSD_HEAD_DIM]
        xdt = x_h * dt[:, h:h + 1]
        seg = acs[:, h:h + 1] - acs_t[h:h + 1, :]
        decay = jnp.exp(jnp.where(causal, seg, -jnp.inf))
        scores = _dot_nt(cg, bg) * decay
        y_h = _dot(scores.astype(BF16), xdt.astype(BF16))
        h_prev = h_scr[h]
        y_h = y_h + _dot_nt(cg, h_prev.astype(BF16)) * e_acs[:, h:h + 1]
        st = _dot_tn((xdt * e_end[:, h:h + 1]).astype(BF16), bg)
        h_scr[h] = h_prev * e_last[:, h:h + 1] + st
        ys.append(y_h)
    y = jnp.concatenate(ys, axis=-1) + xs * dsk_ref[...]
    y = y * _silu(z_ref[...])
    y_ref[...] = _rms(y, ng_ref[...])

    @pl.when(c == pl.num_programs(1) - 1)
    def _():
        hout_ref[0] = h_scr[...]
        conv_ref[0] = xpad_scr[pad - halo:pad, :]


def _ssd(z, xbc, dtr, lp, *, batch, seq):
    chunk = SSD_CHUNK
    nc = seq // chunk
    rspec = lambda w: pl.BlockSpec((chunk, w), lambda b, c: (b * nc + c, 0))
    consts = (lp["conv_w"], lp["conv_b"], lp["dt_bias"], lp["a_log"], lp["d_skip"], lp["ssd_norm"])
    return pl.pallas_call(
        functools.partial(_ssd_body, chunk=chunk),
        grid=(batch, nc),
        in_specs=[rspec(GROUP_WIDTH), rspec(SSD_CONV_DIM), rspec(LANES)] + [_full_spec(a.shape) for a in consts],
        out_specs=[rspec(GROUP_WIDTH),
                   pl.BlockSpec((1, SSD_CONV - 1, SSD_CONV_DIM), lambda b, c: (b, 0, 0)),
                   pl.BlockSpec((1, SSD_HEADS, SSD_HEAD_DIM, SSD_STATE), lambda b, c: (b, 0, 0, 0))],
        out_shape=[jax.ShapeDtypeStruct((batch * seq, GROUP_WIDTH), F32),
                   jax.ShapeDtypeStruct((batch, SSD_CONV - 1, SSD_CONV_DIM), F32),
                   jax.ShapeDtypeStruct((batch, SSD_HEADS, SSD_HEAD_DIM, SSD_STATE), F32)],
        scratch_shapes=[pltpu.VMEM((SUBLANES + chunk, SSD_CONV_DIM), F32),
                        pltpu.VMEM((SSD_HEADS, SSD_HEAD_DIM, SSD_STATE), F32)],
        compiler_params=_cparams("parallel", "arbitrary"),
        name="ssd",
    )(z, xbc, dtr, *consts)


def _ssd_step_body(z_ref, xbc_ref, dt_ref, conv0_ref, h0_ref, *rest, seq, batch, layer):
    hdone_ref, rest = (rest[0], rest[1:]) if layer else (None, rest)
    (cw_ref, cb_ref, dtb_ref, aneg_ref, dsk_ref, ng_ref, hexp_ref, y_ref, conv_ref, hout_ref,
     xs_scr, bm_scr, cm_scr, xdt_scr, dec_scr, y_scr) = rest
    T, B = seq, batch
    if layer:
        hout_ref[0:layer] = hdone_ref[...]
    GW = GROUP_WIDTH
    j = pl.program_id(0)
    tiles = SSD_STEP_TILES

    @pl.when(j == 0)
    def _():
        rows = [conv0_ref[:, i * SSD_CONV_DIM:(i + 1) * SSD_CONV_DIM] for i in range(SSD_CONV - 1)]
        rows += [xbc_ref[t * B:(t + 1) * B, :] for t in range(T)]
        for t in range(T):
            conv = cb_ref[...] + cw_ref[0:1, :] * rows[t]
            for i in range(1, SSD_CONV):
                conv = conv + cw_ref[i:i + 1, :] * rows[t + i]
            conv = _silu(conv)
            xs = conv[:, 0:GW]
            xs_scr[t] = xs
            for g in range(SSD_GROUPS):
                bm_scr[t, g] = conv[:, GW + g * SSD_STATE:GW + (g + 1) * SSD_STATE].T
                cm_scr[t, g] = conv[:, 2 * GW + g * SSD_STATE:2 * GW + (g + 1) * SSD_STATE].T
            dt = _softplus(dt_ref[t * B:(t + 1) * B, :] + dtb_ref[...])
            dte = _dot(dt, hexp_ref[...], precision=HIGHEST)
            xdt_scr[t] = (xs * dte).T
            dec_scr[t] = jnp.exp(dte * aneg_ref[...]).T
        for i in range(SSD_CONV - 1):
            conv_ref[:, i * SSD_CONV_DIM:(i + 1) * SSD_CONV_DIM] = rows[T + i]

    hp0 = j * tiles
    grp = hp0 // (SSD_HEAD_DIM * (SSD_HEADS // SSD_GROUPS))
    for q in range(tiles):
        hp = pl.ds(hp0 + q, 1)
        h = h0_ref[:, q * SSD_STATE:(q + 1) * SSD_STATE].T
        for t in range(T):
            h = h * dec_scr[t, hp, :] + bm_scr[t, grp] * xdt_scr[t, hp, :]
            y_scr[t, hp, :] = jnp.sum(h * cm_scr[t, grp], axis=0, keepdims=True)
        hout_ref[layer, :, q * SSD_STATE:(q + 1) * SSD_STATE] = h.T

    @pl.when(j == pl.num_programs(0) - 1)
    def _():
        for t in range(T):
            y = y_scr[t].T + xs_scr[t] * dsk_ref[...]
            y = y * _silu(z_ref[t * B:(t + 1) * B, :])
            y_ref[t * B:(t + 1) * B, :] = _rms(y, ng_ref[...])


def _layer_state_specs(layer, batch, tile_w):
    cur = pl.BlockSpec((None, batch, tile_w), lambda j: (layer, 0, j))
    prev = [pl.BlockSpec((layer, batch, tile_w), lambda j: (0, 0, j))] if layer else []
    out = pl.BlockSpec((layer + 1, batch, tile_w), lambda j: (0, 0, j))
    return cur, prev, out


def _ssd_step(z, xbc, dtr, conv0, h_all, h_done, lp, *, batch, seq, layer):
    n = batch * seq
    nstate = SSD_HEADS * SSD_HEAD_DIM * SSD_STATE
    tile_w = SSD_STEP_TILES * SSD_STATE
    consts = (lp["conv_w"], lp["conv_b"], lp["dt_bias"], lp["a_neg_exp"], lp["d_skip"], lp["ssd_norm"], lp["head_expand"])
    hspec, prev_specs, hout_spec = _layer_state_specs(layer, batch, tile_w)
    prev_args = [h_done] if layer else []
    cflat = (SSD_CONV - 1) * SSD_CONV_DIM
    y, conv_new, h_new = pl.pallas_call(
        functools.partial(_ssd_step_body, seq=seq, batch=batch, layer=layer),
        grid=(nstate // tile_w,),
        in_specs=[_full_spec((n, GROUP_WIDTH)), _full_spec((n, SSD_CONV_DIM)), _full_spec((n, LANES)),
                  _full_spec((batch, cflat)), hspec] + prev_specs + [_full_spec(a.shape) for a in consts],
        out_specs=[_full_spec((n, GROUP_WIDTH)), _full_spec((batch, cflat)), hout_spec],
        out_shape=[jax.ShapeDtypeStruct((n, GROUP_WIDTH), F32),
                   jax.ShapeDtypeStruct((batch, cflat), F32),
                   jax.ShapeDtypeStruct((layer + 1, batch, nstate), F32)],
        scratch_shapes=[pltpu.VMEM((seq, batch, GROUP_WIDTH), F32),
                        pltpu.VMEM((seq, SSD_GROUPS, SSD_STATE, batch), F32),
                        pltpu.VMEM((seq, SSD_GROUPS, SSD_STATE, batch), F32),
                        pltpu.VMEM((seq, GROUP_WIDTH, batch), F32),
                        pltpu.VMEM((seq, GROUP_WIDTH, batch), F32),
                        pltpu.VMEM((seq, GROUP_WIDTH, batch), F32)],
        compiler_params=_cparams("arbitrary"),
        name="ssd_step",
    )(z, xbc, dtr, conv0.reshape(batch, cflat), h_all.reshape(h_all.shape[0], batch, nstate), *prev_args, *consts)
    return y, conv_new.reshape(batch, SSD_CONV - 1, SSD_CONV_DIM), h_new


PAIR = 2 * RWKV_HEAD
RWKV_PAIRS = RWKV_HEADS // 2


def _bd(x):
    lane = lax.broadcasted_iota(jnp.int32, x.shape, 1)
    zero = jnp.zeros_like(x)
    return jnp.concatenate([jnp.where(lane < RWKV_HEAD, x, zero), jnp.where(lane >= RWKV_HEAD, x, zero)], axis=0)


def _half_sums(x, lo):
    s_lo = jnp.sum(jnp.where(lo, x, 0.0), axis=-1, keepdims=True)
    s_hi = jnp.sum(jnp.where(lo, 0.0, x), axis=-1, keepdims=True)
    return jnp.where(lo, s_lo, s_hi)


def _head_sum(x):
    lo = lax.broadcasted_iota(jnp.int32, (x.shape[0], PAIR), 1) < RWKV_HEAD
    return jnp.concatenate([_half_sums(x[:, p * PAIR:(p + 1) * PAIR], lo) for p in range(RWKV_PAIRS)], axis=-1)


def _rwkv_pointwise(u, prev, mu_ref, w0_ref, w2_ref, a0_ref, a2_ref, g2_ref, kk_ref, ka_ref):
    GW = GROUP_WIDTH
    xs = u + (prev - u) * mu_ref[...]
    r = xs[:, 0:GW]
    k = xs[:, GW:2 * GW]
    v = xs[:, 2 * GW:3 * GW]
    wd = xs[:, 3 * GW:3 * GW + 64]
    ad = xs[:, 3 * GW + 64:3 * GW + 128]
    gd = xs[:, 3 * GW + 128:3 * GW + 256]
    w_lin = w0_ref[...] + _dot(jnp.tanh(wd).astype(BF16), w2_ref[...])
    logdecay = -jnp.exp(-_softplus(-w_lin) - 0.5)
    a = _sigmoid(a0_ref[...] + _dot(ad.astype(BF16), a2_ref[...]))
    g = _dot(_sigmoid(gd).astype(BF16), g2_ref[...])
    kk = k * kk_ref[...]
    kk = kk / jnp.maximum(jnp.sqrt(_head_sum(kk * kk)), 1e-12)
    k = k * (1.0 + (a - 1.0) * ka_ref[...])
    return r, k, v, logdecay, a, g, kk


def _rwkv_finish(y, r, k, v, g, rk_ref, lng_ref, lnb_ref):
    mean = _head_sum(y) * (1.0 / RWKV_HEAD)
    yc = y - mean
    var = _head_sum(yc * yc) * (1.0 / RWKV_HEAD)
    y = yc * lax.rsqrt(var + RWKV_LN_EPS) * lng_ref[...] + lnb_ref[...]
    bonus = _head_sum(r * k * rk_ref[...]) * v
    return (y + bonus) * g


def _rwkv_body(u_ref, mu_ref, w0_ref, w2_ref, a0_ref, a2_ref, g2_ref, kk_ref, ka_ref, rk_ref,
               lng_ref, lnb_ref, y_ref, shift_ref, sout_ref, upad_scr, s_scr, *, chunk, group):
    L, G = chunk, group
    GL = G * L
    c = pl.program_id(1)
    pad = SUBLANES

    @pl.when(c == 0)
    def _():
        upad_scr[0:pad, :] = jnp.zeros((pad, RWKV_PROJ), F32)
        s_scr[...] = jnp.zeros(s_scr.shape, F32)

    u = u_ref[...]
    upad_scr[pad:pad + GL, :] = u
    prev = upad_scr[pad - 1:pad - 1 + GL, :]
    upad_scr[pad - 1:pad, :] = u[GL - 1:GL, :]
    r, k, v, logdecay, a, g, kk = _rwkv_pointwise(u, prev, mu_ref, w0_ref, w2_ref, a0_ref, a2_ref, g2_ref,
                                                  kk_ref, ka_ref)

    tril = jnp.where(lax.broadcasted_iota(jnp.int32, (L, L), 0) >= lax.broadcasted_iota(jnp.int32, (L, L), 1),
                     1.0, 0.0).astype(F32)
    cl = jnp.concatenate([_dot(tril, logdecay[i * L:(i + 1) * L, :], precision=HIGHEST) for i in range(G)], axis=0)
    e_in = jnp.exp(cl)
    e_inv = jnp.exp(-cl)
    r_t = r * e_in
    r_tb = r_t.astype(BF16)
    a_tb = (-kk * jnp.exp(cl - logdecay)).astype(BF16)
    b_tb = (kk * a * e_inv).astype(BF16)
    k_tb = (k * e_inv).astype(BF16)
    vb = v.astype(BF16)

    row = lax.broadcasted_iota(jnp.int32, (L, PAIR), 0)
    colh = lax.broadcasted_iota(jnp.int32, (L, PAIR), 1) & (RWKV_HEAD - 1)
    strict = row > colh
    incl = row >= colh
    eye_pair = jnp.where(row == colh, 1.0, 0.0).astype(F32)
    lane_lo = lax.broadcasted_iota(jnp.int32, (RWKV_HEAD, PAIR), 1) < RWKV_HEAD
    same_head = (lax.broadcasted_iota(jnp.int32, (PAIR, PAIR), 0) < RWKV_HEAD) == \
                (lax.broadcasted_iota(jnp.int32, (PAIR, PAIR), 1) < RWKV_HEAD)

    streams = [(i, p) for i in range(G) for p in range(RWKV_PAIRS)]
    ns = len(streams)
    blk = lambda x, i, p: x[i * L:(i + 1) * L, p * PAIR:(p + 1) * PAIR]
    lhs = [jnp.concatenate([blk(a_tb, i, p), blk(r_tb, i, p)], axis=0) for i, p in streams]
    m_ab = [_dot_nt(lhs[s], _bd(blk(b_tb, i, p))) for s, (i, p) in enumerate(streams)]
    m_ak = [_dot_nt(lhs[s], _bd(blk(k_tb, i, p))) for s, (i, p) in enumerate(streams)]
    n_ab = [jnp.where(strict, m[0:L], 0.0) for m in m_ab]
    m_rb = [jnp.where(incl, m[L:2 * L], 0.0).astype(BF16) for m in m_ab]
    n_ak = [jnp.where(strict, m[0:L], 0.0).astype(BF16) for m in m_ak]
    m_rk = [jnp.where(incl, m[L:2 * L], 0.0).astype(BF16) for m in m_ak]
    tinv = [eye_pair + n for n in n_ab]
    pwb = [n.astype(BF16) for n in n_ab]
    pw = [_dot(x, _bd(x)) for x in pwb]
    for _ in range(int(math.log2(L)) - 2):
        pwb = [x.astype(BF16) for x in pw]
        both = [_dot(jnp.concatenate([pwb[s], tinv[s].astype(BF16)], axis=0), _bd(pwb[s])) for s in range(ns)]
        pw = [x[0:L] for x in both]
        tinv = [tinv[s] + both[s][L:2 * L] for s in range(ns)]
    pwb = [x.astype(BF16) for x in pw]
    tinv = [tinv[s] + _dot(tinv[s].astype(BF16), _bd(pwb[s])) for s in range(ns)]
    tinvb = [x.astype(BF16) for x in tinv]
    nv_mv = [_dot(jnp.concatenate([n_ak[s], m_rk[s]], axis=0), _bd(blk(vb, i, p))) for s, (i, p) in enumerate(streams)]
    wu = [_dot(tinvb[s], jnp.concatenate([_bd(blk(a_tb, i, p)), _bd(nv_mv[s][0:L].astype(BF16))], axis=1))
          for s, (i, p) in enumerate(streams)]
    wub = [x.astype(BF16) for x in wu]
    qy = [_dot(m_rb[s], jnp.concatenate([_bd(wub[s][:, 0:PAIR]), _bd(wub[s][:, PAIR:2 * PAIR])], axis=1))
          for s in range(ns)]
    q = [(blk(r_t, i, p) + qy[s][:, 0:PAIR]).astype(BF16) for s, (i, p) in enumerate(streams)]
    y_loc = [qy[s][:, PAIR:2 * PAIR] + nv_mv[s][L:2 * L] for s in range(ns)]
    zeros_b = jnp.zeros((L, PAIR), BF16)
    mg = [_dot_tn(jnp.concatenate([wub[s], jnp.concatenate([zeros_b, blk(vb, i, p)], axis=1)], axis=0),
                  jnp.concatenate([blk(b_tb, i, p), blk(k_tb, i, p)], axis=0))
          for s, (i, p) in enumerate(streams)]
    p_end = [e_in[(i + 1) * L - 1:(i + 1) * L, p * PAIR:(p + 1) * PAIR] for i, p in streams]
    m_t = [(jnp.where(same_head, mg[s][0:PAIR], 0.0) * p_end[s]).astype(BF16) for s in range(ns)]
    g_t = [jnp.where(lane_lo, mg[s][PAIR:PAIR + RWKV_HEAD], mg[s][PAIR + RWKV_HEAD:2 * PAIR]) * p_end[s]
           for s in range(ns)]

    y_rows = []
    for i in range(G):
        y_pairs = []
        for p in range(RWKV_PAIRS):
            s = i * RWKV_PAIRS + p
            s0 = s_scr[p]
            s0b = s0.astype(BF16)
            y_pairs.append(_dot_nt(q[s], _bd(s0b)) + y_loc[s])
            s_scr[p] = s0 * p_end[s] + _dot(s0b, m_t[s]) + g_t[s]
        y_rows.append(jnp.concatenate(y_pairs, axis=-1))
    y = jnp.concatenate(y_rows, axis=0)
    y_ref[...] = _rwkv_finish(y, r, k, v, g, rk_ref, lng_ref, lnb_ref)

    @pl.when(c == pl.num_programs(1) - 1)
    def _():
        sout_ref[0] = s_scr[...]
        shift_ref[0] = upad_scr[pad - 1:pad, :]


_RWKV_PARAM_NAMES = ("mu", "w0", "w2", "a0", "a2", "g2", "k_k", "k_a", "r_k", "ln_g", "ln_b")


def _rwkv(u, p, *, batch, seq):
    rows = RWKV_CHUNK * RWKV_GROUP
    nc = seq // rows
    params = [p[n] for n in _RWKV_PARAM_NAMES]
    sspec = pl.BlockSpec((1, RWKV_PAIRS, RWKV_HEAD, PAIR), lambda b, c: (b, 0, 0, 0))
    y, shift, s_last = pl.pallas_call(
        functools.partial(_rwkv_body, chunk=RWKV_CHUNK, group=RWKV_GROUP),
        grid=(batch, nc),
        in_specs=[pl.BlockSpec((rows, RWKV_PROJ), lambda b, c: (b * nc + c, 0))] + [_full_spec(a.shape) for a in params],
        out_specs=[pl.BlockSpec((rows, GROUP_WIDTH), lambda b, c: (b * nc + c, 0)),
                   pl.BlockSpec((1, 1, RWKV_PROJ), lambda b, c: (b, 0, 0)), sspec],
        out_shape=[jax.ShapeDtypeStruct((batch * seq, GROUP_WIDTH), F32),
                   jax.ShapeDtypeStruct((batch, 1, RWKV_PROJ), F32),
                   jax.ShapeDtypeStruct((batch, RWKV_PAIRS, RWKV_HEAD, PAIR), F32)],
        scratch_shapes=[pltpu.VMEM((SUBLANES + rows, RWKV_PROJ), F32),
                        pltpu.VMEM((RWKV_PAIRS, RWKV_HEAD, PAIR), F32)],
        compiler_params=_cparams("parallel", "arbitrary"),
        name="rwkv",
    )(u, *params)
    s_last = s_last.reshape(batch, RWKV_PAIRS, RWKV_HEAD, 2, RWKV_HEAD).transpose(0, 1, 3, 2, 4).reshape(
        batch, RWKV_HEADS, RWKV_HEAD, RWKV_HEAD)
    return y, shift.reshape(batch, RWKV_PROJ), s_last


def _rwkv_step_body(u_ref, shift0_ref, s0_ref, *rest, seq, batch, layer):
    sdone_ref, rest = (rest[0], rest[1:]) if layer else (None, rest)
    (mu_ref, w0_ref, w2_ref, a0_ref, a2_ref, g2_ref, kk_ref, ka_ref, rk_ref, lng_ref, lnb_ref, y_ref, sout_ref,
     r_scr, w_scr, k_scr, b_scr, nkk_scr, v_scr, y_scr) = rest
    T, B = seq, batch
    j = pl.program_id(0)
    if layer:
        sout_ref[0:layer] = sdone_ref[...]
    tiles = RWKV_STEP_TILES

    def pointwise(t):
        u = u_ref[t * B:(t + 1) * B, :]
        prev = shift0_ref[...] if t == 0 else u_ref[(t - 1) * B:t * B, :]
        return _rwkv_pointwise(u, prev, mu_ref, w0_ref, w2_ref, a0_ref, a2_ref, g2_ref, kk_ref, ka_ref)

    @pl.when(j == 0)
    def _():
        for t in range(T):
            r, k, v, logdecay, a, _, kk = pointwise(t)
            r_scr[t] = r.T
            w_scr[t] = jnp.exp(logdecay).T
            k_scr[t] = k.T
            b_scr[t] = (kk * a).T
            nkk_scr[t] = (-kk).T
            v_scr[t] = v.T

    i0 = j * (2 * tiles)
    keys = pl.ds(pl.multiple_of((i0 // RWKV_HEAD) * RWKV_HEAD, RWKV_HEAD), RWKV_HEAD)
    for q in range(tiles):
        s_pair = s0_ref[:, q * PAIR:(q + 1) * PAIR].T
        halves = []
        for half in range(2):
            vi = pl.ds(i0 + 2 * q + half, 1)
            s = s_pair[half * RWKV_HEAD:(half + 1) * RWKV_HEAD]
            for t in range(T):
                sa = jnp.sum(s * nkk_scr[t, keys, :], axis=0, keepdims=True)
                s = s * w_scr[t, keys, :] + k_scr[t, keys, :] * v_scr[t, vi, :] + b_scr[t, keys, :] * sa
                y_scr[t, vi, :] = jnp.sum(s * r_scr[t, keys, :], axis=0, keepdims=True)
            halves.append(s)
        sout_ref[layer, :, q * PAIR:(q + 1) * PAIR] = jnp.concatenate(halves, axis=0).T

    @pl.when(j == pl.num_programs(0) - 1)
    def _():
        for t in range(T):
            r, k, v, _, _, g, _ = pointwise(t)
            y_ref[t * B:(t + 1) * B, :] = _rwkv_finish(y_scr[t].T, r, k, v, g, rk_ref, lng_ref, lnb_ref)


def _rwkv_step(u, shift0, s_all, s_done, p, *, batch, seq, layer):
    n = batch * seq
    nstate = RWKV_HEADS * RWKV_HEAD * RWKV_HEAD
    tile_w = RWKV_STEP_TILES * PAIR
    params = [p[nm] for nm in _RWKV_PARAM_NAMES]
    sspec, prev_specs, sout_spec = _layer_state_specs(layer, batch, tile_w)
    prev_args = [s_done] if layer else []
    tposed = pltpu.VMEM((seq, GROUP_WIDTH, batch), F32)
    return pl.pallas_call(
        functools.partial(_rwkv_step_body, seq=seq, batch=batch, layer=layer),
        grid=(nstate // tile_w,),
        in_specs=[_full_spec((n, RWKV_PROJ)), _full_spec((batch, RWKV_PROJ)), sspec] + prev_specs
                 + [_full_spec(a.shape) for a in params],
        out_specs=[_full_spec((n, GROUP_WIDTH)), sout_spec],
        out_shape=[jax.ShapeDtypeStruct((n, GROUP_WIDTH), F32),
                   jax.ShapeDtypeStruct((layer + 1, batch, nstate), F32)],
        scratch_shapes=[tposed] * 7,
        compiler_params=_cparams("arbitrary"),
        name="rwkv_step",
    )(u, shift0, s_all.reshape(s_all.shape[0], batch, nstate), *prev_args, *params)


def _s5_body(u_ref, hre0_ref, him0_ref, are_ref, aim_ref, bmat_ref, cmat_ref, d_ref, gw_ref, gb_ref,
             y_ref, hre_ref, him_ref, hs_scr, *, steps):
    c = pl.program_id(1)
    ns = S5_WIDTH
    bsub = u_ref.shape[1]

    @pl.when(c == 0)
    def _():
        hre_ref[...] = hre0_ref[...]
        him_ref[...] = him0_ref[...]

    u = u_ref[...].reshape(steps * bsub, GROUP_WIDTH)
    hs_scr[...] = _dot(u.astype(BF16), bmat_ref[...])
    are = jnp.broadcast_to(are_ref[...], (bsub, ns))
    aim = jnp.broadcast_to(aim_ref[...], (bsub, ns))

    def step(t, carry):
        hre, him = carry
        r0 = pl.multiple_of(t * bsub, bsub)
        nre = are * hre - aim * him + hs_scr[pl.ds(r0, bsub), 0:ns]
        nim = are * him + aim * hre + hs_scr[pl.ds(r0, bsub), ns:2 * ns]
        hs_scr[pl.ds(r0, bsub), 0:ns] = nre
        hs_scr[pl.ds(r0, bsub), ns:2 * ns] = nim
        return nre, nim

    hre, him = lax.fori_loop(0, steps, step, (hre_ref[...], him_ref[...]))
    hre_ref[...] = hre
    him_ref[...] = him
    y = _dot(hs_scr[...].astype(BF16), cmat_ref[...]) + u * d_ref[...]
    y = _gelu_tanh(y)
    yy = _dot(y.astype(BF16), gw_ref[...]) + gb_ref[...]
    out = yy[:, 0:GROUP_WIDTH] * _sigmoid(yy[:, GROUP_WIDTH:2 * GROUP_WIDTH])
    y_ref[...] = out.reshape(steps, bsub, GROUP_WIDTH)


def _s5(u_tm, hre0, him0, lp):
    seq, batch, _ = u_tm.shape
    steps = min(TM_CHUNK, seq)
    bsub = SUBLANES
    hspec = pl.BlockSpec((bsub, S5_WIDTH), lambda b, c: (b, 0))
    tspec = pl.BlockSpec((steps, bsub, GROUP_WIDTH), lambda b, c: (c, b, 0))
    consts = (lp["s5_are"], lp["s5_aim"], lp["s5_bmat"], lp["s5_cmat"], lp["s5_d"], lp["s5_gw"], lp["s5_gb"])
    return pl.pallas_call(
        functools.partial(_s5_body, steps=steps),
        grid=(batch // bsub, seq // steps),
        in_specs=[tspec, hspec, hspec] + [_full_spec(a.shape) for a in consts],
        out_specs=[tspec, hspec, hspec],
        out_shape=[jax.ShapeDtypeStruct((seq, batch, GROUP_WIDTH), F32),
                   jax.ShapeDtypeStruct((batch, S5_WIDTH), F32),
                   jax.ShapeDtypeStruct((batch, S5_WIDTH), F32)],
        scratch_shapes=[pltpu.VMEM((steps * bsub, 2 * S5_WIDTH), F32)],
        compiler_params=_cparams("parallel", "arbitrary"),
        name="s5",
    )(u_tm, hre0, him0, *consts)


def _pool_body(u_ref, buf0_ref, pw_ref, sc_ref, y_ref, buf_ref, f_scr, *, steps, pos0):
    c = pl.program_id(1)
    bsub = u_ref.shape[1]
    GW = GROUP_WIDTH
    halo = POOL_BUF + 1

    @pl.when(c == 0)
    def _():
        f_scr[0] = jnp.zeros((bsub, GW), F32)
        for i in range(POOL_BUF):
            f_scr[1 + i] = buf0_ref[:, i * GW:(i + 1) * GW]

    u = u_ref[...]
    f_scr[halo:halo + steps] = u
    f = f_scr[...]
    s2 = f[1:] + f[:-1]
    s4 = s2[2:] + s2[:-2]
    s8 = s4[4:] + s4[:-4]
    s16 = s8[8:] + s8[:-8]
    f_scr[0:halo] = f[steps:steps + halo]
    lane = lax.broadcasted_iota(jnp.int32, (steps, bsub, GW), 2)
    tpos = lax.broadcasted_iota(jnp.int32, (steps, bsub, GW), 0) + (pos0 + 1) + c * steps
    win = jnp.where(lane < POOL_CH, s2[halo - 1:halo - 1 + steps],
                    jnp.where(lane < 2 * POOL_CH, s4[halo - 3:halo - 3 + steps],
                              jnp.where(lane < 3 * POOL_CH, s8[halo - 7:halo - 7 + steps],
                                        s16[halo - 15:halo - 15 + steps])))
    wlen = jnp.where(lane < POOL_CH, POOL_WINDOWS[0],
                     jnp.where(lane < 2 * POOL_CH, POOL_WINDOWS[1],
                               jnp.where(lane < 3 * POOL_CH, POOL_WINDOWS[2], POOL_WINDOWS[3])))
    cnt = jnp.minimum(tpos, wlen).astype(F32)
    pooled = (win / cnt - u).reshape(steps * bsub, GW)
    y = _dot(pooled.astype(BF16), pw_ref[...]) * sc_ref[...]
    y_ref[...] = y.reshape(steps, bsub, GW)

    @pl.when(c == pl.num_programs(1) - 1)
    def _():
        for i in range(POOL_BUF):
            buf_ref[:, i * GW:(i + 1) * GW] = f_scr[1 + i]


def _pool(u_tm, buf0, lp, *, pos0):
    seq, batch, _ = u_tm.shape
    steps = min(TM_CHUNK, seq)
    bsub = SUBLANES
    flat = POOL_BUF * GROUP_WIDTH
    tspec = pl.BlockSpec((steps, bsub, GROUP_WIDTH), lambda b, c: (c, b, 0))
    bspec = pl.BlockSpec((bsub, flat), lambda b, c: (b, 0))
    y, buf = pl.pallas_call(
        functools.partial(_pool_body, steps=steps, pos0=pos0),
        grid=(batch // bsub, seq // steps),
        in_specs=[tspec, bspec, _full_spec(lp["pool_w"].shape), _full_spec(lp["pool_scale"].shape)],
        out_specs=[tspec, bspec],
        out_shape=[jax.ShapeDtypeStruct((seq, batch, GROUP_WIDTH), F32), jax.ShapeDtypeStruct((batch, flat), F32)],
        scratch_shapes=[pltpu.VMEM((POOL_BUF + 1 + steps, bsub, GROUP_WIDTH), F32)],
        compiler_params=_cparams("parallel", "arbitrary"),
        name="pool",
    )(u_tm, buf0.reshape(batch, flat), lp["pool_w"], lp["pool_scale"])
    return y, buf.reshape(batch, POOL_BUF, GROUP_WIDTH)


def _block_diag(blocks):
    g, r, c = blocks.shape
    eye = jnp.eye(g, dtype=blocks.dtype)
    return (eye[:, None, :, None] * blocks[:, :, None, :]).reshape(g * r, g * c)


def _pad_lanes(v, n=LANES):
    return jnp.pad(v, (0, n - v.shape[0])).reshape(1, n)


def _layer_params(l, P):
    row = lambda a: a.reshape(1, -1)
    w_in = P["w_in"][l]
    z_w, xbc_w, dt_w, rw_w, s5_w, pool_w = jnp.split(
        w_in, [256, 256 + 768, 256 + 768 + 4, 1028 + 1024, 1028 + 1024 + 256], axis=1)
    w_all = jnp.concatenate([z_w, xbc_w, rw_w, s5_w, pool_w, jnp.pad(dt_w, ((0, 0), (0, LANES - SSD_HEADS)))], axis=1)

    lam = lax.complex(P["s5_lam_re"][l], P["s5_lam_im"][l])
    a_bar = jnp.exp(lam * jnp.exp(P["s5_log_step"][l])[:, None])
    b_bar = ((a_bar - 1.0) / lam)[..., None] * lax.complex(P["s5_b_re"][l], P["s5_b_im"][l])
    b_t = jnp.swapaxes(b_bar, 1, 2)
    bmat = jnp.concatenate([_block_diag(jnp.real(b_t)), _block_diag(jnp.imag(b_t))], axis=1)
    c_t = jnp.swapaxes(lax.complex(P["s5_c_re"][l], P["s5_c_im"][l]), 1, 2)
    cmat = jnp.concatenate([_block_diag(jnp.real(c_t)), -_block_diag(jnp.imag(c_t))], axis=0)
    head_expand = jnp.pad(jnp.repeat(jnp.eye(SSD_HEADS, dtype=F32), SSD_HEAD_DIM, axis=1),
                          ((0, LANES - SSD_HEADS), (0, 0)))

    return dict(
        norm_ffn1=row(P["norm_ffn1"][l]), ffn1_in=P["ffn1_in"][l].astype(BF16), ffn1_out=P["ffn1_out"][l].astype(BF16),
        norm_mix=row(P["norm_mix"][l]), w_all=w_all.astype(BF16),
        conv_w=P["ssd_conv_w"][l], conv_b=row(P["ssd_conv_b"][l]),
        dt_bias=_pad_lanes(P["ssd_dt_bias"][l]), a_log=_pad_lanes(P["ssd_a_log"][l]),
        a_neg_exp=row(jnp.repeat(-jnp.exp(P["ssd_a_log"][l]), SSD_HEAD_DIM)), head_expand=head_expand,
        d_skip=row(jnp.repeat(P["ssd_d"][l], SSD_HEAD_DIM)), ssd_norm=row(P["ssd_norm"][l]),
        rwkv=dict(mu=row(P["rwkv_mu"][l]), w0=row(P["rwkv_w0"][l]), w2=P["rwkv_w2"][l].astype(BF16),
                  a0=row(P["rwkv_a0"][l]), a2=P["rwkv_a2"][l].astype(BF16), g2=P["rwkv_g2"][l].astype(BF16),
                  k_k=row(P["rwkv_k_k"][l]), k_a=row(P["rwkv_k_a"][l]), r_k=row(P["rwkv_r_k"][l]),
                  ln_g=row(P["rwkv_ln_g"][l]), ln_b=row(P["rwkv_ln_b"][l])),
        s5_are=row(jnp.real(a_bar)), s5_aim=row(jnp.imag(a_bar)), s5_bmat=bmat.astype(BF16), s5_cmat=cmat.astype(BF16),
        s5_d=row(P["s5_d"][l]), s5_gw=P["s5_glu_w"][l].astype(BF16), s5_gb=row(P["s5_glu_b"][l]),
        pool_w=_block_diag(P["pool_w"][l]).astype(BF16), pool_scale=row(P["pool_scale"][l]),
        w_out=P["w_out"][l].astype(BF16),
        norm_ffn2=row(P["norm_ffn2"][l]), ffn2_in=P["ffn2_in"][l].astype(BF16), ffn2_out=P["ffn2_out"][l].astype(BF16),
    )


def _to_tm(rows, batch, seq):
    return jnp.swapaxes(rows.reshape(batch, seq, rows.shape[-1]), 0, 1)


def _from_tm(x_tm):
    seq, batch, w = x_tm.shape
    return jnp.swapaxes(x_tm, 0, 1).reshape(batch * seq, w)


def _mixers_prompt(lp, proj, *, batch, seq):
    z, xbc, ur, us5, upool, dtr = proj
    y_ssd, conv_new, ssd_new = _ssd(z, xbc, dtr, lp, batch=batch, seq=seq)
    y_rwkv, shift_new, rwkv_new = _rwkv(ur, lp["rwkv"], batch=batch, seq=seq)
    zeros = jnp.zeros((batch, S5_WIDTH), F32)
    y_s5, s5re, s5im = _s5(_to_tm(us5, batch, seq), zeros, zeros, lp)
    y_pool, pool_new = _pool(_to_tm(upool, batch, seq), jnp.zeros((batch, POOL_BUF, GROUP_WIDTH), F32), lp, pos0=0)
    ys = (y_ssd, y_rwkv, _from_tm(y_s5), _from_tm(y_pool))
    states = (conv_new, ssd_new, shift_new, rwkv_new, s5re.reshape(batch, S5_GROUPS, S5_STATE),
              s5im.reshape(batch, S5_GROUPS, S5_STATE), pool_new)
    return ys, states


def _mixers_decode(lp, proj, states, done, *, batch, seq, layer):
    z, xbc, ur, us5, upool, dtr = proj
    conv0, shift0, s5re0, s5im0, pool0 = (states[i][layer] for i in (0, 2, 4, 5, 6))
    ssd_done, rwkv_done = (done[1], done[3]) if layer else (None, None)
    y_ssd, conv_new, ssd_new = _ssd_step(z, xbc, dtr, conv0, states[1], ssd_done, lp, batch=batch, seq=seq,
                                         layer=layer)
    y_rwkv, rwkv_new = _rwkv_step(ur, shift0, states[3], rwkv_done, lp["rwkv"], batch=batch, seq=seq, layer=layer)
    shift_new = ur[(seq - 1) * batch:, :]
    tm = lambda a: a.reshape(seq, batch, a.shape[-1])
    y_s5, s5re, s5im = _s5(tm(us5), s5re0.reshape(batch, S5_WIDTH), s5im0.reshape(batch, S5_WIDTH), lp)
    y_pool, pool_new = _pool(tm(upool), pool0, lp, pos0=PAST_LEN)
    rows = lambda a: a.reshape(seq * batch, a.shape[-1])
    ys = (y_ssd, y_rwkv, rows(y_s5), rows(y_pool))
    new_states = (conv_new, ssd_new, shift_new, rwkv_new, s5re.reshape(batch, S5_GROUPS, S5_STATE),
                  s5im.reshape(batch, S5_GROUPS, S5_STATE), pool_new)
    return ys, new_states


_WIDTHS = (GROUP_WIDTH, SSD_CONV_DIM, RWKV_PROJ, GROUP_WIDTH, GROUP_WIDTH, LANES)


def _trunk(x, layer_params, norm_final, mixers):
    states = []
    mix, lp = None, None
    for l, lp_next in enumerate(layer_params):
        if l > 0:
            x = _ffn(x, lp["norm_ffn2"], lp["ffn2_in"], lp["ffn2_out"], mix=mix, wmix=lp["w_out"])
        lp = lp_next
        x = _ffn(x, lp["norm_ffn1"], lp["ffn1_in"], lp["ffn1_out"])
        proj = _inproj(x, lp["norm_mix"], lp["w_all"], _WIDTHS)
        mix, st = mixers(l, lp, proj, states[-1] if states else None)
        states.append(st)
    x = _ffn(x, lp["norm_ffn2"], lp["ffn2_in"], lp["ffn2_out"], mix=mix, wmix=lp["w_out"], gf=norm_final)
    return x, states


def kernel(x_prompt, x_sample, state_ssd_conv, state_ssd, state_rwkv_shift, state_rwkv, state_s5_re, state_s5_im, state_pool, norm_ffn1, ffn1_in, ffn1_out, norm_mix, w_in, ssd_conv_w, ssd_conv_b, ssd_dt_bias, ssd_a_log, ssd_d, ssd_norm, rwkv_mu, rwkv_w0, rwkv_w2, rwkv_a0, rwkv_a2, rwkv_g2, rwkv_k_k, rwkv_k_a, rwkv_r_k, rwkv_ln_g, rwkv_ln_b, s5_lam_re, s5_lam_im, s5_log_step, s5_b_re, s5_b_im, s5_c_re, s5_c_im, s5_d, s5_glu_w, s5_glu_b, pool_w, pool_scale, w_out, norm_ffn2, ffn2_in, ffn2_out, norm_final):
    P = dict(norm_ffn1=norm_ffn1, ffn1_in=ffn1_in, ffn1_out=ffn1_out, norm_mix=norm_mix, w_in=w_in,
             ssd_conv_w=ssd_conv_w, ssd_conv_b=ssd_conv_b, ssd_dt_bias=ssd_dt_bias, ssd_a_log=ssd_a_log,
             ssd_d=ssd_d, ssd_norm=ssd_norm, rwkv_mu=rwkv_mu, rwkv_w0=rwkv_w0, rwkv_w2=rwkv_w2, rwkv_a0=rwkv_a0,
             rwkv_a2=rwkv_a2, rwkv_g2=rwkv_g2, rwkv_k_k=rwkv_k_k, rwkv_k_a=rwkv_k_a,
             rwkv_r_k=rwkv_r_k.reshape(rwkv_r_k.shape[0], -1), rwkv_ln_g=rwkv_ln_g, rwkv_ln_b=rwkv_ln_b,
             s5_lam_re=s5_lam_re, s5_lam_im=s5_lam_im, s5_log_step=s5_log_step, s5_b_re=s5_b_re, s5_b_im=s5_b_im,
             s5_c_re=s5_c_re, s5_c_im=s5_c_im, s5_d=s5_d, s5_glu_w=s5_glu_w, s5_glu_b=s5_glu_b, pool_w=pool_w,
             pool_scale=pool_scale, w_out=w_out, norm_ffn2=norm_ffn2, ffn2_in=ffn2_in, ffn2_out=ffn2_out)
    depth = norm_ffn1.shape[0]
    bp, tp, d = x_prompt.shape
    bs, ts, _ = x_sample.shape
    layer_params = [_layer_params(l, P) for l in range(depth)]
    gf = norm_final.reshape(1, -1)
    sample_states = (state_ssd_conv, state_ssd, state_rwkv_shift, state_rwkv, state_s5_re, state_s5_im, state_pool)

    y_p, st_p = _trunk(x_prompt.reshape(bp * tp, d), layer_params, gf,
                       lambda l, lp, proj, done: _mixers_prompt(lp, proj, batch=bp, seq=tp))
    x_s = jnp.swapaxes(x_sample, 0, 1).reshape(ts * bs, d)
    y_s, st_s = _trunk(x_s, layer_params, gf,
                       lambda l, lp, proj, done: _mixers_decode(lp, proj, sample_states, done,
                                                                batch=bs, seq=ts, layer=l))
    outs = [y_p.reshape(bp, tp, d), jnp.swapaxes(y_s.reshape(ts, bs, d), 0, 1)]
    for i, ref_state in enumerate(sample_states):
        outs.append(jnp.stack([st[i] for st in st_p]))
        if i in (1, 3):
            outs.append(st_s[-1][i].reshape(ref_state.shape))
        else:
            outs.append(jnp.stack([st[i] for st in st_s]))
    return tuple(outs)
```

```python
import functools
import math

import jax
import jax.numpy as jnp
from jax import lax
from jax.experimental import pallas as pl
from jax.experimental.pallas import tpu as pltpu

F32 = jnp.float32
BF16 = jnp.bfloat16
HIGHEST = lax.Precision.HIGHEST

SUBLANES = 8
LANES = 128
VMEM_LIMIT_BYTES = 56 * 1024 * 1024

GROUP_WIDTH = 256
SSD_HEAD_DIM = 64
SSD_HEADS = 4
SSD_GROUPS = 2
SSD_STATE = 128
SSD_CONV = 4
SSD_CONV_DIM = GROUP_WIDTH + 2 * SSD_GROUPS * SSD_STATE
SSD_CHUNK = 128
RWKV_HEAD = 64
RWKV_HEADS = 4
RWKV_PROJ = 1024
RWKV_LN_EPS = 64e-5
RWKV_CHUNK = 64
RWKV_GROUP = 8
S5_GROUP_CH = 16
S5_GROUPS = 16
S5_STATE = 64
S5_WIDTH = S5_GROUPS * S5_STATE
POOL_WINDOWS = (2, 4, 8, 16)
POOL_CH = 64
POOL_BUF = 15
RMS_EPS = 1e-6
PAST_LEN = 16384

ROW_TILE = 512
FFN_CHUNK = 256
TM_CHUNK = 64
SSD_STEP_TILES = 16
RWKV_STEP_TILES = 16


def _cparams(*sem):
    return pltpu.CompilerParams(dimension_semantics=sem, vmem_limit_bytes=VMEM_LIMIT_BYTES)


def _dot(a, b, **kw):
    return jnp.dot(a, b, preferred_element_type=F32, **kw)


def _dot_nt(a, b):
    return lax.dot_general(a, b, (((1,), (1,)), ((), ())), preferred_element_type=F32)


def _dot_tn(a, b):
    return lax.dot_general(a, b, (((0,), (0,)), ((), ())), preferred_element_type=F32)


def _sigmoid(x):
    return 1.0 / (1.0 + jnp.exp(-x))


def _silu(x):
    return x * _sigmoid(x)


def _softplus(x):
    return jnp.maximum(x, 0.0) + jnp.log(1.0 + jnp.exp(-jnp.abs(x)))


def _gelu_tanh(x):
    c = math.sqrt(2.0 / math.pi)
    return x * (0.5 * (1.0 + jnp.tanh(c * (x + 0.044715 * (x * x * x)))))


def _rms(x, g):
    return x * lax.rsqrt(jnp.mean(x * x, axis=-1, keepdims=True) + RMS_EPS) * g


def _full_spec(shape):
    n = len(shape)
    return pl.BlockSpec(shape, lambda *_: (0,) * n)


def _ffn_body(*refs, has_mix, final_norm):
    it = iter(refs)
    x_ref = next(it)
    x = x_ref[...]
    if has_mix:
        y_refs = [next(it) for _ in range(4)]
        wmix_ref = next(it)
        for i, y_ref in enumerate(y_refs):
            x = x + _dot(y_ref[...].astype(BF16), wmix_ref[i * GROUP_WIDTH:(i + 1) * GROUP_WIDTH, :])
    g_ref, wi_ref, wo_ref = next(it), next(it), next(it)
    gf_ref = next(it) if final_norm else None
    o_ref = next(it)
    h = _rms(x, g_ref[...]).astype(BF16)
    d_ff = wo_ref.shape[0]
    acc = jnp.zeros_like(x)
    for c in range(d_ff // FFN_CHUNK):
        lo = c * FFN_CHUNK
        gate = _dot(h, wi_ref[:, lo:lo + FFN_CHUNK])
        up = _dot(h, wi_ref[:, d_ff + lo:d_ff + lo + FFN_CHUNK])
        act = (_silu(gate) * up).astype(BF16)
        acc = acc + _dot(act, wo_ref[lo:lo + FFN_CHUNK, :])
    x = x + 0.5 * acc
    if final_norm:
        x = _rms(x, gf_ref[...])
    o_ref[...] = x


def _ffn(x, g, wi, wo, mix=None, wmix=None, gf=None):
    rows, d = x.shape
    row_spec = lambda w: pl.BlockSpec((ROW_TILE, w), lambda i: (i, 0))
    args, specs = [x], [row_spec(d)]
    if mix is not None:
        for y in mix:
            args.append(y)
            specs.append(row_spec(y.shape[1]))
        args.append(wmix)
        specs.append(_full_spec(wmix.shape))
    for a in (g, wi, wo) + ((gf,) if gf is not None else ()):
        args.append(a)
        specs.append(_full_spec(a.shape))
    return pl.pallas_call(
        functools.partial(_ffn_body, has_mix=mix is not None, final_norm=gf is not None),
        grid=(rows // ROW_TILE,),
        in_specs=specs,
        out_specs=row_spec(d),
        out_shape=jax.ShapeDtypeStruct((rows, d), F32),
        compiler_params=_cparams("parallel"),
        name="ffn",
    )(*args)


def _inproj_body(x_ref, g_ref, w_ref, *o_refs):
    h = _rms(x_ref[...], g_ref[...]).astype(BF16)
    off = 0
    for o_ref in o_refs:
        n = o_ref.shape[-1]
        o_ref[...] = _dot(h, w_ref[:, off:off + n])
        off += n


def _inproj(x, g, w, widths):
    rows, d = x.shape
    row_spec = lambda w_: pl.BlockSpec((ROW_TILE, w_), lambda i: (i, 0))
    return pl.pallas_call(
        _inproj_body,
        grid=(rows // ROW_TILE,),
        in_specs=[row_spec(d), _full_spec(g.shape), _full_spec(w.shape)],
        out_specs=[row_spec(n) for n in widths],
        out_shape=[jax.ShapeDtypeStruct((rows, n), F32) for n in widths],
        compiler_params=_cparams("parallel"),
        name="inproj",
    )(x, g, w)


def _ssd_body(z_ref, xbc_ref, dt_ref, cw_ref, cb_ref, dtb_ref, alog_ref, dsk_ref, ng_ref,
              y_ref, conv_ref, hout_ref, xpad_scr, h_scr, *, chunk):
    L = chunk
    c = pl.program_id(1)
    pad = SUBLANES
    halo = SSD_CONV - 1

    @pl.when(c == 0)
    def _():
        xpad_scr[0:pad, :] = jnp.zeros((pad, SSD_CONV_DIM), F32)
        h_scr[...] = jnp.zeros(h_scr.shape, F32)

    xpad_scr[pad:pad + L, :] = xbc_ref[...]
    conv = cb_ref[...] + cw_ref[0:1, :] * xpad_scr[pad - halo:pad - halo + L, :]
    for j in range(1, SSD_CONV):
        conv = conv + cw_ref[j:j + 1, :] * xpad_scr[pad - halo + j:pad - halo + j + L, :]
    xpad_scr[pad - halo:pad, :] = xpad_scr[pad + L - halo:pad + L, :]
    conv = _silu(conv)
    xs = conv[:, 0:GROUP_WIDTH]
    bm = conv[:, GROUP_WIDTH:2 * GROUP_WIDTH].astype(BF16)
    cm = conv[:, 2 * GROUP_WIDTH:3 * GROUP_WIDTH].astype(BF16)

    row = lax.broadcasted_iota(jnp.int32, (L, L), 0)
    col = lax.broadcasted_iota(jnp.int32, (L, L), 1)
    causal = row >= col
    dt = _softplus(dt_ref[...] + dtb_ref[...])
    da = dt * (-jnp.exp(alog_ref[...]))
    acs = _dot(jnp.where(causal, 1.0, 0.0).astype(F32), da, precision=HIGHEST)
    acs_t = acs.T
    e_acs = jnp.exp(acs)
    acs_last = acs[L - 1:L, :]
    e_end = jnp.exp(acs_last - acs)
    e_last = jnp.exp(acs_last)

    ys = []
    for h in range(SSD_HEADS):
        g = h // (SSD_HEADS // SSD_GROUPS)
        bg = bm[:, g * SSD_STATE:(g + 1) * SSD_STATE]
        cg = cm[:, g * SSD_STATE:(g + 1) * SSD_STATE]
        x_h = xs[:, h * SSD_HEAD_DIM:(h + 1) * SSD_HEAD_DIM]
        xdt = x_h * dt[:, h:h + 1]
        seg = acs[:, h:h + 1] - acs_t[h:h + 1, :]
        decay = jnp.exp(jnp.where(causal, seg, -jnp.inf))
        scores = _dot_nt(cg, bg) * decay
        y_h = _dot(scores.astype(BF16), xdt.astype(BF16))
        h_prev = h_scr[h]
        y_h = y_h + _dot_nt(cg, h_prev.astype(BF16)) * e_acs[:, h:h + 1]
        st = _dot_tn((xdt * e_end[:, h:h + 1]).astype(BF16), bg)
        h_scr[h] = h_prev * e_last[:, h:h + 1] + st
        ys.append(y_h)
    y = jnp.concatenate(ys, axis=-1) + xs * dsk_ref[...]
    y = y * _silu(z_ref[...])
    y_ref[...] = _rms(y, ng_ref[...])

    @pl.when(c == pl.num_programs(1) - 1)
    def _():
        hout_ref[0] = h_scr[...]
        conv_ref[0] = xpad_scr[pad - halo:pad, :]


def _ssd(z, xbc, dtr, lp, *, batch, seq):
    chunk = SSD_CHUNK
    nc = seq // chunk
    rspec = lambda w: pl.BlockSpec((chunk, w), lambda b, c: (b * nc + c, 0))
    consts = (lp["conv_w"], lp["conv_b"], lp["dt_bias"], lp["a_log"], lp["d_skip"], lp["ssd_norm"])
    return pl.pallas_call(
        functools.partial(_ssd_body, chunk=chunk),
        grid=(batch, nc),
        in_specs=[rspec(GROUP_WIDTH), rspec(SSD_CONV_DIM), rspec(LANES)] + [_full_spec(a.shape) for a in consts],
        out_specs=[rspec(GROUP_WIDTH),
                   pl.BlockSpec((1, SSD_CONV - 1, SSD_CONV_DIM), lambda b, c: (b, 0, 0)),
                   pl.BlockSpec((1, SSD_HEADS, SSD_HEAD_DIM, SSD_STATE), lambda b, c: (b, 0, 0, 0))],
        out_shape=[jax.ShapeDtypeStruct((batch * seq, GROUP_WIDTH), F32),
                   jax.ShapeDtypeStruct((batch, SSD_CONV - 1, SSD_CONV_DIM), F32),
                   jax.ShapeDtypeStruct((batch, SSD_HEADS, SSD_HEAD_DIM, SSD_STATE), F32)],
        scratch_shapes=[pltpu.VMEM((SUBLANES + chunk, SSD_CONV_DIM), F32),
                        pltpu.VMEM((SSD_HEADS, SSD_HEAD_DIM, SSD_STATE), F32)],
        compiler_params=_cparams("parallel", "arbitrary"),
        name="ssd",
    )(z, xbc, dtr, *consts)


def _ssd_step_body(z_ref, xbc_ref, dt_ref, conv0_ref, h0_ref, *rest, seq, batch, layer):
    hdone_ref, rest = (rest[0], rest[1:]) if layer else (None, rest)
    (cw_ref, cb_ref, dtb_ref, aneg_ref, dsk_ref, ng_ref, hexp_ref, y_ref, conv_ref, hout_ref,
     xs_scr, bm_scr, cm_scr, xdt_scr, dec_scr, y_scr) = rest
    T, B = seq, batch
    if layer:
        hout_ref[0:layer] = hdone_ref[...]
    GW = GROUP_WIDTH
    j = pl.program_id(0)
    tiles = SSD_STEP_TILES

    @pl.when(j == 0)
    def _():
        rows = [conv0_ref[:, i * SSD_CONV_DIM:(i + 1) * SSD_CONV_DIM] for i in range(SSD_CONV - 1)]
        rows += [xbc_ref[t * B:(t + 1) * B, :] for t in range(T)]
        for t in range(T):
            conv = cb_ref[...] + cw_ref[0:1, :] * rows[t]
            for i in range(1, SSD_CONV):
                conv = conv + cw_ref[i:i + 1, :] * rows[t + i]
            conv = _silu(conv)
            xs = conv[:, 0:GW]
            xs_scr[t] = xs
            for g in range(SSD_GROUPS):
                bm_scr[t, g] = conv[:, GW + g * SSD_STATE:GW + (g + 1) * SSD_STATE].T
                cm_scr[t, g] = conv[:, 2 * GW + g * SSD_STATE:2 * GW + (g + 1) * SSD_STATE].T
            dt = _softplus(dt_ref[t * B:(t + 1) * B, :] + dtb_ref[...])
            dte = _dot(dt, hexp_ref[...], precision=HIGHEST)
            xdt_scr[t] = (xs * dte).T
            dec_scr[t] = jnp.exp(dte * aneg_ref[...]).T
        for i in range(SSD_CONV - 1):
            conv_ref[:, i * SSD_CONV_DIM:(i + 1) * SSD_CONV_DIM] = rows[T + i]

    hp0 = j * tiles
    grp = hp0 // (SSD_HEAD_DIM * (SSD_HEADS // SSD_GROUPS))
    for q in range(tiles):
        hp = pl.ds(hp0 + q, 1)
        h = h0_ref[:, q, :].T
        for t in range(T):
            h = h * dec_scr[t, hp, :] + bm_scr[t, grp] * xdt_scr[t, hp, :]
            y_scr[t, hp, :] = jnp.sum(h * cm_scr[t, grp], axis=0, keepdims=True)
        hout_ref[layer, :, q, :] = h.T

    @pl.when(j == pl.num_programs(0) - 1)
    def _():
        for t in range(T):
            y = y_scr[t].T + xs_scr[t] * dsk_ref[...]
            y = y * _silu(z_ref[t * B:(t + 1) * B, :])
            y_ref[t * B:(t + 1) * B, :] = _rms(y, ng_ref[...])


def _layer_state_specs(layer, batch, rows, width):
    cur = pl.BlockSpec((None, batch, rows, width), lambda j: (layer, 0, j, 0))
    prev = [pl.BlockSpec((layer, batch, rows, width), lambda j: (0, 0, j, 0))] if layer else []
    out = pl.BlockSpec((layer + 1, batch, rows, width), lambda j: (0, 0, j, 0))
    return cur, prev, out


def _ssd_step(z, xbc, dtr, conv0, h_all, h_done, lp, *, batch, seq, layer):
    n = batch * seq
    srows = SSD_HEADS * SSD_HEAD_DIM
    consts = (lp["conv_w"], lp["conv_b"], lp["dt_bias"], lp["a_neg_exp"], lp["d_skip"], lp["ssd_norm"], lp["head_expand"])
    hspec, prev_specs, hout_spec = _layer_state_specs(layer, batch, SSD_STEP_TILES, SSD_STATE)
    prev_args = [h_done] if layer else []
    cflat = (SSD_CONV - 1) * SSD_CONV_DIM
    y, conv_new, h_new = pl.pallas_call(
        functools.partial(_ssd_step_body, seq=seq, batch=batch, layer=layer),
        grid=(srows // SSD_STEP_TILES,),
        in_specs=[_full_spec((n, GROUP_WIDTH)), _full_spec((n, SSD_CONV_DIM)), _full_spec((n, LANES)),
                  _full_spec((batch, cflat)), hspec] + prev_specs + [_full_spec(a.shape) for a in consts],
        out_specs=[_full_spec((n, GROUP_WIDTH)), _full_spec((batch, cflat)), hout_spec],
        out_shape=[jax.ShapeDtypeStruct((n, GROUP_WIDTH), F32),
                   jax.ShapeDtypeStruct((batch, cflat), F32),
                   jax.ShapeDtypeStruct((layer + 1, batch, srows, SSD_STATE), F32)],
        scratch_shapes=[pltpu.VMEM((seq, batch, GROUP_WIDTH), F32),
                        pltpu.VMEM((seq, SSD_GROUPS, SSD_STATE, batch), F32),
                        pltpu.VMEM((seq, SSD_GROUPS, SSD_STATE, batch), F32),
                        pltpu.VMEM((seq, GROUP_WIDTH, batch), F32),
                        pltpu.VMEM((seq, GROUP_WIDTH, batch), F32),
                        pltpu.VMEM((seq, GROUP_WIDTH, batch), F32)],
        compiler_params=_cparams("arbitrary"),
        name="ssd_step",
    )(z, xbc, dtr, conv0.reshape(batch, cflat), h_all.reshape(h_all.shape[0], batch, srows, SSD_STATE),
      *prev_args, *consts)
    return y, conv_new.reshape(batch, SSD_CONV - 1, SSD_CONV_DIM), h_new


PAIR = 2 * RWKV_HEAD
RWKV_PAIRS = RWKV_HEADS // 2


def _bd(x):
    lane = lax.broadcasted_iota(jnp.int32, x.shape, 1)
    zero = jnp.zeros_like(x)
    return jnp.concatenate([jnp.where(lane < RWKV_HEAD, x, zero), jnp.where(lane >= RWKV_HEAD, x, zero)], axis=0)


def _half_sums(x, lo):
    s_lo = jnp.sum(jnp.where(lo, x, 0.0), axis=-1, keepdims=True)
    s_hi = jnp.sum(jnp.where(lo, 0.0, x), axis=-1, keepdims=True)
    return jnp.where(lo, s_lo, s_hi)


def _head_sum(x):
    lo = lax.broadcasted_iota(jnp.int32, (x.shape[0], PAIR), 1) < RWKV_HEAD
    return jnp.concatenate([_half_sums(x[:, p * PAIR:(p + 1) * PAIR], lo) for p in range(RWKV_PAIRS)], axis=-1)


def _rwkv_pointwise(u, prev, mu_ref, w0_ref, w2_ref, a0_ref, a2_ref, g2_ref, kk_ref, ka_ref):
    GW = GROUP_WIDTH
    xs = u + (prev - u) * mu_ref[...]
    r = xs[:, 0:GW]
    k = xs[:, GW:2 * GW]
    v = xs[:, 2 * GW:3 * GW]
    wd = xs[:, 3 * GW:3 * GW + 64]
    ad = xs[:, 3 * GW + 64:3 * GW + 128]
    gd = xs[:, 3 * GW + 128:3 * GW + 256]
    w_lin = w0_ref[...] + _dot(jnp.tanh(wd).astype(BF16), w2_ref[...])
    logdecay = -jnp.exp(-_softplus(-w_lin) - 0.5)
    a = _sigmoid(a0_ref[...] + _dot(ad.astype(BF16), a2_ref[...]))
    g = _dot(_sigmoid(gd).astype(BF16), g2_ref[...])
    kk = k * kk_ref[...]
    kk = kk / jnp.maximum(jnp.sqrt(_head_sum(kk * kk)), 1e-12)
    k = k * (1.0 + (a - 1.0) * ka_ref[...])
    return r, k, v, logdecay, a, g, kk


def _rwkv_finish(y, r, k, v, g, rk_ref, lng_ref, lnb_ref):
    mean = _head_sum(y) * (1.0 / RWKV_HEAD)
    yc = y - mean
    var = _head_sum(yc * yc) * (1.0 / RWKV_HEAD)
    y = yc * lax.rsqrt(var + RWKV_LN_EPS) * lng_ref[...] + lnb_ref[...]
    bonus = _head_sum(r * k * rk_ref[...]) * v
    return (y + bonus) * g


def _rwkv_body(u_ref, mu_ref, w0_ref, w2_ref, a0_ref, a2_ref, g2_ref, kk_ref, ka_ref, rk_ref,
               lng_ref, lnb_ref, y_ref, shift_ref, sout_ref, upad_scr, s_scr, *, chunk, group):
    L, G = chunk, group
    GL = G * L
    c = pl.program_id(1)
    pad = SUBLANES

    @pl.when(c == 0)
    def _():
        upad_scr[0:pad, :] = jnp.zeros((pad, RWKV_PROJ), F32)
        s_scr[...] = jnp.zeros(s_scr.shape, F32)

    u = u_ref[...]
    upad_scr[pad:pad + GL, :] = u
    prev = upad_scr[pad - 1:pad - 1 + GL, :]
    upad_scr[pad - 1:pad, :] = u[GL - 1:GL, :]
    r, k, v, logdecay, a, g, kk = _rwkv_pointwise(u, prev, mu_ref, w0_ref, w2_ref, a0_ref, a2_ref, g2_ref,
                                                  kk_ref, ka_ref)

    tril = jnp.where(lax.broadcasted_iota(jnp.int32, (L, L), 0) >= lax.broadcasted_iota(jnp.int32, (L, L), 1),
                     1.0, 0.0).astype(F32)
    cl = jnp.concatenate([_dot(tril, logdecay[i * L:(i + 1) * L, :], precision=HIGHEST) for i in range(G)], axis=0)
    e_in = jnp.exp(cl)
    e_inv = jnp.exp(-cl)
    r_t = r * e_in
    r_tb = r_t.astype(BF16)
    a_tb = (-kk * jnp.exp(cl - logdecay)).astype(BF16)
    b_tb = (kk * a * e_inv).astype(BF16)
    k_tb = (k * e_inv).astype(BF16)
    vb = v.astype(BF16)

    row = lax.broadcasted_iota(jnp.int32, (L, PAIR), 0)
    colh = lax.broadcasted_iota(jnp.int32, (L, PAIR), 1) & (RWKV_HEAD - 1)
    strict = row > colh
    incl = row >= colh
    eye_pair = jnp.where(row == colh, 1.0, 0.0).astype(F32)
    lane_lo = lax.broadcasted_iota(jnp.int32, (RWKV_HEAD, PAIR), 1) < RWKV_HEAD
    same_head = (lax.broadcasted_iota(jnp.int32, (PAIR, PAIR), 0) < RWKV_HEAD) == \
                (lax.broadcasted_iota(jnp.int32, (PAIR, PAIR), 1) < RWKV_HEAD)

    streams = [(i, p) for i in range(G) for p in range(RWKV_PAIRS)]
    ns = len(streams)
    blk = lambda x, i, p: x[i * L:(i + 1) * L, p * PAIR:(p + 1) * PAIR]
    lhs = [jnp.concatenate([blk(a_tb, i, p), blk(r_tb, i, p)], axis=0) for i, p in streams]
    m_ab = [_dot_nt(lhs[s], _bd(blk(b_tb, i, p))) for s, (i, p) in enumerate(streams)]
    m_ak = [_dot_nt(lhs[s], _bd(blk(k_tb, i, p))) for s, (i, p) in enumerate(streams)]
    n_ab = [jnp.where(strict, m[0:L], 0.0) for m in m_ab]
    m_rb = [jnp.where(incl, m[L:2 * L], 0.0).astype(BF16) for m in m_ab]
    n_ak = [jnp.where(strict, m[0:L], 0.0).astype(BF16) for m in m_ak]
    m_rk = [jnp.where(incl, m[L:2 * L], 0.0).astype(BF16) for m in m_ak]
    tinv = [eye_pair + n for n in n_ab]
    pwb = [n.astype(BF16) for n in n_ab]
    pw = [_dot(x, _bd(x)) for x in pwb]
    for _ in range(int(math.log2(L)) - 2):
        pwb = [x.astype(BF16) for x in pw]
        both = [_dot(jnp.concatenate([pwb[s], tinv[s].astype(BF16)], axis=0), _bd(pwb[s])) for s in range(ns)]
        pw = [x[0:L] for x in both]
        tinv = [tinv[s] + both[s][L:2 * L] for s in range(ns)]
    pwb = [x.astype(BF16) for x in pw]
    tinv = [tinv[s] + _dot(tinv[s].astype(BF16), _bd(pwb[s])) for s in range(ns)]
    tinvb = [x.astype(BF16) for x in tinv]
    nv_mv = [_dot(jnp.concatenate([n_ak[s], m_rk[s]], axis=0), _bd(blk(vb, i, p))) for s, (i, p) in enumerate(streams)]
    wu = [_dot(tinvb[s], jnp.concatenate([_bd(blk(a_tb, i, p)), _bd(nv_mv[s][0:L].astype(BF16))], axis=1))
          for s, (i, p) in enumerate(streams)]
    wub = [x.astype(BF16) for x in wu]
    qy = [_dot(m_rb[s], jnp.concatenate([_bd(wub[s][:, 0:PAIR]), _bd(wub[s][:, PAIR:2 * PAIR])], axis=1))
          for s in range(ns)]
    q = [(blk(r_t, i, p) + qy[s][:, 0:PAIR]).astype(BF16) for s, (i, p) in enumerate(streams)]
    y_loc = [qy[s][:, PAIR:2 * PAIR] + nv_mv[s][L:2 * L] for s in range(ns)]
    zeros_b = jnp.zeros((L, PAIR), BF16)
    mg = [_dot_tn(jnp.concatenate([wub[s], jnp.concatenate([zeros_b, blk(vb, i, p)], axis=1)], axis=0),
                  jnp.concatenate([blk(b_tb, i, p), blk(k_tb, i, p)], axis=0))
          for s, (i, p) in enumerate(streams)]
    p_end = [e_in[(i + 1) * L - 1:(i + 1) * L, p * PAIR:(p + 1) * PAIR] for i, p in streams]
    m_t = [(jnp.where(same_head, mg[s][0:PAIR], 0.0) * p_end[s]).astype(BF16) for s in range(ns)]
    g_t = [jnp.where(lane_lo, mg[s][PAIR:PAIR + RWKV_HEAD], mg[s][PAIR + RWKV_HEAD:2 * PAIR]) * p_end[s]
           for s in range(ns)]

    y_rows = []
    for i in range(G):
        y_pairs = []
        for p in range(RWKV_PAIRS):
            s = i * RWKV_PAIRS + p
            s0 = s_scr[p]
            s0b = s0.astype(BF16)
            y_pairs.append(_dot_nt(q[s], _bd(s0b)) + y_loc[s])
            s_scr[p] = s0 * p_end[s] + _dot(s0b, m_t[s]) + g_t[s]
        y_rows.append(jnp.concatenate(y_pairs, axis=-1))
    y = jnp.concatenate(y_rows, axis=0)
    y_ref[...] = _rwkv_finish(y, r, k, v, g, rk_ref, lng_ref, lnb_ref)

    @pl.when(c == pl.num_programs(1) - 1)
    def _():
        sout_ref[0] = s_scr[...]
        shift_ref[0] = upad_scr[pad - 1:pad, :]


_RWKV_PARAM_NAMES = ("mu", "w0", "w2", "a0", "a2", "g2", "k_k", "k_a", "r_k", "ln_g", "ln_b")


def _rwkv(u, p, *, batch, seq):
    rows = RWKV_CHUNK * RWKV_GROUP
    nc = seq // rows
    params = [p[n] for n in _RWKV_PARAM_NAMES]
    sspec = pl.BlockSpec((1, RWKV_PAIRS, RWKV_HEAD, PAIR), lambda b, c: (b, 0, 0, 0))
    y, shift, s_last = pl.pallas_call(
        functools.partial(_rwkv_body, chunk=RWKV_CHUNK, group=RWKV_GROUP),
        grid=(batch, nc),
        in_specs=[pl.BlockSpec((rows, RWKV_PROJ), lambda b, c: (b * nc + c, 0))] + [_full_spec(a.shape) for a in params],
        out_specs=[pl.BlockSpec((rows, GROUP_WIDTH), lambda b, c: (b * nc + c, 0)),
                   pl.BlockSpec((1, 1, RWKV_PROJ), lambda b, c: (b, 0, 0)), sspec],
        out_shape=[jax.ShapeDtypeStruct((batch * seq, GROUP_WIDTH), F32),
                   jax.ShapeDtypeStruct((batch, 1, RWKV_PROJ), F32),
                   jax.ShapeDtypeStruct((batch, RWKV_PAIRS, RWKV_HEAD, PAIR), F32)],
        scratch_shapes=[pltpu.VMEM((SUBLANES + rows, RWKV_PROJ), F32),
                        pltpu.VMEM((RWKV_PAIRS, RWKV_HEAD, PAIR), F32)],
        compiler_params=_cparams("parallel", "arbitrary"),
        name="rwkv",
    )(u, *params)
    s_last = s_last.reshape(batch, RWKV_PAIRS, RWKV_HEAD, 2, RWKV_HEAD).transpose(0, 1, 3, 2, 4).reshape(
        batch, RWKV_HEADS, RWKV_HEAD, RWKV_HEAD)
    return y, shift.reshape(batch, RWKV_PROJ), s_last


def _rwkv_step_body(u_ref, shift0_ref, s0_ref, *rest, seq, batch, layer):
    sdone_ref, rest = (rest[0], rest[1:]) if layer else (None, rest)
    (mu_ref, w0_ref, w2_ref, a0_ref, a2_ref, g2_ref, kk_ref, ka_ref, rk_ref, lng_ref, lnb_ref, y_ref, sout_ref,
     r_scr, w_scr, k_scr, b_scr, nkk_scr, v_scr, y_scr) = rest
    T, B = seq, batch
    j = pl.program_id(0)
    if layer:
        sout_ref[0:layer] = sdone_ref[...]
    tiles = RWKV_STEP_TILES

    def pointwise(t):
        u = u_ref[t * B:(t + 1) * B, :]
        prev = shift0_ref[...] if t == 0 else u_ref[(t - 1) * B:t * B, :]
        return _rwkv_pointwise(u, prev, mu_ref, w0_ref, w2_ref, a0_ref, a2_ref, g2_ref, kk_ref, ka_ref)

    @pl.when(j == 0)
    def _():
        for t in range(T):
            r, k, v, logdecay, a, _, kk = pointwise(t)
            r_scr[t] = r.T
            w_scr[t] = jnp.exp(logdecay).T
            k_scr[t] = k.T
            b_scr[t] = (kk * a).T
            nkk_scr[t] = (-kk).T
            v_scr[t] = v.T

    i0 = j * tiles
    keys = pl.ds(pl.multiple_of((i0 // RWKV_HEAD) * RWKV_HEAD, RWKV_HEAD), RWKV_HEAD)
    for q in range(tiles):
        vi = pl.ds(i0 + q, 1)
        s = s0_ref[:, q, :].T
        for t in range(T):
            sa = jnp.sum(s * nkk_scr[t, keys, :], axis=0, keepdims=True)
            s = s * w_scr[t, keys, :] + k_scr[t, keys, :] * v_scr[t, vi, :] + b_scr[t, keys, :] * sa
            y_scr[t, vi, :] = jnp.sum(s * r_scr[t, keys, :], axis=0, keepdims=True)
        sout_ref[layer, :, q, :] = s.T

    @pl.when(j == pl.num_programs(0) - 1)
    def _():
        for t in range(T):
            r, k, v, _, _, g, _ = pointwise(t)
            y_ref[t * B:(t + 1) * B, :] = _rwkv_finish(y_scr[t].T, r, k, v, g, rk_ref, lng_ref, lnb_ref)


def _rwkv_step(u, shift0, s_all, s_done, p, *, batch, seq, layer):
    n = batch * seq
    srows = RWKV_HEADS * RWKV_HEAD
    params = [p[nm] for nm in _RWKV_PARAM_NAMES]
    sspec, prev_specs, sout_spec = _layer_state_specs(layer, batch, RWKV_STEP_TILES, RWKV_HEAD)
    prev_args = [s_done] if layer else []
    tposed = pltpu.VMEM((seq, GROUP_WIDTH, batch), F32)
    return pl.pallas_call(
        functools.partial(_rwkv_step_body, seq=seq, batch=batch, layer=layer),
        grid=(srows // RWKV_STEP_TILES,),
        in_specs=[_full_spec((n, RWKV_PROJ)), _full_spec((batch, RWKV_PROJ)), sspec] + prev_specs
                 + [_full_spec(a.shape) for a in params],
        out_specs=[_full_spec((n, GROUP_WIDTH)), sout_spec],
        out_shape=[jax.ShapeDtypeStruct((n, GROUP_WIDTH), F32),
                   jax.ShapeDtypeStruct((layer + 1, batch, srows, RWKV_HEAD), F32)],
        scratch_shapes=[tposed] * 7,
        compiler_params=_cparams("arbitrary"),
        name="rwkv_step",
    )(u, shift0, s_all.reshape(s_all.shape[0], batch, srows, RWKV_HEAD), *prev_args, *params)


def _s5_body(u_ref, hre0_ref, him0_ref, are_ref, aim_ref, bmat_ref, cmat_ref, d_ref, gw_ref, gb_ref,
             y_ref, hre_ref, him_ref, hs_scr, *, steps):
    c = pl.program_id(1)
    ns = S5_WIDTH
    bsub = u_ref.shape[1]

    @pl.when(c == 0)
    def _():
        hre_ref[...] = hre0_ref[...]
        him_ref[...] = him0_ref[...]

    u = u_ref[...].reshape(steps * bsub, GROUP_WIDTH)
    hs_scr[...] = _dot(u.astype(BF16), bmat_ref[...])
    are = jnp.broadcast_to(are_ref[...], (bsub, ns))
    aim = jnp.broadcast_to(aim_ref[...], (bsub, ns))

    def step(t, carry):
        hre, him = carry
        r0 = pl.multiple_of(t * bsub, bsub)
        nre = are * hre - aim * him + hs_scr[pl.ds(r0, bsub), 0:ns]
        nim = are * him + aim * hre + hs_scr[pl.ds(r0, bsub), ns:2 * ns]
        hs_scr[pl.ds(r0, bsub), 0:ns] = nre
        hs_scr[pl.ds(r0, bsub), ns:2 * ns] = nim
        return nre, nim

    hre, him = lax.fori_loop(0, steps, step, (hre_ref[...], him_ref[...]))
    hre_ref[...] = hre
    him_ref[...] = him
    y = _dot(hs_scr[...].astype(BF16), cmat_ref[...]) + u * d_ref[...]
    y = _gelu_tanh(y)
    yy = _dot(y.astype(BF16), gw_ref[...]) + gb_ref[...]
    out = yy[:, 0:GROUP_WIDTH] * _sigmoid(yy[:, GROUP_WIDTH:2 * GROUP_WIDTH])
    y_ref[...] = out.reshape(steps, bsub, GROUP_WIDTH)


def _s5(u_tm, hre0, him0, lp):
    seq, batch, _ = u_tm.shape
    steps = min(TM_CHUNK, seq)
    bsub = SUBLANES
    hspec = pl.BlockSpec((bsub, S5_WIDTH), lambda b, c: (b, 0))
    tspec = pl.BlockSpec((steps, bsub, GROUP_WIDTH), lambda b, c: (c, b, 0))
    consts = (lp["s5_are"], lp["s5_aim"], lp["s5_bmat"], lp["s5_cmat"], lp["s5_d"], lp["s5_gw"], lp["s5_gb"])
    return pl.pallas_call(
        functools.partial(_s5_body, steps=steps),
        grid=(batch // bsub, seq // steps),
        in_specs=[tspec, hspec, hspec] + [_full_spec(a.shape) for a in consts],
        out_specs=[tspec, hspec, hspec],
        out_shape=[jax.ShapeDtypeStruct((seq, batch, GROUP_WIDTH), F32),
                   jax.ShapeDtypeStruct((batch, S5_WIDTH), F32),
                   jax.ShapeDtypeStruct((batch, S5_WIDTH), F32)],
        scratch_shapes=[pltpu.VMEM((steps * bsub, 2 * S5_WIDTH), F32)],
        compiler_params=_cparams("parallel", "arbitrary"),
        name="s5",
    )(u_tm, hre0, him0, *consts)


def _pool_body(u_ref, buf0_ref, pw_ref, sc_ref, y_ref, buf_ref, f_scr, *, steps, pos0):
    c = pl.program_id(1)
    bsub = u_ref.shape[1]
    GW = GROUP_WIDTH
    halo = POOL_BUF + 1

    @pl.when(c == 0)
    def _():
        f_scr[0] = jnp.zeros((bsub, GW), F32)
        for i in range(POOL_BUF):
            f_scr[1 + i] = buf0_ref[:, i * GW:(i + 1) * GW]

    u = u_ref[...]
    f_scr[halo:halo + steps] = u
    f = f_scr[...]
    s2 = f[1:] + f[:-1]
    s4 = s2[2:] + s2[:-2]
    s8 = s4[4:] + s4[:-4]
    s16 = s8[8:] + s8[:-8]
    f_scr[0:halo] = f[steps:steps + halo]
    lane = lax.broadcasted_iota(jnp.int32, (steps, bsub, GW), 2)
    tpos = lax.broadcasted_iota(jnp.int32, (steps, bsub, GW), 0) + (pos0 + 1) + c * steps
    win = jnp.where(lane < POOL_CH, s2[halo - 1:halo - 1 + steps],
                    jnp.where(lane < 2 * POOL_CH, s4[halo - 3:halo - 3 + steps],
                              jnp.where(lane < 3 * POOL_CH, s8[halo - 7:halo - 7 + steps],
                                        s16[halo - 15:halo - 15 + steps])))
    wlen = jnp.where(lane < POOL_CH, POOL_WINDOWS[0],
                     jnp.where(lane < 2 * POOL_CH, POOL_WINDOWS[1],
                               jnp.where(lane < 3 * POOL_CH, POOL_WINDOWS[2], POOL_WINDOWS[3])))
    cnt = jnp.minimum(tpos, wlen).astype(F32)
    pooled = (win / cnt - u).reshape(steps * bsub, GW)
    y = _dot(pooled.astype(BF16), pw_ref[...]) * sc_ref[...]
    y_ref[...] = y.reshape(steps, bsub, GW)

    @pl.when(c == pl.num_programs(1) - 1)
    def _():
        for i in range(POOL_BUF):
            buf_ref[:, i * GW:(i + 1) * GW] = f_scr[1 + i]


def _pool(u_tm, buf0, lp, *, pos0):
    seq, batch, _ = u_tm.shape
    steps = min(TM_CHUNK, seq)
    bsub = SUBLANES
    flat = POOL_BUF * GROUP_WIDTH
    tspec = pl.BlockSpec((steps, bsub, GROUP_WIDTH), lambda b, c: (c, b, 0))
    bspec = pl.BlockSpec((bsub, flat), lambda b, c: (b, 0))
    y, buf = pl.pallas_call(
        functools.partial(_pool_body, steps=steps, pos0=pos0),
        grid=(batch // bsub, seq // steps),
        in_specs=[tspec, bspec, _full_spec(lp["pool_w"].shape), _full_spec(lp["pool_scale"].shape)],
        out_specs=[tspec, bspec],
        out_shape=[jax.ShapeDtypeStruct((seq, batch, GROUP_WIDTH), F32), jax.ShapeDtypeStruct((batch, flat), F32)],
        scratch_shapes=[pltpu.VMEM((POOL_BUF + 1 + steps, bsub, GROUP_WIDTH), F32)],
        compiler_params=_cparams("parallel", "arbitrary"),
        name="pool",
    )(u_tm, buf0.reshape(batch, flat), lp["pool_w"], lp["pool_scale"])
    return y, buf.reshape(batch, POOL_BUF, GROUP_WIDTH)


def _block_diag(blocks):
    g, r, c = blocks.shape
    eye = jnp.eye(g, dtype=blocks.dtype)
    return (eye[:, None, :, None] * blocks[:, :, None, :]).reshape(g * r, g * c)


def _pad_lanes(v, n=LANES):
    return jnp.pad(v, (0, n - v.shape[0])).reshape(1, n)


def _layer_params(l, P):
    row = lambda a: a.reshape(1, -1)
    w_in = P["w_in"][l]
    z_w, xbc_w, dt_w, rw_w, s5_w, pool_w = jnp.split(
        w_in, [256, 256 + 768, 256 + 768 + 4, 1028 + 1024, 1028 + 1024 + 256], axis=1)
    w_all = jnp.concatenate([z_w, xbc_w, rw_w, s5_w, pool_w, jnp.pad(dt_w, ((0, 0), (0, LANES - SSD_HEADS)))], axis=1)

    lam = lax.complex(P["s5_lam_re"][l], P["s5_lam_im"][l])
    a_bar = jnp.exp(lam * jnp.exp(P["s5_log_step"][l])[:, None])
    b_bar = ((a_bar - 1.0) / lam)[..., None] * lax.complex(P["s5_b_re"][l], P["s5_b_im"][l])
    b_t = jnp.swapaxes(b_bar, 1, 2)
    bmat = jnp.concatenate([_block_diag(jnp.real(b_t)), _block_diag(jnp.imag(b_t))], axis=1)
    c_t = jnp.swapaxes(lax.complex(P["s5_c_re"][l], P["s5_c_im"][l]), 1, 2)
    cmat = jnp.concatenate([_block_diag(jnp.real(c_t)), -_block_diag(jnp.imag(c_t))], axis=0)
    head_expand = jnp.pad(jnp.repeat(jnp.eye(SSD_HEADS, dtype=F32), SSD_HEAD_DIM, axis=1),
                          ((0, LANES - SSD_HEADS), (0, 0)))

    return dict(
        norm_ffn1=row(P["norm_ffn1"][l]), ffn1_in=P["ffn1_in"][l].astype(BF16), ffn1_out=P["ffn1_out"][l].astype(BF16),
        norm_mix=row(P["norm_mix"][l]), w_all=w_all.astype(BF16),
        conv_w=P["ssd_conv_w"][l], conv_b=row(P["ssd_conv_b"][l]),
        dt_bias=_pad_lanes(P["ssd_dt_bias"][l]), a_log=_pad_lanes(P["ssd_a_log"][l]),
        a_neg_exp=row(jnp.repeat(-jnp.exp(P["ssd_a_log"][l]), SSD_HEAD_DIM)), head_expand=head_expand,
        d_skip=row(jnp.repeat(P["ssd_d"][l], SSD_HEAD_DIM)), ssd_norm=row(P["ssd_norm"][l]),
        rwkv=dict(mu=row(P["rwkv_mu"][l]), w0=row(P["rwkv_w0"][l]), w2=P["rwkv_w2"][l].astype(BF16),
                  a0=row(P["rwkv_a0"][l]), a2=P["rwkv_a2"][l].astype(BF16), g2=P["rwkv_g2"][l].astype(BF16),
                  k_k=row(P["rwkv_k_k"][l]), k_a=row(P["rwkv_k_a"][l]), r_k=row(P["rwkv_r_k"][l]),
                  ln_g=row(P["rwkv_ln_g"][l]), ln_b=row(P["rwkv_ln_b"][l])),
        s5_are=row(jnp.real(a_bar)), s5_aim=row(jnp.imag(a_bar)), s5_bmat=bmat.astype(BF16), s5_cmat=cmat.astype(BF16),
        s5_d=row(P["s5_d"][l]), s5_gw=P["s5_glu_w"][l].astype(BF16), s5_gb=row(P["s5_glu_b"][l]),
        pool_w=_block_diag(P["pool_w"][l]).astype(BF16), pool_scale=row(P["pool_scale"][l]),
        w_out=P["w_out"][l].astype(BF16),
        norm_ffn2=row(P["norm_ffn2"][l]), ffn2_in=P["ffn2_in"][l].astype(BF16), ffn2_out=P["ffn2_out"][l].astype(BF16),
    )


def _to_tm(rows, batch, seq):
    return jnp.swapaxes(rows.reshape(batch, seq, rows.shape[-1]), 0, 1)


def _from_tm(x_tm):
    seq, batch, w = x_tm.shape
    return jnp.swapaxes(x_tm, 0, 1).reshape(batch * seq, w)


def _mixers_prompt(lp, proj, *, batch, seq):
    z, xbc, ur, us5, upool, dtr = proj
    y_ssd, conv_new, ssd_new = _ssd(z, xbc, dtr, lp, batch=batch, seq=seq)
    y_rwkv, shift_new, rwkv_new = _rwkv(ur, lp["rwkv"], batch=batch, seq=seq)
    zeros = jnp.zeros((batch, S5_WIDTH), F32)
    y_s5, s5re, s5im = _s5(_to_tm(us5, batch, seq), zeros, zeros, lp)
    y_pool, pool_new = _pool(_to_tm(upool, batch, seq), jnp.zeros((batch, POOL_BUF, GROUP_WIDTH), F32), lp, pos0=0)
    ys = (y_ssd, y_rwkv, _from_tm(y_s5), _from_tm(y_pool))
    states = (conv_new, ssd_new, shift_new, rwkv_new, s5re.reshape(batch, S5_GROUPS, S5_STATE),
              s5im.reshape(batch, S5_GROUPS, S5_STATE), pool_new)
    return ys, states


def _mixers_decode(lp, proj, states, done, *, batch, seq, layer):
    z, xbc, ur, us5, upool, dtr = proj
    conv0, shift0, s5re0, s5im0, pool0 = (states[i][layer] for i in (0, 2, 4, 5, 6))
    ssd_done, rwkv_done = (done[1], done[3]) if layer else (None, None)
    y_ssd, conv_new, ssd_new = _ssd_step(z, xbc, dtr, conv0, states[1], ssd_done, lp, batch=batch, seq=seq,
                                         layer=layer)
    y_rwkv, rwkv_new = _rwkv_step(ur, shift0, states[3], rwkv_done, lp["rwkv"], batch=batch, seq=seq, layer=layer)
    shift_new = ur[(seq - 1) * batch:, :]
    tm = lambda a: a.reshape(seq, batch, a.shape[-1])
    y_s5, s5re, s5im = _s5(tm(us5), s5re0.reshape(batch, S5_WIDTH), s5im0.reshape(batch, S5_WIDTH), lp)
    y_pool, pool_new = _pool(tm(upool), pool0, lp, pos0=PAST_LEN)
    rows = lambda a: a.reshape(seq * batch, a.shape[-1])
    ys = (y_ssd, y_rwkv, rows(y_s5), rows(y_pool))
    new_states = (conv_new, ssd_new, shift_new, rwkv_new, s5re.reshape(batch, S5_GROUPS, S5_STATE),
                  s5im.reshape(batch, S5_GROUPS, S5_STATE), pool_new)
    return ys, new_states


_WIDTHS = (GROUP_WIDTH, SSD_CONV_DIM, RWKV_PROJ, GROUP_WIDTH, GROUP_WIDTH, LANES)


def _trunk(x, layer_params, norm_final, mixers):
    states = []
    mix, lp = None, None
    for l, lp_next in enumerate(layer_params):
        if l > 0:
            x = _ffn(x, lp["norm_ffn2"], lp["ffn2_in"], lp["ffn2_out"], mix=mix, wmix=lp["w_out"])
        lp = lp_next
        x = _ffn(x, lp["norm_ffn1"], lp["ffn1_in"], lp["ffn1_out"])
        proj = _inproj(x, lp["norm_mix"], lp["w_all"], _WIDTHS)
        mix, st = mixers(l, lp, proj, states[-1] if states else None)
        states.append(st)
    x = _ffn(x, lp["norm_ffn2"], lp["ffn2_in"], lp["ffn2_out"], mix=mix, wmix=lp["w_out"], gf=norm_final)
    return x, states


def kernel(x_prompt, x_sample, state_ssd_conv, state_ssd, state_rwkv_shift, state_rwkv, state_s5_re, state_s5_im, state_pool, norm_ffn1, ffn1_in, ffn1_out, norm_mix, w_in, ssd_conv_w, ssd_conv_b, ssd_dt_bias, ssd_a_log, ssd_d, ssd_norm, rwkv_mu, rwkv_w0, rwkv_w2, rwkv_a0, rwkv_a2, rwkv_g2, rwkv_k_k, rwkv_k_a, rwkv_r_k, rwkv_ln_g, rwkv_ln_b, s5_lam_re, s5_lam_im, s5_log_step, s5_b_re, s5_b_im, s5_c_re, s5_c_im, s5_d, s5_glu_w, s5_glu_b, pool_w, pool_scale, w_out, norm_ffn2, ffn2_in, ffn2_out, norm_final):
    P = dict(norm_ffn1=norm_ffn1, ffn1_in=ffn1_in, ffn1_out=ffn1_out, norm_mix=norm_mix, w_in=w_in,
             ssd_conv_w=ssd_conv_w, ssd_conv_b=ssd_conv_b, ssd_dt_bias=ssd_dt_bias, ssd_a_log=ssd_a_log,
             ssd_d=ssd_d, ssd_norm=ssd_norm, rwkv_mu=rwkv_mu, rwkv_w0=rwkv_w0, rwkv_w2=rwkv_w2, rwkv_a0=rwkv_a0,
             rwkv_a2=rwkv_a2, rwkv_g2=rwkv_g2, rwkv_k_k=rwkv_k_k, rwkv_k_a=rwkv_k_a,
             rwkv_r_k=rwkv_r_k.reshape(rwkv_r_k.shape[0], -1), rwkv_ln_g=rwkv_ln_g, rwkv_ln_b=rwkv_ln_b,
             s5_lam_re=s5_lam_re, s5_lam_im=s5_lam_im, s5_log_step=s5_log_step, s5_b_re=s5_b_re, s5_b_im=s5_b_im,
             s5_c_re=s5_c_re, s5_c_im=s5_c_im, s5_d=s5_d, s5_glu_w=s5_glu_w, s5_glu_b=s5_glu_b, pool_w=pool_w,
             pool_scale=pool_scale, w_out=w_out, norm_ffn2=norm_ffn2, ffn2_in=ffn2_in, ffn2_out=ffn2_out)
    depth = norm_ffn1.shape[0]
    bp, tp, d = x_prompt.shape
    bs, ts, _ = x_sample.shape
    layer_params = [_layer_params(l, P) for l in range(depth)]
    gf = norm_final.reshape(1, -1)
    sample_states = (state_ssd_conv, state_ssd, state_rwkv_shift, state_rwkv, state_s5_re, state_s5_im, state_pool)

    y_p, st_p = _trunk(x_prompt.reshape(bp * tp, d), layer_params, gf,
                       lambda l, lp, proj, done: _mixers_prompt(lp, proj, batch=bp, seq=tp))
    x_s = jnp.swapaxes(x_sample, 0, 1).reshape(ts * bs, d)
    y_s, st_s = _trunk(x_s, layer_params, gf,
                       lambda l, lp, proj, done: _mixers_decode(lp, proj, sample_states, done,
                                                                batch=bs, seq=ts, layer=l))
    outs = [y_p.reshape(bp, tp, d), jnp.swapaxes(y_s.reshape(ts, bs, d), 0, 1)]
    for i, ref_state in enumerate(sample_states):
        outs.append(jnp.stack([st[i] for st in st_p]))
        if i in (1, 3):
            outs.append(st_s[-1][i].reshape(ref_state.shape))
        else:
            outs.append(jnp.stack([st[i] for st in st_s]))
    return tuple(outs)
```

```python
import functools
import math

import jax
import jax.numpy as jnp
from jax import lax
from jax.experimental import pallas as pl
from jax.experimental.pallas import tpu as pltpu

F32 = jnp.float32
BF16 = jnp.bfloat16
HIGHEST = lax.Precision.HIGHEST

SUBLANES = 8
LANES = 128
VMEM_LIMIT_BYTES = 56 * 1024 * 1024

GROUP_WIDTH = 256
SSD_HEAD_DIM = 64
SSD_HEADS = 4
SSD_GROUPS = 2
SSD_STATE = 128
SSD_CONV = 4
SSD_CONV_DIM = GROUP_WIDTH + 2 * SSD_GROUPS * SSD_STATE
SSD_CHUNK = 128
RWKV_HEAD = 64
RWKV_HEADS = 4
RWKV_PROJ = 1024
RWKV_LN_EPS = 64e-5
RWKV_CHUNK = 64
RWKV_GROUP = 8
S5_GROUP_CH = 16
S5_GROUPS = 16
S5_STATE = 64
S5_WIDTH = S5_GROUPS * S5_STATE
POOL_WINDOWS = (2, 4, 8, 16)
POOL_CH = 64
POOL_BUF = 15
RMS_EPS = 1e-6
PAST_LEN = 16384

ROW_TILE = 512
FFN_CHUNK = 256
TM_CHUNK = 64
SSD_STEP_TILES = 16
RWKV_STEP_TILES = 16


def _cparams(*sem):
    return pltpu.CompilerParams(dimension_semantics=sem, vmem_limit_bytes=VMEM_LIMIT_BYTES)


def _dot(a, b, **kw):
    return jnp.dot(a, b, preferred_element_type=F32, **kw)


def _dot_nt(a, b):
    return lax.dot_general(a, b, (((1,), (1,)), ((), ())), preferred_element_type=F32)


def _dot_tn(a, b):
    return lax.dot_general(a, b, (((0,), (0,)), ((), ())), preferred_element_type=F32)


def _sigmoid(x):
    return 1.0 / (1.0 + jnp.exp(-x))


def _silu(x):
    return x * _sigmoid(x)


def _softplus(x):
    return jnp.maximum(x, 0.0) + jnp.log(1.0 + jnp.exp(-jnp.abs(x)))


def _gelu_tanh(x):
    c = math.sqrt(2.0 / math.pi)
    return x * (0.5 * (1.0 + jnp.tanh(c * (x + 0.044715 * (x * x * x)))))


def _rms(x, g):
    return x * lax.rsqrt(jnp.mean(x * x, axis=-1, keepdims=True) + RMS_EPS) * g


def _full_spec(shape):
    n = len(shape)
    return pl.BlockSpec(shape, lambda *_: (0,) * n)


class _Layered(tuple):
    pass


def _pspec(p):
    if isinstance(p, _Layered):
        a, l = p
        return pl.BlockSpec((None,) + a.shape[1:], lambda *_: (l,) + (0,) * (a.ndim - 1))
    return _full_spec(p.shape)


def _parg(p):
    return p[0] if isinstance(p, _Layered) else p


def _ffn_body(*refs, has_mix, final_norm):
    it = iter(refs)
    x_ref = next(it)
    x = x_ref[...]
    if has_mix:
        y_refs = [next(it) for _ in range(4)]
        wmix_ref = next(it)
        for i, y_ref in enumerate(y_refs):
            x = x + _dot(y_ref[...].astype(BF16), wmix_ref[i * GROUP_WIDTH:(i + 1) * GROUP_WIDTH, :])
    g_ref, wi_ref, wo_ref = next(it), next(it), next(it)
    gf_ref = next(it) if final_norm else None
    o_ref = next(it)
    h = _rms(x, g_ref[...]).astype(BF16)
    d_ff = wo_ref.shape[0]
    acc = jnp.zeros_like(x)
    for c in range(d_ff // FFN_CHUNK):
        lo = c * FFN_CHUNK
        gate = _dot(h, wi_ref[:, lo:lo + FFN_CHUNK])
        up = _dot(h, wi_ref[:, d_ff + lo:d_ff + lo + FFN_CHUNK])
        act = (_silu(gate) * up).astype(BF16)
        acc = acc + _dot(act, wo_ref[lo:lo + FFN_CHUNK, :])
    x = x + 0.5 * acc
    if final_norm:
        x = _rms(x, gf_ref[...])
    o_ref[...] = x


def _ffn(x, g, wi, wo, mix=None, wmix=None, gf=None):
    rows, d = x.shape
    row_spec = lambda w: pl.BlockSpec((ROW_TILE, w), lambda i: (i, 0))
    args, specs = [x], [row_spec(d)]
    if mix is not None:
        for y in mix:
            args.append(y)
            specs.append(row_spec(y.shape[1]))
        args.append(_parg(wmix))
        specs.append(_pspec(wmix))
    for a in (g, wi, wo) + ((gf,) if gf is not None else ()):
        args.append(_parg(a))
        specs.append(_pspec(a))
    return pl.pallas_call(
        functools.partial(_ffn_body, has_mix=mix is not None, final_norm=gf is not None),
        grid=(rows // ROW_TILE,),
        in_specs=specs,
        out_specs=row_spec(d),
        out_shape=jax.ShapeDtypeStruct((rows, d), F32),
        compiler_params=_cparams("parallel"),
        name="ffn",
    )(*args)


def _inproj_body(x_ref, g_ref, w_ref, *o_refs):
    h = _rms(x_ref[...], g_ref[...]).astype(BF16)
    off = 0
    for o_ref in o_refs:
        n = o_ref.shape[-1]
        o_ref[...] = _dot(h, w_ref[:, off:off + n])
        off += n


def _inproj(x, g, w, widths):
    rows, d = x.shape
    row_spec = lambda w_: pl.BlockSpec((ROW_TILE, w_), lambda i: (i, 0))
    return pl.pallas_call(
        _inproj_body,
        grid=(rows // ROW_TILE,),
        in_specs=[row_spec(d), _pspec(g), _pspec(w)],
        out_specs=[row_spec(n) for n in widths],
        out_shape=[jax.ShapeDtypeStruct((rows, n), F32) for n in widths],
        compiler_params=_cparams("parallel"),
        name="inproj",
    )(x, _parg(g), _parg(w))


def _ssd_body(z_ref, xbc_ref, dt_ref, cw_ref, cb_ref, dtb_ref, alog_ref, dsk_ref, ng_ref,
              y_ref, conv_ref, hout_ref, xpad_scr, h_scr, *, chunk):
    L = chunk
    c = pl.program_id(1)
    pad = SUBLANES
    halo = SSD_CONV - 1

    @pl.when(c == 0)
    def _():
        xpad_scr[0:pad, :] = jnp.zeros((pad, SSD_CONV_DIM), F32)
        h_scr[...] = jnp.zeros(h_scr.shape, F32)

    xpad_scr[pad:pad + L, :] = xbc_ref[...]
    conv = cb_ref[...] + cw_ref[0:1, :] * xpad_scr[pad - halo:pad - halo + L, :]
    for j in range(1, SSD_CONV):
        conv = conv + cw_ref[j:j + 1, :] * xpad_scr[pad - halo + j:pad - halo + j + L, :]
    xpad_scr[pad - halo:pad, :] = xpad_scr[pad + L - halo:pad + L, :]
    conv = _silu(conv)
    xs = conv[:, 0:GROUP_WIDTH]
    bm = conv[:, GROUP_WIDTH:2 * GROUP_WIDTH].astype(BF16)
    cm = conv[:, 2 * GROUP_WIDTH:3 * GROUP_WIDTH].astype(BF16)

    row = lax.broadcasted_iota(jnp.int32, (L, L), 0)
    col = lax.broadcasted_iota(jnp.int32, (L, L), 1)
    causal = row >= col
    dt = _softplus(dt_ref[...] + dtb_ref[...])
    da = dt * (-jnp.exp(alog_ref[...]))
    acs = _dot(jnp.where(causal, 1.0, 0.0).astype(F32), da, precision=HIGHEST)
    acs_t = acs.T
    e_acs = jnp.exp(acs)
    acs_last = acs[L - 1:L, :]
    e_end = jnp.exp(acs_last - acs)
    e_last = jnp.exp(acs_last)

    ys = []
    for h in range(SSD_HEADS):
        g = h // (SSD_HEADS // SSD_GROUPS)
        bg = bm[:, g * SSD_STATE:(g + 1) * SSD_STATE]
        cg = cm[:, g * SSD_STATE:(g + 1) * SSD_STATE]
        x_h = xs[:, h * SSD_HEAD_DIM:(h + 1) * SSD_HEAD_DIM]
        xdt = x_h * dt[:, h:h + 1]
        seg = acs[:, h:h + 1] - acs_t[h:h + 1, :]
        decay = jnp.exp(jnp.where(causal, seg, -jnp.inf))
        scores = _dot_nt(cg, bg) * decay
        y_h = _dot(scores.astype(BF16), xdt.astype(BF16))
        h_prev = h_scr[h]
        y_h = y_h + _dot_nt(cg, h_prev.astype(BF16)) * e_acs[:, h:h + 1]
        st = _dot_tn((xdt * e_end[:, h:h + 1]).astype(BF16), bg)
        h_scr[h] = h_prev * e_last[:, h:h + 1] + st
        ys.append(y_h)
    y = jnp.concatenate(ys, axis=-1) + xs * dsk_ref[...]
    y = y * _silu(z_ref[...])
    y_ref[...] = _rms(y, ng_ref[...])

    @pl.when(c == pl.num_programs(1) - 1)
    def _():
        hout_ref[0] = h_scr[...]
        conv_ref[0] = xpad_scr[pad - halo:pad, :]


def _ssd(z, xbc, dtr, lp, *, batch, seq):
    chunk = SSD_CHUNK
    nc = seq // chunk
    rspec = lambda w: pl.BlockSpec((chunk, w), lambda b, c: (b * nc + c, 0))
    consts = (lp["conv_w"], lp["conv_b"], lp["dt_bias"], lp["a_log"], lp["d_skip"], lp["ssd_norm"])
    return pl.pallas_call(
        functools.partial(_ssd_body, chunk=chunk),
        grid=(batch, nc),
        in_specs=[rspec(GROUP_WIDTH), rspec(SSD_CONV_DIM), rspec(LANES)] + [_pspec(a) for a in consts],
        out_specs=[rspec(GROUP_WIDTH),
                   pl.BlockSpec((1, SSD_CONV - 1, SSD_CONV_DIM), lambda b, c: (b, 0, 0)),
                   pl.BlockSpec((1, SSD_HEADS, SSD_HEAD_DIM, SSD_STATE), lambda b, c: (b, 0, 0, 0))],
        out_shape=[jax.ShapeDtypeStruct((batch * seq, GROUP_WIDTH), F32),
                   jax.ShapeDtypeStruct((batch, SSD_CONV - 1, SSD_CONV_DIM), F32),
                   jax.ShapeDtypeStruct((batch, SSD_HEADS, SSD_HEAD_DIM, SSD_STATE), F32)],
        scratch_shapes=[pltpu.VMEM((SUBLANES + chunk, SSD_CONV_DIM), F32),
                        pltpu.VMEM((SSD_HEADS, SSD_HEAD_DIM, SSD_STATE), F32)],
        compiler_params=_cparams("parallel", "arbitrary"),
        name="ssd",
    )(z, xbc, dtr, *[_parg(a) for a in consts])


def _ssd_step_body(z_ref, xbc_ref, dt_ref, conv0_ref, h0_ref, *rest, seq, batch, layer):
    hdone_ref, rest = (rest[0], rest[1:]) if layer else (None, rest)
    (cw_ref, cb_ref, dtb_ref, aneg_ref, dsk_ref, ng_ref, hexp_ref, y_ref, conv_ref, hout_ref,
     xs_scr, bm_scr, cm_scr, xdt_scr, dec_scr, y_scr) = rest
    T, B = seq, batch
    if layer:
        hout_ref[0:layer] = hdone_ref[...]
    GW = GROUP_WIDTH
    j = pl.program_id(0)
    tiles = SSD_STEP_TILES

    @pl.when(j == 0)
    def _():
        rows = [conv0_ref[:, i * SSD_CONV_DIM:(i + 1) * SSD_CONV_DIM] for i in range(SSD_CONV - 1)]
        rows += [xbc_ref[t * B:(t + 1) * B, :] for t in range(T)]
        for t in range(T):
            conv = cb_ref[...] + cw_ref[0:1, :] * rows[t]
            for i in range(1, SSD_CONV):
                conv = conv + cw_ref[i:i + 1, :] * rows[t + i]
            conv = _silu(conv)
            xs = conv[:, 0:GW]
            xs_scr[t] = xs
            for g in range(SSD_GROUPS):
                bm_scr[t, g] = conv[:, GW + g * SSD_STATE:GW + (g + 1) * SSD_STATE].T
                cm_scr[t, g] = conv[:, 2 * GW + g * SSD_STATE:2 * GW + (g + 1) * SSD_STATE].T
            dt = _softplus(dt_ref[t * B:(t + 1) * B, :] + dtb_ref[...])
            dte = _dot(dt, hexp_ref[...], precision=HIGHEST)
            xdt_scr[t] = (xs * dte).T
            dec_scr[t] = jnp.exp(dte * aneg_ref[...]).T
        for i in range(SSD_CONV - 1):
            conv_ref[:, i * SSD_CONV_DIM:(i + 1) * SSD_CONV_DIM] = rows[T + i]

    hp0 = j * tiles
    grp = hp0 // (SSD_HEAD_DIM * (SSD_HEADS // SSD_GROUPS))
    for q in range(tiles):
        hp = pl.ds(hp0 + q, 1)
        h = h0_ref[:, q, :].T
        for t in range(T):
            h = h * dec_scr[t, hp, :] + bm_scr[t, grp] * xdt_scr[t, hp, :]
            y_scr[t, hp, :] = jnp.sum(h * cm_scr[t, grp], axis=0, keepdims=True)
        hout_ref[layer, :, q, :] = h.T

    @pl.when(j == pl.num_programs(0) - 1)
    def _():
        for t in range(T):
            y = y_scr[t].T + xs_scr[t] * dsk_ref[...]
            y = y * _silu(z_ref[t * B:(t + 1) * B, :])
            y_ref[t * B:(t + 1) * B, :] = _rms(y, ng_ref[...])


def _layer_state_specs(layer, batch, rows, width):
    cur = pl.BlockSpec((None, batch, rows, width), lambda j: (layer, 0, j, 0))
    prev = [pl.BlockSpec((layer, batch, rows, width), lambda j: (0, 0, j, 0))] if layer else []
    out = pl.BlockSpec((layer + 1, batch, rows, width), lambda j: (0, 0, j, 0))
    return cur, prev, out


def _ssd_step(z, xbc, dtr, conv0, h_all, h_done, lp, *, batch, seq, layer):
    n = batch * seq
    srows = SSD_HEADS * SSD_HEAD_DIM
    consts = (lp["conv_w"], lp["conv_b"], lp["dt_bias"], lp["a_neg_exp"], lp["d_skip"], lp["ssd_norm"], lp["head_expand"])
    hspec, prev_specs, hout_spec = _layer_state_specs(layer, batch, SSD_STEP_TILES, SSD_STATE)
    prev_args = [h_done] if layer else []
    cflat = (SSD_CONV - 1) * SSD_CONV_DIM
    y, conv_new, h_new = pl.pallas_call(
        functools.partial(_ssd_step_body, seq=seq, batch=batch, layer=layer),
        grid=(srows // SSD_STEP_TILES,),
        in_specs=[_full_spec((n, GROUP_WIDTH)), _full_spec((n, SSD_CONV_DIM)), _full_spec((n, LANES)),
                  _full_spec((batch, cflat)), hspec] + prev_specs + [_pspec(a) for a in consts],
        out_specs=[_full_spec((n, GROUP_WIDTH)), _full_spec((batch, cflat)), hout_spec],
        out_shape=[jax.ShapeDtypeStruct((n, GROUP_WIDTH), F32),
                   jax.ShapeDtypeStruct((batch, cflat), F32),
                   jax.ShapeDtypeStruct((layer + 1, batch, srows, SSD_STATE), F32)],
        scratch_shapes=[pltpu.VMEM((seq, batch, GROUP_WIDTH), F32),
                        pltpu.VMEM((seq, SSD_GROUPS, SSD_STATE, batch), F32),
                        pltpu.VMEM((seq, SSD_GROUPS, SSD_STATE, batch), F32),
                        pltpu.VMEM((seq, GROUP_WIDTH, batch), F32),
                        pltpu.VMEM((seq, GROUP_WIDTH, batch), F32),
                        pltpu.VMEM((seq, GROUP_WIDTH, batch), F32)],
        compiler_params=_cparams("arbitrary"),
        name="ssd_step",
    )(z, xbc, dtr, conv0.reshape(batch, cflat), h_all.reshape(h_all.shape[0], batch, srows, SSD_STATE),
      *prev_args, *[_parg(a) for a in consts])
    return y, conv_new.reshape(batch, SSD_CONV - 1, SSD_CONV_DIM), h_new


PAIR = 2 * RWKV_HEAD
RWKV_PAIRS = RWKV_HEADS // 2


def _bd(x):
    lane = lax.broadcasted_iota(jnp.int32, x.shape, 1)
    zero = jnp.zeros_like(x)
    return jnp.concatenate([jnp.where(lane < RWKV_HEAD, x, zero), jnp.where(lane >= RWKV_HEAD, x, zero)], axis=0)


def _half_sums(x, lo):
    s_lo = jnp.sum(jnp.where(lo, x, 0.0), axis=-1, keepdims=True)
    s_hi = jnp.sum(jnp.where(lo, 0.0, x), axis=-1, keepdims=True)
    return jnp.where(lo, s_lo, s_hi)


def _head_sum(x):
    lo = lax.broadcasted_iota(jnp.int32, (x.shape[0], PAIR), 1) < RWKV_HEAD
    return jnp.concatenate([_half_sums(x[:, p * PAIR:(p + 1) * PAIR], lo) for p in range(RWKV_PAIRS)], axis=-1)


def _rwkv_pointwise(u, prev, mu_ref, w0_ref, w2_ref, a0_ref, a2_ref, g2_ref, kk_ref, ka_ref):
    GW = GROUP_WIDTH
    xs = u + (prev - u) * mu_ref[...]
    r = xs[:, 0:GW]
    k = xs[:, GW:2 * GW]
    v = xs[:, 2 * GW:3 * GW]
    wd = xs[:, 3 * GW:3 * GW + 64]
    ad = xs[:, 3 * GW + 64:3 * GW + 128]
    gd = xs[:, 3 * GW + 128:3 * GW + 256]
    w_lin = w0_ref[...] + _dot(jnp.tanh(wd).astype(BF16), w2_ref[...])
    logdecay = -jnp.exp(-_softplus(-w_lin) - 0.5)
    a = _sigmoid(a0_ref[...] + _dot(ad.astype(BF16), a2_ref[...]))
    g = _dot(_sigmoid(gd).astype(BF16), g2_ref[...])
    kk = k * kk_ref[...]
    kk = kk / jnp.maximum(jnp.sqrt(_head_sum(kk * kk)), 1e-12)
    k = k * (1.0 + (a - 1.0) * ka_ref[...])
    return r, k, v, logdecay, a, g, kk


def _rwkv_finish(y, r, k, v, g, rk_ref, lng_ref, lnb_ref):
    mean = _head_sum(y) * (1.0 / RWKV_HEAD)
    yc = y - mean
    var = _head_sum(yc * yc) * (1.0 / RWKV_HEAD)
    y = yc * lax.rsqrt(var + RWKV_LN_EPS) * lng_ref[...] + lnb_ref[...]
    bonus = _head_sum(r * k * rk_ref[...]) * v
    return (y + bonus) * g


def _rwkv_body(u_ref, mu_ref, w0_ref, w2_ref, a0_ref, a2_ref, g2_ref, kk_ref, ka_ref, rk_ref,
               lng_ref, lnb_ref, y_ref, shift_ref, sout_ref, upad_scr, s_scr, *, chunk, group):
    L, G = chunk, group
    GL = G * L
    c = pl.program_id(1)
    pad = SUBLANES

    @pl.when(c == 0)
    def _():
        upad_scr[0:pad, :] = jnp.zeros((pad, RWKV_PROJ), F32)
        s_scr[...] = jnp.zeros(s_scr.shape, F32)

    u = u_ref[...]
    upad_scr[pad:pad + GL, :] = u
    prev = upad_scr[pad - 1:pad - 1 + GL, :]
    upad_scr[pad - 1:pad, :] = u[GL - 1:GL, :]
    r, k, v, logdecay, a, g, kk = _rwkv_pointwise(u, prev, mu_ref, w0_ref, w2_ref, a0_ref, a2_ref, g2_ref,
                                                  kk_ref, ka_ref)

    tril = jnp.where(lax.broadcasted_iota(jnp.int32, (L, L), 0) >= lax.broadcasted_iota(jnp.int32, (L, L), 1),
                     1.0, 0.0).astype(F32)
    cl = jnp.concatenate([_dot(tril, logdecay[i * L:(i + 1) * L, :], precision=HIGHEST) for i in range(G)], axis=0)
    e_in = jnp.exp(cl)
    e_inv = jnp.exp(-cl)
    r_t = r * e_in
    r_tb = r_t.astype(BF16)
    a_tb = (-kk * jnp.exp(cl - logdecay)).astype(BF16)
    b_tb = (kk * a * e_inv).astype(BF16)
    k_tb = (k * e_inv).astype(BF16)
    vb = v.astype(BF16)

    row = lax.broadcasted_iota(jnp.int32, (L, PAIR), 0)
    colh = lax.broadcasted_iota(jnp.int32, (L, PAIR), 1) & (RWKV_HEAD - 1)
    strict = row > colh
    incl = row >= colh
    eye_pair = jnp.where(row == colh, 1.0, 0.0).astype(F32)
    lane_lo = lax.broadcasted_iota(jnp.int32, (RWKV_HEAD, PAIR), 1) < RWKV_HEAD
    same_head = (lax.broadcasted_iota(jnp.int32, (PAIR, PAIR), 0) < RWKV_HEAD) == \
                (lax.broadcasted_iota(jnp.int32, (PAIR, PAIR), 1) < RWKV_HEAD)

    streams = [(i, p) for i in range(G) for p in range(RWKV_PAIRS)]
    ns = len(streams)
    blk = lambda x, i, p: x[i * L:(i + 1) * L, p * PAIR:(p + 1) * PAIR]
    lhs = [jnp.concatenate([blk(a_tb, i, p), blk(r_tb, i, p)], axis=0) for i, p in streams]
    m_ab = [_dot_nt(lhs[s], _bd(blk(b_tb, i, p))) for s, (i, p) in enumerate(streams)]
    m_ak = [_dot_nt(lhs[s], _bd(blk(k_tb, i, p))) for s, (i, p) in enumerate(streams)]
    n_ab = [jnp.where(strict, m[0:L], 0.0) for m in m_ab]
    m_rb = [jnp.where(incl, m[L:2 * L], 0.0).astype(BF16) for m in m_ab]
    n_ak = [jnp.where(strict, m[0:L], 0.0).astype(BF16) for m in m_ak]
    m_rk = [jnp.where(incl, m[L:2 * L], 0.0).astype(BF16) for m in m_ak]
    tinv = [eye_pair + n for n in n_ab]
    pwb = [n.astype(BF16) for n in n_ab]
    pw = [_dot(x, _bd(x)) for x in pwb]
    for _ in range(int(math.log2(L)) - 2):
        pwb = [x.astype(BF16) for x in pw]
        both = [_dot(jnp.concatenate([pwb[s], tinv[s].astype(BF16)], axis=0), _bd(pwb[s])) for s in range(ns)]
        pw = [x[0:L] for x in both]
        tinv = [tinv[s] + both[s][L:2 * L] for s in range(ns)]
    pwb = [x.astype(BF16) for x in pw]
    tinv = [tinv[s] + _dot(tinv[s].astype(BF16), _bd(pwb[s])) for s in range(ns)]
    tinvb = [x.astype(BF16) for x in tinv]
    nv_mv = [_dot(jnp.concatenate([n_ak[s], m_rk[s]], axis=0), _bd(blk(vb, i, p))) for s, (i, p) in enumerate(streams)]
    wu = [_dot(tinvb[s], jnp.concatenate([_bd(blk(a_tb, i, p)), _bd(nv_mv[s][0:L].astype(BF16))], axis=1))
          for s, (i, p) in enumerate(streams)]
    wub = [x.astype(BF16) for x in wu]
    qy = [_dot(m_rb[s], jnp.concatenate([_bd(wub[s][:, 0:PAIR]), _bd(wub[s][:, PAIR:2 * PAIR])], axis=1))
          for s in range(ns)]
    q = [(blk(r_t, i, p) + qy[s][:, 0:PAIR]).astype(BF16) for s, (i, p) in enumerate(streams)]
    y_loc = [qy[s][:, PAIR:2 * PAIR] + nv_mv[s][L:2 * L] for s in range(ns)]
    zeros_b = jnp.zeros((L, PAIR), BF16)
    mg = [_dot_tn(jnp.concatenate([wub[s], jnp.concatenate([zeros_b, blk(vb, i, p)], axis=1)], axis=0),
                  jnp.concatenate([blk(b_tb, i, p), blk(k_tb, i, p)], axis=0))
          for s, (i, p) in enumerate(streams)]
    p_end = [e_in[(i + 1) * L - 1:(i + 1) * L, p * PAIR:(p + 1) * PAIR] for i, p in streams]
    m_t = [(jnp.where(same_head, mg[s][0:PAIR], 0.0) * p_end[s]).astype(BF16) for s in range(ns)]
    g_t = [jnp.where(lane_lo, mg[s][PAIR:PAIR + RWKV_HEAD], mg[s][PAIR + RWKV_HEAD:2 * PAIR]) * p_end[s]
           for s in range(ns)]

    y_rows = []
    for i in range(G):
        y_pairs = []
        for p in range(RWKV_PAIRS):
            s = i * RWKV_PAIRS + p
            s0 = s_scr[p]
            s0b = s0.astype(BF16)
            y_pairs.append(_dot_nt(q[s], _bd(s0b)) + y_loc[s])
            s_scr[p] = s0 * p_end[s] + _dot(s0b, m_t[s]) + g_t[s]
        y_rows.append(jnp.concatenate(y_pairs, axis=-1))
    y = jnp.concatenate(y_rows, axis=0)
    y_ref[...] = _rwkv_finish(y, r, k, v, g, rk_ref, lng_ref, lnb_ref)

    @pl.when(c == pl.num_programs(1) - 1)
    def _():
        sout_ref[0] = s_scr[...]
        shift_ref[0] = upad_scr[pad - 1:pad, :]


_RWKV_PARAM_NAMES = ("mu", "w0", "w2", "a0", "a2", "g2", "k_k", "k_a", "r_k", "ln_g", "ln_b")


def _rwkv(u, p, *, batch, seq):
    rows = RWKV_CHUNK * RWKV_GROUP
    nc = seq // rows
    params = [p[n] for n in _RWKV_PARAM_NAMES]
    sspec = pl.BlockSpec((1, RWKV_PAIRS, RWKV_HEAD, PAIR), lambda b, c: (b, 0, 0, 0))
    y, shift, s_last = pl.pallas_call(
        functools.partial(_rwkv_body, chunk=RWKV_CHUNK, group=RWKV_GROUP),
        grid=(batch, nc),
        in_specs=[pl.BlockSpec((rows, RWKV_PROJ), lambda b, c: (b * nc + c, 0))] + [_pspec(a) for a in params],
        out_specs=[pl.BlockSpec((rows, GROUP_WIDTH), lambda b, c: (b * nc + c, 0)),
                   pl.BlockSpec((1, 1, RWKV_PROJ), lambda b, c: (b, 0, 0)), sspec],
        out_shape=[jax.ShapeDtypeStruct((batch * seq, GROUP_WIDTH), F32),
                   jax.ShapeDtypeStruct((batch, 1, RWKV_PROJ), F32),
                   jax.ShapeDtypeStruct((batch, RWKV_PAIRS, RWKV_HEAD, PAIR), F32)],
        scratch_shapes=[pltpu.VMEM((SUBLANES + rows, RWKV_PROJ), F32),
                        pltpu.VMEM((RWKV_PAIRS, RWKV_HEAD, PAIR), F32)],
        compiler_params=_cparams("parallel", "arbitrary"),
        name="rwkv",
    )(u, *[_parg(a) for a in params])
    s_last = s_last.reshape(batch, RWKV_PAIRS, RWKV_HEAD, 2, RWKV_HEAD).transpose(0, 1, 3, 2, 4).reshape(
        batch, RWKV_HEADS, RWKV_HEAD, RWKV_HEAD)
    return y, shift.reshape(batch, RWKV_PROJ), s_last


def _rwkv_step_body(u_ref, shift0_ref, s0_ref, *rest, seq, batch, layer):
    sdone_ref, rest = (rest[0], rest[1:]) if layer else (None, rest)
    (mu_ref, w0_ref, w2_ref, a0_ref, a2_ref, g2_ref, kk_ref, ka_ref, rk_ref, lng_ref, lnb_ref, y_ref, sout_ref,
     r_scr, w_scr, k_scr, b_scr, nkk_scr, v_scr, y_scr) = rest
    T, B = seq, batch
    j = pl.program_id(0)
    if layer:
        sout_ref[0:layer] = sdone_ref[...]
    tiles = RWKV_STEP_TILES

    def pointwise(t):
        u = u_ref[t * B:(t + 1) * B, :]
        prev = shift0_ref[...] if t == 0 else u_ref[(t - 1) * B:t * B, :]
        return _rwkv_pointwise(u, prev, mu_ref, w0_ref, w2_ref, a0_ref, a2_ref, g2_ref, kk_ref, ka_ref)

    @pl.when(j == 0)
    def _():
        for t in range(T):
            r, k, v, logdecay, a, _, kk = pointwise(t)
            r_scr[t] = r.T
            w_scr[t] = jnp.exp(logdecay).T
            k_scr[t] = k.T
            b_scr[t] = (kk * a).T
            nkk_scr[t] = (-kk).T
            v_scr[t] = v.T

    i0 = j * tiles
    keys = pl.ds(pl.multiple_of((i0 // RWKV_HEAD) * RWKV_HEAD, RWKV_HEAD), RWKV_HEAD)
    for q in range(tiles):
        vi = pl.ds(i0 + q, 1)
        s = s0_ref[:, q, :].T
        for t in range(T):
            sa = jnp.sum(s * nkk_scr[t, keys, :], axis=0, keepdims=True)
            s = s * w_scr[t, keys, :] + k_scr[t, keys, :] * v_scr[t, vi, :] + b_scr[t, keys, :] * sa
            y_scr[t, vi, :] = jnp.sum(s * r_scr[t, keys, :], axis=0, keepdims=True)
        sout_ref[layer, :, q, :] = s.T

    @pl.when(j == pl.num_programs(0) - 1)
    def _():
        for t in range(T):
            r, k, v, _, _, g, _ = pointwise(t)
            y_ref[t * B:(t + 1) * B, :] = _rwkv_finish(y_scr[t].T, r, k, v, g, rk_ref, lng_ref, lnb_ref)


def _rwkv_step(u, shift0, s_all, s_done, p, *, batch, seq, layer):
    n = batch * seq
    srows = RWKV_HEADS * RWKV_HEAD
    params = [p[nm] for nm in _RWKV_PARAM_NAMES]
    sspec, prev_specs, sout_spec = _layer_state_specs(layer, batch, RWKV_STEP_TILES, RWKV_HEAD)
    prev_args = [s_done] if layer else []
    tposed = pltpu.VMEM((seq, GROUP_WIDTH, batch), F32)
    return pl.pallas_call(
        functools.partial(_rwkv_step_body, seq=seq, batch=batch, layer=layer),
        grid=(srows // RWKV_STEP_TILES,),
        in_specs=[_full_spec((n, RWKV_PROJ)), _full_spec((batch, RWKV_PROJ)), sspec] + prev_specs
                 + [_pspec(a) for a in params],
        out_specs=[_full_spec((n, GROUP_WIDTH)), sout_spec],
        out_shape=[jax.ShapeDtypeStruct((n, GROUP_WIDTH), F32),
                   jax.ShapeDtypeStruct((layer + 1, batch, srows, RWKV_HEAD), F32)],
        scratch_shapes=[tposed] * 7,
        compiler_params=_cparams("arbitrary"),
        name="rwkv_step",
    )(u, shift0, s_all.reshape(s_all.shape[0], batch, srows, RWKV_HEAD), *prev_args, *[_parg(a) for a in params])


def _s5_body(u_ref, hre0_ref, him0_ref, are_ref, aim_ref, bmat_ref, cmat_ref, d_ref, gw_ref, gb_ref,
             y_ref, hre_ref, him_ref, hs_scr, *, steps):
    c = pl.program_id(1)
    ns = S5_WIDTH
    bsub = u_ref.shape[1]

    @pl.when(c == 0)
    def _():
        hre_ref[...] = hre0_ref[...]
        him_ref[...] = him0_ref[...]

    u = u_ref[...].reshape(steps * bsub, GROUP_WIDTH)
    hs_scr[...] = _dot(u.astype(BF16), bmat_ref[...])
    are = jnp.broadcast_to(are_ref[...], (bsub, ns))
    aim = jnp.broadcast_to(aim_ref[...], (bsub, ns))

    def step(t, carry):
        hre, him = carry
        r0 = pl.multiple_of(t * bsub, bsub)
        nre = are * hre - aim * him + hs_scr[pl.ds(r0, bsub), 0:ns]
        nim = are * him + aim * hre + hs_scr[pl.ds(r0, bsub), ns:2 * ns]
        hs_scr[pl.ds(r0, bsub), 0:ns] = nre
        hs_scr[pl.ds(r0, bsub), ns:2 * ns] = nim
        return nre, nim

    hre, him = lax.fori_loop(0, steps, step, (hre_ref[...], him_ref[...]))
    hre_ref[...] = hre
    him_ref[...] = him
    y = _dot(hs_scr[...].astype(BF16), cmat_ref[...]) + u * d_ref[...]
    y = _gelu_tanh(y)
    yy = _dot(y.astype(BF16), gw_ref[...]) + gb_ref[...]
    out = yy[:, 0:GROUP_WIDTH] * _sigmoid(yy[:, GROUP_WIDTH:2 * GROUP_WIDTH])
    y_ref[...] = out.reshape(steps, bsub, GROUP_WIDTH)


def _s5(u_tm, hre0, him0, lp):
    seq, batch, _ = u_tm.shape
    steps = min(TM_CHUNK, seq)
    bsub = SUBLANES
    hspec = pl.BlockSpec((bsub, S5_WIDTH), lambda b, c: (b, 0))
    tspec = pl.BlockSpec((steps, bsub, GROUP_WIDTH), lambda b, c: (c, b, 0))
    consts = (lp["s5_are"], lp["s5_aim"], lp["s5_bmat"], lp["s5_cmat"], lp["s5_d"], lp["s5_gw"], lp["s5_gb"])
    return pl.pallas_call(
        functools.partial(_s5_body, steps=steps),
        grid=(batch // bsub, seq // steps),
        in_specs=[tspec, hspec, hspec] + [_pspec(a) for a in consts],
        out_specs=[tspec, hspec, hspec],
        out_shape=[jax.ShapeDtypeStruct((seq, batch, GROUP_WIDTH), F32),
                   jax.ShapeDtypeStruct((batch, S5_WIDTH), F32),
                   jax.ShapeDtypeStruct((batch, S5_WIDTH), F32)],
        scratch_shapes=[pltpu.VMEM((steps * bsub, 2 * S5_WIDTH), F32)],
        compiler_params=_cparams("parallel", "arbitrary"),
        name="s5",
    )(u_tm, hre0, him0, *[_parg(a) for a in consts])


def _pool_body(u_ref, buf0_ref, pw_ref, sc_ref, y_ref, buf_ref, f_scr, *, steps, pos0):
    c = pl.program_id(1)
    bsub = u_ref.shape[1]
    GW = GROUP_WIDTH
    halo = POOL_BUF + 1

    @pl.when(c == 0)
    def _():
        f_scr[0] = jnp.zeros((bsub, GW), F32)
        for i in range(POOL_BUF):
            f_scr[1 + i] = buf0_ref[:, i * GW:(i + 1) * GW]

    u = u_ref[...]
    f_scr[halo:halo + steps] = u
    f = f_scr[...]
    s2 = f[1:] + f[:-1]
    s4 = s2[2:] + s2[:-2]
    s8 = s4[4:] + s4[:-4]
    s16 = s8[8:] + s8[:-8]
    f_scr[0:halo] = f[steps:steps + halo]
    lane = lax.broadcasted_iota(jnp.int32, (steps, bsub, GW), 2)
    tpos = lax.broadcasted_iota(jnp.int32, (steps, bsub, GW), 0) + (pos0 + 1) + c * steps
    win = jnp.where(lane < POOL_CH, s2[halo - 1:halo - 1 + steps],
                    jnp.where(lane < 2 * POOL_CH, s4[halo - 3:halo - 3 + steps],
                              jnp.where(lane < 3 * POOL_CH, s8[halo - 7:halo - 7 + steps],
                                        s16[halo - 15:halo - 15 + steps])))
    wlen = jnp.where(lane < POOL_CH, POOL_WINDOWS[0],
                     jnp.where(lane < 2 * POOL_CH, POOL_WINDOWS[1],
                               jnp.where(lane < 3 * POOL_CH, POOL_WINDOWS[2], POOL_WINDOWS[3])))
    cnt = jnp.minimum(tpos, wlen).astype(F32)
    pooled = (win / cnt - u).reshape(steps * bsub, GW)
    y = _dot(pooled.astype(BF16), pw_ref[...]) * sc_ref[...]
    y_ref[...] = y.reshape(steps, bsub, GW)

    @pl.when(c == pl.num_programs(1) - 1)
    def _():
        for i in range(POOL_BUF):
            buf_ref[:, i * GW:(i + 1) * GW] = f_scr[1 + i]


def _pool(u_tm, buf0, lp, *, pos0):
    seq, batch, _ = u_tm.shape
    steps = min(TM_CHUNK, seq)
    bsub = SUBLANES
    flat = POOL_BUF * GROUP_WIDTH
    tspec = pl.BlockSpec((steps, bsub, GROUP_WIDTH), lambda b, c: (c, b, 0))
    bspec = pl.BlockSpec((bsub, flat), lambda b, c: (b, 0))
    y, buf = pl.pallas_call(
        functools.partial(_pool_body, steps=steps, pos0=pos0),
        grid=(batch // bsub, seq // steps),
        in_specs=[tspec, bspec, _pspec(lp["pool_w"]), _pspec(lp["pool_scale"])],
        out_specs=[tspec, bspec],
        out_shape=[jax.ShapeDtypeStruct((seq, batch, GROUP_WIDTH), F32), jax.ShapeDtypeStruct((batch, flat), F32)],
        scratch_shapes=[pltpu.VMEM((POOL_BUF + 1 + steps, bsub, GROUP_WIDTH), F32)],
        compiler_params=_cparams("parallel", "arbitrary"),
        name="pool",
    )(u_tm, buf0.reshape(batch, flat), _parg(lp["pool_w"]), _parg(lp["pool_scale"]))
    return y, buf.reshape(batch, POOL_BUF, GROUP_WIDTH)


def _block_diag(blocks):
    n, g, r, c = blocks.shape
    eye = jnp.eye(g, dtype=blocks.dtype)
    return (eye[None, :, None, :, None] * blocks[:, :, :, None, :]).reshape(n, g * r, g * c)


def _stacked_params(P):
    row = lambda a: a.reshape(a.shape[0], 1, -1)
    pad_lanes = lambda a: jnp.pad(a, ((0, 0), (0, LANES - a.shape[1])))
    bf = lambda a: a.astype(BF16)
    w_in = P["w_in"]
    split = GROUP_WIDTH + SSD_CONV_DIM
    w_all = jnp.concatenate([w_in[:, :, :split], w_in[:, :, split + SSD_HEADS:],
                             jnp.pad(w_in[:, :, split:split + SSD_HEADS], ((0, 0), (0, 0), (0, LANES - SSD_HEADS)))],
                            axis=2)

    lam = lax.complex(P["s5_lam_re"], P["s5_lam_im"])
    a_bar = jnp.exp(lam * jnp.exp(P["s5_log_step"])[..., None])
    b_bar = ((a_bar - 1.0) / lam)[..., None] * lax.complex(P["s5_b_re"], P["s5_b_im"])
    b_t = jnp.swapaxes(b_bar, 2, 3)
    bmat = jnp.concatenate([_block_diag(jnp.real(b_t)), _block_diag(jnp.imag(b_t))], axis=2)
    c_t = jnp.swapaxes(lax.complex(P["s5_c_re"], P["s5_c_im"]), 2, 3)
    cmat = jnp.concatenate([_block_diag(jnp.real(c_t)), -_block_diag(jnp.imag(c_t))], axis=1)

    out = dict(
        norm_ffn1=row(P["norm_ffn1"]), ffn1_in=bf(P["ffn1_in"]), ffn1_out=bf(P["ffn1_out"]),
        norm_mix=row(P["norm_mix"]), w_all=bf(w_all),
        conv_w=P["ssd_conv_w"], conv_b=row(P["ssd_conv_b"]),
        dt_bias=row(pad_lanes(P["ssd_dt_bias"])), a_log=row(pad_lanes(P["ssd_a_log"])),
        a_neg_exp=row(jnp.repeat(-jnp.exp(P["ssd_a_log"]), SSD_HEAD_DIM, axis=1)),
        d_skip=row(jnp.repeat(P["ssd_d"], SSD_HEAD_DIM, axis=1)), ssd_norm=row(P["ssd_norm"]),
        s5_are=row(jnp.real(a_bar)), s5_aim=row(jnp.imag(a_bar)), s5_bmat=bf(bmat), s5_cmat=bf(cmat),
        s5_d=row(P["s5_d"]), s5_gw=bf(P["s5_glu_w"]), s5_gb=row(P["s5_glu_b"]),
        pool_w=bf(_block_diag(P["pool_w"])), pool_scale=row(P["pool_scale"]),
        w_out=bf(P["w_out"]),
        norm_ffn2=row(P["norm_ffn2"]), ffn2_in=bf(P["ffn2_in"]), ffn2_out=bf(P["ffn2_out"]),
    )
    for name in _RWKV_PARAM_NAMES:
        a = P["rwkv_" + name]
        out["rwkv_" + name] = bf(a) if name in ("w2", "a2", "g2") else row(a)
    return out


def _layer_params(stacked, l):
    lp = {k: _Layered((v, l)) for k, v in stacked.items()}
    lp["rwkv"] = {n: lp["rwkv_" + n] for n in _RWKV_PARAM_NAMES}
    lp["head_expand"] = jnp.pad(jnp.repeat(jnp.eye(SSD_HEADS, dtype=F32), SSD_HEAD_DIM, axis=1),
                                ((0, LANES - SSD_HEADS), (0, 0)))
    return lp


def _to_tm(rows, batch, seq):
    return jnp.swapaxes(rows.reshape(batch, seq, rows.shape[-1]), 0, 1)


def _from_tm(x_tm):
    seq, batch, w = x_tm.shape
    return jnp.swapaxes(x_tm, 0, 1).reshape(batch * seq, w)


def _mixers_prompt(lp, proj, *, batch, seq):
    z, xbc, ur, us5, upool, dtr = proj
    y_ssd, conv_new, ssd_new = _ssd(z, xbc, dtr, lp, batch=batch, seq=seq)
    y_rwkv, shift_new, rwkv_new = _rwkv(ur, lp["rwkv"], batch=batch, seq=seq)
    zeros = jnp.zeros((batch, S5_WIDTH), F32)
    y_s5, s5re, s5im = _s5(_to_tm(us5, batch, seq), zeros, zeros, lp)
    y_pool, pool_new = _pool(_to_tm(upool, batch, seq), jnp.zeros((batch, POOL_BUF, GROUP_WIDTH), F32), lp, pos0=0)
    ys = (y_ssd, y_rwkv, _from_tm(y_s5), _from_tm(y_pool))
    states = (conv_new, ssd_new, shift_new, rwkv_new, s5re.reshape(batch, S5_GROUPS, S5_STATE),
              s5im.reshape(batch, S5_GROUPS, S5_STATE), pool_new)
    return ys, states


def _mixers_decode(lp, proj, states, done, *, batch, seq, layer):
    z, xbc, ur, us5, upool, dtr = proj
    conv0, shift0, s5re0, s5im0, pool0 = (states[i][layer] for i in (0, 2, 4, 5, 6))
    ssd_done, rwkv_done = (done[1], done[3]) if layer else (None, None)
    y_ssd, conv_new, ssd_new = _ssd_step(z, xbc, dtr, conv0, states[1], ssd_done, lp, batch=batch, seq=seq,
                                         layer=layer)
    y_rwkv, rwkv_new = _rwkv_step(ur, shift0, states[3], rwkv_done, lp["rwkv"], batch=batch, seq=seq, layer=layer)
    shift_new = ur[(seq - 1) * batch:, :]
    tm = lambda a: a.reshape(seq, batch, a.shape[-1])
    y_s5, s5re, s5im = _s5(tm(us5), s5re0.reshape(batch, S5_WIDTH), s5im0.reshape(batch, S5_WIDTH), lp)
    y_pool, pool_new = _pool(tm(upool), pool0, lp, pos0=PAST_LEN)
    rows = lambda a: a.reshape(seq * batch, a.shape[-1])
    ys = (y_ssd, y_rwkv, rows(y_s5), rows(y_pool))
    new_states = (conv_new, ssd_new, shift_new, rwkv_new, s5re.reshape(batch, S5_GROUPS, S5_STATE),
                  s5im.reshape(batch, S5_GROUPS, S5_STATE), pool_new)
    return ys, new_states


_WIDTHS = (GROUP_WIDTH, SSD_CONV_DIM, RWKV_PROJ, GROUP_WIDTH, GROUP_WIDTH, LANES)


def _trunk(x, layer_params, norm_final, mixers):
    states = []
    mix, lp = None, None
    for l, lp_next in enumerate(layer_params):
        if l > 0:
            x = _ffn(x, lp["norm_ffn2"], lp["ffn2_in"], lp["ffn2_out"], mix=mix, wmix=lp["w_out"])
        lp = lp_next
        x = _ffn(x, lp["norm_ffn1"], lp["ffn1_in"], lp["ffn1_out"])
        proj = _inproj(x, lp["norm_mix"], lp["w_all"], _WIDTHS)
        mix, st = mixers(l, lp, proj, states[-1] if states else None)
        states.append(st)
    x = _ffn(x, lp["norm_ffn2"], lp["ffn2_in"], lp["ffn2_out"], mix=mix, wmix=lp["w_out"], gf=norm_final)
    return x, states


def kernel(x_prompt, x_sample, state_ssd_conv, state_ssd, state_rwkv_shift, state_rwkv, state_s5_re, state_s5_im, state_pool, norm_ffn1, ffn1_in, ffn1_out, norm_mix, w_in, ssd_conv_w, ssd_conv_b, ssd_dt_bias, ssd_a_log, ssd_d, ssd_norm, rwkv_mu, rwkv_w0, rwkv_w2, rwkv_a0, rwkv_a2, rwkv_g2, rwkv_k_k, rwkv_k_a, rwkv_r_k, rwkv_ln_g, rwkv_ln_b, s5_lam_re, s5_lam_im, s5_log_step, s5_b_re, s5_b_im, s5_c_re, s5_c_im, s5_d, s5_glu_w, s5_glu_b, pool_w, pool_scale, w_out, norm_ffn2, ffn2_in, ffn2_out, norm_final):
    P = dict(norm_ffn1=norm_ffn1, ffn1_in=ffn1_in, ffn1_out=ffn1_out, norm_mix=norm_mix, w_in=w_in,
             ssd_conv_w=ssd_conv_w, ssd_conv_b=ssd_conv_b, ssd_dt_bias=ssd_dt_bias, ssd_a_log=ssd_a_log,
             ssd_d=ssd_d, ssd_norm=ssd_norm, rwkv_mu=rwkv_mu, rwkv_w0=rwkv_w0, rwkv_w2=rwkv_w2, rwkv_a0=rwkv_a0,
             rwkv_a2=rwkv_a2, rwkv_g2=rwkv_g2, rwkv_k_k=rwkv_k_k, rwkv_k_a=rwkv_k_a,
             rwkv_r_k=rwkv_r_k.reshape(rwkv_r_k.shape[0], -1), rwkv_ln_g=rwkv_ln_g, rwkv_ln_b=rwkv_ln_b,
             s5_lam_re=s5_lam_re, s5_lam_im=s5_lam_im, s5_log_step=s5_log_step, s5_b_re=s5_b_re, s5_b_im=s5_b_im,
             s5_c_re=s5_c_re, s5_c_im=s5_c_im, s5_d=s5_d, s5_glu_w=s5_glu_w, s5_glu_b=s5_glu_b, pool_w=pool_w,
             pool_scale=pool_scale, w_out=w_out, norm_ffn2=norm_ffn2, ffn2_in=ffn2_in, ffn2_out=ffn2_out)
    depth = norm_ffn1.shape[0]
    bp, tp, d = x_prompt.shape
    bs, ts, _ = x_sample.shape
    stacked = _stacked_params(P)
    layer_params = [_layer_params(stacked, l) for l in range(depth)]
    gf = norm_final.reshape(1, -1)
    sample_states = (state_ssd_conv, state_ssd, state_rwkv_shift, state_rwkv, state_s5_re, state_s5_im, state_pool)

    y_p, st_p = _trunk(x_prompt.reshape(bp * tp, d), layer_params, gf,
                       lambda l, lp, proj, done: _mixers_prompt(lp, proj, batch=bp, seq=tp))
    x_s = jnp.swapaxes(x_sample, 0, 1).reshape(ts * bs, d)
    y_s, st_s = _trunk(x_s, layer_params, gf,
                       lambda l, lp, proj, done: _mixers_decode(lp, proj, sample_states, done,
                                                                batch=bs, seq=ts, layer=l))
    outs = [y_p.reshape(bp, tp, d), jnp.swapaxes(y_s.reshape(ts, bs, d), 0, 1)]
    for i, ref_state in enumerate(sample_states):
        outs.append(jnp.stack([st[i] for st in st_p]))
        if i in (1, 3):
            outs.append(st_s[-1][i].reshape(ref_state.shape))
        else:
            outs.append(jnp.stack([st[i] for st in st_s]))
    return tuple(outs)
```

```python
import functools
import math

import jax
import jax.numpy as jnp
from jax import lax
from jax.experimental import pallas as pl
from jax.experimental.pallas import tpu as pltpu

F32 = jnp.float32
BF16 = jnp.bfloat16
HIGHEST = lax.Precision.HIGHEST

SUBLANES = 8
LANES = 128
VMEM_LIMIT_BYTES = 56 * 1024 * 1024

GROUP_WIDTH = 256
SSD_HEAD_DIM = 64
SSD_HEADS = 4
SSD_GROUPS = 2
SSD_STATE = 128
SSD_CONV = 4
SSD_CONV_DIM = GROUP_WIDTH + 2 * SSD_GROUPS * SSD_STATE
SSD_CHUNK = 128
RWKV_HEAD = 64
RWKV_HEADS = 4
RWKV_PROJ = 1024
RWKV_LN_EPS = 64e-5
RWKV_CHUNK = 64
RWKV_GROUP = 8
S5_GROUP_CH = 16
S5_GROUPS = 16
S5_STATE = 64
S5_WIDTH = S5_GROUPS * S5_STATE
POOL_WINDOWS = (2, 4, 8, 16)
POOL_CH = 64
POOL_BUF = 15
RMS_EPS = 1e-6
PAST_LEN = 16384

ROW_TILE = 512
FFN_CHUNK = 256
TM_CHUNK = 64
SSD_STEP_TILES = 16
RWKV_STEP_TILES = 16


def _cparams(*sem):
    return pltpu.CompilerParams(dimension_semantics=sem, vmem_limit_bytes=VMEM_LIMIT_BYTES)


def _dot(a, b, **kw):
    return jnp.dot(a, b, preferred_element_type=F32, **kw)


def _dot_nt(a, b):
    return lax.dot_general(a, b, (((1,), (1,)), ((), ())), preferred_element_type=F32)


def _dot_tn(a, b):
    return lax.dot_general(a, b, (((0,), (0,)), ((), ())), preferred_element_type=F32)


def _sigmoid(x):
    return 1.0 / (1.0 + jnp.exp(-x))


def _silu(x):
    return x * _sigmoid(x)


def _softplus(x):
    return jnp.maximum(x, 0.0) + jnp.log(1.0 + jnp.exp(-jnp.abs(x)))


def _gelu_tanh(x):
    c = math.sqrt(2.0 / math.pi)
    return x * (0.5 * (1.0 + jnp.tanh(c * (x + 0.044715 * (x * x * x)))))


def _rms(x, g):
    return x * lax.rsqrt(jnp.mean(x * x, axis=-1, keepdims=True) + RMS_EPS) * g


def _full_spec(shape):
    n = len(shape)
    return pl.BlockSpec(shape, lambda *_: (0,) * n)


class _Layered(tuple):
    pass


def _pspec(p):
    if isinstance(p, _Layered):
        a, l = p
        return pl.BlockSpec((None,) + a.shape[1:], lambda *_: (l,) + (0,) * (a.ndim - 1))
    return _full_spec(p.shape)


def _parg(p):
    return p[0] if isinstance(p, _Layered) else p


def _ffn_body(*refs, has_mix, final_norm):
    it = iter(refs)
    x_ref = next(it)
    x = x_ref[...]
    if has_mix:
        y_refs = [next(it) for _ in range(4)]
        wmix_ref = next(it)
        for i, y_ref in enumerate(y_refs):
            x = x + _dot(y_ref[...].astype(BF16), wmix_ref[i * GROUP_WIDTH:(i + 1) * GROUP_WIDTH, :])
    g_ref, wi_ref, wo_ref = next(it), next(it), next(it)
    gf_ref = next(it) if final_norm else None
    o_ref = next(it)
    h = _rms(x, g_ref[...]).astype(BF16)
    d_ff = wo_ref.shape[0]
    acc = jnp.zeros_like(x)
    for c in range(d_ff // FFN_CHUNK):
        lo = c * FFN_CHUNK
        gate = _dot(h, wi_ref[:, lo:lo + FFN_CHUNK])
        up = _dot(h, wi_ref[:, d_ff + lo:d_ff + lo + FFN_CHUNK])
        act = (_silu(gate) * up).astype(BF16)
        acc = acc + _dot(act, wo_ref[lo:lo + FFN_CHUNK, :])
    x = x + 0.5 * acc
    if final_norm:
        x = _rms(x, gf_ref[...])
    o_ref[...] = x


def _ffn(x, g, wi, wo, mix=None, wmix=None, gf=None):
    rows, d = x.shape
    row_spec = lambda w: pl.BlockSpec((ROW_TILE, w), lambda i: (i, 0))
    args, specs = [x], [row_spec(d)]
    if mix is not None:
        for y in mix:
            args.append(y)
            specs.append(row_spec(y.shape[1]))
        args.append(_parg(wmix))
        specs.append(_pspec(wmix))
    for a in (g, wi, wo) + ((gf,) if gf is not None else ()):
        args.append(_parg(a))
        specs.append(_pspec(a))
    return pl.pallas_call(
        functools.partial(_ffn_body, has_mix=mix is not None, final_norm=gf is not None),
        grid=(rows // ROW_TILE,),
        in_specs=specs,
        out_specs=row_spec(d),
        out_shape=jax.ShapeDtypeStruct((rows, d), F32),
        compiler_params=_cparams("parallel"),
        name="ffn",
    )(*args)


def _inproj_body(x_ref, g_ref, w_ref, *o_refs):
    h = _rms(x_ref[...], g_ref[...]).astype(BF16)
    off = 0
    for o_ref in o_refs:
        n = o_ref.shape[-1]
        o_ref[...] = _dot(h, w_ref[:, off:off + n])
        off += n


def _inproj(x, g, w, widths):
    rows, d = x.shape
    row_spec = lambda w_: pl.BlockSpec((ROW_TILE, w_), lambda i: (i, 0))
    return pl.pallas_call(
        _inproj_body,
        grid=(rows // ROW_TILE,),
        in_specs=[row_spec(d), _pspec(g), _pspec(w)],
        out_specs=[row_spec(n) for n in widths],
        out_shape=[jax.ShapeDtypeStruct((rows, n), F32) for n in widths],
        compiler_params=_cparams("parallel"),
        name="inproj",
    )(x, _parg(g), _parg(w))


def _ssd_body(z_ref, xbc_ref, dt_ref, cw_ref, cb_ref, dtb_ref, alog_ref, dsk_ref, ng_ref,
              y_ref, conv_ref, hout_ref, xpad_scr, h_scr, *, chunk):
    L = chunk
    c = pl.program_id(1)
    pad = SUBLANES
    halo = SSD_CONV - 1

    @pl.when(c == 0)
    def _():
        xpad_scr[0:pad, :] = jnp.zeros((pad, SSD_CONV_DIM), F32)
        h_scr[...] = jnp.zeros(h_scr.shape, F32)

    xpad_scr[pad:pad + L, :] = xbc_ref[...]
    conv = cb_ref[...] + cw_ref[0:1, :] * xpad_scr[pad - halo:pad - halo + L, :]
    for j in range(1, SSD_CONV):
        conv = conv + cw_ref[j:j + 1, :] * xpad_scr[pad - halo + j:pad - halo + j + L, :]
    xpad_scr[pad - halo:pad, :] = xpad_scr[pad + L - halo:pad + L, :]
    conv = _silu(conv)
    xs = conv[:, 0:GROUP_WIDTH]
    bm = conv[:, GROUP_WIDTH:2 * GROUP_WIDTH].astype(BF16)
    cm = conv[:, 2 * GROUP_WIDTH:3 * GROUP_WIDTH].astype(BF16)

    row = lax.broadcasted_iota(jnp.int32, (L, L), 0)
    col = lax.broadcasted_iota(jnp.int32, (L, L), 1)
    causal = row >= col
    dt = _softplus(dt_ref[...] + dtb_ref[...])
    da = dt * (-jnp.exp(alog_ref[...]))
    acs = _dot(jnp.where(causal, 1.0, 0.0).astype(F32), da, precision=HIGHEST)
    acs_t = acs.T
    e_acs = jnp.exp(acs)
    acs_last = acs[L - 1:L, :]
    e_end = jnp.exp(acs_last - acs)
    e_last = jnp.exp(acs_last)

    ys = []
    for h in range(SSD_HEADS):
        g = h // (SSD_HEADS // SSD_GROUPS)
        bg = bm[:, g * SSD_STATE:(g + 1) * SSD_STATE]
        cg = cm[:, g * SSD_STATE:(g + 1) * SSD_STATE]
        x_h = xs[:, h * SSD_HEAD_DIM:(h + 1) * SSD_HEAD_DIM]
        xdt = x_h * dt[:, h:h + 1]
        seg = acs[:, h:h + 1] - acs_t[h:h + 1, :]
        decay = jnp.exp(jnp.where(causal, seg, -jnp.inf))
        scores = _dot_nt(cg, bg) * decay
        y_h = _dot(scores.astype(BF16), xdt.astype(BF16))
        h_prev = h_scr[h]
        y_h = y_h + _dot_nt(cg, h_prev.astype(BF16)) * e_acs[:, h:h + 1]
        st = _dot_tn((xdt * e_end[:, h:h + 1]).astype(BF16), bg)
        h_scr[h] = h_prev * e_last[:, h:h + 1] + st
        ys.append(y_h)
    y = jnp.concatenate(ys, axis=-1) + xs * dsk_ref[...]
    y = y * _silu(z_ref[...])
    y_ref[...] = _rms(y, ng_ref[...])

    @pl.when(c == pl.num_programs(1) - 1)
    def _():
        hout_ref[0] = h_scr[...]
        conv_ref[0] = xpad_scr[pad - halo:pad, :]


def _ssd(z, xbc, dtr, lp, *, batch, seq):
    chunk = SSD_CHUNK
    nc = seq // chunk
    rspec = lambda w: pl.BlockSpec((chunk, w), lambda b, c: (b * nc + c, 0))
    consts = (lp["conv_w"], lp["conv_b"], lp["dt_bias"], lp["a_log"], lp["d_skip"], lp["ssd_norm"])
    return pl.pallas_call(
        functools.partial(_ssd_body, chunk=chunk),
        grid=(batch, nc),
        in_specs=[rspec(GROUP_WIDTH), rspec(SSD_CONV_DIM), rspec(LANES)] + [_pspec(a) for a in consts],
        out_specs=[rspec(GROUP_WIDTH),
                   pl.BlockSpec((1, SSD_CONV - 1, SSD_CONV_DIM), lambda b, c: (b, 0, 0)),
                   pl.BlockSpec((1, SSD_HEADS, SSD_HEAD_DIM, SSD_STATE), lambda b, c: (b, 0, 0, 0))],
        out_shape=[jax.ShapeDtypeStruct((batch * seq, GROUP_WIDTH), F32),
                   jax.ShapeDtypeStruct((batch, SSD_CONV - 1, SSD_CONV_DIM), F32),
                   jax.ShapeDtypeStruct((batch, SSD_HEADS, SSD_HEAD_DIM, SSD_STATE), F32)],
        scratch_shapes=[pltpu.VMEM((SUBLANES + chunk, SSD_CONV_DIM), F32),
                        pltpu.VMEM((SSD_HEADS, SSD_HEAD_DIM, SSD_STATE), F32)],
        compiler_params=_cparams("parallel", "arbitrary"),
        name="ssd",
    )(z, xbc, dtr, *[_parg(a) for a in consts])


def _ssd_step_body(z_ref, xbc_ref, dt_ref, conv0_ref, h0_ref, *rest, seq, batch, layer):
    hdone_ref, rest = (rest[0], rest[1:]) if layer else (None, rest)
    (cw_ref, cb_ref, dtb_ref, aneg_ref, dsk_ref, ng_ref, hexp_ref, y_ref, conv_ref, hout_ref,
     xs_scr, bm_scr, cm_scr, xdt_scr, dec_scr, y_scr) = rest
    T, B = seq, batch
    if layer:
        hout_ref[0:layer] = hdone_ref[...]
    GW = GROUP_WIDTH
    j = pl.program_id(0)
    tiles = SSD_STEP_TILES

    @pl.when(j == 0)
    def _():
        rows = [conv0_ref[:, i * SSD_CONV_DIM:(i + 1) * SSD_CONV_DIM] for i in range(SSD_CONV - 1)]
        rows += [xbc_ref[t * B:(t + 1) * B, :] for t in range(T)]
        for t in range(T):
            conv = cb_ref[...] + cw_ref[0:1, :] * rows[t]
            for i in range(1, SSD_CONV):
                conv = conv + cw_ref[i:i + 1, :] * rows[t + i]
            conv = _silu(conv)
            xs = conv[:, 0:GW]
            xs_scr[t] = xs
            for g in range(SSD_GROUPS):
                bm_scr[t, g] = conv[:, GW + g * SSD_STATE:GW + (g + 1) * SSD_STATE].T
                cm_scr[t, g] = conv[:, 2 * GW + g * SSD_STATE:2 * GW + (g + 1) * SSD_STATE].T
            dt = _softplus(dt_ref[t * B:(t + 1) * B, :] + dtb_ref[...])
            dte = _dot(dt, hexp_ref[...], precision=HIGHEST)
            xdt_scr[t] = (xs * dte).T
            dec_scr[t] = jnp.exp(dte * aneg_ref[...]).T
        for i in range(SSD_CONV - 1):
            conv_ref[:, i * SSD_CONV_DIM:(i + 1) * SSD_CONV_DIM] = rows[T + i]

    hp0 = j * tiles
    grp = hp0 // (SSD_HEAD_DIM * (SSD_HEADS // SSD_GROUPS))
    for q in range(tiles):
        hp = pl.ds(hp0 + q, 1)
        h = h0_ref[:, q, :].T
        for t in range(T):
            h = h * dec_scr[t, hp, :] + bm_scr[t, grp] * xdt_scr[t, hp, :]
            y_scr[t, hp, :] = jnp.sum(h * cm_scr[t, grp], axis=0, keepdims=True)
        hout_ref[layer, :, q, :] = h.T

    @pl.when(j == pl.num_programs(0) - 1)
    def _():
        for t in range(T):
            y = y_scr[t].T + xs_scr[t] * dsk_ref[...]
            y = y * _silu(z_ref[t * B:(t + 1) * B, :])
            y_ref[t * B:(t + 1) * B, :] = _rms(y, ng_ref[...])


def _layer_state_specs(layer, batch, rows, width):
    cur = pl.BlockSpec((None, batch, rows, width), lambda j: (layer, 0, j, 0))
    prev = [pl.BlockSpec((layer, batch, rows, width), lambda j: (0, 0, j, 0))] if layer else []
    out = pl.BlockSpec((layer + 1, batch, rows, width), lambda j: (0, 0, j, 0))
    return cur, prev, out


def _ssd_step(z, xbc, dtr, conv0, h_all, h_done, lp, *, batch, seq, layer):
    n = batch * seq
    srows = SSD_HEADS * SSD_HEAD_DIM
    consts = (lp["conv_w"], lp["conv_b"], lp["dt_bias"], lp["a_neg_exp"], lp["d_skip"], lp["ssd_norm"], lp["head_expand"])
    hspec, prev_specs, hout_spec = _layer_state_specs(layer, batch, SSD_STEP_TILES, SSD_STATE)
    prev_args = [h_done] if layer else []
    cflat = (SSD_CONV - 1) * SSD_CONV_DIM
    y, conv_new, h_new = pl.pallas_call(
        functools.partial(_ssd_step_body, seq=seq, batch=batch, layer=layer),
        grid=(srows // SSD_STEP_TILES,),
        in_specs=[_full_spec((n, GROUP_WIDTH)), _full_spec((n, SSD_CONV_DIM)), _full_spec((n, LANES)),
                  _full_spec((batch, cflat)), hspec] + prev_specs + [_pspec(a) for a in consts],
        out_specs=[_full_spec((n, GROUP_WIDTH)), _full_spec((batch, cflat)), hout_spec],
        out_shape=[jax.ShapeDtypeStruct((n, GROUP_WIDTH), F32),
                   jax.ShapeDtypeStruct((batch, cflat), F32),
                   jax.ShapeDtypeStruct((layer + 1, batch, srows, SSD_STATE), F32)],
        scratch_shapes=[pltpu.VMEM((seq, batch, GROUP_WIDTH), F32),
                        pltpu.VMEM((seq, SSD_GROUPS, SSD_STATE, batch), F32),
                        pltpu.VMEM((seq, SSD_GROUPS, SSD_STATE, batch), F32),
                        pltpu.VMEM((seq, GROUP_WIDTH, batch), F32),
                        pltpu.VMEM((seq, GROUP_WIDTH, batch), F32),
                        pltpu.VMEM((seq, GROUP_WIDTH, batch), F32)],
        compiler_params=_cparams("arbitrary"),
        name="ssd_step",
    )(z, xbc, dtr, conv0.reshape(batch, cflat), h_all.reshape(h_all.shape[0], batch, srows, SSD_STATE),
      *prev_args, *[_parg(a) for a in consts])
    return y, conv_new.reshape(batch, SSD_CONV - 1, SSD_CONV_DIM), h_new


PAIR = 2 * RWKV_HEAD
RWKV_PAIRS = RWKV_HEADS // 2


def _bd(x):
    lane = lax.broadcasted_iota(jnp.int32, x.shape, 1)
    zero = jnp.zeros_like(x)
    return jnp.concatenate([jnp.where(lane < RWKV_HEAD, x, zero), jnp.where(lane >= RWKV_HEAD, x, zero)], axis=0)


def _half_sums(x, lo):
    s_lo = jnp.sum(jnp.where(lo, x, 0.0), axis=-1, keepdims=True)
    s_hi = jnp.sum(jnp.where(lo, 0.0, x), axis=-1, keepdims=True)
    return jnp.where(lo, s_lo, s_hi)


def _head_sum(x):
    lo = lax.broadcasted_iota(jnp.int32, (x.shape[0], PAIR), 1) < RWKV_HEAD
    return jnp.concatenate([_half_sums(x[:, p * PAIR:(p + 1) * PAIR], lo) for p in range(RWKV_PAIRS)], axis=-1)


def _rwkv_pointwise(u, prev, mu_ref, w0_ref, w2_ref, a0_ref, a2_ref, g2_ref, kk_ref, ka_ref):
    GW = GROUP_WIDTH
    xs = u + (prev - u) * mu_ref[...]
    r = xs[:, 0:GW]
    k = xs[:, GW:2 * GW]
    v = xs[:, 2 * GW:3 * GW]
    wd = xs[:, 3 * GW:3 * GW + 64]
    ad = xs[:, 3 * GW + 64:3 * GW + 128]
    gd = xs[:, 3 * GW + 128:3 * GW + 256]
    w_lin = w0_ref[...] + _dot(jnp.tanh(wd).astype(BF16), w2_ref[...])
    logdecay = -jnp.exp(-_softplus(-w_lin) - 0.5)
    a = _sigmoid(a0_ref[...] + _dot(ad.astype(BF16), a2_ref[...]))
    g = _dot(_sigmoid(gd).astype(BF16), g2_ref[...])
    kk = k * kk_ref[...]
    kk = kk / jnp.maximum(jnp.sqrt(_head_sum(kk * kk)), 1e-12)
    k = k * (1.0 + (a - 1.0) * ka_ref[...])
    return r, k, v, logdecay, a, g, kk


def _rwkv_finish(y, r, k, v, g, rk_ref, lng_ref, lnb_ref):
    mean = _head_sum(y) * (1.0 / RWKV_HEAD)
    yc = y - mean
    var = _head_sum(yc * yc) * (1.0 / RWKV_HEAD)
    y = yc * lax.rsqrt(var + RWKV_LN_EPS) * lng_ref[...] + lnb_ref[...]
    bonus = _head_sum(r * k * rk_ref[...]) * v
    return (y + bonus) * g


def _rwkv_body(u_ref, mu_ref, w0_ref, w2_ref, a0_ref, a2_ref, g2_ref, kk_ref, ka_ref, rk_ref,
               lng_ref, lnb_ref, y_ref, shift_ref, sout_ref, upad_scr, s_scr, *, chunk, group):
    L, G = chunk, group
    GL = G * L
    c = pl.program_id(1)
    pad = SUBLANES

    @pl.when(c == 0)
    def _():
        upad_scr[0:pad, :] = jnp.zeros((pad, RWKV_PROJ), F32)
        s_scr[...] = jnp.zeros(s_scr.shape, F32)

    u = u_ref[...]
    upad_scr[pad:pad + GL, :] = u
    prev = upad_scr[pad - 1:pad - 1 + GL, :]
    upad_scr[pad - 1:pad, :] = u[GL - 1:GL, :]
    r, k, v, logdecay, a, g, kk = _rwkv_pointwise(u, prev, mu_ref, w0_ref, w2_ref, a0_ref, a2_ref, g2_ref,
                                                  kk_ref, ka_ref)

    tril = jnp.where(lax.broadcasted_iota(jnp.int32, (L, L), 0) >= lax.broadcasted_iota(jnp.int32, (L, L), 1),
                     1.0, 0.0).astype(F32)
    cl = jnp.concatenate([_dot(tril, logdecay[i * L:(i + 1) * L, :], precision=HIGHEST) for i in range(G)], axis=0)
    e_in = jnp.exp(cl)
    e_inv = jnp.exp(-cl)
    r_t = r * e_in
    r_tb = r_t.astype(BF16)
    a_tb = (-kk * jnp.exp(cl - logdecay)).astype(BF16)
    b_tb = (kk * a * e_inv).astype(BF16)
    k_tb = (k * e_inv).astype(BF16)
    vb = v.astype(BF16)

    row = lax.broadcasted_iota(jnp.int32, (L, PAIR), 0)
    colh = lax.broadcasted_iota(jnp.int32, (L, PAIR), 1) & (RWKV_HEAD - 1)
    strict = row > colh
    incl = row >= colh
    eye_pair = jnp.where(row == colh, 1.0, 0.0).astype(F32)
    lane_lo = lax.broadcasted_iota(jnp.int32, (RWKV_HEAD, PAIR), 1) < RWKV_HEAD
    same_head = (lax.broadcasted_iota(jnp.int32, (PAIR, PAIR), 0) < RWKV_HEAD) == \
                (lax.broadcasted_iota(jnp.int32, (PAIR, PAIR), 1) < RWKV_HEAD)

    streams = [(i, p) for i in range(G) for p in range(RWKV_PAIRS)]
    ns = len(streams)
    blk = lambda x, i, p: x[i * L:(i + 1) * L, p * PAIR:(p + 1) * PAIR]
    lhs = [jnp.concatenate([blk(a_tb, i, p), blk(r_tb, i, p)], axis=0) for i, p in streams]
    m_ab = [_dot_nt(lhs[s], _bd(blk(b_tb, i, p))) for s, (i, p) in enumerate(streams)]
    m_ak = [_dot_nt(lhs[s], _bd(blk(k_tb, i, p))) for s, (i, p) in enumerate(streams)]
    n_ab = [jnp.where(strict, m[0:L], 0.0) for m in m_ab]
    m_rb = [jnp.where(incl, m[L:2 * L], 0.0).astype(BF16) for m in m_ab]
    n_ak = [jnp.where(strict, m[0:L], 0.0).astype(BF16) for m in m_ak]
    m_rk = [jnp.where(incl, m[L:2 * L], 0.0).astype(BF16) for m in m_ak]
    tinv = [eye_pair + n for n in n_ab]
    pwb = [n.astype(BF16) for n in n_ab]
    pw = [_dot(x, _bd(x)) for x in pwb]
    for _ in range(int(math.log2(L)) - 2):
        pwb = [x.astype(BF16) for x in pw]
        both = [_dot(jnp.concatenate([pwb[s], tinv[s].astype(BF16)], axis=0), _bd(pwb[s])) for s in range(ns)]
        pw = [x[0:L] for x in both]
        tinv = [tinv[s] + both[s][L:2 * L] for s in range(ns)]
    pwb = [x.astype(BF16) for x in pw]
    tinv = [tinv[s] + _dot(tinv[s].astype(BF16), _bd(pwb[s])) for s in range(ns)]
    tinvb = [x.astype(BF16) for x in tinv]
    nv_mv = [_dot(jnp.concatenate([n_ak[s], m_rk[s]], axis=0), _bd(blk(vb, i, p))) for s, (i, p) in enumerate(streams)]
    wu = [_dot(tinvb[s], jnp.concatenate([_bd(blk(a_tb, i, p)), _bd(nv_mv[s][0:L].astype(BF16))], axis=1))
          for s, (i, p) in enumerate(streams)]
    wub = [x.astype(BF16) for x in wu]
    qy = [_dot(m_rb[s], jnp.concatenate([_bd(wub[s][:, 0:PAIR]), _bd(wub[s][:, PAIR:2 * PAIR])], axis=1))
          for s in range(ns)]
    q = [(blk(r_t, i, p) + qy[s][:, 0:PAIR]).astype(BF16) for s, (i, p) in enumerate(streams)]
    y_loc = [qy[s][:, PAIR:2 * PAIR] + nv_mv[s][L:2 * L] for s in range(ns)]
    zeros_b = jnp.zeros((L, PAIR), BF16)
    mg = [_dot_tn(jnp.concatenate([wub[s], jnp.concatenate([zeros_b, blk(vb, i, p)], axis=1)], axis=0),
                  jnp.concatenate([blk(b_tb, i, p), blk(k_tb, i, p)], axis=0))
          for s, (i, p) in enumerate(streams)]
    p_end = [e_in[(i + 1) * L - 1:(i + 1) * L, p * PAIR:(p + 1) * PAIR] for i, p in streams]
    m_t = [(jnp.where(same_head, mg[s][0:PAIR], 0.0) * p_end[s]).astype(BF16) for s in range(ns)]
    g_t = [jnp.where(lane_lo, mg[s][PAIR:PAIR + RWKV_HEAD], mg[s][PAIR + RWKV_HEAD:2 * PAIR]) * p_end[s]
           for s in range(ns)]

    y_rows = []
    for i in range(G):
        y_pairs = []
        for p in range(RWKV_PAIRS):
            s = i * RWKV_PAIRS + p
            s0 = s_scr[p]
            s0b = s0.astype(BF16)
            y_pairs.append(_dot_nt(q[s], _bd(s0b)) + y_loc[s])
            s_scr[p] = s0 * p_end[s] + _dot(s0b, m_t[s]) + g_t[s]
        y_rows.append(jnp.concatenate(y_pairs, axis=-1))
    y = jnp.concatenate(y_rows, axis=0)
    y_ref[...] = _rwkv_finish(y, r, k, v, g, rk_ref, lng_ref, lnb_ref)

    @pl.when(c == pl.num_programs(1) - 1)
    def _():
        sout_ref[0] = s_scr[...]
        shift_ref[0] = upad_scr[pad - 1:pad, :]


_RWKV_PARAM_NAMES = ("mu", "w0", "w2", "a0", "a2", "g2", "k_k", "k_a", "r_k", "ln_g", "ln_b")


def _rwkv(u, p, *, batch, seq):
    rows = RWKV_CHUNK * RWKV_GROUP
    nc = seq // rows
    params = [p[n] for n in _RWKV_PARAM_NAMES]
    sspec = pl.BlockSpec((1, RWKV_PAIRS, RWKV_HEAD, PAIR), lambda b, c: (b, 0, 0, 0))
    y, shift, s_last = pl.pallas_call(
        functools.partial(_rwkv_body, chunk=RWKV_CHUNK, group=RWKV_GROUP),
        grid=(batch, nc),
        in_specs=[pl.BlockSpec((rows, RWKV_PROJ), lambda b, c: (b * nc + c, 0))] + [_pspec(a) for a in params],
        out_specs=[pl.BlockSpec((rows, GROUP_WIDTH), lambda b, c: (b * nc + c, 0)),
                   pl.BlockSpec((1, 1, RWKV_PROJ), lambda b, c: (b, 0, 0)), sspec],
        out_shape=[jax.ShapeDtypeStruct((batch * seq, GROUP_WIDTH), F32),
                   jax.ShapeDtypeStruct((batch, 1, RWKV_PROJ), F32),
                   jax.ShapeDtypeStruct((batch, RWKV_PAIRS, RWKV_HEAD, PAIR), F32)],
        scratch_shapes=[pltpu.VMEM((SUBLANES + rows, RWKV_PROJ), F32),
                        pltpu.VMEM((RWKV_PAIRS, RWKV_HEAD, PAIR), F32)],
        compiler_params=_cparams("parallel", "arbitrary"),
        name="rwkv",
    )(u, *[_parg(a) for a in params])
    s_last = s_last.reshape(batch, RWKV_PAIRS, RWKV_HEAD, 2, RWKV_HEAD).transpose(0, 1, 3, 2, 4).reshape(
        batch, RWKV_HEADS, RWKV_HEAD, RWKV_HEAD)
    return y, shift.reshape(batch, RWKV_PROJ), s_last


def _rwkv_step_body(u_ref, shift0_ref, s0_ref, *rest, seq, batch, layer):
    sdone_ref, rest = (rest[0], rest[1:]) if layer else (None, rest)
    (mu_ref, w0_ref, w2_ref, a0_ref, a2_ref, g2_ref, kk_ref, ka_ref, rk_ref, lng_ref, lnb_ref, y_ref, sout_ref,
     r_scr, w_scr, k_scr, b_scr, nkk_scr, v_scr, y_scr) = rest
    T, B = seq, batch
    j = pl.program_id(0)
    if layer:
        sout_ref[0:layer] = sdone_ref[...]
    tiles = RWKV_STEP_TILES

    def pointwise(t):
        u = u_ref[t * B:(t + 1) * B, :]
        prev = shift0_ref[...] if t == 0 else u_ref[(t - 1) * B:t * B, :]
        return _rwkv_pointwise(u, prev, mu_ref, w0_ref, w2_ref, a0_ref, a2_ref, g2_ref, kk_ref, ka_ref)

    @pl.when(j == 0)
    def _():
        for t in range(T):
            r, k, v, logdecay, a, _, kk = pointwise(t)
            r_scr[t] = r.T
            w_scr[t] = jnp.exp(logdecay).T
            k_scr[t] = k.T
            b_scr[t] = (kk * a).T
            nkk_scr[t] = (-kk).T
            v_scr[t] = v.T

    i0 = j * tiles
    keys = pl.ds(pl.multiple_of((i0 // RWKV_HEAD) * RWKV_HEAD, RWKV_HEAD), RWKV_HEAD)
    for q in range(tiles):
        vi = pl.ds(i0 + q, 1)
        s = s0_ref[:, q, :].T
        for t in range(T):
            sa = jnp.sum(s * nkk_scr[t, keys, :], axis=0, keepdims=True)
            s = s * w_scr[t, keys, :] + k_scr[t, keys, :] * v_scr[t, vi, :] + b_scr[t, keys, :] * sa
            y_scr[t, vi, :] = jnp.sum(s * r_scr[t, keys, :], axis=0, keepdims=True)
        sout_ref[layer, :, q, :] = s.T

    @pl.when(j == pl.num_programs(0) - 1)
    def _():
        for t in range(T):
            r, k, v, _, _, g, _ = pointwise(t)
            y_ref[t * B:(t + 1) * B, :] = _rwkv_finish(y_scr[t].T, r, k, v, g, rk_ref, lng_ref, lnb_ref)


def _rwkv_step(u, shift0, s_all, s_done, p, *, batch, seq, layer):
    n = batch * seq
    srows = RWKV_HEADS * RWKV_HEAD
    params = [p[nm] for nm in _RWKV_PARAM_NAMES]
    sspec, prev_specs, sout_spec = _layer_state_specs(layer, batch, RWKV_STEP_TILES, RWKV_HEAD)
    prev_args = [s_done] if layer else []
    tposed = pltpu.VMEM((seq, GROUP_WIDTH, batch), F32)
    return pl.pallas_call(
        functools.partial(_rwkv_step_body, seq=seq, batch=batch, layer=layer),
        grid=(srows // RWKV_STEP_TILES,),
        in_specs=[_full_spec((n, RWKV_PROJ)), _full_spec((batch, RWKV_PROJ)), sspec] + prev_specs
                 + [_pspec(a) for a in params],
        out_specs=[_full_spec((n, GROUP_WIDTH)), sout_spec],
        out_shape=[jax.ShapeDtypeStruct((n, GROUP_WIDTH), F32),
                   jax.ShapeDtypeStruct((layer + 1, batch, srows, RWKV_HEAD), F32)],
        scratch_shapes=[tposed] * 7,
        compiler_params=_cparams("arbitrary"),
        name="rwkv_step",
    )(u, shift0, s_all.reshape(s_all.shape[0], batch, srows, RWKV_HEAD), *prev_args, *[_parg(a) for a in params])


def _s5_body(u_ref, hre0_ref, him0_ref, are_ref, aim_ref, bmat_ref, cmat_ref, d_ref, gw_ref, gb_ref,
             y_ref, hre_ref, him_ref, hs_scr, tm_scr, *, steps, batch_major):
    c = pl.program_id(1)
    ns = S5_WIDTH
    bsub = SUBLANES

    @pl.when(c == 0)
    def _():
        hre_ref[...] = hre0_ref[...]
        him_ref[...] = him0_ref[...]

    if batch_major:
        for b in range(bsub):
            tm_scr[:, b, :] = u_ref[b]
        u = tm_scr[...].reshape(steps * bsub, GROUP_WIDTH)
    else:
        u = u_ref[...].reshape(steps * bsub, GROUP_WIDTH)
    hs_scr[...] = _dot(u.astype(BF16), bmat_ref[...])
    are = jnp.broadcast_to(are_ref[...], (bsub, ns))
    aim = jnp.broadcast_to(aim_ref[...], (bsub, ns))

    def step(t, carry):
        hre, him = carry
        r0 = pl.multiple_of(t * bsub, bsub)
        nre = are * hre - aim * him + hs_scr[pl.ds(r0, bsub), 0:ns]
        nim = are * him + aim * hre + hs_scr[pl.ds(r0, bsub), ns:2 * ns]
        hs_scr[pl.ds(r0, bsub), 0:ns] = nre
        hs_scr[pl.ds(r0, bsub), ns:2 * ns] = nim
        return nre, nim

    hre, him = lax.fori_loop(0, steps, step, (hre_ref[...], him_ref[...]))
    hre_ref[...] = hre
    him_ref[...] = him
    y = _dot(hs_scr[...].astype(BF16), cmat_ref[...]) + u * d_ref[...]
    y = _gelu_tanh(y)
    yy = _dot(y.astype(BF16), gw_ref[...]) + gb_ref[...]
    out = (yy[:, 0:GROUP_WIDTH] * _sigmoid(yy[:, GROUP_WIDTH:2 * GROUP_WIDTH])).reshape(steps, bsub, GROUP_WIDTH)
    if batch_major:
        tm_scr[...] = out
        for b in range(bsub):
            y_ref[b] = tm_scr[:, b, :]
    else:
        y_ref[...] = out


def _time_specs(u, batch_major):
    bsub = SUBLANES
    if batch_major:
        batch, seq, _ = u.shape
        steps = min(TM_CHUNK, seq)
        spec = pl.BlockSpec((bsub, steps, GROUP_WIDTH), lambda b, c: (b, c, 0))
    else:
        seq, batch, _ = u.shape
        steps = min(TM_CHUNK, seq)
        spec = pl.BlockSpec((steps, bsub, GROUP_WIDTH), lambda b, c: (c, b, 0))
    return batch, seq, steps, spec


def _s5(u, hre0, him0, lp, *, batch_major):
    batch, seq, steps, tspec = _time_specs(u, batch_major)
    bsub = SUBLANES
    hspec = pl.BlockSpec((bsub, S5_WIDTH), lambda b, c: (b, 0))
    consts = (lp["s5_are"], lp["s5_aim"], lp["s5_bmat"], lp["s5_cmat"], lp["s5_d"], lp["s5_gw"], lp["s5_gb"])
    return pl.pallas_call(
        functools.partial(_s5_body, steps=steps, batch_major=batch_major),
        grid=(batch // bsub, seq // steps),
        in_specs=[tspec, hspec, hspec] + [_pspec(a) for a in consts],
        out_specs=[tspec, hspec, hspec],
        out_shape=[jax.ShapeDtypeStruct(u.shape, F32),
                   jax.ShapeDtypeStruct((batch, S5_WIDTH), F32),
                   jax.ShapeDtypeStruct((batch, S5_WIDTH), F32)],
        scratch_shapes=[pltpu.VMEM((steps * bsub, 2 * S5_WIDTH), F32),
                        pltpu.VMEM((steps, bsub, GROUP_WIDTH), F32)],
        compiler_params=_cparams("parallel", "arbitrary"),
        name="s5",
    )(u, hre0, him0, *[_parg(a) for a in consts])


def _pool_body(u_ref, buf0_ref, pw_ref, sc_ref, y_ref, buf_ref, f_scr, tm_scr, *, steps, pos0, batch_major):
    c = pl.program_id(1)
    bsub = SUBLANES
    GW = GROUP_WIDTH
    halo = POOL_BUF + 1

    @pl.when(c == 0)
    def _():
        f_scr[0] = jnp.zeros((bsub, GW), F32)
        for i in range(POOL_BUF):
            f_scr[1 + i] = buf0_ref[:, i * GW:(i + 1) * GW]

    if batch_major:
        for b in range(bsub):
            f_scr[halo:halo + steps, b, :] = u_ref[b]
    else:
        f_scr[halo:halo + steps] = u_ref[...]
    f = f_scr[...]
    u = f[halo:halo + steps]
    s2 = f[1:] + f[:-1]
    s4 = s2[2:] + s2[:-2]
    s8 = s4[4:] + s4[:-4]
    s16 = s8[8:] + s8[:-8]
    f_scr[0:halo] = f[steps:steps + halo]
    lane = lax.broadcasted_iota(jnp.int32, (steps, bsub, GW), 2)
    tpos = lax.broadcasted_iota(jnp.int32, (steps, bsub, GW), 0) + (pos0 + 1) + c * steps
    win = jnp.where(lane < POOL_CH, s2[halo - 1:halo - 1 + steps],
                    jnp.where(lane < 2 * POOL_CH, s4[halo - 3:halo - 3 + steps],
                              jnp.where(lane < 3 * POOL_CH, s8[halo - 7:halo - 7 + steps],
                                        s16[halo - 15:halo - 15 + steps])))
    wlen = jnp.where(lane < POOL_CH, POOL_WINDOWS[0],
                     jnp.where(lane < 2 * POOL_CH, POOL_WINDOWS[1],
                               jnp.where(lane < 3 * POOL_CH, POOL_WINDOWS[2], POOL_WINDOWS[3])))
    cnt = jnp.minimum(tpos, wlen).astype(F32)
    pooled = (win / cnt - u).reshape(steps * bsub, GW)
    y = (_dot(pooled.astype(BF16), pw_ref[...]) * sc_ref[...]).reshape(steps, bsub, GW)
    if batch_major:
        tm_scr[...] = y
        for b in range(bsub):
            y_ref[b] = tm_scr[:, b, :]
    else:
        y_ref[...] = y

    @pl.when(c == pl.num_programs(1) - 1)
    def _():
        for i in range(POOL_BUF):
            buf_ref[:, i * GW:(i + 1) * GW] = f_scr[1 + i]


def _pool(u, buf0, lp, *, pos0, batch_major):
    batch, seq, steps, tspec = _time_specs(u, batch_major)
    bsub = SUBLANES
    flat = POOL_BUF * GROUP_WIDTH
    bspec = pl.BlockSpec((bsub, flat), lambda b, c: (b, 0))
    y, buf = pl.pallas_call(
        functools.partial(_pool_body, steps=steps, pos0=pos0, batch_major=batch_major),
        grid=(batch // bsub, seq // steps),
        in_specs=[tspec, bspec, _pspec(lp["pool_w"]), _pspec(lp["pool_scale"])],
        out_specs=[tspec, bspec],
        out_shape=[jax.ShapeDtypeStruct(u.shape, F32), jax.ShapeDtypeStruct((batch, flat), F32)],
        scratch_shapes=[pltpu.VMEM((POOL_BUF + 1 + steps, bsub, GROUP_WIDTH), F32),
                        pltpu.VMEM((steps, bsub, GROUP_WIDTH), F32)],
        compiler_params=_cparams("parallel", "arbitrary"),
        name="pool",
    )(u, buf0.reshape(batch, flat), _parg(lp["pool_w"]), _parg(lp["pool_scale"]))
    return y, buf.reshape(batch, POOL_BUF, GROUP_WIDTH)


def _block_diag(blocks):
    n, g, r, c = blocks.shape
    eye = jnp.eye(g, dtype=blocks.dtype)
    return (eye[None, :, None, :, None] * blocks[:, :, :, None, :]).reshape(n, g * r, g * c)


def _stacked_params(P):
    row = lambda a: a.reshape(a.shape[0], 1, -1)
    pad_lanes = lambda a: jnp.pad(a, ((0, 0), (0, LANES - a.shape[1])))
    bf = lambda a: a.astype(BF16)
    w_in = P["w_in"]
    split = GROUP_WIDTH + SSD_CONV_DIM
    w_all = jnp.concatenate([w_in[:, :, :split], w_in[:, :, split + SSD_HEADS:],
                             jnp.pad(w_in[:, :, split:split + SSD_HEADS], ((0, 0), (0, 0), (0, LANES - SSD_HEADS)))],
                            axis=2)

    lam = lax.complex(P["s5_lam_re"], P["s5_lam_im"])
    a_bar = jnp.exp(lam * jnp.exp(P["s5_log_step"])[..., None])
    b_bar = ((a_bar - 1.0) / lam)[..., None] * lax.complex(P["s5_b_re"], P["s5_b_im"])
    b_t = jnp.swapaxes(b_bar, 2, 3)
    bmat = jnp.concatenate([_block_diag(jnp.real(b_t)), _block_diag(jnp.imag(b_t))], axis=2)
    c_t = jnp.swapaxes(lax.complex(P["s5_c_re"], P["s5_c_im"]), 2, 3)
    cmat = jnp.concatenate([_block_diag(jnp.real(c_t)), -_block_diag(jnp.imag(c_t))], axis=1)

    out = dict(
        norm_ffn1=row(P["norm_ffn1"]), ffn1_in=bf(P["ffn1_in"]), ffn1_out=bf(P["ffn1_out"]),
        norm_mix=row(P["norm_mix"]), w_all=bf(w_all),
        conv_w=P["ssd_conv_w"], conv_b=row(P["ssd_conv_b"]),
        dt_bias=row(pad_lanes(P["ssd_dt_bias"])), a_log=row(pad_lanes(P["ssd_a_log"])),
        a_neg_exp=row(jnp.repeat(-jnp.exp(P["ssd_a_log"]), SSD_HEAD_DIM, axis=1)),
        d_skip=row(jnp.repeat(P["ssd_d"], SSD_HEAD_DIM, axis=1)), ssd_norm=row(P["ssd_norm"]),
        s5_are=row(jnp.real(a_bar)), s5_aim=row(jnp.imag(a_bar)), s5_bmat=bf(bmat), s5_cmat=bf(cmat),
        s5_d=row(P["s5_d"]), s5_gw=bf(P["s5_glu_w"]), s5_gb=row(P["s5_glu_b"]),
        pool_w=bf(_block_diag(P["pool_w"])), pool_scale=row(P["pool_scale"]),
        w_out=bf(P["w_out"]),
        norm_ffn2=row(P["norm_ffn2"]), ffn2_in=bf(P["ffn2_in"]), ffn2_out=bf(P["ffn2_out"]),
    )
    for name in _RWKV_PARAM_NAMES:
        a = P["rwkv_" + name]
        out["rwkv_" + name] = bf(a) if name in ("w2", "a2", "g2") else row(a)
    return out


def _layer_params(stacked, l):
    lp = {k: _Layered((v, l)) for k, v in stacked.items()}
    lp["rwkv"] = {n: lp["rwkv_" + n] for n in _RWKV_PARAM_NAMES}
    lp["head_expand"] = jnp.pad(jnp.repeat(jnp.eye(SSD_HEADS, dtype=F32), SSD_HEAD_DIM, axis=1),
                                ((0, LANES - SSD_HEADS), (0, 0)))
    return lp


def _mixers_prompt(lp, proj, *, batch, seq):
    z, xbc, ur, us5, upool, dtr = proj
    y_ssd, conv_new, ssd_new = _ssd(z, xbc, dtr, lp, batch=batch, seq=seq)
    y_rwkv, shift_new, rwkv_new = _rwkv(ur, lp["rwkv"], batch=batch, seq=seq)
    zeros = jnp.zeros((batch, S5_WIDTH), F32)
    bm = lambda a: a.reshape(batch, seq, a.shape[-1])
    rows = lambda a: a.reshape(batch * seq, a.shape[-1])
    y_s5, s5re, s5im = _s5(bm(us5), zeros, zeros, lp, batch_major=True)
    y_pool, pool_new = _pool(bm(upool), jnp.zeros((batch, POOL_BUF, GROUP_WIDTH), F32), lp, pos0=0,
                             batch_major=True)
    ys = (y_ssd, y_rwkv, rows(y_s5), rows(y_pool))
    states = (conv_new, ssd_new, shift_new, rwkv_new, s5re.reshape(batch, S5_GROUPS, S5_STATE),
              s5im.reshape(batch, S5_GROUPS, S5_STATE), pool_new)
    return ys, states


def _mixers_decode(lp, proj, states, done, *, batch, seq, layer):
    z, xbc, ur, us5, upool, dtr = proj
    conv0, shift0, s5re0, s5im0, pool0 = (states[i][layer] for i in (0, 2, 4, 5, 6))
    ssd_done, rwkv_done = (done[1], done[3]) if layer else (None, None)
    y_ssd, conv_new, ssd_new = _ssd_step(z, xbc, dtr, conv0, states[1], ssd_done, lp, batch=batch, seq=seq,
                                         layer=layer)
    y_rwkv, rwkv_new = _rwkv_step(ur, shift0, states[3], rwkv_done, lp["rwkv"], batch=batch, seq=seq, layer=layer)
    shift_new = ur[(seq - 1) * batch:, :]
    tm = lambda a: a.reshape(seq, batch, a.shape[-1])
    y_s5, s5re, s5im = _s5(tm(us5), s5re0.reshape(batch, S5_WIDTH), s5im0.reshape(batch, S5_WIDTH), lp,
                           batch_major=False)
    y_pool, pool_new = _pool(tm(upool), pool0, lp, pos0=PAST_LEN, batch_major=False)
    rows = lambda a: a.reshape(seq * batch, a.shape[-1])
    ys = (y_ssd, y_rwkv, rows(y_s5), rows(y_pool))
    new_states = (conv_new, ssd_new, shift_new, rwkv_new, s5re.reshape(batch, S5_GROUPS, S5_STATE),
                  s5im.reshape(batch, S5_GROUPS, S5_STATE), pool_new)
    return ys, new_states


_WIDTHS = (GROUP_WIDTH, SSD_CONV_DIM, RWKV_PROJ, GROUP_WIDTH, GROUP_WIDTH, LANES)


def _trunk(x, layer_params, norm_final, mixers):
    states = []
    mix, lp = None, None
    for l, lp_next in enumerate(layer_params):
        if l > 0:
            x = _ffn(x, lp["norm_ffn2"], lp["ffn2_in"], lp["ffn2_out"], mix=mix, wmix=lp["w_out"])
        lp = lp_next
        x = _ffn(x, lp["norm_ffn1"], lp["ffn1_in"], lp["ffn1_out"])
        proj = _inproj(x, lp["norm_mix"], lp["w_all"], _WIDTHS)
        mix, st = mixers(l, lp, proj, states[-1] if states else None)
        states.append(st)
    x = _ffn(x, lp["norm_ffn2"], lp["ffn2_in"], lp["ffn2_out"], mix=mix, wmix=lp["w_out"], gf=norm_final)
    return x, states


def kernel(x_prompt, x_sample, state_ssd_conv, state_ssd, state_rwkv_shift, state_rwkv, state_s5_re, state_s5_im, state_pool, norm_ffn1, ffn1_in, ffn1_out, norm_mix, w_in, ssd_conv_w, ssd_conv_b, ssd_dt_bias, ssd_a_log, ssd_d, ssd_norm, rwkv_mu, rwkv_w0, rwkv_w2, rwkv_a0, rwkv_a2, rwkv_g2, rwkv_k_k, rwkv_k_a, rwkv_r_k, rwkv_ln_g, rwkv_ln_b, s5_lam_re, s5_lam_im, s5_log_step, s5_b_re, s5_b_im, s5_c_re, s5_c_im, s5_d, s5_glu_w, s5_glu_b, pool_w, pool_scale, w_out, norm_ffn2, ffn2_in, ffn2_out, norm_final):
    P = dict(norm_ffn1=norm_ffn1, ffn1_in=ffn1_in, ffn1_out=ffn1_out, norm_mix=norm_mix, w_in=w_in,
             ssd_conv_w=ssd_conv_w, ssd_conv_b=ssd_conv_b, ssd_dt_bias=ssd_dt_bias, ssd_a_log=ssd_a_log,
             ssd_d=ssd_d, ssd_norm=ssd_norm, rwkv_mu=rwkv_mu, rwkv_w0=rwkv_w0, rwkv_w2=rwkv_w2, rwkv_a0=rwkv_a0,
             rwkv_a2=rwkv_a2, rwkv_g2=rwkv_g2, rwkv_k_k=rwkv_k_k, rwkv_k_a=rwkv_k_a,
             rwkv_r_k=rwkv_r_k.reshape(rwkv_r_k.shape[0], -1), rwkv_ln_g=rwkv_ln_g, rwkv_ln_b=rwkv_ln_b,
             s5_lam_re=s5_lam_re, s5_lam_im=s5_lam_im, s5_log_step=s5_log_step, s5_b_re=s5_b_re, s5_b_im=s5_b_im,
             s5_c_re=s5_c_re, s5_c_im=s5_c_im, s5_d=s5_d, s5_glu_w=s5_glu_w, s5_glu_b=s5_glu_b, pool_w=pool_w,
             pool_scale=pool_scale, w_out=w_out, norm_ffn2=norm_ffn2, ffn2_in=ffn2_in, ffn2_out=ffn2_out)
    depth = norm_ffn1.shape[0]
    bp, tp, d = x_prompt.shape
    bs, ts, _ = x_sample.shape
    stacked = _stacked_params(P)
    layer_params = [_layer_params(stacked, l) for l in range(depth)]
    gf = norm_final.reshape(1, -1)
    sample_states = (state_ssd_conv, state_ssd, state_rwkv_shift, state_rwkv, state_s5_re, state_s5_im, state_pool)

    y_p, st_p = _trunk(x_prompt.reshape(bp * tp, d), layer_params, gf,
                       lambda l, lp, proj, done: _mixers_prompt(lp, proj, batch=bp, seq=tp))
    x_s = jnp.swapaxes(x_sample, 0, 1).reshape(ts * bs, d)
    y_s, st_s = _trunk(x_s, layer_params, gf,
                       lambda l, lp, proj, done: _mixers_decode(lp, proj, sample_states, done,
                                                                batch=bs, seq=ts, layer=l))
    outs = [y_p.reshape(bp, tp, d), jnp.swapaxes(y_s.reshape(ts, bs, d), 0, 1)]
    for i, ref_state in enumerate(sample_states):
        outs.append(jnp.stack([st[i] for st in st_p]))
        if i in (1, 3):
            outs.append(st_s[-1][i].reshape(ref_state.shape))
        else:
            outs.append(jnp.stack([st[i] for st in st_s]))
    return tuple(outs)
```

```python
import functools
import math

import jax
import jax.numpy as jnp
from jax import lax
from jax.experimental import pallas as pl
from jax.experimental.pallas import tpu as pltpu

F32 = jnp.float32
BF16 = jnp.bfloat16
HIGHEST = lax.Precision.HIGHEST

SUBLANES = 8
LANES = 128
VMEM_LIMIT_BYTES = 56 * 1024 * 1024

GROUP_WIDTH = 256
SSD_HEAD_DIM = 64
SSD_HEADS = 4
SSD_GROUPS = 2
SSD_STATE = 128
SSD_CONV = 4
SSD_CONV_DIM = GROUP_WIDTH + 2 * SSD_GROUPS * SSD_STATE
SSD_CHUNK = 128
RWKV_HEAD = 64
RWKV_HEADS = 4
RWKV_PROJ = 1024
RWKV_LN_EPS = 64e-5
RWKV_CHUNK = 64
RWKV_GROUP = 8
S5_GROUP_CH = 16
S5_GROUPS = 16
S5_STATE = 64
S5_WIDTH = S5_GROUPS * S5_STATE
POOL_WINDOWS = (2, 4, 8, 16)
POOL_CH = 64
POOL_BUF = 15
RMS_EPS = 1e-6
PAST_LEN = 16384

ROW_TILE = 512
FFN_CHUNK = 256
TM_CHUNK = 64
SSD_STEP_TILES = 16
RWKV_STEP_TILES = 16


def _cparams(*sem):
    return pltpu.CompilerParams(dimension_semantics=sem, vmem_limit_bytes=VMEM_LIMIT_BYTES)


def _dot(a, b, **kw):
    return jnp.dot(a, b, preferred_element_type=F32, **kw)


def _dot_nt(a, b):
    return lax.dot_general(a, b, (((1,), (1,)), ((), ())), preferred_element_type=F32)


def _dot_tn(a, b):
    return lax.dot_general(a, b, (((0,), (0,)), ((), ())), preferred_element_type=F32)


def _sigmoid(x):
    return 1.0 / (1.0 + jnp.exp(-x))


def _silu(x):
    return x * _sigmoid(x)


def _softplus(x):
    return jnp.maximum(x, 0.0) + jnp.log(1.0 + jnp.exp(-jnp.abs(x)))


def _gelu_tanh(x):
    c = math.sqrt(2.0 / math.pi)
    return x * (0.5 * (1.0 + jnp.tanh(c * (x + 0.044715 * (x * x * x)))))


def _rms(x, g):
    return x * lax.rsqrt(jnp.mean(x * x, axis=-1, keepdims=True) + RMS_EPS) * g


def _full_spec(shape):
    n = len(shape)
    return pl.BlockSpec(shape, lambda *_: (0,) * n)


class _Layered(tuple):
    pass


def _pspec(p):
    if isinstance(p, _Layered):
        a, l = p
        return pl.BlockSpec((None,) + a.shape[1:], lambda *_: (l,) + (0,) * (a.ndim - 1))
    return _full_spec(p.shape)


def _parg(p):
    return p[0] if isinstance(p, _Layered) else p


def _ffn_body(*refs, has_mix, final_norm):
    it = iter(refs)
    x_ref = next(it)
    x = x_ref[...]
    if has_mix:
        y_refs = [next(it) for _ in range(4)]
        wmix_ref = next(it)
        for i, y_ref in enumerate(y_refs):
            x = x + _dot(y_ref[...].astype(BF16), wmix_ref[i * GROUP_WIDTH:(i + 1) * GROUP_WIDTH, :])
    g_ref, wi_ref, wo_ref = next(it), next(it), next(it)
    gf_ref = next(it) if final_norm else None
    o_ref = next(it)
    h = _rms(x, g_ref[...]).astype(BF16)
    d_ff = wo_ref.shape[0]
    acc = jnp.zeros_like(x)
    for c in range(d_ff // FFN_CHUNK):
        lo = c * FFN_CHUNK
        gate = _dot(h, wi_ref[:, lo:lo + FFN_CHUNK])
        up = _dot(h, wi_ref[:, d_ff + lo:d_ff + lo + FFN_CHUNK])
        act = (_silu(gate) * up).astype(BF16)
        acc = acc + _dot(act, wo_ref[lo:lo + FFN_CHUNK, :])
    x = x + 0.5 * acc
    if final_norm:
        x = _rms(x, gf_ref[...])
    o_ref[...] = x


def _ffn(x, g, wi, wo, mix=None, wmix=None, gf=None):
    rows, d = x.shape
    row_spec = lambda w: pl.BlockSpec((ROW_TILE, w), lambda i: (i, 0))
    args, specs = [x], [row_spec(d)]
    if mix is not None:
        for y in mix:
            args.append(y)
            specs.append(row_spec(y.shape[1]))
        args.append(_parg(wmix))
        specs.append(_pspec(wmix))
    for a in (g, wi, wo) + ((gf,) if gf is not None else ()):
        args.append(_parg(a))
        specs.append(_pspec(a))
    return pl.pallas_call(
        functools.partial(_ffn_body, has_mix=mix is not None, final_norm=gf is not None),
        grid=(rows // ROW_TILE,),
        in_specs=specs,
        out_specs=row_spec(d),
        out_shape=jax.ShapeDtypeStruct((rows, d), F32),
        compiler_params=_cparams("parallel"),
        name="ffn",
    )(*args)


def _inproj_body(x_ref, g_ref, w_ref, *o_refs):
    h = _rms(x_ref[...], g_ref[...]).astype(BF16)
    off = 0
    for o_ref in o_refs:
        n = o_ref.shape[-1]
        o_ref[...] = _dot(h, w_ref[:, off:off + n])
        off += n


def _inproj(x, g, w, widths):
    rows, d = x.shape
    row_spec = lambda w_: pl.BlockSpec((ROW_TILE, w_), lambda i: (i, 0))
    return pl.pallas_call(
        _inproj_body,
        grid=(rows // ROW_TILE,),
        in_specs=[row_spec(d), _pspec(g), _pspec(w)],
        out_specs=[row_spec(n) for n in widths],
        out_shape=[jax.ShapeDtypeStruct((rows, n), F32) for n in widths],
        compiler_params=_cparams("parallel"),
        name="inproj",
    )(x, _parg(g), _parg(w))


def _ssd_body(z_ref, xbc_ref, dt_ref, cw_ref, cb_ref, dtb_ref, alog_ref, dsk_ref, ng_ref,
              y_ref, conv_ref, hout_ref, xpad_scr, h_scr, *, chunk):
    L = chunk
    c = pl.program_id(1)
    pad = SUBLANES
    halo = SSD_CONV - 1

    @pl.when(c == 0)
    def _():
        xpad_scr[0:pad, :] = jnp.zeros((pad, SSD_CONV_DIM), F32)
        h_scr[...] = jnp.zeros(h_scr.shape, F32)

    xpad_scr[pad:pad + L, :] = xbc_ref[...]
    conv = cb_ref[...] + cw_ref[0:1, :] * xpad_scr[pad - halo:pad - halo + L, :]
    for j in range(1, SSD_CONV):
        conv = conv + cw_ref[j:j + 1, :] * xpad_scr[pad - halo + j:pad - halo + j + L, :]
    xpad_scr[pad - halo:pad, :] = xpad_scr[pad + L - halo:pad + L, :]
    conv = _silu(conv)
    xs = conv[:, 0:GROUP_WIDTH]
    bm = conv[:, GROUP_WIDTH:2 * GROUP_WIDTH].astype(BF16)
    cm = conv[:, 2 * GROUP_WIDTH:3 * GROUP_WIDTH].astype(BF16)

    row = lax.broadcasted_iota(jnp.int32, (L, L), 0)
    col = lax.broadcasted_iota(jnp.int32, (L, L), 1)
    causal = row >= col
    dt = _softplus(dt_ref[...] + dtb_ref[...])
    da = dt * (-jnp.exp(alog_ref[...]))
    acs = _dot(jnp.where(causal, 1.0, 0.0).astype(F32), da, precision=HIGHEST)
    acs_t = acs.T
    e_acs = jnp.exp(acs)
    acs_last = acs[L - 1:L, :]
    e_end = jnp.exp(acs_last - acs)
    e_last = jnp.exp(acs_last)

    ys = []
    for h in range(SSD_HEADS):
        g = h // (SSD_HEADS // SSD_GROUPS)
        bg = bm[:, g * SSD_STATE:(g + 1) * SSD_STATE]
        cg = cm[:, g * SSD_STATE:(g + 1) * SSD_STATE]
        x_h = xs[:, h * SSD_HEAD_DIM:(h + 1) * SSD_HEAD_DIM]
        xdt = x_h * dt[:, h:h + 1]
        seg = acs[:, h:h + 1] - acs_t[h:h + 1, :]
        decay = jnp.exp(jnp.where(causal, seg, -jnp.inf))
        scores = _dot_nt(cg, bg) * decay
        y_h = _dot(scores.astype(BF16), xdt.astype(BF16))
        h_prev = h_scr[h]
        y_h = y_h + _dot_nt(cg, h_prev.astype(BF16)) * e_acs[:, h:h + 1]
        st = _dot_tn((xdt * e_end[:, h:h + 1]).astype(BF16), bg)
        h_scr[h] = h_prev * e_last[:, h:h + 1] + st
        ys.append(y_h)
    y = jnp.concatenate(ys, axis=-1) + xs * dsk_ref[...]
    y = y * _silu(z_ref[...])
    y_ref[...] = _rms(y, ng_ref[...])

    @pl.when(c == pl.num_programs(1) - 1)
    def _():
        hout_ref[0] = h_scr[...]
        conv_ref[0] = xpad_scr[pad - halo:pad, :]


def _ssd(z, xbc, dtr, lp, *, batch, seq):
    chunk = SSD_CHUNK
    nc = seq // chunk
    rspec = lambda w: pl.BlockSpec((chunk, w), lambda b, c: (b * nc + c, 0))
    consts = (lp["conv_w"], lp["conv_b"], lp["dt_bias"], lp["a_log"], lp["d_skip"], lp["ssd_norm"])
    return pl.pallas_call(
        functools.partial(_ssd_body, chunk=chunk),
        grid=(batch, nc),
        in_specs=[rspec(GROUP_WIDTH), rspec(SSD_CONV_DIM), rspec(LANES)] + [_pspec(a) for a in consts],
        out_specs=[rspec(GROUP_WIDTH),
                   pl.BlockSpec((1, SSD_CONV - 1, SSD_CONV_DIM), lambda b, c: (b, 0, 0)),
                   pl.BlockSpec((1, SSD_HEADS, SSD_HEAD_DIM, SSD_STATE), lambda b, c: (b, 0, 0, 0))],
        out_shape=[jax.ShapeDtypeStruct((batch * seq, GROUP_WIDTH), F32),
                   jax.ShapeDtypeStruct((batch, SSD_CONV - 1, SSD_CONV_DIM), F32),
                   jax.ShapeDtypeStruct((batch, SSD_HEADS, SSD_HEAD_DIM, SSD_STATE), F32)],
        scratch_shapes=[pltpu.VMEM((SUBLANES + chunk, SSD_CONV_DIM), F32),
                        pltpu.VMEM((SSD_HEADS, SSD_HEAD_DIM, SSD_STATE), F32)],
        compiler_params=_cparams("parallel", "arbitrary"),
        name="ssd",
    )(z, xbc, dtr, *[_parg(a) for a in consts])


def _ssd_step_body(z_ref, xbc_ref, dt_ref, conv0_ref, h0_ref, *rest, seq, batch, layer):
    hdone_ref, rest = (rest[0], rest[1:]) if layer else (None, rest)
    (cw_ref, cb_ref, dtb_ref, aneg_ref, dsk_ref, ng_ref, hexp_ref, y_ref, conv_ref, hout_ref,
     xs_scr, bm_scr, cm_scr, xdt_scr, dec_scr, y_scr) = rest
    T, B = seq, batch
    if layer:
        hout_ref[0:layer] = hdone_ref[...]
    GW = GROUP_WIDTH
    j = pl.program_id(0)
    tiles = SSD_STEP_TILES

    @pl.when(j == 0)
    def _():
        rows = [conv0_ref[i] for i in range(SSD_CONV - 1)]
        rows += [xbc_ref[t * B:(t + 1) * B, :] for t in range(T)]
        for t in range(T):
            conv = cb_ref[...] + cw_ref[0:1, :] * rows[t]
            for i in range(1, SSD_CONV):
                conv = conv + cw_ref[i:i + 1, :] * rows[t + i]
            conv = _silu(conv)
            xs = conv[:, 0:GW]
            xs_scr[t] = xs
            for g in range(SSD_GROUPS):
                bm_scr[t, g] = conv[:, GW + g * SSD_STATE:GW + (g + 1) * SSD_STATE].T
                cm_scr[t, g] = conv[:, 2 * GW + g * SSD_STATE:2 * GW + (g + 1) * SSD_STATE].T
            dt = _softplus(dt_ref[t * B:(t + 1) * B, :] + dtb_ref[...])
            dte = _dot(dt, hexp_ref[...], precision=HIGHEST)
            xdt_scr[t] = (xs * dte).T
            dec_scr[t] = jnp.exp(dte * aneg_ref[...]).T
        for i in range(SSD_CONV - 1):
            conv_ref[i] = rows[T + i]

    hp0 = j * tiles
    grp = hp0 // (SSD_HEAD_DIM * (SSD_HEADS // SSD_GROUPS))
    for q in range(tiles):
        hp = pl.ds(hp0 + q, 1)
        h = h0_ref[:, q, :].T
        for t in range(T):
            h = h * dec_scr[t, hp, :] + bm_scr[t, grp] * xdt_scr[t, hp, :]
            y_scr[t, hp, :] = jnp.sum(h * cm_scr[t, grp], axis=0, keepdims=True)
        hout_ref[layer, :, q, :] = h.T

    @pl.when(j == pl.num_programs(0) - 1)
    def _():
        for t in range(T):
            y = y_scr[t].T + xs_scr[t] * dsk_ref[...]
            y = y * _silu(z_ref[t * B:(t + 1) * B, :])
            y_ref[t * B:(t + 1) * B, :] = _rms(y, ng_ref[...])


def _layer_state_specs(layer, block, axis):
    idx = lambda first: (lambda j: (first,) + tuple(j if a == axis else 0 for a in range(len(block))))
    cur = pl.BlockSpec((None,) + block, idx(layer))
    prev = [pl.BlockSpec((layer,) + block, idx(0))] if layer else []
    out = pl.BlockSpec((layer + 1,) + block, idx(0))
    return cur, prev, out


def _ssd_step(z, xbc, dtr, conv_all, h_all, h_done, lp, *, batch, seq, layer):
    n = batch * seq
    srows = SSD_HEADS * SSD_HEAD_DIM
    consts = (lp["conv_w"], lp["conv_b"], lp["dt_bias"], lp["a_neg_exp"], lp["d_skip"], lp["ssd_norm"], lp["head_expand"])
    hspec, prev_specs, hout_spec = _layer_state_specs(layer, (batch, SSD_STEP_TILES, SSD_STATE), 1)
    prev_args = [h_done] if layer else []
    cshape = (SSD_CONV - 1, batch, SSD_CONV_DIM)
    return pl.pallas_call(
        functools.partial(_ssd_step_body, seq=seq, batch=batch, layer=layer),
        grid=(srows // SSD_STEP_TILES,),
        in_specs=[_full_spec((n, GROUP_WIDTH)), _full_spec((n, SSD_CONV_DIM)), _full_spec((n, LANES)),
                  pl.BlockSpec((None,) + cshape, lambda j: (layer, 0, 0, 0)), hspec] + prev_specs
                 + [_pspec(a) for a in consts],
        out_specs=[_full_spec((n, GROUP_WIDTH)), _full_spec(cshape), hout_spec],
        out_shape=[jax.ShapeDtypeStruct((n, GROUP_WIDTH), F32),
                   jax.ShapeDtypeStruct(cshape, F32),
                   jax.ShapeDtypeStruct((layer + 1, batch, srows, SSD_STATE), F32)],
        scratch_shapes=[pltpu.VMEM((seq, batch, GROUP_WIDTH), F32),
                        pltpu.VMEM((seq, SSD_GROUPS, SSD_STATE, batch), F32),
                        pltpu.VMEM((seq, SSD_GROUPS, SSD_STATE, batch), F32),
                        pltpu.VMEM((seq, GROUP_WIDTH, batch), F32),
                        pltpu.VMEM((seq, GROUP_WIDTH, batch), F32),
                        pltpu.VMEM((seq, GROUP_WIDTH, batch), F32)],
        compiler_params=_cparams("arbitrary"),
        name="ssd_step",
    )(z, xbc, dtr, conv_all, h_all.reshape(h_all.shape[0], batch, srows, SSD_STATE),
      *prev_args, *[_parg(a) for a in consts])


PAIR = 2 * RWKV_HEAD
RWKV_PAIRS = RWKV_HEADS // 2


def _bd(x):
    lane = lax.broadcasted_iota(jnp.int32, x.shape, 1)
    zero = jnp.zeros_like(x)
    return jnp.concatenate([jnp.where(lane < RWKV_HEAD, x, zero), jnp.where(lane >= RWKV_HEAD, x, zero)], axis=0)


def _half_sums(x, lo):
    s_lo = jnp.sum(jnp.where(lo, x, 0.0), axis=-1, keepdims=True)
    s_hi = jnp.sum(jnp.where(lo, 0.0, x), axis=-1, keepdims=True)
    return jnp.where(lo, s_lo, s_hi)


def _head_sum(x):
    lo = lax.broadcasted_iota(jnp.int32, (x.shape[0], PAIR), 1) < RWKV_HEAD
    return jnp.concatenate([_half_sums(x[:, p * PAIR:(p + 1) * PAIR], lo) for p in range(RWKV_PAIRS)], axis=-1)


def _rwkv_pointwise(u, prev, mu_ref, w0_ref, w2_ref, a0_ref, a2_ref, g2_ref, kk_ref, ka_ref):
    GW = GROUP_WIDTH
    xs = u + (prev - u) * mu_ref[...]
    r = xs[:, 0:GW]
    k = xs[:, GW:2 * GW]
    v = xs[:, 2 * GW:3 * GW]
    wd = xs[:, 3 * GW:3 * GW + 64]
    ad = xs[:, 3 * GW + 64:3 * GW + 128]
    gd = xs[:, 3 * GW + 128:3 * GW + 256]
    w_lin = w0_ref[...] + _dot(jnp.tanh(wd).astype(BF16), w2_ref[...])
    logdecay = -jnp.exp(-_softplus(-w_lin) - 0.5)
    a = _sigmoid(a0_ref[...] + _dot(ad.astype(BF16), a2_ref[...]))
    g = _dot(_sigmoid(gd).astype(BF16), g2_ref[...])
    kk = k * kk_ref[...]
    kk = kk / jnp.maximum(jnp.sqrt(_head_sum(kk * kk)), 1e-12)
    k = k * (1.0 + (a - 1.0) * ka_ref[...])
    return r, k, v, logdecay, a, g, kk


def _rwkv_finish(y, r, k, v, g, rk_ref, lng_ref, lnb_ref):
    mean = _head_sum(y) * (1.0 / RWKV_HEAD)
    yc = y - mean
    var = _head_sum(yc * yc) * (1.0 / RWKV_HEAD)
    y = yc * lax.rsqrt(var + RWKV_LN_EPS) * lng_ref[...] + lnb_ref[...]
    bonus = _head_sum(r * k * rk_ref[...]) * v
    return (y + bonus) * g


def _rwkv_body(u_ref, mu_ref, w0_ref, w2_ref, a0_ref, a2_ref, g2_ref, kk_ref, ka_ref, rk_ref,
               lng_ref, lnb_ref, y_ref, shift_ref, sout_ref, upad_scr, s_scr, *, chunk, group):
    L, G = chunk, group
    GL = G * L
    c = pl.program_id(1)
    pad = SUBLANES

    @pl.when(c == 0)
    def _():
        upad_scr[0:pad, :] = jnp.zeros((pad, RWKV_PROJ), F32)
        s_scr[...] = jnp.zeros(s_scr.shape, F32)

    u = u_ref[...]
    upad_scr[pad:pad + GL, :] = u
    prev = upad_scr[pad - 1:pad - 1 + GL, :]
    upad_scr[pad - 1:pad, :] = u[GL - 1:GL, :]
    r, k, v, logdecay, a, g, kk = _rwkv_pointwise(u, prev, mu_ref, w0_ref, w2_ref, a0_ref, a2_ref, g2_ref,
                                                  kk_ref, ka_ref)

    tril = jnp.where(lax.broadcasted_iota(jnp.int32, (L, L), 0) >= lax.broadcasted_iota(jnp.int32, (L, L), 1),
                     1.0, 0.0).astype(F32)
    cl = jnp.concatenate([_dot(tril, logdecay[i * L:(i + 1) * L, :], precision=HIGHEST) for i in range(G)], axis=0)
    e_in = jnp.exp(cl)
    e_inv = jnp.exp(-cl)
    r_t = r * e_in
    r_tb = r_t.astype(BF16)
    a_tb = (-kk * jnp.exp(cl - logdecay)).astype(BF16)
    b_tb = (kk * a * e_inv).astype(BF16)
    k_tb = (k * e_inv).astype(BF16)
    vb = v.astype(BF16)

    row = lax.broadcasted_iota(jnp.int32, (L, PAIR), 0)
    colh = lax.broadcasted_iota(jnp.int32, (L, PAIR), 1) & (RWKV_HEAD - 1)
    strict = row > colh
    incl = row >= colh
    eye_pair = jnp.where(row == colh, 1.0, 0.0).astype(F32)
    lane_lo = lax.broadcasted_iota(jnp.int32, (RWKV_HEAD, PAIR), 1) < RWKV_HEAD
    same_head = (lax.broadcasted_iota(jnp.int32, (PAIR, PAIR), 0) < RWKV_HEAD) == \
                (lax.broadcasted_iota(jnp.int32, (PAIR, PAIR), 1) < RWKV_HEAD)

    streams = [(i, p) for i in range(G) for p in range(RWKV_PAIRS)]
    ns = len(streams)
    blk = lambda x, i, p: x[i * L:(i + 1) * L, p * PAIR:(p + 1) * PAIR]
    lhs = [jnp.concatenate([blk(a_tb, i, p), blk(r_tb, i, p)], axis=0) for i, p in streams]
    m_ab = [_dot_nt(lhs[s], _bd(blk(b_tb, i, p))) for s, (i, p) in enumerate(streams)]
    m_ak = [_dot_nt(lhs[s], _bd(blk(k_tb, i, p))) for s, (i, p) in enumerate(streams)]
    n_ab = [jnp.where(strict, m[0:L], 0.0) for m in m_ab]
    m_rb = [jnp.where(incl, m[L:2 * L], 0.0).astype(BF16) for m in m_ab]
    n_ak = [jnp.where(strict, m[0:L], 0.0).astype(BF16) for m in m_ak]
    m_rk = [jnp.where(incl, m[L:2 * L], 0.0).astype(BF16) for m in m_ak]
    tinv = [eye_pair + n for n in n_ab]
    pwb = [n.astype(BF16) for n in n_ab]
    pw = [_dot(x, _bd(x)) for x in pwb]
    for _ in range(int(math.log2(L)) - 2):
        pwb = [x.astype(BF16) for x in pw]
        both = [_dot(jnp.concatenate([pwb[s], tinv[s].astype(BF16)], axis=0), _bd(pwb[s])) for s in range(ns)]
        pw = [x[0:L] for x in both]
        tinv = [tinv[s] + both[s][L:2 * L] for s in range(ns)]
    pwb = [x.astype(BF16) for x in pw]
    tinv = [tinv[s] + _dot(tinv[s].astype(BF16), _bd(pwb[s])) for s in range(ns)]
    tinvb = [x.astype(BF16) for x in tinv]
    nv_mv = [_dot(jnp.concatenate([n_ak[s], m_rk[s]], axis=0), _bd(blk(vb, i, p))) for s, (i, p) in enumerate(streams)]
    wu = [_dot(tinvb[s], jnp.concatenate([_bd(blk(a_tb, i, p)), _bd(nv_mv[s][0:L].astype(BF16))], axis=1))
          for s, (i, p) in enumerate(streams)]
    wub = [x.astype(BF16) for x in wu]
    qy = [_dot(m_rb[s], jnp.concatenate([_bd(wub[s][:, 0:PAIR]), _bd(wub[s][:, PAIR:2 * PAIR])], axis=1))
          for s in range(ns)]
    q = [(blk(r_t, i, p) + qy[s][:, 0:PAIR]).astype(BF16) for s, (i, p) in enumerate(streams)]
    y_loc = [qy[s][:, PAIR:2 * PAIR] + nv_mv[s][L:2 * L] for s in range(ns)]
    zeros_b = jnp.zeros((L, PAIR), BF16)
    mg = [_dot_tn(jnp.concatenate([wub[s], jnp.concatenate([zeros_b, blk(vb, i, p)], axis=1)], axis=0),
                  jnp.concatenate([blk(b_tb, i, p), blk(k_tb, i, p)], axis=0))
          for s, (i, p) in enumerate(streams)]
    p_end = [e_in[(i + 1) * L - 1:(i + 1) * L, p * PAIR:(p + 1) * PAIR] for i, p in streams]
    m_t = [(jnp.where(same_head, mg[s][0:PAIR], 0.0) * p_end[s]).astype(BF16) for s in range(ns)]
    g_t = [jnp.where(lane_lo, mg[s][PAIR:PAIR + RWKV_HEAD], mg[s][PAIR + RWKV_HEAD:2 * PAIR]) * p_end[s]
           for s in range(ns)]

    y_rows = []
    for i in range(G):
        y_pairs = []
        for p in range(RWKV_PAIRS):
            s = i * RWKV_PAIRS + p
            s0 = s_scr[p]
            s0b = s0.astype(BF16)
            y_pairs.append(_dot_nt(q[s], _bd(s0b)) + y_loc[s])
            s_scr[p] = s0 * p_end[s] + _dot(s0b, m_t[s]) + g_t[s]
        y_rows.append(jnp.concatenate(y_pairs, axis=-1))
    y = jnp.concatenate(y_rows, axis=0)
    y_ref[...] = _rwkv_finish(y, r, k, v, g, rk_ref, lng_ref, lnb_ref)

    @pl.when(c == pl.num_programs(1) - 1)
    def _():
        sout_ref[0] = s_scr[...]
        shift_ref[0] = upad_scr[pad - 1:pad, :]


_RWKV_PARAM_NAMES = ("mu", "w0", "w2", "a0", "a2", "g2", "k_k", "k_a", "r_k", "ln_g", "ln_b")


def _rwkv(u, p, *, batch, seq):
    rows = RWKV_CHUNK * RWKV_GROUP
    nc = seq // rows
    params = [p[n] for n in _RWKV_PARAM_NAMES]
    sspec = pl.BlockSpec((1, RWKV_PAIRS, RWKV_HEAD, PAIR), lambda b, c: (b, 0, 0, 0))
    y, shift, s_last = pl.pallas_call(
        functools.partial(_rwkv_body, chunk=RWKV_CHUNK, group=RWKV_GROUP),
        grid=(batch, nc),
        in_specs=[pl.BlockSpec((rows, RWKV_PROJ), lambda b, c: (b * nc + c, 0))] + [_pspec(a) for a in params],
        out_specs=[pl.BlockSpec((rows, GROUP_WIDTH), lambda b, c: (b * nc + c, 0)),
                   pl.BlockSpec((1, 1, RWKV_PROJ), lambda b, c: (b, 0, 0)), sspec],
        out_shape=[jax.ShapeDtypeStruct((batch * seq, GROUP_WIDTH), F32),
                   jax.ShapeDtypeStruct((batch, 1, RWKV_PROJ), F32),
                   jax.ShapeDtypeStruct((batch, RWKV_PAIRS, RWKV_HEAD, PAIR), F32)],
        scratch_shapes=[pltpu.VMEM((SUBLANES + rows, RWKV_PROJ), F32),
                        pltpu.VMEM((RWKV_PAIRS, RWKV_HEAD, PAIR), F32)],
        compiler_params=_cparams("parallel", "arbitrary"),
        name="rwkv",
    )(u, *[_parg(a) for a in params])
    s_last = s_last.reshape(batch, RWKV_PAIRS, RWKV_HEAD, 2, RWKV_HEAD).transpose(0, 1, 3, 2, 4).reshape(
        batch, RWKV_HEADS, RWKV_HEAD, RWKV_HEAD)
    return y, shift.reshape(batch, RWKV_PROJ), s_last


def _rwkv_step_body(u_ref, shift0_ref, s0_ref, *rest, seq, batch, layer):
    sdone_ref, rest = (rest[0], rest[1:]) if layer else (None, rest)
    (mu_ref, w0_ref, w2_ref, a0_ref, a2_ref, g2_ref, kk_ref, ka_ref, rk_ref, lng_ref, lnb_ref, y_ref, sout_ref,
     r_scr, w_scr, k_scr, b_scr, nkk_scr, v_scr, y_scr) = rest
    T, B = seq, batch
    j = pl.program_id(0)
    if layer:
        sout_ref[0:layer] = sdone_ref[...]
    tiles = RWKV_STEP_TILES

    def pointwise(t):
        u = u_ref[t * B:(t + 1) * B, :]
        prev = shift0_ref[...] if t == 0 else u_ref[(t - 1) * B:t * B, :]
        return _rwkv_pointwise(u, prev, mu_ref, w0_ref, w2_ref, a0_ref, a2_ref, g2_ref, kk_ref, ka_ref)

    @pl.when(j == 0)
    def _():
        for t in range(T):
            r, k, v, logdecay, a, _, kk = pointwise(t)
            r_scr[t] = r.T
            w_scr[t] = jnp.exp(logdecay).T
            k_scr[t] = k.T
            b_scr[t] = (kk * a).T
            nkk_scr[t] = (-kk).T
            v_scr[t] = v.T

    i0 = j * tiles
    keys = pl.ds(pl.multiple_of((i0 // RWKV_HEAD) * RWKV_HEAD, RWKV_HEAD), RWKV_HEAD)
    for q in range(tiles):
        vi = pl.ds(i0 + q, 1)
        s = s0_ref[q]
        for t in range(T):
            sa = jnp.sum(s * nkk_scr[t, keys, :], axis=0, keepdims=True)
            s = s * w_scr[t, keys, :] + k_scr[t, keys, :] * v_scr[t, vi, :] + b_scr[t, keys, :] * sa
            y_scr[t, vi, :] = jnp.sum(s * r_scr[t, keys, :], axis=0, keepdims=True)
        sout_ref[layer, q] = s

    @pl.when(j == pl.num_programs(0) - 1)
    def _():
        for t in range(T):
            r, k, v, _, _, g, _ = pointwise(t)
            y_ref[t * B:(t + 1) * B, :] = _rwkv_finish(y_scr[t].T, r, k, v, g, rk_ref, lng_ref, lnb_ref)


def _rwkv_step(u, shift0, s_all, s_done, p, *, batch, seq, layer):
    n = batch * seq
    srows = RWKV_HEADS * RWKV_HEAD
    params = [p[nm] for nm in _RWKV_PARAM_NAMES]
    sspec, prev_specs, sout_spec = _layer_state_specs(layer, (RWKV_STEP_TILES, RWKV_HEAD, batch), 0)
    prev_args = [s_done] if layer else []
    tposed = pltpu.VMEM((seq, GROUP_WIDTH, batch), F32)
    return pl.pallas_call(
        functools.partial(_rwkv_step_body, seq=seq, batch=batch, layer=layer),
        grid=(srows // RWKV_STEP_TILES,),
        in_specs=[_full_spec((n, RWKV_PROJ)), _full_spec((batch, RWKV_PROJ)), sspec] + prev_specs
                 + [_pspec(a) for a in params],
        out_specs=[_full_spec((n, GROUP_WIDTH)), sout_spec],
        out_shape=[jax.ShapeDtypeStruct((n, GROUP_WIDTH), F32),
                   jax.ShapeDtypeStruct((layer + 1, srows, RWKV_HEAD, batch), F32)],
        scratch_shapes=[tposed] * 7,
        compiler_params=_cparams("arbitrary"),
        name="rwkv_step",
    )(u, shift0, s_all, *prev_args, *[_parg(a) for a in params])


def _s5_body(u_ref, hre0_ref, him0_ref, are_ref, aim_ref, bmat_ref, cmat_ref, d_ref, gw_ref, gb_ref,
             y_ref, hre_ref, him_ref, hs_scr, tm_scr, *, steps, batch_major):
    c = pl.program_id(1)
    ns = S5_WIDTH
    bsub = SUBLANES

    @pl.when(c == 0)
    def _():
        hre_ref[...] = hre0_ref[...]
        him_ref[...] = him0_ref[...]

    if batch_major:
        for b in range(bsub):
            tm_scr[:, b, :] = u_ref[b]
        u = tm_scr[...].reshape(steps * bsub, GROUP_WIDTH)
    else:
        u = u_ref[...].reshape(steps * bsub, GROUP_WIDTH)
    hs_scr[...] = _dot(u.astype(BF16), bmat_ref[...])
    are = jnp.broadcast_to(are_ref[...], (bsub, ns))
    aim = jnp.broadcast_to(aim_ref[...], (bsub, ns))

    def step(t, carry):
        hre, him = carry
        r0 = pl.multiple_of(t * bsub, bsub)
        nre = are * hre - aim * him + hs_scr[pl.ds(r0, bsub), 0:ns]
        nim = are * him + aim * hre + hs_scr[pl.ds(r0, bsub), ns:2 * ns]
        hs_scr[pl.ds(r0, bsub), 0:ns] = nre
        hs_scr[pl.ds(r0, bsub), ns:2 * ns] = nim
        return nre, nim

    hre, him = lax.fori_loop(0, steps, step, (hre_ref[...], him_ref[...]))
    hre_ref[...] = hre
    him_ref[...] = him
    y = _dot(hs_scr[...].astype(BF16), cmat_ref[...]) + u * d_ref[...]
    y = _gelu_tanh(y)
    yy = _dot(y.astype(BF16), gw_ref[...]) + gb_ref[...]
    out = (yy[:, 0:GROUP_WIDTH] * _sigmoid(yy[:, GROUP_WIDTH:2 * GROUP_WIDTH])).reshape(steps, bsub, GROUP_WIDTH)
    if batch_major:
        tm_scr[...] = out
        for b in range(bsub):
            y_ref[b] = tm_scr[:, b, :]
    else:
        y_ref[...] = out


def _time_specs(u, batch_major):
    bsub = SUBLANES
    if batch_major:
        batch, seq, _ = u.shape
        steps = min(TM_CHUNK, seq)
        spec = pl.BlockSpec((bsub, steps, GROUP_WIDTH), lambda b, c: (b, c, 0))
    else:
        seq, batch, _ = u.shape
        steps = min(TM_CHUNK, seq)
        spec = pl.BlockSpec((steps, bsub, GROUP_WIDTH), lambda b, c: (c, b, 0))
    return batch, seq, steps, spec


def _s5(u, hre0, him0, lp, *, batch_major):
    batch, seq, steps, tspec = _time_specs(u, batch_major)
    bsub = SUBLANES
    hspec = pl.BlockSpec((bsub, S5_WIDTH), lambda b, c: (b, 0))
    consts = (lp["s5_are"], lp["s5_aim"], lp["s5_bmat"], lp["s5_cmat"], lp["s5_d"], lp["s5_gw"], lp["s5_gb"])
    return pl.pallas_call(
        functools.partial(_s5_body, steps=steps, batch_major=batch_major),
        grid=(batch // bsub, seq // steps),
        in_specs=[tspec, hspec, hspec] + [_pspec(a) for a in consts],
        out_specs=[tspec, hspec, hspec],
        out_shape=[jax.ShapeDtypeStruct(u.shape, F32),
                   jax.ShapeDtypeStruct((batch, S5_WIDTH), F32),
                   jax.ShapeDtypeStruct((batch, S5_WIDTH), F32)],
        scratch_shapes=[pltpu.VMEM((steps * bsub, 2 * S5_WIDTH), F32),
                        pltpu.VMEM((steps, bsub, GROUP_WIDTH), F32)],
        compiler_params=_cparams("parallel", "arbitrary"),
        name="s5",
    )(u, hre0, him0, *[_parg(a) for a in consts])


def _pool_body(u_ref, buf0_ref, pw_ref, sc_ref, y_ref, buf_ref, f_scr, tm_scr, *, steps, pos0, batch_major):
    c = pl.program_id(1)
    bsub = SUBLANES
    GW = GROUP_WIDTH
    halo = POOL_BUF + 1

    @pl.when(c == 0)
    def _():
        f_scr[0] = jnp.zeros((bsub, GW), F32)
        f_scr[1:halo] = buf0_ref[...]

    if batch_major:
        for b in range(bsub):
            f_scr[halo:halo + steps, b, :] = u_ref[b]
    else:
        f_scr[halo:halo + steps] = u_ref[...]
    f = f_scr[...]
    u = f[halo:halo + steps]
    s2 = f[1:] + f[:-1]
    s4 = s2[2:] + s2[:-2]
    s8 = s4[4:] + s4[:-4]
    s16 = s8[8:] + s8[:-8]
    f_scr[0:halo] = f[steps:steps + halo]
    lane = lax.broadcasted_iota(jnp.int32, (steps, bsub, GW), 2)
    tpos = lax.broadcasted_iota(jnp.int32, (steps, bsub, GW), 0) + (pos0 + 1) + c * steps
    win = jnp.where(lane < POOL_CH, s2[halo - 1:halo - 1 + steps],
                    jnp.where(lane < 2 * POOL_CH, s4[halo - 3:halo - 3 + steps],
                              jnp.where(lane < 3 * POOL_CH, s8[halo - 7:halo - 7 + steps],
                                        s16[halo - 15:halo - 15 + steps])))
    wlen = jnp.where(lane < POOL_CH, POOL_WINDOWS[0],
                     jnp.where(lane < 2 * POOL_CH, POOL_WINDOWS[1],
                               jnp.where(lane < 3 * POOL_CH, POOL_WINDOWS[2], POOL_WINDOWS[3])))
    cnt = jnp.minimum(tpos, wlen).astype(F32)
    pooled = (win / cnt - u).reshape(steps * bsub, GW)
    y = (_dot(pooled.astype(BF16), pw_ref[...]) * sc_ref[...]).reshape(steps, bsub, GW)
    if batch_major:
        tm_scr[...] = y
        for b in range(bsub):
            y_ref[b] = tm_scr[:, b, :]
    else:
        y_ref[...] = y

    @pl.when(c == pl.num_programs(1) - 1)
    def _():
        buf_ref[...] = f_scr[1:halo]


def _pool(u, buf0, lp, *, pos0, batch_major, layer=None):
    batch, seq, steps, tspec = _time_specs(u, batch_major)
    bsub = SUBLANES
    bblock = (POOL_BUF, bsub, GROUP_WIDTH)
    bspec = pl.BlockSpec(bblock, lambda b, c: (0, b, 0))
    if layer is None:
        bspec_in = bspec
    else:
        bspec_in = pl.BlockSpec((None,) + bblock, lambda b, c: (layer, 0, b, 0))
    return pl.pallas_call(
        functools.partial(_pool_body, steps=steps, pos0=pos0, batch_major=batch_major),
        grid=(batch // bsub, seq // steps),
        in_specs=[tspec, bspec_in, _pspec(lp["pool_w"]), _pspec(lp["pool_scale"])],
        out_specs=[tspec, bspec],
        out_shape=[jax.ShapeDtypeStruct(u.shape, F32), jax.ShapeDtypeStruct((POOL_BUF, batch, GROUP_WIDTH), F32)],
        scratch_shapes=[pltpu.VMEM((POOL_BUF + 1 + steps, bsub, GROUP_WIDTH), F32),
                        pltpu.VMEM((steps, bsub, GROUP_WIDTH), F32)],
        compiler_params=_cparams("parallel", "arbitrary"),
        name="pool",
    )(u, buf0, _parg(lp["pool_w"]), _parg(lp["pool_scale"]))


def _block_diag(blocks):
    n, g, r, c = blocks.shape
    eye = jnp.eye(g, dtype=blocks.dtype)
    return (eye[None, :, None, :, None] * blocks[:, :, :, None, :]).reshape(n, g * r, g * c)


def _stacked_params(P):
    row = lambda a: a.reshape(a.shape[0], 1, -1)
    pad_lanes = lambda a: jnp.pad(a, ((0, 0), (0, LANES - a.shape[1])))
    bf = lambda a: a.astype(BF16)
    w_in = P["w_in"]
    split = GROUP_WIDTH + SSD_CONV_DIM
    w_all = jnp.concatenate([w_in[:, :, :split], w_in[:, :, split + SSD_HEADS:],
                             jnp.pad(w_in[:, :, split:split + SSD_HEADS], ((0, 0), (0, 0), (0, LANES - SSD_HEADS)))],
                            axis=2)

    lam = lax.complex(P["s5_lam_re"], P["s5_lam_im"])
    a_bar = jnp.exp(lam * jnp.exp(P["s5_log_step"])[..., None])
    b_bar = ((a_bar - 1.0) / lam)[..., None] * lax.complex(P["s5_b_re"], P["s5_b_im"])
    b_t = jnp.swapaxes(b_bar, 2, 3)
    bmat = jnp.concatenate([_block_diag(jnp.real(b_t)), _block_diag(jnp.imag(b_t))], axis=2)
    c_t = jnp.swapaxes(lax.complex(P["s5_c_re"], P["s5_c_im"]), 2, 3)
    cmat = jnp.concatenate([_block_diag(jnp.real(c_t)), -_block_diag(jnp.imag(c_t))], axis=1)

    out = dict(
        norm_ffn1=row(P["norm_ffn1"]), ffn1_in=bf(P["ffn1_in"]), ffn1_out=bf(P["ffn1_out"]),
        norm_mix=row(P["norm_mix"]), w_all=bf(w_all),
        conv_w=P["ssd_conv_w"], conv_b=row(P["ssd_conv_b"]),
        dt_bias=row(pad_lanes(P["ssd_dt_bias"])), a_log=row(pad_lanes(P["ssd_a_log"])),
        a_neg_exp=row(jnp.repeat(-jnp.exp(P["ssd_a_log"]), SSD_HEAD_DIM, axis=1)),
        d_skip=row(jnp.repeat(P["ssd_d"], SSD_HEAD_DIM, axis=1)), ssd_norm=row(P["ssd_norm"]),
        s5_are=row(jnp.real(a_bar)), s5_aim=row(jnp.imag(a_bar)), s5_bmat=bf(bmat), s5_cmat=bf(cmat),
        s5_d=row(P["s5_d"]), s5_gw=bf(P["s5_glu_w"]), s5_gb=row(P["s5_glu_b"]),
        pool_w=bf(_block_diag(P["pool_w"])), pool_scale=row(P["pool_scale"]),
        w_out=bf(P["w_out"]),
        norm_ffn2=row(P["norm_ffn2"]), ffn2_in=bf(P["ffn2_in"]), ffn2_out=bf(P["ffn2_out"]),
    )
    for name in _RWKV_PARAM_NAMES:
        a = P["rwkv_" + name]
        out["rwkv_" + name] = bf(a) if name in ("w2", "a2", "g2") else row(a)
    return out


def _layer_params(stacked, l):
    lp = {k: _Layered((v, l)) for k, v in stacked.items()}
    lp["rwkv"] = {n: lp["rwkv_" + n] for n in _RWKV_PARAM_NAMES}
    lp["head_expand"] = jnp.pad(jnp.repeat(jnp.eye(SSD_HEADS, dtype=F32), SSD_HEAD_DIM, axis=1),
                                ((0, LANES - SSD_HEADS), (0, 0)))
    return lp


def _mixers_prompt(lp, proj, *, batch, seq):
    z, xbc, ur, us5, upool, dtr = proj
    y_ssd, conv_new, ssd_new = _ssd(z, xbc, dtr, lp, batch=batch, seq=seq)
    y_rwkv, shift_new, rwkv_new = _rwkv(ur, lp["rwkv"], batch=batch, seq=seq)
    zeros = jnp.zeros((batch, S5_WIDTH), F32)
    bm = lambda a: a.reshape(batch, seq, a.shape[-1])
    rows = lambda a: a.reshape(batch * seq, a.shape[-1])
    y_s5, s5re, s5im = _s5(bm(us5), zeros, zeros, lp, batch_major=True)
    y_pool, pool_new = _pool(bm(upool), jnp.zeros((POOL_BUF, batch, GROUP_WIDTH), F32), lp, pos0=0,
                             batch_major=True)
    ys = (y_ssd, y_rwkv, rows(y_s5), rows(y_pool))
    states = (conv_new, ssd_new, shift_new, rwkv_new, s5re.reshape(batch, S5_GROUPS, S5_STATE),
              s5im.reshape(batch, S5_GROUPS, S5_STATE), jnp.swapaxes(pool_new, 0, 1))
    return ys, states


def _mixers_decode(lp, proj, states, done, *, batch, seq, layer):
    z, xbc, ur, us5, upool, dtr = proj
    shift0, s5re0, s5im0 = (states[i][layer] for i in (2, 4, 5))
    ssd_done, rwkv_done = (done[1], done[3]) if layer else (None, None)
    y_ssd, conv_new, ssd_new = _ssd_step(z, xbc, dtr, states[0], states[1], ssd_done, lp, batch=batch, seq=seq,
                                         layer=layer)
    y_rwkv, rwkv_new = _rwkv_step(ur, shift0, states[3], rwkv_done, lp["rwkv"], batch=batch, seq=seq, layer=layer)
    shift_new = ur[(seq - 1) * batch:, :]
    tm = lambda a: a.reshape(seq, batch, a.shape[-1])
    y_s5, s5re, s5im = _s5(tm(us5), s5re0.reshape(batch, S5_WIDTH), s5im0.reshape(batch, S5_WIDTH), lp,
                           batch_major=False)
    y_pool, pool_new = _pool(tm(upool), states[6], lp, pos0=PAST_LEN, batch_major=False, layer=layer)
    rows = lambda a: a.reshape(seq * batch, a.shape[-1])
    ys = (y_ssd, y_rwkv, rows(y_s5), rows(y_pool))
    new_states = (jnp.swapaxes(conv_new, 0, 1), ssd_new, shift_new, rwkv_new,
                  s5re.reshape(batch, S5_GROUPS, S5_STATE), s5im.reshape(batch, S5_GROUPS, S5_STATE),
                  jnp.swapaxes(pool_new, 0, 1))
    return ys, new_states


_WIDTHS = (GROUP_WIDTH, SSD_CONV_DIM, RWKV_PROJ, GROUP_WIDTH, GROUP_WIDTH, LANES)


def _trunk(x, layer_params, norm_final, mixers):
    states = []
    mix, lp = None, None
    for l, lp_next in enumerate(layer_params):
        if l > 0:
            x = _ffn(x, lp["norm_ffn2"], lp["ffn2_in"], lp["ffn2_out"], mix=mix, wmix=lp["w_out"])
        lp = lp_next
        x = _ffn(x, lp["norm_ffn1"], lp["ffn1_in"], lp["ffn1_out"])
        proj = _inproj(x, lp["norm_mix"], lp["w_all"], _WIDTHS)
        mix, st = mixers(l, lp, proj, states[-1] if states else None)
        states.append(st)
    x = _ffn(x, lp["norm_ffn2"], lp["ffn2_in"], lp["ffn2_out"], mix=mix, wmix=lp["w_out"], gf=norm_final)
    return x, states


def kernel(x_prompt, x_sample, state_ssd_conv, state_ssd, state_rwkv_shift, state_rwkv, state_s5_re, state_s5_im, state_pool, norm_ffn1, ffn1_in, ffn1_out, norm_mix, w_in, ssd_conv_w, ssd_conv_b, ssd_dt_bias, ssd_a_log, ssd_d, ssd_norm, rwkv_mu, rwkv_w0, rwkv_w2, rwkv_a0, rwkv_a2, rwkv_g2, rwkv_k_k, rwkv_k_a, rwkv_r_k, rwkv_ln_g, rwkv_ln_b, s5_lam_re, s5_lam_im, s5_log_step, s5_b_re, s5_b_im, s5_c_re, s5_c_im, s5_d, s5_glu_w, s5_glu_b, pool_w, pool_scale, w_out, norm_ffn2, ffn2_in, ffn2_out, norm_final):
    P = dict(norm_ffn1=norm_ffn1, ffn1_in=ffn1_in, ffn1_out=ffn1_out, norm_mix=norm_mix, w_in=w_in,
             ssd_conv_w=ssd_conv_w, ssd_conv_b=ssd_conv_b, ssd_dt_bias=ssd_dt_bias, ssd_a_log=ssd_a_log,
             ssd_d=ssd_d, ssd_norm=ssd_norm, rwkv_mu=rwkv_mu, rwkv_w0=rwkv_w0, rwkv_w2=rwkv_w2, rwkv_a0=rwkv_a0,
             rwkv_a2=rwkv_a2, rwkv_g2=rwkv_g2, rwkv_k_k=rwkv_k_k, rwkv_k_a=rwkv_k_a,
             rwkv_r_k=rwkv_r_k.reshape(rwkv_r_k.shape[0], -1), rwkv_ln_g=rwkv_ln_g, rwkv_ln_b=rwkv_ln_b,
             s5_lam_re=s5_lam_re, s5_lam_im=s5_lam_im, s5_log_step=s5_log_step, s5_b_re=s5_b_re, s5_b_im=s5_b_im,
             s5_c_re=s5_c_re, s5_c_im=s5_c_im, s5_d=s5_d, s5_glu_w=s5_glu_w, s5_glu_b=s5_glu_b, pool_w=pool_w,
             pool_scale=pool_scale, w_out=w_out, norm_ffn2=norm_ffn2, ffn2_in=ffn2_in, ffn2_out=ffn2_out)
    depth = norm_ffn1.shape[0]
    bp, tp, d = x_prompt.shape
    bs, ts, _ = x_sample.shape
    stacked = _stacked_params(P)
    layer_params = [_layer_params(stacked, l) for l in range(depth)]
    gf = norm_final.reshape(1, -1)
    sample_states = (state_ssd_conv, state_ssd, state_rwkv_shift, state_rwkv, state_s5_re, state_s5_im, state_pool)
    rwkv_rows = RWKV_HEADS * RWKV_HEAD
    decode_states = (jnp.swapaxes(state_ssd_conv, 1, 2), state_ssd, state_rwkv_shift,
                     jnp.transpose(state_rwkv, (0, 2, 3, 4, 1)).reshape(depth, rwkv_rows, RWKV_HEAD, bs),
                     state_s5_re, state_s5_im, jnp.swapaxes(state_pool, 1, 2))

    y_p, st_p = _trunk(x_prompt.reshape(bp * tp, d), layer_params, gf,
                       lambda l, lp, proj, done: _mixers_prompt(lp, proj, batch=bp, seq=tp))
    x_s = jnp.swapaxes(x_sample, 0, 1).reshape(ts * bs, d)
    y_s, st_s = _trunk(x_s, layer_params, gf,
                       lambda l, lp, proj, done: _mixers_decode(lp, proj, decode_states, done,
                                                                batch=bs, seq=ts, layer=l))
    outs = [y_p.reshape(bp, tp, d), jnp.swapaxes(y_s.reshape(ts, bs, d), 0, 1)]
    for i, ref_state in enumerate(sample_states):
        outs.append(jnp.stack([st[i] for st in st_p]))
        if i == 1:
            outs.append(st_s[-1][i].reshape(ref_state.shape))
        elif i == 3:
            s_new = st_s[-1][i].reshape(depth, RWKV_HEADS, RWKV_HEAD, RWKV_HEAD, bs)
            outs.append(jnp.transpose(s_new, (0, 4, 1, 2, 3)))
        else:
            outs.append(jnp.stack([st[i] for st in st_s]))
    return tuple(outs)
```

```python
import functools
import math

import jax
import jax.numpy as jnp
from jax import lax
from jax.experimental import pallas as pl
from jax.experimental.pallas import tpu as pltpu

F32 = jnp.float32
BF16 = jnp.bfloat16
HIGHEST = lax.Precision.HIGHEST

SUBLANES = 8
LANES = 128
VMEM_LIMIT_BYTES = 56 * 1024 * 1024

GROUP_WIDTH = 256
SSD_HEAD_DIM = 64
SSD_HEADS = 4
SSD_GROUPS = 2
SSD_STATE = 128
SSD_CONV = 4
SSD_CONV_DIM = GROUP_WIDTH + 2 * SSD_GROUPS * SSD_STATE
SSD_CHUNK = 128
SSD_GROUP = 2
LOG2_E = math.log2(math.e)
RWKV_HEAD = 64
RWKV_HEADS = 4
RWKV_PROJ = 1024
RWKV_LN_EPS = 64e-5
RWKV_CHUNK = 64
RWKV_GROUP = 8
S5_GROUP_CH = 16
S5_GROUPS = 16
S5_STATE = 64
S5_WIDTH = S5_GROUPS * S5_STATE
POOL_WINDOWS = (2, 4, 8, 16)
POOL_CH = 64
POOL_BUF = 15
RMS_EPS = 1e-6
PAST_LEN = 16384

ROW_TILE = 512
FFN_CHUNK = 256
TM_CHUNK = 64
S5_SUB = 16
SSD_STEP_TILES = 16
RWKV_STEP_TILES = 16


def _cparams(*sem):
    return pltpu.CompilerParams(dimension_semantics=sem, vmem_limit_bytes=VMEM_LIMIT_BYTES)


def _dot(a, b, **kw):
    return jnp.dot(a, b, preferred_element_type=F32, **kw)


def _dot_nt(a, b):
    return lax.dot_general(a, b, (((1,), (1,)), ((), ())), preferred_element_type=F32)


def _dot_tn(a, b):
    return lax.dot_general(a, b, (((0,), (0,)), ((), ())), preferred_element_type=F32)


def _sigmoid(x):
    return 1.0 / (1.0 + jnp.exp(-x))


def _silu(x):
    return x * _sigmoid(x)


def _softplus(x):
    return jnp.maximum(x, 0.0) + jnp.log(1.0 + jnp.exp(-jnp.abs(x)))


def _gelu_tanh(x):
    c = math.sqrt(2.0 / math.pi)
    return x * (0.5 * (1.0 + jnp.tanh(c * (x + 0.044715 * (x * x * x)))))


def _rms(x, g):
    return x * lax.rsqrt(jnp.mean(x * x, axis=-1, keepdims=True) + RMS_EPS) * g


def _full_spec(shape):
    n = len(shape)
    return pl.BlockSpec(shape, lambda *_: (0,) * n)


class _Layered(tuple):
    pass


def _pspec(p):
    if isinstance(p, _Layered):
        a, l = p
        return pl.BlockSpec((None,) + a.shape[1:], lambda *_: (l,) + (0,) * (a.ndim - 1))
    return _full_spec(p.shape)


def _parg(p):
    return p[0] if isinstance(p, _Layered) else p


def _ffn_body(*refs, has_mix, final_norm):
    it = iter(refs)
    x_ref = next(it)
    x = x_ref[...]
    if has_mix:
        y_refs = [next(it) for _ in range(4)]
        wmix_ref = next(it)
        for i, y_ref in enumerate(y_refs):
            x = x + _dot(y_ref[...].astype(BF16), wmix_ref[i * GROUP_WIDTH:(i + 1) * GROUP_WIDTH, :])
    g_ref, wi_ref, wo_ref = next(it), next(it), next(it)
    gf_ref = next(it) if final_norm else None
    o_ref = next(it)
    h = _rms(x, g_ref[...]).astype(BF16)
    d_ff = wo_ref.shape[0]
    acc = jnp.zeros_like(x)
    for c in range(d_ff // FFN_CHUNK):
        lo = c * FFN_CHUNK
        gate = _dot(h, wi_ref[:, lo:lo + FFN_CHUNK])
        up = _dot(h, wi_ref[:, d_ff + lo:d_ff + lo + FFN_CHUNK])
        act = (_silu(gate) * up).astype(BF16)
        acc = acc + _dot(act, wo_ref[lo:lo + FFN_CHUNK, :])
    x = x + 0.5 * acc
    if final_norm:
        x = _rms(x, gf_ref[...])
    o_ref[...] = x


def _ffn(x, g, wi, wo, mix=None, wmix=None, gf=None):
    rows, d = x.shape
    row_spec = lambda w: pl.BlockSpec((ROW_TILE, w), lambda i: (i, 0))
    args, specs = [x], [row_spec(d)]
    if mix is not None:
        for y in mix:
            args.append(y)
            specs.append(row_spec(y.shape[1]))
        args.append(_parg(wmix))
        specs.append(_pspec(wmix))
    for a in (g, wi, wo) + ((gf,) if gf is not None else ()):
        args.append(_parg(a))
        specs.append(_pspec(a))
    return pl.pallas_call(
        functools.partial(_ffn_body, has_mix=mix is not None, final_norm=gf is not None),
        grid=(rows // ROW_TILE,),
        in_specs=specs,
        out_specs=row_spec(d),
        out_shape=jax.ShapeDtypeStruct((rows, d), F32),
        compiler_params=_cparams("parallel"),
        name="ffn",
    )(*args)


def _inproj_body(x_ref, g_ref, w_ref, *o_refs):
    h = _rms(x_ref[...], g_ref[...]).astype(BF16)
    off = 0
    for o_ref in o_refs:
        n = o_ref.shape[-1]
        o_ref[...] = _dot(h, w_ref[:, off:off + n])
        off += n


def _inproj(x, g, w, widths):
    rows, d = x.shape
    row_spec = lambda w_: pl.BlockSpec((ROW_TILE, w_), lambda i: (i, 0))
    return pl.pallas_call(
        _inproj_body,
        grid=(rows // ROW_TILE,),
        in_specs=[row_spec(d), _pspec(g), _pspec(w)],
        out_specs=[row_spec(n) for n in widths],
        out_shape=[jax.ShapeDtypeStruct((rows, n), F32) for n in widths],
        compiler_params=_cparams("parallel"),
        name="inproj",
    )(x, _parg(g), _parg(w))


def _ssd_body(z_ref, xbc_ref, dt_ref, cw_ref, cb_ref, dtb_ref, alog_ref, dsk_ref, ng_ref,
              y_ref, conv_ref, hout_ref, xpad_scr, h_scr, *, chunk, group):
    L, G = chunk, group
    GL = G * L
    c = pl.program_id(1)
    pad = SUBLANES
    halo = SSD_CONV - 1
    hpg = SSD_HEADS // SSD_GROUPS
    assert hpg == 2 and hpg * SSD_HEAD_DIM == SSD_STATE

    @pl.when(c == 0)
    def _():
        xpad_scr[0:pad, :] = jnp.zeros((pad, SSD_CONV_DIM), F32)
        h_scr[...] = jnp.zeros(h_scr.shape, F32)

    xpad_scr[pad:pad + GL, :] = xbc_ref[...]
    conv = cb_ref[...] + cw_ref[0:1, :] * xpad_scr[pad - halo:pad - halo + GL, :]
    for j in range(1, SSD_CONV):
        conv = conv + cw_ref[j:j + 1, :] * xpad_scr[pad - halo + j:pad - halo + j + GL, :]
    xpad_scr[pad - halo:pad, :] = xpad_scr[pad + GL - halo:pad + GL, :]
    conv = _silu(conv)
    xs = conv[:, 0:GROUP_WIDTH]
    bm = conv[:, GROUP_WIDTH:2 * GROUP_WIDTH].astype(BF16)
    cm = conv[:, 2 * GROUP_WIDTH:3 * GROUP_WIDTH].astype(BF16)

    row = lax.broadcasted_iota(jnp.int32, (L, L), 0)
    col = lax.broadcasted_iota(jnp.int32, (L, L), 1)
    causal = row >= col
    tril = jnp.where(causal, 1.0, 0.0).astype(F32)
    dt = _softplus(dt_ref[...] + dtb_ref[...])
    da = dt * (-jnp.exp(alog_ref[...]) * LOG2_E)
    acs = [_dot(tril, da[i * L:(i + 1) * L, :], precision=HIGHEST) for i in range(G)]
    acs_t = [a.T for a in acs]
    e_acs = [jnp.exp2(a) for a in acs]
    e_end = [jnp.exp2(a[L - 1:L, :] - a) for a in acs]
    e_last = [jnp.exp2(a[L - 1:L, :]) for a in acs]

    keys = [(i, g) for i in range(G) for g in range(SSD_GROUPS)]
    rows_of = lambda x, i: x[i * L:(i + 1) * L]
    lanes_of = lambda x, g: x[:, g * SSD_STATE:(g + 1) * SSD_STATE]
    lane_lo = lax.broadcasted_iota(jnp.int32, (L, hpg * SSD_HEAD_DIM), 1) < SSD_HEAD_DIM
    row_lo = lax.broadcasted_iota(jnp.int32, (hpg * SSD_HEAD_DIM, SSD_STATE), 0) < SSD_HEAD_DIM
    head_cols = lambda a, g: jnp.where(lane_lo, a[:, g * hpg:g * hpg + 1], a[:, g * hpg + 1:g * hpg + 2])
    bg = {(i, g): lanes_of(rows_of(bm, i), g) for i, g in keys}
    cg = {(i, g): lanes_of(rows_of(cm, i), g) for i, g in keys}
    scores = {k: _dot_nt(cg[k], bg[k]) for k in keys}
    xdt = {(i, g): lanes_of(rows_of(xs, i), g) * head_cols(rows_of(dt, i), g) for i, g in keys}
    decay = {(i, h): jnp.exp2(jnp.where(causal, acs[i][:, h:h + 1] - acs_t[i][h:h + 1, :], -jnp.inf))
             for i in range(G) for h in range(SSD_HEADS)}
    p_mat = {(i, g): jnp.concatenate([(scores[(i, g)] * decay[(i, g * hpg + k)]).astype(BF16) for k in range(hpg)],
                                     axis=1) for i, g in keys}
    y_in = {k: _dot(p_mat[k], _bd(xdt[k].astype(BF16))) for k in keys}
    st = {(i, g): _dot_tn((xdt[(i, g)] * head_cols(e_end[i], g)).astype(BF16), bg[(i, g)]) for i, g in keys}

    y_rows = []
    for i in range(G):
        ys = []
        for g in range(SSD_GROUPS):
            h_prev = h_scr[g * hpg:(g + 1) * hpg].reshape(hpg * SSD_HEAD_DIM, SSD_STATE)
            ys.append(y_in[(i, g)] + _dot_nt(cg[(i, g)], h_prev.astype(BF16)) * head_cols(e_acs[i], g))
            keep = jnp.where(row_lo, e_last[i][:, g * hpg:g * hpg + 1], e_last[i][:, g * hpg + 1:g * hpg + 2])
            h_scr[g * hpg:(g + 1) * hpg] = (h_prev * keep + st[(i, g)]).reshape(hpg, SSD_HEAD_DIM, SSD_STATE)
        y_rows.append(jnp.concatenate(ys, axis=-1))
    y = jnp.concatenate(y_rows, axis=0) + xs * dsk_ref[...]
    y = y * _silu(z_ref[...])
    y_ref[...] = _rms(y, ng_ref[...])

    @pl.when(c == pl.num_programs(1) - 1)
    def _():
        hout_ref[0] = h_scr[...]
        conv_ref[0] = xpad_scr[pad - halo:pad, :]


def _ssd(z, xbc, dtr, lp, *, batch, seq):
    chunk = SSD_CHUNK
    rows = chunk * SSD_GROUP
    nc = seq // rows
    rspec = lambda w: pl.BlockSpec((rows, w), lambda b, c: (b * nc + c, 0))
    consts = (lp["conv_w"], lp["conv_b"], lp["dt_bias"], lp["a_log"], lp["d_skip"], lp["ssd_norm"])
    return pl.pallas_call(
        functools.partial(_ssd_body, chunk=chunk, group=SSD_GROUP),
        grid=(batch, nc),
        in_specs=[rspec(GROUP_WIDTH), rspec(SSD_CONV_DIM), rspec(LANES)] + [_pspec(a) for a in consts],
        out_specs=[rspec(GROUP_WIDTH),
                   pl.BlockSpec((1, SSD_CONV - 1, SSD_CONV_DIM), lambda b, c: (b, 0, 0)),
                   pl.BlockSpec((1, SSD_HEADS, SSD_HEAD_DIM, SSD_STATE), lambda b, c: (b, 0, 0, 0))],
        out_shape=[jax.ShapeDtypeStruct((batch * seq, GROUP_WIDTH), F32),
                   jax.ShapeDtypeStruct((batch, SSD_CONV - 1, SSD_CONV_DIM), F32),
                   jax.ShapeDtypeStruct((batch, SSD_HEADS, SSD_HEAD_DIM, SSD_STATE), F32)],
        scratch_shapes=[pltpu.VMEM((SUBLANES + rows, SSD_CONV_DIM), F32),
                        pltpu.VMEM((SSD_HEADS, SSD_HEAD_DIM, SSD_STATE), F32)],
        compiler_params=_cparams("parallel", "arbitrary"),
        name="ssd",
    )(z, xbc, dtr, *[_parg(a) for a in consts])


def _ssd_step_body(z_ref, xbc_ref, dt_ref, conv0_ref, h0_ref, *rest, seq, batch, layer):
    hdone_ref, rest = (rest[0], rest[1:]) if layer else (None, rest)
    (cw_ref, cb_ref, dtb_ref, aneg_ref, dsk_ref, ng_ref, hexp_ref, y_ref, conv_ref, hout_ref,
     xs_scr, bm_scr, cm_scr, xdt_scr, dec_scr, y_scr) = rest
    T, B = seq, batch
    if layer:
        hout_ref[0:layer] = hdone_ref[...]
    GW = GROUP_WIDTH
    j = pl.program_id(0)
    tiles = SSD_STEP_TILES

    @pl.when(j == 0)
    def _():
        rows = [conv0_ref[i] for i in range(SSD_CONV - 1)]
        rows += [xbc_ref[t * B:(t + 1) * B, :] for t in range(T)]
        for t in range(T):
            conv = cb_ref[...] + cw_ref[0:1, :] * rows[t]
            for i in range(1, SSD_CONV):
                conv = conv + cw_ref[i:i + 1, :] * rows[t + i]
            conv = _silu(conv)
            xs = conv[:, 0:GW]
            xs_scr[t] = xs
            for g in range(SSD_GROUPS):
                bm_scr[t, g] = conv[:, GW + g * SSD_STATE:GW + (g + 1) * SSD_STATE].T
                cm_scr[t, g] = conv[:, 2 * GW + g * SSD_STATE:2 * GW + (g + 1) * SSD_STATE].T
            dt = _softplus(dt_ref[t * B:(t + 1) * B, :] + dtb_ref[...])
            dte = _dot(dt, hexp_ref[...], precision=HIGHEST)
            xdt_scr[t] = (xs * dte).T
            dec_scr[t] = jnp.exp(dte * aneg_ref[...]).T
        for i in range(SSD_CONV - 1):
            conv_ref[i] = rows[T + i]

    hp0 = j * tiles
    grp = hp0 // (SSD_HEAD_DIM * (SSD_HEADS // SSD_GROUPS))
    for q in range(tiles):
        hp = pl.ds(hp0 + q, 1)
        h = h0_ref[:, q, :].T
        for t in range(T):
            h = h * dec_scr[t, hp, :] + bm_scr[t, grp] * xdt_scr[t, hp, :]
            y_scr[t, hp, :] = jnp.sum(h * cm_scr[t, grp], axis=0, keepdims=True)
        hout_ref[layer, :, q, :] = h.T

    @pl.when(j == pl.num_programs(0) - 1)
    def _():
        for t in range(T):
            y = y_scr[t].T + xs_scr[t] * dsk_ref[...]
            y = y * _silu(z_ref[t * B:(t + 1) * B, :])
            y_ref[t * B:(t + 1) * B, :] = _rms(y, ng_ref[...])


def _layer_state_specs(layer, block, axis):
    idx = lambda first: (lambda j: (first,) + tuple(j if a == axis else 0 for a in range(len(block))))
    cur = pl.BlockSpec((None,) + block, idx(layer))
    prev = [pl.BlockSpec((layer,) + block, idx(0))] if layer else []
    out = pl.BlockSpec((layer + 1,) + block, idx(0))
    return cur, prev, out


def _ssd_step(z, xbc, dtr, conv_all, h_all, h_done, lp, *, batch, seq, layer):
    n = batch * seq
    srows = SSD_HEADS * SSD_HEAD_DIM
    consts = (lp["conv_w"], lp["conv_b"], lp["dt_bias"], lp["a_neg_exp"], lp["d_skip"], lp["ssd_norm"], lp["head_expand"])
    hspec, prev_specs, hout_spec = _layer_state_specs(layer, (batch, SSD_STEP_TILES, SSD_STATE), 1)
    prev_args = [h_done] if layer else []
    cshape = (SSD_CONV - 1, batch, SSD_CONV_DIM)
    return pl.pallas_call(
        functools.partial(_ssd_step_body, seq=seq, batch=batch, layer=layer),
        grid=(srows // SSD_STEP_TILES,),
        in_specs=[_full_spec((n, GROUP_WIDTH)), _full_spec((n, SSD_CONV_DIM)), _full_spec((n, LANES)),
                  pl.BlockSpec((None,) + cshape, lambda j: (layer, 0, 0, 0)), hspec] + prev_specs
                 + [_pspec(a) for a in consts],
        out_specs=[_full_spec((n, GROUP_WIDTH)), _full_spec(cshape), hout_spec],
        out_shape=[jax.ShapeDtypeStruct((n, GROUP_WIDTH), F32),
                   jax.ShapeDtypeStruct(cshape, F32),
                   jax.ShapeDtypeStruct((layer + 1, batch, srows, SSD_STATE), F32)],
        scratch_shapes=[pltpu.VMEM((seq, batch, GROUP_WIDTH), F32),
                        pltpu.VMEM((seq, SSD_GROUPS, SSD_STATE, batch), F32),
                        pltpu.VMEM((seq, SSD_GROUPS, SSD_STATE, batch), F32),
                        pltpu.VMEM((seq, GROUP_WIDTH, batch), F32),
                        pltpu.VMEM((seq, GROUP_WIDTH, batch), F32),
                        pltpu.VMEM((seq, GROUP_WIDTH, batch), F32)],
        compiler_params=_cparams("arbitrary"),
        name="ssd_step",
    )(z, xbc, dtr, conv_all, h_all.reshape(h_all.shape[0], batch, srows, SSD_STATE),
      *prev_args, *[_parg(a) for a in consts])


PAIR = 2 * RWKV_HEAD
RWKV_PAIRS = RWKV_HEADS // 2


def _bd(x):
    half = x.shape[1] // 2
    lane = lax.broadcasted_iota(jnp.int32, x.shape, 1)
    zero = jnp.zeros_like(x)
    return jnp.concatenate([jnp.where(lane < half, x, zero), jnp.where(lane >= half, x, zero)], axis=0)


def _half_sums(x, lo):
    s_lo = jnp.sum(jnp.where(lo, x, 0.0), axis=-1, keepdims=True)
    s_hi = jnp.sum(jnp.where(lo, 0.0, x), axis=-1, keepdims=True)
    return jnp.where(lo, s_lo, s_hi)


def _head_sum(x):
    lo = lax.broadcasted_iota(jnp.int32, (x.shape[0], PAIR), 1) < RWKV_HEAD
    return jnp.concatenate([_half_sums(x[:, p * PAIR:(p + 1) * PAIR], lo) for p in range(RWKV_PAIRS)], axis=-1)


def _rwkv_pointwise(u, prev, mu_ref, w0_ref, w2_ref, a0_ref, a2_ref, g2_ref, kk_ref, ka_ref):
    GW = GROUP_WIDTH
    xs = u + (prev - u) * mu_ref[...]
    r = xs[:, 0:GW]
    k = xs[:, GW:2 * GW]
    v = xs[:, 2 * GW:3 * GW]
    wd = xs[:, 3 * GW:3 * GW + 64]
    ad = xs[:, 3 * GW + 64:3 * GW + 128]
    gd = xs[:, 3 * GW + 128:3 * GW + 256]
    w_lin = w0_ref[...] + _dot(jnp.tanh(wd).astype(BF16), w2_ref[...])
    logdecay = -jnp.exp(-_softplus(-w_lin) - 0.5)
    a = _sigmoid(a0_ref[...] + _dot(ad.astype(BF16), a2_ref[...]))
    g = _dot(_sigmoid(gd).astype(BF16), g2_ref[...])
    kk = k * kk_ref[...]
    kk = kk / jnp.maximum(jnp.sqrt(_head_sum(kk * kk)), 1e-12)
    k = k * (1.0 + (a - 1.0) * ka_ref[...])
    return r, k, v, logdecay, a, g, kk


def _rwkv_finish(y, r, k, v, g, rk_ref, lng_ref, lnb_ref):
    mean = _head_sum(y) * (1.0 / RWKV_HEAD)
    yc = y - mean
    var = _head_sum(yc * yc) * (1.0 / RWKV_HEAD)
    y = yc * lax.rsqrt(var + RWKV_LN_EPS) * lng_ref[...] + lnb_ref[...]
    bonus = _head_sum(r * k * rk_ref[...]) * v
    return (y + bonus) * g


def _rwkv_body(u_ref, mu_ref, w0_ref, w2_ref, a0_ref, a2_ref, g2_ref, kk_ref, ka_ref, rk_ref,
               lng_ref, lnb_ref, y_ref, shift_ref, sout_ref, upad_scr, s_scr, *, chunk, group):
    L, G = chunk, group
    GL = G * L
    c = pl.program_id(1)
    pad = SUBLANES

    @pl.when(c == 0)
    def _():
        upad_scr[0:pad, :] = jnp.zeros((pad, RWKV_PROJ), F32)
        s_scr[...] = jnp.zeros(s_scr.shape, F32)

    u = u_ref[...]
    upad_scr[pad:pad + GL, :] = u
    prev = upad_scr[pad - 1:pad - 1 + GL, :]
    upad_scr[pad - 1:pad, :] = u[GL - 1:GL, :]
    r, k, v, logdecay, a, g, kk = _rwkv_pointwise(u, prev, mu_ref, w0_ref, w2_ref, a0_ref, a2_ref, g2_ref,
                                                  kk_ref, ka_ref)

    tril = jnp.where(lax.broadcasted_iota(jnp.int32, (L, L), 0) >= lax.broadcasted_iota(jnp.int32, (L, L), 1),
                     1.0, 0.0).astype(F32)
    cl = jnp.concatenate([_dot(tril, logdecay[i * L:(i + 1) * L, :], precision=HIGHEST) for i in range(G)], axis=0)
    e_in = jnp.exp(cl)
    e_inv = jnp.exp(-cl)
    r_t = r * e_in
    r_tb = r_t.astype(BF16)
    a_tb = (-kk * jnp.exp(cl - logdecay)).astype(BF16)
    b_tb = (kk * a * e_inv).astype(BF16)
    k_tb = (k * e_inv).astype(BF16)
    vb = v.astype(BF16)

    row = lax.broadcasted_iota(jnp.int32, (L, PAIR), 0)
    colh = lax.broadcasted_iota(jnp.int32, (L, PAIR), 1) & (RWKV_HEAD - 1)
    strict = row > colh
    incl = row >= colh
    eye_pair = jnp.where(row == colh, 1.0, 0.0).astype(F32)
    lane_lo = lax.broadcasted_iota(jnp.int32, (RWKV_HEAD, PAIR), 1) < RWKV_HEAD
    same_head = (lax.broadcasted_iota(jnp.int32, (PAIR, PAIR), 0) < RWKV_HEAD) == \
                (lax.broadcasted_iota(jnp.int32, (PAIR, PAIR), 1) < RWKV_HEAD)

    streams = [(i, p) for i in range(G) for p in range(RWKV_PAIRS)]
    ns = len(streams)
    blk = lambda x, i, p: x[i * L:(i + 1) * L, p * PAIR:(p + 1) * PAIR]
    lhs = [jnp.concatenate([blk(a_tb, i, p), blk(r_tb, i, p)], axis=0) for i, p in streams]
    m_ab = [_dot_nt(lhs[s], _bd(blk(b_tb, i, p))) for s, (i, p) in enumerate(streams)]
    m_ak = [_dot_nt(lhs[s], _bd(blk(k_tb, i, p))) for s, (i, p) in enumerate(streams)]
    n_ab = [jnp.where(strict, m[0:L], 0.0) for m in m_ab]
    m_rb = [jnp.where(incl, m[L:2 * L], 0.0).astype(BF16) for m in m_ab]
    n_ak = [jnp.where(strict, m[0:L], 0.0).astype(BF16) for m in m_ak]
    m_rk = [jnp.where(incl, m[L:2 * L], 0.0).astype(BF16) for m in m_ak]
    tinv = [eye_pair + n for n in n_ab]
    pwb = [n.astype(BF16) for n in n_ab]
    pw = [_dot(x, _bd(x)) for x in pwb]
    for _ in range(int(math.log2(L)) - 2):
        pwb = [x.astype(BF16) for x in pw]
        both = [_dot(jnp.concatenate([pwb[s], tinv[s].astype(BF16)], axis=0), _bd(pwb[s])) for s in range(ns)]
        pw = [x[0:L] for x in both]
        tinv = [tinv[s] + both[s][L:2 * L] for s in range(ns)]
    pwb = [x.astype(BF16) for x in pw]
    tinv = [tinv[s] + _dot(tinv[s].astype(BF16), _bd(pwb[s])) for s in range(ns)]
    tinvb = [x.astype(BF16) for x in tinv]
    nv_mv = [_dot(jnp.concatenate([n_ak[s], m_rk[s]], axis=0), _bd(blk(vb, i, p))) for s, (i, p) in enumerate(streams)]
    wu = [_dot(tinvb[s], jnp.concatenate([_bd(blk(a_tb, i, p)), _bd(nv_mv[s][0:L].astype(BF16))], axis=1))
          for s, (i, p) in enumerate(streams)]
    wub = [x.astype(BF16) for x in wu]
    qy = [_dot(m_rb[s], jnp.concatenate([_bd(wub[s][:, 0:PAIR]), _bd(wub[s][:, PAIR:2 * PAIR])], axis=1))
          for s in range(ns)]
    q = [(blk(r_t, i, p) + qy[s][:, 0:PAIR]).astype(BF16) for s, (i, p) in enumerate(streams)]
    y_loc = [qy[s][:, PAIR:2 * PAIR] + nv_mv[s][L:2 * L] for s in range(ns)]
    zeros_b = jnp.zeros((L, PAIR), BF16)
    mg = [_dot_tn(jnp.concatenate([wub[s], jnp.concatenate([zeros_b, blk(vb, i, p)], axis=1)], axis=0),
                  jnp.concatenate([blk(b_tb, i, p), blk(k_tb, i, p)], axis=0))
          for s, (i, p) in enumerate(streams)]
    p_end = [e_in[(i + 1) * L - 1:(i + 1) * L, p * PAIR:(p + 1) * PAIR] for i, p in streams]
    m_t = [(jnp.where(same_head, mg[s][0:PAIR], 0.0) * p_end[s]).astype(BF16) for s in range(ns)]
    g_t = [jnp.where(lane_lo, mg[s][PAIR:PAIR + RWKV_HEAD], mg[s][PAIR + RWKV_HEAD:2 * PAIR]) * p_end[s]
           for s in range(ns)]

    y_rows = []
    for i in range(G):
        y_pairs = []
        for p in range(RWKV_PAIRS):
            s = i * RWKV_PAIRS + p
            s0 = s_scr[p]
            s0b = s0.astype(BF16)
            y_pairs.append(_dot_nt(q[s], _bd(s0b)) + y_loc[s])
            s_scr[p] = s0 * p_end[s] + _dot(s0b, m_t[s]) + g_t[s]
        y_rows.append(jnp.concatenate(y_pairs, axis=-1))
    y = jnp.concatenate(y_rows, axis=0)
    y_ref[...] = _rwkv_finish(y, r, k, v, g, rk_ref, lng_ref, lnb_ref)

    @pl.when(c == pl.num_programs(1) - 1)
    def _():
        sout_ref[0] = s_scr[...]
        shift_ref[0] = upad_scr[pad - 1:pad, :]


_RWKV_PARAM_NAMES = ("mu", "w0", "w2", "a0", "a2", "g2", "k_k", "k_a", "r_k", "ln_g", "ln_b")


def _rwkv(u, p, *, batch, seq):
    rows = RWKV_CHUNK * RWKV_GROUP
    nc = seq // rows
    params = [p[n] for n in _RWKV_PARAM_NAMES]
    sspec = pl.BlockSpec((1, RWKV_PAIRS, RWKV_HEAD, PAIR), lambda b, c: (b, 0, 0, 0))
    y, shift, s_last = pl.pallas_call(
        functools.partial(_rwkv_body, chunk=RWKV_CHUNK, group=RWKV_GROUP),
        grid=(batch, nc),
        in_specs=[pl.BlockSpec((rows, RWKV_PROJ), lambda b, c: (b * nc + c, 0))] + [_pspec(a) for a in params],
        out_specs=[pl.BlockSpec((rows, GROUP_WIDTH), lambda b, c: (b * nc + c, 0)),
                   pl.BlockSpec((1, 1, RWKV_PROJ), lambda b, c: (b, 0, 0)), sspec],
        out_shape=[jax.ShapeDtypeStruct((batch * seq, GROUP_WIDTH), F32),
                   jax.ShapeDtypeStruct((batch, 1, RWKV_PROJ), F32),
                   jax.ShapeDtypeStruct((batch, RWKV_PAIRS, RWKV_HEAD, PAIR), F32)],
        scratch_shapes=[pltpu.VMEM((SUBLANES + rows, RWKV_PROJ), F32),
                        pltpu.VMEM((RWKV_PAIRS, RWKV_HEAD, PAIR), F32)],
        compiler_params=_cparams("parallel", "arbitrary"),
        name="rwkv",
    )(u, *[_parg(a) for a in params])
    s_last = s_last.reshape(batch, RWKV_PAIRS, RWKV_HEAD, 2, RWKV_HEAD).transpose(0, 1, 3, 2, 4).reshape(
        batch, RWKV_HEADS, RWKV_HEAD, RWKV_HEAD)
    return y, shift.reshape(batch, RWKV_PROJ), s_last


def _rwkv_step_body(u_ref, shift0_ref, s0_ref, *rest, seq, batch, layer):
    sdone_ref, rest = (rest[0], rest[1:]) if layer else (None, rest)
    (mu_ref, w0_ref, w2_ref, a0_ref, a2_ref, g2_ref, kk_ref, ka_ref, rk_ref, lng_ref, lnb_ref, y_ref, sout_ref,
     r_scr, w_scr, k_scr, b_scr, nkk_scr, v_scr, y_scr) = rest
    T, B = seq, batch
    j = pl.program_id(0)
    if layer:
        sout_ref[0:layer] = sdone_ref[...]
    tiles = RWKV_STEP_TILES

    def pointwise(t):
        u = u_ref[t * B:(t + 1) * B, :]
        prev = shift0_ref[...] if t == 0 else u_ref[(t - 1) * B:t * B, :]
        return _rwkv_pointwise(u, prev, mu_ref, w0_ref, w2_ref, a0_ref, a2_ref, g2_ref, kk_ref, ka_ref)

    @pl.when(j == 0)
    def _():
        for t in range(T):
            r, k, v, logdecay, a, _, kk = pointwise(t)
            r_scr[t] = r.T
            w_scr[t] = jnp.exp(logdecay).T
            k_scr[t] = k.T
            b_scr[t] = (kk * a).T
            nkk_scr[t] = (-kk).T
            v_scr[t] = v.T

    i0 = j * tiles
    keys = pl.ds(pl.multiple_of((i0 // RWKV_HEAD) * RWKV_HEAD, RWKV_HEAD), RWKV_HEAD)
    for q in range(tiles):
        vi = pl.ds(i0 + q, 1)
        s = s0_ref[q]
        for t in range(T):
            sa = jnp.sum(s * nkk_scr[t, keys, :], axis=0, keepdims=True)
            s = s * w_scr[t, keys, :] + k_scr[t, keys, :] * v_scr[t, vi, :] + b_scr[t, keys, :] * sa
            y_scr[t, vi, :] = jnp.sum(s * r_scr[t, keys, :], axis=0, keepdims=True)
        sout_ref[layer, q] = s

    @pl.when(j == pl.num_programs(0) - 1)
    def _():
        for t in range(T):
            r, k, v, _, _, g, _ = pointwise(t)
            y_ref[t * B:(t + 1) * B, :] = _rwkv_finish(y_scr[t].T, r, k, v, g, rk_ref, lng_ref, lnb_ref)


def _rwkv_step(u, shift0, s_all, s_done, p, *, batch, seq, layer):
    n = batch * seq
    srows = RWKV_HEADS * RWKV_HEAD
    params = [p[nm] for nm in _RWKV_PARAM_NAMES]
    sspec, prev_specs, sout_spec = _layer_state_specs(layer, (RWKV_STEP_TILES, RWKV_HEAD, batch), 0)
    prev_args = [s_done] if layer else []
    tposed = pltpu.VMEM((seq, GROUP_WIDTH, batch), F32)
    return pl.pallas_call(
        functools.partial(_rwkv_step_body, seq=seq, batch=batch, layer=layer),
        grid=(srows // RWKV_STEP_TILES,),
        in_specs=[_full_spec((n, RWKV_PROJ)), _full_spec((batch, RWKV_PROJ)), sspec] + prev_specs
                 + [_pspec(a) for a in params],
        out_specs=[_full_spec((n, GROUP_WIDTH)), sout_spec],
        out_shape=[jax.ShapeDtypeStruct((n, GROUP_WIDTH), F32),
                   jax.ShapeDtypeStruct((layer + 1, srows, RWKV_HEAD, batch), F32)],
        scratch_shapes=[tposed] * 7,
        compiler_params=_cparams("arbitrary"),
        name="rwkv_step",
    )(u, shift0, s_all, *prev_args, *[_parg(a) for a in params])


def _s5_body(u_ref, hre0_ref, him0_ref, are_ref, aim_ref, bmat_ref, cmat_ref, d_ref, gw_ref, gb_ref,
             y_ref, hre_ref, him_ref, hs_scr, tm_scr, *, steps, batch_major):
    c = pl.program_id(1)
    ns = S5_WIDTH
    bsub = SUBLANES

    @pl.when(c == 0)
    def _():
        hre_ref[...] = hre0_ref[...]
        him_ref[...] = him0_ref[...]

    if batch_major:
        for b in range(bsub):
            tm_scr[:, b, :] = u_ref[b]
        u = tm_scr[...].reshape(steps * bsub, GROUP_WIDTH)
    else:
        u = u_ref[...].reshape(steps * bsub, GROUP_WIDTH)
    are = jnp.broadcast_to(are_ref[...], (bsub, ns))
    aim = jnp.broadcast_to(aim_ref[...], (bsub, ns))
    hre, him = hre_ref[...], him_ref[...]
    sub = min(S5_SUB, steps)
    rows = sub * bsub
    outs = []
    for k in range(steps // sub):
        r0 = k * rows
        u_k = u[r0:r0 + rows]
        hs_scr[r0:r0 + rows, :] = _dot(u_k.astype(BF16), bmat_ref[...])
        for t in range(sub):
            rs = slice(r0 + t * bsub, r0 + (t + 1) * bsub)
            hre, him = (are * hre - aim * him + hs_scr[rs, 0:ns], are * him + aim * hre + hs_scr[rs, ns:2 * ns])
            hs_scr[rs, 0:ns] = hre
            hs_scr[rs, ns:2 * ns] = him
        y = _dot(hs_scr[r0:r0 + rows, :].astype(BF16), cmat_ref[...]) + u_k * d_ref[...]
        y = _gelu_tanh(y)
        yy = _dot(y.astype(BF16), gw_ref[...]) + gb_ref[...]
        outs.append(yy[:, 0:GROUP_WIDTH] * _sigmoid(yy[:, GROUP_WIDTH:2 * GROUP_WIDTH]))
    hre_ref[...] = hre
    him_ref[...] = him
    out = jnp.concatenate(outs, axis=0).reshape(steps, bsub, GROUP_WIDTH)
    if batch_major:
        tm_scr[...] = out
        for b in range(bsub):
            y_ref[b] = tm_scr[:, b, :]
    else:
        y_ref[...] = out


def _time_specs(u, batch_major):
    bsub = SUBLANES
    if batch_major:
        batch, seq, _ = u.shape
        steps = min(TM_CHUNK, seq)
        spec = pl.BlockSpec((bsub, steps, GROUP_WIDTH), lambda b, c: (b, c, 0))
    else:
        seq, batch, _ = u.shape
        steps = min(TM_CHUNK, seq)
        spec = pl.BlockSpec((steps, bsub, GROUP_WIDTH), lambda b, c: (c, b, 0))
    return batch, seq, steps, spec


def _s5(u, hre0, him0, lp, *, batch_major):
    batch, seq, steps, tspec = _time_specs(u, batch_major)
    bsub = SUBLANES
    hspec = pl.BlockSpec((bsub, S5_WIDTH), lambda b, c: (b, 0))
    consts = (lp["s5_are"], lp["s5_aim"], lp["s5_bmat"], lp["s5_cmat"], lp["s5_d"], lp["s5_gw"], lp["s5_gb"])
    return pl.pallas_call(
        functools.partial(_s5_body, steps=steps, batch_major=batch_major),
        grid=(batch // bsub, seq // steps),
        in_specs=[tspec, hspec, hspec] + [_pspec(a) for a in consts],
        out_specs=[tspec, hspec, hspec],
        out_shape=[jax.ShapeDtypeStruct(u.shape, F32),
                   jax.ShapeDtypeStruct((batch, S5_WIDTH), F32),
                   jax.ShapeDtypeStruct((batch, S5_WIDTH), F32)],
        scratch_shapes=[pltpu.VMEM((steps * bsub, 2 * S5_WIDTH), F32),
                        pltpu.VMEM((steps, bsub, GROUP_WIDTH), F32)],
        compiler_params=_cparams("parallel", "arbitrary"),
        name="s5",
    )(u, hre0, him0, *[_parg(a) for a in consts])


def _pool_body(u_ref, buf0_ref, pw_ref, sc_ref, y_ref, buf_ref, f_scr, tm_scr, *, steps, pos0, batch_major):
    c = pl.program_id(1)
    bsub = SUBLANES
    GW = GROUP_WIDTH
    halo = POOL_BUF + 1

    @pl.when(c == 0)
    def _():
        f_scr[0] = jnp.zeros((bsub, GW), F32)
        f_scr[1:halo] = buf0_ref[...]

    if batch_major:
        for b in range(bsub):
            f_scr[halo:halo + steps, b, :] = u_ref[b]
    else:
        f_scr[halo:halo + steps] = u_ref[...]
    f = f_scr[...]
    u = f[halo:halo + steps]
    s2 = f[1:] + f[:-1]
    s4 = s2[2:] + s2[:-2]
    s8 = s4[4:] + s4[:-4]
    s16 = s8[8:] + s8[:-8]
    f_scr[0:halo] = f[steps:steps + halo]
    lane = lax.broadcasted_iota(jnp.int32, (steps, bsub, GW), 2)
    tpos = lax.broadcasted_iota(jnp.int32, (steps, bsub, GW), 0) + (pos0 + 1) + c * steps
    win = jnp.where(lane < POOL_CH, s2[halo - 1:halo - 1 + steps],
                    jnp.where(lane < 2 * POOL_CH, s4[halo - 3:halo - 3 + steps],
                              jnp.where(lane < 3 * POOL_CH, s8[halo - 7:halo - 7 + steps],
                                        s16[halo - 15:halo - 15 + steps])))
    wlen = jnp.where(lane < POOL_CH, POOL_WINDOWS[0],
                     jnp.where(lane < 2 * POOL_CH, POOL_WINDOWS[1],
                               jnp.where(lane < 3 * POOL_CH, POOL_WINDOWS[2], POOL_WINDOWS[3])))
    cnt = jnp.minimum(tpos, wlen).astype(F32)
    pooled = (win / cnt - u).reshape(steps * bsub, GW)
    y = (_dot(pooled.astype(BF16), pw_ref[...]) * sc_ref[...]).reshape(steps, bsub, GW)
    if batch_major:
        tm_scr[...] = y
        for b in range(bsub):
            y_ref[b] = tm_scr[:, b, :]
    else:
        y_ref[...] = y

    @pl.when(c == pl.num_programs(1) - 1)
    def _():
        buf_ref[...] = f_scr[1:halo]


def _pool(u, buf0, lp, *, pos0, batch_major, layer=None):
    batch, seq, steps, tspec = _time_specs(u, batch_major)
    bsub = SUBLANES
    bblock = (POOL_BUF, bsub, GROUP_WIDTH)
    bspec = pl.BlockSpec(bblock, lambda b, c: (0, b, 0))
    if layer is None:
        bspec_in = bspec
    else:
        bspec_in = pl.BlockSpec((None,) + bblock, lambda b, c: (layer, 0, b, 0))
    return pl.pallas_call(
        functools.partial(_pool_body, steps=steps, pos0=pos0, batch_major=batch_major),
        grid=(batch // bsub, seq // steps),
        in_specs=[tspec, bspec_in, _pspec(lp["pool_w"]), _pspec(lp["pool_scale"])],
        out_specs=[tspec, bspec],
        out_shape=[jax.ShapeDtypeStruct(u.shape, F32), jax.ShapeDtypeStruct((POOL_BUF, batch, GROUP_WIDTH), F32)],
        scratch_shapes=[pltpu.VMEM((POOL_BUF + 1 + steps, bsub, GROUP_WIDTH), F32),
                        pltpu.VMEM((steps, bsub, GROUP_WIDTH), F32)],
        compiler_params=_cparams("parallel", "arbitrary"),
        name="pool",
    )(u, buf0, _parg(lp["pool_w"]), _parg(lp["pool_scale"]))


def _block_diag(blocks):
    n, g, r, c = blocks.shape
    eye = jnp.eye(g, dtype=blocks.dtype)
    return (eye[None, :, None, :, None] * blocks[:, :, :, None, :]).reshape(n, g * r, g * c)


def _stacked_params(P):
    row = lambda a: a.reshape(a.shape[0], 1, -1)
    pad_lanes = lambda a: jnp.pad(a, ((0, 0), (0, LANES - a.shape[1])))
    bf = lambda a: a.astype(BF16)
    w_in = P["w_in"]
    split = GROUP_WIDTH + SSD_CONV_DIM
    w_all = jnp.concatenate([w_in[:, :, :split], w_in[:, :, split + SSD_HEADS:],
                             jnp.pad(w_in[:, :, split:split + SSD_HEADS], ((0, 0), (0, 0), (0, LANES - SSD_HEADS)))],
                            axis=2)

    lam = lax.complex(P["s5_lam_re"], P["s5_lam_im"])
    a_bar = jnp.exp(lam * jnp.exp(P["s5_log_step"])[..., None])
    b_bar = ((a_bar - 1.0) / lam)[..., None] * lax.complex(P["s5_b_re"], P["s5_b_im"])
    b_t = jnp.swapaxes(b_bar, 2, 3)
    bmat = jnp.concatenate([_block_diag(jnp.real(b_t)), _block_diag(jnp.imag(b_t))], axis=2)
    c_t = jnp.swapaxes(lax.complex(P["s5_c_re"], P["s5_c_im"]), 2, 3)
    cmat = jnp.concatenate([_block_diag(jnp.real(c_t)), -_block_diag(jnp.imag(c_t))], axis=1)

    out = dict(
        norm_ffn1=row(P["norm_ffn1"]), ffn1_in=bf(P["ffn1_in"]), ffn1_out=bf(P["ffn1_out"]),
        norm_mix=row(P["norm_mix"]), w_all=bf(w_all),
        conv_w=P["ssd_conv_w"], conv_b=row(P["ssd_conv_b"]),
        dt_bias=row(pad_lanes(P["ssd_dt_bias"])), a_log=row(pad_lanes(P["ssd_a_log"])),
        a_neg_exp=row(jnp.repeat(-jnp.exp(P["ssd_a_log"]), SSD_HEAD_DIM, axis=1)),
        d_skip=row(jnp.repeat(P["ssd_d"], SSD_HEAD_DIM, axis=1)), ssd_norm=row(P["ssd_norm"]),
        s5_are=row(jnp.real(a_bar)), s5_aim=row(jnp.imag(a_bar)), s5_bmat=bf(bmat), s5_cmat=bf(cmat),
        s5_d=row(P["s5_d"]), s5_gw=bf(P["s5_glu_w"]), s5_gb=row(P["s5_glu_b"]),
        pool_w=bf(_block_diag(P["pool_w"])), pool_scale=row(P["pool_scale"]),
        w_out=bf(P["w_out"]),
        norm_ffn2=row(P["norm_ffn2"]), ffn2_in=bf(P["ffn2_in"]), ffn2_out=bf(P["ffn2_out"]),
    )
    for name in _RWKV_PARAM_NAMES:
        a = P["rwkv_" + name]
        out["rwkv_" + name] = bf(a) if name in ("w2", "a2", "g2") else row(a)
    return out


def _layer_params(stacked, l):
    lp = {k: _Layered((v, l)) for k, v in stacked.items()}
    lp["rwkv"] = {n: lp["rwkv_" + n] for n in _RWKV_PARAM_NAMES}
    lp["head_expand"] = jnp.pad(jnp.repeat(jnp.eye(SSD_HEADS, dtype=F32), SSD_HEAD_DIM, axis=1),
                                ((0, LANES - SSD_HEADS), (0, 0)))
    return lp


def _mixers_prompt(lp, proj, *, batch, seq):
    z, xbc, ur, us5, upool, dtr = proj
    y_ssd, conv_new, ssd_new = _ssd(z, xbc, dtr, lp, batch=batch, seq=seq)
    y_rwkv, shift_new, rwkv_new = _rwkv(ur, lp["rwkv"], batch=batch, seq=seq)
    zeros = jnp.zeros((batch, S5_WIDTH), F32)
    bm = lambda a: a.reshape(batch, seq, a.shape[-1])
    rows = lambda a: a.reshape(batch * seq, a.shape[-1])
    y_s5, s5re, s5im = _s5(bm(us5), zeros, zeros, lp, batch_major=True)
    y_pool, pool_new = _pool(bm(upool), jnp.zeros((POOL_BUF, batch, GROUP_WIDTH), F32), lp, pos0=0,
                             batch_major=True)
    ys = (y_ssd, y_rwkv, rows(y_s5), rows(y_pool))
    states = (conv_new, ssd_new, shift_new, rwkv_new, s5re.reshape(batch, S5_GROUPS, S5_STATE),
              s5im.reshape(batch, S5_GROUPS, S5_STATE), jnp.swapaxes(pool_new, 0, 1))
    return ys, states


def _mixers_decode(lp, proj, states, done, *, batch, seq, layer):
    z, xbc, ur, us5, upool, dtr = proj
    shift0, s5re0, s5im0 = (states[i][layer] for i in (2, 4, 5))
    ssd_done, rwkv_done = (done[1], done[3]) if layer else (None, None)
    y_ssd, conv_new, ssd_new = _ssd_step(z, xbc, dtr, states[0], states[1], ssd_done, lp, batch=batch, seq=seq,
                                         layer=layer)
    y_rwkv, rwkv_new = _rwkv_step(ur, shift0, states[3], rwkv_done, lp["rwkv"], batch=batch, seq=seq, layer=layer)
    shift_new = ur[(seq - 1) * batch:, :]
    tm = lambda a: a.reshape(seq, batch, a.shape[-1])
    y_s5, s5re, s5im = _s5(tm(us5), s5re0.reshape(batch, S5_WIDTH), s5im0.reshape(batch, S5_WIDTH), lp,
                           batch_major=False)
    y_pool, pool_new = _pool(tm(upool), states[6], lp, pos0=PAST_LEN, batch_major=False, layer=layer)
    rows = lambda a: a.reshape(seq * batch, a.shape[-1])
    ys = (y_ssd, y_rwkv, rows(y_s5), rows(y_pool))
    new_states = (jnp.swapaxes(conv_new, 0, 1), ssd_new, shift_new, rwkv_new,
                  s5re.reshape(batch, S5_GROUPS, S5_STATE), s5im.reshape(batch, S5_GROUPS, S5_STATE),
                  jnp.swapaxes(pool_new, 0, 1))
    return ys, new_states


_WIDTHS = (GROUP_WIDTH, SSD_CONV_DIM, RWKV_PROJ, GROUP_WIDTH, GROUP_WIDTH, LANES)


def _trunk(x, layer_params, norm_final, mixers):
    states = []
    mix, lp = None, None
    for l, lp_next in enumerate(layer_params):
        if l > 0:
            x = _ffn(x, lp["norm_ffn2"], lp["ffn2_in"], lp["ffn2_out"], mix=mix, wmix=lp["w_out"])
        lp = lp_next
        x = _ffn(x, lp["norm_ffn1"], lp["ffn1_in"], lp["ffn1_out"])
        proj = _inproj(x, lp["norm_mix"], lp["w_all"], _WIDTHS)
        mix, st = mixers(l, lp, proj, states[-1] if states else None)
        states.append(st)
    x = _ffn(x, lp["norm_ffn2"], lp["ffn2_in"], lp["ffn2_out"], mix=mix, wmix=lp["w_out"], gf=norm_final)
    return x, states


def kernel(x_prompt, x_sample, state_ssd_conv, state_ssd, state_rwkv_shift, state_rwkv, state_s5_re, state_s5_im, state_pool, norm_ffn1, ffn1_in, ffn1_out, norm_mix, w_in, ssd_conv_w, ssd_conv_b, ssd_dt_bias, ssd_a_log, ssd_d, ssd_norm, rwkv_mu, rwkv_w0, rwkv_w2, rwkv_a0, rwkv_a2, rwkv_g2, rwkv_k_k, rwkv_k_a, rwkv_r_k, rwkv_ln_g, rwkv_ln_b, s5_lam_re, s5_lam_im, s5_log_step, s5_b_re, s5_b_im, s5_c_re, s5_c_im, s5_d, s5_glu_w, s5_glu_b, pool_w, pool_scale, w_out, norm_ffn2, ffn2_in, ffn2_out, norm_final):
    P = dict(norm_ffn1=norm_ffn1, ffn1_in=ffn1_in, ffn1_out=ffn1_out, norm_mix=norm_mix, w_in=w_in,
             ssd_conv_w=ssd_conv_w, ssd_conv_b=ssd_conv_b, ssd_dt_bias=ssd_dt_bias, ssd_a_log=ssd_a_log,
             ssd_d=ssd_d, ssd_norm=ssd_norm, rwkv_mu=rwkv_mu, rwkv_w0=rwkv_w0, rwkv_w2=rwkv_w2, rwkv_a0=rwkv_a0,
             rwkv_a2=rwkv_a2, rwkv_g2=rwkv_g2, rwkv_k_k=rwkv_k_k, rwkv_k_a=rwkv_k_a,
             rwkv_r_k=rwkv_r_k.reshape(rwkv_r_k.shape[0], -1), rwkv_ln_g=rwkv_ln_g, rwkv_ln_b=rwkv_ln_b,
             s5_lam_re=s5_lam_re, s5_lam_im=s5_lam_im, s5_log_step=s5_log_step, s5_b_re=s5_b_re, s5_b_im=s5_b_im,
             s5_c_re=s5_c_re, s5_c_im=s5_c_im, s5_d=s5_d, s5_glu_w=s5_glu_w, s5_glu_b=s5_glu_b, pool_w=pool_w,
             pool_scale=pool_scale, w_out=w_out, norm_ffn2=norm_ffn2, ffn2_in=ffn2_in, ffn2_out=ffn2_out)
    depth = norm_ffn1.shape[0]
    bp, tp, d = x_prompt.shape
    bs, ts, _ = x_sample.shape
    stacked = _stacked_params(P)
    layer_params = [_layer_params(stacked, l) for l in range(depth)]
    gf = norm_final.reshape(1, -1)
    sample_states = (state_ssd_conv, state_ssd, state_rwkv_shift, state_rwkv, state_s5_re, state_s5_im, state_pool)
    rwkv_rows = RWKV_HEADS * RWKV_HEAD
    decode_states = (jnp.swapaxes(state_ssd_conv, 1, 2), state_ssd, state_rwkv_shift,
                     jnp.transpose(state_rwkv, (0, 2, 3, 4, 1)).reshape(depth, rwkv_rows, RWKV_HEAD, bs),
                     state_s5_re, state_s5_im, jnp.swapaxes(state_pool, 1, 2))

    y_p, st_p = _trunk(x_prompt.reshape(bp * tp, d), layer_params, gf,
                       lambda l, lp, proj, done: _mixers_prompt(lp, proj, batch=bp, seq=tp))
    x_s = jnp.swapaxes(x_sample, 0, 1).reshape(ts * bs, d)
    y_s, st_s = _trunk(x_s, layer_params, gf,
                       lambda l, lp, proj, done: _mixers_decode(lp, proj, decode_states, done,
                                                                batch=bs, seq=ts, layer=l))
    outs = [y_p.reshape(bp, tp, d), jnp.swapaxes(y_s.reshape(ts, bs, d), 0, 1)]
    for i, ref_state in enumerate(sample_states):
        outs.append(jnp.stack([st[i] for st in st_p]))
        if i == 1:
            outs.append(st_s[-1][i].reshape(ref_state.shape))
        elif i == 3:
            s_new = st_s[-1][i].reshape(depth, RWKV_HEADS, RWKV_HEAD, RWKV_HEAD, bs)
            outs.append(jnp.transpose(s_new, (0, 4, 1, 2, 3)))
        else:
            outs.append(jnp.stack([st[i] for st in st_s]))
    return tuple(outs)
```

```python
import functools
import math

import jax
import jax.numpy as jnp
from jax import lax
from jax.experimental import pallas as pl
from jax.experimental.pallas import tpu as pltpu

F32 = jnp.float32
BF16 = jnp.bfloat16
HIGHEST = lax.Precision.HIGHEST

SUBLANES = 8
LANES = 128
VMEM_LIMIT_BYTES = 56 * 1024 * 1024

GROUP_WIDTH = 256
SSD_HEAD_DIM = 64
SSD_HEADS = 4
SSD_GROUPS = 2
SSD_STATE = 128
SSD_CONV = 4
SSD_CONV_DIM = GROUP_WIDTH + 2 * SSD_GROUPS * SSD_STATE
SSD_CHUNK = 128
SSD_GROUP = 4
LOG2_E = math.log2(math.e)
RWKV_HEAD = 64
RWKV_HEADS = 4
RWKV_PROJ = 1024
RWKV_LN_EPS = 64e-5
RWKV_CHUNK = 64
RWKV_GROUP = 8
S5_GROUP_CH = 16
S5_GROUPS = 16
S5_STATE = 64
S5_WIDTH = S5_GROUPS * S5_STATE
POOL_WINDOWS = (2, 4, 8, 16)
POOL_CH = 64
POOL_BUF = 15
RMS_EPS = 1e-6
PAST_LEN = 16384

ROW_TILE = 512
FFN_CHUNK = 256
TM_CHUNK = 64
S5_SUB = 16
SSD_STEP_TILES = 16
RWKV_STEP_TILES = 16


def _cparams(*sem):
    return pltpu.CompilerParams(dimension_semantics=sem, vmem_limit_bytes=VMEM_LIMIT_BYTES)


def _dot(a, b, **kw):
    return jnp.dot(a, b, preferred_element_type=F32, **kw)


def _dot_nt(a, b):
    return lax.dot_general(a, b, (((1,), (1,)), ((), ())), preferred_element_type=F32)


def _dot_tn(a, b):
    return lax.dot_general(a, b, (((0,), (0,)), ((), ())), preferred_element_type=F32)


def _sigmoid(x):
    return 1.0 / (1.0 + jnp.exp(-x))


def _silu(x):
    return x * _sigmoid(x)


def _softplus(x):
    return jnp.maximum(x, 0.0) + jnp.log(1.0 + jnp.exp(-jnp.abs(x)))


def _gelu_tanh(x):
    c = math.sqrt(2.0 / math.pi)
    return x * (0.5 * (1.0 + jnp.tanh(c * (x + 0.044715 * (x * x * x)))))


def _rms(x, g):
    return x * lax.rsqrt(jnp.mean(x * x, axis=-1, keepdims=True) + RMS_EPS) * g


def _full_spec(shape):
    n = len(shape)
    return pl.BlockSpec(shape, lambda *_: (0,) * n)


class _Layered(tuple):
    pass


def _pspec(p):
    if isinstance(p, _Layered):
        a, l = p
        return pl.BlockSpec((None,) + a.shape[1:], lambda *_: (l,) + (0,) * (a.ndim - 1))
    return _full_spec(p.shape)


def _parg(p):
    return p[0] if isinstance(p, _Layered) else p


def _ffn_body(*refs, has_mix, final_norm):
    it = iter(refs)
    x_ref = next(it)
    x = x_ref[...]
    if has_mix:
        y_refs = [next(it) for _ in range(4)]
        wmix_ref = next(it)
        for i, y_ref in enumerate(y_refs):
            x = x + _dot(y_ref[...].astype(BF16), wmix_ref[i * GROUP_WIDTH:(i + 1) * GROUP_WIDTH, :])
    g_ref, wi_ref, wo_ref = next(it), next(it), next(it)
    gf_ref = next(it) if final_norm else None
    o_ref = next(it)
    h = _rms(x, g_ref[...]).astype(BF16)
    d_ff = wo_ref.shape[0]
    acc = jnp.zeros_like(x)
    for c in range(d_ff // FFN_CHUNK):
        lo = c * FFN_CHUNK
        gate = _dot(h, wi_ref[:, lo:lo + FFN_CHUNK])
        up = _dot(h, wi_ref[:, d_ff + lo:d_ff + lo + FFN_CHUNK])
        act = (_silu(gate) * up).astype(BF16)
        acc = acc + _dot(act, wo_ref[lo:lo + FFN_CHUNK, :])
    x = x + 0.5 * acc
    if final_norm:
        x = _rms(x, gf_ref[...])
    o_ref[...] = x


def _ffn(x, g, wi, wo, mix=None, wmix=None, gf=None):
    rows, d = x.shape
    row_spec = lambda w: pl.BlockSpec((ROW_TILE, w), lambda i: (i, 0))
    args, specs = [x], [row_spec(d)]
    if mix is not None:
        for y in mix:
            args.append(y)
            specs.append(row_spec(y.shape[1]))
        args.append(_parg(wmix))
        specs.append(_pspec(wmix))
    for a in (g, wi, wo) + ((gf,) if gf is not None else ()):
        args.append(_parg(a))
        specs.append(_pspec(a))
    return pl.pallas_call(
        functools.partial(_ffn_body, has_mix=mix is not None, final_norm=gf is not None),
        grid=(rows // ROW_TILE,),
        in_specs=specs,
        out_specs=row_spec(d),
        out_shape=jax.ShapeDtypeStruct((rows, d), F32),
        compiler_params=_cparams("parallel"),
        name="ffn",
    )(*args)


def _inproj_body(x_ref, g_ref, w_ref, *o_refs):
    h = _rms(x_ref[...], g_ref[...]).astype(BF16)
    off = 0
    for o_ref in o_refs:
        n = o_ref.shape[-1]
        o_ref[...] = _dot(h, w_ref[:, off:off + n])
        off += n


def _inproj(x, g, w, widths):
    rows, d = x.shape
    row_spec = lambda w_: pl.BlockSpec((ROW_TILE, w_), lambda i: (i, 0))
    return pl.pallas_call(
        _inproj_body,
        grid=(rows // ROW_TILE,),
        in_specs=[row_spec(d), _pspec(g), _pspec(w)],
        out_specs=[row_spec(n) for n in widths],
        out_shape=[jax.ShapeDtypeStruct((rows, n), F32) for n in widths],
        compiler_params=_cparams("parallel"),
        name="inproj",
    )(x, _parg(g), _parg(w))


def _ssd_body(z_ref, xbc_ref, dt_ref, cw_ref, cb_ref, dtb_ref, alog_ref, dsk_ref, ng_ref,
              y_ref, conv_ref, hout_ref, xpad_scr, h_scr, *, chunk, group):
    L, G = chunk, group
    GL = G * L
    c = pl.program_id(1)
    pad = SUBLANES
    halo = SSD_CONV - 1
    hpg = SSD_HEADS // SSD_GROUPS
    assert hpg == 2 and hpg * SSD_HEAD_DIM == SSD_STATE

    @pl.when(c == 0)
    def _():
        xpad_scr[0:pad, :] = jnp.zeros((pad, SSD_CONV_DIM), F32)
        h_scr[...] = jnp.zeros(h_scr.shape, F32)

    xpad_scr[pad:pad + GL, :] = xbc_ref[...]
    xfull = xpad_scr[...]
    conv = cb_ref[...] + cw_ref[halo:halo + 1, :] * xfull[pad:pad + GL]
    for j in range(halo):
        conv = conv + cw_ref[j:j + 1, :] * pltpu.roll(xfull, halo - j, axis=0)[pad:pad + GL]
    xpad_scr[pad - halo:pad, :] = xpad_scr[pad + GL - halo:pad + GL, :]
    conv = _silu(conv)
    xs = conv[:, 0:GROUP_WIDTH]
    bm = conv[:, GROUP_WIDTH:2 * GROUP_WIDTH].astype(BF16)
    cm = conv[:, 2 * GROUP_WIDTH:3 * GROUP_WIDTH].astype(BF16)

    row = lax.broadcasted_iota(jnp.int32, (L, L), 0)
    col = lax.broadcasted_iota(jnp.int32, (L, L), 1)
    causal = row >= col
    tril = jnp.where(causal, 1.0, 0.0).astype(F32)
    dt = _softplus(dt_ref[...] + dtb_ref[...])
    da = dt * (-jnp.exp(alog_ref[...]) * LOG2_E)
    acs = [_dot(tril, da[i * L:(i + 1) * L, :], precision=HIGHEST) for i in range(G)]
    acs_t = [a.T for a in acs]
    e_acs = [jnp.exp2(a) for a in acs]
    e_end = [jnp.exp2(a[L - 1:L, :] - a) for a in acs]
    e_last = [jnp.exp2(a[L - 1:L, :]) for a in acs]

    keys = [(i, g) for i in range(G) for g in range(SSD_GROUPS)]
    rows_of = lambda x, i: x[i * L:(i + 1) * L]
    lanes_of = lambda x, g: x[:, g * SSD_STATE:(g + 1) * SSD_STATE]
    lane_lo = lax.broadcasted_iota(jnp.int32, (L, hpg * SSD_HEAD_DIM), 1) < SSD_HEAD_DIM
    row_lo = lax.broadcasted_iota(jnp.int32, (hpg * SSD_HEAD_DIM, SSD_STATE), 0) < SSD_HEAD_DIM
    head_cols = lambda a, g: jnp.where(lane_lo, a[:, g * hpg:g * hpg + 1], a[:, g * hpg + 1:g * hpg + 2])
    bg = {(i, g): lanes_of(rows_of(bm, i), g) for i, g in keys}
    cg = {(i, g): lanes_of(rows_of(cm, i), g) for i, g in keys}
    scores = {k: _dot_nt(cg[k], bg[k]) for k in keys}
    xdt = {(i, g): lanes_of(rows_of(xs, i), g) * head_cols(rows_of(dt, i), g) for i, g in keys}
    decay = {(i, h): jnp.exp2(jnp.where(causal, acs[i][:, h:h + 1] - acs_t[i][h:h + 1, :], -jnp.inf))
             for i in range(G) for h in range(SSD_HEADS)}
    p_mat = {(i, g): jnp.concatenate([(scores[(i, g)] * decay[(i, g * hpg + k)]).astype(BF16) for k in range(hpg)],
                                     axis=1) for i, g in keys}
    y_in = {k: _dot(p_mat[k], _bd(xdt[k].astype(BF16))) for k in keys}
    st = {(i, g): _dot_tn((xdt[(i, g)] * head_cols(e_end[i], g)).astype(BF16), bg[(i, g)]) for i, g in keys}

    y_rows = []
    for i in range(G):
        ys = []
        for g in range(SSD_GROUPS):
            h_prev = h_scr[g * hpg:(g + 1) * hpg].reshape(hpg * SSD_HEAD_DIM, SSD_STATE)
            ys.append(y_in[(i, g)] + _dot_nt(cg[(i, g)], h_prev.astype(BF16)) * head_cols(e_acs[i], g))
            keep = jnp.where(row_lo, e_last[i][:, g * hpg:g * hpg + 1], e_last[i][:, g * hpg + 1:g * hpg + 2])
            h_scr[g * hpg:(g + 1) * hpg] = (h_prev * keep + st[(i, g)]).reshape(hpg, SSD_HEAD_DIM, SSD_STATE)
        y_rows.append(jnp.concatenate(ys, axis=-1))
    y = jnp.concatenate(y_rows, axis=0) + xs * dsk_ref[...]
    y = y * _silu(z_ref[...])
    y_ref[...] = _rms(y, ng_ref[...])

    @pl.when(c == pl.num_programs(1) - 1)
    def _():
        hout_ref[0] = h_scr[...]
        conv_ref[0] = xpad_scr[pad - halo:pad, :]


def _ssd(z, xbc, dtr, lp, *, batch, seq):
    chunk = SSD_CHUNK
    rows = chunk * SSD_GROUP
    nc = seq // rows
    rspec = lambda w: pl.BlockSpec((rows, w), lambda b, c: (b * nc + c, 0))
    consts = (lp["conv_w"], lp["conv_b"], lp["dt_bias"], lp["a_log"], lp["d_skip"], lp["ssd_norm"])
    return pl.pallas_call(
        functools.partial(_ssd_body, chunk=chunk, group=SSD_GROUP),
        grid=(batch, nc),
        in_specs=[rspec(GROUP_WIDTH), rspec(SSD_CONV_DIM), rspec(LANES)] + [_pspec(a) for a in consts],
        out_specs=[rspec(GROUP_WIDTH),
                   pl.BlockSpec((1, SSD_CONV - 1, SSD_CONV_DIM), lambda b, c: (b, 0, 0)),
                   pl.BlockSpec((1, SSD_HEADS, SSD_HEAD_DIM, SSD_STATE), lambda b, c: (b, 0, 0, 0))],
        out_shape=[jax.ShapeDtypeStruct((batch * seq, GROUP_WIDTH), F32),
                   jax.ShapeDtypeStruct((batch, SSD_CONV - 1, SSD_CONV_DIM), F32),
                   jax.ShapeDtypeStruct((batch, SSD_HEADS, SSD_HEAD_DIM, SSD_STATE), F32)],
        scratch_shapes=[pltpu.VMEM((SUBLANES + rows, SSD_CONV_DIM), F32),
                        pltpu.VMEM((SSD_HEADS, SSD_HEAD_DIM, SSD_STATE), F32)],
        compiler_params=_cparams("parallel", "arbitrary"),
        name="ssd",
    )(z, xbc, dtr, *[_parg(a) for a in consts])


def _ssd_step_body(z_ref, xbc_ref, dt_ref, conv0_ref, h0_ref, *rest, seq, batch, layer):
    hdone_ref, rest = (rest[0], rest[1:]) if layer else (None, rest)
    (cw_ref, cb_ref, dtb_ref, aneg_ref, dsk_ref, ng_ref, hexp_ref, y_ref, conv_ref, hout_ref,
     xs_scr, bm_scr, cm_scr, xdt_scr, dec_scr, y_scr) = rest
    T, B = seq, batch
    if layer:
        hout_ref[0:layer] = hdone_ref[...]
    GW = GROUP_WIDTH
    j = pl.program_id(0)
    tiles = SSD_STEP_TILES

    @pl.when(j == 0)
    def _():
        rows = [conv0_ref[i] for i in range(SSD_CONV - 1)]
        rows += [xbc_ref[t * B:(t + 1) * B, :] for t in range(T)]
        for t in range(T):
            conv = cb_ref[...] + cw_ref[0:1, :] * rows[t]
            for i in range(1, SSD_CONV):
                conv = conv + cw_ref[i:i + 1, :] * rows[t + i]
            conv = _silu(conv)
            xs = conv[:, 0:GW]
            xs_scr[t] = xs
            for g in range(SSD_GROUPS):
                bm_scr[t, g] = conv[:, GW + g * SSD_STATE:GW + (g + 1) * SSD_STATE].T
                cm_scr[t, g] = conv[:, 2 * GW + g * SSD_STATE:2 * GW + (g + 1) * SSD_STATE].T
            dt = _softplus(dt_ref[t * B:(t + 1) * B, :] + dtb_ref[...])
            dte = _dot(dt, hexp_ref[...], precision=HIGHEST)
            xdt_scr[t] = (xs * dte).T
            dec_scr[t] = jnp.exp(dte * aneg_ref[...]).T
        for i in range(SSD_CONV - 1):
            conv_ref[i] = rows[T + i]

    hp0 = j * tiles
    grp = hp0 // (SSD_HEAD_DIM * (SSD_HEADS // SSD_GROUPS))
    for q in range(tiles):
        hp = pl.ds(hp0 + q, 1)
        h = h0_ref[:, q, :].T
        for t in range(T):
            h = h * dec_scr[t, hp, :] + bm_scr[t, grp] * xdt_scr[t, hp, :]
            y_scr[t, hp, :] = jnp.sum(h * cm_scr[t, grp], axis=0, keepdims=True)
        hout_ref[layer, :, q, :] = h.T

    @pl.when(j == pl.num_programs(0) - 1)
    def _():
        for t in range(T):
            y = y_scr[t].T + xs_scr[t] * dsk_ref[...]
            y = y * _silu(z_ref[t * B:(t + 1) * B, :])
            y_ref[t * B:(t + 1) * B, :] = _rms(y, ng_ref[...])


def _layer_state_specs(layer, block, axis):
    idx = lambda first: (lambda j: (first,) + tuple(j if a == axis else 0 for a in range(len(block))))
    cur = pl.BlockSpec((None,) + block, idx(layer))
    prev = [pl.BlockSpec((layer,) + block, idx(0))] if layer else []
    out = pl.BlockSpec((layer + 1,) + block, idx(0))
    return cur, prev, out


def _ssd_step(z, xbc, dtr, conv_all, h_all, h_done, lp, *, batch, seq, layer):
    n = batch * seq
    srows = SSD_HEADS * SSD_HEAD_DIM
    consts = (lp["conv_w"], lp["conv_b"], lp["dt_bias"], lp["a_neg_exp"], lp["d_skip"], lp["ssd_norm"], lp["head_expand"])
    hspec, prev_specs, hout_spec = _layer_state_specs(layer, (batch, SSD_STEP_TILES, SSD_STATE), 1)
    prev_args = [h_done] if layer else []
    cshape = (SSD_CONV - 1, batch, SSD_CONV_DIM)
    return pl.pallas_call(
        functools.partial(_ssd_step_body, seq=seq, batch=batch, layer=layer),
        grid=(srows // SSD_STEP_TILES,),
        in_specs=[_full_spec((n, GROUP_WIDTH)), _full_spec((n, SSD_CONV_DIM)), _full_spec((n, LANES)),
                  pl.BlockSpec((None,) + cshape, lambda j: (layer, 0, 0, 0)), hspec] + prev_specs
                 + [_pspec(a) for a in consts],
        out_specs=[_full_spec((n, GROUP_WIDTH)), _full_spec(cshape), hout_spec],
        out_shape=[jax.ShapeDtypeStruct((n, GROUP_WIDTH), F32),
                   jax.ShapeDtypeStruct(cshape, F32),
                   jax.ShapeDtypeStruct((layer + 1, batch, srows, SSD_STATE), F32)],
        scratch_shapes=[pltpu.VMEM((seq, batch, GROUP_WIDTH), F32),
                        pltpu.VMEM((seq, SSD_GROUPS, SSD_STATE, batch), F32),
                        pltpu.VMEM((seq, SSD_GROUPS, SSD_STATE, batch), F32),
                        pltpu.VMEM((seq, GROUP_WIDTH, batch), F32),
                        pltpu.VMEM((seq, GROUP_WIDTH, batch), F32),
                        pltpu.VMEM((seq, GROUP_WIDTH, batch), F32)],
        compiler_params=_cparams("arbitrary"),
        name="ssd_step",
    )(z, xbc, dtr, conv_all, h_all.reshape(h_all.shape[0], batch, srows, SSD_STATE),
      *prev_args, *[_parg(a) for a in consts])


PAIR = 2 * RWKV_HEAD
RWKV_PAIRS = RWKV_HEADS // 2


def _bd(x):
    half = x.shape[1] // 2
    lane = lax.broadcasted_iota(jnp.int32, x.shape, 1)
    zero = jnp.zeros_like(x)
    return jnp.concatenate([jnp.where(lane < half, x, zero), jnp.where(lane >= half, x, zero)], axis=0)


def _half_sums(x, lo):
    s_lo = jnp.sum(jnp.where(lo, x, 0.0), axis=-1, keepdims=True)
    s_hi = jnp.sum(jnp.where(lo, 0.0, x), axis=-1, keepdims=True)
    return jnp.where(lo, s_lo, s_hi)


def _head_sum(x):
    lo = lax.broadcasted_iota(jnp.int32, (x.shape[0], PAIR), 1) < RWKV_HEAD
    return jnp.concatenate([_half_sums(x[:, p * PAIR:(p + 1) * PAIR], lo) for p in range(RWKV_PAIRS)], axis=-1)


def _rwkv_pointwise(u, prev, mu_ref, w0_ref, w2_ref, a0_ref, a2_ref, g2_ref, kk_ref, ka_ref):
    GW = GROUP_WIDTH
    xs = u + (prev - u) * mu_ref[...]
    r = xs[:, 0:GW]
    k = xs[:, GW:2 * GW]
    v = xs[:, 2 * GW:3 * GW]
    wd = xs[:, 3 * GW:3 * GW + 64]
    ad = xs[:, 3 * GW + 64:3 * GW + 128]
    gd = xs[:, 3 * GW + 128:3 * GW + 256]
    w_lin = w0_ref[...] + _dot(jnp.tanh(wd).astype(BF16), w2_ref[...])
    logdecay = -jnp.exp(-_softplus(-w_lin) - 0.5)
    a = _sigmoid(a0_ref[...] + _dot(ad.astype(BF16), a2_ref[...]))
    g = _dot(_sigmoid(gd).astype(BF16), g2_ref[...])
    kk = k * kk_ref[...]
    kk = kk / jnp.maximum(jnp.sqrt(_head_sum(kk * kk)), 1e-12)
    k = k * (1.0 + (a - 1.0) * ka_ref[...])
    return r, k, v, logdecay, a, g, kk


def _rwkv_finish(y, r, k, v, g, rk_ref, lng_ref, lnb_ref):
    mean = _head_sum(y) * (1.0 / RWKV_HEAD)
    yc = y - mean
    var = _head_sum(yc * yc) * (1.0 / RWKV_HEAD)
    y = yc * lax.rsqrt(var + RWKV_LN_EPS) * lng_ref[...] + lnb_ref[...]
    bonus = _head_sum(r * k * rk_ref[...]) * v
    return (y + bonus) * g


def _rwkv_body(u_ref, mu_ref, w0_ref, w2_ref, a0_ref, a2_ref, g2_ref, kk_ref, ka_ref, rk_ref,
               lng_ref, lnb_ref, y_ref, shift_ref, sout_ref, upad_scr, s_scr, *, chunk, group):
    L, G = chunk, group
    GL = G * L
    c = pl.program_id(1)
    pad = SUBLANES

    @pl.when(c == 0)
    def _():
        upad_scr[0:pad, :] = jnp.zeros((pad, RWKV_PROJ), F32)
        s_scr[...] = jnp.zeros(s_scr.shape, F32)

    u = u_ref[...]
    upad_scr[pad:pad + GL, :] = u
    prev = pltpu.roll(upad_scr[...], 1, axis=0)[pad:pad + GL]
    upad_scr[pad - 1:pad, :] = u[GL - 1:GL, :]
    r, k, v, logdecay, a, g, kk = _rwkv_pointwise(u, prev, mu_ref, w0_ref, w2_ref, a0_ref, a2_ref, g2_ref,
                                                  kk_ref, ka_ref)

    tril = jnp.where(lax.broadcasted_iota(jnp.int32, (L, L), 0) >= lax.broadcasted_iota(jnp.int32, (L, L), 1),
                     1.0, 0.0).astype(F32)
    cl = jnp.concatenate([_dot(tril, logdecay[i * L:(i + 1) * L, :], precision=HIGHEST) for i in range(G)], axis=0)
    e_in = jnp.exp(cl)
    e_inv = jnp.exp(-cl)
    r_t = r * e_in
    r_tb = r_t.astype(BF16)
    a_tb = (-kk * jnp.exp(cl - logdecay)).astype(BF16)
    b_tb = (kk * a * e_inv).astype(BF16)
    k_tb = (k * e_inv).astype(BF16)
    vb = v.astype(BF16)

    row = lax.broadcasted_iota(jnp.int32, (L, PAIR), 0)
    colh = lax.broadcasted_iota(jnp.int32, (L, PAIR), 1) & (RWKV_HEAD - 1)
    strict = row > colh
    incl = row >= colh
    eye_pair = jnp.where(row == colh, 1.0, 0.0).astype(F32)
    lane_lo = lax.broadcasted_iota(jnp.int32, (RWKV_HEAD, PAIR), 1) < RWKV_HEAD
    same_head = (lax.broadcasted_iota(jnp.int32, (PAIR, PAIR), 0) < RWKV_HEAD) == \
                (lax.broadcasted_iota(jnp.int32, (PAIR, PAIR), 1) < RWKV_HEAD)

    streams = [(i, p) for i in range(G) for p in range(RWKV_PAIRS)]
    ns = len(streams)
    blk = lambda x, i, p: x[i * L:(i + 1) * L, p * PAIR:(p + 1) * PAIR]
    lhs = [jnp.concatenate([blk(a_tb, i, p), blk(r_tb, i, p)], axis=0) for i, p in streams]
    m_ab = [_dot_nt(lhs[s], _bd(blk(b_tb, i, p))) for s, (i, p) in enumerate(streams)]
    m_ak = [_dot_nt(lhs[s], _bd(blk(k_tb, i, p))) for s, (i, p) in enumerate(streams)]
    n_ab = [jnp.where(strict, m[0:L], 0.0) for m in m_ab]
    m_rb = [jnp.where(incl, m[L:2 * L], 0.0).astype(BF16) for m in m_ab]
    n_ak = [jnp.where(strict, m[0:L], 0.0).astype(BF16) for m in m_ak]
    m_rk = [jnp.where(incl, m[L:2 * L], 0.0).astype(BF16) for m in m_ak]
    tinv = [eye_pair + n for n in n_ab]
    pwb = [n.astype(BF16) for n in n_ab]
    pw = [_dot(x, _bd(x)) for x in pwb]
    for _ in range(int(math.log2(L)) - 2):
        pwb = [x.astype(BF16) for x in pw]
        both = [_dot(jnp.concatenate([pwb[s], tinv[s].astype(BF16)], axis=0), _bd(pwb[s])) for s in range(ns)]
        pw = [x[0:L] for x in both]
        tinv = [tinv[s] + both[s][L:2 * L] for s in range(ns)]
    pwb = [x.astype(BF16) for x in pw]
    tinv = [tinv[s] + _dot(tinv[s].astype(BF16), _bd(pwb[s])) for s in range(ns)]
    tinvb = [x.astype(BF16) for x in tinv]
    nv_mv = [_dot(jnp.concatenate([n_ak[s], m_rk[s]], axis=0), _bd(blk(vb, i, p))) for s, (i, p) in enumerate(streams)]
    wu = [_dot(tinvb[s], jnp.concatenate([_bd(blk(a_tb, i, p)), _bd(nv_mv[s][0:L].astype(BF16))], axis=1))
          for s, (i, p) in enumerate(streams)]
    wub = [x.astype(BF16) for x in wu]
    qy = [_dot(m_rb[s], jnp.concatenate([_bd(wub[s][:, 0:PAIR]), _bd(wub[s][:, PAIR:2 * PAIR])], axis=1))
          for s in range(ns)]
    q = [(blk(r_t, i, p) + qy[s][:, 0:PAIR]).astype(BF16) for s, (i, p) in enumerate(streams)]
    y_loc = [qy[s][:, PAIR:2 * PAIR] + nv_mv[s][L:2 * L] for s in range(ns)]
    zeros_b = jnp.zeros((L, PAIR), BF16)
    mg = [_dot_tn(jnp.concatenate([wub[s], jnp.concatenate([zeros_b, blk(vb, i, p)], axis=1)], axis=0),
                  jnp.concatenate([blk(b_tb, i, p), blk(k_tb, i, p)], axis=0))
          for s, (i, p) in enumerate(streams)]
    p_end = [e_in[(i + 1) * L - 1:(i + 1) * L, p * PAIR:(p + 1) * PAIR] for i, p in streams]
    m_t = [(jnp.where(same_head, mg[s][0:PAIR], 0.0) * p_end[s]).astype(BF16) for s in range(ns)]
    g_t = [jnp.where(lane_lo, mg[s][PAIR:PAIR + RWKV_HEAD], mg[s][PAIR + RWKV_HEAD:2 * PAIR]) * p_end[s]
           for s in range(ns)]

    y_rows = []
    for i in range(G):
        y_pairs = []
        for p in range(RWKV_PAIRS):
            s = i * RWKV_PAIRS + p
            s0 = s_scr[p]
            s0b = s0.astype(BF16)
            y_pairs.append(_dot_nt(q[s], _bd(s0b)) + y_loc[s])
            s_scr[p] = s0 * p_end[s] + _dot(s0b, m_t[s]) + g_t[s]
        y_rows.append(jnp.concatenate(y_pairs, axis=-1))
    y = jnp.concatenate(y_rows, axis=0)
    y_ref[...] = _rwkv_finish(y, r, k, v, g, rk_ref, lng_ref, lnb_ref)

    @pl.when(c == pl.num_programs(1) - 1)
    def _():
        sout_ref[0] = s_scr[...]
        shift_ref[0] = upad_scr[pad - 1:pad, :]


_RWKV_PARAM_NAMES = ("mu", "w0", "w2", "a0", "a2", "g2", "k_k", "k_a", "r_k", "ln_g", "ln_b")


def _rwkv(u, p, *, batch, seq):
    rows = RWKV_CHUNK * RWKV_GROUP
    nc = seq // rows
    params = [p[n] for n in _RWKV_PARAM_NAMES]
    sspec = pl.BlockSpec((1, RWKV_PAIRS, RWKV_HEAD, PAIR), lambda b, c: (b, 0, 0, 0))
    y, shift, s_last = pl.pallas_call(
        functools.partial(_rwkv_body, chunk=RWKV_CHUNK, group=RWKV_GROUP),
        grid=(batch, nc),
        in_specs=[pl.BlockSpec((rows, RWKV_PROJ), lambda b, c: (b * nc + c, 0))] + [_pspec(a) for a in params],
        out_specs=[pl.BlockSpec((rows, GROUP_WIDTH), lambda b, c: (b * nc + c, 0)),
                   pl.BlockSpec((1, 1, RWKV_PROJ), lambda b, c: (b, 0, 0)), sspec],
        out_shape=[jax.ShapeDtypeStruct((batch * seq, GROUP_WIDTH), F32),
                   jax.ShapeDtypeStruct((batch, 1, RWKV_PROJ), F32),
                   jax.ShapeDtypeStruct((batch, RWKV_PAIRS, RWKV_HEAD, PAIR), F32)],
        scratch_shapes=[pltpu.VMEM((SUBLANES + rows, RWKV_PROJ), F32),
                        pltpu.VMEM((RWKV_PAIRS, RWKV_HEAD, PAIR), F32)],
        compiler_params=_cparams("parallel", "arbitrary"),
        name="rwkv",
    )(u, *[_parg(a) for a in params])
    s_last = s_last.reshape(batch, RWKV_PAIRS, RWKV_HEAD, 2, RWKV_HEAD).transpose(0, 1, 3, 2, 4).reshape(
        batch, RWKV_HEADS, RWKV_HEAD, RWKV_HEAD)
    return y, shift.reshape(batch, RWKV_PROJ), s_last


def _rwkv_step_body(u_ref, shift0_ref, s0_ref, *rest, seq, batch, layer):
    sdone_ref, rest = (rest[0], rest[1:]) if layer else (None, rest)
    (mu_ref, w0_ref, w2_ref, a0_ref, a2_ref, g2_ref, kk_ref, ka_ref, rk_ref, lng_ref, lnb_ref, y_ref, sout_ref,
     r_scr, w_scr, k_scr, b_scr, nkk_scr, v_scr, y_scr) = rest
    T, B = seq, batch
    j = pl.program_id(0)
    if layer:
        sout_ref[0:layer] = sdone_ref[...]
    tiles = RWKV_STEP_TILES

    def pointwise(t):
        u = u_ref[t * B:(t + 1) * B, :]
        prev = shift0_ref[...] if t == 0 else u_ref[(t - 1) * B:t * B, :]
        return _rwkv_pointwise(u, prev, mu_ref, w0_ref, w2_ref, a0_ref, a2_ref, g2_ref, kk_ref, ka_ref)

    @pl.when(j == 0)
    def _():
        for t in range(T):
            r, k, v, logdecay, a, _, kk = pointwise(t)
            r_scr[t] = r.T
            w_scr[t] = jnp.exp(logdecay).T
            k_scr[t] = k.T
            b_scr[t] = (kk * a).T
            nkk_scr[t] = (-kk).T
            v_scr[t] = v.T

    i0 = j * tiles
    keys = pl.ds(pl.multiple_of((i0 // RWKV_HEAD) * RWKV_HEAD, RWKV_HEAD), RWKV_HEAD)
    for q in range(tiles):
        vi = pl.ds(i0 + q, 1)
        s = s0_ref[q]
        for t in range(T):
            sa = jnp.sum(s * nkk_scr[t, keys, :], axis=0, keepdims=True)
            s = s * w_scr[t, keys, :] + k_scr[t, keys, :] * v_scr[t, vi, :] + b_scr[t, keys, :] * sa
            y_scr[t, vi, :] = jnp.sum(s * r_scr[t, keys, :], axis=0, keepdims=True)
        sout_ref[layer, q] = s

    @pl.when(j == pl.num_programs(0) - 1)
    def _():
        for t in range(T):
            r, k, v, _, _, g, _ = pointwise(t)
            y_ref[t * B:(t + 1) * B, :] = _rwkv_finish(y_scr[t].T, r, k, v, g, rk_ref, lng_ref, lnb_ref)


def _rwkv_step(u, shift0, s_all, s_done, p, *, batch, seq, layer):
    n = batch * seq
    srows = RWKV_HEADS * RWKV_HEAD
    params = [p[nm] for nm in _RWKV_PARAM_NAMES]
    sspec, prev_specs, sout_spec = _layer_state_specs(layer, (RWKV_STEP_TILES, RWKV_HEAD, batch), 0)
    prev_args = [s_done] if layer else []
    tposed = pltpu.VMEM((seq, GROUP_WIDTH, batch), F32)
    return pl.pallas_call(
        functools.partial(_rwkv_step_body, seq=seq, batch=batch, layer=layer),
        grid=(srows // RWKV_STEP_TILES,),
        in_specs=[_full_spec((n, RWKV_PROJ)), _full_spec((batch, RWKV_PROJ)), sspec] + prev_specs
                 + [_pspec(a) for a in params],
        out_specs=[_full_spec((n, GROUP_WIDTH)), sout_spec],
        out_shape=[jax.ShapeDtypeStruct((n, GROUP_WIDTH), F32),
                   jax.ShapeDtypeStruct((layer + 1, srows, RWKV_HEAD, batch), F32)],
        scratch_shapes=[tposed] * 7,
        compiler_params=_cparams("arbitrary"),
        name="rwkv_step",
    )(u, shift0, s_all, *prev_args, *[_parg(a) for a in params])


def _s5_body(u_ref, hre0_ref, him0_ref, are_ref, aim_ref, bmat_ref, cmat_ref, d_ref, gw_ref, gb_ref,
             y_ref, hre_ref, him_ref, hs_scr, tm_scr, *, steps, batch_major):
    c = pl.program_id(1)
    ns = S5_WIDTH
    bsub = SUBLANES

    @pl.when(c == 0)
    def _():
        hre_ref[...] = hre0_ref[...]
        him_ref[...] = him0_ref[...]

    if batch_major:
        for b in range(bsub):
            tm_scr[:, b, :] = u_ref[b]
        u = tm_scr[...].reshape(steps * bsub, GROUP_WIDTH)
    else:
        u = u_ref[...].reshape(steps * bsub, GROUP_WIDTH)
    are = jnp.broadcast_to(are_ref[...], (bsub, ns))
    aim = jnp.broadcast_to(aim_ref[...], (bsub, ns))
    hre, him = hre_ref[...], him_ref[...]
    sub = min(S5_SUB, steps)
    rows = sub * bsub
    outs = []
    for k in range(steps // sub):
        r0 = k * rows
        u_k = u[r0:r0 + rows]
        hs_scr[r0:r0 + rows, :] = _dot(u_k.astype(BF16), bmat_ref[...])
        for t in range(sub):
            rs = slice(r0 + t * bsub, r0 + (t + 1) * bsub)
            hre, him = (are * hre - aim * him + hs_scr[rs, 0:ns], are * him + aim * hre + hs_scr[rs, ns:2 * ns])
            hs_scr[rs, 0:ns] = hre
            hs_scr[rs, ns:2 * ns] = him
        y = _dot(hs_scr[r0:r0 + rows, :].astype(BF16), cmat_ref[...]) + u_k * d_ref[...]
        y = _gelu_tanh(y)
        yy = _dot(y.astype(BF16), gw_ref[...]) + gb_ref[...]
        outs.append(yy[:, 0:GROUP_WIDTH] * _sigmoid(yy[:, GROUP_WIDTH:2 * GROUP_WIDTH]))
    hre_ref[...] = hre
    him_ref[...] = him
    out = jnp.concatenate(outs, axis=0).reshape(steps, bsub, GROUP_WIDTH)
    if batch_major:
        tm_scr[...] = out
        for b in range(bsub):
            y_ref[b] = tm_scr[:, b, :]
    else:
        y_ref[...] = out


def _time_specs(u, batch_major):
    bsub = SUBLANES
    if batch_major:
        batch, seq, _ = u.shape
        steps = min(TM_CHUNK, seq)
        spec = pl.BlockSpec((bsub, steps, GROUP_WIDTH), lambda b, c: (b, c, 0))
    else:
        seq, batch, _ = u.shape
        steps = min(TM_CHUNK, seq)
        spec = pl.BlockSpec((steps, bsub, GROUP_WIDTH), lambda b, c: (c, b, 0))
    return batch, seq, steps, spec


def _s5(u, hre0, him0, lp, *, batch_major):
    batch, seq, steps, tspec = _time_specs(u, batch_major)
    bsub = SUBLANES
    hspec = pl.BlockSpec((bsub, S5_WIDTH), lambda b, c: (b, 0))
    consts = (lp["s5_are"], lp["s5_aim"], lp["s5_bmat"], lp["s5_cmat"], lp["s5_d"], lp["s5_gw"], lp["s5_gb"])
    return pl.pallas_call(
        functools.partial(_s5_body, steps=steps, batch_major=batch_major),
        grid=(batch // bsub, seq // steps),
        in_specs=[tspec, hspec, hspec] + [_pspec(a) for a in consts],
        out_specs=[tspec, hspec, hspec],
        out_shape=[jax.ShapeDtypeStruct(u.shape, F32),
                   jax.ShapeDtypeStruct((batch, S5_WIDTH), F32),
                   jax.ShapeDtypeStruct((batch, S5_WIDTH), F32)],
        scratch_shapes=[pltpu.VMEM((steps * bsub, 2 * S5_WIDTH), F32),
                        pltpu.VMEM((steps, bsub, GROUP_WIDTH), F32)],
        compiler_params=_cparams("parallel", "arbitrary"),
        name="s5",
    )(u, hre0, him0, *[_parg(a) for a in consts])


def _pool_body(u_ref, buf0_ref, pw_ref, sc_ref, y_ref, buf_ref, f_scr, tm_scr, *, steps, pos0, batch_major):
    c = pl.program_id(1)
    bsub = SUBLANES
    GW = GROUP_WIDTH
    halo = POOL_BUF + 1

    @pl.when(c == 0)
    def _():
        f_scr[0] = jnp.zeros((bsub, GW), F32)
        f_scr[1:halo] = buf0_ref[...]

    if batch_major:
        for b in range(bsub):
            f_scr[halo:halo + steps, b, :] = u_ref[b]
    else:
        f_scr[halo:halo + steps] = u_ref[...]
    f = f_scr[...]
    u = f[halo:halo + steps]
    s2 = f[1:] + f[:-1]
    s4 = s2[2:] + s2[:-2]
    s8 = s4[4:] + s4[:-4]
    s16 = s8[8:] + s8[:-8]
    f_scr[0:halo] = f[steps:steps + halo]
    lane = lax.broadcasted_iota(jnp.int32, (steps, bsub, GW), 2)
    tpos = lax.broadcasted_iota(jnp.int32, (steps, bsub, GW), 0) + (pos0 + 1) + c * steps
    win = jnp.where(lane < POOL_CH, s2[halo - 1:halo - 1 + steps],
                    jnp.where(lane < 2 * POOL_CH, s4[halo - 3:halo - 3 + steps],
                              jnp.where(lane < 3 * POOL_CH, s8[halo - 7:halo - 7 + steps],
                                        s16[halo - 15:halo - 15 + steps])))
    wlen = jnp.where(lane < POOL_CH, POOL_WINDOWS[0],
                     jnp.where(lane < 2 * POOL_CH, POOL_WINDOWS[1],
                               jnp.where(lane < 3 * POOL_CH, POOL_WINDOWS[2], POOL_WINDOWS[3])))
    cnt = jnp.minimum(tpos, wlen).astype(F32)
    pooled = (win / cnt - u).reshape(steps * bsub, GW)
    y = (_dot(pooled.astype(BF16), pw_ref[...]) * sc_ref[...]).reshape(steps, bsub, GW)
    if batch_major:
        tm_scr[...] = y
        for b in range(bsub):
            y_ref[b] = tm_scr[:, b, :]
    else:
        y_ref[...] = y

    @pl.when(c == pl.num_programs(1) - 1)
    def _():
        buf_ref[...] = f_scr[1:halo]


def _pool(u, buf0, lp, *, pos0, batch_major, layer=None):
    batch, seq, steps, tspec = _time_specs(u, batch_major)
    bsub = SUBLANES
    bblock = (POOL_BUF, bsub, GROUP_WIDTH)
    bspec = pl.BlockSpec(bblock, lambda b, c: (0, b, 0))
    if layer is None:
        bspec_in = bspec
    else:
        bspec_in = pl.BlockSpec((None,) + bblock, lambda b, c: (layer, 0, b, 0))
    return pl.pallas_call(
        functools.partial(_pool_body, steps=steps, pos0=pos0, batch_major=batch_major),
        grid=(batch // bsub, seq // steps),
        in_specs=[tspec, bspec_in, _pspec(lp["pool_w"]), _pspec(lp["pool_scale"])],
        out_specs=[tspec, bspec],
        out_shape=[jax.ShapeDtypeStruct(u.shape, F32), jax.ShapeDtypeStruct((POOL_BUF, batch, GROUP_WIDTH), F32)],
        scratch_shapes=[pltpu.VMEM((POOL_BUF + 1 + steps, bsub, GROUP_WIDTH), F32),
                        pltpu.VMEM((steps, bsub, GROUP_WIDTH), F32)],
        compiler_params=_cparams("parallel", "arbitrary"),
        name="pool",
    )(u, buf0, _parg(lp["pool_w"]), _parg(lp["pool_scale"]))


def _block_diag(blocks):
    n, g, r, c = blocks.shape
    eye = jnp.eye(g, dtype=blocks.dtype)
    return (eye[None, :, None, :, None] * blocks[:, :, :, None, :]).reshape(n, g * r, g * c)


def _stacked_params(P):
    row = lambda a: a.reshape(a.shape[0], 1, -1)
    pad_lanes = lambda a: jnp.pad(a, ((0, 0), (0, LANES - a.shape[1])))
    bf = lambda a: a.astype(BF16)
    w_in = P["w_in"]
    split = GROUP_WIDTH + SSD_CONV_DIM
    w_all = jnp.concatenate([w_in[:, :, :split], w_in[:, :, split + SSD_HEADS:],
                             jnp.pad(w_in[:, :, split:split + SSD_HEADS], ((0, 0), (0, 0), (0, LANES - SSD_HEADS)))],
                            axis=2)

    lam = lax.complex(P["s5_lam_re"], P["s5_lam_im"])
    a_bar = jnp.exp(lam * jnp.exp(P["s5_log_step"])[..., None])
    b_bar = ((a_bar - 1.0) / lam)[..., None] * lax.complex(P["s5_b_re"], P["s5_b_im"])
    b_t = jnp.swapaxes(b_bar, 2, 3)
    bmat = jnp.concatenate([_block_diag(jnp.real(b_t)), _block_diag(jnp.imag(b_t))], axis=2)
    c_t = jnp.swapaxes(lax.complex(P["s5_c_re"], P["s5_c_im"]), 2, 3)
    cmat = jnp.concatenate([_block_diag(jnp.real(c_t)), -_block_diag(jnp.imag(c_t))], axis=1)

    out = dict(
        norm_ffn1=row(P["norm_ffn1"]), ffn1_in=bf(P["ffn1_in"]), ffn1_out=bf(P["ffn1_out"]),
        norm_mix=row(P["norm_mix"]), w_all=bf(w_all),
        conv_w=P["ssd_conv_w"], conv_b=row(P["ssd_conv_b"]),
        dt_bias=row(pad_lanes(P["ssd_dt_bias"])), a_log=row(pad_lanes(P["ssd_a_log"])),
        a_neg_exp=row(jnp.repeat(-jnp.exp(P["ssd_a_log"]), SSD_HEAD_DIM, axis=1)),
        d_skip=row(jnp.repeat(P["ssd_d"], SSD_HEAD_DIM, axis=1)), ssd_norm=row(P["ssd_norm"]),
        s5_are=row(jnp.real(a_bar)), s5_aim=row(jnp.imag(a_bar)), s5_bmat=bf(bmat), s5_cmat=bf(cmat),
        s5_d=row(P["s5_d"]), s5_gw=bf(P["s5_glu_w"]), s5_gb=row(P["s5_glu_b"]),
        pool_w=bf(_block_diag(P["pool_w"])), pool_scale=row(P["pool_scale"]),
        w_out=bf(P["w_out"]),
        norm_ffn2=row(P["norm_ffn2"]), ffn2_in=bf(P["ffn2_in"]), ffn2_out=bf(P["ffn2_out"]),
    )
    for name in _RWKV_PARAM_NAMES:
        a = P["rwkv_" + name]
        out["rwkv_" + name] = bf(a) if name in ("w2", "a2", "g2") else row(a)
    return out


def _layer_params(stacked, l):
    lp = {k: _Layered((v, l)) for k, v in stacked.items()}
    lp["rwkv"] = {n: lp["rwkv_" + n] for n in _RWKV_PARAM_NAMES}
    lp["head_expand"] = jnp.pad(jnp.repeat(jnp.eye(SSD_HEADS, dtype=F32), SSD_HEAD_DIM, axis=1),
                                ((0, LANES - SSD_HEADS), (0, 0)))
    return lp


def _mixers_prompt(lp, proj, *, batch, seq):
    z, xbc, ur, us5, upool, dtr = proj
    y_ssd, conv_new, ssd_new = _ssd(z, xbc, dtr, lp, batch=batch, seq=seq)
    y_rwkv, shift_new, rwkv_new = _rwkv(ur, lp["rwkv"], batch=batch, seq=seq)
    zeros = jnp.zeros((batch, S5_WIDTH), F32)
    bm = lambda a: a.reshape(batch, seq, a.shape[-1])
    rows = lambda a: a.reshape(batch * seq, a.shape[-1])
    y_s5, s5re, s5im = _s5(bm(us5), zeros, zeros, lp, batch_major=True)
    y_pool, pool_new = _pool(bm(upool), jnp.zeros((POOL_BUF, batch, GROUP_WIDTH), F32), lp, pos0=0,
                             batch_major=True)
    ys = (y_ssd, y_rwkv, rows(y_s5), rows(y_pool))
    states = (conv_new, ssd_new, shift_new, rwkv_new, s5re.reshape(batch, S5_GROUPS, S5_STATE),
              s5im.reshape(batch, S5_GROUPS, S5_STATE), jnp.swapaxes(pool_new, 0, 1))
    return ys, states


def _mixers_decode(lp, proj, states, done, *, batch, seq, layer):
    z, xbc, ur, us5, upool, dtr = proj
    shift0, s5re0, s5im0 = (states[i][layer] for i in (2, 4, 5))
    ssd_done, rwkv_done = (done[1], done[3]) if layer else (None, None)
    y_ssd, conv_new, ssd_new = _ssd_step(z, xbc, dtr, states[0], states[1], ssd_done, lp, batch=batch, seq=seq,
                                         layer=layer)
    y_rwkv, rwkv_new = _rwkv_step(ur, shift0, states[3], rwkv_done, lp["rwkv"], batch=batch, seq=seq, layer=layer)
    shift_new = ur[(seq - 1) * batch:, :]
    tm = lambda a: a.reshape(seq, batch, a.shape[-1])
    y_s5, s5re, s5im = _s5(tm(us5), s5re0.reshape(batch, S5_WIDTH), s5im0.reshape(batch, S5_WIDTH), lp,
                           batch_major=False)
    y_pool, pool_new = _pool(tm(upool), states[6], lp, pos0=PAST_LEN, batch_major=False, layer=layer)
    rows = lambda a: a.reshape(seq * batch, a.shape[-1])
    ys = (y_ssd, y_rwkv, rows(y_s5), rows(y_pool))
    new_states = (jnp.swapaxes(conv_new, 0, 1), ssd_new, shift_new, rwkv_new,
                  s5re.reshape(batch, S5_GROUPS, S5_STATE), s5im.reshape(batch, S5_GROUPS, S5_STATE),
                  jnp.swapaxes(pool_new, 0, 1))
    return ys, new_states


_WIDTHS = (GROUP_WIDTH, SSD_CONV_DIM, RWKV_PROJ, GROUP_WIDTH, GROUP_WIDTH, LANES)


def _trunk(x, layer_params, norm_final, mixers):
    states = []
    mix, lp = None, None
    for l, lp_next in enumerate(layer_params):
        if l > 0:
            x = _ffn(x, lp["norm_ffn2"], lp["ffn2_in"], lp["ffn2_out"], mix=mix, wmix=lp["w_out"])
        lp = lp_next
        x = _ffn(x, lp["norm_ffn1"], lp["ffn1_in"], lp["ffn1_out"])
        proj = _inproj(x, lp["norm_mix"], lp["w_all"], _WIDTHS)
        mix, st = mixers(l, lp, proj, states[-1] if states else None)
        states.append(st)
    x = _ffn(x, lp["norm_ffn2"], lp["ffn2_in"], lp["ffn2_out"], mix=mix, wmix=lp["w_out"], gf=norm_final)
    return x, states


def kernel(x_prompt, x_sample, state_ssd_conv, state_ssd, state_rwkv_shift, state_rwkv, state_s5_re, state_s5_im, state_pool, norm_ffn1, ffn1_in, ffn1_out, norm_mix, w_in, ssd_conv_w, ssd_conv_b, ssd_dt_bias, ssd_a_log, ssd_d, ssd_norm, rwkv_mu, rwkv_w0, rwkv_w2, rwkv_a0, rwkv_a2, rwkv_g2, rwkv_k_k, rwkv_k_a, rwkv_r_k, rwkv_ln_g, rwkv_ln_b, s5_lam_re, s5_lam_im, s5_log_step, s5_b_re, s5_b_im, s5_c_re, s5_c_im, s5_d, s5_glu_w, s5_glu_b, pool_w, pool_scale, w_out, norm_ffn2, ffn2_in, ffn2_out, norm_final):
    P = dict(norm_ffn1=norm_ffn1, ffn1_in=ffn1_in, ffn1_out=ffn1_out, norm_mix=norm_mix, w_in=w_in,
             ssd_conv_w=ssd_conv_w, ssd_conv_b=ssd_conv_b, ssd_dt_bias=ssd_dt_bias, ssd_a_log=ssd_a_log,
             ssd_d=ssd_d, ssd_norm=ssd_norm, rwkv_mu=rwkv_mu, rwkv_w0=rwkv_w0, rwkv_w2=rwkv_w2, rwkv_a0=rwkv_a0,
             rwkv_a2=rwkv_a2, rwkv_g2=rwkv_g2, rwkv_k_k=rwkv_k_k, rwkv_k_a=rwkv_k_a,
             rwkv_r_k=rwkv_r_k.reshape(rwkv_r_k.shape[0], -1), rwkv_ln_g=rwkv_ln_g, rwkv_ln_b=rwkv_ln_b,
             s5_lam_re=s5_lam_re, s5_lam_im=s5_lam_im, s5_log_step=s5_log_step, s5_b_re=s5_b_re, s5_b_im=s5_b_im,
             s5_c_re=s5_c_re, s5_c_im=s5_c_im, s5_d=s5_d, s5_glu_w=s5_glu_w, s5_glu_b=s5_glu_b, pool_w=pool_w,
             pool_scale=pool_scale, w_out=w_out, norm_ffn2=norm_ffn2, ffn2_in=ffn2_in, ffn2_out=ffn2_out)
    depth = norm_ffn1.shape[0]
    bp, tp, d = x_prompt.shape
    bs, ts, _ = x_sample.shape
    stacked = _stacked_params(P)
    layer_params = [_layer_params(stacked, l) for l in range(depth)]
    gf = norm_final.reshape(1, -1)
    sample_states = (state_ssd_conv, state_ssd, state_rwkv_shift, state_rwkv, state_s5_re, state_s5_im, state_pool)
    rwkv_rows = RWKV_HEADS * RWKV_HEAD
    decode_states = (jnp.swapaxes(state_ssd_conv, 1, 2), state_ssd, state_rwkv_shift,
                     jnp.transpose(state_rwkv, (0, 2, 3, 4, 1)).reshape(depth, rwkv_rows, RWKV_HEAD, bs),
                     state_s5_re, state_s5_im, jnp.swapaxes(state_pool, 1, 2))

    y_p, st_p = _trunk(x_prompt.reshape(bp * tp, d), layer_params, gf,
                       lambda l, lp, proj, done: _mixers_prompt(lp, proj, batch=bp, seq=tp))
    x_s = jnp.swapaxes(x_sample, 0, 1).reshape(ts * bs, d)
    y_s, st_s = _trunk(x_s, layer_params, gf,
                       lambda l, lp, proj, done: _mixers_decode(lp, proj, decode_states, done,
                                                                batch=bs, seq=ts, layer=l))
    outs = [y_p.reshape(bp, tp, d), jnp.swapaxes(y_s.reshape(ts, bs, d), 0, 1)]
    for i, ref_state in enumerate(sample_states):
        outs.append(jnp.stack([st[i] for st in st_p]))
        if i == 1:
            outs.append(st_s[-1][i].reshape(ref_state.shape))
        elif i == 3:
            s_new = st_s[-1][i].reshape(depth, RWKV_HEADS, RWKV_HEAD, RWKV_HEAD, bs)
            outs.append(jnp.transpose(s_new, (0, 4, 1, 2, 3)))
        else:
            outs.append(jnp.stack([st[i] for st in st_s]))
    return tuple(outs)
```

```python
import functools
import math

import jax
import jax.numpy as jnp
from jax import lax
from jax.experimental import pallas as pl
from jax.experimental.pallas import tpu as pltpu

F32 = jnp.float32
BF16 = jnp.bfloat16
HIGHEST = lax.Precision.HIGHEST

SUBLANES = 8
LANES = 128
VMEM_LIMIT_BYTES = 56 * 1024 * 1024

GROUP_WIDTH = 256
SSD_HEAD_DIM = 64
SSD_HEADS = 4
SSD_GROUPS = 2
SSD_STATE = 128
SSD_CONV = 4
SSD_CONV_DIM = GROUP_WIDTH + 2 * SSD_GROUPS * SSD_STATE
SSD_CHUNK = 128
SSD_GROUP = 4
LOG2_E = math.log2(math.e)
RWKV_HEAD = 64
RWKV_HEADS = 4
RWKV_PROJ = 1024
RWKV_LN_EPS = 64e-5
RWKV_CHUNK = 64
RWKV_GROUP = 8
S5_GROUP_CH = 16
S5_GROUPS = 16
S5_STATE = 64
S5_WIDTH = S5_GROUPS * S5_STATE
POOL_WINDOWS = (2, 4, 8, 16)
POOL_CH = 64
POOL_BUF = 15
RMS_EPS = 1e-6
PAST_LEN = 16384

ROW_TILE = 512
FFN_CHUNK = 256
TM_CHUNK = 64
S5_SUB = 16
SSD_STEP_TILES = 16
RWKV_STEP_TILES = 16


def _cparams(*sem):
    return pltpu.CompilerParams(dimension_semantics=sem, vmem_limit_bytes=VMEM_LIMIT_BYTES)


def _dot(a, b, **kw):
    return jnp.dot(a, b, preferred_element_type=F32, **kw)


def _dot_nt(a, b):
    return lax.dot_general(a, b, (((1,), (1,)), ((), ())), preferred_element_type=F32)


def _dot_tn(a, b):
    return lax.dot_general(a, b, (((0,), (0,)), ((), ())), preferred_element_type=F32)


def _sigmoid(x):
    return 1.0 / (1.0 + jnp.exp(-x))


def _silu(x):
    return x * _sigmoid(x)


def _softplus(x):
    return jnp.maximum(x, 0.0) + jnp.log(1.0 + jnp.exp(-jnp.abs(x)))


def _gelu_tanh(x):
    c = math.sqrt(2.0 / math.pi)
    return x * (0.5 * (1.0 + jnp.tanh(c * (x + 0.044715 * (x * x * x)))))


def _rms(x, g):
    return x * lax.rsqrt(jnp.mean(x * x, axis=-1, keepdims=True) + RMS_EPS) * g


def _full_spec(shape):
    n = len(shape)
    return pl.BlockSpec(shape, lambda *_: (0,) * n)


class _Layered(tuple):
    pass


def _pspec(p, single=False):
    mode = pl.Buffered(1) if single else None
    if isinstance(p, _Layered):
        a, l = p
        return pl.BlockSpec((None,) + a.shape[1:], lambda *_: (l,) + (0,) * (a.ndim - 1), pipeline_mode=mode)
    n = p.ndim
    return pl.BlockSpec(p.shape, lambda *_: (0,) * n, pipeline_mode=mode)


def _parg(p):
    return p[0] if isinstance(p, _Layered) else p


def _stream_steps(row_counts, tile):
    plan, off = [], 0
    for r in row_counts:
        plan.append((off, r // tile))
        off += r // tile
    return tuple(plan), off


def _stream_spec(step0, ntiles, tile, width):
    return pl.BlockSpec((tile, width), lambda i: (jnp.clip(i - step0, 0, ntiles - 1), 0))


def _stream_active(streams):
    i = pl.program_id(0)
    return [(i >= s0) & (i < s0 + n) for s0, n in streams]


def _stream_pick(refs, active):
    v = refs[0][...]
    for k in range(1, len(refs)):
        v = jnp.where(active[k], refs[k][...], v)
    return v


def _stream_store(o_refs, active, value):
    if len(o_refs) == 1:
        o_refs[0][...] = value
        return
    for k, o_ref in enumerate(o_refs):
        @pl.when(active[k])
        def _(o_ref=o_ref):
            o_ref[...] = value


def _ffn_body(*refs, streams, has_mix, final_norm):
    ns = len(streams)
    it = iter(refs)
    active = _stream_active(streams)
    x = _stream_pick([next(it) for _ in range(ns)], active)
    if has_mix:
        y_refs = [[next(it) for _ in range(4)] for _ in range(ns)]
        wmix_ref = next(it)
        for j in range(4):
            y = _stream_pick([y_refs[k][j] for k in range(ns)], active)
            x = x + _dot(y.astype(BF16), wmix_ref[j * GROUP_WIDTH:(j + 1) * GROUP_WIDTH, :])
    g_ref, wi_ref, wo_ref = next(it), next(it), next(it)
    gf_ref = next(it) if final_norm else None
    o_refs = [next(it) for _ in range(ns)]
    h = _rms(x, g_ref[...]).astype(BF16)
    d_ff = wo_ref.shape[0]
    acc = jnp.zeros_like(x)
    for c in range(d_ff // FFN_CHUNK):
        lo = c * FFN_CHUNK
        gate = _dot(h, wi_ref[:, lo:lo + FFN_CHUNK])
        up = _dot(h, wi_ref[:, d_ff + lo:d_ff + lo + FFN_CHUNK])
        act = (_silu(gate) * up).astype(BF16)
        acc = acc + _dot(act, wo_ref[lo:lo + FFN_CHUNK, :])
    x = x + 0.5 * acc
    if final_norm:
        x = _rms(x, gf_ref[...])
    _stream_store(o_refs, active, x)


def _ffn(xs, g, wi, wo, mixes=None, wmix=None, gf=None):
    d = xs[0].shape[1]
    ns = len(xs)
    streams, steps = _stream_steps([x.shape[0] for x in xs], ROW_TILE)
    spec = lambda k, w: _stream_spec(*streams[k], ROW_TILE, w)
    args, specs = list(xs), [spec(k, d) for k in range(ns)]
    if mixes is not None:
        for k, mix in enumerate(mixes):
            for y in mix:
                args.append(y)
                specs.append(spec(k, y.shape[1]))
        args.append(_parg(wmix))
        specs.append(_pspec(wmix, single=True))
    for a in (g, wi, wo) + ((gf,) if gf is not None else ()):
        args.append(_parg(a))
        specs.append(_pspec(a, single=True))
    return pl.pallas_call(
        functools.partial(_ffn_body, streams=streams, has_mix=mixes is not None, final_norm=gf is not None),
        grid=(steps,),
        in_specs=specs,
        out_specs=[spec(k, d) for k in range(ns)],
        out_shape=[jax.ShapeDtypeStruct(x.shape, F32) for x in xs],
        compiler_params=_cparams("arbitrary"),
        name="ffn",
    )(*args)


def _inproj_body(*refs, streams, nout):
    ns = len(streams)
    active = _stream_active(streams)
    x = _stream_pick(list(refs[:ns]), active)
    g_ref, w_ref = refs[ns], refs[ns + 1]
    o_refs = refs[ns + 2:]
    h = _rms(x, g_ref[...]).astype(BF16)
    off = 0
    for j in range(nout):
        outs_j = [o_refs[k * nout + j] for k in range(ns)]
        n = outs_j[0].shape[-1]
        _stream_store(outs_j, active, _dot(h, w_ref[:, off:off + n]))
        off += n


def _inproj(xs, g, w, widths):
    d = xs[0].shape[1]
    ns, nout = len(xs), len(widths)
    streams, steps = _stream_steps([x.shape[0] for x in xs], ROW_TILE)
    spec = lambda k, w_: _stream_spec(*streams[k], ROW_TILE, w_)
    outs = pl.pallas_call(
        functools.partial(_inproj_body, streams=streams, nout=nout),
        grid=(steps,),
        in_specs=[spec(k, d) for k in range(ns)] + [_pspec(g), _pspec(w, single=True)],
        out_specs=[spec(k, n) for k in range(ns) for n in widths],
        out_shape=[jax.ShapeDtypeStruct((x.shape[0], n), F32) for x in xs for n in widths],
        compiler_params=_cparams("arbitrary"),
        name="inproj",
    )(*xs, _parg(g), _parg(w))
    return [outs[k * nout:(k + 1) * nout] for k in range(ns)]


def _ssd_body(z_ref, xbc_ref, dt_ref, cw_ref, cb_ref, dtb_ref, alog_ref, dsk_ref, ng_ref,
              y_ref, conv_ref, hout_ref, xpad_scr, h_scr, *, chunk, group):
    L, G = chunk, group
    GL = G * L
    c = pl.program_id(1)
    pad = SUBLANES
    halo = SSD_CONV - 1
    hpg = SSD_HEADS // SSD_GROUPS
    assert hpg == 2 and hpg * SSD_HEAD_DIM == SSD_STATE

    @pl.when(c == 0)
    def _():
        xpad_scr[0:pad, :] = jnp.zeros((pad, SSD_CONV_DIM), F32)
        h_scr[...] = jnp.zeros(h_scr.shape, F32)

    xpad_scr[pad:pad + GL, :] = xbc_ref[...]
    xfull = xpad_scr[...]
    conv = cb_ref[...] + cw_ref[halo:halo + 1, :] * xfull[pad:pad + GL]
    for j in range(halo):
        conv = conv + cw_ref[j:j + 1, :] * pltpu.roll(xfull, halo - j, axis=0)[pad:pad + GL]
    xpad_scr[pad - halo:pad, :] = xpad_scr[pad + GL - halo:pad + GL, :]
    conv = _silu(conv)
    xs = conv[:, 0:GROUP_WIDTH]
    bm = conv[:, GROUP_WIDTH:2 * GROUP_WIDTH].astype(BF16)
    cm = conv[:, 2 * GROUP_WIDTH:3 * GROUP_WIDTH].astype(BF16)

    row = lax.broadcasted_iota(jnp.int32, (L, L), 0)
    col = lax.broadcasted_iota(jnp.int32, (L, L), 1)
    causal = row >= col
    tril = jnp.where(causal, 1.0, 0.0).astype(F32)
    dt = _softplus(dt_ref[...] + dtb_ref[...])
    da = dt * (-jnp.exp(alog_ref[...]) * LOG2_E)
    acs = [_dot(tril, da[i * L:(i + 1) * L, :], precision=HIGHEST) for i in range(G)]
    acs_t = [a.T for a in acs]
    e_acs = [jnp.exp2(a) for a in acs]
    e_end = [jnp.exp2(a[L - 1:L, :] - a) for a in acs]
    e_last = [jnp.exp2(a[L - 1:L, :]) for a in acs]

    keys = [(i, g) for i in range(G) for g in range(SSD_GROUPS)]
    rows_of = lambda x, i: x[i * L:(i + 1) * L]
    lanes_of = lambda x, g: x[:, g * SSD_STATE:(g + 1) * SSD_STATE]
    lane_lo = lax.broadcasted_iota(jnp.int32, (L, hpg * SSD_HEAD_DIM), 1) < SSD_HEAD_DIM
    row_lo = lax.broadcasted_iota(jnp.int32, (hpg * SSD_HEAD_DIM, SSD_STATE), 0) < SSD_HEAD_DIM
    head_cols = lambda a, g: jnp.where(lane_lo, a[:, g * hpg:g * hpg + 1], a[:, g * hpg + 1:g * hpg + 2])
    bg = {(i, g): lanes_of(rows_of(bm, i), g) for i, g in keys}
    cg = {(i, g): lanes_of(rows_of(cm, i), g) for i, g in keys}
    scores = {k: _dot_nt(cg[k], bg[k]) for k in keys}
    xdt = {(i, g): lanes_of(rows_of(xs, i), g) * head_cols(rows_of(dt, i), g) for i, g in keys}
    decay = {(i, h): jnp.exp2(jnp.where(causal, acs[i][:, h:h + 1] - acs_t[i][h:h + 1, :], -jnp.inf))
             for i in range(G) for h in range(SSD_HEADS)}
    p_mat = {(i, g): jnp.concatenate([(scores[(i, g)] * decay[(i, g * hpg + k)]).astype(BF16) for k in range(hpg)],
                                     axis=1) for i, g in keys}
    y_in = {k: _dot(p_mat[k], _bd(xdt[k].astype(BF16))) for k in keys}
    st = {(i, g): _dot_tn((xdt[(i, g)] * head_cols(e_end[i], g)).astype(BF16), bg[(i, g)]) for i, g in keys}

    y_rows = []
    for i in range(G):
        ys = []
        for g in range(SSD_GROUPS):
            h_prev = h_scr[g * hpg:(g + 1) * hpg].reshape(hpg * SSD_HEAD_DIM, SSD_STATE)
            ys.append(y_in[(i, g)] + _dot_nt(cg[(i, g)], h_prev.astype(BF16)) * head_cols(e_acs[i], g))
            keep = jnp.where(row_lo, e_last[i][:, g * hpg:g * hpg + 1], e_last[i][:, g * hpg + 1:g * hpg + 2])
            h_scr[g * hpg:(g + 1) * hpg] = (h_prev * keep + st[(i, g)]).reshape(hpg, SSD_HEAD_DIM, SSD_STATE)
        y_rows.append(jnp.concatenate(ys, axis=-1))
    y = jnp.concatenate(y_rows, axis=0) + xs * dsk_ref[...]
    y = y * _silu(z_ref[...])
    y_ref[...] = _rms(y, ng_ref[...])

    @pl.when(c == pl.num_programs(1) - 1)
    def _():
        hout_ref[0] = h_scr[...]
        conv_ref[0] = xpad_scr[pad - halo:pad, :]


def _ssd(z, xbc, dtr, lp, *, batch, seq):
    chunk = SSD_CHUNK
    rows = chunk * SSD_GROUP
    nc = seq // rows
    rspec = lambda w: pl.BlockSpec((rows, w), lambda b, c: (b * nc + c, 0))
    consts = (lp["conv_w"], lp["conv_b"], lp["dt_bias"], lp["a_log"], lp["d_skip"], lp["ssd_norm"])
    return pl.pallas_call(
        functools.partial(_ssd_body, chunk=chunk, group=SSD_GROUP),
        grid=(batch, nc),
        in_specs=[rspec(GROUP_WIDTH), rspec(SSD_CONV_DIM), rspec(LANES)] + [_pspec(a) for a in consts],
        out_specs=[rspec(GROUP_WIDTH),
                   pl.BlockSpec((1, SSD_CONV - 1, SSD_CONV_DIM), lambda b, c: (b, 0, 0)),
                   pl.BlockSpec((1, SSD_HEADS, SSD_HEAD_DIM, SSD_STATE), lambda b, c: (b, 0, 0, 0))],
        out_shape=[jax.ShapeDtypeStruct((batch * seq, GROUP_WIDTH), F32),
                   jax.ShapeDtypeStruct((batch, SSD_CONV - 1, SSD_CONV_DIM), F32),
                   jax.ShapeDtypeStruct((batch, SSD_HEADS, SSD_HEAD_DIM, SSD_STATE), F32)],
        scratch_shapes=[pltpu.VMEM((SUBLANES + rows, SSD_CONV_DIM), F32),
                        pltpu.VMEM((SSD_HEADS, SSD_HEAD_DIM, SSD_STATE), F32)],
        compiler_params=_cparams("parallel", "arbitrary"),
        name="ssd",
    )(z, xbc, dtr, *[_parg(a) for a in consts])


def _ssd_step_body(z_ref, xbc_ref, dt_ref, conv0_ref, h0_ref, *rest, seq, batch, layer):
    hdone_ref, rest = (rest[0], rest[1:]) if layer else (None, rest)
    (cw_ref, cb_ref, dtb_ref, aneg_ref, dsk_ref, ng_ref, hexp_ref, y_ref, conv_ref, hout_ref,
     xs_scr, bm_scr, cm_scr, xdt_scr, dec_scr, y_scr) = rest
    T, B = seq, batch
    if layer:
        hout_ref[0:layer] = hdone_ref[...]
    GW = GROUP_WIDTH
    j = pl.program_id(0)
    tiles = SSD_STEP_TILES

    @pl.when(j == 0)
    def _():
        rows = [conv0_ref[i] for i in range(SSD_CONV - 1)]
        rows += [xbc_ref[t * B:(t + 1) * B, :] for t in range(T)]
        for t in range(T):
            conv = cb_ref[...] + cw_ref[0:1, :] * rows[t]
            for i in range(1, SSD_CONV):
                conv = conv + cw_ref[i:i + 1, :] * rows[t + i]
            conv = _silu(conv)
            xs = conv[:, 0:GW]
            xs_scr[t] = xs
            for g in range(SSD_GROUPS):
                bm_scr[t, g] = conv[:, GW + g * SSD_STATE:GW + (g + 1) * SSD_STATE].T
                cm_scr[t, g] = conv[:, 2 * GW + g * SSD_STATE:2 * GW + (g + 1) * SSD_STATE].T
            dt = _softplus(dt_ref[t * B:(t + 1) * B, :] + dtb_ref[...])
            dte = _dot(dt, hexp_ref[...], precision=HIGHEST)
            xdt_scr[t] = (xs * dte).T
            dec_scr[t] = jnp.exp(dte * aneg_ref[...]).T
        for i in range(SSD_CONV - 1):
            conv_ref[i] = rows[T + i]

    hp0 = j * tiles
    grp = hp0 // (SSD_HEAD_DIM * (SSD_HEADS // SSD_GROUPS))
    for q in range(tiles):
        hp = pl.ds(hp0 + q, 1)
        h = h0_ref[:, q, :].T
        for t in range(T):
            h = h * dec_scr[t, hp, :] + bm_scr[t, grp] * xdt_scr[t, hp, :]
            y_scr[t, hp, :] = jnp.sum(h * cm_scr[t, grp], axis=0, keepdims=True)
        hout_ref[layer, :, q, :] = h.T

    @pl.when(j == pl.num_programs(0) - 1)
    def _():
        for t in range(T):
            y = y_scr[t].T + xs_scr[t] * dsk_ref[...]
            y = y * _silu(z_ref[t * B:(t + 1) * B, :])
            y_ref[t * B:(t + 1) * B, :] = _rms(y, ng_ref[...])


def _layer_state_specs(layer, block, axis):
    idx = lambda first: (lambda j: (first,) + tuple(j if a == axis else 0 for a in range(len(block))))
    cur = pl.BlockSpec((None,) + block, idx(layer))
    prev = [pl.BlockSpec((layer,) + block, idx(0))] if layer else []
    out = pl.BlockSpec((layer + 1,) + block, idx(0))
    return cur, prev, out


def _ssd_step(z, xbc, dtr, conv_all, h_all, h_done, lp, *, batch, seq, layer):
    n = batch * seq
    srows = SSD_HEADS * SSD_HEAD_DIM
    consts = (lp["conv_w"], lp["conv_b"], lp["dt_bias"], lp["a_neg_exp"], lp["d_skip"], lp["ssd_norm"], lp["head_expand"])
    hspec, prev_specs, hout_spec = _layer_state_specs(layer, (batch, SSD_STEP_TILES, SSD_STATE), 1)
    prev_args = [h_done] if layer else []
    cshape = (SSD_CONV - 1, batch, SSD_CONV_DIM)
    return pl.pallas_call(
        functools.partial(_ssd_step_body, seq=seq, batch=batch, layer=layer),
        grid=(srows // SSD_STEP_TILES,),
        in_specs=[_full_spec((n, GROUP_WIDTH)), _full_spec((n, SSD_CONV_DIM)), _full_spec((n, LANES)),
                  pl.BlockSpec((None,) + cshape, lambda j: (layer, 0, 0, 0)), hspec] + prev_specs
                 + [_pspec(a) for a in consts],
        out_specs=[_full_spec((n, GROUP_WIDTH)), _full_spec(cshape), hout_spec],
        out_shape=[jax.ShapeDtypeStruct((n, GROUP_WIDTH), F32),
                   jax.ShapeDtypeStruct(cshape, F32),
                   jax.ShapeDtypeStruct((layer + 1, batch, srows, SSD_STATE), F32)],
        scratch_shapes=[pltpu.VMEM((seq, batch, GROUP_WIDTH), F32),
                        pltpu.VMEM((seq, SSD_GROUPS, SSD_STATE, batch), F32),
                        pltpu.VMEM((seq, SSD_GROUPS, SSD_STATE, batch), F32),
                        pltpu.VMEM((seq, GROUP_WIDTH, batch), F32),
                        pltpu.VMEM((seq, GROUP_WIDTH, batch), F32),
                        pltpu.VMEM((seq, GROUP_WIDTH, batch), F32)],
        compiler_params=_cparams("arbitrary"),
        name="ssd_step",
    )(z, xbc, dtr, conv_all, h_all.reshape(h_all.shape[0], batch, srows, SSD_STATE),
      *prev_args, *[_parg(a) for a in consts])


PAIR = 2 * RWKV_HEAD
RWKV_PAIRS = RWKV_HEADS // 2


def _bd(x):
    half = x.shape[1] // 2
    lane = lax.broadcasted_iota(jnp.int32, x.shape, 1)
    zero = jnp.zeros_like(x)
    return jnp.concatenate([jnp.where(lane < half, x, zero), jnp.where(lane >= half, x, zero)], axis=0)


def _half_sums(x, lo):
    s_lo = jnp.sum(jnp.where(lo, x, 0.0), axis=-1, keepdims=True)
    s_hi = jnp.sum(jnp.where(lo, 0.0, x), axis=-1, keepdims=True)
    return jnp.where(lo, s_lo, s_hi)


def _head_sum(x):
    lo = lax.broadcasted_iota(jnp.int32, (x.shape[0], PAIR), 1) < RWKV_HEAD
    return jnp.concatenate([_half_sums(x[:, p * PAIR:(p + 1) * PAIR], lo) for p in range(RWKV_PAIRS)], axis=-1)


def _rwkv_pointwise(u, prev, mu_ref, w0_ref, w2_ref, a0_ref, a2_ref, g2_ref, kk_ref, ka_ref):
    GW = GROUP_WIDTH
    xs = u + (prev - u) * mu_ref[...]
    r = xs[:, 0:GW]
    k = xs[:, GW:2 * GW]
    v = xs[:, 2 * GW:3 * GW]
    wd = xs[:, 3 * GW:3 * GW + 64]
    ad = xs[:, 3 * GW + 64:3 * GW + 128]
    gd = xs[:, 3 * GW + 128:3 * GW + 256]
    w_lin = w0_ref[...] + _dot(jnp.tanh(wd).astype(BF16), w2_ref[...])
    logdecay = -jnp.exp(-_softplus(-w_lin) - 0.5)
    a = _sigmoid(a0_ref[...] + _dot(ad.astype(BF16), a2_ref[...]))
    g = _dot(_sigmoid(gd).astype(BF16), g2_ref[...])
    kk = k * kk_ref[...]
    kk = kk / jnp.maximum(jnp.sqrt(_head_sum(kk * kk)), 1e-12)
    k = k * (1.0 + (a - 1.0) * ka_ref[...])
    return r, k, v, logdecay, a, g, kk


def _rwkv_finish(y, r, k, v, g, rk_ref, lng_ref, lnb_ref):
    mean = _head_sum(y) * (1.0 / RWKV_HEAD)
    yc = y - mean
    var = _head_sum(yc * yc) * (1.0 / RWKV_HEAD)
    y = yc * lax.rsqrt(var + RWKV_LN_EPS) * lng_ref[...] + lnb_ref[...]
    bonus = _head_sum(r * k * rk_ref[...]) * v
    return (y + bonus) * g


def _rwkv_body(u_ref, mu_ref, w0_ref, w2_ref, a0_ref, a2_ref, g2_ref, kk_ref, ka_ref, rk_ref,
               lng_ref, lnb_ref, y_ref, shift_ref, sout_ref, upad_scr, s_scr, *, chunk, group):
    L, G = chunk, group
    GL = G * L
    c = pl.program_id(1)
    pad = SUBLANES

    @pl.when(c == 0)
    def _():
        upad_scr[0:pad, :] = jnp.zeros((pad, RWKV_PROJ), F32)
        s_scr[...] = jnp.zeros(s_scr.shape, F32)

    u = u_ref[...]
    upad_scr[pad:pad + GL, :] = u
    prev = pltpu.roll(upad_scr[...], 1, axis=0)[pad:pad + GL]
    upad_scr[pad - 1:pad, :] = u[GL - 1:GL, :]
    r, k, v, logdecay, a, g, kk = _rwkv_pointwise(u, prev, mu_ref, w0_ref, w2_ref, a0_ref, a2_ref, g2_ref,
                                                  kk_ref, ka_ref)

    tril = jnp.where(lax.broadcasted_iota(jnp.int32, (L, L), 0) >= lax.broadcasted_iota(jnp.int32, (L, L), 1),
                     1.0, 0.0).astype(F32)
    cl = jnp.concatenate([_dot(tril, logdecay[i * L:(i + 1) * L, :], precision=HIGHEST) for i in range(G)], axis=0)
    e_in = jnp.exp(cl)
    e_inv = jnp.exp(-cl)
    r_t = r * e_in
    r_tb = r_t.astype(BF16)
    a_tb = (-kk * jnp.exp(cl - logdecay)).astype(BF16)
    b_tb = (kk * a * e_inv).astype(BF16)
    k_tb = (k * e_inv).astype(BF16)
    vb = v.astype(BF16)

    row = lax.broadcasted_iota(jnp.int32, (L, PAIR), 0)
    colh = lax.broadcasted_iota(jnp.int32, (L, PAIR), 1) & (RWKV_HEAD - 1)
    strict = row > colh
    incl = row >= colh
    eye_pair = jnp.where(row == colh, 1.0, 0.0).astype(F32)
    lane_lo = lax.broadcasted_iota(jnp.int32, (RWKV_HEAD, PAIR), 1) < RWKV_HEAD
    same_head = (lax.broadcasted_iota(jnp.int32, (PAIR, PAIR), 0) < RWKV_HEAD) == \
                (lax.broadcasted_iota(jnp.int32, (PAIR, PAIR), 1) < RWKV_HEAD)

    streams = [(i, p) for i in range(G) for p in range(RWKV_PAIRS)]
    ns = len(streams)
    blk = lambda x, i, p: x[i * L:(i + 1) * L, p * PAIR:(p + 1) * PAIR]
    lhs = [jnp.concatenate([blk(a_tb, i, p), blk(r_tb, i, p)], axis=0) for i, p in streams]
    m_ab = [_dot_nt(lhs[s], _bd(blk(b_tb, i, p))) for s, (i, p) in enumerate(streams)]
    m_ak = [_dot_nt(lhs[s], _bd(blk(k_tb, i, p))) for s, (i, p) in enumerate(streams)]
    n_ab = [jnp.where(strict, m[0:L], 0.0) for m in m_ab]
    m_rb = [jnp.where(incl, m[L:2 * L], 0.0).astype(BF16) for m in m_ab]
    n_ak = [jnp.where(strict, m[0:L], 0.0).astype(BF16) for m in m_ak]
    m_rk = [jnp.where(incl, m[L:2 * L], 0.0).astype(BF16) for m in m_ak]
    tinv = [eye_pair + n for n in n_ab]
    pwb = [n.astype(BF16) for n in n_ab]
    pw = [_dot(x, _bd(x)) for x in pwb]
    for _ in range(int(math.log2(L)) - 2):
        pwb = [x.astype(BF16) for x in pw]
        both = [_dot(jnp.concatenate([pwb[s], tinv[s].astype(BF16)], axis=0), _bd(pwb[s])) for s in range(ns)]
        pw = [x[0:L] for x in both]
        tinv = [tinv[s] + both[s][L:2 * L] for s in range(ns)]
    pwb = [x.astype(BF16) for x in pw]
    tinv = [tinv[s] + _dot(tinv[s].astype(BF16), _bd(pwb[s])) for s in range(ns)]
    tinvb = [x.astype(BF16) for x in tinv]
    nv_mv = [_dot(jnp.concatenate([n_ak[s], m_rk[s]], axis=0), _bd(blk(vb, i, p))) for s, (i, p) in enumerate(streams)]
    wu = [_dot(tinvb[s], jnp.concatenate([_bd(blk(a_tb, i, p)), _bd(nv_mv[s][0:L].astype(BF16))], axis=1))
          for s, (i, p) in enumerate(streams)]
    wub = [x.astype(BF16) for x in wu]
    qy = [_dot(m_rb[s], jnp.concatenate([_bd(wub[s][:, 0:PAIR]), _bd(wub[s][:, PAIR:2 * PAIR])], axis=1))
          for s in range(ns)]
    q = [(blk(r_t, i, p) + qy[s][:, 0:PAIR]).astype(BF16) for s, (i, p) in enumerate(streams)]
    y_loc = [qy[s][:, PAIR:2 * PAIR] + nv_mv[s][L:2 * L] for s in range(ns)]
    zeros_b = jnp.zeros((L, PAIR), BF16)
    mg = [_dot_tn(jnp.concatenate([wub[s], jnp.concatenate([zeros_b, blk(vb, i, p)], axis=1)], axis=0),
                  jnp.concatenate([blk(b_tb, i, p), blk(k_tb, i, p)], axis=0))
          for s, (i, p) in enumerate(streams)]
    p_end = [e_in[(i + 1) * L - 1:(i + 1) * L, p * PAIR:(p + 1) * PAIR] for i, p in streams]
    m_t = [(jnp.where(same_head, mg[s][0:PAIR], 0.0) * p_end[s]).astype(BF16) for s in range(ns)]
    g_t = [jnp.where(lane_lo, mg[s][PAIR:PAIR + RWKV_HEAD], mg[s][PAIR + RWKV_HEAD:2 * PAIR]) * p_end[s]
           for s in range(ns)]

    y_rows = []
    for i in range(G):
        y_pairs = []
        for p in range(RWKV_PAIRS):
            s = i * RWKV_PAIRS + p
            s0 = s_scr[p]
            s0b = s0.astype(BF16)
            y_pairs.append(_dot_nt(q[s], _bd(s0b)) + y_loc[s])
            s_scr[p] = s0 * p_end[s] + _dot(s0b, m_t[s]) + g_t[s]
        y_rows.append(jnp.concatenate(y_pairs, axis=-1))
    y = jnp.concatenate(y_rows, axis=0)
    y_ref[...] = _rwkv_finish(y, r, k, v, g, rk_ref, lng_ref, lnb_ref)

    @pl.when(c == pl.num_programs(1) - 1)
    def _():
        sout_ref[0] = s_scr[...]
        shift_ref[0] = upad_scr[pad - 1:pad, :]


_RWKV_PARAM_NAMES = ("mu", "w0", "w2", "a0", "a2", "g2", "k_k", "k_a", "r_k", "ln_g", "ln_b")


def _rwkv(u, p, *, batch, seq):
    rows = RWKV_CHUNK * RWKV_GROUP
    nc = seq // rows
    params = [p[n] for n in _RWKV_PARAM_NAMES]
    sspec = pl.BlockSpec((1, RWKV_PAIRS, RWKV_HEAD, PAIR), lambda b, c: (b, 0, 0, 0))
    y, shift, s_last = pl.pallas_call(
        functools.partial(_rwkv_body, chunk=RWKV_CHUNK, group=RWKV_GROUP),
        grid=(batch, nc),
        in_specs=[pl.BlockSpec((rows, RWKV_PROJ), lambda b, c: (b * nc + c, 0))] + [_pspec(a) for a in params],
        out_specs=[pl.BlockSpec((rows, GROUP_WIDTH), lambda b, c: (b * nc + c, 0)),
                   pl.BlockSpec((1, 1, RWKV_PROJ), lambda b, c: (b, 0, 0)), sspec],
        out_shape=[jax.ShapeDtypeStruct((batch * seq, GROUP_WIDTH), F32),
                   jax.ShapeDtypeStruct((batch, 1, RWKV_PROJ), F32),
                   jax.ShapeDtypeStruct((batch, RWKV_PAIRS, RWKV_HEAD, PAIR), F32)],
        scratch_shapes=[pltpu.VMEM((SUBLANES + rows, RWKV_PROJ), F32),
                        pltpu.VMEM((RWKV_PAIRS, RWKV_HEAD, PAIR), F32)],
        compiler_params=_cparams("parallel", "arbitrary"),
        name="rwkv",
    )(u, *[_parg(a) for a in params])
    s_last = s_last.reshape(batch, RWKV_PAIRS, RWKV_HEAD, 2, RWKV_HEAD).transpose(0, 1, 3, 2, 4).reshape(
        batch, RWKV_HEADS, RWKV_HEAD, RWKV_HEAD)
    return y, shift.reshape(batch, RWKV_PROJ), s_last


def _rwkv_step_body(u_ref, shift0_ref, s0_ref, *rest, seq, batch, layer):
    sdone_ref, rest = (rest[0], rest[1:]) if layer else (None, rest)
    (mu_ref, w0_ref, w2_ref, a0_ref, a2_ref, g2_ref, kk_ref, ka_ref, rk_ref, lng_ref, lnb_ref, y_ref, sout_ref,
     r_scr, w_scr, k_scr, b_scr, nkk_scr, v_scr, y_scr) = rest
    T, B = seq, batch
    j = pl.program_id(0)
    if layer:
        sout_ref[0:layer] = sdone_ref[...]
    tiles = RWKV_STEP_TILES

    def pointwise(t):
        u = u_ref[t * B:(t + 1) * B, :]
        prev = shift0_ref[...] if t == 0 else u_ref[(t - 1) * B:t * B, :]
        return _rwkv_pointwise(u, prev, mu_ref, w0_ref, w2_ref, a0_ref, a2_ref, g2_ref, kk_ref, ka_ref)

    @pl.when(j == 0)
    def _():
        for t in range(T):
            r, k, v, logdecay, a, _, kk = pointwise(t)
            r_scr[t] = r.T
            w_scr[t] = jnp.exp(logdecay).T
            k_scr[t] = k.T
            b_scr[t] = (kk * a).T
            nkk_scr[t] = (-kk).T
            v_scr[t] = v.T

    i0 = j * tiles
    keys = pl.ds(pl.multiple_of((i0 // RWKV_HEAD) * RWKV_HEAD, RWKV_HEAD), RWKV_HEAD)
    for q in range(tiles):
        vi = pl.ds(i0 + q, 1)
        s = s0_ref[q]
        for t in range(T):
            sa = jnp.sum(s * nkk_scr[t, keys, :], axis=0, keepdims=True)
            s = s * w_scr[t, keys, :] + k_scr[t, keys, :] * v_scr[t, vi, :] + b_scr[t, keys, :] * sa
            y_scr[t, vi, :] = jnp.sum(s * r_scr[t, keys, :], axis=0, keepdims=True)
        sout_ref[layer, q] = s

    @pl.when(j == pl.num_programs(0) - 1)
    def _():
        for t in range(T):
            r, k, v, _, _, g, _ = pointwise(t)
            y_ref[t * B:(t + 1) * B, :] = _rwkv_finish(y_scr[t].T, r, k, v, g, rk_ref, lng_ref, lnb_ref)


def _rwkv_step(u, shift0, s_all, s_done, p, *, batch, seq, layer):
    n = batch * seq
    srows = RWKV_HEADS * RWKV_HEAD
    params = [p[nm] for nm in _RWKV_PARAM_NAMES]
    sspec, prev_specs, sout_spec = _layer_state_specs(layer, (RWKV_STEP_TILES, RWKV_HEAD, batch), 0)
    prev_args = [s_done] if layer else []
    tposed = pltpu.VMEM((seq, GROUP_WIDTH, batch), F32)
    return pl.pallas_call(
        functools.partial(_rwkv_step_body, seq=seq, batch=batch, layer=layer),
        grid=(srows // RWKV_STEP_TILES,),
        in_specs=[_full_spec((n, RWKV_PROJ)), _full_spec((batch, RWKV_PROJ)), sspec] + prev_specs
                 + [_pspec(a) for a in params],
        out_specs=[_full_spec((n, GROUP_WIDTH)), sout_spec],
        out_shape=[jax.ShapeDtypeStruct((n, GROUP_WIDTH), F32),
                   jax.ShapeDtypeStruct((layer + 1, srows, RWKV_HEAD, batch), F32)],
        scratch_shapes=[tposed] * 7,
        compiler_params=_cparams("arbitrary"),
        name="rwkv_step",
    )(u, shift0, s_all, *prev_args, *[_parg(a) for a in params])


def _s5_body(u_ref, hre0_ref, him0_ref, are_ref, aim_ref, bmat_ref, cmat_ref, d_ref, gw_ref, gb_ref,
             y_ref, hre_ref, him_ref, hs_scr, tm_scr, *, steps, batch_major):
    c = pl.program_id(1)
    ns = S5_WIDTH
    bsub = SUBLANES

    @pl.when(c == 0)
    def _():
        hre_ref[...] = hre0_ref[...]
        him_ref[...] = him0_ref[...]

    if batch_major:
        for b in range(bsub):
            tm_scr[:, b, :] = u_ref[b]
        u = tm_scr[...].reshape(steps * bsub, GROUP_WIDTH)
    else:
        u = u_ref[...].reshape(steps * bsub, GROUP_WIDTH)
    are = jnp.broadcast_to(are_ref[...], (bsub, ns))
    aim = jnp.broadcast_to(aim_ref[...], (bsub, ns))
    hre, him = hre_ref[...], him_ref[...]
    sub = min(S5_SUB, steps)
    rows = sub * bsub
    outs = []
    for k in range(steps // sub):
        r0 = k * rows
        u_k = u[r0:r0 + rows]
        hs_scr[r0:r0 + rows, :] = _dot(u_k.astype(BF16), bmat_ref[...])
        for t in range(sub):
            rs = slice(r0 + t * bsub, r0 + (t + 1) * bsub)
            hre, him = (are * hre - aim * him + hs_scr[rs, 0:ns], are * him + aim * hre + hs_scr[rs, ns:2 * ns])
            hs_scr[rs, 0:ns] = hre
            hs_scr[rs, ns:2 * ns] = him
        y = _dot(hs_scr[r0:r0 + rows, :].astype(BF16), cmat_ref[...]) + u_k * d_ref[...]
        y = _gelu_tanh(y)
        yy = _dot(y.astype(BF16), gw_ref[...]) + gb_ref[...]
        outs.append(yy[:, 0:GROUP_WIDTH] * _sigmoid(yy[:, GROUP_WIDTH:2 * GROUP_WIDTH]))
    hre_ref[...] = hre
    him_ref[...] = him
    out = jnp.concatenate(outs, axis=0).reshape(steps, bsub, GROUP_WIDTH)
    if batch_major:
        tm_scr[...] = out
        for b in range(bsub):
            y_ref[b] = tm_scr[:, b, :]
    else:
        y_ref[...] = out


def _time_specs(u, batch_major):
    bsub = SUBLANES
    if batch_major:
        batch, seq, _ = u.shape
        steps = min(TM_CHUNK, seq)
        spec = pl.BlockSpec((bsub, steps, GROUP_WIDTH), lambda b, c: (b, c, 0))
    else:
        seq, batch, _ = u.shape
        steps = min(TM_CHUNK, seq)
        spec = pl.BlockSpec((steps, bsub, GROUP_WIDTH), lambda b, c: (c, b, 0))
    return batch, seq, steps, spec


def _s5(u, hre0, him0, lp, *, batch_major):
    batch, seq, steps, tspec = _time_specs(u, batch_major)
    bsub = SUBLANES
    hspec = pl.BlockSpec((bsub, S5_WIDTH), lambda b, c: (b, 0))
    consts = (lp["s5_are"], lp["s5_aim"], lp["s5_bmat"], lp["s5_cmat"], lp["s5_d"], lp["s5_gw"], lp["s5_gb"])
    return pl.pallas_call(
        functools.partial(_s5_body, steps=steps, batch_major=batch_major),
        grid=(batch // bsub, seq // steps),
        in_specs=[tspec, hspec, hspec] + [_pspec(a) for a in consts],
        out_specs=[tspec, hspec, hspec],
        out_shape=[jax.ShapeDtypeStruct(u.shape, F32),
                   jax.ShapeDtypeStruct((batch, S5_WIDTH), F32),
                   jax.ShapeDtypeStruct((batch, S5_WIDTH), F32)],
        scratch_shapes=[pltpu.VMEM((steps * bsub, 2 * S5_WIDTH), F32),
                        pltpu.VMEM((steps, bsub, GROUP_WIDTH), F32)],
        compiler_params=_cparams("parallel", "arbitrary"),
        name="s5",
    )(u, hre0, him0, *[_parg(a) for a in consts])


def _pool_body(u_ref, buf0_ref, pw_ref, sc_ref, y_ref, buf_ref, f_scr, tm_scr, *, steps, pos0, batch_major):
    c = pl.program_id(1)
    bsub = SUBLANES
    GW = GROUP_WIDTH
    halo = POOL_BUF + 1

    @pl.when(c == 0)
    def _():
        f_scr[0] = jnp.zeros((bsub, GW), F32)
        f_scr[1:halo] = buf0_ref[...]

    if batch_major:
        for b in range(bsub):
            f_scr[halo:halo + steps, b, :] = u_ref[b]
    else:
        f_scr[halo:halo + steps] = u_ref[...]
    f = f_scr[...]
    u = f[halo:halo + steps]
    s2 = f[1:] + f[:-1]
    s4 = s2[2:] + s2[:-2]
    s8 = s4[4:] + s4[:-4]
    s16 = s8[8:] + s8[:-8]
    f_scr[0:halo] = f[steps:steps + halo]
    lane = lax.broadcasted_iota(jnp.int32, (steps, bsub, GW), 2)
    tpos = lax.broadcasted_iota(jnp.int32, (steps, bsub, GW), 0) + (pos0 + 1) + c * steps
    win = jnp.where(lane < POOL_CH, s2[halo - 1:halo - 1 + steps],
                    jnp.where(lane < 2 * POOL_CH, s4[halo - 3:halo - 3 + steps],
                              jnp.where(lane < 3 * POOL_CH, s8[halo - 7:halo - 7 + steps],
                                        s16[halo - 15:halo - 15 + steps])))
    wlen = jnp.where(lane < POOL_CH, POOL_WINDOWS[0],
                     jnp.where(lane < 2 * POOL_CH, POOL_WINDOWS[1],
                               jnp.where(lane < 3 * POOL_CH, POOL_WINDOWS[2], POOL_WINDOWS[3])))
    cnt = jnp.minimum(tpos, wlen).astype(F32)
    pooled = (win / cnt - u).reshape(steps * bsub, GW)
    y = (_dot(pooled.astype(BF16), pw_ref[...]) * sc_ref[...]).reshape(steps, bsub, GW)
    if batch_major:
        tm_scr[...] = y
        for b in range(bsub):
            y_ref[b] = tm_scr[:, b, :]
    else:
        y_ref[...] = y

    @pl.when(c == pl.num_programs(1) - 1)
    def _():
        buf_ref[...] = f_scr[1:halo]


def _pool(u, buf0, lp, *, pos0, batch_major, layer=None):
    batch, seq, steps, tspec = _time_specs(u, batch_major)
    bsub = SUBLANES
    bblock = (POOL_BUF, bsub, GROUP_WIDTH)
    bspec = pl.BlockSpec(bblock, lambda b, c: (0, b, 0))
    if layer is None:
        bspec_in = bspec
    else:
        bspec_in = pl.BlockSpec((None,) + bblock, lambda b, c: (layer, 0, b, 0))
    return pl.pallas_call(
        functools.partial(_pool_body, steps=steps, pos0=pos0, batch_major=batch_major),
        grid=(batch // bsub, seq // steps),
        in_specs=[tspec, bspec_in, _pspec(lp["pool_w"]), _pspec(lp["pool_scale"])],
        out_specs=[tspec, bspec],
        out_shape=[jax.ShapeDtypeStruct(u.shape, F32), jax.ShapeDtypeStruct((POOL_BUF, batch, GROUP_WIDTH), F32)],
        scratch_shapes=[pltpu.VMEM((POOL_BUF + 1 + steps, bsub, GROUP_WIDTH), F32),
                        pltpu.VMEM((steps, bsub, GROUP_WIDTH), F32)],
        compiler_params=_cparams("parallel", "arbitrary"),
        name="pool",
    )(u, buf0, _parg(lp["pool_w"]), _parg(lp["pool_scale"]))


def _block_diag(blocks):
    n, g, r, c = blocks.shape
    eye = jnp.eye(g, dtype=blocks.dtype)
    return (eye[None, :, None, :, None] * blocks[:, :, :, None, :]).reshape(n, g * r, g * c)


def _stacked_params(P):
    row = lambda a: a.reshape(a.shape[0], 1, -1)
    pad_lanes = lambda a: jnp.pad(a, ((0, 0), (0, LANES - a.shape[1])))
    bf = lambda a: a.astype(BF16)
    w_in = P["w_in"]
    split = GROUP_WIDTH + SSD_CONV_DIM
    w_all = jnp.concatenate([w_in[:, :, :split], w_in[:, :, split + SSD_HEADS:],
                             jnp.pad(w_in[:, :, split:split + SSD_HEADS], ((0, 0), (0, 0), (0, LANES - SSD_HEADS)))],
                            axis=2)

    lam = lax.complex(P["s5_lam_re"], P["s5_lam_im"])
    a_bar = jnp.exp(lam * jnp.exp(P["s5_log_step"])[..., None])
    b_bar = ((a_bar - 1.0) / lam)[..., None] * lax.complex(P["s5_b_re"], P["s5_b_im"])
    b_t = jnp.swapaxes(b_bar, 2, 3)
    bmat = jnp.concatenate([_block_diag(jnp.real(b_t)), _block_diag(jnp.imag(b_t))], axis=2)
    c_t = jnp.swapaxes(lax.complex(P["s5_c_re"], P["s5_c_im"]), 2, 3)
    cmat = jnp.concatenate([_block_diag(jnp.real(c_t)), -_block_diag(jnp.imag(c_t))], axis=1)

    out = dict(
        norm_ffn1=row(P["norm_ffn1"]), ffn1_in=bf(P["ffn1_in"]), ffn1_out=bf(P["ffn1_out"]),
        norm_mix=row(P["norm_mix"]), w_all=bf(w_all),
        conv_w=P["ssd_conv_w"], conv_b=row(P["ssd_conv_b"]),
        dt_bias=row(pad_lanes(P["ssd_dt_bias"])), a_log=row(pad_lanes(P["ssd_a_log"])),
        a_neg_exp=row(jnp.repeat(-jnp.exp(P["ssd_a_log"]), SSD_HEAD_DIM, axis=1)),
        d_skip=row(jnp.repeat(P["ssd_d"], SSD_HEAD_DIM, axis=1)), ssd_norm=row(P["ssd_norm"]),
        s5_are=row(jnp.real(a_bar)), s5_aim=row(jnp.imag(a_bar)), s5_bmat=bf(bmat), s5_cmat=bf(cmat),
        s5_d=row(P["s5_d"]), s5_gw=bf(P["s5_glu_w"]), s5_gb=row(P["s5_glu_b"]),
        pool_w=bf(_block_diag(P["pool_w"])), pool_scale=row(P["pool_scale"]),
        w_out=bf(P["w_out"]),
        norm_ffn2=row(P["norm_ffn2"]), ffn2_in=bf(P["ffn2_in"]), ffn2_out=bf(P["ffn2_out"]),
    )
    for name in _RWKV_PARAM_NAMES:
        a = P["rwkv_" + name]
        out["rwkv_" + name] = bf(a) if name in ("w2", "a2", "g2") else row(a)
    return out


def _layer_params(stacked, l):
    lp = {k: _Layered((v, l)) for k, v in stacked.items()}
    lp["rwkv"] = {n: lp["rwkv_" + n] for n in _RWKV_PARAM_NAMES}
    lp["head_expand"] = jnp.pad(jnp.repeat(jnp.eye(SSD_HEADS, dtype=F32), SSD_HEAD_DIM, axis=1),
                                ((0, LANES - SSD_HEADS), (0, 0)))
    return lp


def _mixers_prompt(lp, proj, *, batch, seq):
    z, xbc, ur, us5, upool, dtr = proj
    y_ssd, conv_new, ssd_new = _ssd(z, xbc, dtr, lp, batch=batch, seq=seq)
    y_rwkv, shift_new, rwkv_new = _rwkv(ur, lp["rwkv"], batch=batch, seq=seq)
    zeros = jnp.zeros((batch, S5_WIDTH), F32)
    bm = lambda a: a.reshape(batch, seq, a.shape[-1])
    rows = lambda a: a.reshape(batch * seq, a.shape[-1])
    y_s5, s5re, s5im = _s5(bm(us5), zeros, zeros, lp, batch_major=True)
    y_pool, pool_new = _pool(bm(upool), jnp.zeros((POOL_BUF, batch, GROUP_WIDTH), F32), lp, pos0=0,
                             batch_major=True)
    ys = (y_ssd, y_rwkv, rows(y_s5), rows(y_pool))
    states = (conv_new, ssd_new, shift_new, rwkv_new, s5re.reshape(batch, S5_GROUPS, S5_STATE),
              s5im.reshape(batch, S5_GROUPS, S5_STATE), jnp.swapaxes(pool_new, 0, 1))
    return ys, states


def _mixers_decode(lp, proj, states, done, *, batch, seq, layer):
    z, xbc, ur, us5, upool, dtr = proj
    shift0, s5re0, s5im0 = (states[i][layer] for i in (2, 4, 5))
    ssd_done, rwkv_done = (done[1], done[3]) if layer else (None, None)
    y_ssd, conv_new, ssd_new = _ssd_step(z, xbc, dtr, states[0], states[1], ssd_done, lp, batch=batch, seq=seq,
                                         layer=layer)
    y_rwkv, rwkv_new = _rwkv_step(ur, shift0, states[3], rwkv_done, lp["rwkv"], batch=batch, seq=seq, layer=layer)
    shift_new = ur[(seq - 1) * batch:, :]
    tm = lambda a: a.reshape(seq, batch, a.shape[-1])
    y_s5, s5re, s5im = _s5(tm(us5), s5re0.reshape(batch, S5_WIDTH), s5im0.reshape(batch, S5_WIDTH), lp,
                           batch_major=False)
    y_pool, pool_new = _pool(tm(upool), states[6], lp, pos0=PAST_LEN, batch_major=False, layer=layer)
    rows = lambda a: a.reshape(seq * batch, a.shape[-1])
    ys = (y_ssd, y_rwkv, rows(y_s5), rows(y_pool))
    new_states = (jnp.swapaxes(conv_new, 0, 1), ssd_new, shift_new, rwkv_new,
                  s5re.reshape(batch, S5_GROUPS, S5_STATE), s5im.reshape(batch, S5_GROUPS, S5_STATE),
                  jnp.swapaxes(pool_new, 0, 1))
    return ys, new_states


_WIDTHS = (GROUP_WIDTH, SSD_CONV_DIM, RWKV_PROJ, GROUP_WIDTH, GROUP_WIDTH, LANES)


def _trunk(xs, layer_params, norm_final, mixers):
    states = [[] for _ in xs]
    mixes, lp = None, None
    each = lambda fn, *per_stream: [fn(*[[a] for a in args])[0] for args in zip(*per_stream)]
    for l, lp_next in enumerate(layer_params):
        if l > 0:
            xs = each(lambda x, m: _ffn(x, lp["norm_ffn2"], lp["ffn2_in"], lp["ffn2_out"], mixes=m,
                                        wmix=lp["w_out"]), xs, mixes)
        lp = lp_next
        xs = each(lambda x: _ffn(x, lp["norm_ffn1"], lp["ffn1_in"], lp["ffn1_out"]), xs)
        projs = each(lambda x: _inproj(x, lp["norm_mix"], lp["w_all"], _WIDTHS), xs)
        mixes = []
        for k, mixer in enumerate(mixers):
            mix, st = mixer(l, lp, projs[k], states[k][-1] if states[k] else None)
            mixes.append(mix)
            states[k].append(st)
    xs = each(lambda x, m: _ffn(x, lp["norm_ffn2"], lp["ffn2_in"], lp["ffn2_out"], mixes=m, wmix=lp["w_out"],
                                gf=norm_final), xs, mixes)
    return xs, states


def kernel(x_prompt, x_sample, state_ssd_conv, state_ssd, state_rwkv_shift, state_rwkv, state_s5_re, state_s5_im, state_pool, norm_ffn1, ffn1_in, ffn1_out, norm_mix, w_in, ssd_conv_w, ssd_conv_b, ssd_dt_bias, ssd_a_log, ssd_d, ssd_norm, rwkv_mu, rwkv_w0, rwkv_w2, rwkv_a0, rwkv_a2, rwkv_g2, rwkv_k_k, rwkv_k_a, rwkv_r_k, rwkv_ln_g, rwkv_ln_b, s5_lam_re, s5_lam_im, s5_log_step, s5_b_re, s5_b_im, s5_c_re, s5_c_im, s5_d, s5_glu_w, s5_glu_b, pool_w, pool_scale, w_out, norm_ffn2, ffn2_in, ffn2_out, norm_final):
    P = dict(norm_ffn1=norm_ffn1, ffn1_in=ffn1_in, ffn1_out=ffn1_out, norm_mix=norm_mix, w_in=w_in,
             ssd_conv_w=ssd_conv_w, ssd_conv_b=ssd_conv_b, ssd_dt_bias=ssd_dt_bias, ssd_a_log=ssd_a_log,
             ssd_d=ssd_d, ssd_norm=ssd_norm, rwkv_mu=rwkv_mu, rwkv_w0=rwkv_w0, rwkv_w2=rwkv_w2, rwkv_a0=rwkv_a0,
             rwkv_a2=rwkv_a2, rwkv_g2=rwkv_g2, rwkv_k_k=rwkv_k_k, rwkv_k_a=rwkv_k_a,
             rwkv_r_k=rwkv_r_k.reshape(rwkv_r_k.shape[0], -1), rwkv_ln_g=rwkv_ln_g, rwkv_ln_b=rwkv_ln_b,
             s5_lam_re=s5_lam_re, s5_lam_im=s5_lam_im, s5_log_step=s5_log_step, s5_b_re=s5_b_re, s5_b_im=s5_b_im,
             s5_c_re=s5_c_re, s5_c_im=s5_c_im, s5_d=s5_d, s5_glu_w=s5_glu_w, s5_glu_b=s5_glu_b, pool_w=pool_w,
             pool_scale=pool_scale, w_out=w_out, norm_ffn2=norm_ffn2, ffn2_in=ffn2_in, ffn2_out=ffn2_out)
    depth = norm_ffn1.shape[0]
    bp, tp, d = x_prompt.shape
    bs, ts, _ = x_sample.shape
    stacked = _stacked_params(P)
    layer_params = [_layer_params(stacked, l) for l in range(depth)]
    gf = norm_final.reshape(1, -1)
    sample_states = (state_ssd_conv, state_ssd, state_rwkv_shift, state_rwkv, state_s5_re, state_s5_im, state_pool)
    rwkv_rows = RWKV_HEADS * RWKV_HEAD
    decode_states = (jnp.swapaxes(state_ssd_conv, 1, 2), state_ssd, state_rwkv_shift,
                     jnp.transpose(state_rwkv, (0, 2, 3, 4, 1)).reshape(depth, rwkv_rows, RWKV_HEAD, bs),
                     state_s5_re, state_s5_im, jnp.swapaxes(state_pool, 1, 2))

    x_s = jnp.swapaxes(x_sample, 0, 1).reshape(ts * bs, d)
    (y_p, y_s), (st_p, st_s) = _trunk(
        [x_prompt.reshape(bp * tp, d), x_s], layer_params, gf,
        [lambda l, lp, proj, done: _mixers_prompt(lp, proj, batch=bp, seq=tp),
         lambda l, lp, proj, done: _mixers_decode(lp, proj, decode_states, done, batch=bs, seq=ts, layer=l)])
    outs = [y_p.reshape(bp, tp, d), jnp.swapaxes(y_s.reshape(ts, bs, d), 0, 1)]
    for i, ref_state in enumerate(sample_states):
        outs.append(jnp.stack([st[i] for st in st_p]))
        if i == 1:
            outs.append(st_s[-1][i].reshape(ref_state.shape))
        elif i == 3:
            s_new = st_s[-1][i].reshape(depth, RWKV_HEADS, RWKV_HEAD, RWKV_HEAD, bs)
            outs.append(jnp.transpose(s_new, (0, 4, 1, 2, 3)))
        else:
            outs.append(jnp.stack([st[i] for st in st_s]))
    return tuple(outs)
```

```python
import functools
import math

import jax
import jax.numpy as jnp
from jax import lax
from jax.experimental import pallas as pl
from jax.experimental.pallas import tpu as pltpu

F32 = jnp.float32
BF16 = jnp.bfloat16
HIGHEST = lax.Precision.HIGHEST

SUBLANES = 8
LANES = 128
VMEM_LIMIT_BYTES = 56 * 1024 * 1024

GROUP_WIDTH = 256
SSD_HEAD_DIM = 64
SSD_HEADS = 4
SSD_GROUPS = 2
SSD_STATE = 128
SSD_CONV = 4
SSD_CONV_DIM = GROUP_WIDTH + 2 * SSD_GROUPS * SSD_STATE
SSD_CHUNK = 128
SSD_GROUP = 4
LOG2_E = math.log2(math.e)
RWKV_HEAD = 64
RWKV_HEADS = 4
RWKV_PROJ = 1024
RWKV_LN_EPS = 64e-5
RWKV_CHUNK = 64
RWKV_GROUP = 8
S5_GROUP_CH = 16
S5_GROUPS = 16
S5_STATE = 64
S5_WIDTH = S5_GROUPS * S5_STATE
POOL_WINDOWS = (2, 4, 8, 16)
POOL_CH = 64
POOL_BUF = 15
RMS_EPS = 1e-6
PAST_LEN = 16384

ROW_TILE = 512
FFN_CHUNK = 256
TM_CHUNK = 64
S5_SUB = 16
SSD_STEP_TILES = 16
RWKV_STEP_TILES = 16


def _cparams(*sem):
    return pltpu.CompilerParams(dimension_semantics=sem, vmem_limit_bytes=VMEM_LIMIT_BYTES)


def _dot(a, b, **kw):
    return jnp.dot(a, b, preferred_element_type=F32, **kw)


def _dot_nt(a, b):
    return lax.dot_general(a, b, (((1,), (1,)), ((), ())), preferred_element_type=F32)


def _dot_tn(a, b):
    return lax.dot_general(a, b, (((0,), (0,)), ((), ())), preferred_element_type=F32)


def _sigmoid(x):
    return 1.0 / (1.0 + jnp.exp(-x))


def _silu(x):
    return x * _sigmoid(x)


def _softplus(x):
    return jnp.maximum(x, 0.0) + jnp.log(1.0 + jnp.exp(-jnp.abs(x)))


def _gelu_tanh(x):
    c = math.sqrt(2.0 / math.pi)
    return x * (0.5 * (1.0 + jnp.tanh(c * (x + 0.044715 * (x * x * x)))))


def _rms(x, g):
    return x * lax.rsqrt(jnp.mean(x * x, axis=-1, keepdims=True) + RMS_EPS) * g


def _full_spec(shape):
    n = len(shape)
    return pl.BlockSpec(shape, lambda *_: (0,) * n)


class _Layered(tuple):
    pass


def _pspec(p, single=False):
    mode = pl.Buffered(1) if single else None
    if isinstance(p, _Layered):
        a, l = p
        return pl.BlockSpec((None,) + a.shape[1:], lambda *_: (l,) + (0,) * (a.ndim - 1), pipeline_mode=mode)
    n = p.ndim
    return pl.BlockSpec(p.shape, lambda *_: (0,) * n, pipeline_mode=mode)


def _parg(p):
    return p[0] if isinstance(p, _Layered) else p


def _mix_residual(x, y_refs, wmix_ref):
    for j, y_ref in enumerate(y_refs):
        x = x + _dot(y_ref[...].astype(BF16), wmix_ref[j * GROUP_WIDTH:(j + 1) * GROUP_WIDTH, :])
    return x


def _swiglu_chunk(h, wg, wu, wo):
    act = (_silu(_dot(h, wg)) * _dot(h, wu)).astype(BF16)
    return _dot(act, wo)


def _ffn_body(*refs, has_mix, final_norm):
    it = iter(refs)
    x = next(it)[...]
    if has_mix:
        y_refs = [next(it) for _ in range(4)]
        x = _mix_residual(x, y_refs, next(it))
    g_ref, wg_ref, wu_ref, wo_ref = next(it), next(it), next(it), next(it)
    gf_ref = next(it) if final_norm else None
    o_ref = next(it)
    h = _rms(x, g_ref[...]).astype(BF16)
    acc = jnp.zeros_like(x)
    for c in range(wo_ref.shape[0] // FFN_CHUNK):
        cols = slice(c * FFN_CHUNK, (c + 1) * FFN_CHUNK)
        acc = acc + _swiglu_chunk(h, wg_ref[:, cols], wu_ref[:, cols], wo_ref[cols, :])
    x = x + 0.5 * acc
    if final_norm:
        x = _rms(x, gf_ref[...])
    o_ref[...] = x


def _ffn(x, g, wg, wu, wo, mix=None, wmix=None, gf=None):
    rows, d = x.shape
    row_spec = lambda w: pl.BlockSpec((ROW_TILE, w), lambda i: (i, 0))
    args, specs = [x], [row_spec(d)]
    if mix is not None:
        for y in mix:
            args.append(y)
            specs.append(row_spec(y.shape[1]))
        args.append(_parg(wmix))
        specs.append(_pspec(wmix, single=True))
    for a in (g, wg, wu, wo) + ((gf,) if gf is not None else ()):
        args.append(_parg(a))
        specs.append(_pspec(a, single=True))
    return pl.pallas_call(
        functools.partial(_ffn_body, has_mix=mix is not None, final_norm=gf is not None),
        grid=(rows // ROW_TILE,),
        in_specs=specs,
        out_specs=row_spec(d),
        out_shape=jax.ShapeDtypeStruct((rows, d), F32),
        compiler_params=_cparams("parallel"),
        name="ffn",
    )(*args)


def _ffn_cast_body(*refs, has_mix, final_norm):
    it = iter(refs)
    x_ref = next(it)
    if has_mix:
        y_refs = [next(it) for _ in range(4)]
        wmix_ref = next(it)
    g_ref, wg_ref, wu_ref, wo_ref = next(it), next(it), next(it), next(it)
    gf_ref = next(it) if final_norm else None
    o_ref, wg_out, wu_out, wo_out, x_scr, h_scr, acc_scr = (next(it) for _ in range(7))
    c = pl.program_id(0)

    @pl.when(c == 0)
    def _():
        x = x_ref[...]
        if has_mix:
            x = _mix_residual(x, y_refs, wmix_ref)
        x_scr[...] = x
        h_scr[...] = _rms(x, g_ref[...]).astype(BF16)
        acc_scr[...] = jnp.zeros(acc_scr.shape, F32)

    wg = wg_ref[...].astype(BF16)
    wu = wu_ref[...].astype(BF16)
    wo = wo_ref[...].astype(BF16)
    wg_out[...] = wg
    wu_out[...] = wu
    wo_out[...] = wo
    acc_scr[...] += _swiglu_chunk(h_scr[...], wg, wu, wo)

    @pl.when(c == pl.num_programs(0) - 1)
    def _():
        x = x_scr[...] + 0.5 * acc_scr[...]
        if final_norm:
            x = _rms(x, gf_ref[...])
        o_ref[...] = x


def _ffn_cast(x, g, wi, wo, mix=None, wmix=None, gf=None):
    rows, d = x.shape
    wi_all, l = wi
    wo_all, _ = wo
    d_ff = wo_all.shape[1]
    nchunks = d_ff // FFN_CHUNK
    args, specs = [x], [_full_spec(x.shape)]
    if mix is not None:
        for y in mix:
            args.append(y)
            specs.append(_full_spec(y.shape))
        args.append(_parg(wmix))
        specs.append(_pspec(wmix, single=True))
    args += [_parg(g), wi_all, wi_all, wo_all]
    specs += [_pspec(g),
              pl.BlockSpec((None, d, FFN_CHUNK), lambda c: (l, 0, c)),
              pl.BlockSpec((None, d, FFN_CHUNK), lambda c: (l, 0, c + nchunks)),
              pl.BlockSpec((None, FFN_CHUNK, d), lambda c: (l, c, 0))]
    if gf is not None:
        args.append(gf)
        specs.append(_full_spec(gf.shape))
    col_spec = pl.BlockSpec((d, FFN_CHUNK), lambda c: (0, c))
    return pl.pallas_call(
        functools.partial(_ffn_cast_body, has_mix=mix is not None, final_norm=gf is not None),
        grid=(nchunks,),
        in_specs=specs,
        out_specs=[_full_spec(x.shape), col_spec, col_spec, pl.BlockSpec((FFN_CHUNK, d), lambda c: (c, 0))],
        out_shape=[jax.ShapeDtypeStruct((rows, d), F32), jax.ShapeDtypeStruct((d, d_ff), BF16),
                   jax.ShapeDtypeStruct((d, d_ff), BF16), jax.ShapeDtypeStruct((d_ff, d), BF16)],
        scratch_shapes=[pltpu.VMEM((rows, d), F32), pltpu.VMEM((rows, d), BF16), pltpu.VMEM((rows, d), F32)],
        compiler_params=_cparams("arbitrary"),
        name="ffn_cast",
    )(*args)


def _inproj_body(x_ref, g_ref, w_ref, *o_refs):
    h = _rms(x_ref[...], g_ref[...]).astype(BF16)
    off = 0
    for o_ref in o_refs:
        n = o_ref.shape[-1]
        o_ref[...] = _dot(h, w_ref[:, off:off + n])
        off += n


def _inproj(x, g, w, widths):
    rows, d = x.shape
    row_spec = lambda w_: pl.BlockSpec((ROW_TILE, w_), lambda i: (i, 0))
    return pl.pallas_call(
        _inproj_body,
        grid=(rows // ROW_TILE,),
        in_specs=[row_spec(d), _pspec(g), _pspec(w, single=True)],
        out_specs=[row_spec(n) for n in widths],
        out_shape=[jax.ShapeDtypeStruct((rows, n), F32) for n in widths],
        compiler_params=_cparams("parallel"),
        name="inproj",
    )(x, _parg(g), _parg(w))


def _ssd_body(z_ref, xbc_ref, dt_ref, cw_ref, cb_ref, dtb_ref, alog_ref, dsk_ref, ng_ref,
              y_ref, conv_ref, hout_ref, xpad_scr, h_scr, *, chunk, group):
    L, G = chunk, group
    GL = G * L
    c = pl.program_id(1)
    pad = SUBLANES
    halo = SSD_CONV - 1
    hpg = SSD_HEADS // SSD_GROUPS
    assert hpg == 2 and hpg * SSD_HEAD_DIM == SSD_STATE

    @pl.when(c == 0)
    def _():
        xpad_scr[0:pad, :] = jnp.zeros((pad, SSD_CONV_DIM), F32)
        h_scr[...] = jnp.zeros(h_scr.shape, F32)

    xpad_scr[pad:pad + GL, :] = xbc_ref[...]
    xfull = xpad_scr[...]
    conv = cb_ref[...] + cw_ref[halo:halo + 1, :] * xfull[pad:pad + GL]
    for j in range(halo):
        conv = conv + cw_ref[j:j + 1, :] * pltpu.roll(xfull, halo - j, axis=0)[pad:pad + GL]
    xpad_scr[pad - halo:pad, :] = xpad_scr[pad + GL - halo:pad + GL, :]
    conv = _silu(conv)
    xs = conv[:, 0:GROUP_WIDTH]
    bm = conv[:, GROUP_WIDTH:2 * GROUP_WIDTH].astype(BF16)
    cm = conv[:, 2 * GROUP_WIDTH:3 * GROUP_WIDTH].astype(BF16)

    row = lax.broadcasted_iota(jnp.int32, (L, L), 0)
    col = lax.broadcasted_iota(jnp.int32, (L, L), 1)
    causal = row >= col
    tril = jnp.where(causal, 1.0, 0.0).astype(F32)
    dt = _softplus(dt_ref[...] + dtb_ref[...])
    da = dt * (-jnp.exp(alog_ref[...]) * LOG2_E)
    acs = [_dot(tril, da[i * L:(i + 1) * L, :], precision=HIGHEST) for i in range(G)]
    acs_t = [a.T for a in acs]
    e_acs = [jnp.exp2(a) for a in acs]
    e_end = [jnp.exp2(a[L - 1:L, :] - a) for a in acs]
    e_last = [jnp.exp2(a[L - 1:L, :]) for a in acs]

    keys = [(i, g) for i in range(G) for g in range(SSD_GROUPS)]
    rows_of = lambda x, i: x[i * L:(i + 1) * L]
    lanes_of = lambda x, g: x[:, g * SSD_STATE:(g + 1) * SSD_STATE]
    lane_lo = lax.broadcasted_iota(jnp.int32, (L, hpg * SSD_HEAD_DIM), 1) < SSD_HEAD_DIM
    row_lo = lax.broadcasted_iota(jnp.int32, (hpg * SSD_HEAD_DIM, SSD_STATE), 0) < SSD_HEAD_DIM
    head_cols = lambda a, g: jnp.where(lane_lo, a[:, g * hpg:g * hpg + 1], a[:, g * hpg + 1:g * hpg + 2])
    bg = {(i, g): lanes_of(rows_of(bm, i), g) for i, g in keys}
    cg = {(i, g): lanes_of(rows_of(cm, i), g) for i, g in keys}
    scores = {k: _dot_nt(cg[k], bg[k]) for k in keys}
    xdt = {(i, g): lanes_of(rows_of(xs, i), g) * head_cols(rows_of(dt, i), g) for i, g in keys}
    decay = {(i, h): jnp.exp2(jnp.where(causal, acs[i][:, h:h + 1] - acs_t[i][h:h + 1, :], -jnp.inf))
             for i in range(G) for h in range(SSD_HEADS)}
    p_mat = {(i, g): jnp.concatenate([(scores[(i, g)] * decay[(i, g * hpg + k)]).astype(BF16) for k in range(hpg)],
                                     axis=1) for i, g in keys}
    y_in = {k: _dot(p_mat[k], _bd(xdt[k].astype(BF16))) for k in keys}
    st = {(i, g): _dot_tn((xdt[(i, g)] * head_cols(e_end[i], g)).astype(BF16), bg[(i, g)]) for i, g in keys}

    y_rows = []
    for i in range(G):
        ys = []
        for g in range(SSD_GROUPS):
            h_prev = h_scr[g * hpg:(g + 1) * hpg].reshape(hpg * SSD_HEAD_DIM, SSD_STATE)
            ys.append(y_in[(i, g)] + _dot_nt(cg[(i, g)], h_prev.astype(BF16)) * head_cols(e_acs[i], g))
            keep = jnp.where(row_lo, e_last[i][:, g * hpg:g * hpg + 1], e_last[i][:, g * hpg + 1:g * hpg + 2])
            h_scr[g * hpg:(g + 1) * hpg] = (h_prev * keep + st[(i, g)]).reshape(hpg, SSD_HEAD_DIM, SSD_STATE)
        y_rows.append(jnp.concatenate(ys, axis=-1))
    y = jnp.concatenate(y_rows, axis=0) + xs * dsk_ref[...]
    y = y * _silu(z_ref[...])
    y_ref[...] = _rms(y, ng_ref[...])

    @pl.when(c == pl.num_programs(1) - 1)
    def _():
        hout_ref[0] = h_scr[...]
        conv_ref[0] = xpad_scr[pad - halo:pad, :]


def _ssd(z, xbc, dtr, lp, *, batch, seq):
    chunk = SSD_CHUNK
    rows = chunk * SSD_GROUP
    nc = seq // rows
    rspec = lambda w: pl.BlockSpec((rows, w), lambda b, c: (b * nc + c, 0))
    consts = (lp["conv_w"], lp["conv_b"], lp["dt_bias"], lp["a_log"], lp["d_skip"], lp["ssd_norm"])
    return pl.pallas_call(
        functools.partial(_ssd_body, chunk=chunk, group=SSD_GROUP),
        grid=(batch, nc),
        in_specs=[rspec(GROUP_WIDTH), rspec(SSD_CONV_DIM), rspec(LANES)] + [_pspec(a) for a in consts],
        out_specs=[rspec(GROUP_WIDTH),
                   pl.BlockSpec((1, SSD_CONV - 1, SSD_CONV_DIM), lambda b, c: (b, 0, 0)),
                   pl.BlockSpec((1, SSD_HEADS, SSD_HEAD_DIM, SSD_STATE), lambda b, c: (b, 0, 0, 0))],
        out_shape=[jax.ShapeDtypeStruct((batch * seq, GROUP_WIDTH), F32),
                   jax.ShapeDtypeStruct((batch, SSD_CONV - 1, SSD_CONV_DIM), F32),
                   jax.ShapeDtypeStruct((batch, SSD_HEADS, SSD_HEAD_DIM, SSD_STATE), F32)],
        scratch_shapes=[pltpu.VMEM((SUBLANES + rows, SSD_CONV_DIM), F32),
                        pltpu.VMEM((SSD_HEADS, SSD_HEAD_DIM, SSD_STATE), F32)],
        compiler_params=_cparams("parallel", "arbitrary"),
        name="ssd",
    )(z, xbc, dtr, *[_parg(a) for a in consts])


def _ssd_step_body(z_ref, xbc_ref, dt_ref, conv0_ref, h0_ref, *rest, seq, batch, layer):
    hdone_ref, rest = (rest[0], rest[1:]) if layer else (None, rest)
    (cw_ref, cb_ref, dtb_ref, aneg_ref, dsk_ref, ng_ref, hexp_ref, y_ref, conv_ref, hout_ref,
     xs_scr, bm_scr, cm_scr, xdt_scr, dec_scr, y_scr) = rest
    T, B = seq, batch
    if layer:
        hout_ref[0:layer] = hdone_ref[...]
    GW = GROUP_WIDTH
    j = pl.program_id(0)
    tiles = SSD_STEP_TILES

    @pl.when(j == 0)
    def _():
        rows = [conv0_ref[i] for i in range(SSD_CONV - 1)]
        rows += [xbc_ref[t * B:(t + 1) * B, :] for t in range(T)]
        for t in range(T):
            conv = cb_ref[...] + cw_ref[0:1, :] * rows[t]
            for i in range(1, SSD_CONV):
                conv = conv + cw_ref[i:i + 1, :] * rows[t + i]
            conv = _silu(conv)
            xs = conv[:, 0:GW]
            xs_scr[t] = xs
            for g in range(SSD_GROUPS):
                bm_scr[t, g] = conv[:, GW + g * SSD_STATE:GW + (g + 1) * SSD_STATE].T
                cm_scr[t, g] = conv[:, 2 * GW + g * SSD_STATE:2 * GW + (g + 1) * SSD_STATE].T
            dt = _softplus(dt_ref[t * B:(t + 1) * B, :] + dtb_ref[...])
            dte = _dot(dt, hexp_ref[...], precision=HIGHEST)
            xdt_scr[t] = (xs * dte).T
            dec_scr[t] = jnp.exp(dte * aneg_ref[...]).T
        for i in range(SSD_CONV - 1):
            conv_ref[i] = rows[T + i]

    hp0 = j * tiles
    grp = hp0 // (SSD_HEAD_DIM * (SSD_HEADS // SSD_GROUPS))
    for q in range(tiles):
        hp = pl.ds(hp0 + q, 1)
        h = h0_ref[:, q, :].T
        for t in range(T):
            h = h * dec_scr[t, hp, :] + bm_scr[t, grp] * xdt_scr[t, hp, :]
            y_scr[t, hp, :] = jnp.sum(h * cm_scr[t, grp], axis=0, keepdims=True)
        hout_ref[layer, :, q, :] = h.T

    @pl.when(j == pl.num_programs(0) - 1)
    def _():
        for t in range(T):
            y = y_scr[t].T + xs_scr[t] * dsk_ref[...]
            y = y * _silu(z_ref[t * B:(t + 1) * B, :])
            y_ref[t * B:(t + 1) * B, :] = _rms(y, ng_ref[...])


def _layer_state_specs(layer, block, axis):
    idx = lambda first: (lambda j: (first,) + tuple(j if a == axis else 0 for a in range(len(block))))
    cur = pl.BlockSpec((None,) + block, idx(layer))
    prev = [pl.BlockSpec((layer,) + block, idx(0))] if layer else []
    out = pl.BlockSpec((layer + 1,) + block, idx(0))
    return cur, prev, out


def _ssd_step(z, xbc, dtr, conv_all, h_all, h_done, lp, *, batch, seq, layer):
    n = batch * seq
    srows = SSD_HEADS * SSD_HEAD_DIM
    consts = (lp["conv_w"], lp["conv_b"], lp["dt_bias"], lp["a_neg_exp"], lp["d_skip"], lp["ssd_norm"], lp["head_expand"])
    hspec, prev_specs, hout_spec = _layer_state_specs(layer, (batch, SSD_STEP_TILES, SSD_STATE), 1)
    prev_args = [h_done] if layer else []
    cshape = (SSD_CONV - 1, batch, SSD_CONV_DIM)
    return pl.pallas_call(
        functools.partial(_ssd_step_body, seq=seq, batch=batch, layer=layer),
        grid=(srows // SSD_STEP_TILES,),
        in_specs=[_full_spec((n, GROUP_WIDTH)), _full_spec((n, SSD_CONV_DIM)), _full_spec((n, LANES)),
                  pl.BlockSpec((None,) + cshape, lambda j: (layer, 0, 0, 0)), hspec] + prev_specs
                 + [_pspec(a) for a in consts],
        out_specs=[_full_spec((n, GROUP_WIDTH)), _full_spec(cshape), hout_spec],
        out_shape=[jax.ShapeDtypeStruct((n, GROUP_WIDTH), F32),
                   jax.ShapeDtypeStruct(cshape, F32),
                   jax.ShapeDtypeStruct((layer + 1, batch, srows, SSD_STATE), F32)],
        scratch_shapes=[pltpu.VMEM((seq, batch, GROUP_WIDTH), F32),
                        pltpu.VMEM((seq, SSD_GROUPS, SSD_STATE, batch), F32),
                        pltpu.VMEM((seq, SSD_GROUPS, SSD_STATE, batch), F32),
                        pltpu.VMEM((seq, GROUP_WIDTH, batch), F32),
                        pltpu.VMEM((seq, GROUP_WIDTH, batch), F32),
                        pltpu.VMEM((seq, GROUP_WIDTH, batch), F32)],
        compiler_params=_cparams("arbitrary"),
        name="ssd_step",
    )(z, xbc, dtr, conv_all, h_all.reshape(h_all.shape[0], batch, srows, SSD_STATE),
      *prev_args, *[_parg(a) for a in consts])


PAIR = 2 * RWKV_HEAD
RWKV_PAIRS = RWKV_HEADS // 2


def _bd(x):
    half = x.shape[1] // 2
    lane = lax.broadcasted_iota(jnp.int32, x.shape, 1)
    zero = jnp.zeros_like(x)
    return jnp.concatenate([jnp.where(lane < half, x, zero), jnp.where(lane >= half, x, zero)], axis=0)


def _half_sums(x, lo):
    s_lo = jnp.sum(jnp.where(lo, x, 0.0), axis=-1, keepdims=True)
    s_hi = jnp.sum(jnp.where(lo, 0.0, x), axis=-1, keepdims=True)
    return jnp.where(lo, s_lo, s_hi)


def _head_sum(x):
    lo = lax.broadcasted_iota(jnp.int32, (x.shape[0], PAIR), 1) < RWKV_HEAD
    return jnp.concatenate([_half_sums(x[:, p * PAIR:(p + 1) * PAIR], lo) for p in range(RWKV_PAIRS)], axis=-1)


def _rwkv_pointwise(u, prev, mu_ref, w0_ref, w2_ref, a0_ref, a2_ref, g2_ref, kk_ref, ka_ref):
    GW = GROUP_WIDTH
    xs = u + (prev - u) * mu_ref[...]
    r = xs[:, 0:GW]
    k = xs[:, GW:2 * GW]
    v = xs[:, 2 * GW:3 * GW]
    wd = xs[:, 3 * GW:3 * GW + 64]
    ad = xs[:, 3 * GW + 64:3 * GW + 128]
    gd = xs[:, 3 * GW + 128:3 * GW + 256]
    w_lin = w0_ref[...] + _dot(jnp.tanh(wd).astype(BF16), w2_ref[...])
    logdecay = -jnp.exp(-_softplus(-w_lin) - 0.5)
    a = _sigmoid(a0_ref[...] + _dot(ad.astype(BF16), a2_ref[...]))
    g = _dot(_sigmoid(gd).astype(BF16), g2_ref[...])
    kk = k * kk_ref[...]
    kk = kk / jnp.maximum(jnp.sqrt(_head_sum(kk * kk)), 1e-12)
    k = k * (1.0 + (a - 1.0) * ka_ref[...])
    return r, k, v, logdecay, a, g, kk


def _rwkv_finish(y, r, k, v, g, rk_ref, lng_ref, lnb_ref):
    mean = _head_sum(y) * (1.0 / RWKV_HEAD)
    yc = y - mean
    var = _head_sum(yc * yc) * (1.0 / RWKV_HEAD)
    y = yc * lax.rsqrt(var + RWKV_LN_EPS) * lng_ref[...] + lnb_ref[...]
    bonus = _head_sum(r * k * rk_ref[...]) * v
    return (y + bonus) * g


def _rwkv_body(u_ref, mu_ref, w0_ref, w2_ref, a0_ref, a2_ref, g2_ref, kk_ref, ka_ref, rk_ref,
               lng_ref, lnb_ref, y_ref, shift_ref, sout_ref, upad_scr, s_scr, *, chunk, group):
    L, G = chunk, group
    GL = G * L
    c = pl.program_id(1)
    pad = SUBLANES

    @pl.when(c == 0)
    def _():
        upad_scr[0:pad, :] = jnp.zeros((pad, RWKV_PROJ), F32)
        s_scr[...] = jnp.zeros(s_scr.shape, F32)

    u = u_ref[...]
    upad_scr[pad:pad + GL, :] = u
    prev = pltpu.roll(upad_scr[...], 1, axis=0)[pad:pad + GL]
    upad_scr[pad - 1:pad, :] = u[GL - 1:GL, :]
    r, k, v, logdecay, a, g, kk = _rwkv_pointwise(u, prev, mu_ref, w0_ref, w2_ref, a0_ref, a2_ref, g2_ref,
                                                  kk_ref, ka_ref)

    tril = jnp.where(lax.broadcasted_iota(jnp.int32, (L, L), 0) >= lax.broadcasted_iota(jnp.int32, (L, L), 1),
                     1.0, 0.0).astype(F32)
    cl = jnp.concatenate([_dot(tril, logdecay[i * L:(i + 1) * L, :], precision=HIGHEST) for i in range(G)], axis=0)
    e_in = jnp.exp(cl)
    e_inv = jnp.exp(-cl)
    r_t = r * e_in
    r_tb = r_t.astype(BF16)
    a_tb = (-kk * jnp.exp(cl - logdecay)).astype(BF16)
    b_tb = (kk * a * e_inv).astype(BF16)
    k_tb = (k * e_inv).astype(BF16)
    vb = v.astype(BF16)

    row = lax.broadcasted_iota(jnp.int32, (L, PAIR), 0)
    colh = lax.broadcasted_iota(jnp.int32, (L, PAIR), 1) & (RWKV_HEAD - 1)
    strict = row > colh
    incl = row >= colh
    eye_pair = jnp.where(row == colh, 1.0, 0.0).astype(F32)
    lane_lo = lax.broadcasted_iota(jnp.int32, (RWKV_HEAD, PAIR), 1) < RWKV_HEAD
    same_head = (lax.broadcasted_iota(jnp.int32, (PAIR, PAIR), 0) < RWKV_HEAD) == \
                (lax.broadcasted_iota(jnp.int32, (PAIR, PAIR), 1) < RWKV_HEAD)

    streams = [(i, p) for i in range(G) for p in range(RWKV_PAIRS)]
    ns = len(streams)
    blk = lambda x, i, p: x[i * L:(i + 1) * L, p * PAIR:(p + 1) * PAIR]
    lhs = [jnp.concatenate([blk(a_tb, i, p), blk(r_tb, i, p)], axis=0) for i, p in streams]
    m_ab = [_dot_nt(lhs[s], _bd(blk(b_tb, i, p))) for s, (i, p) in enumerate(streams)]
    m_ak = [_dot_nt(lhs[s], _bd(blk(k_tb, i, p))) for s, (i, p) in enumerate(streams)]
    n_ab = [jnp.where(strict, m[0:L], 0.0) for m in m_ab]
    m_rb = [jnp.where(incl, m[L:2 * L], 0.0).astype(BF16) for m in m_ab]
    n_ak = [jnp.where(strict, m[0:L], 0.0).astype(BF16) for m in m_ak]
    m_rk = [jnp.where(incl, m[L:2 * L], 0.0).astype(BF16) for m in m_ak]
    tinv = [eye_pair + n for n in n_ab]
    pwb = [n.astype(BF16) for n in n_ab]
    pw = [_dot(x, _bd(x)) for x in pwb]
    for _ in range(int(math.log2(L)) - 2):
        pwb = [x.astype(BF16) for x in pw]
        both = [_dot(jnp.concatenate([pwb[s], tinv[s].astype(BF16)], axis=0), _bd(pwb[s])) for s in range(ns)]
        pw = [x[0:L] for x in both]
        tinv = [tinv[s] + both[s][L:2 * L] for s in range(ns)]
    pwb = [x.astype(BF16) for x in pw]
    tinv = [tinv[s] + _dot(tinv[s].astype(BF16), _bd(pwb[s])) for s in range(ns)]
    tinvb = [x.astype(BF16) for x in tinv]
    nv_mv = [_dot(jnp.concatenate([n_ak[s], m_rk[s]], axis=0), _bd(blk(vb, i, p))) for s, (i, p) in enumerate(streams)]
    wu = [_dot(tinvb[s], jnp.concatenate([_bd(blk(a_tb, i, p)), _bd(nv_mv[s][0:L].astype(BF16))], axis=1))
          for s, (i, p) in enumerate(streams)]
    wub = [x.astype(BF16) for x in wu]
    qy = [_dot(m_rb[s], jnp.concatenate([_bd(wub[s][:, 0:PAIR]), _bd(wub[s][:, PAIR:2 * PAIR])], axis=1))
          for s in range(ns)]
    q = [(blk(r_t, i, p) + qy[s][:, 0:PAIR]).astype(BF16) for s, (i, p) in enumerate(streams)]
    y_loc = [qy[s][:, PAIR:2 * PAIR] + nv_mv[s][L:2 * L] for s in range(ns)]
    zeros_b = jnp.zeros((L, PAIR), BF16)
    mg = [_dot_tn(jnp.concatenate([wub[s], jnp.concatenate([zeros_b, blk(vb, i, p)], axis=1)], axis=0),
                  jnp.concatenate([blk(b_tb, i, p), blk(k_tb, i, p)], axis=0))
          for s, (i, p) in enumerate(streams)]
    p_end = [e_in[(i + 1) * L - 1:(i + 1) * L, p * PAIR:(p + 1) * PAIR] for i, p in streams]
    m_t = [(jnp.where(same_head, mg[s][0:PAIR], 0.0) * p_end[s]).astype(BF16) for s in range(ns)]
    g_t = [jnp.where(lane_lo, mg[s][PAIR:PAIR + RWKV_HEAD], mg[s][PAIR + RWKV_HEAD:2 * PAIR]) * p_end[s]
           for s in range(ns)]

    y_rows = []
    for i in range(G):
        y_pairs = []
        for p in range(RWKV_PAIRS):
            s = i * RWKV_PAIRS + p
            s0 = s_scr[p]
            s0b = s0.astype(BF16)
            y_pairs.append(_dot_nt(q[s], _bd(s0b)) + y_loc[s])
            s_scr[p] = s0 * p_end[s] + _dot(s0b, m_t[s]) + g_t[s]
        y_rows.append(jnp.concatenate(y_pairs, axis=-1))
    y = jnp.concatenate(y_rows, axis=0)
    y_ref[...] = _rwkv_finish(y, r, k, v, g, rk_ref, lng_ref, lnb_ref)

    @pl.when(c == pl.num_programs(1) - 1)
    def _():
        sout_ref[0] = s_scr[...]
        shift_ref[0] = upad_scr[pad - 1:pad, :]


_RWKV_PARAM_NAMES = ("mu", "w0", "w2", "a0", "a2", "g2", "k_k", "k_a", "r_k", "ln_g", "ln_b")


def _rwkv(u, p, *, batch, seq):
    rows = RWKV_CHUNK * RWKV_GROUP
    nc = seq // rows
    params = [p[n] for n in _RWKV_PARAM_NAMES]
    sspec = pl.BlockSpec((1, RWKV_PAIRS, RWKV_HEAD, PAIR), lambda b, c: (b, 0, 0, 0))
    y, shift, s_last = pl.pallas_call(
        functools.partial(_rwkv_body, chunk=RWKV_CHUNK, group=RWKV_GROUP),
        grid=(batch, nc),
        in_specs=[pl.BlockSpec((rows, RWKV_PROJ), lambda b, c: (b * nc + c, 0))] + [_pspec(a) for a in params],
        out_specs=[pl.BlockSpec((rows, GROUP_WIDTH), lambda b, c: (b * nc + c, 0)),
                   pl.BlockSpec((1, 1, RWKV_PROJ), lambda b, c: (b, 0, 0)), sspec],
        out_shape=[jax.ShapeDtypeStruct((batch * seq, GROUP_WIDTH), F32),
                   jax.ShapeDtypeStruct((batch, 1, RWKV_PROJ), F32),
                   jax.ShapeDtypeStruct((batch, RWKV_PAIRS, RWKV_HEAD, PAIR), F32)],
        scratch_shapes=[pltpu.VMEM((SUBLANES + rows, RWKV_PROJ), F32),
                        pltpu.VMEM((RWKV_PAIRS, RWKV_HEAD, PAIR), F32)],
        compiler_params=_cparams("parallel", "arbitrary"),
        name="rwkv",
    )(u, *[_parg(a) for a in params])
    s_last = s_last.reshape(batch, RWKV_PAIRS, RWKV_HEAD, 2, RWKV_HEAD).transpose(0, 1, 3, 2, 4).reshape(
        batch, RWKV_HEADS, RWKV_HEAD, RWKV_HEAD)
    return y, shift.reshape(batch, RWKV_PROJ), s_last


def _rwkv_step_body(u_ref, shift0_ref, s0_ref, *rest, seq, batch, layer):
    sdone_ref, rest = (rest[0], rest[1:]) if layer else (None, rest)
    (mu_ref, w0_ref, w2_ref, a0_ref, a2_ref, g2_ref, kk_ref, ka_ref, rk_ref, lng_ref, lnb_ref, y_ref, sout_ref,
     r_scr, w_scr, k_scr, b_scr, nkk_scr, v_scr, y_scr) = rest
    T, B = seq, batch
    j = pl.program_id(0)
    if layer:
        sout_ref[0:layer] = sdone_ref[...]
    tiles = RWKV_STEP_TILES

    def pointwise(t):
        u = u_ref[t * B:(t + 1) * B, :]
        prev = shift0_ref[...] if t == 0 else u_ref[(t - 1) * B:t * B, :]
        return _rwkv_pointwise(u, prev, mu_ref, w0_ref, w2_ref, a0_ref, a2_ref, g2_ref, kk_ref, ka_ref)

    @pl.when(j == 0)
    def _():
        for t in range(T):
            r, k, v, logdecay, a, _, kk = pointwise(t)
            r_scr[t] = r.T
            w_scr[t] = jnp.exp(logdecay).T
            k_scr[t] = k.T
            b_scr[t] = (kk * a).T
            nkk_scr[t] = (-kk).T
            v_scr[t] = v.T

    i0 = j * tiles
    keys = pl.ds(pl.multiple_of((i0 // RWKV_HEAD) * RWKV_HEAD, RWKV_HEAD), RWKV_HEAD)
    for q in range(tiles):
        vi = pl.ds(i0 + q, 1)
        s = s0_ref[q]
        for t in range(T):
            sa = jnp.sum(s * nkk_scr[t, keys, :], axis=0, keepdims=True)
            s = s * w_scr[t, keys, :] + k_scr[t, keys, :] * v_scr[t, vi, :] + b_scr[t, keys, :] * sa
            y_scr[t, vi, :] = jnp.sum(s * r_scr[t, keys, :], axis=0, keepdims=True)
        sout_ref[layer, q] = s

    @pl.when(j == pl.num_programs(0) - 1)
    def _():
        for t in range(T):
            r, k, v, _, _, g, _ = pointwise(t)
            y_ref[t * B:(t + 1) * B, :] = _rwkv_finish(y_scr[t].T, r, k, v, g, rk_ref, lng_ref, lnb_ref)


def _rwkv_step(u, shift0, s_all, s_done, p, *, batch, seq, layer):
    n = batch * seq
    srows = RWKV_HEADS * RWKV_HEAD
    params = [p[nm] for nm in _RWKV_PARAM_NAMES]
    sspec, prev_specs, sout_spec = _layer_state_specs(layer, (RWKV_STEP_TILES, RWKV_HEAD, batch), 0)
    prev_args = [s_done] if layer else []
    tposed = pltpu.VMEM((seq, GROUP_WIDTH, batch), F32)
    return pl.pallas_call(
        functools.partial(_rwkv_step_body, seq=seq, batch=batch, layer=layer),
        grid=(srows // RWKV_STEP_TILES,),
        in_specs=[_full_spec((n, RWKV_PROJ)), _full_spec((batch, RWKV_PROJ)), sspec] + prev_specs
                 + [_pspec(a) for a in params],
        out_specs=[_full_spec((n, GROUP_WIDTH)), sout_spec],
        out_shape=[jax.ShapeDtypeStruct((n, GROUP_WIDTH), F32),
                   jax.ShapeDtypeStruct((layer + 1, srows, RWKV_HEAD, batch), F32)],
        scratch_shapes=[tposed] * 7,
        compiler_params=_cparams("arbitrary"),
        name="rwkv_step",
    )(u, shift0, s_all, *prev_args, *[_parg(a) for a in params])


def _s5_body(u_ref, hre0_ref, him0_ref, are_ref, aim_ref, bmat_ref, cmat_ref, d_ref, gw_ref, gb_ref,
             y_ref, hre_ref, him_ref, hs_scr, tm_scr, *, steps, batch_major):
    c = pl.program_id(1)
    ns = S5_WIDTH
    bsub = SUBLANES

    @pl.when(c == 0)
    def _():
        hre_ref[...] = hre0_ref[...]
        him_ref[...] = him0_ref[...]

    if batch_major:
        for b in range(bsub):
            tm_scr[:, b, :] = u_ref[b]
        u = tm_scr[...].reshape(steps * bsub, GROUP_WIDTH)
    else:
        u = u_ref[...].reshape(steps * bsub, GROUP_WIDTH)
    are = jnp.broadcast_to(are_ref[...], (bsub, ns))
    aim = jnp.broadcast_to(aim_ref[...], (bsub, ns))
    hre, him = hre_ref[...], him_ref[...]
    sub = min(S5_SUB, steps)
    rows = sub * bsub
    outs = []
    for k in range(steps // sub):
        r0 = k * rows
        u_k = u[r0:r0 + rows]
        hs_scr[r0:r0 + rows, :] = _dot(u_k.astype(BF16), bmat_ref[...])
        for t in range(sub):
            rs = slice(r0 + t * bsub, r0 + (t + 1) * bsub)
            hre, him = (are * hre - aim * him + hs_scr[rs, 0:ns], are * him + aim * hre + hs_scr[rs, ns:2 * ns])
            hs_scr[rs, 0:ns] = hre
            hs_scr[rs, ns:2 * ns] = him
        y = _dot(hs_scr[r0:r0 + rows, :].astype(BF16), cmat_ref[...]) + u_k * d_ref[...]
        y = _gelu_tanh(y)
        yy = _dot(y.astype(BF16), gw_ref[...]) + gb_ref[...]
        outs.append(yy[:, 0:GROUP_WIDTH] * _sigmoid(yy[:, GROUP_WIDTH:2 * GROUP_WIDTH]))
    hre_ref[...] = hre
    him_ref[...] = him
    out = jnp.concatenate(outs, axis=0).reshape(steps, bsub, GROUP_WIDTH)
    if batch_major:
        tm_scr[...] = out
        for b in range(bsub):
            y_ref[b] = tm_scr[:, b, :]
    else:
        y_ref[...] = out


def _time_specs(u, batch_major):
    bsub = SUBLANES
    if batch_major:
        batch, seq, _ = u.shape
        steps = min(TM_CHUNK, seq)
        spec = pl.BlockSpec((bsub, steps, GROUP_WIDTH), lambda b, c: (b, c, 0))
    else:
        seq, batch, _ = u.shape
        steps = min(TM_CHUNK, seq)
        spec = pl.BlockSpec((steps, bsub, GROUP_WIDTH), lambda b, c: (c, b, 0))
    return batch, seq, steps, spec


def _s5(u, hre0, him0, lp, *, batch_major):
    batch, seq, steps, tspec = _time_specs(u, batch_major)
    bsub = SUBLANES
    hspec = pl.BlockSpec((bsub, S5_WIDTH), lambda b, c: (b, 0))
    consts = (lp["s5_are"], lp["s5_aim"], lp["s5_bmat"], lp["s5_cmat"], lp["s5_d"], lp["s5_gw"], lp["s5_gb"])
    return pl.pallas_call(
        functools.partial(_s5_body, steps=steps, batch_major=batch_major),
        grid=(batch // bsub, seq // steps),
        in_specs=[tspec, hspec, hspec] + [_pspec(a) for a in consts],
        out_specs=[tspec, hspec, hspec],
        out_shape=[jax.ShapeDtypeStruct(u.shape, F32),
                   jax.ShapeDtypeStruct((batch, S5_WIDTH), F32),
                   jax.ShapeDtypeStruct((batch, S5_WIDTH), F32)],
        scratch_shapes=[pltpu.VMEM((steps * bsub, 2 * S5_WIDTH), F32),
                        pltpu.VMEM((steps, bsub, GROUP_WIDTH), F32)],
        compiler_params=_cparams("parallel", "arbitrary"),
        name="s5",
    )(u, hre0, him0, *[_parg(a) for a in consts])


def _pool_body(u_ref, buf0_ref, pw_ref, sc_ref, y_ref, buf_ref, f_scr, tm_scr, *, steps, pos0, batch_major):
    c = pl.program_id(1)
    bsub = SUBLANES
    GW = GROUP_WIDTH
    halo = POOL_BUF + 1

    @pl.when(c == 0)
    def _():
        f_scr[0] = jnp.zeros((bsub, GW), F32)
        f_scr[1:halo] = buf0_ref[...]

    if batch_major:
        for b in range(bsub):
            f_scr[halo:halo + steps, b, :] = u_ref[b]
    else:
        f_scr[halo:halo + steps] = u_ref[...]
    f = f_scr[...]
    u = f[halo:halo + steps]
    s2 = f[1:] + f[:-1]
    s4 = s2[2:] + s2[:-2]
    s8 = s4[4:] + s4[:-4]
    s16 = s8[8:] + s8[:-8]
    f_scr[0:halo] = f[steps:steps + halo]
    lane = lax.broadcasted_iota(jnp.int32, (steps, bsub, GW), 2)
    tpos = lax.broadcasted_iota(jnp.int32, (steps, bsub, GW), 0) + (pos0 + 1) + c * steps
    win = jnp.where(lane < POOL_CH, s2[halo - 1:halo - 1 + steps],
                    jnp.where(lane < 2 * POOL_CH, s4[halo - 3:halo - 3 + steps],
                              jnp.where(lane < 3 * POOL_CH, s8[halo - 7:halo - 7 + steps],
                                        s16[halo - 15:halo - 15 + steps])))
    wlen = jnp.where(lane < POOL_CH, POOL_WINDOWS[0],
                     jnp.where(lane < 2 * POOL_CH, POOL_WINDOWS[1],
                               jnp.where(lane < 3 * POOL_CH, POOL_WINDOWS[2], POOL_WINDOWS[3])))
    cnt = jnp.minimum(tpos, wlen).astype(F32)
    pooled = (win / cnt - u).reshape(steps * bsub, GW)
    y = (_dot(pooled.astype(BF16), pw_ref[...]) * sc_ref[...]).reshape(steps, bsub, GW)
    if batch_major:
        tm_scr[...] = y
        for b in range(bsub):
            y_ref[b] = tm_scr[:, b, :]
    else:
        y_ref[...] = y

    @pl.when(c == pl.num_programs(1) - 1)
    def _():
        buf_ref[...] = f_scr[1:halo]


def _pool(u, buf0, lp, *, pos0, batch_major, layer=None):
    batch, seq, steps, tspec = _time_specs(u, batch_major)
    bsub = SUBLANES
    bblock = (POOL_BUF, bsub, GROUP_WIDTH)
    bspec = pl.BlockSpec(bblock, lambda b, c: (0, b, 0))
    if layer is None:
        bspec_in = bspec
    else:
        bspec_in = pl.BlockSpec((None,) + bblock, lambda b, c: (layer, 0, b, 0))
    return pl.pallas_call(
        functools.partial(_pool_body, steps=steps, pos0=pos0, batch_major=batch_major),
        grid=(batch // bsub, seq // steps),
        in_specs=[tspec, bspec_in, _pspec(lp["pool_w"]), _pspec(lp["pool_scale"])],
        out_specs=[tspec, bspec],
        out_shape=[jax.ShapeDtypeStruct(u.shape, F32), jax.ShapeDtypeStruct((POOL_BUF, batch, GROUP_WIDTH), F32)],
        scratch_shapes=[pltpu.VMEM((POOL_BUF + 1 + steps, bsub, GROUP_WIDTH), F32),
                        pltpu.VMEM((steps, bsub, GROUP_WIDTH), F32)],
        compiler_params=_cparams("parallel", "arbitrary"),
        name="pool",
    )(u, buf0, _parg(lp["pool_w"]), _parg(lp["pool_scale"]))


def _block_diag(blocks):
    n, g, r, c = blocks.shape
    eye = jnp.eye(g, dtype=blocks.dtype)
    return (eye[None, :, None, :, None] * blocks[:, :, :, None, :]).reshape(n, g * r, g * c)


def _stacked_params(P):
    row = lambda a: a.reshape(a.shape[0], 1, -1)
    pad_lanes = lambda a: jnp.pad(a, ((0, 0), (0, LANES - a.shape[1])))
    bf = lambda a: a.astype(BF16)
    w_in = P["w_in"]
    split = GROUP_WIDTH + SSD_CONV_DIM
    w_all = jnp.concatenate([w_in[:, :, :split], w_in[:, :, split + SSD_HEADS:],
                             jnp.pad(w_in[:, :, split:split + SSD_HEADS], ((0, 0), (0, 0), (0, LANES - SSD_HEADS)))],
                            axis=2)

    lam = lax.complex(P["s5_lam_re"], P["s5_lam_im"])
    a_bar = jnp.exp(lam * jnp.exp(P["s5_log_step"])[..., None])
    b_bar = ((a_bar - 1.0) / lam)[..., None] * lax.complex(P["s5_b_re"], P["s5_b_im"])
    b_t = jnp.swapaxes(b_bar, 2, 3)
    bmat = jnp.concatenate([_block_diag(jnp.real(b_t)), _block_diag(jnp.imag(b_t))], axis=2)
    c_t = jnp.swapaxes(lax.complex(P["s5_c_re"], P["s5_c_im"]), 2, 3)
    cmat = jnp.concatenate([_block_diag(jnp.real(c_t)), -_block_diag(jnp.imag(c_t))], axis=1)

    out = dict(
        norm_ffn1=row(P["norm_ffn1"]), ffn1_in=P["ffn1_in"], ffn1_out=P["ffn1_out"],
        norm_mix=row(P["norm_mix"]), w_all=bf(w_all),
        conv_w=P["ssd_conv_w"], conv_b=row(P["ssd_conv_b"]),
        dt_bias=row(pad_lanes(P["ssd_dt_bias"])), a_log=row(pad_lanes(P["ssd_a_log"])),
        a_neg_exp=row(jnp.repeat(-jnp.exp(P["ssd_a_log"]), SSD_HEAD_DIM, axis=1)),
        d_skip=row(jnp.repeat(P["ssd_d"], SSD_HEAD_DIM, axis=1)), ssd_norm=row(P["ssd_norm"]),
        s5_are=row(jnp.real(a_bar)), s5_aim=row(jnp.imag(a_bar)), s5_bmat=bf(bmat), s5_cmat=bf(cmat),
        s5_d=row(P["s5_d"]), s5_gw=bf(P["s5_glu_w"]), s5_gb=row(P["s5_glu_b"]),
        pool_w=bf(_block_diag(P["pool_w"])), pool_scale=row(P["pool_scale"]),
        w_out=bf(P["w_out"]),
        norm_ffn2=row(P["norm_ffn2"]), ffn2_in=P["ffn2_in"], ffn2_out=P["ffn2_out"],
    )
    for name in _RWKV_PARAM_NAMES:
        a = P["rwkv_" + name]
        out["rwkv_" + name] = bf(a) if name in ("w2", "a2", "g2") else row(a)
    return out


def _layer_params(stacked, l):
    lp = {k: _Layered((v, l)) for k, v in stacked.items()}
    lp["rwkv"] = {n: lp["rwkv_" + n] for n in _RWKV_PARAM_NAMES}
    lp["head_expand"] = jnp.pad(jnp.repeat(jnp.eye(SSD_HEADS, dtype=F32), SSD_HEAD_DIM, axis=1),
                                ((0, LANES - SSD_HEADS), (0, 0)))
    return lp


def _mixers_prompt(lp, proj, *, batch, seq):
    z, xbc, ur, us5, upool, dtr = proj
    y_ssd, conv_new, ssd_new = _ssd(z, xbc, dtr, lp, batch=batch, seq=seq)
    y_rwkv, shift_new, rwkv_new = _rwkv(ur, lp["rwkv"], batch=batch, seq=seq)
    zeros = jnp.zeros((batch, S5_WIDTH), F32)
    bm = lambda a: a.reshape(batch, seq, a.shape[-1])
    rows = lambda a: a.reshape(batch * seq, a.shape[-1])
    y_s5, s5re, s5im = _s5(bm(us5), zeros, zeros, lp, batch_major=True)
    y_pool, pool_new = _pool(bm(upool), jnp.zeros((POOL_BUF, batch, GROUP_WIDTH), F32), lp, pos0=0,
                             batch_major=True)
    ys = (y_ssd, y_rwkv, rows(y_s5), rows(y_pool))
    states = (conv_new, ssd_new, shift_new, rwkv_new, s5re.reshape(batch, S5_GROUPS, S5_STATE),
              s5im.reshape(batch, S5_GROUPS, S5_STATE), jnp.swapaxes(pool_new, 0, 1))
    return ys, states


def _mixers_decode(lp, proj, states, done, *, batch, seq, layer):
    z, xbc, ur, us5, upool, dtr = proj
    shift0, s5re0, s5im0 = (states[i][layer] for i in (2, 4, 5))
    ssd_done, rwkv_done = (done[1], done[3]) if layer else (None, None)
    y_ssd, conv_new, ssd_new = _ssd_step(z, xbc, dtr, states[0], states[1], ssd_done, lp, batch=batch, seq=seq,
                                         layer=layer)
    y_rwkv, rwkv_new = _rwkv_step(ur, shift0, states[3], rwkv_done, lp["rwkv"], batch=batch, seq=seq, layer=layer)
    shift_new = ur[(seq - 1) * batch:, :]
    tm = lambda a: a.reshape(seq, batch, a.shape[-1])
    y_s5, s5re, s5im = _s5(tm(us5), s5re0.reshape(batch, S5_WIDTH), s5im0.reshape(batch, S5_WIDTH), lp,
                           batch_major=False)
    y_pool, pool_new = _pool(tm(upool), states[6], lp, pos0=PAST_LEN, batch_major=False, layer=layer)
    rows = lambda a: a.reshape(seq * batch, a.shape[-1])
    ys = (y_ssd, y_rwkv, rows(y_s5), rows(y_pool))
    new_states = (jnp.swapaxes(conv_new, 0, 1), ssd_new, shift_new, rwkv_new,
                  s5re.reshape(batch, S5_GROUPS, S5_STATE), s5im.reshape(batch, S5_GROUPS, S5_STATE),
                  jnp.swapaxes(pool_new, 0, 1))
    return ys, new_states


_WIDTHS = (GROUP_WIDTH, SSD_CONV_DIM, RWKV_PROJ, GROUP_WIDTH, GROUP_WIDTH, LANES)


def _trunk(x_p, x_s, layer_params, norm_final, mixers_p, mixers_s):
    st_p, st_s = [], []
    mix_p, mix_s, lp = None, None, None
    for l, lp_next in enumerate(layer_params):
        if l > 0:
            x_s, wg, wu, wo = _ffn_cast(x_s, lp["norm_ffn2"], lp["ffn2_in"], lp["ffn2_out"], mix=mix_s, wmix=lp["w_out"])
            x_p = _ffn(x_p, lp["norm_ffn2"], wg, wu, wo, mix=mix_p, wmix=lp["w_out"])
        lp = lp_next
        x_s, wg, wu, wo = _ffn_cast(x_s, lp["norm_ffn1"], lp["ffn1_in"], lp["ffn1_out"])
        x_p = _ffn(x_p, lp["norm_ffn1"], wg, wu, wo)
        mix_p, st = mixers_p(l, lp, _inproj(x_p, lp["norm_mix"], lp["w_all"], _WIDTHS), st_p[-1] if st_p else None)
        st_p.append(st)
        mix_s, st = mixers_s(l, lp, _inproj(x_s, lp["norm_mix"], lp["w_all"], _WIDTHS), st_s[-1] if st_s else None)
        st_s.append(st)
    x_s, wg, wu, wo = _ffn_cast(x_s, lp["norm_ffn2"], lp["ffn2_in"], lp["ffn2_out"], mix=mix_s, wmix=lp["w_out"],
                                gf=norm_final)
    x_p = _ffn(x_p, lp["norm_ffn2"], wg, wu, wo, mix=mix_p, wmix=lp["w_out"], gf=norm_final)
    return (x_p, x_s), (st_p, st_s)


def kernel(x_prompt, x_sample, state_ssd_conv, state_ssd, state_rwkv_shift, state_rwkv, state_s5_re, state_s5_im, state_pool, norm_ffn1, ffn1_in, ffn1_out, norm_mix, w_in, ssd_conv_w, ssd_conv_b, ssd_dt_bias, ssd_a_log, ssd_d, ssd_norm, rwkv_mu, rwkv_w0, rwkv_w2, rwkv_a0, rwkv_a2, rwkv_g2, rwkv_k_k, rwkv_k_a, rwkv_r_k, rwkv_ln_g, rwkv_ln_b, s5_lam_re, s5_lam_im, s5_log_step, s5_b_re, s5_b_im, s5_c_re, s5_c_im, s5_d, s5_glu_w, s5_glu_b, pool_w, pool_scale, w_out, norm_ffn2, ffn2_in, ffn2_out, norm_final):
    P = dict(norm_ffn1=norm_ffn1, ffn1_in=ffn1_in, ffn1_out=ffn1_out, norm_mix=norm_mix, w_in=w_in,
             ssd_conv_w=ssd_conv_w, ssd_conv_b=ssd_conv_b, ssd_dt_bias=ssd_dt_bias, ssd_a_log=ssd_a_log,
             ssd_d=ssd_d, ssd_norm=ssd_norm, rwkv_mu=rwkv_mu, rwkv_w0=rwkv_w0, rwkv_w2=rwkv_w2, rwkv_a0=rwkv_a0,
             rwkv_a2=rwkv_a2, rwkv_g2=rwkv_g2, rwkv_k_k=rwkv_k_k, rwkv_k_a=rwkv_k_a,
             rwkv_r_k=rwkv_r_k.reshape(rwkv_r_k.shape[0], -1), rwkv_ln_g=rwkv_ln_g, rwkv_ln_b=rwkv_ln_b,
             s5_lam_re=s5_lam_re, s5_lam_im=s5_lam_im, s5_log_step=s5_log_step, s5_b_re=s5_b_re, s5_b_im=s5_b_im,
             s5_c_re=s5_c_re, s5_c_im=s5_c_im, s5_d=s5_d, s5_glu_w=s5_glu_w, s5_glu_b=s5_glu_b, pool_w=pool_w,
             pool_scale=pool_scale, w_out=w_out, norm_ffn2=norm_ffn2, ffn2_in=ffn2_in, ffn2_out=ffn2_out)
    depth = norm_ffn1.shape[0]
    bp, tp, d = x_prompt.shape
    bs, ts, _ = x_sample.shape
    stacked = _stacked_params(P)
    layer_params = [_layer_params(stacked, l) for l in range(depth)]
    gf = norm_final.reshape(1, -1)
    sample_states = (state_ssd_conv, state_ssd, state_rwkv_shift, state_rwkv, state_s5_re, state_s5_im, state_pool)
    rwkv_rows = RWKV_HEADS * RWKV_HEAD
    decode_states = (jnp.swapaxes(state_ssd_conv, 1, 2), state_ssd, state_rwkv_shift,
                     jnp.transpose(state_rwkv, (0, 2, 3, 4, 1)).reshape(depth, rwkv_rows, RWKV_HEAD, bs),
                     state_s5_re, state_s5_im, jnp.swapaxes(state_pool, 1, 2))

    x_s = jnp.swapaxes(x_sample, 0, 1).reshape(ts * bs, d)
    (y_p, y_s), (st_p, st_s) = _trunk(
        x_prompt.reshape(bp * tp, d), x_s, layer_params, gf,
        lambda l, lp, proj, done: _mixers_prompt(lp, proj, batch=bp, seq=tp),
        lambda l, lp, proj, done: _mixers_decode(lp, proj, decode_states, done, batch=bs, seq=ts, layer=l))
    outs = [y_p.reshape(bp, tp, d), jnp.swapaxes(y_s.reshape(ts, bs, d), 0, 1)]
    for i, ref_state in enumerate(sample_states):
        outs.append(jnp.stack([st[i] for st in st_p]))
        if i == 1:
            outs.append(st_s[-1][i].reshape(ref_state.shape))
        elif i == 3:
            s_new = st_s[-1][i].reshape(depth, RWKV_HEADS, RWKV_HEAD, RWKV_HEAD, bs)
            outs.append(jnp.transpose(s_new, (0, 4, 1, 2, 3)))
        else:
            outs.append(jnp.stack([st[i] for st in st_s]))
    return tuple(outs)
```

```python
import functools
import math

import jax
import jax.numpy as jnp
from jax import lax
from jax.experimental import pallas as pl
from jax.experimental.pallas import tpu as pltpu

F32 = jnp.float32
BF16 = jnp.bfloat16
HIGHEST = lax.Precision.HIGHEST

SUBLANES = 8
LANES = 128
VMEM_LIMIT_BYTES = 56 * 1024 * 1024

GROUP_WIDTH = 256
SSD_HEAD_DIM = 64
SSD_HEADS = 4
SSD_GROUPS = 2
SSD_STATE = 128
SSD_CONV = 4
SSD_CONV_DIM = GROUP_WIDTH + 2 * SSD_GROUPS * SSD_STATE
SSD_CHUNK = 128
SSD_GROUP = 4
LOG2_E = math.log2(math.e)
RWKV_HEAD = 64
RWKV_HEADS = 4
RWKV_PROJ = 1024
RWKV_LN_EPS = 64e-5
RWKV_CHUNK = 64
RWKV_GROUP = 8
S5_GROUP_CH = 16
S5_GROUPS = 16
S5_STATE = 64
S5_WIDTH = S5_GROUPS * S5_STATE
POOL_WINDOWS = (2, 4, 8, 16)
POOL_CH = 64
POOL_BUF = 15
RMS_EPS = 1e-6
PAST_LEN = 16384

ROW_TILE = 512
FFN_CHUNK = 256
TM_CHUNK = 64
S5_SUB = 16
SSD_STEP_TILES = 16
RWKV_STEP_TILES = 16


def _cparams(*sem):
    return pltpu.CompilerParams(dimension_semantics=sem, vmem_limit_bytes=VMEM_LIMIT_BYTES)


def _dot(a, b, **kw):
    return jnp.dot(a, b, preferred_element_type=F32, **kw)


def _dot_nt(a, b):
    return lax.dot_general(a, b, (((1,), (1,)), ((), ())), preferred_element_type=F32)


def _dot_tn(a, b):
    return lax.dot_general(a, b, (((0,), (0,)), ((), ())), preferred_element_type=F32)


def _sigmoid(x):
    return 1.0 / (1.0 + jnp.exp(-x))


def _silu(x):
    return x * _sigmoid(x)


def _softplus(x):
    return jnp.maximum(x, 0.0) + jnp.log(1.0 + jnp.exp(-jnp.abs(x)))


def _gelu_tanh(x):
    c = math.sqrt(2.0 / math.pi)
    return x * (0.5 * (1.0 + jnp.tanh(c * (x + 0.044715 * (x * x * x)))))


def _rms(x, g):
    return x * lax.rsqrt(jnp.mean(x * x, axis=-1, keepdims=True) + RMS_EPS) * g


def _full_spec(shape):
    n = len(shape)
    return pl.BlockSpec(shape, lambda *_: (0,) * n)


class _Layered(tuple):
    pass


def _pspec(p, single=False):
    mode = pl.Buffered(1) if single else None
    if isinstance(p, _Layered):
        a, l = p
        return pl.BlockSpec((None,) + a.shape[1:], lambda *_: (l,) + (0,) * (a.ndim - 1), pipeline_mode=mode)
    n = p.ndim
    return pl.BlockSpec(p.shape, lambda *_: (0,) * n, pipeline_mode=mode)


def _parg(p):
    return p[0] if isinstance(p, _Layered) else p


def _mix_residual(x, y_refs, wmix_ref):
    for j, y_ref in enumerate(y_refs):
        x = x + _dot(y_ref[...].astype(BF16), wmix_ref[j * GROUP_WIDTH:(j + 1) * GROUP_WIDTH, :])
    return x


def _swiglu_chunk(h, wg, wu, wo):
    act = (_silu(_dot(h, wg)) * _dot(h, wu)).astype(BF16)
    return _dot(act, wo)


def _ffn_body(*refs, has_mix, final_norm):
    it = iter(refs)
    x = next(it)[...]
    if has_mix:
        y_refs = [next(it) for _ in range(4)]
        x = _mix_residual(x, y_refs, next(it))
    g_ref, wg_ref, wu_ref, wo_ref = next(it), next(it), next(it), next(it)
    gf_ref = next(it) if final_norm else None
    o_ref = next(it)
    h = _rms(x, g_ref[...]).astype(BF16)
    acc = jnp.zeros_like(x)
    for c in range(wo_ref.shape[0] // FFN_CHUNK):
        cols = slice(c * FFN_CHUNK, (c + 1) * FFN_CHUNK)
        acc = acc + _swiglu_chunk(h, wg_ref[:, cols], wu_ref[:, cols], wo_ref[cols, :])
    x = x + 0.5 * acc
    if final_norm:
        x = _rms(x, gf_ref[...])
    o_ref[...] = x


def _ffn(x, g, wg, wu, wo, mix=None, wmix=None, gf=None):
    rows, d = x.shape
    row_spec = lambda w: pl.BlockSpec((ROW_TILE, w), lambda i: (i, 0))
    args, specs = [x], [row_spec(d)]
    if mix is not None:
        for y in mix:
            args.append(y)
            specs.append(row_spec(y.shape[1]))
        args.append(_parg(wmix))
        specs.append(_pspec(wmix, single=True))
    for a in (g, wg, wu, wo) + ((gf,) if gf is not None else ()):
        args.append(_parg(a))
        specs.append(_pspec(a, single=True))
    return pl.pallas_call(
        functools.partial(_ffn_body, has_mix=mix is not None, final_norm=gf is not None),
        grid=(rows // ROW_TILE,),
        in_specs=specs,
        out_specs=row_spec(d),
        out_shape=jax.ShapeDtypeStruct((rows, d), F32),
        compiler_params=_cparams("parallel"),
        name="ffn",
    )(*args)


def _ffn_cast_body(*refs, has_mix, final_norm):
    it = iter(refs)
    x_ref = next(it)
    if has_mix:
        y_refs = [next(it) for _ in range(4)]
        wmix_ref = next(it)
    g_ref, wg_ref, wu_ref, wo_ref = next(it), next(it), next(it), next(it)
    gf_ref = next(it) if final_norm else None
    o_ref, wg_out, wu_out, wo_out, x_scr, h_scr, acc_scr = (next(it) for _ in range(7))
    c = pl.program_id(0)

    @pl.when(c == 0)
    def _():
        x = x_ref[...]
        if has_mix:
            x = _mix_residual(x, y_refs, wmix_ref)
        x_scr[...] = x
        h_scr[...] = _rms(x, g_ref[...]).astype(BF16)
        acc_scr[...] = jnp.zeros(acc_scr.shape, F32)

    wg = wg_ref[...].astype(BF16)
    wu = wu_ref[...].astype(BF16)
    wo = wo_ref[...].astype(BF16)
    wg_out[...] = wg
    wu_out[...] = wu
    wo_out[...] = wo
    acc_scr[...] += _swiglu_chunk(h_scr[...], wg, wu, wo)

    @pl.when(c == pl.num_programs(0) - 1)
    def _():
        x = x_scr[...] + 0.5 * acc_scr[...]
        if final_norm:
            x = _rms(x, gf_ref[...])
        o_ref[...] = x


def _ffn_cast(x, g, wi, wo, mix=None, wmix=None, gf=None):
    rows, d = x.shape
    wi_all, l = wi
    wo_all, _ = wo
    d_ff = wo_all.shape[1]
    nchunks = d_ff // FFN_CHUNK
    args, specs = [x], [_full_spec(x.shape)]
    if mix is not None:
        for y in mix:
            args.append(y)
            specs.append(_full_spec(y.shape))
        args.append(_parg(wmix))
        specs.append(_pspec(wmix, single=True))
    args += [_parg(g), wi_all, wi_all, wo_all]
    specs += [_pspec(g),
              pl.BlockSpec((None, d, FFN_CHUNK), lambda c: (l, 0, c)),
              pl.BlockSpec((None, d, FFN_CHUNK), lambda c: (l, 0, c + nchunks)),
              pl.BlockSpec((None, FFN_CHUNK, d), lambda c: (l, c, 0))]
    if gf is not None:
        args.append(gf)
        specs.append(_full_spec(gf.shape))
    col_spec = pl.BlockSpec((d, FFN_CHUNK), lambda c: (0, c))
    return pl.pallas_call(
        functools.partial(_ffn_cast_body, has_mix=mix is not None, final_norm=gf is not None),
        grid=(nchunks,),
        in_specs=specs,
        out_specs=[_full_spec(x.shape), col_spec, col_spec, pl.BlockSpec((FFN_CHUNK, d), lambda c: (c, 0))],
        out_shape=[jax.ShapeDtypeStruct((rows, d), F32), jax.ShapeDtypeStruct((d, d_ff), BF16),
                   jax.ShapeDtypeStruct((d, d_ff), BF16), jax.ShapeDtypeStruct((d_ff, d), BF16)],
        scratch_shapes=[pltpu.VMEM((rows, d), F32), pltpu.VMEM((rows, d), BF16), pltpu.VMEM((rows, d), F32)],
        compiler_params=_cparams("arbitrary"),
        name="ffn_cast",
    )(*args)


def _inproj_body(x_ref, g_ref, w_ref, *o_refs):
    h = _rms(x_ref[...], g_ref[...]).astype(BF16)
    off = 0
    for o_ref in o_refs:
        n = o_ref.shape[-1]
        o_ref[...] = _dot(h, w_ref[:, off:off + n])
        off += n


def _inproj_cast_body(x_ref, g_ref, win_ref, *o_refs):
    *proj_refs, wall_ref = o_refs
    split = GROUP_WIDTH + SSD_CONV_DIM
    w = win_ref[...]
    pad = jnp.zeros((w.shape[0], LANES - SSD_HEADS), F32)
    wall_ref[...] = jnp.concatenate([w[:, :split], w[:, split + SSD_HEADS:], w[:, split:split + SSD_HEADS], pad],
                                    axis=1).astype(BF16)
    _inproj_body(x_ref, g_ref, wall_ref, *proj_refs)


def _inproj_cast(x, g, w_in, widths):
    rows, d = x.shape
    w_all, l = w_in
    outs = pl.pallas_call(
        _inproj_cast_body,
        grid=(1,),
        in_specs=[_full_spec(x.shape), _pspec(g),
                  pl.BlockSpec((None,) + w_all.shape[1:], lambda i: (l, 0, 0), pipeline_mode=pl.Buffered(1))],
        out_specs=[_full_spec((rows, n)) for n in widths] + [_full_spec((d, sum(widths)))],
        out_shape=[jax.ShapeDtypeStruct((rows, n), F32) for n in widths]
                  + [jax.ShapeDtypeStruct((d, sum(widths)), BF16)],
        compiler_params=_cparams("arbitrary"),
        name="inproj_cast",
    )(x, _parg(g), w_all)
    return outs[:-1], outs[-1]


def _inproj(x, g, w, widths):
    rows, d = x.shape
    row_spec = lambda w_: pl.BlockSpec((ROW_TILE, w_), lambda i: (i, 0))
    return pl.pallas_call(
        _inproj_body,
        grid=(rows // ROW_TILE,),
        in_specs=[row_spec(d), _pspec(g), _pspec(w, single=True)],
        out_specs=[row_spec(n) for n in widths],
        out_shape=[jax.ShapeDtypeStruct((rows, n), F32) for n in widths],
        compiler_params=_cparams("parallel"),
        name="inproj",
    )(x, _parg(g), _parg(w))


def _ssd_body(z_ref, xbc_ref, dt_ref, cw_ref, cb_ref, dtb_ref, alog_ref, dsk_ref, ng_ref,
              y_ref, conv_ref, hout_ref, xpad_scr, h_scr, *, chunk, group):
    L, G = chunk, group
    GL = G * L
    c = pl.program_id(1)
    pad = SUBLANES
    halo = SSD_CONV - 1
    hpg = SSD_HEADS // SSD_GROUPS
    assert hpg == 2 and hpg * SSD_HEAD_DIM == SSD_STATE

    @pl.when(c == 0)
    def _():
        xpad_scr[0:pad, :] = jnp.zeros((pad, SSD_CONV_DIM), F32)
        h_scr[...] = jnp.zeros(h_scr.shape, F32)

    xpad_scr[pad:pad + GL, :] = xbc_ref[...]
    xfull = xpad_scr[...]
    conv = cb_ref[...] + cw_ref[halo:halo + 1, :] * xfull[pad:pad + GL]
    for j in range(halo):
        conv = conv + cw_ref[j:j + 1, :] * pltpu.roll(xfull, halo - j, axis=0)[pad:pad + GL]
    xpad_scr[pad - halo:pad, :] = xpad_scr[pad + GL - halo:pad + GL, :]
    conv = _silu(conv)
    xs = conv[:, 0:GROUP_WIDTH]
    bm = conv[:, GROUP_WIDTH:2 * GROUP_WIDTH].astype(BF16)
    cm = conv[:, 2 * GROUP_WIDTH:3 * GROUP_WIDTH].astype(BF16)

    row = lax.broadcasted_iota(jnp.int32, (L, L), 0)
    col = lax.broadcasted_iota(jnp.int32, (L, L), 1)
    causal = row >= col
    tril = jnp.where(causal, 1.0, 0.0).astype(F32)
    dt = _softplus(dt_ref[...] + dtb_ref[...])
    da = dt * (-jnp.exp(alog_ref[...]) * LOG2_E)
    acs = [_dot(tril, da[i * L:(i + 1) * L, :], precision=HIGHEST) for i in range(G)]
    acs_t = [a.T for a in acs]
    e_acs = [jnp.exp2(a) for a in acs]
    e_end = [jnp.exp2(a[L - 1:L, :] - a) for a in acs]
    e_last = [jnp.exp2(a[L - 1:L, :]) for a in acs]

    keys = [(i, g) for i in range(G) for g in range(SSD_GROUPS)]
    rows_of = lambda x, i: x[i * L:(i + 1) * L]
    lanes_of = lambda x, g: x[:, g * SSD_STATE:(g + 1) * SSD_STATE]
    lane_lo = lax.broadcasted_iota(jnp.int32, (L, hpg * SSD_HEAD_DIM), 1) < SSD_HEAD_DIM
    row_lo = lax.broadcasted_iota(jnp.int32, (hpg * SSD_HEAD_DIM, SSD_STATE), 0) < SSD_HEAD_DIM
    head_cols = lambda a, g: jnp.where(lane_lo, a[:, g * hpg:g * hpg + 1], a[:, g * hpg + 1:g * hpg + 2])
    bg = {(i, g): lanes_of(rows_of(bm, i), g) for i, g in keys}
    cg = {(i, g): lanes_of(rows_of(cm, i), g) for i, g in keys}
    scores = {k: _dot_nt(cg[k], bg[k]) for k in keys}
    xdt = {(i, g): lanes_of(rows_of(xs, i), g) * head_cols(rows_of(dt, i), g) for i, g in keys}
    decay = {(i, h): jnp.exp2(jnp.where(causal, acs[i][:, h:h + 1] - acs_t[i][h:h + 1, :], -jnp.inf))
             for i in range(G) for h in range(SSD_HEADS)}
    p_mat = {(i, g): jnp.concatenate([(scores[(i, g)] * decay[(i, g * hpg + k)]).astype(BF16) for k in range(hpg)],
                                     axis=1) for i, g in keys}
    y_in = {k: _dot(p_mat[k], _bd(xdt[k].astype(BF16))) for k in keys}
    st = {(i, g): _dot_tn((xdt[(i, g)] * head_cols(e_end[i], g)).astype(BF16), bg[(i, g)]) for i, g in keys}

    y_rows = []
    for i in range(G):
        ys = []
        for g in range(SSD_GROUPS):
            h_prev = h_scr[g * hpg:(g + 1) * hpg].reshape(hpg * SSD_HEAD_DIM, SSD_STATE)
            ys.append(y_in[(i, g)] + _dot_nt(cg[(i, g)], h_prev.astype(BF16)) * head_cols(e_acs[i], g))
            keep = jnp.where(row_lo, e_last[i][:, g * hpg:g * hpg + 1], e_last[i][:, g * hpg + 1:g * hpg + 2])
            h_scr[g * hpg:(g + 1) * hpg] = (h_prev * keep + st[(i, g)]).reshape(hpg, SSD_HEAD_DIM, SSD_STATE)
        y_rows.append(jnp.concatenate(ys, axis=-1))
    y = jnp.concatenate(y_rows, axis=0) + xs * dsk_ref[...]
    y = y * _silu(z_ref[...])
    y_ref[...] = _rms(y, ng_ref[...])

    @pl.when(c == pl.num_programs(1) - 1)
    def _():
        hout_ref[0] = h_scr[...]
        conv_ref[0] = xpad_scr[pad - halo:pad, :]


def _ssd(z, xbc, dtr, lp, *, batch, seq):
    chunk = SSD_CHUNK
    rows = chunk * SSD_GROUP
    nc = seq // rows
    rspec = lambda w: pl.BlockSpec((rows, w), lambda b, c: (b * nc + c, 0))
    consts = (lp["conv_w"], lp["conv_b"], lp["dt_bias"], lp["a_log"], lp["d_skip"], lp["ssd_norm"])
    return pl.pallas_call(
        functools.partial(_ssd_body, chunk=chunk, group=SSD_GROUP),
        grid=(batch, nc),
        in_specs=[rspec(GROUP_WIDTH), rspec(SSD_CONV_DIM), rspec(LANES)] + [_pspec(a) for a in consts],
        out_specs=[rspec(GROUP_WIDTH),
                   pl.BlockSpec((1, SSD_CONV - 1, SSD_CONV_DIM), lambda b, c: (b, 0, 0)),
                   pl.BlockSpec((1, SSD_HEADS, SSD_HEAD_DIM, SSD_STATE), lambda b, c: (b, 0, 0, 0))],
        out_shape=[jax.ShapeDtypeStruct((batch * seq, GROUP_WIDTH), F32),
                   jax.ShapeDtypeStruct((batch, SSD_CONV - 1, SSD_CONV_DIM), F32),
                   jax.ShapeDtypeStruct((batch, SSD_HEADS, SSD_HEAD_DIM, SSD_STATE), F32)],
        scratch_shapes=[pltpu.VMEM((SUBLANES + rows, SSD_CONV_DIM), F32),
                        pltpu.VMEM((SSD_HEADS, SSD_HEAD_DIM, SSD_STATE), F32)],
        compiler_params=_cparams("parallel", "arbitrary"),
        name="ssd",
    )(z, xbc, dtr, *[_parg(a) for a in consts])


def _ssd_step_body(z_ref, xbc_ref, dt_ref, conv0_ref, h0_ref, *rest, seq, batch, layer):
    hdone_ref, rest = (rest[0], rest[1:]) if layer else (None, rest)
    (cw_ref, cb_ref, dtb_ref, aneg_ref, dsk_ref, ng_ref, hexp_ref, y_ref, conv_ref, hout_ref,
     xs_scr, bm_scr, cm_scr, xdt_scr, dec_scr, y_scr) = rest
    T, B = seq, batch
    if layer:
        hout_ref[0:layer] = hdone_ref[...]
    GW = GROUP_WIDTH
    j = pl.program_id(0)
    tiles = SSD_STEP_TILES

    @pl.when(j == 0)
    def _():
        rows = [conv0_ref[i] for i in range(SSD_CONV - 1)]
        rows += [xbc_ref[t * B:(t + 1) * B, :] for t in range(T)]
        for t in range(T):
            conv = cb_ref[...] + cw_ref[0:1, :] * rows[t]
            for i in range(1, SSD_CONV):
                conv = conv + cw_ref[i:i + 1, :] * rows[t + i]
            conv = _silu(conv)
            xs = conv[:, 0:GW]
            xs_scr[t] = xs
            for g in range(SSD_GROUPS):
                bm_scr[t, g] = conv[:, GW + g * SSD_STATE:GW + (g + 1) * SSD_STATE].T
                cm_scr[t, g] = conv[:, 2 * GW + g * SSD_STATE:2 * GW + (g + 1) * SSD_STATE].T
            dt = _softplus(dt_ref[t * B:(t + 1) * B, :] + dtb_ref[...])
            dte = _dot(dt, hexp_ref[...], precision=HIGHEST)
            xdt_scr[t] = (xs * dte).T
            dec_scr[t] = jnp.exp(dte * aneg_ref[...]).T
        for i in range(SSD_CONV - 1):
            conv_ref[i] = rows[T + i]

    hp0 = j * tiles
    grp = hp0 // (SSD_HEAD_DIM * (SSD_HEADS // SSD_GROUPS))
    for q in range(tiles):
        hp = pl.ds(hp0 + q, 1)
        h = h0_ref[:, q, :].T
        for t in range(T):
            h = h * dec_scr[t, hp, :] + bm_scr[t, grp] * xdt_scr[t, hp, :]
            y_scr[t, hp, :] = jnp.sum(h * cm_scr[t, grp], axis=0, keepdims=True)
        hout_ref[layer, :, q, :] = h.T

    @pl.when(j == pl.num_programs(0) - 1)
    def _():
        for t in range(T):
            y = y_scr[t].T + xs_scr[t] * dsk_ref[...]
            y = y * _silu(z_ref[t * B:(t + 1) * B, :])
            y_ref[t * B:(t + 1) * B, :] = _rms(y, ng_ref[...])


def _layer_state_specs(layer, block, axis):
    idx = lambda first: (lambda j: (first,) + tuple(j if a == axis else 0 for a in range(len(block))))
    cur = pl.BlockSpec((None,) + block, idx(layer))
    prev = [pl.BlockSpec((layer,) + block, idx(0))] if layer else []
    out = pl.BlockSpec((layer + 1,) + block, idx(0))
    return cur, prev, out


def _ssd_step(z, xbc, dtr, conv_all, h_all, h_done, lp, *, batch, seq, layer):
    n = batch * seq
    srows = SSD_HEADS * SSD_HEAD_DIM
    consts = (lp["conv_w"], lp["conv_b"], lp["dt_bias"], lp["a_neg_exp"], lp["d_skip"], lp["ssd_norm"], lp["head_expand"])
    hspec, prev_specs, hout_spec = _layer_state_specs(layer, (batch, SSD_STEP_TILES, SSD_STATE), 1)
    prev_args = [h_done] if layer else []
    cshape = (SSD_CONV - 1, batch, SSD_CONV_DIM)
    return pl.pallas_call(
        functools.partial(_ssd_step_body, seq=seq, batch=batch, layer=layer),
        grid=(srows // SSD_STEP_TILES,),
        in_specs=[_full_spec((n, GROUP_WIDTH)), _full_spec((n, SSD_CONV_DIM)), _full_spec((n, LANES)),
                  pl.BlockSpec((None,) + cshape, lambda j: (layer, 0, 0, 0)), hspec] + prev_specs
                 + [_pspec(a) for a in consts],
        out_specs=[_full_spec((n, GROUP_WIDTH)), _full_spec(cshape), hout_spec],
        out_shape=[jax.ShapeDtypeStruct((n, GROUP_WIDTH), F32),
                   jax.ShapeDtypeStruct(cshape, F32),
                   jax.ShapeDtypeStruct((layer + 1, batch, srows, SSD_STATE), F32)],
        scratch_shapes=[pltpu.VMEM((seq, batch, GROUP_WIDTH), F32),
                        pltpu.VMEM((seq, SSD_GROUPS, SSD_STATE, batch), F32),
                        pltpu.VMEM((seq, SSD_GROUPS, SSD_STATE, batch), F32),
                        pltpu.VMEM((seq, GROUP_WIDTH, batch), F32),
                        pltpu.VMEM((seq, GROUP_WIDTH, batch), F32),
                        pltpu.VMEM((seq, GROUP_WIDTH, batch), F32)],
        compiler_params=_cparams("arbitrary"),
        name="ssd_step",
    )(z, xbc, dtr, conv_all, h_all.reshape(h_all.shape[0], batch, srows, SSD_STATE),
      *prev_args, *[_parg(a) for a in consts])


PAIR = 2 * RWKV_HEAD
RWKV_PAIRS = RWKV_HEADS // 2


def _bd(x):
    half = x.shape[1] // 2
    lane = lax.broadcasted_iota(jnp.int32, x.shape, 1)
    zero = jnp.zeros_like(x)
    return jnp.concatenate([jnp.where(lane < half, x, zero), jnp.where(lane >= half, x, zero)], axis=0)


def _half_sums(x, lo):
    s_lo = jnp.sum(jnp.where(lo, x, 0.0), axis=-1, keepdims=True)
    s_hi = jnp.sum(jnp.where(lo, 0.0, x), axis=-1, keepdims=True)
    return jnp.where(lo, s_lo, s_hi)


def _head_sum(x):
    lo = lax.broadcasted_iota(jnp.int32, (x.shape[0], PAIR), 1) < RWKV_HEAD
    return jnp.concatenate([_half_sums(x[:, p * PAIR:(p + 1) * PAIR], lo) for p in range(RWKV_PAIRS)], axis=-1)


def _rwkv_pointwise(u, prev, mu_ref, w0_ref, w2_ref, a0_ref, a2_ref, g2_ref, kk_ref, ka_ref):
    GW = GROUP_WIDTH
    xs = u + (prev - u) * mu_ref[...]
    r = xs[:, 0:GW]
    k = xs[:, GW:2 * GW]
    v = xs[:, 2 * GW:3 * GW]
    wd = xs[:, 3 * GW:3 * GW + 64]
    ad = xs[:, 3 * GW + 64:3 * GW + 128]
    gd = xs[:, 3 * GW + 128:3 * GW + 256]
    w_lin = w0_ref[...] + _dot(jnp.tanh(wd).astype(BF16), w2_ref[...])
    logdecay = -jnp.exp(-_softplus(-w_lin) - 0.5)
    a = _sigmoid(a0_ref[...] + _dot(ad.astype(BF16), a2_ref[...]))
    g = _dot(_sigmoid(gd).astype(BF16), g2_ref[...])
    kk = k * kk_ref[...]
    kk = kk / jnp.maximum(jnp.sqrt(_head_sum(kk * kk)), 1e-12)
    k = k * (1.0 + (a - 1.0) * ka_ref[...])
    return r, k, v, logdecay, a, g, kk


def _rwkv_finish(y, r, k, v, g, rk_ref, lng_ref, lnb_ref):
    mean = _head_sum(y) * (1.0 / RWKV_HEAD)
    yc = y - mean
    var = _head_sum(yc * yc) * (1.0 / RWKV_HEAD)
    y = yc * lax.rsqrt(var + RWKV_LN_EPS) * lng_ref[...] + lnb_ref[...]
    bonus = _head_sum(r * k * rk_ref[...]) * v
    return (y + bonus) * g


def _rwkv_body(u_ref, mu_ref, w0_ref, w2_ref, a0_ref, a2_ref, g2_ref, kk_ref, ka_ref, rk_ref,
               lng_ref, lnb_ref, y_ref, shift_ref, sout_ref, upad_scr, s_scr, *, chunk, group):
    L, G = chunk, group
    GL = G * L
    c = pl.program_id(1)
    pad = SUBLANES

    @pl.when(c == 0)
    def _():
        upad_scr[0:pad, :] = jnp.zeros((pad, RWKV_PROJ), F32)
        s_scr[...] = jnp.zeros(s_scr.shape, F32)

    u = u_ref[...]
    upad_scr[pad:pad + GL, :] = u
    prev = pltpu.roll(upad_scr[...], 1, axis=0)[pad:pad + GL]
    upad_scr[pad - 1:pad, :] = u[GL - 1:GL, :]
    r, k, v, logdecay, a, g, kk = _rwkv_pointwise(u, prev, mu_ref, w0_ref, w2_ref, a0_ref, a2_ref, g2_ref,
                                                  kk_ref, ka_ref)

    tril = jnp.where(lax.broadcasted_iota(jnp.int32, (L, L), 0) >= lax.broadcasted_iota(jnp.int32, (L, L), 1),
                     1.0, 0.0).astype(F32)
    cl = jnp.concatenate([_dot(tril, logdecay[i * L:(i + 1) * L, :], precision=HIGHEST) for i in range(G)], axis=0)
    e_in = jnp.exp(cl)
    e_inv = jnp.exp(-cl)
    r_t = r * e_in
    r_tb = r_t.astype(BF16)
    a_tb = (-kk * jnp.exp(cl - logdecay)).astype(BF16)
    b_tb = (kk * a * e_inv).astype(BF16)
    k_tb = (k * e_inv).astype(BF16)
    vb = v.astype(BF16)

    row = lax.broadcasted_iota(jnp.int32, (L, PAIR), 0)
    colh = lax.broadcasted_iota(jnp.int32, (L, PAIR), 1) & (RWKV_HEAD - 1)
    strict = row > colh
    incl = row >= colh
    eye_pair = jnp.where(row == colh, 1.0, 0.0).astype(F32)
    lane_lo = lax.broadcasted_iota(jnp.int32, (RWKV_HEAD, PAIR), 1) < RWKV_HEAD
    same_head = (lax.broadcasted_iota(jnp.int32, (PAIR, PAIR), 0) < RWKV_HEAD) == \
                (lax.broadcasted_iota(jnp.int32, (PAIR, PAIR), 1) < RWKV_HEAD)

    streams = [(i, p) for i in range(G) for p in range(RWKV_PAIRS)]
    ns = len(streams)
    blk = lambda x, i, p: x[i * L:(i + 1) * L, p * PAIR:(p + 1) * PAIR]
    lhs = [jnp.concatenate([blk(a_tb, i, p), blk(r_tb, i, p)], axis=0) for i, p in streams]
    m_ab = [_dot_nt(lhs[s], _bd(blk(b_tb, i, p))) for s, (i, p) in enumerate(streams)]
    m_ak = [_dot_nt(lhs[s], _bd(blk(k_tb, i, p))) for s, (i, p) in enumerate(streams)]
    n_ab = [jnp.where(strict, m[0:L], 0.0) for m in m_ab]
    m_rb = [jnp.where(incl, m[L:2 * L], 0.0).astype(BF16) for m in m_ab]
    n_ak = [jnp.where(strict, m[0:L], 0.0).astype(BF16) for m in m_ak]
    m_rk = [jnp.where(incl, m[L:2 * L], 0.0).astype(BF16) for m in m_ak]
    tinv = [eye_pair + n for n in n_ab]
    pwb = [n.astype(BF16) for n in n_ab]
    pw = [_dot(x, _bd(x)) for x in pwb]
    for _ in range(int(math.log2(L)) - 2):
        pwb = [x.astype(BF16) for x in pw]
        both = [_dot(jnp.concatenate([pwb[s], tinv[s].astype(BF16)], axis=0), _bd(pwb[s])) for s in range(ns)]
        pw = [x[0:L] for x in both]
        tinv = [tinv[s] + both[s][L:2 * L] for s in range(ns)]
    pwb = [x.astype(BF16) for x in pw]
    tinv = [tinv[s] + _dot(tinv[s].astype(BF16), _bd(pwb[s])) for s in range(ns)]
    tinvb = [x.astype(BF16) for x in tinv]
    nv_mv = [_dot(jnp.concatenate([n_ak[s], m_rk[s]], axis=0), _bd(blk(vb, i, p))) for s, (i, p) in enumerate(streams)]
    wu = [_dot(tinvb[s], jnp.concatenate([_bd(blk(a_tb, i, p)), _bd(nv_mv[s][0:L].astype(BF16))], axis=1))
          for s, (i, p) in enumerate(streams)]
    wub = [x.astype(BF16) for x in wu]
    qy = [_dot(m_rb[s], jnp.concatenate([_bd(wub[s][:, 0:PAIR]), _bd(wub[s][:, PAIR:2 * PAIR])], axis=1))
          for s in range(ns)]
    q = [(blk(r_t, i, p) + qy[s][:, 0:PAIR]).astype(BF16) for s, (i, p) in enumerate(streams)]
    y_loc = [qy[s][:, PAIR:2 * PAIR] + nv_mv[s][L:2 * L] for s in range(ns)]
    zeros_b = jnp.zeros((L, PAIR), BF16)
    mg = [_dot_tn(jnp.concatenate([wub[s], jnp.concatenate([zeros_b, blk(vb, i, p)], axis=1)], axis=0),
                  jnp.concatenate([blk(b_tb, i, p), blk(k_tb, i, p)], axis=0))
          for s, (i, p) in enumerate(streams)]
    p_end = [e_in[(i + 1) * L - 1:(i + 1) * L, p * PAIR:(p + 1) * PAIR] for i, p in streams]
    m_t = [(jnp.where(same_head, mg[s][0:PAIR], 0.0) * p_end[s]).astype(BF16) for s in range(ns)]
    g_t = [jnp.where(lane_lo, mg[s][PAIR:PAIR + RWKV_HEAD], mg[s][PAIR + RWKV_HEAD:2 * PAIR]) * p_end[s]
           for s in range(ns)]

    y_rows = []
    for i in range(G):
        y_pairs = []
        for p in range(RWKV_PAIRS):
            s = i * RWKV_PAIRS + p
            s0 = s_scr[p]
            s0b = s0.astype(BF16)
            y_pairs.append(_dot_nt(q[s], _bd(s0b)) + y_loc[s])
            s_scr[p] = s0 * p_end[s] + _dot(s0b, m_t[s]) + g_t[s]
        y_rows.append(jnp.concatenate(y_pairs, axis=-1))
    y = jnp.concatenate(y_rows, axis=0)
    y_ref[...] = _rwkv_finish(y, r, k, v, g, rk_ref, lng_ref, lnb_ref)

    @pl.when(c == pl.num_programs(1) - 1)
    def _():
        sout_ref[0] = s_scr[...]
        shift_ref[0] = upad_scr[pad - 1:pad, :]


_RWKV_PARAM_NAMES = ("mu", "w0", "w2", "a0", "a2", "g2", "k_k", "k_a", "r_k", "ln_g", "ln_b")


def _rwkv(u, p, *, batch, seq):
    rows = RWKV_CHUNK * RWKV_GROUP
    nc = seq // rows
    params = [p[n] for n in _RWKV_PARAM_NAMES]
    sspec = pl.BlockSpec((1, RWKV_PAIRS, RWKV_HEAD, PAIR), lambda b, c: (b, 0, 0, 0))
    y, shift, s_last = pl.pallas_call(
        functools.partial(_rwkv_body, chunk=RWKV_CHUNK, group=RWKV_GROUP),
        grid=(batch, nc),
        in_specs=[pl.BlockSpec((rows, RWKV_PROJ), lambda b, c: (b * nc + c, 0))] + [_pspec(a) for a in params],
        out_specs=[pl.BlockSpec((rows, GROUP_WIDTH), lambda b, c: (b * nc + c, 0)),
                   pl.BlockSpec((1, 1, RWKV_PROJ), lambda b, c: (b, 0, 0)), sspec],
        out_shape=[jax.ShapeDtypeStruct((batch * seq, GROUP_WIDTH), F32),
                   jax.ShapeDtypeStruct((batch, 1, RWKV_PROJ), F32),
                   jax.ShapeDtypeStruct((batch, RWKV_PAIRS, RWKV_HEAD, PAIR), F32)],
        scratch_shapes=[pltpu.VMEM((SUBLANES + rows, RWKV_PROJ), F32),
                        pltpu.VMEM((RWKV_PAIRS, RWKV_HEAD, PAIR), F32)],
        compiler_params=_cparams("parallel", "arbitrary"),
        name="rwkv",
    )(u, *[_parg(a) for a in params])
    s_last = s_last.reshape(batch, RWKV_PAIRS, RWKV_HEAD, 2, RWKV_HEAD).transpose(0, 1, 3, 2, 4).reshape(
        batch, RWKV_HEADS, RWKV_HEAD, RWKV_HEAD)
    return y, shift.reshape(batch, RWKV_PROJ), s_last


def _rwkv_step_body(u_ref, shift0_ref, s0_ref, *rest, seq, batch, layer):
    sdone_ref, rest = (rest[0], rest[1:]) if layer else (None, rest)
    (mu_ref, w0_ref, w2_ref, a0_ref, a2_ref, g2_ref, kk_ref, ka_ref, rk_ref, lng_ref, lnb_ref, y_ref, sout_ref,
     r_scr, w_scr, k_scr, b_scr, nkk_scr, v_scr, y_scr) = rest
    T, B = seq, batch
    j = pl.program_id(0)
    if layer:
        sout_ref[0:layer] = sdone_ref[...]
    tiles = RWKV_STEP_TILES

    def pointwise(t):
        u = u_ref[t * B:(t + 1) * B, :]
        prev = shift0_ref[...] if t == 0 else u_ref[(t - 1) * B:t * B, :]
        return _rwkv_pointwise(u, prev, mu_ref, w0_ref, w2_ref, a0_ref, a2_ref, g2_ref, kk_ref, ka_ref)

    @pl.when(j == 0)
    def _():
        for t in range(T):
            r, k, v, logdecay, a, _, kk = pointwise(t)
            r_scr[t] = r.T
            w_scr[t] = jnp.exp(logdecay).T
            k_scr[t] = k.T
            b_scr[t] = (kk * a).T
            nkk_scr[t] = (-kk).T
            v_scr[t] = v.T

    i0 = j * tiles
    keys = pl.ds(pl.multiple_of((i0 // RWKV_HEAD) * RWKV_HEAD, RWKV_HEAD), RWKV_HEAD)
    for q in range(tiles):
        vi = pl.ds(i0 + q, 1)
        s = s0_ref[q]
        for t in range(T):
            sa = jnp.sum(s * nkk_scr[t, keys, :], axis=0, keepdims=True)
            s = s * w_scr[t, keys, :] + k_scr[t, keys, :] * v_scr[t, vi, :] + b_scr[t, keys, :] * sa
            y_scr[t, vi, :] = jnp.sum(s * r_scr[t, keys, :], axis=0, keepdims=True)
        sout_ref[layer, q] = s

    @pl.when(j == pl.num_programs(0) - 1)
    def _():
        for t in range(T):
            r, k, v, _, _, g, _ = pointwise(t)
            y_ref[t * B:(t + 1) * B, :] = _rwkv_finish(y_scr[t].T, r, k, v, g, rk_ref, lng_ref, lnb_ref)


def _rwkv_step(u, shift0, s_all, s_done, p, *, batch, seq, layer):
    n = batch * seq
    srows = RWKV_HEADS * RWKV_HEAD
    params = [p[nm] for nm in _RWKV_PARAM_NAMES]
    sspec, prev_specs, sout_spec = _layer_state_specs(layer, (RWKV_STEP_TILES, RWKV_HEAD, batch), 0)
    prev_args = [s_done] if layer else []
    tposed = pltpu.VMEM((seq, GROUP_WIDTH, batch), F32)
    return pl.pallas_call(
        functools.partial(_rwkv_step_body, seq=seq, batch=batch, layer=layer),
        grid=(srows // RWKV_STEP_TILES,),
        in_specs=[_full_spec((n, RWKV_PROJ)), _full_spec((batch, RWKV_PROJ)), sspec] + prev_specs
                 + [_pspec(a) for a in params],
        out_specs=[_full_spec((n, GROUP_WIDTH)), sout_spec],
        out_shape=[jax.ShapeDtypeStruct((n, GROUP_WIDTH), F32),
                   jax.ShapeDtypeStruct((layer + 1, srows, RWKV_HEAD, batch), F32)],
        scratch_shapes=[tposed] * 7,
        compiler_params=_cparams("arbitrary"),
        name="rwkv_step",
    )(u, shift0, s_all, *prev_args, *[_parg(a) for a in params])


def _s5_body(u_ref, hre0_ref, him0_ref, are_ref, aim_ref, bmat_ref, cmat_ref, d_ref, gw_ref, gb_ref,
             y_ref, hre_ref, him_ref, hs_scr, tm_scr, *, steps, batch_major):
    c = pl.program_id(1)
    ns = S5_WIDTH
    bsub = SUBLANES

    @pl.when(c == 0)
    def _():
        hre_ref[...] = hre0_ref[...]
        him_ref[...] = him0_ref[...]

    if batch_major:
        for b in range(bsub):
            tm_scr[:, b, :] = u_ref[b]
        u = tm_scr[...].reshape(steps * bsub, GROUP_WIDTH)
    else:
        u = u_ref[...].reshape(steps * bsub, GROUP_WIDTH)
    are = jnp.broadcast_to(are_ref[...], (bsub, ns))
    aim = jnp.broadcast_to(aim_ref[...], (bsub, ns))
    hre, him = hre_ref[...], him_ref[...]
    sub = min(S5_SUB, steps)
    rows = sub * bsub
    outs = []
    for k in range(steps // sub):
        r0 = k * rows
        u_k = u[r0:r0 + rows]
        hs_scr[r0:r0 + rows, :] = _dot(u_k.astype(BF16), bmat_ref[...])
        for t in range(sub):
            rs = slice(r0 + t * bsub, r0 + (t + 1) * bsub)
            hre, him = (are * hre - aim * him + hs_scr[rs, 0:ns], are * him + aim * hre + hs_scr[rs, ns:2 * ns])
            hs_scr[rs, 0:ns] = hre
            hs_scr[rs, ns:2 * ns] = him
        y = _dot(hs_scr[r0:r0 + rows, :].astype(BF16), cmat_ref[...]) + u_k * d_ref[...]
        y = _gelu_tanh(y)
        yy = _dot(y.astype(BF16), gw_ref[...]) + gb_ref[...]
        outs.append(yy[:, 0:GROUP_WIDTH] * _sigmoid(yy[:, GROUP_WIDTH:2 * GROUP_WIDTH]))
    hre_ref[...] = hre
    him_ref[...] = him
    out = jnp.concatenate(outs, axis=0).reshape(steps, bsub, GROUP_WIDTH)
    if batch_major:
        tm_scr[...] = out
        for b in range(bsub):
            y_ref[b] = tm_scr[:, b, :]
    else:
        y_ref[...] = out


def _time_specs(u, batch_major):
    bsub = SUBLANES
    if batch_major:
        batch, seq, _ = u.shape
        steps = min(TM_CHUNK, seq)
        spec = pl.BlockSpec((bsub, steps, GROUP_WIDTH), lambda b, c: (b, c, 0))
    else:
        seq, batch, _ = u.shape
        steps = min(TM_CHUNK, seq)
        spec = pl.BlockSpec((steps, bsub, GROUP_WIDTH), lambda b, c: (c, b, 0))
    return batch, seq, steps, spec


def _s5(u, hre0, him0, lp, *, batch_major):
    batch, seq, steps, tspec = _time_specs(u, batch_major)
    bsub = SUBLANES
    hspec = pl.BlockSpec((bsub, S5_WIDTH), lambda b, c: (b, 0))
    consts = (lp["s5_are"], lp["s5_aim"], lp["s5_bmat"], lp["s5_cmat"], lp["s5_d"], lp["s5_gw"], lp["s5_gb"])
    return pl.pallas_call(
        functools.partial(_s5_body, steps=steps, batch_major=batch_major),
        grid=(batch // bsub, seq // steps),
        in_specs=[tspec, hspec, hspec] + [_pspec(a) for a in consts],
        out_specs=[tspec, hspec, hspec],
        out_shape=[jax.ShapeDtypeStruct(u.shape, F32),
                   jax.ShapeDtypeStruct((batch, S5_WIDTH), F32),
                   jax.ShapeDtypeStruct((batch, S5_WIDTH), F32)],
        scratch_shapes=[pltpu.VMEM((steps * bsub, 2 * S5_WIDTH), F32),
                        pltpu.VMEM((steps, bsub, GROUP_WIDTH), F32)],
        compiler_params=_cparams("parallel", "arbitrary"),
        name="s5",
    )(u, hre0, him0, *[_parg(a) for a in consts])


def _pool_body(u_ref, buf0_ref, pw_ref, sc_ref, y_ref, buf_ref, f_scr, tm_scr, *, steps, pos0, batch_major):
    c = pl.program_id(1)
    bsub = SUBLANES
    GW = GROUP_WIDTH
    halo = POOL_BUF + 1

    @pl.when(c == 0)
    def _():
        f_scr[0] = jnp.zeros((bsub, GW), F32)
        f_scr[1:halo] = buf0_ref[...]

    if batch_major:
        for b in range(bsub):
            f_scr[halo:halo + steps, b, :] = u_ref[b]
    else:
        f_scr[halo:halo + steps] = u_ref[...]
    f = f_scr[...]
    u = f[halo:halo + steps]
    s2 = f[1:] + f[:-1]
    s4 = s2[2:] + s2[:-2]
    s8 = s4[4:] + s4[:-4]
    s16 = s8[8:] + s8[:-8]
    f_scr[0:halo] = f[steps:steps + halo]
    lane = lax.broadcasted_iota(jnp.int32, (steps, bsub, GW), 2)
    tpos = lax.broadcasted_iota(jnp.int32, (steps, bsub, GW), 0) + (pos0 + 1) + c * steps
    win = jnp.where(lane < POOL_CH, s2[halo - 1:halo - 1 + steps],
                    jnp.where(lane < 2 * POOL_CH, s4[halo - 3:halo - 3 + steps],
                              jnp.where(lane < 3 * POOL_CH, s8[halo - 7:halo - 7 + steps],
                                        s16[halo - 15:halo - 15 + steps])))
    wlen = jnp.where(lane < POOL_CH, POOL_WINDOWS[0],
                     jnp.where(lane < 2 * POOL_CH, POOL_WINDOWS[1],
                               jnp.where(lane < 3 * POOL_CH, POOL_WINDOWS[2], POOL_WINDOWS[3])))
    cnt = jnp.minimum(tpos, wlen).astype(F32)
    pooled = (win / cnt - u).reshape(steps * bsub, GW)
    y = (_dot(pooled.astype(BF16), pw_ref[...]) * sc_ref[...]).reshape(steps, bsub, GW)
    if batch_major:
        tm_scr[...] = y
        for b in range(bsub):
            y_ref[b] = tm_scr[:, b, :]
    else:
        y_ref[...] = y

    @pl.when(c == pl.num_programs(1) - 1)
    def _():
        buf_ref[...] = f_scr[1:halo]


def _pool(u, buf0, lp, *, pos0, batch_major, layer=None):
    batch, seq, steps, tspec = _time_specs(u, batch_major)
    bsub = SUBLANES
    bblock = (POOL_BUF, bsub, GROUP_WIDTH)
    bspec = pl.BlockSpec(bblock, lambda b, c: (0, b, 0))
    if layer is None:
        bspec_in = bspec
    else:
        bspec_in = pl.BlockSpec((None,) + bblock, lambda b, c: (layer, 0, b, 0))
    return pl.pallas_call(
        functools.partial(_pool_body, steps=steps, pos0=pos0, batch_major=batch_major),
        grid=(batch // bsub, seq // steps),
        in_specs=[tspec, bspec_in, _pspec(lp["pool_w"]), _pspec(lp["pool_scale"])],
        out_specs=[tspec, bspec],
        out_shape=[jax.ShapeDtypeStruct(u.shape, F32), jax.ShapeDtypeStruct((POOL_BUF, batch, GROUP_WIDTH), F32)],
        scratch_shapes=[pltpu.VMEM((POOL_BUF + 1 + steps, bsub, GROUP_WIDTH), F32),
                        pltpu.VMEM((steps, bsub, GROUP_WIDTH), F32)],
        compiler_params=_cparams("parallel", "arbitrary"),
        name="pool",
    )(u, buf0, _parg(lp["pool_w"]), _parg(lp["pool_scale"]))


def _block_diag(blocks):
    n, g, r, c = blocks.shape
    eye = jnp.eye(g, dtype=blocks.dtype)
    return (eye[None, :, None, :, None] * blocks[:, :, :, None, :]).reshape(n, g * r, g * c)


def _stacked_params(P):
    row = lambda a: a.reshape(a.shape[0], 1, -1)
    pad_lanes = lambda a: jnp.pad(a, ((0, 0), (0, LANES - a.shape[1])))
    bf = lambda a: a.astype(BF16)

    lam = lax.complex(P["s5_lam_re"], P["s5_lam_im"])
    a_bar = jnp.exp(lam * jnp.exp(P["s5_log_step"])[..., None])
    b_bar = ((a_bar - 1.0) / lam)[..., None] * lax.complex(P["s5_b_re"], P["s5_b_im"])
    b_t = jnp.swapaxes(b_bar, 2, 3)
    bmat = jnp.concatenate([_block_diag(jnp.real(b_t)), _block_diag(jnp.imag(b_t))], axis=2)
    c_t = jnp.swapaxes(lax.complex(P["s5_c_re"], P["s5_c_im"]), 2, 3)
    cmat = jnp.concatenate([_block_diag(jnp.real(c_t)), -_block_diag(jnp.imag(c_t))], axis=1)

    out = dict(
        norm_ffn1=row(P["norm_ffn1"]), ffn1_in=P["ffn1_in"], ffn1_out=P["ffn1_out"],
        norm_mix=row(P["norm_mix"]), w_in=P["w_in"],
        conv_w=P["ssd_conv_w"], conv_b=row(P["ssd_conv_b"]),
        dt_bias=row(pad_lanes(P["ssd_dt_bias"])), a_log=row(pad_lanes(P["ssd_a_log"])),
        a_neg_exp=row(jnp.repeat(-jnp.exp(P["ssd_a_log"]), SSD_HEAD_DIM, axis=1)),
        d_skip=row(jnp.repeat(P["ssd_d"], SSD_HEAD_DIM, axis=1)), ssd_norm=row(P["ssd_norm"]),
        s5_are=row(jnp.real(a_bar)), s5_aim=row(jnp.imag(a_bar)), s5_bmat=bf(bmat), s5_cmat=bf(cmat),
        s5_d=row(P["s5_d"]), s5_gw=bf(P["s5_glu_w"]), s5_gb=row(P["s5_glu_b"]),
        pool_w=bf(_block_diag(P["pool_w"])), pool_scale=row(P["pool_scale"]),
        w_out=bf(P["w_out"]),
        norm_ffn2=row(P["norm_ffn2"]), ffn2_in=P["ffn2_in"], ffn2_out=P["ffn2_out"],
    )
    for name in _RWKV_PARAM_NAMES:
        a = P["rwkv_" + name]
        out["rwkv_" + name] = bf(a) if name in ("w2", "a2", "g2") else row(a)
    return out


def _layer_params(stacked, l):
    lp = {k: _Layered((v, l)) for k, v in stacked.items()}
    lp["rwkv"] = {n: lp["rwkv_" + n] for n in _RWKV_PARAM_NAMES}
    lp["head_expand"] = jnp.pad(jnp.repeat(jnp.eye(SSD_HEADS, dtype=F32), SSD_HEAD_DIM, axis=1),
                                ((0, LANES - SSD_HEADS), (0, 0)))
    return lp


def _mixers_prompt(lp, proj, *, batch, seq):
    z, xbc, ur, us5, upool, dtr = proj
    y_ssd, conv_new, ssd_new = _ssd(z, xbc, dtr, lp, batch=batch, seq=seq)
    y_rwkv, shift_new, rwkv_new = _rwkv(ur, lp["rwkv"], batch=batch, seq=seq)
    zeros = jnp.zeros((batch, S5_WIDTH), F32)
    bm = lambda a: a.reshape(batch, seq, a.shape[-1])
    rows = lambda a: a.reshape(batch * seq, a.shape[-1])
    y_s5, s5re, s5im = _s5(bm(us5), zeros, zeros, lp, batch_major=True)
    y_pool, pool_new = _pool(bm(upool), jnp.zeros((POOL_BUF, batch, GROUP_WIDTH), F32), lp, pos0=0,
                             batch_major=True)
    ys = (y_ssd, y_rwkv, rows(y_s5), rows(y_pool))
    states = (conv_new, ssd_new, shift_new, rwkv_new, s5re.reshape(batch, S5_GROUPS, S5_STATE),
              s5im.reshape(batch, S5_GROUPS, S5_STATE), jnp.swapaxes(pool_new, 0, 1))
    return ys, states


def _mixers_decode(lp, proj, states, done, *, batch, seq, layer):
    z, xbc, ur, us5, upool, dtr = proj
    shift0, s5re0, s5im0 = (states[i][layer] for i in (2, 4, 5))
    ssd_done, rwkv_done = (done[1], done[3]) if layer else (None, None)
    y_ssd, conv_new, ssd_new = _ssd_step(z, xbc, dtr, states[0], states[1], ssd_done, lp, batch=batch, seq=seq,
                                         layer=layer)
    y_rwkv, rwkv_new = _rwkv_step(ur, shift0, states[3], rwkv_done, lp["rwkv"], batch=batch, seq=seq, layer=layer)
    shift_new = ur[(seq - 1) * batch:, :]
    tm = lambda a: a.reshape(seq, batch, a.shape[-1])
    y_s5, s5re, s5im = _s5(tm(us5), s5re0.reshape(batch, S5_WIDTH), s5im0.reshape(batch, S5_WIDTH), lp,
                           batch_major=False)
    y_pool, pool_new = _pool(tm(upool), states[6], lp, pos0=PAST_LEN, batch_major=False, layer=layer)
    rows = lambda a: a.reshape(seq * batch, a.shape[-1])
    ys = (y_ssd, y_rwkv, rows(y_s5), rows(y_pool))
    new_states = (jnp.swapaxes(conv_new, 0, 1), ssd_new, shift_new, rwkv_new,
                  s5re.reshape(batch, S5_GROUPS, S5_STATE), s5im.reshape(batch, S5_GROUPS, S5_STATE),
                  jnp.swapaxes(pool_new, 0, 1))
    return ys, new_states


_WIDTHS = (GROUP_WIDTH, SSD_CONV_DIM, RWKV_PROJ, GROUP_WIDTH, GROUP_WIDTH, LANES)


def _trunk(x_p, x_s, layer_params, norm_final, mixers_p, mixers_s):
    st_p, st_s = [], []
    mix_p, mix_s, lp = None, None, None
    for l, lp_next in enumerate(layer_params):
        if l > 0:
            x_s, wg, wu, wo = _ffn_cast(x_s, lp["norm_ffn2"], lp["ffn2_in"], lp["ffn2_out"], mix=mix_s, wmix=lp["w_out"])
            x_p = _ffn(x_p, lp["norm_ffn2"], wg, wu, wo, mix=mix_p, wmix=lp["w_out"])
        lp = lp_next
        x_s, wg, wu, wo = _ffn_cast(x_s, lp["norm_ffn1"], lp["ffn1_in"], lp["ffn1_out"])
        x_p = _ffn(x_p, lp["norm_ffn1"], wg, wu, wo)
        proj_s, w_all = _inproj_cast(x_s, lp["norm_mix"], lp["w_in"], _WIDTHS)
        mix_p, st = mixers_p(l, lp, _inproj(x_p, lp["norm_mix"], w_all, _WIDTHS), st_p[-1] if st_p else None)
        st_p.append(st)
        mix_s, st = mixers_s(l, lp, proj_s, st_s[-1] if st_s else None)
        st_s.append(st)
    x_s, wg, wu, wo = _ffn_cast(x_s, lp["norm_ffn2"], lp["ffn2_in"], lp["ffn2_out"], mix=mix_s, wmix=lp["w_out"],
                                gf=norm_final)
    x_p = _ffn(x_p, lp["norm_ffn2"], wg, wu, wo, mix=mix_p, wmix=lp["w_out"], gf=norm_final)
    return (x_p, x_s), (st_p, st_s)


def kernel(x_prompt, x_sample, state_ssd_conv, state_ssd, state_rwkv_shift, state_rwkv, state_s5_re, state_s5_im, state_pool, norm_ffn1, ffn1_in, ffn1_out, norm_mix, w_in, ssd_conv_w, ssd_conv_b, ssd_dt_bias, ssd_a_log, ssd_d, ssd_norm, rwkv_mu, rwkv_w0, rwkv_w2, rwkv_a0, rwkv_a2, rwkv_g2, rwkv_k_k, rwkv_k_a, rwkv_r_k, rwkv_ln_g, rwkv_ln_b, s5_lam_re, s5_lam_im, s5_log_step, s5_b_re, s5_b_im, s5_c_re, s5_c_im, s5_d, s5_glu_w, s5_glu_b, pool_w, pool_scale, w_out, norm_ffn2, ffn2_in, ffn2_out, norm_final):
    P = dict(norm_ffn1=norm_ffn1, ffn1_in=ffn1_in, ffn1_out=ffn1_out, norm_mix=norm_mix, w_in=w_in,
             ssd_conv_w=ssd_conv_w, ssd_conv_b=ssd_conv_b, ssd_dt_bias=ssd_dt_bias, ssd_a_log=ssd_a_log,
             ssd_d=ssd_d, ssd_norm=ssd_norm, rwkv_mu=rwkv_mu, rwkv_w0=rwkv_w0, rwkv_w2=rwkv_w2, rwkv_a0=rwkv_a0,
             rwkv_a2=rwkv_a2, rwkv_g2=rwkv_g2, rwkv_k_k=rwkv_k_k, rwkv_k_a=rwkv_k_a,
             rwkv_r_k=rwkv_r_k.reshape(rwkv_r_k.shape[0], -1), rwkv_ln_g=rwkv_ln_g, rwkv_ln_b=rwkv_ln_b,
             s5_lam_re=s5_lam_re, s5_lam_im=s5_lam_im, s5_log_step=s5_log_step, s5_b_re=s5_b_re, s5_b_im=s5_b_im,
             s5_c_re=s5_c_re, s5_c_im=s5_c_im, s5_d=s5_d, s5_glu_w=s5_glu_w, s5_glu_b=s5_glu_b, pool_w=pool_w,
             pool_scale=pool_scale, w_out=w_out, norm_ffn2=norm_ffn2, ffn2_in=ffn2_in, ffn2_out=ffn2_out)
    depth = norm_ffn1.shape[0]
    bp, tp, d = x_prompt.shape
    bs, ts, _ = x_sample.shape
    stacked = _stacked_params(P)
    layer_params = [_layer_params(stacked, l) for l in range(depth)]
    gf = norm_final.reshape(1, -1)
    sample_states = (state_ssd_conv, state_ssd, state_rwkv_shift, state_rwkv, state_s5_re, state_s5_im, state_pool)
    rwkv_rows = RWKV_HEADS * RWKV_HEAD
    decode_states = (jnp.swapaxes(state_ssd_conv, 1, 2), state_ssd, state_rwkv_shift,
                     jnp.transpose(state_rwkv, (0, 2, 3, 4, 1)).reshape(depth, rwkv_rows, RWKV_HEAD, bs),
                     state_s5_re, state_s5_im, jnp.swapaxes(state_pool, 1, 2))

    x_s = jnp.swapaxes(x_sample, 0, 1).reshape(ts * bs, d)
    (y_p, y_s), (st_p, st_s) = _trunk(
        x_prompt.reshape(bp * tp, d), x_s, layer_params, gf,
        lambda l, lp, proj, done: _mixers_prompt(lp, proj, batch=bp, seq=tp),
        lambda l, lp, proj, done: _mixers_decode(lp, proj, decode_states, done, batch=bs, seq=ts, layer=l))
    outs = [y_p.reshape(bp, tp, d), jnp.swapaxes(y_s.reshape(ts, bs, d), 0, 1)]
    for i, ref_state in enumerate(sample_states):
        outs.append(jnp.stack([st[i] for st in st_p]))
        if i == 1:
            outs.append(st_s[-1][i].reshape(ref_state.shape))
        elif i == 3:
            s_new = st_s[-1][i].reshape(depth, RWKV_HEADS, RWKV_HEAD, RWKV_HEAD, bs)
            outs.append(jnp.transpose(s_new, (0, 4, 1, 2, 3)))
        else:
            outs.append(jnp.stack([st[i] for st in st_s]))
    return tuple(outs)
```

```python
import functools
import math

import jax
import jax.numpy as jnp
from jax import lax
from jax.experimental import pallas as pl
from jax.experimental.pallas import tpu as pltpu

F32 = jnp.float32
BF16 = jnp.bfloat16
HIGHEST = lax.Precision.HIGHEST

SUBLANES = 8
LANES = 128
VMEM_LIMIT_BYTES = 56 * 1024 * 1024

GROUP_WIDTH = 256
SSD_HEAD_DIM = 64
SSD_HEADS = 4
SSD_GROUPS = 2
SSD_STATE = 128
SSD_CONV = 4
SSD_CONV_DIM = GROUP_WIDTH + 2 * SSD_GROUPS * SSD_STATE
SSD_CHUNK = 128
SSD_GROUP = 4
LOG2_E = math.log2(math.e)
RWKV_HEAD = 64
RWKV_HEADS = 4
RWKV_PROJ = 1024
RWKV_LN_EPS = 64e-5
RWKV_CHUNK = 64
RWKV_GROUP = 8
S5_GROUP_CH = 16
S5_GROUPS = 16
S5_STATE = 64
S5_WIDTH = S5_GROUPS * S5_STATE
POOL_WINDOWS = (2, 4, 8, 16)
POOL_CH = 64
POOL_BUF = 15
RMS_EPS = 1e-6
PAST_LEN = 16384

ROW_TILE = 512
FFN_CHUNK = 256
TM_CHUNK = 64
POOL_CHUNK = 256
S5_SUB = 16
SSD_STEP_TILES = 16
RWKV_STEP_TILES = 16


def _cparams(*sem):
    return pltpu.CompilerParams(dimension_semantics=sem, vmem_limit_bytes=VMEM_LIMIT_BYTES)


def _dot(a, b, **kw):
    return jnp.dot(a, b, preferred_element_type=F32, **kw)


def _dot_nt(a, b):
    return lax.dot_general(a, b, (((1,), (1,)), ((), ())), preferred_element_type=F32)


def _dot_tn(a, b):
    return lax.dot_general(a, b, (((0,), (0,)), ((), ())), preferred_element_type=F32)


def _sigmoid(x):
    return 1.0 / (1.0 + jnp.exp(-x))


def _silu(x):
    return x * _sigmoid(x)


def _softplus(x):
    return jnp.maximum(x, 0.0) + jnp.log(1.0 + jnp.exp(-jnp.abs(x)))


def _gelu_tanh(x):
    c = math.sqrt(2.0 / math.pi)
    return x * (0.5 * (1.0 + jnp.tanh(c * (x + 0.044715 * (x * x * x)))))


def _rms(x, g):
    return x * lax.rsqrt(jnp.mean(x * x, axis=-1, keepdims=True) + RMS_EPS) * g


def _full_spec(shape):
    n = len(shape)
    return pl.BlockSpec(shape, lambda *_: (0,) * n)


class _Layered(tuple):
    pass


def _pspec(p, single=False):
    mode = pl.Buffered(1) if single else None
    if isinstance(p, _Layered):
        a, l = p
        return pl.BlockSpec((None,) + a.shape[1:], lambda *_: (l,) + (0,) * (a.ndim - 1), pipeline_mode=mode)
    n = p.ndim
    return pl.BlockSpec(p.shape, lambda *_: (0,) * n, pipeline_mode=mode)


def _parg(p):
    return p[0] if isinstance(p, _Layered) else p


def _mix_residual(x, y_refs, wmix_ref):
    for j, y_ref in enumerate(y_refs):
        x = x + _dot(y_ref[...].astype(BF16), wmix_ref[j * GROUP_WIDTH:(j + 1) * GROUP_WIDTH, :])
    return x


def _swiglu_chunk(h, wg, wu, wo):
    act = (_silu(_dot(h, wg)) * _dot(h, wu)).astype(BF16)
    return _dot(act, wo)


def _ffn_body(*refs, has_mix, final_norm):
    it = iter(refs)
    x = next(it)[...]
    if has_mix:
        y_refs = [next(it) for _ in range(4)]
        x = _mix_residual(x, y_refs, next(it))
    g_ref, wg_ref, wu_ref, wo_ref = next(it), next(it), next(it), next(it)
    gf_ref = next(it) if final_norm else None
    o_ref = next(it)
    h = _rms(x, g_ref[...]).astype(BF16)
    acc = jnp.zeros_like(x)
    for c in range(wo_ref.shape[0] // FFN_CHUNK):
        cols = slice(c * FFN_CHUNK, (c + 1) * FFN_CHUNK)
        acc = acc + _swiglu_chunk(h, wg_ref[:, cols], wu_ref[:, cols], wo_ref[cols, :])
    x = x + 0.5 * acc
    if final_norm:
        x = _rms(x, gf_ref[...])
    o_ref[...] = x


def _ffn(x, g, wg, wu, wo, mix=None, wmix=None, gf=None):
    rows, d = x.shape
    row_spec = lambda w: pl.BlockSpec((ROW_TILE, w), lambda i: (i, 0))
    args, specs = [x], [row_spec(d)]
    if mix is not None:
        for y in mix:
            args.append(y)
            specs.append(row_spec(y.shape[1]))
        args.append(_parg(wmix))
        specs.append(_pspec(wmix, single=True))
    for a in (g, wg, wu, wo) + ((gf,) if gf is not None else ()):
        args.append(_parg(a))
        specs.append(_pspec(a, single=True))
    return pl.pallas_call(
        functools.partial(_ffn_body, has_mix=mix is not None, final_norm=gf is not None),
        grid=(rows // ROW_TILE,),
        in_specs=specs,
        out_specs=row_spec(d),
        out_shape=jax.ShapeDtypeStruct((rows, d), F32),
        compiler_params=_cparams("parallel"),
        name="ffn",
    )(*args)


def _ffn_cast_body(*refs, has_mix, final_norm):
    it = iter(refs)
    x_ref = next(it)
    if has_mix:
        y_refs = [next(it) for _ in range(4)]
        wmix_ref = next(it)
    g_ref, wg_ref, wu_ref, wo_ref = next(it), next(it), next(it), next(it)
    gf_ref = next(it) if final_norm else None
    o_ref, wg_out, wu_out, wo_out, x_scr, h_scr, acc_scr = (next(it) for _ in range(7))
    c = pl.program_id(0)

    @pl.when(c == 0)
    def _():
        x = x_ref[...]
        if has_mix:
            x = _mix_residual(x, y_refs, wmix_ref)
        x_scr[...] = x
        h_scr[...] = _rms(x, g_ref[...]).astype(BF16)
        acc_scr[...] = jnp.zeros(acc_scr.shape, F32)

    wg = wg_ref[...].astype(BF16)
    wu = wu_ref[...].astype(BF16)
    wo = wo_ref[...].astype(BF16)
    wg_out[...] = wg
    wu_out[...] = wu
    wo_out[...] = wo
    acc_scr[...] += _swiglu_chunk(h_scr[...], wg, wu, wo)

    @pl.when(c == pl.num_programs(0) - 1)
    def _():
        x = x_scr[...] + 0.5 * acc_scr[...]
        if final_norm:
            x = _rms(x, gf_ref[...])
        o_ref[...] = x


def _ffn_cast(x, g, wi, wo, mix=None, wmix=None, gf=None):
    rows, d = x.shape
    wi_all, l = wi
    wo_all, _ = wo
    d_ff = wo_all.shape[1]
    nchunks = d_ff // FFN_CHUNK
    args, specs = [x], [_full_spec(x.shape)]
    if mix is not None:
        for y in mix:
            args.append(y)
            specs.append(_full_spec(y.shape))
        args.append(_parg(wmix))
        specs.append(_pspec(wmix, single=True))
    args += [_parg(g), wi_all, wi_all, wo_all]
    specs += [_pspec(g),
              pl.BlockSpec((None, d, FFN_CHUNK), lambda c: (l, 0, c)),
              pl.BlockSpec((None, d, FFN_CHUNK), lambda c: (l, 0, c + nchunks)),
              pl.BlockSpec((None, FFN_CHUNK, d), lambda c: (l, c, 0))]
    if gf is not None:
        args.append(gf)
        specs.append(_full_spec(gf.shape))
    col_spec = pl.BlockSpec((d, FFN_CHUNK), lambda c: (0, c))
    return pl.pallas_call(
        functools.partial(_ffn_cast_body, has_mix=mix is not None, final_norm=gf is not None),
        grid=(nchunks,),
        in_specs=specs,
        out_specs=[_full_spec(x.shape), col_spec, col_spec, pl.BlockSpec((FFN_CHUNK, d), lambda c: (c, 0))],
        out_shape=[jax.ShapeDtypeStruct((rows, d), F32), jax.ShapeDtypeStruct((d, d_ff), BF16),
                   jax.ShapeDtypeStruct((d, d_ff), BF16), jax.ShapeDtypeStruct((d_ff, d), BF16)],
        scratch_shapes=[pltpu.VMEM((rows, d), F32), pltpu.VMEM((rows, d), BF16), pltpu.VMEM((rows, d), F32)],
        compiler_params=_cparams("arbitrary"),
        name="ffn_cast",
    )(*args)


def _inproj_body(x_ref, g_ref, wt_ref, *o_refs):
    h = _rms(x_ref[...], g_ref[...]).astype(BF16)
    off = 0
    for o_ref in o_refs:
        n = o_ref.shape[-1]
        o_ref[...] = _dot_nt(h, wt_ref[off:off + n, :])
        off += n


def _inproj_cast_body(x_ref, g_ref, win_ref, *o_refs, layer):
    *proj_refs, wall_ref = o_refs
    split = GROUP_WIDTH + SSD_CONV_DIM
    wt = win_ref[:, layer, :]
    tail = wt.shape[0] - split - SSD_HEADS
    wall_ref[0:split, :] = wt[0:split].astype(BF16)
    wall_ref[split:split + tail, :] = wt[split + SSD_HEADS:].astype(BF16)
    dt_rows = jnp.concatenate([wt[split:split + SSD_HEADS], jnp.zeros((LANES - SSD_HEADS, wt.shape[1]), F32)], axis=0)
    wall_ref[split + tail:, :] = dt_rows.astype(BF16)
    _inproj_body(x_ref, g_ref, wall_ref, *proj_refs)


def _inproj_cast(x, g, w_in, widths):
    rows, d = x.shape
    wt_all, l = w_in
    outs = pl.pallas_call(
        functools.partial(_inproj_cast_body, layer=l),
        grid=(1,),
        in_specs=[_full_spec(x.shape), _pspec(g),
                  pl.BlockSpec(wt_all.shape, lambda i: (0, 0, 0), pipeline_mode=pl.Buffered(1))],
        out_specs=[_full_spec((rows, n)) for n in widths] + [_full_spec((sum(widths), d))],
        out_shape=[jax.ShapeDtypeStruct((rows, n), F32) for n in widths]
                  + [jax.ShapeDtypeStruct((sum(widths), d), BF16)],
        compiler_params=_cparams("arbitrary"),
        name="inproj_cast",
    )(x, _parg(g), wt_all)
    return outs[:-1], outs[-1]


def _inproj(x, g, w, widths):
    rows, d = x.shape
    row_spec = lambda w_: pl.BlockSpec((ROW_TILE, w_), lambda i: (i, 0))
    return pl.pallas_call(
        _inproj_body,
        grid=(rows // ROW_TILE,),
        in_specs=[row_spec(d), _pspec(g), _pspec(w, single=True)],
        out_specs=[row_spec(n) for n in widths],
        out_shape=[jax.ShapeDtypeStruct((rows, n), F32) for n in widths],
        compiler_params=_cparams("parallel"),
        name="inproj",
    )(x, _parg(g), _parg(w))


def _ssd_body(z_ref, xbc_ref, dt_ref, cw_ref, cb_ref, dtb_ref, alog_ref, dsk_ref, ng_ref,
              y_ref, conv_ref, hout_ref, xpad_scr, h_scr, *, chunk, group):
    L, G = chunk, group
    GL = G * L
    c = pl.program_id(1)
    pad = SUBLANES
    halo = SSD_CONV - 1
    hpg = SSD_HEADS // SSD_GROUPS
    assert hpg == 2 and hpg * SSD_HEAD_DIM == SSD_STATE

    @pl.when(c == 0)
    def _():
        xpad_scr[0:pad, :] = jnp.zeros((pad, SSD_CONV_DIM), F32)
        h_scr[...] = jnp.zeros(h_scr.shape, F32)

    xpad_scr[pad:pad + GL, :] = xbc_ref[...]
    xfull = xpad_scr[...]
    conv = cb_ref[...] + cw_ref[halo:halo + 1, :] * xfull[pad:pad + GL]
    for j in range(halo):
        conv = conv + cw_ref[j:j + 1, :] * pltpu.roll(xfull, halo - j, axis=0)[pad:pad + GL]
    xpad_scr[pad - halo:pad, :] = xpad_scr[pad + GL - halo:pad + GL, :]
    conv = _silu(conv)
    xs = conv[:, 0:GROUP_WIDTH]
    bm = conv[:, GROUP_WIDTH:2 * GROUP_WIDTH].astype(BF16)
    cm = conv[:, 2 * GROUP_WIDTH:3 * GROUP_WIDTH].astype(BF16)

    row = lax.broadcasted_iota(jnp.int32, (L, L), 0)
    col = lax.broadcasted_iota(jnp.int32, (L, L), 1)
    causal = row >= col
    tril = jnp.where(causal, 1.0, 0.0).astype(F32)
    dt = _softplus(dt_ref[...] + dtb_ref[...])
    da = dt * (-jnp.exp(alog_ref[...]) * LOG2_E)
    acs = [_dot(tril, da[i * L:(i + 1) * L, :], precision=HIGHEST) for i in range(G)]
    acs_t = [a.T for a in acs]
    e_acs = [jnp.exp2(a) for a in acs]
    e_end = [jnp.exp2(a[L - 1:L, :] - a) for a in acs]
    e_last = [jnp.exp2(a[L - 1:L, :]) for a in acs]

    keys = [(i, g) for i in range(G) for g in range(SSD_GROUPS)]
    rows_of = lambda x, i: x[i * L:(i + 1) * L]
    lanes_of = lambda x, g: x[:, g * SSD_STATE:(g + 1) * SSD_STATE]
    lane_lo = lax.broadcasted_iota(jnp.int32, (L, hpg * SSD_HEAD_DIM), 1) < SSD_HEAD_DIM
    row_lo = lax.broadcasted_iota(jnp.int32, (hpg * SSD_HEAD_DIM, SSD_STATE), 0) < SSD_HEAD_DIM
    head_cols = lambda a, g: jnp.where(lane_lo, a[:, g * hpg:g * hpg + 1], a[:, g * hpg + 1:g * hpg + 2])
    bg = {(i, g): lanes_of(rows_of(bm, i), g) for i, g in keys}
    cg = {(i, g): lanes_of(rows_of(cm, i), g) for i, g in keys}
    scores = {k: _dot_nt(cg[k], bg[k]) for k in keys}
    xdt = {(i, g): lanes_of(rows_of(xs, i), g) * head_cols(rows_of(dt, i), g) for i, g in keys}
    decay = {(i, h): jnp.exp2(jnp.where(causal, acs[i][:, h:h + 1] - acs_t[i][h:h + 1, :], -jnp.inf))
             for i in range(G) for h in range(SSD_HEADS)}
    p_mat = {(i, g): jnp.concatenate([(scores[(i, g)] * decay[(i, g * hpg + k)]).astype(BF16) for k in range(hpg)],
                                     axis=1) for i, g in keys}
    y_in = {k: _dot(p_mat[k], _bd(xdt[k].astype(BF16))) for k in keys}
    st = {(i, g): _dot_tn((xdt[(i, g)] * head_cols(e_end[i], g)).astype(BF16), bg[(i, g)]) for i, g in keys}

    y_rows = []
    for i in range(G):
        ys = []
        for g in range(SSD_GROUPS):
            h_prev = h_scr[g * hpg:(g + 1) * hpg].reshape(hpg * SSD_HEAD_DIM, SSD_STATE)
            ys.append(y_in[(i, g)] + _dot_nt(cg[(i, g)], h_prev.astype(BF16)) * head_cols(e_acs[i], g))
            keep = jnp.where(row_lo, e_last[i][:, g * hpg:g * hpg + 1], e_last[i][:, g * hpg + 1:g * hpg + 2])
            h_scr[g * hpg:(g + 1) * hpg] = (h_prev * keep + st[(i, g)]).reshape(hpg, SSD_HEAD_DIM, SSD_STATE)
        y_rows.append(jnp.concatenate(ys, axis=-1))
    y = jnp.concatenate(y_rows, axis=0) + xs * dsk_ref[...]
    y = y * _silu(z_ref[...])
    y_ref[...] = _rms(y, ng_ref[...])

    @pl.when(c == pl.num_programs(1) - 1)
    def _():
        hout_ref[0] = h_scr[...]
        conv_ref[0] = xpad_scr[pad - halo:pad, :]


def _ssd(z, xbc, dtr, lp, *, batch, seq):
    chunk = SSD_CHUNK
    rows = chunk * SSD_GROUP
    nc = seq // rows
    rspec = lambda w: pl.BlockSpec((rows, w), lambda b, c: (b * nc + c, 0))
    consts = (lp["conv_w"], lp["conv_b"], lp["dt_bias"], lp["a_log"], lp["d_skip"], lp["ssd_norm"])
    return pl.pallas_call(
        functools.partial(_ssd_body, chunk=chunk, group=SSD_GROUP),
        grid=(batch, nc),
        in_specs=[rspec(GROUP_WIDTH), rspec(SSD_CONV_DIM), rspec(LANES)] + [_pspec(a) for a in consts],
        out_specs=[rspec(GROUP_WIDTH),
                   pl.BlockSpec((1, SSD_CONV - 1, SSD_CONV_DIM), lambda b, c: (b, 0, 0)),
                   pl.BlockSpec((1, SSD_HEADS, SSD_HEAD_DIM, SSD_STATE), lambda b, c: (b, 0, 0, 0))],
        out_shape=[jax.ShapeDtypeStruct((batch * seq, GROUP_WIDTH), F32),
                   jax.ShapeDtypeStruct((batch, SSD_CONV - 1, SSD_CONV_DIM), F32),
                   jax.ShapeDtypeStruct((batch, SSD_HEADS, SSD_HEAD_DIM, SSD_STATE), F32)],
        scratch_shapes=[pltpu.VMEM((SUBLANES + rows, SSD_CONV_DIM), F32),
                        pltpu.VMEM((SSD_HEADS, SSD_HEAD_DIM, SSD_STATE), F32)],
        compiler_params=_cparams("parallel", "arbitrary"),
        name="ssd",
    )(z, xbc, dtr, *[_parg(a) for a in consts])


def _ssd_step_body(z_ref, xbc_ref, dt_ref, conv0_ref, h0_ref, *rest, seq, batch, layer):
    hdone_ref, rest = (rest[0], rest[1:]) if layer else (None, rest)
    (cw_ref, cb_ref, dtb_ref, aneg_ref, dsk_ref, ng_ref, hexp_ref, y_ref, conv_ref, hout_ref,
     xs_scr, bm_scr, cm_scr, xdt_scr, dec_scr, y_scr) = rest
    T, B = seq, batch
    if layer:
        hout_ref[0:layer] = hdone_ref[...]
    GW = GROUP_WIDTH
    j = pl.program_id(0)
    tiles = SSD_STEP_TILES

    @pl.when(j == 0)
    def _():
        rows = [conv0_ref[i] for i in range(SSD_CONV - 1)]
        rows += [xbc_ref[t * B:(t + 1) * B, :] for t in range(T)]
        for t in range(T):
            conv = cb_ref[...] + cw_ref[0:1, :] * rows[t]
            for i in range(1, SSD_CONV):
                conv = conv + cw_ref[i:i + 1, :] * rows[t + i]
            conv = _silu(conv)
            xs = conv[:, 0:GW]
            xs_scr[t] = xs
            for g in range(SSD_GROUPS):
                bm_scr[t, g] = conv[:, GW + g * SSD_STATE:GW + (g + 1) * SSD_STATE].T
                cm_scr[t, g] = conv[:, 2 * GW + g * SSD_STATE:2 * GW + (g + 1) * SSD_STATE].T
            dt = _softplus(dt_ref[t * B:(t + 1) * B, :] + dtb_ref[...])
            dte = _dot(dt, hexp_ref[...], precision=HIGHEST)
            xdt_scr[t] = (xs * dte).T
            dec_scr[t] = jnp.exp(dte * aneg_ref[...]).T
        for i in range(SSD_CONV - 1):
            conv_ref[i] = rows[T + i]

    hp0 = j * tiles
    grp = hp0 // (SSD_HEAD_DIM * (SSD_HEADS // SSD_GROUPS))
    for q in range(tiles):
        hp = pl.ds(hp0 + q, 1)
        h = h0_ref[:, q, :].T
        for t in range(T):
            h = h * dec_scr[t, hp, :] + bm_scr[t, grp] * xdt_scr[t, hp, :]
            y_scr[t, hp, :] = jnp.sum(h * cm_scr[t, grp], axis=0, keepdims=True)
        hout_ref[layer, :, q, :] = h.T

    @pl.when(j == pl.num_programs(0) - 1)
    def _():
        for t in range(T):
            y = y_scr[t].T + xs_scr[t] * dsk_ref[...]
            y = y * _silu(z_ref[t * B:(t + 1) * B, :])
            y_ref[t * B:(t + 1) * B, :] = _rms(y, ng_ref[...])


def _layer_state_specs(layer, block, axis):
    idx = lambda first: (lambda j: (first,) + tuple(j if a == axis else 0 for a in range(len(block))))
    cur = pl.BlockSpec((None,) + block, idx(layer))
    prev = [pl.BlockSpec((layer,) + block, idx(0))] if layer else []
    out = pl.BlockSpec((layer + 1,) + block, idx(0))
    return cur, prev, out


def _ssd_step(z, xbc, dtr, conv_all, h_all, h_done, lp, *, batch, seq, layer):
    n = batch * seq
    srows = SSD_HEADS * SSD_HEAD_DIM
    consts = (lp["conv_w"], lp["conv_b"], lp["dt_bias"], lp["a_neg_exp"], lp["d_skip"], lp["ssd_norm"], lp["head_expand"])
    hspec, prev_specs, hout_spec = _layer_state_specs(layer, (batch, SSD_STEP_TILES, SSD_STATE), 1)
    prev_args = [h_done] if layer else []
    cshape = (SSD_CONV - 1, batch, SSD_CONV_DIM)
    return pl.pallas_call(
        functools.partial(_ssd_step_body, seq=seq, batch=batch, layer=layer),
        grid=(srows // SSD_STEP_TILES,),
        in_specs=[_full_spec((n, GROUP_WIDTH)), _full_spec((n, SSD_CONV_DIM)), _full_spec((n, LANES)),
                  pl.BlockSpec((None,) + cshape, lambda j: (layer, 0, 0, 0)), hspec] + prev_specs
                 + [_pspec(a) for a in consts],
        out_specs=[_full_spec((n, GROUP_WIDTH)), _full_spec(cshape), hout_spec],
        out_shape=[jax.ShapeDtypeStruct((n, GROUP_WIDTH), F32),
                   jax.ShapeDtypeStruct(cshape, F32),
                   jax.ShapeDtypeStruct((layer + 1, batch, srows, SSD_STATE), F32)],
        scratch_shapes=[pltpu.VMEM((seq, batch, GROUP_WIDTH), F32),
                        pltpu.VMEM((seq, SSD_GROUPS, SSD_STATE, batch), F32),
                        pltpu.VMEM((seq, SSD_GROUPS, SSD_STATE, batch), F32),
                        pltpu.VMEM((seq, GROUP_WIDTH, batch), F32),
                        pltpu.VMEM((seq, GROUP_WIDTH, batch), F32),
                        pltpu.VMEM((seq, GROUP_WIDTH, batch), F32)],
        compiler_params=_cparams("arbitrary"),
        name="ssd_step",
    )(z, xbc, dtr, conv_all, h_all.reshape(h_all.shape[0], batch, srows, SSD_STATE),
      *prev_args, *[_parg(a) for a in consts])


PAIR = 2 * RWKV_HEAD
RWKV_PAIRS = RWKV_HEADS // 2


def _bd(x):
    half = x.shape[1] // 2
    lane = lax.broadcasted_iota(jnp.int32, x.shape, 1)
    zero = jnp.zeros_like(x)
    return jnp.concatenate([jnp.where(lane < half, x, zero), jnp.where(lane >= half, x, zero)], axis=0)


def _half_sums(x, lo):
    s_lo = jnp.sum(jnp.where(lo, x, 0.0), axis=-1, keepdims=True)
    s_hi = jnp.sum(jnp.where(lo, 0.0, x), axis=-1, keepdims=True)
    return jnp.where(lo, s_lo, s_hi)


def _head_sum(x):
    lo = lax.broadcasted_iota(jnp.int32, (x.shape[0], PAIR), 1) < RWKV_HEAD
    return jnp.concatenate([_half_sums(x[:, p * PAIR:(p + 1) * PAIR], lo) for p in range(RWKV_PAIRS)], axis=-1)


def _rwkv_pointwise(u, prev, mu_ref, w0_ref, w2_ref, a0_ref, a2_ref, g2_ref, kk_ref, ka_ref):
    GW = GROUP_WIDTH
    xs = u + (prev - u) * mu_ref[...]
    r = xs[:, 0:GW]
    k = xs[:, GW:2 * GW]
    v = xs[:, 2 * GW:3 * GW]
    wd = xs[:, 3 * GW:3 * GW + 64]
    ad = xs[:, 3 * GW + 64:3 * GW + 128]
    gd = xs[:, 3 * GW + 128:3 * GW + 256]
    w_lin = w0_ref[...] + _dot(jnp.tanh(wd).astype(BF16), w2_ref[...])
    logdecay = -math.exp(-0.5) * _sigmoid(w_lin)
    a = _sigmoid(a0_ref[...] + _dot(ad.astype(BF16), a2_ref[...]))
    g = _dot(_sigmoid(gd).astype(BF16), g2_ref[...])
    kk = k * kk_ref[...]
    kk = kk * lax.rsqrt(jnp.maximum(_head_sum(kk * kk), 1e-24))
    k = k * (1.0 + (a - 1.0) * ka_ref[...])
    return r, k, v, logdecay, a, g, kk


def _rwkv_finish(y, r, k, v, g, rk_ref, lng_ref, lnb_ref):
    mean = _head_sum(y) * (1.0 / RWKV_HEAD)
    yc = y - mean
    var = _head_sum(yc * yc) * (1.0 / RWKV_HEAD)
    y = yc * lax.rsqrt(var + RWKV_LN_EPS) * lng_ref[...] + lnb_ref[...]
    bonus = _head_sum(r * k * rk_ref[...]) * v
    return (y + bonus) * g


def _rwkv_body(u_ref, mu_ref, w0_ref, w2_ref, a0_ref, a2_ref, g2_ref, kk_ref, ka_ref, rk_ref,
               lng_ref, lnb_ref, y_ref, shift_ref, sout_ref, upad_scr, s_scr, *, chunk, group):
    L, G = chunk, group
    GL = G * L
    c = pl.program_id(1)
    pad = SUBLANES

    @pl.when(c == 0)
    def _():
        upad_scr[0:pad, :] = jnp.zeros((pad, RWKV_PROJ), F32)
        s_scr[...] = jnp.zeros(s_scr.shape, F32)

    u = u_ref[...]
    upad_scr[pad:pad + GL, :] = u
    prev = pltpu.roll(upad_scr[...], 1, axis=0)[pad:pad + GL]
    upad_scr[pad - 1:pad, :] = u[GL - 1:GL, :]
    r, k, v, logdecay, a, g, kk = _rwkv_pointwise(u, prev, mu_ref, w0_ref, w2_ref, a0_ref, a2_ref, g2_ref,
                                                  kk_ref, ka_ref)

    tril = jnp.where(lax.broadcasted_iota(jnp.int32, (L, L), 0) >= lax.broadcasted_iota(jnp.int32, (L, L), 1),
                     1.0, 0.0).astype(F32)
    cl = jnp.concatenate([_dot(tril, logdecay[i * L:(i + 1) * L, :], precision=HIGHEST) for i in range(G)], axis=0)
    e_in = jnp.exp(cl)
    e_inv = jnp.exp(-cl)
    r_t = r * e_in
    r_tb = r_t.astype(BF16)
    a_tb = (-kk * jnp.exp(cl - logdecay)).astype(BF16)
    b_tb = (kk * a * e_inv).astype(BF16)
    k_tb = (k * e_inv).astype(BF16)
    vb = v.astype(BF16)

    row = lax.broadcasted_iota(jnp.int32, (L, PAIR), 0)
    colh = lax.broadcasted_iota(jnp.int32, (L, PAIR), 1) & (RWKV_HEAD - 1)
    strict = row > colh
    incl = row >= colh
    eye_pair = jnp.where(row == colh, 1.0, 0.0).astype(F32)
    lane_lo = lax.broadcasted_iota(jnp.int32, (RWKV_HEAD, PAIR), 1) < RWKV_HEAD
    same_head = (lax.broadcasted_iota(jnp.int32, (PAIR, PAIR), 0) < RWKV_HEAD) == \
                (lax.broadcasted_iota(jnp.int32, (PAIR, PAIR), 1) < RWKV_HEAD)

    streams = [(i, p) for i in range(G) for p in range(RWKV_PAIRS)]
    ns = len(streams)
    blk = lambda x, i, p: x[i * L:(i + 1) * L, p * PAIR:(p + 1) * PAIR]
    lhs = [jnp.concatenate([blk(a_tb, i, p), blk(r_tb, i, p)], axis=0) for i, p in streams]
    m_ab = [_dot_nt(lhs[s], _bd(blk(b_tb, i, p))) for s, (i, p) in enumerate(streams)]
    m_ak = [_dot_nt(lhs[s], _bd(blk(k_tb, i, p))) for s, (i, p) in enumerate(streams)]
    n_ab = [jnp.where(strict, m[0:L], 0.0) for m in m_ab]
    m_rb = [jnp.where(incl, m[L:2 * L], 0.0).astype(BF16) for m in m_ab]
    n_ak = [jnp.where(strict, m[0:L], 0.0).astype(BF16) for m in m_ak]
    m_rk = [jnp.where(incl, m[L:2 * L], 0.0).astype(BF16) for m in m_ak]
    tinv = [eye_pair + n for n in n_ab]
    pwb = [n.astype(BF16) for n in n_ab]
    pw = [_dot(x, _bd(x)) for x in pwb]
    for _ in range(int(math.log2(L)) - 2):
        pwb = [x.astype(BF16) for x in pw]
        both = [_dot(jnp.concatenate([pwb[s], tinv[s].astype(BF16)], axis=0), _bd(pwb[s])) for s in range(ns)]
        pw = [x[0:L] for x in both]
        tinv = [tinv[s] + both[s][L:2 * L] for s in range(ns)]
    pwb = [x.astype(BF16) for x in pw]
    tinv = [tinv[s] + _dot(tinv[s].astype(BF16), _bd(pwb[s])) for s in range(ns)]
    tinvb = [x.astype(BF16) for x in tinv]
    nv_mv = [_dot(jnp.concatenate([n_ak[s], m_rk[s]], axis=0), _bd(blk(vb, i, p))) for s, (i, p) in enumerate(streams)]
    wu = [_dot(tinvb[s], jnp.concatenate([_bd(blk(a_tb, i, p)), _bd(nv_mv[s][0:L].astype(BF16))], axis=1))
          for s, (i, p) in enumerate(streams)]
    wub = [x.astype(BF16) for x in wu]
    qy = [_dot(m_rb[s], jnp.concatenate([_bd(wub[s][:, 0:PAIR]), _bd(wub[s][:, PAIR:2 * PAIR])], axis=1))
          for s in range(ns)]
    q = [(blk(r_t, i, p) + qy[s][:, 0:PAIR]).astype(BF16) for s, (i, p) in enumerate(streams)]
    y_loc = [qy[s][:, PAIR:2 * PAIR] + nv_mv[s][L:2 * L] for s in range(ns)]
    zeros_b = jnp.zeros((L, PAIR), BF16)
    mg = [_dot_tn(jnp.concatenate([wub[s], jnp.concatenate([zeros_b, blk(vb, i, p)], axis=1)], axis=0),
                  jnp.concatenate([blk(b_tb, i, p), blk(k_tb, i, p)], axis=0))
          for s, (i, p) in enumerate(streams)]
    p_end = [e_in[(i + 1) * L - 1:(i + 1) * L, p * PAIR:(p + 1) * PAIR] for i, p in streams]
    m_t = [(jnp.where(same_head, mg[s][0:PAIR], 0.0) * p_end[s]).astype(BF16) for s in range(ns)]
    g_t = [jnp.where(lane_lo, mg[s][PAIR:PAIR + RWKV_HEAD], mg[s][PAIR + RWKV_HEAD:2 * PAIR]) * p_end[s]
           for s in range(ns)]

    y_rows = []
    for i in range(G):
        y_pairs = []
        for p in range(RWKV_PAIRS):
            s = i * RWKV_PAIRS + p
            s0 = s_scr[p]
            s0b = s0.astype(BF16)
            y_pairs.append(_dot_nt(q[s], _bd(s0b)) + y_loc[s])
            s_scr[p] = s0 * p_end[s] + _dot(s0b, m_t[s]) + g_t[s]
        y_rows.append(jnp.concatenate(y_pairs, axis=-1))
    y = jnp.concatenate(y_rows, axis=0)
    y_ref[...] = _rwkv_finish(y, r, k, v, g, rk_ref, lng_ref, lnb_ref)

    @pl.when(c == pl.num_programs(1) - 1)
    def _():
        sout_ref[0] = s_scr[...]
        shift_ref[0] = upad_scr[pad - 1:pad, :]


_RWKV_PARAM_NAMES = ("mu", "w0", "w2", "a0", "a2", "g2", "k_k", "k_a", "r_k", "ln_g", "ln_b")


def _rwkv(u, p, *, batch, seq):
    rows = RWKV_CHUNK * RWKV_GROUP
    nc = seq // rows
    params = [p[n] for n in _RWKV_PARAM_NAMES]
    sspec = pl.BlockSpec((1, RWKV_PAIRS, RWKV_HEAD, PAIR), lambda b, c: (b, 0, 0, 0))
    y, shift, s_last = pl.pallas_call(
        functools.partial(_rwkv_body, chunk=RWKV_CHUNK, group=RWKV_GROUP),
        grid=(batch, nc),
        in_specs=[pl.BlockSpec((rows, RWKV_PROJ), lambda b, c: (b * nc + c, 0))] + [_pspec(a) for a in params],
        out_specs=[pl.BlockSpec((rows, GROUP_WIDTH), lambda b, c: (b * nc + c, 0)),
                   pl.BlockSpec((1, 1, RWKV_PROJ), lambda b, c: (b, 0, 0)), sspec],
        out_shape=[jax.ShapeDtypeStruct((batch * seq, GROUP_WIDTH), F32),
                   jax.ShapeDtypeStruct((batch, 1, RWKV_PROJ), F32),
                   jax.ShapeDtypeStruct((batch, RWKV_PAIRS, RWKV_HEAD, PAIR), F32)],
        scratch_shapes=[pltpu.VMEM((SUBLANES + rows, RWKV_PROJ), F32),
                        pltpu.VMEM((RWKV_PAIRS, RWKV_HEAD, PAIR), F32)],
        compiler_params=_cparams("parallel", "arbitrary"),
        name="rwkv",
    )(u, *[_parg(a) for a in params])
    s_last = s_last.reshape(batch, RWKV_PAIRS, RWKV_HEAD, 2, RWKV_HEAD).transpose(0, 1, 3, 2, 4).reshape(
        batch, RWKV_HEADS, RWKV_HEAD, RWKV_HEAD)
    return y, shift.reshape(batch, RWKV_PROJ), s_last


def _rwkv_step_body(u_ref, shift0_ref, s0_ref, *rest, seq, batch, layer):
    sdone_ref, rest = (rest[0], rest[1:]) if layer else (None, rest)
    (mu_ref, w0_ref, w2_ref, a0_ref, a2_ref, g2_ref, kk_ref, ka_ref, rk_ref, lng_ref, lnb_ref, y_ref, sout_ref,
     r_scr, w_scr, k_scr, b_scr, nkk_scr, v_scr, y_scr) = rest
    T, B = seq, batch
    j = pl.program_id(0)
    if layer:
        sout_ref[0:layer] = sdone_ref[...]
    tiles = RWKV_STEP_TILES

    def pointwise(t):
        u = u_ref[t * B:(t + 1) * B, :]
        prev = shift0_ref[...] if t == 0 else u_ref[(t - 1) * B:t * B, :]
        return _rwkv_pointwise(u, prev, mu_ref, w0_ref, w2_ref, a0_ref, a2_ref, g2_ref, kk_ref, ka_ref)

    @pl.when(j == 0)
    def _():
        for t in range(T):
            r, k, v, logdecay, a, _, kk = pointwise(t)
            r_scr[t] = r.T
            w_scr[t] = jnp.exp(logdecay).T
            k_scr[t] = k.T
            b_scr[t] = (kk * a).T
            nkk_scr[t] = (-kk).T
            v_scr[t] = v.T

    i0 = j * tiles
    keys = pl.ds(pl.multiple_of((i0 // RWKV_HEAD) * RWKV_HEAD, RWKV_HEAD), RWKV_HEAD)
    for q in range(tiles):
        vi = pl.ds(i0 + q, 1)
        s = s0_ref[q]
        for t in range(T):
            sa = jnp.sum(s * nkk_scr[t, keys, :], axis=0, keepdims=True)
            s = s * w_scr[t, keys, :] + k_scr[t, keys, :] * v_scr[t, vi, :] + b_scr[t, keys, :] * sa
            y_scr[t, vi, :] = jnp.sum(s * r_scr[t, keys, :], axis=0, keepdims=True)
        sout_ref[layer, q] = s

    @pl.when(j == pl.num_programs(0) - 1)
    def _():
        for t in range(T):
            r, k, v, _, _, g, _ = pointwise(t)
            y_ref[t * B:(t + 1) * B, :] = _rwkv_finish(y_scr[t].T, r, k, v, g, rk_ref, lng_ref, lnb_ref)


def _rwkv_step(u, shift0, s_all, s_done, p, *, batch, seq, layer):
    n = batch * seq
    srows = RWKV_HEADS * RWKV_HEAD
    params = [p[nm] for nm in _RWKV_PARAM_NAMES]
    sspec, prev_specs, sout_spec = _layer_state_specs(layer, (RWKV_STEP_TILES, RWKV_HEAD, batch), 0)
    prev_args = [s_done] if layer else []
    tposed = pltpu.VMEM((seq, GROUP_WIDTH, batch), F32)
    return pl.pallas_call(
        functools.partial(_rwkv_step_body, seq=seq, batch=batch, layer=layer),
        grid=(srows // RWKV_STEP_TILES,),
        in_specs=[_full_spec((n, RWKV_PROJ)), _full_spec((batch, RWKV_PROJ)), sspec] + prev_specs
                 + [_pspec(a) for a in params],
        out_specs=[_full_spec((n, GROUP_WIDTH)), sout_spec],
        out_shape=[jax.ShapeDtypeStruct((n, GROUP_WIDTH), F32),
                   jax.ShapeDtypeStruct((layer + 1, srows, RWKV_HEAD, batch), F32)],
        scratch_shapes=[tposed] * 7,
        compiler_params=_cparams("arbitrary"),
        name="rwkv_step",
    )(u, shift0, s_all, *prev_args, *[_parg(a) for a in params])


def _s5_body(u_ref, hre0_ref, him0_ref, are_ref, aim_ref, bmat_ref, cmat_ref, d_ref, gw_ref, gb_ref,
             y_ref, hre_ref, him_ref, hs_scr, tm_scr, *, steps, batch_major):
    c = pl.program_id(1)
    ns = S5_WIDTH
    bsub = SUBLANES

    @pl.when(c == 0)
    def _():
        hre_ref[...] = hre0_ref[...]
        him_ref[...] = him0_ref[...]

    if batch_major:
        for b in range(bsub):
            tm_scr[:, b, :] = u_ref[b]
        u = tm_scr[...].reshape(steps * bsub, GROUP_WIDTH)
    else:
        u = u_ref[...].reshape(steps * bsub, GROUP_WIDTH)
    are = jnp.broadcast_to(are_ref[...], (bsub, ns))
    aim = jnp.broadcast_to(aim_ref[...], (bsub, ns))
    hre, him = hre_ref[...], him_ref[...]
    sub = min(S5_SUB, steps)
    rows = sub * bsub
    outs = []
    for k in range(steps // sub):
        r0 = k * rows
        u_k = u[r0:r0 + rows]
        hs_scr[r0:r0 + rows, :] = _dot(u_k.astype(BF16), bmat_ref[...])
        for t in range(sub):
            rs = slice(r0 + t * bsub, r0 + (t + 1) * bsub)
            hre, him = (are * hre - aim * him + hs_scr[rs, 0:ns], are * him + aim * hre + hs_scr[rs, ns:2 * ns])
            hs_scr[rs, 0:ns] = hre
            hs_scr[rs, ns:2 * ns] = him
        y = _dot(hs_scr[r0:r0 + rows, :].astype(BF16), cmat_ref[...]) + u_k * d_ref[...]
        y = _gelu_tanh(y)
        yy = _dot(y.astype(BF16), gw_ref[...]) + gb_ref[...]
        outs.append(yy[:, 0:GROUP_WIDTH] * _sigmoid(yy[:, GROUP_WIDTH:2 * GROUP_WIDTH]))
    hre_ref[...] = hre
    him_ref[...] = him
    out = jnp.concatenate(outs, axis=0).reshape(steps, bsub, GROUP_WIDTH)
    if batch_major:
        tm_scr[...] = out
        for b in range(bsub):
            y_ref[b] = tm_scr[:, b, :]
    else:
        y_ref[...] = out


def _time_specs(u, batch_major, chunk):
    bsub = SUBLANES
    if batch_major:
        batch, seq, _ = u.shape
        steps = min(chunk, seq)
        spec = pl.BlockSpec((bsub, steps, GROUP_WIDTH), lambda b, c: (b, c, 0))
    else:
        seq, batch, _ = u.shape
        steps = min(chunk, seq)
        spec = pl.BlockSpec((steps, bsub, GROUP_WIDTH), lambda b, c: (c, b, 0))
    return batch, seq, steps, spec


def _s5(u, hre0, him0, lp, *, batch_major):
    batch, seq, steps, tspec = _time_specs(u, batch_major, TM_CHUNK)
    bsub = SUBLANES
    hspec = pl.BlockSpec((bsub, S5_WIDTH), lambda b, c: (b, 0))
    consts = (lp["s5_are"], lp["s5_aim"], lp["s5_bmat"], lp["s5_cmat"], lp["s5_d"], lp["s5_gw"], lp["s5_gb"])
    return pl.pallas_call(
        functools.partial(_s5_body, steps=steps, batch_major=batch_major),
        grid=(batch // bsub, seq // steps),
        in_specs=[tspec, hspec, hspec] + [_pspec(a) for a in consts],
        out_specs=[tspec, hspec, hspec],
        out_shape=[jax.ShapeDtypeStruct(u.shape, F32),
                   jax.ShapeDtypeStruct((batch, S5_WIDTH), F32),
                   jax.ShapeDtypeStruct((batch, S5_WIDTH), F32)],
        scratch_shapes=[pltpu.VMEM((steps * bsub, 2 * S5_WIDTH), F32),
                        pltpu.VMEM((steps, bsub, GROUP_WIDTH), F32)],
        compiler_params=_cparams("parallel", "arbitrary"),
        name="s5",
    )(u, hre0, him0, *[_parg(a) for a in consts])


def _pool_body(u_ref, buf0_ref, pw_ref, sc_ref, y_ref, buf_ref, f_scr, tm_scr, *, steps, pos0, batch_major):
    c = pl.program_id(1)
    bsub = SUBLANES
    GW = GROUP_WIDTH
    halo = POOL_BUF + 1

    @pl.when(c == 0)
    def _():
        f_scr[0] = jnp.zeros((bsub, GW), F32)
        f_scr[1:halo] = buf0_ref[...]

    if batch_major:
        for b in range(bsub):
            f_scr[halo:halo + steps, b, :] = u_ref[b]
    else:
        f_scr[halo:halo + steps] = u_ref[...]
    f = f_scr[...]
    u = f[halo:halo + steps]
    s2 = f[1:] + f[:-1]
    s4 = s2[2:] + s2[:-2]
    s8 = s4[4:] + s4[:-4]
    s16 = s8[8:] + s8[:-8]
    f_scr[0:halo] = f[steps:steps + halo]
    lane = lax.broadcasted_iota(jnp.int32, (steps, bsub, GW), 2)
    tpos = lax.broadcasted_iota(jnp.int32, (steps, bsub, GW), 0) + (pos0 + 1) + c * steps
    win = jnp.where(lane < POOL_CH, s2[halo - 1:halo - 1 + steps],
                    jnp.where(lane < 2 * POOL_CH, s4[halo - 3:halo - 3 + steps],
                              jnp.where(lane < 3 * POOL_CH, s8[halo - 7:halo - 7 + steps],
                                        s16[halo - 15:halo - 15 + steps])))
    wlen = jnp.where(lane < POOL_CH, POOL_WINDOWS[0],
                     jnp.where(lane < 2 * POOL_CH, POOL_WINDOWS[1],
                               jnp.where(lane < 3 * POOL_CH, POOL_WINDOWS[2], POOL_WINDOWS[3])))
    cnt = jnp.minimum(tpos, wlen).astype(F32)
    pooled = (win / cnt - u).reshape(steps * bsub, GW)
    y = (_dot(pooled.astype(BF16), pw_ref[...]) * sc_ref[...]).reshape(steps, bsub, GW)
    if batch_major:
        tm_scr[...] = y
        for b in range(bsub):
            y_ref[b] = tm_scr[:, b, :]
    else:
        y_ref[...] = y

    @pl.when(c == pl.num_programs(1) - 1)
    def _():
        buf_ref[...] = f_scr[1:halo]


def _pool(u, buf0, lp, *, pos0, batch_major, layer=None):
    batch, seq, steps, tspec = _time_specs(u, batch_major, POOL_CHUNK)
    bsub = SUBLANES
    bblock =(POOL_BUF, bsub, GROUP_WIDTH)
    bspec = pl.BlockSpec(bblock, lambda b, c: (0, b, 0))
    if layer is None:
        bspec_in = bspec
    else:
        bspec_in = pl.BlockSpec((None,) + bblock, lambda b, c: (layer, 0, b, 0))
    return pl.pallas_call(
        functools.partial(_pool_body, steps=steps, pos0=pos0, batch_major=batch_major),
        grid=(batch // bsub, seq // steps),
        in_specs=[tspec, bspec_in, _pspec(lp["pool_w"]), _pspec(lp["pool_scale"])],
        out_specs=[tspec, bspec],
        out_shape=[jax.ShapeDtypeStruct(u.shape, F32), jax.ShapeDtypeStruct((POOL_BUF, batch, GROUP_WIDTH), F32)],
        scratch_shapes=[pltpu.VMEM((POOL_BUF + 1 + steps, bsub, GROUP_WIDTH), F32),
                        pltpu.VMEM((steps, bsub, GROUP_WIDTH), F32)],
        compiler_params=_cparams("parallel", "arbitrary"),
        name="pool",
    )(u, buf0, _parg(lp["pool_w"]), _parg(lp["pool_scale"]))


def _block_diag(blocks):
    n, g, r, c = blocks.shape
    eye = jnp.eye(g, dtype=blocks.dtype)
    return (eye[None, :, None, :, None] * blocks[:, :, :, None, :]).reshape(n, g * r, g * c)


def _stacked_params(P):
    row = lambda a: a.reshape(a.shape[0], 1, -1)
    pad_lanes = lambda a: jnp.pad(a, ((0, 0), (0, LANES - a.shape[1])))
    bf = lambda a: a.astype(BF16)

    lam = lax.complex(P["s5_lam_re"], P["s5_lam_im"])
    a_bar = jnp.exp(lam * jnp.exp(P["s5_log_step"])[..., None])
    b_bar = ((a_bar - 1.0) / lam)[..., None] * lax.complex(P["s5_b_re"], P["s5_b_im"])
    b_t = jnp.swapaxes(b_bar, 2, 3)
    bmat = jnp.concatenate([_block_diag(jnp.real(b_t)), _block_diag(jnp.imag(b_t))], axis=2)
    c_t = jnp.swapaxes(lax.complex(P["s5_c_re"], P["s5_c_im"]), 2, 3)
    cmat = jnp.concatenate([_block_diag(jnp.real(c_t)), -_block_diag(jnp.imag(c_t))], axis=1)

    out = dict(
        norm_ffn1=row(P["norm_ffn1"]), ffn1_in=P["ffn1_in"], ffn1_out=P["ffn1_out"],
        norm_mix=row(P["norm_mix"]),
        w_in=jnp.transpose(P["w_in"], (2, 0, 1)),
        conv_w=P["ssd_conv_w"], conv_b=row(P["ssd_conv_b"]),
        dt_bias=row(pad_lanes(P["ssd_dt_bias"])), a_log=row(pad_lanes(P["ssd_a_log"])),
        a_neg_exp=row(jnp.repeat(-jnp.exp(P["ssd_a_log"]), SSD_HEAD_DIM, axis=1)),
        d_skip=row(jnp.repeat(P["ssd_d"], SSD_HEAD_DIM, axis=1)), ssd_norm=row(P["ssd_norm"]),
        s5_are=row(jnp.real(a_bar)), s5_aim=row(jnp.imag(a_bar)), s5_bmat=bf(bmat), s5_cmat=bf(cmat),
        s5_d=row(P["s5_d"]), s5_gw=bf(P["s5_glu_w"]), s5_gb=row(P["s5_glu_b"]),
        pool_w=bf(_block_diag(P["pool_w"])), pool_scale=row(P["pool_scale"]),
        w_out=bf(P["w_out"]),
        norm_ffn2=row(P["norm_ffn2"]), ffn2_in=P["ffn2_in"], ffn2_out=P["ffn2_out"],
    )
    for name in _RWKV_PARAM_NAMES:
        a = P["rwkv_" + name]
        out["rwkv_" + name] = bf(a) if name in ("w2", "a2", "g2") else row(a)
    return out


def _layer_params(stacked, l):
    lp = {k: _Layered((v, l)) for k, v in stacked.items()}
    lp["rwkv"] = {n: lp["rwkv_" + n] for n in _RWKV_PARAM_NAMES}
    lp["head_expand"] = jnp.pad(jnp.repeat(jnp.eye(SSD_HEADS, dtype=F32), SSD_HEAD_DIM, axis=1),
                                ((0, LANES - SSD_HEADS), (0, 0)))
    return lp


def _mixers_prompt(lp, proj, *, batch, seq):
    z, xbc, ur, us5, upool, dtr = proj
    y_ssd, conv_new, ssd_new = _ssd(z, xbc, dtr, lp, batch=batch, seq=seq)
    y_rwkv, shift_new, rwkv_new = _rwkv(ur, lp["rwkv"], batch=batch, seq=seq)
    zeros = jnp.zeros((batch, S5_WIDTH), F32)
    bm = lambda a: a.reshape(batch, seq, a.shape[-1])
    rows = lambda a: a.reshape(batch * seq, a.shape[-1])
    y_s5, s5re, s5im = _s5(bm(us5), zeros, zeros, lp, batch_major=True)
    y_pool, pool_new = _pool(bm(upool), jnp.zeros((POOL_BUF, batch, GROUP_WIDTH), F32), lp, pos0=0,
                             batch_major=True)
    ys = (y_ssd, y_rwkv, rows(y_s5), rows(y_pool))
    states = (conv_new, ssd_new, shift_new, rwkv_new, s5re.reshape(batch, S5_GROUPS, S5_STATE),
              s5im.reshape(batch, S5_GROUPS, S5_STATE), jnp.swapaxes(pool_new, 0, 1))
    return ys, states


def _mixers_decode(lp, proj, states, done, *, batch, seq, layer):
    z, xbc, ur, us5, upool, dtr = proj
    shift0, s5re0, s5im0 = (states[i][layer] for i in (2, 4, 5))
    ssd_done, rwkv_done = (done[1], done[3]) if layer else (None, None)
    y_ssd, conv_new, ssd_new = _ssd_step(z, xbc, dtr, states[0], states[1], ssd_done, lp, batch=batch, seq=seq,
                                         layer=layer)
    y_rwkv, rwkv_new = _rwkv_step(ur, shift0, states[3], rwkv_done, lp["rwkv"], batch=batch, seq=seq, layer=layer)
    shift_new = ur[(seq - 1) * batch:, :]
    tm = lambda a: a.reshape(seq, batch, a.shape[-1])
    y_s5, s5re, s5im = _s5(tm(us5), s5re0.reshape(batch, S5_WIDTH), s5im0.reshape(batch, S5_WIDTH), lp,
                           batch_major=False)
    y_pool, pool_new = _pool(tm(upool), states[6], lp, pos0=PAST_LEN, batch_major=False, layer=layer)
    rows = lambda a: a.reshape(seq * batch, a.shape[-1])
    ys = (y_ssd, y_rwkv, rows(y_s5), rows(y_pool))
    new_states = (jnp.swapaxes(conv_new, 0, 1), ssd_new, shift_new, rwkv_new,
                  s5re.reshape(batch, S5_GROUPS, S5_STATE), s5im.reshape(batch, S5_GROUPS, S5_STATE),
                  jnp.swapaxes(pool_new, 0, 1))
    return ys, new_states


_WIDTHS = (GROUP_WIDTH, SSD_CONV_DIM, RWKV_PROJ, GROUP_WIDTH, GROUP_WIDTH, LANES)


def _trunk(x_p, x_s, layer_params, norm_final, mixers_p, mixers_s):
    st_p, st_s = [], []
    mix_p, mix_s, lp = None, None, None
    for l, lp_next in enumerate(layer_params):
        if l > 0:
            x_s, wg, wu, wo = _ffn_cast(x_s, lp["norm_ffn2"], lp["ffn2_in"], lp["ffn2_out"], mix=mix_s, wmix=lp["w_out"])
            x_p = _ffn(x_p, lp["norm_ffn2"], wg, wu, wo, mix=mix_p, wmix=lp["w_out"])
        lp = lp_next
        x_s, wg, wu, wo = _ffn_cast(x_s, lp["norm_ffn1"], lp["ffn1_in"], lp["ffn1_out"])
        x_p = _ffn(x_p, lp["norm_ffn1"], wg, wu, wo)
        proj_s, w_all = _inproj_cast(x_s, lp["norm_mix"], lp["w_in"], _WIDTHS)
        mix_p, st = mixers_p(l, lp, _inproj(x_p, lp["norm_mix"], w_all, _WIDTHS), st_p[-1] if st_p else None)
        st_p.append(st)
        mix_s, st = mixers_s(l, lp, proj_s, st_s[-1] if st_s else None)
        st_s.append(st)
    x_s, wg, wu, wo = _ffn_cast(x_s, lp["norm_ffn2"], lp["ffn2_in"], lp["ffn2_out"], mix=mix_s, wmix=lp["w_out"],
                                gf=norm_final)
    x_p = _ffn(x_p, lp["norm_ffn2"], wg, wu, wo, mix=mix_p, wmix=lp["w_out"], gf=norm_final)
    return (x_p, x_s), (st_p, st_s)


def kernel(x_prompt, x_sample, state_ssd_conv, state_ssd, state_rwkv_shift, state_rwkv, state_s5_re, state_s5_im, state_pool, norm_ffn1, ffn1_in, ffn1_out, norm_mix, w_in, ssd_conv_w, ssd_conv_b, ssd_dt_bias, ssd_a_log, ssd_d, ssd_norm, rwkv_mu, rwkv_w0, rwkv_w2, rwkv_a0, rwkv_a2, rwkv_g2, rwkv_k_k, rwkv_k_a, rwkv_r_k, rwkv_ln_g, rwkv_ln_b, s5_lam_re, s5_lam_im, s5_log_step, s5_b_re, s5_b_im, s5_c_re, s5_c_im, s5_d, s5_glu_w, s5_glu_b, pool_w, pool_scale, w_out, norm_ffn2, ffn2_in, ffn2_out, norm_final):
    P = dict(norm_ffn1=norm_ffn1, ffn1_in=ffn1_in, ffn1_out=ffn1_out, norm_mix=norm_mix, w_in=w_in,
             ssd_conv_w=ssd_conv_w, ssd_conv_b=ssd_conv_b, ssd_dt_bias=ssd_dt_bias, ssd_a_log=ssd_a_log,
             ssd_d=ssd_d, ssd_norm=ssd_norm, rwkv_mu=rwkv_mu, rwkv_w0=rwkv_w0, rwkv_w2=rwkv_w2, rwkv_a0=rwkv_a0,
             rwkv_a2=rwkv_a2, rwkv_g2=rwkv_g2, rwkv_k_k=rwkv_k_k, rwkv_k_a=rwkv_k_a,
             rwkv_r_k=rwkv_r_k.reshape(rwkv_r_k.shape[0], -1), rwkv_ln_g=rwkv_ln_g, rwkv_ln_b=rwkv_ln_b,
             s5_lam_re=s5_lam_re, s5_lam_im=s5_lam_im, s5_log_step=s5_log_step, s5_b_re=s5_b_re, s5_b_im=s5_b_im,
             s5_c_re=s5_c_re, s5_c_im=s5_c_im, s5_d=s5_d, s5_glu_w=s5_glu_w, s5_glu_b=s5_glu_b, pool_w=pool_w,
             pool_scale=pool_scale, w_out=w_out, norm_ffn2=norm_ffn2, ffn2_in=ffn2_in, ffn2_out=ffn2_out)
    depth = norm_ffn1.shape[0]
    bp, tp, d = x_prompt.shape
    bs, ts, _ = x_sample.shape
    stacked = _stacked_params(P)
    layer_params = [_layer_params(stacked, l) for l in range(depth)]
    gf = norm_final.reshape(1, -1)
    sample_states = (state_ssd_conv, state_ssd, state_rwkv_shift, state_rwkv, state_s5_re, state_s5_im, state_pool)
    rwkv_rows = RWKV_HEADS * RWKV_HEAD
    decode_states = (jnp.swapaxes(state_ssd_conv, 1, 2), state_ssd, state_rwkv_shift,
                     jnp.transpose(state_rwkv, (0, 2, 3, 4, 1)).reshape(depth, rwkv_rows, RWKV_HEAD, bs),
                     state_s5_re, state_s5_im, jnp.swapaxes(state_pool, 1, 2))

    x_s = jnp.swapaxes(x_sample, 0, 1).reshape(ts * bs, d)
    (y_p, y_s), (st_p, st_s) = _trunk(
        x_prompt.reshape(bp * tp, d), x_s, layer_params, gf,
        lambda l, lp, proj, done: _mixers_prompt(lp, proj, batch=bp, seq=tp),
        lambda l, lp, proj, done: _mixers_decode(lp, proj, decode_states, done, batch=bs, seq=ts, layer=l))
    outs = [y_p.reshape(bp, tp, d), jnp.swapaxes(y_s.reshape(ts, bs, d), 0, 1)]
    for i, ref_state in enumerate(sample_states):
        outs.append(jnp.stack([st[i] for st in st_p]))
        if i == 1:
            outs.append(st_s[-1][i].reshape(ref_state.shape))
        elif i == 3:
            s_new = st_s[-1][i].reshape(depth, RWKV_HEADS, RWKV_HEAD, RWKV_HEAD, bs)
            outs.append(jnp.transpose(s_new, (0, 4, 1, 2, 3)))
        else:
            outs.append(jnp.stack([st[i] for st in st_s]))
    return tuple(outs)
```

```python
import functools
import math

import jax
import jax.numpy as jnp
from jax import lax
from jax.experimental import pallas as pl
from jax.experimental.pallas import tpu as pltpu

F32 = jnp.float32
BF16 = jnp.bfloat16
HIGHEST = lax.Precision.HIGHEST

SUBLANES = 8
LANES = 128
VMEM_LIMIT_BYTES = 56 * 1024 * 1024

GROUP_WIDTH = 256
SSD_HEAD_DIM = 64
SSD_HEADS = 4
SSD_GROUPS = 2
SSD_STATE = 128
SSD_CONV = 4
SSD_CONV_DIM = GROUP_WIDTH + 2 * SSD_GROUPS * SSD_STATE
SSD_CHUNK = 128
SSD_GROUP = 4
LOG2_E = math.log2(math.e)
RWKV_HEAD = 64
RWKV_HEADS = 4
RWKV_PROJ = 1024
RWKV_LN_EPS = 64e-5
RWKV_CHUNK = 64
RWKV_GROUP = 8
S5_GROUP_CH = 16
S5_GROUPS = 16
S5_STATE = 64
S5_WIDTH = S5_GROUPS * S5_STATE
POOL_WINDOWS = (2, 4, 8, 16)
POOL_CH = 64
POOL_BUF = 15
RMS_EPS = 1e-6
PAST_LEN = 16384

ROW_TILE = 512
FFN_CHUNK = 256
TM_CHUNK = 64
POOL_CHUNK = 256
S5_SUB = 16
SSD_STEP_TILES = 16
RWKV_STEP_TILES = 16


def _cparams(*sem):
    return pltpu.CompilerParams(dimension_semantics=sem, vmem_limit_bytes=VMEM_LIMIT_BYTES)


def _dot(a, b, **kw):
    return jnp.dot(a, b, preferred_element_type=F32, **kw)


def _dot_nt(a, b):
    return lax.dot_general(a, b, (((1,), (1,)), ((), ())), preferred_element_type=F32)


def _dot_tn(a, b):
    return lax.dot_general(a, b, (((0,), (0,)), ((), ())), preferred_element_type=F32)


def _sigmoid(x):
    return 1.0 / (1.0 + jnp.exp(-x))


def _silu(x):
    return x * _sigmoid(x)


def _softplus(x):
    return jnp.maximum(x, 0.0) + jnp.log(1.0 + jnp.exp(-jnp.abs(x)))


def _gelu_tanh(x):
    c = math.sqrt(2.0 / math.pi)
    return x * (0.5 * (1.0 + jnp.tanh(c * (x + 0.044715 * (x * x * x)))))


def _rms(x, g):
    return x * lax.rsqrt(jnp.mean(x * x, axis=-1, keepdims=True) + RMS_EPS) * g


def _full_spec(shape):
    n = len(shape)
    return pl.BlockSpec(shape, lambda *_: (0,) * n)


class _Layered(tuple):
    pass


def _pspec(p, single=False):
    mode = pl.Buffered(1) if single else None
    if isinstance(p, _Layered):
        a, l = p
        return pl.BlockSpec((None,) + a.shape[1:], lambda *_: (l,) + (0,) * (a.ndim - 1), pipeline_mode=mode)
    n = p.ndim
    return pl.BlockSpec(p.shape, lambda *_: (0,) * n, pipeline_mode=mode)


def _parg(p):
    return p[0] if isinstance(p, _Layered) else p


def _mix_residual(x, y_refs, wmix_ref):
    for j, y_ref in enumerate(y_refs):
        x = x + _dot(y_ref[...].astype(BF16), wmix_ref[j * GROUP_WIDTH:(j + 1) * GROUP_WIDTH, :])
    return x


def _swiglu_chunk(h, wg, wu, wo):
    act = (_silu(_dot(h, wg)) * _dot(h, wu)).astype(BF16)
    return _dot(act, wo)


def _ffn_body(*refs, has_mix, final_norm):
    it = iter(refs)
    x = next(it)[...]
    if has_mix:
        y_refs = [next(it) for _ in range(4)]
        x = _mix_residual(x, y_refs, next(it))
    g_ref, wg_ref, wu_ref, wo_ref = next(it), next(it), next(it), next(it)
    gf_ref = next(it) if final_norm else None
    o_ref = next(it)
    h = _rms(x, g_ref[...]).astype(BF16)
    acc = jnp.zeros_like(x)
    for c in range(wo_ref.shape[0] // FFN_CHUNK):
        cols = slice(c * FFN_CHUNK, (c + 1) * FFN_CHUNK)
        acc = acc + _swiglu_chunk(h, wg_ref[:, cols], wu_ref[:, cols], wo_ref[cols, :])
    x = x + 0.5 * acc
    if final_norm:
        x = _rms(x, gf_ref[...])
    o_ref[...] = x


def _ffn(x, g, wg, wu, wo, mix=None, wmix=None, gf=None):
    rows, d = x.shape
    row_spec = lambda w: pl.BlockSpec((ROW_TILE, w), lambda i: (i, 0))
    args, specs = [x], [row_spec(d)]
    if mix is not None:
        for y in mix:
            args.append(y)
            specs.append(row_spec(y.shape[1]))
        args.append(_parg(wmix))
        specs.append(_pspec(wmix, single=True))
    for a in (g, wg, wu, wo) + ((gf,) if gf is not None else ()):
        args.append(_parg(a))
        specs.append(_pspec(a, single=True))
    return pl.pallas_call(
        functools.partial(_ffn_body, has_mix=mix is not None, final_norm=gf is not None),
        grid=(rows // ROW_TILE,),
        in_specs=specs,
        out_specs=row_spec(d),
        out_shape=jax.ShapeDtypeStruct((rows, d), F32),
        compiler_params=_cparams("parallel"),
        name="ffn",
    )(*args)


def _ffn_cast_body(*refs, has_mix, final_norm):
    it = iter(refs)
    x_ref = next(it)
    if has_mix:
        y_refs = [next(it) for _ in range(4)]
        wmix_ref = next(it)
    g_ref, wg_ref, wu_ref, wo_ref = next(it), next(it), next(it), next(it)
    gf_ref = next(it) if final_norm else None
    o_ref, wg_out, wu_out, wo_out, x_scr, h_scr, acc_scr = (next(it) for _ in range(7))
    c = pl.program_id(0)

    @pl.when(c == 0)
    def _():
        x = x_ref[...]
        if has_mix:
            x = _mix_residual(x, y_refs, wmix_ref)
        x_scr[...] = x
        h_scr[...] = _rms(x, g_ref[...]).astype(BF16)
        acc_scr[...] = jnp.zeros(acc_scr.shape, F32)

    wg = wg_ref[...].astype(BF16)
    wu = wu_ref[...].astype(BF16)
    wo = wo_ref[...].astype(BF16)
    wg_out[...] = wg
    wu_out[...] = wu
    wo_out[...] = wo
    acc_scr[...] += _swiglu_chunk(h_scr[...], wg, wu, wo)

    @pl.when(c == pl.num_programs(0) - 1)
    def _():
        x = x_scr[...] + 0.5 * acc_scr[...]
        if final_norm:
            x = _rms(x, gf_ref[...])
        o_ref[...] = x


def _ffn_cast(x, g, wi, wo, mix=None, wmix=None, gf=None):
    rows, d = x.shape
    wi_all, l = wi
    wo_all, _ = wo
    d_ff = wo_all.shape[1]
    nchunks = d_ff // FFN_CHUNK
    args, specs = [x], [_full_spec(x.shape)]
    if mix is not None:
        for y in mix:
            args.append(y)
            specs.append(_full_spec(y.shape))
        args.append(_parg(wmix))
        specs.append(_pspec(wmix, single=True))
    args += [_parg(g), wi_all, wi_all, wo_all]
    specs += [_pspec(g),
              pl.BlockSpec((None, d, FFN_CHUNK), lambda c: (l, 0, c)),
              pl.BlockSpec((None, d, FFN_CHUNK), lambda c: (l, 0, c + nchunks)),
              pl.BlockSpec((None, FFN_CHUNK, d), lambda c: (l, c, 0))]
    if gf is not None:
        args.append(gf)
        specs.append(_full_spec(gf.shape))
    col_spec = pl.BlockSpec((d, FFN_CHUNK), lambda c: (0, c))
    return pl.pallas_call(
        functools.partial(_ffn_cast_body, has_mix=mix is not None, final_norm=gf is not None),
        grid=(nchunks,),
        in_specs=specs,
        out_specs=[_full_spec(x.shape), col_spec, col_spec, pl.BlockSpec((FFN_CHUNK, d), lambda c: (c, 0))],
        out_shape=[jax.ShapeDtypeStruct((rows, d), F32), jax.ShapeDtypeStruct((d, d_ff), BF16),
                   jax.ShapeDtypeStruct((d, d_ff), BF16), jax.ShapeDtypeStruct((d_ff, d), BF16)],
        scratch_shapes=[pltpu.VMEM((rows, d), F32), pltpu.VMEM((rows, d), BF16), pltpu.VMEM((rows, d), F32)],
        compiler_params=_cparams("arbitrary"),
        name="ffn_cast",
    )(*args)


def _inproj_body(x_ref, g_ref, wt_ref, *o_refs):
    h = _rms(x_ref[...], g_ref[...]).astype(BF16)
    off = 0
    for o_ref in o_refs:
        n = o_ref.shape[-1]
        o_ref[...] = _dot_nt(h, wt_ref[off:off + n, :])
        off += n


def _inproj_cast_body(x_ref, g_ref, win_ref, *o_refs, layer):
    *proj_refs, wall_ref = o_refs
    split = GROUP_WIDTH + SSD_CONV_DIM
    wt = win_ref[:, layer, :]
    tail = wt.shape[0] - split - SSD_HEADS
    wall_ref[0:split, :] = wt[0:split].astype(BF16)
    wall_ref[split:split + tail, :] = wt[split + SSD_HEADS:].astype(BF16)
    dt_rows = jnp.concatenate([wt[split:split + SSD_HEADS], jnp.zeros((LANES - SSD_HEADS, wt.shape[1]), F32)], axis=0)
    wall_ref[split + tail:, :] = dt_rows.astype(BF16)
    _inproj_body(x_ref, g_ref, wall_ref, *proj_refs)


def _inproj_cast(x, g, w_in, widths):
    rows, d = x.shape
    wt_all, l = w_in
    outs = pl.pallas_call(
        functools.partial(_inproj_cast_body, layer=l),
        grid=(1,),
        in_specs=[_full_spec(x.shape), _pspec(g),
                  pl.BlockSpec(wt_all.shape, lambda i: (0, 0, 0), pipeline_mode=pl.Buffered(1))],
        out_specs=[_full_spec((rows, n)) for n in widths] + [_full_spec((sum(widths), d))],
        out_shape=[jax.ShapeDtypeStruct((rows, n), F32) for n in widths]
                  + [jax.ShapeDtypeStruct((sum(widths), d), BF16)],
        compiler_params=_cparams("arbitrary"),
        name="inproj_cast",
    )(x, _parg(g), wt_all)
    return outs[:-1], outs[-1]


def _inproj(x, g, w, widths):
    rows, d = x.shape
    row_spec = lambda w_: pl.BlockSpec((ROW_TILE, w_), lambda i: (i, 0))
    return pl.pallas_call(
        _inproj_body,
        grid=(rows // ROW_TILE,),
        in_specs=[row_spec(d), _pspec(g), _pspec(w, single=True)],
        out_specs=[row_spec(n) for n in widths],
        out_shape=[jax.ShapeDtypeStruct((rows, n), F32) for n in widths],
        compiler_params=_cparams("parallel"),
        name="inproj",
    )(x, _parg(g), _parg(w))


def _ssd_body(z_ref, xbc_ref, dt_ref, cw_ref, cb_ref, dtb_ref, alog_ref, dsk_ref, ng_ref,
              y_ref, conv_ref, hout_ref, xpad_scr, h_scr, *, chunk, group):
    L, G = chunk, group
    GL = G * L
    c = pl.program_id(1)
    pad = SUBLANES
    halo = SSD_CONV - 1
    hpg = SSD_HEADS // SSD_GROUPS
    assert hpg == 2 and hpg * SSD_HEAD_DIM == SSD_STATE

    @pl.when(c == 0)
    def _():
        xpad_scr[0:pad, :] = jnp.zeros((pad, SSD_CONV_DIM), F32)
        h_scr[...] = jnp.zeros(h_scr.shape, F32)

    xpad_scr[pad:pad + GL, :] = xbc_ref[...]
    xfull = xpad_scr[...]
    conv = cb_ref[...] + cw_ref[halo:halo + 1, :] * xfull[pad:pad + GL]
    for j in range(halo):
        conv = conv + cw_ref[j:j + 1, :] * pltpu.roll(xfull, halo - j, axis=0)[pad:pad + GL]
    xpad_scr[pad - halo:pad, :] = xpad_scr[pad + GL - halo:pad + GL, :]
    conv = _silu(conv)
    xs = conv[:, 0:GROUP_WIDTH]
    bm = conv[:, GROUP_WIDTH:2 * GROUP_WIDTH].astype(BF16)
    cm = conv[:, 2 * GROUP_WIDTH:3 * GROUP_WIDTH].astype(BF16)

    row = lax.broadcasted_iota(jnp.int32, (L, L), 0)
    col = lax.broadcasted_iota(jnp.int32, (L, L), 1)
    causal = row >= col
    tril = jnp.where(causal, 1.0, 0.0).astype(F32)
    dt = _softplus(dt_ref[...] + dtb_ref[...])
    da = dt * (-jnp.exp(alog_ref[...]) * LOG2_E)
    acs = [_dot(tril, da[i * L:(i + 1) * L, :], precision=HIGHEST) for i in range(G)]
    acs_t = [a.T for a in acs]
    e_acs = [jnp.exp2(a) for a in acs]
    e_end = [jnp.exp2(a[L - 1:L, :] - a) for a in acs]
    e_last = [jnp.exp2(a[L - 1:L, :]) for a in acs]

    keys = [(i, g) for i in range(G) for g in range(SSD_GROUPS)]
    rows_of = lambda x, i: x[i * L:(i + 1) * L]
    lanes_of = lambda x, g: x[:, g * SSD_STATE:(g + 1) * SSD_STATE]
    lane_lo = lax.broadcasted_iota(jnp.int32, (L, hpg * SSD_HEAD_DIM), 1) < SSD_HEAD_DIM
    row_lo = lax.broadcasted_iota(jnp.int32, (hpg * SSD_HEAD_DIM, SSD_STATE), 0) < SSD_HEAD_DIM
    head_cols = lambda a, g: jnp.where(lane_lo, a[:, g * hpg:g * hpg + 1], a[:, g * hpg + 1:g * hpg + 2])
    bg = {(i, g): lanes_of(rows_of(bm, i), g) for i, g in keys}
    cg = {(i, g): lanes_of(rows_of(cm, i), g) for i, g in keys}
    scores = {k: _dot_nt(cg[k], bg[k]) for k in keys}
    xdt = {(i, g): lanes_of(rows_of(xs, i), g) * head_cols(rows_of(dt, i), g) for i, g in keys}
    decay = {(i, h): jnp.exp2(jnp.where(causal, acs[i][:, h:h + 1] - acs_t[i][h:h + 1, :], -jnp.inf))
             for i in range(G) for h in range(SSD_HEADS)}
    p_mat = {(i, g): jnp.concatenate([(scores[(i, g)] * decay[(i, g * hpg + k)]).astype(BF16) for k in range(hpg)],
                                     axis=1) for i, g in keys}
    y_in = {k: _dot(p_mat[k], _bd(xdt[k].astype(BF16))) for k in keys}
    st = {(i, g): _dot_tn((xdt[(i, g)] * head_cols(e_end[i], g)).astype(BF16), bg[(i, g)]) for i, g in keys}

    y_rows = []
    for i in range(G):
        ys = []
        for g in range(SSD_GROUPS):
            h_prev = h_scr[g * hpg:(g + 1) * hpg].reshape(hpg * SSD_HEAD_DIM, SSD_STATE)
            ys.append(y_in[(i, g)] + _dot_nt(cg[(i, g)], h_prev.astype(BF16)) * head_cols(e_acs[i], g))
            keep = jnp.where(row_lo, e_last[i][:, g * hpg:g * hpg + 1], e_last[i][:, g * hpg + 1:g * hpg + 2])
            h_scr[g * hpg:(g + 1) * hpg] = (h_prev * keep + st[(i, g)]).reshape(hpg, SSD_HEAD_DIM, SSD_STATE)
        y_rows.append(jnp.concatenate(ys, axis=-1))
    y = jnp.concatenate(y_rows, axis=0) + xs * dsk_ref[...]
    y = y * _silu(z_ref[...])
    y_ref[...] = _rms(y, ng_ref[...])

    @pl.when(c == pl.num_programs(1) - 1)
    def _():
        hout_ref[0] = h_scr[...]
        conv_ref[0] = xpad_scr[pad - halo:pad, :]


def _ssd(z, xbc, dtr, lp, *, batch, seq):
    chunk = SSD_CHUNK
    rows = chunk * SSD_GROUP
    nc = seq // rows
    rspec = lambda w: pl.BlockSpec((rows, w), lambda b, c: (b * nc + c, 0))
    consts = (lp["conv_w"], lp["conv_b"], lp["dt_bias"], lp["a_log"], lp["d_skip"], lp["ssd_norm"])
    return pl.pallas_call(
        functools.partial(_ssd_body, chunk=chunk, group=SSD_GROUP),
        grid=(batch, nc),
        in_specs=[rspec(GROUP_WIDTH), rspec(SSD_CONV_DIM), rspec(LANES)] + [_pspec(a) for a in consts],
        out_specs=[rspec(GROUP_WIDTH),
                   pl.BlockSpec((1, SSD_CONV - 1, SSD_CONV_DIM), lambda b, c: (b, 0, 0)),
                   pl.BlockSpec((1, SSD_HEADS, SSD_HEAD_DIM, SSD_STATE), lambda b, c: (b, 0, 0, 0))],
        out_shape=[jax.ShapeDtypeStruct((batch * seq, GROUP_WIDTH), F32),
                   jax.ShapeDtypeStruct((batch, SSD_CONV - 1, SSD_CONV_DIM), F32),
                   jax.ShapeDtypeStruct((batch, SSD_HEADS, SSD_HEAD_DIM, SSD_STATE), F32)],
        scratch_shapes=[pltpu.VMEM((SUBLANES + rows, SSD_CONV_DIM), F32),
                        pltpu.VMEM((SSD_HEADS, SSD_HEAD_DIM, SSD_STATE), F32)],
        compiler_params=_cparams("parallel", "arbitrary"),
        name="ssd",
    )(z, xbc, dtr, *[_parg(a) for a in consts])


def _ssd_step_body(z_ref, xbc_ref, dt_ref, conv0_ref, h0_ref, *rest, seq, batch, layer):
    hdone_ref, rest = (rest[0], rest[1:]) if layer else (None, rest)
    (cw_ref, cb_ref, dtb_ref, aneg_ref, dsk_ref, ng_ref, hexp_ref, y_ref, conv_ref, hout_ref,
     xs_scr, bm_scr, cm_scr, xdt_scr, dec_scr, y_scr) = rest
    T, B = seq, batch
    if layer:
        hout_ref[0:layer] = hdone_ref[...]
    GW = GROUP_WIDTH
    j = pl.program_id(0)
    tiles = SSD_STEP_TILES

    @pl.when(j == 0)
    def _():
        rows = [conv0_ref[i] for i in range(SSD_CONV - 1)]
        rows += [xbc_ref[t * B:(t + 1) * B, :] for t in range(T)]
        for t in range(T):
            conv = cb_ref[...] + cw_ref[0:1, :] * rows[t]
            for i in range(1, SSD_CONV):
                conv = conv + cw_ref[i:i + 1, :] * rows[t + i]
            conv = _silu(conv)
            xs = conv[:, 0:GW]
            xs_scr[t] = xs
            for g in range(SSD_GROUPS):
                bm_scr[t, g] = conv[:, GW + g * SSD_STATE:GW + (g + 1) * SSD_STATE].T
                cm_scr[t, g] = conv[:, 2 * GW + g * SSD_STATE:2 * GW + (g + 1) * SSD_STATE].T
            dt = _softplus(dt_ref[t * B:(t + 1) * B, :] + dtb_ref[...])
            dte = _dot(dt, hexp_ref[...], precision=HIGHEST)
            xdt_scr[t] = (xs * dte).T
            dec_scr[t] = jnp.exp(dte * aneg_ref[...]).T
        for i in range(SSD_CONV - 1):
            conv_ref[i] = rows[T + i]

    hp0 = j * tiles
    grp = hp0 // (SSD_HEAD_DIM * (SSD_HEADS // SSD_GROUPS))
    for q in range(tiles):
        hp = pl.ds(hp0 + q, 1)
        h = h0_ref[:, q, :].T
        for t in range(T):
            h = h * dec_scr[t, hp, :] + bm_scr[t, grp] * xdt_scr[t, hp, :]
            y_scr[t, hp, :] = jnp.sum(h * cm_scr[t, grp], axis=0, keepdims=True)
        hout_ref[layer, :, q, :] = h.T

    @pl.when(j == pl.num_programs(0) - 1)
    def _():
        for t in range(T):
            y = y_scr[t].T + xs_scr[t] * dsk_ref[...]
            y = y * _silu(z_ref[t * B:(t + 1) * B, :])
            y_ref[t * B:(t + 1) * B, :] = _rms(y, ng_ref[...])


def _layer_state_specs(layer, block, axis):
    idx = lambda first: (lambda j: (first,) + tuple(j if a == axis else 0 for a in range(len(block))))
    cur = pl.BlockSpec((None,) + block, idx(layer))
    prev = [pl.BlockSpec((layer,) + block, idx(0))] if layer else []
    out = pl.BlockSpec((layer + 1,) + block, idx(0))
    return cur, prev, out


def _ssd_step(z, xbc, dtr, conv_all, h_all, h_done, lp, *, batch, seq, layer):
    n = batch * seq
    srows = SSD_HEADS * SSD_HEAD_DIM
    consts = (lp["conv_w"], lp["conv_b"], lp["dt_bias"], lp["a_neg_exp"], lp["d_skip"], lp["ssd_norm"], lp["head_expand"])
    hspec, prev_specs, hout_spec = _layer_state_specs(layer, (batch, SSD_STEP_TILES, SSD_STATE), 1)
    prev_args = [h_done] if layer else []
    cshape = (SSD_CONV - 1, batch, SSD_CONV_DIM)
    return pl.pallas_call(
        functools.partial(_ssd_step_body, seq=seq, batch=batch, layer=layer),
        grid=(srows // SSD_STEP_TILES,),
        in_specs=[_full_spec((n, GROUP_WIDTH)), _full_spec((n, SSD_CONV_DIM)), _full_spec((n, LANES)),
                  pl.BlockSpec((None,) + cshape, lambda j: (layer, 0, 0, 0)), hspec] + prev_specs
                 + [_pspec(a) for a in consts],
        out_specs=[_full_spec((n, GROUP_WIDTH)), _full_spec(cshape), hout_spec],
        out_shape=[jax.ShapeDtypeStruct((n, GROUP_WIDTH), F32),
                   jax.ShapeDtypeStruct(cshape, F32),
                   jax.ShapeDtypeStruct((layer + 1, batch, srows, SSD_STATE), F32)],
        scratch_shapes=[pltpu.VMEM((seq, batch, GROUP_WIDTH), F32),
                        pltpu.VMEM((seq, SSD_GROUPS, SSD_STATE, batch), F32),
                        pltpu.VMEM((seq, SSD_GROUPS, SSD_STATE, batch), F32),
                        pltpu.VMEM((seq, GROUP_WIDTH, batch), F32),
                        pltpu.VMEM((seq, GROUP_WIDTH, batch), F32),
                        pltpu.VMEM((seq, GROUP_WIDTH, batch), F32)],
        compiler_params=_cparams("arbitrary"),
        name="ssd_step",
    )(z, xbc, dtr, conv_all, h_all.reshape(h_all.shape[0], batch, srows, SSD_STATE),
      *prev_args, *[_parg(a) for a in consts])


PAIR = 2 * RWKV_HEAD
RWKV_PAIRS = RWKV_HEADS // 2


def _bd(x):
    half = x.shape[1] // 2
    lane = lax.broadcasted_iota(jnp.int32, x.shape, 1)
    zero = jnp.zeros_like(x)
    return jnp.concatenate([jnp.where(lane < half, x, zero), jnp.where(lane >= half, x, zero)], axis=0)


def _half_sums(x, lo):
    s_lo = jnp.sum(jnp.where(lo, x, 0.0), axis=-1, keepdims=True)
    s_hi = jnp.sum(jnp.where(lo, 0.0, x), axis=-1, keepdims=True)
    return jnp.where(lo, s_lo, s_hi)


def _head_sum(x):
    lo = lax.broadcasted_iota(jnp.int32, (x.shape[0], PAIR), 1) < RWKV_HEAD
    return jnp.concatenate([_half_sums(x[:, p * PAIR:(p + 1) * PAIR], lo) for p in range(RWKV_PAIRS)], axis=-1)


def _rwkv_pointwise(u, prev, mu_ref, w0_ref, w2_ref, a0_ref, a2_ref, g2_ref, kk_ref, ka_ref):
    GW = GROUP_WIDTH
    xs = u + (prev - u) * mu_ref[...]
    r = xs[:, 0:GW]
    k = xs[:, GW:2 * GW]
    v = xs[:, 2 * GW:3 * GW]
    wd = xs[:, 3 * GW:3 * GW + 64]
    ad = xs[:, 3 * GW + 64:3 * GW + 128]
    gd = xs[:, 3 * GW + 128:3 * GW + 256]
    w_lin = w0_ref[...] + _dot(jnp.tanh(wd).astype(BF16), w2_ref[...])
    logdecay = -math.exp(-0.5) * _sigmoid(w_lin)
    a = _sigmoid(a0_ref[...] + _dot(ad.astype(BF16), a2_ref[...]))
    g = _dot(_sigmoid(gd).astype(BF16), g2_ref[...])
    kk = k * kk_ref[...]
    kk = kk * lax.rsqrt(jnp.maximum(_head_sum(kk * kk), 1e-24))
    k = k * (1.0 + (a - 1.0) * ka_ref[...])
    return r, k, v, logdecay, a, g, kk


def _rwkv_finish(y, r, k, v, g, rk_ref, lng_ref, lnb_ref):
    mean = _head_sum(y) * (1.0 / RWKV_HEAD)
    yc = y - mean
    var = _head_sum(yc * yc) * (1.0 / RWKV_HEAD)
    y = yc * lax.rsqrt(var + RWKV_LN_EPS) * lng_ref[...] + lnb_ref[...]
    bonus = _head_sum(r * k * rk_ref[...]) * v
    return (y + bonus) * g


def _rwkv_body(u_ref, mu_ref, w0_ref, w2_ref, a0_ref, a2_ref, g2_ref, kk_ref, ka_ref, rk_ref,
               lng_ref, lnb_ref, y_ref, shift_ref, sout_ref, upad_scr, s_scr, *, chunk, group):
    L, G = chunk, group
    GL = G * L
    c = pl.program_id(1)
    pad = SUBLANES

    @pl.when(c == 0)
    def _():
        upad_scr[0:pad, :] = jnp.zeros((pad, RWKV_PROJ), F32)
        s_scr[...] = jnp.zeros(s_scr.shape, F32)

    u = u_ref[...]
    upad_scr[pad:pad + GL, :] = u
    prev = pltpu.roll(upad_scr[...], 1, axis=0)[pad:pad + GL]
    upad_scr[pad - 1:pad, :] = u[GL - 1:GL, :]
    r, k, v, logdecay, a, g, kk = _rwkv_pointwise(u, prev, mu_ref, w0_ref, w2_ref, a0_ref, a2_ref, g2_ref,
                                                  kk_ref, ka_ref)

    tril = jnp.where(lax.broadcasted_iota(jnp.int32, (L, L), 0) >= lax.broadcasted_iota(jnp.int32, (L, L), 1),
                     1.0, 0.0).astype(F32)
    cl = jnp.concatenate([_dot(tril, logdecay[i * L:(i + 1) * L, :], precision=HIGHEST) for i in range(G)], axis=0)
    e_in = jnp.exp(cl)
    e_inv = jnp.exp(-cl)
    r_t = r * e_in
    r_tb = r_t.astype(BF16)
    a_tb = (-kk * jnp.exp(cl - logdecay)).astype(BF16)
    b_tb = (kk * a * e_inv).astype(BF16)
    k_tb = (k * e_inv).astype(BF16)
    vb = v.astype(BF16)

    row = lax.broadcasted_iota(jnp.int32, (L, PAIR), 0)
    colh = lax.broadcasted_iota(jnp.int32, (L, PAIR), 1) & (RWKV_HEAD - 1)
    strict = row > colh
    incl = row >= colh
    eye_pair = jnp.where(row == colh, 1.0, 0.0).astype(F32)
    lane_lo = lax.broadcasted_iota(jnp.int32, (RWKV_HEAD, PAIR), 1) < RWKV_HEAD
    same_head = (lax.broadcasted_iota(jnp.int32, (PAIR, PAIR), 0) < RWKV_HEAD) == \
                (lax.broadcasted_iota(jnp.int32, (PAIR, PAIR), 1) < RWKV_HEAD)

    streams = [(i, p) for i in range(G) for p in range(RWKV_PAIRS)]
    ns = len(streams)
    blk = lambda x, i, p: x[i * L:(i + 1) * L, p * PAIR:(p + 1) * PAIR]
    lhs = [jnp.concatenate([blk(a_tb, i, p), blk(r_tb, i, p)], axis=0) for i, p in streams]
    m_both = [_dot_nt(lhs[s], jnp.concatenate([_bd(blk(b_tb, i, p)), _bd(blk(k_tb, i, p))], axis=0))
              for s, (i, p) in enumerate(streams)]
    m_ab = [m[:, 0:PAIR] for m in m_both]
    m_ak = [m[:, PAIR:2 * PAIR] for m in m_both]
    n_ab = [jnp.where(strict, m[0:L], 0.0) for m in m_ab]
    m_rb = [jnp.where(incl, m[L:2 * L], 0.0).astype(BF16) for m in m_ab]
    n_ak = [jnp.where(strict, m[0:L], 0.0).astype(BF16) for m in m_ak]
    m_rk = [jnp.where(incl, m[L:2 * L], 0.0).astype(BF16) for m in m_ak]
    tinv = [eye_pair + n for n in n_ab]
    pwb = [n.astype(BF16) for n in n_ab]
    pw = [_dot(x, _bd(x)) for x in pwb]
    for _ in range(int(math.log2(L)) - 2):
        pwb = [x.astype(BF16) for x in pw]
        both = [_dot(jnp.concatenate([pwb[s], tinv[s].astype(BF16)], axis=0), _bd(pwb[s])) for s in range(ns)]
        pw = [x[0:L] for x in both]
        tinv = [tinv[s] + both[s][L:2 * L] for s in range(ns)]
    pwb = [x.astype(BF16) for x in pw]
    tinv = [tinv[s] + _dot(tinv[s].astype(BF16), _bd(pwb[s])) for s in range(ns)]
    tinvb = [x.astype(BF16) for x in tinv]
    nv_mv = [_dot(jnp.concatenate([n_ak[s], m_rk[s]], axis=0), _bd(blk(vb, i, p))) for s, (i, p) in enumerate(streams)]
    wu = [_dot(tinvb[s], jnp.concatenate([_bd(blk(a_tb, i, p)), _bd(nv_mv[s][0:L].astype(BF16))], axis=1))
          for s, (i, p) in enumerate(streams)]
    wub = [x.astype(BF16) for x in wu]
    qy = [_dot(m_rb[s], jnp.concatenate([_bd(wub[s][:, 0:PAIR]), _bd(wub[s][:, PAIR:2 * PAIR])], axis=1))
          for s in range(ns)]
    q = [(blk(r_t, i, p) + qy[s][:, 0:PAIR]).astype(BF16) for s, (i, p) in enumerate(streams)]
    y_loc = [qy[s][:, PAIR:2 * PAIR] + nv_mv[s][L:2 * L] for s in range(ns)]
    zeros_b = jnp.zeros((L, PAIR), BF16)
    mg = [_dot_tn(jnp.concatenate([wub[s], jnp.concatenate([zeros_b, blk(vb, i, p)], axis=1)], axis=0),
                  jnp.concatenate([blk(b_tb, i, p), blk(k_tb, i, p)], axis=0))
          for s, (i, p) in enumerate(streams)]
    p_end = [e_in[(i + 1) * L - 1:(i + 1) * L, p * PAIR:(p + 1) * PAIR] for i, p in streams]
    m_t = [(jnp.where(same_head, mg[s][0:PAIR], 0.0) * p_end[s]).astype(BF16) for s in range(ns)]
    g_t = [jnp.where(lane_lo, mg[s][PAIR:PAIR + RWKV_HEAD], mg[s][PAIR + RWKV_HEAD:2 * PAIR]) * p_end[s]
           for s in range(ns)]

    y_rows = []
    for i in range(G):
        y_pairs = []
        for p in range(RWKV_PAIRS):
            s = i * RWKV_PAIRS + p
            s0 = s_scr[p]
            s0b = s0.astype(BF16)
            y_pairs.append(_dot_nt(q[s], _bd(s0b)) + y_loc[s])
            s_scr[p] = s0 * p_end[s] + _dot(s0b, m_t[s]) + g_t[s]
        y_rows.append(jnp.concatenate(y_pairs, axis=-1))
    y = jnp.concatenate(y_rows, axis=0)
    y_ref[...] = _rwkv_finish(y, r, k, v, g, rk_ref, lng_ref, lnb_ref)

    @pl.when(c == pl.num_programs(1) - 1)
    def _():
        sout_ref[0] = s_scr[...]
        shift_ref[0] = upad_scr[pad - 1:pad, :]


_RWKV_PARAM_NAMES = ("mu", "w0", "w2", "a0", "a2", "g2", "k_k", "k_a", "r_k", "ln_g", "ln_b")


def _rwkv(u, p, *, batch, seq):
    rows = RWKV_CHUNK * RWKV_GROUP
    nc = seq // rows
    params = [p[n] for n in _RWKV_PARAM_NAMES]
    sspec = pl.BlockSpec((1, RWKV_PAIRS, RWKV_HEAD, PAIR), lambda b, c: (b, 0, 0, 0))
    y, shift, s_last = pl.pallas_call(
        functools.partial(_rwkv_body, chunk=RWKV_CHUNK, group=RWKV_GROUP),
        grid=(batch, nc),
        in_specs=[pl.BlockSpec((rows, RWKV_PROJ), lambda b, c: (b * nc + c, 0))] + [_pspec(a) for a in params],
        out_specs=[pl.BlockSpec((rows, GROUP_WIDTH), lambda b, c: (b * nc + c, 0)),
                   pl.BlockSpec((1, 1, RWKV_PROJ), lambda b, c: (b, 0, 0)), sspec],
        out_shape=[jax.ShapeDtypeStruct((batch * seq, GROUP_WIDTH), F32),
                   jax.ShapeDtypeStruct((batch, 1, RWKV_PROJ), F32),
                   jax.ShapeDtypeStruct((batch, RWKV_PAIRS, RWKV_HEAD, PAIR), F32)],
        scratch_shapes=[pltpu.VMEM((SUBLANES + rows, RWKV_PROJ), F32),
                        pltpu.VMEM((RWKV_PAIRS, RWKV_HEAD, PAIR), F32)],
        compiler_params=_cparams("parallel", "arbitrary"),
        name="rwkv",
    )(u, *[_parg(a) for a in params])
    s_last = s_last.reshape(batch, RWKV_PAIRS, RWKV_HEAD, 2, RWKV_HEAD).transpose(0, 1, 3, 2, 4).reshape(
        batch, RWKV_HEADS, RWKV_HEAD, RWKV_HEAD)
    return y, shift.reshape(batch, RWKV_PROJ), s_last


def _rwkv_step_body(u_ref, shift0_ref, s0_ref, *rest, seq, batch, layer):
    sdone_ref, rest = (rest[0], rest[1:]) if layer else (None, rest)
    (mu_ref, w0_ref, w2_ref, a0_ref, a2_ref, g2_ref, kk_ref, ka_ref, rk_ref, lng_ref, lnb_ref, y_ref, sout_ref,
     r_scr, w_scr, k_scr, b_scr, nkk_scr, v_scr, y_scr) = rest
    T, B = seq, batch
    j = pl.program_id(0)
    if layer:
        sout_ref[0:layer] = sdone_ref[...]
    tiles = RWKV_STEP_TILES

    def pointwise(t):
        u = u_ref[t * B:(t + 1) * B, :]
        prev = shift0_ref[...] if t == 0 else u_ref[(t - 1) * B:t * B, :]
        return _rwkv_pointwise(u, prev, mu_ref, w0_ref, w2_ref, a0_ref, a2_ref, g2_ref, kk_ref, ka_ref)

    @pl.when(j == 0)
    def _():
        for t in range(T):
            r, k, v, logdecay, a, _, kk = pointwise(t)
            r_scr[t] = r.T
            w_scr[t] = jnp.exp(logdecay).T
            k_scr[t] = k.T
            b_scr[t] = (kk * a).T
            nkk_scr[t] = (-kk).T
            v_scr[t] = v.T

    i0 = j * tiles
    keys = pl.ds(pl.multiple_of((i0 // RWKV_HEAD) * RWKV_HEAD, RWKV_HEAD), RWKV_HEAD)
    for q in range(tiles):
        vi = pl.ds(i0 + q, 1)
        s = s0_ref[q]
        for t in range(T):
            sa = jnp.sum(s * nkk_scr[t, keys, :], axis=0, keepdims=True)
            s = s * w_scr[t, keys, :] + k_scr[t, keys, :] * v_scr[t, vi, :] + b_scr[t, keys, :] * sa
            y_scr[t, vi, :] = jnp.sum(s * r_scr[t, keys, :], axis=0, keepdims=True)
        sout_ref[layer, q] = s

    @pl.when(j == pl.num_programs(0) - 1)
    def _():
        for t in range(T):
            r, k, v, _, _, g, _ = pointwise(t)
            y_ref[t * B:(t + 1) * B, :] = _rwkv_finish(y_scr[t].T, r, k, v, g, rk_ref, lng_ref, lnb_ref)


def _rwkv_step(u, shift0, s_all, s_done, p, *, batch, seq, layer):
    n = batch * seq
    srows = RWKV_HEADS * RWKV_HEAD
    params = [p[nm] for nm in _RWKV_PARAM_NAMES]
    sspec, prev_specs, sout_spec = _layer_state_specs(layer, (RWKV_STEP_TILES, RWKV_HEAD, batch), 0)
    prev_args = [s_done] if layer else []
    tposed = pltpu.VMEM((seq, GROUP_WIDTH, batch), F32)
    return pl.pallas_call(
        functools.partial(_rwkv_step_body, seq=seq, batch=batch, layer=layer),
        grid=(srows // RWKV_STEP_TILES,),
        in_specs=[_full_spec((n, RWKV_PROJ)), _full_spec((batch, RWKV_PROJ)), sspec] + prev_specs
                 + [_pspec(a) for a in params],
        out_specs=[_full_spec((n, GROUP_WIDTH)), sout_spec],
        out_shape=[jax.ShapeDtypeStruct((n, GROUP_WIDTH), F32),
                   jax.ShapeDtypeStruct((layer + 1, srows, RWKV_HEAD, batch), F32)],
        scratch_shapes=[tposed] * 7,
        compiler_params=_cparams("arbitrary"),
        name="rwkv_step",
    )(u, shift0, s_all, *prev_args, *[_parg(a) for a in params])


def _s5_body(u_ref, hre0_ref, him0_ref, are_ref, aim_ref, bmat_ref, cmat_ref, d_ref, gw_ref, gb_ref,
             y_ref, hre_ref, him_ref, hs_scr, tm_scr, *, steps, batch_major):
    c = pl.program_id(1)
    ns = S5_WIDTH
    bsub = SUBLANES

    @pl.when(c == 0)
    def _():
        hre_ref[...] = hre0_ref[...]
        him_ref[...] = him0_ref[...]

    if batch_major:
        for b in range(bsub):
            tm_scr[:, b, :] = u_ref[b]
        u = tm_scr[...].reshape(steps * bsub, GROUP_WIDTH)
    else:
        u = u_ref[...].reshape(steps * bsub, GROUP_WIDTH)
    are = jnp.broadcast_to(are_ref[...], (bsub, ns))
    aim = jnp.broadcast_to(aim_ref[...], (bsub, ns))
    hre, him = hre_ref[...], him_ref[...]
    sub = min(S5_SUB, steps)
    rows = sub * bsub
    outs = []
    for k in range(steps // sub):
        r0 = k * rows
        u_k = u[r0:r0 + rows]
        hs_scr[r0:r0 + rows, :] = _dot(u_k.astype(BF16), bmat_ref[...])
        for t in range(sub):
            rs = slice(r0 + t * bsub, r0 + (t + 1) * bsub)
            hre, him = (are * hre - aim * him + hs_scr[rs, 0:ns], are * him + aim * hre + hs_scr[rs, ns:2 * ns])
            hs_scr[rs, 0:ns] = hre
            hs_scr[rs, ns:2 * ns] = him
        y = _dot(hs_scr[r0:r0 + rows, :].astype(BF16), cmat_ref[...]) + u_k * d_ref[...]
        y = _gelu_tanh(y)
        yy = _dot(y.astype(BF16), gw_ref[...]) + gb_ref[...]
        outs.append(yy[:, 0:GROUP_WIDTH] * _sigmoid(yy[:, GROUP_WIDTH:2 * GROUP_WIDTH]))
    hre_ref[...] = hre
    him_ref[...] = him
    out = jnp.concatenate(outs, axis=0).reshape(steps, bsub, GROUP_WIDTH)
    if batch_major:
        tm_scr[...] = out
        for b in range(bsub):
            y_ref[b] = tm_scr[:, b, :]
    else:
        y_ref[...] = out


def _time_specs(u, batch_major, chunk):
    bsub = SUBLANES
    if batch_major:
        batch, seq, _ = u.shape
        steps = min(chunk, seq)
        spec = pl.BlockSpec((bsub, steps, GROUP_WIDTH), lambda b, c: (b, c, 0))
    else:
        seq, batch, _ = u.shape
        steps = min(chunk, seq)
        spec = pl.BlockSpec((steps, bsub, GROUP_WIDTH), lambda b, c: (c, b, 0))
    return batch, seq, steps, spec


def _s5(u, hre0, him0, lp, *, batch_major):
    batch, seq, steps, tspec = _time_specs(u, batch_major, TM_CHUNK)
    bsub = SUBLANES
    hspec = pl.BlockSpec((bsub, S5_WIDTH), lambda b, c: (b, 0))
    consts = (lp["s5_are"], lp["s5_aim"], lp["s5_bmat"], lp["s5_cmat"], lp["s5_d"], lp["s5_gw"], lp["s5_gb"])
    return pl.pallas_call(
        functools.partial(_s5_body, steps=steps, batch_major=batch_major),
        grid=(batch // bsub, seq // steps),
        in_specs=[tspec, hspec, hspec] + [_pspec(a) for a in consts],
        out_specs=[tspec, hspec, hspec],
        out_shape=[jax.ShapeDtypeStruct(u.shape, F32),
                   jax.ShapeDtypeStruct((batch, S5_WIDTH), F32),
                   jax.ShapeDtypeStruct((batch, S5_WIDTH), F32)],
        scratch_shapes=[pltpu.VMEM((steps * bsub, 2 * S5_WIDTH), F32),
                        pltpu.VMEM((steps, bsub, GROUP_WIDTH), F32)],
        compiler_params=_cparams("parallel", "arbitrary"),
        name="s5",
    )(u, hre0, him0, *[_parg(a) for a in consts])


def _pool_body(u_ref, buf0_ref, pw_ref, sc_ref, y_ref, buf_ref, f_scr, tm_scr, *, steps, pos0, batch_major):
    c = pl.program_id(1)
    bsub = SUBLANES
    GW = GROUP_WIDTH
    halo = POOL_BUF + 1

    @pl.when(c == 0)
    def _():
        f_scr[0] = jnp.zeros((bsub, GW), F32)
        f_scr[1:halo] = buf0_ref[...]

    if batch_major:
        for b in range(bsub):
            f_scr[halo:halo + steps, b, :] = u_ref[b]
    else:
        f_scr[halo:halo + steps] = u_ref[...]
    f = f_scr[...]
    u = f[halo:halo + steps]
    s2 = f[1:] + f[:-1]
    s4 = s2[2:] + s2[:-2]
    s8 = s4[4:] + s4[:-4]
    s16 = s8[8:] + s8[:-8]
    f_scr[0:halo] = f[steps:steps + halo]
    lane = lax.broadcasted_iota(jnp.int32, (steps, bsub, GW), 2)
    tpos = lax.broadcasted_iota(jnp.int32, (steps, bsub, GW), 0) + (pos0 + 1) + c * steps
    win = jnp.where(lane < POOL_CH, s2[halo - 1:halo - 1 + steps],
                    jnp.where(lane < 2 * POOL_CH, s4[halo - 3:halo - 3 + steps],
                              jnp.where(lane < 3 * POOL_CH, s8[halo - 7:halo - 7 + steps],
                                        s16[halo - 15:halo - 15 + steps])))
    wlen = jnp.where(lane < POOL_CH, POOL_WINDOWS[0],
                     jnp.where(lane < 2 * POOL_CH, POOL_WINDOWS[1],
                               jnp.where(lane < 3 * POOL_CH, POOL_WINDOWS[2], POOL_WINDOWS[3])))
    cnt = jnp.minimum(tpos, wlen).astype(F32)
    pooled = (win / cnt - u).reshape(steps * bsub, GW)
    y = (_dot(pooled.astype(BF16), pw_ref[...]) * sc_ref[...]).reshape(steps, bsub, GW)
    if batch_major:
        tm_scr[...] = y
        for b in range(bsub):
            y_ref[b] = tm_scr[:, b, :]
    else:
        y_ref[...] = y

    @pl.when(c == pl.num_programs(1) - 1)
    def _():
        buf_ref[...] = f_scr[1:halo]


def _pool(u, buf0, lp, *, pos0, batch_major, layer=None):
    batch, seq, steps, tspec = _time_specs(u, batch_major, POOL_CHUNK)
    bsub = SUBLANES
    bblock =(POOL_BUF, bsub, GROUP_WIDTH)
    bspec = pl.BlockSpec(bblock, lambda b, c: (0, b, 0))
    if layer is None:
        bspec_in = bspec
    else:
        bspec_in = pl.BlockSpec((None,) + bblock, lambda b, c: (layer, 0, b, 0))
    return pl.pallas_call(
        functools.partial(_pool_body, steps=steps, pos0=pos0, batch_major=batch_major),
        grid=(batch // bsub, seq // steps),
        in_specs=[tspec, bspec_in, _pspec(lp["pool_w"]), _pspec(lp["pool_scale"])],
        out_specs=[tspec, bspec],
        out_shape=[jax.ShapeDtypeStruct(u.shape, F32), jax.ShapeDtypeStruct((POOL_BUF, batch, GROUP_WIDTH), F32)],
        scratch_shapes=[pltpu.VMEM((POOL_BUF + 1 + steps, bsub, GROUP_WIDTH), F32),
                        pltpu.VMEM((steps, bsub, GROUP_WIDTH), F32)],
        compiler_params=_cparams("parallel", "arbitrary"),
        name="pool",
    )(u, buf0, _parg(lp["pool_w"]), _parg(lp["pool_scale"]))


def _block_diag(blocks):
    n, g, r, c = blocks.shape
    eye = jnp.eye(g, dtype=blocks.dtype)
    return (eye[None, :, None, :, None] * blocks[:, :, :, None, :]).reshape(n, g * r, g * c)


def _stacked_params(P):
    row = lambda a: a.reshape(a.shape[0], 1, -1)
    pad_lanes = lambda a: jnp.pad(a, ((0, 0), (0, LANES - a.shape[1])))
    bf = lambda a: a.astype(BF16)

    lam = lax.complex(P["s5_lam_re"], P["s5_lam_im"])
    a_bar = jnp.exp(lam * jnp.exp(P["s5_log_step"])[..., None])
    b_bar = ((a_bar - 1.0) / lam)[..., None] * lax.complex(P["s5_b_re"], P["s5_b_im"])
    b_t = jnp.swapaxes(b_bar, 2, 3)
    bmat = jnp.concatenate([_block_diag(jnp.real(b_t)), _block_diag(jnp.imag(b_t))], axis=2)
    c_t = jnp.swapaxes(lax.complex(P["s5_c_re"], P["s5_c_im"]), 2, 3)
    cmat = jnp.concatenate([_block_diag(jnp.real(c_t)), -_block_diag(jnp.imag(c_t))], axis=1)

    out = dict(
        norm_ffn1=row(P["norm_ffn1"]), ffn1_in=P["ffn1_in"], ffn1_out=P["ffn1_out"],
        norm_mix=row(P["norm_mix"]),
        w_in=jnp.transpose(P["w_in"], (2, 0, 1)),
        conv_w=P["ssd_conv_w"], conv_b=row(P["ssd_conv_b"]),
        dt_bias=row(pad_lanes(P["ssd_dt_bias"])), a_log=row(pad_lanes(P["ssd_a_log"])),
        a_neg_exp=row(jnp.repeat(-jnp.exp(P["ssd_a_log"]), SSD_HEAD_DIM, axis=1)),
        d_skip=row(jnp.repeat(P["ssd_d"], SSD_HEAD_DIM, axis=1)), ssd_norm=row(P["ssd_norm"]),
        s5_are=row(jnp.real(a_bar)), s5_aim=row(jnp.imag(a_bar)), s5_bmat=bf(bmat), s5_cmat=bf(cmat),
        s5_d=row(P["s5_d"]), s5_gw=bf(P["s5_glu_w"]), s5_gb=row(P["s5_glu_b"]),
        pool_w=bf(_block_diag(P["pool_w"])), pool_scale=row(P["pool_scale"]),
        w_out=bf(P["w_out"]),
        norm_ffn2=row(P["norm_ffn2"]), ffn2_in=P["ffn2_in"], ffn2_out=P["ffn2_out"],
    )
    for name in _RWKV_PARAM_NAMES:
        a = P["rwkv_" + name]
        out["rwkv_" + name] = bf(a) if name in ("w2", "a2", "g2") else row(a)
    return out


def _layer_params(stacked, l):
    lp = {k: _Layered((v, l)) for k, v in stacked.items()}
    lp["rwkv"] = {n: lp["rwkv_" + n] for n in _RWKV_PARAM_NAMES}
    lp["head_expand"] = jnp.pad(jnp.repeat(jnp.eye(SSD_HEADS, dtype=F32), SSD_HEAD_DIM, axis=1),
                                ((0, LANES - SSD_HEADS), (0, 0)))
    return lp


def _mixers_prompt(lp, proj, *, batch, seq):
    z, xbc, ur, us5, upool, dtr = proj
    y_ssd, conv_new, ssd_new = _ssd(z, xbc, dtr, lp, batch=batch, seq=seq)
    y_rwkv, shift_new, rwkv_new = _rwkv(ur, lp["rwkv"], batch=batch, seq=seq)
    zeros = jnp.zeros((batch, S5_WIDTH), F32)
    bm = lambda a: a.reshape(batch, seq, a.shape[-1])
    rows = lambda a: a.reshape(batch * seq, a.shape[-1])
    y_s5, s5re, s5im = _s5(bm(us5), zeros, zeros, lp, batch_major=True)
    y_pool, pool_new = _pool(bm(upool), jnp.zeros((POOL_BUF, batch, GROUP_WIDTH), F32), lp, pos0=0,
                             batch_major=True)
    ys = (y_ssd, y_rwkv, rows(y_s5), rows(y_pool))
    states = (conv_new, ssd_new, shift_new, rwkv_new, s5re.reshape(batch, S5_GROUPS, S5_STATE),
              s5im.reshape(batch, S5_GROUPS, S5_STATE), jnp.swapaxes(pool_new, 0, 1))
    return ys, states


def _mixers_decode(lp, proj, states, done, *, batch, seq, layer):
    z, xbc, ur, us5, upool, dtr = proj
    shift0, s5re0, s5im0 = (states[i][layer] for i in (2, 4, 5))
    ssd_done, rwkv_done = (done[1], done[3]) if layer else (None, None)
    y_ssd, conv_new, ssd_new = _ssd_step(z, xbc, dtr, states[0], states[1], ssd_done, lp, batch=batch, seq=seq,
                                         layer=layer)
    y_rwkv, rwkv_new = _rwkv_step(ur, shift0, states[3], rwkv_done, lp["rwkv"], batch=batch, seq=seq, layer=layer)
    shift_new = ur[(seq - 1) * batch:, :]
    tm = lambda a: a.reshape(seq, batch, a.shape[-1])
    y_s5, s5re, s5im = _s5(tm(us5), s5re0.reshape(batch, S5_WIDTH), s5im0.reshape(batch, S5_WIDTH), lp,
                           batch_major=False)
    y_pool, pool_new = _pool(tm(upool), states[6], lp, pos0=PAST_LEN, batch_major=False, layer=layer)
    rows = lambda a: a.reshape(seq * batch, a.shape[-1])
    ys = (y_ssd, y_rwkv, rows(y_s5), rows(y_pool))
    new_states = (jnp.swapaxes(conv_new, 0, 1), ssd_new, shift_new, rwkv_new,
                  s5re.reshape(batch, S5_GROUPS, S5_STATE), s5im.reshape(batch, S5_GROUPS, S5_STATE),
                  jnp.swapaxes(pool_new, 0, 1))
    return ys, new_states


_WIDTHS = (GROUP_WIDTH, SSD_CONV_DIM, RWKV_PROJ, GROUP_WIDTH, GROUP_WIDTH, LANES)


def _trunk(x_p, x_s, layer_params, norm_final, mixers_p, mixers_s):
    st_p, st_s = [], []
    mix_p, mix_s, lp = None, None, None
    for l, lp_next in enumerate(layer_params):
        if l > 0:
            x_s, wg, wu, wo = _ffn_cast(x_s, lp["norm_ffn2"], lp["ffn2_in"], lp["ffn2_out"], mix=mix_s, wmix=lp["w_out"])
            x_p = _ffn(x_p, lp["norm_ffn2"], wg, wu, wo, mix=mix_p, wmix=lp["w_out"])
        lp = lp_next
        x_s, wg, wu, wo = _ffn_cast(x_s, lp["norm_ffn1"], lp["ffn1_in"], lp["ffn1_out"])
        x_p = _ffn(x_p, lp["norm_ffn1"], wg, wu, wo)
        proj_s, w_all = _inproj_cast(x_s, lp["norm_mix"], lp["w_in"], _WIDTHS)
        mix_p, st = mixers_p(l, lp, _inproj(x_p, lp["norm_mix"], w_all, _WIDTHS), st_p[-1] if st_p else None)
        st_p.append(st)
        mix_s, st = mixers_s(l, lp, proj_s, st_s[-1] if st_s else None)
        st_s.append(st)
    x_s, wg, wu, wo = _ffn_cast(x_s, lp["norm_ffn2"], lp["ffn2_in"], lp["ffn2_out"], mix=mix_s, wmix=lp["w_out"],
                                gf=norm_final)
    x_p = _ffn(x_p, lp["norm_ffn2"], wg, wu, wo, mix=mix_p, wmix=lp["w_out"], gf=norm_final)
    return (x_p, x_s), (st_p, st_s)


def kernel(x_prompt, x_sample, state_ssd_conv, state_ssd, state_rwkv_shift, state_rwkv, state_s5_re, state_s5_im, state_pool, norm_ffn1, ffn1_in, ffn1_out, norm_mix, w_in, ssd_conv_w, ssd_conv_b, ssd_dt_bias, ssd_a_log, ssd_d, ssd_norm, rwkv_mu, rwkv_w0, rwkv_w2, rwkv_a0, rwkv_a2, rwkv_g2, rwkv_k_k, rwkv_k_a, rwkv_r_k, rwkv_ln_g, rwkv_ln_b, s5_lam_re, s5_lam_im, s5_log_step, s5_b_re, s5_b_im, s5_c_re, s5_c_im, s5_d, s5_glu_w, s5_glu_b, pool_w, pool_scale, w_out, norm_ffn2, ffn2_in, ffn2_out, norm_final):
    P = dict(norm_ffn1=norm_ffn1, ffn1_in=ffn1_in, ffn1_out=ffn1_out, norm_mix=norm_mix, w_in=w_in,
             ssd_conv_w=ssd_conv_w, ssd_conv_b=ssd_conv_b, ssd_dt_bias=ssd_dt_bias, ssd_a_log=ssd_a_log,
             ssd_d=ssd_d, ssd_norm=ssd_norm, rwkv_mu=rwkv_mu, rwkv_w0=rwkv_w0, rwkv_w2=rwkv_w2, rwkv_a0=rwkv_a0,
             rwkv_a2=rwkv_a2, rwkv_g2=rwkv_g2, rwkv_k_k=rwkv_k_k, rwkv_k_a=rwkv_k_a,
             rwkv_r_k=rwkv_r_k.reshape(rwkv_r_k.shape[0], -1), rwkv_ln_g=rwkv_ln_g, rwkv_ln_b=rwkv_ln_b,
             s5_lam_re=s5_lam_re, s5_lam_im=s5_lam_im, s5_log_step=s5_log_step, s5_b_re=s5_b_re, s5_b_im=s5_b_im,
             s5_c_re=s5_c_re, s5_c_im=s5_c_im, s5_d=s5_d, s5_glu_w=s5_glu_w, s5_glu_b=s5_glu_b, pool_w=pool_w,
             pool_scale=pool_scale, w_out=w_out, norm_ffn2=norm_ffn2, ffn2_in=ffn2_in, ffn2_out=ffn2_out)
    depth = norm_ffn1.shape[0]
    bp, tp, d = x_prompt.shape
    bs, ts, _ = x_sample.shape
    stacked = _stacked_params(P)
    layer_params = [_layer_params(stacked, l) for l in range(depth)]
    gf = norm_final.reshape(1, -1)
    sample_states = (state_ssd_conv, state_ssd, state_rwkv_shift, state_rwkv, state_s5_re, state_s5_im, state_pool)
    rwkv_rows = RWKV_HEADS * RWKV_HEAD
    decode_states = (jnp.swapaxes(state_ssd_conv, 1, 2), state_ssd, state_rwkv_shift,
                     jnp.transpose(state_rwkv, (0, 2, 3, 4, 1)).reshape(depth, rwkv_rows, RWKV_HEAD, bs),
                     state_s5_re, state_s5_im, jnp.swapaxes(state_pool, 1, 2))

    x_s = jnp.swapaxes(x_sample, 0, 1).reshape(ts * bs, d)
    (y_p, y_s), (st_p, st_s) = _trunk(
        x_prompt.reshape(bp * tp, d), x_s, layer_params, gf,
        lambda l, lp, proj, done: _mixers_prompt(lp, proj, batch=bp, seq=tp),
        lambda l, lp, proj, done: _mixers_decode(lp, proj, decode_states, done, batch=bs, seq=ts, layer=l))
    outs = [y_p.reshape(bp, tp, d), jnp.swapaxes(y_s.reshape(ts, bs, d), 0, 1)]
    for i, ref_state in enumerate(sample_states):
        outs.append(jnp.stack([st[i] for st in st_p]))
        if i == 1:
            outs.append(st_s[-1][i].reshape(ref_state.shape))
        elif i == 3:
            s_new = st_s[-1][i].reshape(depth, RWKV_HEADS, RWKV_HEAD, RWKV_HEAD, bs)
            outs.append(jnp.transpose(s_new, (0, 4, 1, 2, 3)))
        else:
            outs.append(jnp.stack([st[i] for st in st_s]))
    return tuple(outs)
```

```python
import functools
import math

import jax
import jax.numpy as jnp
from jax import lax
from jax.experimental import pallas as pl
from jax.experimental.pallas import tpu as pltpu

F32 = jnp.float32
BF16 = jnp.bfloat16
HIGHEST = lax.Precision.HIGHEST

SUBLANES = 8
LANES = 128
VMEM_LIMIT_BYTES = 56 * 1024 * 1024

GROUP_WIDTH = 256
SSD_HEAD_DIM = 64
SSD_HEADS = 4
SSD_GROUPS = 2
SSD_STATE = 128
SSD_CONV = 4
SSD_CONV_DIM = GROUP_WIDTH + 2 * SSD_GROUPS * SSD_STATE
SSD_CHUNK = 128
SSD_GROUP = 4
LOG2_E = math.log2(math.e)
RWKV_HEAD = 64
RWKV_HEADS = 4
RWKV_PROJ = 1024
RWKV_LN_EPS = 64e-5
RWKV_CHUNK = 64
RWKV_GROUP = 8
S5_GROUP_CH = 16
S5_GROUPS = 16
S5_STATE = 64
S5_WIDTH = S5_GROUPS * S5_STATE
POOL_WINDOWS = (2, 4, 8, 16)
POOL_CH = 64
POOL_BUF = 15
RMS_EPS = 1e-6
PAST_LEN = 16384

ROW_TILE = 512
FFN_CHUNK = 256
TM_CHUNK = 64
POOL_CHUNK = 256
S5_SUB = 32
SSD_STEP_TILES = 16
RWKV_STEP_TILES = 16


def _cparams(*sem):
    return pltpu.CompilerParams(dimension_semantics=sem, vmem_limit_bytes=VMEM_LIMIT_BYTES)


def _dot(a, b, **kw):
    return jnp.dot(a, b, preferred_element_type=F32, **kw)


def _dot_nt(a, b):
    return lax.dot_general(a, b, (((1,), (1,)), ((), ())), preferred_element_type=F32)


def _dot_tn(a, b):
    return lax.dot_general(a, b, (((0,), (0,)), ((), ())), preferred_element_type=F32)


def _sigmoid(x):
    return 1.0 / (1.0 + jnp.exp(-x))


def _silu(x):
    return x * _sigmoid(x)


def _softplus(x):
    return jnp.maximum(x, 0.0) + jnp.log(1.0 + jnp.exp(-jnp.abs(x)))


def _gelu_tanh(x):
    c = math.sqrt(2.0 / math.pi)
    return x * (0.5 * (1.0 + jnp.tanh(c * (x + 0.044715 * (x * x * x)))))


def _rms(x, g):
    return x * lax.rsqrt(jnp.mean(x * x, axis=-1, keepdims=True) + RMS_EPS) * g


def _full_spec(shape):
    n = len(shape)
    return pl.BlockSpec(shape, lambda *_: (0,) * n)


class _Layered(tuple):
    pass


def _pspec(p, single=False):
    mode = pl.Buffered(1) if single else None
    if isinstance(p, _Layered):
        a, l = p
        return pl.BlockSpec((None,) + a.shape[1:], lambda *_: (l,) + (0,) * (a.ndim - 1), pipeline_mode=mode)
    n = p.ndim
    return pl.BlockSpec(p.shape, lambda *_: (0,) * n, pipeline_mode=mode)


def _parg(p):
    return p[0] if isinstance(p, _Layered) else p


def _mix_residual(x, y_refs, wmix_ref):
    for j, y_ref in enumerate(y_refs):
        x = x + _dot(y_ref[...].astype(BF16), wmix_ref[j * GROUP_WIDTH:(j + 1) * GROUP_WIDTH, :])
    return x


def _swiglu_chunk(h, wg, wu, wo):
    act = (_silu(_dot(h, wg)) * _dot(h, wu)).astype(BF16)
    return _dot(act, wo)


def _ffn_body(*refs, has_mix, final_norm):
    it = iter(refs)
    x = next(it)[...]
    if has_mix:
        y_refs = [next(it) for _ in range(4)]
        x = _mix_residual(x, y_refs, next(it))
    g_ref, wg_ref, wu_ref, wo_ref = next(it), next(it), next(it), next(it)
    gf_ref = next(it) if final_norm else None
    o_ref = next(it)
    h = _rms(x, g_ref[...]).astype(BF16)
    acc = jnp.zeros_like(x)
    for c in range(wo_ref.shape[0] // FFN_CHUNK):
        cols = slice(c * FFN_CHUNK, (c + 1) * FFN_CHUNK)
        acc = acc + _swiglu_chunk(h, wg_ref[:, cols], wu_ref[:, cols], wo_ref[cols, :])
    x = x + 0.5 * acc
    if final_norm:
        x = _rms(x, gf_ref[...])
    o_ref[...] = x


def _ffn(x, g, wg, wu, wo, mix=None, wmix=None, gf=None):
    rows, d = x.shape
    row_spec = lambda w: pl.BlockSpec((ROW_TILE, w), lambda i: (i, 0))
    args, specs = [x], [row_spec(d)]
    if mix is not None:
        for y in mix:
            args.append(y)
            specs.append(row_spec(y.shape[1]))
        args.append(_parg(wmix))
        specs.append(_pspec(wmix, single=True))
    for a in (g, wg, wu, wo) + ((gf,) if gf is not None else ()):
        args.append(_parg(a))
        specs.append(_pspec(a, single=True))
    return pl.pallas_call(
        functools.partial(_ffn_body, has_mix=mix is not None, final_norm=gf is not None),
        grid=(rows // ROW_TILE,),
        in_specs=specs,
        out_specs=row_spec(d),
        out_shape=jax.ShapeDtypeStruct((rows, d), F32),
        compiler_params=_cparams("parallel"),
        name="ffn",
    )(*args)


def _ffn_cast_body(*refs, has_mix, final_norm):
    it = iter(refs)
    x_ref = next(it)
    if has_mix:
        y_refs = [next(it) for _ in range(4)]
        wmix_ref = next(it)
    g_ref, wg_ref, wu_ref, wo_ref = next(it), next(it), next(it), next(it)
    gf_ref = next(it) if final_norm else None
    o_ref, wg_out, wu_out, wo_out, x_scr, h_scr, acc_scr = (next(it) for _ in range(7))
    c = pl.program_id(0)

    @pl.when(c == 0)
    def _():
        x = x_ref[...]
        if has_mix:
            x = _mix_residual(x, y_refs, wmix_ref)
        x_scr[...] = x
        h_scr[...] = _rms(x, g_ref[...]).astype(BF16)
        acc_scr[...] = jnp.zeros(acc_scr.shape, F32)

    wg = wg_ref[...].astype(BF16)
    wu = wu_ref[...].astype(BF16)
    wo = wo_ref[...].astype(BF16)
    wg_out[...] = wg
    wu_out[...] = wu
    wo_out[...] = wo
    acc_scr[...] += _swiglu_chunk(h_scr[...], wg, wu, wo)

    @pl.when(c == pl.num_programs(0) - 1)
    def _():
        x = x_scr[...] + 0.5 * acc_scr[...]
        if final_norm:
            x = _rms(x, gf_ref[...])
        o_ref[...] = x


def _ffn_cast(x, g, wi, wo, mix=None, wmix=None, gf=None):
    rows, d = x.shape
    wi_all, l = wi
    wo_all, _ = wo
    d_ff = wo_all.shape[1]
    nchunks = d_ff // FFN_CHUNK
    args, specs = [x], [_full_spec(x.shape)]
    if mix is not None:
        for y in mix:
            args.append(y)
            specs.append(_full_spec(y.shape))
        args.append(_parg(wmix))
        specs.append(_pspec(wmix, single=True))
    args += [_parg(g), wi_all, wi_all, wo_all]
    specs += [_pspec(g),
              pl.BlockSpec((None, d, FFN_CHUNK), lambda c: (l, 0, c)),
              pl.BlockSpec((None, d, FFN_CHUNK), lambda c: (l, 0, c + nchunks)),
              pl.BlockSpec((None, FFN_CHUNK, d), lambda c: (l, c, 0))]
    if gf is not None:
        args.append(gf)
        specs.append(_full_spec(gf.shape))
    col_spec = pl.BlockSpec((d, FFN_CHUNK), lambda c: (0, c))
    return pl.pallas_call(
        functools.partial(_ffn_cast_body, has_mix=mix is not None, final_norm=gf is not None),
        grid=(nchunks,),
        in_specs=specs,
        out_specs=[_full_spec(x.shape), col_spec, col_spec, pl.BlockSpec((FFN_CHUNK, d), lambda c: (c, 0))],
        out_shape=[jax.ShapeDtypeStruct((rows, d), F32), jax.ShapeDtypeStruct((d, d_ff), BF16),
                   jax.ShapeDtypeStruct((d, d_ff), BF16), jax.ShapeDtypeStruct((d_ff, d), BF16)],
        scratch_shapes=[pltpu.VMEM((rows, d), F32), pltpu.VMEM((rows, d), BF16), pltpu.VMEM((rows, d), F32)],
        compiler_params=_cparams("arbitrary"),
        name="ffn_cast",
    )(*args)


def _inproj_body(x_ref, g_ref, wt_ref, *o_refs):
    h = _rms(x_ref[...], g_ref[...]).astype(BF16)
    off = 0
    for o_ref in o_refs:
        n = o_ref.shape[-1]
        o_ref[...] = _dot_nt(h, wt_ref[off:off + n, :])
        off += n


def _inproj_cast_body(x_ref, g_ref, win_ref, *o_refs, layer):
    *proj_refs, wall_ref = o_refs
    split = GROUP_WIDTH + SSD_CONV_DIM
    wt = win_ref[:, layer, :]
    tail = wt.shape[0] - split - SSD_HEADS
    wall_ref[0:split, :] = wt[0:split].astype(BF16)
    wall_ref[split:split + tail, :] = wt[split + SSD_HEADS:].astype(BF16)
    dt_rows = jnp.concatenate([wt[split:split + SSD_HEADS], jnp.zeros((LANES - SSD_HEADS, wt.shape[1]), F32)], axis=0)
    wall_ref[split + tail:, :] = dt_rows.astype(BF16)
    _inproj_body(x_ref, g_ref, wall_ref, *proj_refs)


def _inproj_cast(x, g, w_in, widths):
    rows, d = x.shape
    wt_all, l = w_in
    outs = pl.pallas_call(
        functools.partial(_inproj_cast_body, layer=l),
        grid=(1,),
        in_specs=[_full_spec(x.shape), _pspec(g),
                  pl.BlockSpec(wt_all.shape, lambda i: (0, 0, 0), pipeline_mode=pl.Buffered(1))],
        out_specs=[_full_spec((rows, n)) for n in widths] + [_full_spec((sum(widths), d))],
        out_shape=[jax.ShapeDtypeStruct((rows, n), F32) for n in widths]
                  + [jax.ShapeDtypeStruct((sum(widths), d), BF16)],
        compiler_params=_cparams("arbitrary"),
        name="inproj_cast",
    )(x, _parg(g), wt_all)
    return outs[:-1], outs[-1]


def _inproj(x, g, w, widths):
    rows, d = x.shape
    row_spec = lambda w_: pl.BlockSpec((ROW_TILE, w_), lambda i: (i, 0))
    return pl.pallas_call(
        _inproj_body,
        grid=(rows // ROW_TILE,),
        in_specs=[row_spec(d), _pspec(g), _pspec(w, single=True)],
        out_specs=[row_spec(n) for n in widths],
        out_shape=[jax.ShapeDtypeStruct((rows, n), F32) for n in widths],
        compiler_params=_cparams("parallel"),
        name="inproj",
    )(x, _parg(g), _parg(w))


def _ssd_body(z_ref, xbc_ref, dt_ref, cw_ref, cb_ref, dtb_ref, alog_ref, dsk_ref, ng_ref,
              y_ref, conv_ref, hout_ref, xpad_scr, h_scr, *, chunk, group):
    L, G = chunk, group
    GL = G * L
    c = pl.program_id(1)
    pad = SUBLANES
    halo = SSD_CONV - 1
    hpg = SSD_HEADS // SSD_GROUPS
    assert hpg == 2 and hpg * SSD_HEAD_DIM == SSD_STATE

    @pl.when(c == 0)
    def _():
        xpad_scr[0:pad, :] = jnp.zeros((pad, SSD_CONV_DIM), F32)
        h_scr[...] = jnp.zeros(h_scr.shape, F32)

    xpad_scr[pad:pad + GL, :] = xbc_ref[...]
    xfull = xpad_scr[...]
    conv = cb_ref[...] + cw_ref[halo:halo + 1, :] * xfull[pad:pad + GL]
    for j in range(halo):
        conv = conv + cw_ref[j:j + 1, :] * pltpu.roll(xfull, halo - j, axis=0)[pad:pad + GL]
    xpad_scr[pad - halo:pad, :] = xpad_scr[pad + GL - halo:pad + GL, :]
    conv = _silu(conv)
    xs = conv[:, 0:GROUP_WIDTH]
    bm = conv[:, GROUP_WIDTH:2 * GROUP_WIDTH].astype(BF16)
    cm = conv[:, 2 * GROUP_WIDTH:3 * GROUP_WIDTH].astype(BF16)

    row = lax.broadcasted_iota(jnp.int32, (L, L), 0)
    col = lax.broadcasted_iota(jnp.int32, (L, L), 1)
    causal = row >= col
    tril = jnp.where(causal, 1.0, 0.0).astype(F32)
    dt = _softplus(dt_ref[...] + dtb_ref[...])
    da = dt * (-jnp.exp(alog_ref[...]) * LOG2_E)
    acs = [_dot(tril, da[i * L:(i + 1) * L, :], precision=HIGHEST) for i in range(G)]
    acs_t = [a.T for a in acs]
    e_acs = [jnp.exp2(a) for a in acs]
    e_end = [jnp.exp2(a[L - 1:L, :] - a) for a in acs]
    e_last = [jnp.exp2(a[L - 1:L, :]) for a in acs]

    keys = [(i, g) for i in range(G) for g in range(SSD_GROUPS)]
    rows_of = lambda x, i: x[i * L:(i + 1) * L]
    lanes_of = lambda x, g: x[:, g * SSD_STATE:(g + 1) * SSD_STATE]
    lane_lo = lax.broadcasted_iota(jnp.int32, (L, hpg * SSD_HEAD_DIM), 1) < SSD_HEAD_DIM
    row_lo = lax.broadcasted_iota(jnp.int32, (hpg * SSD_HEAD_DIM, SSD_STATE), 0) < SSD_HEAD_DIM
    head_cols = lambda a, g: jnp.where(lane_lo, a[:, g * hpg:g * hpg + 1], a[:, g * hpg + 1:g * hpg + 2])
    bg = {(i, g): lanes_of(rows_of(bm, i), g) for i, g in keys}
    cg = {(i, g): lanes_of(rows_of(cm, i), g) for i, g in keys}
    scores = {k: _dot_nt(cg[k], bg[k]) for k in keys}
    xdt = {(i, g): lanes_of(rows_of(xs, i), g) * head_cols(rows_of(dt, i), g) for i, g in keys}
    decay = {(i, h): jnp.exp2(jnp.where(causal, acs[i][:, h:h + 1] - acs_t[i][h:h + 1, :], -jnp.inf))
             for i in range(G) for h in range(SSD_HEADS)}
    p_mat = {(i, g): jnp.concatenate([(scores[(i, g)] * decay[(i, g * hpg + k)]).astype(BF16) for k in range(hpg)],
                                     axis=1) for i, g in keys}
    y_in = {k: _dot(p_mat[k], _bd(xdt[k].astype(BF16))) for k in keys}
    st = {(i, g): _dot_tn((xdt[(i, g)] * head_cols(e_end[i], g)).astype(BF16), bg[(i, g)]) for i, g in keys}

    y_rows = []
    for i in range(G):
        ys = []
        for g in range(SSD_GROUPS):
            h_prev = h_scr[g * hpg:(g + 1) * hpg].reshape(hpg * SSD_HEAD_DIM, SSD_STATE)
            ys.append(y_in[(i, g)] + _dot_nt(cg[(i, g)], h_prev.astype(BF16)) * head_cols(e_acs[i], g))
            keep = jnp.where(row_lo, e_last[i][:, g * hpg:g * hpg + 1], e_last[i][:, g * hpg + 1:g * hpg + 2])
            h_scr[g * hpg:(g + 1) * hpg] = (h_prev * keep + st[(i, g)]).reshape(hpg, SSD_HEAD_DIM, SSD_STATE)
        y_rows.append(jnp.concatenate(ys, axis=-1))
    y = jnp.concatenate(y_rows, axis=0) + xs * dsk_ref[...]
    y = y * _silu(z_ref[...])
    y_ref[...] = _rms(y, ng_ref[...])

    @pl.when(c == pl.num_programs(1) - 1)
    def _():
        hout_ref[0] = h_scr[...]
        conv_ref[0] = xpad_scr[pad - halo:pad, :]


def _ssd(z, xbc, dtr, lp, *, batch, seq):
    chunk = SSD_CHUNK
    rows = chunk * SSD_GROUP
    nc = seq // rows
    rspec = lambda w: pl.BlockSpec((rows, w), lambda b, c: (b * nc + c, 0))
    consts = (lp["conv_w"], lp["conv_b"], lp["dt_bias"], lp["a_log"], lp["d_skip"], lp["ssd_norm"])
    return pl.pallas_call(
        functools.partial(_ssd_body, chunk=chunk, group=SSD_GROUP),
        grid=(batch, nc),
        in_specs=[rspec(GROUP_WIDTH), rspec(SSD_CONV_DIM), rspec(LANES)] + [_pspec(a) for a in consts],
        out_specs=[rspec(GROUP_WIDTH),
                   pl.BlockSpec((1, SSD_CONV - 1, SSD_CONV_DIM), lambda b, c: (b, 0, 0)),
                   pl.BlockSpec((1, SSD_HEADS, SSD_HEAD_DIM, SSD_STATE), lambda b, c: (b, 0, 0, 0))],
        out_shape=[jax.ShapeDtypeStruct((batch * seq, GROUP_WIDTH), F32),
                   jax.ShapeDtypeStruct((batch, SSD_CONV - 1, SSD_CONV_DIM), F32),
                   jax.ShapeDtypeStruct((batch, SSD_HEADS, SSD_HEAD_DIM, SSD_STATE), F32)],
        scratch_shapes=[pltpu.VMEM((SUBLANES + rows, SSD_CONV_DIM), F32),
                        pltpu.VMEM((SSD_HEADS, SSD_HEAD_DIM, SSD_STATE), F32)],
        compiler_params=_cparams("parallel", "arbitrary"),
        name="ssd",
    )(z, xbc, dtr, *[_parg(a) for a in consts])


def _ssd_step_body(z_ref, xbc_ref, dt_ref, conv0_ref, h0_ref, *rest, seq, batch, layer):
    hdone_ref, rest = (rest[0], rest[1:]) if layer else (None, rest)
    (cw_ref, cb_ref, dtb_ref, aneg_ref, dsk_ref, ng_ref, hexp_ref, y_ref, conv_ref, hout_ref,
     xs_scr, bm_scr, cm_scr, xdt_scr, dec_scr, y_scr) = rest
    T, B = seq, batch
    if layer:
        hout_ref[0:layer] = hdone_ref[...]
    GW = GROUP_WIDTH
    j = pl.program_id(0)
    tiles = SSD_STEP_TILES

    @pl.when(j == 0)
    def _():
        rows = [conv0_ref[i] for i in range(SSD_CONV - 1)]
        rows += [xbc_ref[t * B:(t + 1) * B, :] for t in range(T)]
        for t in range(T):
            conv = cb_ref[...] + cw_ref[0:1, :] * rows[t]
            for i in range(1, SSD_CONV):
                conv = conv + cw_ref[i:i + 1, :] * rows[t + i]
            conv = _silu(conv)
            xs = conv[:, 0:GW]
            xs_scr[t] = xs
            for g in range(SSD_GROUPS):
                bm_scr[t, g] = conv[:, GW + g * SSD_STATE:GW + (g + 1) * SSD_STATE].T
                cm_scr[t, g] = conv[:, 2 * GW + g * SSD_STATE:2 * GW + (g + 1) * SSD_STATE].T
            dt = _softplus(dt_ref[t * B:(t + 1) * B, :] + dtb_ref[...])
            dte = _dot(dt, hexp_ref[...], precision=HIGHEST)
            xdt_scr[t] = (xs * dte).T
            dec_scr[t] = jnp.exp(dte * aneg_ref[...]).T
        for i in range(SSD_CONV - 1):
            conv_ref[i] = rows[T + i]

    hp0 = j * tiles
    grp = hp0 // (SSD_HEAD_DIM * (SSD_HEADS // SSD_GROUPS))
    for q in range(tiles):
        hp = pl.ds(hp0 + q, 1)
        h = h0_ref[:, q, :].T
        for t in range(T):
            h = h * dec_scr[t, hp, :] + bm_scr[t, grp] * xdt_scr[t, hp, :]
            y_scr[t, hp, :] = jnp.sum(h * cm_scr[t, grp], axis=0, keepdims=True)
        hout_ref[layer, :, q, :] = h.T

    @pl.when(j == pl.num_programs(0) - 1)
    def _():
        for t in range(T):
            y = y_scr[t].T + xs_scr[t] * dsk_ref[...]
            y = y * _silu(z_ref[t * B:(t + 1) * B, :])
            y_ref[t * B:(t + 1) * B, :] = _rms(y, ng_ref[...])


def _layer_state_specs(layer, block, axis):
    idx = lambda first: (lambda j: (first,) + tuple(j if a == axis else 0 for a in range(len(block))))
    cur = pl.BlockSpec((None,) + block, idx(layer))
    prev = [pl.BlockSpec((layer,) + block, idx(0))] if layer else []
    out = pl.BlockSpec((layer + 1,) + block, idx(0))
    return cur, prev, out


def _ssd_step(z, xbc, dtr, conv_all, h_all, h_done, lp, *, batch, seq, layer):
    n = batch * seq
    srows = SSD_HEADS * SSD_HEAD_DIM
    consts = (lp["conv_w"], lp["conv_b"], lp["dt_bias"], lp["a_neg_exp"], lp["d_skip"], lp["ssd_norm"], lp["head_expand"])
    hspec, prev_specs, hout_spec = _layer_state_specs(layer, (batch, SSD_STEP_TILES, SSD_STATE), 1)
    prev_args = [h_done] if layer else []
    cshape = (SSD_CONV - 1, batch, SSD_CONV_DIM)
    return pl.pallas_call(
        functools.partial(_ssd_step_body, seq=seq, batch=batch, layer=layer),
        grid=(srows // SSD_STEP_TILES,),
        in_specs=[_full_spec((n, GROUP_WIDTH)), _full_spec((n, SSD_CONV_DIM)), _full_spec((n, LANES)),
                  pl.BlockSpec((None,) + cshape, lambda j: (layer, 0, 0, 0)), hspec] + prev_specs
                 + [_pspec(a) for a in consts],
        out_specs=[_full_spec((n, GROUP_WIDTH)), _full_spec(cshape), hout_spec],
        out_shape=[jax.ShapeDtypeStruct((n, GROUP_WIDTH), F32),
                   jax.ShapeDtypeStruct(cshape, F32),
                   jax.ShapeDtypeStruct((layer + 1, batch, srows, SSD_STATE), F32)],
        scratch_shapes=[pltpu.VMEM((seq, batch, GROUP_WIDTH), F32),
                        pltpu.VMEM((seq, SSD_GROUPS, SSD_STATE, batch), F32),
                        pltpu.VMEM((seq, SSD_GROUPS, SSD_STATE, batch), F32),
                        pltpu.VMEM((seq, GROUP_WIDTH, batch), F32),
                        pltpu.VMEM((seq, GROUP_WIDTH, batch), F32),
                        pltpu.VMEM((seq, GROUP_WIDTH, batch), F32)],
        compiler_params=_cparams("arbitrary"),
        name="ssd_step",
    )(z, xbc, dtr, conv_all, h_all.reshape(h_all.shape[0], batch, srows, SSD_STATE),
      *prev_args, *[_parg(a) for a in consts])


PAIR = 2 * RWKV_HEAD
RWKV_PAIRS = RWKV_HEADS // 2


def _bd(x):
    half = x.shape[1] // 2
    lane = lax.broadcasted_iota(jnp.int32, x.shape, 1)
    zero = jnp.zeros_like(x)
    return jnp.concatenate([jnp.where(lane < half, x, zero), jnp.where(lane >= half, x, zero)], axis=0)


def _half_sums(x, lo):
    s_lo = jnp.sum(jnp.where(lo, x, 0.0), axis=-1, keepdims=True)
    s_hi = jnp.sum(jnp.where(lo, 0.0, x), axis=-1, keepdims=True)
    return jnp.where(lo, s_lo, s_hi)


def _head_sum(x):
    lo = lax.broadcasted_iota(jnp.int32, (x.shape[0], PAIR), 1) < RWKV_HEAD
    return jnp.concatenate([_half_sums(x[:, p * PAIR:(p + 1) * PAIR], lo) for p in range(RWKV_PAIRS)], axis=-1)


def _rwkv_pointwise(u, prev, mu_ref, w0_ref, w2_ref, a0_ref, a2_ref, g2_ref, kk_ref, ka_ref):
    GW = GROUP_WIDTH
    xs = u + (prev - u) * mu_ref[...]
    r = xs[:, 0:GW]
    k = xs[:, GW:2 * GW]
    v = xs[:, 2 * GW:3 * GW]
    wd = xs[:, 3 * GW:3 * GW + 64]
    ad = xs[:, 3 * GW + 64:3 * GW + 128]
    gd = xs[:, 3 * GW + 128:3 * GW + 256]
    w_lin = w0_ref[...] + _dot(jnp.tanh(wd).astype(BF16), w2_ref[...])
    logdecay = -math.exp(-0.5) * _sigmoid(w_lin)
    a = _sigmoid(a0_ref[...] + _dot(ad.astype(BF16), a2_ref[...]))
    g = _dot(_sigmoid(gd).astype(BF16), g2_ref[...])
    kk = k * kk_ref[...]
    kk = kk * lax.rsqrt(jnp.maximum(_head_sum(kk * kk), 1e-24))
    k = k * (1.0 + (a - 1.0) * ka_ref[...])
    return r, k, v, logdecay, a, g, kk


def _rwkv_finish(y, r, k, v, g, rk_ref, lng_ref, lnb_ref):
    mean = _head_sum(y) * (1.0 / RWKV_HEAD)
    yc = y - mean
    var = _head_sum(yc * yc) * (1.0 / RWKV_HEAD)
    y = yc * lax.rsqrt(var + RWKV_LN_EPS) * lng_ref[...] + lnb_ref[...]
    bonus = _head_sum(r * k * rk_ref[...]) * v
    return (y + bonus) * g


def _rwkv_body(u_ref, mu_ref, w0_ref, w2_ref, a0_ref, a2_ref, g2_ref, kk_ref, ka_ref, rk_ref,
               lng_ref, lnb_ref, y_ref, shift_ref, sout_ref, upad_scr, s_scr, *, chunk, group):
    L, G = chunk, group
    GL = G * L
    c = pl.program_id(1)
    pad = SUBLANES

    @pl.when(c == 0)
    def _():
        upad_scr[0:pad, :] = jnp.zeros((pad, RWKV_PROJ), F32)
        s_scr[...] = jnp.zeros(s_scr.shape, F32)

    u = u_ref[...]
    upad_scr[pad:pad + GL, :] = u
    prev = pltpu.roll(upad_scr[...], 1, axis=0)[pad:pad + GL]
    upad_scr[pad - 1:pad, :] = u[GL - 1:GL, :]
    r, k, v, logdecay, a, g, kk = _rwkv_pointwise(u, prev, mu_ref, w0_ref, w2_ref, a0_ref, a2_ref, g2_ref,
                                                  kk_ref, ka_ref)

    tril = jnp.where(lax.broadcasted_iota(jnp.int32, (L, L), 0) >= lax.broadcasted_iota(jnp.int32, (L, L), 1),
                     1.0, 0.0).astype(F32)
    cl = jnp.concatenate([_dot(tril, logdecay[i * L:(i + 1) * L, :], precision=HIGHEST) for i in range(G)], axis=0)
    e_in = jnp.exp(cl)
    e_inv = jnp.exp(-cl)
    r_t = r * e_in
    r_tb = r_t.astype(BF16)
    a_tb = (-kk * jnp.exp(cl - logdecay)).astype(BF16)
    b_tb = (kk * a * e_inv).astype(BF16)
    k_tb = (k * e_inv).astype(BF16)
    vb = v.astype(BF16)

    row = lax.broadcasted_iota(jnp.int32, (L, PAIR), 0)
    colh = lax.broadcasted_iota(jnp.int32, (L, PAIR), 1) & (RWKV_HEAD - 1)
    strict = row > colh
    incl = row >= colh
    eye_pair = jnp.where(row == colh, 1.0, 0.0).astype(F32)
    lane_lo = lax.broadcasted_iota(jnp.int32, (RWKV_HEAD, PAIR), 1) < RWKV_HEAD
    same_head = (lax.broadcasted_iota(jnp.int32, (PAIR, PAIR), 0) < RWKV_HEAD) == \
                (lax.broadcasted_iota(jnp.int32, (PAIR, PAIR), 1) < RWKV_HEAD)

    streams = [(i, p) for i in range(G) for p in range(RWKV_PAIRS)]
    ns = len(streams)
    blk = lambda x, i, p: x[i * L:(i + 1) * L, p * PAIR:(p + 1) * PAIR]
    lhs = [jnp.concatenate([blk(a_tb, i, p), blk(r_tb, i, p)], axis=0) for i, p in streams]
    m_both = [_dot_nt(lhs[s], jnp.concatenate([_bd(blk(b_tb, i, p)), _bd(blk(k_tb, i, p))], axis=0))
              for s, (i, p) in enumerate(streams)]
    m_ab = [m[:, 0:PAIR] for m in m_both]
    m_ak = [m[:, PAIR:2 * PAIR] for m in m_both]
    n_ab = [jnp.where(strict, m[0:L], 0.0) for m in m_ab]
    m_rb = [jnp.where(incl, m[L:2 * L], 0.0).astype(BF16) for m in m_ab]
    n_ak = [jnp.where(strict, m[0:L], 0.0).astype(BF16) for m in m_ak]
    m_rk = [jnp.where(incl, m[L:2 * L], 0.0).astype(BF16) for m in m_ak]
    tinv = [eye_pair + n for n in n_ab]
    pwb = [n.astype(BF16) for n in n_ab]
    pw = [_dot(x, _bd(x)) for x in pwb]
    for _ in range(int(math.log2(L)) - 2):
        pwb = [x.astype(BF16) for x in pw]
        both = [_dot(jnp.concatenate([pwb[s], tinv[s].astype(BF16)], axis=0), _bd(pwb[s])) for s in range(ns)]
        pw = [x[0:L] for x in both]
        tinv = [tinv[s] + both[s][L:2 * L] for s in range(ns)]
    pwb = [x.astype(BF16) for x in pw]
    tinv = [tinv[s] + _dot(tinv[s].astype(BF16), _bd(pwb[s])) for s in range(ns)]
    tinvb = [x.astype(BF16) for x in tinv]
    nv_mv = [_dot(jnp.concatenate([n_ak[s], m_rk[s]], axis=0), _bd(blk(vb, i, p))) for s, (i, p) in enumerate(streams)]
    wu = [_dot(tinvb[s], jnp.concatenate([_bd(blk(a_tb, i, p)), _bd(nv_mv[s][0:L].astype(BF16))], axis=1))
          for s, (i, p) in enumerate(streams)]
    wub = [x.astype(BF16) for x in wu]
    qy = [_dot(m_rb[s], jnp.concatenate([_bd(wub[s][:, 0:PAIR]), _bd(wub[s][:, PAIR:2 * PAIR])], axis=1))
          for s in range(ns)]
    q = [(blk(r_t, i, p) + qy[s][:, 0:PAIR]).astype(BF16) for s, (i, p) in enumerate(streams)]
    y_loc = [qy[s][:, PAIR:2 * PAIR] + nv_mv[s][L:2 * L] for s in range(ns)]
    zeros_b = jnp.zeros((L, PAIR), BF16)
    mg = [_dot_tn(jnp.concatenate([wub[s], jnp.concatenate([zeros_b, blk(vb, i, p)], axis=1)], axis=0),
                  jnp.concatenate([blk(b_tb, i, p), blk(k_tb, i, p)], axis=0))
          for s, (i, p) in enumerate(streams)]
    p_end = [e_in[(i + 1) * L - 1:(i + 1) * L, p * PAIR:(p + 1) * PAIR] for i, p in streams]
    m_t = [(jnp.where(same_head, mg[s][0:PAIR], 0.0) * p_end[s]).astype(BF16) for s in range(ns)]
    g_t = [jnp.where(lane_lo, mg[s][PAIR:PAIR + RWKV_HEAD], mg[s][PAIR + RWKV_HEAD:2 * PAIR]) * p_end[s]
           for s in range(ns)]

    y_rows = []
    for i in range(G):
        y_pairs = []
        for p in range(RWKV_PAIRS):
            s = i * RWKV_PAIRS + p
            s0 = s_scr[p]
            s0b = s0.astype(BF16)
            y_pairs.append(_dot_nt(q[s], _bd(s0b)) + y_loc[s])
            s_scr[p] = s0 * p_end[s] + _dot(s0b, m_t[s]) + g_t[s]
        y_rows.append(jnp.concatenate(y_pairs, axis=-1))
    y = jnp.concatenate(y_rows, axis=0)
    y_ref[...] = _rwkv_finish(y, r, k, v, g, rk_ref, lng_ref, lnb_ref)

    @pl.when(c == pl.num_programs(1) - 1)
    def _():
        sout_ref[0] = s_scr[...]
        shift_ref[0] = upad_scr[pad - 1:pad, :]


_RWKV_PARAM_NAMES = ("mu", "w0", "w2", "a0", "a2", "g2", "k_k", "k_a", "r_k", "ln_g", "ln_b")


def _rwkv(u, p, *, batch, seq):
    rows = RWKV_CHUNK * RWKV_GROUP
    nc = seq // rows
    params = [p[n] for n in _RWKV_PARAM_NAMES]
    sspec = pl.BlockSpec((1, RWKV_PAIRS, RWKV_HEAD, PAIR), lambda b, c: (b, 0, 0, 0))
    y, shift, s_last = pl.pallas_call(
        functools.partial(_rwkv_body, chunk=RWKV_CHUNK, group=RWKV_GROUP),
        grid=(batch, nc),
        in_specs=[pl.BlockSpec((rows, RWKV_PROJ), lambda b, c: (b * nc + c, 0))] + [_pspec(a) for a in params],
        out_specs=[pl.BlockSpec((rows, GROUP_WIDTH), lambda b, c: (b * nc + c, 0)),
                   pl.BlockSpec((1, 1, RWKV_PROJ), lambda b, c: (b, 0, 0)), sspec],
        out_shape=[jax.ShapeDtypeStruct((batch * seq, GROUP_WIDTH), F32),
                   jax.ShapeDtypeStruct((batch, 1, RWKV_PROJ), F32),
                   jax.ShapeDtypeStruct((batch, RWKV_PAIRS, RWKV_HEAD, PAIR), F32)],
        scratch_shapes=[pltpu.VMEM((SUBLANES + rows, RWKV_PROJ), F32),
                        pltpu.VMEM((RWKV_PAIRS, RWKV_HEAD, PAIR), F32)],
        compiler_params=_cparams("parallel", "arbitrary"),
        name="rwkv",
    )(u, *[_parg(a) for a in params])
    s_last = s_last.reshape(batch, RWKV_PAIRS, RWKV_HEAD, 2, RWKV_HEAD).transpose(0, 1, 3, 2, 4).reshape(
        batch, RWKV_HEADS, RWKV_HEAD, RWKV_HEAD)
    return y, shift.reshape(batch, RWKV_PROJ), s_last


def _rwkv_step_body(u_ref, shift0_ref, s0_ref, *rest, seq, batch, layer):
    sdone_ref, rest = (rest[0], rest[1:]) if layer else (None, rest)
    (mu_ref, w0_ref, w2_ref, a0_ref, a2_ref, g2_ref, kk_ref, ka_ref, rk_ref, lng_ref, lnb_ref, y_ref, sout_ref,
     r_scr, w_scr, k_scr, b_scr, nkk_scr, v_scr, y_scr) = rest
    T, B = seq, batch
    j = pl.program_id(0)
    if layer:
        sout_ref[0:layer] = sdone_ref[...]
    tiles = RWKV_STEP_TILES

    def pointwise(t):
        u = u_ref[t * B:(t + 1) * B, :]
        prev = shift0_ref[...] if t == 0 else u_ref[(t - 1) * B:t * B, :]
        return _rwkv_pointwise(u, prev, mu_ref, w0_ref, w2_ref, a0_ref, a2_ref, g2_ref, kk_ref, ka_ref)

    @pl.when(j == 0)
    def _():
        for t in range(T):
            r, k, v, logdecay, a, _, kk = pointwise(t)
            r_scr[t] = r.T
            w_scr[t] = jnp.exp(logdecay).T
            k_scr[t] = k.T
            b_scr[t] = (kk * a).T
            nkk_scr[t] = (-kk).T
            v_scr[t] = v.T

    i0 = j * tiles
    keys = pl.ds(pl.multiple_of((i0 // RWKV_HEAD) * RWKV_HEAD, RWKV_HEAD), RWKV_HEAD)
    for q in range(tiles):
        vi = pl.ds(i0 + q, 1)
        s = s0_ref[q]
        for t in range(T):
            sa = jnp.sum(s * nkk_scr[t, keys, :], axis=0, keepdims=True)
            s = s * w_scr[t, keys, :] + k_scr[t, keys, :] * v_scr[t, vi, :] + b_scr[t, keys, :] * sa
            y_scr[t, vi, :] = jnp.sum(s * r_scr[t, keys, :], axis=0, keepdims=True)
        sout_ref[layer, q] = s

    @pl.when(j == pl.num_programs(0) - 1)
    def _():
        for t in range(T):
            r, k, v, _, _, g, _ = pointwise(t)
            y_ref[t * B:(t + 1) * B, :] = _rwkv_finish(y_scr[t].T, r, k, v, g, rk_ref, lng_ref, lnb_ref)


def _rwkv_step(u, shift0, s_all, s_done, p, *, batch, seq, layer):
    n = batch * seq
    srows = RWKV_HEADS * RWKV_HEAD
    params = [p[nm] for nm in _RWKV_PARAM_NAMES]
    sspec, prev_specs, sout_spec = _layer_state_specs(layer, (RWKV_STEP_TILES, RWKV_HEAD, batch), 0)
    prev_args = [s_done] if layer else []
    tposed = pltpu.VMEM((seq, GROUP_WIDTH, batch), F32)
    return pl.pallas_call(
        functools.partial(_rwkv_step_body, seq=seq, batch=batch, layer=layer),
        grid=(srows // RWKV_STEP_TILES,),
        in_specs=[_full_spec((n, RWKV_PROJ)), _full_spec((batch, RWKV_PROJ)), sspec] + prev_specs
                 + [_pspec(a) for a in params],
        out_specs=[_full_spec((n, GROUP_WIDTH)), sout_spec],
        out_shape=[jax.ShapeDtypeStruct((n, GROUP_WIDTH), F32),
                   jax.ShapeDtypeStruct((layer + 1, srows, RWKV_HEAD, batch), F32)],
        scratch_shapes=[tposed] * 7,
        compiler_params=_cparams("arbitrary"),
        name="rwkv_step",
    )(u, shift0, s_all, *prev_args, *[_parg(a) for a in params])


def _s5_body(u_ref, hre0_ref, him0_ref, are_ref, aim_ref, bmat_ref, cmat_ref, d_ref, gw_ref, gb_ref,
             y_ref, hre_ref, him_ref, hs_scr, tm_scr, *, steps, batch_major):
    c = pl.program_id(1)
    ns = S5_WIDTH
    bsub = SUBLANES

    @pl.when(c == 0)
    def _():
        hre_ref[...] = hre0_ref[...]
        him_ref[...] = him0_ref[...]

    if batch_major:
        for b in range(bsub):
            tm_scr[:, b, :] = u_ref[b]
        u = tm_scr[...].reshape(steps * bsub, GROUP_WIDTH)
    else:
        u = u_ref[...].reshape(steps * bsub, GROUP_WIDTH)
    are = jnp.broadcast_to(are_ref[...], (bsub, ns))
    aim = jnp.broadcast_to(aim_ref[...], (bsub, ns))
    hre, him = hre_ref[...], him_ref[...]
    sub = min(S5_SUB, steps)
    rows = sub * bsub
    outs = []
    hs_scr[...] = _dot(u.astype(BF16), bmat_ref[...])
    for k in range(steps // sub):
        r0 = k * rows
        u_k = u[r0:r0 + rows]
        for t in range(sub):
            rs = slice(r0 + t * bsub, r0 + (t + 1) * bsub)
            hre, him = (are * hre - aim * him + hs_scr[rs, 0:ns], are * him + aim * hre + hs_scr[rs, ns:2 * ns])
            hs_scr[rs, 0:ns] = hre
            hs_scr[rs, ns:2 * ns] = him
        y = _dot(hs_scr[r0:r0 + rows, :].astype(BF16), cmat_ref[...]) + u_k * d_ref[...]
        y = _gelu_tanh(y)
        yy = _dot(y.astype(BF16), gw_ref[...]) + gb_ref[...]
        outs.append(yy[:, 0:GROUP_WIDTH] * _sigmoid(yy[:, GROUP_WIDTH:2 * GROUP_WIDTH]))
    hre_ref[...] = hre
    him_ref[...] = him
    out = jnp.concatenate(outs, axis=0).reshape(steps, bsub, GROUP_WIDTH)
    if batch_major:
        tm_scr[...] = out
        for b in range(bsub):
            y_ref[b] = tm_scr[:, b, :]
    else:
        y_ref[...] = out


def _time_specs(u, batch_major, chunk):
    bsub = SUBLANES
    if batch_major:
        batch, seq, _ = u.shape
        steps = min(chunk, seq)
        spec = pl.BlockSpec((bsub, steps, GROUP_WIDTH), lambda b, c: (b, c, 0))
    else:
        seq, batch, _ = u.shape
        steps = min(chunk, seq)
        spec = pl.BlockSpec((steps, bsub, GROUP_WIDTH), lambda b, c: (c, b, 0))
    return batch, seq, steps, spec


def _s5(u, hre0, him0, lp, *, batch_major):
    batch, seq, steps, tspec = _time_specs(u, batch_major, TM_CHUNK)
    bsub = SUBLANES
    hspec = pl.BlockSpec((bsub, S5_WIDTH), lambda b, c: (b, 0))
    consts = (lp["s5_are"], lp["s5_aim"], lp["s5_bmat"], lp["s5_cmat"], lp["s5_d"], lp["s5_gw"], lp["s5_gb"])
    return pl.pallas_call(
        functools.partial(_s5_body, steps=steps, batch_major=batch_major),
        grid=(batch // bsub, seq // steps),
        in_specs=[tspec, hspec, hspec] + [_pspec(a) for a in consts],
        out_specs=[tspec, hspec, hspec],
        out_shape=[jax.ShapeDtypeStruct(u.shape, F32),
                   jax.ShapeDtypeStruct((batch, S5_WIDTH), F32),
                   jax.ShapeDtypeStruct((batch, S5_WIDTH), F32)],
        scratch_shapes=[pltpu.VMEM((steps * bsub, 2 * S5_WIDTH), F32),
                        pltpu.VMEM((steps, bsub, GROUP_WIDTH), F32)],
        compiler_params=_cparams("parallel", "arbitrary"),
        name="s5",
    )(u, hre0, him0, *[_parg(a) for a in consts])


def _pool_body(u_ref, buf0_ref, pw_ref, sc_ref, y_ref, buf_ref, f_scr, tm_scr, *, steps, pos0, batch_major):
    c = pl.program_id(1)
    bsub = SUBLANES
    GW = GROUP_WIDTH
    halo = POOL_BUF + 1

    @pl.when(c == 0)
    def _():
        f_scr[0] = jnp.zeros((bsub, GW), F32)
        f_scr[1:halo] = buf0_ref[...]

    if batch_major:
        for b in range(bsub):
            f_scr[halo:halo + steps, b, :] = u_ref[b]
    else:
        f_scr[halo:halo + steps] = u_ref[...]
    f = f_scr[...]
    u = f[halo:halo + steps]
    s2 = f[1:] + f[:-1]
    s4 = s2[2:] + s2[:-2]
    s8 = s4[4:] + s4[:-4]
    s16 = s8[8:] + s8[:-8]
    f_scr[0:halo] = f[steps:steps + halo]
    lane = lax.broadcasted_iota(jnp.int32, (steps, bsub, GW), 2)
    tpos = lax.broadcasted_iota(jnp.int32, (steps, bsub, GW), 0) + (pos0 + 1) + c * steps
    win = jnp.where(lane < POOL_CH, s2[halo - 1:halo - 1 + steps],
                    jnp.where(lane < 2 * POOL_CH, s4[halo - 3:halo - 3 + steps],
                              jnp.where(lane < 3 * POOL_CH, s8[halo - 7:halo - 7 + steps],
                                        s16[halo - 15:halo - 15 + steps])))
    wlen = jnp.where(lane < POOL_CH, POOL_WINDOWS[0],
                     jnp.where(lane < 2 * POOL_CH, POOL_WINDOWS[1],
                               jnp.where(lane < 3 * POOL_CH, POOL_WINDOWS[2], POOL_WINDOWS[3])))
    cnt = jnp.minimum(tpos, wlen).astype(F32)
    pooled = (win / cnt - u).reshape(steps * bsub, GW)
    y = (_dot(pooled.astype(BF16), pw_ref[...]) * sc_ref[...]).reshape(steps, bsub, GW)
    if batch_major:
        tm_scr[...] = y
        for b in range(bsub):
            y_ref[b] = tm_scr[:, b, :]
    else:
        y_ref[...] = y

    @pl.when(c == pl.num_programs(1) - 1)
    def _():
        buf_ref[...] = f_scr[1:halo]


def _pool(u, buf0, lp, *, pos0, batch_major, layer=None):
    batch, seq, steps, tspec = _time_specs(u, batch_major, POOL_CHUNK)
    bsub = SUBLANES
    bblock =(POOL_BUF, bsub, GROUP_WIDTH)
    bspec = pl.BlockSpec(bblock, lambda b, c: (0, b, 0))
    if layer is None:
        bspec_in = bspec
    else:
        bspec_in = pl.BlockSpec((None,) + bblock, lambda b, c: (layer, 0, b, 0))
    return pl.pallas_call(
        functools.partial(_pool_body, steps=steps, pos0=pos0, batch_major=batch_major),
        grid=(batch // bsub, seq // steps),
        in_specs=[tspec, bspec_in, _pspec(lp["pool_w"]), _pspec(lp["pool_scale"])],
        out_specs=[tspec, bspec],
        out_shape=[jax.ShapeDtypeStruct(u.shape, F32), jax.ShapeDtypeStruct((POOL_BUF, batch, GROUP_WIDTH), F32)],
        scratch_shapes=[pltpu.VMEM((POOL_BUF + 1 + steps, bsub, GROUP_WIDTH), F32),
                        pltpu.VMEM((steps, bsub, GROUP_WIDTH), F32)],
        compiler_params=_cparams("parallel", "arbitrary"),
        name="pool",
    )(u, buf0, _parg(lp["pool_w"]), _parg(lp["pool_scale"]))


def _block_diag(blocks):
    n, g, r, c = blocks.shape
    eye = jnp.eye(g, dtype=blocks.dtype)
    return (eye[None, :, None, :, None] * blocks[:, :, :, None, :]).reshape(n, g * r, g * c)


def _stacked_params(P):
    row = lambda a: a.reshape(a.shape[0], 1, -1)
    pad_lanes = lambda a: jnp.pad(a, ((0, 0), (0, LANES - a.shape[1])))
    bf = lambda a: a.astype(BF16)

    lam = lax.complex(P["s5_lam_re"], P["s5_lam_im"])
    a_bar = jnp.exp(lam * jnp.exp(P["s5_log_step"])[..., None])
    b_bar = ((a_bar - 1.0) / lam)[..., None] * lax.complex(P["s5_b_re"], P["s5_b_im"])
    b_t = jnp.swapaxes(b_bar, 2, 3)
    bmat = jnp.concatenate([_block_diag(jnp.real(b_t)), _block_diag(jnp.imag(b_t))], axis=2)
    c_t = jnp.swapaxes(lax.complex(P["s5_c_re"], P["s5_c_im"]), 2, 3)
    cmat = jnp.concatenate([_block_diag(jnp.real(c_t)), -_block_diag(jnp.imag(c_t))], axis=1)

    out = dict(
        norm_ffn1=row(P["norm_ffn1"]), ffn1_in=P["ffn1_in"], ffn1_out=P["ffn1_out"],
        norm_mix=row(P["norm_mix"]),
        w_in=jnp.transpose(P["w_in"], (2, 0, 1)),
        conv_w=P["ssd_conv_w"], conv_b=row(P["ssd_conv_b"]),
        dt_bias=row(pad_lanes(P["ssd_dt_bias"])), a_log=row(pad_lanes(P["ssd_a_log"])),
        a_neg_exp=row(jnp.repeat(-jnp.exp(P["ssd_a_log"]), SSD_HEAD_DIM, axis=1)),
        d_skip=row(jnp.repeat(P["ssd_d"], SSD_HEAD_DIM, axis=1)), ssd_norm=row(P["ssd_norm"]),
        s5_are=row(jnp.real(a_bar)), s5_aim=row(jnp.imag(a_bar)), s5_bmat=bf(bmat), s5_cmat=bf(cmat),
        s5_d=row(P["s5_d"]), s5_gw=bf(P["s5_glu_w"]), s5_gb=row(P["s5_glu_b"]),
        pool_w=bf(_block_diag(P["pool_w"])), pool_scale=row(P["pool_scale"]),
        w_out=bf(P["w_out"]),
        norm_ffn2=row(P["norm_ffn2"]), ffn2_in=P["ffn2_in"], ffn2_out=P["ffn2_out"],
    )
    for name in _RWKV_PARAM_NAMES:
        a = P["rwkv_" + name]
        out["rwkv_" + name] = bf(a) if name in ("w2", "a2", "g2") else row(a)
    return out


def _layer_params(stacked, l):
    lp = {k: _Layered((v, l)) for k, v in stacked.items()}
    lp["rwkv"] = {n: lp["rwkv_" + n] for n in _RWKV_PARAM_NAMES}
    lp["head_expand"] = jnp.pad(jnp.repeat(jnp.eye(SSD_HEADS, dtype=F32), SSD_HEAD_DIM, axis=1),
                                ((0, LANES - SSD_HEADS), (0, 0)))
    return lp


def _mixers_prompt(lp, proj, *, batch, seq):
    z, xbc, ur, us5, upool, dtr = proj
    y_ssd, conv_new, ssd_new = _ssd(z, xbc, dtr, lp, batch=batch, seq=seq)
    y_rwkv, shift_new, rwkv_new = _rwkv(ur, lp["rwkv"], batch=batch, seq=seq)
    zeros = jnp.zeros((batch, S5_WIDTH), F32)
    bm = lambda a: a.reshape(batch, seq, a.shape[-1])
    rows = lambda a: a.reshape(batch * seq, a.shape[-1])
    y_s5, s5re, s5im = _s5(bm(us5), zeros, zeros, lp, batch_major=True)
    y_pool, pool_new = _pool(bm(upool), jnp.zeros((POOL_BUF, batch, GROUP_WIDTH), F32), lp, pos0=0,
                             batch_major=True)
    ys = (y_ssd, y_rwkv, rows(y_s5), rows(y_pool))
    states = (conv_new, ssd_new, shift_new, rwkv_new, s5re.reshape(batch, S5_GROUPS, S5_STATE),
              s5im.reshape(batch, S5_GROUPS, S5_STATE), jnp.swapaxes(pool_new, 0, 1))
    return ys, states


def _mixers_decode(lp, proj, states, done, *, batch, seq, layer):
    z, xbc, ur, us5, upool, dtr = proj
    shift0, s5re0, s5im0 = (states[i][layer] for i in (2, 4, 5))
    ssd_done, rwkv_done = (done[1], done[3]) if layer else (None, None)
    y_ssd, conv_new, ssd_new = _ssd_step(z, xbc, dtr, states[0], states[1], ssd_done, lp, batch=batch, seq=seq,
                                         layer=layer)
    y_rwkv, rwkv_new = _rwkv_step(ur, shift0, states[3], rwkv_done, lp["rwkv"], batch=batch, seq=seq, layer=layer)
    shift_new = ur[(seq - 1) * batch:, :]
    tm = lambda a: a.reshape(seq, batch, a.shape[-1])
    y_s5, s5re, s5im = _s5(tm(us5), s5re0.reshape(batch, S5_WIDTH), s5im0.reshape(batch, S5_WIDTH), lp,
                           batch_major=False)
    y_pool, pool_new = _pool(tm(upool), states[6], lp, pos0=PAST_LEN, batch_major=False, layer=layer)
    rows = lambda a: a.reshape(seq * batch, a.shape[-1])
    ys = (y_ssd, y_rwkv, rows(y_s5), rows(y_pool))
    new_states = (jnp.swapaxes(conv_new, 0, 1), ssd_new, shift_new, rwkv_new,
                  s5re.reshape(batch, S5_GROUPS, S5_STATE), s5im.reshape(batch, S5_GROUPS, S5_STATE),
                  jnp.swapaxes(pool_new, 0, 1))
    return ys, new_states


_WIDTHS = (GROUP_WIDTH, SSD_CONV_DIM, RWKV_PROJ, GROUP_WIDTH, GROUP_WIDTH, LANES)


def _trunk(x_p, x_s, layer_params, norm_final, mixers_p, mixers_s):
    st_p, st_s = [], []
    mix_p, mix_s, lp = None, None, None
    for l, lp_next in enumerate(layer_params):
        if l > 0:
            x_s, wg, wu, wo = _ffn_cast(x_s, lp["norm_ffn2"], lp["ffn2_in"], lp["ffn2_out"], mix=mix_s, wmix=lp["w_out"])
            x_p = _ffn(x_p, lp["norm_ffn2"], wg, wu, wo, mix=mix_p, wmix=lp["w_out"])
        lp = lp_next
        x_s, wg, wu, wo = _ffn_cast(x_s, lp["norm_ffn1"], lp["ffn1_in"], lp["ffn1_out"])
        x_p = _ffn(x_p, lp["norm_ffn1"], wg, wu, wo)
        proj_s, w_all = _inproj_cast(x_s, lp["norm_mix"], lp["w_in"], _WIDTHS)
        mix_p, st = mixers_p(l, lp, _inproj(x_p, lp["norm_mix"], w_all, _WIDTHS), st_p[-1] if st_p else None)
        st_p.append(st)
        mix_s, st = mixers_s(l, lp, proj_s, st_s[-1] if st_s else None)
        st_s.append(st)
    x_s, wg, wu, wo = _ffn_cast(x_s, lp["norm_ffn2"], lp["ffn2_in"], lp["ffn2_out"], mix=mix_s, wmix=lp["w_out"],
                                gf=norm_final)
    x_p = _ffn(x_p, lp["norm_ffn2"], wg, wu, wo, mix=mix_p, wmix=lp["w_out"], gf=norm_final)
    return (x_p, x_s), (st_p, st_s)


def kernel(x_prompt, x_sample, state_ssd_conv, state_ssd, state_rwkv_shift, state_rwkv, state_s5_re, state_s5_im, state_pool, norm_ffn1, ffn1_in, ffn1_out, norm_mix, w_in, ssd_conv_w, ssd_conv_b, ssd_dt_bias, ssd_a_log, ssd_d, ssd_norm, rwkv_mu, rwkv_w0, rwkv_w2, rwkv_a0, rwkv_a2, rwkv_g2, rwkv_k_k, rwkv_k_a, rwkv_r_k, rwkv_ln_g, rwkv_ln_b, s5_lam_re, s5_lam_im, s5_log_step, s5_b_re, s5_b_im, s5_c_re, s5_c_im, s5_d, s5_glu_w, s5_glu_b, pool_w, pool_scale, w_out, norm_ffn2, ffn2_in, ffn2_out, norm_final):
    P = dict(norm_ffn1=norm_ffn1, ffn1_in=ffn1_in, ffn1_out=ffn1_out, norm_mix=norm_mix, w_in=w_in,
             ssd_conv_w=ssd_conv_w, ssd_conv_b=ssd_conv_b, ssd_dt_bias=ssd_dt_bias, ssd_a_log=ssd_a_log,
             ssd_d=ssd_d, ssd_norm=ssd_norm, rwkv_mu=rwkv_mu, rwkv_w0=rwkv_w0, rwkv_w2=rwkv_w2, rwkv_a0=rwkv_a0,
             rwkv_a2=rwkv_a2, rwkv_g2=rwkv_g2, rwkv_k_k=rwkv_k_k, rwkv_k_a=rwkv_k_a,
             rwkv_r_k=rwkv_r_k.reshape(rwkv_r_k.shape[0], -1), rwkv_ln_g=rwkv_ln_g, rwkv_ln_b=rwkv_ln_b,
             s5_lam_re=s5_lam_re, s5_lam_im=s5_lam_im, s5_log_step=s5_log_step, s5_b_re=s5_b_re, s5_b_im=s5_b_im,
             s5_c_re=s5_c_re, s5_c_im=s5_c_im, s5_d=s5_d, s5_glu_w=s5_glu_w, s5_glu_b=s5_glu_b, pool_w=pool_w,
             pool_scale=pool_scale, w_out=w_out, norm_ffn2=norm_ffn2, ffn2_in=ffn2_in, ffn2_out=ffn2_out)
    depth = norm_ffn1.shape[0]
    bp, tp, d = x_prompt.shape
    bs, ts, _ = x_sample.shape
    stacked = _stacked_params(P)
    layer_params = [_layer_params(stacked, l) for l in range(depth)]
    gf = norm_final.reshape(1, -1)
    sample_states = (state_ssd_conv, state_ssd, state_rwkv_shift, state_rwkv, state_s5_re, state_s5_im, state_pool)
    rwkv_rows = RWKV_HEADS * RWKV_HEAD
    decode_states = (jnp.swapaxes(state_ssd_conv, 1, 2), state_ssd, state_rwkv_shift,
                     jnp.transpose(state_rwkv, (0, 2, 3, 4, 1)).reshape(depth, rwkv_rows, RWKV_HEAD, bs),
                     state_s5_re, state_s5_im, jnp.swapaxes(state_pool, 1, 2))

    x_s = jnp.swapaxes(x_sample, 0, 1).reshape(ts * bs, d)
    (y_p, y_s), (st_p, st_s) = _trunk(
        x_prompt.reshape(bp * tp, d), x_s, layer_params, gf,
        lambda l, lp, proj, done: _mixers_prompt(lp, proj, batch=bp, seq=tp),
        lambda l, lp, proj, done: _mixers_decode(lp, proj, decode_states, done, batch=bs, seq=ts, layer=l))
    outs = [y_p.reshape(bp, tp, d), jnp.swapaxes(y_s.reshape(ts, bs, d), 0, 1)]
    for i, ref_state in enumerate(sample_states):
        outs.append(jnp.stack([st[i] for st in st_p]))
        if i == 1:
            outs.append(st_s[-1][i].reshape(ref_state.shape))
        elif i == 3:
            s_new = st_s[-1][i].reshape(depth, RWKV_HEADS, RWKV_HEAD, RWKV_HEAD, bs)
            outs.append(jnp.transpose(s_new, (0, 4, 1, 2, 3)))
        else:
            outs.append(jnp.stack([st[i] for st in st_s]))
    return tuple(outs)
```

```python
import functools
import math

import jax
import jax.numpy as jnp
from jax import lax
from jax.experimental import pallas as pl
from jax.experimental.pallas import tpu as pltpu

F32 = jnp.float32
BF16 = jnp.bfloat16
HIGHEST = lax.Precision.HIGHEST

SUBLANES = 8
LANES = 128
VMEM_LIMIT_BYTES = 56 * 1024 * 1024

GROUP_WIDTH = 256
SSD_HEAD_DIM = 64
SSD_HEADS = 4
SSD_GROUPS = 2
SSD_STATE = 128
SSD_CONV = 4
SSD_CONV_DIM = GROUP_WIDTH + 2 * SSD_GROUPS * SSD_STATE
SSD_CHUNK = 128
SSD_GROUP = 4
LOG2_E = math.log2(math.e)
RWKV_HEAD = 64
RWKV_HEADS = 4
RWKV_PROJ = 1024
RWKV_LN_EPS = 64e-5
RWKV_CHUNK = 64
RWKV_GROUP = 8
S5_GROUP_CH = 16
S5_GROUPS = 16
S5_STATE = 64
S5_WIDTH = S5_GROUPS * S5_STATE
POOL_WINDOWS = (2, 4, 8, 16)
POOL_CH = 64
POOL_BUF = 15
RMS_EPS = 1e-6
PAST_LEN = 16384

ROW_TILE = 512
FFN_CHUNK = 256
TM_CHUNK = 128
POOL_CHUNK = 256
S5_SUB = 64
SSD_STEP_TILES = 16
RWKV_STEP_TILES = 16


def _cparams(*sem):
    return pltpu.CompilerParams(dimension_semantics=sem, vmem_limit_bytes=VMEM_LIMIT_BYTES)


def _dot(a, b, **kw):
    return jnp.dot(a, b, preferred_element_type=F32, **kw)


def _dot_nt(a, b):
    return lax.dot_general(a, b, (((1,), (1,)), ((), ())), preferred_element_type=F32)


def _dot_tn(a, b):
    return lax.dot_general(a, b, (((0,), (0,)), ((), ())), preferred_element_type=F32)


def _sigmoid(x):
    return 1.0 / (1.0 + jnp.exp(-x))


def _silu(x):
    return x * _sigmoid(x)


def _softplus(x):
    return jnp.maximum(x, 0.0) + jnp.log(1.0 + jnp.exp(-jnp.abs(x)))


def _gelu_tanh(x):
    c = math.sqrt(2.0 / math.pi)
    return x * (0.5 * (1.0 + jnp.tanh(c * (x + 0.044715 * (x * x * x)))))


def _rms(x, g):
    return x * lax.rsqrt(jnp.mean(x * x, axis=-1, keepdims=True) + RMS_EPS) * g


def _full_spec(shape):
    n = len(shape)
    return pl.BlockSpec(shape, lambda *_: (0,) * n)


class _Layered(tuple):
    pass


def _pspec(p, single=False):
    mode = pl.Buffered(1) if single else None
    if isinstance(p, _Layered):
        a, l = p
        return pl.BlockSpec((None,) + a.shape[1:], lambda *_: (l,) + (0,) * (a.ndim - 1), pipeline_mode=mode)
    n = p.ndim
    return pl.BlockSpec(p.shape, lambda *_: (0,) * n, pipeline_mode=mode)


def _parg(p):
    return p[0] if isinstance(p, _Layered) else p


def _mix_residual(x, y_refs, wmix_ref):
    for j, y_ref in enumerate(y_refs):
        x = x + _dot(y_ref[...].astype(BF16), wmix_ref[j * GROUP_WIDTH:(j + 1) * GROUP_WIDTH, :])
    return x


def _swiglu_chunk(h, wg, wu, wo):
    act = (_silu(_dot(h, wg)) * _dot(h, wu)).astype(BF16)
    return _dot(act, wo)


def _ffn_body(*refs, has_mix, final_norm):
    it = iter(refs)
    x = next(it)[...]
    if has_mix:
        y_refs = [next(it) for _ in range(4)]
        x = _mix_residual(x, y_refs, next(it))
    g_ref, wg_ref, wu_ref, wo_ref = next(it), next(it), next(it), next(it)
    gf_ref = next(it) if final_norm else None
    o_ref = next(it)
    h = _rms(x, g_ref[...]).astype(BF16)
    acc = jnp.zeros_like(x)
    for c in range(wo_ref.shape[0] // FFN_CHUNK):
        cols = slice(c * FFN_CHUNK, (c + 1) * FFN_CHUNK)
        acc = acc + _swiglu_chunk(h, wg_ref[:, cols], wu_ref[:, cols], wo_ref[cols, :])
    x = x + 0.5 * acc
    if final_norm:
        x = _rms(x, gf_ref[...])
    o_ref[...] = x


def _ffn(x, g, wg, wu, wo, mix=None, wmix=None, gf=None):
    rows, d = x.shape
    row_spec = lambda w: pl.BlockSpec((ROW_TILE, w), lambda i: (i, 0))
    args, specs = [x], [row_spec(d)]
    if mix is not None:
        for y in mix:
            args.append(y)
            specs.append(row_spec(y.shape[1]))
        args.append(_parg(wmix))
        specs.append(_pspec(wmix, single=True))
    for a in (g, wg, wu, wo) + ((gf,) if gf is not None else ()):
        args.append(_parg(a))
        specs.append(_pspec(a, single=True))
    return pl.pallas_call(
        functools.partial(_ffn_body, has_mix=mix is not None, final_norm=gf is not None),
        grid=(rows // ROW_TILE,),
        in_specs=specs,
        out_specs=row_spec(d),
        out_shape=jax.ShapeDtypeStruct((rows, d), F32),
        compiler_params=_cparams("parallel"),
        name="ffn",
    )(*args)


def _ffn_cast_body(*refs, has_mix, final_norm):
    it = iter(refs)
    x_ref = next(it)
    if has_mix:
        y_refs = [next(it) for _ in range(4)]
        wmix_ref = next(it)
    g_ref, wg_ref, wu_ref, wo_ref = next(it), next(it), next(it), next(it)
    gf_ref = next(it) if final_norm else None
    o_ref, wg_out, wu_out, wo_out, x_scr, h_scr, acc_scr = (next(it) for _ in range(7))
    c = pl.program_id(0)

    @pl.when(c == 0)
    def _():
        x = x_ref[...]
        if has_mix:
            x = _mix_residual(x, y_refs, wmix_ref)
        x_scr[...] = x
        h_scr[...] = _rms(x, g_ref[...]).astype(BF16)
        acc_scr[...] = jnp.zeros(acc_scr.shape, F32)

    wg = wg_ref[...].astype(BF16)
    wu = wu_ref[...].astype(BF16)
    wo = wo_ref[...].astype(BF16)
    wg_out[...] = wg
    wu_out[...] = wu
    wo_out[...] = wo
    acc_scr[...] += _swiglu_chunk(h_scr[...], wg, wu, wo)

    @pl.when(c == pl.num_programs(0) - 1)
    def _():
        x = x_scr[...] + 0.5 * acc_scr[...]
        if final_norm:
            x = _rms(x, gf_ref[...])
        o_ref[...] = x


def _ffn_cast(x, g, wi, wo, mix=None, wmix=None, gf=None):
    rows, d = x.shape
    wi_all, l = wi
    wo_all, _ = wo
    d_ff = wo_all.shape[1]
    nchunks = d_ff // FFN_CHUNK
    args, specs = [x], [_full_spec(x.shape)]
    if mix is not None:
        for y in mix:
            args.append(y)
            specs.append(_full_spec(y.shape))
        args.append(_parg(wmix))
        specs.append(_pspec(wmix, single=True))
    args += [_parg(g), wi_all, wi_all, wo_all]
    specs += [_pspec(g),
              pl.BlockSpec((None, d, FFN_CHUNK), lambda c: (l, 0, c)),
              pl.BlockSpec((None, d, FFN_CHUNK), lambda c: (l, 0, c + nchunks)),
              pl.BlockSpec((None, FFN_CHUNK, d), lambda c: (l, c, 0))]
    if gf is not None:
        args.append(gf)
        specs.append(_full_spec(gf.shape))
    col_spec = pl.BlockSpec((d, FFN_CHUNK), lambda c: (0, c))
    return pl.pallas_call(
        functools.partial(_ffn_cast_body, has_mix=mix is not None, final_norm=gf is not None),
        grid=(nchunks,),
        in_specs=specs,
        out_specs=[_full_spec(x.shape), col_spec, col_spec, pl.BlockSpec((FFN_CHUNK, d), lambda c: (c, 0))],
        out_shape=[jax.ShapeDtypeStruct((rows, d), F32), jax.ShapeDtypeStruct((d, d_ff), BF16),
                   jax.ShapeDtypeStruct((d, d_ff), BF16), jax.ShapeDtypeStruct((d_ff, d), BF16)],
        scratch_shapes=[pltpu.VMEM((rows, d), F32), pltpu.VMEM((rows, d), BF16), pltpu.VMEM((rows, d), F32)],
        compiler_params=_cparams("arbitrary"),
        name="ffn_cast",
    )(*args)


def _inproj_body(x_ref, g_ref, wt_ref, *o_refs):
    h = _rms(x_ref[...], g_ref[...]).astype(BF16)
    off = 0
    for o_ref in o_refs:
        n = o_ref.shape[-1]
        o_ref[...] = _dot_nt(h, wt_ref[off:off + n, :])
        off += n


def _inproj_cast_body(x_ref, g_ref, win_ref, *o_refs, layer):
    *proj_refs, wall_ref = o_refs
    split = GROUP_WIDTH + SSD_CONV_DIM
    wt = win_ref[:, layer, :]
    tail = wt.shape[0] - split - SSD_HEADS
    wall_ref[0:split, :] = wt[0:split].astype(BF16)
    wall_ref[split:split + tail, :] = wt[split + SSD_HEADS:].astype(BF16)
    dt_rows = jnp.concatenate([wt[split:split + SSD_HEADS], jnp.zeros((LANES - SSD_HEADS, wt.shape[1]), F32)], axis=0)
    wall_ref[split + tail:, :] = dt_rows.astype(BF16)
    _inproj_body(x_ref, g_ref, wall_ref, *proj_refs)


def _inproj_cast(x, g, w_in, widths):
    rows, d = x.shape
    wt_all, l = w_in
    outs = pl.pallas_call(
        functools.partial(_inproj_cast_body, layer=l),
        grid=(1,),
        in_specs=[_full_spec(x.shape), _pspec(g),
                  pl.BlockSpec(wt_all.shape, lambda i: (0, 0, 0), pipeline_mode=pl.Buffered(1))],
        out_specs=[_full_spec((rows, n)) for n in widths] + [_full_spec((sum(widths), d))],
        out_shape=[jax.ShapeDtypeStruct((rows, n), F32) for n in widths]
                  + [jax.ShapeDtypeStruct((sum(widths), d), BF16)],
        compiler_params=_cparams("arbitrary"),
        name="inproj_cast",
    )(x, _parg(g), wt_all)
    return outs[:-1], outs[-1]


def _inproj(x, g, w, widths):
    rows, d = x.shape
    row_spec = lambda w_: pl.BlockSpec((ROW_TILE, w_), lambda i: (i, 0))
    return pl.pallas_call(
        _inproj_body,
        grid=(rows // ROW_TILE,),
        in_specs=[row_spec(d), _pspec(g), _pspec(w, single=True)],
        out_specs=[row_spec(n) for n in widths],
        out_shape=[jax.ShapeDtypeStruct((rows, n), F32) for n in widths],
        compiler_params=_cparams("parallel"),
        name="inproj",
    )(x, _parg(g), _parg(w))


def _ssd_body(z_ref, xbc_ref, dt_ref, cw_ref, cb_ref, dtb_ref, alog_ref, dsk_ref, ng_ref,
              y_ref, conv_ref, hout_ref, xpad_scr, h_scr, *, chunk, group):
    L, G = chunk, group
    GL = G * L
    c = pl.program_id(1)
    pad = SUBLANES
    halo = SSD_CONV - 1
    hpg = SSD_HEADS // SSD_GROUPS
    assert hpg == 2 and hpg * SSD_HEAD_DIM == SSD_STATE

    @pl.when(c == 0)
    def _():
        xpad_scr[0:pad, :] = jnp.zeros((pad, SSD_CONV_DIM), F32)
        h_scr[...] = jnp.zeros(h_scr.shape, F32)

    xpad_scr[pad:pad + GL, :] = xbc_ref[...]
    xfull = xpad_scr[...]
    conv = cb_ref[...] + cw_ref[halo:halo + 1, :] * xfull[pad:pad + GL]
    for j in range(halo):
        conv = conv + cw_ref[j:j + 1, :] * pltpu.roll(xfull, halo - j, axis=0)[pad:pad + GL]
    xpad_scr[pad - halo:pad, :] = xpad_scr[pad + GL - halo:pad + GL, :]
    conv = _silu(conv)
    xs = conv[:, 0:GROUP_WIDTH]
    bm = conv[:, GROUP_WIDTH:2 * GROUP_WIDTH].astype(BF16)
    cm = conv[:, 2 * GROUP_WIDTH:3 * GROUP_WIDTH].astype(BF16)

    row = lax.broadcasted_iota(jnp.int32, (L, L), 0)
    col = lax.broadcasted_iota(jnp.int32, (L, L), 1)
    causal = row >= col
    tril = jnp.where(causal, 1.0, 0.0).astype(F32)
    dt = _softplus(dt_ref[...] + dtb_ref[...])
    da = dt * (-jnp.exp(alog_ref[...]) * LOG2_E)
    acs = [_dot(tril, da[i * L:(i + 1) * L, :], precision=HIGHEST) for i in range(G)]
    acs_t = [a.T for a in acs]
    e_acs = [jnp.exp2(a) for a in acs]
    e_end = [jnp.exp2(a[L - 1:L, :] - a) for a in acs]
    e_last = [jnp.exp2(a[L - 1:L, :]) for a in acs]

    keys = [(i, g) for i in range(G) for g in range(SSD_GROUPS)]
    rows_of = lambda x, i: x[i * L:(i + 1) * L]
    lanes_of = lambda x, g: x[:, g * SSD_STATE:(g + 1) * SSD_STATE]
    lane_lo = lax.broadcasted_iota(jnp.int32, (L, hpg * SSD_HEAD_DIM), 1) < SSD_HEAD_DIM
    row_lo = lax.broadcasted_iota(jnp.int32, (hpg * SSD_HEAD_DIM, SSD_STATE), 0) < SSD_HEAD_DIM
    head_cols = lambda a, g: jnp.where(lane_lo, a[:, g * hpg:g * hpg + 1], a[:, g * hpg + 1:g * hpg + 2])
    bg = {(i, g): lanes_of(rows_of(bm, i), g) for i, g in keys}
    cg = {(i, g): lanes_of(rows_of(cm, i), g) for i, g in keys}
    scores = {k: _dot_nt(cg[k], bg[k]) for k in keys}
    xdt = {(i, g): lanes_of(rows_of(xs, i), g) * head_cols(rows_of(dt, i), g) for i, g in keys}
    decay = {(i, h): jnp.exp2(jnp.where(causal, acs[i][:, h:h + 1] - acs_t[i][h:h + 1, :], -jnp.inf))
             for i in range(G) for h in range(SSD_HEADS)}
    p_mat = {(i, g): jnp.concatenate([(scores[(i, g)] * decay[(i, g * hpg + k)]).astype(BF16) for k in range(hpg)],
                                     axis=1) for i, g in keys}
    y_in = {k: _dot(p_mat[k], _bd(xdt[k].astype(BF16))) for k in keys}
    st = {(i, g): _dot_tn((xdt[(i, g)] * head_cols(e_end[i], g)).astype(BF16), bg[(i, g)]) for i, g in keys}

    y_rows = []
    for i in range(G):
        ys = []
        for g in range(SSD_GROUPS):
            h_prev = h_scr[g * hpg:(g + 1) * hpg].reshape(hpg * SSD_HEAD_DIM, SSD_STATE)
            ys.append(y_in[(i, g)] + _dot_nt(cg[(i, g)], h_prev.astype(BF16)) * head_cols(e_acs[i], g))
            keep = jnp.where(row_lo, e_last[i][:, g * hpg:g * hpg + 1], e_last[i][:, g * hpg + 1:g * hpg + 2])
            h_scr[g * hpg:(g + 1) * hpg] = (h_prev * keep + st[(i, g)]).reshape(hpg, SSD_HEAD_DIM, SSD_STATE)
        y_rows.append(jnp.concatenate(ys, axis=-1))
    y = jnp.concatenate(y_rows, axis=0) + xs * dsk_ref[...]
    y = y * _silu(z_ref[...])
    y_ref[...] = _rms(y, ng_ref[...])

    @pl.when(c == pl.num_programs(1) - 1)
    def _():
        hout_ref[0] = h_scr[...]
        conv_ref[0] = xpad_scr[pad - halo:pad, :]


def _ssd(z, xbc, dtr, lp, *, batch, seq):
    chunk = SSD_CHUNK
    rows = chunk * SSD_GROUP
    nc = seq // rows
    rspec = lambda w: pl.BlockSpec((rows, w), lambda b, c: (b * nc + c, 0))
    consts = (lp["conv_w"], lp["conv_b"], lp["dt_bias"], lp["a_log"], lp["d_skip"], lp["ssd_norm"])
    return pl.pallas_call(
        functools.partial(_ssd_body, chunk=chunk, group=SSD_GROUP),
        grid=(batch, nc),
        in_specs=[rspec(GROUP_WIDTH), rspec(SSD_CONV_DIM), rspec(LANES)] + [_pspec(a) for a in consts],
        out_specs=[rspec(GROUP_WIDTH),
                   pl.BlockSpec((1, SSD_CONV - 1, SSD_CONV_DIM), lambda b, c: (b, 0, 0)),
                   pl.BlockSpec((1, SSD_HEADS, SSD_HEAD_DIM, SSD_STATE), lambda b, c: (b, 0, 0, 0))],
        out_shape=[jax.ShapeDtypeStruct((batch * seq, GROUP_WIDTH), F32),
                   jax.ShapeDtypeStruct((batch, SSD_CONV - 1, SSD_CONV_DIM), F32),
                   jax.ShapeDtypeStruct((batch, SSD_HEADS, SSD_HEAD_DIM, SSD_STATE), F32)],
        scratch_shapes=[pltpu.VMEM((SUBLANES + rows, SSD_CONV_DIM), F32),
                        pltpu.VMEM((SSD_HEADS, SSD_HEAD_DIM, SSD_STATE), F32)],
        compiler_params=_cparams("parallel", "arbitrary"),
        name="ssd",
    )(z, xbc, dtr, *[_parg(a) for a in consts])


def _ssd_step_body(z_ref, xbc_ref, dt_ref, conv0_ref, h0_ref, *rest, seq, batch, layer):
    hdone_ref, rest = (rest[0], rest[1:]) if layer else (None, rest)
    (cw_ref, cb_ref, dtb_ref, aneg_ref, dsk_ref, ng_ref, hexp_ref, y_ref, conv_ref, hout_ref,
     xs_scr, bm_scr, cm_scr, xdt_scr, dec_scr, y_scr) = rest
    T, B = seq, batch
    if layer:
        hout_ref[0:layer] = hdone_ref[...]
    GW = GROUP_WIDTH
    j = pl.program_id(0)
    tiles = SSD_STEP_TILES

    @pl.when(j == 0)
    def _():
        rows = [conv0_ref[i] for i in range(SSD_CONV - 1)]
        rows += [xbc_ref[t * B:(t + 1) * B, :] for t in range(T)]
        for t in range(T):
            conv = cb_ref[...] + cw_ref[0:1, :] * rows[t]
            for i in range(1, SSD_CONV):
                conv = conv + cw_ref[i:i + 1, :] * rows[t + i]
            conv = _silu(conv)
            xs = conv[:, 0:GW]
            xs_scr[t] = xs
            for g in range(SSD_GROUPS):
                bm_scr[t, g] = conv[:, GW + g * SSD_STATE:GW + (g + 1) * SSD_STATE].T
                cm_scr[t, g] = conv[:, 2 * GW + g * SSD_STATE:2 * GW + (g + 1) * SSD_STATE].T
            dt = _softplus(dt_ref[t * B:(t + 1) * B, :] + dtb_ref[...])
            dte = _dot(dt, hexp_ref[...], precision=HIGHEST)
            xdt_scr[t] = (xs * dte).T
            dec_scr[t] = jnp.exp(dte * aneg_ref[...]).T
        for i in range(SSD_CONV - 1):
            conv_ref[i] = rows[T + i]

    hp0 = j * tiles
    grp = hp0 // (SSD_HEAD_DIM * (SSD_HEADS // SSD_GROUPS))
    for q in range(tiles):
        hp = pl.ds(hp0 + q, 1)
        h = h0_ref[:, q, :].T
        for t in range(T):
            h = h * dec_scr[t, hp, :] + bm_scr[t, grp] * xdt_scr[t, hp, :]
            y_scr[t, hp, :] = jnp.sum(h * cm_scr[t, grp], axis=0, keepdims=True)
        hout_ref[layer, :, q, :] = h.T

    @pl.when(j == pl.num_programs(0) - 1)
    def _():
        for t in range(T):
            y = y_scr[t].T + xs_scr[t] * dsk_ref[...]
            y = y * _silu(z_ref[t * B:(t + 1) * B, :])
            y_ref[t * B:(t + 1) * B, :] = _rms(y, ng_ref[...])


def _layer_state_specs(layer, block, axis):
    idx = lambda first: (lambda j: (first,) + tuple(j if a == axis else 0 for a in range(len(block))))
    cur = pl.BlockSpec((None,) + block, idx(layer))
    prev = [pl.BlockSpec((layer,) + block, idx(0))] if layer else []
    out = pl.BlockSpec((layer + 1,) + block, idx(0))
    return cur, prev, out


def _ssd_step(z, xbc, dtr, conv_all, h_all, h_done, lp, *, batch, seq, layer):
    n = batch * seq
    srows = SSD_HEADS * SSD_HEAD_DIM
    consts = (lp["conv_w"], lp["conv_b"], lp["dt_bias"], lp["a_neg_exp"], lp["d_skip"], lp["ssd_norm"], lp["head_expand"])
    hspec, prev_specs, hout_spec = _layer_state_specs(layer, (batch, SSD_STEP_TILES, SSD_STATE), 1)
    prev_args = [h_done] if layer else []
    cshape = (SSD_CONV - 1, batch, SSD_CONV_DIM)
    return pl.pallas_call(
        functools.partial(_ssd_step_body, seq=seq, batch=batch, layer=layer),
        grid=(srows // SSD_STEP_TILES,),
        in_specs=[_full_spec((n, GROUP_WIDTH)), _full_spec((n, SSD_CONV_DIM)), _full_spec((n, LANES)),
                  pl.BlockSpec((None,) + cshape, lambda j: (layer, 0, 0, 0)), hspec] + prev_specs
                 + [_pspec(a) for a in consts],
        out_specs=[_full_spec((n, GROUP_WIDTH)), _full_spec(cshape), hout_spec],
        out_shape=[jax.ShapeDtypeStruct((n, GROUP_WIDTH), F32),
                   jax.ShapeDtypeStruct(cshape, F32),
                   jax.ShapeDtypeStruct((layer + 1, batch, srows, SSD_STATE), F32)],
        scratch_shapes=[pltpu.VMEM((seq, batch, GROUP_WIDTH), F32),
                        pltpu.VMEM((seq, SSD_GROUPS, SSD_STATE, batch), F32),
                        pltpu.VMEM((seq, SSD_GROUPS, SSD_STATE, batch), F32),
                        pltpu.VMEM((seq, GROUP_WIDTH, batch), F32),
                        pltpu.VMEM((seq, GROUP_WIDTH, batch), F32),
                        pltpu.VMEM((seq, GROUP_WIDTH, batch), F32)],
        compiler_params=_cparams("arbitrary"),
        name="ssd_step",
    )(z, xbc, dtr, conv_all, h_all.reshape(h_all.shape[0], batch, srows, SSD_STATE),
      *prev_args, *[_parg(a) for a in consts])


PAIR = 2 * RWKV_HEAD
RWKV_PAIRS = RWKV_HEADS // 2


def _bd(x):
    half = x.shape[1] // 2
    lane = lax.broadcasted_iota(jnp.int32, x.shape, 1)
    zero = jnp.zeros_like(x)
    return jnp.concatenate([jnp.where(lane < half, x, zero), jnp.where(lane >= half, x, zero)], axis=0)


def _half_sums(x, lo):
    s_lo = jnp.sum(jnp.where(lo, x, 0.0), axis=-1, keepdims=True)
    s_hi = jnp.sum(jnp.where(lo, 0.0, x), axis=-1, keepdims=True)
    return jnp.where(lo, s_lo, s_hi)


def _head_sum(x):
    lo = lax.broadcasted_iota(jnp.int32, (x.shape[0], PAIR), 1) < RWKV_HEAD
    return jnp.concatenate([_half_sums(x[:, p * PAIR:(p + 1) * PAIR], lo) for p in range(RWKV_PAIRS)], axis=-1)


def _rwkv_pointwise(u, prev, mu_ref, w0_ref, w2_ref, a0_ref, a2_ref, g2_ref, kk_ref, ka_ref):
    GW = GROUP_WIDTH
    xs = u + (prev - u) * mu_ref[...]
    r = xs[:, 0:GW]
    k = xs[:, GW:2 * GW]
    v = xs[:, 2 * GW:3 * GW]
    wd = xs[:, 3 * GW:3 * GW + 64]
    ad = xs[:, 3 * GW + 64:3 * GW + 128]
    gd = xs[:, 3 * GW + 128:3 * GW + 256]
    w_lin = w0_ref[...] + _dot(jnp.tanh(wd).astype(BF16), w2_ref[...])
    logdecay = -math.exp(-0.5) * _sigmoid(w_lin)
    a = _sigmoid(a0_ref[...] + _dot(ad.astype(BF16), a2_ref[...]))
    g = _dot(_sigmoid(gd).astype(BF16), g2_ref[...])
    kk = k * kk_ref[...]
    kk = kk * lax.rsqrt(jnp.maximum(_head_sum(kk * kk), 1e-24))
    k = k * (1.0 + (a - 1.0) * ka_ref[...])
    return r, k, v, logdecay, a, g, kk


def _rwkv_finish(y, r, k, v, g, rk_ref, lng_ref, lnb_ref):
    mean = _head_sum(y) * (1.0 / RWKV_HEAD)
    yc = y - mean
    var = _head_sum(yc * yc) * (1.0 / RWKV_HEAD)
    y = yc * lax.rsqrt(var + RWKV_LN_EPS) * lng_ref[...] + lnb_ref[...]
    bonus = _head_sum(r * k * rk_ref[...]) * v
    return (y + bonus) * g


def _rwkv_body(u_ref, mu_ref, w0_ref, w2_ref, a0_ref, a2_ref, g2_ref, kk_ref, ka_ref, rk_ref,
               lng_ref, lnb_ref, y_ref, shift_ref, sout_ref, upad_scr, s_scr, *, chunk, group):
    L, G = chunk, group
    GL = G * L
    c = pl.program_id(1)
    pad = SUBLANES

    @pl.when(c == 0)
    def _():
        upad_scr[0:pad, :] = jnp.zeros((pad, RWKV_PROJ), F32)
        s_scr[...] = jnp.zeros(s_scr.shape, F32)

    u = u_ref[...]
    upad_scr[pad:pad + GL, :] = u
    prev = pltpu.roll(upad_scr[...], 1, axis=0)[pad:pad + GL]
    upad_scr[pad - 1:pad, :] = u[GL - 1:GL, :]
    r, k, v, logdecay, a, g, kk = _rwkv_pointwise(u, prev, mu_ref, w0_ref, w2_ref, a0_ref, a2_ref, g2_ref,
                                                  kk_ref, ka_ref)

    tril = jnp.where(lax.broadcasted_iota(jnp.int32, (L, L), 0) >= lax.broadcasted_iota(jnp.int32, (L, L), 1),
                     1.0, 0.0).astype(F32)
    cl = jnp.concatenate([_dot(tril, logdecay[i * L:(i + 1) * L, :], precision=HIGHEST) for i in range(G)], axis=0)
    e_in = jnp.exp(cl)
    e_inv = jnp.exp(-cl)
    r_t = r * e_in
    r_tb = r_t.astype(BF16)
    a_tb = (-kk * jnp.exp(cl - logdecay)).astype(BF16)
    b_tb = (kk * a * e_inv).astype(BF16)
    k_tb = (k * e_inv).astype(BF16)
    vb = v.astype(BF16)

    row = lax.broadcasted_iota(jnp.int32, (L, PAIR), 0)
    colh = lax.broadcasted_iota(jnp.int32, (L, PAIR), 1) & (RWKV_HEAD - 1)
    strict = row > colh
    incl = row >= colh
    eye_pair = jnp.where(row == colh, 1.0, 0.0).astype(F32)
    lane_lo = lax.broadcasted_iota(jnp.int32, (RWKV_HEAD, PAIR), 1) < RWKV_HEAD
    same_head = (lax.broadcasted_iota(jnp.int32, (PAIR, PAIR), 0) < RWKV_HEAD) == \
                (lax.broadcasted_iota(jnp.int32, (PAIR, PAIR), 1) < RWKV_HEAD)

    streams = [(i, p) for i in range(G) for p in range(RWKV_PAIRS)]
    ns = len(streams)
    blk = lambda x, i, p: x[i * L:(i + 1) * L, p * PAIR:(p + 1) * PAIR]
    lhs = [jnp.concatenate([blk(a_tb, i, p), blk(r_tb, i, p)], axis=0) for i, p in streams]
    m_both = [_dot_nt(lhs[s], jnp.concatenate([_bd(blk(b_tb, i, p)), _bd(blk(k_tb, i, p))], axis=0))
              for s, (i, p) in enumerate(streams)]
    m_ab = [m[:, 0:PAIR] for m in m_both]
    m_ak = [m[:, PAIR:2 * PAIR] for m in m_both]
    n_ab = [jnp.where(strict, m[0:L], 0.0) for m in m_ab]
    m_rb = [jnp.where(incl, m[L:2 * L], 0.0).astype(BF16) for m in m_ab]
    n_ak = [jnp.where(strict, m[0:L], 0.0).astype(BF16) for m in m_ak]
    m_rk = [jnp.where(incl, m[L:2 * L], 0.0).astype(BF16) for m in m_ak]
    tinv = [eye_pair + n for n in n_ab]
    pwb = [n.astype(BF16) for n in n_ab]
    pw = [_dot(x, _bd(x)) for x in pwb]
    for _ in range(int(math.log2(L)) - 2):
        pwb = [x.astype(BF16) for x in pw]
        both = [_dot(jnp.concatenate([pwb[s], tinv[s].astype(BF16)], axis=0), _bd(pwb[s])) for s in range(ns)]
        pw = [x[0:L] for x in both]
        tinv = [tinv[s] + both[s][L:2 * L] for s in range(ns)]
    pwb = [x.astype(BF16) for x in pw]
    tinv = [tinv[s] + _dot(tinv[s].astype(BF16), _bd(pwb[s])) for s in range(ns)]
    tinvb = [x.astype(BF16) for x in tinv]
    nv_mv = [_dot(jnp.concatenate([n_ak[s], m_rk[s]], axis=0), _bd(blk(vb, i, p))) for s, (i, p) in enumerate(streams)]
    wu = [_dot(tinvb[s], jnp.concatenate([_bd(blk(a_tb, i, p)), _bd(nv_mv[s][0:L].astype(BF16))], axis=1))
          for s, (i, p) in enumerate(streams)]
    wub = [x.astype(BF16) for x in wu]
    qy = [_dot(m_rb[s], jnp.concatenate([_bd(wub[s][:, 0:PAIR]), _bd(wub[s][:, PAIR:2 * PAIR])], axis=1))
          for s in range(ns)]
    q = [(blk(r_t, i, p) + qy[s][:, 0:PAIR]).astype(BF16) for s, (i, p) in enumerate(streams)]
    y_loc = [qy[s][:, PAIR:2 * PAIR] + nv_mv[s][L:2 * L] for s in range(ns)]
    zeros_b = jnp.zeros((L, PAIR), BF16)
    mg = [_dot_tn(jnp.concatenate([wub[s], jnp.concatenate([zeros_b, blk(vb, i, p)], axis=1)], axis=0),
                  jnp.concatenate([blk(b_tb, i, p), blk(k_tb, i, p)], axis=0))
          for s, (i, p) in enumerate(streams)]
    p_end = [e_in[(i + 1) * L - 1:(i + 1) * L, p * PAIR:(p + 1) * PAIR] for i, p in streams]
    m_t = [(jnp.where(same_head, mg[s][0:PAIR], 0.0) * p_end[s]).astype(BF16) for s in range(ns)]
    g_t = [jnp.where(lane_lo, mg[s][PAIR:PAIR + RWKV_HEAD], mg[s][PAIR + RWKV_HEAD:2 * PAIR]) * p_end[s]
           for s in range(ns)]

    y_rows = []
    for i in range(G):
        y_pairs = []
        for p in range(RWKV_PAIRS):
            s = i * RWKV_PAIRS + p
            s0 = s_scr[p]
            s0b = s0.astype(BF16)
            y_pairs.append(_dot_nt(q[s], _bd(s0b)) + y_loc[s])
            s_scr[p] = s0 * p_end[s] + _dot(s0b, m_t[s]) + g_t[s]
        y_rows.append(jnp.concatenate(y_pairs, axis=-1))
    y = jnp.concatenate(y_rows, axis=0)
    y_ref[...] = _rwkv_finish(y, r, k, v, g, rk_ref, lng_ref, lnb_ref)

    @pl.when(c == pl.num_programs(1) - 1)
    def _():
        sout_ref[0] = s_scr[...]
        shift_ref[0] = upad_scr[pad - 1:pad, :]


_RWKV_PARAM_NAMES = ("mu", "w0", "w2", "a0", "a2", "g2", "k_k", "k_a", "r_k", "ln_g", "ln_b")


def _rwkv(u, p, *, batch, seq):
    rows = RWKV_CHUNK * RWKV_GROUP
    nc = seq // rows
    params = [p[n] for n in _RWKV_PARAM_NAMES]
    sspec = pl.BlockSpec((1, RWKV_PAIRS, RWKV_HEAD, PAIR), lambda b, c: (b, 0, 0, 0))
    y, shift, s_last = pl.pallas_call(
        functools.partial(_rwkv_body, chunk=RWKV_CHUNK, group=RWKV_GROUP),
        grid=(batch, nc),
        in_specs=[pl.BlockSpec((rows, RWKV_PROJ), lambda b, c: (b * nc + c, 0))] + [_pspec(a) for a in params],
        out_specs=[pl.BlockSpec((rows, GROUP_WIDTH), lambda b, c: (b * nc + c, 0)),
                   pl.BlockSpec((1, 1, RWKV_PROJ), lambda b, c: (b, 0, 0)), sspec],
        out_shape=[jax.ShapeDtypeStruct((batch * seq, GROUP_WIDTH), F32),
                   jax.ShapeDtypeStruct((batch, 1, RWKV_PROJ), F32),
                   jax.ShapeDtypeStruct((batch, RWKV_PAIRS, RWKV_HEAD, PAIR), F32)],
        scratch_shapes=[pltpu.VMEM((SUBLANES + rows, RWKV_PROJ), F32),
                        pltpu.VMEM((RWKV_PAIRS, RWKV_HEAD, PAIR), F32)],
        compiler_params=_cparams("parallel", "arbitrary"),
        name="rwkv",
    )(u, *[_parg(a) for a in params])
    s_last = s_last.reshape(batch, RWKV_PAIRS, RWKV_HEAD, 2, RWKV_HEAD).transpose(0, 1, 3, 2, 4).reshape(
        batch, RWKV_HEADS, RWKV_HEAD, RWKV_HEAD)
    return y, shift.reshape(batch, RWKV_PROJ), s_last


def _rwkv_step_body(u_ref, shift0_ref, s0_ref, *rest, seq, batch, layer):
    sdone_ref, rest = (rest[0], rest[1:]) if layer else (None, rest)
    (mu_ref, w0_ref, w2_ref, a0_ref, a2_ref, g2_ref, kk_ref, ka_ref, rk_ref, lng_ref, lnb_ref, y_ref, sout_ref,
     r_scr, w_scr, k_scr, b_scr, nkk_scr, v_scr, y_scr) = rest
    T, B = seq, batch
    j = pl.program_id(0)
    if layer:
        sout_ref[0:layer] = sdone_ref[...]
    tiles = RWKV_STEP_TILES

    def pointwise(t):
        u = u_ref[t * B:(t + 1) * B, :]
        prev = shift0_ref[...] if t == 0 else u_ref[(t - 1) * B:t * B, :]
        return _rwkv_pointwise(u, prev, mu_ref, w0_ref, w2_ref, a0_ref, a2_ref, g2_ref, kk_ref, ka_ref)

    @pl.when(j == 0)
    def _():
        for t in range(T):
            r, k, v, logdecay, a, _, kk = pointwise(t)
            r_scr[t] = r.T
            w_scr[t] = jnp.exp(logdecay).T
            k_scr[t] = k.T
            b_scr[t] = (kk * a).T
            nkk_scr[t] = (-kk).T
            v_scr[t] = v.T

    i0 = j * tiles
    keys = pl.ds(pl.multiple_of((i0 // RWKV_HEAD) * RWKV_HEAD, RWKV_HEAD), RWKV_HEAD)
    for q in range(tiles):
        vi = pl.ds(i0 + q, 1)
        s = s0_ref[q]
        for t in range(T):
            sa = jnp.sum(s * nkk_scr[t, keys, :], axis=0, keepdims=True)
            s = s * w_scr[t, keys, :] + k_scr[t, keys, :] * v_scr[t, vi, :] + b_scr[t, keys, :] * sa
            y_scr[t, vi, :] = jnp.sum(s * r_scr[t, keys, :], axis=0, keepdims=True)
        sout_ref[layer, q] = s

    @pl.when(j == pl.num_programs(0) - 1)
    def _():
        for t in range(T):
            r, k, v, _, _, g, _ = pointwise(t)
            y_ref[t * B:(t + 1) * B, :] = _rwkv_finish(y_scr[t].T, r, k, v, g, rk_ref, lng_ref, lnb_ref)


def _rwkv_step(u, shift0, s_all, s_done, p, *, batch, seq, layer):
    n = batch * seq
    srows = RWKV_HEADS * RWKV_HEAD
    params = [p[nm] for nm in _RWKV_PARAM_NAMES]
    sspec, prev_specs, sout_spec = _layer_state_specs(layer, (RWKV_STEP_TILES, RWKV_HEAD, batch), 0)
    prev_args = [s_done] if layer else []
    tposed = pltpu.VMEM((seq, GROUP_WIDTH, batch), F32)
    return pl.pallas_call(
        functools.partial(_rwkv_step_body, seq=seq, batch=batch, layer=layer),
        grid=(srows // RWKV_STEP_TILES,),
        in_specs=[_full_spec((n, RWKV_PROJ)), _full_spec((batch, RWKV_PROJ)), sspec] + prev_specs
                 + [_pspec(a) for a in params],
        out_specs=[_full_spec((n, GROUP_WIDTH)), sout_spec],
        out_shape=[jax.ShapeDtypeStruct((n, GROUP_WIDTH), F32),
                   jax.ShapeDtypeStruct((layer + 1, srows, RWKV_HEAD, batch), F32)],
        scratch_shapes=[tposed] * 7,
        compiler_params=_cparams("arbitrary"),
        name="rwkv_step",
    )(u, shift0, s_all, *prev_args, *[_parg(a) for a in params])


def _s5_body(u_ref, hre0_ref, him0_ref, are_ref, aim_ref, bmat_ref, cmat_ref, d_ref, gw_ref, gb_ref,
             y_ref, hre_ref, him_ref, hs_scr, tm_scr, *, steps, batch_major):
    c = pl.program_id(1)
    ns = S5_WIDTH
    bsub = SUBLANES

    @pl.when(c == 0)
    def _():
        hre_ref[...] = hre0_ref[...]
        him_ref[...] = him0_ref[...]

    if batch_major:
        for b in range(bsub):
            tm_scr[:, b, :] = u_ref[b]
        u = tm_scr[...].reshape(steps * bsub, GROUP_WIDTH)
    else:
        u = u_ref[...].reshape(steps * bsub, GROUP_WIDTH)
    are = jnp.broadcast_to(are_ref[...], (bsub, ns))
    aim = jnp.broadcast_to(aim_ref[...], (bsub, ns))
    hre, him = hre_ref[...], him_ref[...]
    sub = min(S5_SUB, steps)
    rows = sub * bsub
    outs = []
    hs_scr[...] = _dot(u.astype(BF16), bmat_ref[...])
    for k in range(steps // sub):
        r0 = k * rows
        u_k = u[r0:r0 + rows]
        for t in range(sub):
            rs = slice(r0 + t * bsub, r0 + (t + 1) * bsub)
            hre, him = (are * hre - aim * him + hs_scr[rs, 0:ns], are * him + aim * hre + hs_scr[rs, ns:2 * ns])
            hs_scr[rs, 0:ns] = hre
            hs_scr[rs, ns:2 * ns] = him
        y = _dot(hs_scr[r0:r0 + rows, :].astype(BF16), cmat_ref[...]) + u_k * d_ref[...]
        y = _gelu_tanh(y)
        yy = _dot(y.astype(BF16), gw_ref[...]) + gb_ref[...]
        outs.append(yy[:, 0:GROUP_WIDTH] * _sigmoid(yy[:, GROUP_WIDTH:2 * GROUP_WIDTH]))
    hre_ref[...] = hre
    him_ref[...] = him
    out = jnp.concatenate(outs, axis=0).reshape(steps, bsub, GROUP_WIDTH)
    if batch_major:
        tm_scr[...] = out
        for b in range(bsub):
            y_ref[b] = tm_scr[:, b, :]
    else:
        y_ref[...] = out


def _time_specs(u, batch_major, chunk):
    bsub = SUBLANES
    if batch_major:
        batch, seq, _ = u.shape
        steps = min(chunk, seq)
        spec = pl.BlockSpec((bsub, steps, GROUP_WIDTH), lambda b, c: (b, c, 0))
    else:
        seq, batch, _ = u.shape
        steps = min(chunk, seq)
        spec = pl.BlockSpec((steps, bsub, GROUP_WIDTH), lambda b, c: (c, b, 0))
    return batch, seq, steps, spec


def _s5(u, hre0, him0, lp, *, batch_major):
    batch, seq, steps, tspec = _time_specs(u, batch_major, TM_CHUNK)
    bsub = SUBLANES
    hspec = pl.BlockSpec((bsub, S5_WIDTH), lambda b, c: (b, 0))
    consts = (lp["s5_are"], lp["s5_aim"], lp["s5_bmat"], lp["s5_cmat"], lp["s5_d"], lp["s5_gw"], lp["s5_gb"])
    return pl.pallas_call(
        functools.partial(_s5_body, steps=steps, batch_major=batch_major),
        grid=(batch // bsub, seq // steps),
        in_specs=[tspec, hspec, hspec] + [_pspec(a) for a in consts],
        out_specs=[tspec, hspec, hspec],
        out_shape=[jax.ShapeDtypeStruct(u.shape, F32),
                   jax.ShapeDtypeStruct((batch, S5_WIDTH), F32),
                   jax.ShapeDtypeStruct((batch, S5_WIDTH), F32)],
        scratch_shapes=[pltpu.VMEM((steps * bsub, 2 * S5_WIDTH), F32),
                        pltpu.VMEM((steps, bsub, GROUP_WIDTH), F32)],
        compiler_params=_cparams("parallel", "arbitrary"),
        name="s5",
    )(u, hre0, him0, *[_parg(a) for a in consts])


def _pool_body(u_ref, buf0_ref, pw_ref, sc_ref, y_ref, buf_ref, f_scr, tm_scr, *, steps, pos0, batch_major):
    c = pl.program_id(1)
    bsub = SUBLANES
    GW = GROUP_WIDTH
    halo = POOL_BUF + 1

    @pl.when(c == 0)
    def _():
        f_scr[0] = jnp.zeros((bsub, GW), F32)
        f_scr[1:halo] = buf0_ref[...]

    if batch_major:
        for b in range(bsub):
            f_scr[halo:halo + steps, b, :] = u_ref[b]
    else:
        f_scr[halo:halo + steps] = u_ref[...]
    f = f_scr[...]
    u = f[halo:halo + steps]
    s2 = f[1:] + f[:-1]
    s4 = s2[2:] + s2[:-2]
    s8 = s4[4:] + s4[:-4]
    s16 = s8[8:] + s8[:-8]
    f_scr[0:halo] = f[steps:steps + halo]
    lane = lax.broadcasted_iota(jnp.int32, (steps, bsub, GW), 2)
    tpos = lax.broadcasted_iota(jnp.int32, (steps, bsub, GW), 0) + (pos0 + 1) + c * steps
    win = jnp.where(lane < POOL_CH, s2[halo - 1:halo - 1 + steps],
                    jnp.where(lane < 2 * POOL_CH, s4[halo - 3:halo - 3 + steps],
                              jnp.where(lane < 3 * POOL_CH, s8[halo - 7:halo - 7 + steps],
                                        s16[halo - 15:halo - 15 + steps])))
    wlen = jnp.where(lane < POOL_CH, POOL_WINDOWS[0],
                     jnp.where(lane < 2 * POOL_CH, POOL_WINDOWS[1],
                               jnp.where(lane < 3 * POOL_CH, POOL_WINDOWS[2], POOL_WINDOWS[3])))
    cnt = jnp.minimum(tpos, wlen).astype(F32)
    pooled = (win / cnt - u).reshape(steps * bsub, GW)
    y = (_dot(pooled.astype(BF16), pw_ref[...]) * sc_ref[...]).reshape(steps, bsub, GW)
    if batch_major:
        tm_scr[...] = y
        for b in range(bsub):
            y_ref[b] = tm_scr[:, b, :]
    else:
        y_ref[...] = y

    @pl.when(c == pl.num_programs(1) - 1)
    def _():
        buf_ref[...] = f_scr[1:halo]


def _pool(u, buf0, lp, *, pos0, batch_major, layer=None):
    batch, seq, steps, tspec = _time_specs(u, batch_major, POOL_CHUNK)
    bsub = SUBLANES
    bblock =(POOL_BUF, bsub, GROUP_WIDTH)
    bspec = pl.BlockSpec(bblock, lambda b, c: (0, b, 0))
    if layer is None:
        bspec_in = bspec
    else:
        bspec_in = pl.BlockSpec((None,) + bblock, lambda b, c: (layer, 0, b, 0))
    return pl.pallas_call(
        functools.partial(_pool_body, steps=steps, pos0=pos0, batch_major=batch_major),
        grid=(batch // bsub, seq // steps),
        in_specs=[tspec, bspec_in, _pspec(lp["pool_w"]), _pspec(lp["pool_scale"])],
        out_specs=[tspec, bspec],
        out_shape=[jax.ShapeDtypeStruct(u.shape, F32), jax.ShapeDtypeStruct((POOL_BUF, batch, GROUP_WIDTH), F32)],
        scratch_shapes=[pltpu.VMEM((POOL_BUF + 1 + steps, bsub, GROUP_WIDTH), F32),
                        pltpu.VMEM((steps, bsub, GROUP_WIDTH), F32)],
        compiler_params=_cparams("parallel", "arbitrary"),
        name="pool",
    )(u, buf0, _parg(lp["pool_w"]), _parg(lp["pool_scale"]))


def _block_diag(blocks):
    n, g, r, c = blocks.shape
    eye = jnp.eye(g, dtype=blocks.dtype)
    return (eye[None, :, None, :, None] * blocks[:, :, :, None, :]).reshape(n, g * r, g * c)


def _stacked_params(P):
    row = lambda a: a.reshape(a.shape[0], 1, -1)
    pad_lanes = lambda a: jnp.pad(a, ((0, 0), (0, LANES - a.shape[1])))
    bf = lambda a: a.astype(BF16)

    lam = lax.complex(P["s5_lam_re"], P["s5_lam_im"])
    a_bar = jnp.exp(lam * jnp.exp(P["s5_log_step"])[..., None])
    b_bar = ((a_bar - 1.0) / lam)[..., None] * lax.complex(P["s5_b_re"], P["s5_b_im"])
    b_t = jnp.swapaxes(b_bar, 2, 3)
    bmat = jnp.concatenate([_block_diag(jnp.real(b_t)), _block_diag(jnp.imag(b_t))], axis=2)
    c_t = jnp.swapaxes(lax.complex(P["s5_c_re"], P["s5_c_im"]), 2, 3)
    cmat = jnp.concatenate([_block_diag(jnp.real(c_t)), -_block_diag(jnp.imag(c_t))], axis=1)

    out = dict(
        norm_ffn1=row(P["norm_ffn1"]), ffn1_in=P["ffn1_in"], ffn1_out=P["ffn1_out"],
        norm_mix=row(P["norm_mix"]),
        w_in=jnp.transpose(P["w_in"], (2, 0, 1)),
        conv_w=P["ssd_conv_w"], conv_b=row(P["ssd_conv_b"]),
        dt_bias=row(pad_lanes(P["ssd_dt_bias"])), a_log=row(pad_lanes(P["ssd_a_log"])),
        a_neg_exp=row(jnp.repeat(-jnp.exp(P["ssd_a_log"]), SSD_HEAD_DIM, axis=1)),
        d_skip=row(jnp.repeat(P["ssd_d"], SSD_HEAD_DIM, axis=1)), ssd_norm=row(P["ssd_norm"]),
        s5_are=row(jnp.real(a_bar)), s5_aim=row(jnp.imag(a_bar)), s5_bmat=bf(bmat), s5_cmat=bf(cmat),
        s5_d=row(P["s5_d"]), s5_gw=bf(P["s5_glu_w"]), s5_gb=row(P["s5_glu_b"]),
        pool_w=bf(_block_diag(P["pool_w"])), pool_scale=row(P["pool_scale"]),
        w_out=bf(P["w_out"]),
        norm_ffn2=row(P["norm_ffn2"]), ffn2_in=P["ffn2_in"], ffn2_out=P["ffn2_out"],
    )
    for name in _RWKV_PARAM_NAMES:
        a = P["rwkv_" + name]
        out["rwkv_" + name] = bf(a) if name in ("w2", "a2", "g2") else row(a)
    return out


def _layer_params(stacked, l):
    lp = {k: _Layered((v, l)) for k, v in stacked.items()}
    lp["rwkv"] = {n: lp["rwkv_" + n] for n in _RWKV_PARAM_NAMES}
    lp["head_expand"] = jnp.pad(jnp.repeat(jnp.eye(SSD_HEADS, dtype=F32), SSD_HEAD_DIM, axis=1),
                                ((0, LANES - SSD_HEADS), (0, 0)))
    return lp


def _mixers_prompt(lp, proj, *, batch, seq):
    z, xbc, ur, us5, upool, dtr = proj
    y_ssd, conv_new, ssd_new = _ssd(z, xbc, dtr, lp, batch=batch, seq=seq)
    y_rwkv, shift_new, rwkv_new = _rwkv(ur, lp["rwkv"], batch=batch, seq=seq)
    zeros = jnp.zeros((batch, S5_WIDTH), F32)
    bm = lambda a: a.reshape(batch, seq, a.shape[-1])
    rows = lambda a: a.reshape(batch * seq, a.shape[-1])
    y_s5, s5re, s5im = _s5(bm(us5), zeros, zeros, lp, batch_major=True)
    y_pool, pool_new = _pool(bm(upool), jnp.zeros((POOL_BUF, batch, GROUP_WIDTH), F32), lp, pos0=0,
                             batch_major=True)
    ys = (y_ssd, y_rwkv, rows(y_s5), rows(y_pool))
    states = (conv_new, ssd_new, shift_new, rwkv_new, s5re.reshape(batch, S5_GROUPS, S5_STATE),
              s5im.reshape(batch, S5_GROUPS, S5_STATE), jnp.swapaxes(pool_new, 0, 1))
    return ys, states


def _mixers_decode(lp, proj, states, done, *, batch, seq, layer):
    z, xbc, ur, us5, upool, dtr = proj
    shift0, s5re0, s5im0 = (states[i][layer] for i in (2, 4, 5))
    ssd_done, rwkv_done = (done[1], done[3]) if layer else (None, None)
    y_ssd, conv_new, ssd_new = _ssd_step(z, xbc, dtr, states[0], states[1], ssd_done, lp, batch=batch, seq=seq,
                                         layer=layer)
    y_rwkv, rwkv_new = _rwkv_step(ur, shift0, states[3], rwkv_done, lp["rwkv"], batch=batch, seq=seq, layer=layer)
    shift_new = ur[(seq - 1) * batch:, :]
    tm = lambda a: a.reshape(seq, batch, a.shape[-1])
    y_s5, s5re, s5im = _s5(tm(us5), s5re0.reshape(batch, S5_WIDTH), s5im0.reshape(batch, S5_WIDTH), lp,
                           batch_major=False)
    y_pool, pool_new = _pool(tm(upool), states[6], lp, pos0=PAST_LEN, batch_major=False, layer=layer)
    rows = lambda a: a.reshape(seq * batch, a.shape[-1])
    ys = (y_ssd, y_rwkv, rows(y_s5), rows(y_pool))
    new_states = (jnp.swapaxes(conv_new, 0, 1), ssd_new, shift_new, rwkv_new,
                  s5re.reshape(batch, S5_GROUPS, S5_STATE), s5im.reshape(batch, S5_GROUPS, S5_STATE),
                  jnp.swapaxes(pool_new, 0, 1))
    return ys, new_states


_WIDTHS = (GROUP_WIDTH, SSD_CONV_DIM, RWKV_PROJ, GROUP_WIDTH, GROUP_WIDTH, LANES)


def _trunk(x_p, x_s, layer_params, norm_final, mixers_p, mixers_s):
    st_p, st_s = [], []
    mix_p, mix_s, lp = None, None, None
    for l, lp_next in enumerate(layer_params):
        if l > 0:
            x_s, wg, wu, wo = _ffn_cast(x_s, lp["norm_ffn2"], lp["ffn2_in"], lp["ffn2_out"], mix=mix_s, wmix=lp["w_out"])
            x_p = _ffn(x_p, lp["norm_ffn2"], wg, wu, wo, mix=mix_p, wmix=lp["w_out"])
        lp = lp_next
        x_s, wg, wu, wo = _ffn_cast(x_s, lp["norm_ffn1"], lp["ffn1_in"], lp["ffn1_out"])
        x_p = _ffn(x_p, lp["norm_ffn1"], wg, wu, wo)
        proj_s, w_all = _inproj_cast(x_s, lp["norm_mix"], lp["w_in"], _WIDTHS)
        mix_p, st = mixers_p(l, lp, _inproj(x_p, lp["norm_mix"], w_all, _WIDTHS), st_p[-1] if st_p else None)
        st_p.append(st)
        mix_s, st = mixers_s(l, lp, proj_s, st_s[-1] if st_s else None)
        st_s.append(st)
    x_s, wg, wu, wo = _ffn_cast(x_s, lp["norm_ffn2"], lp["ffn2_in"], lp["ffn2_out"], mix=mix_s, wmix=lp["w_out"],
                                gf=norm_final)
    x_p = _ffn(x_p, lp["norm_ffn2"], wg, wu, wo, mix=mix_p, wmix=lp["w_out"], gf=norm_final)
    return (x_p, x_s), (st_p, st_s)


def kernel(x_prompt, x_sample, state_ssd_conv, state_ssd, state_rwkv_shift, state_rwkv, state_s5_re, state_s5_im, state_pool, norm_ffn1, ffn1_in, ffn1_out, norm_mix, w_in, ssd_conv_w, ssd_conv_b, ssd_dt_bias, ssd_a_log, ssd_d, ssd_norm, rwkv_mu, rwkv_w0, rwkv_w2, rwkv_a0, rwkv_a2, rwkv_g2, rwkv_k_k, rwkv_k_a, rwkv_r_k, rwkv_ln_g, rwkv_ln_b, s5_lam_re, s5_lam_im, s5_log_step, s5_b_re, s5_b_im, s5_c_re, s5_c_im, s5_d, s5_glu_w, s5_glu_b, pool_w, pool_scale, w_out, norm_ffn2, ffn2_in, ffn2_out, norm_final):
    P = dict(norm_ffn1=norm_ffn1, ffn1_in=ffn1_in, ffn1_out=ffn1_out, norm_mix=norm_mix, w_in=w_in,
             ssd_conv_w=ssd_conv_w, ssd_conv_b=ssd_conv_b, ssd_dt_bias=ssd_dt_bias, ssd_a_log=ssd_a_log,
             ssd_d=ssd_d, ssd_norm=ssd_norm, rwkv_mu=rwkv_mu, rwkv_w0=rwkv_w0, rwkv_w2=rwkv_w2, rwkv_a0=rwkv_a0,
             rwkv_a2=rwkv_a2, rwkv_g2=rwkv_g2, rwkv_k_k=rwkv_k_k, rwkv_k_a=rwkv_k_a,
             rwkv_r_k=rwkv_r_k.reshape(rwkv_r_k.shape[0], -1), rwkv_ln_g=rwkv_ln_g, rwkv_ln_b=rwkv_ln_b,
             s5_lam_re=s5_lam_re, s5_lam_im=s5_lam_im, s5_log_step=s5_log_step, s5_b_re=s5_b_re, s5_b_im=s5_b_im,
             s5_c_re=s5_c_re, s5_c_im=s5_c_im, s5_d=s5_d, s5_glu_w=s5_glu_w, s5_glu_b=s5_glu_b, pool_w=pool_w,
             pool_scale=pool_scale, w_out=w_out, norm_ffn2=norm_ffn2, ffn2_in=ffn2_in, ffn2_out=ffn2_out)
    depth = norm_ffn1.shape[0]
    bp, tp, d = x_prompt.shape
    bs, ts, _ = x_sample.shape
    stacked = _stacked_params(P)
    layer_params = [_layer_params(stacked, l) for l in range(depth)]
    gf = norm_final.reshape(1, -1)
    sample_states = (state_ssd_conv, state_ssd, state_rwkv_shift, state_rwkv, state_s5_re, state_s5_im, state_pool)
    rwkv_rows = RWKV_HEADS * RWKV_HEAD
    decode_states = (jnp.swapaxes(state_ssd_conv, 1, 2), state_ssd, state_rwkv_shift,
                     jnp.transpose(state_rwkv, (0, 2, 3, 4, 1)).reshape(depth, rwkv_rows, RWKV_HEAD, bs),
                     state_s5_re, state_s5_im, jnp.swapaxes(state_pool, 1, 2))

    x_s = jnp.swapaxes(x_sample, 0, 1).reshape(ts * bs, d)
    (y_p, y_s), (st_p, st_s) = _trunk(
        x_prompt.reshape(bp * tp, d), x_s, layer_params, gf,
        lambda l, lp, proj, done: _mixers_prompt(lp, proj, batch=bp, seq=tp),
        lambda l, lp, proj, done: _mixers_decode(lp, proj, decode_states, done, batch=bs, seq=ts, layer=l))
    outs = [y_p.reshape(bp, tp, d), jnp.swapaxes(y_s.reshape(ts, bs, d), 0, 1)]
    for i, ref_state in enumerate(sample_states):
        outs.append(jnp.stack([st[i] for st in st_p]))
        if i == 1:
            outs.append(st_s[-1][i].reshape(ref_state.shape))
        elif i == 3:
            s_new = st_s[-1][i].reshape(depth, RWKV_HEADS, RWKV_HEAD, RWKV_HEAD, bs)
            outs.append(jnp.transpose(s_new, (0, 4, 1, 2, 3)))
        else:
            outs.append(jnp.stack([st[i] for st in st_s]))
    return tuple(outs)
```

```python
import functools
import math

import jax
import jax.numpy as jnp
from jax import lax
from jax.experimental import pallas as pl
from jax.experimental.pallas import tpu as pltpu

F32 = jnp.float32
BF16 = jnp.bfloat16
HIGHEST = lax.Precision.HIGHEST

SUBLANES = 8
LANES = 128
VMEM_LIMIT_BYTES = 56 * 1024 * 1024

GROUP_WIDTH = 256
SSD_HEAD_DIM = 64
SSD_HEADS = 4
SSD_GROUPS = 2
SSD_STATE = 128
SSD_CONV = 4
SSD_CONV_DIM = GROUP_WIDTH + 2 * SSD_GROUPS * SSD_STATE
SSD_CHUNK = 128
SSD_GROUP = 4
LOG2_E = math.log2(math.e)
RWKV_HEAD = 64
RWKV_HEADS = 4
RWKV_PROJ = 1024
RWKV_LN_EPS = 64e-5
RWKV_CHUNK = 64
RWKV_GROUP = 8
S5_GROUPS = 16
S5_STATE = 64
S5_WIDTH = S5_GROUPS * S5_STATE
POOL_WINDOWS = (2, 4, 8, 16)
POOL_CH = 64
POOL_BUF = 15
RMS_EPS = 1e-6
PAST_LEN = 16384

ROW_TILE = 512
FFN_CHUNK = 256
TM_CHUNK = 128
POOL_CHUNK = 256
S5_SUB = 64
SSD_STEP_TILES = 16
RWKV_STEP_TILES = 16


def _cparams(*sem):
    return pltpu.CompilerParams(dimension_semantics=sem, vmem_limit_bytes=VMEM_LIMIT_BYTES)


def _dot(a, b, **kw):
    return jnp.dot(a, b, preferred_element_type=F32, **kw)


def _dot_nt(a, b):
    return lax.dot_general(a, b, (((1,), (1,)), ((), ())), preferred_element_type=F32)


def _dot_tn(a, b):
    return lax.dot_general(a, b, (((0,), (0,)), ((), ())), preferred_element_type=F32)


def _sigmoid(x):
    return 0.5 * jnp.tanh(0.5 * x) + 0.5


def _silu(x):
    return x * _sigmoid(x)


def _softplus(x):
    return jnp.maximum(x, 0.0) + jnp.log(1.0 + jnp.exp(-jnp.abs(x)))


def _gelu_tanh(x):
    c = math.sqrt(2.0 / math.pi)
    return x * (0.5 * (1.0 + jnp.tanh(c * (x + 0.044715 * (x * x * x)))))


def _rms(x, g):
    return x * lax.rsqrt(jnp.mean(x * x, axis=-1, keepdims=True) + RMS_EPS) * g


def _full_spec(shape):
    n = len(shape)
    return pl.BlockSpec(shape, lambda *_: (0,) * n)


class _Layered(tuple):
    pass


def _pspec(p, single=False):
    mode = pl.Buffered(1) if single else None
    if isinstance(p, _Layered):
        a, l = p
        return pl.BlockSpec((None,) + a.shape[1:], lambda *_: (l,) + (0,) * (a.ndim - 1), pipeline_mode=mode)
    n = p.ndim
    return pl.BlockSpec(p.shape, lambda *_: (0,) * n, pipeline_mode=mode)


def _parg(p):
    return p[0] if isinstance(p, _Layered) else p


def _mix_residual(x, y_refs, wmix_ref):
    for j, y_ref in enumerate(y_refs):
        x = x + _dot(y_ref[...].astype(BF16), wmix_ref[j * GROUP_WIDTH:(j + 1) * GROUP_WIDTH, :])
    return x


def _swiglu_chunk(h, wg, wu, wo):
    act = (_silu(_dot(h, wg)) * _dot(h, wu)).astype(BF16)
    return _dot(act, wo)


def _ffn_body(*refs, has_mix, final_norm):
    it = iter(refs)
    x = next(it)[...]
    if has_mix:
        y_refs = [next(it) for _ in range(4)]
        x = _mix_residual(x, y_refs, next(it))
    g_ref, wg_ref, wu_ref, wo_ref = next(it), next(it), next(it), next(it)
    gf_ref = next(it) if final_norm else None
    o_ref = next(it)
    h = _rms(x, g_ref[...]).astype(BF16)
    acc = jnp.zeros_like(x)
    for c in range(wo_ref.shape[0] // FFN_CHUNK):
        cols = slice(c * FFN_CHUNK, (c + 1) * FFN_CHUNK)
        acc = acc + _swiglu_chunk(h, wg_ref[:, cols], wu_ref[:, cols], wo_ref[cols, :])
    x = x + 0.5 * acc
    if final_norm:
        x = _rms(x, gf_ref[...])
    o_ref[...] = x


def _ffn(x, g, wg, wu, wo, mix=None, wmix=None, gf=None):
    rows, d = x.shape
    row_spec = lambda w: pl.BlockSpec((ROW_TILE, w), lambda i: (i, 0))
    args, specs = [x], [row_spec(d)]
    if mix is not None:
        for y in mix:
            args.append(y)
            specs.append(row_spec(y.shape[1]))
        args.append(_parg(wmix))
        specs.append(_pspec(wmix, single=True))
    for a in (g, wg, wu, wo) + ((gf,) if gf is not None else ()):
        args.append(_parg(a))
        specs.append(_pspec(a, single=True))
    return pl.pallas_call(
        functools.partial(_ffn_body, has_mix=mix is not None, final_norm=gf is not None),
        grid=(rows // ROW_TILE,),
        in_specs=specs,
        out_specs=row_spec(d),
        out_shape=jax.ShapeDtypeStruct((rows, d), F32),
        compiler_params=_cparams("parallel"),
        name="ffn",
    )(*args)


def _ffn_cast_body(*refs, has_mix, final_norm):
    it = iter(refs)
    x_ref = next(it)
    if has_mix:
        y_refs = [next(it) for _ in range(4)]
        wmix_ref = next(it)
    g_ref, wg_ref, wu_ref, wo_ref = next(it), next(it), next(it), next(it)
    gf_ref = next(it) if final_norm else None
    o_ref, wg_out, wu_out, wo_out, x_scr, h_scr, acc_scr = (next(it) for _ in range(7))
    c = pl.program_id(0)

    @pl.when(c == 0)
    def _():
        x = x_ref[...]
        if has_mix:
            x = _mix_residual(x, y_refs, wmix_ref)
        x_scr[...] = x
        h_scr[...] = _rms(x, g_ref[...]).astype(BF16)
        acc_scr[...] = jnp.zeros(acc_scr.shape, F32)

    wg = wg_ref[...].astype(BF16)
    wu = wu_ref[...].astype(BF16)
    wo = wo_ref[...].astype(BF16)
    wg_out[...] = wg
    wu_out[...] = wu
    wo_out[...] = wo
    acc_scr[...] += _swiglu_chunk(h_scr[...], wg, wu, wo)

    @pl.when(c == pl.num_programs(0) - 1)
    def _():
        x = x_scr[...] + 0.5 * acc_scr[...]
        if final_norm:
            x = _rms(x, gf_ref[...])
        o_ref[...] = x


def _ffn_cast(x, g, wi, wo, mix=None, wmix=None, gf=None):
    rows, d = x.shape
    wi_all, l = wi
    wo_all, _ = wo
    d_ff = wo_all.shape[1]
    nchunks = d_ff // FFN_CHUNK
    args, specs = [x], [_full_spec(x.shape)]
    if mix is not None:
        for y in mix:
            args.append(y)
            specs.append(_full_spec(y.shape))
        args.append(_parg(wmix))
        specs.append(_pspec(wmix, single=True))
    args += [_parg(g), wi_all, wi_all, wo_all]
    specs += [_pspec(g),
              pl.BlockSpec((None, d, FFN_CHUNK), lambda c: (l, 0, c)),
              pl.BlockSpec((None, d, FFN_CHUNK), lambda c: (l, 0, c + nchunks)),
              pl.BlockSpec((None, FFN_CHUNK, d), lambda c: (l, c, 0))]
    if gf is not None:
        args.append(gf)
        specs.append(_full_spec(gf.shape))
    col_spec = pl.BlockSpec((d, FFN_CHUNK), lambda c: (0, c))
    return pl.pallas_call(
        functools.partial(_ffn_cast_body, has_mix=mix is not None, final_norm=gf is not None),
        grid=(nchunks,),
        in_specs=specs,
        out_specs=[_full_spec(x.shape), col_spec, col_spec, pl.BlockSpec((FFN_CHUNK, d), lambda c: (c, 0))],
        out_shape=[jax.ShapeDtypeStruct((rows, d), F32), jax.ShapeDtypeStruct((d, d_ff), BF16),
                   jax.ShapeDtypeStruct((d, d_ff), BF16), jax.ShapeDtypeStruct((d_ff, d), BF16)],
        scratch_shapes=[pltpu.VMEM((rows, d), F32), pltpu.VMEM((rows, d), BF16), pltpu.VMEM((rows, d), F32)],
        compiler_params=_cparams("arbitrary"),
        name="ffn_cast",
    )(*args)


def _inproj_body(x_ref, g_ref, wt_ref, *o_refs):
    h = _rms(x_ref[...], g_ref[...]).astype(BF16)
    off = 0
    for o_ref in o_refs:
        n = o_ref.shape[-1]
        o_ref[...] = _dot_nt(h, wt_ref[off:off + n, :])
        off += n


def _inproj_cast_body(x_ref, g_ref, win_ref, *o_refs, layer):
    *proj_refs, wall_ref = o_refs
    split = GROUP_WIDTH + SSD_CONV_DIM
    wt = win_ref[:, layer, :]
    tail = wt.shape[0] - split - SSD_HEADS
    wall_ref[0:split, :] = wt[0:split].astype(BF16)
    wall_ref[split:split + tail, :] = wt[split + SSD_HEADS:].astype(BF16)
    dt_rows = jnp.concatenate([wt[split:split + SSD_HEADS], jnp.zeros((LANES - SSD_HEADS, wt.shape[1]), F32)], axis=0)
    wall_ref[split + tail:, :] = dt_rows.astype(BF16)
    _inproj_body(x_ref, g_ref, wall_ref, *proj_refs)


def _inproj_cast(x, g, w_in, widths):
    rows, d = x.shape
    wt_all, l = w_in
    outs = pl.pallas_call(
        functools.partial(_inproj_cast_body, layer=l),
        grid=(1,),
        in_specs=[_full_spec(x.shape), _pspec(g),
                  pl.BlockSpec(wt_all.shape, lambda i: (0, 0, 0), pipeline_mode=pl.Buffered(1))],
        out_specs=[_full_spec((rows, n)) for n in widths] + [_full_spec((sum(widths), d))],
        out_shape=[jax.ShapeDtypeStruct((rows, n), F32) for n in widths]
                  + [jax.ShapeDtypeStruct((sum(widths), d), BF16)],
        compiler_params=_cparams("arbitrary"),
        name="inproj_cast",
    )(x, _parg(g), wt_all)
    return outs[:-1], outs[-1]


def _inproj(x, g, w, widths):
    rows, d = x.shape
    row_spec = lambda w_: pl.BlockSpec((ROW_TILE, w_), lambda i: (i, 0))
    return pl.pallas_call(
        _inproj_body,
        grid=(rows // ROW_TILE,),
        in_specs=[row_spec(d), _pspec(g), _pspec(w, single=True)],
        out_specs=[row_spec(n) for n in widths],
        out_shape=[jax.ShapeDtypeStruct((rows, n), F32) for n in widths],
        compiler_params=_cparams("parallel"),
        name="inproj",
    )(x, _parg(g), _parg(w))


def _ssd_body(z_ref, xbc_ref, dt_ref, cw_ref, cb_ref, dtb_ref, alog_ref, dsk_ref, ng_ref,
              y_ref, conv_ref, hout_ref, xpad_scr, h_scr, *, chunk, group):
    L, G = chunk, group
    GL = G * L
    c = pl.program_id(1)
    pad = SUBLANES
    halo = SSD_CONV - 1
    hpg = SSD_HEADS // SSD_GROUPS
    assert hpg == 2 and hpg * SSD_HEAD_DIM == SSD_STATE

    @pl.when(c == 0)
    def _():
        xpad_scr[0:pad, :] = jnp.zeros((pad, SSD_CONV_DIM), F32)
        h_scr[...] = jnp.zeros(h_scr.shape, F32)

    xpad_scr[pad:pad + GL, :] = xbc_ref[...]
    xfull = xpad_scr[...]
    conv = cb_ref[...] + cw_ref[halo:halo + 1, :] * xfull[pad:pad + GL]
    for j in range(halo):
        conv = conv + cw_ref[j:j + 1, :] * pltpu.roll(xfull, halo - j, axis=0)[pad:pad + GL]
    xpad_scr[pad - halo:pad, :] = xpad_scr[pad + GL - halo:pad + GL, :]
    conv = _silu(conv)
    xs = conv[:, 0:GROUP_WIDTH]
    bm = conv[:, GROUP_WIDTH:2 * GROUP_WIDTH].astype(BF16)
    cm = conv[:, 2 * GROUP_WIDTH:3 * GROUP_WIDTH].astype(BF16)

    row = lax.broadcasted_iota(jnp.int32, (L, L), 0)
    col = lax.broadcasted_iota(jnp.int32, (L, L), 1)
    causal = row >= col
    tril = jnp.where(causal, 1.0, 0.0).astype(F32)
    dt = _softplus(dt_ref[...] + dtb_ref[...])
    da = dt * (-jnp.exp(alog_ref[...]) * LOG2_E)
    acs = [_dot(tril, da[i * L:(i + 1) * L, :], precision=HIGHEST) for i in range(G)]
    acs_t = [a.T for a in acs]
    e_acs = [jnp.exp2(a) for a in acs]
    e_end = [jnp.exp2(a[L - 1:L, :] - a) for a in acs]
    e_last = [jnp.exp2(a[L - 1:L, :]) for a in acs]

    keys = [(i, g) for i in range(G) for g in range(SSD_GROUPS)]
    rows_of = lambda x, i: x[i * L:(i + 1) * L]
    lanes_of = lambda x, g: x[:, g * SSD_STATE:(g + 1) * SSD_STATE]
    lane_lo = lax.broadcasted_iota(jnp.int32, (L, hpg * SSD_HEAD_DIM), 1) < SSD_HEAD_DIM
    row_lo = lax.broadcasted_iota(jnp.int32, (hpg * SSD_HEAD_DIM, SSD_STATE), 0) < SSD_HEAD_DIM
    head_cols = lambda a, g: jnp.where(lane_lo, a[:, g * hpg:g * hpg + 1], a[:, g * hpg + 1:g * hpg + 2])
    bg = {(i, g): lanes_of(rows_of(bm, i), g) for i, g in keys}
    cg = {(i, g): lanes_of(rows_of(cm, i), g) for i, g in keys}
    scores = {k: _dot_nt(cg[k], bg[k]) for k in keys}
    xdt = {(i, g): lanes_of(rows_of(xs, i), g) * head_cols(rows_of(dt, i), g) for i, g in keys}
    decay = {(i, h): jnp.exp2(jnp.where(causal, acs[i][:, h:h + 1] - acs_t[i][h:h + 1, :], -jnp.inf))
             for i in range(G) for h in range(SSD_HEADS)}
    p_mat = {(i, g): jnp.concatenate([(scores[(i, g)] * decay[(i, g * hpg + k)]).astype(BF16) for k in range(hpg)],
                                     axis=1) for i, g in keys}
    y_in = {k: _dot(p_mat[k], _bd(xdt[k].astype(BF16))) for k in keys}
    st = {(i, g): _dot_tn((xdt[(i, g)] * head_cols(e_end[i], g)).astype(BF16), bg[(i, g)]) for i, g in keys}

    y_rows = []
    for i in range(G):
        ys = []
        for g in range(SSD_GROUPS):
            h_prev = h_scr[g * hpg:(g + 1) * hpg].reshape(hpg * SSD_HEAD_DIM, SSD_STATE)
            ys.append(y_in[(i, g)] + _dot_nt(cg[(i, g)], h_prev.astype(BF16)) * head_cols(e_acs[i], g))
            keep = jnp.where(row_lo, e_last[i][:, g * hpg:g * hpg + 1], e_last[i][:, g * hpg + 1:g * hpg + 2])
            h_scr[g * hpg:(g + 1) * hpg] = (h_prev * keep + st[(i, g)]).reshape(hpg, SSD_HEAD_DIM, SSD_STATE)
        y_rows.append(jnp.concatenate(ys, axis=-1))
    y = jnp.concatenate(y_rows, axis=0) + xs * dsk_ref[...]
    y = y * _silu(z_ref[...])
    y_ref[...] = _rms(y, ng_ref[...])

    @pl.when(c == pl.num_programs(1) - 1)
    def _():
        hout_ref[0] = h_scr[...]
        conv_ref[0] = xpad_scr[pad - halo:pad, :]


def _ssd(z, xbc, dtr, lp, *, batch, seq):
    chunk = SSD_CHUNK
    rows = chunk * SSD_GROUP
    nc = seq // rows
    rspec = lambda w: pl.BlockSpec((rows, w), lambda b, c: (b * nc + c, 0))
    consts = (lp["conv_w"], lp["conv_b"], lp["dt_bias"], lp["a_log"], lp["d_skip"], lp["ssd_norm"])
    return pl.pallas_call(
        functools.partial(_ssd_body, chunk=chunk, group=SSD_GROUP),
        grid=(batch, nc),
        in_specs=[rspec(GROUP_WIDTH), rspec(SSD_CONV_DIM), rspec(LANES)] + [_pspec(a) for a in consts],
        out_specs=[rspec(GROUP_WIDTH),
                   pl.BlockSpec((1, SSD_CONV - 1, SSD_CONV_DIM), lambda b, c: (b, 0, 0)),
                   pl.BlockSpec((1, SSD_HEADS, SSD_HEAD_DIM, SSD_STATE), lambda b, c: (b, 0, 0, 0))],
        out_shape=[jax.ShapeDtypeStruct((batch * seq, GROUP_WIDTH), F32),
                   jax.ShapeDtypeStruct((batch, SSD_CONV - 1, SSD_CONV_DIM), F32),
                   jax.ShapeDtypeStruct((batch, SSD_HEADS, SSD_HEAD_DIM, SSD_STATE), F32)],
        scratch_shapes=[pltpu.VMEM((SUBLANES + rows, SSD_CONV_DIM), F32),
                        pltpu.VMEM((SSD_HEADS, SSD_HEAD_DIM, SSD_STATE), F32)],
        compiler_params=_cparams("parallel", "arbitrary"),
        name="ssd",
    )(z, xbc, dtr, *[_parg(a) for a in consts])


def _ssd_step_body(z_ref, xbc_ref, dt_ref, conv0_ref, h0_ref, *rest, seq, batch, layer):
    hdone_ref, rest = (rest[0], rest[1:]) if layer else (None, rest)
    (cw_ref, cb_ref, dtb_ref, aneg_ref, dsk_ref, ng_ref, hexp_ref, y_ref, conv_ref, hout_ref,
     xs_scr, bm_scr, cm_scr, xdt_scr, dec_scr, y_scr) = rest
    T, B = seq, batch
    if layer:
        hout_ref[0:layer] = hdone_ref[...]
    GW = GROUP_WIDTH
    j = pl.program_id(0)
    tiles = SSD_STEP_TILES

    @pl.when(j == 0)
    def _():
        rows = [conv0_ref[i] for i in range(SSD_CONV - 1)]
        rows += [xbc_ref[t * B:(t + 1) * B, :] for t in range(T)]
        for t in range(T):
            conv = cb_ref[...] + cw_ref[0:1, :] * rows[t]
            for i in range(1, SSD_CONV):
                conv = conv + cw_ref[i:i + 1, :] * rows[t + i]
            conv = _silu(conv)
            xs = conv[:, 0:GW]
            xs_scr[t] = xs
            for g in range(SSD_GROUPS):
                bm_scr[t, g] = conv[:, GW + g * SSD_STATE:GW + (g + 1) * SSD_STATE].T
                cm_scr[t, g] = conv[:, 2 * GW + g * SSD_STATE:2 * GW + (g + 1) * SSD_STATE].T
            dt = _softplus(dt_ref[t * B:(t + 1) * B, :] + dtb_ref[...])
            dte = _dot(dt, hexp_ref[...], precision=HIGHEST)
            xdt_scr[t] = (xs * dte).T
            dec_scr[t] = jnp.exp(dte * aneg_ref[...]).T
        for i in range(SSD_CONV - 1):
            conv_ref[i] = rows[T + i]

    hp0 = j * tiles
    grp = hp0 // (SSD_HEAD_DIM * (SSD_HEADS // SSD_GROUPS))
    for q in range(tiles):
        hp = pl.ds(hp0 + q, 1)
        h = h0_ref[:, q, :].T
        for t in range(T):
            h = h * dec_scr[t, hp, :] + bm_scr[t, grp] * xdt_scr[t, hp, :]
            y_scr[t, hp, :] = jnp.sum(h * cm_scr[t, grp], axis=0, keepdims=True)
        hout_ref[layer, :, q, :] = h.T

    @pl.when(j == pl.num_programs(0) - 1)
    def _():
        for t in range(T):
            y = y_scr[t].T + xs_scr[t] * dsk_ref[...]
            y = y * _silu(z_ref[t * B:(t + 1) * B, :])
            y_ref[t * B:(t + 1) * B, :] = _rms(y, ng_ref[...])


def _layer_state_specs(layer, block, axis):
    idx = lambda first: (lambda j: (first,) + tuple(j if a == axis else 0 for a in range(len(block))))
    cur = pl.BlockSpec((None,) + block, idx(layer))
    prev = [pl.BlockSpec((layer,) + block, idx(0))] if layer else []
    out = pl.BlockSpec((layer + 1,) + block, idx(0))
    return cur, prev, out


def _ssd_step(z, xbc, dtr, conv_all, h_all, h_done, lp, *, batch, seq, layer):
    n = batch * seq
    srows = SSD_HEADS * SSD_HEAD_DIM
    consts = (lp["conv_w"], lp["conv_b"], lp["dt_bias"], lp["a_neg_exp"], lp["d_skip"], lp["ssd_norm"], lp["head_expand"])
    hspec, prev_specs, hout_spec = _layer_state_specs(layer, (batch, SSD_STEP_TILES, SSD_STATE), 1)
    prev_args = [h_done] if layer else []
    cshape = (SSD_CONV - 1, batch, SSD_CONV_DIM)
    return pl.pallas_call(
        functools.partial(_ssd_step_body, seq=seq, batch=batch, layer=layer),
        grid=(srows // SSD_STEP_TILES,),
        in_specs=[_full_spec((n, GROUP_WIDTH)), _full_spec((n, SSD_CONV_DIM)), _full_spec((n, LANES)),
                  pl.BlockSpec((None,) + cshape, lambda j: (layer, 0, 0, 0)), hspec] + prev_specs
                 + [_pspec(a) for a in consts],
        out_specs=[_full_spec((n, GROUP_WIDTH)), _full_spec(cshape), hout_spec],
        out_shape=[jax.ShapeDtypeStruct((n, GROUP_WIDTH), F32),
                   jax.ShapeDtypeStruct(cshape, F32),
                   jax.ShapeDtypeStruct((layer + 1, batch, srows, SSD_STATE), F32)],
        scratch_shapes=[pltpu.VMEM((seq, batch, GROUP_WIDTH), F32),
                        pltpu.VMEM((seq, SSD_GROUPS, SSD_STATE, batch), F32),
                        pltpu.VMEM((seq, SSD_GROUPS, SSD_STATE, batch), F32),
                        pltpu.VMEM((seq, GROUP_WIDTH, batch), F32),
                        pltpu.VMEM((seq, GROUP_WIDTH, batch), F32),
                        pltpu.VMEM((seq, GROUP_WIDTH, batch), F32)],
        compiler_params=_cparams("arbitrary"),
        name="ssd_step",
    )(z, xbc, dtr, conv_all, h_all.reshape(h_all.shape[0], batch, srows, SSD_STATE),
      *prev_args, *[_parg(a) for a in consts])


PAIR = 2 * RWKV_HEAD
RWKV_PAIRS = RWKV_HEADS // 2


def _bd(x):
    half = x.shape[1] // 2
    lane = lax.broadcasted_iota(jnp.int32, x.shape, 1)
    zero = jnp.zeros_like(x)
    return jnp.concatenate([jnp.where(lane < half, x, zero), jnp.where(lane >= half, x, zero)], axis=0)


def _half_sums(x, lo):
    s_lo = jnp.sum(jnp.where(lo, x, 0.0), axis=-1, keepdims=True)
    s_hi = jnp.sum(jnp.where(lo, 0.0, x), axis=-1, keepdims=True)
    return jnp.where(lo, s_lo, s_hi)


def _head_sum(x):
    lo = lax.broadcasted_iota(jnp.int32, (x.shape[0], PAIR), 1) < RWKV_HEAD
    return jnp.concatenate([_half_sums(x[:, p * PAIR:(p + 1) * PAIR], lo) for p in range(RWKV_PAIRS)], axis=-1)


def _rwkv_pointwise(u, prev, mu_ref, w0_ref, w2_ref, a0_ref, a2_ref, g2_ref, kk_ref, ka_ref):
    GW = GROUP_WIDTH
    xs = u + (prev - u) * mu_ref[...]
    r = xs[:, 0:GW]
    k = xs[:, GW:2 * GW]
    v = xs[:, 2 * GW:3 * GW]
    wd = xs[:, 3 * GW:3 * GW + 64]
    ad = xs[:, 3 * GW + 64:3 * GW + 128]
    gd = xs[:, 3 * GW + 128:3 * GW + 256]
    w_lin = w0_ref[...] + _dot(jnp.tanh(wd).astype(BF16), w2_ref[...])
    logdecay = -math.exp(-0.5) * _sigmoid(w_lin)
    a = _sigmoid(a0_ref[...] + _dot(ad.astype(BF16), a2_ref[...]))
    g = _dot(_sigmoid(gd).astype(BF16), g2_ref[...])
    kk = k * kk_ref[...]
    kk = kk * lax.rsqrt(jnp.maximum(_head_sum(kk * kk), 1e-24))
    k = k * (1.0 + (a - 1.0) * ka_ref[...])
    return r, k, v, logdecay, a, g, kk


def _rwkv_finish(y, r, k, v, g, rk_ref, lng_ref, lnb_ref):
    mean = _head_sum(y) * (1.0 / RWKV_HEAD)
    yc = y - mean
    var = _head_sum(yc * yc) * (1.0 / RWKV_HEAD)
    y = yc * lax.rsqrt(var + RWKV_LN_EPS) * lng_ref[...] + lnb_ref[...]
    bonus = _head_sum(r * k * rk_ref[...]) * v
    return (y + bonus) * g


def _rwkv_body(u_ref, mu_ref, w0_ref, w2_ref, a0_ref, a2_ref, g2_ref, kk_ref, ka_ref, rk_ref,
               lng_ref, lnb_ref, y_ref, shift_ref, sout_ref, upad_scr, s_scr, *, chunk, group):
    L, G = chunk, group
    GL = G * L
    c = pl.program_id(1)
    pad = SUBLANES

    @pl.when(c == 0)
    def _():
        upad_scr[0:pad, :] = jnp.zeros((pad, RWKV_PROJ), F32)
        s_scr[...] = jnp.zeros(s_scr.shape, F32)

    u = u_ref[...]
    upad_scr[pad:pad + GL, :] = u
    prev = pltpu.roll(upad_scr[...], 1, axis=0)[pad:pad + GL]
    upad_scr[pad - 1:pad, :] = u[GL - 1:GL, :]
    r, k, v, logdecay, a, g, kk = _rwkv_pointwise(u, prev, mu_ref, w0_ref, w2_ref, a0_ref, a2_ref, g2_ref,
                                                  kk_ref, ka_ref)

    tril = jnp.where(lax.broadcasted_iota(jnp.int32, (L, L), 0) >= lax.broadcasted_iota(jnp.int32, (L, L), 1),
                     1.0, 0.0).astype(F32)
    cl = jnp.concatenate([_dot(tril, logdecay[i * L:(i + 1) * L, :], precision=HIGHEST) for i in range(G)], axis=0)
    e_in = jnp.exp(cl)
    e_inv = jnp.exp(-cl)
    r_t = r * e_in
    r_tb = r_t.astype(BF16)
    a_tb = (-kk * jnp.exp(cl - logdecay)).astype(BF16)
    b_tb = (kk * a * e_inv).astype(BF16)
    k_tb = (k * e_inv).astype(BF16)
    vb = v.astype(BF16)

    row = lax.broadcasted_iota(jnp.int32, (L, PAIR), 0)
    colh = lax.broadcasted_iota(jnp.int32, (L, PAIR), 1) & (RWKV_HEAD - 1)
    strict = row > colh
    incl = row >= colh
    eye_pair = jnp.where(row == colh, 1.0, 0.0).astype(F32)
    lane_lo = lax.broadcasted_iota(jnp.int32, (RWKV_HEAD, PAIR), 1) < RWKV_HEAD
    same_head = (lax.broadcasted_iota(jnp.int32, (PAIR, PAIR), 0) < RWKV_HEAD) == \
                (lax.broadcasted_iota(jnp.int32, (PAIR, PAIR), 1) < RWKV_HEAD)

    streams = [(i, p) for i in range(G) for p in range(RWKV_PAIRS)]
    ns = len(streams)
    blk = lambda x, i, p: x[i * L:(i + 1) * L, p * PAIR:(p + 1) * PAIR]
    lhs = [jnp.concatenate([blk(a_tb, i, p), blk(r_tb, i, p)], axis=0) for i, p in streams]
    m_both = [_dot_nt(lhs[s], jnp.concatenate([_bd(blk(b_tb, i, p)), _bd(blk(k_tb, i, p))], axis=0))
              for s, (i, p) in enumerate(streams)]
    m_ab = [m[:, 0:PAIR] for m in m_both]
    m_ak = [m[:, PAIR:2 * PAIR] for m in m_both]
    n_ab = [jnp.where(strict, m[0:L], 0.0) for m in m_ab]
    m_rb = [jnp.where(incl, m[L:2 * L], 0.0).astype(BF16) for m in m_ab]
    n_ak = [jnp.where(strict, m[0:L], 0.0).astype(BF16) for m in m_ak]
    m_rk = [jnp.where(incl, m[L:2 * L], 0.0).astype(BF16) for m in m_ak]
    tinv = [eye_pair + n for n in n_ab]
    pwb = [n.astype(BF16) for n in n_ab]
    pw = [_dot(x, _bd(x)) for x in pwb]
    for _ in range(int(math.log2(L)) - 2):
        pwb = [x.astype(BF16) for x in pw]
        both = [_dot(jnp.concatenate([pwb[s], tinv[s].astype(BF16)], axis=0), _bd(pwb[s])) for s in range(ns)]
        pw = [x[0:L] for x in both]
        tinv = [tinv[s] + both[s][L:2 * L] for s in range(ns)]
    pwb = [x.astype(BF16) for x in pw]
    tinv = [tinv[s] + _dot(tinv[s].astype(BF16), _bd(pwb[s])) for s in range(ns)]
    tinvb = [x.astype(BF16) for x in tinv]
    nv_mv = [_dot(jnp.concatenate([n_ak[s], m_rk[s]], axis=0), _bd(blk(vb, i, p))) for s, (i, p) in enumerate(streams)]
    wu = [_dot(tinvb[s], jnp.concatenate([_bd(blk(a_tb, i, p)), _bd(nv_mv[s][0:L].astype(BF16))], axis=1))
          for s, (i, p) in enumerate(streams)]
    wub = [x.astype(BF16) for x in wu]
    qy = [_dot(m_rb[s], jnp.concatenate([_bd(wub[s][:, 0:PAIR]), _bd(wub[s][:, PAIR:2 * PAIR])], axis=1))
          for s in range(ns)]
    q = [(blk(r_t, i, p) + qy[s][:, 0:PAIR]).astype(BF16) for s, (i, p) in enumerate(streams)]
    y_loc = [qy[s][:, PAIR:2 * PAIR] + nv_mv[s][L:2 * L] for s in range(ns)]
    zeros_b = jnp.zeros((L, PAIR), BF16)
    mg = [_dot_tn(jnp.concatenate([wub[s], jnp.concatenate([zeros_b, blk(vb, i, p)], axis=1)], axis=0),
                  jnp.concatenate([blk(b_tb, i, p), blk(k_tb, i, p)], axis=0))
          for s, (i, p) in enumerate(streams)]
    p_end = [e_in[(i + 1) * L - 1:(i + 1) * L, p * PAIR:(p + 1) * PAIR] for i, p in streams]
    m_t = [(jnp.where(same_head, mg[s][0:PAIR], 0.0) * p_end[s]).astype(BF16) for s in range(ns)]
    g_t = [jnp.where(lane_lo, mg[s][PAIR:PAIR + RWKV_HEAD], mg[s][PAIR + RWKV_HEAD:2 * PAIR]) * p_end[s]
           for s in range(ns)]

    y_rows = []
    for i in range(G):
        y_pairs = []
        for p in range(RWKV_PAIRS):
            s = i * RWKV_PAIRS + p
            s0 = s_scr[p]
            s0b = s0.astype(BF16)
            y_pairs.append(_dot_nt(q[s], _bd(s0b)) + y_loc[s])
            s_scr[p] = s0 * p_end[s] + _dot(s0b, m_t[s]) + g_t[s]
        y_rows.append(jnp.concatenate(y_pairs, axis=-1))
    y = jnp.concatenate(y_rows, axis=0)
    y_ref[...] = _rwkv_finish(y, r, k, v, g, rk_ref, lng_ref, lnb_ref)

    @pl.when(c == pl.num_programs(1) - 1)
    def _():
        sout_ref[0] = s_scr[...]
        shift_ref[0] = upad_scr[pad - 1:pad, :]


_RWKV_PARAM_NAMES = ("mu", "w0", "w2", "a0", "a2", "g2", "k_k", "k_a", "r_k", "ln_g", "ln_b")


def _rwkv(u, p, *, batch, seq):
    rows = RWKV_CHUNK * RWKV_GROUP
    nc = seq // rows
    params = [p[n] for n in _RWKV_PARAM_NAMES]
    sspec = pl.BlockSpec((1, RWKV_PAIRS, RWKV_HEAD, PAIR), lambda b, c: (b, 0, 0, 0))
    y, shift, s_last = pl.pallas_call(
        functools.partial(_rwkv_body, chunk=RWKV_CHUNK, group=RWKV_GROUP),
        grid=(batch, nc),
        in_specs=[pl.BlockSpec((rows, RWKV_PROJ), lambda b, c: (b * nc + c, 0))] + [_pspec(a) for a in params],
        out_specs=[pl.BlockSpec((rows, GROUP_WIDTH), lambda b, c: (b * nc + c, 0)),
                   pl.BlockSpec((1, 1, RWKV_PROJ), lambda b, c: (b, 0, 0)), sspec],
        out_shape=[jax.ShapeDtypeStruct((batch * seq, GROUP_WIDTH), F32),
                   jax.ShapeDtypeStruct((batch, 1, RWKV_PROJ), F32),
                   jax.ShapeDtypeStruct((batch, RWKV_PAIRS, RWKV_HEAD, PAIR), F32)],
        scratch_shapes=[pltpu.VMEM((SUBLANES + rows, RWKV_PROJ), F32),
                        pltpu.VMEM((RWKV_PAIRS, RWKV_HEAD, PAIR), F32)],
        compiler_params=_cparams("parallel", "arbitrary"),
        name="rwkv",
    )(u, *[_parg(a) for a in params])
    s_last = s_last.reshape(batch, RWKV_PAIRS, RWKV_HEAD, 2, RWKV_HEAD).transpose(0, 1, 3, 2, 4).reshape(
        batch, RWKV_HEADS, RWKV_HEAD, RWKV_HEAD)
    return y, shift.reshape(batch, RWKV_PROJ), s_last


def _rwkv_step_body(u_ref, shift0_ref, s0_ref, *rest, seq, batch, layer):
    sdone_ref, rest = (rest[0], rest[1:]) if layer else (None, rest)
    (mu_ref, w0_ref, w2_ref, a0_ref, a2_ref, g2_ref, kk_ref, ka_ref, rk_ref, lng_ref, lnb_ref, y_ref, sout_ref,
     r_scr, w_scr, k_scr, b_scr, nkk_scr, v_scr, y_scr) = rest
    T, B = seq, batch
    j = pl.program_id(0)
    if layer:
        sout_ref[0:layer] = sdone_ref[...]
    tiles = RWKV_STEP_TILES

    def pointwise(t):
        u = u_ref[t * B:(t + 1) * B, :]
        prev = shift0_ref[...] if t == 0 else u_ref[(t - 1) * B:t * B, :]
        return _rwkv_pointwise(u, prev, mu_ref, w0_ref, w2_ref, a0_ref, a2_ref, g2_ref, kk_ref, ka_ref)

    @pl.when(j == 0)
    def _():
        for t in range(T):
            r, k, v, logdecay, a, _, kk = pointwise(t)
            r_scr[t] = r.T
            w_scr[t] = jnp.exp(logdecay).T
            k_scr[t] = k.T
            b_scr[t] = (kk * a).T
            nkk_scr[t] = (-kk).T
            v_scr[t] = v.T

    i0 = j * tiles
    keys = pl.ds(pl.multiple_of((i0 // RWKV_HEAD) * RWKV_HEAD, RWKV_HEAD), RWKV_HEAD)
    for q in range(tiles):
        vi = pl.ds(i0 + q, 1)
        s = s0_ref[q]
        for t in range(T):
            sa = jnp.sum(s * nkk_scr[t, keys, :], axis=0, keepdims=True)
            s = s * w_scr[t, keys, :] + k_scr[t, keys, :] * v_scr[t, vi, :] + b_scr[t, keys, :] * sa
            y_scr[t, vi, :] = jnp.sum(s * r_scr[t, keys, :], axis=0, keepdims=True)
        sout_ref[layer, q] = s

    @pl.when(j == pl.num_programs(0) - 1)
    def _():
        for t in range(T):
            r, k, v, _, _, g, _ = pointwise(t)
            y_ref[t * B:(t + 1) * B, :] = _rwkv_finish(y_scr[t].T, r, k, v, g, rk_ref, lng_ref, lnb_ref)


def _rwkv_step(u, shift0, s_all, s_done, p, *, batch, seq, layer):
    n = batch * seq
    srows = RWKV_HEADS * RWKV_HEAD
    params = [p[nm] for nm in _RWKV_PARAM_NAMES]
    sspec, prev_specs, sout_spec = _layer_state_specs(layer, (RWKV_STEP_TILES, RWKV_HEAD, batch), 0)
    prev_args = [s_done] if layer else []
    tposed = pltpu.VMEM((seq, GROUP_WIDTH, batch), F32)
    return pl.pallas_call(
        functools.partial(_rwkv_step_body, seq=seq, batch=batch, layer=layer),
        grid=(srows // RWKV_STEP_TILES,),
        in_specs=[_full_spec((n, RWKV_PROJ)), _full_spec((batch, RWKV_PROJ)), sspec] + prev_specs
                 + [_pspec(a) for a in params],
        out_specs=[_full_spec((n, GROUP_WIDTH)), sout_spec],
        out_shape=[jax.ShapeDtypeStruct((n, GROUP_WIDTH), F32),
                   jax.ShapeDtypeStruct((layer + 1, srows, RWKV_HEAD, batch), F32)],
        scratch_shapes=[tposed] * 7,
        compiler_params=_cparams("arbitrary"),
        name="rwkv_step",
    )(u, shift0, s_all, *prev_args, *[_parg(a) for a in params])


def _s5_body(u_ref, hre0_ref, him0_ref, are_ref, aim_ref, bmat_ref, cmat_ref, d_ref, gw_ref, gb_ref,
             y_ref, hre_ref, him_ref, hs_scr, tm_scr, *, steps, batch_major):
    c = pl.program_id(1)
    ns = S5_WIDTH
    bsub = SUBLANES

    @pl.when(c == 0)
    def _():
        hre_ref[...] = hre0_ref[...]
        him_ref[...] = him0_ref[...]

    if batch_major:
        for b in range(bsub):
            tm_scr[:, b, :] = u_ref[b]
        u = tm_scr[...].reshape(steps * bsub, GROUP_WIDTH)
    else:
        u = u_ref[...].reshape(steps * bsub, GROUP_WIDTH)
    are = jnp.broadcast_to(are_ref[...], (bsub, ns))
    aim = jnp.broadcast_to(aim_ref[...], (bsub, ns))
    hre, him = hre_ref[...], him_ref[...]
    sub = min(S5_SUB, steps)
    rows = sub * bsub
    outs = []
    hs_scr[...] = _dot(u.astype(BF16), bmat_ref[...])
    for k in range(steps // sub):
        r0 = k * rows
        u_k = u[r0:r0 + rows]
        for t in range(sub):
            rs = slice(r0 + t * bsub, r0 + (t + 1) * bsub)
            hre, him = (are * hre - aim * him + hs_scr[rs, 0:ns], are * him + aim * hre + hs_scr[rs, ns:2 * ns])
            hs_scr[rs, 0:ns] = hre
            hs_scr[rs, ns:2 * ns] = him
        y = _dot(hs_scr[r0:r0 + rows, :].astype(BF16), cmat_ref[...]) + u_k * d_ref[...]
        y = _gelu_tanh(y)
        yy = _dot(y.astype(BF16), gw_ref[...]) + gb_ref[...]
        outs.append(yy[:, 0:GROUP_WIDTH] * _sigmoid(yy[:, GROUP_WIDTH:2 * GROUP_WIDTH]))
    hre_ref[...] = hre
    him_ref[...] = him
    out = jnp.concatenate(outs, axis=0).reshape(steps, bsub, GROUP_WIDTH)
    if batch_major:
        tm_scr[...] = out
        for b in range(bsub):
            y_ref[b] = tm_scr[:, b, :]
    else:
        y_ref[...] = out


def _time_specs(u, batch_major, chunk):
    bsub = SUBLANES
    if batch_major:
        batch, seq, _ = u.shape
        steps = min(chunk, seq)
        spec = pl.BlockSpec((bsub, steps, GROUP_WIDTH), lambda b, c: (b, c, 0))
    else:
        seq, batch, _ = u.shape
        steps = min(chunk, seq)
        spec = pl.BlockSpec((steps, bsub, GROUP_WIDTH), lambda b, c: (c, b, 0))
    return batch, seq, steps, spec


def _s5(u, hre0, him0, lp, *, batch_major):
    batch, seq, steps, tspec = _time_specs(u, batch_major, TM_CHUNK)
    bsub = SUBLANES
    hspec = pl.BlockSpec((bsub, S5_WIDTH), lambda b, c: (b, 0))
    consts = (lp["s5_are"], lp["s5_aim"], lp["s5_bmat"], lp["s5_cmat"], lp["s5_d"], lp["s5_gw"], lp["s5_gb"])
    return pl.pallas_call(
        functools.partial(_s5_body, steps=steps, batch_major=batch_major),
        grid=(batch // bsub, seq // steps),
        in_specs=[tspec, hspec, hspec] + [_pspec(a) for a in consts],
        out_specs=[tspec, hspec, hspec],
        out_shape=[jax.ShapeDtypeStruct(u.shape, F32),
                   jax.ShapeDtypeStruct((batch, S5_WIDTH), F32),
                   jax.ShapeDtypeStruct((batch, S5_WIDTH), F32)],
        scratch_shapes=[pltpu.VMEM((steps * bsub, 2 * S5_WIDTH), F32),
                        pltpu.VMEM((steps, bsub, GROUP_WIDTH), F32)],
        compiler_params=_cparams("parallel", "arbitrary"),
        name="s5",
    )(u, hre0, him0, *[_parg(a) for a in consts])


def _pool_body(u_ref, buf0_ref, pw_ref, sc_ref, y_ref, buf_ref, f_scr, tm_scr, *, steps, pos0, batch_major):
    c = pl.program_id(1)
    bsub = SUBLANES
    GW = GROUP_WIDTH
    halo = POOL_BUF + 1

    @pl.when(c == 0)
    def _():
        f_scr[0] = jnp.zeros((bsub, GW), F32)
        f_scr[1:halo] = buf0_ref[...]

    if batch_major:
        for b in range(bsub):
            f_scr[halo:halo + steps, b, :] = u_ref[b]
    else:
        f_scr[halo:halo + steps] = u_ref[...]
    f = f_scr[...]
    u = f[halo:halo + steps]
    s2 = f[1:] + f[:-1]
    s4 = s2[2:] + s2[:-2]
    s8 = s4[4:] + s4[:-4]
    s16 = s8[8:] + s8[:-8]
    f_scr[0:halo] = f[steps:steps + halo]
    lane = lax.broadcasted_iota(jnp.int32, (steps, bsub, GW), 2)
    tpos = lax.broadcasted_iota(jnp.int32, (steps, bsub, GW), 0) + (pos0 + 1) + c * steps
    win = jnp.where(lane < POOL_CH, s2[halo - 1:halo - 1 + steps],
                    jnp.where(lane < 2 * POOL_CH, s4[halo - 3:halo - 3 + steps],
                              jnp.where(lane < 3 * POOL_CH, s8[halo - 7:halo - 7 + steps],
                                        s16[halo - 15:halo - 15 + steps])))
    wlen = jnp.where(lane < POOL_CH, POOL_WINDOWS[0],
                     jnp.where(lane < 2 * POOL_CH, POOL_WINDOWS[1],
                               jnp.where(lane < 3 * POOL_CH, POOL_WINDOWS[2], POOL_WINDOWS[3])))
    cnt = jnp.minimum(tpos, wlen).astype(F32)
    pooled = (win / cnt - u).reshape(steps * bsub, GW)
    y = (_dot(pooled.astype(BF16), pw_ref[...]) * sc_ref[...]).reshape(steps, bsub, GW)
    if batch_major:
        tm_scr[...] = y
        for b in range(bsub):
            y_ref[b] = tm_scr[:, b, :]
    else:
        y_ref[...] = y

    @pl.when(c == pl.num_programs(1) - 1)
    def _():
        buf_ref[...] = f_scr[1:halo]


def _pool(u, buf0, lp, *, pos0, batch_major, layer=None):
    batch, seq, steps, tspec = _time_specs(u, batch_major, POOL_CHUNK)
    bsub = SUBLANES
    bblock =(POOL_BUF, bsub, GROUP_WIDTH)
    bspec = pl.BlockSpec(bblock, lambda b, c: (0, b, 0))
    if layer is None:
        bspec_in = bspec
    else:
        bspec_in = pl.BlockSpec((None,) + bblock, lambda b, c: (layer, 0, b, 0))
    return pl.pallas_call(
        functools.partial(_pool_body, steps=steps, pos0=pos0, batch_major=batch_major),
        grid=(batch // bsub, seq // steps),
        in_specs=[tspec, bspec_in, _pspec(lp["pool_w"]), _pspec(lp["pool_scale"])],
        out_specs=[tspec, bspec],
        out_shape=[jax.ShapeDtypeStruct(u.shape, F32), jax.ShapeDtypeStruct((POOL_BUF, batch, GROUP_WIDTH), F32)],
        scratch_shapes=[pltpu.VMEM((POOL_BUF + 1 + steps, bsub, GROUP_WIDTH), F32),
                        pltpu.VMEM((steps, bsub, GROUP_WIDTH), F32)],
        compiler_params=_cparams("parallel", "arbitrary"),
        name="pool",
    )(u, buf0, _parg(lp["pool_w"]), _parg(lp["pool_scale"]))


def _block_diag(blocks):
    n, g, r, c = blocks.shape
    eye = jnp.eye(g, dtype=blocks.dtype)
    return (eye[None, :, None, :, None] * blocks[:, :, :, None, :]).reshape(n, g * r, g * c)


def _stacked_params(P):
    row = lambda a: a.reshape(a.shape[0], 1, -1)
    pad_lanes = lambda a: jnp.pad(a, ((0, 0), (0, LANES - a.shape[1])))
    bf = lambda a: a.astype(BF16)

    lam = lax.complex(P["s5_lam_re"], P["s5_lam_im"])
    a_bar = jnp.exp(lam * jnp.exp(P["s5_log_step"])[..., None])
    b_bar = ((a_bar - 1.0) / lam)[..., None] * lax.complex(P["s5_b_re"], P["s5_b_im"])
    b_t = jnp.swapaxes(b_bar, 2, 3)
    bmat = jnp.concatenate([_block_diag(jnp.real(b_t)), _block_diag(jnp.imag(b_t))], axis=2)
    c_t = jnp.swapaxes(lax.complex(P["s5_c_re"], P["s5_c_im"]), 2, 3)
    cmat = jnp.concatenate([_block_diag(jnp.real(c_t)), -_block_diag(jnp.imag(c_t))], axis=1)

    out = dict(
        norm_ffn1=row(P["norm_ffn1"]), ffn1_in=P["ffn1_in"], ffn1_out=P["ffn1_out"],
        norm_mix=row(P["norm_mix"]),
        w_in=jnp.transpose(P["w_in"], (2, 0, 1)),
        conv_w=P["ssd_conv_w"], conv_b=row(P["ssd_conv_b"]),
        dt_bias=row(pad_lanes(P["ssd_dt_bias"])), a_log=row(pad_lanes(P["ssd_a_log"])),
        a_neg_exp=row(jnp.repeat(-jnp.exp(P["ssd_a_log"]), SSD_HEAD_DIM, axis=1)),
        d_skip=row(jnp.repeat(P["ssd_d"], SSD_HEAD_DIM, axis=1)), ssd_norm=row(P["ssd_norm"]),
        s5_are=row(jnp.real(a_bar)), s5_aim=row(jnp.imag(a_bar)), s5_bmat=bf(bmat), s5_cmat=bf(cmat),
        s5_d=row(P["s5_d"]), s5_gw=bf(P["s5_glu_w"]), s5_gb=row(P["s5_glu_b"]),
        pool_w=bf(_block_diag(P["pool_w"])), pool_scale=row(P["pool_scale"]),
        w_out=bf(P["w_out"]),
        norm_ffn2=row(P["norm_ffn2"]), ffn2_in=P["ffn2_in"], ffn2_out=P["ffn2_out"],
    )
    for name in _RWKV_PARAM_NAMES:
        a = P["rwkv_" + name]
        out["rwkv_" + name] = bf(a) if name in ("w2", "a2", "g2") else row(a)
    return out


def _layer_params(stacked, l):
    lp = {k: _Layered((v, l)) for k, v in stacked.items()}
    lp["rwkv"] = {n: lp["rwkv_" + n] for n in _RWKV_PARAM_NAMES}
    lp["head_expand"] = jnp.pad(jnp.repeat(jnp.eye(SSD_HEADS, dtype=F32), SSD_HEAD_DIM, axis=1),
                                ((0, LANES - SSD_HEADS), (0, 0)))
    return lp


def _mixers_prompt(lp, proj, *, batch, seq):
    z, xbc, ur, us5, upool, dtr = proj
    y_ssd, conv_new, ssd_new = _ssd(z, xbc, dtr, lp, batch=batch, seq=seq)
    y_rwkv, shift_new, rwkv_new = _rwkv(ur, lp["rwkv"], batch=batch, seq=seq)
    zeros = jnp.zeros((batch, S5_WIDTH), F32)
    bm = lambda a: a.reshape(batch, seq, a.shape[-1])
    rows = lambda a: a.reshape(batch * seq, a.shape[-1])
    y_s5, s5re, s5im = _s5(bm(us5), zeros, zeros, lp, batch_major=True)
    y_pool, pool_new = _pool(bm(upool), jnp.zeros((POOL_BUF, batch, GROUP_WIDTH), F32), lp, pos0=0,
                             batch_major=True)
    ys = (y_ssd, y_rwkv, rows(y_s5), rows(y_pool))
    states = (conv_new, ssd_new, shift_new, rwkv_new, s5re.reshape(batch, S5_GROUPS, S5_STATE),
              s5im.reshape(batch, S5_GROUPS, S5_STATE), jnp.swapaxes(pool_new, 0, 1))
    return ys, states


def _mixers_decode(lp, proj, states, done, *, batch, seq, layer):
    z, xbc, ur, us5, upool, dtr = proj
    shift0, s5re0, s5im0 = (states[i][layer] for i in (2, 4, 5))
    ssd_done, rwkv_done = (done[1], done[3]) if layer else (None, None)
    y_ssd, conv_new, ssd_new = _ssd_step(z, xbc, dtr, states[0], states[1], ssd_done, lp, batch=batch, seq=seq,
                                         layer=layer)
    y_rwkv, rwkv_new = _rwkv_step(ur, shift0, states[3], rwkv_done, lp["rwkv"], batch=batch, seq=seq, layer=layer)
    shift_new = ur[(seq - 1) * batch:, :]
    tm = lambda a: a.reshape(seq, batch, a.shape[-1])
    y_s5, s5re, s5im = _s5(tm(us5), s5re0.reshape(batch, S5_WIDTH), s5im0.reshape(batch, S5_WIDTH), lp,
                           batch_major=False)
    y_pool, pool_new = _pool(tm(upool), states[6], lp, pos0=PAST_LEN, batch_major=False, layer=layer)
    rows = lambda a: a.reshape(seq * batch, a.shape[-1])
    ys = (y_ssd, y_rwkv, rows(y_s5), rows(y_pool))
    new_states = (jnp.swapaxes(conv_new, 0, 1), ssd_new, shift_new, rwkv_new,
                  s5re.reshape(batch, S5_GROUPS, S5_STATE), s5im.reshape(batch, S5_GROUPS, S5_STATE),
                  jnp.swapaxes(pool_new, 0, 1))
    return ys, new_states


_WIDTHS = (GROUP_WIDTH, SSD_CONV_DIM, RWKV_PROJ, GROUP_WIDTH, GROUP_WIDTH, LANES)


def _trunk(x_p, x_s, layer_params, norm_final, mixers_p, mixers_s):
    st_p, st_s = [], []
    mix_p, mix_s, lp = None, None, None
    for l, lp_next in enumerate(layer_params):
        if l > 0:
            x_s, wg, wu, wo = _ffn_cast(x_s, lp["norm_ffn2"], lp["ffn2_in"], lp["ffn2_out"], mix=mix_s, wmix=lp["w_out"])
            x_p = _ffn(x_p, lp["norm_ffn2"], wg, wu, wo, mix=mix_p, wmix=lp["w_out"])
        lp = lp_next
        x_s, wg, wu, wo = _ffn_cast(x_s, lp["norm_ffn1"], lp["ffn1_in"], lp["ffn1_out"])
        x_p = _ffn(x_p, lp["norm_ffn1"], wg, wu, wo)
        proj_s, w_all = _inproj_cast(x_s, lp["norm_mix"], lp["w_in"], _WIDTHS)
        mix_p, st = mixers_p(l, lp, _inproj(x_p, lp["norm_mix"], w_all, _WIDTHS), st_p[-1] if st_p else None)
        st_p.append(st)
        mix_s, st = mixers_s(l, lp, proj_s, st_s[-1] if st_s else None)
        st_s.append(st)
    x_s, wg, wu, wo = _ffn_cast(x_s, lp["norm_ffn2"], lp["ffn2_in"], lp["ffn2_out"], mix=mix_s, wmix=lp["w_out"],
                                gf=norm_final)
    x_p = _ffn(x_p, lp["norm_ffn2"], wg, wu, wo, mix=mix_p, wmix=lp["w_out"], gf=norm_final)
    return (x_p, x_s), (st_p, st_s)


def kernel(x_prompt, x_sample, state_ssd_conv, state_ssd, state_rwkv_shift, state_rwkv, state_s5_re, state_s5_im, state_pool, norm_ffn1, ffn1_in, ffn1_out, norm_mix, w_in, ssd_conv_w, ssd_conv_b, ssd_dt_bias, ssd_a_log, ssd_d, ssd_norm, rwkv_mu, rwkv_w0, rwkv_w2, rwkv_a0, rwkv_a2, rwkv_g2, rwkv_k_k, rwkv_k_a, rwkv_r_k, rwkv_ln_g, rwkv_ln_b, s5_lam_re, s5_lam_im, s5_log_step, s5_b_re, s5_b_im, s5_c_re, s5_c_im, s5_d, s5_glu_w, s5_glu_b, pool_w, pool_scale, w_out, norm_ffn2, ffn2_in, ffn2_out, norm_final):
    P = dict(norm_ffn1=norm_ffn1, ffn1_in=ffn1_in, ffn1_out=ffn1_out, norm_mix=norm_mix, w_in=w_in,
             ssd_conv_w=ssd_conv_w, ssd_conv_b=ssd_conv_b, ssd_dt_bias=ssd_dt_bias, ssd_a_log=ssd_a_log,
             ssd_d=ssd_d, ssd_norm=ssd_norm, rwkv_mu=rwkv_mu, rwkv_w0=rwkv_w0, rwkv_w2=rwkv_w2, rwkv_a0=rwkv_a0,
             rwkv_a2=rwkv_a2, rwkv_g2=rwkv_g2, rwkv_k_k=rwkv_k_k, rwkv_k_a=rwkv_k_a,
             rwkv_r_k=rwkv_r_k.reshape(rwkv_r_k.shape[0], -1), rwkv_ln_g=rwkv_ln_g, rwkv_ln_b=rwkv_ln_b,
             s5_lam_re=s5_lam_re, s5_lam_im=s5_lam_im, s5_log_step=s5_log_step, s5_b_re=s5_b_re, s5_b_im=s5_b_im,
             s5_c_re=s5_c_re, s5_c_im=s5_c_im, s5_d=s5_d, s5_glu_w=s5_glu_w, s5_glu_b=s5_glu_b, pool_w=pool_w,
             pool_scale=pool_scale, w_out=w_out, norm_ffn2=norm_ffn2, ffn2_in=ffn2_in, ffn2_out=ffn2_out)
    depth = norm_ffn1.shape[0]
    bp, tp, d = x_prompt.shape
    bs, ts, _ = x_sample.shape
    stacked = _stacked_params(P)
    layer_params = [_layer_params(stacked, l) for l in range(depth)]
    gf = norm_final.reshape(1, -1)
    sample_states = (state_ssd_conv, state_ssd, state_rwkv_shift, state_rwkv, state_s5_re, state_s5_im, state_pool)
    rwkv_rows = RWKV_HEADS * RWKV_HEAD
    decode_states = (jnp.swapaxes(state_ssd_conv, 1, 2), state_ssd, state_rwkv_shift,
                     jnp.transpose(state_rwkv, (0, 2, 3, 4, 1)).reshape(depth, rwkv_rows, RWKV_HEAD, bs),
                     state_s5_re, state_s5_im, jnp.swapaxes(state_pool, 1, 2))

    x_s = jnp.swapaxes(x_sample, 0, 1).reshape(ts * bs, d)
    (y_p, y_s), (st_p, st_s) = _trunk(
        x_prompt.reshape(bp * tp, d), x_s, layer_params, gf,
        lambda l, lp, proj, done: _mixers_prompt(lp, proj, batch=bp, seq=tp),
        lambda l, lp, proj, done: _mixers_decode(lp, proj, decode_states, done, batch=bs, seq=ts, layer=l))
    outs = [y_p.reshape(bp, tp, d), jnp.swapaxes(y_s.reshape(ts, bs, d), 0, 1)]
    for i, ref_state in enumerate(sample_states):
        outs.append(jnp.stack([st[i] for st in st_p]))
        if i == 1:
            outs.append(st_s[-1][i].reshape(ref_state.shape))
        elif i == 3:
            s_new = st_s[-1][i].reshape(depth, RWKV_HEADS, RWKV_HEAD, RWKV_HEAD, bs)
            outs.append(jnp.transpose(s_new, (0, 4, 1, 2, 3)))
        else:
            outs.append(jnp.stack([st[i] for st in st_s]))
    return tuple(outs)
```

```python
import functools
import math

import jax
import jax.numpy as jnp
from jax import lax
from jax.experimental import pallas as pl
from jax.experimental.pallas import tpu as pltpu

F32 = jnp.float32
BF16 = jnp.bfloat16
HIGHEST = lax.Precision.HIGHEST

SUBLANES = 8
LANES = 128
VMEM_LIMIT_BYTES = 56 * 1024 * 1024

GROUP_WIDTH = 256
SSD_HEAD_DIM = 64
SSD_HEADS = 4
SSD_GROUPS = 2
SSD_STATE = 128
SSD_CONV = 4
SSD_CONV_DIM = GROUP_WIDTH + 2 * SSD_GROUPS * SSD_STATE
SSD_CHUNK = 128
SSD_GROUP = 4
LOG2_E = math.log2(math.e)
RWKV_HEAD = 64
RWKV_HEADS = 4
RWKV_PROJ = 1024
RWKV_LN_EPS = 64e-5
RWKV_CHUNK = 64
RWKV_GROUP = 8
S5_GROUPS = 16
S5_STATE = 64
S5_WIDTH = S5_GROUPS * S5_STATE
POOL_WINDOWS = (2, 4, 8, 16)
POOL_CH = 64
POOL_BUF = 15
RMS_EPS = 1e-6
PAST_LEN = 16384

ROW_TILE = 512
FFN_CHUNK = 256
TM_CHUNK = 128
POOL_CHUNK = 256
S5_SUB = 64
SSD_STEP_TILES = 16
RWKV_STEP_TILES = 16


def _cparams(*sem):
    return pltpu.CompilerParams(dimension_semantics=sem, vmem_limit_bytes=VMEM_LIMIT_BYTES)


def _dot(a, b, **kw):
    return jnp.dot(a, b, preferred_element_type=F32, **kw)


def _dot_nt(a, b):
    return lax.dot_general(a, b, (((1,), (1,)), ((), ())), preferred_element_type=F32)


def _dot_tn(a, b):
    return lax.dot_general(a, b, (((0,), (0,)), ((), ())), preferred_element_type=F32)


def _sigmoid(x):
    return 0.5 * jnp.tanh(0.5 * x) + 0.5


def _silu(x):
    h = 0.5 * x
    return h + h * jnp.tanh(h)


def _softplus(x):
    return jnp.maximum(x, 0.0) + jnp.log(1.0 + jnp.exp(-jnp.abs(x)))


def _gelu_tanh(x):
    c = math.sqrt(2.0 / math.pi)
    return x * (0.5 * (1.0 + jnp.tanh(c * (x + 0.044715 * (x * x * x)))))


def _rms(x, g):
    return x * lax.rsqrt(jnp.mean(x * x, axis=-1, keepdims=True) + RMS_EPS) * g


def _full_spec(shape):
    n = len(shape)
    return pl.BlockSpec(shape, lambda *_: (0,) * n)


class _Layered(tuple):
    pass


def _pspec(p, single=False):
    mode = pl.Buffered(1) if single else None
    if isinstance(p, _Layered):
        a, l = p
        return pl.BlockSpec((None,) + a.shape[1:], lambda *_: (l,) + (0,) * (a.ndim - 1), pipeline_mode=mode)
    n = p.ndim
    return pl.BlockSpec(p.shape, lambda *_: (0,) * n, pipeline_mode=mode)


def _parg(p):
    return p[0] if isinstance(p, _Layered) else p


def _mix_residual(x, y_refs, wmix_ref):
    for j, y_ref in enumerate(y_refs):
        x = x + _dot(y_ref[...].astype(BF16), wmix_ref[j * GROUP_WIDTH:(j + 1) * GROUP_WIDTH, :])
    return x


def _swiglu_chunk(h, wg, wu, wo):
    act = (_silu(_dot(h, wg)) * _dot(h, wu)).astype(BF16)
    return _dot(act, wo)


def _ffn_body(*refs, has_mix, final_norm):
    it = iter(refs)
    x = next(it)[...]
    if has_mix:
        y_refs = [next(it) for _ in range(4)]
        x = _mix_residual(x, y_refs, next(it))
    g_ref, wg_ref, wu_ref, wo_ref = next(it), next(it), next(it), next(it)
    gf_ref = next(it) if final_norm else None
    o_ref = next(it)
    h = _rms(x, g_ref[...]).astype(BF16)
    acc = jnp.zeros_like(x)
    for c in range(wo_ref.shape[0] // FFN_CHUNK):
        cols = slice(c * FFN_CHUNK, (c + 1) * FFN_CHUNK)
        acc = acc + _swiglu_chunk(h, wg_ref[:, cols], wu_ref[:, cols], wo_ref[cols, :])
    x = x + 0.5 * acc
    if final_norm:
        x = _rms(x, gf_ref[...])
    o_ref[...] = x


def _ffn(x, g, wg, wu, wo, mix=None, wmix=None, gf=None):
    rows, d = x.shape
    row_spec = lambda w: pl.BlockSpec((ROW_TILE, w), lambda i: (i, 0))
    args, specs = [x], [row_spec(d)]
    if mix is not None:
        for y in mix:
            args.append(y)
            specs.append(row_spec(y.shape[1]))
        args.append(_parg(wmix))
        specs.append(_pspec(wmix, single=True))
    for a in (g, wg, wu, wo) + ((gf,) if gf is not None else ()):
        args.append(_parg(a))
        specs.append(_pspec(a, single=True))
    return pl.pallas_call(
        functools.partial(_ffn_body, has_mix=mix is not None, final_norm=gf is not None),
        grid=(rows // ROW_TILE,),
        in_specs=specs,
        out_specs=row_spec(d),
        out_shape=jax.ShapeDtypeStruct((rows, d), F32),
        compiler_params=_cparams("parallel"),
        name="ffn",
    )(*args)


def _ffn_cast_body(*refs, has_mix, final_norm):
    it = iter(refs)
    x_ref = next(it)
    if has_mix:
        y_refs = [next(it) for _ in range(4)]
        wmix_ref = next(it)
    g_ref, wg_ref, wu_ref, wo_ref = next(it), next(it), next(it), next(it)
    gf_ref = next(it) if final_norm else None
    o_ref, wg_out, wu_out, wo_out, x_scr, h_scr, acc_scr = (next(it) for _ in range(7))
    c = pl.program_id(0)

    @pl.when(c == 0)
    def _():
        x = x_ref[...]
        if has_mix:
            x = _mix_residual(x, y_refs, wmix_ref)
        x_scr[...] = x
        h_scr[...] = _rms(x, g_ref[...]).astype(BF16)
        acc_scr[...] = jnp.zeros(acc_scr.shape, F32)

    wg = wg_ref[...].astype(BF16)
    wu = wu_ref[...].astype(BF16)
    wo = wo_ref[...].astype(BF16)
    wg_out[...] = wg
    wu_out[...] = wu
    wo_out[...] = wo
    acc_scr[...] += _swiglu_chunk(h_scr[...], wg, wu, wo)

    @pl.when(c == pl.num_programs(0) - 1)
    def _():
        x = x_scr[...] + 0.5 * acc_scr[...]
        if final_norm:
            x = _rms(x, gf_ref[...])
        o_ref[...] = x


def _ffn_cast(x, g, wi, wo, mix=None, wmix=None, gf=None):
    rows, d = x.shape
    wi_all, l = wi
    wo_all, _ = wo
    d_ff = wo_all.shape[1]
    nchunks = d_ff // FFN_CHUNK
    args, specs = [x], [_full_spec(x.shape)]
    if mix is not None:
        for y in mix:
            args.append(y)
            specs.append(_full_spec(y.shape))
        args.append(_parg(wmix))
        specs.append(_pspec(wmix, single=True))
    args += [_parg(g), wi_all, wi_all, wo_all]
    specs += [_pspec(g),
              pl.BlockSpec((None, d, FFN_CHUNK), lambda c: (l, 0, c)),
              pl.BlockSpec((None, d, FFN_CHUNK), lambda c: (l, 0, c + nchunks)),
              pl.BlockSpec((None, FFN_CHUNK, d), lambda c: (l, c, 0))]
    if gf is not None:
        args.append(gf)
        specs.append(_full_spec(gf.shape))
    col_spec = pl.BlockSpec((d, FFN_CHUNK), lambda c: (0, c))
    return pl.pallas_call(
        functools.partial(_ffn_cast_body, has_mix=mix is not None, final_norm=gf is not None),
        grid=(nchunks,),
        in_specs=specs,
        out_specs=[_full_spec(x.shape), col_spec, col_spec, pl.BlockSpec((FFN_CHUNK, d), lambda c: (c, 0))],
        out_shape=[jax.ShapeDtypeStruct((rows, d), F32), jax.ShapeDtypeStruct((d, d_ff), BF16),
                   jax.ShapeDtypeStruct((d, d_ff), BF16), jax.ShapeDtypeStruct((d_ff, d), BF16)],
        scratch_shapes=[pltpu.VMEM((rows, d), F32), pltpu.VMEM((rows, d), BF16), pltpu.VMEM((rows, d), F32)],
        compiler_params=_cparams("arbitrary"),
        name="ffn_cast",
    )(*args)


def _inproj_body(x_ref, g_ref, wt_ref, *o_refs):
    h = _rms(x_ref[...], g_ref[...]).astype(BF16)
    off = 0
    for o_ref in o_refs:
        n = o_ref.shape[-1]
        o_ref[...] = _dot_nt(h, wt_ref[off:off + n, :])
        off += n


def _inproj_cast_body(x_ref, g_ref, win_ref, *o_refs, layer):
    *proj_refs, wall_ref = o_refs
    split = GROUP_WIDTH + SSD_CONV_DIM
    wt = win_ref[:, layer, :]
    tail = wt.shape[0] - split - SSD_HEADS
    wall_ref[0:split, :] = wt[0:split].astype(BF16)
    wall_ref[split:split + tail, :] = wt[split + SSD_HEADS:].astype(BF16)
    dt_rows = jnp.concatenate([wt[split:split + SSD_HEADS], jnp.zeros((LANES - SSD_HEADS, wt.shape[1]), F32)], axis=0)
    wall_ref[split + tail:, :] = dt_rows.astype(BF16)
    _inproj_body(x_ref, g_ref, wall_ref, *proj_refs)


def _inproj_cast(x, g, w_in, widths):
    rows, d = x.shape
    wt_all, l = w_in
    outs = pl.pallas_call(
        functools.partial(_inproj_cast_body, layer=l),
        grid=(1,),
        in_specs=[_full_spec(x.shape), _pspec(g),
                  pl.BlockSpec(wt_all.shape, lambda i: (0, 0, 0), pipeline_mode=pl.Buffered(1))],
        out_specs=[_full_spec((rows, n)) for n in widths] + [_full_spec((sum(widths), d))],
        out_shape=[jax.ShapeDtypeStruct((rows, n), F32) for n in widths]
                  + [jax.ShapeDtypeStruct((sum(widths), d), BF16)],
        compiler_params=_cparams("arbitrary"),
        name="inproj_cast",
    )(x, _parg(g), wt_all)
    return outs[:-1], outs[-1]


def _inproj(x, g, w, widths):
    rows, d = x.shape
    row_spec = lambda w_: pl.BlockSpec((ROW_TILE, w_), lambda i: (i, 0))
    return pl.pallas_call(
        _inproj_body,
        grid=(rows // ROW_TILE,),
        in_specs=[row_spec(d), _pspec(g), _pspec(w, single=True)],
        out_specs=[row_spec(n) for n in widths],
        out_shape=[jax.ShapeDtypeStruct((rows, n), F32) for n in widths],
        compiler_params=_cparams("parallel"),
        name="inproj",
    )(x, _parg(g), _parg(w))


def _ssd_body(z_ref, xbc_ref, dt_ref, cw_ref, cb_ref, dtb_ref, alog_ref, dsk_ref, ng_ref,
              y_ref, conv_ref, hout_ref, xpad_scr, h_scr, *, chunk, group):
    L, G = chunk, group
    GL = G * L
    c = pl.program_id(1)
    pad = SUBLANES
    halo = SSD_CONV - 1
    hpg = SSD_HEADS // SSD_GROUPS
    assert hpg == 2 and hpg * SSD_HEAD_DIM == SSD_STATE

    @pl.when(c == 0)
    def _():
        xpad_scr[0:pad, :] = jnp.zeros((pad, SSD_CONV_DIM), F32)
        h_scr[...] = jnp.zeros(h_scr.shape, F32)

    xpad_scr[pad:pad + GL, :] = xbc_ref[...]
    xfull = xpad_scr[...]
    conv = cb_ref[...] + cw_ref[halo:halo + 1, :] * xfull[pad:pad + GL]
    for j in range(halo):
        conv = conv + cw_ref[j:j + 1, :] * pltpu.roll(xfull, halo - j, axis=0)[pad:pad + GL]
    xpad_scr[pad - halo:pad, :] = xpad_scr[pad + GL - halo:pad + GL, :]
    conv = _silu(conv)
    xs = conv[:, 0:GROUP_WIDTH]
    bm = conv[:, GROUP_WIDTH:2 * GROUP_WIDTH].astype(BF16)
    cm = conv[:, 2 * GROUP_WIDTH:3 * GROUP_WIDTH].astype(BF16)

    row = lax.broadcasted_iota(jnp.int32, (L, L), 0)
    col = lax.broadcasted_iota(jnp.int32, (L, L), 1)
    causal = row >= col
    tril = jnp.where(causal, 1.0, 0.0).astype(F32)
    dt = _softplus(dt_ref[...] + dtb_ref[...])
    da = dt * (-jnp.exp(alog_ref[...]) * LOG2_E)
    acs = [_dot(tril, da[i * L:(i + 1) * L, :], precision=HIGHEST) for i in range(G)]
    acs_t = [a.T for a in acs]
    e_acs = [jnp.exp2(a) for a in acs]
    e_end = [jnp.exp2(a[L - 1:L, :] - a) for a in acs]
    e_last = [jnp.exp2(a[L - 1:L, :]) for a in acs]

    keys = [(i, g) for i in range(G) for g in range(SSD_GROUPS)]
    rows_of = lambda x, i: x[i * L:(i + 1) * L]
    lanes_of = lambda x, g: x[:, g * SSD_STATE:(g + 1) * SSD_STATE]
    lane_lo = lax.broadcasted_iota(jnp.int32, (L, hpg * SSD_HEAD_DIM), 1) < SSD_HEAD_DIM
    row_lo = lax.broadcasted_iota(jnp.int32, (hpg * SSD_HEAD_DIM, SSD_STATE), 0) < SSD_HEAD_DIM
    head_cols = lambda a, g: jnp.where(lane_lo, a[:, g * hpg:g * hpg + 1], a[:, g * hpg + 1:g * hpg + 2])
    bg = {(i, g): lanes_of(rows_of(bm, i), g) for i, g in keys}
    cg = {(i, g): lanes_of(rows_of(cm, i), g) for i, g in keys}
    scores = {k: _dot_nt(cg[k], bg[k]) for k in keys}
    xdt = {(i, g): lanes_of(rows_of(xs, i), g) * head_cols(rows_of(dt, i), g) for i, g in keys}
    decay = {(i, h): jnp.exp2(jnp.where(causal, acs[i][:, h:h + 1] - acs_t[i][h:h + 1, :], -jnp.inf))
             for i in range(G) for h in range(SSD_HEADS)}
    p_mat = {(i, g): jnp.concatenate([(scores[(i, g)] * decay[(i, g * hpg + k)]).astype(BF16) for k in range(hpg)],
                                     axis=1) for i, g in keys}
    y_in = {k: _dot(p_mat[k], _bd(xdt[k].astype(BF16))) for k in keys}
    st = {(i, g): _dot_tn((xdt[(i, g)] * head_cols(e_end[i], g)).astype(BF16), bg[(i, g)]) for i, g in keys}

    y_rows = []
    for i in range(G):
        ys = []
        for g in range(SSD_GROUPS):
            h_prev = h_scr[g * hpg:(g + 1) * hpg].reshape(hpg * SSD_HEAD_DIM, SSD_STATE)
            ys.append(y_in[(i, g)] + _dot_nt(cg[(i, g)], h_prev.astype(BF16)) * head_cols(e_acs[i], g))
            keep = jnp.where(row_lo, e_last[i][:, g * hpg:g * hpg + 1], e_last[i][:, g * hpg + 1:g * hpg + 2])
            h_scr[g * hpg:(g + 1) * hpg] = (h_prev * keep + st[(i, g)]).reshape(hpg, SSD_HEAD_DIM, SSD_STATE)
        y_rows.append(jnp.concatenate(ys, axis=-1))
    y = jnp.concatenate(y_rows, axis=0) + xs * dsk_ref[...]
    y = y * _silu(z_ref[...])
    y_ref[...] = _rms(y, ng_ref[...])

    @pl.when(c == pl.num_programs(1) - 1)
    def _():
        hout_ref[0] = h_scr[...]
        conv_ref[0] = xpad_scr[pad - halo:pad, :]


def _ssd(z, xbc, dtr, lp, *, batch, seq):
    chunk = SSD_CHUNK
    rows = chunk * SSD_GROUP
    nc = seq // rows
    rspec = lambda w: pl.BlockSpec((rows, w), lambda b, c: (b * nc + c, 0))
    consts = (lp["conv_w"], lp["conv_b"], lp["dt_bias"], lp["a_log"], lp["d_skip"], lp["ssd_norm"])
    return pl.pallas_call(
        functools.partial(_ssd_body, chunk=chunk, group=SSD_GROUP),
        grid=(batch, nc),
        in_specs=[rspec(GROUP_WIDTH), rspec(SSD_CONV_DIM), rspec(LANES)] + [_pspec(a) for a in consts],
        out_specs=[rspec(GROUP_WIDTH),
                   pl.BlockSpec((1, SSD_CONV - 1, SSD_CONV_DIM), lambda b, c: (b, 0, 0)),
                   pl.BlockSpec((1, SSD_HEADS, SSD_HEAD_DIM, SSD_STATE), lambda b, c: (b, 0, 0, 0))],
        out_shape=[jax.ShapeDtypeStruct((batch * seq, GROUP_WIDTH), F32),
                   jax.ShapeDtypeStruct((batch, SSD_CONV - 1, SSD_CONV_DIM), F32),
                   jax.ShapeDtypeStruct((batch, SSD_HEADS, SSD_HEAD_DIM, SSD_STATE), F32)],
        scratch_shapes=[pltpu.VMEM((SUBLANES + rows, SSD_CONV_DIM), F32),
                        pltpu.VMEM((SSD_HEADS, SSD_HEAD_DIM, SSD_STATE), F32)],
        compiler_params=_cparams("parallel", "arbitrary"),
        name="ssd",
    )(z, xbc, dtr, *[_parg(a) for a in consts])


def _ssd_step_body(z_ref, xbc_ref, dt_ref, conv0_ref, h0_ref, *rest, seq, batch, layer):
    hdone_ref, rest = (rest[0], rest[1:]) if layer else (None, rest)
    (cw_ref, cb_ref, dtb_ref, aneg_ref, dsk_ref, ng_ref, hexp_ref, y_ref, conv_ref, hout_ref,
     xs_scr, bm_scr, cm_scr, xdt_scr, dec_scr, y_scr) = rest
    T, B = seq, batch
    if layer:
        hout_ref[0:layer] = hdone_ref[...]
    GW = GROUP_WIDTH
    j = pl.program_id(0)
    tiles = SSD_STEP_TILES

    @pl.when(j == 0)
    def _():
        rows = [conv0_ref[i] for i in range(SSD_CONV - 1)]
        rows += [xbc_ref[t * B:(t + 1) * B, :] for t in range(T)]
        for t in range(T):
            conv = cb_ref[...] + cw_ref[0:1, :] * rows[t]
            for i in range(1, SSD_CONV):
                conv = conv + cw_ref[i:i + 1, :] * rows[t + i]
            conv = _silu(conv)
            xs = conv[:, 0:GW]
            xs_scr[t] = xs
            for g in range(SSD_GROUPS):
                bm_scr[t, g] = conv[:, GW + g * SSD_STATE:GW + (g + 1) * SSD_STATE].T
                cm_scr[t, g] = conv[:, 2 * GW + g * SSD_STATE:2 * GW + (g + 1) * SSD_STATE].T
            dt = _softplus(dt_ref[t * B:(t + 1) * B, :] + dtb_ref[...])
            dte = _dot(dt, hexp_ref[...], precision=HIGHEST)
            xdt_scr[t] = (xs * dte).T
            dec_scr[t] = jnp.exp(dte * aneg_ref[...]).T
        for i in range(SSD_CONV - 1):
            conv_ref[i] = rows[T + i]

    hp0 = j * tiles
    grp = hp0 // (SSD_HEAD_DIM * (SSD_HEADS // SSD_GROUPS))
    for q in range(tiles):
        hp = pl.ds(hp0 + q, 1)
        h = h0_ref[:, q, :].T
        for t in range(T):
            h = h * dec_scr[t, hp, :] + bm_scr[t, grp] * xdt_scr[t, hp, :]
            y_scr[t, hp, :] = jnp.sum(h * cm_scr[t, grp], axis=0, keepdims=True)
        hout_ref[layer, :, q, :] = h.T

    @pl.when(j == pl.num_programs(0) - 1)
    def _():
        for t in range(T):
            y = y_scr[t].T + xs_scr[t] * dsk_ref[...]
            y = y * _silu(z_ref[t * B:(t + 1) * B, :])
            y_ref[t * B:(t + 1) * B, :] = _rms(y, ng_ref[...])


def _layer_state_specs(layer, block, axis):
    idx = lambda first: (lambda j: (first,) + tuple(j if a == axis else 0 for a in range(len(block))))
    cur = pl.BlockSpec((None,) + block, idx(layer))
    prev = [pl.BlockSpec((layer,) + block, idx(0))] if layer else []
    out = pl.BlockSpec((layer + 1,) + block, idx(0))
    return cur, prev, out


def _ssd_step(z, xbc, dtr, conv_all, h_all, h_done, lp, *, batch, seq, layer):
    n = batch * seq
    srows = SSD_HEADS * SSD_HEAD_DIM
    consts = (lp["conv_w"], lp["conv_b"], lp["dt_bias"], lp["a_neg_exp"], lp["d_skip"], lp["ssd_norm"], lp["head_expand"])
    hspec, prev_specs, hout_spec = _layer_state_specs(layer, (batch, SSD_STEP_TILES, SSD_STATE), 1)
    prev_args = [h_done] if layer else []
    cshape = (SSD_CONV - 1, batch, SSD_CONV_DIM)
    return pl.pallas_call(
        functools.partial(_ssd_step_body, seq=seq, batch=batch, layer=layer),
        grid=(srows // SSD_STEP_TILES,),
        in_specs=[_full_spec((n, GROUP_WIDTH)), _full_spec((n, SSD_CONV_DIM)), _full_spec((n, LANES)),
                  pl.BlockSpec((None,) + cshape, lambda j: (layer, 0, 0, 0)), hspec] + prev_specs
                 + [_pspec(a) for a in consts],
        out_specs=[_full_spec((n, GROUP_WIDTH)), _full_spec(cshape), hout_spec],
        out_shape=[jax.ShapeDtypeStruct((n, GROUP_WIDTH), F32),
                   jax.ShapeDtypeStruct(cshape, F32),
                   jax.ShapeDtypeStruct((layer + 1, batch, srows, SSD_STATE), F32)],
        scratch_shapes=[pltpu.VMEM((seq, batch, GROUP_WIDTH), F32),
                        pltpu.VMEM((seq, SSD_GROUPS, SSD_STATE, batch), F32),
                        pltpu.VMEM((seq, SSD_GROUPS, SSD_STATE, batch), F32),
                        pltpu.VMEM((seq, GROUP_WIDTH, batch), F32),
                        pltpu.VMEM((seq, GROUP_WIDTH, batch), F32),
                        pltpu.VMEM((seq, GROUP_WIDTH, batch), F32)],
        compiler_params=_cparams("arbitrary"),
        name="ssd_step",
    )(z, xbc, dtr, conv_all, h_all.reshape(h_all.shape[0], batch, srows, SSD_STATE),
      *prev_args, *[_parg(a) for a in consts])


PAIR = 2 * RWKV_HEAD
RWKV_PAIRS = RWKV_HEADS // 2


def _bd(x):
    half = x.shape[1] // 2
    lane = lax.broadcasted_iota(jnp.int32, x.shape, 1)
    zero = jnp.zeros_like(x)
    return jnp.concatenate([jnp.where(lane < half, x, zero), jnp.where(lane >= half, x, zero)], axis=0)


def _half_sums(x, lo):
    s_lo = jnp.sum(jnp.where(lo, x, 0.0), axis=-1, keepdims=True)
    s_hi = jnp.sum(jnp.where(lo, 0.0, x), axis=-1, keepdims=True)
    return jnp.where(lo, s_lo, s_hi)


def _head_sum(x):
    lo = lax.broadcasted_iota(jnp.int32, (x.shape[0], PAIR), 1) < RWKV_HEAD
    return jnp.concatenate([_half_sums(x[:, p * PAIR:(p + 1) * PAIR], lo) for p in range(RWKV_PAIRS)], axis=-1)


def _rwkv_pointwise(u, prev, mu_ref, w0_ref, w2_ref, a0_ref, a2_ref, g2_ref, kk_ref, ka_ref):
    GW = GROUP_WIDTH
    xs = u + (prev - u) * mu_ref[...]
    r = xs[:, 0:GW]
    k = xs[:, GW:2 * GW]
    v = xs[:, 2 * GW:3 * GW]
    wd = xs[:, 3 * GW:3 * GW + 64]
    ad = xs[:, 3 * GW + 64:3 * GW + 128]
    gd = xs[:, 3 * GW + 128:3 * GW + 256]
    w_lin = w0_ref[...] + _dot(jnp.tanh(wd).astype(BF16), w2_ref[...])
    logdecay = -math.exp(-0.5) * _sigmoid(w_lin)
    a = _sigmoid(a0_ref[...] + _dot(ad.astype(BF16), a2_ref[...]))
    g = _dot(_sigmoid(gd).astype(BF16), g2_ref[...])
    kk = k * kk_ref[...]
    kk = kk * lax.rsqrt(jnp.maximum(_head_sum(kk * kk), 1e-24))
    k = k * (1.0 + (a - 1.0) * ka_ref[...])
    return r, k, v, logdecay, a, g, kk


def _rwkv_finish(y, r, k, v, g, rk_ref, lng_ref, lnb_ref):
    mean = _head_sum(y) * (1.0 / RWKV_HEAD)
    yc = y - mean
    var = _head_sum(yc * yc) * (1.0 / RWKV_HEAD)
    y = yc * lax.rsqrt(var + RWKV_LN_EPS) * lng_ref[...] + lnb_ref[...]
    bonus = _head_sum(r * k * rk_ref[...]) * v
    return (y + bonus) * g


def _rwkv_body(u_ref, mu_ref, w0_ref, w2_ref, a0_ref, a2_ref, g2_ref, kk_ref, ka_ref, rk_ref,
               lng_ref, lnb_ref, y_ref, shift_ref, sout_ref, upad_scr, s_scr, *, chunk, group):
    L, G = chunk, group
    GL = G * L
    c = pl.program_id(1)
    pad = SUBLANES

    @pl.when(c == 0)
    def _():
        upad_scr[0:pad, :] = jnp.zeros((pad, RWKV_PROJ), F32)
        s_scr[...] = jnp.zeros(s_scr.shape, F32)

    u = u_ref[...]
    upad_scr[pad:pad + GL, :] = u
    prev = pltpu.roll(upad_scr[...], 1, axis=0)[pad:pad + GL]
    upad_scr[pad - 1:pad, :] = u[GL - 1:GL, :]
    r, k, v, logdecay, a, g, kk = _rwkv_pointwise(u, prev, mu_ref, w0_ref, w2_ref, a0_ref, a2_ref, g2_ref,
                                                  kk_ref, ka_ref)

    tril = jnp.where(lax.broadcasted_iota(jnp.int32, (L, L), 0) >= lax.broadcasted_iota(jnp.int32, (L, L), 1),
                     1.0, 0.0).astype(F32)
    cl = jnp.concatenate([_dot(tril, logdecay[i * L:(i + 1) * L, :], precision=HIGHEST) for i in range(G)], axis=0)
    e_in = jnp.exp(cl)
    e_inv = jnp.exp(-cl)
    r_t = r * e_in
    r_tb = r_t.astype(BF16)
    a_tb = (-kk * jnp.exp(cl - logdecay)).astype(BF16)
    b_tb = (kk * a * e_inv).astype(BF16)
    k_tb = (k * e_inv).astype(BF16)
    vb = v.astype(BF16)

    row = lax.broadcasted_iota(jnp.int32, (L, PAIR), 0)
    colh = lax.broadcasted_iota(jnp.int32, (L, PAIR), 1) & (RWKV_HEAD - 1)
    strict = row > colh
    incl = row >= colh
    eye_pair = jnp.where(row == colh, 1.0, 0.0).astype(F32)
    lane_lo = lax.broadcasted_iota(jnp.int32, (RWKV_HEAD, PAIR), 1) < RWKV_HEAD
    same_head = (lax.broadcasted_iota(jnp.int32, (PAIR, PAIR), 0) < RWKV_HEAD) == \
                (lax.broadcasted_iota(jnp.int32, (PAIR, PAIR), 1) < RWKV_HEAD)

    streams = [(i, p) for i in range(G) for p in range(RWKV_PAIRS)]
    ns = len(streams)
    blk = lambda x, i, p: x[i * L:(i + 1) * L, p * PAIR:(p + 1) * PAIR]
    lhs = [jnp.concatenate([blk(a_tb, i, p), blk(r_tb, i, p)], axis=0) for i, p in streams]
    m_both = [_dot_nt(lhs[s], jnp.concatenate([_bd(blk(b_tb, i, p)), _bd(blk(k_tb, i, p))], axis=0))
              for s, (i, p) in enumerate(streams)]
    m_ab = [m[:, 0:PAIR] for m in m_both]
    m_ak = [m[:, PAIR:2 * PAIR] for m in m_both]
    n_ab = [jnp.where(strict, m[0:L], 0.0) for m in m_ab]
    m_rb = [jnp.where(incl, m[L:2 * L], 0.0).astype(BF16) for m in m_ab]
    n_ak = [jnp.where(strict, m[0:L], 0.0).astype(BF16) for m in m_ak]
    m_rk = [jnp.where(incl, m[L:2 * L], 0.0).astype(BF16) for m in m_ak]
    tinv = [eye_pair + n for n in n_ab]
    pwb = [n.astype(BF16) for n in n_ab]
    pw = [_dot(x, _bd(x)) for x in pwb]
    for _ in range(int(math.log2(L)) - 2):
        pwb = [x.astype(BF16) for x in pw]
        both = [_dot(jnp.concatenate([pwb[s], tinv[s].astype(BF16)], axis=0), _bd(pwb[s])) for s in range(ns)]
        pw = [x[0:L] for x in both]
        tinv = [tinv[s] + both[s][L:2 * L] for s in range(ns)]
    pwb = [x.astype(BF16) for x in pw]
    tinv = [tinv[s] + _dot(tinv[s].astype(BF16), _bd(pwb[s])) for s in range(ns)]
    tinvb = [x.astype(BF16) for x in tinv]
    nv_mv = [_dot(jnp.concatenate([n_ak[s], m_rk[s]], axis=0), _bd(blk(vb, i, p))) for s, (i, p) in enumerate(streams)]
    wu = [_dot(tinvb[s], jnp.concatenate([_bd(blk(a_tb, i, p)), _bd(nv_mv[s][0:L].astype(BF16))], axis=1))
          for s, (i, p) in enumerate(streams)]
    wub = [x.astype(BF16) for x in wu]
    qy = [_dot(m_rb[s], jnp.concatenate([_bd(wub[s][:, 0:PAIR]), _bd(wub[s][:, PAIR:2 * PAIR])], axis=1))
          for s in range(ns)]
    q = [(blk(r_t, i, p) + qy[s][:, 0:PAIR]).astype(BF16) for s, (i, p) in enumerate(streams)]
    y_loc = [qy[s][:, PAIR:2 * PAIR] + nv_mv[s][L:2 * L] for s in range(ns)]
    zeros_b = jnp.zeros((L, PAIR), BF16)
    mg = [_dot_tn(jnp.concatenate([wub[s], jnp.concatenate([zeros_b, blk(vb, i, p)], axis=1)], axis=0),
                  jnp.concatenate([blk(b_tb, i, p), blk(k_tb, i, p)], axis=0))
          for s, (i, p) in enumerate(streams)]
    p_end = [e_in[(i + 1) * L - 1:(i + 1) * L, p * PAIR:(p + 1) * PAIR] for i, p in streams]
    m_t = [(jnp.where(same_head, mg[s][0:PAIR], 0.0) * p_end[s]).astype(BF16) for s in range(ns)]
    g_t = [jnp.where(lane_lo, mg[s][PAIR:PAIR + RWKV_HEAD], mg[s][PAIR + RWKV_HEAD:2 * PAIR]) * p_end[s]
           for s in range(ns)]

    y_rows = []
    for i in range(G):
        y_pairs = []
        for p in range(RWKV_PAIRS):
            s = i * RWKV_PAIRS + p
            s0 = s_scr[p]
            s0b = s0.astype(BF16)
            y_pairs.append(_dot_nt(q[s], _bd(s0b)) + y_loc[s])
            s_scr[p] = s0 * p_end[s] + _dot(s0b, m_t[s]) + g_t[s]
        y_rows.append(jnp.concatenate(y_pairs, axis=-1))
    y = jnp.concatenate(y_rows, axis=0)
    y_ref[...] = _rwkv_finish(y, r, k, v, g, rk_ref, lng_ref, lnb_ref)

    @pl.when(c == pl.num_programs(1) - 1)
    def _():
        sout_ref[0] = s_scr[...]
        shift_ref[0] = upad_scr[pad - 1:pad, :]


_RWKV_PARAM_NAMES = ("mu", "w0", "w2", "a0", "a2", "g2", "k_k", "k_a", "r_k", "ln_g", "ln_b")


def _rwkv(u, p, *, batch, seq):
    rows = RWKV_CHUNK * RWKV_GROUP
    nc = seq // rows
    params = [p[n] for n in _RWKV_PARAM_NAMES]
    sspec = pl.BlockSpec((1, RWKV_PAIRS, RWKV_HEAD, PAIR), lambda b, c: (b, 0, 0, 0))
    y, shift, s_last = pl.pallas_call(
        functools.partial(_rwkv_body, chunk=RWKV_CHUNK, group=RWKV_GROUP),
        grid=(batch, nc),
        in_specs=[pl.BlockSpec((rows, RWKV_PROJ), lambda b, c: (b * nc + c, 0))] + [_pspec(a) for a in params],
        out_specs=[pl.BlockSpec((rows, GROUP_WIDTH), lambda b, c: (b * nc + c, 0)),
                   pl.BlockSpec((1, 1, RWKV_PROJ), lambda b, c: (b, 0, 0)), sspec],
        out_shape=[jax.ShapeDtypeStruct((batch * seq, GROUP_WIDTH), F32),
                   jax.ShapeDtypeStruct((batch, 1, RWKV_PROJ), F32),
                   jax.ShapeDtypeStruct((batch, RWKV_PAIRS, RWKV_HEAD, PAIR), F32)],
        scratch_shapes=[pltpu.VMEM((SUBLANES + rows, RWKV_PROJ), F32),
                        pltpu.VMEM((RWKV_PAIRS, RWKV_HEAD, PAIR), F32)],
        compiler_params=_cparams("parallel", "arbitrary"),
        name="rwkv",
    )(u, *[_parg(a) for a in params])
    s_last = s_last.reshape(batch, RWKV_PAIRS, RWKV_HEAD, 2, RWKV_HEAD).transpose(0, 1, 3, 2, 4).reshape(
        batch, RWKV_HEADS, RWKV_HEAD, RWKV_HEAD)
    return y, shift.reshape(batch, RWKV_PROJ), s_last


def _rwkv_step_body(u_ref, shift0_ref, s0_ref, *rest, seq, batch, layer):
    sdone_ref, rest = (rest[0], rest[1:]) if layer else (None, rest)
    (mu_ref, w0_ref, w2_ref, a0_ref, a2_ref, g2_ref, kk_ref, ka_ref, rk_ref, lng_ref, lnb_ref, y_ref, sout_ref,
     r_scr, w_scr, k_scr, b_scr, nkk_scr, v_scr, y_scr) = rest
    T, B = seq, batch
    j = pl.program_id(0)
    if layer:
        sout_ref[0:layer] = sdone_ref[...]
    tiles = RWKV_STEP_TILES

    def pointwise(t):
        u = u_ref[t * B:(t + 1) * B, :]
        prev = shift0_ref[...] if t == 0 else u_ref[(t - 1) * B:t * B, :]
        return _rwkv_pointwise(u, prev, mu_ref, w0_ref, w2_ref, a0_ref, a2_ref, g2_ref, kk_ref, ka_ref)

    @pl.when(j == 0)
    def _():
        for t in range(T):
            r, k, v, logdecay, a, _, kk = pointwise(t)
            r_scr[t] = r.T
            w_scr[t] = jnp.exp(logdecay).T
            k_scr[t] = k.T
            b_scr[t] = (kk * a).T
            nkk_scr[t] = (-kk).T
            v_scr[t] = v.T

    i0 = j * tiles
    keys = pl.ds(pl.multiple_of((i0 // RWKV_HEAD) * RWKV_HEAD, RWKV_HEAD), RWKV_HEAD)
    for q in range(tiles):
        vi = pl.ds(i0 + q, 1)
        s = s0_ref[q]
        for t in range(T):
            sa = jnp.sum(s * nkk_scr[t, keys, :], axis=0, keepdims=True)
            s = s * w_scr[t, keys, :] + k_scr[t, keys, :] * v_scr[t, vi, :] + b_scr[t, keys, :] * sa
            y_scr[t, vi, :] = jnp.sum(s * r_scr[t, keys, :], axis=0, keepdims=True)
        sout_ref[layer, q] = s

    @pl.when(j == pl.num_programs(0) - 1)
    def _():
        for t in range(T):
            r, k, v, _, _, g, _ = pointwise(t)
            y_ref[t * B:(t + 1) * B, :] = _rwkv_finish(y_scr[t].T, r, k, v, g, rk_ref, lng_ref, lnb_ref)


def _rwkv_step(u, shift0, s_all, s_done, p, *, batch, seq, layer):
    n = batch * seq
    srows = RWKV_HEADS * RWKV_HEAD
    params = [p[nm] for nm in _RWKV_PARAM_NAMES]
    sspec, prev_specs, sout_spec = _layer_state_specs(layer, (RWKV_STEP_TILES, RWKV_HEAD, batch), 0)
    prev_args = [s_done] if layer else []
    tposed = pltpu.VMEM((seq, GROUP_WIDTH, batch), F32)
    return pl.pallas_call(
        functools.partial(_rwkv_step_body, seq=seq, batch=batch, layer=layer),
        grid=(srows // RWKV_STEP_TILES,),
        in_specs=[_full_spec((n, RWKV_PROJ)), _full_spec((batch, RWKV_PROJ)), sspec] + prev_specs
                 + [_pspec(a) for a in params],
        out_specs=[_full_spec((n, GROUP_WIDTH)), sout_spec],
        out_shape=[jax.ShapeDtypeStruct((n, GROUP_WIDTH), F32),
                   jax.ShapeDtypeStruct((layer + 1, srows, RWKV_HEAD, batch), F32)],
        scratch_shapes=[tposed] * 7,
        compiler_params=_cparams("arbitrary"),
        name="rwkv_step",
    )(u, shift0, s_all, *prev_args, *[_parg(a) for a in params])


def _s5_body(u_ref, hre0_ref, him0_ref, are_ref, aim_ref, bmat_ref, cmat_ref, d_ref, gw_ref, gb_ref,
             y_ref, hre_ref, him_ref, hs_scr, tm_scr, *, steps, batch_major):
    c = pl.program_id(1)
    ns = S5_WIDTH
    bsub = SUBLANES

    @pl.when(c == 0)
    def _():
        hre_ref[...] = hre0_ref[...]
        him_ref[...] = him0_ref[...]

    if batch_major:
        for b in range(bsub):
            tm_scr[:, b, :] = u_ref[b]
        u = tm_scr[...].reshape(steps * bsub, GROUP_WIDTH)
    else:
        u = u_ref[...].reshape(steps * bsub, GROUP_WIDTH)
    are = jnp.broadcast_to(are_ref[...], (bsub, ns))
    aim = jnp.broadcast_to(aim_ref[...], (bsub, ns))
    hre, him = hre_ref[...], him_ref[...]
    sub = min(S5_SUB, steps)
    rows = sub * bsub
    outs = []
    hs_scr[...] = _dot(u.astype(BF16), bmat_ref[...])
    for k in range(steps // sub):
        r0 = k * rows
        u_k = u[r0:r0 + rows]
        for t in range(sub):
            rs = slice(r0 + t * bsub, r0 + (t + 1) * bsub)
            hre, him = (are * hre - aim * him + hs_scr[rs, 0:ns], are * him + aim * hre + hs_scr[rs, ns:2 * ns])
            hs_scr[rs, 0:ns] = hre
            hs_scr[rs, ns:2 * ns] = him
        y = _dot(hs_scr[r0:r0 + rows, :].astype(BF16), cmat_ref[...]) + u_k * d_ref[...]
        y = _gelu_tanh(y)
        yy = _dot(y.astype(BF16), gw_ref[...]) + gb_ref[...]
        outs.append(yy[:, 0:GROUP_WIDTH] * _sigmoid(yy[:, GROUP_WIDTH:2 * GROUP_WIDTH]))
    hre_ref[...] = hre
    him_ref[...] = him
    out = jnp.concatenate(outs, axis=0).reshape(steps, bsub, GROUP_WIDTH)
    if batch_major:
        tm_scr[...] = out
        for b in range(bsub):
            y_ref[b] = tm_scr[:, b, :]
    else:
        y_ref[...] = out


def _time_specs(u, batch_major, chunk):
    bsub = SUBLANES
    if batch_major:
        batch, seq, _ = u.shape
        steps = min(chunk, seq)
        spec = pl.BlockSpec((bsub, steps, GROUP_WIDTH), lambda b, c: (b, c, 0))
    else:
        seq, batch, _ = u.shape
        steps = min(chunk, seq)
        spec = pl.BlockSpec((steps, bsub, GROUP_WIDTH), lambda b, c: (c, b, 0))
    return batch, seq, steps, spec


def _s5(u, hre0, him0, lp, *, batch_major):
    batch, seq, steps, tspec = _time_specs(u, batch_major, TM_CHUNK)
    bsub = SUBLANES
    hspec = pl.BlockSpec((bsub, S5_WIDTH), lambda b, c: (b, 0))
    consts = (lp["s5_are"], lp["s5_aim"], lp["s5_bmat"], lp["s5_cmat"], lp["s5_d"], lp["s5_gw"], lp["s5_gb"])
    return pl.pallas_call(
        functools.partial(_s5_body, steps=steps, batch_major=batch_major),
        grid=(batch // bsub, seq // steps),
        in_specs=[tspec, hspec, hspec] + [_pspec(a) for a in consts],
        out_specs=[tspec, hspec, hspec],
        out_shape=[jax.ShapeDtypeStruct(u.shape, F32),
                   jax.ShapeDtypeStruct((batch, S5_WIDTH), F32),
                   jax.ShapeDtypeStruct((batch, S5_WIDTH), F32)],
        scratch_shapes=[pltpu.VMEM((steps * bsub, 2 * S5_WIDTH), F32),
                        pltpu.VMEM((steps, bsub, GROUP_WIDTH), F32)],
        compiler_params=_cparams("parallel", "arbitrary"),
        name="s5",
    )(u, hre0, him0, *[_parg(a) for a in consts])


def _pool_body(u_ref, buf0_ref, pw_ref, sc_ref, y_ref, buf_ref, f_scr, tm_scr, *, steps, pos0, batch_major):
    c = pl.program_id(1)
    bsub = SUBLANES
    GW = GROUP_WIDTH
    halo = POOL_BUF + 1

    @pl.when(c == 0)
    def _():
        f_scr[0] = jnp.zeros((bsub, GW), F32)
        f_scr[1:halo] = buf0_ref[...]

    if batch_major:
        for b in range(bsub):
            f_scr[halo:halo + steps, b, :] = u_ref[b]
    else:
        f_scr[halo:halo + steps] = u_ref[...]
    f = f_scr[...]
    u = f[halo:halo + steps]
    s2 = f[1:] + f[:-1]
    s4 = s2[2:] + s2[:-2]
    s8 = s4[4:] + s4[:-4]
    s16 = s8[8:] + s8[:-8]
    f_scr[0:halo] = f[steps:steps + halo]
    lane = lax.broadcasted_iota(jnp.int32, (steps, bsub, GW), 2)
    tpos = lax.broadcasted_iota(jnp.int32, (steps, bsub, GW), 0) + (pos0 + 1) + c * steps
    win = jnp.where(lane < POOL_CH, s2[halo - 1:halo - 1 + steps],
                    jnp.where(lane < 2 * POOL_CH, s4[halo - 3:halo - 3 + steps],
                              jnp.where(lane < 3 * POOL_CH, s8[halo - 7:halo - 7 + steps],
                                        s16[halo - 15:halo - 15 + steps])))
    wlen = jnp.where(lane < POOL_CH, POOL_WINDOWS[0],
                     jnp.where(lane < 2 * POOL_CH, POOL_WINDOWS[1],
                               jnp.where(lane < 3 * POOL_CH, POOL_WINDOWS[2], POOL_WINDOWS[3])))
    cnt = jnp.minimum(tpos, wlen).astype(F32)
    pooled = (win / cnt - u).reshape(steps * bsub, GW)
    y = (_dot(pooled.astype(BF16), pw_ref[...]) * sc_ref[...]).reshape(steps, bsub, GW)
    if batch_major:
        tm_scr[...] = y
        for b in range(bsub):
            y_ref[b] = tm_scr[:, b, :]
    else:
        y_ref[...] = y

    @pl.when(c == pl.num_programs(1) - 1)
    def _():
        buf_ref[...] = f_scr[1:halo]


def _pool(u, buf0, lp, *, pos0, batch_major, layer=None):
    batch, seq, steps, tspec = _time_specs(u, batch_major, POOL_CHUNK)
    bsub = SUBLANES
    bblock =(POOL_BUF, bsub, GROUP_WIDTH)
    bspec = pl.BlockSpec(bblock, lambda b, c: (0, b, 0))
    if layer is None:
        bspec_in = bspec
    else:
        bspec_in = pl.BlockSpec((None,) + bblock, lambda b, c: (layer, 0, b, 0))
    return pl.pallas_call(
        functools.partial(_pool_body, steps=steps, pos0=pos0, batch_major=batch_major),
        grid=(batch // bsub, seq // steps),
        in_specs=[tspec, bspec_in, _pspec(lp["pool_w"]), _pspec(lp["pool_scale"])],
        out_specs=[tspec, bspec],
        out_shape=[jax.ShapeDtypeStruct(u.shape, F32), jax.ShapeDtypeStruct((POOL_BUF, batch, GROUP_WIDTH), F32)],
        scratch_shapes=[pltpu.VMEM((POOL_BUF + 1 + steps, bsub, GROUP_WIDTH), F32),
                        pltpu.VMEM((steps, bsub, GROUP_WIDTH), F32)],
        compiler_params=_cparams("parallel", "arbitrary"),
        name="pool",
    )(u, buf0, _parg(lp["pool_w"]), _parg(lp["pool_scale"]))


def _block_diag(blocks):
    n, g, r, c = blocks.shape
    eye = jnp.eye(g, dtype=blocks.dtype)
    return (eye[None, :, None, :, None] * blocks[:, :, :, None, :]).reshape(n, g * r, g * c)


def _stacked_params(P):
    row = lambda a: a.reshape(a.shape[0], 1, -1)
    pad_lanes = lambda a: jnp.pad(a, ((0, 0), (0, LANES - a.shape[1])))
    bf = lambda a: a.astype(BF16)

    lam = lax.complex(P["s5_lam_re"], P["s5_lam_im"])
    a_bar = jnp.exp(lam * jnp.exp(P["s5_log_step"])[..., None])
    b_bar = ((a_bar - 1.0) / lam)[..., None] * lax.complex(P["s5_b_re"], P["s5_b_im"])
    b_t = jnp.swapaxes(b_bar, 2, 3)
    bmat = jnp.concatenate([_block_diag(jnp.real(b_t)), _block_diag(jnp.imag(b_t))], axis=2)
    c_t = jnp.swapaxes(lax.complex(P["s5_c_re"], P["s5_c_im"]), 2, 3)
    cmat = jnp.concatenate([_block_diag(jnp.real(c_t)), -_block_diag(jnp.imag(c_t))], axis=1)

    out = dict(
        norm_ffn1=row(P["norm_ffn1"]), ffn1_in=P["ffn1_in"], ffn1_out=P["ffn1_out"],
        norm_mix=row(P["norm_mix"]),
        w_in=jnp.transpose(P["w_in"], (2, 0, 1)),
        conv_w=P["ssd_conv_w"], conv_b=row(P["ssd_conv_b"]),
        dt_bias=row(pad_lanes(P["ssd_dt_bias"])), a_log=row(pad_lanes(P["ssd_a_log"])),
        a_neg_exp=row(jnp.repeat(-jnp.exp(P["ssd_a_log"]), SSD_HEAD_DIM, axis=1)),
        d_skip=row(jnp.repeat(P["ssd_d"], SSD_HEAD_DIM, axis=1)), ssd_norm=row(P["ssd_norm"]),
        s5_are=row(jnp.real(a_bar)), s5_aim=row(jnp.imag(a_bar)), s5_bmat=bf(bmat), s5_cmat=bf(cmat),
        s5_d=row(P["s5_d"]), s5_gw=bf(P["s5_glu_w"]), s5_gb=row(P["s5_glu_b"]),
        pool_w=bf(_block_diag(P["pool_w"])), pool_scale=row(P["pool_scale"]),
        w_out=bf(P["w_out"]),
        norm_ffn2=row(P["norm_ffn2"]), ffn2_in=P["ffn2_in"], ffn2_out=P["ffn2_out"],
    )
    for name in _RWKV_PARAM_NAMES:
        a = P["rwkv_" + name]
        out["rwkv_" + name] = bf(a) if name in ("w2", "a2", "g2") else row(a)
    return out


def _layer_params(stacked, l):
    lp = {k: _Layered((v, l)) for k, v in stacked.items()}
    lp["rwkv"] = {n: lp["rwkv_" + n] for n in _RWKV_PARAM_NAMES}
    lp["head_expand"] = jnp.pad(jnp.repeat(jnp.eye(SSD_HEADS, dtype=F32), SSD_HEAD_DIM, axis=1),
                                ((0, LANES - SSD_HEADS), (0, 0)))
    return lp


def _mixers_prompt(lp, proj, *, batch, seq):
    z, xbc, ur, us5, upool, dtr = proj
    y_ssd, conv_new, ssd_new = _ssd(z, xbc, dtr, lp, batch=batch, seq=seq)
    y_rwkv, shift_new, rwkv_new = _rwkv(ur, lp["rwkv"], batch=batch, seq=seq)
    zeros = jnp.zeros((batch, S5_WIDTH), F32)
    bm = lambda a: a.reshape(batch, seq, a.shape[-1])
    rows = lambda a: a.reshape(batch * seq, a.shape[-1])
    y_s5, s5re, s5im = _s5(bm(us5), zeros, zeros, lp, batch_major=True)
    y_pool, pool_new = _pool(bm(upool), jnp.zeros((POOL_BUF, batch, GROUP_WIDTH), F32), lp, pos0=0,
                             batch_major=True)
    ys = (y_ssd, y_rwkv, rows(y_s5), rows(y_pool))
    states = (conv_new, ssd_new, shift_new, rwkv_new, s5re.reshape(batch, S5_GROUPS, S5_STATE),
              s5im.reshape(batch, S5_GROUPS, S5_STATE), jnp.swapaxes(pool_new, 0, 1))
    return ys, states


def _mixers_decode(lp, proj, states, done, *, batch, seq, layer):
    z, xbc, ur, us5, upool, dtr = proj
    shift0, s5re0, s5im0 = (states[i][layer] for i in (2, 4, 5))
    ssd_done, rwkv_done = (done[1], done[3]) if layer else (None, None)
    y_ssd, conv_new, ssd_new = _ssd_step(z, xbc, dtr, states[0], states[1], ssd_done, lp, batch=batch, seq=seq,
                                         layer=layer)
    y_rwkv, rwkv_new = _rwkv_step(ur, shift0, states[3], rwkv_done, lp["rwkv"], batch=batch, seq=seq, layer=layer)
    shift_new = ur[(seq - 1) * batch:, :]
    tm = lambda a: a.reshape(seq, batch, a.shape[-1])
    y_s5, s5re, s5im = _s5(tm(us5), s5re0.reshape(batch, S5_WIDTH), s5im0.reshape(batch, S5_WIDTH), lp,
                           batch_major=False)
    y_pool, pool_new = _pool(tm(upool), states[6], lp, pos0=PAST_LEN, batch_major=False, layer=layer)
    rows = lambda a: a.reshape(seq * batch, a.shape[-1])
    ys = (y_ssd, y_rwkv, rows(y_s5), rows(y_pool))
    new_states = (jnp.swapaxes(conv_new, 0, 1), ssd_new, shift_new, rwkv_new,
                  s5re.reshape(batch, S5_GROUPS, S5_STATE), s5im.reshape(batch, S5_GROUPS, S5_STATE),
                  jnp.swapaxes(pool_new, 0, 1))
    return ys, new_states


_WIDTHS = (GROUP_WIDTH, SSD_CONV_DIM, RWKV_PROJ, GROUP_WIDTH, GROUP_WIDTH, LANES)


def _trunk(x_p, x_s, layer_params, norm_final, mixers_p, mixers_s):
    st_p, st_s = [], []
    mix_p, mix_s, lp = None, None, None
    for l, lp_next in enumerate(layer_params):
        if l > 0:
            x_s, wg, wu, wo = _ffn_cast(x_s, lp["norm_ffn2"], lp["ffn2_in"], lp["ffn2_out"], mix=mix_s, wmix=lp["w_out"])
            x_p = _ffn(x_p, lp["norm_ffn2"], wg, wu, wo, mix=mix_p, wmix=lp["w_out"])
        lp = lp_next
        x_s, wg, wu, wo = _ffn_cast(x_s, lp["norm_ffn1"], lp["ffn1_in"], lp["ffn1_out"])
        x_p = _ffn(x_p, lp["norm_ffn1"], wg, wu, wo)
        proj_s, w_all = _inproj_cast(x_s, lp["norm_mix"], lp["w_in"], _WIDTHS)
        mix_p, st = mixers_p(l, lp, _inproj(x_p, lp["norm_mix"], w_all, _WIDTHS), st_p[-1] if st_p else None)
        st_p.append(st)
        mix_s, st = mixers_s(l, lp, proj_s, st_s[-1] if st_s else None)
        st_s.append(st)
    x_s, wg, wu, wo = _ffn_cast(x_s, lp["norm_ffn2"], lp["ffn2_in"], lp["ffn2_out"], mix=mix_s, wmix=lp["w_out"],
                                gf=norm_final)
    x_p = _ffn(x_p, lp["norm_ffn2"], wg, wu, wo, mix=mix_p, wmix=lp["w_out"], gf=norm_final)
    return (x_p, x_s), (st_p, st_s)


def kernel(x_prompt, x_sample, state_ssd_conv, state_ssd, state_rwkv_shift, state_rwkv, state_s5_re, state_s5_im, state_pool, norm_ffn1, ffn1_in, ffn1_out, norm_mix, w_in, ssd_conv_w, ssd_conv_b, ssd_dt_bias, ssd_a_log, ssd_d, ssd_norm, rwkv_mu, rwkv_w0, rwkv_w2, rwkv_a0, rwkv_a2, rwkv_g2, rwkv_k_k, rwkv_k_a, rwkv_r_k, rwkv_ln_g, rwkv_ln_b, s5_lam_re, s5_lam_im, s5_log_step, s5_b_re, s5_b_im, s5_c_re, s5_c_im, s5_d, s5_glu_w, s5_glu_b, pool_w, pool_scale, w_out, norm_ffn2, ffn2_in, ffn2_out, norm_final):
    P = dict(norm_ffn1=norm_ffn1, ffn1_in=ffn1_in, ffn1_out=ffn1_out, norm_mix=norm_mix, w_in=w_in,
             ssd_conv_w=ssd_conv_w, ssd_conv_b=ssd_conv_b, ssd_dt_bias=ssd_dt_bias, ssd_a_log=ssd_a_log,
             ssd_d=ssd_d, ssd_norm=ssd_norm, rwkv_mu=rwkv_mu, rwkv_w0=rwkv_w0, rwkv_w2=rwkv_w2, rwkv_a0=rwkv_a0,
             rwkv_a2=rwkv_a2, rwkv_g2=rwkv_g2, rwkv_k_k=rwkv_k_k, rwkv_k_a=rwkv_k_a,
             rwkv_r_k=rwkv_r_k.reshape(rwkv_r_k.shape[0], -1), rwkv_ln_g=rwkv_ln_g, rwkv_ln_b=rwkv_ln_b,
             s5_lam_re=s5_lam_re, s5_lam_im=s5_lam_im, s5_log_step=s5_log_step, s5_b_re=s5_b_re, s5_b_im=s5_b_im,
             s5_c_re=s5_c_re, s5_c_im=s5_c_im, s5_d=s5_d, s5_glu_w=s5_glu_w, s5_glu_b=s5_glu_b, pool_w=pool_w,
             pool_scale=pool_scale, w_out=w_out, norm_ffn2=norm_ffn2, ffn2_in=ffn2_in, ffn2_out=ffn2_out)
    depth = norm_ffn1.shape[0]
    bp, tp, d = x_prompt.shape
    bs, ts, _ = x_sample.shape
    stacked = _stacked_params(P)
    layer_params = [_layer_params(stacked, l) for l in range(depth)]
    gf = norm_final.reshape(1, -1)
    sample_states = (state_ssd_conv, state_ssd, state_rwkv_shift, state_rwkv, state_s5_re, state_s5_im, state_pool)
    rwkv_rows = RWKV_HEADS * RWKV_HEAD
    decode_states = (jnp.swapaxes(state_ssd_conv, 1, 2), state_ssd, state_rwkv_shift,
                     jnp.transpose(state_rwkv, (0, 2, 3, 4, 1)).reshape(depth, rwkv_rows, RWKV_HEAD, bs),
                     state_s5_re, state_s5_im, jnp.swapaxes(state_pool, 1, 2))

    x_s = jnp.swapaxes(x_sample, 0, 1).reshape(ts * bs, d)
    (y_p, y_s), (st_p, st_s) = _trunk(
        x_prompt.reshape(bp * tp, d), x_s, layer_params, gf,
        lambda l, lp, proj, done: _mixers_prompt(lp, proj, batch=bp, seq=tp),
        lambda l, lp, proj, done: _mixers_decode(lp, proj, decode_states, done, batch=bs, seq=ts, layer=l))
    outs = [y_p.reshape(bp, tp, d), jnp.swapaxes(y_s.reshape(ts, bs, d), 0, 1)]
    for i, ref_state in enumerate(sample_states):
        outs.append(jnp.stack([st[i] for st in st_p]))
        if i == 1:
            outs.append(st_s[-1][i].reshape(ref_state.shape))
        elif i == 3:
            s_new = st_s[-1][i].reshape(depth, RWKV_HEADS, RWKV_HEAD, RWKV_HEAD, bs)
            outs.append(jnp.transpose(s_new, (0, 4, 1, 2, 3)))
        else:
            outs.append(jnp.stack([st[i] for st in st_s]))
    return tuple(outs)
```

```python
import functools
import math

import jax
import jax.numpy as jnp
from jax import lax
from jax.experimental import pallas as pl
from jax.experimental.pallas import tpu as pltpu

F32 = jnp.float32
BF16 = jnp.bfloat16
HIGHEST = lax.Precision.HIGHEST

SUBLANES = 8
LANES = 128
VMEM_LIMIT_BYTES = 56 * 1024 * 1024

GROUP_WIDTH = 256
SSD_HEAD_DIM = 64
SSD_HEADS = 4
SSD_GROUPS = 2
SSD_STATE = 128
SSD_CONV = 4
SSD_CONV_DIM = GROUP_WIDTH + 2 * SSD_GROUPS * SSD_STATE
SSD_CHUNK = 128
SSD_GROUP = 4
LOG2_E = math.log2(math.e)
RWKV_HEAD = 64
RWKV_HEADS = 4
RWKV_PROJ = 1024
RWKV_LN_EPS = 64e-5
RWKV_CHUNK = 64
RWKV_GROUP = 8
S5_GROUPS = 16
S5_STATE = 64
S5_WIDTH = S5_GROUPS * S5_STATE
POOL_WINDOWS = (2, 4, 8, 16)
POOL_CH = 64
POOL_BUF = 15
RMS_EPS = 1e-6
PAST_LEN = 16384

ROW_TILE = 512
FFN_CHUNK = 256
TM_CHUNK = 128
POOL_CHUNK = 256
S5_SUB = 64
SSD_STEP_TILES = 16
RWKV_STEP_TILES = 16


def _cparams(*sem):
    return pltpu.CompilerParams(dimension_semantics=sem, vmem_limit_bytes=VMEM_LIMIT_BYTES)


def _dot(a, b, **kw):
    return jnp.dot(a, b, preferred_element_type=F32, **kw)


def _dot_nt(a, b):
    return lax.dot_general(a, b, (((1,), (1,)), ((), ())), preferred_element_type=F32)


def _dot_tn(a, b):
    return lax.dot_general(a, b, (((0,), (0,)), ((), ())), preferred_element_type=F32)


def _sigmoid(x):
    return 0.5 * jnp.tanh(0.5 * x) + 0.5


def _silu(x):
    h = 0.5 * x
    return h + h * jnp.tanh(h)


def _softplus(x):
    return jnp.maximum(x, 0.0) + jnp.log(1.0 + jnp.exp(-jnp.abs(x)))


def _gelu_tanh(x):
    c = math.sqrt(2.0 / math.pi)
    return x * (0.5 * (1.0 + jnp.tanh(c * (x + 0.044715 * (x * x * x)))))


def _rms(x, g):
    return x * lax.rsqrt(jnp.mean(x * x, axis=-1, keepdims=True) + RMS_EPS) * g


def _full_spec(shape):
    n = len(shape)
    return pl.BlockSpec(shape, lambda *_: (0,) * n)


class _Layered(tuple):
    pass


def _pspec(p, single=False):
    mode = pl.Buffered(1) if single else None
    if isinstance(p, _Layered):
        a, l = p
        return pl.BlockSpec((None,) + a.shape[1:], lambda *_: (l,) + (0,) * (a.ndim - 1), pipeline_mode=mode)
    n = p.ndim
    return pl.BlockSpec(p.shape, lambda *_: (0,) * n, pipeline_mode=mode)


def _parg(p):
    return p[0] if isinstance(p, _Layered) else p


def _mix_residual(x, y_refs, wmix_ref):
    for j, y_ref in enumerate(y_refs):
        x = x + _dot(y_ref[...].astype(BF16), wmix_ref[j * GROUP_WIDTH:(j + 1) * GROUP_WIDTH, :])
    return x


def _swiglu_chunk(h, wg, wu, wo):
    act = (_silu(_dot(h, wg)) * _dot(h, wu)).astype(BF16)
    return _dot(act, wo)


def _ffn_body(*refs, has_mix, final_norm):
    it = iter(refs)
    x = next(it)[...]
    if has_mix:
        y_refs = [next(it) for _ in range(4)]
        x = _mix_residual(x, y_refs, next(it))
    g_ref, wg_ref, wu_ref, wo_ref = next(it), next(it), next(it), next(it)
    gf_ref = next(it) if final_norm else None
    o_ref = next(it)
    h = _rms(x, g_ref[...]).astype(BF16)
    acc = jnp.zeros_like(x)
    for c in range(wo_ref.shape[0] // FFN_CHUNK):
        cols = slice(c * FFN_CHUNK, (c + 1) * FFN_CHUNK)
        acc = acc + _swiglu_chunk(h, wg_ref[:, cols], wu_ref[:, cols], wo_ref[cols, :])
    x = x + 0.5 * acc
    if final_norm:
        x = _rms(x, gf_ref[...])
    o_ref[...] = x


def _ffn(x, g, wg, wu, wo, mix=None, wmix=None, gf=None):
    rows, d = x.shape
    row_spec = lambda w: pl.BlockSpec((ROW_TILE, w), lambda i: (i, 0))
    args, specs = [x], [row_spec(d)]
    if mix is not None:
        for y in mix:
            args.append(y)
            specs.append(row_spec(y.shape[1]))
        args.append(_parg(wmix))
        specs.append(_pspec(wmix, single=True))
    for a in (g, wg, wu, wo) + ((gf,) if gf is not None else ()):
        args.append(_parg(a))
        specs.append(_pspec(a, single=True))
    return pl.pallas_call(
        functools.partial(_ffn_body, has_mix=mix is not None, final_norm=gf is not None),
        grid=(rows // ROW_TILE,),
        in_specs=specs,
        out_specs=row_spec(d),
        out_shape=jax.ShapeDtypeStruct((rows, d), F32),
        compiler_params=_cparams("parallel"),
        name="ffn",
    )(*args)


def _ffn_cast_body(*refs, has_mix, final_norm):
    it = iter(refs)
    x_ref = next(it)
    if has_mix:
        y_refs = [next(it) for _ in range(4)]
        wmix_ref = next(it)
    g_ref, wg_ref, wu_ref, wo_ref = next(it), next(it), next(it), next(it)
    gf_ref = next(it) if final_norm else None
    o_ref, wg_out, wu_out, wo_out, x_scr, h_scr, acc_scr = (next(it) for _ in range(7))
    c = pl.program_id(0)

    @pl.when(c == 0)
    def _():
        x = x_ref[...]
        if has_mix:
            x = _mix_residual(x, y_refs, wmix_ref)
        x_scr[...] = x
        h_scr[...] = _rms(x, g_ref[...]).astype(BF16)
        acc_scr[...] = jnp.zeros(acc_scr.shape, F32)

    wg = wg_ref[...].astype(BF16)
    wu = wu_ref[...].astype(BF16)
    wo = wo_ref[...].astype(BF16)
    wg_out[...] = wg
    wu_out[...] = wu
    wo_out[...] = wo
    acc_scr[...] += _swiglu_chunk(h_scr[...], wg, wu, wo)

    @pl.when(c == pl.num_programs(0) - 1)
    def _():
        x = x_scr[...] + 0.5 * acc_scr[...]
        if final_norm:
            x = _rms(x, gf_ref[...])
        o_ref[...] = x


def _ffn_cast(x, g, wi, wo, mix=None, wmix=None, gf=None):
    rows, d = x.shape
    wi_all, l = wi
    wo_all, _ = wo
    d_ff = wo_all.shape[1]
    nchunks = d_ff // FFN_CHUNK
    args, specs = [x], [_full_spec(x.shape)]
    if mix is not None:
        for y in mix:
            args.append(y)
            specs.append(_full_spec(y.shape))
        args.append(_parg(wmix))
        specs.append(_pspec(wmix, single=True))
    args += [_parg(g), wi_all, wi_all, wo_all]
    specs += [_pspec(g),
              pl.BlockSpec((None, d, FFN_CHUNK), lambda c: (l, 0, c)),
              pl.BlockSpec((None, d, FFN_CHUNK), lambda c: (l, 0, c + nchunks)),
              pl.BlockSpec((None, FFN_CHUNK, d), lambda c: (l, c, 0))]
    if gf is not None:
        args.append(gf)
        specs.append(_full_spec(gf.shape))
    col_spec = pl.BlockSpec((d, FFN_CHUNK), lambda c: (0, c))
    return pl.pallas_call(
        functools.partial(_ffn_cast_body, has_mix=mix is not None, final_norm=gf is not None),
        grid=(nchunks,),
        in_specs=specs,
        out_specs=[_full_spec(x.shape), col_spec, col_spec, pl.BlockSpec((FFN_CHUNK, d), lambda c: (c, 0))],
        out_shape=[jax.ShapeDtypeStruct((rows, d), F32), jax.ShapeDtypeStruct((d, d_ff), BF16),
                   jax.ShapeDtypeStruct((d, d_ff), BF16), jax.ShapeDtypeStruct((d_ff, d), BF16)],
        scratch_shapes=[pltpu.VMEM((rows, d), F32), pltpu.VMEM((rows, d), BF16), pltpu.VMEM((rows, d), F32)],
        compiler_params=_cparams("arbitrary"),
        name="ffn_cast",
    )(*args)


def _inproj_body(x_ref, g_ref, wt_ref, *o_refs):
    h = _rms(x_ref[...], g_ref[...]).astype(BF16)
    off = 0
    for o_ref in o_refs:
        n = o_ref.shape[-1]
        o_ref[...] = _dot_nt(h, wt_ref[off:off + n, :])
        off += n


def _inproj_cast_body(x_ref, g_ref, win_ref, *o_refs, layer):
    *proj_refs, wall_ref = o_refs
    split = GROUP_WIDTH + SSD_CONV_DIM
    wt = win_ref[:, layer, :]
    tail = wt.shape[0] - split - SSD_HEADS
    wall_ref[0:split, :] = wt[0:split].astype(BF16)
    wall_ref[split:split + tail, :] = wt[split + SSD_HEADS:].astype(BF16)
    dt_rows = jnp.concatenate([wt[split:split + SSD_HEADS], jnp.zeros((LANES - SSD_HEADS, wt.shape[1]), F32)], axis=0)
    wall_ref[split + tail:, :] = dt_rows.astype(BF16)
    _inproj_body(x_ref, g_ref, wall_ref, *proj_refs)


def _inproj_cast(x, g, w_in, widths):
    rows, d = x.shape
    wt_all, l = w_in
    outs = pl.pallas_call(
        functools.partial(_inproj_cast_body, layer=l),
        grid=(1,),
        in_specs=[_full_spec(x.shape), _pspec(g),
                  pl.BlockSpec(wt_all.shape, lambda i: (0, 0, 0), pipeline_mode=pl.Buffered(1))],
        out_specs=[_full_spec((rows, n)) for n in widths] + [_full_spec((sum(widths), d))],
        out_shape=[jax.ShapeDtypeStruct((rows, n), F32) for n in widths]
                  + [jax.ShapeDtypeStruct((sum(widths), d), BF16)],
        compiler_params=_cparams("arbitrary"),
        name="inproj_cast",
    )(x, _parg(g), wt_all)
    return outs[:-1], outs[-1]


def _inproj(x, g, w, widths):
    rows, d = x.shape
    row_spec = lambda w_: pl.BlockSpec((ROW_TILE, w_), lambda i: (i, 0))
    return pl.pallas_call(
        _inproj_body,
        grid=(rows // ROW_TILE,),
        in_specs=[row_spec(d), _pspec(g), _pspec(w, single=True)],
        out_specs=[row_spec(n) for n in widths],
        out_shape=[jax.ShapeDtypeStruct((rows, n), F32) for n in widths],
        compiler_params=_cparams("parallel"),
        name="inproj",
    )(x, _parg(g), _parg(w))


def _ssd_body(z_ref, xbc_ref, dt_ref, cw_ref, cb_ref, dtb_ref, alog_ref, dsk_ref, ng_ref,
              y_ref, conv_ref, hout_ref, xpad_scr, h_scr, *, chunk, group):
    L, G = chunk, group
    GL = G * L
    c = pl.program_id(1)
    pad = SUBLANES
    halo = SSD_CONV - 1
    hpg = SSD_HEADS // SSD_GROUPS
    assert hpg == 2 and hpg * SSD_HEAD_DIM == SSD_STATE

    @pl.when(c == 0)
    def _():
        xpad_scr[0:pad, :] = jnp.zeros((pad, SSD_CONV_DIM), F32)
        h_scr[...] = jnp.zeros(h_scr.shape, F32)

    xpad_scr[pad:pad + GL, :] = xbc_ref[...]
    xfull = xpad_scr[...]
    conv = cb_ref[...] + cw_ref[halo:halo + 1, :] * xfull[pad:pad + GL]
    for j in range(halo):
        conv = conv + cw_ref[j:j + 1, :] * pltpu.roll(xfull, halo - j, axis=0)[pad:pad + GL]
    xpad_scr[pad - halo:pad, :] = xpad_scr[pad + GL - halo:pad + GL, :]
    conv = _silu(conv)
    xs = conv[:, 0:GROUP_WIDTH]
    bm = conv[:, GROUP_WIDTH:2 * GROUP_WIDTH].astype(BF16)
    cm = conv[:, 2 * GROUP_WIDTH:3 * GROUP_WIDTH].astype(BF16)

    row = lax.broadcasted_iota(jnp.int32, (L, L), 0)
    col = lax.broadcasted_iota(jnp.int32, (L, L), 1)
    causal = row >= col
    tril = jnp.where(causal, 1.0, 0.0).astype(F32)
    dt = _softplus(dt_ref[...] + dtb_ref[...])
    da = dt * (-jnp.exp(alog_ref[...]) * LOG2_E)
    acs = [_dot(tril, da[i * L:(i + 1) * L, :], precision=HIGHEST) for i in range(G)]
    acs_t = [a.T for a in acs]
    e_acs = [jnp.exp2(a) for a in acs]
    e_end = [jnp.exp2(a[L - 1:L, :] - a) for a in acs]
    e_last = [jnp.exp2(a[L - 1:L, :]) for a in acs]

    keys = [(i, g) for i in range(G) for g in range(SSD_GROUPS)]
    rows_of = lambda x, i: x[i * L:(i + 1) * L]
    lanes_of = lambda x, g: x[:, g * SSD_STATE:(g + 1) * SSD_STATE]
    lane_lo = lax.broadcasted_iota(jnp.int32, (L, hpg * SSD_HEAD_DIM), 1) < SSD_HEAD_DIM
    row_lo = lax.broadcasted_iota(jnp.int32, (hpg * SSD_HEAD_DIM, SSD_STATE), 0) < SSD_HEAD_DIM
    head_cols = lambda a, g: jnp.where(lane_lo, a[:, g * hpg:g * hpg + 1], a[:, g * hpg + 1:g * hpg + 2])
    bg = {(i, g): lanes_of(rows_of(bm, i), g) for i, g in keys}
    cg = {(i, g): lanes_of(rows_of(cm, i), g) for i, g in keys}
    scores = {k: _dot_nt(cg[k], bg[k]) for k in keys}
    xdt = {(i, g): lanes_of(rows_of(xs, i), g) * head_cols(rows_of(dt, i), g) for i, g in keys}
    decay = {(i, h): jnp.exp2(jnp.where(causal, acs[i][:, h:h + 1] - acs_t[i][h:h + 1, :], -jnp.inf))
             for i in range(G) for h in range(SSD_HEADS)}
    p_mat = {(i, g): jnp.concatenate([(scores[(i, g)] * decay[(i, g * hpg + k)]).astype(BF16) for k in range(hpg)],
                                     axis=1) for i, g in keys}
    y_in = {k: _dot(p_mat[k], _bd(xdt[k].astype(BF16))) for k in keys}
    st = {(i, g): _dot_tn((xdt[(i, g)] * head_cols(e_end[i], g)).astype(BF16), bg[(i, g)]) for i, g in keys}

    y_rows = []
    for i in range(G):
        ys = []
        for g in range(SSD_GROUPS):
            h_prev = h_scr[g * hpg:(g + 1) * hpg].reshape(hpg * SSD_HEAD_DIM, SSD_STATE)
            ys.append(y_in[(i, g)] + _dot_nt(cg[(i, g)], h_prev.astype(BF16)) * head_cols(e_acs[i], g))
            keep = jnp.where(row_lo, e_last[i][:, g * hpg:g * hpg + 1], e_last[i][:, g * hpg + 1:g * hpg + 2])
            h_scr[g * hpg:(g + 1) * hpg] = (h_prev * keep + st[(i, g)]).reshape(hpg, SSD_HEAD_DIM, SSD_STATE)
        y_rows.append(jnp.concatenate(ys, axis=-1))
    y = jnp.concatenate(y_rows, axis=0) + xs * dsk_ref[...]
    y = y * _silu(z_ref[...])
    y_ref[...] = _rms(y, ng_ref[...])

    @pl.when(c == pl.num_programs(1) - 1)
    def _():
        hout_ref[0] = h_scr[...]
        conv_ref[0] = xpad_scr[pad - halo:pad, :]


def _ssd(z, xbc, dtr, lp, *, batch, seq):
    chunk = SSD_CHUNK
    rows = chunk * SSD_GROUP
    nc = seq // rows
    rspec = lambda w: pl.BlockSpec((rows, w), lambda b, c: (b * nc + c, 0))
    consts = (lp["conv_w"], lp["conv_b"], lp["dt_bias"], lp["a_log"], lp["d_skip"], lp["ssd_norm"])
    return pl.pallas_call(
        functools.partial(_ssd_body, chunk=chunk, group=SSD_GROUP),
        grid=(batch, nc),
        in_specs=[rspec(GROUP_WIDTH), rspec(SSD_CONV_DIM), rspec(LANES)] + [_pspec(a) for a in consts],
        out_specs=[rspec(GROUP_WIDTH),
                   pl.BlockSpec((1, SSD_CONV - 1, SSD_CONV_DIM), lambda b, c: (b, 0, 0)),
                   pl.BlockSpec((1, SSD_HEADS, SSD_HEAD_DIM, SSD_STATE), lambda b, c: (b, 0, 0, 0))],
        out_shape=[jax.ShapeDtypeStruct((batch * seq, GROUP_WIDTH), F32),
                   jax.ShapeDtypeStruct((batch, SSD_CONV - 1, SSD_CONV_DIM), F32),
                   jax.ShapeDtypeStruct((batch, SSD_HEADS, SSD_HEAD_DIM, SSD_STATE), F32)],
        scratch_shapes=[pltpu.VMEM((SUBLANES + rows, SSD_CONV_DIM), F32),
                        pltpu.VMEM((SSD_HEADS, SSD_HEAD_DIM, SSD_STATE), F32)],
        compiler_params=_cparams("parallel", "arbitrary"),
        name="ssd",
    )(z, xbc, dtr, *[_parg(a) for a in consts])


def _ssd_step_body(z_ref, xbc_ref, dt_ref, conv0_ref, h0_ref, *rest, seq, batch, layer):
    hdone_ref, rest = (rest[0], rest[1:]) if layer else (None, rest)
    (cw_ref, cb_ref, dtb_ref, aneg_ref, dsk_ref, ng_ref, hexp_ref, y_ref, conv_ref, hout_ref,
     xs_scr, bm_scr, cm_scr, xdt_scr, dec_scr, y_scr) = rest
    T, B = seq, batch
    if layer:
        hout_ref[0:layer] = hdone_ref[...]
    GW = GROUP_WIDTH
    j = pl.program_id(0)
    tiles = SSD_STEP_TILES

    @pl.when(j == 0)
    def _():
        rows = [conv0_ref[i] for i in range(SSD_CONV - 1)]
        rows += [xbc_ref[t * B:(t + 1) * B, :] for t in range(T)]
        for t in range(T):
            conv = cb_ref[...] + cw_ref[0:1, :] * rows[t]
            for i in range(1, SSD_CONV):
                conv = conv + cw_ref[i:i + 1, :] * rows[t + i]
            conv = _silu(conv)
            xs = conv[:, 0:GW]
            xs_scr[t] = xs
            for g in range(SSD_GROUPS):
                bm_scr[t, g] = conv[:, GW + g * SSD_STATE:GW + (g + 1) * SSD_STATE].T
                cm_scr[t, g] = conv[:, 2 * GW + g * SSD_STATE:2 * GW + (g + 1) * SSD_STATE].T
            dt = _softplus(dt_ref[t * B:(t + 1) * B, :] + dtb_ref[...])
            dte = _dot(dt, hexp_ref[...], precision=HIGHEST)
            xdt_scr[t] = (xs * dte).T
            dec_scr[t] = jnp.exp(dte * aneg_ref[...]).T
        for i in range(SSD_CONV - 1):
            conv_ref[i] = rows[T + i]

    hp0 = j * tiles
    grp = hp0 // (SSD_HEAD_DIM * (SSD_HEADS // SSD_GROUPS))
    for q in range(tiles):
        hp = pl.ds(hp0 + q, 1)
        h = h0_ref[:, q, :].T
        for t in range(T):
            h = h * dec_scr[t, hp, :] + bm_scr[t, grp] * xdt_scr[t, hp, :]
            y_scr[t, hp, :] = jnp.sum(h * cm_scr[t, grp], axis=0, keepdims=True)
        hout_ref[layer, :, q, :] = h.T

    @pl.when(j == pl.num_programs(0) - 1)
    def _():
        for t in range(T):
            y = y_scr[t].T + xs_scr[t] * dsk_ref[...]
            y = y * _silu(z_ref[t * B:(t + 1) * B, :])
            y_ref[t * B:(t + 1) * B, :] = _rms(y, ng_ref[...])


def _layer_state_specs(layer, block, axis):
    idx = lambda first: (lambda j: (first,) + tuple(j if a == axis else 0 for a in range(len(block))))
    cur = pl.BlockSpec((None,) + block, idx(layer))
    prev = [pl.BlockSpec((layer,) + block, idx(0))] if layer else []
    out = pl.BlockSpec((layer + 1,) + block, idx(0))
    return cur, prev, out


def _ssd_step(z, xbc, dtr, conv_all, h_all, h_done, lp, *, batch, seq, layer):
    n = batch * seq
    srows = SSD_HEADS * SSD_HEAD_DIM
    consts = (lp["conv_w"], lp["conv_b"], lp["dt_bias"], lp["a_neg_exp"], lp["d_skip"], lp["ssd_norm"], lp["head_expand"])
    hspec, prev_specs, hout_spec = _layer_state_specs(layer, (batch, SSD_STEP_TILES, SSD_STATE), 1)
    prev_args = [h_done] if layer else []
    cshape = (SSD_CONV - 1, batch, SSD_CONV_DIM)
    return pl.pallas_call(
        functools.partial(_ssd_step_body, seq=seq, batch=batch, layer=layer),
        grid=(srows // SSD_STEP_TILES,),
        in_specs=[_full_spec((n, GROUP_WIDTH)), _full_spec((n, SSD_CONV_DIM)), _full_spec((n, LANES)),
                  pl.BlockSpec((None,) + cshape, lambda j: (layer, 0, 0, 0)), hspec] + prev_specs
                 + [_pspec(a) for a in consts],
        out_specs=[_full_spec((n, GROUP_WIDTH)), _full_spec(cshape), hout_spec],
        out_shape=[jax.ShapeDtypeStruct((n, GROUP_WIDTH), F32),
                   jax.ShapeDtypeStruct(cshape, F32),
                   jax.ShapeDtypeStruct((layer + 1, batch, srows, SSD_STATE), F32)],
        scratch_shapes=[pltpu.VMEM((seq, batch, GROUP_WIDTH), F32),
                        pltpu.VMEM((seq, SSD_GROUPS, SSD_STATE, batch), F32),
                        pltpu.VMEM((seq, SSD_GROUPS, SSD_STATE, batch), F32),
                        pltpu.VMEM((seq, GROUP_WIDTH, batch), F32),
                        pltpu.VMEM((seq, GROUP_WIDTH, batch), F32),
                        pltpu.VMEM((seq, GROUP_WIDTH, batch), F32)],
        compiler_params=_cparams("arbitrary"),
        name="ssd_step",
    )(z, xbc, dtr, conv_all, h_all.reshape(h_all.shape[0], batch, srows, SSD_STATE),
      *prev_args, *[_parg(a) for a in consts])


PAIR = 2 * RWKV_HEAD
RWKV_PAIRS = RWKV_HEADS // 2


def _bd(x):
    half = x.shape[1] // 2
    lane = lax.broadcasted_iota(jnp.int32, x.shape, 1)
    zero = jnp.zeros_like(x)
    return jnp.concatenate([jnp.where(lane < half, x, zero), jnp.where(lane >= half, x, zero)], axis=0)


def _half_sums(x, lo):
    s_lo = jnp.sum(jnp.where(lo, x, 0.0), axis=-1, keepdims=True)
    s_hi = jnp.sum(jnp.where(lo, 0.0, x), axis=-1, keepdims=True)
    return jnp.where(lo, s_lo, s_hi)


def _head_sum(x):
    lo = lax.broadcasted_iota(jnp.int32, (x.shape[0], PAIR), 1) < RWKV_HEAD
    return jnp.concatenate([_half_sums(x[:, p * PAIR:(p + 1) * PAIR], lo) for p in range(RWKV_PAIRS)], axis=-1)


def _rwkv_pointwise(u, prev, mu_ref, w0_ref, w2_ref, a0_ref, a2_ref, g2_ref, kk_ref, ka_ref):
    GW = GROUP_WIDTH
    xs = u + (prev - u) * mu_ref[...]
    r = xs[:, 0:GW]
    k = xs[:, GW:2 * GW]
    v = xs[:, 2 * GW:3 * GW]
    wd = xs[:, 3 * GW:3 * GW + 64]
    ad = xs[:, 3 * GW + 64:3 * GW + 128]
    gd = xs[:, 3 * GW + 128:3 * GW + 256]
    w_lin = w0_ref[...] + _dot(jnp.tanh(wd).astype(BF16), w2_ref[...])
    logdecay = -math.exp(-0.5) * _sigmoid(w_lin)
    a = _sigmoid(a0_ref[...] + _dot(ad.astype(BF16), a2_ref[...]))
    g = _dot(_sigmoid(gd).astype(BF16), g2_ref[...])
    kk = k * kk_ref[...]
    kk = kk * lax.rsqrt(jnp.maximum(_head_sum(kk * kk), 1e-24))
    k = k * (1.0 + (a - 1.0) * ka_ref[...])
    return r, k, v, logdecay, a, g, kk


def _rwkv_finish(y, r, k, v, g, rk_ref, lng_ref, lnb_ref):
    mean = _head_sum(y) * (1.0 / RWKV_HEAD)
    yc = y - mean
    var = _head_sum(yc * yc) * (1.0 / RWKV_HEAD)
    y = yc * lax.rsqrt(var + RWKV_LN_EPS) * lng_ref[...] + lnb_ref[...]
    bonus = _head_sum(r * k * rk_ref[...]) * v
    return (y + bonus) * g


def _rwkv_body(u_ref, mu_ref, w0_ref, w2_ref, a0_ref, a2_ref, g2_ref, kk_ref, ka_ref, rk_ref,
               lng_ref, lnb_ref, y_ref, shift_ref, sout_ref, upad_scr, s_scr, *, chunk, group):
    L, G = chunk, group
    GL = G * L
    c = pl.program_id(1)
    pad = SUBLANES

    @pl.when(c == 0)
    def _():
        upad_scr[0:pad, :] = jnp.zeros((pad, RWKV_PROJ), F32)
        s_scr[...] = jnp.zeros(s_scr.shape, F32)

    u = u_ref[...]
    upad_scr[pad:pad + GL, :] = u
    prev = pltpu.roll(upad_scr[...], 1, axis=0)[pad:pad + GL]
    upad_scr[pad - 1:pad, :] = u[GL - 1:GL, :]
    r, k, v, logdecay, a, g, kk = _rwkv_pointwise(u, prev, mu_ref, w0_ref, w2_ref, a0_ref, a2_ref, g2_ref,
                                                  kk_ref, ka_ref)

    tril = jnp.where(lax.broadcasted_iota(jnp.int32, (L, L), 0) >= lax.broadcasted_iota(jnp.int32, (L, L), 1),
                     1.0, 0.0).astype(F32)
    cl = jnp.concatenate([_dot(tril, logdecay[i * L:(i + 1) * L, :], precision=HIGHEST) for i in range(G)], axis=0)
    e_in = jnp.exp(cl)
    e_inv = jnp.exp(-cl)
    r_t = r * e_in
    r_tb = r_t.astype(BF16)
    a_tb = (-kk * jnp.exp(cl - logdecay)).astype(BF16)
    b_tb = (kk * a * e_inv).astype(BF16)
    k_tb = (k * e_inv).astype(BF16)
    vb = v.astype(BF16)

    row = lax.broadcasted_iota(jnp.int32, (L, PAIR), 0)
    colh = lax.broadcasted_iota(jnp.int32, (L, PAIR), 1) & (RWKV_HEAD - 1)
    strict = row > colh
    incl = row >= colh
    eye_pair = jnp.where(row == colh, 1.0, 0.0).astype(F32)
    lane_lo = lax.broadcasted_iota(jnp.int32, (RWKV_HEAD, PAIR), 1) < RWKV_HEAD
    same_head = (lax.broadcasted_iota(jnp.int32, (PAIR, PAIR), 0) < RWKV_HEAD) == \
                (lax.broadcasted_iota(jnp.int32, (PAIR, PAIR), 1) < RWKV_HEAD)

    streams = [(i, p) for i in range(G) for p in range(RWKV_PAIRS)]
    ns = len(streams)
    blk = lambda x, i, p: x[i * L:(i + 1) * L, p * PAIR:(p + 1) * PAIR]
    lhs = [jnp.concatenate([blk(a_tb, i, p), blk(r_tb, i, p)], axis=0) for i, p in streams]
    m_both = [_dot_nt(lhs[s], jnp.concatenate([_bd(blk(b_tb, i, p)), _bd(blk(k_tb, i, p))], axis=0))
              for s, (i, p) in enumerate(streams)]
    m_ab = [m[:, 0:PAIR] for m in m_both]
    m_ak = [m[:, PAIR:2 * PAIR] for m in m_both]
    n_ab = [jnp.where(strict, m[0:L], 0.0) for m in m_ab]
    m_rb = [jnp.where(incl, m[L:2 * L], 0.0).astype(BF16) for m in m_ab]
    n_ak = [jnp.where(strict, m[0:L], 0.0).astype(BF16) for m in m_ak]
    m_rk = [jnp.where(incl, m[L:2 * L], 0.0).astype(BF16) for m in m_ak]
    tinv = [eye_pair + n for n in n_ab]
    pwb = [n.astype(BF16) for n in n_ab]
    pw = [_dot(x, _bd(x)) for x in pwb]
    for _ in range(int(math.log2(L)) - 2):
        pwb = [x.astype(BF16) for x in pw]
        both = [_dot(jnp.concatenate([pwb[s], tinv[s].astype(BF16)], axis=0), _bd(pwb[s])) for s in range(ns)]
        pw = [x[0:L] for x in both]
        tinv = [tinv[s] + both[s][L:2 * L] for s in range(ns)]
    pwb = [x.astype(BF16) for x in pw]
    tinv = [tinv[s] + _dot(tinv[s].astype(BF16), _bd(pwb[s])) for s in range(ns)]
    tinvb = [x.astype(BF16) for x in tinv]
    nv_mv = [_dot(jnp.concatenate([n_ak[s], m_rk[s]], axis=0), _bd(blk(vb, i, p))) for s, (i, p) in enumerate(streams)]
    wu = [_dot(tinvb[s], jnp.concatenate([_bd(blk(a_tb, i, p)), _bd(nv_mv[s][0:L].astype(BF16))], axis=1))
          for s, (i, p) in enumerate(streams)]
    wub = [x.astype(BF16) for x in wu]
    qy = [_dot(m_rb[s], jnp.concatenate([_bd(wub[s][:, 0:PAIR]), _bd(wub[s][:, PAIR:2 * PAIR])], axis=1))
          for s in range(ns)]
    q = [(blk(r_t, i, p) + qy[s][:, 0:PAIR]).astype(BF16) for s, (i, p) in enumerate(streams)]
    y_loc = [qy[s][:, PAIR:2 * PAIR] + nv_mv[s][L:2 * L] for s in range(ns)]
    zeros_b = jnp.zeros((L, PAIR), BF16)
    mg = [_dot_tn(jnp.concatenate([wub[s], jnp.concatenate([zeros_b, blk(vb, i, p)], axis=1)], axis=0),
                  jnp.concatenate([blk(b_tb, i, p), blk(k_tb, i, p)], axis=0))
          for s, (i, p) in enumerate(streams)]
    p_end = [e_in[(i + 1) * L - 1:(i + 1) * L, p * PAIR:(p + 1) * PAIR] for i, p in streams]
    m_t = [(jnp.where(same_head, mg[s][0:PAIR], 0.0) * p_end[s]).astype(BF16) for s in range(ns)]
    g_t = [jnp.where(lane_lo, mg[s][PAIR:PAIR + RWKV_HEAD], mg[s][PAIR + RWKV_HEAD:2 * PAIR]) * p_end[s]
           for s in range(ns)]

    y_rows = []
    for i in range(G):
        y_pairs = []
        for p in range(RWKV_PAIRS):
            s = i * RWKV_PAIRS + p
            s0 = s_scr[p]
            s0b = s0.astype(BF16)
            y_pairs.append(_dot_nt(q[s], _bd(s0b)) + y_loc[s])
            s_scr[p] = s0 * p_end[s] + _dot(s0b, m_t[s]) + g_t[s]
        y_rows.append(jnp.concatenate(y_pairs, axis=-1))
    y = jnp.concatenate(y_rows, axis=0)
    y_ref[...] = _rwkv_finish(y, r, k, v, g, rk_ref, lng_ref, lnb_ref)

    @pl.when(c == pl.num_programs(1) - 1)
    def _():
        sout_ref[0] = s_scr[...]
        shift_ref[0] = upad_scr[pad - 1:pad, :]


_RWKV_PARAM_NAMES = ("mu", "w0", "w2", "a0", "a2", "g2", "k_k", "k_a", "r_k", "ln_g", "ln_b")


def _rwkv(u, p, *, batch, seq):
    rows = RWKV_CHUNK * RWKV_GROUP
    nc = seq // rows
    params = [p[n] for n in _RWKV_PARAM_NAMES]
    sspec = pl.BlockSpec((1, RWKV_PAIRS, RWKV_HEAD, PAIR), lambda b, c: (b, 0, 0, 0))
    y, shift, s_last = pl.pallas_call(
        functools.partial(_rwkv_body, chunk=RWKV_CHUNK, group=RWKV_GROUP),
        grid=(batch, nc),
        in_specs=[pl.BlockSpec((rows, RWKV_PROJ), lambda b, c: (b * nc + c, 0))] + [_pspec(a) for a in params],
        out_specs=[pl.BlockSpec((rows, GROUP_WIDTH), lambda b, c: (b * nc + c, 0)),
                   pl.BlockSpec((1, 1, RWKV_PROJ), lambda b, c: (b, 0, 0)), sspec],
        out_shape=[jax.ShapeDtypeStruct((batch * seq, GROUP_WIDTH), F32),
                   jax.ShapeDtypeStruct((batch, 1, RWKV_PROJ), F32),
                   jax.ShapeDtypeStruct((batch, RWKV_PAIRS, RWKV_HEAD, PAIR), F32)],
        scratch_shapes=[pltpu.VMEM((SUBLANES + rows, RWKV_PROJ), F32),
                        pltpu.VMEM((RWKV_PAIRS, RWKV_HEAD, PAIR), F32)],
        compiler_params=_cparams("parallel", "arbitrary"),
        name="rwkv",
    )(u, *[_parg(a) for a in params])
    s_last = s_last.reshape(batch, RWKV_PAIRS, RWKV_HEAD, 2, RWKV_HEAD).transpose(0, 1, 3, 2, 4).reshape(
        batch, RWKV_HEADS, RWKV_HEAD, RWKV_HEAD)
    return y, shift.reshape(batch, RWKV_PROJ), s_last


def _rwkv_step_body(u_ref, shift0_ref, s0_ref, *rest, seq, batch, layer):
    sdone_ref, rest = (rest[0], rest[1:]) if layer else (None, rest)
    (mu_ref, w0_ref, w2_ref, a0_ref, a2_ref, g2_ref, kk_ref, ka_ref, rk_ref, lng_ref, lnb_ref, y_ref, sout_ref,
     r_scr, w_scr, k_scr, b_scr, nkk_scr, v_scr, y_scr) = rest
    T, B = seq, batch
    j = pl.program_id(0)
    if layer:
        sout_ref[0:layer] = sdone_ref[...]
    tiles = RWKV_STEP_TILES

    def pointwise(t):
        u = u_ref[t * B:(t + 1) * B, :]
        prev = shift0_ref[...] if t == 0 else u_ref[(t - 1) * B:t * B, :]
        return _rwkv_pointwise(u, prev, mu_ref, w0_ref, w2_ref, a0_ref, a2_ref, g2_ref, kk_ref, ka_ref)

    @pl.when(j == 0)
    def _():
        for t in range(T):
            r, k, v, logdecay, a, _, kk = pointwise(t)
            r_scr[t] = r.T
            w_scr[t] = jnp.exp(logdecay).T
            k_scr[t] = k.T
            b_scr[t] = (kk * a).T
            nkk_scr[t] = (-kk).T
            v_scr[t] = v.T

    i0 = j * tiles
    keys = pl.ds(pl.multiple_of((i0 // RWKV_HEAD) * RWKV_HEAD, RWKV_HEAD), RWKV_HEAD)
    for q in range(tiles):
        vi = pl.ds(i0 + q, 1)
        s = s0_ref[q]
        for t in range(T):
            sa = jnp.sum(s * nkk_scr[t, keys, :], axis=0, keepdims=True)
            s = s * w_scr[t, keys, :] + k_scr[t, keys, :] * v_scr[t, vi, :] + b_scr[t, keys, :] * sa
            y_scr[t, vi, :] = jnp.sum(s * r_scr[t, keys, :], axis=0, keepdims=True)
        sout_ref[layer, q] = s

    @pl.when(j == pl.num_programs(0) - 1)
    def _():
        for t in range(T):
            r, k, v, _, _, g, _ = pointwise(t)
            y_ref[t * B:(t + 1) * B, :] = _rwkv_finish(y_scr[t].T, r, k, v, g, rk_ref, lng_ref, lnb_ref)


def _rwkv_step(u, shift0, s_all, s_done, p, *, batch, seq, layer):
    n = batch * seq
    srows = RWKV_HEADS * RWKV_HEAD
    params = [p[nm] for nm in _RWKV_PARAM_NAMES]
    sspec, prev_specs, sout_spec = _layer_state_specs(layer, (RWKV_STEP_TILES, RWKV_HEAD, batch), 0)
    prev_args = [s_done] if layer else []
    tposed = pltpu.VMEM((seq, GROUP_WIDTH, batch), F32)
    return pl.pallas_call(
        functools.partial(_rwkv_step_body, seq=seq, batch=batch, layer=layer),
        grid=(srows // RWKV_STEP_TILES,),
        in_specs=[_full_spec((n, RWKV_PROJ)), _full_spec((batch, RWKV_PROJ)), sspec] + prev_specs
                 + [_pspec(a) for a in params],
        out_specs=[_full_spec((n, GROUP_WIDTH)), sout_spec],
        out_shape=[jax.ShapeDtypeStruct((n, GROUP_WIDTH), F32),
                   jax.ShapeDtypeStruct((layer + 1, srows, RWKV_HEAD, batch), F32)],
        scratch_shapes=[tposed] * 7,
        compiler_params=_cparams("arbitrary"),
        name="rwkv_step",
    )(u, shift0, s_all, *prev_args, *[_parg(a) for a in params])


def _s5_body(u_ref, hre0_ref, him0_ref, are_ref, aim_ref, bmat_ref, cmat_ref, d_ref, gw_ref, gb_ref,
             y_ref, hre_ref, him_ref, hs_scr, tm_scr, *, steps, batch_major):
    c = pl.program_id(1)
    ns = S5_WIDTH
    bsub = hre_ref.shape[0]

    @pl.when(c == 0)
    def _():
        hre_ref[...] = hre0_ref[...]
        him_ref[...] = him0_ref[...]

    if batch_major:
        for b in range(bsub):
            tm_scr[:, b, :] = u_ref[b]
        u = tm_scr[...].reshape(steps * bsub, GROUP_WIDTH)
    else:
        u = u_ref[...].reshape(steps * bsub, GROUP_WIDTH)
    are = jnp.broadcast_to(are_ref[...], (bsub, ns))
    aim = jnp.broadcast_to(aim_ref[...], (bsub, ns))
    hre, him = hre_ref[...], him_ref[...]
    sub = min(S5_SUB, steps)
    rows = sub * bsub
    outs = []
    hs_scr[...] = _dot(u.astype(BF16), bmat_ref[...])
    for k in range(steps // sub):
        r0 = k * rows
        u_k = u[r0:r0 + rows]
        for t in range(sub):
            rs = slice(r0 + t * bsub, r0 + (t + 1) * bsub)
            hre, him = (are * hre - aim * him + hs_scr[rs, 0:ns], are * him + aim * hre + hs_scr[rs, ns:2 * ns])
            hs_scr[rs, 0:ns] = hre
            hs_scr[rs, ns:2 * ns] = him
        y = _dot(hs_scr[r0:r0 + rows, :].astype(BF16), cmat_ref[...]) + u_k * d_ref[...]
        y = _gelu_tanh(y)
        yy = _dot(y.astype(BF16), gw_ref[...]) + gb_ref[...]
        outs.append(yy[:, 0:GROUP_WIDTH] * _sigmoid(yy[:, GROUP_WIDTH:2 * GROUP_WIDTH]))
    hre_ref[...] = hre
    him_ref[...] = him
    out = jnp.concatenate(outs, axis=0).reshape(steps, bsub, GROUP_WIDTH)
    if batch_major:
        tm_scr[...] = out
        for b in range(bsub):
            y_ref[b] = tm_scr[:, b, :]
    else:
        y_ref[...] = out


def _time_specs(u, batch_major, chunk):
    if batch_major:
        batch, seq, _ = u.shape
        steps = min(chunk, seq)
        bsub = SUBLANES
        spec = pl.BlockSpec((bsub, steps, GROUP_WIDTH), lambda b, c: (b, c, 0))
    else:
        seq, batch, _ = u.shape
        steps = min(chunk, seq)
        bsub = min(batch, SUBLANES * max(1, chunk // steps))
        spec = pl.BlockSpec((steps, bsub, GROUP_WIDTH), lambda b, c: (c, b, 0))
    return batch, seq, steps, bsub, spec


def _s5(u, hre0, him0, lp, *, batch_major):
    batch, seq, steps, bsub, tspec = _time_specs(u, batch_major, TM_CHUNK)
    hspec =pl.BlockSpec((bsub, S5_WIDTH), lambda b, c: (b, 0))
    consts = (lp["s5_are"], lp["s5_aim"], lp["s5_bmat"], lp["s5_cmat"], lp["s5_d"], lp["s5_gw"], lp["s5_gb"])
    return pl.pallas_call(
        functools.partial(_s5_body, steps=steps, batch_major=batch_major),
        grid=(batch // bsub, seq // steps),
        in_specs=[tspec, hspec, hspec] + [_pspec(a) for a in consts],
        out_specs=[tspec, hspec, hspec],
        out_shape=[jax.ShapeDtypeStruct(u.shape, F32),
                   jax.ShapeDtypeStruct((batch, S5_WIDTH), F32),
                   jax.ShapeDtypeStruct((batch, S5_WIDTH), F32)],
        scratch_shapes=[pltpu.VMEM((steps * bsub, 2 * S5_WIDTH), F32),
                        pltpu.VMEM((steps, bsub, GROUP_WIDTH), F32)],
        compiler_params=_cparams("parallel", "arbitrary"),
        name="s5",
    )(u, hre0, him0, *[_parg(a) for a in consts])


def _pool_body(u_ref, buf0_ref, pw_ref, sc_ref, y_ref, buf_ref, f_scr, tm_scr, *, steps, pos0, batch_major):
    c = pl.program_id(1)
    bsub = f_scr.shape[1]
    GW = GROUP_WIDTH
    halo = POOL_BUF + 1

    @pl.when(c == 0)
    def _():
        f_scr[0] = jnp.zeros((bsub, GW), F32)
        f_scr[1:halo] = buf0_ref[...]

    if batch_major:
        for b in range(bsub):
            f_scr[halo:halo + steps, b, :] = u_ref[b]
    else:
        f_scr[halo:halo + steps] = u_ref[...]
    f = f_scr[...]
    u = f[halo:halo + steps]
    s2 = f[1:] + f[:-1]
    s4 = s2[2:] + s2[:-2]
    s8 = s4[4:] + s4[:-4]
    s16 = s8[8:] + s8[:-8]
    f_scr[0:halo] = f[steps:steps + halo]
    lane = lax.broadcasted_iota(jnp.int32, (steps, bsub, GW), 2)
    tpos = lax.broadcasted_iota(jnp.int32, (steps, bsub, GW), 0) + (pos0 + 1) + c * steps
    win = jnp.where(lane < POOL_CH, s2[halo - 1:halo - 1 + steps],
                    jnp.where(lane < 2 * POOL_CH, s4[halo - 3:halo - 3 + steps],
                              jnp.where(lane < 3 * POOL_CH, s8[halo - 7:halo - 7 + steps],
                                        s16[halo - 15:halo - 15 + steps])))
    wlen = jnp.where(lane < POOL_CH, POOL_WINDOWS[0],
                     jnp.where(lane < 2 * POOL_CH, POOL_WINDOWS[1],
                               jnp.where(lane < 3 * POOL_CH, POOL_WINDOWS[2], POOL_WINDOWS[3])))
    cnt = jnp.minimum(tpos, wlen).astype(F32)
    pooled = (win / cnt - u).reshape(steps * bsub, GW)
    y = (_dot(pooled.astype(BF16), pw_ref[...]) * sc_ref[...]).reshape(steps, bsub, GW)
    if batch_major:
        tm_scr[...] = y
        for b in range(bsub):
            y_ref[b] = tm_scr[:, b, :]
    else:
        y_ref[...] = y

    @pl.when(c == pl.num_programs(1) - 1)
    def _():
        buf_ref[...] = f_scr[1:halo]


def _pool(u, buf0, lp, *, pos0, batch_major, layer=None):
    batch, seq, steps, bsub, tspec = _time_specs(u, batch_major, POOL_CHUNK)
    bblock =(POOL_BUF, bsub, GROUP_WIDTH)
    bspec = pl.BlockSpec(bblock, lambda b, c: (0, b, 0))
    if layer is None:
        bspec_in = bspec
    else:
        bspec_in = pl.BlockSpec((None,) + bblock, lambda b, c: (layer, 0, b, 0))
    return pl.pallas_call(
        functools.partial(_pool_body, steps=steps, pos0=pos0, batch_major=batch_major),
        grid=(batch // bsub, seq // steps),
        in_specs=[tspec, bspec_in, _pspec(lp["pool_w"]), _pspec(lp["pool_scale"])],
        out_specs=[tspec, bspec],
        out_shape=[jax.ShapeDtypeStruct(u.shape, F32), jax.ShapeDtypeStruct((POOL_BUF, batch, GROUP_WIDTH), F32)],
        scratch_shapes=[pltpu.VMEM((POOL_BUF + 1 + steps, bsub, GROUP_WIDTH), F32),
                        pltpu.VMEM((steps, bsub, GROUP_WIDTH), F32)],
        compiler_params=_cparams("parallel", "arbitrary"),
        name="pool",
    )(u, buf0, _parg(lp["pool_w"]), _parg(lp["pool_scale"]))


def _block_diag(blocks):
    n, g, r, c = blocks.shape
    eye = jnp.eye(g, dtype=blocks.dtype)
    return (eye[None, :, None, :, None] * blocks[:, :, :, None, :]).reshape(n, g * r, g * c)


def _stacked_params(P):
    row = lambda a: a.reshape(a.shape[0], 1, -1)
    pad_lanes = lambda a: jnp.pad(a, ((0, 0), (0, LANES - a.shape[1])))
    bf = lambda a: a.astype(BF16)

    lam = lax.complex(P["s5_lam_re"], P["s5_lam_im"])
    a_bar = jnp.exp(lam * jnp.exp(P["s5_log_step"])[..., None])
    b_bar = ((a_bar - 1.0) / lam)[..., None] * lax.complex(P["s5_b_re"], P["s5_b_im"])
    b_t = jnp.swapaxes(b_bar, 2, 3)
    bmat = jnp.concatenate([_block_diag(jnp.real(b_t)), _block_diag(jnp.imag(b_t))], axis=2)
    c_t = jnp.swapaxes(lax.complex(P["s5_c_re"], P["s5_c_im"]), 2, 3)
    cmat = jnp.concatenate([_block_diag(jnp.real(c_t)), -_block_diag(jnp.imag(c_t))], axis=1)

    out = dict(
        norm_ffn1=row(P["norm_ffn1"]), ffn1_in=P["ffn1_in"], ffn1_out=P["ffn1_out"],
        norm_mix=row(P["norm_mix"]),
        w_in=jnp.transpose(P["w_in"], (2, 0, 1)),
        conv_w=P["ssd_conv_w"], conv_b=row(P["ssd_conv_b"]),
        dt_bias=row(pad_lanes(P["ssd_dt_bias"])), a_log=row(pad_lanes(P["ssd_a_log"])),
        a_neg_exp=row(jnp.repeat(-jnp.exp(P["ssd_a_log"]), SSD_HEAD_DIM, axis=1)),
        d_skip=row(jnp.repeat(P["ssd_d"], SSD_HEAD_DIM, axis=1)), ssd_norm=row(P["ssd_norm"]),
        s5_are=row(jnp.real(a_bar)), s5_aim=row(jnp.imag(a_bar)), s5_bmat=bf(bmat), s5_cmat=bf(cmat),
        s5_d=row(P["s5_d"]), s5_gw=bf(P["s5_glu_w"]), s5_gb=row(P["s5_glu_b"]),
        pool_w=bf(_block_diag(P["pool_w"])), pool_scale=row(P["pool_scale"]),
        w_out=bf(P["w_out"]),
        norm_ffn2=row(P["norm_ffn2"]), ffn2_in=P["ffn2_in"], ffn2_out=P["ffn2_out"],
    )
    for name in _RWKV_PARAM_NAMES:
        a = P["rwkv_" + name]
        out["rwkv_" + name] = bf(a) if name in ("w2", "a2", "g2") else row(a)
    return out


def _layer_params(stacked, l):
    lp = {k: _Layered((v, l)) for k, v in stacked.items()}
    lp["rwkv"] = {n: lp["rwkv_" + n] for n in _RWKV_PARAM_NAMES}
    lp["head_expand"] = jnp.pad(jnp.repeat(jnp.eye(SSD_HEADS, dtype=F32), SSD_HEAD_DIM, axis=1),
                                ((0, LANES - SSD_HEADS), (0, 0)))
    return lp


def _mixers_prompt(lp, proj, *, batch, seq):
    z, xbc, ur, us5, upool, dtr = proj
    y_ssd, conv_new, ssd_new = _ssd(z, xbc, dtr, lp, batch=batch, seq=seq)
    y_rwkv, shift_new, rwkv_new = _rwkv(ur, lp["rwkv"], batch=batch, seq=seq)
    zeros = jnp.zeros((batch, S5_WIDTH), F32)
    bm = lambda a: a.reshape(batch, seq, a.shape[-1])
    rows = lambda a: a.reshape(batch * seq, a.shape[-1])
    y_s5, s5re, s5im = _s5(bm(us5), zeros, zeros, lp, batch_major=True)
    y_pool, pool_new = _pool(bm(upool), jnp.zeros((POOL_BUF, batch, GROUP_WIDTH), F32), lp, pos0=0,
                             batch_major=True)
    ys = (y_ssd, y_rwkv, rows(y_s5), rows(y_pool))
    states = (conv_new, ssd_new, shift_new, rwkv_new, s5re.reshape(batch, S5_GROUPS, S5_STATE),
              s5im.reshape(batch, S5_GROUPS, S5_STATE), jnp.swapaxes(pool_new, 0, 1))
    return ys, states


def _mixers_decode(lp, proj, states, done, *, batch, seq, layer):
    z, xbc, ur, us5, upool, dtr = proj
    shift0, s5re0, s5im0 = (states[i][layer] for i in (2, 4, 5))
    ssd_done, rwkv_done = (done[1], done[3]) if layer else (None, None)
    y_ssd, conv_new, ssd_new = _ssd_step(z, xbc, dtr, states[0], states[1], ssd_done, lp, batch=batch, seq=seq,
                                         layer=layer)
    y_rwkv, rwkv_new = _rwkv_step(ur, shift0, states[3], rwkv_done, lp["rwkv"], batch=batch, seq=seq, layer=layer)
    shift_new = ur[(seq - 1) * batch:, :]
    tm = lambda a: a.reshape(seq, batch, a.shape[-1])
    y_s5, s5re, s5im = _s5(tm(us5), s5re0.reshape(batch, S5_WIDTH), s5im0.reshape(batch, S5_WIDTH), lp,
                           batch_major=False)
    y_pool, pool_new = _pool(tm(upool), states[6], lp, pos0=PAST_LEN, batch_major=False, layer=layer)
    rows = lambda a: a.reshape(seq * batch, a.shape[-1])
    ys = (y_ssd, y_rwkv, rows(y_s5), rows(y_pool))
    new_states = (jnp.swapaxes(conv_new, 0, 1), ssd_new, shift_new, rwkv_new,
                  s5re.reshape(batch, S5_GROUPS, S5_STATE), s5im.reshape(batch, S5_GROUPS, S5_STATE),
                  jnp.swapaxes(pool_new, 0, 1))
    return ys, new_states


_WIDTHS = (GROUP_WIDTH, SSD_CONV_DIM, RWKV_PROJ, GROUP_WIDTH, GROUP_WIDTH, LANES)


def _trunk(x_p, x_s, layer_params, norm_final, mixers_p, mixers_s):
    st_p, st_s = [], []
    mix_p, mix_s, lp = None, None, None
    for l, lp_next in enumerate(layer_params):
        if l > 0:
            x_s, wg, wu, wo = _ffn_cast(x_s, lp["norm_ffn2"], lp["ffn2_in"], lp["ffn2_out"], mix=mix_s, wmix=lp["w_out"])
            x_p = _ffn(x_p, lp["norm_ffn2"], wg, wu, wo, mix=mix_p, wmix=lp["w_out"])
        lp = lp_next
        x_s, wg, wu, wo = _ffn_cast(x_s, lp["norm_ffn1"], lp["ffn1_in"], lp["ffn1_out"])
        x_p = _ffn(x_p, lp["norm_ffn1"], wg, wu, wo)
        proj_s, w_all = _inproj_cast(x_s, lp["norm_mix"], lp["w_in"], _WIDTHS)
        mix_p, st = mixers_p(l, lp, _inproj(x_p, lp["norm_mix"], w_all, _WIDTHS), st_p[-1] if st_p else None)
        st_p.append(st)
        mix_s, st = mixers_s(l, lp, proj_s, st_s[-1] if st_s else None)
        st_s.append(st)
    x_s, wg, wu, wo = _ffn_cast(x_s, lp["norm_ffn2"], lp["ffn2_in"], lp["ffn2_out"], mix=mix_s, wmix=lp["w_out"],
                                gf=norm_final)
    x_p = _ffn(x_p, lp["norm_ffn2"], wg, wu, wo, mix=mix_p, wmix=lp["w_out"], gf=norm_final)
    return (x_p, x_s), (st_p, st_s)


def kernel(x_prompt, x_sample, state_ssd_conv, state_ssd, state_rwkv_shift, state_rwkv, state_s5_re, state_s5_im, state_pool, norm_ffn1, ffn1_in, ffn1_out, norm_mix, w_in, ssd_conv_w, ssd_conv_b, ssd_dt_bias, ssd_a_log, ssd_d, ssd_norm, rwkv_mu, rwkv_w0, rwkv_w2, rwkv_a0, rwkv_a2, rwkv_g2, rwkv_k_k, rwkv_k_a, rwkv_r_k, rwkv_ln_g, rwkv_ln_b, s5_lam_re, s5_lam_im, s5_log_step, s5_b_re, s5_b_im, s5_c_re, s5_c_im, s5_d, s5_glu_w, s5_glu_b, pool_w, pool_scale, w_out, norm_ffn2, ffn2_in, ffn2_out, norm_final):
    P = dict(norm_ffn1=norm_ffn1, ffn1_in=ffn1_in, ffn1_out=ffn1_out, norm_mix=norm_mix, w_in=w_in,
             ssd_conv_w=ssd_conv_w, ssd_conv_b=ssd_conv_b, ssd_dt_bias=ssd_dt_bias, ssd_a_log=ssd_a_log,
             ssd_d=ssd_d, ssd_norm=ssd_norm, rwkv_mu=rwkv_mu, rwkv_w0=rwkv_w0, rwkv_w2=rwkv_w2, rwkv_a0=rwkv_a0,
             rwkv_a2=rwkv_a2, rwkv_g2=rwkv_g2, rwkv_k_k=rwkv_k_k, rwkv_k_a=rwkv_k_a,
             rwkv_r_k=rwkv_r_k.reshape(rwkv_r_k.shape[0], -1), rwkv_ln_g=rwkv_ln_g, rwkv_ln_b=rwkv_ln_b,
             s5_lam_re=s5_lam_re, s5_lam_im=s5_lam_im, s5_log_step=s5_log_step, s5_b_re=s5_b_re, s5_b_im=s5_b_im,
             s5_c_re=s5_c_re, s5_c_im=s5_c_im, s5_d=s5_d, s5_glu_w=s5_glu_w, s5_glu_b=s5_glu_b, pool_w=pool_w,
             pool_scale=pool_scale, w_out=w_out, norm_ffn2=norm_ffn2, ffn2_in=ffn2_in, ffn2_out=ffn2_out)
    depth = norm_ffn1.shape[0]
    bp, tp, d = x_prompt.shape
    bs, ts, _ = x_sample.shape
    stacked = _stacked_params(P)
    layer_params = [_layer_params(stacked, l) for l in range(depth)]
    gf = norm_final.reshape(1, -1)
    sample_states = (state_ssd_conv, state_ssd, state_rwkv_shift, state_rwkv, state_s5_re, state_s5_im, state_pool)
    rwkv_rows = RWKV_HEADS * RWKV_HEAD
    decode_states = (jnp.swapaxes(state_ssd_conv, 1, 2), state_ssd, state_rwkv_shift,
                     jnp.transpose(state_rwkv, (0, 2, 3, 4, 1)).reshape(depth, rwkv_rows, RWKV_HEAD, bs),
                     state_s5_re, state_s5_im, jnp.swapaxes(state_pool, 1, 2))

    x_s = jnp.swapaxes(x_sample, 0, 1).reshape(ts * bs, d)
    (y_p, y_s), (st_p, st_s) = _trunk(
        x_prompt.reshape(bp * tp, d), x_s, layer_params, gf,
        lambda l, lp, proj, done: _mixers_prompt(lp, proj, batch=bp, seq=tp),
        lambda l, lp, proj, done: _mixers_decode(lp, proj, decode_states, done, batch=bs, seq=ts, layer=l))
    outs = [y_p.reshape(bp, tp, d), jnp.swapaxes(y_s.reshape(ts, bs, d), 0, 1)]
    for i, ref_state in enumerate(sample_states):
        outs.append(jnp.stack([st[i] for st in st_p]))
        if i == 1:
            outs.append(st_s[-1][i].reshape(ref_state.shape))
        elif i == 3:
            s_new = st_s[-1][i].reshape(depth, RWKV_HEADS, RWKV_HEAD, RWKV_HEAD, bs)
            outs.append(jnp.transpose(s_new, (0, 4, 1, 2, 3)))
        else:
            outs.append(jnp.stack([st[i] for st in st_s]))
    return tuple(outs)
```

```python
import functools
import math

import jax
import jax.numpy as jnp
from jax import lax
from jax.experimental import pallas as pl
from jax.experimental.pallas import tpu as pltpu

F32 = jnp.float32
BF16 = jnp.bfloat16
HIGHEST = lax.Precision.HIGHEST

SUBLANES = 8
LANES = 128
VMEM_LIMIT_BYTES = 56 * 1024 * 1024

GROUP_WIDTH = 256
SSD_HEAD_DIM = 64
SSD_HEADS = 4
SSD_GROUPS = 2
SSD_STATE = 128
SSD_CONV = 4
SSD_CONV_DIM = GROUP_WIDTH + 2 * SSD_GROUPS * SSD_STATE
SSD_CHUNK = 128
SSD_GROUP = 4
LOG2_E = math.log2(math.e)
RWKV_HEAD = 64
RWKV_HEADS = 4
RWKV_PROJ = 1024
RWKV_LN_EPS = 64e-5
RWKV_CHUNK = 64
RWKV_GROUP = 8
S5_GROUPS = 16
S5_STATE = 64
S5_WIDTH = S5_GROUPS * S5_STATE
POOL_WINDOWS = (2, 4, 8, 16)
POOL_CH = 64
POOL_BUF = 15
RMS_EPS = 1e-6
PAST_LEN = 16384

ROW_TILE = 512
INPROJ_ROW_TILE = 1024
FFN_CHUNK = 256
TM_CHUNK = 128
POOL_CHUNK = 256
S5_SUB = 64
SSD_STEP_TILES = 16
RWKV_STEP_TILES = 16


def _cparams(*sem):
    return pltpu.CompilerParams(dimension_semantics=sem, vmem_limit_bytes=VMEM_LIMIT_BYTES)


def _dot(a, b, **kw):
    return jnp.dot(a, b, preferred_element_type=F32, **kw)


def _dot_nt(a, b):
    return lax.dot_general(a, b, (((1,), (1,)), ((), ())), preferred_element_type=F32)


def _dot_tn(a, b):
    return lax.dot_general(a, b, (((0,), (0,)), ((), ())), preferred_element_type=F32)


def _sigmoid(x):
    return 0.5 * jnp.tanh(0.5 * x) + 0.5


def _silu(x):
    h = 0.5 * x
    return h + h * jnp.tanh(h)


def _softplus(x):
    return jnp.maximum(x, 0.0) + jnp.log(1.0 + jnp.exp(-jnp.abs(x)))


def _gelu_tanh(x):
    c = math.sqrt(2.0 / math.pi)
    return x * (0.5 * (1.0 + jnp.tanh(c * (x + 0.044715 * (x * x * x)))))


def _rms(x, g):
    return x * lax.rsqrt(jnp.mean(x * x, axis=-1, keepdims=True) + RMS_EPS) * g


def _full_spec(shape):
    n = len(shape)
    return pl.BlockSpec(shape, lambda *_: (0,) * n)


class _Layered(tuple):
    pass


def _pspec(p, single=False):
    mode = pl.Buffered(1) if single else None
    if isinstance(p, _Layered):
        a, l = p
        return pl.BlockSpec((None,) + a.shape[1:], lambda *_: (l,) + (0,) * (a.ndim - 1), pipeline_mode=mode)
    n = p.ndim
    return pl.BlockSpec(p.shape, lambda *_: (0,) * n, pipeline_mode=mode)


def _parg(p):
    return p[0] if isinstance(p, _Layered) else p


def _mix_residual(x, y_refs, wmix_ref):
    for j, y_ref in enumerate(y_refs):
        x = x + _dot(y_ref[...].astype(BF16), wmix_ref[j * GROUP_WIDTH:(j + 1) * GROUP_WIDTH, :])
    return x


def _swiglu_chunk(h, wg, wu, wo):
    act = (_silu(_dot(h, wg)) * _dot(h, wu)).astype(BF16)
    return _dot(act, wo)


def _ffn_body(*refs, has_mix, final_norm):
    it = iter(refs)
    x = next(it)[...]
    if has_mix:
        y_refs = [next(it) for _ in range(4)]
        x = _mix_residual(x, y_refs, next(it))
    g_ref, wg_ref, wu_ref, wo_ref = next(it), next(it), next(it), next(it)
    gf_ref = next(it) if final_norm else None
    o_ref = next(it)
    h = _rms(x, g_ref[...]).astype(BF16)
    acc = jnp.zeros_like(x)
    for c in range(wo_ref.shape[0] // FFN_CHUNK):
        cols = slice(c * FFN_CHUNK, (c + 1) * FFN_CHUNK)
        acc = acc + _swiglu_chunk(h, wg_ref[:, cols], wu_ref[:, cols], wo_ref[cols, :])
    x = x + 0.5 * acc
    if final_norm:
        x = _rms(x, gf_ref[...])
    o_ref[...] = x


def _ffn(x, g, wg, wu, wo, mix=None, wmix=None, gf=None):
    rows, d = x.shape
    row_spec = lambda w: pl.BlockSpec((ROW_TILE, w), lambda i: (i, 0))
    args, specs = [x], [row_spec(d)]
    if mix is not None:
        for y in mix:
            args.append(y)
            specs.append(row_spec(y.shape[1]))
        args.append(_parg(wmix))
        specs.append(_pspec(wmix, single=True))
    for a in (g, wg, wu, wo) + ((gf,) if gf is not None else ()):
        args.append(_parg(a))
        specs.append(_pspec(a, single=True))
    return pl.pallas_call(
        functools.partial(_ffn_body, has_mix=mix is not None, final_norm=gf is not None),
        grid=(rows // ROW_TILE,),
        in_specs=specs,
        out_specs=row_spec(d),
        out_shape=jax.ShapeDtypeStruct((rows, d), F32),
        compiler_params=_cparams("parallel"),
        name="ffn",
    )(*args)


def _ffn_cast_body(*refs, has_mix, final_norm):
    it = iter(refs)
    x_ref = next(it)
    if has_mix:
        y_refs = [next(it) for _ in range(4)]
        wmix_ref = next(it)
    g_ref, wg_ref, wu_ref, wo_ref = next(it), next(it), next(it), next(it)
    gf_ref = next(it) if final_norm else None
    o_ref, wg_out, wu_out, wo_out, x_scr, h_scr, acc_scr = (next(it) for _ in range(7))
    c = pl.program_id(0)

    @pl.when(c == 0)
    def _():
        x = x_ref[...]
        if has_mix:
            x = _mix_residual(x, y_refs, wmix_ref)
        x_scr[...] = x
        h_scr[...] = _rms(x, g_ref[...]).astype(BF16)
        acc_scr[...] = jnp.zeros(acc_scr.shape, F32)

    wg = wg_ref[...].astype(BF16)
    wu = wu_ref[...].astype(BF16)
    wo = wo_ref[...].astype(BF16)
    wg_out[...] = wg
    wu_out[...] = wu
    wo_out[...] = wo
    acc_scr[...] += _swiglu_chunk(h_scr[...], wg, wu, wo)

    @pl.when(c == pl.num_programs(0) - 1)
    def _():
        x = x_scr[...] + 0.5 * acc_scr[...]
        if final_norm:
            x = _rms(x, gf_ref[...])
        o_ref[...] = x


def _ffn_cast(x, g, wi, wo, mix=None, wmix=None, gf=None):
    rows, d = x.shape
    wi_all, l = wi
    wo_all, _ = wo
    d_ff = wo_all.shape[1]
    nchunks = d_ff // FFN_CHUNK
    args, specs = [x], [_full_spec(x.shape)]
    if mix is not None:
        for y in mix:
            args.append(y)
            specs.append(_full_spec(y.shape))
        args.append(_parg(wmix))
        specs.append(_pspec(wmix, single=True))
    args += [_parg(g), wi_all, wi_all, wo_all]
    specs += [_pspec(g),
              pl.BlockSpec((None, d, FFN_CHUNK), lambda c: (l, 0, c)),
              pl.BlockSpec((None, d, FFN_CHUNK), lambda c: (l, 0, c + nchunks)),
              pl.BlockSpec((None, FFN_CHUNK, d), lambda c: (l, c, 0))]
    if gf is not None:
        args.append(gf)
        specs.append(_full_spec(gf.shape))
    col_spec = pl.BlockSpec((d, FFN_CHUNK), lambda c: (0, c))
    return pl.pallas_call(
        functools.partial(_ffn_cast_body, has_mix=mix is not None, final_norm=gf is not None),
        grid=(nchunks,),
        in_specs=specs,
        out_specs=[_full_spec(x.shape), col_spec, col_spec, pl.BlockSpec((FFN_CHUNK, d), lambda c: (c, 0))],
        out_shape=[jax.ShapeDtypeStruct((rows, d), F32), jax.ShapeDtypeStruct((d, d_ff), BF16),
                   jax.ShapeDtypeStruct((d, d_ff), BF16), jax.ShapeDtypeStruct((d_ff, d), BF16)],
        scratch_shapes=[pltpu.VMEM((rows, d), F32), pltpu.VMEM((rows, d), BF16), pltpu.VMEM((rows, d), F32)],
        compiler_params=_cparams("arbitrary"),
        name="ffn_cast",
    )(*args)


def _inproj_body(x_ref, g_ref, wt_ref, *o_refs):
    h = _rms(x_ref[...], g_ref[...]).astype(BF16)
    off = 0
    for o_ref in o_refs:
        n = o_ref.shape[-1]
        o_ref[...] = _dot_nt(h, wt_ref[off:off + n, :])
        off += n


def _inproj_cast_body(x_ref, g_ref, win_ref, *o_refs, layer):
    *proj_refs, wall_ref = o_refs
    split = GROUP_WIDTH + SSD_CONV_DIM
    wt = win_ref[:, layer, :]
    tail = wt.shape[0] - split - SSD_HEADS
    wall_ref[0:split, :] = wt[0:split].astype(BF16)
    wall_ref[split:split + tail, :] = wt[split + SSD_HEADS:].astype(BF16)
    dt_rows = jnp.concatenate([wt[split:split + SSD_HEADS], jnp.zeros((LANES - SSD_HEADS, wt.shape[1]), F32)], axis=0)
    wall_ref[split + tail:, :] = dt_rows.astype(BF16)
    _inproj_body(x_ref, g_ref, wall_ref, *proj_refs)


def _inproj_cast(x, g, w_in, widths):
    rows, d = x.shape
    wt_all, l = w_in
    outs = pl.pallas_call(
        functools.partial(_inproj_cast_body, layer=l),
        grid=(1,),
        in_specs=[_full_spec(x.shape), _pspec(g),
                  pl.BlockSpec(wt_all.shape, lambda i: (0, 0, 0), pipeline_mode=pl.Buffered(1))],
        out_specs=[_full_spec((rows, n)) for n in widths] + [_full_spec((sum(widths), d))],
        out_shape=[jax.ShapeDtypeStruct((rows, n), F32) for n in widths]
                  + [jax.ShapeDtypeStruct((sum(widths), d), BF16)],
        compiler_params=_cparams("arbitrary"),
        name="inproj_cast",
    )(x, _parg(g), wt_all)
    return outs[:-1], outs[-1]


def _inproj(x, g, w, widths):
    rows, d = x.shape
    tile = INPROJ_ROW_TILE
    row_spec = lambda w_: pl.BlockSpec((tile, w_), lambda i: (i, 0))
    return pl.pallas_call(
        _inproj_body,
        grid=(rows // tile,),
        in_specs=[row_spec(d), _pspec(g), _pspec(w, single=True)],
        out_specs=[row_spec(n) for n in widths],
        out_shape=[jax.ShapeDtypeStruct((rows, n), F32) for n in widths],
        compiler_params=_cparams("parallel"),
        name="inproj",
    )(x, _parg(g), _parg(w))


def _ssd_body(z_ref, xbc_ref, dt_ref, cw_ref, cb_ref, dtb_ref, alog_ref, dsk_ref, ng_ref,
              y_ref, conv_ref, hout_ref, xpad_scr, h_scr, *, chunk, group):
    L, G = chunk, group
    GL = G * L
    c = pl.program_id(1)
    pad = SUBLANES
    halo = SSD_CONV - 1
    hpg = SSD_HEADS // SSD_GROUPS
    assert hpg == 2 and hpg * SSD_HEAD_DIM == SSD_STATE

    @pl.when(c == 0)
    def _():
        xpad_scr[0:pad, :] = jnp.zeros((pad, SSD_CONV_DIM), F32)
        h_scr[...] = jnp.zeros(h_scr.shape, F32)

    xpad_scr[pad:pad + GL, :] = xbc_ref[...]
    xfull = xpad_scr[...]
    conv = cb_ref[...] + cw_ref[halo:halo + 1, :] * xfull[pad:pad + GL]
    for j in range(halo):
        conv = conv + cw_ref[j:j + 1, :] * pltpu.roll(xfull, halo - j, axis=0)[pad:pad + GL]
    xpad_scr[pad - halo:pad, :] = xpad_scr[pad + GL - halo:pad + GL, :]
    conv = _silu(conv)
    xs = conv[:, 0:GROUP_WIDTH]
    bm = conv[:, GROUP_WIDTH:2 * GROUP_WIDTH].astype(BF16)
    cm = conv[:, 2 * GROUP_WIDTH:3 * GROUP_WIDTH].astype(BF16)

    row = lax.broadcasted_iota(jnp.int32, (L, L), 0)
    col = lax.broadcasted_iota(jnp.int32, (L, L), 1)
    causal = row >= col
    tril = jnp.where(causal, 1.0, 0.0).astype(F32)
    dt = _softplus(dt_ref[...] + dtb_ref[...])
    da = dt * (-jnp.exp(alog_ref[...]) * LOG2_E)
    acs = [_dot(tril, da[i * L:(i + 1) * L, :], precision=HIGHEST) for i in range(G)]
    acs_t = [a.T for a in acs]
    e_acs = [jnp.exp2(a) for a in acs]
    e_end = [jnp.exp2(a[L - 1:L, :] - a) for a in acs]
    e_last = [jnp.exp2(a[L - 1:L, :]) for a in acs]

    keys = [(i, g) for i in range(G) for g in range(SSD_GROUPS)]
    rows_of = lambda x, i: x[i * L:(i + 1) * L]
    lanes_of = lambda x, g: x[:, g * SSD_STATE:(g + 1) * SSD_STATE]
    lane_lo = lax.broadcasted_iota(jnp.int32, (L, hpg * SSD_HEAD_DIM), 1) < SSD_HEAD_DIM
    row_lo = lax.broadcasted_iota(jnp.int32, (hpg * SSD_HEAD_DIM, SSD_STATE), 0) < SSD_HEAD_DIM
    head_cols = lambda a, g: jnp.where(lane_lo, a[:, g * hpg:g * hpg + 1], a[:, g * hpg + 1:g * hpg + 2])
    bg = {(i, g): lanes_of(rows_of(bm, i), g) for i, g in keys}
    cg = {(i, g): lanes_of(rows_of(cm, i), g) for i, g in keys}
    scores = {k: _dot_nt(cg[k], bg[k]) for k in keys}
    xdt = {(i, g): lanes_of(rows_of(xs, i), g) * head_cols(rows_of(dt, i), g) for i, g in keys}
    decay = {(i, h): jnp.exp2(jnp.where(causal, acs[i][:, h:h + 1] - acs_t[i][h:h + 1, :], -jnp.inf))
             for i in range(G) for h in range(SSD_HEADS)}
    p_mat = {(i, g): jnp.concatenate([(scores[(i, g)] * decay[(i, g * hpg + k)]).astype(BF16) for k in range(hpg)],
                                     axis=1) for i, g in keys}
    y_in = {k: _dot(p_mat[k], _bd(xdt[k].astype(BF16))) for k in keys}
    st = {(i, g): _dot_tn((xdt[(i, g)] * head_cols(e_end[i], g)).astype(BF16), bg[(i, g)]) for i, g in keys}

    y_rows = []
    for i in range(G):
        ys = []
        for g in range(SSD_GROUPS):
            h_prev = h_scr[g * hpg:(g + 1) * hpg].reshape(hpg * SSD_HEAD_DIM, SSD_STATE)
            ys.append(y_in[(i, g)] + _dot_nt(cg[(i, g)], h_prev.astype(BF16)) * head_cols(e_acs[i], g))
            keep = jnp.where(row_lo, e_last[i][:, g * hpg:g * hpg + 1], e_last[i][:, g * hpg + 1:g * hpg + 2])
            h_scr[g * hpg:(g + 1) * hpg] = (h_prev * keep + st[(i, g)]).reshape(hpg, SSD_HEAD_DIM, SSD_STATE)
        y_rows.append(jnp.concatenate(ys, axis=-1))
    y = jnp.concatenate(y_rows, axis=0) + xs * dsk_ref[...]
    y = y * _silu(z_ref[...])
    y_ref[...] = _rms(y, ng_ref[...])

    @pl.when(c == pl.num_programs(1) - 1)
    def _():
        hout_ref[0] = h_scr[...]
        conv_ref[0] = xpad_scr[pad - halo:pad, :]


def _ssd(z, xbc, dtr, lp, *, batch, seq):
    chunk = SSD_CHUNK
    rows = chunk * SSD_GROUP
    nc = seq // rows
    rspec = lambda w: pl.BlockSpec((rows, w), lambda b, c: (b * nc + c, 0))
    consts = (lp["conv_w"], lp["conv_b"], lp["dt_bias"], lp["a_log"], lp["d_skip"], lp["ssd_norm"])
    return pl.pallas_call(
        functools.partial(_ssd_body, chunk=chunk, group=SSD_GROUP),
        grid=(batch, nc),
        in_specs=[rspec(GROUP_WIDTH), rspec(SSD_CONV_DIM), rspec(LANES)] + [_pspec(a) for a in consts],
        out_specs=[rspec(GROUP_WIDTH),
                   pl.BlockSpec((1, SSD_CONV - 1, SSD_CONV_DIM), lambda b, c: (b, 0, 0)),
                   pl.BlockSpec((1, SSD_HEADS, SSD_HEAD_DIM, SSD_STATE), lambda b, c: (b, 0, 0, 0))],
        out_shape=[jax.ShapeDtypeStruct((batch * seq, GROUP_WIDTH), F32),
                   jax.ShapeDtypeStruct((batch, SSD_CONV - 1, SSD_CONV_DIM), F32),
                   jax.ShapeDtypeStruct((batch, SSD_HEADS, SSD_HEAD_DIM, SSD_STATE), F32)],
        scratch_shapes=[pltpu.VMEM((SUBLANES + rows, SSD_CONV_DIM), F32),
                        pltpu.VMEM((SSD_HEADS, SSD_HEAD_DIM, SSD_STATE), F32)],
        compiler_params=_cparams("parallel", "arbitrary"),
        name="ssd",
    )(z, xbc, dtr, *[_parg(a) for a in consts])


def _ssd_step_body(z_ref, xbc_ref, dt_ref, conv0_ref, h0_ref, *rest, seq, batch, layer):
    hdone_ref, rest = (rest[0], rest[1:]) if layer else (None, rest)
    (cw_ref, cb_ref, dtb_ref, aneg_ref, dsk_ref, ng_ref, hexp_ref, y_ref, conv_ref, hout_ref,
     xs_scr, bm_scr, cm_scr, xdt_scr, dec_scr, y_scr) = rest
    T, B = seq, batch
    if layer:
        hout_ref[0:layer] = hdone_ref[...]
    GW = GROUP_WIDTH
    j = pl.program_id(0)
    tiles = SSD_STEP_TILES

    @pl.when(j == 0)
    def _():
        rows = [conv0_ref[i] for i in range(SSD_CONV - 1)]
        rows += [xbc_ref[t * B:(t + 1) * B, :] for t in range(T)]
        for t in range(T):
            conv = cb_ref[...] + cw_ref[0:1, :] * rows[t]
            for i in range(1, SSD_CONV):
                conv = conv + cw_ref[i:i + 1, :] * rows[t + i]
            conv = _silu(conv)
            xs = conv[:, 0:GW]
            xs_scr[t] = xs
            for g in range(SSD_GROUPS):
                bm_scr[t, g] = conv[:, GW + g * SSD_STATE:GW + (g + 1) * SSD_STATE].T
                cm_scr[t, g] = conv[:, 2 * GW + g * SSD_STATE:2 * GW + (g + 1) * SSD_STATE].T
            dt = _softplus(dt_ref[t * B:(t + 1) * B, :] + dtb_ref[...])
            dte = _dot(dt, hexp_ref[...], precision=HIGHEST)
            xdt_scr[t] = (xs * dte).T
            dec_scr[t] = jnp.exp(dte * aneg_ref[...]).T
        for i in range(SSD_CONV - 1):
            conv_ref[i] = rows[T + i]

    hp0 = j * tiles
    grp = hp0 // (SSD_HEAD_DIM * (SSD_HEADS // SSD_GROUPS))
    for q in range(tiles):
        hp = pl.ds(hp0 + q, 1)
        h = h0_ref[:, q, :].T
        for t in range(T):
            h = h * dec_scr[t, hp, :] + bm_scr[t, grp] * xdt_scr[t, hp, :]
            y_scr[t, hp, :] = jnp.sum(h * cm_scr[t, grp], axis=0, keepdims=True)
        hout_ref[layer, :, q, :] = h.T

    @pl.when(j == pl.num_programs(0) - 1)
    def _():
        for t in range(T):
            y = y_scr[t].T + xs_scr[t] * dsk_ref[...]
            y = y * _silu(z_ref[t * B:(t + 1) * B, :])
            y_ref[t * B:(t + 1) * B, :] = _rms(y, ng_ref[...])


def _layer_state_specs(layer, block, axis):
    idx = lambda first: (lambda j: (first,) + tuple(j if a == axis else 0 for a in range(len(block))))
    cur = pl.BlockSpec((None,) + block, idx(layer))
    prev = [pl.BlockSpec((layer,) + block, idx(0))] if layer else []
    out = pl.BlockSpec((layer + 1,) + block, idx(0))
    return cur, prev, out


def _ssd_step(z, xbc, dtr, conv_all, h_all, h_done, lp, *, batch, seq, layer):
    n = batch * seq
    srows = SSD_HEADS * SSD_HEAD_DIM
    consts = (lp["conv_w"], lp["conv_b"], lp["dt_bias"], lp["a_neg_exp"], lp["d_skip"], lp["ssd_norm"], lp["head_expand"])
    hspec, prev_specs, hout_spec = _layer_state_specs(layer, (batch, SSD_STEP_TILES, SSD_STATE), 1)
    prev_args = [h_done] if layer else []
    cshape = (SSD_CONV - 1, batch, SSD_CONV_DIM)
    return pl.pallas_call(
        functools.partial(_ssd_step_body, seq=seq, batch=batch, layer=layer),
        grid=(srows // SSD_STEP_TILES,),
        in_specs=[_full_spec((n, GROUP_WIDTH)), _full_spec((n, SSD_CONV_DIM)), _full_spec((n, LANES)),
                  pl.BlockSpec((None,) + cshape, lambda j: (layer, 0, 0, 0)), hspec] + prev_specs
                 + [_pspec(a) for a in consts],
        out_specs=[_full_spec((n, GROUP_WIDTH)), _full_spec(cshape), hout_spec],
        out_shape=[jax.ShapeDtypeStruct((n, GROUP_WIDTH), F32),
                   jax.ShapeDtypeStruct(cshape, F32),
                   jax.ShapeDtypeStruct((layer + 1, batch, srows, SSD_STATE), F32)],
        scratch_shapes=[pltpu.VMEM((seq, batch, GROUP_WIDTH), F32),
                        pltpu.VMEM((seq, SSD_GROUPS, SSD_STATE, batch), F32),
                        pltpu.VMEM((seq, SSD_GROUPS, SSD_STATE, batch), F32),
                        pltpu.VMEM((seq, GROUP_WIDTH, batch), F32),
                        pltpu.VMEM((seq, GROUP_WIDTH, batch), F32),
                        pltpu.VMEM((seq, GROUP_WIDTH, batch), F32)],
        compiler_params=_cparams("arbitrary"),
        name="ssd_step",
    )(z, xbc, dtr, conv_all, h_all.reshape(h_all.shape[0], batch, srows, SSD_STATE),
      *prev_args, *[_parg(a) for a in consts])


PAIR = 2 * RWKV_HEAD
RWKV_PAIRS = RWKV_HEADS // 2


def _bd(x):
    half = x.shape[1] // 2
    lane = lax.broadcasted_iota(jnp.int32, x.shape, 1)
    zero = jnp.zeros_like(x)
    return jnp.concatenate([jnp.where(lane < half, x, zero), jnp.where(lane >= half, x, zero)], axis=0)


def _half_sums(x, lo):
    s_lo = jnp.sum(jnp.where(lo, x, 0.0), axis=-1, keepdims=True)
    s_hi = jnp.sum(jnp.where(lo, 0.0, x), axis=-1, keepdims=True)
    return jnp.where(lo, s_lo, s_hi)


def _head_sum(x):
    lo = lax.broadcasted_iota(jnp.int32, (x.shape[0], PAIR), 1) < RWKV_HEAD
    return jnp.concatenate([_half_sums(x[:, p * PAIR:(p + 1) * PAIR], lo) for p in range(RWKV_PAIRS)], axis=-1)


def _rwkv_pointwise(u, prev, mu_ref, w0_ref, w2_ref, a0_ref, a2_ref, g2_ref, kk_ref, ka_ref):
    GW = GROUP_WIDTH
    xs = u + (prev - u) * mu_ref[...]
    r = xs[:, 0:GW]
    k = xs[:, GW:2 * GW]
    v = xs[:, 2 * GW:3 * GW]
    wd = xs[:, 3 * GW:3 * GW + 64]
    ad = xs[:, 3 * GW + 64:3 * GW + 128]
    gd = xs[:, 3 * GW + 128:3 * GW + 256]
    w_lin = w0_ref[...] + _dot(jnp.tanh(wd).astype(BF16), w2_ref[...])
    logdecay = -math.exp(-0.5) * _sigmoid(w_lin)
    a = _sigmoid(a0_ref[...] + _dot(ad.astype(BF16), a2_ref[...]))
    g = _dot(_sigmoid(gd).astype(BF16), g2_ref[...])
    kk = k * kk_ref[...]
    kk = kk * lax.rsqrt(jnp.maximum(_head_sum(kk * kk), 1e-24))
    k = k * (1.0 + (a - 1.0) * ka_ref[...])
    return r, k, v, logdecay, a, g, kk


def _rwkv_finish(y, r, k, v, g, rk_ref, lng_ref, lnb_ref):
    mean = _head_sum(y) * (1.0 / RWKV_HEAD)
    yc = y - mean
    var = _head_sum(yc * yc) * (1.0 / RWKV_HEAD)
    y = yc * lax.rsqrt(var + RWKV_LN_EPS) * lng_ref[...] + lnb_ref[...]
    bonus = _head_sum(r * k * rk_ref[...]) * v
    return (y + bonus) * g


def _rwkv_body(u_ref, mu_ref, w0_ref, w2_ref, a0_ref, a2_ref, g2_ref, kk_ref, ka_ref, rk_ref,
               lng_ref, lnb_ref, y_ref, shift_ref, sout_ref, upad_scr, s_scr, *, chunk, group):
    L, G = chunk, group
    GL = G * L
    c = pl.program_id(1)
    pad = SUBLANES

    @pl.when(c == 0)
    def _():
        upad_scr[0:pad, :] = jnp.zeros((pad, RWKV_PROJ), F32)
        s_scr[...] = jnp.zeros(s_scr.shape, F32)

    u = u_ref[...]
    upad_scr[pad:pad + GL, :] = u
    prev = pltpu.roll(upad_scr[...], 1, axis=0)[pad:pad + GL]
    upad_scr[pad - 1:pad, :] = u[GL - 1:GL, :]
    r, k, v, logdecay, a, g, kk = _rwkv_pointwise(u, prev, mu_ref, w0_ref, w2_ref, a0_ref, a2_ref, g2_ref,
                                                  kk_ref, ka_ref)

    tril = jnp.where(lax.broadcasted_iota(jnp.int32, (L, L), 0) >= lax.broadcasted_iota(jnp.int32, (L, L), 1),
                     1.0, 0.0).astype(F32)
    cl = jnp.concatenate([_dot(tril, logdecay[i * L:(i + 1) * L, :], precision=HIGHEST) for i in range(G)], axis=0)
    e_in = jnp.exp(cl)
    e_inv = jnp.exp(-cl)
    r_t = r * e_in
    r_tb = r_t.astype(BF16)
    a_tb = (-kk * jnp.exp(cl - logdecay)).astype(BF16)
    b_tb = (kk * a * e_inv).astype(BF16)
    k_tb = (k * e_inv).astype(BF16)
    vb = v.astype(BF16)

    row = lax.broadcasted_iota(jnp.int32, (L, PAIR), 0)
    colh = lax.broadcasted_iota(jnp.int32, (L, PAIR), 1) & (RWKV_HEAD - 1)
    strict = row > colh
    incl = row >= colh
    eye_pair = jnp.where(row == colh, 1.0, 0.0).astype(F32)
    lane_lo = lax.broadcasted_iota(jnp.int32, (RWKV_HEAD, PAIR), 1) < RWKV_HEAD
    same_head = (lax.broadcasted_iota(jnp.int32, (PAIR, PAIR), 0) < RWKV_HEAD) == \
                (lax.broadcasted_iota(jnp.int32, (PAIR, PAIR), 1) < RWKV_HEAD)

    streams = [(i, p) for i in range(G) for p in range(RWKV_PAIRS)]
    ns = len(streams)
    blk = lambda x, i, p: x[i * L:(i + 1) * L, p * PAIR:(p + 1) * PAIR]
    lhs = [jnp.concatenate([blk(a_tb, i, p), blk(r_tb, i, p)], axis=0) for i, p in streams]
    m_both = [_dot_nt(lhs[s], jnp.concatenate([_bd(blk(b_tb, i, p)), _bd(blk(k_tb, i, p))], axis=0))
              for s, (i, p) in enumerate(streams)]
    m_ab = [m[:, 0:PAIR] for m in m_both]
    m_ak = [m[:, PAIR:2 * PAIR] for m in m_both]
    n_ab = [jnp.where(strict, m[0:L], 0.0) for m in m_ab]
    m_rb = [jnp.where(incl, m[L:2 * L], 0.0).astype(BF16) for m in m_ab]
    n_ak = [jnp.where(strict, m[0:L], 0.0).astype(BF16) for m in m_ak]
    m_rk = [jnp.where(incl, m[L:2 * L], 0.0).astype(BF16) for m in m_ak]
    tinv = [eye_pair + n for n in n_ab]
    pwb = [n.astype(BF16) for n in n_ab]
    pw = [_dot(x, _bd(x)) for x in pwb]
    for _ in range(int(math.log2(L)) - 2):
        pwb = [x.astype(BF16) for x in pw]
        both = [_dot(jnp.concatenate([pwb[s], tinv[s].astype(BF16)], axis=0), _bd(pwb[s])) for s in range(ns)]
        pw = [x[0:L] for x in both]
        tinv = [tinv[s] + both[s][L:2 * L] for s in range(ns)]
    pwb = [x.astype(BF16) for x in pw]
    tinv = [tinv[s] + _dot(tinv[s].astype(BF16), _bd(pwb[s])) for s in range(ns)]
    tinvb = [x.astype(BF16) for x in tinv]
    nv_mv = [_dot(jnp.concatenate([n_ak[s], m_rk[s]], axis=0), _bd(blk(vb, i, p))) for s, (i, p) in enumerate(streams)]
    wu = [_dot(tinvb[s], jnp.concatenate([_bd(blk(a_tb, i, p)), _bd(nv_mv[s][0:L].astype(BF16))], axis=1))
          for s, (i, p) in enumerate(streams)]
    wub = [x.astype(BF16) for x in wu]
    qy = [_dot(m_rb[s], jnp.concatenate([_bd(wub[s][:, 0:PAIR]), _bd(wub[s][:, PAIR:2 * PAIR])], axis=1))
          for s in range(ns)]
    q = [(blk(r_t, i, p) + qy[s][:, 0:PAIR]).astype(BF16) for s, (i, p) in enumerate(streams)]
    y_loc = [qy[s][:, PAIR:2 * PAIR] + nv_mv[s][L:2 * L] for s in range(ns)]
    zeros_b = jnp.zeros((L, PAIR), BF16)
    mg = [_dot_tn(jnp.concatenate([wub[s], jnp.concatenate([zeros_b, blk(vb, i, p)], axis=1)], axis=0),
                  jnp.concatenate([blk(b_tb, i, p), blk(k_tb, i, p)], axis=0))
          for s, (i, p) in enumerate(streams)]
    p_end = [e_in[(i + 1) * L - 1:(i + 1) * L, p * PAIR:(p + 1) * PAIR] for i, p in streams]
    m_t = [(jnp.where(same_head, mg[s][0:PAIR], 0.0) * p_end[s]).astype(BF16) for s in range(ns)]
    g_t = [jnp.where(lane_lo, mg[s][PAIR:PAIR + RWKV_HEAD], mg[s][PAIR + RWKV_HEAD:2 * PAIR]) * p_end[s]
           for s in range(ns)]

    y_rows = []
    for i in range(G):
        y_pairs = []
        for p in range(RWKV_PAIRS):
            s = i * RWKV_PAIRS + p
            s0 = s_scr[p]
            s0b = s0.astype(BF16)
            y_pairs.append(_dot_nt(q[s], _bd(s0b)) + y_loc[s])
            s_scr[p] = s0 * p_end[s] + _dot(s0b, m_t[s]) + g_t[s]
        y_rows.append(jnp.concatenate(y_pairs, axis=-1))
    y = jnp.concatenate(y_rows, axis=0)
    y_ref[...] = _rwkv_finish(y, r, k, v, g, rk_ref, lng_ref, lnb_ref)

    @pl.when(c == pl.num_programs(1) - 1)
    def _():
        sout_ref[0] = s_scr[...]
        shift_ref[0] = upad_scr[pad - 1:pad, :]


_RWKV_PARAM_NAMES = ("mu", "w0", "w2", "a0", "a2", "g2", "k_k", "k_a", "r_k", "ln_g", "ln_b")


def _rwkv(u, p, *, batch, seq):
    rows = RWKV_CHUNK * RWKV_GROUP
    nc = seq // rows
    params = [p[n] for n in _RWKV_PARAM_NAMES]
    sspec = pl.BlockSpec((1, RWKV_PAIRS, RWKV_HEAD, PAIR), lambda b, c: (b, 0, 0, 0))
    y, shift, s_last = pl.pallas_call(
        functools.partial(_rwkv_body, chunk=RWKV_CHUNK, group=RWKV_GROUP),
        grid=(batch, nc),
        in_specs=[pl.BlockSpec((rows, RWKV_PROJ), lambda b, c: (b * nc + c, 0))] + [_pspec(a) for a in params],
        out_specs=[pl.BlockSpec((rows, GROUP_WIDTH), lambda b, c: (b * nc + c, 0)),
                   pl.BlockSpec((1, 1, RWKV_PROJ), lambda b, c: (b, 0, 0)), sspec],
        out_shape=[jax.ShapeDtypeStruct((batch * seq, GROUP_WIDTH), F32),
                   jax.ShapeDtypeStruct((batch, 1, RWKV_PROJ), F32),
                   jax.ShapeDtypeStruct((batch, RWKV_PAIRS, RWKV_HEAD, PAIR), F32)],
        scratch_shapes=[pltpu.VMEM((SUBLANES + rows, RWKV_PROJ), F32),
                        pltpu.VMEM((RWKV_PAIRS, RWKV_HEAD, PAIR), F32)],
        compiler_params=_cparams("parallel", "arbitrary"),
        name="rwkv",
    )(u, *[_parg(a) for a in params])
    s_last = s_last.reshape(batch, RWKV_PAIRS, RWKV_HEAD, 2, RWKV_HEAD).transpose(0, 1, 3, 2, 4).reshape(
        batch, RWKV_HEADS, RWKV_HEAD, RWKV_HEAD)
    return y, shift.reshape(batch, RWKV_PROJ), s_last


def _rwkv_step_body(u_ref, shift0_ref, s0_ref, *rest, seq, batch, layer):
    sdone_ref, rest = (rest[0], rest[1:]) if layer else (None, rest)
    (mu_ref, w0_ref, w2_ref, a0_ref, a2_ref, g2_ref, kk_ref, ka_ref, rk_ref, lng_ref, lnb_ref, y_ref, sout_ref,
     r_scr, w_scr, k_scr, b_scr, nkk_scr, v_scr, y_scr) = rest
    T, B = seq, batch
    j = pl.program_id(0)
    if layer:
        sout_ref[0:layer] = sdone_ref[...]
    tiles = RWKV_STEP_TILES

    def pointwise(t):
        u = u_ref[t * B:(t + 1) * B, :]
        prev = shift0_ref[...] if t == 0 else u_ref[(t - 1) * B:t * B, :]
        return _rwkv_pointwise(u, prev, mu_ref, w0_ref, w2_ref, a0_ref, a2_ref, g2_ref, kk_ref, ka_ref)

    @pl.when(j == 0)
    def _():
        for t in range(T):
            r, k, v, logdecay, a, _, kk = pointwise(t)
            r_scr[t] = r.T
            w_scr[t] = jnp.exp(logdecay).T
            k_scr[t] = k.T
            b_scr[t] = (kk * a).T
            nkk_scr[t] = (-kk).T
            v_scr[t] = v.T

    i0 = j * tiles
    keys = pl.ds(pl.multiple_of((i0 // RWKV_HEAD) * RWKV_HEAD, RWKV_HEAD), RWKV_HEAD)
    for q in range(tiles):
        vi = pl.ds(i0 + q, 1)
        s = s0_ref[q]
        for t in range(T):
            sa = jnp.sum(s * nkk_scr[t, keys, :], axis=0, keepdims=True)
            s = s * w_scr[t, keys, :] + k_scr[t, keys, :] * v_scr[t, vi, :] + b_scr[t, keys, :] * sa
            y_scr[t, vi, :] = jnp.sum(s * r_scr[t, keys, :], axis=0, keepdims=True)
        sout_ref[layer, q] = s

    @pl.when(j == pl.num_programs(0) - 1)
    def _():
        for t in range(T):
            r, k, v, _, _, g, _ = pointwise(t)
            y_ref[t * B:(t + 1) * B, :] = _rwkv_finish(y_scr[t].T, r, k, v, g, rk_ref, lng_ref, lnb_ref)


def _rwkv_step(u, shift0, s_all, s_done, p, *, batch, seq, layer):
    n = batch * seq
    srows = RWKV_HEADS * RWKV_HEAD
    params = [p[nm] for nm in _RWKV_PARAM_NAMES]
    sspec, prev_specs, sout_spec = _layer_state_specs(layer, (RWKV_STEP_TILES, RWKV_HEAD, batch), 0)
    prev_args = [s_done] if layer else []
    tposed = pltpu.VMEM((seq, GROUP_WIDTH, batch), F32)
    return pl.pallas_call(
        functools.partial(_rwkv_step_body, seq=seq, batch=batch, layer=layer),
        grid=(srows // RWKV_STEP_TILES,),
        in_specs=[_full_spec((n, RWKV_PROJ)), _full_spec((batch, RWKV_PROJ)), sspec] + prev_specs
                 + [_pspec(a) for a in params],
        out_specs=[_full_spec((n, GROUP_WIDTH)), sout_spec],
        out_shape=[jax.ShapeDtypeStruct((n, GROUP_WIDTH), F32),
                   jax.ShapeDtypeStruct((layer + 1, srows, RWKV_HEAD, batch), F32)],
        scratch_shapes=[tposed] * 7,
        compiler_params=_cparams("arbitrary"),
        name="rwkv_step",
    )(u, shift0, s_all, *prev_args, *[_parg(a) for a in params])


def _s5_body(u_ref, hre0_ref, him0_ref, are_ref, aim_ref, bmat_ref, cmat_ref, d_ref, gw_ref, gb_ref,
             y_ref, hre_ref, him_ref, hs_scr, tm_scr, *, steps, batch_major):
    c = pl.program_id(1)
    ns = S5_WIDTH
    bsub = hre_ref.shape[0]

    @pl.when(c == 0)
    def _():
        hre_ref[...] = hre0_ref[...]
        him_ref[...] = him0_ref[...]

    if batch_major:
        for b in range(bsub):
            tm_scr[:, b, :] = u_ref[b]
        u = tm_scr[...].reshape(steps * bsub, GROUP_WIDTH)
    else:
        u = u_ref[...].reshape(steps * bsub, GROUP_WIDTH)
    are = jnp.broadcast_to(are_ref[...], (bsub, ns))
    aim = jnp.broadcast_to(aim_ref[...], (bsub, ns))
    hre, him = hre_ref[...], him_ref[...]
    sub = min(S5_SUB, steps)
    rows = sub * bsub
    outs = []
    hs_scr[...] = _dot(u.astype(BF16), bmat_ref[...])
    for k in range(steps // sub):
        r0 = k * rows
        u_k = u[r0:r0 + rows]
        for t in range(sub):
            rs = slice(r0 + t * bsub, r0 + (t + 1) * bsub)
            hre, him = (are * hre - aim * him + hs_scr[rs, 0:ns], are * him + aim * hre + hs_scr[rs, ns:2 * ns])
            hs_scr[rs, 0:ns] = hre
            hs_scr[rs, ns:2 * ns] = him
        y = _dot(hs_scr[r0:r0 + rows, :].astype(BF16), cmat_ref[...]) + u_k * d_ref[...]
        y = _gelu_tanh(y)
        yy = _dot(y.astype(BF16), gw_ref[...]) + gb_ref[...]
        outs.append(yy[:, 0:GROUP_WIDTH] * _sigmoid(yy[:, GROUP_WIDTH:2 * GROUP_WIDTH]))
    hre_ref[...] = hre
    him_ref[...] = him
    out = jnp.concatenate(outs, axis=0).reshape(steps, bsub, GROUP_WIDTH)
    if batch_major:
        tm_scr[...] = out
        for b in range(bsub):
            y_ref[b] = tm_scr[:, b, :]
    else:
        y_ref[...] = out


def _time_specs(u, batch_major, chunk):
    if batch_major:
        batch, seq, _ = u.shape
        steps = min(chunk, seq)
        bsub = SUBLANES
        spec = pl.BlockSpec((bsub, steps, GROUP_WIDTH), lambda b, c: (b, c, 0))
    else:
        seq, batch, _ = u.shape
        steps = min(chunk, seq)
        bsub = min(batch, SUBLANES * max(1, chunk // steps))
        spec = pl.BlockSpec((steps, bsub, GROUP_WIDTH), lambda b, c: (c, b, 0))
    return batch, seq, steps, bsub, spec


def _s5(u, hre0, him0, lp, *, batch_major):
    batch, seq, steps, bsub, tspec = _time_specs(u, batch_major, TM_CHUNK)
    hspec =pl.BlockSpec((bsub, S5_WIDTH), lambda b, c: (b, 0))
    consts = (lp["s5_are"], lp["s5_aim"], lp["s5_bmat"], lp["s5_cmat"], lp["s5_d"], lp["s5_gw"], lp["s5_gb"])
    return pl.pallas_call(
        functools.partial(_s5_body, steps=steps, batch_major=batch_major),
        grid=(batch // bsub, seq // steps),
        in_specs=[tspec, hspec, hspec] + [_pspec(a) for a in consts],
        out_specs=[tspec, hspec, hspec],
        out_shape=[jax.ShapeDtypeStruct(u.shape, F32),
                   jax.ShapeDtypeStruct((batch, S5_WIDTH), F32),
                   jax.ShapeDtypeStruct((batch, S5_WIDTH), F32)],
        scratch_shapes=[pltpu.VMEM((steps * bsub, 2 * S5_WIDTH), F32),
                        pltpu.VMEM((steps, bsub, GROUP_WIDTH), F32)],
        compiler_params=_cparams("parallel", "arbitrary"),
        name="s5",
    )(u, hre0, him0, *[_parg(a) for a in consts])


def _pool_body(u_ref, buf0_ref, pw_ref, sc_ref, y_ref, buf_ref, f_scr, tm_scr, *, steps, pos0, batch_major):
    c = pl.program_id(1)
    bsub = f_scr.shape[1]
    GW = GROUP_WIDTH
    halo = POOL_BUF + 1

    @pl.when(c == 0)
    def _():
        f_scr[0] = jnp.zeros((bsub, GW), F32)
        f_scr[1:halo] = buf0_ref[...]

    if batch_major:
        for b in range(bsub):
            f_scr[halo:halo + steps, b, :] = u_ref[b]
    else:
        f_scr[halo:halo + steps] = u_ref[...]
    f = f_scr[...]
    u = f[halo:halo + steps]
    s2 = f[1:] + f[:-1]
    s4 = s2[2:] + s2[:-2]
    s8 = s4[4:] + s4[:-4]
    s16 = s8[8:] + s8[:-8]
    f_scr[0:halo] = f[steps:steps + halo]
    lane = lax.broadcasted_iota(jnp.int32, (steps, bsub, GW), 2)
    tpos = lax.broadcasted_iota(jnp.int32, (steps, bsub, GW), 0) + (pos0 + 1) + c * steps
    win = jnp.where(lane < POOL_CH, s2[halo - 1:halo - 1 + steps],
                    jnp.where(lane < 2 * POOL_CH, s4[halo - 3:halo - 3 + steps],
                              jnp.where(lane < 3 * POOL_CH, s8[halo - 7:halo - 7 + steps],
                                        s16[halo - 15:halo - 15 + steps])))
    wlen = jnp.where(lane < POOL_CH, POOL_WINDOWS[0],
                     jnp.where(lane < 2 * POOL_CH, POOL_WINDOWS[1],
                               jnp.where(lane < 3 * POOL_CH, POOL_WINDOWS[2], POOL_WINDOWS[3])))
    cnt = jnp.minimum(tpos, wlen).astype(F32)
    pooled = (win / cnt - u).reshape(steps * bsub, GW)
    y = (_dot(pooled.astype(BF16), pw_ref[...]) * sc_ref[...]).reshape(steps, bsub, GW)
    if batch_major:
        tm_scr[...] = y
        for b in range(bsub):
            y_ref[b] = tm_scr[:, b, :]
    else:
        y_ref[...] = y

    @pl.when(c == pl.num_programs(1) - 1)
    def _():
        buf_ref[...] = f_scr[1:halo]


def _pool(u, buf0, lp, *, pos0, batch_major, layer=None):
    batch, seq, steps, bsub, tspec = _time_specs(u, batch_major, POOL_CHUNK)
    bblock =(POOL_BUF, bsub, GROUP_WIDTH)
    bspec = pl.BlockSpec(bblock, lambda b, c: (0, b, 0))
    if layer is None:
        bspec_in = bspec
    else:
        bspec_in = pl.BlockSpec((None,) + bblock, lambda b, c: (layer, 0, b, 0))
    return pl.pallas_call(
        functools.partial(_pool_body, steps=steps, pos0=pos0, batch_major=batch_major),
        grid=(batch // bsub, seq // steps),
        in_specs=[tspec, bspec_in, _pspec(lp["pool_w"]), _pspec(lp["pool_scale"])],
        out_specs=[tspec, bspec],
        out_shape=[jax.ShapeDtypeStruct(u.shape, F32), jax.ShapeDtypeStruct((POOL_BUF, batch, GROUP_WIDTH), F32)],
        scratch_shapes=[pltpu.VMEM((POOL_BUF + 1 + steps, bsub, GROUP_WIDTH), F32),
                        pltpu.VMEM((steps, bsub, GROUP_WIDTH), F32)],
        compiler_params=_cparams("parallel", "arbitrary"),
        name="pool",
    )(u, buf0, _parg(lp["pool_w"]), _parg(lp["pool_scale"]))


def _block_diag(blocks):
    n, g, r, c = blocks.shape
    eye = jnp.eye(g, dtype=blocks.dtype)
    return (eye[None, :, None, :, None] * blocks[:, :, :, None, :]).reshape(n, g * r, g * c)


def _stacked_params(P):
    row = lambda a: a.reshape(a.shape[0], 1, -1)
    pad_lanes = lambda a: jnp.pad(a, ((0, 0), (0, LANES - a.shape[1])))
    bf = lambda a: a.astype(BF16)

    lam = lax.complex(P["s5_lam_re"], P["s5_lam_im"])
    a_bar = jnp.exp(lam * jnp.exp(P["s5_log_step"])[..., None])
    b_bar = ((a_bar - 1.0) / lam)[..., None] * lax.complex(P["s5_b_re"], P["s5_b_im"])
    b_t = jnp.swapaxes(b_bar, 2, 3)
    bmat = jnp.concatenate([_block_diag(jnp.real(b_t)), _block_diag(jnp.imag(b_t))], axis=2)
    c_t = jnp.swapaxes(lax.complex(P["s5_c_re"], P["s5_c_im"]), 2, 3)
    cmat = jnp.concatenate([_block_diag(jnp.real(c_t)), -_block_diag(jnp.imag(c_t))], axis=1)

    out = dict(
        norm_ffn1=row(P["norm_ffn1"]), ffn1_in=P["ffn1_in"], ffn1_out=P["ffn1_out"],
        norm_mix=row(P["norm_mix"]),
        w_in=jnp.transpose(P["w_in"], (2, 0, 1)),
        conv_w=P["ssd_conv_w"], conv_b=row(P["ssd_conv_b"]),
        dt_bias=row(pad_lanes(P["ssd_dt_bias"])), a_log=row(pad_lanes(P["ssd_a_log"])),
        a_neg_exp=row(jnp.repeat(-jnp.exp(P["ssd_a_log"]), SSD_HEAD_DIM, axis=1)),
        d_skip=row(jnp.repeat(P["ssd_d"], SSD_HEAD_DIM, axis=1)), ssd_norm=row(P["ssd_norm"]),
        s5_are=row(jnp.real(a_bar)), s5_aim=row(jnp.imag(a_bar)), s5_bmat=bf(bmat), s5_cmat=bf(cmat),
        s5_d=row(P["s5_d"]), s5_gw=bf(P["s5_glu_w"]), s5_gb=row(P["s5_glu_b"]),
        pool_w=bf(_block_diag(P["pool_w"])), pool_scale=row(P["pool_scale"]),
        w_out=bf(P["w_out"]),
        norm_ffn2=row(P["norm_ffn2"]), ffn2_in=P["ffn2_in"], ffn2_out=P["ffn2_out"],
    )
    for name in _RWKV_PARAM_NAMES:
        a = P["rwkv_" + name]
        out["rwkv_" + name] = bf(a) if name in ("w2", "a2", "g2") else row(a)
    return out


def _layer_params(stacked, l):
    lp = {k: _Layered((v, l)) for k, v in stacked.items()}
    lp["rwkv"] = {n: lp["rwkv_" + n] for n in _RWKV_PARAM_NAMES}
    lp["head_expand"] = jnp.pad(jnp.repeat(jnp.eye(SSD_HEADS, dtype=F32), SSD_HEAD_DIM, axis=1),
                                ((0, LANES - SSD_HEADS), (0, 0)))
    return lp


def _mixers_prompt(lp, proj, *, batch, seq):
    z, xbc, ur, us5, upool, dtr = proj
    y_ssd, conv_new, ssd_new = _ssd(z, xbc, dtr, lp, batch=batch, seq=seq)
    y_rwkv, shift_new, rwkv_new = _rwkv(ur, lp["rwkv"], batch=batch, seq=seq)
    zeros = jnp.zeros((batch, S5_WIDTH), F32)
    bm = lambda a: a.reshape(batch, seq, a.shape[-1])
    rows = lambda a: a.reshape(batch * seq, a.shape[-1])
    y_s5, s5re, s5im = _s5(bm(us5), zeros, zeros, lp, batch_major=True)
    y_pool, pool_new = _pool(bm(upool), jnp.zeros((POOL_BUF, batch, GROUP_WIDTH), F32), lp, pos0=0,
                             batch_major=True)
    ys = (y_ssd, y_rwkv, rows(y_s5), rows(y_pool))
    states = (conv_new, ssd_new, shift_new, rwkv_new, s5re.reshape(batch, S5_GROUPS, S5_STATE),
              s5im.reshape(batch, S5_GROUPS, S5_STATE), jnp.swapaxes(pool_new, 0, 1))
    return ys, states


def _mixers_decode(lp, proj, states, done, *, batch, seq, layer):
    z, xbc, ur, us5, upool, dtr = proj
    shift0, s5re0, s5im0 = (states[i][layer] for i in (2, 4, 5))
    ssd_done, rwkv_done = (done[1], done[3]) if layer else (None, None)
    y_ssd, conv_new, ssd_new = _ssd_step(z, xbc, dtr, states[0], states[1], ssd_done, lp, batch=batch, seq=seq,
                                         layer=layer)
    y_rwkv, rwkv_new = _rwkv_step(ur, shift0, states[3], rwkv_done, lp["rwkv"], batch=batch, seq=seq, layer=layer)
    shift_new = ur[(seq - 1) * batch:, :]
    tm = lambda a: a.reshape(seq, batch, a.shape[-1])
    y_s5, s5re, s5im = _s5(tm(us5), s5re0.reshape(batch, S5_WIDTH), s5im0.reshape(batch, S5_WIDTH), lp,
                           batch_major=False)
    y_pool, pool_new = _pool(tm(upool), states[6], lp, pos0=PAST_LEN, batch_major=False, layer=layer)
    rows = lambda a: a.reshape(seq * batch, a.shape[-1])
    ys = (y_ssd, y_rwkv, rows(y_s5), rows(y_pool))
    new_states = (jnp.swapaxes(conv_new, 0, 1), ssd_new, shift_new, rwkv_new,
                  s5re.reshape(batch, S5_GROUPS, S5_STATE), s5im.reshape(batch, S5_GROUPS, S5_STATE),
                  jnp.swapaxes(pool_new, 0, 1))
    return ys, new_states


_WIDTHS = (GROUP_WIDTH, SSD_CONV_DIM, RWKV_PROJ, GROUP_WIDTH, GROUP_WIDTH, LANES)


def _trunk(x_p, x_s, layer_params, norm_final, mixers_p, mixers_s):
    st_p, st_s = [], []
    mix_p, mix_s, lp = None, None, None
    for l, lp_next in enumerate(layer_params):
        if l > 0:
            x_s, wg, wu, wo = _ffn_cast(x_s, lp["norm_ffn2"], lp["ffn2_in"], lp["ffn2_out"], mix=mix_s, wmix=lp["w_out"])
            x_p = _ffn(x_p, lp["norm_ffn2"], wg, wu, wo, mix=mix_p, wmix=lp["w_out"])
        lp = lp_next
        x_s, wg, wu, wo = _ffn_cast(x_s, lp["norm_ffn1"], lp["ffn1_in"], lp["ffn1_out"])
        x_p = _ffn(x_p, lp["norm_ffn1"], wg, wu, wo)
        proj_s, w_all = _inproj_cast(x_s, lp["norm_mix"], lp["w_in"], _WIDTHS)
        mix_p, st = mixers_p(l, lp, _inproj(x_p, lp["norm_mix"], w_all, _WIDTHS), st_p[-1] if st_p else None)
        st_p.append(st)
        mix_s, st = mixers_s(l, lp, proj_s, st_s[-1] if st_s else None)
        st_s.append(st)
    x_s, wg, wu, wo = _ffn_cast(x_s, lp["norm_ffn2"], lp["ffn2_in"], lp["ffn2_out"], mix=mix_s, wmix=lp["w_out"],
                                gf=norm_final)
    x_p = _ffn(x_p, lp["norm_ffn2"], wg, wu, wo, mix=mix_p, wmix=lp["w_out"], gf=norm_final)
    return (x_p, x_s), (st_p, st_s)


def kernel(x_prompt, x_sample, state_ssd_conv, state_ssd, state_rwkv_shift, state_rwkv, state_s5_re, state_s5_im, state_pool, norm_ffn1, ffn1_in, ffn1_out, norm_mix, w_in, ssd_conv_w, ssd_conv_b, ssd_dt_bias, ssd_a_log, ssd_d, ssd_norm, rwkv_mu, rwkv_w0, rwkv_w2, rwkv_a0, rwkv_a2, rwkv_g2, rwkv_k_k, rwkv_k_a, rwkv_r_k, rwkv_ln_g, rwkv_ln_b, s5_lam_re, s5_lam_im, s5_log_step, s5_b_re, s5_b_im, s5_c_re, s5_c_im, s5_d, s5_glu_w, s5_glu_b, pool_w, pool_scale, w_out, norm_ffn2, ffn2_in, ffn2_out, norm_final):
    P = dict(norm_ffn1=norm_ffn1, ffn1_in=ffn1_in, ffn1_out=ffn1_out, norm_mix=norm_mix, w_in=w_in,
             ssd_conv_w=ssd_conv_w, ssd_conv_b=ssd_conv_b, ssd_dt_bias=ssd_dt_bias, ssd_a_log=ssd_a_log,
             ssd_d=ssd_d, ssd_norm=ssd_norm, rwkv_mu=rwkv_mu, rwkv_w0=rwkv_w0, rwkv_w2=rwkv_w2, rwkv_a0=rwkv_a0,
             rwkv_a2=rwkv_a2, rwkv_g2=rwkv_g2, rwkv_k_k=rwkv_k_k, rwkv_k_a=rwkv_k_a,
             rwkv_r_k=rwkv_r_k.reshape(rwkv_r_k.shape[0], -1), rwkv_ln_g=rwkv_ln_g, rwkv_ln_b=rwkv_ln_b,
             s5_lam_re=s5_lam_re, s5_lam_im=s5_lam_im, s5_log_step=s5_log_step, s5_b_re=s5_b_re, s5_b_im=s5_b_im,
             s5_c_re=s5_c_re, s5_c_im=s5_c_im, s5_d=s5_d, s5_glu_w=s5_glu_w, s5_glu_b=s5_glu_b, pool_w=pool_w,
             pool_scale=pool_scale, w_out=w_out, norm_ffn2=norm_ffn2, ffn2_in=ffn2_in, ffn2_out=ffn2_out)
    depth = norm_ffn1.shape[0]
    bp, tp, d = x_prompt.shape
    bs, ts, _ = x_sample.shape
    stacked = _stacked_params(P)
    layer_params = [_layer_params(stacked, l) for l in range(depth)]
    gf = norm_final.reshape(1, -1)
    sample_states = (state_ssd_conv, state_ssd, state_rwkv_shift, state_rwkv, state_s5_re, state_s5_im, state_pool)
    rwkv_rows = RWKV_HEADS * RWKV_HEAD
    decode_states = (jnp.swapaxes(state_ssd_conv, 1, 2), state_ssd, state_rwkv_shift,
                     jnp.transpose(state_rwkv, (0, 2, 3, 4, 1)).reshape(depth, rwkv_rows, RWKV_HEAD, bs),
                     state_s5_re, state_s5_im, jnp.swapaxes(state_pool, 1, 2))

    x_s = jnp.swapaxes(x_sample, 0, 1).reshape(ts * bs, d)
    (y_p, y_s), (st_p, st_s) = _trunk(
        x_prompt.reshape(bp * tp, d), x_s, layer_params, gf,
        lambda l, lp, proj, done: _mixers_prompt(lp, proj, batch=bp, seq=tp),
        lambda l, lp, proj, done: _mixers_decode(lp, proj, decode_states, done, batch=bs, seq=ts, layer=l))
    outs = [y_p.reshape(bp, tp, d), jnp.swapaxes(y_s.reshape(ts, bs, d), 0, 1)]
    for i, ref_state in enumerate(sample_states):
        outs.append(jnp.stack([st[i] for st in st_p]))
        if i == 1:
            outs.append(st_s[-1][i].reshape(ref_state.shape))
        elif i == 3:
            s_new = st_s[-1][i].reshape(depth, RWKV_HEADS, RWKV_HEAD, RWKV_HEAD, bs)
            outs.append(jnp.transpose(s_new, (0, 4, 1, 2, 3)))
        else:
            outs.append(jnp.stack([st[i] for st in st_s]))
    return tuple(outs)
```

```python
import functools
import math

import jax
import jax.numpy as jnp
from jax import lax
from jax.experimental import pallas as pl
from jax.experimental.pallas import tpu as pltpu

F32 = jnp.float32
BF16 = jnp.bfloat16
HIGHEST = lax.Precision.HIGHEST

SUBLANES = 8
LANES = 128
VMEM_LIMIT_BYTES = 56 * 1024 * 1024

GROUP_WIDTH = 256
SSD_HEAD_DIM = 64
SSD_HEADS = 4
SSD_GROUPS = 2
SSD_STATE = 128
SSD_CONV = 4
SSD_CONV_DIM = GROUP_WIDTH + 2 * SSD_GROUPS * SSD_STATE
SSD_CHUNK = 128
SSD_GROUP = 4
LOG2_E = math.log2(math.e)
RWKV_HEAD = 64
RWKV_HEADS = 4
RWKV_PROJ = 1024
RWKV_LN_EPS = 64e-5
RWKV_CHUNK = 64
RWKV_GROUP = 16
S5_GROUPS = 16
S5_STATE = 64
S5_WIDTH = S5_GROUPS * S5_STATE
POOL_WINDOWS = (2, 4, 8, 16)
POOL_CH = 64
POOL_BUF = 15
RMS_EPS = 1e-6
PAST_LEN = 16384

ROW_TILE = 512
INPROJ_ROW_TILE = 1024
FFN_CHUNK = 256
TM_CHUNK = 128
POOL_CHUNK = 256
S5_SUB = 64
SSD_STEP_TILES = 16
RWKV_STEP_TILES = 16


def _cparams(*sem):
    return pltpu.CompilerParams(dimension_semantics=sem, vmem_limit_bytes=VMEM_LIMIT_BYTES)


def _dot(a, b, **kw):
    return jnp.dot(a, b, preferred_element_type=F32, **kw)


def _dot_nt(a, b):
    return lax.dot_general(a, b, (((1,), (1,)), ((), ())), preferred_element_type=F32)


def _dot_tn(a, b):
    return lax.dot_general(a, b, (((0,), (0,)), ((), ())), preferred_element_type=F32)


def _sigmoid(x):
    return 0.5 * jnp.tanh(0.5 * x) + 0.5


def _silu(x):
    h = 0.5 * x
    return h + h * jnp.tanh(h)


def _softplus(x):
    return jnp.maximum(x, 0.0) + jnp.log(1.0 + jnp.exp(-jnp.abs(x)))


def _gelu_tanh(x):
    c = math.sqrt(2.0 / math.pi)
    return x * (0.5 * (1.0 + jnp.tanh(c * (x + 0.044715 * (x * x * x)))))


def _rms(x, g):
    return x * lax.rsqrt(jnp.mean(x * x, axis=-1, keepdims=True) + RMS_EPS) * g


def _full_spec(shape):
    n = len(shape)
    return pl.BlockSpec(shape, lambda *_: (0,) * n)


class _Layered(tuple):
    pass


def _pspec(p, single=False):
    mode = pl.Buffered(1) if single else None
    if isinstance(p, _Layered):
        a, l = p
        return pl.BlockSpec((None,) + a.shape[1:], lambda *_: (l,) + (0,) * (a.ndim - 1), pipeline_mode=mode)
    n = p.ndim
    return pl.BlockSpec(p.shape, lambda *_: (0,) * n, pipeline_mode=mode)


def _parg(p):
    return p[0] if isinstance(p, _Layered) else p


def _mix_residual(x, y_refs, wmix_ref):
    for j, y_ref in enumerate(y_refs):
        x = x + _dot(y_ref[...].astype(BF16), wmix_ref[j * GROUP_WIDTH:(j + 1) * GROUP_WIDTH, :])
    return x


def _swiglu_chunk(h, wg, wu, wo):
    act = (_silu(_dot(h, wg)) * _dot(h, wu)).astype(BF16)
    return _dot(act, wo)


def _ffn_body(*refs, has_mix, final_norm):
    it = iter(refs)
    x = next(it)[...]
    if has_mix:
        y_refs = [next(it) for _ in range(4)]
        x = _mix_residual(x, y_refs, next(it))
    g_ref, wg_ref, wu_ref, wo_ref = next(it), next(it), next(it), next(it)
    gf_ref = next(it) if final_norm else None
    o_ref = next(it)
    h = _rms(x, g_ref[...]).astype(BF16)
    acc = jnp.zeros_like(x)
    for c in range(wo_ref.shape[0] // FFN_CHUNK):
        cols = slice(c * FFN_CHUNK, (c + 1) * FFN_CHUNK)
        acc = acc + _swiglu_chunk(h, wg_ref[:, cols], wu_ref[:, cols], wo_ref[cols, :])
    x = x + 0.5 * acc
    if final_norm:
        x = _rms(x, gf_ref[...])
    o_ref[...] = x


def _ffn(x, g, wg, wu, wo, mix=None, wmix=None, gf=None):
    rows, d = x.shape
    row_spec = lambda w: pl.BlockSpec((ROW_TILE, w), lambda i: (i, 0))
    args, specs = [x], [row_spec(d)]
    if mix is not None:
        for y in mix:
            args.append(y)
            specs.append(row_spec(y.shape[1]))
        args.append(_parg(wmix))
        specs.append(_pspec(wmix, single=True))
    for a in (g, wg, wu, wo) + ((gf,) if gf is not None else ()):
        args.append(_parg(a))
        specs.append(_pspec(a, single=True))
    return pl.pallas_call(
        functools.partial(_ffn_body, has_mix=mix is not None, final_norm=gf is not None),
        grid=(rows // ROW_TILE,),
        in_specs=specs,
        out_specs=row_spec(d),
        out_shape=jax.ShapeDtypeStruct((rows, d), F32),
        compiler_params=_cparams("parallel"),
        name="ffn",
    )(*args)


def _ffn_cast_body(*refs, has_mix, final_norm):
    it = iter(refs)
    x_ref = next(it)
    if has_mix:
        y_refs = [next(it) for _ in range(4)]
        wmix_ref = next(it)
    g_ref, wg_ref, wu_ref, wo_ref = next(it), next(it), next(it), next(it)
    gf_ref = next(it) if final_norm else None
    o_ref, wg_out, wu_out, wo_out, x_scr, h_scr, acc_scr = (next(it) for _ in range(7))
    c = pl.program_id(0)

    @pl.when(c == 0)
    def _():
        x = x_ref[...]
        if has_mix:
            x = _mix_residual(x, y_refs, wmix_ref)
        x_scr[...] = x
        h_scr[...] = _rms(x, g_ref[...]).astype(BF16)
        acc_scr[...] = jnp.zeros(acc_scr.shape, F32)

    wg = wg_ref[...].astype(BF16)
    wu = wu_ref[...].astype(BF16)
    wo = wo_ref[...].astype(BF16)
    wg_out[...] = wg
    wu_out[...] = wu
    wo_out[...] = wo
    acc_scr[...] += _swiglu_chunk(h_scr[...], wg, wu, wo)

    @pl.when(c == pl.num_programs(0) - 1)
    def _():
        x = x_scr[...] + 0.5 * acc_scr[...]
        if final_norm:
            x = _rms(x, gf_ref[...])
        o_ref[...] = x


def _ffn_cast(x, g, wi, wo, mix=None, wmix=None, gf=None):
    rows, d = x.shape
    wi_all, l = wi
    wo_all, _ = wo
    d_ff = wo_all.shape[1]
    nchunks = d_ff // FFN_CHUNK
    args, specs = [x], [_full_spec(x.shape)]
    if mix is not None:
        for y in mix:
            args.append(y)
            specs.append(_full_spec(y.shape))
        args.append(_parg(wmix))
        specs.append(_pspec(wmix, single=True))
    args += [_parg(g), wi_all, wi_all, wo_all]
    specs += [_pspec(g),
              pl.BlockSpec((None, d, FFN_CHUNK), lambda c: (l, 0, c)),
              pl.BlockSpec((None, d, FFN_CHUNK), lambda c: (l, 0, c + nchunks)),
              pl.BlockSpec((None, FFN_CHUNK, d), lambda c: (l, c, 0))]
    if gf is not None:
        args.append(gf)
        specs.append(_full_spec(gf.shape))
    col_spec = pl.BlockSpec((d, FFN_CHUNK), lambda c: (0, c))
    return pl.pallas_call(
        functools.partial(_ffn_cast_body, has_mix=mix is not None, final_norm=gf is not None),
        grid=(nchunks,),
        in_specs=specs,
        out_specs=[_full_spec(x.shape), col_spec, col_spec, pl.BlockSpec((FFN_CHUNK, d), lambda c: (c, 0))],
        out_shape=[jax.ShapeDtypeStruct((rows, d), F32), jax.ShapeDtypeStruct((d, d_ff), BF16),
                   jax.ShapeDtypeStruct((d, d_ff), BF16), jax.ShapeDtypeStruct((d_ff, d), BF16)],
        scratch_shapes=[pltpu.VMEM((rows, d), F32), pltpu.VMEM((rows, d), BF16), pltpu.VMEM((rows, d), F32)],
        compiler_params=_cparams("arbitrary"),
        name="ffn_cast",
    )(*args)


def _inproj_body(x_ref, g_ref, wt_ref, *o_refs):
    h = _rms(x_ref[...], g_ref[...]).astype(BF16)
    off = 0
    for o_ref in o_refs:
        n = o_ref.shape[-1]
        o_ref[...] = _dot_nt(h, wt_ref[off:off + n, :])
        off += n


def _inproj_cast_body(x_ref, g_ref, win_ref, *o_refs, layer):
    *proj_refs, wall_ref = o_refs
    split = GROUP_WIDTH + SSD_CONV_DIM
    wt = win_ref[:, layer, :]
    tail = wt.shape[0] - split - SSD_HEADS
    wall_ref[0:split, :] = wt[0:split].astype(BF16)
    wall_ref[split:split + tail, :] = wt[split + SSD_HEADS:].astype(BF16)
    dt_rows = jnp.concatenate([wt[split:split + SSD_HEADS], jnp.zeros((LANES - SSD_HEADS, wt.shape[1]), F32)], axis=0)
    wall_ref[split + tail:, :] = dt_rows.astype(BF16)
    _inproj_body(x_ref, g_ref, wall_ref, *proj_refs)


def _inproj_cast(x, g, w_in, widths):
    rows, d = x.shape
    wt_all, l = w_in
    outs = pl.pallas_call(
        functools.partial(_inproj_cast_body, layer=l),
        grid=(1,),
        in_specs=[_full_spec(x.shape), _pspec(g),
                  pl.BlockSpec(wt_all.shape, lambda i: (0, 0, 0), pipeline_mode=pl.Buffered(1))],
        out_specs=[_full_spec((rows, n)) for n in widths] + [_full_spec((sum(widths), d))],
        out_shape=[jax.ShapeDtypeStruct((rows, n), F32) for n in widths]
                  + [jax.ShapeDtypeStruct((sum(widths), d), BF16)],
        compiler_params=_cparams("arbitrary"),
        name="inproj_cast",
    )(x, _parg(g), wt_all)
    return outs[:-1], outs[-1]


def _inproj(x, g, w, widths):
    rows, d = x.shape
    tile = INPROJ_ROW_TILE
    row_spec = lambda w_: pl.BlockSpec((tile, w_), lambda i: (i, 0))
    return pl.pallas_call(
        _inproj_body,
        grid=(rows // tile,),
        in_specs=[row_spec(d), _pspec(g), _pspec(w, single=True)],
        out_specs=[row_spec(n) for n in widths],
        out_shape=[jax.ShapeDtypeStruct((rows, n), F32) for n in widths],
        compiler_params=_cparams("parallel"),
        name="inproj",
    )(x, _parg(g), _parg(w))


def _ssd_body(z_ref, xbc_ref, dt_ref, cw_ref, cb_ref, dtb_ref, alog_ref, dsk_ref, ng_ref,
              y_ref, conv_ref, hout_ref, xpad_scr, h_scr, *, chunk, group):
    L, G = chunk, group
    GL = G * L
    c = pl.program_id(1)
    pad = SUBLANES
    halo = SSD_CONV - 1
    hpg = SSD_HEADS // SSD_GROUPS
    assert hpg == 2 and hpg * SSD_HEAD_DIM == SSD_STATE

    @pl.when(c == 0)
    def _():
        xpad_scr[0:pad, :] = jnp.zeros((pad, SSD_CONV_DIM), F32)
        h_scr[...] = jnp.zeros(h_scr.shape, F32)

    xpad_scr[pad:pad + GL, :] = xbc_ref[...]
    xfull = xpad_scr[...]
    conv = cb_ref[...] + cw_ref[halo:halo + 1, :] * xfull[pad:pad + GL]
    for j in range(halo):
        conv = conv + cw_ref[j:j + 1, :] * pltpu.roll(xfull, halo - j, axis=0)[pad:pad + GL]
    xpad_scr[pad - halo:pad, :] = xpad_scr[pad + GL - halo:pad + GL, :]
    conv = _silu(conv)
    xs = conv[:, 0:GROUP_WIDTH]
    bm = conv[:, GROUP_WIDTH:2 * GROUP_WIDTH].astype(BF16)
    cm = conv[:, 2 * GROUP_WIDTH:3 * GROUP_WIDTH].astype(BF16)

    row = lax.broadcasted_iota(jnp.int32, (L, L), 0)
    col = lax.broadcasted_iota(jnp.int32, (L, L), 1)
    causal = row >= col
    tril = jnp.where(causal, 1.0, 0.0).astype(F32)
    dt = _softplus(dt_ref[...] + dtb_ref[...])
    da = dt * (-jnp.exp(alog_ref[...]) * LOG2_E)
    acs = [_dot(tril, da[i * L:(i + 1) * L, :], precision=HIGHEST) for i in range(G)]
    acs_t = [a.T for a in acs]
    e_acs = [jnp.exp2(a) for a in acs]
    e_end = [jnp.exp2(a[L - 1:L, :] - a) for a in acs]
    e_last = [jnp.exp2(a[L - 1:L, :]) for a in acs]

    keys = [(i, g) for i in range(G) for g in range(SSD_GROUPS)]
    rows_of = lambda x, i: x[i * L:(i + 1) * L]
    lanes_of = lambda x, g: x[:, g * SSD_STATE:(g + 1) * SSD_STATE]
    lane_lo = lax.broadcasted_iota(jnp.int32, (L, hpg * SSD_HEAD_DIM), 1) < SSD_HEAD_DIM
    row_lo = lax.broadcasted_iota(jnp.int32, (hpg * SSD_HEAD_DIM, SSD_STATE), 0) < SSD_HEAD_DIM
    head_cols = lambda a, g: jnp.where(lane_lo, a[:, g * hpg:g * hpg + 1], a[:, g * hpg + 1:g * hpg + 2])
    bg = {(i, g): lanes_of(rows_of(bm, i), g) for i, g in keys}
    cg = {(i, g): lanes_of(rows_of(cm, i), g) for i, g in keys}
    scores = {k: _dot_nt(cg[k], bg[k]) for k in keys}
    xdt = {(i, g): lanes_of(rows_of(xs, i), g) * head_cols(rows_of(dt, i), g) for i, g in keys}
    decay = {(i, h): jnp.exp2(jnp.where(causal, acs[i][:, h:h + 1] - acs_t[i][h:h + 1, :], -jnp.inf))
             for i in range(G) for h in range(SSD_HEADS)}
    p_mat = {(i, g): jnp.concatenate([(scores[(i, g)] * decay[(i, g * hpg + k)]).astype(BF16) for k in range(hpg)],
                                     axis=1) for i, g in keys}
    y_in = {k: _dot(p_mat[k], _bd(xdt[k].astype(BF16))) for k in keys}
    st = {(i, g): _dot_tn((xdt[(i, g)] * head_cols(e_end[i], g)).astype(BF16), bg[(i, g)]) for i, g in keys}

    y_rows = []
    for i in range(G):
        ys = []
        for g in range(SSD_GROUPS):
            h_prev = h_scr[g * hpg:(g + 1) * hpg].reshape(hpg * SSD_HEAD_DIM, SSD_STATE)
            ys.append(y_in[(i, g)] + _dot_nt(cg[(i, g)], h_prev.astype(BF16)) * head_cols(e_acs[i], g))
            keep = jnp.where(row_lo, e_last[i][:, g * hpg:g * hpg + 1], e_last[i][:, g * hpg + 1:g * hpg + 2])
            h_scr[g * hpg:(g + 1) * hpg] = (h_prev * keep + st[(i, g)]).reshape(hpg, SSD_HEAD_DIM, SSD_STATE)
        y_rows.append(jnp.concatenate(ys, axis=-1))
    y = jnp.concatenate(y_rows, axis=0) + xs * dsk_ref[...]
    y = y * _silu(z_ref[...])
    y_ref[...] = _rms(y, ng_ref[...])

    @pl.when(c == pl.num_programs(1) - 1)
    def _():
        hout_ref[0] = h_scr[...]
        conv_ref[0] = xpad_scr[pad - halo:pad, :]


def _ssd(z, xbc, dtr, lp, *, batch, seq):
    chunk = SSD_CHUNK
    rows = chunk * SSD_GROUP
    nc = seq // rows
    rspec = lambda w: pl.BlockSpec((rows, w), lambda b, c: (b * nc + c, 0))
    consts = (lp["conv_w"], lp["conv_b"], lp["dt_bias"], lp["a_log"], lp["d_skip"], lp["ssd_norm"])
    return pl.pallas_call(
        functools.partial(_ssd_body, chunk=chunk, group=SSD_GROUP),
        grid=(batch, nc),
        in_specs=[rspec(GROUP_WIDTH), rspec(SSD_CONV_DIM), rspec(LANES)] + [_pspec(a) for a in consts],
        out_specs=[rspec(GROUP_WIDTH),
                   pl.BlockSpec((1, SSD_CONV - 1, SSD_CONV_DIM), lambda b, c: (b, 0, 0)),
                   pl.BlockSpec((1, SSD_HEADS, SSD_HEAD_DIM, SSD_STATE), lambda b, c: (b, 0, 0, 0))],
        out_shape=[jax.ShapeDtypeStruct((batch * seq, GROUP_WIDTH), F32),
                   jax.ShapeDtypeStruct((batch, SSD_CONV - 1, SSD_CONV_DIM), F32),
                   jax.ShapeDtypeStruct((batch, SSD_HEADS, SSD_HEAD_DIM, SSD_STATE), F32)],
        scratch_shapes=[pltpu.VMEM((SUBLANES + rows, SSD_CONV_DIM), F32),
                        pltpu.VMEM((SSD_HEADS, SSD_HEAD_DIM, SSD_STATE), F32)],
        compiler_params=_cparams("parallel", "arbitrary"),
        name="ssd",
    )(z, xbc, dtr, *[_parg(a) for a in consts])


def _ssd_step_body(z_ref, xbc_ref, dt_ref, conv0_ref, h0_ref, *rest, seq, batch, layer):
    hdone_ref, rest = (rest[0], rest[1:]) if layer else (None, rest)
    (cw_ref, cb_ref, dtb_ref, aneg_ref, dsk_ref, ng_ref, hexp_ref, y_ref, conv_ref, hout_ref,
     xs_scr, bm_scr, cm_scr, xdt_scr, dec_scr, y_scr) = rest
    T, B = seq, batch
    if layer:
        hout_ref[0:layer] = hdone_ref[...]
    GW = GROUP_WIDTH
    j = pl.program_id(0)
    tiles = SSD_STEP_TILES

    @pl.when(j == 0)
    def _():
        rows = [conv0_ref[i] for i in range(SSD_CONV - 1)]
        rows += [xbc_ref[t * B:(t + 1) * B, :] for t in range(T)]
        for t in range(T):
            conv = cb_ref[...] + cw_ref[0:1, :] * rows[t]
            for i in range(1, SSD_CONV):
                conv = conv + cw_ref[i:i + 1, :] * rows[t + i]
            conv = _silu(conv)
            xs = conv[:, 0:GW]
            xs_scr[t] = xs
            for g in range(SSD_GROUPS):
                bm_scr[t, g] = conv[:, GW + g * SSD_STATE:GW + (g + 1) * SSD_STATE].T
                cm_scr[t, g] = conv[:, 2 * GW + g * SSD_STATE:2 * GW + (g + 1) * SSD_STATE].T
            dt = _softplus(dt_ref[t * B:(t + 1) * B, :] + dtb_ref[...])
            dte = _dot(dt, hexp_ref[...], precision=HIGHEST)
            xdt_scr[t] = (xs * dte).T
            dec_scr[t] = jnp.exp(dte * aneg_ref[...]).T
        for i in range(SSD_CONV - 1):
            conv_ref[i] = rows[T + i]

    hp0 = j * tiles
    grp = hp0 // (SSD_HEAD_DIM * (SSD_HEADS // SSD_GROUPS))
    for q in range(tiles):
        hp = pl.ds(hp0 + q, 1)
        h = h0_ref[:, q, :].T
        for t in range(T):
            h = h * dec_scr[t, hp, :] + bm_scr[t, grp] * xdt_scr[t, hp, :]
            y_scr[t, hp, :] = jnp.sum(h * cm_scr[t, grp], axis=0, keepdims=True)
        hout_ref[layer, :, q, :] = h.T

    @pl.when(j == pl.num_programs(0) - 1)
    def _():
        for t in range(T):
            y = y_scr[t].T + xs_scr[t] * dsk_ref[...]
            y = y * _silu(z_ref[t * B:(t + 1) * B, :])
            y_ref[t * B:(t + 1) * B, :] = _rms(y, ng_ref[...])


def _layer_state_specs(layer, block, axis):
    idx = lambda first: (lambda j: (first,) + tuple(j if a == axis else 0 for a in range(len(block))))
    cur = pl.BlockSpec((None,) + block, idx(layer))
    prev = [pl.BlockSpec((layer,) + block, idx(0))] if layer else []
    out = pl.BlockSpec((layer + 1,) + block, idx(0))
    return cur, prev, out


def _ssd_step(z, xbc, dtr, conv_all, h_all, h_done, lp, *, batch, seq, layer):
    n = batch * seq
    srows = SSD_HEADS * SSD_HEAD_DIM
    consts = (lp["conv_w"], lp["conv_b"], lp["dt_bias"], lp["a_neg_exp"], lp["d_skip"], lp["ssd_norm"], lp["head_expand"])
    hspec, prev_specs, hout_spec = _layer_state_specs(layer, (batch, SSD_STEP_TILES, SSD_STATE), 1)
    prev_args = [h_done] if layer else []
    cshape = (SSD_CONV - 1, batch, SSD_CONV_DIM)
    return pl.pallas_call(
        functools.partial(_ssd_step_body, seq=seq, batch=batch, layer=layer),
        grid=(srows // SSD_STEP_TILES,),
        in_specs=[_full_spec((n, GROUP_WIDTH)), _full_spec((n, SSD_CONV_DIM)), _full_spec((n, LANES)),
                  pl.BlockSpec((None,) + cshape, lambda j: (layer, 0, 0, 0)), hspec] + prev_specs
                 + [_pspec(a) for a in consts],
        out_specs=[_full_spec((n, GROUP_WIDTH)), _full_spec(cshape), hout_spec],
        out_shape=[jax.ShapeDtypeStruct((n, GROUP_WIDTH), F32),
                   jax.ShapeDtypeStruct(cshape, F32),
                   jax.ShapeDtypeStruct((layer + 1, batch, srows, SSD_STATE), F32)],
        scratch_shapes=[pltpu.VMEM((seq, batch, GROUP_WIDTH), F32),
                        pltpu.VMEM((seq, SSD_GROUPS, SSD_STATE, batch), F32),
                        pltpu.VMEM((seq, SSD_GROUPS, SSD_STATE, batch), F32),
                        pltpu.VMEM((seq, GROUP_WIDTH, batch), F32),
                        pltpu.VMEM((seq, GROUP_WIDTH, batch), F32),
                        pltpu.VMEM((seq, GROUP_WIDTH, batch), F32)],
        compiler_params=_cparams("arbitrary"),
        name="ssd_step",
    )(z, xbc, dtr, conv_all, h_all.reshape(h_all.shape[0], batch, srows, SSD_STATE),
      *prev_args, *[_parg(a) for a in consts])


PAIR = 2 * RWKV_HEAD
RWKV_PAIRS = RWKV_HEADS // 2


def _bd(x):
    half = x.shape[1] // 2
    lane = lax.broadcasted_iota(jnp.int32, x.shape, 1)
    zero = jnp.zeros_like(x)
    return jnp.concatenate([jnp.where(lane < half, x, zero), jnp.where(lane >= half, x, zero)], axis=0)


def _half_sums(x, lo):
    s_lo = jnp.sum(jnp.where(lo, x, 0.0), axis=-1, keepdims=True)
    s_hi = jnp.sum(jnp.where(lo, 0.0, x), axis=-1, keepdims=True)
    return jnp.where(lo, s_lo, s_hi)


def _head_sum(x):
    lo = lax.broadcasted_iota(jnp.int32, (x.shape[0], PAIR), 1) < RWKV_HEAD
    return jnp.concatenate([_half_sums(x[:, p * PAIR:(p + 1) * PAIR], lo) for p in range(RWKV_PAIRS)], axis=-1)


def _rwkv_pointwise(u, prev, mu_ref, w0_ref, w2_ref, a0_ref, a2_ref, g2_ref, kk_ref, ka_ref):
    GW = GROUP_WIDTH
    xs = u + (prev - u) * mu_ref[...]
    r = xs[:, 0:GW]
    k = xs[:, GW:2 * GW]
    v = xs[:, 2 * GW:3 * GW]
    wd = xs[:, 3 * GW:3 * GW + 64]
    ad = xs[:, 3 * GW + 64:3 * GW + 128]
    gd = xs[:, 3 * GW + 128:3 * GW + 256]
    w_lin = w0_ref[...] + _dot(jnp.tanh(wd).astype(BF16), w2_ref[...])
    logdecay = -math.exp(-0.5) * _sigmoid(w_lin)
    a = _sigmoid(a0_ref[...] + _dot(ad.astype(BF16), a2_ref[...]))
    g = _dot(_sigmoid(gd).astype(BF16), g2_ref[...])
    kk = k * kk_ref[...]
    kk = kk * lax.rsqrt(jnp.maximum(_head_sum(kk * kk), 1e-24))
    k = k * (1.0 + (a - 1.0) * ka_ref[...])
    return r, k, v, logdecay, a, g, kk


def _rwkv_finish(y, r, k, v, g, rk_ref, lng_ref, lnb_ref):
    mean = _head_sum(y) * (1.0 / RWKV_HEAD)
    yc = y - mean
    var = _head_sum(yc * yc) * (1.0 / RWKV_HEAD)
    y = yc * lax.rsqrt(var + RWKV_LN_EPS) * lng_ref[...] + lnb_ref[...]
    bonus = _head_sum(r * k * rk_ref[...]) * v
    return (y + bonus) * g


def _rwkv_body(u_ref, mu_ref, w0_ref, w2_ref, a0_ref, a2_ref, g2_ref, kk_ref, ka_ref, rk_ref,
               lng_ref, lnb_ref, y_ref, shift_ref, sout_ref, upad_scr, s_scr, *, chunk, group):
    L, G = chunk, group
    GL = G * L
    c = pl.program_id(1)
    pad = SUBLANES

    @pl.when(c == 0)
    def _():
        upad_scr[0:pad, :] = jnp.zeros((pad, RWKV_PROJ), F32)
        s_scr[...] = jnp.zeros(s_scr.shape, F32)

    u = u_ref[...]
    upad_scr[pad:pad + GL, :] = u
    prev = pltpu.roll(upad_scr[...], 1, axis=0)[pad:pad + GL]
    upad_scr[pad - 1:pad, :] = u[GL - 1:GL, :]
    r, k, v, logdecay, a, g, kk = _rwkv_pointwise(u, prev, mu_ref, w0_ref, w2_ref, a0_ref, a2_ref, g2_ref,
                                                  kk_ref, ka_ref)

    tril = jnp.where(lax.broadcasted_iota(jnp.int32, (L, L), 0) >= lax.broadcasted_iota(jnp.int32, (L, L), 1),
                     1.0, 0.0).astype(F32)
    cl = jnp.concatenate([_dot(tril, logdecay[i * L:(i + 1) * L, :], precision=HIGHEST) for i in range(G)], axis=0)
    e_in = jnp.exp(cl)
    e_inv = jnp.exp(-cl)
    r_t = r * e_in
    r_tb = r_t.astype(BF16)
    a_tb = (-kk * jnp.exp(cl - logdecay)).astype(BF16)
    b_tb = (kk * a * e_inv).astype(BF16)
    k_tb = (k * e_inv).astype(BF16)
    vb = v.astype(BF16)

    row = lax.broadcasted_iota(jnp.int32, (L, PAIR), 0)
    colh = lax.broadcasted_iota(jnp.int32, (L, PAIR), 1) & (RWKV_HEAD - 1)
    strict = row > colh
    incl = row >= colh
    eye_pair = jnp.where(row == colh, 1.0, 0.0).astype(F32)
    lane_lo = lax.broadcasted_iota(jnp.int32, (RWKV_HEAD, PAIR), 1) < RWKV_HEAD
    same_head = (lax.broadcasted_iota(jnp.int32, (PAIR, PAIR), 0) < RWKV_HEAD) == \
                (lax.broadcasted_iota(jnp.int32, (PAIR, PAIR), 1) < RWKV_HEAD)

    streams = [(i, p) for i in range(G) for p in range(RWKV_PAIRS)]
    ns = len(streams)
    blk = lambda x, i, p: x[i * L:(i + 1) * L, p * PAIR:(p + 1) * PAIR]
    lhs = [jnp.concatenate([blk(a_tb, i, p), blk(r_tb, i, p)], axis=0) for i, p in streams]
    m_both = [_dot_nt(lhs[s], jnp.concatenate([_bd(blk(b_tb, i, p)), _bd(blk(k_tb, i, p))], axis=0))
              for s, (i, p) in enumerate(streams)]
    m_ab = [m[:, 0:PAIR] for m in m_both]
    m_ak = [m[:, PAIR:2 * PAIR] for m in m_both]
    n_ab = [jnp.where(strict, m[0:L], 0.0) for m in m_ab]
    m_rb = [jnp.where(incl, m[L:2 * L], 0.0).astype(BF16) for m in m_ab]
    n_ak = [jnp.where(strict, m[0:L], 0.0).astype(BF16) for m in m_ak]
    m_rk = [jnp.where(incl, m[L:2 * L], 0.0).astype(BF16) for m in m_ak]
    tinv = [eye_pair + n for n in n_ab]
    pwb = [n.astype(BF16) for n in n_ab]
    pw = [_dot(x, _bd(x)) for x in pwb]
    for _ in range(int(math.log2(L)) - 2):
        pwb = [x.astype(BF16) for x in pw]
        both = [_dot(jnp.concatenate([pwb[s], tinv[s].astype(BF16)], axis=0), _bd(pwb[s])) for s in range(ns)]
        pw = [x[0:L] for x in both]
        tinv = [tinv[s] + both[s][L:2 * L] for s in range(ns)]
    pwb = [x.astype(BF16) for x in pw]
    tinv = [tinv[s] + _dot(tinv[s].astype(BF16), _bd(pwb[s])) for s in range(ns)]
    tinvb = [x.astype(BF16) for x in tinv]
    nv_mv = [_dot(jnp.concatenate([n_ak[s], m_rk[s]], axis=0), _bd(blk(vb, i, p))) for s, (i, p) in enumerate(streams)]
    wu = [_dot(tinvb[s], jnp.concatenate([_bd(blk(a_tb, i, p)), _bd(nv_mv[s][0:L].astype(BF16))], axis=1))
          for s, (i, p) in enumerate(streams)]
    wub = [x.astype(BF16) for x in wu]
    qy = [_dot(m_rb[s], jnp.concatenate([_bd(wub[s][:, 0:PAIR]), _bd(wub[s][:, PAIR:2 * PAIR])], axis=1))
          for s in range(ns)]
    q = [(blk(r_t, i, p) + qy[s][:, 0:PAIR]).astype(BF16) for s, (i, p) in enumerate(streams)]
    y_loc = [qy[s][:, PAIR:2 * PAIR] + nv_mv[s][L:2 * L] for s in range(ns)]
    zeros_b = jnp.zeros((L, PAIR), BF16)
    mg = [_dot_tn(jnp.concatenate([wub[s], jnp.concatenate([zeros_b, blk(vb, i, p)], axis=1)], axis=0),
                  jnp.concatenate([blk(b_tb, i, p), blk(k_tb, i, p)], axis=0))
          for s, (i, p) in enumerate(streams)]
    p_end = [e_in[(i + 1) * L - 1:(i + 1) * L, p * PAIR:(p + 1) * PAIR] for i, p in streams]
    m_t = [(jnp.where(same_head, mg[s][0:PAIR], 0.0) * p_end[s]).astype(BF16) for s in range(ns)]
    g_t = [jnp.where(lane_lo, mg[s][PAIR:PAIR + RWKV_HEAD], mg[s][PAIR + RWKV_HEAD:2 * PAIR]) * p_end[s]
           for s in range(ns)]

    y_rows = []
    for i in range(G):
        y_pairs = []
        for p in range(RWKV_PAIRS):
            s = i * RWKV_PAIRS + p
            s0 = s_scr[p]
            s0b = s0.astype(BF16)
            y_pairs.append(_dot_nt(q[s], _bd(s0b)) + y_loc[s])
            s_scr[p] = s0 * p_end[s] + _dot(s0b, m_t[s]) + g_t[s]
        y_rows.append(jnp.concatenate(y_pairs, axis=-1))
    y = jnp.concatenate(y_rows, axis=0)
    y_ref[...] = _rwkv_finish(y, r, k, v, g, rk_ref, lng_ref, lnb_ref)

    @pl.when(c == pl.num_programs(1) - 1)
    def _():
        sout_ref[0] = s_scr[...]
        shift_ref[0] = upad_scr[pad - 1:pad, :]


_RWKV_PARAM_NAMES = ("mu", "w0", "w2", "a0", "a2", "g2", "k_k", "k_a", "r_k", "ln_g", "ln_b")


def _rwkv(u, p, *, batch, seq):
    rows = RWKV_CHUNK * RWKV_GROUP
    nc = seq // rows
    params = [p[n] for n in _RWKV_PARAM_NAMES]
    sspec = pl.BlockSpec((1, RWKV_PAIRS, RWKV_HEAD, PAIR), lambda b, c: (b, 0, 0, 0))
    y, shift, s_last = pl.pallas_call(
        functools.partial(_rwkv_body, chunk=RWKV_CHUNK, group=RWKV_GROUP),
        grid=(batch, nc),
        in_specs=[pl.BlockSpec((rows, RWKV_PROJ), lambda b, c: (b * nc + c, 0))] + [_pspec(a) for a in params],
        out_specs=[pl.BlockSpec((rows, GROUP_WIDTH), lambda b, c: (b * nc + c, 0)),
                   pl.BlockSpec((1, 1, RWKV_PROJ), lambda b, c: (b, 0, 0)), sspec],
        out_shape=[jax.ShapeDtypeStruct((batch * seq, GROUP_WIDTH), F32),
                   jax.ShapeDtypeStruct((batch, 1, RWKV_PROJ), F32),
                   jax.ShapeDtypeStruct((batch, RWKV_PAIRS, RWKV_HEAD, PAIR), F32)],
        scratch_shapes=[pltpu.VMEM((SUBLANES + rows, RWKV_PROJ), F32),
                        pltpu.VMEM((RWKV_PAIRS, RWKV_HEAD, PAIR), F32)],
        compiler_params=_cparams("parallel", "arbitrary"),
        name="rwkv",
    )(u, *[_parg(a) for a in params])
    s_last = s_last.reshape(batch, RWKV_PAIRS, RWKV_HEAD, 2, RWKV_HEAD).transpose(0, 1, 3, 2, 4).reshape(
        batch, RWKV_HEADS, RWKV_HEAD, RWKV_HEAD)
    return y, shift.reshape(batch, RWKV_PROJ), s_last


def _rwkv_step_body(u_ref, shift0_ref, s0_ref, *rest, seq, batch, layer):
    sdone_ref, rest = (rest[0], rest[1:]) if layer else (None, rest)
    (mu_ref, w0_ref, w2_ref, a0_ref, a2_ref, g2_ref, kk_ref, ka_ref, rk_ref, lng_ref, lnb_ref, y_ref, sout_ref,
     r_scr, w_scr, k_scr, b_scr, nkk_scr, v_scr, y_scr) = rest
    T, B = seq, batch
    j = pl.program_id(0)
    if layer:
        sout_ref[0:layer] = sdone_ref[...]
    tiles = RWKV_STEP_TILES

    def pointwise(t):
        u = u_ref[t * B:(t + 1) * B, :]
        prev = shift0_ref[...] if t == 0 else u_ref[(t - 1) * B:t * B, :]
        return _rwkv_pointwise(u, prev, mu_ref, w0_ref, w2_ref, a0_ref, a2_ref, g2_ref, kk_ref, ka_ref)

    @pl.when(j == 0)
    def _():
        for t in range(T):
            r, k, v, logdecay, a, _, kk = pointwise(t)
            r_scr[t] = r.T
            w_scr[t] = jnp.exp(logdecay).T
            k_scr[t] = k.T
            b_scr[t] = (kk * a).T
            nkk_scr[t] = (-kk).T
            v_scr[t] = v.T

    i0 = j * tiles
    keys = pl.ds(pl.multiple_of((i0 // RWKV_HEAD) * RWKV_HEAD, RWKV_HEAD), RWKV_HEAD)
    for q in range(tiles):
        vi = pl.ds(i0 + q, 1)
        s = s0_ref[q]
        for t in range(T):
            sa = jnp.sum(s * nkk_scr[t, keys, :], axis=0, keepdims=True)
            s = s * w_scr[t, keys, :] + k_scr[t, keys, :] * v_scr[t, vi, :] + b_scr[t, keys, :] * sa
            y_scr[t, vi, :] = jnp.sum(s * r_scr[t, keys, :], axis=0, keepdims=True)
        sout_ref[layer, q] = s

    @pl.when(j == pl.num_programs(0) - 1)
    def _():
        for t in range(T):
            r, k, v, _, _, g, _ = pointwise(t)
            y_ref[t * B:(t + 1) * B, :] = _rwkv_finish(y_scr[t].T, r, k, v, g, rk_ref, lng_ref, lnb_ref)


def _rwkv_step(u, shift0, s_all, s_done, p, *, batch, seq, layer):
    n = batch * seq
    srows = RWKV_HEADS * RWKV_HEAD
    params = [p[nm] for nm in _RWKV_PARAM_NAMES]
    sspec, prev_specs, sout_spec = _layer_state_specs(layer, (RWKV_STEP_TILES, RWKV_HEAD, batch), 0)
    prev_args = [s_done] if layer else []
    tposed = pltpu.VMEM((seq, GROUP_WIDTH, batch), F32)
    return pl.pallas_call(
        functools.partial(_rwkv_step_body, seq=seq, batch=batch, layer=layer),
        grid=(srows // RWKV_STEP_TILES,),
        in_specs=[_full_spec((n, RWKV_PROJ)), _full_spec((batch, RWKV_PROJ)), sspec] + prev_specs
                 + [_pspec(a) for a in params],
        out_specs=[_full_spec((n, GROUP_WIDTH)), sout_spec],
        out_shape=[jax.ShapeDtypeStruct((n, GROUP_WIDTH), F32),
                   jax.ShapeDtypeStruct((layer + 1, srows, RWKV_HEAD, batch), F32)],
        scratch_shapes=[tposed] * 7,
        compiler_params=_cparams("arbitrary"),
        name="rwkv_step",
    )(u, shift0, s_all, *prev_args, *[_parg(a) for a in params])


def _s5_body(u_ref, hre0_ref, him0_ref, are_ref, aim_ref, bmat_ref, cmat_ref, d_ref, gw_ref, gb_ref,
             y_ref, hre_ref, him_ref, hs_scr, tm_scr, *, steps, batch_major):
    c = pl.program_id(1)
    ns = S5_WIDTH
    bsub = hre_ref.shape[0]

    @pl.when(c == 0)
    def _():
        hre_ref[...] = hre0_ref[...]
        him_ref[...] = him0_ref[...]

    if batch_major:
        for b in range(bsub):
            tm_scr[:, b, :] = u_ref[b]
        u = tm_scr[...].reshape(steps * bsub, GROUP_WIDTH)
    else:
        u = u_ref[...].reshape(steps * bsub, GROUP_WIDTH)
    are = jnp.broadcast_to(are_ref[...], (bsub, ns))
    aim = jnp.broadcast_to(aim_ref[...], (bsub, ns))
    hre, him = hre_ref[...], him_ref[...]
    sub = min(S5_SUB, steps)
    rows = sub * bsub
    outs = []
    hs_scr[...] = _dot(u.astype(BF16), bmat_ref[...])
    for k in range(steps // sub):
        r0 = k * rows
        u_k = u[r0:r0 + rows]
        for t in range(sub):
            rs = slice(r0 + t * bsub, r0 + (t + 1) * bsub)
            hre, him = (are * hre - aim * him + hs_scr[rs, 0:ns], are * him + aim * hre + hs_scr[rs, ns:2 * ns])
            hs_scr[rs, 0:ns] = hre
            hs_scr[rs, ns:2 * ns] = him
        y = _dot(hs_scr[r0:r0 + rows, :].astype(BF16), cmat_ref[...]) + u_k * d_ref[...]
        y = _gelu_tanh(y)
        yy = _dot(y.astype(BF16), gw_ref[...]) + gb_ref[...]
        outs.append(yy[:, 0:GROUP_WIDTH] * _sigmoid(yy[:, GROUP_WIDTH:2 * GROUP_WIDTH]))
    hre_ref[...] = hre
    him_ref[...] = him
    out = jnp.concatenate(outs, axis=0).reshape(steps, bsub, GROUP_WIDTH)
    if batch_major:
        tm_scr[...] = out
        for b in range(bsub):
            y_ref[b] = tm_scr[:, b, :]
    else:
        y_ref[...] = out


def _time_specs(u, batch_major, chunk):
    if batch_major:
        batch, seq, _ = u.shape
        steps = min(chunk, seq)
        bsub = SUBLANES
        spec = pl.BlockSpec((bsub, steps, GROUP_WIDTH), lambda b, c: (b, c, 0))
    else:
        seq, batch, _ = u.shape
        steps = min(chunk, seq)
        bsub = min(batch, SUBLANES * max(1, chunk // steps))
        spec = pl.BlockSpec((steps, bsub, GROUP_WIDTH), lambda b, c: (c, b, 0))
    return batch, seq, steps, bsub, spec


def _s5(u, hre0, him0, lp, *, batch_major):
    batch, seq, steps, bsub, tspec = _time_specs(u, batch_major, TM_CHUNK)
    hspec =pl.BlockSpec((bsub, S5_WIDTH), lambda b, c: (b, 0))
    consts = (lp["s5_are"], lp["s5_aim"], lp["s5_bmat"], lp["s5_cmat"], lp["s5_d"], lp["s5_gw"], lp["s5_gb"])
    return pl.pallas_call(
        functools.partial(_s5_body, steps=steps, batch_major=batch_major),
        grid=(batch // bsub, seq // steps),
        in_specs=[tspec, hspec, hspec] + [_pspec(a) for a in consts],
        out_specs=[tspec, hspec, hspec],
        out_shape=[jax.ShapeDtypeStruct(u.shape, F32),
                   jax.ShapeDtypeStruct((batch, S5_WIDTH), F32),
                   jax.ShapeDtypeStruct((batch, S5_WIDTH), F32)],
        scratch_shapes=[pltpu.VMEM((steps * bsub, 2 * S5_WIDTH), F32),
                        pltpu.VMEM((steps, bsub, GROUP_WIDTH), F32)],
        compiler_params=_cparams("parallel", "arbitrary"),
        name="s5",
    )(u, hre0, him0, *[_parg(a) for a in consts])


def _pool_body(u_ref, buf0_ref, pw_ref, sc_ref, y_ref, buf_ref, f_scr, tm_scr, *, steps, pos0, batch_major):
    c = pl.program_id(1)
    bsub = f_scr.shape[1]
    GW = GROUP_WIDTH
    halo = POOL_BUF + 1

    @pl.when(c == 0)
    def _():
        f_scr[0] = jnp.zeros((bsub, GW), F32)
        f_scr[1:halo] = buf0_ref[...]

    if batch_major:
        for b in range(bsub):
            f_scr[halo:halo + steps, b, :] = u_ref[b]
    else:
        f_scr[halo:halo + steps] = u_ref[...]
    f = f_scr[...]
    u = f[halo:halo + steps]
    s2 = f[1:] + f[:-1]
    s4 = s2[2:] + s2[:-2]
    s8 = s4[4:] + s4[:-4]
    s16 = s8[8:] + s8[:-8]
    f_scr[0:halo] = f[steps:steps + halo]
    lane = lax.broadcasted_iota(jnp.int32, (steps, bsub, GW), 2)
    tpos = lax.broadcasted_iota(jnp.int32, (steps, bsub, GW), 0) + (pos0 + 1) + c * steps
    win = jnp.where(lane < POOL_CH, s2[halo - 1:halo - 1 + steps],
                    jnp.where(lane < 2 * POOL_CH, s4[halo - 3:halo - 3 + steps],
                              jnp.where(lane < 3 * POOL_CH, s8[halo - 7:halo - 7 + steps],
                                        s16[halo - 15:halo - 15 + steps])))
    wlen = jnp.where(lane < POOL_CH, POOL_WINDOWS[0],
                     jnp.where(lane < 2 * POOL_CH, POOL_WINDOWS[1],
                               jnp.where(lane < 3 * POOL_CH, POOL_WINDOWS[2], POOL_WINDOWS[3])))
    cnt = jnp.minimum(tpos, wlen).astype(F32)
    pooled = (win / cnt - u).reshape(steps * bsub, GW)
    y = (_dot(pooled.astype(BF16), pw_ref[...]) * sc_ref[...]).reshape(steps, bsub, GW)
    if batch_major:
        tm_scr[...] = y
        for b in range(bsub):
            y_ref[b] = tm_scr[:, b, :]
    else:
        y_ref[...] = y

    @pl.when(c == pl.num_programs(1) - 1)
    def _():
        buf_ref[...] = f_scr[1:halo]


def _pool(u, buf0, lp, *, pos0, batch_major, layer=None):
    batch, seq, steps, bsub, tspec = _time_specs(u, batch_major, POOL_CHUNK)
    bblock =(POOL_BUF, bsub, GROUP_WIDTH)
    bspec = pl.BlockSpec(bblock, lambda b, c: (0, b, 0))
    if layer is None:
        bspec_in = bspec
    else:
        bspec_in = pl.BlockSpec((None,) + bblock, lambda b, c: (layer, 0, b, 0))
    return pl.pallas_call(
        functools.partial(_pool_body, steps=steps, pos0=pos0, batch_major=batch_major),
        grid=(batch // bsub, seq // steps),
        in_specs=[tspec, bspec_in, _pspec(lp["pool_w"]), _pspec(lp["pool_scale"])],
        out_specs=[tspec, bspec],
        out_shape=[jax.ShapeDtypeStruct(u.shape, F32), jax.ShapeDtypeStruct((POOL_BUF, batch, GROUP_WIDTH), F32)],
        scratch_shapes=[pltpu.VMEM((POOL_BUF + 1 + steps, bsub, GROUP_WIDTH), F32),
                        pltpu.VMEM((steps, bsub, GROUP_WIDTH), F32)],
        compiler_params=_cparams("parallel", "arbitrary"),
        name="pool",
    )(u, buf0, _parg(lp["pool_w"]), _parg(lp["pool_scale"]))


def _block_diag(blocks):
    n, g, r, c = blocks.shape
    eye = jnp.eye(g, dtype=blocks.dtype)
    return (eye[None, :, None, :, None] * blocks[:, :, :, None, :]).reshape(n, g * r, g * c)


def _stacked_params(P):
    row = lambda a: a.reshape(a.shape[0], 1, -1)
    pad_lanes = lambda a: jnp.pad(a, ((0, 0), (0, LANES - a.shape[1])))
    bf = lambda a: a.astype(BF16)

    lam = lax.complex(P["s5_lam_re"], P["s5_lam_im"])
    a_bar = jnp.exp(lam * jnp.exp(P["s5_log_step"])[..., None])
    b_bar = ((a_bar - 1.0) / lam)[..., None] * lax.complex(P["s5_b_re"], P["s5_b_im"])
    b_t = jnp.swapaxes(b_bar, 2, 3)
    bmat = jnp.concatenate([_block_diag(jnp.real(b_t)), _block_diag(jnp.imag(b_t))], axis=2)
    c_t = jnp.swapaxes(lax.complex(P["s5_c_re"], P["s5_c_im"]), 2, 3)
    cmat = jnp.concatenate([_block_diag(jnp.real(c_t)), -_block_diag(jnp.imag(c_t))], axis=1)

    out = dict(
        norm_ffn1=row(P["norm_ffn1"]), ffn1_in=P["ffn1_in"], ffn1_out=P["ffn1_out"],
        norm_mix=row(P["norm_mix"]),
        w_in=jnp.transpose(P["w_in"], (2, 0, 1)),
        conv_w=P["ssd_conv_w"], conv_b=row(P["ssd_conv_b"]),
        dt_bias=row(pad_lanes(P["ssd_dt_bias"])), a_log=row(pad_lanes(P["ssd_a_log"])),
        a_neg_exp=row(jnp.repeat(-jnp.exp(P["ssd_a_log"]), SSD_HEAD_DIM, axis=1)),
        d_skip=row(jnp.repeat(P["ssd_d"], SSD_HEAD_DIM, axis=1)), ssd_norm=row(P["ssd_norm"]),
        s5_are=row(jnp.real(a_bar)), s5_aim=row(jnp.imag(a_bar)), s5_bmat=bf(bmat), s5_cmat=bf(cmat),
        s5_d=row(P["s5_d"]), s5_gw=bf(P["s5_glu_w"]), s5_gb=row(P["s5_glu_b"]),
        pool_w=bf(_block_diag(P["pool_w"])), pool_scale=row(P["pool_scale"]),
        w_out=bf(P["w_out"]),
        norm_ffn2=row(P["norm_ffn2"]), ffn2_in=P["ffn2_in"], ffn2_out=P["ffn2_out"],
    )
    for name in _RWKV_PARAM_NAMES:
        a = P["rwkv_" + name]
        out["rwkv_" + name] = bf(a) if name in ("w2", "a2", "g2") else row(a)
    return out


def _layer_params(stacked, l):
    lp = {k: _Layered((v, l)) for k, v in stacked.items()}
    lp["rwkv"] = {n: lp["rwkv_" + n] for n in _RWKV_PARAM_NAMES}
    lp["head_expand"] = jnp.pad(jnp.repeat(jnp.eye(SSD_HEADS, dtype=F32), SSD_HEAD_DIM, axis=1),
                                ((0, LANES - SSD_HEADS), (0, 0)))
    return lp


def _mixers_prompt(lp, proj, *, batch, seq):
    z, xbc, ur, us5, upool, dtr = proj
    y_ssd, conv_new, ssd_new = _ssd(z, xbc, dtr, lp, batch=batch, seq=seq)
    y_rwkv, shift_new, rwkv_new = _rwkv(ur, lp["rwkv"], batch=batch, seq=seq)
    zeros = jnp.zeros((batch, S5_WIDTH), F32)
    bm = lambda a: a.reshape(batch, seq, a.shape[-1])
    rows = lambda a: a.reshape(batch * seq, a.shape[-1])
    y_s5, s5re, s5im = _s5(bm(us5), zeros, zeros, lp, batch_major=True)
    y_pool, pool_new = _pool(bm(upool), jnp.zeros((POOL_BUF, batch, GROUP_WIDTH), F32), lp, pos0=0,
                             batch_major=True)
    ys = (y_ssd, y_rwkv, rows(y_s5), rows(y_pool))
    states = (conv_new, ssd_new, shift_new, rwkv_new, s5re.reshape(batch, S5_GROUPS, S5_STATE),
              s5im.reshape(batch, S5_GROUPS, S5_STATE), jnp.swapaxes(pool_new, 0, 1))
    return ys, states


def _mixers_decode(lp, proj, states, done, *, batch, seq, layer):
    z, xbc, ur, us5, upool, dtr = proj
    shift0, s5re0, s5im0 = (states[i][layer] for i in (2, 4, 5))
    ssd_done, rwkv_done = (done[1], done[3]) if layer else (None, None)
    y_ssd, conv_new, ssd_new = _ssd_step(z, xbc, dtr, states[0], states[1], ssd_done, lp, batch=batch, seq=seq,
                                         layer=layer)
    y_rwkv, rwkv_new = _rwkv_step(ur, shift0, states[3], rwkv_done, lp["rwkv"], batch=batch, seq=seq, layer=layer)
    shift_new = ur[(seq - 1) * batch:, :]
    tm = lambda a: a.reshape(seq, batch, a.shape[-1])
    y_s5, s5re, s5im = _s5(tm(us5), s5re0.reshape(batch, S5_WIDTH), s5im0.reshape(batch, S5_WIDTH), lp,
                           batch_major=False)
    y_pool, pool_new = _pool(tm(upool), states[6], lp, pos0=PAST_LEN, batch_major=False, layer=layer)
    rows = lambda a: a.reshape(seq * batch, a.shape[-1])
    ys = (y_ssd, y_rwkv, rows(y_s5), rows(y_pool))
    new_states = (jnp.swapaxes(conv_new, 0, 1), ssd_new, shift_new, rwkv_new,
                  s5re.reshape(batch, S5_GROUPS, S5_STATE), s5im.reshape(batch, S5_GROUPS, S5_STATE),
                  jnp.swapaxes(pool_new, 0, 1))
    return ys, new_states


_WIDTHS = (GROUP_WIDTH, SSD_CONV_DIM, RWKV_PROJ, GROUP_WIDTH, GROUP_WIDTH, LANES)


def _trunk(x_p, x_s, layer_params, norm_final, mixers_p, mixers_s):
    st_p, st_s = [], []
    mix_p, mix_s, lp = None, None, None
    for l, lp_next in enumerate(layer_params):
        if l > 0:
            x_s, wg, wu, wo = _ffn_cast(x_s, lp["norm_ffn2"], lp["ffn2_in"], lp["ffn2_out"], mix=mix_s, wmix=lp["w_out"])
            x_p = _ffn(x_p, lp["norm_ffn2"], wg, wu, wo, mix=mix_p, wmix=lp["w_out"])
        lp = lp_next
        x_s, wg, wu, wo = _ffn_cast(x_s, lp["norm_ffn1"], lp["ffn1_in"], lp["ffn1_out"])
        x_p = _ffn(x_p, lp["norm_ffn1"], wg, wu, wo)
        proj_s, w_all = _inproj_cast(x_s, lp["norm_mix"], lp["w_in"], _WIDTHS)
        mix_p, st = mixers_p(l, lp, _inproj(x_p, lp["norm_mix"], w_all, _WIDTHS), st_p[-1] if st_p else None)
        st_p.append(st)
        mix_s, st = mixers_s(l, lp, proj_s, st_s[-1] if st_s else None)
        st_s.append(st)
    x_s, wg, wu, wo = _ffn_cast(x_s, lp["norm_ffn2"], lp["ffn2_in"], lp["ffn2_out"], mix=mix_s, wmix=lp["w_out"],
                                gf=norm_final)
    x_p = _ffn(x_p, lp["norm_ffn2"], wg, wu, wo, mix=mix_p, wmix=lp["w_out"], gf=norm_final)
    return (x_p, x_s), (st_p, st_s)


def kernel(x_prompt, x_sample, state_ssd_conv, state_ssd, state_rwkv_shift, state_rwkv, state_s5_re, state_s5_im, state_pool, norm_ffn1, ffn1_in, ffn1_out, norm_mix, w_in, ssd_conv_w, ssd_conv_b, ssd_dt_bias, ssd_a_log, ssd_d, ssd_norm, rwkv_mu, rwkv_w0, rwkv_w2, rwkv_a0, rwkv_a2, rwkv_g2, rwkv_k_k, rwkv_k_a, rwkv_r_k, rwkv_ln_g, rwkv_ln_b, s5_lam_re, s5_lam_im, s5_log_step, s5_b_re, s5_b_im, s5_c_re, s5_c_im, s5_d, s5_glu_w, s5_glu_b, pool_w, pool_scale, w_out, norm_ffn2, ffn2_in, ffn2_out, norm_final):
    P = dict(norm_ffn1=norm_ffn1, ffn1_in=ffn1_in, ffn1_out=ffn1_out, norm_mix=norm_mix, w_in=w_in,
             ssd_conv_w=ssd_conv_w, ssd_conv_b=ssd_conv_b, ssd_dt_bias=ssd_dt_bias, ssd_a_log=ssd_a_log,
             ssd_d=ssd_d, ssd_norm=ssd_norm, rwkv_mu=rwkv_mu, rwkv_w0=rwkv_w0, rwkv_w2=rwkv_w2, rwkv_a0=rwkv_a0,
             rwkv_a2=rwkv_a2, rwkv_g2=rwkv_g2, rwkv_k_k=rwkv_k_k, rwkv_k_a=rwkv_k_a,
             rwkv_r_k=rwkv_r_k.reshape(rwkv_r_k.shape[0], -1), rwkv_ln_g=rwkv_ln_g, rwkv_ln_b=rwkv_ln_b,
             s5_lam_re=s5_lam_re, s5_lam_im=s5_lam_im, s5_log_step=s5_log_step, s5_b_re=s5_b_re, s5_b_im=s5_b_im,
             s5_c_re=s5_c_re, s5_c_im=s5_c_im, s5_d=s5_d, s5_glu_w=s5_glu_w, s5_glu_b=s5_glu_b, pool_w=pool_w,
             pool_scale=pool_scale, w_out=w_out, norm_ffn2=norm_ffn2, ffn2_in=ffn2_in, ffn2_out=ffn2_out)
    depth = norm_ffn1.shape[0]
    bp, tp, d = x_prompt.shape
    bs, ts, _ = x_sample.shape
    stacked = _stacked_params(P)
    layer_params = [_layer_params(stacked, l) for l in range(depth)]
    gf = norm_final.reshape(1, -1)
    sample_states = (state_ssd_conv, state_ssd, state_rwkv_shift, state_rwkv, state_s5_re, state_s5_im, state_pool)
    rwkv_rows = RWKV_HEADS * RWKV_HEAD
    decode_states = (jnp.swapaxes(state_ssd_conv, 1, 2), state_ssd, state_rwkv_shift,
                     jnp.transpose(state_rwkv, (0, 2, 3, 4, 1)).reshape(depth, rwkv_rows, RWKV_HEAD, bs),
                     state_s5_re, state_s5_im, jnp.swapaxes(state_pool, 1, 2))

    x_s = jnp.swapaxes(x_sample, 0, 1).reshape(ts * bs, d)
    (y_p, y_s), (st_p, st_s) = _trunk(
        x_prompt.reshape(bp * tp, d), x_s, layer_params, gf,
        lambda l, lp, proj, done: _mixers_prompt(lp, proj, batch=bp, seq=tp),
        lambda l, lp, proj, done: _mixers_decode(lp, proj, decode_states, done, batch=bs, seq=ts, layer=l))
    outs = [y_p.reshape(bp, tp, d), jnp.swapaxes(y_s.reshape(ts, bs, d), 0, 1)]
    for i, ref_state in enumerate(sample_states):
        outs.append(jnp.stack([st[i] for st in st_p]))
        if i == 1:
            outs.append(st_s[-1][i].reshape(ref_state.shape))
        elif i == 3:
            s_new = st_s[-1][i].reshape(depth, RWKV_HEADS, RWKV_HEAD, RWKV_HEAD, bs)
            outs.append(jnp.transpose(s_new, (0, 4, 1, 2, 3)))
        else:
            outs.append(jnp.stack([st[i] for st in st_s]))
    return tuple(outs)
```

```python
import functools
import math

import jax
import jax.numpy as jnp
from jax import lax
from jax.experimental import pallas as pl
from jax.experimental.pallas import tpu as pltpu

F32 = jnp.float32
BF16 = jnp.bfloat16
HIGHEST = lax.Precision.HIGHEST

SUBLANES = 8
LANES = 128
VMEM_LIMIT_BYTES = 56 * 1024 * 1024

GROUP_WIDTH = 256
SSD_HEAD_DIM = 64
SSD_HEADS = 4
SSD_GROUPS = 2
SSD_STATE = 128
SSD_CONV = 4
SSD_CONV_DIM = GROUP_WIDTH + 2 * SSD_GROUPS * SSD_STATE
SSD_CHUNK = 128
SSD_GROUP = 4
LOG2_E = math.log2(math.e)
RWKV_HEAD = 64
RWKV_HEADS = 4
RWKV_PROJ = 1024
RWKV_LN_EPS = 64e-5
RWKV_CHUNK = 64
RWKV_GROUP = 16
S5_GROUPS = 16
S5_STATE = 64
S5_WIDTH = S5_GROUPS * S5_STATE
POOL_WINDOWS = (2, 4, 8, 16)
POOL_CH = 64
POOL_BUF = 15
RMS_EPS = 1e-6
PAST_LEN = 16384

ROW_TILE = 1024
INPROJ_ROW_TILE = 1024
FFN_CHUNK = 256
TM_CHUNK = 128
POOL_CHUNK = 256
S5_SUB = 64
SSD_STEP_TILES = 16
RWKV_STEP_TILES = 16


def _cparams(*sem):
    return pltpu.CompilerParams(dimension_semantics=sem, vmem_limit_bytes=VMEM_LIMIT_BYTES)


def _dot(a, b, **kw):
    return jnp.dot(a, b, preferred_element_type=F32, **kw)


def _dot_nt(a, b):
    return lax.dot_general(a, b, (((1,), (1,)), ((), ())), preferred_element_type=F32)


def _dot_tn(a, b):
    return lax.dot_general(a, b, (((0,), (0,)), ((), ())), preferred_element_type=F32)


def _sigmoid(x):
    return 0.5 * jnp.tanh(0.5 * x) + 0.5


def _silu(x):
    h = 0.5 * x
    return h + h * jnp.tanh(h)


def _softplus(x):
    return jnp.maximum(x, 0.0) + jnp.log(1.0 + jnp.exp(-jnp.abs(x)))


def _gelu_tanh(x):
    c = math.sqrt(2.0 / math.pi)
    return x * (0.5 * (1.0 + jnp.tanh(c * (x + 0.044715 * (x * x * x)))))


def _rms(x, g):
    return x * lax.rsqrt(jnp.mean(x * x, axis=-1, keepdims=True) + RMS_EPS) * g


def _full_spec(shape):
    n = len(shape)
    return pl.BlockSpec(shape, lambda *_: (0,) * n)


class _Layered(tuple):
    pass


def _pspec(p, single=False):
    mode = pl.Buffered(1) if single else None
    if isinstance(p, _Layered):
        a, l = p
        return pl.BlockSpec((None,) + a.shape[1:], lambda *_: (l,) + (0,) * (a.ndim - 1), pipeline_mode=mode)
    n = p.ndim
    return pl.BlockSpec(p.shape, lambda *_: (0,) * n, pipeline_mode=mode)


def _parg(p):
    return p[0] if isinstance(p, _Layered) else p


def _mix_residual(x, y_refs, wmix_ref):
    for j, y_ref in enumerate(y_refs):
        x = x + _dot(y_ref[...].astype(BF16), wmix_ref[j * GROUP_WIDTH:(j + 1) * GROUP_WIDTH, :])
    return x


def _swiglu_chunk(h, wg, wu, wo):
    act = (_silu(_dot(h, wg)) * _dot(h, wu)).astype(BF16)
    return _dot(act, wo)


def _ffn_body(*refs, has_mix, final_norm):
    it = iter(refs)
    x = next(it)[...]
    if has_mix:
        y_refs = [next(it) for _ in range(4)]
        x = _mix_residual(x, y_refs, next(it))
    g_ref, wg_ref, wu_ref, wo_ref = next(it), next(it), next(it), next(it)
    gf_ref = next(it) if final_norm else None
    o_ref = next(it)
    h = _rms(x, g_ref[...]).astype(BF16)
    acc = jnp.zeros_like(x)
    for c in range(wo_ref.shape[0] // FFN_CHUNK):
        cols = slice(c * FFN_CHUNK, (c + 1) * FFN_CHUNK)
        acc = acc + _swiglu_chunk(h, wg_ref[:, cols], wu_ref[:, cols], wo_ref[cols, :])
    x = x + 0.5 * acc
    if final_norm:
        x = _rms(x, gf_ref[...])
    o_ref[...] = x


def _ffn(x, g, wg, wu, wo, mix=None, wmix=None, gf=None):
    rows, d = x.shape
    row_spec = lambda w: pl.BlockSpec((ROW_TILE, w), lambda i: (i, 0))
    args, specs = [x], [row_spec(d)]
    if mix is not None:
        for y in mix:
            args.append(y)
            specs.append(row_spec(y.shape[1]))
        args.append(_parg(wmix))
        specs.append(_pspec(wmix, single=True))
    for a in (g, wg, wu, wo) + ((gf,) if gf is not None else ()):
        args.append(_parg(a))
        specs.append(_pspec(a, single=True))
    return pl.pallas_call(
        functools.partial(_ffn_body, has_mix=mix is not None, final_norm=gf is not None),
        grid=(rows // ROW_TILE,),
        in_specs=specs,
        out_specs=row_spec(d),
        out_shape=jax.ShapeDtypeStruct((rows, d), F32),
        compiler_params=_cparams("parallel"),
        name="ffn",
    )(*args)


def _ffn_cast_body(*refs, has_mix, final_norm):
    it = iter(refs)
    x_ref = next(it)
    if has_mix:
        y_refs = [next(it) for _ in range(4)]
        wmix_ref = next(it)
    g_ref, wg_ref, wu_ref, wo_ref = next(it), next(it), next(it), next(it)
    gf_ref = next(it) if final_norm else None
    o_ref, wg_out, wu_out, wo_out, x_scr, h_scr, acc_scr = (next(it) for _ in range(7))
    c = pl.program_id(0)

    @pl.when(c == 0)
    def _():
        x = x_ref[...]
        if has_mix:
            x = _mix_residual(x, y_refs, wmix_ref)
        x_scr[...] = x
        h_scr[...] = _rms(x, g_ref[...]).astype(BF16)
        acc_scr[...] = jnp.zeros(acc_scr.shape, F32)

    wg = wg_ref[...].astype(BF16)
    wu = wu_ref[...].astype(BF16)
    wo = wo_ref[...].astype(BF16)
    wg_out[...] = wg
    wu_out[...] = wu
    wo_out[...] = wo
    acc_scr[...] += _swiglu_chunk(h_scr[...], wg, wu, wo)

    @pl.when(c == pl.num_programs(0) - 1)
    def _():
        x = x_scr[...] + 0.5 * acc_scr[...]
        if final_norm:
            x = _rms(x, gf_ref[...])
        o_ref[...] = x


def _ffn_cast(x, g, wi, wo, mix=None, wmix=None, gf=None):
    rows, d = x.shape
    wi_all, l = wi
    wo_all, _ = wo
    d_ff = wo_all.shape[1]
    nchunks = d_ff // FFN_CHUNK
    args, specs = [x], [_full_spec(x.shape)]
    if mix is not None:
        for y in mix:
            args.append(y)
            specs.append(_full_spec(y.shape))
        args.append(_parg(wmix))
        specs.append(_pspec(wmix, single=True))
    args += [_parg(g), wi_all, wi_all, wo_all]
    specs += [_pspec(g),
              pl.BlockSpec((None, d, FFN_CHUNK), lambda c: (l, 0, c)),
              pl.BlockSpec((None, d, FFN_CHUNK), lambda c: (l, 0, c + nchunks)),
              pl.BlockSpec((None, FFN_CHUNK, d), lambda c: (l, c, 0))]
    if gf is not None:
        args.append(gf)
        specs.append(_full_spec(gf.shape))
    col_spec = pl.BlockSpec((d, FFN_CHUNK), lambda c: (0, c))
    return pl.pallas_call(
        functools.partial(_ffn_cast_body, has_mix=mix is not None, final_norm=gf is not None),
        grid=(nchunks,),
        in_specs=specs,
        out_specs=[_full_spec(x.shape), col_spec, col_spec, pl.BlockSpec((FFN_CHUNK, d), lambda c: (c, 0))],
        out_shape=[jax.ShapeDtypeStruct((rows, d), F32), jax.ShapeDtypeStruct((d, d_ff), BF16),
                   jax.ShapeDtypeStruct((d, d_ff), BF16), jax.ShapeDtypeStruct((d_ff, d), BF16)],
        scratch_shapes=[pltpu.VMEM((rows, d), F32), pltpu.VMEM((rows, d), BF16), pltpu.VMEM((rows, d), F32)],
        compiler_params=_cparams("arbitrary"),
        name="ffn_cast",
    )(*args)


def _inproj_body(x_ref, g_ref, wt_ref, *o_refs):
    h = _rms(x_ref[...], g_ref[...]).astype(BF16)
    off = 0
    for o_ref in o_refs:
        n = o_ref.shape[-1]
        o_ref[...] = _dot_nt(h, wt_ref[off:off + n, :])
        off += n


def _inproj_cast_body(x_ref, g_ref, win_ref, *o_refs, layer):
    *proj_refs, wall_ref = o_refs
    split = GROUP_WIDTH + SSD_CONV_DIM
    wt = win_ref[:, layer, :]
    tail = wt.shape[0] - split - SSD_HEADS
    wall_ref[0:split, :] = wt[0:split].astype(BF16)
    wall_ref[split:split + tail, :] = wt[split + SSD_HEADS:].astype(BF16)
    dt_rows = jnp.concatenate([wt[split:split + SSD_HEADS], jnp.zeros((LANES - SSD_HEADS, wt.shape[1]), F32)], axis=0)
    wall_ref[split + tail:, :] = dt_rows.astype(BF16)
    _inproj_body(x_ref, g_ref, wall_ref, *proj_refs)


def _inproj_cast(x, g, w_in, widths):
    rows, d = x.shape
    wt_all, l = w_in
    outs = pl.pallas_call(
        functools.partial(_inproj_cast_body, layer=l),
        grid=(1,),
        in_specs=[_full_spec(x.shape), _pspec(g),
                  pl.BlockSpec(wt_all.shape, lambda i: (0, 0, 0), pipeline_mode=pl.Buffered(1))],
        out_specs=[_full_spec((rows, n)) for n in widths] + [_full_spec((sum(widths), d))],
        out_shape=[jax.ShapeDtypeStruct((rows, n), F32) for n in widths]
                  + [jax.ShapeDtypeStruct((sum(widths), d), BF16)],
        compiler_params=_cparams("arbitrary"),
        name="inproj_cast",
    )(x, _parg(g), wt_all)
    return outs[:-1], outs[-1]


def _inproj(x, g, w, widths):
    rows, d = x.shape
    tile = INPROJ_ROW_TILE
    row_spec = lambda w_: pl.BlockSpec((tile, w_), lambda i: (i, 0))
    return pl.pallas_call(
        _inproj_body,
        grid=(rows // tile,),
        in_specs=[row_spec(d), _pspec(g), _pspec(w, single=True)],
        out_specs=[row_spec(n) for n in widths],
        out_shape=[jax.ShapeDtypeStruct((rows, n), F32) for n in widths],
        compiler_params=_cparams("parallel"),
        name="inproj",
    )(x, _parg(g), _parg(w))


def _ssd_body(z_ref, xbc_ref, dt_ref, cw_ref, cb_ref, dtb_ref, alog_ref, dsk_ref, ng_ref,
              y_ref, conv_ref, hout_ref, xpad_scr, h_scr, *, chunk, group):
    L, G = chunk, group
    GL = G * L
    c = pl.program_id(1)
    pad = SUBLANES
    halo = SSD_CONV - 1
    hpg = SSD_HEADS // SSD_GROUPS
    assert hpg == 2 and hpg * SSD_HEAD_DIM == SSD_STATE

    @pl.when(c == 0)
    def _():
        xpad_scr[0:pad, :] = jnp.zeros((pad, SSD_CONV_DIM), F32)
        h_scr[...] = jnp.zeros(h_scr.shape, F32)

    xpad_scr[pad:pad + GL, :] = xbc_ref[...]
    xfull = xpad_scr[...]
    conv = cb_ref[...] + cw_ref[halo:halo + 1, :] * xfull[pad:pad + GL]
    for j in range(halo):
        conv = conv + cw_ref[j:j + 1, :] * pltpu.roll(xfull, halo - j, axis=0)[pad:pad + GL]
    xpad_scr[pad - halo:pad, :] = xpad_scr[pad + GL - halo:pad + GL, :]
    conv = _silu(conv)
    xs = conv[:, 0:GROUP_WIDTH]
    bm = conv[:, GROUP_WIDTH:2 * GROUP_WIDTH].astype(BF16)
    cm = conv[:, 2 * GROUP_WIDTH:3 * GROUP_WIDTH].astype(BF16)

    row = lax.broadcasted_iota(jnp.int32, (L, L), 0)
    col = lax.broadcasted_iota(jnp.int32, (L, L), 1)
    causal = row >= col
    tril = jnp.where(causal, 1.0, 0.0).astype(F32)
    dt = _softplus(dt_ref[...] + dtb_ref[...])
    da = dt * (-jnp.exp(alog_ref[...]) * LOG2_E)
    acs = [_dot(tril, da[i * L:(i + 1) * L, :], precision=HIGHEST) for i in range(G)]
    acs_t = [a.T for a in acs]
    e_acs = [jnp.exp2(a) for a in acs]
    e_end = [jnp.exp2(a[L - 1:L, :] - a) for a in acs]
    e_last = [jnp.exp2(a[L - 1:L, :]) for a in acs]

    keys = [(i, g) for i in range(G) for g in range(SSD_GROUPS)]
    rows_of = lambda x, i: x[i * L:(i + 1) * L]
    lanes_of = lambda x, g: x[:, g * SSD_STATE:(g + 1) * SSD_STATE]
    lane_lo = lax.broadcasted_iota(jnp.int32, (L, hpg * SSD_HEAD_DIM), 1) < SSD_HEAD_DIM
    row_lo = lax.broadcasted_iota(jnp.int32, (hpg * SSD_HEAD_DIM, SSD_STATE), 0) < SSD_HEAD_DIM
    head_cols = lambda a, g: jnp.where(lane_lo, a[:, g * hpg:g * hpg + 1], a[:, g * hpg + 1:g * hpg + 2])
    bg = {(i, g): lanes_of(rows_of(bm, i), g) for i, g in keys}
    cg = {(i, g): lanes_of(rows_of(cm, i), g) for i, g in keys}
    scores = {k: _dot_nt(cg[k], bg[k]) for k in keys}
    xdt = {(i, g): lanes_of(rows_of(xs, i), g) * head_cols(rows_of(dt, i), g) for i, g in keys}
    decay = {(i, h): jnp.exp2(jnp.where(causal, acs[i][:, h:h + 1] - acs_t[i][h:h + 1, :], -jnp.inf))
             for i in range(G) for h in range(SSD_HEADS)}
    p_mat = {(i, g): jnp.concatenate([(scores[(i, g)] * decay[(i, g * hpg + k)]).astype(BF16) for k in range(hpg)],
                                     axis=1) for i, g in keys}
    y_in = {k: _dot(p_mat[k], _bd(xdt[k].astype(BF16))) for k in keys}
    st = {(i, g): _dot_tn((xdt[(i, g)] * head_cols(e_end[i], g)).astype(BF16), bg[(i, g)]) for i, g in keys}

    y_rows = []
    for i in range(G):
        ys = []
        for g in range(SSD_GROUPS):
            h_prev = h_scr[g * hpg:(g + 1) * hpg].reshape(hpg * SSD_HEAD_DIM, SSD_STATE)
            ys.append(y_in[(i, g)] + _dot_nt(cg[(i, g)], h_prev.astype(BF16)) * head_cols(e_acs[i], g))
            keep = jnp.where(row_lo, e_last[i][:, g * hpg:g * hpg + 1], e_last[i][:, g * hpg + 1:g * hpg + 2])
            h_scr[g * hpg:(g + 1) * hpg] = (h_prev * keep + st[(i, g)]).reshape(hpg, SSD_HEAD_DIM, SSD_STATE)
        y_rows.append(jnp.concatenate(ys, axis=-1))
    y = jnp.concatenate(y_rows, axis=0) + xs * dsk_ref[...]
    y = y * _silu(z_ref[...])
    y_ref[...] = _rms(y, ng_ref[...])

    @pl.when(c == pl.num_programs(1) - 1)
    def _():
        hout_ref[0] = h_scr[...]
        conv_ref[0] = xpad_scr[pad - halo:pad, :]


def _ssd(z, xbc, dtr, lp, *, batch, seq):
    chunk = SSD_CHUNK
    rows = chunk * SSD_GROUP
    nc = seq // rows
    rspec = lambda w: pl.BlockSpec((rows, w), lambda b, c: (b * nc + c, 0))
    consts = (lp["conv_w"], lp["conv_b"], lp["dt_bias"], lp["a_log"], lp["d_skip"], lp["ssd_norm"])
    return pl.pallas_call(
        functools.partial(_ssd_body, chunk=chunk, group=SSD_GROUP),
        grid=(batch, nc),
        in_specs=[rspec(GROUP_WIDTH), rspec(SSD_CONV_DIM), rspec(LANES)] + [_pspec(a) for a in consts],
        out_specs=[rspec(GROUP_WIDTH),
                   pl.BlockSpec((1, SSD_CONV - 1, SSD_CONV_DIM), lambda b, c: (b, 0, 0)),
                   pl.BlockSpec((1, SSD_HEADS, SSD_HEAD_DIM, SSD_STATE), lambda b, c: (b, 0, 0, 0))],
        out_shape=[jax.ShapeDtypeStruct((batch * seq, GROUP_WIDTH), F32),
                   jax.ShapeDtypeStruct((batch, SSD_CONV - 1, SSD_CONV_DIM), F32),
                   jax.ShapeDtypeStruct((batch, SSD_HEADS, SSD_HEAD_DIM, SSD_STATE), F32)],
        scratch_shapes=[pltpu.VMEM((SUBLANES + rows, SSD_CONV_DIM), F32),
                        pltpu.VMEM((SSD_HEADS, SSD_HEAD_DIM, SSD_STATE), F32)],
        compiler_params=_cparams("parallel", "arbitrary"),
        name="ssd",
    )(z, xbc, dtr, *[_parg(a) for a in consts])


def _ssd_step_body(z_ref, xbc_ref, dt_ref, conv0_ref, h0_ref, *rest, seq, batch, layer):
    hdone_ref, rest = (rest[0], rest[1:]) if layer else (None, rest)
    (cw_ref, cb_ref, dtb_ref, aneg_ref, dsk_ref, ng_ref, hexp_ref, y_ref, conv_ref, hout_ref,
     xs_scr, bm_scr, cm_scr, xdt_scr, dec_scr, y_scr) = rest
    T, B = seq, batch
    if layer:
        hout_ref[0:layer] = hdone_ref[...]
    GW = GROUP_WIDTH
    j = pl.program_id(0)
    tiles = SSD_STEP_TILES

    @pl.when(j == 0)
    def _():
        rows = [conv0_ref[i] for i in range(SSD_CONV - 1)]
        rows += [xbc_ref[t * B:(t + 1) * B, :] for t in range(T)]
        for t in range(T):
            conv = cb_ref[...] + cw_ref[0:1, :] * rows[t]
            for i in range(1, SSD_CONV):
                conv = conv + cw_ref[i:i + 1, :] * rows[t + i]
            conv = _silu(conv)
            xs = conv[:, 0:GW]
            xs_scr[t] = xs
            for g in range(SSD_GROUPS):
                bm_scr[t, g] = conv[:, GW + g * SSD_STATE:GW + (g + 1) * SSD_STATE].T
                cm_scr[t, g] = conv[:, 2 * GW + g * SSD_STATE:2 * GW + (g + 1) * SSD_STATE].T
            dt = _softplus(dt_ref[t * B:(t + 1) * B, :] + dtb_ref[...])
            dte = _dot(dt, hexp_ref[...], precision=HIGHEST)
            xdt_scr[t] = (xs * dte).T
            dec_scr[t] = jnp.exp(dte * aneg_ref[...]).T
        for i in range(SSD_CONV - 1):
            conv_ref[i] = rows[T + i]

    hp0 = j * tiles
    grp = hp0 // (SSD_HEAD_DIM * (SSD_HEADS // SSD_GROUPS))
    for q in range(tiles):
        hp = pl.ds(hp0 + q, 1)
        h = h0_ref[:, q, :].T
        for t in range(T):
            h = h * dec_scr[t, hp, :] + bm_scr[t, grp] * xdt_scr[t, hp, :]
            y_scr[t, hp, :] = jnp.sum(h * cm_scr[t, grp], axis=0, keepdims=True)
        hout_ref[layer, :, q, :] = h.T

    @pl.when(j == pl.num_programs(0) - 1)
    def _():
        for t in range(T):
            y = y_scr[t].T + xs_scr[t] * dsk_ref[...]
            y = y * _silu(z_ref[t * B:(t + 1) * B, :])
            y_ref[t * B:(t + 1) * B, :] = _rms(y, ng_ref[...])


def _layer_state_specs(layer, block, axis):
    idx = lambda first: (lambda j: (first,) + tuple(j if a == axis else 0 for a in range(len(block))))
    cur = pl.BlockSpec((None,) + block, idx(layer))
    prev = [pl.BlockSpec((layer,) + block, idx(0))] if layer else []
    out = pl.BlockSpec((layer + 1,) + block, idx(0))
    return cur, prev, out


def _ssd_step(z, xbc, dtr, conv_all, h_all, h_done, lp, *, batch, seq, layer):
    n = batch * seq
    srows = SSD_HEADS * SSD_HEAD_DIM
    consts = (lp["conv_w"], lp["conv_b"], lp["dt_bias"], lp["a_neg_exp"], lp["d_skip"], lp["ssd_norm"], lp["head_expand"])
    hspec, prev_specs, hout_spec = _layer_state_specs(layer, (batch, SSD_STEP_TILES, SSD_STATE), 1)
    prev_args = [h_done] if layer else []
    cshape = (SSD_CONV - 1, batch, SSD_CONV_DIM)
    return pl.pallas_call(
        functools.partial(_ssd_step_body, seq=seq, batch=batch, layer=layer),
        grid=(srows // SSD_STEP_TILES,),
        in_specs=[_full_spec((n, GROUP_WIDTH)), _full_spec((n, SSD_CONV_DIM)), _full_spec((n, LANES)),
                  pl.BlockSpec((None,) + cshape, lambda j: (layer, 0, 0, 0)), hspec] + prev_specs
                 + [_pspec(a) for a in consts],
        out_specs=[_full_spec((n, GROUP_WIDTH)), _full_spec(cshape), hout_spec],
        out_shape=[jax.ShapeDtypeStruct((n, GROUP_WIDTH), F32),
                   jax.ShapeDtypeStruct(cshape, F32),
                   jax.ShapeDtypeStruct((layer + 1, batch, srows, SSD_STATE), F32)],
        scratch_shapes=[pltpu.VMEM((seq, batch, GROUP_WIDTH), F32),
                        pltpu.VMEM((seq, SSD_GROUPS, SSD_STATE, batch), F32),
                        pltpu.VMEM((seq, SSD_GROUPS, SSD_STATE, batch), F32),
                        pltpu.VMEM((seq, GROUP_WIDTH, batch), F32),
                        pltpu.VMEM((seq, GROUP_WIDTH, batch), F32),
                        pltpu.VMEM((seq, GROUP_WIDTH, batch), F32)],
        compiler_params=_cparams("arbitrary"),
        name="ssd_step",
    )(z, xbc, dtr, conv_all, h_all.reshape(h_all.shape[0], batch, srows, SSD_STATE),
      *prev_args, *[_parg(a) for a in consts])


PAIR = 2 * RWKV_HEAD
RWKV_PAIRS = RWKV_HEADS // 2


def _bd(x):
    half = x.shape[1] // 2
    lane = lax.broadcasted_iota(jnp.int32, x.shape, 1)
    zero = jnp.zeros_like(x)
    return jnp.concatenate([jnp.where(lane < half, x, zero), jnp.where(lane >= half, x, zero)], axis=0)


def _half_sums(x, lo):
    s_lo = jnp.sum(jnp.where(lo, x, 0.0), axis=-1, keepdims=True)
    s_hi = jnp.sum(jnp.where(lo, 0.0, x), axis=-1, keepdims=True)
    return jnp.where(lo, s_lo, s_hi)


def _head_sum(x):
    lo = lax.broadcasted_iota(jnp.int32, (x.shape[0], PAIR), 1) < RWKV_HEAD
    return jnp.concatenate([_half_sums(x[:, p * PAIR:(p + 1) * PAIR], lo) for p in range(RWKV_PAIRS)], axis=-1)


def _rwkv_pointwise(u, prev, mu_ref, w0_ref, w2_ref, a0_ref, a2_ref, g2_ref, kk_ref, ka_ref):
    GW = GROUP_WIDTH
    xs = u + (prev - u) * mu_ref[...]
    r = xs[:, 0:GW]
    k = xs[:, GW:2 * GW]
    v = xs[:, 2 * GW:3 * GW]
    wd = xs[:, 3 * GW:3 * GW + 64]
    ad = xs[:, 3 * GW + 64:3 * GW + 128]
    gd = xs[:, 3 * GW + 128:3 * GW + 256]
    w_lin = w0_ref[...] + _dot(jnp.tanh(wd).astype(BF16), w2_ref[...])
    logdecay = -math.exp(-0.5) * _sigmoid(w_lin)
    a = _sigmoid(a0_ref[...] + _dot(ad.astype(BF16), a2_ref[...]))
    g = _dot(_sigmoid(gd).astype(BF16), g2_ref[...])
    kk = k * kk_ref[...]
    kk = kk * lax.rsqrt(jnp.maximum(_head_sum(kk * kk), 1e-24))
    k = k * (1.0 + (a - 1.0) * ka_ref[...])
    return r, k, v, logdecay, a, g, kk


def _rwkv_finish(y, r, k, v, g, rk_ref, lng_ref, lnb_ref):
    mean = _head_sum(y) * (1.0 / RWKV_HEAD)
    yc = y - mean
    var = _head_sum(yc * yc) * (1.0 / RWKV_HEAD)
    y = yc * lax.rsqrt(var + RWKV_LN_EPS) * lng_ref[...] + lnb_ref[...]
    bonus = _head_sum(r * k * rk_ref[...]) * v
    return (y + bonus) * g


def _rwkv_body(u_ref, mu_ref, w0_ref, w2_ref, a0_ref, a2_ref, g2_ref, kk_ref, ka_ref, rk_ref,
               lng_ref, lnb_ref, y_ref, shift_ref, sout_ref, upad_scr, s_scr, *, chunk, group):
    L, G = chunk, group
    GL = G * L
    c = pl.program_id(1)
    pad = SUBLANES

    @pl.when(c == 0)
    def _():
        upad_scr[0:pad, :] = jnp.zeros((pad, RWKV_PROJ), F32)
        s_scr[...] = jnp.zeros(s_scr.shape, F32)

    u = u_ref[...]
    upad_scr[pad:pad + GL, :] = u
    prev = pltpu.roll(upad_scr[...], 1, axis=0)[pad:pad + GL]
    upad_scr[pad - 1:pad, :] = u[GL - 1:GL, :]
    r, k, v, logdecay, a, g, kk = _rwkv_pointwise(u, prev, mu_ref, w0_ref, w2_ref, a0_ref, a2_ref, g2_ref,
                                                  kk_ref, ka_ref)

    tril = jnp.where(lax.broadcasted_iota(jnp.int32, (L, L), 0) >= lax.broadcasted_iota(jnp.int32, (L, L), 1),
                     1.0, 0.0).astype(F32)
    cl = jnp.concatenate([_dot(tril, logdecay[i * L:(i + 1) * L, :], precision=HIGHEST) for i in range(G)], axis=0)
    e_in = jnp.exp(cl)
    e_inv = jnp.exp(-cl)
    r_t = r * e_in
    r_tb = r_t.astype(BF16)
    a_tb = (-kk * jnp.exp(cl - logdecay)).astype(BF16)
    b_tb = (kk * a * e_inv).astype(BF16)
    k_tb = (k * e_inv).astype(BF16)
    vb = v.astype(BF16)

    row = lax.broadcasted_iota(jnp.int32, (L, PAIR), 0)
    colh = lax.broadcasted_iota(jnp.int32, (L, PAIR), 1) & (RWKV_HEAD - 1)
    strict = row > colh
    incl = row >= colh
    eye_pair = jnp.where(row == colh, 1.0, 0.0).astype(F32)
    lane_lo = lax.broadcasted_iota(jnp.int32, (RWKV_HEAD, PAIR), 1) < RWKV_HEAD
    same_head = (lax.broadcasted_iota(jnp.int32, (PAIR, PAIR), 0) < RWKV_HEAD) == \
                (lax.broadcasted_iota(jnp.int32, (PAIR, PAIR), 1) < RWKV_HEAD)

    streams = [(i, p) for i in range(G) for p in range(RWKV_PAIRS)]
    ns = len(streams)
    blk = lambda x, i, p: x[i * L:(i + 1) * L, p * PAIR:(p + 1) * PAIR]
    lhs = [jnp.concatenate([blk(a_tb, i, p), blk(r_tb, i, p)], axis=0) for i, p in streams]
    m_both = [_dot_nt(lhs[s], jnp.concatenate([_bd(blk(b_tb, i, p)), _bd(blk(k_tb, i, p))], axis=0))
              for s, (i, p) in enumerate(streams)]
    m_ab = [m[:, 0:PAIR] for m in m_both]
    m_ak = [m[:, PAIR:2 * PAIR] for m in m_both]
    n_ab = [jnp.where(strict, m[0:L], 0.0) for m in m_ab]
    m_rb = [jnp.where(incl, m[L:2 * L], 0.0).astype(BF16) for m in m_ab]
    n_ak = [jnp.where(strict, m[0:L], 0.0).astype(BF16) for m in m_ak]
    m_rk = [jnp.where(incl, m[L:2 * L], 0.0).astype(BF16) for m in m_ak]
    tinv = [eye_pair + n for n in n_ab]
    pwb = [n.astype(BF16) for n in n_ab]
    pw = [_dot(x, _bd(x)) for x in pwb]
    for _ in range(int(math.log2(L)) - 2):
        pwb = [x.astype(BF16) for x in pw]
        both = [_dot(jnp.concatenate([pwb[s], tinv[s].astype(BF16)], axis=0), _bd(pwb[s])) for s in range(ns)]
        pw = [x[0:L] for x in both]
        tinv = [tinv[s] + both[s][L:2 * L] for s in range(ns)]
    pwb = [x.astype(BF16) for x in pw]
    tinv = [tinv[s] + _dot(tinv[s].astype(BF16), _bd(pwb[s])) for s in range(ns)]
    tinvb = [x.astype(BF16) for x in tinv]
    nv_mv = [_dot(jnp.concatenate([n_ak[s], m_rk[s]], axis=0), _bd(blk(vb, i, p))) for s, (i, p) in enumerate(streams)]
    wu = [_dot(tinvb[s], jnp.concatenate([_bd(blk(a_tb, i, p)), _bd(nv_mv[s][0:L].astype(BF16))], axis=1))
          for s, (i, p) in enumerate(streams)]
    wub = [x.astype(BF16) for x in wu]
    qy = [_dot(m_rb[s], jnp.concatenate([_bd(wub[s][:, 0:PAIR]), _bd(wub[s][:, PAIR:2 * PAIR])], axis=1))
          for s in range(ns)]
    q = [(blk(r_t, i, p) + qy[s][:, 0:PAIR]).astype(BF16) for s, (i, p) in enumerate(streams)]
    y_loc = [qy[s][:, PAIR:2 * PAIR] + nv_mv[s][L:2 * L] for s in range(ns)]
    zeros_b = jnp.zeros((L, PAIR), BF16)
    mg = [_dot_tn(jnp.concatenate([wub[s], jnp.concatenate([zeros_b, blk(vb, i, p)], axis=1)], axis=0),
                  jnp.concatenate([blk(b_tb, i, p), blk(k_tb, i, p)], axis=0))
          for s, (i, p) in enumerate(streams)]
    p_end = [e_in[(i + 1) * L - 1:(i + 1) * L, p * PAIR:(p + 1) * PAIR] for i, p in streams]
    m_t = [(jnp.where(same_head, mg[s][0:PAIR], 0.0) * p_end[s]).astype(BF16) for s in range(ns)]
    g_t = [jnp.where(lane_lo, mg[s][PAIR:PAIR + RWKV_HEAD], mg[s][PAIR + RWKV_HEAD:2 * PAIR]) * p_end[s]
           for s in range(ns)]

    y_rows = []
    for i in range(G):
        y_pairs = []
        for p in range(RWKV_PAIRS):
            s = i * RWKV_PAIRS + p
            s0 = s_scr[p]
            s0b = s0.astype(BF16)
            y_pairs.append(_dot_nt(q[s], _bd(s0b)) + y_loc[s])
            s_scr[p] = s0 * p_end[s] + _dot(s0b, m_t[s]) + g_t[s]
        y_rows.append(jnp.concatenate(y_pairs, axis=-1))
    y = jnp.concatenate(y_rows, axis=0)
    y_ref[...] = _rwkv_finish(y, r, k, v, g, rk_ref, lng_ref, lnb_ref)

    @pl.when(c == pl.num_programs(1) - 1)
    def _():
        sout_ref[0] = s_scr[...]
        shift_ref[0] = upad_scr[pad - 1:pad, :]


_RWKV_PARAM_NAMES = ("mu", "w0", "w2", "a0", "a2", "g2", "k_k", "k_a", "r_k", "ln_g", "ln_b")


def _rwkv(u, p, *, batch, seq):
    rows = RWKV_CHUNK * RWKV_GROUP
    nc = seq // rows
    params = [p[n] for n in _RWKV_PARAM_NAMES]
    sspec = pl.BlockSpec((1, RWKV_PAIRS, RWKV_HEAD, PAIR), lambda b, c: (b, 0, 0, 0))
    y, shift, s_last = pl.pallas_call(
        functools.partial(_rwkv_body, chunk=RWKV_CHUNK, group=RWKV_GROUP),
        grid=(batch, nc),
        in_specs=[pl.BlockSpec((rows, RWKV_PROJ), lambda b, c: (b * nc + c, 0))] + [_pspec(a) for a in params],
        out_specs=[pl.BlockSpec((rows, GROUP_WIDTH), lambda b, c: (b * nc + c, 0)),
                   pl.BlockSpec((1, 1, RWKV_PROJ), lambda b, c: (b, 0, 0)), sspec],
        out_shape=[jax.ShapeDtypeStruct((batch * seq, GROUP_WIDTH), F32),
                   jax.ShapeDtypeStruct((batch, 1, RWKV_PROJ), F32),
                   jax.ShapeDtypeStruct((batch, RWKV_PAIRS, RWKV_HEAD, PAIR), F32)],
        scratch_shapes=[pltpu.VMEM((SUBLANES + rows, RWKV_PROJ), F32),
                        pltpu.VMEM((RWKV_PAIRS, RWKV_HEAD, PAIR), F32)],
        compiler_params=_cparams("parallel", "arbitrary"),
        name="rwkv",
    )(u, *[_parg(a) for a in params])
    s_last = s_last.reshape(batch, RWKV_PAIRS, RWKV_HEAD, 2, RWKV_HEAD).transpose(0, 1, 3, 2, 4).reshape(
        batch, RWKV_HEADS, RWKV_HEAD, RWKV_HEAD)
    return y, shift.reshape(batch, RWKV_PROJ), s_last


def _rwkv_step_body(u_ref, shift0_ref, s0_ref, *rest, seq, batch, layer):
    sdone_ref, rest = (rest[0], rest[1:]) if layer else (None, rest)
    (mu_ref, w0_ref, w2_ref, a0_ref, a2_ref, g2_ref, kk_ref, ka_ref, rk_ref, lng_ref, lnb_ref, y_ref, sout_ref,
     r_scr, w_scr, k_scr, b_scr, nkk_scr, v_scr, y_scr) = rest
    T, B = seq, batch
    j = pl.program_id(0)
    if layer:
        sout_ref[0:layer] = sdone_ref[...]
    tiles = RWKV_STEP_TILES

    def pointwise(t):
        u = u_ref[t * B:(t + 1) * B, :]
        prev = shift0_ref[...] if t == 0 else u_ref[(t - 1) * B:t * B, :]
        return _rwkv_pointwise(u, prev, mu_ref, w0_ref, w2_ref, a0_ref, a2_ref, g2_ref, kk_ref, ka_ref)

    @pl.when(j == 0)
    def _():
        for t in range(T):
            r, k, v, logdecay, a, _, kk = pointwise(t)
            r_scr[t] = r.T
            w_scr[t] = jnp.exp(logdecay).T
            k_scr[t] = k.T
            b_scr[t] = (kk * a).T
            nkk_scr[t] = (-kk).T
            v_scr[t] = v.T

    i0 = j * tiles
    keys = pl.ds(pl.multiple_of((i0 // RWKV_HEAD) * RWKV_HEAD, RWKV_HEAD), RWKV_HEAD)
    for q in range(tiles):
        vi = pl.ds(i0 + q, 1)
        s = s0_ref[q]
        for t in range(T):
            sa = jnp.sum(s * nkk_scr[t, keys, :], axis=0, keepdims=True)
            s = s * w_scr[t, keys, :] + k_scr[t, keys, :] * v_scr[t, vi, :] + b_scr[t, keys, :] * sa
            y_scr[t, vi, :] = jnp.sum(s * r_scr[t, keys, :], axis=0, keepdims=True)
        sout_ref[layer, q] = s

    @pl.when(j == pl.num_programs(0) - 1)
    def _():
        for t in range(T):
            r, k, v, _, _, g, _ = pointwise(t)
            y_ref[t * B:(t + 1) * B, :] = _rwkv_finish(y_scr[t].T, r, k, v, g, rk_ref, lng_ref, lnb_ref)


def _rwkv_step(u, shift0, s_all, s_done, p, *, batch, seq, layer):
    n = batch * seq
    srows = RWKV_HEADS * RWKV_HEAD
    params = [p[nm] for nm in _RWKV_PARAM_NAMES]
    sspec, prev_specs, sout_spec = _layer_state_specs(layer, (RWKV_STEP_TILES, RWKV_HEAD, batch), 0)
    prev_args = [s_done] if layer else []
    tposed = pltpu.VMEM((seq, GROUP_WIDTH, batch), F32)
    return pl.pallas_call(
        functools.partial(_rwkv_step_body, seq=seq, batch=batch, layer=layer),
        grid=(srows // RWKV_STEP_TILES,),
        in_specs=[_full_spec((n, RWKV_PROJ)), _full_spec((batch, RWKV_PROJ)), sspec] + prev_specs
                 + [_pspec(a) for a in params],
        out_specs=[_full_spec((n, GROUP_WIDTH)), sout_spec],
        out_shape=[jax.ShapeDtypeStruct((n, GROUP_WIDTH), F32),
                   jax.ShapeDtypeStruct((layer + 1, srows, RWKV_HEAD, batch), F32)],
        scratch_shapes=[tposed] * 7,
        compiler_params=_cparams("arbitrary"),
        name="rwkv_step",
    )(u, shift0, s_all, *prev_args, *[_parg(a) for a in params])


def _s5_body(u_ref, hre0_ref, him0_ref, are_ref, aim_ref, bmat_ref, cmat_ref, d_ref, gw_ref, gb_ref,
             y_ref, hre_ref, him_ref, hs_scr, tm_scr, *, steps, batch_major):
    c = pl.program_id(1)
    ns = S5_WIDTH
    bsub = hre_ref.shape[0]

    @pl.when(c == 0)
    def _():
        hre_ref[...] = hre0_ref[...]
        him_ref[...] = him0_ref[...]

    if batch_major:
        for b in range(bsub):
            tm_scr[:, b, :] = u_ref[b]
        u = tm_scr[...].reshape(steps * bsub, GROUP_WIDTH)
    else:
        u = u_ref[...].reshape(steps * bsub, GROUP_WIDTH)
    are = jnp.broadcast_to(are_ref[...], (bsub, ns))
    aim = jnp.broadcast_to(aim_ref[...], (bsub, ns))
    hre, him = hre_ref[...], him_ref[...]
    sub = min(S5_SUB, steps)
    rows = sub * bsub
    outs = []
    hs_scr[...] = _dot(u.astype(BF16), bmat_ref[...])
    for k in range(steps // sub):
        r0 = k * rows
        u_k = u[r0:r0 + rows]
        for t in range(sub):
            rs = slice(r0 + t * bsub, r0 + (t + 1) * bsub)
            hre, him = (are * hre - aim * him + hs_scr[rs, 0:ns], are * him + aim * hre + hs_scr[rs, ns:2 * ns])
            hs_scr[rs, 0:ns] = hre
            hs_scr[rs, ns:2 * ns] = him
        y = _dot(hs_scr[r0:r0 + rows, :].astype(BF16), cmat_ref[...]) + u_k * d_ref[...]
        y = _gelu_tanh(y)
        yy = _dot(y.astype(BF16), gw_ref[...]) + gb_ref[...]
        outs.append(yy[:, 0:GROUP_WIDTH] * _sigmoid(yy[:, GROUP_WIDTH:2 * GROUP_WIDTH]))
    hre_ref[...] = hre
    him_ref[...] = him
    out = jnp.concatenate(outs, axis=0).reshape(steps, bsub, GROUP_WIDTH)
    if batch_major:
        tm_scr[...] = out
        for b in range(bsub):
            y_ref[b] = tm_scr[:, b, :]
    else:
        y_ref[...] = out


def _time_specs(u, batch_major, chunk):
    if batch_major:
        batch, seq, _ = u.shape
        steps = min(chunk, seq)
        bsub = SUBLANES
        spec = pl.BlockSpec((bsub, steps, GROUP_WIDTH), lambda b, c: (b, c, 0))
    else:
        seq, batch, _ = u.shape
        steps = min(chunk, seq)
        bsub = min(batch, SUBLANES * max(1, chunk // steps))
        spec = pl.BlockSpec((steps, bsub, GROUP_WIDTH), lambda b, c: (c, b, 0))
    return batch, seq, steps, bsub, spec


def _s5(u, hre0, him0, lp, *, batch_major):
    batch, seq, steps, bsub, tspec = _time_specs(u, batch_major, TM_CHUNK)
    hspec =pl.BlockSpec((bsub, S5_WIDTH), lambda b, c: (b, 0))
    consts = (lp["s5_are"], lp["s5_aim"], lp["s5_bmat"], lp["s5_cmat"], lp["s5_d"], lp["s5_gw"], lp["s5_gb"])
    return pl.pallas_call(
        functools.partial(_s5_body, steps=steps, batch_major=batch_major),
        grid=(batch // bsub, seq // steps),
        in_specs=[tspec, hspec, hspec] + [_pspec(a) for a in consts],
        out_specs=[tspec, hspec, hspec],
        out_shape=[jax.ShapeDtypeStruct(u.shape, F32),
                   jax.ShapeDtypeStruct((batch, S5_WIDTH), F32),
                   jax.ShapeDtypeStruct((batch, S5_WIDTH), F32)],
        scratch_shapes=[pltpu.VMEM((steps * bsub, 2 * S5_WIDTH), F32),
                        pltpu.VMEM((steps, bsub, GROUP_WIDTH), F32)],
        compiler_params=_cparams("parallel", "arbitrary"),
        name="s5",
    )(u, hre0, him0, *[_parg(a) for a in consts])


def _pool_body(u_ref, buf0_ref, pw_ref, sc_ref, y_ref, buf_ref, f_scr, tm_scr, *, steps, pos0, batch_major):
    c = pl.program_id(1)
    bsub = f_scr.shape[1]
    GW = GROUP_WIDTH
    halo = POOL_BUF + 1

    @pl.when(c == 0)
    def _():
        f_scr[0] = jnp.zeros((bsub, GW), F32)
        f_scr[1:halo] = buf0_ref[...]

    if batch_major:
        for b in range(bsub):
            f_scr[halo:halo + steps, b, :] = u_ref[b]
    else:
        f_scr[halo:halo + steps] = u_ref[...]
    f = f_scr[...]
    u = f[halo:halo + steps]
    s2 = f[1:] + f[:-1]
    s4 = s2[2:] + s2[:-2]
    s8 = s4[4:] + s4[:-4]
    s16 = s8[8:] + s8[:-8]
    f_scr[0:halo] = f[steps:steps + halo]
    lane = lax.broadcasted_iota(jnp.int32, (steps, bsub, GW), 2)
    tpos = lax.broadcasted_iota(jnp.int32, (steps, bsub, GW), 0) + (pos0 + 1) + c * steps
    win = jnp.where(lane < POOL_CH, s2[halo - 1:halo - 1 + steps],
                    jnp.where(lane < 2 * POOL_CH, s4[halo - 3:halo - 3 + steps],
                              jnp.where(lane < 3 * POOL_CH, s8[halo - 7:halo - 7 + steps],
                                        s16[halo - 15:halo - 15 + steps])))
    wlen = jnp.where(lane < POOL_CH, POOL_WINDOWS[0],
                     jnp.where(lane < 2 * POOL_CH, POOL_WINDOWS[1],
                               jnp.where(lane < 3 * POOL_CH, POOL_WINDOWS[2], POOL_WINDOWS[3])))
    cnt = jnp.minimum(tpos, wlen).astype(F32)
    pooled = (win / cnt - u).reshape(steps * bsub, GW)
    y = (_dot(pooled.astype(BF16), pw_ref[...]) * sc_ref[...]).reshape(steps, bsub, GW)
    if batch_major:
        tm_scr[...] = y
        for b in range(bsub):
            y_ref[b] = tm_scr[:, b, :]
    else:
        y_ref[...] = y

    @pl.when(c == pl.num_programs(1) - 1)
    def _():
        buf_ref[...] = f_scr[1:halo]


def _pool(u, buf0, lp, *, pos0, batch_major, layer=None):
    batch, seq, steps, bsub, tspec = _time_specs(u, batch_major, POOL_CHUNK)
    bblock =(POOL_BUF, bsub, GROUP_WIDTH)
    bspec = pl.BlockSpec(bblock, lambda b, c: (0, b, 0))
    if layer is None:
        bspec_in = bspec
    else:
        bspec_in = pl.BlockSpec((None,) + bblock, lambda b, c: (layer, 0, b, 0))
    return pl.pallas_call(
        functools.partial(_pool_body, steps=steps, pos0=pos0, batch_major=batch_major),
        grid=(batch // bsub, seq // steps),
        in_specs=[tspec, bspec_in, _pspec(lp["pool_w"]), _pspec(lp["pool_scale"])],
        out_specs=[tspec, bspec],
        out_shape=[jax.ShapeDtypeStruct(u.shape, F32), jax.ShapeDtypeStruct((POOL_BUF, batch, GROUP_WIDTH), F32)],
        scratch_shapes=[pltpu.VMEM((POOL_BUF + 1 + steps, bsub, GROUP_WIDTH), F32),
                        pltpu.VMEM((steps, bsub, GROUP_WIDTH), F32)],
        compiler_params=_cparams("parallel", "arbitrary"),
        name="pool",
    )(u, buf0, _parg(lp["pool_w"]), _parg(lp["pool_scale"]))


def _block_diag(blocks):
    n, g, r, c = blocks.shape
    eye = jnp.eye(g, dtype=blocks.dtype)
    return (eye[None, :, None, :, None] * blocks[:, :, :, None, :]).reshape(n, g * r, g * c)


def _stacked_params(P):
    row = lambda a: a.reshape(a.shape[0], 1, -1)
    pad_lanes = lambda a: jnp.pad(a, ((0, 0), (0, LANES - a.shape[1])))
    bf = lambda a: a.astype(BF16)

    lam = lax.complex(P["s5_lam_re"], P["s5_lam_im"])
    a_bar = jnp.exp(lam * jnp.exp(P["s5_log_step"])[..., None])
    b_bar = ((a_bar - 1.0) / lam)[..., None] * lax.complex(P["s5_b_re"], P["s5_b_im"])
    b_t = jnp.swapaxes(b_bar, 2, 3)
    bmat = jnp.concatenate([_block_diag(jnp.real(b_t)), _block_diag(jnp.imag(b_t))], axis=2)
    c_t = jnp.swapaxes(lax.complex(P["s5_c_re"], P["s5_c_im"]), 2, 3)
    cmat = jnp.concatenate([_block_diag(jnp.real(c_t)), -_block_diag(jnp.imag(c_t))], axis=1)

    out = dict(
        norm_ffn1=row(P["norm_ffn1"]), ffn1_in=P["ffn1_in"], ffn1_out=P["ffn1_out"],
        norm_mix=row(P["norm_mix"]),
        w_in=jnp.transpose(P["w_in"], (2, 0, 1)),
        conv_w=P["ssd_conv_w"], conv_b=row(P["ssd_conv_b"]),
        dt_bias=row(pad_lanes(P["ssd_dt_bias"])), a_log=row(pad_lanes(P["ssd_a_log"])),
        a_neg_exp=row(jnp.repeat(-jnp.exp(P["ssd_a_log"]), SSD_HEAD_DIM, axis=1)),
        d_skip=row(jnp.repeat(P["ssd_d"], SSD_HEAD_DIM, axis=1)), ssd_norm=row(P["ssd_norm"]),
        s5_are=row(jnp.real(a_bar)), s5_aim=row(jnp.imag(a_bar)), s5_bmat=bf(bmat), s5_cmat=bf(cmat),
        s5_d=row(P["s5_d"]), s5_gw=bf(P["s5_glu_w"]), s5_gb=row(P["s5_glu_b"]),
        pool_w=bf(_block_diag(P["pool_w"])), pool_scale=row(P["pool_scale"]),
        w_out=bf(P["w_out"]),
        norm_ffn2=row(P["norm_ffn2"]), ffn2_in=P["ffn2_in"], ffn2_out=P["ffn2_out"],
    )
    for name in _RWKV_PARAM_NAMES:
        a = P["rwkv_" + name]
        out["rwkv_" + name] = bf(a) if name in ("w2", "a2", "g2") else row(a)
    return out


def _layer_params(stacked, l):
    lp = {k: _Layered((v, l)) for k, v in stacked.items()}
    lp["rwkv"] = {n: lp["rwkv_" + n] for n in _RWKV_PARAM_NAMES}
    lp["head_expand"] = jnp.pad(jnp.repeat(jnp.eye(SSD_HEADS, dtype=F32), SSD_HEAD_DIM, axis=1),
                                ((0, LANES - SSD_HEADS), (0, 0)))
    return lp


def _mixers_prompt(lp, proj, *, batch, seq):
    z, xbc, ur, us5, upool, dtr = proj
    y_ssd, conv_new, ssd_new = _ssd(z, xbc, dtr, lp, batch=batch, seq=seq)
    y_rwkv, shift_new, rwkv_new = _rwkv(ur, lp["rwkv"], batch=batch, seq=seq)
    zeros = jnp.zeros((batch, S5_WIDTH), F32)
    bm = lambda a: a.reshape(batch, seq, a.shape[-1])
    rows = lambda a: a.reshape(batch * seq, a.shape[-1])
    y_s5, s5re, s5im = _s5(bm(us5), zeros, zeros, lp, batch_major=True)
    y_pool, pool_new = _pool(bm(upool), jnp.zeros((POOL_BUF, batch, GROUP_WIDTH), F32), lp, pos0=0,
                             batch_major=True)
    ys = (y_ssd, y_rwkv, rows(y_s5), rows(y_pool))
    states = (conv_new, ssd_new, shift_new, rwkv_new, s5re.reshape(batch, S5_GROUPS, S5_STATE),
              s5im.reshape(batch, S5_GROUPS, S5_STATE), jnp.swapaxes(pool_new, 0, 1))
    return ys, states


def _mixers_decode(lp, proj, states, done, *, batch, seq, layer):
    z, xbc, ur, us5, upool, dtr = proj
    shift0, s5re0, s5im0 = (states[i][layer] for i in (2, 4, 5))
    ssd_done, rwkv_done = (done[1], done[3]) if layer else (None, None)
    y_ssd, conv_new, ssd_new = _ssd_step(z, xbc, dtr, states[0], states[1], ssd_done, lp, batch=batch, seq=seq,
                                         layer=layer)
    y_rwkv, rwkv_new = _rwkv_step(ur, shift0, states[3], rwkv_done, lp["rwkv"], batch=batch, seq=seq, layer=layer)
    shift_new = ur[(seq - 1) * batch:, :]
    tm = lambda a: a.reshape(seq, batch, a.shape[-1])
    y_s5, s5re, s5im = _s5(tm(us5), s5re0.reshape(batch, S5_WIDTH), s5im0.reshape(batch, S5_WIDTH), lp,
                           batch_major=False)
    y_pool, pool_new = _pool(tm(upool), states[6], lp, pos0=PAST_LEN, batch_major=False, layer=layer)
    rows = lambda a: a.reshape(seq * batch, a.shape[-1])
    ys = (y_ssd, y_rwkv, rows(y_s5), rows(y_pool))
    new_states = (jnp.swapaxes(conv_new, 0, 1), ssd_new, shift_new, rwkv_new,
                  s5re.reshape(batch, S5_GROUPS, S5_STATE), s5im.reshape(batch, S5_GROUPS, S5_STATE),
                  jnp.swapaxes(pool_new, 0, 1))
    return ys, new_states


_WIDTHS = (GROUP_WIDTH, SSD_CONV_DIM, RWKV_PROJ, GROUP_WIDTH, GROUP_WIDTH, LANES)


def _trunk(x_p, x_s, layer_params, norm_final, mixers_p, mixers_s):
    st_p, st_s = [], []
    mix_p, mix_s, lp = None, None, None
    for l, lp_next in enumerate(layer_params):
        if l > 0:
            x_s, wg, wu, wo = _ffn_cast(x_s, lp["norm_ffn2"], lp["ffn2_in"], lp["ffn2_out"], mix=mix_s, wmix=lp["w_out"])
            x_p = _ffn(x_p, lp["norm_ffn2"], wg, wu, wo, mix=mix_p, wmix=lp["w_out"])
        lp = lp_next
        x_s, wg, wu, wo = _ffn_cast(x_s, lp["norm_ffn1"], lp["ffn1_in"], lp["ffn1_out"])
        x_p = _ffn(x_p, lp["norm_ffn1"], wg, wu, wo)
        proj_s, w_all = _inproj_cast(x_s, lp["norm_mix"], lp["w_in"], _WIDTHS)
        mix_p, st = mixers_p(l, lp, _inproj(x_p, lp["norm_mix"], w_all, _WIDTHS), st_p[-1] if st_p else None)
        st_p.append(st)
        mix_s, st = mixers_s(l, lp, proj_s, st_s[-1] if st_s else None)
        st_s.append(st)
    x_s, wg, wu, wo = _ffn_cast(x_s, lp["norm_ffn2"], lp["ffn2_in"], lp["ffn2_out"], mix=mix_s, wmix=lp["w_out"],
                                gf=norm_final)
    x_p = _ffn(x_p, lp["norm_ffn2"], wg, wu, wo, mix=mix_p, wmix=lp["w_out"], gf=norm_final)
    return (x_p, x_s), (st_p, st_s)


def kernel(x_prompt, x_sample, state_ssd_conv, state_ssd, state_rwkv_shift, state_rwkv, state_s5_re, state_s5_im, state_pool, norm_ffn1, ffn1_in, ffn1_out, norm_mix, w_in, ssd_conv_w, ssd_conv_b, ssd_dt_bias, ssd_a_log, ssd_d, ssd_norm, rwkv_mu, rwkv_w0, rwkv_w2, rwkv_a0, rwkv_a2, rwkv_g2, rwkv_k_k, rwkv_k_a, rwkv_r_k, rwkv_ln_g, rwkv_ln_b, s5_lam_re, s5_lam_im, s5_log_step, s5_b_re, s5_b_im, s5_c_re, s5_c_im, s5_d, s5_glu_w, s5_glu_b, pool_w, pool_scale, w_out, norm_ffn2, ffn2_in, ffn2_out, norm_final):
    P = dict(norm_ffn1=norm_ffn1, ffn1_in=ffn1_in, ffn1_out=ffn1_out, norm_mix=norm_mix, w_in=w_in,
             ssd_conv_w=ssd_conv_w, ssd_conv_b=ssd_conv_b, ssd_dt_bias=ssd_dt_bias, ssd_a_log=ssd_a_log,
             ssd_d=ssd_d, ssd_norm=ssd_norm, rwkv_mu=rwkv_mu, rwkv_w0=rwkv_w0, rwkv_w2=rwkv_w2, rwkv_a0=rwkv_a0,
             rwkv_a2=rwkv_a2, rwkv_g2=rwkv_g2, rwkv_k_k=rwkv_k_k, rwkv_k_a=rwkv_k_a,
             rwkv_r_k=rwkv_r_k.reshape(rwkv_r_k.shape[0], -1), rwkv_ln_g=rwkv_ln_g, rwkv_ln_b=rwkv_ln_b,
             s5_lam_re=s5_lam_re, s5_lam_im=s5_lam_im, s5_log_step=s5_log_step, s5_b_re=s5_b_re, s5_b_im=s5_b_im,
             s5_c_re=s5_c_re, s5_c_im=s5_c_im, s5_d=s5_d, s5_glu_w=s5_glu_w, s5_glu_b=s5_glu_b, pool_w=pool_w,
             pool_scale=pool_scale, w_out=w_out, norm_ffn2=norm_ffn2, ffn2_in=ffn2_in, ffn2_out=ffn2_out)
    depth = norm_ffn1.shape[0]
    bp, tp, d = x_prompt.shape
    bs, ts, _ = x_sample.shape
    stacked = _stacked_params(P)
    layer_params = [_layer_params(stacked, l) for l in range(depth)]
    gf = norm_final.reshape(1, -1)
    sample_states = (state_ssd_conv, state_ssd, state_rwkv_shift, state_rwkv, state_s5_re, state_s5_im, state_pool)
    rwkv_rows = RWKV_HEADS * RWKV_HEAD
    decode_states = (jnp.swapaxes(state_ssd_conv, 1, 2), state_ssd, state_rwkv_shift,
                     jnp.transpose(state_rwkv, (0, 2, 3, 4, 1)).reshape(depth, rwkv_rows, RWKV_HEAD, bs),
                     state_s5_re, state_s5_im, jnp.swapaxes(state_pool, 1, 2))

    x_s = jnp.swapaxes(x_sample, 0, 1).reshape(ts * bs, d)
    (y_p, y_s), (st_p, st_s) = _trunk(
        x_prompt.reshape(bp * tp, d), x_s, layer_params, gf,
        lambda l, lp, proj, done: _mixers_prompt(lp, proj, batch=bp, seq=tp),
        lambda l, lp, proj, done: _mixers_decode(lp, proj, decode_states, done, batch=bs, seq=ts, layer=l))
    outs = [y_p.reshape(bp, tp, d), jnp.swapaxes(y_s.reshape(ts, bs, d), 0, 1)]
    for i, ref_state in enumerate(sample_states):
        outs.append(jnp.stack([st[i] for st in st_p]))
        if i == 1:
            outs.append(st_s[-1][i].reshape(ref_state.shape))
        elif i == 3:
            s_new = st_s[-1][i].reshape(depth, RWKV_HEADS, RWKV_HEAD, RWKV_HEAD, bs)
            outs.append(jnp.transpose(s_new, (0, 4, 1, 2, 3)))
        else:
            outs.append(jnp.stack([st[i] for st in st_s]))
    return tuple(outs)
```

```python
import functools
import math

import jax
import jax.numpy as jnp
from jax import lax
from jax.experimental import pallas as pl
from jax.experimental.pallas import tpu as pltpu

F32 = jnp.float32
BF16 = jnp.bfloat16
HIGHEST = lax.Precision.HIGHEST

SUBLANES = 8
LANES = 128
VMEM_LIMIT_BYTES = 56 * 1024 * 1024

GROUP_WIDTH = 256
SSD_HEAD_DIM = 64
SSD_HEADS = 4
SSD_GROUPS = 2
SSD_STATE = 128
SSD_CONV = 4
SSD_CONV_DIM = GROUP_WIDTH + 2 * SSD_GROUPS * SSD_STATE
SSD_CHUNK = 128
SSD_GROUP = 8
LOG2_E = math.log2(math.e)
RWKV_HEAD = 64
RWKV_HEADS = 4
RWKV_PROJ = 1024
RWKV_LN_EPS = 64e-5
RWKV_CHUNK = 64
RWKV_GROUP = 16
S5_GROUPS = 16
S5_STATE = 64
S5_WIDTH = S5_GROUPS * S5_STATE
POOL_WINDOWS = (2, 4, 8, 16)
POOL_CH = 64
POOL_BUF = 15
RMS_EPS = 1e-6
PAST_LEN = 16384

ROW_TILE = 1024
INPROJ_ROW_TILE = 1024
FFN_CHUNK = 256
TM_CHUNK = 128
POOL_CHUNK = 256
S5_SUB = 64
SSD_STEP_TILES = 16
RWKV_STEP_TILES = 16


def _cparams(*sem):
    return pltpu.CompilerParams(dimension_semantics=sem, vmem_limit_bytes=VMEM_LIMIT_BYTES)


def _dot(a, b, **kw):
    return jnp.dot(a, b, preferred_element_type=F32, **kw)


def _dot_nt(a, b):
    return lax.dot_general(a, b, (((1,), (1,)), ((), ())), preferred_element_type=F32)


def _dot_tn(a, b):
    return lax.dot_general(a, b, (((0,), (0,)), ((), ())), preferred_element_type=F32)


def _sigmoid(x):
    return 0.5 * jnp.tanh(0.5 * x) + 0.5


def _silu(x):
    h = 0.5 * x
    return h + h * jnp.tanh(h)


def _softplus(x):
    return jnp.maximum(x, 0.0) + jnp.log(1.0 + jnp.exp(-jnp.abs(x)))


def _gelu_tanh(x):
    c = math.sqrt(2.0 / math.pi)
    return x * (0.5 * (1.0 + jnp.tanh(c * (x + 0.044715 * (x * x * x)))))


def _rms(x, g):
    return x * lax.rsqrt(jnp.mean(x * x, axis=-1, keepdims=True) + RMS_EPS) * g


def _full_spec(shape):
    n = len(shape)
    return pl.BlockSpec(shape, lambda *_: (0,) * n)


class _Layered(tuple):
    pass


def _pspec(p, single=False):
    mode = pl.Buffered(1) if single else None
    if isinstance(p, _Layered):
        a, l = p
        return pl.BlockSpec((None,) + a.shape[1:], lambda *_: (l,) + (0,) * (a.ndim - 1), pipeline_mode=mode)
    n = p.ndim
    return pl.BlockSpec(p.shape, lambda *_: (0,) * n, pipeline_mode=mode)


def _parg(p):
    return p[0] if isinstance(p, _Layered) else p


def _mix_residual(x, y_refs, wmix_ref):
    for j, y_ref in enumerate(y_refs):
        x = x + _dot(y_ref[...].astype(BF16), wmix_ref[j * GROUP_WIDTH:(j + 1) * GROUP_WIDTH, :])
    return x


def _swiglu_chunk(h, wg, wu, wo):
    act = (_silu(_dot(h, wg)) * _dot(h, wu)).astype(BF16)
    return _dot(act, wo)


def _ffn_body(*refs, has_mix, final_norm):
    it = iter(refs)
    x = next(it)[...]
    if has_mix:
        y_refs = [next(it) for _ in range(4)]
        x = _mix_residual(x, y_refs, next(it))
    g_ref, wg_ref, wu_ref, wo_ref = next(it), next(it), next(it), next(it)
    gf_ref = next(it) if final_norm else None
    o_ref = next(it)
    h = _rms(x, g_ref[...]).astype(BF16)
    acc = jnp.zeros_like(x)
    for c in range(wo_ref.shape[0] // FFN_CHUNK):
        cols = slice(c * FFN_CHUNK, (c + 1) * FFN_CHUNK)
        acc = acc + _swiglu_chunk(h, wg_ref[:, cols], wu_ref[:, cols], wo_ref[cols, :])
    x = x + 0.5 * acc
    if final_norm:
        x = _rms(x, gf_ref[...])
    o_ref[...] = x


def _ffn(x, g, wg, wu, wo, mix=None, wmix=None, gf=None):
    rows, d = x.shape
    row_spec = lambda w: pl.BlockSpec((ROW_TILE, w), lambda i: (i, 0))
    args, specs = [x], [row_spec(d)]
    if mix is not None:
        for y in mix:
            args.append(y)
            specs.append(row_spec(y.shape[1]))
        args.append(_parg(wmix))
        specs.append(_pspec(wmix, single=True))
    for a in (g, wg, wu, wo) + ((gf,) if gf is not None else ()):
        args.append(_parg(a))
        specs.append(_pspec(a, single=True))
    return pl.pallas_call(
        functools.partial(_ffn_body, has_mix=mix is not None, final_norm=gf is not None),
        grid=(rows // ROW_TILE,),
        in_specs=specs,
        out_specs=row_spec(d),
        out_shape=jax.ShapeDtypeStruct((rows, d), F32),
        compiler_params=_cparams("parallel"),
        name="ffn",
    )(*args)


def _ffn_cast_body(*refs, has_mix, final_norm):
    it = iter(refs)
    x_ref = next(it)
    if has_mix:
        y_refs = [next(it) for _ in range(4)]
        wmix_ref = next(it)
    g_ref, wg_ref, wu_ref, wo_ref = next(it), next(it), next(it), next(it)
    gf_ref = next(it) if final_norm else None
    o_ref, wg_out, wu_out, wo_out, x_scr, h_scr, acc_scr = (next(it) for _ in range(7))
    c = pl.program_id(0)

    @pl.when(c == 0)
    def _():
        x = x_ref[...]
        if has_mix:
            x = _mix_residual(x, y_refs, wmix_ref)
        x_scr[...] = x
        h_scr[...] = _rms(x, g_ref[...]).astype(BF16)
        acc_scr[...] = jnp.zeros(acc_scr.shape, F32)

    wg = wg_ref[...].astype(BF16)
    wu = wu_ref[...].astype(BF16)
    wo = wo_ref[...].astype(BF16)
    wg_out[...] = wg
    wu_out[...] = wu
    wo_out[...] = wo
    acc_scr[...] += _swiglu_chunk(h_scr[...], wg, wu, wo)

    @pl.when(c == pl.num_programs(0) - 1)
    def _():
        x = x_scr[...] + 0.5 * acc_scr[...]
        if final_norm:
            x = _rms(x, gf_ref[...])
        o_ref[...] = x


def _ffn_cast(x, g, wi, wo, mix=None, wmix=None, gf=None):
    rows, d = x.shape
    wi_all, l = wi
    wo_all, _ = wo
    d_ff = wo_all.shape[1]
    nchunks = d_ff // FFN_CHUNK
    args, specs = [x], [_full_spec(x.shape)]
    if mix is not None:
        for y in mix:
            args.append(y)
            specs.append(_full_spec(y.shape))
        args.append(_parg(wmix))
        specs.append(_pspec(wmix, single=True))
    args += [_parg(g), wi_all, wi_all, wo_all]
    specs += [_pspec(g),
              pl.BlockSpec((None, d, FFN_CHUNK), lambda c: (l, 0, c)),
              pl.BlockSpec((None, d, FFN_CHUNK), lambda c: (l, 0, c + nchunks)),
              pl.BlockSpec((None, FFN_CHUNK, d), lambda c: (l, c, 0))]
    if gf is not None:
        args.append(gf)
        specs.append(_full_spec(gf.shape))
    col_spec = pl.BlockSpec((d, FFN_CHUNK), lambda c: (0, c))
    return pl.pallas_call(
        functools.partial(_ffn_cast_body, has_mix=mix is not None, final_norm=gf is not None),
        grid=(nchunks,),
        in_specs=specs,
        out_specs=[_full_spec(x.shape), col_spec, col_spec, pl.BlockSpec((FFN_CHUNK, d), lambda c: (c, 0))],
        out_shape=[jax.ShapeDtypeStruct((rows, d), F32), jax.ShapeDtypeStruct((d, d_ff), BF16),
                   jax.ShapeDtypeStruct((d, d_ff), BF16), jax.ShapeDtypeStruct((d_ff, d), BF16)],
        scratch_shapes=[pltpu.VMEM((rows, d), F32), pltpu.VMEM((rows, d), BF16), pltpu.VMEM((rows, d), F32)],
        compiler_params=_cparams("arbitrary"),
        name="ffn_cast",
    )(*args)


def _inproj_body(x_ref, g_ref, wt_ref, *o_refs):
    h = _rms(x_ref[...], g_ref[...]).astype(BF16)
    off = 0
    for o_ref in o_refs:
        n = o_ref.shape[-1]
        o_ref[...] = _dot_nt(h, wt_ref[off:off + n, :])
        off += n


def _inproj_cast_body(x_ref, g_ref, win_ref, *o_refs, layer):
    *proj_refs, wall_ref = o_refs
    split = GROUP_WIDTH + SSD_CONV_DIM
    wt = win_ref[:, layer, :]
    tail = wt.shape[0] - split - SSD_HEADS
    wall_ref[0:split, :] = wt[0:split].astype(BF16)
    wall_ref[split:split + tail, :] = wt[split + SSD_HEADS:].astype(BF16)
    dt_rows = jnp.concatenate([wt[split:split + SSD_HEADS], jnp.zeros((LANES - SSD_HEADS, wt.shape[1]), F32)], axis=0)
    wall_ref[split + tail:, :] = dt_rows.astype(BF16)
    _inproj_body(x_ref, g_ref, wall_ref, *proj_refs)


def _inproj_cast(x, g, w_in, widths):
    rows, d = x.shape
    wt_all, l = w_in
    outs = pl.pallas_call(
        functools.partial(_inproj_cast_body, layer=l),
        grid=(1,),
        in_specs=[_full_spec(x.shape), _pspec(g),
                  pl.BlockSpec(wt_all.shape, lambda i: (0, 0, 0), pipeline_mode=pl.Buffered(1))],
        out_specs=[_full_spec((rows, n)) for n in widths] + [_full_spec((sum(widths), d))],
        out_shape=[jax.ShapeDtypeStruct((rows, n), F32) for n in widths]
                  + [jax.ShapeDtypeStruct((sum(widths), d), BF16)],
        compiler_params=_cparams("arbitrary"),
        name="inproj_cast",
    )(x, _parg(g), wt_all)
    return outs[:-1], outs[-1]


def _inproj(x, g, w, widths):
    rows, d = x.shape
    tile = INPROJ_ROW_TILE
    row_spec = lambda w_: pl.BlockSpec((tile, w_), lambda i: (i, 0))
    return pl.pallas_call(
        _inproj_body,
        grid=(rows // tile,),
        in_specs=[row_spec(d), _pspec(g), _pspec(w, single=True)],
        out_specs=[row_spec(n) for n in widths],
        out_shape=[jax.ShapeDtypeStruct((rows, n), F32) for n in widths],
        compiler_params=_cparams("parallel"),
        name="inproj",
    )(x, _parg(g), _parg(w))


def _ssd_body(z_ref, xbc_ref, dt_ref, cw_ref, cb_ref, dtb_ref, alog_ref, dsk_ref, ng_ref,
              y_ref, conv_ref, hout_ref, xpad_scr, h_scr, *, chunk, group):
    L, G = chunk, group
    GL = G * L
    c = pl.program_id(1)
    pad = SUBLANES
    halo = SSD_CONV - 1
    hpg = SSD_HEADS // SSD_GROUPS
    assert hpg == 2 and hpg * SSD_HEAD_DIM == SSD_STATE

    @pl.when(c == 0)
    def _():
        xpad_scr[0:pad, :] = jnp.zeros((pad, SSD_CONV_DIM), F32)
        h_scr[...] = jnp.zeros(h_scr.shape, F32)

    xpad_scr[pad:pad + GL, :] = xbc_ref[...]
    xfull = xpad_scr[...]
    conv = cb_ref[...] + cw_ref[halo:halo + 1, :] * xfull[pad:pad + GL]
    for j in range(halo):
        conv = conv + cw_ref[j:j + 1, :] * pltpu.roll(xfull, halo - j, axis=0)[pad:pad + GL]
    xpad_scr[pad - halo:pad, :] = xpad_scr[pad + GL - halo:pad + GL, :]
    conv = _silu(conv)
    xs = conv[:, 0:GROUP_WIDTH]
    bm = conv[:, GROUP_WIDTH:2 * GROUP_WIDTH].astype(BF16)
    cm = conv[:, 2 * GROUP_WIDTH:3 * GROUP_WIDTH].astype(BF16)

    row = lax.broadcasted_iota(jnp.int32, (L, L), 0)
    col = lax.broadcasted_iota(jnp.int32, (L, L), 1)
    causal = row >= col
    tril = jnp.where(causal, 1.0, 0.0).astype(F32)
    dt = _softplus(dt_ref[...] + dtb_ref[...])
    da = dt * (-jnp.exp(alog_ref[...]) * LOG2_E)
    acs = [_dot(tril, da[i * L:(i + 1) * L, :], precision=HIGHEST) for i in range(G)]
    acs_t = [a.T for a in acs]
    e_acs = [jnp.exp2(a) for a in acs]
    e_end = [jnp.exp2(a[L - 1:L, :] - a) for a in acs]
    e_last = [jnp.exp2(a[L - 1:L, :]) for a in acs]

    keys = [(i, g) for i in range(G) for g in range(SSD_GROUPS)]
    rows_of = lambda x, i: x[i * L:(i + 1) * L]
    lanes_of = lambda x, g: x[:, g * SSD_STATE:(g + 1) * SSD_STATE]
    lane_lo = lax.broadcasted_iota(jnp.int32, (L, hpg * SSD_HEAD_DIM), 1) < SSD_HEAD_DIM
    row_lo = lax.broadcasted_iota(jnp.int32, (hpg * SSD_HEAD_DIM, SSD_STATE), 0) < SSD_HEAD_DIM
    head_cols = lambda a, g: jnp.where(lane_lo, a[:, g * hpg:g * hpg + 1], a[:, g * hpg + 1:g * hpg + 2])
    bg = {(i, g): lanes_of(rows_of(bm, i), g) for i, g in keys}
    cg = {(i, g): lanes_of(rows_of(cm, i), g) for i, g in keys}
    scores = {k: _dot_nt(cg[k], bg[k]) for k in keys}
    xdt = {(i, g): lanes_of(rows_of(xs, i), g) * head_cols(rows_of(dt, i), g) for i, g in keys}
    decay = {(i, h): jnp.exp2(jnp.where(causal, acs[i][:, h:h + 1] - acs_t[i][h:h + 1, :], -jnp.inf))
             for i in range(G) for h in range(SSD_HEADS)}
    p_mat = {(i, g): jnp.concatenate([(scores[(i, g)] * decay[(i, g * hpg + k)]).astype(BF16) for k in range(hpg)],
                                     axis=1) for i, g in keys}
    y_in = {k: _dot(p_mat[k], _bd(xdt[k].astype(BF16))) for k in keys}
    st = {(i, g): _dot_tn((xdt[(i, g)] * head_cols(e_end[i], g)).astype(BF16), bg[(i, g)]) for i, g in keys}

    y_rows = []
    for i in range(G):
        ys = []
        for g in range(SSD_GROUPS):
            h_prev = h_scr[g * hpg:(g + 1) * hpg].reshape(hpg * SSD_HEAD_DIM, SSD_STATE)
            ys.append(y_in[(i, g)] + _dot_nt(cg[(i, g)], h_prev.astype(BF16)) * head_cols(e_acs[i], g))
            keep = jnp.where(row_lo, e_last[i][:, g * hpg:g * hpg + 1], e_last[i][:, g * hpg + 1:g * hpg + 2])
            h_scr[g * hpg:(g + 1) * hpg] = (h_prev * keep + st[(i, g)]).reshape(hpg, SSD_HEAD_DIM, SSD_STATE)
        y_rows.append(jnp.concatenate(ys, axis=-1))
    y = jnp.concatenate(y_rows, axis=0) + xs * dsk_ref[...]
    y = y * _silu(z_ref[...])
    y_ref[...] = _rms(y, ng_ref[...])

    @pl.when(c == pl.num_programs(1) - 1)
    def _():
        hout_ref[0] = h_scr[...]
        conv_ref[0] = xpad_scr[pad - halo:pad, :]


def _ssd(z, xbc, dtr, lp, *, batch, seq):
    chunk = SSD_CHUNK
    rows = chunk * SSD_GROUP
    nc = seq // rows
    rspec = lambda w: pl.BlockSpec((rows, w), lambda b, c: (b * nc + c, 0))
    consts = (lp["conv_w"], lp["conv_b"], lp["dt_bias"], lp["a_log"], lp["d_skip"], lp["ssd_norm"])
    return pl.pallas_call(
        functools.partial(_ssd_body, chunk=chunk, group=SSD_GROUP),
        grid=(batch, nc),
        in_specs=[rspec(GROUP_WIDTH), rspec(SSD_CONV_DIM), rspec(LANES)] + [_pspec(a) for a in consts],
        out_specs=[rspec(GROUP_WIDTH),
                   pl.BlockSpec((1, SSD_CONV - 1, SSD_CONV_DIM), lambda b, c: (b, 0, 0)),
                   pl.BlockSpec((1, SSD_HEADS, SSD_HEAD_DIM, SSD_STATE), lambda b, c: (b, 0, 0, 0))],
        out_shape=[jax.ShapeDtypeStruct((batch * seq, GROUP_WIDTH), F32),
                   jax.ShapeDtypeStruct((batch, SSD_CONV - 1, SSD_CONV_DIM), F32),
                   jax.ShapeDtypeStruct((batch, SSD_HEADS, SSD_HEAD_DIM, SSD_STATE), F32)],
        scratch_shapes=[pltpu.VMEM((SUBLANES + rows, SSD_CONV_DIM), F32),
                        pltpu.VMEM((SSD_HEADS, SSD_HEAD_DIM, SSD_STATE), F32)],
        compiler_params=_cparams("parallel", "arbitrary"),
        name="ssd",
    )(z, xbc, dtr, *[_parg(a) for a in consts])


def _ssd_step_body(z_ref, xbc_ref, dt_ref, conv0_ref, h0_ref, *rest, seq, batch, layer):
    hdone_ref, rest = (rest[0], rest[1:]) if layer else (None, rest)
    (cw_ref, cb_ref, dtb_ref, aneg_ref, dsk_ref, ng_ref, hexp_ref, y_ref, conv_ref, hout_ref,
     xs_scr, bm_scr, cm_scr, xdt_scr, dec_scr, y_scr) = rest
    T, B = seq, batch
    if layer:
        hout_ref[0:layer] = hdone_ref[...]
    GW = GROUP_WIDTH
    j = pl.program_id(0)
    tiles = SSD_STEP_TILES

    @pl.when(j == 0)
    def _():
        rows = [conv0_ref[i] for i in range(SSD_CONV - 1)]
        rows += [xbc_ref[t * B:(t + 1) * B, :] for t in range(T)]
        for t in range(T):
            conv = cb_ref[...] + cw_ref[0:1, :] * rows[t]
            for i in range(1, SSD_CONV):
                conv = conv + cw_ref[i:i + 1, :] * rows[t + i]
            conv = _silu(conv)
            xs = conv[:, 0:GW]
            xs_scr[t] = xs
            for g in range(SSD_GROUPS):
                bm_scr[t, g] = conv[:, GW + g * SSD_STATE:GW + (g + 1) * SSD_STATE].T
                cm_scr[t, g] = conv[:, 2 * GW + g * SSD_STATE:2 * GW + (g + 1) * SSD_STATE].T
            dt = _softplus(dt_ref[t * B:(t + 1) * B, :] + dtb_ref[...])
            dte = _dot(dt, hexp_ref[...], precision=HIGHEST)
            xdt_scr[t] = (xs * dte).T
            dec_scr[t] = jnp.exp(dte * aneg_ref[...]).T
        for i in range(SSD_CONV - 1):
            conv_ref[i] = rows[T + i]

    hp0 = j * tiles
    grp = hp0 // (SSD_HEAD_DIM * (SSD_HEADS // SSD_GROUPS))
    for q in range(tiles):
        hp = pl.ds(hp0 + q, 1)
        h = h0_ref[:, q, :].T
        for t in range(T):
            h = h * dec_scr[t, hp, :] + bm_scr[t, grp] * xdt_scr[t, hp, :]
            y_scr[t, hp, :] = jnp.sum(h * cm_scr[t, grp], axis=0, keepdims=True)
        hout_ref[layer, :, q, :] = h.T

    @pl.when(j == pl.num_programs(0) - 1)
    def _():
        for t in range(T):
            y = y_scr[t].T + xs_scr[t] * dsk_ref[...]
            y = y * _silu(z_ref[t * B:(t + 1) * B, :])
            y_ref[t * B:(t + 1) * B, :] = _rms(y, ng_ref[...])


def _layer_state_specs(layer, block, axis):
    idx = lambda first: (lambda j: (first,) + tuple(j if a == axis else 0 for a in range(len(block))))
    cur = pl.BlockSpec((None,) + block, idx(layer))
    prev = [pl.BlockSpec((layer,) + block, idx(0))] if layer else []
    out = pl.BlockSpec((layer + 1,) + block, idx(0))
    return cur, prev, out


def _ssd_step(z, xbc, dtr, conv_all, h_all, h_done, lp, *, batch, seq, layer):
    n = batch * seq
    srows = SSD_HEADS * SSD_HEAD_DIM
    consts = (lp["conv_w"], lp["conv_b"], lp["dt_bias"], lp["a_neg_exp"], lp["d_skip"], lp["ssd_norm"], lp["head_expand"])
    hspec, prev_specs, hout_spec = _layer_state_specs(layer, (batch, SSD_STEP_TILES, SSD_STATE), 1)
    prev_args = [h_done] if layer else []
    cshape = (SSD_CONV - 1, batch, SSD_CONV_DIM)
    return pl.pallas_call(
        functools.partial(_ssd_step_body, seq=seq, batch=batch, layer=layer),
        grid=(srows // SSD_STEP_TILES,),
        in_specs=[_full_spec((n, GROUP_WIDTH)), _full_spec((n, SSD_CONV_DIM)), _full_spec((n, LANES)),
                  pl.BlockSpec((None,) + cshape, lambda j: (layer, 0, 0, 0)), hspec] + prev_specs
                 + [_pspec(a) for a in consts],
        out_specs=[_full_spec((n, GROUP_WIDTH)), _full_spec(cshape), hout_spec],
        out_shape=[jax.ShapeDtypeStruct((n, GROUP_WIDTH), F32),
                   jax.ShapeDtypeStruct(cshape, F32),
                   jax.ShapeDtypeStruct((layer + 1, batch, srows, SSD_STATE), F32)],
        scratch_shapes=[pltpu.VMEM((seq, batch, GROUP_WIDTH), F32),
                        pltpu.VMEM((seq, SSD_GROUPS, SSD_STATE, batch), F32),
                        pltpu.VMEM((seq, SSD_GROUPS, SSD_STATE, batch), F32),
                        pltpu.VMEM((seq, GROUP_WIDTH, batch), F32),
                        pltpu.VMEM((seq, GROUP_WIDTH, batch), F32),
                        pltpu.VMEM((seq, GROUP_WIDTH, batch), F32)],
        compiler_params=_cparams("arbitrary"),
        name="ssd_step",
    )(z, xbc, dtr, conv_all, h_all.reshape(h_all.shape[0], batch, srows, SSD_STATE),
      *prev_args, *[_parg(a) for a in consts])


PAIR = 2 * RWKV_HEAD
RWKV_PAIRS = RWKV_HEADS // 2


def _bd(x):
    half = x.shape[1] // 2
    lane = lax.broadcasted_iota(jnp.int32, x.shape, 1)
    zero = jnp.zeros_like(x)
    return jnp.concatenate([jnp.where(lane < half, x, zero), jnp.where(lane >= half, x, zero)], axis=0)


def _half_sums(x, lo):
    s_lo = jnp.sum(jnp.where(lo, x, 0.0), axis=-1, keepdims=True)
    s_hi = jnp.sum(jnp.where(lo, 0.0, x), axis=-1, keepdims=True)
    return jnp.where(lo, s_lo, s_hi)


def _head_sum(x):
    lo = lax.broadcasted_iota(jnp.int32, (x.shape[0], PAIR), 1) < RWKV_HEAD
    return jnp.concatenate([_half_sums(x[:, p * PAIR:(p + 1) * PAIR], lo) for p in range(RWKV_PAIRS)], axis=-1)


def _rwkv_pointwise(u, prev, mu_ref, w0_ref, w2_ref, a0_ref, a2_ref, g2_ref, kk_ref, ka_ref):
    GW = GROUP_WIDTH
    xs = u + (prev - u) * mu_ref[...]
    r = xs[:, 0:GW]
    k = xs[:, GW:2 * GW]
    v = xs[:, 2 * GW:3 * GW]
    wd = xs[:, 3 * GW:3 * GW + 64]
    ad = xs[:, 3 * GW + 64:3 * GW + 128]
    gd = xs[:, 3 * GW + 128:3 * GW + 256]
    w_lin = w0_ref[...] + _dot(jnp.tanh(wd).astype(BF16), w2_ref[...])
    logdecay = -math.exp(-0.5) * _sigmoid(w_lin)
    a = _sigmoid(a0_ref[...] + _dot(ad.astype(BF16), a2_ref[...]))
    g = _dot(_sigmoid(gd).astype(BF16), g2_ref[...])
    kk = k * kk_ref[...]
    kk = kk * lax.rsqrt(jnp.maximum(_head_sum(kk * kk), 1e-24))
    k = k * (1.0 + (a - 1.0) * ka_ref[...])
    return r, k, v, logdecay, a, g, kk


def _rwkv_finish(y, r, k, v, g, rk_ref, lng_ref, lnb_ref):
    mean = _head_sum(y) * (1.0 / RWKV_HEAD)
    yc = y - mean
    var = _head_sum(yc * yc) * (1.0 / RWKV_HEAD)
    y = yc * lax.rsqrt(var + RWKV_LN_EPS) * lng_ref[...] + lnb_ref[...]
    bonus = _head_sum(r * k * rk_ref[...]) * v
    return (y + bonus) * g


def _rwkv_body(u_ref, mu_ref, w0_ref, w2_ref, a0_ref, a2_ref, g2_ref, kk_ref, ka_ref, rk_ref,
               lng_ref, lnb_ref, y_ref, shift_ref, sout_ref, upad_scr, s_scr, *, chunk, group):
    L, G = chunk, group
    GL = G * L
    c = pl.program_id(1)
    pad = SUBLANES

    @pl.when(c == 0)
    def _():
        upad_scr[0:pad, :] = jnp.zeros((pad, RWKV_PROJ), F32)
        s_scr[...] = jnp.zeros(s_scr.shape, F32)

    u = u_ref[...]
    upad_scr[pad:pad + GL, :] = u
    prev = pltpu.roll(upad_scr[...], 1, axis=0)[pad:pad + GL]
    upad_scr[pad - 1:pad, :] = u[GL - 1:GL, :]
    r, k, v, logdecay, a, g, kk = _rwkv_pointwise(u, prev, mu_ref, w0_ref, w2_ref, a0_ref, a2_ref, g2_ref,
                                                  kk_ref, ka_ref)

    tril = jnp.where(lax.broadcasted_iota(jnp.int32, (L, L), 0) >= lax.broadcasted_iota(jnp.int32, (L, L), 1),
                     1.0, 0.0).astype(F32)
    cl = jnp.concatenate([_dot(tril, logdecay[i * L:(i + 1) * L, :], precision=HIGHEST) for i in range(G)], axis=0)
    e_in = jnp.exp(cl)
    e_inv = jnp.exp(-cl)
    r_t = r * e_in
    r_tb = r_t.astype(BF16)
    a_tb = (-kk * jnp.exp(cl - logdecay)).astype(BF16)
    b_tb = (kk * a * e_inv).astype(BF16)
    k_tb = (k * e_inv).astype(BF16)
    vb = v.astype(BF16)

    row = lax.broadcasted_iota(jnp.int32, (L, PAIR), 0)
    colh = lax.broadcasted_iota(jnp.int32, (L, PAIR), 1) & (RWKV_HEAD - 1)
    strict = row > colh
    incl = row >= colh
    eye_pair = jnp.where(row == colh, 1.0, 0.0).astype(F32)
    lane_lo = lax.broadcasted_iota(jnp.int32, (RWKV_HEAD, PAIR), 1) < RWKV_HEAD
    same_head = (lax.broadcasted_iota(jnp.int32, (PAIR, PAIR), 0) < RWKV_HEAD) == \
                (lax.broadcasted_iota(jnp.int32, (PAIR, PAIR), 1) < RWKV_HEAD)

    streams = [(i, p) for i in range(G) for p in range(RWKV_PAIRS)]
    ns = len(streams)
    blk = lambda x, i, p: x[i * L:(i + 1) * L, p * PAIR:(p + 1) * PAIR]
    lhs = [jnp.concatenate([blk(a_tb, i, p), blk(r_tb, i, p)], axis=0) for i, p in streams]
    m_both = [_dot_nt(lhs[s], jnp.concatenate([_bd(blk(b_tb, i, p)), _bd(blk(k_tb, i, p))], axis=0))
              for s, (i, p) in enumerate(streams)]
    m_ab = [m[:, 0:PAIR] for m in m_both]
    m_ak = [m[:, PAIR:2 * PAIR] for m in m_both]
    n_ab = [jnp.where(strict, m[0:L], 0.0) for m in m_ab]
    m_rb = [jnp.where(incl, m[L:2 * L], 0.0).astype(BF16) for m in m_ab]
    n_ak = [jnp.where(strict, m[0:L], 0.0).astype(BF16) for m in m_ak]
    m_rk = [jnp.where(incl, m[L:2 * L], 0.0).astype(BF16) for m in m_ak]
    tinv = [eye_pair + n for n in n_ab]
    pwb = [n.astype(BF16) for n in n_ab]
    pw = [_dot(x, _bd(x)) for x in pwb]
    for _ in range(int(math.log2(L)) - 2):
        pwb = [x.astype(BF16) for x in pw]
        both = [_dot(jnp.concatenate([pwb[s], tinv[s].astype(BF16)], axis=0), _bd(pwb[s])) for s in range(ns)]
        pw = [x[0:L] for x in both]
        tinv = [tinv[s] + both[s][L:2 * L] for s in range(ns)]
    pwb = [x.astype(BF16) for x in pw]
    tinv = [tinv[s] + _dot(tinv[s].astype(BF16), _bd(pwb[s])) for s in range(ns)]
    tinvb = [x.astype(BF16) for x in tinv]
    nv_mv = [_dot(jnp.concatenate([n_ak[s], m_rk[s]], axis=0), _bd(blk(vb, i, p))) for s, (i, p) in enumerate(streams)]
    wu = [_dot(tinvb[s], jnp.concatenate([_bd(blk(a_tb, i, p)), _bd(nv_mv[s][0:L].astype(BF16))], axis=1))
          for s, (i, p) in enumerate(streams)]
    wub = [x.astype(BF16) for x in wu]
    qy = [_dot(m_rb[s], jnp.concatenate([_bd(wub[s][:, 0:PAIR]), _bd(wub[s][:, PAIR:2 * PAIR])], axis=1))
          for s in range(ns)]
    q = [(blk(r_t, i, p) + qy[s][:, 0:PAIR]).astype(BF16) for s, (i, p) in enumerate(streams)]
    y_loc = [qy[s][:, PAIR:2 * PAIR] + nv_mv[s][L:2 * L] for s in range(ns)]
    zeros_b = jnp.zeros((L, PAIR), BF16)
    mg = [_dot_tn(jnp.concatenate([wub[s], jnp.concatenate([zeros_b, blk(vb, i, p)], axis=1)], axis=0),
                  jnp.concatenate([blk(b_tb, i, p), blk(k_tb, i, p)], axis=0))
          for s, (i, p) in enumerate(streams)]
    p_end = [e_in[(i + 1) * L - 1:(i + 1) * L, p * PAIR:(p + 1) * PAIR] for i, p in streams]
    m_t = [(jnp.where(same_head, mg[s][0:PAIR], 0.0) * p_end[s]).astype(BF16) for s in range(ns)]
    g_t = [jnp.where(lane_lo, mg[s][PAIR:PAIR + RWKV_HEAD], mg[s][PAIR + RWKV_HEAD:2 * PAIR]) * p_end[s]
           for s in range(ns)]

    y_rows = []
    for i in range(G):
        y_pairs = []
        for p in range(RWKV_PAIRS):
            s = i * RWKV_PAIRS + p
            s0 = s_scr[p]
            s0b = s0.astype(BF16)
            y_pairs.append(_dot_nt(q[s], _bd(s0b)) + y_loc[s])
            s_scr[p] = s0 * p_end[s] + _dot(s0b, m_t[s]) + g_t[s]
        y_rows.append(jnp.concatenate(y_pairs, axis=-1))
    y = jnp.concatenate(y_rows, axis=0)
    y_ref[...] = _rwkv_finish(y, r, k, v, g, rk_ref, lng_ref, lnb_ref)

    @pl.when(c == pl.num_programs(1) - 1)
    def _():
        sout_ref[0] = s_scr[...]
        shift_ref[0] = upad_scr[pad - 1:pad, :]


_RWKV_PARAM_NAMES = ("mu", "w0", "w2", "a0", "a2", "g2", "k_k", "k_a", "r_k", "ln_g", "ln_b")


def _rwkv(u, p, *, batch, seq):
    rows = RWKV_CHUNK * RWKV_GROUP
    nc = seq // rows
    params = [p[n] for n in _RWKV_PARAM_NAMES]
    sspec = pl.BlockSpec((1, RWKV_PAIRS, RWKV_HEAD, PAIR), lambda b, c: (b, 0, 0, 0))
    y, shift, s_last = pl.pallas_call(
        functools.partial(_rwkv_body, chunk=RWKV_CHUNK, group=RWKV_GROUP),
        grid=(batch, nc),
        in_specs=[pl.BlockSpec((rows, RWKV_PROJ), lambda b, c: (b * nc + c, 0))] + [_pspec(a) for a in params],
        out_specs=[pl.BlockSpec((rows, GROUP_WIDTH), lambda b, c: (b * nc + c, 0)),
                   pl.BlockSpec((1, 1, RWKV_PROJ), lambda b, c: (b, 0, 0)), sspec],
        out_shape=[jax.ShapeDtypeStruct((batch * seq, GROUP_WIDTH), F32),
                   jax.ShapeDtypeStruct((batch, 1, RWKV_PROJ), F32),
                   jax.ShapeDtypeStruct((batch, RWKV_PAIRS, RWKV_HEAD, PAIR), F32)],
        scratch_shapes=[pltpu.VMEM((SUBLANES + rows, RWKV_PROJ), F32),
                        pltpu.VMEM((RWKV_PAIRS, RWKV_HEAD, PAIR), F32)],
        compiler_params=_cparams("parallel", "arbitrary"),
        name="rwkv",
    )(u, *[_parg(a) for a in params])
    s_last = s_last.reshape(batch, RWKV_PAIRS, RWKV_HEAD, 2, RWKV_HEAD).transpose(0, 1, 3, 2, 4).reshape(
        batch, RWKV_HEADS, RWKV_HEAD, RWKV_HEAD)
    return y, shift.reshape(batch, RWKV_PROJ), s_last


def _rwkv_step_body(u_ref, shift0_ref, s0_ref, *rest, seq, batch, layer):
    sdone_ref, rest = (rest[0], rest[1:]) if layer else (None, rest)
    (mu_ref, w0_ref, w2_ref, a0_ref, a2_ref, g2_ref, kk_ref, ka_ref, rk_ref, lng_ref, lnb_ref, y_ref, sout_ref,
     r_scr, w_scr, k_scr, b_scr, nkk_scr, v_scr, y_scr) = rest
    T, B = seq, batch
    j = pl.program_id(0)
    if layer:
        sout_ref[0:layer] = sdone_ref[...]
    tiles = RWKV_STEP_TILES

    def pointwise(t):
        u = u_ref[t * B:(t + 1) * B, :]
        prev = shift0_ref[...] if t == 0 else u_ref[(t - 1) * B:t * B, :]
        return _rwkv_pointwise(u, prev, mu_ref, w0_ref, w2_ref, a0_ref, a2_ref, g2_ref, kk_ref, ka_ref)

    @pl.when(j == 0)
    def _():
        for t in range(T):
            r, k, v, logdecay, a, _, kk = pointwise(t)
            r_scr[t] = r.T
            w_scr[t] = jnp.exp(logdecay).T
            k_scr[t] = k.T
            b_scr[t] = (kk * a).T
            nkk_scr[t] = (-kk).T
            v_scr[t] = v.T

    i0 = j * tiles
    keys = pl.ds(pl.multiple_of((i0 // RWKV_HEAD) * RWKV_HEAD, RWKV_HEAD), RWKV_HEAD)
    for q in range(tiles):
        vi = pl.ds(i0 + q, 1)
        s = s0_ref[q]
        for t in range(T):
            sa = jnp.sum(s * nkk_scr[t, keys, :], axis=0, keepdims=True)
            s = s * w_scr[t, keys, :] + k_scr[t, keys, :] * v_scr[t, vi, :] + b_scr[t, keys, :] * sa
            y_scr[t, vi, :] = jnp.sum(s * r_scr[t, keys, :], axis=0, keepdims=True)
        sout_ref[layer, q] = s

    @pl.when(j == pl.num_programs(0) - 1)
    def _():
        for t in range(T):
            r, k, v, _, _, g, _ = pointwise(t)
            y_ref[t * B:(t + 1) * B, :] = _rwkv_finish(y_scr[t].T, r, k, v, g, rk_ref, lng_ref, lnb_ref)


def _rwkv_step(u, shift0, s_all, s_done, p, *, batch, seq, layer):
    n = batch * seq
    srows = RWKV_HEADS * RWKV_HEAD
    params = [p[nm] for nm in _RWKV_PARAM_NAMES]
    sspec, prev_specs, sout_spec = _layer_state_specs(layer, (RWKV_STEP_TILES, RWKV_HEAD, batch), 0)
    prev_args = [s_done] if layer else []
    tposed = pltpu.VMEM((seq, GROUP_WIDTH, batch), F32)
    return pl.pallas_call(
        functools.partial(_rwkv_step_body, seq=seq, batch=batch, layer=layer),
        grid=(srows // RWKV_STEP_TILES,),
        in_specs=[_full_spec((n, RWKV_PROJ)), _full_spec((batch, RWKV_PROJ)), sspec] + prev_specs
                 + [_pspec(a) for a in params],
        out_specs=[_full_spec((n, GROUP_WIDTH)), sout_spec],
        out_shape=[jax.ShapeDtypeStruct((n, GROUP_WIDTH), F32),
                   jax.ShapeDtypeStruct((layer + 1, srows, RWKV_HEAD, batch), F32)],
        scratch_shapes=[tposed] * 7,
        compiler_params=_cparams("arbitrary"),
        name="rwkv_step",
    )(u, shift0, s_all, *prev_args, *[_parg(a) for a in params])


def _s5_body(u_ref, hre0_ref, him0_ref, are_ref, aim_ref, bmat_ref, cmat_ref, d_ref, gw_ref, gb_ref,
             y_ref, hre_ref, him_ref, hs_scr, tm_scr, *, steps, batch_major):
    c = pl.program_id(1)
    ns = S5_WIDTH
    bsub = hre_ref.shape[0]

    @pl.when(c == 0)
    def _():
        hre_ref[...] = hre0_ref[...]
        him_ref[...] = him0_ref[...]

    if batch_major:
        for b in range(bsub):
            tm_scr[:, b, :] = u_ref[b]
        u = tm_scr[...].reshape(steps * bsub, GROUP_WIDTH)
    else:
        u = u_ref[...].reshape(steps * bsub, GROUP_WIDTH)
    are = jnp.broadcast_to(are_ref[...], (bsub, ns))
    aim = jnp.broadcast_to(aim_ref[...], (bsub, ns))
    hre, him = hre_ref[...], him_ref[...]
    sub = min(S5_SUB, steps)
    rows = sub * bsub
    outs = []
    hs_scr[...] = _dot(u.astype(BF16), bmat_ref[...])
    for k in range(steps // sub):
        r0 = k * rows
        u_k = u[r0:r0 + rows]
        for t in range(sub):
            rs = slice(r0 + t * bsub, r0 + (t + 1) * bsub)
            hre, him = (are * hre - aim * him + hs_scr[rs, 0:ns], are * him + aim * hre + hs_scr[rs, ns:2 * ns])
            hs_scr[rs, 0:ns] = hre
            hs_scr[rs, ns:2 * ns] = him
        y = _dot(hs_scr[r0:r0 + rows, :].astype(BF16), cmat_ref[...]) + u_k * d_ref[...]
        y = _gelu_tanh(y)
        yy = _dot(y.astype(BF16), gw_ref[...]) + gb_ref[...]
        outs.append(yy[:, 0:GROUP_WIDTH] * _sigmoid(yy[:, GROUP_WIDTH:2 * GROUP_WIDTH]))
    hre_ref[...] = hre
    him_ref[...] = him
    out = jnp.concatenate(outs, axis=0).reshape(steps, bsub, GROUP_WIDTH)
    if batch_major:
        tm_scr[...] = out
        for b in range(bsub):
            y_ref[b] = tm_scr[:, b, :]
    else:
        y_ref[...] = out


def _time_specs(u, batch_major, chunk):
    if batch_major:
        batch, seq, _ = u.shape
        steps = min(chunk, seq)
        bsub = SUBLANES
        spec = pl.BlockSpec((bsub, steps, GROUP_WIDTH), lambda b, c: (b, c, 0))
    else:
        seq, batch, _ = u.shape
        steps = min(chunk, seq)
        bsub = min(batch, SUBLANES * max(1, chunk // steps))
        spec = pl.BlockSpec((steps, bsub, GROUP_WIDTH), lambda b, c: (c, b, 0))
    return batch, seq, steps, bsub, spec


def _s5(u, hre0, him0, lp, *, batch_major):
    batch, seq, steps, bsub, tspec = _time_specs(u, batch_major, TM_CHUNK)
    hspec =pl.BlockSpec((bsub, S5_WIDTH), lambda b, c: (b, 0))
    consts = (lp["s5_are"], lp["s5_aim"], lp["s5_bmat"], lp["s5_cmat"], lp["s5_d"], lp["s5_gw"], lp["s5_gb"])
    return pl.pallas_call(
        functools.partial(_s5_body, steps=steps, batch_major=batch_major),
        grid=(batch // bsub, seq // steps),
        in_specs=[tspec, hspec, hspec] + [_pspec(a) for a in consts],
        out_specs=[tspec, hspec, hspec],
        out_shape=[jax.ShapeDtypeStruct(u.shape, F32),
                   jax.ShapeDtypeStruct((batch, S5_WIDTH), F32),
                   jax.ShapeDtypeStruct((batch, S5_WIDTH), F32)],
        scratch_shapes=[pltpu.VMEM((steps * bsub, 2 * S5_WIDTH), F32),
                        pltpu.VMEM((steps, bsub, GROUP_WIDTH), F32)],
        compiler_params=_cparams("parallel", "arbitrary"),
        name="s5",
    )(u, hre0, him0, *[_parg(a) for a in consts])


def _pool_body(u_ref, buf0_ref, pw_ref, sc_ref, y_ref, buf_ref, f_scr, tm_scr, *, steps, pos0, batch_major):
    c = pl.program_id(1)
    bsub = f_scr.shape[1]
    GW = GROUP_WIDTH
    halo = POOL_BUF + 1

    @pl.when(c == 0)
    def _():
        f_scr[0] = jnp.zeros((bsub, GW), F32)
        f_scr[1:halo] = buf0_ref[...]

    if batch_major:
        for b in range(bsub):
            f_scr[halo:halo + steps, b, :] = u_ref[b]
    else:
        f_scr[halo:halo + steps] = u_ref[...]
    f = f_scr[...]
    u = f[halo:halo + steps]
    s2 = f[1:] + f[:-1]
    s4 = s2[2:] + s2[:-2]
    s8 = s4[4:] + s4[:-4]
    s16 = s8[8:] + s8[:-8]
    f_scr[0:halo] = f[steps:steps + halo]
    lane = lax.broadcasted_iota(jnp.int32, (steps, bsub, GW), 2)
    tpos = lax.broadcasted_iota(jnp.int32, (steps, bsub, GW), 0) + (pos0 + 1) + c * steps
    win = jnp.where(lane < POOL_CH, s2[halo - 1:halo - 1 + steps],
                    jnp.where(lane < 2 * POOL_CH, s4[halo - 3:halo - 3 + steps],
                              jnp.where(lane < 3 * POOL_CH, s8[halo - 7:halo - 7 + steps],
                                        s16[halo - 15:halo - 15 + steps])))
    wlen = jnp.where(lane < POOL_CH, POOL_WINDOWS[0],
                     jnp.where(lane < 2 * POOL_CH, POOL_WINDOWS[1],
                               jnp.where(lane < 3 * POOL_CH, POOL_WINDOWS[2], POOL_WINDOWS[3])))
    cnt = jnp.minimum(tpos, wlen).astype(F32)
    pooled = (win / cnt - u).reshape(steps * bsub, GW)
    y = (_dot(pooled.astype(BF16), pw_ref[...]) * sc_ref[...]).reshape(steps, bsub, GW)
    if batch_major:
        tm_scr[...] = y
        for b in range(bsub):
            y_ref[b] = tm_scr[:, b, :]
    else:
        y_ref[...] = y

    @pl.when(c == pl.num_programs(1) - 1)
    def _():
        buf_ref[...] = f_scr[1:halo]


def _pool(u, buf0, lp, *, pos0, batch_major, layer=None):
    batch, seq, steps, bsub, tspec = _time_specs(u, batch_major, POOL_CHUNK)
    bblock =(POOL_BUF, bsub, GROUP_WIDTH)
    bspec = pl.BlockSpec(bblock, lambda b, c: (0, b, 0))
    if layer is None:
        bspec_in = bspec
    else:
        bspec_in = pl.BlockSpec((None,) + bblock, lambda b, c: (layer, 0, b, 0))
    return pl.pallas_call(
        functools.partial(_pool_body, steps=steps, pos0=pos0, batch_major=batch_major),
        grid=(batch // bsub, seq // steps),
        in_specs=[tspec, bspec_in, _pspec(lp["pool_w"]), _pspec(lp["pool_scale"])],
        out_specs=[tspec, bspec],
        out_shape=[jax.ShapeDtypeStruct(u.shape, F32), jax.ShapeDtypeStruct((POOL_BUF, batch, GROUP_WIDTH), F32)],
        scratch_shapes=[pltpu.VMEM((POOL_BUF + 1 + steps, bsub, GROUP_WIDTH), F32),
                        pltpu.VMEM((steps, bsub, GROUP_WIDTH), F32)],
        compiler_params=_cparams("parallel", "arbitrary"),
        name="pool",
    )(u, buf0, _parg(lp["pool_w"]), _parg(lp["pool_scale"]))


def _block_diag(blocks):
    n, g, r, c = blocks.shape
    eye = jnp.eye(g, dtype=blocks.dtype)
    return (eye[None, :, None, :, None] * blocks[:, :, :, None, :]).reshape(n, g * r, g * c)


def _stacked_params(P):
    row = lambda a: a.reshape(a.shape[0], 1, -1)
    pad_lanes = lambda a: jnp.pad(a, ((0, 0), (0, LANES - a.shape[1])))
    bf = lambda a: a.astype(BF16)

    lam = lax.complex(P["s5_lam_re"], P["s5_lam_im"])
    a_bar = jnp.exp(lam * jnp.exp(P["s5_log_step"])[..., None])
    b_bar = ((a_bar - 1.0) / lam)[..., None] * lax.complex(P["s5_b_re"], P["s5_b_im"])
    b_t = jnp.swapaxes(b_bar, 2, 3)
    bmat = jnp.concatenate([_block_diag(jnp.real(b_t)), _block_diag(jnp.imag(b_t))], axis=2)
    c_t = jnp.swapaxes(lax.complex(P["s5_c_re"], P["s5_c_im"]), 2, 3)
    cmat = jnp.concatenate([_block_diag(jnp.real(c_t)), -_block_diag(jnp.imag(c_t))], axis=1)

    out = dict(
        norm_ffn1=row(P["norm_ffn1"]), ffn1_in=P["ffn1_in"], ffn1_out=P["ffn1_out"],
        norm_mix=row(P["norm_mix"]),
        w_in=jnp.transpose(P["w_in"], (2, 0, 1)),
        conv_w=P["ssd_conv_w"], conv_b=row(P["ssd_conv_b"]),
        dt_bias=row(pad_lanes(P["ssd_dt_bias"])), a_log=row(pad_lanes(P["ssd_a_log"])),
        a_neg_exp=row(jnp.repeat(-jnp.exp(P["ssd_a_log"]), SSD_HEAD_DIM, axis=1)),
        d_skip=row(jnp.repeat(P["ssd_d"], SSD_HEAD_DIM, axis=1)), ssd_norm=row(P["ssd_norm"]),
        s5_are=row(jnp.real(a_bar)), s5_aim=row(jnp.imag(a_bar)), s5_bmat=bf(bmat), s5_cmat=bf(cmat),
        s5_d=row(P["s5_d"]), s5_gw=bf(P["s5_glu_w"]), s5_gb=row(P["s5_glu_b"]),
        pool_w=bf(_block_diag(P["pool_w"])), pool_scale=row(P["pool_scale"]),
        w_out=bf(P["w_out"]),
        norm_ffn2=row(P["norm_ffn2"]), ffn2_in=P["ffn2_in"], ffn2_out=P["ffn2_out"],
    )
    for name in _RWKV_PARAM_NAMES:
        a = P["rwkv_" + name]
        out["rwkv_" + name] = bf(a) if name in ("w2", "a2", "g2") else row(a)
    return out


def _layer_params(stacked, l):
    lp = {k: _Layered((v, l)) for k, v in stacked.items()}
    lp["rwkv"] = {n: lp["rwkv_" + n] for n in _RWKV_PARAM_NAMES}
    lp["head_expand"] = jnp.pad(jnp.repeat(jnp.eye(SSD_HEADS, dtype=F32), SSD_HEAD_DIM, axis=1),
                                ((0, LANES - SSD_HEADS), (0, 0)))
    return lp


def _mixers_prompt(lp, proj, *, batch, seq):
    z, xbc, ur, us5, upool, dtr = proj
    y_ssd, conv_new, ssd_new = _ssd(z, xbc, dtr, lp, batch=batch, seq=seq)
    y_rwkv, shift_new, rwkv_new = _rwkv(ur, lp["rwkv"], batch=batch, seq=seq)
    zeros = jnp.zeros((batch, S5_WIDTH), F32)
    bm = lambda a: a.reshape(batch, seq, a.shape[-1])
    rows = lambda a: a.reshape(batch * seq, a.shape[-1])
    y_s5, s5re, s5im = _s5(bm(us5), zeros, zeros, lp, batch_major=True)
    y_pool, pool_new = _pool(bm(upool), jnp.zeros((POOL_BUF, batch, GROUP_WIDTH), F32), lp, pos0=0,
                             batch_major=True)
    ys = (y_ssd, y_rwkv, rows(y_s5), rows(y_pool))
    states = (conv_new, ssd_new, shift_new, rwkv_new, s5re.reshape(batch, S5_GROUPS, S5_STATE),
              s5im.reshape(batch, S5_GROUPS, S5_STATE), jnp.swapaxes(pool_new, 0, 1))
    return ys, states


def _mixers_decode(lp, proj, states, done, *, batch, seq, layer):
    z, xbc, ur, us5, upool, dtr = proj
    shift0, s5re0, s5im0 = (states[i][layer] for i in (2, 4, 5))
    ssd_done, rwkv_done = (done[1], done[3]) if layer else (None, None)
    y_ssd, conv_new, ssd_new = _ssd_step(z, xbc, dtr, states[0], states[1], ssd_done, lp, batch=batch, seq=seq,
                                         layer=layer)
    y_rwkv, rwkv_new = _rwkv_step(ur, shift0, states[3], rwkv_done, lp["rwkv"], batch=batch, seq=seq, layer=layer)
    shift_new = ur[(seq - 1) * batch:, :]
    tm = lambda a: a.reshape(seq, batch, a.shape[-1])
    y_s5, s5re, s5im = _s5(tm(us5), s5re0.reshape(batch, S5_WIDTH), s5im0.reshape(batch, S5_WIDTH), lp,
                           batch_major=False)
    y_pool, pool_new = _pool(tm(upool), states[6], lp, pos0=PAST_LEN, batch_major=False, layer=layer)
    rows = lambda a: a.reshape(seq * batch, a.shape[-1])
    ys = (y_ssd, y_rwkv, rows(y_s5), rows(y_pool))
    new_states = (jnp.swapaxes(conv_new, 0, 1), ssd_new, shift_new, rwkv_new,
                  s5re.reshape(batch, S5_GROUPS, S5_STATE), s5im.reshape(batch, S5_GROUPS, S5_STATE),
                  jnp.swapaxes(pool_new, 0, 1))
    return ys, new_states


_WIDTHS = (GROUP_WIDTH, SSD_CONV_DIM, RWKV_PROJ, GROUP_WIDTH, GROUP_WIDTH, LANES)


def _trunk(x_p, x_s, layer_params, norm_final, mixers_p, mixers_s):
    st_p, st_s = [], []
    mix_p, mix_s, lp = None, None, None
    for l, lp_next in enumerate(layer_params):
        if l > 0:
            x_s, wg, wu, wo = _ffn_cast(x_s, lp["norm_ffn2"], lp["ffn2_in"], lp["ffn2_out"], mix=mix_s, wmix=lp["w_out"])
            x_p = _ffn(x_p, lp["norm_ffn2"], wg, wu, wo, mix=mix_p, wmix=lp["w_out"])
        lp = lp_next
        x_s, wg, wu, wo = _ffn_cast(x_s, lp["norm_ffn1"], lp["ffn1_in"], lp["ffn1_out"])
        x_p = _ffn(x_p, lp["norm_ffn1"], wg, wu, wo)
        proj_s, w_all = _inproj_cast(x_s, lp["norm_mix"], lp["w_in"], _WIDTHS)
        mix_p, st = mixers_p(l, lp, _inproj(x_p, lp["norm_mix"], w_all, _WIDTHS), st_p[-1] if st_p else None)
        st_p.append(st)
        mix_s, st = mixers_s(l, lp, proj_s, st_s[-1] if st_s else None)
        st_s.append(st)
    x_s, wg, wu, wo = _ffn_cast(x_s, lp["norm_ffn2"], lp["ffn2_in"], lp["ffn2_out"], mix=mix_s, wmix=lp["w_out"],
                                gf=norm_final)
    x_p = _ffn(x_p, lp["norm_ffn2"], wg, wu, wo, mix=mix_p, wmix=lp["w_out"], gf=norm_final)
    return (x_p, x_s), (st_p, st_s)


def kernel(x_prompt, x_sample, state_ssd_conv, state_ssd, state_rwkv_shift, state_rwkv, state_s5_re, state_s5_im, state_pool, norm_ffn1, ffn1_in, ffn1_out, norm_mix, w_in, ssd_conv_w, ssd_conv_b, ssd_dt_bias, ssd_a_log, ssd_d, ssd_norm, rwkv_mu, rwkv_w0, rwkv_w2, rwkv_a0, rwkv_a2, rwkv_g2, rwkv_k_k, rwkv_k_a, rwkv_r_k, rwkv_ln_g, rwkv_ln_b, s5_lam_re, s5_lam_im, s5_log_step, s5_b_re, s5_b_im, s5_c_re, s5_c_im, s5_d, s5_glu_w, s5_glu_b, pool_w, pool_scale, w_out, norm_ffn2, ffn2_in, ffn2_out, norm_final):
    P = dict(norm_ffn1=norm_ffn1, ffn1_in=ffn1_in, ffn1_out=ffn1_out, norm_mix=norm_mix, w_in=w_in,
             ssd_conv_w=ssd_conv_w, ssd_conv_b=ssd_conv_b, ssd_dt_bias=ssd_dt_bias, ssd_a_log=ssd_a_log,
             ssd_d=ssd_d, ssd_norm=ssd_norm, rwkv_mu=rwkv_mu, rwkv_w0=rwkv_w0, rwkv_w2=rwkv_w2, rwkv_a0=rwkv_a0,
             rwkv_a2=rwkv_a2, rwkv_g2=rwkv_g2, rwkv_k_k=rwkv_k_k, rwkv_k_a=rwkv_k_a,
             rwkv_r_k=rwkv_r_k.reshape(rwkv_r_k.shape[0], -1), rwkv_ln_g=rwkv_ln_g, rwkv_ln_b=rwkv_ln_b,
             s5_lam_re=s5_lam_re, s5_lam_im=s5_lam_im, s5_log_step=s5_log_step, s5_b_re=s5_b_re, s5_b_im=s5_b_im,
             s5_c_re=s5_c_re, s5_c_im=s5_c_im, s5_d=s5_d, s5_glu_w=s5_glu_w, s5_glu_b=s5_glu_b, pool_w=pool_w,
             pool_scale=pool_scale, w_out=w_out, norm_ffn2=norm_ffn2, ffn2_in=ffn2_in, ffn2_out=ffn2_out)
    depth = norm_ffn1.shape[0]
    bp, tp, d = x_prompt.shape
    bs, ts, _ = x_sample.shape
    stacked = _stacked_params(P)
    layer_params = [_layer_params(stacked, l) for l in range(depth)]
    gf = norm_final.reshape(1, -1)
    sample_states = (state_ssd_conv, state_ssd, state_rwkv_shift, state_rwkv, state_s5_re, state_s5_im, state_pool)
    rwkv_rows = RWKV_HEADS * RWKV_HEAD
    decode_states = (jnp.swapaxes(state_ssd_conv, 1, 2), state_ssd, state_rwkv_shift,
                     jnp.transpose(state_rwkv, (0, 2, 3, 4, 1)).reshape(depth, rwkv_rows, RWKV_HEAD, bs),
                     state_s5_re, state_s5_im, jnp.swapaxes(state_pool, 1, 2))

    x_s = jnp.swapaxes(x_sample, 0, 1).reshape(ts * bs, d)
    (y_p, y_s), (st_p, st_s) = _trunk(
        x_prompt.reshape(bp * tp, d), x_s, layer_params, gf,
        lambda l, lp, proj, done: _mixers_prompt(lp, proj, batch=bp, seq=tp),
        lambda l, lp, proj, done: _mixers_decode(lp, proj, decode_states, done, batch=bs, seq=ts, layer=l))
    outs = [y_p.reshape(bp, tp, d), jnp.swapaxes(y_s.reshape(ts, bs, d), 0, 1)]
    for i, ref_state in enumerate(sample_states):
        outs.append(jnp.stack([st[i] for st in st_p]))
        if i == 1:
            outs.append(st_s[-1][i].reshape(ref_state.shape))
        elif i == 3:
            s_new = st_s[-1][i].reshape(depth, RWKV_HEADS, RWKV_HEAD, RWKV_HEAD, bs)
            outs.append(jnp.transpose(s_new, (0, 4, 1, 2, 3)))
        else:
            outs.append(jnp.stack([st[i] for st in st_s]))
    return tuple(outs)
```

```python
import functools
import math

import jax
import jax.numpy as jnp
from jax import lax
from jax.experimental import pallas as pl
from jax.experimental.pallas import tpu as pltpu

F32 = jnp.float32
BF16 = jnp.bfloat16
HIGHEST = lax.Precision.HIGHEST

SUBLANES = 8
LANES = 128
VMEM_LIMIT_BYTES = 56 * 1024 * 1024

GROUP_WIDTH = 256
SSD_HEAD_DIM = 64
SSD_HEADS = 4
SSD_GROUPS = 2
SSD_STATE = 128
SSD_CONV = 4
SSD_CONV_DIM = GROUP_WIDTH + 2 * SSD_GROUPS * SSD_STATE
SSD_CHUNK = 128
SSD_GROUP = 8
LOG2_E = math.log2(math.e)
RWKV_HEAD = 64
RWKV_HEADS = 4
RWKV_PROJ = 1024
RWKV_LN_EPS = 64e-5
RWKV_CHUNK = 64
RWKV_GROUP = 16
S5_GROUPS = 16
S5_STATE = 64
S5_WIDTH = S5_GROUPS * S5_STATE
POOL_WINDOWS = (2, 4, 8, 16)
POOL_CH = 64
POOL_BUF = 15
RMS_EPS = 1e-6
PAST_LEN = 16384

ROW_TILE = 1024
FFN_CHUNK = 256
TM_CHUNK = 128
POOL_CHUNK = 512
S5_SUB = 64
SSD_STEP_TILES = 16
RWKV_STEP_TILES = 16


def _cparams(*sem):
    return pltpu.CompilerParams(dimension_semantics=sem, vmem_limit_bytes=VMEM_LIMIT_BYTES)


def _dot(a, b, **kw):
    return jnp.dot(a, b, preferred_element_type=F32, **kw)


def _dot_nt(a, b):
    return lax.dot_general(a, b, (((1,), (1,)), ((), ())), preferred_element_type=F32)


def _dot_tn(a, b):
    return lax.dot_general(a, b, (((0,), (0,)), ((), ())), preferred_element_type=F32)


def _sigmoid(x):
    return 0.5 * jnp.tanh(0.5 * x) + 0.5


def _silu(x):
    h = 0.5 * x
    return h + h * jnp.tanh(h)


def _softplus(x):
    return jnp.maximum(x, 0.0) + jnp.log(1.0 + jnp.exp(-jnp.abs(x)))


def _gelu_tanh(x):
    c = math.sqrt(2.0 / math.pi)
    return x * (0.5 * (1.0 + jnp.tanh(c * (x + 0.044715 * (x * x * x)))))


def _rms(x, g):
    return x * lax.rsqrt(jnp.mean(x * x, axis=-1, keepdims=True) + RMS_EPS) * g


def _full_spec(shape):
    n = len(shape)
    return pl.BlockSpec(shape, lambda *_: (0,) * n)


class _Layered(tuple):
    pass


def _pspec(p, single=False):
    mode = pl.Buffered(1) if single else None
    if isinstance(p, _Layered):
        a, l = p
        return pl.BlockSpec((None,) + a.shape[1:], lambda *_: (l,) + (0,) * (a.ndim - 1), pipeline_mode=mode)
    n = p.ndim
    return pl.BlockSpec(p.shape, lambda *_: (0,) * n, pipeline_mode=mode)


def _parg(p):
    return p[0] if isinstance(p, _Layered) else p


def _mix_residual(x, y_refs, wmix_ref):
    for j, y_ref in enumerate(y_refs):
        x = x + _dot(y_ref[...].astype(BF16), wmix_ref[j * GROUP_WIDTH:(j + 1) * GROUP_WIDTH, :])
    return x


def _swiglu_chunk(h, wg, wu, wo):
    act = (_silu(_dot(h, wg)) * _dot(h, wu)).astype(BF16)
    return _dot(act, wo)


def _ffn_body(*refs, has_mix, final_norm):
    it = iter(refs)
    x = next(it)[...]
    if has_mix:
        y_refs = [next(it) for _ in range(4)]
        x = _mix_residual(x, y_refs, next(it))
    g_ref, wg_ref, wu_ref, wo_ref = next(it), next(it), next(it), next(it)
    gf_ref = next(it) if final_norm else None
    o_ref = next(it)
    h = _rms(x, g_ref[...]).astype(BF16)
    acc = jnp.zeros_like(x)
    for c in range(wo_ref.shape[0] // FFN_CHUNK):
        cols = slice(c * FFN_CHUNK, (c + 1) * FFN_CHUNK)
        acc = acc + _swiglu_chunk(h, wg_ref[:, cols], wu_ref[:, cols], wo_ref[cols, :])
    x = x + 0.5 * acc
    if final_norm:
        x = _rms(x, gf_ref[...])
    o_ref[...] = x


def _ffn(x, g, wg, wu, wo, mix=None, wmix=None, gf=None):
    rows, d = x.shape
    row_spec = lambda w: pl.BlockSpec((ROW_TILE, w), lambda i: (i, 0))
    args, specs = [x], [row_spec(d)]
    if mix is not None:
        for y in mix:
            args.append(y)
            specs.append(row_spec(y.shape[1]))
        args.append(_parg(wmix))
        specs.append(_pspec(wmix, single=True))
    for a in (g, wg, wu, wo) + ((gf,) if gf is not None else ()):
        args.append(_parg(a))
        specs.append(_pspec(a, single=True))
    return pl.pallas_call(
        functools.partial(_ffn_body, has_mix=mix is not None, final_norm=gf is not None),
        grid=(rows // ROW_TILE,),
        in_specs=specs,
        out_specs=row_spec(d),
        out_shape=jax.ShapeDtypeStruct((rows, d), F32),
        compiler_params=_cparams("parallel"),
        name="ffn",
    )(*args)


def _ffn_cast_body(*refs, has_mix, final_norm):
    it = iter(refs)
    x_ref = next(it)
    if has_mix:
        y_refs = [next(it) for _ in range(4)]
        wmix_ref = next(it)
    g_ref, wg_ref, wu_ref, wo_ref = next(it), next(it), next(it), next(it)
    gf_ref = next(it) if final_norm else None
    o_ref, wg_out, wu_out, wo_out, x_scr, h_scr, acc_scr = (next(it) for _ in range(7))
    c = pl.program_id(0)

    @pl.when(c == 0)
    def _():
        x = x_ref[...]
        if has_mix:
            x = _mix_residual(x, y_refs, wmix_ref)
        x_scr[...] = x
        h_scr[...] = _rms(x, g_ref[...]).astype(BF16)
        acc_scr[...] = jnp.zeros(acc_scr.shape, F32)

    wg = wg_ref[...].astype(BF16)
    wu = wu_ref[...].astype(BF16)
    wo = wo_ref[...].astype(BF16)
    wg_out[...] = wg
    wu_out[...] = wu
    wo_out[...] = wo
    acc_scr[...] += _swiglu_chunk(h_scr[...], wg, wu, wo)

    @pl.when(c == pl.num_programs(0) - 1)
    def _():
        x = x_scr[...] + 0.5 * acc_scr[...]
        if final_norm:
            x = _rms(x, gf_ref[...])
        o_ref[...] = x


def _ffn_cast(x, g, wi, wo, mix=None, wmix=None, gf=None):
    rows, d = x.shape
    wi_all, l = wi
    wo_all, _ = wo
    d_ff = wo_all.shape[1]
    nchunks = d_ff // FFN_CHUNK
    args, specs = [x], [_full_spec(x.shape)]
    if mix is not None:
        for y in mix:
            args.append(y)
            specs.append(_full_spec(y.shape))
        args.append(_parg(wmix))
        specs.append(_pspec(wmix, single=True))
    args += [_parg(g), wi_all, wi_all, wo_all]
    specs += [_pspec(g),
              pl.BlockSpec((None, d, FFN_CHUNK), lambda c: (l, 0, c)),
              pl.BlockSpec((None, d, FFN_CHUNK), lambda c: (l, 0, c + nchunks)),
              pl.BlockSpec((None, FFN_CHUNK, d), lambda c: (l, c, 0))]
    if gf is not None:
        args.append(gf)
        specs.append(_full_spec(gf.shape))
    col_spec = pl.BlockSpec((d, FFN_CHUNK), lambda c: (0, c))
    return pl.pallas_call(
        functools.partial(_ffn_cast_body, has_mix=mix is not None, final_norm=gf is not None),
        grid=(nchunks,),
        in_specs=specs,
        out_specs=[_full_spec(x.shape), col_spec, col_spec, pl.BlockSpec((FFN_CHUNK, d), lambda c: (c, 0))],
        out_shape=[jax.ShapeDtypeStruct((rows, d), F32), jax.ShapeDtypeStruct((d, d_ff), BF16),
                   jax.ShapeDtypeStruct((d, d_ff), BF16), jax.ShapeDtypeStruct((d_ff, d), BF16)],
        scratch_shapes=[pltpu.VMEM((rows, d), F32), pltpu.VMEM((rows, d), BF16), pltpu.VMEM((rows, d), F32)],
        compiler_params=_cparams("arbitrary"),
        name="ffn_cast",
    )(*args)


def _inproj_body(x_ref, g_ref, wt_ref, *o_refs):
    h = _rms(x_ref[...], g_ref[...]).astype(BF16)
    off = 0
    for o_ref in o_refs:
        n = o_ref.shape[-1]
        o_ref[...] = _dot_nt(h, wt_ref[off:off + n, :])
        off += n


def _inproj_cast_body(x_ref, g_ref, win_ref, *o_refs, layer):
    *proj_refs, wall_ref = o_refs
    split = GROUP_WIDTH + SSD_CONV_DIM
    wt = win_ref[:, layer, :]
    tail = wt.shape[0] - split - SSD_HEADS
    wall_ref[0:split, :] = wt[0:split].astype(BF16)
    wall_ref[split:split + tail, :] = wt[split + SSD_HEADS:].astype(BF16)
    dt_rows = jnp.concatenate([wt[split:split + SSD_HEADS], jnp.zeros((LANES - SSD_HEADS, wt.shape[1]), F32)], axis=0)
    wall_ref[split + tail:, :] = dt_rows.astype(BF16)
    _inproj_body(x_ref, g_ref, wall_ref, *proj_refs)


def _inproj_cast(x, g, w_in, widths):
    rows, d = x.shape
    wt_all, l = w_in
    outs = pl.pallas_call(
        functools.partial(_inproj_cast_body, layer=l),
        grid=(1,),
        in_specs=[_full_spec(x.shape), _pspec(g),
                  pl.BlockSpec(wt_all.shape, lambda i: (0, 0, 0), pipeline_mode=pl.Buffered(1))],
        out_specs=[_full_spec((rows, n)) for n in widths] + [_full_spec((sum(widths), d))],
        out_shape=[jax.ShapeDtypeStruct((rows, n), F32) for n in widths]
                  + [jax.ShapeDtypeStruct((sum(widths), d), BF16)],
        compiler_params=_cparams("arbitrary"),
        name="inproj_cast",
    )(x, _parg(g), wt_all)
    return outs[:-1], outs[-1]


def _inproj(x, g, w, widths):
    rows, d = x.shape
    row_spec = lambda w_: pl.BlockSpec((ROW_TILE, w_), lambda i: (i, 0))
    return pl.pallas_call(
        _inproj_body,
        grid=(rows // ROW_TILE,),
        in_specs=[row_spec(d), _pspec(g), _pspec(w, single=True)],
        out_specs=[row_spec(n) for n in widths],
        out_shape=[jax.ShapeDtypeStruct((rows, n), F32) for n in widths],
        compiler_params=_cparams("parallel"),
        name="inproj",
    )(x, _parg(g), _parg(w))


def _ssd_body(z_ref, xbc_ref, dt_ref, cw_ref, cb_ref, dtb_ref, alog_ref, dsk_ref, ng_ref,
              y_ref, conv_ref, hout_ref, xpad_scr, h_scr, *, chunk, group):
    L, G = chunk, group
    GL = G * L
    c = pl.program_id(1)
    pad = SUBLANES
    halo = SSD_CONV - 1
    hpg = SSD_HEADS // SSD_GROUPS
    assert hpg == 2 and hpg * SSD_HEAD_DIM == SSD_STATE

    @pl.when(c == 0)
    def _():
        xpad_scr[0:pad, :] = jnp.zeros((pad, SSD_CONV_DIM), F32)
        h_scr[...] = jnp.zeros(h_scr.shape, F32)

    xpad_scr[pad:pad + GL, :] = xbc_ref[...]
    xfull = xpad_scr[...]
    conv = cb_ref[...] + cw_ref[halo:halo + 1, :] * xfull[pad:pad + GL]
    for j in range(halo):
        conv = conv + cw_ref[j:j + 1, :] * pltpu.roll(xfull, halo - j, axis=0)[pad:pad + GL]
    xpad_scr[pad - halo:pad, :] = xpad_scr[pad + GL - halo:pad + GL, :]
    conv = _silu(conv)
    xs = conv[:, 0:GROUP_WIDTH]
    bm = conv[:, GROUP_WIDTH:2 * GROUP_WIDTH].astype(BF16)
    cm = conv[:, 2 * GROUP_WIDTH:3 * GROUP_WIDTH].astype(BF16)

    row = lax.broadcasted_iota(jnp.int32, (L, L), 0)
    col = lax.broadcasted_iota(jnp.int32, (L, L), 1)
    causal = row >= col
    tril = jnp.where(causal, 1.0, 0.0).astype(F32)
    dt = _softplus(dt_ref[...] + dtb_ref[...])
    da = dt * (-jnp.exp(alog_ref[...]) * LOG2_E)
    acs = [_dot(tril, da[i * L:(i + 1) * L, :], precision=HIGHEST) for i in range(G)]
    acs_t = [a.T for a in acs]
    e_acs = [jnp.exp2(a) for a in acs]
    e_end = [jnp.exp2(a[L - 1:L, :] - a) for a in acs]
    e_last = [jnp.exp2(a[L - 1:L, :]) for a in acs]

    keys = [(i, g) for i in range(G) for g in range(SSD_GROUPS)]
    rows_of = lambda x, i: x[i * L:(i + 1) * L]
    lanes_of = lambda x, g: x[:, g * SSD_STATE:(g + 1) * SSD_STATE]
    lane_lo = lax.broadcasted_iota(jnp.int32, (L, hpg * SSD_HEAD_DIM), 1) < SSD_HEAD_DIM
    row_lo = lax.broadcasted_iota(jnp.int32, (hpg * SSD_HEAD_DIM, SSD_STATE), 0) < SSD_HEAD_DIM
    head_cols = lambda a, g: jnp.where(lane_lo, a[:, g * hpg:g * hpg + 1], a[:, g * hpg + 1:g * hpg + 2])
    bg = {(i, g): lanes_of(rows_of(bm, i), g) for i, g in keys}
    cg = {(i, g): lanes_of(rows_of(cm, i), g) for i, g in keys}
    scores = {k: _dot_nt(cg[k], bg[k]) for k in keys}
    xdt = {(i, g): lanes_of(rows_of(xs, i), g) * head_cols(rows_of(dt, i), g) for i, g in keys}
    decay = {(i, h): jnp.exp2(jnp.where(causal, acs[i][:, h:h + 1] - acs_t[i][h:h + 1, :], -jnp.inf))
             for i in range(G) for h in range(SSD_HEADS)}
    p_mat = {(i, g): jnp.concatenate([(scores[(i, g)] * decay[(i, g * hpg + k)]).astype(BF16) for k in range(hpg)],
                                     axis=1) for i, g in keys}
    y_in = {k: _dot(p_mat[k], _bd(xdt[k].astype(BF16))) for k in keys}
    st = {(i, g): _dot_tn((xdt[(i, g)] * head_cols(e_end[i], g)).astype(BF16), bg[(i, g)]) for i, g in keys}

    y_rows = []
    for i in range(G):
        ys = []
        for g in range(SSD_GROUPS):
            h_prev = h_scr[g * hpg:(g + 1) * hpg].reshape(hpg * SSD_HEAD_DIM, SSD_STATE)
            ys.append(y_in[(i, g)] + _dot_nt(cg[(i, g)], h_prev.astype(BF16)) * head_cols(e_acs[i], g))
            keep = jnp.where(row_lo, e_last[i][:, g * hpg:g * hpg + 1], e_last[i][:, g * hpg + 1:g * hpg + 2])
            h_scr[g * hpg:(g + 1) * hpg] = (h_prev * keep + st[(i, g)]).reshape(hpg, SSD_HEAD_DIM, SSD_STATE)
        y_rows.append(jnp.concatenate(ys, axis=-1))
    y = jnp.concatenate(y_rows, axis=0) + xs * dsk_ref[...]
    y = y * _silu(z_ref[...])
    y_ref[...] = _rms(y, ng_ref[...])

    @pl.when(c == pl.num_programs(1) - 1)
    def _():
        hout_ref[0] = h_scr[...]
        conv_ref[0] = xpad_scr[pad - halo:pad, :]


def _ssd(z, xbc, dtr, lp, *, batch, seq):
    chunk = SSD_CHUNK
    rows = chunk * SSD_GROUP
    nc = seq // rows
    rspec = lambda w: pl.BlockSpec((rows, w), lambda b, c: (b * nc + c, 0))
    consts = (lp["conv_w"], lp["conv_b"], lp["dt_bias"], lp["a_log"], lp["d_skip"], lp["ssd_norm"])
    return pl.pallas_call(
        functools.partial(_ssd_body, chunk=chunk, group=SSD_GROUP),
        grid=(batch, nc),
        in_specs=[rspec(GROUP_WIDTH), rspec(SSD_CONV_DIM), rspec(LANES)] + [_pspec(a) for a in consts],
        out_specs=[rspec(GROUP_WIDTH),
                   pl.BlockSpec((1, SSD_CONV - 1, SSD_CONV_DIM), lambda b, c: (b, 0, 0)),
                   pl.BlockSpec((1, SSD_HEADS, SSD_HEAD_DIM, SSD_STATE), lambda b, c: (b, 0, 0, 0))],
        out_shape=[jax.ShapeDtypeStruct((batch * seq, GROUP_WIDTH), F32),
                   jax.ShapeDtypeStruct((batch, SSD_CONV - 1, SSD_CONV_DIM), F32),
                   jax.ShapeDtypeStruct((batch, SSD_HEADS, SSD_HEAD_DIM, SSD_STATE), F32)],
        scratch_shapes=[pltpu.VMEM((SUBLANES + rows, SSD_CONV_DIM), F32),
                        pltpu.VMEM((SSD_HEADS, SSD_HEAD_DIM, SSD_STATE), F32)],
        compiler_params=_cparams("parallel", "arbitrary"),
        name="ssd",
    )(z, xbc, dtr, *[_parg(a) for a in consts])


def _ssd_step_body(z_ref, xbc_ref, dt_ref, conv0_ref, h0_ref, *rest, seq, batch, layer):
    hdone_ref, rest = (rest[0], rest[1:]) if layer else (None, rest)
    (cw_ref, cb_ref, dtb_ref, aneg_ref, dsk_ref, ng_ref, hexp_ref, y_ref, conv_ref, hout_ref,
     xs_scr, bm_scr, cm_scr, xdt_scr, dec_scr, y_scr) = rest
    T, B = seq, batch
    if layer:
        hout_ref[0:layer] = hdone_ref[...]
    GW = GROUP_WIDTH
    j = pl.program_id(0)
    tiles = SSD_STEP_TILES

    @pl.when(j == 0)
    def _():
        rows = [conv0_ref[i] for i in range(SSD_CONV - 1)]
        rows += [xbc_ref[t * B:(t + 1) * B, :] for t in range(T)]
        for t in range(T):
            conv = cb_ref[...] + cw_ref[0:1, :] * rows[t]
            for i in range(1, SSD_CONV):
                conv = conv + cw_ref[i:i + 1, :] * rows[t + i]
            conv = _silu(conv)
            xs = conv[:, 0:GW]
            xs_scr[t] = xs
            for g in range(SSD_GROUPS):
                bm_scr[t, g] = conv[:, GW + g * SSD_STATE:GW + (g + 1) * SSD_STATE].T
                cm_scr[t, g] = conv[:, 2 * GW + g * SSD_STATE:2 * GW + (g + 1) * SSD_STATE].T
            dt = _softplus(dt_ref[t * B:(t + 1) * B, :] + dtb_ref[...])
            dte = _dot(dt, hexp_ref[...], precision=HIGHEST)
            xdt_scr[t] = (xs * dte).T
            dec_scr[t] = jnp.exp(dte * aneg_ref[...]).T
        for i in range(SSD_CONV - 1):
            conv_ref[i] = rows[T + i]

    hp0 = j * tiles
    grp = hp0 // (SSD_HEAD_DIM * (SSD_HEADS // SSD_GROUPS))
    for q in range(tiles):
        hp = pl.ds(hp0 + q, 1)
        h = h0_ref[:, q, :].T
        for t in range(T):
            h = h * dec_scr[t, hp, :] + bm_scr[t, grp] * xdt_scr[t, hp, :]
            y_scr[t, hp, :] = jnp.sum(h * cm_scr[t, grp], axis=0, keepdims=True)
        hout_ref[layer, :, q, :] = h.T

    @pl.when(j == pl.num_programs(0) - 1)
    def _():
        for t in range(T):
            y = y_scr[t].T + xs_scr[t] * dsk_ref[...]
            y = y * _silu(z_ref[t * B:(t + 1) * B, :])
            y_ref[t * B:(t + 1) * B, :] = _rms(y, ng_ref[...])


def _layer_state_specs(layer, block, axis):
    idx = lambda first: (lambda j: (first,) + tuple(j if a == axis else 0 for a in range(len(block))))
    cur = pl.BlockSpec((None,) + block, idx(layer))
    prev = [pl.BlockSpec((layer,) + block, idx(0))] if layer else []
    out = pl.BlockSpec((layer + 1,) + block, idx(0))
    return cur, prev, out


def _ssd_step(z, xbc, dtr, conv_all, h_all, h_done, lp, *, batch, seq, layer):
    n = batch * seq
    srows = SSD_HEADS * SSD_HEAD_DIM
    consts = (lp["conv_w"], lp["conv_b"], lp["dt_bias"], lp["a_neg_exp"], lp["d_skip"], lp["ssd_norm"], lp["head_expand"])
    hspec, prev_specs, hout_spec = _layer_state_specs(layer, (batch, SSD_STEP_TILES, SSD_STATE), 1)
    prev_args = [h_done] if layer else []
    cshape = (SSD_CONV - 1, batch, SSD_CONV_DIM)
    return pl.pallas_call(
        functools.partial(_ssd_step_body, seq=seq, batch=batch, layer=layer),
        grid=(srows // SSD_STEP_TILES,),
        in_specs=[_full_spec((n, GROUP_WIDTH)), _full_spec((n, SSD_CONV_DIM)), _full_spec((n, LANES)),
                  pl.BlockSpec((None,) + cshape, lambda j: (layer, 0, 0, 0)), hspec] + prev_specs
                 + [_pspec(a) for a in consts],
        out_specs=[_full_spec((n, GROUP_WIDTH)), _full_spec(cshape), hout_spec],
        out_shape=[jax.ShapeDtypeStruct((n, GROUP_WIDTH), F32),
                   jax.ShapeDtypeStruct(cshape, F32),
                   jax.ShapeDtypeStruct((layer + 1, batch, srows, SSD_STATE), F32)],
        scratch_shapes=[pltpu.VMEM((seq, batch, GROUP_WIDTH), F32),
                        pltpu.VMEM((seq, SSD_GROUPS, SSD_STATE, batch), F32),
                        pltpu.VMEM((seq, SSD_GROUPS, SSD_STATE, batch), F32),
                        pltpu.VMEM((seq, GROUP_WIDTH, batch), F32),
                        pltpu.VMEM((seq, GROUP_WIDTH, batch), F32),
                        pltpu.VMEM((seq, GROUP_WIDTH, batch), F32)],
        compiler_params=_cparams("arbitrary"),
        name="ssd_step",
    )(z, xbc, dtr, conv_all, h_all.reshape(h_all.shape[0], batch, srows, SSD_STATE),
      *prev_args, *[_parg(a) for a in consts])


PAIR = 2 * RWKV_HEAD
RWKV_PAIRS = RWKV_HEADS // 2


def _bd(x):
    half = x.shape[1] // 2
    lane = lax.broadcasted_iota(jnp.int32, x.shape, 1)
    zero = jnp.zeros_like(x)
    return jnp.concatenate([jnp.where(lane < half, x, zero), jnp.where(lane >= half, x, zero)], axis=0)


def _half_sums(x, lo):
    s_lo = jnp.sum(jnp.where(lo, x, 0.0), axis=-1, keepdims=True)
    s_hi = jnp.sum(jnp.where(lo, 0.0, x), axis=-1, keepdims=True)
    return jnp.where(lo, s_lo, s_hi)


def _head_sum(x):
    lo = lax.broadcasted_iota(jnp.int32, (x.shape[0], PAIR), 1) < RWKV_HEAD
    return jnp.concatenate([_half_sums(x[:, p * PAIR:(p + 1) * PAIR], lo) for p in range(RWKV_PAIRS)], axis=-1)


def _rwkv_pointwise(u, prev, mu_ref, w0_ref, w2_ref, a0_ref, a2_ref, g2_ref, kk_ref, ka_ref):
    GW = GROUP_WIDTH
    xs = u + (prev - u) * mu_ref[...]
    r = xs[:, 0:GW]
    k = xs[:, GW:2 * GW]
    v = xs[:, 2 * GW:3 * GW]
    wd = xs[:, 3 * GW:3 * GW + 64]
    ad = xs[:, 3 * GW + 64:3 * GW + 128]
    gd = xs[:, 3 * GW + 128:3 * GW + 256]
    w_lin = w0_ref[...] + _dot(jnp.tanh(wd).astype(BF16), w2_ref[...])
    logdecay = -math.exp(-0.5) * _sigmoid(w_lin)
    a = _sigmoid(a0_ref[...] + _dot(ad.astype(BF16), a2_ref[...]))
    g = _dot(_sigmoid(gd).astype(BF16), g2_ref[...])
    kk = k * kk_ref[...]
    kk = kk * lax.rsqrt(jnp.maximum(_head_sum(kk * kk), 1e-24))
    k = k * (1.0 + (a - 1.0) * ka_ref[...])
    return r, k, v, logdecay, a, g, kk


def _rwkv_finish(y, r, k, v, g, rk_ref, lng_ref, lnb_ref):
    mean = _head_sum(y) * (1.0 / RWKV_HEAD)
    yc = y - mean
    var = _head_sum(yc * yc) * (1.0 / RWKV_HEAD)
    y = yc * lax.rsqrt(var + RWKV_LN_EPS) * lng_ref[...] + lnb_ref[...]
    bonus = _head_sum(r * k * rk_ref[...]) * v
    return (y + bonus) * g


def _rwkv_body(u_ref, mu_ref, w0_ref, w2_ref, a0_ref, a2_ref, g2_ref, kk_ref, ka_ref, rk_ref,
               lng_ref, lnb_ref, y_ref, shift_ref, sout_ref, upad_scr, s_scr, *, chunk, group):
    L, G = chunk, group
    GL = G * L
    c = pl.program_id(1)
    pad = SUBLANES

    @pl.when(c == 0)
    def _():
        upad_scr[0:pad, :] = jnp.zeros((pad, RWKV_PROJ), F32)
        s_scr[...] = jnp.zeros(s_scr.shape, F32)

    u = u_ref[...]
    upad_scr[pad:pad + GL, :] = u
    prev = pltpu.roll(upad_scr[...], 1, axis=0)[pad:pad + GL]
    upad_scr[pad - 1:pad, :] = u[GL - 1:GL, :]
    r, k, v, logdecay, a, g, kk = _rwkv_pointwise(u, prev, mu_ref, w0_ref, w2_ref, a0_ref, a2_ref, g2_ref,
                                                  kk_ref, ka_ref)

    tril = jnp.where(lax.broadcasted_iota(jnp.int32, (L, L), 0) >= lax.broadcasted_iota(jnp.int32, (L, L), 1),
                     1.0, 0.0).astype(F32)
    cl = jnp.concatenate([_dot(tril, logdecay[i * L:(i + 1) * L, :], precision=HIGHEST) for i in range(G)], axis=0)
    e_in = jnp.exp(cl)
    e_inv = jnp.exp(-cl)
    r_t = r * e_in
    r_tb = r_t.astype(BF16)
    a_tb = (-kk * jnp.exp(cl - logdecay)).astype(BF16)
    b_tb = (kk * a * e_inv).astype(BF16)
    k_tb = (k * e_inv).astype(BF16)
    vb = v.astype(BF16)

    row = lax.broadcasted_iota(jnp.int32, (L, PAIR), 0)
    colh = lax.broadcasted_iota(jnp.int32, (L, PAIR), 1) & (RWKV_HEAD - 1)
    strict = row > colh
    incl = row >= colh
    eye_pair = jnp.where(row == colh, 1.0, 0.0).astype(F32)
    lane_lo = lax.broadcasted_iota(jnp.int32, (RWKV_HEAD, PAIR), 1) < RWKV_HEAD
    same_head = (lax.broadcasted_iota(jnp.int32, (PAIR, PAIR), 0) < RWKV_HEAD) == \
                (lax.broadcasted_iota(jnp.int32, (PAIR, PAIR), 1) < RWKV_HEAD)

    streams = [(i, p) for i in range(G) for p in range(RWKV_PAIRS)]
    ns = len(streams)
    blk = lambda x, i, p: x[i * L:(i + 1) * L, p * PAIR:(p + 1) * PAIR]
    lhs = [jnp.concatenate([blk(a_tb, i, p), blk(r_tb, i, p)], axis=0) for i, p in streams]
    m_both = [_dot_nt(lhs[s], jnp.concatenate([_bd(blk(b_tb, i, p)), _bd(blk(k_tb, i, p))], axis=0))
              for s, (i, p) in enumerate(streams)]
    m_ab = [m[:, 0:PAIR] for m in m_both]
    m_ak = [m[:, PAIR:2 * PAIR] for m in m_both]
    n_ab = [jnp.where(strict, m[0:L], 0.0) for m in m_ab]
    m_rb = [jnp.where(incl, m[L:2 * L], 0.0).astype(BF16) for m in m_ab]
    n_ak = [jnp.where(strict, m[0:L], 0.0).astype(BF16) for m in m_ak]
    m_rk = [jnp.where(incl, m[L:2 * L], 0.0).astype(BF16) for m in m_ak]
    tinv = [eye_pair + n for n in n_ab]
    pwb = [n.astype(BF16) for n in n_ab]
    pw = [_dot(x, _bd(x)) for x in pwb]
    for _ in range(int(math.log2(L)) - 2):
        pwb = [x.astype(BF16) for x in pw]
        both = [_dot(jnp.concatenate([pwb[s], tinv[s].astype(BF16)], axis=0), _bd(pwb[s])) for s in range(ns)]
        pw = [x[0:L] for x in both]
        tinv = [tinv[s] + both[s][L:2 * L] for s in range(ns)]
    pwb = [x.astype(BF16) for x in pw]
    tinv = [tinv[s] + _dot(tinv[s].astype(BF16), _bd(pwb[s])) for s in range(ns)]
    tinvb = [x.astype(BF16) for x in tinv]
    nv_mv = [_dot(jnp.concatenate([n_ak[s], m_rk[s]], axis=0), _bd(blk(vb, i, p))) for s, (i, p) in enumerate(streams)]
    wu = [_dot(tinvb[s], jnp.concatenate([_bd(blk(a_tb, i, p)), _bd(nv_mv[s][0:L].astype(BF16))], axis=1))
          for s, (i, p) in enumerate(streams)]
    wub = [x.astype(BF16) for x in wu]
    qy = [_dot(m_rb[s], jnp.concatenate([_bd(wub[s][:, 0:PAIR]), _bd(wub[s][:, PAIR:2 * PAIR])], axis=1))
          for s in range(ns)]
    q = [(blk(r_t, i, p) + qy[s][:, 0:PAIR]).astype(BF16) for s, (i, p) in enumerate(streams)]
    y_loc = [qy[s][:, PAIR:2 * PAIR] + nv_mv[s][L:2 * L] for s in range(ns)]
    zeros_b = jnp.zeros((L, PAIR), BF16)
    mg = [_dot_tn(jnp.concatenate([wub[s], jnp.concatenate([zeros_b, blk(vb, i, p)], axis=1)], axis=0),
                  jnp.concatenate([blk(b_tb, i, p), blk(k_tb, i, p)], axis=0))
          for s, (i, p) in enumerate(streams)]
    p_end = [e_in[(i + 1) * L - 1:(i + 1) * L, p * PAIR:(p + 1) * PAIR] for i, p in streams]
    m_t = [(jnp.where(same_head, mg[s][0:PAIR], 0.0) * p_end[s]).astype(BF16) for s in range(ns)]
    g_t = [jnp.where(lane_lo, mg[s][PAIR:PAIR + RWKV_HEAD], mg[s][PAIR + RWKV_HEAD:2 * PAIR]) * p_end[s]
           for s in range(ns)]

    y_rows = []
    for i in range(G):
        y_pairs = []
        for p in range(RWKV_PAIRS):
            s = i * RWKV_PAIRS + p
            s0 = s_scr[p]
            s0b = s0.astype(BF16)
            y_pairs.append(_dot_nt(q[s], _bd(s0b)) + y_loc[s])
            s_scr[p] = s0 * p_end[s] + _dot(s0b, m_t[s]) + g_t[s]
        y_rows.append(jnp.concatenate(y_pairs, axis=-1))
    y = jnp.concatenate(y_rows, axis=0)
    y_ref[...] = _rwkv_finish(y, r, k, v, g, rk_ref, lng_ref, lnb_ref)

    @pl.when(c == pl.num_programs(1) - 1)
    def _():
        sout_ref[0] = s_scr[...]
        shift_ref[0] = upad_scr[pad - 1:pad, :]


_RWKV_PARAM_NAMES = ("mu", "w0", "w2", "a0", "a2", "g2", "k_k", "k_a", "r_k", "ln_g", "ln_b")


def _rwkv(u, p, *, batch, seq):
    rows = RWKV_CHUNK * RWKV_GROUP
    nc = seq // rows
    params = [p[n] for n in _RWKV_PARAM_NAMES]
    sspec = pl.BlockSpec((1, RWKV_PAIRS, RWKV_HEAD, PAIR), lambda b, c: (b, 0, 0, 0))
    y, shift, s_last = pl.pallas_call(
        functools.partial(_rwkv_body, chunk=RWKV_CHUNK, group=RWKV_GROUP),
        grid=(batch, nc),
        in_specs=[pl.BlockSpec((rows, RWKV_PROJ), lambda b, c: (b * nc + c, 0))] + [_pspec(a) for a in params],
        out_specs=[pl.BlockSpec((rows, GROUP_WIDTH), lambda b, c: (b * nc + c, 0)),
                   pl.BlockSpec((1, 1, RWKV_PROJ), lambda b, c: (b, 0, 0)), sspec],
        out_shape=[jax.ShapeDtypeStruct((batch * seq, GROUP_WIDTH), F32),
                   jax.ShapeDtypeStruct((batch, 1, RWKV_PROJ), F32),
                   jax.ShapeDtypeStruct((batch, RWKV_PAIRS, RWKV_HEAD, PAIR), F32)],
        scratch_shapes=[pltpu.VMEM((SUBLANES + rows, RWKV_PROJ), F32),
                        pltpu.VMEM((RWKV_PAIRS, RWKV_HEAD, PAIR), F32)],
        compiler_params=_cparams("parallel", "arbitrary"),
        name="rwkv",
    )(u, *[_parg(a) for a in params])
    s_last = s_last.reshape(batch, RWKV_PAIRS, RWKV_HEAD, 2, RWKV_HEAD).transpose(0, 1, 3, 2, 4).reshape(
        batch, RWKV_HEADS, RWKV_HEAD, RWKV_HEAD)
    return y, shift.reshape(batch, RWKV_PROJ), s_last


def _rwkv_step_body(u_ref, shift0_ref, s0_ref, *rest, seq, batch, layer):
    sdone_ref, rest = (rest[0], rest[1:]) if layer else (None, rest)
    (mu_ref, w0_ref, w2_ref, a0_ref, a2_ref, g2_ref, kk_ref, ka_ref, rk_ref, lng_ref, lnb_ref, y_ref, sout_ref,
     r_scr, w_scr, k_scr, b_scr, nkk_scr, v_scr, y_scr) = rest
    T, B = seq, batch
    j = pl.program_id(0)
    if layer:
        sout_ref[0:layer] = sdone_ref[...]
    tiles = RWKV_STEP_TILES

    def pointwise(t):
        u = u_ref[t * B:(t + 1) * B, :]
        prev = shift0_ref[...] if t == 0 else u_ref[(t - 1) * B:t * B, :]
        return _rwkv_pointwise(u, prev, mu_ref, w0_ref, w2_ref, a0_ref, a2_ref, g2_ref, kk_ref, ka_ref)

    @pl.when(j == 0)
    def _():
        for t in range(T):
            r, k, v, logdecay, a, _, kk = pointwise(t)
            r_scr[t] = r.T
            w_scr[t] = jnp.exp(logdecay).T
            k_scr[t] = k.T
            b_scr[t] = (kk * a).T
            nkk_scr[t] = (-kk).T
            v_scr[t] = v.T

    i0 = j * tiles
    keys = pl.ds(pl.multiple_of((i0 // RWKV_HEAD) * RWKV_HEAD, RWKV_HEAD), RWKV_HEAD)
    for q in range(tiles):
        vi = pl.ds(i0 + q, 1)
        s = s0_ref[q]
        for t in range(T):
            sa = jnp.sum(s * nkk_scr[t, keys, :], axis=0, keepdims=True)
            s = s * w_scr[t, keys, :] + k_scr[t, keys, :] * v_scr[t, vi, :] + b_scr[t, keys, :] * sa
            y_scr[t, vi, :] = jnp.sum(s * r_scr[t, keys, :], axis=0, keepdims=True)
        sout_ref[layer, q] = s

    @pl.when(j == pl.num_programs(0) - 1)
    def _():
        for t in range(T):
            r, k, v, _, _, g, _ = pointwise(t)
            y_ref[t * B:(t + 1) * B, :] = _rwkv_finish(y_scr[t].T, r, k, v, g, rk_ref, lng_ref, lnb_ref)


def _rwkv_step(u, shift0, s_all, s_done, p, *, batch, seq, layer):
    n = batch * seq
    srows = RWKV_HEADS * RWKV_HEAD
    params = [p[nm] for nm in _RWKV_PARAM_NAMES]
    sspec, prev_specs, sout_spec = _layer_state_specs(layer, (RWKV_STEP_TILES, RWKV_HEAD, batch), 0)
    prev_args = [s_done] if layer else []
    tposed = pltpu.VMEM((seq, GROUP_WIDTH, batch), F32)
    return pl.pallas_call(
        functools.partial(_rwkv_step_body, seq=seq, batch=batch, layer=layer),
        grid=(srows // RWKV_STEP_TILES,),
        in_specs=[_full_spec((n, RWKV_PROJ)), _full_spec((batch, RWKV_PROJ)), sspec] + prev_specs
                 + [_pspec(a) for a in params],
        out_specs=[_full_spec((n, GROUP_WIDTH)), sout_spec],
        out_shape=[jax.ShapeDtypeStruct((n, GROUP_WIDTH), F32),
                   jax.ShapeDtypeStruct((layer + 1, srows, RWKV_HEAD, batch), F32)],
        scratch_shapes=[tposed] * 7,
        compiler_params=_cparams("arbitrary"),
        name="rwkv_step",
    )(u, shift0, s_all, *prev_args, *[_parg(a) for a in params])


def _s5_body(u_ref, hre0_ref, him0_ref, are_ref, aim_ref, bmat_ref, cmat_ref, d_ref, gw_ref, gb_ref,
             y_ref, hre_ref, him_ref, hs_scr, tm_scr, *, steps, batch_major):
    c = pl.program_id(1)
    ns = S5_WIDTH
    bsub = hre_ref.shape[0]

    @pl.when(c == 0)
    def _():
        hre_ref[...] = hre0_ref[...]
        him_ref[...] = him0_ref[...]

    if batch_major:
        for b in range(bsub):
            tm_scr[:, b, :] = u_ref[b]
        u = tm_scr[...].reshape(steps * bsub, GROUP_WIDTH)
    else:
        u = u_ref[...].reshape(steps * bsub, GROUP_WIDTH)
    are = jnp.broadcast_to(are_ref[...], (bsub, ns))
    aim = jnp.broadcast_to(aim_ref[...], (bsub, ns))
    hre, him = hre_ref[...], him_ref[...]
    sub = min(S5_SUB, steps)
    rows = sub * bsub
    outs = []
    hs_scr[...] = _dot(u.astype(BF16), bmat_ref[...])
    for k in range(steps // sub):
        r0 = k * rows
        u_k = u[r0:r0 + rows]
        for t in range(sub):
            rs = slice(r0 + t * bsub, r0 + (t + 1) * bsub)
            hre, him = (are * hre - aim * him + hs_scr[rs, 0:ns], are * him + aim * hre + hs_scr[rs, ns:2 * ns])
            hs_scr[rs, 0:ns] = hre
            hs_scr[rs, ns:2 * ns] = him
        y = _dot(hs_scr[r0:r0 + rows, :].astype(BF16), cmat_ref[...]) + u_k * d_ref[...]
        y = _gelu_tanh(y)
        yy = _dot(y.astype(BF16), gw_ref[...]) + gb_ref[...]
        outs.append(yy[:, 0:GROUP_WIDTH] * _sigmoid(yy[:, GROUP_WIDTH:2 * GROUP_WIDTH]))
    hre_ref[...] = hre
    him_ref[...] = him
    out = jnp.concatenate(outs, axis=0).reshape(steps, bsub, GROUP_WIDTH)
    if batch_major:
        tm_scr[...] = out
        for b in range(bsub):
            y_ref[b] = tm_scr[:, b, :]
    else:
        y_ref[...] = out


def _time_specs(u, batch_major, chunk):
    if batch_major:
        batch, seq, _ = u.shape
        steps = min(chunk, seq)
        bsub = SUBLANES
        spec = pl.BlockSpec((bsub, steps, GROUP_WIDTH), lambda b, c: (b, c, 0))
    else:
        seq, batch, _ = u.shape
        steps = min(chunk, seq)
        bsub = min(batch, SUBLANES * max(1, chunk // steps))
        spec = pl.BlockSpec((steps, bsub, GROUP_WIDTH), lambda b, c: (c, b, 0))
    return batch, seq, steps, bsub, spec


def _s5(u, hre0, him0, lp, *, batch_major):
    batch, seq, steps, bsub, tspec = _time_specs(u, batch_major, TM_CHUNK)
    hspec =pl.BlockSpec((bsub, S5_WIDTH), lambda b, c: (b, 0))
    consts = (lp["s5_are"], lp["s5_aim"], lp["s5_bmat"], lp["s5_cmat"], lp["s5_d"], lp["s5_gw"], lp["s5_gb"])
    return pl.pallas_call(
        functools.partial(_s5_body, steps=steps, batch_major=batch_major),
        grid=(batch // bsub, seq // steps),
        in_specs=[tspec, hspec, hspec] + [_pspec(a) for a in consts],
        out_specs=[tspec, hspec, hspec],
        out_shape=[jax.ShapeDtypeStruct(u.shape, F32),
                   jax.ShapeDtypeStruct((batch, S5_WIDTH), F32),
                   jax.ShapeDtypeStruct((batch, S5_WIDTH), F32)],
        scratch_shapes=[pltpu.VMEM((steps * bsub, 2 * S5_WIDTH), F32),
                        pltpu.VMEM((steps, bsub, GROUP_WIDTH), F32)],
        compiler_params=_cparams("parallel", "arbitrary"),
        name="s5",
    )(u, hre0, him0, *[_parg(a) for a in consts])


def _pool_body(u_ref, buf0_ref, pw_ref, sc_ref, y_ref, buf_ref, f_scr, tm_scr, *, steps, pos0, batch_major):
    c = pl.program_id(1)
    bsub = f_scr.shape[1]
    GW = GROUP_WIDTH
    halo = POOL_BUF + 1

    @pl.when(c == 0)
    def _():
        f_scr[0] = jnp.zeros((bsub, GW), F32)
        f_scr[1:halo] = buf0_ref[...]

    if batch_major:
        for b in range(bsub):
            f_scr[halo:halo + steps, b, :] = u_ref[b]
    else:
        f_scr[halo:halo + steps] = u_ref[...]
    f = f_scr[...]
    u = f[halo:halo + steps]
    s2 = f[1:] + f[:-1]
    s4 = s2[2:] + s2[:-2]
    s8 = s4[4:] + s4[:-4]
    s16 = s8[8:] + s8[:-8]
    f_scr[0:halo] = f[steps:steps + halo]
    lane = lax.broadcasted_iota(jnp.int32, (steps, bsub, GW), 2)
    tpos = lax.broadcasted_iota(jnp.int32, (steps, bsub, GW), 0) + (pos0 + 1) + c * steps
    win = jnp.where(lane < POOL_CH, s2[halo - 1:halo - 1 + steps],
                    jnp.where(lane < 2 * POOL_CH, s4[halo - 3:halo - 3 + steps],
                              jnp.where(lane < 3 * POOL_CH, s8[halo - 7:halo - 7 + steps],
                                        s16[halo - 15:halo - 15 + steps])))
    wlen = jnp.where(lane < POOL_CH, POOL_WINDOWS[0],
                     jnp.where(lane < 2 * POOL_CH, POOL_WINDOWS[1],
                               jnp.where(lane < 3 * POOL_CH, POOL_WINDOWS[2], POOL_WINDOWS[3])))
    cnt = jnp.minimum(tpos, wlen).astype(F32)
    pooled = (win / cnt - u).reshape(steps * bsub, GW)
    y = (_dot(pooled.astype(BF16), pw_ref[...]) * sc_ref[...]).reshape(steps, bsub, GW)
    if batch_major:
        tm_scr[...] = y
        for b in range(bsub):
            y_ref[b] = tm_scr[:, b, :]
    else:
        y_ref[...] = y

    @pl.when(c == pl.num_programs(1) - 1)
    def _():
        buf_ref[...] = f_scr[1:halo]


def _pool(u, buf0, lp, *, pos0, batch_major, layer=None):
    batch, seq, steps, bsub, tspec = _time_specs(u, batch_major, POOL_CHUNK)
    bblock =(POOL_BUF, bsub, GROUP_WIDTH)
    bspec = pl.BlockSpec(bblock, lambda b, c: (0, b, 0))
    if layer is None:
        bspec_in = bspec
    else:
        bspec_in = pl.BlockSpec((None,) + bblock, lambda b, c: (layer, 0, b, 0))
    return pl.pallas_call(
        functools.partial(_pool_body, steps=steps, pos0=pos0, batch_major=batch_major),
        grid=(batch // bsub, seq // steps),
        in_specs=[tspec, bspec_in, _pspec(lp["pool_w"]), _pspec(lp["pool_scale"])],
        out_specs=[tspec, bspec],
        out_shape=[jax.ShapeDtypeStruct(u.shape, F32), jax.ShapeDtypeStruct((POOL_BUF, batch, GROUP_WIDTH), F32)],
        scratch_shapes=[pltpu.VMEM((POOL_BUF + 1 + steps, bsub, GROUP_WIDTH), F32),
                        pltpu.VMEM((steps, bsub, GROUP_WIDTH), F32)],
        compiler_params=_cparams("parallel", "arbitrary"),
        name="pool",
    )(u, buf0, _parg(lp["pool_w"]), _parg(lp["pool_scale"]))


def _block_diag(blocks):
    n, g, r, c = blocks.shape
    eye = jnp.eye(g, dtype=blocks.dtype)
    return (eye[None, :, None, :, None] * blocks[:, :, :, None, :]).reshape(n, g * r, g * c)


def _stacked_params(P):
    row = lambda a: a.reshape(a.shape[0], 1, -1)
    pad_lanes = lambda a: jnp.pad(a, ((0, 0), (0, LANES - a.shape[1])))
    bf = lambda a: a.astype(BF16)

    lam = lax.complex(P["s5_lam_re"], P["s5_lam_im"])
    a_bar = jnp.exp(lam * jnp.exp(P["s5_log_step"])[..., None])
    b_bar = ((a_bar - 1.0) / lam)[..., None] * lax.complex(P["s5_b_re"], P["s5_b_im"])
    b_t = jnp.swapaxes(b_bar, 2, 3)
    bmat = jnp.concatenate([_block_diag(jnp.real(b_t)), _block_diag(jnp.imag(b_t))], axis=2)
    c_t = jnp.swapaxes(lax.complex(P["s5_c_re"], P["s5_c_im"]), 2, 3)
    cmat = jnp.concatenate([_block_diag(jnp.real(c_t)), -_block_diag(jnp.imag(c_t))], axis=1)

    out = dict(
        norm_ffn1=row(P["norm_ffn1"]), ffn1_in=P["ffn1_in"], ffn1_out=P["ffn1_out"],
        norm_mix=row(P["norm_mix"]),
        w_in=jnp.transpose(P["w_in"], (2, 0, 1)),
        conv_w=P["ssd_conv_w"], conv_b=row(P["ssd_conv_b"]),
        dt_bias=row(pad_lanes(P["ssd_dt_bias"])), a_log=row(pad_lanes(P["ssd_a_log"])),
        a_neg_exp=row(jnp.repeat(-jnp.exp(P["ssd_a_log"]), SSD_HEAD_DIM, axis=1)),
        d_skip=row(jnp.repeat(P["ssd_d"], SSD_HEAD_DIM, axis=1)), ssd_norm=row(P["ssd_norm"]),
        s5_are=row(jnp.real(a_bar)), s5_aim=row(jnp.imag(a_bar)), s5_bmat=bf(bmat), s5_cmat=bf(cmat),
        s5_d=row(P["s5_d"]), s5_gw=bf(P["s5_glu_w"]), s5_gb=row(P["s5_glu_b"]),
        pool_w=bf(_block_diag(P["pool_w"])), pool_scale=row(P["pool_scale"]),
        w_out=bf(P["w_out"]),
        norm_ffn2=row(P["norm_ffn2"]), ffn2_in=P["ffn2_in"], ffn2_out=P["ffn2_out"],
    )
    for name in _RWKV_PARAM_NAMES:
        a = P["rwkv_" + name]
        out["rwkv_" + name] = bf(a) if name in ("w2", "a2", "g2") else row(a)
    return out


def _layer_params(stacked, l):
    lp = {k: _Layered((v, l)) for k, v in stacked.items()}
    lp["rwkv"] = {n: lp["rwkv_" + n] for n in _RWKV_PARAM_NAMES}
    lp["head_expand"] = jnp.pad(jnp.repeat(jnp.eye(SSD_HEADS, dtype=F32), SSD_HEAD_DIM, axis=1),
                                ((0, LANES - SSD_HEADS), (0, 0)))
    return lp


def _mixers_prompt(lp, proj, *, batch, seq):
    z, xbc, ur, us5, upool, dtr = proj
    y_ssd, conv_new, ssd_new = _ssd(z, xbc, dtr, lp, batch=batch, seq=seq)
    y_rwkv, shift_new, rwkv_new = _rwkv(ur, lp["rwkv"], batch=batch, seq=seq)
    zeros = jnp.zeros((batch, S5_WIDTH), F32)
    bm = lambda a: a.reshape(batch, seq, a.shape[-1])
    rows = lambda a: a.reshape(batch * seq, a.shape[-1])
    y_s5, s5re, s5im = _s5(bm(us5), zeros, zeros, lp, batch_major=True)
    y_pool, pool_new = _pool(bm(upool), jnp.zeros((POOL_BUF, batch, GROUP_WIDTH), F32), lp, pos0=0,
                             batch_major=True)
    ys = (y_ssd, y_rwkv, rows(y_s5), rows(y_pool))
    states = (conv_new, ssd_new, shift_new, rwkv_new, s5re.reshape(batch, S5_GROUPS, S5_STATE),
              s5im.reshape(batch, S5_GROUPS, S5_STATE), jnp.swapaxes(pool_new, 0, 1))
    return ys, states


def _mixers_decode(lp, proj, states, done, *, batch, seq, layer):
    z, xbc, ur, us5, upool, dtr = proj
    shift0, s5re0, s5im0 = (states[i][layer] for i in (2, 4, 5))
    ssd_done, rwkv_done = (done[1], done[3]) if layer else (None, None)
    y_ssd, conv_new, ssd_new = _ssd_step(z, xbc, dtr, states[0], states[1], ssd_done, lp, batch=batch, seq=seq,
                                         layer=layer)
    y_rwkv, rwkv_new = _rwkv_step(ur, shift0, states[3], rwkv_done, lp["rwkv"], batch=batch, seq=seq, layer=layer)
    shift_new = ur[(seq - 1) * batch:, :]
    tm = lambda a: a.reshape(seq, batch, a.shape[-1])
    y_s5, s5re, s5im = _s5(tm(us5), s5re0.reshape(batch, S5_WIDTH), s5im0.reshape(batch, S5_WIDTH), lp,
                           batch_major=False)
    y_pool, pool_new = _pool(tm(upool), states[6], lp, pos0=PAST_LEN, batch_major=False, layer=layer)
    rows = lambda a: a.reshape(seq * batch, a.shape[-1])
    ys = (y_ssd, y_rwkv, rows(y_s5), rows(y_pool))
    new_states = (jnp.swapaxes(conv_new, 0, 1), ssd_new, shift_new, rwkv_new,
                  s5re.reshape(batch, S5_GROUPS, S5_STATE), s5im.reshape(batch, S5_GROUPS, S5_STATE),
                  jnp.swapaxes(pool_new, 0, 1))
    return ys, new_states


_WIDTHS = (GROUP_WIDTH, SSD_CONV_DIM, RWKV_PROJ, GROUP_WIDTH, GROUP_WIDTH, LANES)


def _trunk(x_p, x_s, layer_params, norm_final, mixers_p, mixers_s):
    st_p, st_s = [], []
    mix_p, mix_s, lp = None, None, None
    for l, lp_next in enumerate(layer_params):
        if l > 0:
            x_s, wg, wu, wo = _ffn_cast(x_s, lp["norm_ffn2"], lp["ffn2_in"], lp["ffn2_out"], mix=mix_s, wmix=lp["w_out"])
            x_p = _ffn(x_p, lp["norm_ffn2"], wg, wu, wo, mix=mix_p, wmix=lp["w_out"])
        lp = lp_next
        x_s, wg, wu, wo = _ffn_cast(x_s, lp["norm_ffn1"], lp["ffn1_in"], lp["ffn1_out"])
        x_p = _ffn(x_p, lp["norm_ffn1"], wg, wu, wo)
        proj_s, w_all = _inproj_cast(x_s, lp["norm_mix"], lp["w_in"], _WIDTHS)
        mix_p, st = mixers_p(l, lp, _inproj(x_p, lp["norm_mix"], w_all, _WIDTHS), st_p[-1] if st_p else None)
        st_p.append(st)
        mix_s, st = mixers_s(l, lp, proj_s, st_s[-1] if st_s else None)
        st_s.append(st)
    x_s, wg, wu, wo = _ffn_cast(x_s, lp["norm_ffn2"], lp["ffn2_in"], lp["ffn2_out"], mix=mix_s, wmix=lp["w_out"],
                                gf=norm_final)
    x_p = _ffn(x_p, lp["norm_ffn2"], wg, wu, wo, mix=mix_p, wmix=lp["w_out"], gf=norm_final)
    return (x_p, x_s), (st_p, st_s)


def kernel(x_prompt, x_sample, state_ssd_conv, state_ssd, state_rwkv_shift, state_rwkv, state_s5_re, state_s5_im, state_pool, norm_ffn1, ffn1_in, ffn1_out, norm_mix, w_in, ssd_conv_w, ssd_conv_b, ssd_dt_bias, ssd_a_log, ssd_d, ssd_norm, rwkv_mu, rwkv_w0, rwkv_w2, rwkv_a0, rwkv_a2, rwkv_g2, rwkv_k_k, rwkv_k_a, rwkv_r_k, rwkv_ln_g, rwkv_ln_b, s5_lam_re, s5_lam_im, s5_log_step, s5_b_re, s5_b_im, s5_c_re, s5_c_im, s5_d, s5_glu_w, s5_glu_b, pool_w, pool_scale, w_out, norm_ffn2, ffn2_in, ffn2_out, norm_final):
    P = dict(norm_ffn1=norm_ffn1, ffn1_in=ffn1_in, ffn1_out=ffn1_out, norm_mix=norm_mix, w_in=w_in,
             ssd_conv_w=ssd_conv_w, ssd_conv_b=ssd_conv_b, ssd_dt_bias=ssd_dt_bias, ssd_a_log=ssd_a_log,
             ssd_d=ssd_d, ssd_norm=ssd_norm, rwkv_mu=rwkv_mu, rwkv_w0=rwkv_w0, rwkv_w2=rwkv_w2, rwkv_a0=rwkv_a0,
             rwkv_a2=rwkv_a2, rwkv_g2=rwkv_g2, rwkv_k_k=rwkv_k_k, rwkv_k_a=rwkv_k_a,
             rwkv_r_k=rwkv_r_k.reshape(rwkv_r_k.shape[0], -1), rwkv_ln_g=rwkv_ln_g, rwkv_ln_b=rwkv_ln_b,
             s5_lam_re=s5_lam_re, s5_lam_im=s5_lam_im, s5_log_step=s5_log_step, s5_b_re=s5_b_re, s5_b_im=s5_b_im,
             s5_c_re=s5_c_re, s5_c_im=s5_c_im, s5_d=s5_d, s5_glu_w=s5_glu_w, s5_glu_b=s5_glu_b, pool_w=pool_w,
             pool_scale=pool_scale, w_out=w_out, norm_ffn2=norm_ffn2, ffn2_in=ffn2_in, ffn2_out=ffn2_out)
    depth = norm_ffn1.shape[0]
    bp, tp, d = x_prompt.shape
    bs, ts, _ = x_sample.shape
    stacked = _stacked_params(P)
    layer_params = [_layer_params(stacked, l) for l in range(depth)]
    gf = norm_final.reshape(1, -1)
    sample_states = (state_ssd_conv, state_ssd, state_rwkv_shift, state_rwkv, state_s5_re, state_s5_im, state_pool)
    rwkv_rows = RWKV_HEADS * RWKV_HEAD
    decode_states = (jnp.swapaxes(state_ssd_conv, 1, 2), state_ssd, state_rwkv_shift,
                     jnp.transpose(state_rwkv, (0, 2, 3, 4, 1)).reshape(depth, rwkv_rows, RWKV_HEAD, bs),
                     state_s5_re, state_s5_im, jnp.swapaxes(state_pool, 1, 2))

    x_s = jnp.swapaxes(x_sample, 0, 1).reshape(ts * bs, d)
    (y_p, y_s), (st_p, st_s) = _trunk(
        x_prompt.reshape(bp * tp, d), x_s, layer_params, gf,
        lambda l, lp, proj, done: _mixers_prompt(lp, proj, batch=bp, seq=tp),
        lambda l, lp, proj, done: _mixers_decode(lp, proj, decode_states, done, batch=bs, seq=ts, layer=l))
    outs = [y_p.reshape(bp, tp, d), jnp.swapaxes(y_s.reshape(ts, bs, d), 0, 1)]
    for i, ref_state in enumerate(sample_states):
        outs.append(jnp.stack([st[i] for st in st_p]))
        if i == 1:
            outs.append(st_s[-1][i].reshape(ref_state.shape))
        elif i == 3:
            s_new = st_s[-1][i].reshape(depth, RWKV_HEADS, RWKV_HEAD, RWKV_HEAD, bs)
            outs.append(jnp.transpose(s_new, (0, 4, 1, 2, 3)))
        else:
            outs.append(jnp.stack([st[i] for st in st_s]))
    return tuple(outs)
```

```python
import functools
import math

import jax
import jax.numpy as jnp
from jax import lax
from jax.experimental import pallas as pl
from jax.experimental.pallas import tpu as pltpu

F32 = jnp.float32
BF16 = jnp.bfloat16
HIGHEST = lax.Precision.HIGHEST

SUBLANES = 8
LANES = 128
VMEM_LIMIT_BYTES = 56 * 1024 * 1024

GROUP_WIDTH = 256
SSD_HEAD_DIM = 64
SSD_HEADS = 4
SSD_GROUPS = 2
SSD_STATE = 128
SSD_CONV = 4
SSD_CONV_DIM = GROUP_WIDTH + 2 * SSD_GROUPS * SSD_STATE
SSD_CHUNK = 128
SSD_GROUP = 8
LOG2_E = math.log2(math.e)
RWKV_HEAD = 64
RWKV_HEADS = 4
RWKV_PROJ = 1024
RWKV_LN_EPS = 64e-5
RWKV_CHUNK = 64
RWKV_GROUP = 16
S5_GROUPS = 16
S5_STATE = 64
S5_WIDTH = S5_GROUPS * S5_STATE
POOL_WINDOWS = (2, 4, 8, 16)
POOL_CH = 64
POOL_BUF = 15
RMS_EPS = 1e-6
PAST_LEN = 16384

ROW_TILE = 1024
INPROJ_ROW_TILE = 1024
FFN_CHUNK = 256
TM_CHUNK = 128
POOL_CHUNK = 256
S5_SUB = 128
SSD_STEP_TILES = 16
RWKV_STEP_TILES = 16


def _cparams(*sem):
    return pltpu.CompilerParams(dimension_semantics=sem, vmem_limit_bytes=VMEM_LIMIT_BYTES)


def _dot(a, b, **kw):
    return jnp.dot(a, b, preferred_element_type=F32, **kw)


def _dot_nt(a, b):
    return lax.dot_general(a, b, (((1,), (1,)), ((), ())), preferred_element_type=F32)


def _dot_tn(a, b):
    return lax.dot_general(a, b, (((0,), (0,)), ((), ())), preferred_element_type=F32)


def _sigmoid(x):
    return 0.5 * jnp.tanh(0.5 * x) + 0.5


def _silu(x):
    h = 0.5 * x
    return h + h * jnp.tanh(h)


def _softplus(x):
    return jnp.maximum(x, 0.0) + jnp.log(1.0 + jnp.exp(-jnp.abs(x)))


def _gelu_tanh(x):
    c = math.sqrt(2.0 / math.pi)
    return x * (0.5 * (1.0 + jnp.tanh(c * (x + 0.044715 * (x * x * x)))))


def _rms(x, g):
    return x * lax.rsqrt(jnp.mean(x * x, axis=-1, keepdims=True) + RMS_EPS) * g


def _full_spec(shape):
    n = len(shape)
    return pl.BlockSpec(shape, lambda *_: (0,) * n)


class _Layered(tuple):
    pass


def _pspec(p, single=False):
    mode = pl.Buffered(1) if single else None
    if isinstance(p, _Layered):
        a, l = p
        return pl.BlockSpec((None,) + a.shape[1:], lambda *_: (l,) + (0,) * (a.ndim - 1), pipeline_mode=mode)
    n = p.ndim
    return pl.BlockSpec(p.shape, lambda *_: (0,) * n, pipeline_mode=mode)


def _parg(p):
    return p[0] if isinstance(p, _Layered) else p


def _mix_residual(x, y_refs, wmix_ref):
    for j, y_ref in enumerate(y_refs):
        x = x + _dot(y_ref[...].astype(BF16), wmix_ref[j * GROUP_WIDTH:(j + 1) * GROUP_WIDTH, :])
    return x


def _swiglu_chunk(h, wg, wu, wo):
    act = (_silu(_dot(h, wg)) * _dot(h, wu)).astype(BF16)
    return _dot(act, wo)


def _ffn_body(*refs, has_mix, final_norm):
    it = iter(refs)
    x = next(it)[...]
    if has_mix:
        y_refs = [next(it) for _ in range(4)]
        x = _mix_residual(x, y_refs, next(it))
    g_ref, wg_ref, wu_ref, wo_ref = next(it), next(it), next(it), next(it)
    gf_ref = next(it) if final_norm else None
    o_ref = next(it)
    h = _rms(x, g_ref[...]).astype(BF16)
    acc = jnp.zeros_like(x)
    for c in range(wo_ref.shape[0] // FFN_CHUNK):
        cols = slice(c * FFN_CHUNK, (c + 1) * FFN_CHUNK)
        acc = acc + _swiglu_chunk(h, wg_ref[:, cols], wu_ref[:, cols], wo_ref[cols, :])
    x = x + 0.5 * acc
    if final_norm:
        x = _rms(x, gf_ref[...])
    o_ref[...] = x


def _ffn(x, g, wg, wu, wo, mix=None, wmix=None, gf=None):
    rows, d = x.shape
    row_spec = lambda w: pl.BlockSpec((ROW_TILE, w), lambda i: (i, 0))
    args, specs = [x], [row_spec(d)]
    if mix is not None:
        for y in mix:
            args.append(y)
            specs.append(row_spec(y.shape[1]))
        args.append(_parg(wmix))
        specs.append(_pspec(wmix, single=True))
    for a in (g, wg, wu, wo) + ((gf,) if gf is not None else ()):
        args.append(_parg(a))
        specs.append(_pspec(a, single=True))
    return pl.pallas_call(
        functools.partial(_ffn_body, has_mix=mix is not None, final_norm=gf is not None),
        grid=(rows // ROW_TILE,),
        in_specs=specs,
        out_specs=row_spec(d),
        out_shape=jax.ShapeDtypeStruct((rows, d), F32),
        compiler_params=_cparams("parallel"),
        name="ffn",
    )(*args)


def _ffn_cast_body(*refs, has_mix, final_norm):
    it = iter(refs)
    x_ref = next(it)
    if has_mix:
        y_refs = [next(it) for _ in range(4)]
        wmix_ref = next(it)
    g_ref, wg_ref, wu_ref, wo_ref = next(it), next(it), next(it), next(it)
    gf_ref = next(it) if final_norm else None
    o_ref, wg_out, wu_out, wo_out, x_scr, h_scr, acc_scr = (next(it) for _ in range(7))
    c = pl.program_id(0)

    @pl.when(c == 0)
    def _():
        x = x_ref[...]
        if has_mix:
            x = _mix_residual(x, y_refs, wmix_ref)
        x_scr[...] = x
        h_scr[...] = _rms(x, g_ref[...]).astype(BF16)
        acc_scr[...] = jnp.zeros(acc_scr.shape, F32)

    wg = wg_ref[...].astype(BF16)
    wu = wu_ref[...].astype(BF16)
    wo = wo_ref[...].astype(BF16)
    wg_out[...] = wg
    wu_out[...] = wu
    wo_out[...] = wo
    acc_scr[...] += _swiglu_chunk(h_scr[...], wg, wu, wo)

    @pl.when(c == pl.num_programs(0) - 1)
    def _():
        x = x_scr[...] + 0.5 * acc_scr[...]
        if final_norm:
            x = _rms(x, gf_ref[...])
        o_ref[...] = x


def _ffn_cast(x, g, wi, wo, mix=None, wmix=None, gf=None):
    rows, d = x.shape
    wi_all, l = wi
    wo_all, _ = wo
    d_ff = wo_all.shape[1]
    nchunks = d_ff // FFN_CHUNK
    args, specs = [x], [_full_spec(x.shape)]
    if mix is not None:
        for y in mix:
            args.append(y)
            specs.append(_full_spec(y.shape))
        args.append(_parg(wmix))
        specs.append(_pspec(wmix, single=True))
    args += [_parg(g), wi_all, wi_all, wo_all]
    specs += [_pspec(g),
              pl.BlockSpec((None, d, FFN_CHUNK), lambda c: (l, 0, c)),
              pl.BlockSpec((None, d, FFN_CHUNK), lambda c: (l, 0, c + nchunks)),
              pl.BlockSpec((None, FFN_CHUNK, d), lambda c: (l, c, 0))]
    if gf is not None:
        args.append(gf)
        specs.append(_full_spec(gf.shape))
    col_spec = pl.BlockSpec((d, FFN_CHUNK), lambda c: (0, c))
    return pl.pallas_call(
        functools.partial(_ffn_cast_body, has_mix=mix is not None, final_norm=gf is not None),
        grid=(nchunks,),
        in_specs=specs,
        out_specs=[_full_spec(x.shape), col_spec, col_spec, pl.BlockSpec((FFN_CHUNK, d), lambda c: (c, 0))],
        out_shape=[jax.ShapeDtypeStruct((rows, d), F32), jax.ShapeDtypeStruct((d, d_ff), BF16),
                   jax.ShapeDtypeStruct((d, d_ff), BF16), jax.ShapeDtypeStruct((d_ff, d), BF16)],
        scratch_shapes=[pltpu.VMEM((rows, d), F32), pltpu.VMEM((rows, d), BF16), pltpu.VMEM((rows, d), F32)],
        compiler_params=_cparams("arbitrary"),
        name="ffn_cast",
    )(*args)


def _inproj_body(x_ref, g_ref, wt_ref, *o_refs):
    h = _rms(x_ref[...], g_ref[...]).astype(BF16)
    off = 0
    for o_ref in o_refs:
        n = o_ref.shape[-1]
        o_ref[...] = _dot_nt(h, wt_ref[off:off + n, :])
        off += n


def _inproj_cast_body(x_ref, g_ref, win_ref, *o_refs, layer):
    *proj_refs, wall_ref = o_refs
    split = GROUP_WIDTH + SSD_CONV_DIM
    wt = win_ref[:, layer, :]
    tail = wt.shape[0] - split - SSD_HEADS
    wall_ref[0:split, :] = wt[0:split].astype(BF16)
    wall_ref[split:split + tail, :] = wt[split + SSD_HEADS:].astype(BF16)
    dt_rows = jnp.concatenate([wt[split:split + SSD_HEADS], jnp.zeros((LANES - SSD_HEADS, wt.shape[1]), F32)], axis=0)
    wall_ref[split + tail:, :] = dt_rows.astype(BF16)
    _inproj_body(x_ref, g_ref, wall_ref, *proj_refs)


def _inproj_cast(x, g, w_in, widths):
    rows, d = x.shape
    wt_all, l = w_in
    outs = pl.pallas_call(
        functools.partial(_inproj_cast_body, layer=l),
        grid=(1,),
        in_specs=[_full_spec(x.shape), _pspec(g),
                  pl.BlockSpec(wt_all.shape, lambda i: (0, 0, 0), pipeline_mode=pl.Buffered(1))],
        out_specs=[_full_spec((rows, n)) for n in widths] + [_full_spec((sum(widths), d))],
        out_shape=[jax.ShapeDtypeStruct((rows, n), F32) for n in widths]
                  + [jax.ShapeDtypeStruct((sum(widths), d), BF16)],
        compiler_params=_cparams("arbitrary"),
        name="inproj_cast",
    )(x, _parg(g), wt_all)
    return outs[:-1], outs[-1]


def _inproj(x, g, w, widths):
    rows, d = x.shape
    tile = INPROJ_ROW_TILE
    row_spec = lambda w_: pl.BlockSpec((tile, w_), lambda i: (i, 0))
    return pl.pallas_call(
        _inproj_body,
        grid=(rows // tile,),
        in_specs=[row_spec(d), _pspec(g), _pspec(w, single=True)],
        out_specs=[row_spec(n) for n in widths],
        out_shape=[jax.ShapeDtypeStruct((rows, n), F32) for n in widths],
        compiler_params=_cparams("parallel"),
        name="inproj",
    )(x, _parg(g), _parg(w))


def _ssd_body(z_ref, xbc_ref, dt_ref, cw_ref, cb_ref, dtb_ref, alog_ref, dsk_ref, ng_ref,
              y_ref, conv_ref, hout_ref, xpad_scr, h_scr, *, chunk, group):
    L, G = chunk, group
    GL = G * L
    c = pl.program_id(1)
    pad = SUBLANES
    halo = SSD_CONV - 1
    hpg = SSD_HEADS // SSD_GROUPS
    assert hpg == 2 and hpg * SSD_HEAD_DIM == SSD_STATE

    @pl.when(c == 0)
    def _():
        xpad_scr[0:pad, :] = jnp.zeros((pad, SSD_CONV_DIM), F32)
        h_scr[...] = jnp.zeros(h_scr.shape, F32)

    xpad_scr[pad:pad + GL, :] = xbc_ref[...]
    xfull = xpad_scr[...]
    conv = cb_ref[...] + cw_ref[halo:halo + 1, :] * xfull[pad:pad + GL]
    for j in range(halo):
        conv = conv + cw_ref[j:j + 1, :] * pltpu.roll(xfull, halo - j, axis=0)[pad:pad + GL]
    xpad_scr[pad - halo:pad, :] = xpad_scr[pad + GL - halo:pad + GL, :]
    conv = _silu(conv)
    xs = conv[:, 0:GROUP_WIDTH]
    bm = conv[:, GROUP_WIDTH:2 * GROUP_WIDTH].astype(BF16)
    cm = conv[:, 2 * GROUP_WIDTH:3 * GROUP_WIDTH].astype(BF16)

    row = lax.broadcasted_iota(jnp.int32, (L, L), 0)
    col = lax.broadcasted_iota(jnp.int32, (L, L), 1)
    causal = row >= col
    tril = jnp.where(causal, 1.0, 0.0).astype(F32)
    dt = _softplus(dt_ref[...] + dtb_ref[...])
    da = dt * (-jnp.exp(alog_ref[...]) * LOG2_E)
    acs = [_dot(tril, da[i * L:(i + 1) * L, :], precision=HIGHEST) for i in range(G)]
    acs_t = [a.T for a in acs]
    e_acs = [jnp.exp2(a) for a in acs]
    e_end = [jnp.exp2(a[L - 1:L, :] - a) for a in acs]
    e_last = [jnp.exp2(a[L - 1:L, :]) for a in acs]

    keys = [(i, g) for i in range(G) for g in range(SSD_GROUPS)]
    rows_of = lambda x, i: x[i * L:(i + 1) * L]
    lanes_of = lambda x, g: x[:, g * SSD_STATE:(g + 1) * SSD_STATE]
    lane_lo = lax.broadcasted_iota(jnp.int32, (L, hpg * SSD_HEAD_DIM), 1) < SSD_HEAD_DIM
    row_lo = lax.broadcasted_iota(jnp.int32, (hpg * SSD_HEAD_DIM, SSD_STATE), 0) < SSD_HEAD_DIM
    head_cols = lambda a, g: jnp.where(lane_lo, a[:, g * hpg:g * hpg + 1], a[:, g * hpg + 1:g * hpg + 2])
    bg = {(i, g): lanes_of(rows_of(bm, i), g) for i, g in keys}
    cg = {(i, g): lanes_of(rows_of(cm, i), g) for i, g in keys}
    scores = {k: _dot_nt(cg[k], bg[k]) for k in keys}
    xdt = {(i, g): lanes_of(rows_of(xs, i), g) * head_cols(rows_of(dt, i), g) for i, g in keys}
    decay = {(i, h): jnp.exp2(jnp.where(causal, acs[i][:, h:h + 1] - acs_t[i][h:h + 1, :], -jnp.inf))
             for i in range(G) for h in range(SSD_HEADS)}
    p_mat = {(i, g): jnp.concatenate([(scores[(i, g)] * decay[(i, g * hpg + k)]).astype(BF16) for k in range(hpg)],
                                     axis=1) for i, g in keys}
    y_in = {k: _dot(p_mat[k], _bd(xdt[k].astype(BF16))) for k in keys}
    st = {(i, g): _dot_tn((xdt[(i, g)] * head_cols(e_end[i], g)).astype(BF16), bg[(i, g)]) for i, g in keys}

    y_rows = []
    for i in range(G):
        ys = []
        for g in range(SSD_GROUPS):
            h_prev = h_scr[g * hpg:(g + 1) * hpg].reshape(hpg * SSD_HEAD_DIM, SSD_STATE)
            ys.append(y_in[(i, g)] + _dot_nt(cg[(i, g)], h_prev.astype(BF16)) * head_cols(e_acs[i], g))
            keep = jnp.where(row_lo, e_last[i][:, g * hpg:g * hpg + 1], e_last[i][:, g * hpg + 1:g * hpg + 2])
            h_scr[g * hpg:(g + 1) * hpg] = (h_prev * keep + st[(i, g)]).reshape(hpg, SSD_HEAD_DIM, SSD_STATE)
        y_rows.append(jnp.concatenate(ys, axis=-1))
    y = jnp.concatenate(y_rows, axis=0) + xs * dsk_ref[...]
    y = y * _silu(z_ref[...])
    y_ref[...] = _rms(y, ng_ref[...])

    @pl.when(c == pl.num_programs(1) - 1)
    def _():
        hout_ref[0] = h_scr[...]
        conv_ref[0] = xpad_scr[pad - halo:pad, :]


def _ssd(z, xbc, dtr, lp, *, batch, seq):
    chunk = SSD_CHUNK
    rows = chunk * SSD_GROUP
    nc = seq // rows
    rspec = lambda w: pl.BlockSpec((rows, w), lambda b, c: (b * nc + c, 0))
    consts = (lp["conv_w"], lp["conv_b"], lp["dt_bias"], lp["a_log"], lp["d_skip"], lp["ssd_norm"])
    return pl.pallas_call(
        functools.partial(_ssd_body, chunk=chunk, group=SSD_GROUP),
        grid=(batch, nc),
        in_specs=[rspec(GROUP_WIDTH), rspec(SSD_CONV_DIM), rspec(LANES)] + [_pspec(a) for a in consts],
        out_specs=[rspec(GROUP_WIDTH),
                   pl.BlockSpec((1, SSD_CONV - 1, SSD_CONV_DIM), lambda b, c: (b, 0, 0)),
                   pl.BlockSpec((1, SSD_HEADS, SSD_HEAD_DIM, SSD_STATE), lambda b, c: (b, 0, 0, 0))],
        out_shape=[jax.ShapeDtypeStruct((batch * seq, GROUP_WIDTH), F32),
                   jax.ShapeDtypeStruct((batch, SSD_CONV - 1, SSD_CONV_DIM), F32),
                   jax.ShapeDtypeStruct((batch, SSD_HEADS, SSD_HEAD_DIM, SSD_STATE), F32)],
        scratch_shapes=[pltpu.VMEM((SUBLANES + rows, SSD_CONV_DIM), F32),
                        pltpu.VMEM((SSD_HEADS, SSD_HEAD_DIM, SSD_STATE), F32)],
        compiler_params=_cparams("parallel", "arbitrary"),
        name="ssd",
    )(z, xbc, dtr, *[_parg(a) for a in consts])


def _ssd_step_body(z_ref, xbc_ref, dt_ref, conv0_ref, h0_ref, *rest, seq, batch, layer):
    hdone_ref, rest = (rest[0], rest[1:]) if layer else (None, rest)
    (cw_ref, cb_ref, dtb_ref, aneg_ref, dsk_ref, ng_ref, hexp_ref, y_ref, conv_ref, hout_ref,
     xs_scr, bm_scr, cm_scr, xdt_scr, dec_scr, y_scr) = rest
    T, B = seq, batch
    if layer:
        hout_ref[0:layer] = hdone_ref[...]
    GW = GROUP_WIDTH
    j = pl.program_id(0)
    tiles = SSD_STEP_TILES

    @pl.when(j == 0)
    def _():
        rows = [conv0_ref[i] for i in range(SSD_CONV - 1)]
        rows += [xbc_ref[t * B:(t + 1) * B, :] for t in range(T)]
        for t in range(T):
            conv = cb_ref[...] + cw_ref[0:1, :] * rows[t]
            for i in range(1, SSD_CONV):
                conv = conv + cw_ref[i:i + 1, :] * rows[t + i]
            conv = _silu(conv)
            xs = conv[:, 0:GW]
            xs_scr[t] = xs
            for g in range(SSD_GROUPS):
                bm_scr[t, g] = conv[:, GW + g * SSD_STATE:GW + (g + 1) * SSD_STATE].T
                cm_scr[t, g] = conv[:, 2 * GW + g * SSD_STATE:2 * GW + (g + 1) * SSD_STATE].T
            dt = _softplus(dt_ref[t * B:(t + 1) * B, :] + dtb_ref[...])
            dte = _dot(dt, hexp_ref[...], precision=HIGHEST)
            xdt_scr[t] = (xs * dte).T
            dec_scr[t] = jnp.exp(dte * aneg_ref[...]).T
        for i in range(SSD_CONV - 1):
            conv_ref[i] = rows[T + i]

    hp0 = j * tiles
    grp = hp0 // (SSD_HEAD_DIM * (SSD_HEADS // SSD_GROUPS))
    for q in range(tiles):
        hp = pl.ds(hp0 + q, 1)
        h = h0_ref[:, q, :].T
        for t in range(T):
            h = h * dec_scr[t, hp, :] + bm_scr[t, grp] * xdt_scr[t, hp, :]
            y_scr[t, hp, :] = jnp.sum(h * cm_scr[t, grp], axis=0, keepdims=True)
        hout_ref[layer, :, q, :] = h.T

    @pl.when(j == pl.num_programs(0) - 1)
    def _():
        for t in range(T):
            y = y_scr[t].T + xs_scr[t] * dsk_ref[...]
            y = y * _silu(z_ref[t * B:(t + 1) * B, :])
            y_ref[t * B:(t + 1) * B, :] = _rms(y, ng_ref[...])


def _layer_state_specs(layer, block, axis):
    idx = lambda first: (lambda j: (first,) + tuple(j if a == axis else 0 for a in range(len(block))))
    cur = pl.BlockSpec((None,) + block, idx(layer))
    prev = [pl.BlockSpec((layer,) + block, idx(0))] if layer else []
    out = pl.BlockSpec((layer + 1,) + block, idx(0))
    return cur, prev, out


def _ssd_step(z, xbc, dtr, conv_all, h_all, h_done, lp, *, batch, seq, layer):
    n = batch * seq
    srows = SSD_HEADS * SSD_HEAD_DIM
    consts = (lp["conv_w"], lp["conv_b"], lp["dt_bias"], lp["a_neg_exp"], lp["d_skip"], lp["ssd_norm"], lp["head_expand"])
    hspec, prev_specs, hout_spec = _layer_state_specs(layer, (batch, SSD_STEP_TILES, SSD_STATE), 1)
    prev_args = [h_done] if layer else []
    cshape = (SSD_CONV - 1, batch, SSD_CONV_DIM)
    return pl.pallas_call(
        functools.partial(_ssd_step_body, seq=seq, batch=batch, layer=layer),
        grid=(srows // SSD_STEP_TILES,),
        in_specs=[_full_spec((n, GROUP_WIDTH)), _full_spec((n, SSD_CONV_DIM)), _full_spec((n, LANES)),
                  pl.BlockSpec((None,) + cshape, lambda j: (layer, 0, 0, 0)), hspec] + prev_specs
                 + [_pspec(a) for a in consts],
        out_specs=[_full_spec((n, GROUP_WIDTH)), _full_spec(cshape), hout_spec],
        out_shape=[jax.ShapeDtypeStruct((n, GROUP_WIDTH), F32),
                   jax.ShapeDtypeStruct(cshape, F32),
                   jax.ShapeDtypeStruct((layer + 1, batch, srows, SSD_STATE), F32)],
        scratch_shapes=[pltpu.VMEM((seq, batch, GROUP_WIDTH), F32),
                        pltpu.VMEM((seq, SSD_GROUPS, SSD_STATE, batch), F32),
                        pltpu.VMEM((seq, SSD_GROUPS, SSD_STATE, batch), F32),
                        pltpu.VMEM((seq, GROUP_WIDTH, batch), F32),
                        pltpu.VMEM((seq, GROUP_WIDTH, batch), F32),
                        pltpu.VMEM((seq, GROUP_WIDTH, batch), F32)],
        compiler_params=_cparams("arbitrary"),
        name="ssd_step",
    )(z, xbc, dtr, conv_all, h_all.reshape(h_all.shape[0], batch, srows, SSD_STATE),
      *prev_args, *[_parg(a) for a in consts])


PAIR = 2 * RWKV_HEAD
RWKV_PAIRS = RWKV_HEADS // 2


def _bd(x):
    half = x.shape[1] // 2
    lane = lax.broadcasted_iota(jnp.int32, x.shape, 1)
    zero = jnp.zeros_like(x)
    return jnp.concatenate([jnp.where(lane < half, x, zero), jnp.where(lane >= half, x, zero)], axis=0)


def _half_sums(x, lo):
    s_lo = jnp.sum(jnp.where(lo, x, 0.0), axis=-1, keepdims=True)
    s_hi = jnp.sum(jnp.where(lo, 0.0, x), axis=-1, keepdims=True)
    return jnp.where(lo, s_lo, s_hi)


def _head_sum(x):
    lo = lax.broadcasted_iota(jnp.int32, (x.shape[0], PAIR), 1) < RWKV_HEAD
    return jnp.concatenate([_half_sums(x[:, p * PAIR:(p + 1) * PAIR], lo) for p in range(RWKV_PAIRS)], axis=-1)


def _rwkv_pointwise(u, prev, mu_ref, w0_ref, w2_ref, a0_ref, a2_ref, g2_ref, kk_ref, ka_ref):
    GW = GROUP_WIDTH
    xs = u + (prev - u) * mu_ref[...]
    r = xs[:, 0:GW]
    k = xs[:, GW:2 * GW]
    v = xs[:, 2 * GW:3 * GW]
    wd = xs[:, 3 * GW:3 * GW + 64]
    ad = xs[:, 3 * GW + 64:3 * GW + 128]
    gd = xs[:, 3 * GW + 128:3 * GW + 256]
    w_lin = w0_ref[...] + _dot(jnp.tanh(wd).astype(BF16), w2_ref[...])
    logdecay = -math.exp(-0.5) * _sigmoid(w_lin)
    a = _sigmoid(a0_ref[...] + _dot(ad.astype(BF16), a2_ref[...]))
    g = _dot(_sigmoid(gd).astype(BF16), g2_ref[...])
    kk = k * kk_ref[...]
    kk = kk * lax.rsqrt(jnp.maximum(_head_sum(kk * kk), 1e-24))
    k = k * (1.0 + (a - 1.0) * ka_ref[...])
    return r, k, v, logdecay, a, g, kk


def _rwkv_finish(y, r, k, v, g, rk_ref, lng_ref, lnb_ref):
    mean = _head_sum(y) * (1.0 / RWKV_HEAD)
    yc = y - mean
    var = _head_sum(yc * yc) * (1.0 / RWKV_HEAD)
    y = yc * lax.rsqrt(var + RWKV_LN_EPS) * lng_ref[...] + lnb_ref[...]
    bonus = _head_sum(r * k * rk_ref[...]) * v
    return (y + bonus) * g


def _rwkv_body(u_ref, mu_ref, w0_ref, w2_ref, a0_ref, a2_ref, g2_ref, kk_ref, ka_ref, rk_ref,
               lng_ref, lnb_ref, y_ref, shift_ref, sout_ref, upad_scr, s_scr, *, chunk, group):
    L, G = chunk, group
    GL = G * L
    c = pl.program_id(1)
    pad = SUBLANES

    @pl.when(c == 0)
    def _():
        upad_scr[0:pad, :] = jnp.zeros((pad, RWKV_PROJ), F32)
        s_scr[...] = jnp.zeros(s_scr.shape, F32)

    u = u_ref[...]
    upad_scr[pad:pad + GL, :] = u
    prev = pltpu.roll(upad_scr[...], 1, axis=0)[pad:pad + GL]
    upad_scr[pad - 1:pad, :] = u[GL - 1:GL, :]
    r, k, v, logdecay, a, g, kk = _rwkv_pointwise(u, prev, mu_ref, w0_ref, w2_ref, a0_ref, a2_ref, g2_ref,
                                                  kk_ref, ka_ref)

    tril = jnp.where(lax.broadcasted_iota(jnp.int32, (L, L), 0) >= lax.broadcasted_iota(jnp.int32, (L, L), 1),
                     1.0, 0.0).astype(F32)
    cl = jnp.concatenate([_dot(tril, logdecay[i * L:(i + 1) * L, :], precision=HIGHEST) for i in range(G)], axis=0)
    e_in = jnp.exp(cl)
    e_inv = jnp.exp(-cl)
    r_t = r * e_in
    r_tb = r_t.astype(BF16)
    a_tb = (-kk * jnp.exp(cl - logdecay)).astype(BF16)
    b_tb = (kk * a * e_inv).astype(BF16)
    k_tb = (k * e_inv).astype(BF16)
    vb = v.astype(BF16)

    row = lax.broadcasted_iota(jnp.int32, (L, PAIR), 0)
    colh = lax.broadcasted_iota(jnp.int32, (L, PAIR), 1) & (RWKV_HEAD - 1)
    strict = row > colh
    incl = row >= colh
    eye_pair = jnp.where(row == colh, 1.0, 0.0).astype(F32)
    lane_lo = lax.broadcasted_iota(jnp.int32, (RWKV_HEAD, PAIR), 1) < RWKV_HEAD
    same_head = (lax.broadcasted_iota(jnp.int32, (PAIR, PAIR), 0) < RWKV_HEAD) == \
                (lax.broadcasted_iota(jnp.int32, (PAIR, PAIR), 1) < RWKV_HEAD)

    streams = [(i, p) for i in range(G) for p in range(RWKV_PAIRS)]
    ns = len(streams)
    blk = lambda x, i, p: x[i * L:(i + 1) * L, p * PAIR:(p + 1) * PAIR]
    lhs = [jnp.concatenate([blk(a_tb, i, p), blk(r_tb, i, p)], axis=0) for i, p in streams]
    m_both = [_dot_nt(lhs[s], jnp.concatenate([_bd(blk(b_tb, i, p)), _bd(blk(k_tb, i, p))], axis=0))
              for s, (i, p) in enumerate(streams)]
    m_ab = [m[:, 0:PAIR] for m in m_both]
    m_ak = [m[:, PAIR:2 * PAIR] for m in m_both]
    n_ab = [jnp.where(strict, m[0:L], 0.0) for m in m_ab]
    m_rb = [jnp.where(incl, m[L:2 * L], 0.0).astype(BF16) for m in m_ab]
    n_ak = [jnp.where(strict, m[0:L], 0.0).astype(BF16) for m in m_ak]
    m_rk = [jnp.where(incl, m[L:2 * L], 0.0).astype(BF16) for m in m_ak]
    tinv = [eye_pair + n for n in n_ab]
    pwb = [n.astype(BF16) for n in n_ab]
    pw = [_dot(x, _bd(x)) for x in pwb]
    for _ in range(int(math.log2(L)) - 2):
        pwb = [x.astype(BF16) for x in pw]
        both = [_dot(jnp.concatenate([pwb[s], tinv[s].astype(BF16)], axis=0), _bd(pwb[s])) for s in range(ns)]
        pw = [x[0:L] for x in both]
        tinv = [tinv[s] + both[s][L:2 * L] for s in range(ns)]
    pwb = [x.astype(BF16) for x in pw]
    tinv = [tinv[s] + _dot(tinv[s].astype(BF16), _bd(pwb[s])) for s in range(ns)]
    tinvb = [x.astype(BF16) for x in tinv]
    nv_mv = [_dot(jnp.concatenate([n_ak[s], m_rk[s]], axis=0), _bd(blk(vb, i, p))) for s, (i, p) in enumerate(streams)]
    wu = [_dot(tinvb[s], jnp.concatenate([_bd(blk(a_tb, i, p)), _bd(nv_mv[s][0:L].astype(BF16))], axis=1))
          for s, (i, p) in enumerate(streams)]
    wub = [x.astype(BF16) for x in wu]
    qy = [_dot(m_rb[s], jnp.concatenate([_bd(wub[s][:, 0:PAIR]), _bd(wub[s][:, PAIR:2 * PAIR])], axis=1))
          for s in range(ns)]
    q = [(blk(r_t, i, p) + qy[s][:, 0:PAIR]).astype(BF16) for s, (i, p) in enumerate(streams)]
    y_loc = [qy[s][:, PAIR:2 * PAIR] + nv_mv[s][L:2 * L] for s in range(ns)]
    zeros_b = jnp.zeros((L, PAIR), BF16)
    mg = [_dot_tn(jnp.concatenate([wub[s], jnp.concatenate([zeros_b, blk(vb, i, p)], axis=1)], axis=0),
                  jnp.concatenate([blk(b_tb, i, p), blk(k_tb, i, p)], axis=0))
          for s, (i, p) in enumerate(streams)]
    p_end = [e_in[(i + 1) * L - 1:(i + 1) * L, p * PAIR:(p + 1) * PAIR] for i, p in streams]
    m_t = [(jnp.where(same_head, mg[s][0:PAIR], 0.0) * p_end[s]).astype(BF16) for s in range(ns)]
    g_t = [jnp.where(lane_lo, mg[s][PAIR:PAIR + RWKV_HEAD], mg[s][PAIR + RWKV_HEAD:2 * PAIR]) * p_end[s]
           for s in range(ns)]

    y_rows = []
    for i in range(G):
        y_pairs = []
        for p in range(RWKV_PAIRS):
            s = i * RWKV_PAIRS + p
            s0 = s_scr[p]
            s0b = s0.astype(BF16)
            y_pairs.append(_dot_nt(q[s], _bd(s0b)) + y_loc[s])
            s_scr[p] = s0 * p_end[s] + _dot(s0b, m_t[s]) + g_t[s]
        y_rows.append(jnp.concatenate(y_pairs, axis=-1))
    y = jnp.concatenate(y_rows, axis=0)
    y_ref[...] = _rwkv_finish(y, r, k, v, g, rk_ref, lng_ref, lnb_ref)

    @pl.when(c == pl.num_programs(1) - 1)
    def _():
        sout_ref[0] = s_scr[...]
        shift_ref[0] = upad_scr[pad - 1:pad, :]


_RWKV_PARAM_NAMES = ("mu", "w0", "w2", "a0", "a2", "g2", "k_k", "k_a", "r_k", "ln_g", "ln_b")


def _rwkv(u, p, *, batch, seq):
    rows = RWKV_CHUNK * RWKV_GROUP
    nc = seq // rows
    params = [p[n] for n in _RWKV_PARAM_NAMES]
    sspec = pl.BlockSpec((1, RWKV_PAIRS, RWKV_HEAD, PAIR), lambda b, c: (b, 0, 0, 0))
    y, shift, s_last = pl.pallas_call(
        functools.partial(_rwkv_body, chunk=RWKV_CHUNK, group=RWKV_GROUP),
        grid=(batch, nc),
        in_specs=[pl.BlockSpec((rows, RWKV_PROJ), lambda b, c: (b * nc + c, 0))] + [_pspec(a) for a in params],
        out_specs=[pl.BlockSpec((rows, GROUP_WIDTH), lambda b, c: (b * nc + c, 0)),
                   pl.BlockSpec((1, 1, RWKV_PROJ), lambda b, c: (b, 0, 0)), sspec],
        out_shape=[jax.ShapeDtypeStruct((batch * seq, GROUP_WIDTH), F32),
                   jax.ShapeDtypeStruct((batch, 1, RWKV_PROJ), F32),
                   jax.ShapeDtypeStruct((batch, RWKV_PAIRS, RWKV_HEAD, PAIR), F32)],
        scratch_shapes=[pltpu.VMEM((SUBLANES + rows, RWKV_PROJ), F32),
                        pltpu.VMEM((RWKV_PAIRS, RWKV_HEAD, PAIR), F32)],
        compiler_params=_cparams("parallel", "arbitrary"),
        name="rwkv",
    )(u, *[_parg(a) for a in params])
    s_last = s_last.reshape(batch, RWKV_PAIRS, RWKV_HEAD, 2, RWKV_HEAD).transpose(0, 1, 3, 2, 4).reshape(
        batch, RWKV_HEADS, RWKV_HEAD, RWKV_HEAD)
    return y, shift.reshape(batch, RWKV_PROJ), s_last


def _rwkv_step_body(u_ref, shift0_ref, s0_ref, *rest, seq, batch, layer):
    sdone_ref, rest = (rest[0], rest[1:]) if layer else (None, rest)
    (mu_ref, w0_ref, w2_ref, a0_ref, a2_ref, g2_ref, kk_ref, ka_ref, rk_ref, lng_ref, lnb_ref, y_ref, sout_ref,
     r_scr, w_scr, k_scr, b_scr, nkk_scr, v_scr, y_scr) = rest
    T, B = seq, batch
    j = pl.program_id(0)
    if layer:
        sout_ref[0:layer] = sdone_ref[...]
    tiles = RWKV_STEP_TILES

    def pointwise(t):
        u = u_ref[t * B:(t + 1) * B, :]
        prev = shift0_ref[...] if t == 0 else u_ref[(t - 1) * B:t * B, :]
        return _rwkv_pointwise(u, prev, mu_ref, w0_ref, w2_ref, a0_ref, a2_ref, g2_ref, kk_ref, ka_ref)

    @pl.when(j == 0)
    def _():
        for t in range(T):
            r, k, v, logdecay, a, _, kk = pointwise(t)
            r_scr[t] = r.T
            w_scr[t] = jnp.exp(logdecay).T
            k_scr[t] = k.T
            b_scr[t] = (kk * a).T
            nkk_scr[t] = (-kk).T
            v_scr[t] = v.T

    i0 = j * tiles
    keys = pl.ds(pl.multiple_of((i0 // RWKV_HEAD) * RWKV_HEAD, RWKV_HEAD), RWKV_HEAD)
    for q in range(tiles):
        vi = pl.ds(i0 + q, 1)
        s = s0_ref[q]
        for t in range(T):
            sa = jnp.sum(s * nkk_scr[t, keys, :], axis=0, keepdims=True)
            s = s * w_scr[t, keys, :] + k_scr[t, keys, :] * v_scr[t, vi, :] + b_scr[t, keys, :] * sa
            y_scr[t, vi, :] = jnp.sum(s * r_scr[t, keys, :], axis=0, keepdims=True)
        sout_ref[layer, q] = s

    @pl.when(j == pl.num_programs(0) - 1)
    def _():
        for t in range(T):
            r, k, v, _, _, g, _ = pointwise(t)
            y_ref[t * B:(t + 1) * B, :] = _rwkv_finish(y_scr[t].T, r, k, v, g, rk_ref, lng_ref, lnb_ref)


def _rwkv_step(u, shift0, s_all, s_done, p, *, batch, seq, layer):
    n = batch * seq
    srows = RWKV_HEADS * RWKV_HEAD
    params = [p[nm] for nm in _RWKV_PARAM_NAMES]
    sspec, prev_specs, sout_spec = _layer_state_specs(layer, (RWKV_STEP_TILES, RWKV_HEAD, batch), 0)
    prev_args = [s_done] if layer else []
    tposed = pltpu.VMEM((seq, GROUP_WIDTH, batch), F32)
    return pl.pallas_call(
        functools.partial(_rwkv_step_body, seq=seq, batch=batch, layer=layer),
        grid=(srows // RWKV_STEP_TILES,),
        in_specs=[_full_spec((n, RWKV_PROJ)), _full_spec((batch, RWKV_PROJ)), sspec] + prev_specs
                 + [_pspec(a) for a in params],
        out_specs=[_full_spec((n, GROUP_WIDTH)), sout_spec],
        out_shape=[jax.ShapeDtypeStruct((n, GROUP_WIDTH), F32),
                   jax.ShapeDtypeStruct((layer + 1, srows, RWKV_HEAD, batch), F32)],
        scratch_shapes=[tposed] * 7,
        compiler_params=_cparams("arbitrary"),
        name="rwkv_step",
    )(u, shift0, s_all, *prev_args, *[_parg(a) for a in params])


def _s5_body(u_ref, hre0_ref, him0_ref, are_ref, aim_ref, bmat_ref, cmat_ref, d_ref, gw_ref, gb_ref,
             y_ref, hre_ref, him_ref, hs_scr, tm_scr, *, steps, batch_major):
    c = pl.program_id(1)
    ns = S5_WIDTH
    bsub = hre_ref.shape[0]

    @pl.when(c == 0)
    def _():
        hre_ref[...] = hre0_ref[...]
        him_ref[...] = him0_ref[...]

    if batch_major:
        for b in range(bsub):
            tm_scr[:, b, :] = u_ref[b]
        u = tm_scr[...].reshape(steps * bsub, GROUP_WIDTH)
    else:
        u = u_ref[...].reshape(steps * bsub, GROUP_WIDTH)
    are = jnp.broadcast_to(are_ref[...], (bsub, ns))
    aim = jnp.broadcast_to(aim_ref[...], (bsub, ns))
    hre, him = hre_ref[...], him_ref[...]
    sub = min(S5_SUB, steps)
    rows = sub * bsub
    outs = []
    hs_scr[...] = _dot(u.astype(BF16), bmat_ref[...])
    for k in range(steps // sub):
        r0 = k * rows
        u_k = u[r0:r0 + rows]
        for t in range(sub):
            rs = slice(r0 + t * bsub, r0 + (t + 1) * bsub)
            hre, him = (are * hre - aim * him + hs_scr[rs, 0:ns], are * him + aim * hre + hs_scr[rs, ns:2 * ns])
            hs_scr[rs, 0:ns] = hre
            hs_scr[rs, ns:2 * ns] = him
        y = _dot(hs_scr[r0:r0 + rows, :].astype(BF16), cmat_ref[...]) + u_k * d_ref[...]
        y = _gelu_tanh(y)
        yy = _dot(y.astype(BF16), gw_ref[...]) + gb_ref[...]
        outs.append(yy[:, 0:GROUP_WIDTH] * _sigmoid(yy[:, GROUP_WIDTH:2 * GROUP_WIDTH]))
    hre_ref[...] = hre
    him_ref[...] = him
    out = jnp.concatenate(outs, axis=0).reshape(steps, bsub, GROUP_WIDTH)
    if batch_major:
        tm_scr[...] = out
        for b in range(bsub):
            y_ref[b] = tm_scr[:, b, :]
    else:
        y_ref[...] = out


def _time_specs(u, batch_major, chunk):
    if batch_major:
        batch, seq, _ = u.shape
        steps = min(chunk, seq)
        bsub = SUBLANES
        spec = pl.BlockSpec((bsub, steps, GROUP_WIDTH), lambda b, c: (b, c, 0))
    else:
        seq, batch, _ = u.shape
        steps = min(chunk, seq)
        bsub = min(batch, SUBLANES * max(1, chunk // steps))
        spec = pl.BlockSpec((steps, bsub, GROUP_WIDTH), lambda b, c: (c, b, 0))
    return batch, seq, steps, bsub, spec


def _s5(u, hre0, him0, lp, *, batch_major):
    batch, seq, steps, bsub, tspec = _time_specs(u, batch_major, TM_CHUNK)
    hspec =pl.BlockSpec((bsub, S5_WIDTH), lambda b, c: (b, 0))
    consts = (lp["s5_are"], lp["s5_aim"], lp["s5_bmat"], lp["s5_cmat"], lp["s5_d"], lp["s5_gw"], lp["s5_gb"])
    return pl.pallas_call(
        functools.partial(_s5_body, steps=steps, batch_major=batch_major),
        grid=(batch // bsub, seq // steps),
        in_specs=[tspec, hspec, hspec] + [_pspec(a) for a in consts],
        out_specs=[tspec, hspec, hspec],
        out_shape=[jax.ShapeDtypeStruct(u.shape, F32),
                   jax.ShapeDtypeStruct((batch, S5_WIDTH), F32),
                   jax.ShapeDtypeStruct((batch, S5_WIDTH), F32)],
        scratch_shapes=[pltpu.VMEM((steps * bsub, 2 * S5_WIDTH), F32),
                        pltpu.VMEM((steps, bsub, GROUP_WIDTH), F32)],
        compiler_params=_cparams("parallel", "arbitrary"),
        name="s5",
    )(u, hre0, him0, *[_parg(a) for a in consts])


def _pool_body(u_ref, buf0_ref, pw_ref, sc_ref, y_ref, buf_ref, f_scr, tm_scr, *, steps, pos0, batch_major):
    c = pl.program_id(1)
    bsub = f_scr.shape[1]
    GW = GROUP_WIDTH
    halo = POOL_BUF + 1

    @pl.when(c == 0)
    def _():
        f_scr[0] = jnp.zeros((bsub, GW), F32)
        f_scr[1:halo] = buf0_ref[...]

    if batch_major:
        for b in range(bsub):
            f_scr[halo:halo + steps, b, :] = u_ref[b]
    else:
        f_scr[halo:halo + steps] = u_ref[...]
    f = f_scr[...]
    u = f[halo:halo + steps]
    s2 = f[1:] + f[:-1]
    s4 = s2[2:] + s2[:-2]
    s8 = s4[4:] + s4[:-4]
    s16 = s8[8:] + s8[:-8]
    f_scr[0:halo] = f[steps:steps + halo]
    lane = lax.broadcasted_iota(jnp.int32, (steps, bsub, GW), 2)
    tpos = lax.broadcasted_iota(jnp.int32, (steps, bsub, GW), 0) + (pos0 + 1) + c * steps
    win = jnp.where(lane < POOL_CH, s2[halo - 1:halo - 1 + steps],
                    jnp.where(lane < 2 * POOL_CH, s4[halo - 3:halo - 3 + steps],
                              jnp.where(lane < 3 * POOL_CH, s8[halo - 7:halo - 7 + steps],
                                        s16[halo - 15:halo - 15 + steps])))
    wlen = jnp.where(lane < POOL_CH, POOL_WINDOWS[0],
                     jnp.where(lane < 2 * POOL_CH, POOL_WINDOWS[1],
                               jnp.where(lane < 3 * POOL_CH, POOL_WINDOWS[2], POOL_WINDOWS[3])))
    cnt = jnp.minimum(tpos, wlen).astype(F32)
    pooled = (win / cnt - u).reshape(steps * bsub, GW)
    y = (_dot(pooled.astype(BF16), pw_ref[...]) * sc_ref[...]).reshape(steps, bsub, GW)
    if batch_major:
        tm_scr[...] = y
        for b in range(bsub):
            y_ref[b] = tm_scr[:, b, :]
    else:
        y_ref[...] = y

    @pl.when(c == pl.num_programs(1) - 1)
    def _():
        buf_ref[...] = f_scr[1:halo]


def _pool(u, buf0, lp, *, pos0, batch_major, layer=None):
    batch, seq, steps, bsub, tspec = _time_specs(u, batch_major, POOL_CHUNK)
    bblock =(POOL_BUF, bsub, GROUP_WIDTH)
    bspec = pl.BlockSpec(bblock, lambda b, c: (0, b, 0))
    if layer is None:
        bspec_in = bspec
    else:
        bspec_in = pl.BlockSpec((None,) + bblock, lambda b, c: (layer, 0, b, 0))
    return pl.pallas_call(
        functools.partial(_pool_body, steps=steps, pos0=pos0, batch_major=batch_major),
        grid=(batch // bsub, seq // steps),
        in_specs=[tspec, bspec_in, _pspec(lp["pool_w"]), _pspec(lp["pool_scale"])],
        out_specs=[tspec, bspec],
        out_shape=[jax.ShapeDtypeStruct(u.shape, F32), jax.ShapeDtypeStruct((POOL_BUF, batch, GROUP_WIDTH), F32)],
        scratch_shapes=[pltpu.VMEM((POOL_BUF + 1 + steps, bsub, GROUP_WIDTH), F32),
                        pltpu.VMEM((steps, bsub, GROUP_WIDTH), F32)],
        compiler_params=_cparams("parallel", "arbitrary"),
        name="pool",
    )(u, buf0, _parg(lp["pool_w"]), _parg(lp["pool_scale"]))


def _block_diag(blocks):
    n, g, r, c = blocks.shape
    eye = jnp.eye(g, dtype=blocks.dtype)
    return (eye[None, :, None, :, None] * blocks[:, :, :, None, :]).reshape(n, g * r, g * c)


def _stacked_params(P):
    row = lambda a: a.reshape(a.shape[0], 1, -1)
    pad_lanes = lambda a: jnp.pad(a, ((0, 0), (0, LANES - a.shape[1])))
    bf = lambda a: a.astype(BF16)

    lam = lax.complex(P["s5_lam_re"], P["s5_lam_im"])
    a_bar = jnp.exp(lam * jnp.exp(P["s5_log_step"])[..., None])
    b_bar = ((a_bar - 1.0) / lam)[..., None] * lax.complex(P["s5_b_re"], P["s5_b_im"])
    b_t = jnp.swapaxes(b_bar, 2, 3)
    bmat = jnp.concatenate([_block_diag(jnp.real(b_t)), _block_diag(jnp.imag(b_t))], axis=2)
    c_t = jnp.swapaxes(lax.complex(P["s5_c_re"], P["s5_c_im"]), 2, 3)
    cmat = jnp.concatenate([_block_diag(jnp.real(c_t)), -_block_diag(jnp.imag(c_t))], axis=1)

    out = dict(
        norm_ffn1=row(P["norm_ffn1"]), ffn1_in=P["ffn1_in"], ffn1_out=P["ffn1_out"],
        norm_mix=row(P["norm_mix"]),
        w_in=jnp.transpose(P["w_in"], (2, 0, 1)),
        conv_w=P["ssd_conv_w"], conv_b=row(P["ssd_conv_b"]),
        dt_bias=row(pad_lanes(P["ssd_dt_bias"])), a_log=row(pad_lanes(P["ssd_a_log"])),
        a_neg_exp=row(jnp.repeat(-jnp.exp(P["ssd_a_log"]), SSD_HEAD_DIM, axis=1)),
        d_skip=row(jnp.repeat(P["ssd_d"], SSD_HEAD_DIM, axis=1)), ssd_norm=row(P["ssd_norm"]),
        s5_are=row(jnp.real(a_bar)), s5_aim=row(jnp.imag(a_bar)), s5_bmat=bf(bmat), s5_cmat=bf(cmat),
        s5_d=row(P["s5_d"]), s5_gw=bf(P["s5_glu_w"]), s5_gb=row(P["s5_glu_b"]),
        pool_w=bf(_block_diag(P["pool_w"])), pool_scale=row(P["pool_scale"]),
        w_out=bf(P["w_out"]),
        norm_ffn2=row(P["norm_ffn2"]), ffn2_in=P["ffn2_in"], ffn2_out=P["ffn2_out"],
    )
    for name in _RWKV_PARAM_NAMES:
        a = P["rwkv_" + name]
        out["rwkv_" + name] = bf(a) if name in ("w2", "a2", "g2") else row(a)
    return out


def _layer_params(stacked, l):
    lp = {k: _Layered((v, l)) for k, v in stacked.items()}
    lp["rwkv"] = {n: lp["rwkv_" + n] for n in _RWKV_PARAM_NAMES}
    lp["head_expand"] = jnp.pad(jnp.repeat(jnp.eye(SSD_HEADS, dtype=F32), SSD_HEAD_DIM, axis=1),
                                ((0, LANES - SSD_HEADS), (0, 0)))
    return lp


def _mixers_prompt(lp, proj, *, batch, seq):
    z, xbc, ur, us5, upool, dtr = proj
    y_ssd, conv_new, ssd_new = _ssd(z, xbc, dtr, lp, batch=batch, seq=seq)
    y_rwkv, shift_new, rwkv_new = _rwkv(ur, lp["rwkv"], batch=batch, seq=seq)
    zeros = jnp.zeros((batch, S5_WIDTH), F32)
    bm = lambda a: a.reshape(batch, seq, a.shape[-1])
    rows = lambda a: a.reshape(batch * seq, a.shape[-1])
    y_s5, s5re, s5im = _s5(bm(us5), zeros, zeros, lp, batch_major=True)
    y_pool, pool_new = _pool(bm(upool), jnp.zeros((POOL_BUF, batch, GROUP_WIDTH), F32), lp, pos0=0,
                             batch_major=True)
    ys = (y_ssd, y_rwkv, rows(y_s5), rows(y_pool))
    states = (conv_new, ssd_new, shift_new, rwkv_new, s5re.reshape(batch, S5_GROUPS, S5_STATE),
              s5im.reshape(batch, S5_GROUPS, S5_STATE), jnp.swapaxes(pool_new, 0, 1))
    return ys, states


def _mixers_decode(lp, proj, states, done, *, batch, seq, layer):
    z, xbc, ur, us5, upool, dtr = proj
    shift0, s5re0, s5im0 = (states[i][layer] for i in (2, 4, 5))
    ssd_done, rwkv_done = (done[1], done[3]) if layer else (None, None)
    y_ssd, conv_new, ssd_new = _ssd_step(z, xbc, dtr, states[0], states[1], ssd_done, lp, batch=batch, seq=seq,
                                         layer=layer)
    y_rwkv, rwkv_new = _rwkv_step(ur, shift0, states[3], rwkv_done, lp["rwkv"], batch=batch, seq=seq, layer=layer)
    shift_new = ur[(seq - 1) * batch:, :]
    tm = lambda a: a.reshape(seq, batch, a.shape[-1])
    y_s5, s5re, s5im = _s5(tm(us5), s5re0.reshape(batch, S5_WIDTH), s5im0.reshape(batch, S5_WIDTH), lp,
                           batch_major=False)
    y_pool, pool_new = _pool(tm(upool), states[6], lp, pos0=PAST_LEN, batch_major=False, layer=layer)
    rows = lambda a: a.reshape(seq * batch, a.shape[-1])
    ys = (y_ssd, y_rwkv, rows(y_s5), rows(y_pool))
    new_states = (jnp.swapaxes(conv_new, 0, 1), ssd_new, shift_new, rwkv_new,
                  s5re.reshape(batch, S5_GROUPS, S5_STATE), s5im.reshape(batch, S5_GROUPS, S5_STATE),
                  jnp.swapaxes(pool_new, 0, 1))
    return ys, new_states


_WIDTHS = (GROUP_WIDTH, SSD_CONV_DIM, RWKV_PROJ, GROUP_WIDTH, GROUP_WIDTH, LANES)


def _trunk(x_p, x_s, layer_params, norm_final, mixers_p, mixers_s):
    st_p, st_s = [], []
    mix_p, mix_s, lp = None, None, None
    for l, lp_next in enumerate(layer_params):
        if l > 0:
            x_s, wg, wu, wo = _ffn_cast(x_s, lp["norm_ffn2"], lp["ffn2_in"], lp["ffn2_out"], mix=mix_s, wmix=lp["w_out"])
            x_p = _ffn(x_p, lp["norm_ffn2"], wg, wu, wo, mix=mix_p, wmix=lp["w_out"])
        lp = lp_next
        x_s, wg, wu, wo = _ffn_cast(x_s, lp["norm_ffn1"], lp["ffn1_in"], lp["ffn1_out"])
        x_p = _ffn(x_p, lp["norm_ffn1"], wg, wu, wo)
        proj_s, w_all = _inproj_cast(x_s, lp["norm_mix"], lp["w_in"], _WIDTHS)
        mix_p, st = mixers_p(l, lp, _inproj(x_p, lp["norm_mix"], w_all, _WIDTHS), st_p[-1] if st_p else None)
        st_p.append(st)
        mix_s, st = mixers_s(l, lp, proj_s, st_s[-1] if st_s else None)
        st_s.append(st)
    x_s, wg, wu, wo = _ffn_cast(x_s, lp["norm_ffn2"], lp["ffn2_in"], lp["ffn2_out"], mix=mix_s, wmix=lp["w_out"],
                                gf=norm_final)
    x_p = _ffn(x_p, lp["norm_ffn2"], wg, wu, wo, mix=mix_p, wmix=lp["w_out"], gf=norm_final)
    return (x_p, x_s), (st_p, st_s)


def kernel(x_prompt, x_sample, state_ssd_conv, state_ssd, state_rwkv_shift, state_rwkv, state_s5_re, state_s5_im, state_pool, norm_ffn1, ffn1_in, ffn1_out, norm_mix, w_in, ssd_conv_w, ssd_conv_b, ssd_dt_bias, ssd_a_log, ssd_d, ssd_norm, rwkv_mu, rwkv_w0, rwkv_w2, rwkv_a0, rwkv_a2, rwkv_g2, rwkv_k_k, rwkv_k_a, rwkv_r_k, rwkv_ln_g, rwkv_ln_b, s5_lam_re, s5_lam_im, s5_log_step, s5_b_re, s5_b_im, s5_c_re, s5_c_im, s5_d, s5_glu_w, s5_glu_b, pool_w, pool_scale, w_out, norm_ffn2, ffn2_in, ffn2_out, norm_final):
    P = dict(norm_ffn1=norm_ffn1, ffn1_in=ffn1_in, ffn1_out=ffn1_out, norm_mix=norm_mix, w_in=w_in,
             ssd_conv_w=ssd_conv_w, ssd_conv_b=ssd_conv_b, ssd_dt_bias=ssd_dt_bias, ssd_a_log=ssd_a_log,
             ssd_d=ssd_d, ssd_norm=ssd_norm, rwkv_mu=rwkv_mu, rwkv_w0=rwkv_w0, rwkv_w2=rwkv_w2, rwkv_a0=rwkv_a0,
             rwkv_a2=rwkv_a2, rwkv_g2=rwkv_g2, rwkv_k_k=rwkv_k_k, rwkv_k_a=rwkv_k_a,
             rwkv_r_k=rwkv_r_k.reshape(rwkv_r_k.shape[0], -1), rwkv_ln_g=rwkv_ln_g, rwkv_ln_b=rwkv_ln_b,
             s5_lam_re=s5_lam_re, s5_lam_im=s5_lam_im, s5_log_step=s5_log_step, s5_b_re=s5_b_re, s5_b_im=s5_b_im,
             s5_c_re=s5_c_re, s5_c_im=s5_c_im, s5_d=s5_d, s5_glu_w=s5_glu_w, s5_glu_b=s5_glu_b, pool_w=pool_w,
             pool_scale=pool_scale, w_out=w_out, norm_ffn2=norm_ffn2, ffn2_in=ffn2_in, ffn2_out=ffn2_out)
    depth = norm_ffn1.shape[0]
    bp, tp, d = x_prompt.shape
    bs, ts, _ = x_sample.shape
    stacked = _stacked_params(P)
    layer_params = [_layer_params(stacked, l) for l in range(depth)]
    gf = norm_final.reshape(1, -1)
    sample_states = (state_ssd_conv, state_ssd, state_rwkv_shift, state_rwkv, state_s5_re, state_s5_im, state_pool)
    rwkv_rows = RWKV_HEADS * RWKV_HEAD
    decode_states = (jnp.swapaxes(state_ssd_conv, 1, 2), state_ssd, state_rwkv_shift,
                     jnp.transpose(state_rwkv, (0, 2, 3, 4, 1)).reshape(depth, rwkv_rows, RWKV_HEAD, bs),
                     state_s5_re, state_s5_im, jnp.swapaxes(state_pool, 1, 2))

    x_s = jnp.swapaxes(x_sample, 0, 1).reshape(ts * bs, d)
    (y_p, y_s), (st_p, st_s) = _trunk(
        x_prompt.reshape(bp * tp, d), x_s, layer_params, gf,
        lambda l, lp, proj, done: _mixers_prompt(lp, proj, batch=bp, seq=tp),
        lambda l, lp, proj, done: _mixers_decode(lp, proj, decode_states, done, batch=bs, seq=ts, layer=l))
    outs = [y_p.reshape(bp, tp, d), jnp.swapaxes(y_s.reshape(ts, bs, d), 0, 1)]
    for i, ref_state in enumerate(sample_states):
        outs.append(jnp.stack([st[i] for st in st_p]))
        if i == 1:
            outs.append(st_s[-1][i].reshape(ref_state.shape))
        elif i == 3:
            s_new = st_s[-1][i].reshape(depth, RWKV_HEADS, RWKV_HEAD, RWKV_HEAD, bs)
            outs.append(jnp.transpose(s_new, (0, 4, 1, 2, 3)))
        else:
            outs.append(jnp.stack([st[i] for st in st_s]))
    return tuple(outs)
```

```python
import functools
import math

import jax
import jax.numpy as jnp
from jax import lax
from jax.experimental import pallas as pl
from jax.experimental.pallas import tpu as pltpu

F32 = jnp.float32
BF16 = jnp.bfloat16
HIGHEST = lax.Precision.HIGHEST

SUBLANES = 8
LANES = 128
VMEM_LIMIT_BYTES = 56 * 1024 * 1024

GROUP_WIDTH = 256
SSD_HEAD_DIM = 64
SSD_HEADS = 4
SSD_GROUPS = 2
SSD_STATE = 128
SSD_CONV = 4
SSD_CONV_DIM = GROUP_WIDTH + 2 * SSD_GROUPS * SSD_STATE
SSD_CHUNK = 128
SSD_GROUP = 8
LOG2_E = math.log2(math.e)
RWKV_HEAD = 64
RWKV_HEADS = 4
RWKV_PROJ = 1024
RWKV_LN_EPS = 64e-5
RWKV_CHUNK = 64
RWKV_GROUP = 16
S5_GROUPS = 16
S5_STATE = 64
S5_WIDTH = S5_GROUPS * S5_STATE
POOL_WINDOWS = (2, 4, 8, 16)
POOL_CH = 64
POOL_BUF = 15
RMS_EPS = 1e-6
PAST_LEN = 16384

ROW_TILE = 1024
INPROJ_ROW_TILE = 1024
FFN_CHUNK = 256
TM_CHUNK = 128
POOL_CHUNK = 256
S5_SUB = 128
SSD_STEP_TILES = 16
RWKV_STEP_TILES = 32


def _cparams(*sem):
    return pltpu.CompilerParams(dimension_semantics=sem, vmem_limit_bytes=VMEM_LIMIT_BYTES)


def _dot(a, b, **kw):
    return jnp.dot(a, b, preferred_element_type=F32, **kw)


def _dot_nt(a, b):
    return lax.dot_general(a, b, (((1,), (1,)), ((), ())), preferred_element_type=F32)


def _dot_tn(a, b):
    return lax.dot_general(a, b, (((0,), (0,)), ((), ())), preferred_element_type=F32)


def _sigmoid(x):
    return 0.5 * jnp.tanh(0.5 * x) + 0.5


def _silu(x):
    h = 0.5 * x
    return h + h * jnp.tanh(h)


def _softplus(x):
    return jnp.maximum(x, 0.0) + jnp.log(1.0 + jnp.exp(-jnp.abs(x)))


def _gelu_tanh(x):
    c = math.sqrt(2.0 / math.pi)
    return x * (0.5 * (1.0 + jnp.tanh(c * (x + 0.044715 * (x * x * x)))))


def _rms(x, g):
    return x * lax.rsqrt(jnp.mean(x * x, axis=-1, keepdims=True) + RMS_EPS) * g


def _full_spec(shape):
    n = len(shape)
    return pl.BlockSpec(shape, lambda *_: (0,) * n)


class _Layered(tuple):
    pass


def _pspec(p, single=False):
    mode = pl.Buffered(1) if single else None
    if isinstance(p, _Layered):
        a, l = p
        return pl.BlockSpec((None,) + a.shape[1:], lambda *_: (l,) + (0,) * (a.ndim - 1), pipeline_mode=mode)
    n = p.ndim
    return pl.BlockSpec(p.shape, lambda *_: (0,) * n, pipeline_mode=mode)


def _parg(p):
    return p[0] if isinstance(p, _Layered) else p


def _mix_residual(x, y_refs, wmix_ref):
    for j, y_ref in enumerate(y_refs):
        x = x + _dot(y_ref[...].astype(BF16), wmix_ref[j * GROUP_WIDTH:(j + 1) * GROUP_WIDTH, :])
    return x


def _swiglu_chunk(h, wg, wu, wo):
    act = (_silu(_dot(h, wg)) * _dot(h, wu)).astype(BF16)
    return _dot(act, wo)


def _ffn_body(*refs, has_mix, final_norm):
    it = iter(refs)
    x = next(it)[...]
    if has_mix:
        y_refs = [next(it) for _ in range(4)]
        x = _mix_residual(x, y_refs, next(it))
    g_ref, wg_ref, wu_ref, wo_ref = next(it), next(it), next(it), next(it)
    gf_ref = next(it) if final_norm else None
    o_ref = next(it)
    h = _rms(x, g_ref[...]).astype(BF16)
    acc = jnp.zeros_like(x)
    for c in range(wo_ref.shape[0] // FFN_CHUNK):
        cols = slice(c * FFN_CHUNK, (c + 1) * FFN_CHUNK)
        acc = acc + _swiglu_chunk(h, wg_ref[:, cols], wu_ref[:, cols], wo_ref[cols, :])
    x = x + 0.5 * acc
    if final_norm:
        x = _rms(x, gf_ref[...])
    o_ref[...] = x


def _ffn(x, g, wg, wu, wo, mix=None, wmix=None, gf=None):
    rows, d = x.shape
    row_spec = lambda w: pl.BlockSpec((ROW_TILE, w), lambda i: (i, 0))
    args, specs = [x], [row_spec(d)]
    if mix is not None:
        for y in mix:
            args.append(y)
            specs.append(row_spec(y.shape[1]))
        args.append(_parg(wmix))
        specs.append(_pspec(wmix, single=True))
    for a in (g, wg, wu, wo) + ((gf,) if gf is not None else ()):
        args.append(_parg(a))
        specs.append(_pspec(a, single=True))
    return pl.pallas_call(
        functools.partial(_ffn_body, has_mix=mix is not None, final_norm=gf is not None),
        grid=(rows // ROW_TILE,),
        in_specs=specs,
        out_specs=row_spec(d),
        out_shape=jax.ShapeDtypeStruct((rows, d), F32),
        compiler_params=_cparams("parallel"),
        name="ffn",
    )(*args)


def _ffn_cast_body(*refs, has_mix, final_norm):
    it = iter(refs)
    x_ref = next(it)
    if has_mix:
        y_refs = [next(it) for _ in range(4)]
        wmix_ref = next(it)
    g_ref, wg_ref, wu_ref, wo_ref = next(it), next(it), next(it), next(it)
    gf_ref = next(it) if final_norm else None
    o_ref, wg_out, wu_out, wo_out, x_scr, h_scr, acc_scr = (next(it) for _ in range(7))
    c = pl.program_id(0)

    @pl.when(c == 0)
    def _():
        x = x_ref[...]
        if has_mix:
            x = _mix_residual(x, y_refs, wmix_ref)
        x_scr[...] = x
        h_scr[...] = _rms(x, g_ref[...]).astype(BF16)
        acc_scr[...] = jnp.zeros(acc_scr.shape, F32)

    wg = wg_ref[...].astype(BF16)
    wu = wu_ref[...].astype(BF16)
    wo = wo_ref[...].astype(BF16)
    wg_out[...] = wg
    wu_out[...] = wu
    wo_out[...] = wo
    acc_scr[...] += _swiglu_chunk(h_scr[...], wg, wu, wo)

    @pl.when(c == pl.num_programs(0) - 1)
    def _():
        x = x_scr[...] + 0.5 * acc_scr[...]
        if final_norm:
            x = _rms(x, gf_ref[...])
        o_ref[...] = x


def _ffn_cast(x, g, wi, wo, mix=None, wmix=None, gf=None):
    rows, d = x.shape
    wi_all, l = wi
    wo_all, _ = wo
    d_ff = wo_all.shape[1]
    nchunks = d_ff // FFN_CHUNK
    args, specs = [x], [_full_spec(x.shape)]
    if mix is not None:
        for y in mix:
            args.append(y)
            specs.append(_full_spec(y.shape))
        args.append(_parg(wmix))
        specs.append(_pspec(wmix, single=True))
    args += [_parg(g), wi_all, wi_all, wo_all]
    specs += [_pspec(g),
              pl.BlockSpec((None, d, FFN_CHUNK), lambda c: (l, 0, c)),
              pl.BlockSpec((None, d, FFN_CHUNK), lambda c: (l, 0, c + nchunks)),
              pl.BlockSpec((None, FFN_CHUNK, d), lambda c: (l, c, 0))]
    if gf is not None:
        args.append(gf)
        specs.append(_full_spec(gf.shape))
    col_spec = pl.BlockSpec((d, FFN_CHUNK), lambda c: (0, c))
    return pl.pallas_call(
        functools.partial(_ffn_cast_body, has_mix=mix is not None, final_norm=gf is not None),
        grid=(nchunks,),
        in_specs=specs,
        out_specs=[_full_spec(x.shape), col_spec, col_spec, pl.BlockSpec((FFN_CHUNK, d), lambda c: (c, 0))],
        out_shape=[jax.ShapeDtypeStruct((rows, d), F32), jax.ShapeDtypeStruct((d, d_ff), BF16),
                   jax.ShapeDtypeStruct((d, d_ff), BF16), jax.ShapeDtypeStruct((d_ff, d), BF16)],
        scratch_shapes=[pltpu.VMEM((rows, d), F32), pltpu.VMEM((rows, d), BF16), pltpu.VMEM((rows, d), F32)],
        compiler_params=_cparams("arbitrary"),
        name="ffn_cast",
    )(*args)


def _inproj_body(x_ref, g_ref, wt_ref, *o_refs):
    h = _rms(x_ref[...], g_ref[...]).astype(BF16)
    off = 0
    for o_ref in o_refs:
        n = o_ref.shape[-1]
        o_ref[...] = _dot_nt(h, wt_ref[off:off + n, :])
        off += n


def _inproj_cast_body(x_ref, g_ref, win_ref, *o_refs, layer):
    *proj_refs, wall_ref = o_refs
    split = GROUP_WIDTH + SSD_CONV_DIM
    wt = win_ref[:, layer, :]
    tail = wt.shape[0] - split - SSD_HEADS
    wall_ref[0:split, :] = wt[0:split].astype(BF16)
    wall_ref[split:split + tail, :] = wt[split + SSD_HEADS:].astype(BF16)
    dt_rows = jnp.concatenate([wt[split:split + SSD_HEADS], jnp.zeros((LANES - SSD_HEADS, wt.shape[1]), F32)], axis=0)
    wall_ref[split + tail:, :] = dt_rows.astype(BF16)
    _inproj_body(x_ref, g_ref, wall_ref, *proj_refs)


def _inproj_cast(x, g, w_in, widths):
    rows, d = x.shape
    wt_all, l = w_in
    outs = pl.pallas_call(
        functools.partial(_inproj_cast_body, layer=l),
        grid=(1,),
        in_specs=[_full_spec(x.shape), _pspec(g),
                  pl.BlockSpec(wt_all.shape, lambda i: (0, 0, 0), pipeline_mode=pl.Buffered(1))],
        out_specs=[_full_spec((rows, n)) for n in widths] + [_full_spec((sum(widths), d))],
        out_shape=[jax.ShapeDtypeStruct((rows, n), F32) for n in widths]
                  + [jax.ShapeDtypeStruct((sum(widths), d), BF16)],
        compiler_params=_cparams("arbitrary"),
        name="inproj_cast",
    )(x, _parg(g), wt_all)
    return outs[:-1], outs[-1]


def _inproj(x, g, w, widths):
    rows, d = x.shape
    tile = INPROJ_ROW_TILE
    row_spec = lambda w_: pl.BlockSpec((tile, w_), lambda i: (i, 0))
    return pl.pallas_call(
        _inproj_body,
        grid=(rows // tile,),
        in_specs=[row_spec(d), _pspec(g), _pspec(w, single=True)],
        out_specs=[row_spec(n) for n in widths],
        out_shape=[jax.ShapeDtypeStruct((rows, n), F32) for n in widths],
        compiler_params=_cparams("parallel"),
        name="inproj",
    )(x, _parg(g), _parg(w))


def _ssd_body(z_ref, xbc_ref, dt_ref, cw_ref, cb_ref, dtb_ref, alog_ref, dsk_ref, ng_ref,
              y_ref, conv_ref, hout_ref, xpad_scr, h_scr, *, chunk, group):
    L, G = chunk, group
    GL = G * L
    c = pl.program_id(1)
    pad = SUBLANES
    halo = SSD_CONV - 1
    hpg = SSD_HEADS // SSD_GROUPS
    assert hpg == 2 and hpg * SSD_HEAD_DIM == SSD_STATE

    @pl.when(c == 0)
    def _():
        xpad_scr[0:pad, :] = jnp.zeros((pad, SSD_CONV_DIM), F32)
        h_scr[...] = jnp.zeros(h_scr.shape, F32)

    xpad_scr[pad:pad + GL, :] = xbc_ref[...]
    xfull = xpad_scr[...]
    conv = cb_ref[...] + cw_ref[halo:halo + 1, :] * xfull[pad:pad + GL]
    for j in range(halo):
        conv = conv + cw_ref[j:j + 1, :] * pltpu.roll(xfull, halo - j, axis=0)[pad:pad + GL]
    xpad_scr[pad - halo:pad, :] = xpad_scr[pad + GL - halo:pad + GL, :]
    conv = _silu(conv)
    xs = conv[:, 0:GROUP_WIDTH]
    bm = conv[:, GROUP_WIDTH:2 * GROUP_WIDTH].astype(BF16)
    cm = conv[:, 2 * GROUP_WIDTH:3 * GROUP_WIDTH].astype(BF16)

    row = lax.broadcasted_iota(jnp.int32, (L, L), 0)
    col = lax.broadcasted_iota(jnp.int32, (L, L), 1)
    causal = row >= col
    tril = jnp.where(causal, 1.0, 0.0).astype(F32)
    dt = _softplus(dt_ref[...] + dtb_ref[...])
    da = dt * (-jnp.exp(alog_ref[...]) * LOG2_E)
    acs = [_dot(tril, da[i * L:(i + 1) * L, :], precision=HIGHEST) for i in range(G)]
    acs_t = [a.T for a in acs]
    e_acs = [jnp.exp2(a) for a in acs]
    e_end = [jnp.exp2(a[L - 1:L, :] - a) for a in acs]
    e_last = [jnp.exp2(a[L - 1:L, :]) for a in acs]

    keys = [(i, g) for i in range(G) for g in range(SSD_GROUPS)]
    rows_of = lambda x, i: x[i * L:(i + 1) * L]
    lanes_of = lambda x, g: x[:, g * SSD_STATE:(g + 1) * SSD_STATE]
    lane_lo = lax.broadcasted_iota(jnp.int32, (L, hpg * SSD_HEAD_DIM), 1) < SSD_HEAD_DIM
    row_lo = lax.broadcasted_iota(jnp.int32, (hpg * SSD_HEAD_DIM, SSD_STATE), 0) < SSD_HEAD_DIM
    head_cols = lambda a, g: jnp.where(lane_lo, a[:, g * hpg:g * hpg + 1], a[:, g * hpg + 1:g * hpg + 2])
    bg = {(i, g): lanes_of(rows_of(bm, i), g) for i, g in keys}
    cg = {(i, g): lanes_of(rows_of(cm, i), g) for i, g in keys}
    scores = {k: _dot_nt(cg[k], bg[k]) for k in keys}
    xdt = {(i, g): lanes_of(rows_of(xs, i), g) * head_cols(rows_of(dt, i), g) for i, g in keys}
    decay = {(i, h): jnp.exp2(jnp.where(causal, acs[i][:, h:h + 1] - acs_t[i][h:h + 1, :], -jnp.inf))
             for i in range(G) for h in range(SSD_HEADS)}
    p_mat = {(i, g): jnp.concatenate([(scores[(i, g)] * decay[(i, g * hpg + k)]).astype(BF16) for k in range(hpg)],
                                     axis=1) for i, g in keys}
    y_in = {k: _dot(p_mat[k], _bd(xdt[k].astype(BF16))) for k in keys}
    st = {(i, g): _dot_tn((xdt[(i, g)] * head_cols(e_end[i], g)).astype(BF16), bg[(i, g)]) for i, g in keys}

    y_rows = []
    for i in range(G):
        ys = []
        for g in range(SSD_GROUPS):
            h_prev = h_scr[g * hpg:(g + 1) * hpg].reshape(hpg * SSD_HEAD_DIM, SSD_STATE)
            ys.append(y_in[(i, g)] + _dot_nt(cg[(i, g)], h_prev.astype(BF16)) * head_cols(e_acs[i], g))
            keep = jnp.where(row_lo, e_last[i][:, g * hpg:g * hpg + 1], e_last[i][:, g * hpg + 1:g * hpg + 2])
            h_scr[g * hpg:(g + 1) * hpg] = (h_prev * keep + st[(i, g)]).reshape(hpg, SSD_HEAD_DIM, SSD_STATE)
        y_rows.append(jnp.concatenate(ys, axis=-1))
    y = jnp.concatenate(y_rows, axis=0) + xs * dsk_ref[...]
    y = y * _silu(z_ref[...])
    y_ref[...] = _rms(y, ng_ref[...])

    @pl.when(c == pl.num_programs(1) - 1)
    def _():
        hout_ref[0] = h_scr[...]
        conv_ref[0] = xpad_scr[pad - halo:pad, :]


def _ssd(z, xbc, dtr, lp, *, batch, seq):
    chunk = SSD_CHUNK
    rows = chunk * SSD_GROUP
    nc = seq // rows
    rspec = lambda w: pl.BlockSpec((rows, w), lambda b, c: (b * nc + c, 0))
    consts = (lp["conv_w"], lp["conv_b"], lp["dt_bias"], lp["a_log"], lp["d_skip"], lp["ssd_norm"])
    return pl.pallas_call(
        functools.partial(_ssd_body, chunk=chunk, group=SSD_GROUP),
        grid=(batch, nc),
        in_specs=[rspec(GROUP_WIDTH), rspec(SSD_CONV_DIM), rspec(LANES)] + [_pspec(a) for a in consts],
        out_specs=[rspec(GROUP_WIDTH),
                   pl.BlockSpec((1, SSD_CONV - 1, SSD_CONV_DIM), lambda b, c: (b, 0, 0)),
                   pl.BlockSpec((1, SSD_HEADS, SSD_HEAD_DIM, SSD_STATE), lambda b, c: (b, 0, 0, 0))],
        out_shape=[jax.ShapeDtypeStruct((batch * seq, GROUP_WIDTH), F32),
                   jax.ShapeDtypeStruct((batch, SSD_CONV - 1, SSD_CONV_DIM), F32),
                   jax.ShapeDtypeStruct((batch, SSD_HEADS, SSD_HEAD_DIM, SSD_STATE), F32)],
        scratch_shapes=[pltpu.VMEM((SUBLANES + rows, SSD_CONV_DIM), F32),
                        pltpu.VMEM((SSD_HEADS, SSD_HEAD_DIM, SSD_STATE), F32)],
        compiler_params=_cparams("parallel", "arbitrary"),
        name="ssd",
    )(z, xbc, dtr, *[_parg(a) for a in consts])


def _ssd_step_body(z_ref, xbc_ref, dt_ref, conv0_ref, h0_ref, *rest, seq, batch, layer):
    hdone_ref, rest = (rest[0], rest[1:]) if layer else (None, rest)
    (cw_ref, cb_ref, dtb_ref, aneg_ref, dsk_ref, ng_ref, hexp_ref, y_ref, conv_ref, hout_ref,
     xs_scr, bm_scr, cm_scr, xdt_scr, dec_scr, y_scr) = rest
    T, B = seq, batch
    if layer:
        hout_ref[0:layer] = hdone_ref[...]
    GW = GROUP_WIDTH
    j = pl.program_id(0)
    tiles = SSD_STEP_TILES

    @pl.when(j == 0)
    def _():
        rows = [conv0_ref[i] for i in range(SSD_CONV - 1)]
        rows += [xbc_ref[t * B:(t + 1) * B, :] for t in range(T)]
        for t in range(T):
            conv = cb_ref[...] + cw_ref[0:1, :] * rows[t]
            for i in range(1, SSD_CONV):
                conv = conv + cw_ref[i:i + 1, :] * rows[t + i]
            conv = _silu(conv)
            xs = conv[:, 0:GW]
            xs_scr[t] = xs
            for g in range(SSD_GROUPS):
                bm_scr[t, g] = conv[:, GW + g * SSD_STATE:GW + (g + 1) * SSD_STATE].T
                cm_scr[t, g] = conv[:, 2 * GW + g * SSD_STATE:2 * GW + (g + 1) * SSD_STATE].T
            dt = _softplus(dt_ref[t * B:(t + 1) * B, :] + dtb_ref[...])
            dte = _dot(dt, hexp_ref[...], precision=HIGHEST)
            xdt_scr[t] = (xs * dte).T
            dec_scr[t] = jnp.exp(dte * aneg_ref[...]).T
        for i in range(SSD_CONV - 1):
            conv_ref[i] = rows[T + i]

    hp0 = j * tiles
    grp = hp0 // (SSD_HEAD_DIM * (SSD_HEADS // SSD_GROUPS))
    for q in range(tiles):
        hp = pl.ds(hp0 + q, 1)
        h = h0_ref[:, q, :].T
        for t in range(T):
            h = h * dec_scr[t, hp, :] + bm_scr[t, grp] * xdt_scr[t, hp, :]
            y_scr[t, hp, :] = jnp.sum(h * cm_scr[t, grp], axis=0, keepdims=True)
        hout_ref[layer, :, q, :] = h.T

    @pl.when(j == pl.num_programs(0) - 1)
    def _():
        for t in range(T):
            y = y_scr[t].T + xs_scr[t] * dsk_ref[...]
            y = y * _silu(z_ref[t * B:(t + 1) * B, :])
            y_ref[t * B:(t + 1) * B, :] = _rms(y, ng_ref[...])


def _layer_state_specs(layer, block, axis):
    idx = lambda first: (lambda j: (first,) + tuple(j if a == axis else 0 for a in range(len(block))))
    cur = pl.BlockSpec((None,) + block, idx(layer))
    prev = [pl.BlockSpec((layer,) + block, idx(0))] if layer else []
    out = pl.BlockSpec((layer + 1,) + block, idx(0))
    return cur, prev, out


def _ssd_step(z, xbc, dtr, conv_all, h_all, h_done, lp, *, batch, seq, layer):
    n = batch * seq
    srows = SSD_HEADS * SSD_HEAD_DIM
    consts = (lp["conv_w"], lp["conv_b"], lp["dt_bias"], lp["a_neg_exp"], lp["d_skip"], lp["ssd_norm"], lp["head_expand"])
    hspec, prev_specs, hout_spec = _layer_state_specs(layer, (batch, SSD_STEP_TILES, SSD_STATE), 1)
    prev_args = [h_done] if layer else []
    cshape = (SSD_CONV - 1, batch, SSD_CONV_DIM)
    return pl.pallas_call(
        functools.partial(_ssd_step_body, seq=seq, batch=batch, layer=layer),
        grid=(srows // SSD_STEP_TILES,),
        in_specs=[_full_spec((n, GROUP_WIDTH)), _full_spec((n, SSD_CONV_DIM)), _full_spec((n, LANES)),
                  pl.BlockSpec((None,) + cshape, lambda j: (layer, 0, 0, 0)), hspec] + prev_specs
                 + [_pspec(a) for a in consts],
        out_specs=[_full_spec((n, GROUP_WIDTH)), _full_spec(cshape), hout_spec],
        out_shape=[jax.ShapeDtypeStruct((n, GROUP_WIDTH), F32),
                   jax.ShapeDtypeStruct(cshape, F32),
                   jax.ShapeDtypeStruct((layer + 1, batch, srows, SSD_STATE), F32)],
        scratch_shapes=[pltpu.VMEM((seq, batch, GROUP_WIDTH), F32),
                        pltpu.VMEM((seq, SSD_GROUPS, SSD_STATE, batch), F32),
                        pltpu.VMEM((seq, SSD_GROUPS, SSD_STATE, batch), F32),
                        pltpu.VMEM((seq, GROUP_WIDTH, batch), F32),
                        pltpu.VMEM((seq, GROUP_WIDTH, batch), F32),
                        pltpu.VMEM((seq, GROUP_WIDTH, batch), F32)],
        compiler_params=_cparams("arbitrary"),
        name="ssd_step",
    )(z, xbc, dtr, conv_all, h_all.reshape(h_all.shape[0], batch, srows, SSD_STATE),
      *prev_args, *[_parg(a) for a in consts])


PAIR = 2 * RWKV_HEAD
RWKV_PAIRS = RWKV_HEADS // 2


def _bd(x):
    half = x.shape[1] // 2
    lane = lax.broadcasted_iota(jnp.int32, x.shape, 1)
    zero = jnp.zeros_like(x)
    return jnp.concatenate([jnp.where(lane < half, x, zero), jnp.where(lane >= half, x, zero)], axis=0)


def _half_sums(x, lo):
    s_lo = jnp.sum(jnp.where(lo, x, 0.0), axis=-1, keepdims=True)
    s_hi = jnp.sum(jnp.where(lo, 0.0, x), axis=-1, keepdims=True)
    return jnp.where(lo, s_lo, s_hi)


def _head_sum(x):
    lo = lax.broadcasted_iota(jnp.int32, (x.shape[0], PAIR), 1) < RWKV_HEAD
    return jnp.concatenate([_half_sums(x[:, p * PAIR:(p + 1) * PAIR], lo) for p in range(RWKV_PAIRS)], axis=-1)


def _rwkv_pointwise(u, prev, mu_ref, w0_ref, w2_ref, a0_ref, a2_ref, g2_ref, kk_ref, ka_ref):
    GW = GROUP_WIDTH
    xs = u + (prev - u) * mu_ref[...]
    r = xs[:, 0:GW]
    k = xs[:, GW:2 * GW]
    v = xs[:, 2 * GW:3 * GW]
    wd = xs[:, 3 * GW:3 * GW + 64]
    ad = xs[:, 3 * GW + 64:3 * GW + 128]
    gd = xs[:, 3 * GW + 128:3 * GW + 256]
    w_lin = w0_ref[...] + _dot(jnp.tanh(wd).astype(BF16), w2_ref[...])
    logdecay = -math.exp(-0.5) * _sigmoid(w_lin)
    a = _sigmoid(a0_ref[...] + _dot(ad.astype(BF16), a2_ref[...]))
    g = _dot(_sigmoid(gd).astype(BF16), g2_ref[...])
    kk = k * kk_ref[...]
    kk = kk * lax.rsqrt(jnp.maximum(_head_sum(kk * kk), 1e-24))
    k = k * (1.0 + (a - 1.0) * ka_ref[...])
    return r, k, v, logdecay, a, g, kk


def _rwkv_finish(y, r, k, v, g, rk_ref, lng_ref, lnb_ref):
    mean = _head_sum(y) * (1.0 / RWKV_HEAD)
    yc = y - mean
    var = _head_sum(yc * yc) * (1.0 / RWKV_HEAD)
    y = yc * lax.rsqrt(var + RWKV_LN_EPS) * lng_ref[...] + lnb_ref[...]
    bonus = _head_sum(r * k * rk_ref[...]) * v
    return (y + bonus) * g


def _rwkv_body(u_ref, mu_ref, w0_ref, w2_ref, a0_ref, a2_ref, g2_ref, kk_ref, ka_ref, rk_ref,
               lng_ref, lnb_ref, y_ref, shift_ref, sout_ref, upad_scr, s_scr, *, chunk, group):
    L, G = chunk, group
    GL = G * L
    c = pl.program_id(1)
    pad = SUBLANES

    @pl.when(c == 0)
    def _():
        upad_scr[0:pad, :] = jnp.zeros((pad, RWKV_PROJ), F32)
        s_scr[...] = jnp.zeros(s_scr.shape, F32)

    u = u_ref[...]
    upad_scr[pad:pad + GL, :] = u
    prev = pltpu.roll(upad_scr[...], 1, axis=0)[pad:pad + GL]
    upad_scr[pad - 1:pad, :] = u[GL - 1:GL, :]
    r, k, v, logdecay, a, g, kk = _rwkv_pointwise(u, prev, mu_ref, w0_ref, w2_ref, a0_ref, a2_ref, g2_ref,
                                                  kk_ref, ka_ref)

    tril = jnp.where(lax.broadcasted_iota(jnp.int32, (L, L), 0) >= lax.broadcasted_iota(jnp.int32, (L, L), 1),
                     1.0, 0.0).astype(F32)
    cl = jnp.concatenate([_dot(tril, logdecay[i * L:(i + 1) * L, :], precision=HIGHEST) for i in range(G)], axis=0)
    e_in = jnp.exp(cl)
    e_inv = jnp.exp(-cl)
    r_t = r * e_in
    r_tb = r_t.astype(BF16)
    a_tb = (-kk * jnp.exp(cl - logdecay)).astype(BF16)
    b_tb = (kk * a * e_inv).astype(BF16)
    k_tb = (k * e_inv).astype(BF16)
    vb = v.astype(BF16)

    row = lax.broadcasted_iota(jnp.int32, (L, PAIR), 0)
    colh = lax.broadcasted_iota(jnp.int32, (L, PAIR), 1) & (RWKV_HEAD - 1)
    strict = row > colh
    incl = row >= colh
    eye_pair = jnp.where(row == colh, 1.0, 0.0).astype(F32)
    lane_lo = lax.broadcasted_iota(jnp.int32, (RWKV_HEAD, PAIR), 1) < RWKV_HEAD
    same_head = (lax.broadcasted_iota(jnp.int32, (PAIR, PAIR), 0) < RWKV_HEAD) == \
                (lax.broadcasted_iota(jnp.int32, (PAIR, PAIR), 1) < RWKV_HEAD)

    streams = [(i, p) for i in range(G) for p in range(RWKV_PAIRS)]
    ns = len(streams)
    blk = lambda x, i, p: x[i * L:(i + 1) * L, p * PAIR:(p + 1) * PAIR]
    lhs = [jnp.concatenate([blk(a_tb, i, p), blk(r_tb, i, p)], axis=0) for i, p in streams]
    m_both = [_dot_nt(lhs[s], jnp.concatenate([_bd(blk(b_tb, i, p)), _bd(blk(k_tb, i, p))], axis=0))
              for s, (i, p) in enumerate(streams)]
    m_ab = [m[:, 0:PAIR] for m in m_both]
    m_ak = [m[:, PAIR:2 * PAIR] for m in m_both]
    n_ab = [jnp.where(strict, m[0:L], 0.0) for m in m_ab]
    m_rb = [jnp.where(incl, m[L:2 * L], 0.0).astype(BF16) for m in m_ab]
    n_ak = [jnp.where(strict, m[0:L], 0.0).astype(BF16) for m in m_ak]
    m_rk = [jnp.where(incl, m[L:2 * L], 0.0).astype(BF16) for m in m_ak]
    tinv = [eye_pair + n for n in n_ab]
    pwb = [n.astype(BF16) for n in n_ab]
    pw = [_dot(x, _bd(x)) for x in pwb]
    for _ in range(int(math.log2(L)) - 2):
        pwb = [x.astype(BF16) for x in pw]
        both = [_dot(jnp.concatenate([pwb[s], tinv[s].astype(BF16)], axis=0), _bd(pwb[s])) for s in range(ns)]
        pw = [x[0:L] for x in both]
        tinv = [tinv[s] + both[s][L:2 * L] for s in range(ns)]
    pwb = [x.astype(BF16) for x in pw]
    tinv = [tinv[s] + _dot(tinv[s].astype(BF16), _bd(pwb[s])) for s in range(ns)]
    tinvb = [x.astype(BF16) for x in tinv]
    nv_mv = [_dot(jnp.concatenate([n_ak[s], m_rk[s]], axis=0), _bd(blk(vb, i, p))) for s, (i, p) in enumerate(streams)]
    wu = [_dot(tinvb[s], jnp.concatenate([_bd(blk(a_tb, i, p)), _bd(nv_mv[s][0:L].astype(BF16))], axis=1))
          for s, (i, p) in enumerate(streams)]
    wub = [x.astype(BF16) for x in wu]
    qy = [_dot(m_rb[s], jnp.concatenate([_bd(wub[s][:, 0:PAIR]), _bd(wub[s][:, PAIR:2 * PAIR])], axis=1))
          for s in range(ns)]
    q = [(blk(r_t, i, p) + qy[s][:, 0:PAIR]).astype(BF16) for s, (i, p) in enumerate(streams)]
    y_loc = [qy[s][:, PAIR:2 * PAIR] + nv_mv[s][L:2 * L] for s in range(ns)]
    zeros_b = jnp.zeros((L, PAIR), BF16)
    mg = [_dot_tn(jnp.concatenate([wub[s], jnp.concatenate([zeros_b, blk(vb, i, p)], axis=1)], axis=0),
                  jnp.concatenate([blk(b_tb, i, p), blk(k_tb, i, p)], axis=0))
          for s, (i, p) in enumerate(streams)]
    p_end = [e_in[(i + 1) * L - 1:(i + 1) * L, p * PAIR:(p + 1) * PAIR] for i, p in streams]
    m_t = [(jnp.where(same_head, mg[s][0:PAIR], 0.0) * p_end[s]).astype(BF16) for s in range(ns)]
    g_t = [jnp.where(lane_lo, mg[s][PAIR:PAIR + RWKV_HEAD], mg[s][PAIR + RWKV_HEAD:2 * PAIR]) * p_end[s]
           for s in range(ns)]

    y_rows = []
    for i in range(G):
        y_pairs = []
        for p in range(RWKV_PAIRS):
            s = i * RWKV_PAIRS + p
            s0 = s_scr[p]
            s0b = s0.astype(BF16)
            y_pairs.append(_dot_nt(q[s], _bd(s0b)) + y_loc[s])
            s_scr[p] = s0 * p_end[s] + _dot(s0b, m_t[s]) + g_t[s]
        y_rows.append(jnp.concatenate(y_pairs, axis=-1))
    y = jnp.concatenate(y_rows, axis=0)
    y_ref[...] = _rwkv_finish(y, r, k, v, g, rk_ref, lng_ref, lnb_ref)

    @pl.when(c == pl.num_programs(1) - 1)
    def _():
        sout_ref[0] = s_scr[...]
        shift_ref[0] = upad_scr[pad - 1:pad, :]


_RWKV_PARAM_NAMES = ("mu", "w0", "w2", "a0", "a2", "g2", "k_k", "k_a", "r_k", "ln_g", "ln_b")


def _rwkv(u, p, *, batch, seq):
    rows = RWKV_CHUNK * RWKV_GROUP
    nc = seq // rows
    params = [p[n] for n in _RWKV_PARAM_NAMES]
    sspec = pl.BlockSpec((1, RWKV_PAIRS, RWKV_HEAD, PAIR), lambda b, c: (b, 0, 0, 0))
    y, shift, s_last = pl.pallas_call(
        functools.partial(_rwkv_body, chunk=RWKV_CHUNK, group=RWKV_GROUP),
        grid=(batch, nc),
        in_specs=[pl.BlockSpec((rows, RWKV_PROJ), lambda b, c: (b * nc + c, 0))] + [_pspec(a) for a in params],
        out_specs=[pl.BlockSpec((rows, GROUP_WIDTH), lambda b, c: (b * nc + c, 0)),
                   pl.BlockSpec((1, 1, RWKV_PROJ), lambda b, c: (b, 0, 0)), sspec],
        out_shape=[jax.ShapeDtypeStruct((batch * seq, GROUP_WIDTH), F32),
                   jax.ShapeDtypeStruct((batch, 1, RWKV_PROJ), F32),
                   jax.ShapeDtypeStruct((batch, RWKV_PAIRS, RWKV_HEAD, PAIR), F32)],
        scratch_shapes=[pltpu.VMEM((SUBLANES + rows, RWKV_PROJ), F32),
                        pltpu.VMEM((RWKV_PAIRS, RWKV_HEAD, PAIR), F32)],
        compiler_params=_cparams("parallel", "arbitrary"),
        name="rwkv",
    )(u, *[_parg(a) for a in params])
    s_last = s_last.reshape(batch, RWKV_PAIRS, RWKV_HEAD, 2, RWKV_HEAD).transpose(0, 1, 3, 2, 4).reshape(
        batch, RWKV_HEADS, RWKV_HEAD, RWKV_HEAD)
    return y, shift.reshape(batch, RWKV_PROJ), s_last


def _rwkv_step_body(u_ref, shift0_ref, s0_ref, *rest, seq, batch, layer):
    sdone_ref, rest = (rest[0], rest[1:]) if layer else (None, rest)
    (mu_ref, w0_ref, w2_ref, a0_ref, a2_ref, g2_ref, kk_ref, ka_ref, rk_ref, lng_ref, lnb_ref, y_ref, sout_ref,
     r_scr, w_scr, k_scr, b_scr, nkk_scr, v_scr, y_scr) = rest
    T, B = seq, batch
    j = pl.program_id(0)
    if layer:
        sout_ref[0:layer] = sdone_ref[...]
    tiles = RWKV_STEP_TILES

    def pointwise(t):
        u = u_ref[t * B:(t + 1) * B, :]
        prev = shift0_ref[...] if t == 0 else u_ref[(t - 1) * B:t * B, :]
        return _rwkv_pointwise(u, prev, mu_ref, w0_ref, w2_ref, a0_ref, a2_ref, g2_ref, kk_ref, ka_ref)

    @pl.when(j == 0)
    def _():
        for t in range(T):
            r, k, v, logdecay, a, _, kk = pointwise(t)
            r_scr[t] = r.T
            w_scr[t] = jnp.exp(logdecay).T
            k_scr[t] = k.T
            b_scr[t] = (kk * a).T
            nkk_scr[t] = (-kk).T
            v_scr[t] = v.T

    i0 = j * tiles
    keys = pl.ds(pl.multiple_of((i0 // RWKV_HEAD) * RWKV_HEAD, RWKV_HEAD), RWKV_HEAD)
    for q in range(tiles):
        vi = pl.ds(i0 + q, 1)
        s = s0_ref[q]
        for t in range(T):
            sa = jnp.sum(s * nkk_scr[t, keys, :], axis=0, keepdims=True)
            s = s * w_scr[t, keys, :] + k_scr[t, keys, :] * v_scr[t, vi, :] + b_scr[t, keys, :] * sa
            y_scr[t, vi, :] = jnp.sum(s * r_scr[t, keys, :], axis=0, keepdims=True)
        sout_ref[layer, q] = s

    @pl.when(j == pl.num_programs(0) - 1)
    def _():
        for t in range(T):
            r, k, v, _, _, g, _ = pointwise(t)
            y_ref[t * B:(t + 1) * B, :] = _rwkv_finish(y_scr[t].T, r, k, v, g, rk_ref, lng_ref, lnb_ref)


def _rwkv_step(u, shift0, s_all, s_done, p, *, batch, seq, layer):
    n = batch * seq
    srows = RWKV_HEADS * RWKV_HEAD
    params = [p[nm] for nm in _RWKV_PARAM_NAMES]
    sspec, prev_specs, sout_spec = _layer_state_specs(layer, (RWKV_STEP_TILES, RWKV_HEAD, batch), 0)
    prev_args = [s_done] if layer else []
    tposed = pltpu.VMEM((seq, GROUP_WIDTH, batch), F32)
    return pl.pallas_call(
        functools.partial(_rwkv_step_body, seq=seq, batch=batch, layer=layer),
        grid=(srows // RWKV_STEP_TILES,),
        in_specs=[_full_spec((n, RWKV_PROJ)), _full_spec((batch, RWKV_PROJ)), sspec] + prev_specs
                 + [_pspec(a) for a in params],
        out_specs=[_full_spec((n, GROUP_WIDTH)), sout_spec],
        out_shape=[jax.ShapeDtypeStruct((n, GROUP_WIDTH), F32),
                   jax.ShapeDtypeStruct((layer + 1, srows, RWKV_HEAD, batch), F32)],
        scratch_shapes=[tposed] * 7,
        compiler_params=_cparams("arbitrary"),
        name="rwkv_step",
    )(u, shift0, s_all, *prev_args, *[_parg(a) for a in params])


def _s5_body(u_ref, hre0_ref, him0_ref, are_ref, aim_ref, bmat_ref, cmat_ref, d_ref, gw_ref, gb_ref,
             y_ref, hre_ref, him_ref, hs_scr, tm_scr, *, steps, batch_major):
    c = pl.program_id(1)
    ns = S5_WIDTH
    bsub = hre_ref.shape[0]

    @pl.when(c == 0)
    def _():
        hre_ref[...] = hre0_ref[...]
        him_ref[...] = him0_ref[...]

    if batch_major:
        for b in range(bsub):
            tm_scr[:, b, :] = u_ref[b]
        u = tm_scr[...].reshape(steps * bsub, GROUP_WIDTH)
    else:
        u = u_ref[...].reshape(steps * bsub, GROUP_WIDTH)
    are = jnp.broadcast_to(are_ref[...], (bsub, ns))
    aim = jnp.broadcast_to(aim_ref[...], (bsub, ns))
    hre, him = hre_ref[...], him_ref[...]
    sub = min(S5_SUB, steps)
    rows = sub * bsub
    outs = []
    hs_scr[...] = _dot(u.astype(BF16), bmat_ref[...])
    for k in range(steps // sub):
        r0 = k * rows
        u_k = u[r0:r0 + rows]
        for t in range(sub):
            rs = slice(r0 + t * bsub, r0 + (t + 1) * bsub)
            hre, him = (are * hre - aim * him + hs_scr[rs, 0:ns], are * him + aim * hre + hs_scr[rs, ns:2 * ns])
            hs_scr[rs, 0:ns] = hre
            hs_scr[rs, ns:2 * ns] = him
        y = _dot(hs_scr[r0:r0 + rows, :].astype(BF16), cmat_ref[...]) + u_k * d_ref[...]
        y = _gelu_tanh(y)
        yy = _dot(y.astype(BF16), gw_ref[...]) + gb_ref[...]
        outs.append(yy[:, 0:GROUP_WIDTH] * _sigmoid(yy[:, GROUP_WIDTH:2 * GROUP_WIDTH]))
    hre_ref[...] = hre
    him_ref[...] = him
    out = jnp.concatenate(outs, axis=0).reshape(steps, bsub, GROUP_WIDTH)
    if batch_major:
        tm_scr[...] = out
        for b in range(bsub):
            y_ref[b] = tm_scr[:, b, :]
    else:
        y_ref[...] = out


def _time_specs(u, batch_major, chunk):
    if batch_major:
        batch, seq, _ = u.shape
        steps = min(chunk, seq)
        bsub = SUBLANES
        spec = pl.BlockSpec((bsub, steps, GROUP_WIDTH), lambda b, c: (b, c, 0))
    else:
        seq, batch, _ = u.shape
        steps = min(chunk, seq)
        bsub = min(batch, SUBLANES * max(1, chunk // steps))
        spec = pl.BlockSpec((steps, bsub, GROUP_WIDTH), lambda b, c: (c, b, 0))
    return batch, seq, steps, bsub, spec


def _s5(u, hre0, him0, lp, *, batch_major):
    batch, seq, steps, bsub, tspec = _time_specs(u, batch_major, TM_CHUNK)
    hspec =pl.BlockSpec((bsub, S5_WIDTH), lambda b, c: (b, 0))
    consts = (lp["s5_are"], lp["s5_aim"], lp["s5_bmat"], lp["s5_cmat"], lp["s5_d"], lp["s5_gw"], lp["s5_gb"])
    return pl.pallas_call(
        functools.partial(_s5_body, steps=steps, batch_major=batch_major),
        grid=(batch // bsub, seq // steps),
        in_specs=[tspec, hspec, hspec] + [_pspec(a) for a in consts],
        out_specs=[tspec, hspec, hspec],
        out_shape=[jax.ShapeDtypeStruct(u.shape, F32),
                   jax.ShapeDtypeStruct((batch, S5_WIDTH), F32),
                   jax.ShapeDtypeStruct((batch, S5_WIDTH), F32)],
        scratch_shapes=[pltpu.VMEM((steps * bsub, 2 * S5_WIDTH), F32),
                        pltpu.VMEM((steps, bsub, GROUP_WIDTH), F32)],
        compiler_params=_cparams("parallel", "arbitrary"),
        name="s5",
    )(u, hre0, him0, *[_parg(a) for a in consts])


def _pool_body(u_ref, buf0_ref, pw_ref, sc_ref, y_ref, buf_ref, f_scr, tm_scr, *, steps, pos0, batch_major):
    c = pl.program_id(1)
    bsub = f_scr.shape[1]
    GW = GROUP_WIDTH
    halo = POOL_BUF + 1

    @pl.when(c == 0)
    def _():
        f_scr[0] = jnp.zeros((bsub, GW), F32)
        f_scr[1:halo] = buf0_ref[...]

    if batch_major:
        for b in range(bsub):
            f_scr[halo:halo + steps, b, :] = u_ref[b]
    else:
        f_scr[halo:halo + steps] = u_ref[...]
    f = f_scr[...]
    u = f[halo:halo + steps]
    s2 = f[1:] + f[:-1]
    s4 = s2[2:] + s2[:-2]
    s8 = s4[4:] + s4[:-4]
    s16 = s8[8:] + s8[:-8]
    f_scr[0:halo] = f[steps:steps + halo]
    lane = lax.broadcasted_iota(jnp.int32, (steps, bsub, GW), 2)
    tpos = lax.broadcasted_iota(jnp.int32, (steps, bsub, GW), 0) + (pos0 + 1) + c * steps
    win = jnp.where(lane < POOL_CH, s2[halo - 1:halo - 1 + steps],
                    jnp.where(lane < 2 * POOL_CH, s4[halo - 3:halo - 3 + steps],
                              jnp.where(lane < 3 * POOL_CH, s8[halo - 7:halo - 7 + steps],
                                        s16[halo - 15:halo - 15 + steps])))
    wlen = jnp.where(lane < POOL_CH, POOL_WINDOWS[0],
                     jnp.where(lane < 2 * POOL_CH, POOL_WINDOWS[1],
                               jnp.where(lane < 3 * POOL_CH, POOL_WINDOWS[2], POOL_WINDOWS[3])))
    cnt = jnp.minimum(tpos, wlen).astype(F32)
    pooled = (win / cnt - u).reshape(steps * bsub, GW)
    y = (_dot(pooled.astype(BF16), pw_ref[...]) * sc_ref[...]).reshape(steps, bsub, GW)
    if batch_major:
        tm_scr[...] = y
        for b in range(bsub):
            y_ref[b] = tm_scr[:, b, :]
    else:
        y_ref[...] = y

    @pl.when(c == pl.num_programs(1) - 1)
    def _():
        buf_ref[...] = f_scr[1:halo]


def _pool(u, buf0, lp, *, pos0, batch_major, layer=None):
    batch, seq, steps, bsub, tspec = _time_specs(u, batch_major, POOL_CHUNK)
    bblock =(POOL_BUF, bsub, GROUP_WIDTH)
    bspec = pl.BlockSpec(bblock, lambda b, c: (0, b, 0))
    if layer is None:
        bspec_in = bspec
    else:
        bspec_in = pl.BlockSpec((None,) + bblock, lambda b, c: (layer, 0, b, 0))
    return pl.pallas_call(
        functools.partial(_pool_body, steps=steps, pos0=pos0, batch_major=batch_major),
        grid=(batch // bsub, seq // steps),
        in_specs=[tspec, bspec_in, _pspec(lp["pool_w"]), _pspec(lp["pool_scale"])],
        out_specs=[tspec, bspec],
        out_shape=[jax.ShapeDtypeStruct(u.shape, F32), jax.ShapeDtypeStruct((POOL_BUF, batch, GROUP_WIDTH), F32)],
        scratch_shapes=[pltpu.VMEM((POOL_BUF + 1 + steps, bsub, GROUP_WIDTH), F32),
                        pltpu.VMEM((steps, bsub, GROUP_WIDTH), F32)],
        compiler_params=_cparams("parallel", "arbitrary"),
        name="pool",
    )(u, buf0, _parg(lp["pool_w"]), _parg(lp["pool_scale"]))


def _block_diag(blocks):
    n, g, r, c = blocks.shape
    eye = jnp.eye(g, dtype=blocks.dtype)
    return (eye[None, :, None, :, None] * blocks[:, :, :, None, :]).reshape(n, g * r, g * c)


def _stacked_params(P):
    row = lambda a: a.reshape(a.shape[0], 1, -1)
    pad_lanes = lambda a: jnp.pad(a, ((0, 0), (0, LANES - a.shape[1])))
    bf = lambda a: a.astype(BF16)

    lam = lax.complex(P["s5_lam_re"], P["s5_lam_im"])
    a_bar = jnp.exp(lam * jnp.exp(P["s5_log_step"])[..., None])
    b_bar = ((a_bar - 1.0) / lam)[..., None] * lax.complex(P["s5_b_re"], P["s5_b_im"])
    b_t = jnp.swapaxes(b_bar, 2, 3)
    bmat = jnp.concatenate([_block_diag(jnp.real(b_t)), _block_diag(jnp.imag(b_t))], axis=2)
    c_t = jnp.swapaxes(lax.complex(P["s5_c_re"], P["s5_c_im"]), 2, 3)
    cmat = jnp.concatenate([_block_diag(jnp.real(c_t)), -_block_diag(jnp.imag(c_t))], axis=1)

    out = dict(
        norm_ffn1=row(P["norm_ffn1"]), ffn1_in=P["ffn1_in"], ffn1_out=P["ffn1_out"],
        norm_mix=row(P["norm_mix"]),
        w_in=jnp.transpose(P["w_in"], (2, 0, 1)),
        conv_w=P["ssd_conv_w"], conv_b=row(P["ssd_conv_b"]),
        dt_bias=row(pad_lanes(P["ssd_dt_bias"])), a_log=row(pad_lanes(P["ssd_a_log"])),
        a_neg_exp=row(jnp.repeat(-jnp.exp(P["ssd_a_log"]), SSD_HEAD_DIM, axis=1)),
        d_skip=row(jnp.repeat(P["ssd_d"], SSD_HEAD_DIM, axis=1)), ssd_norm=row(P["ssd_norm"]),
        s5_are=row(jnp.real(a_bar)), s5_aim=row(jnp.imag(a_bar)), s5_bmat=bf(bmat), s5_cmat=bf(cmat),
        s5_d=row(P["s5_d"]), s5_gw=bf(P["s5_glu_w"]), s5_gb=row(P["s5_glu_b"]),
        pool_w=bf(_block_diag(P["pool_w"])), pool_scale=row(P["pool_scale"]),
        w_out=bf(P["w_out"]),
        norm_ffn2=row(P["norm_ffn2"]), ffn2_in=P["ffn2_in"], ffn2_out=P["ffn2_out"],
    )
    for name in _RWKV_PARAM_NAMES:
        a = P["rwkv_" + name]
        out["rwkv_" + name] = bf(a) if name in ("w2", "a2", "g2") else row(a)
    return out


def _layer_params(stacked, l):
    lp = {k: _Layered((v, l)) for k, v in stacked.items()}
    lp["rwkv"] = {n: lp["rwkv_" + n] for n in _RWKV_PARAM_NAMES}
    lp["head_expand"] = jnp.pad(jnp.repeat(jnp.eye(SSD_HEADS, dtype=F32), SSD_HEAD_DIM, axis=1),
                                ((0, LANES - SSD_HEADS), (0, 0)))
    return lp


def _mixers_prompt(lp, proj, *, batch, seq):
    z, xbc, ur, us5, upool, dtr = proj
    y_ssd, conv_new, ssd_new = _ssd(z, xbc, dtr, lp, batch=batch, seq=seq)
    y_rwkv, shift_new, rwkv_new = _rwkv(ur, lp["rwkv"], batch=batch, seq=seq)
    zeros = jnp.zeros((batch, S5_WIDTH), F32)
    bm = lambda a: a.reshape(batch, seq, a.shape[-1])
    rows = lambda a: a.reshape(batch * seq, a.shape[-1])
    y_s5, s5re, s5im = _s5(bm(us5), zeros, zeros, lp, batch_major=True)
    y_pool, pool_new = _pool(bm(upool), jnp.zeros((POOL_BUF, batch, GROUP_WIDTH), F32), lp, pos0=0,
                             batch_major=True)
    ys = (y_ssd, y_rwkv, rows(y_s5), rows(y_pool))
    states = (conv_new, ssd_new, shift_new, rwkv_new, s5re.reshape(batch, S5_GROUPS, S5_STATE),
              s5im.reshape(batch, S5_GROUPS, S5_STATE), jnp.swapaxes(pool_new, 0, 1))
    return ys, states


def _mixers_decode(lp, proj, states, done, *, batch, seq, layer):
    z, xbc, ur, us5, upool, dtr = proj
    shift0, s5re0, s5im0 = (states[i][layer] for i in (2, 4, 5))
    ssd_done, rwkv_done = (done[1], done[3]) if layer else (None, None)
    y_ssd, conv_new, ssd_new = _ssd_step(z, xbc, dtr, states[0], states[1], ssd_done, lp, batch=batch, seq=seq,
                                         layer=layer)
    y_rwkv, rwkv_new = _rwkv_step(ur, shift0, states[3], rwkv_done, lp["rwkv"], batch=batch, seq=seq, layer=layer)
    shift_new = ur[(seq - 1) * batch:, :]
    tm = lambda a: a.reshape(seq, batch, a.shape[-1])
    y_s5, s5re, s5im = _s5(tm(us5), s5re0.reshape(batch, S5_WIDTH), s5im0.reshape(batch, S5_WIDTH), lp,
                           batch_major=False)
    y_pool, pool_new = _pool(tm(upool), states[6], lp, pos0=PAST_LEN, batch_major=False, layer=layer)
    rows = lambda a: a.reshape(seq * batch, a.shape[-1])
    ys = (y_ssd, y_rwkv, rows(y_s5), rows(y_pool))
    new_states = (jnp.swapaxes(conv_new, 0, 1), ssd_new, shift_new, rwkv_new,
                  s5re.reshape(batch, S5_GROUPS, S5_STATE), s5im.reshape(batch, S5_GROUPS, S5_STATE),
                  jnp.swapaxes(pool_new, 0, 1))
    return ys, new_states


_WIDTHS = (GROUP_WIDTH, SSD_CONV_DIM, RWKV_PROJ, GROUP_WIDTH, GROUP_WIDTH, LANES)


def _trunk(x_p, x_s, layer_params, norm_final, mixers_p, mixers_s):
    st_p, st_s = [], []
    mix_p, mix_s, lp = None, None, None
    for l, lp_next in enumerate(layer_params):
        if l > 0:
            x_s, wg, wu, wo = _ffn_cast(x_s, lp["norm_ffn2"], lp["ffn2_in"], lp["ffn2_out"], mix=mix_s, wmix=lp["w_out"])
            x_p = _ffn(x_p, lp["norm_ffn2"], wg, wu, wo, mix=mix_p, wmix=lp["w_out"])
        lp = lp_next
        x_s, wg, wu, wo = _ffn_cast(x_s, lp["norm_ffn1"], lp["ffn1_in"], lp["ffn1_out"])
        x_p = _ffn(x_p, lp["norm_ffn1"], wg, wu, wo)
        proj_s, w_all = _inproj_cast(x_s, lp["norm_mix"], lp["w_in"], _WIDTHS)
        mix_p, st = mixers_p(l, lp, _inproj(x_p, lp["norm_mix"], w_all, _WIDTHS), st_p[-1] if st_p else None)
        st_p.append(st)
        mix_s, st = mixers_s(l, lp, proj_s, st_s[-1] if st_s else None)
        st_s.append(st)
    x_s, wg, wu, wo = _ffn_cast(x_s, lp["norm_ffn2"], lp["ffn2_in"], lp["ffn2_out"], mix=mix_s, wmix=lp["w_out"],
                                gf=norm_final)
    x_p = _ffn(x_p, lp["norm_ffn2"], wg, wu, wo, mix=mix_p, wmix=lp["w_out"], gf=norm_final)
    return (x_p, x_s), (st_p, st_s)


def kernel(x_prompt, x_sample, state_ssd_conv, state_ssd, state_rwkv_shift, state_rwkv, state_s5_re, state_s5_im, state_pool, norm_ffn1, ffn1_in, ffn1_out, norm_mix, w_in, ssd_conv_w, ssd_conv_b, ssd_dt_bias, ssd_a_log, ssd_d, ssd_norm, rwkv_mu, rwkv_w0, rwkv_w2, rwkv_a0, rwkv_a2, rwkv_g2, rwkv_k_k, rwkv_k_a, rwkv_r_k, rwkv_ln_g, rwkv_ln_b, s5_lam_re, s5_lam_im, s5_log_step, s5_b_re, s5_b_im, s5_c_re, s5_c_im, s5_d, s5_glu_w, s5_glu_b, pool_w, pool_scale, w_out, norm_ffn2, ffn2_in, ffn2_out, norm_final):
    P = dict(norm_ffn1=norm_ffn1, ffn1_in=ffn1_in, ffn1_out=ffn1_out, norm_mix=norm_mix, w_in=w_in,
             ssd_conv_w=ssd_conv_w, ssd_conv_b=ssd_conv_b, ssd_dt_bias=ssd_dt_bias, ssd_a_log=ssd_a_log,
             ssd_d=ssd_d, ssd_norm=ssd_norm, rwkv_mu=rwkv_mu, rwkv_w0=rwkv_w0, rwkv_w2=rwkv_w2, rwkv_a0=rwkv_a0,
             rwkv_a2=rwkv_a2, rwkv_g2=rwkv_g2, rwkv_k_k=rwkv_k_k, rwkv_k_a=rwkv_k_a,
             rwkv_r_k=rwkv_r_k.reshape(rwkv_r_k.shape[0], -1), rwkv_ln_g=rwkv_ln_g, rwkv_ln_b=rwkv_ln_b,
             s5_lam_re=s5_lam_re, s5_lam_im=s5_lam_im, s5_log_step=s5_log_step, s5_b_re=s5_b_re, s5_b_im=s5_b_im,
             s5_c_re=s5_c_re, s5_c_im=s5_c_im, s5_d=s5_d, s5_glu_w=s5_glu_w, s5_glu_b=s5_glu_b, pool_w=pool_w,
             pool_scale=pool_scale, w_out=w_out, norm_ffn2=norm_ffn2, ffn2_in=ffn2_in, ffn2_out=ffn2_out)
    depth = norm_ffn1.shape[0]
    bp, tp, d = x_prompt.shape
    bs, ts, _ = x_sample.shape
    stacked = _stacked_params(P)
    layer_params = [_layer_params(stacked, l) for l in range(depth)]
    gf = norm_final.reshape(1, -1)
    sample_states = (state_ssd_conv, state_ssd, state_rwkv_shift, state_rwkv, state_s5_re, state_s5_im, state_pool)
    rwkv_rows = RWKV_HEADS * RWKV_HEAD
    decode_states = (jnp.swapaxes(state_ssd_conv, 1, 2), state_ssd, state_rwkv_shift,
                     jnp.transpose(state_rwkv, (0, 2, 3, 4, 1)).reshape(depth, rwkv_rows, RWKV_HEAD, bs),
                     state_s5_re, state_s5_im, jnp.swapaxes(state_pool, 1, 2))

    x_s = jnp.swapaxes(x_sample, 0, 1).reshape(ts * bs, d)
    (y_p, y_s), (st_p, st_s) = _trunk(
        x_prompt.reshape(bp * tp, d), x_s, layer_params, gf,
        lambda l, lp, proj, done: _mixers_prompt(lp, proj, batch=bp, seq=tp),
        lambda l, lp, proj, done: _mixers_decode(lp, proj, decode_states, done, batch=bs, seq=ts, layer=l))
    outs = [y_p.reshape(bp, tp, d), jnp.swapaxes(y_s.reshape(ts, bs, d), 0, 1)]
    for i, ref_state in enumerate(sample_states):
        outs.append(jnp.stack([st[i] for st in st_p]))
        if i == 1:
            outs.append(st_s[-1][i].reshape(ref_state.shape))
        elif i == 3:
            s_new = st_s[-1][i].reshape(depth, RWKV_HEADS, RWKV_HEAD, RWKV_HEAD, bs)
            outs.append(jnp.transpose(s_new, (0, 4, 1, 2, 3)))
        else:
            outs.append(jnp.stack([st[i] for st in st_s]))
    return tuple(outs)
```

```python
import functools
import math

import jax
import jax.numpy as jnp
from jax import lax
from jax.experimental import pallas as pl
from jax.experimental.pallas import tpu as pltpu

F32 = jnp.float32
BF16 = jnp.bfloat16
HIGHEST = lax.Precision.HIGHEST

SUBLANES = 8
LANES = 128
VMEM_LIMIT_BYTES = 56 * 1024 * 1024

GROUP_WIDTH = 256
SSD_HEAD_DIM = 64
SSD_HEADS = 4
SSD_GROUPS = 2
SSD_STATE = 128
SSD_CONV = 4
SSD_CONV_DIM = GROUP_WIDTH + 2 * SSD_GROUPS * SSD_STATE
SSD_CHUNK = 128
SSD_GROUP = 8
LOG2_E = math.log2(math.e)
RWKV_HEAD = 64
RWKV_HEADS = 4
RWKV_PROJ = 1024
RWKV_LN_EPS = 64e-5
RWKV_CHUNK = 64
RWKV_GROUP = 16
S5_GROUPS = 16
S5_STATE = 64
S5_WIDTH = S5_GROUPS * S5_STATE
POOL_WINDOWS = (2, 4, 8, 16)
POOL_CH = 64
POOL_BUF = 15
RMS_EPS = 1e-6
PAST_LEN = 16384

ROW_TILE = 1024
INPROJ_ROW_TILE = 1024
FFN_CHUNK = 256
TM_CHUNK = 128
POOL_CHUNK = 256
S5_SUB = 128
SSD_STEP_TILES = 16
RWKV_STEP_TILES = 64


def _cparams(*sem):
    return pltpu.CompilerParams(dimension_semantics=sem, vmem_limit_bytes=VMEM_LIMIT_BYTES)


def _dot(a, b, **kw):
    return jnp.dot(a, b, preferred_element_type=F32, **kw)


def _dot_nt(a, b):
    return lax.dot_general(a, b, (((1,), (1,)), ((), ())), preferred_element_type=F32)


def _dot_tn(a, b):
    return lax.dot_general(a, b, (((0,), (0,)), ((), ())), preferred_element_type=F32)


def _sigmoid(x):
    return 0.5 * jnp.tanh(0.5 * x) + 0.5


def _silu(x):
    h = 0.5 * x
    return h + h * jnp.tanh(h)


def _softplus(x):
    return jnp.maximum(x, 0.0) + jnp.log(1.0 + jnp.exp(-jnp.abs(x)))


def _gelu_tanh(x):
    c = math.sqrt(2.0 / math.pi)
    return x * (0.5 * (1.0 + jnp.tanh(c * (x + 0.044715 * (x * x * x)))))


def _rms(x, g):
    return x * lax.rsqrt(jnp.mean(x * x, axis=-1, keepdims=True) + RMS_EPS) * g


def _full_spec(shape):
    n = len(shape)
    return pl.BlockSpec(shape, lambda *_: (0,) * n)


class _Layered(tuple):
    pass


def _pspec(p, single=False):
    mode = pl.Buffered(1) if single else None
    if isinstance(p, _Layered):
        a, l = p
        return pl.BlockSpec((None,) + a.shape[1:], lambda *_: (l,) + (0,) * (a.ndim - 1), pipeline_mode=mode)
    n = p.ndim
    return pl.BlockSpec(p.shape, lambda *_: (0,) * n, pipeline_mode=mode)


def _parg(p):
    return p[0] if isinstance(p, _Layered) else p


def _mix_residual(x, y_refs, wmix_ref):
    for j, y_ref in enumerate(y_refs):
        x = x + _dot(y_ref[...].astype(BF16), wmix_ref[j * GROUP_WIDTH:(j + 1) * GROUP_WIDTH, :])
    return x


def _swiglu_chunk(h, wg, wu, wo):
    act = (_silu(_dot(h, wg)) * _dot(h, wu)).astype(BF16)
    return _dot(act, wo)


def _ffn_body(*refs, has_mix, final_norm):
    it = iter(refs)
    x = next(it)[...]
    if has_mix:
        y_refs = [next(it) for _ in range(4)]
        x = _mix_residual(x, y_refs, next(it))
    g_ref, wg_ref, wu_ref, wo_ref = next(it), next(it), next(it), next(it)
    gf_ref = next(it) if final_norm else None
    o_ref = next(it)
    h = _rms(x, g_ref[...]).astype(BF16)
    acc = jnp.zeros_like(x)
    for c in range(wo_ref.shape[0] // FFN_CHUNK):
        cols = slice(c * FFN_CHUNK, (c + 1) * FFN_CHUNK)
        acc = acc + _swiglu_chunk(h, wg_ref[:, cols], wu_ref[:, cols], wo_ref[cols, :])
    x = x + 0.5 * acc
    if final_norm:
        x = _rms(x, gf_ref[...])
    o_ref[...] = x


def _ffn(x, g, wg, wu, wo, mix=None, wmix=None, gf=None):
    rows, d = x.shape
    row_spec = lambda w: pl.BlockSpec((ROW_TILE, w), lambda i: (i, 0))
    args, specs = [x], [row_spec(d)]
    if mix is not None:
        for y in mix:
            args.append(y)
            specs.append(row_spec(y.shape[1]))
        args.append(_parg(wmix))
        specs.append(_pspec(wmix, single=True))
    for a in (g, wg, wu, wo) + ((gf,) if gf is not None else ()):
        args.append(_parg(a))
        specs.append(_pspec(a, single=True))
    return pl.pallas_call(
        functools.partial(_ffn_body, has_mix=mix is not None, final_norm=gf is not None),
        grid=(rows // ROW_TILE,),
        in_specs=specs,
        out_specs=row_spec(d),
        out_shape=jax.ShapeDtypeStruct((rows, d), F32),
        compiler_params=_cparams("parallel"),
        name="ffn",
    )(*args)


def _ffn_cast_body(*refs, has_mix, final_norm):
    it = iter(refs)
    x_ref = next(it)
    if has_mix:
        y_refs = [next(it) for _ in range(4)]
        wmix_ref = next(it)
    g_ref, wg_ref, wu_ref, wo_ref = next(it), next(it), next(it), next(it)
    gf_ref = next(it) if final_norm else None
    o_ref, wg_out, wu_out, wo_out, x_scr, h_scr, acc_scr = (next(it) for _ in range(7))
    c = pl.program_id(0)

    @pl.when(c == 0)
    def _():
        x = x_ref[...]
        if has_mix:
            x = _mix_residual(x, y_refs, wmix_ref)
        x_scr[...] = x
        h_scr[...] = _rms(x, g_ref[...]).astype(BF16)
        acc_scr[...] = jnp.zeros(acc_scr.shape, F32)

    wg = wg_ref[...].astype(BF16)
    wu = wu_ref[...].astype(BF16)
    wo = wo_ref[...].astype(BF16)
    wg_out[...] = wg
    wu_out[...] = wu
    wo_out[...] = wo
    acc_scr[...] += _swiglu_chunk(h_scr[...], wg, wu, wo)

    @pl.when(c == pl.num_programs(0) - 1)
    def _():
        x = x_scr[...] + 0.5 * acc_scr[...]
        if final_norm:
            x = _rms(x, gf_ref[...])
        o_ref[...] = x


def _ffn_cast(x, g, wi, wo, mix=None, wmix=None, gf=None):
    rows, d = x.shape
    wi_all, l = wi
    wo_all, _ = wo
    d_ff = wo_all.shape[1]
    nchunks = d_ff // FFN_CHUNK
    args, specs = [x], [_full_spec(x.shape)]
    if mix is not None:
        for y in mix:
            args.append(y)
            specs.append(_full_spec(y.shape))
        args.append(_parg(wmix))
        specs.append(_pspec(wmix, single=True))
    args += [_parg(g), wi_all, wi_all, wo_all]
    specs += [_pspec(g),
              pl.BlockSpec((None, d, FFN_CHUNK), lambda c: (l, 0, c)),
              pl.BlockSpec((None, d, FFN_CHUNK), lambda c: (l, 0, c + nchunks)),
              pl.BlockSpec((None, FFN_CHUNK, d), lambda c: (l, c, 0))]
    if gf is not None:
        args.append(gf)
        specs.append(_full_spec(gf.shape))
    col_spec = pl.BlockSpec((d, FFN_CHUNK), lambda c: (0, c))
    return pl.pallas_call(
        functools.partial(_ffn_cast_body, has_mix=mix is not None, final_norm=gf is not None),
        grid=(nchunks,),
        in_specs=specs,
        out_specs=[_full_spec(x.shape), col_spec, col_spec, pl.BlockSpec((FFN_CHUNK, d), lambda c: (c, 0))],
        out_shape=[jax.ShapeDtypeStruct((rows, d), F32), jax.ShapeDtypeStruct((d, d_ff), BF16),
                   jax.ShapeDtypeStruct((d, d_ff), BF16), jax.ShapeDtypeStruct((d_ff, d), BF16)],
        scratch_shapes=[pltpu.VMEM((rows, d), F32), pltpu.VMEM((rows, d), BF16), pltpu.VMEM((rows, d), F32)],
        compiler_params=_cparams("arbitrary"),
        name="ffn_cast",
    )(*args)


def _inproj_body(x_ref, g_ref, wt_ref, *o_refs):
    h = _rms(x_ref[...], g_ref[...]).astype(BF16)
    off = 0
    for o_ref in o_refs:
        n = o_ref.shape[-1]
        o_ref[...] = _dot_nt(h, wt_ref[off:off + n, :])
        off += n


def _inproj_cast_body(x_ref, g_ref, win_ref, *o_refs, layer):
    *proj_refs, wall_ref = o_refs
    split = GROUP_WIDTH + SSD_CONV_DIM
    wt = win_ref[:, layer, :]
    tail = wt.shape[0] - split - SSD_HEADS
    wall_ref[0:split, :] = wt[0:split].astype(BF16)
    wall_ref[split:split + tail, :] = wt[split + SSD_HEADS:].astype(BF16)
    dt_rows = jnp.concatenate([wt[split:split + SSD_HEADS], jnp.zeros((LANES - SSD_HEADS, wt.shape[1]), F32)], axis=0)
    wall_ref[split + tail:, :] = dt_rows.astype(BF16)
    _inproj_body(x_ref, g_ref, wall_ref, *proj_refs)


def _inproj_cast(x, g, w_in, widths):
    rows, d = x.shape
    wt_all, l = w_in
    outs = pl.pallas_call(
        functools.partial(_inproj_cast_body, layer=l),
        grid=(1,),
        in_specs=[_full_spec(x.shape), _pspec(g),
                  pl.BlockSpec(wt_all.shape, lambda i: (0, 0, 0), pipeline_mode=pl.Buffered(1))],
        out_specs=[_full_spec((rows, n)) for n in widths] + [_full_spec((sum(widths), d))],
        out_shape=[jax.ShapeDtypeStruct((rows, n), F32) for n in widths]
                  + [jax.ShapeDtypeStruct((sum(widths), d), BF16)],
        compiler_params=_cparams("arbitrary"),
        name="inproj_cast",
    )(x, _parg(g), wt_all)
    return outs[:-1], outs[-1]


def _inproj(x, g, w, widths):
    rows, d = x.shape
    tile = INPROJ_ROW_TILE
    row_spec = lambda w_: pl.BlockSpec((tile, w_), lambda i: (i, 0))
    return pl.pallas_call(
        _inproj_body,
        grid=(rows // tile,),
        in_specs=[row_spec(d), _pspec(g), _pspec(w, single=True)],
        out_specs=[row_spec(n) for n in widths],
        out_shape=[jax.ShapeDtypeStruct((rows, n), F32) for n in widths],
        compiler_params=_cparams("parallel"),
        name="inproj",
    )(x, _parg(g), _parg(w))


def _ssd_body(z_ref, xbc_ref, dt_ref, cw_ref, cb_ref, dtb_ref, alog_ref, dsk_ref, ng_ref,
              y_ref, conv_ref, hout_ref, xpad_scr, h_scr, *, chunk, group):
    L, G = chunk, group
    GL = G * L
    c = pl.program_id(1)
    pad = SUBLANES
    halo = SSD_CONV - 1
    hpg = SSD_HEADS // SSD_GROUPS
    assert hpg == 2 and hpg * SSD_HEAD_DIM == SSD_STATE

    @pl.when(c == 0)
    def _():
        xpad_scr[0:pad, :] = jnp.zeros((pad, SSD_CONV_DIM), F32)
        h_scr[...] = jnp.zeros(h_scr.shape, F32)

    xpad_scr[pad:pad + GL, :] = xbc_ref[...]
    xfull = xpad_scr[...]
    conv = cb_ref[...] + cw_ref[halo:halo + 1, :] * xfull[pad:pad + GL]
    for j in range(halo):
        conv = conv + cw_ref[j:j + 1, :] * pltpu.roll(xfull, halo - j, axis=0)[pad:pad + GL]
    xpad_scr[pad - halo:pad, :] = xpad_scr[pad + GL - halo:pad + GL, :]
    conv = _silu(conv)
    xs = conv[:, 0:GROUP_WIDTH]
    bm = conv[:, GROUP_WIDTH:2 * GROUP_WIDTH].astype(BF16)
    cm = conv[:, 2 * GROUP_WIDTH:3 * GROUP_WIDTH].astype(BF16)

    row = lax.broadcasted_iota(jnp.int32, (L, L), 0)
    col = lax.broadcasted_iota(jnp.int32, (L, L), 1)
    causal = row >= col
    tril = jnp.where(causal, 1.0, 0.0).astype(F32)
    dt = _softplus(dt_ref[...] + dtb_ref[...])
    da = dt * (-jnp.exp(alog_ref[...]) * LOG2_E)
    acs = [_dot(tril, da[i * L:(i + 1) * L, :], precision=HIGHEST) for i in range(G)]
    acs_t = [a.T for a in acs]
    e_acs = [jnp.exp2(a) for a in acs]
    e_end = [jnp.exp2(a[L - 1:L, :] - a) for a in acs]
    e_last = [jnp.exp2(a[L - 1:L, :]) for a in acs]

    keys = [(i, g) for i in range(G) for g in range(SSD_GROUPS)]
    rows_of = lambda x, i: x[i * L:(i + 1) * L]
    lanes_of = lambda x, g: x[:, g * SSD_STATE:(g + 1) * SSD_STATE]
    lane_lo = lax.broadcasted_iota(jnp.int32, (L, hpg * SSD_HEAD_DIM), 1) < SSD_HEAD_DIM
    row_lo = lax.broadcasted_iota(jnp.int32, (hpg * SSD_HEAD_DIM, SSD_STATE), 0) < SSD_HEAD_DIM
    head_cols = lambda a, g: jnp.where(lane_lo, a[:, g * hpg:g * hpg + 1], a[:, g * hpg + 1:g * hpg + 2])
    bg = {(i, g): lanes_of(rows_of(bm, i), g) for i, g in keys}
    cg = {(i, g): lanes_of(rows_of(cm, i), g) for i, g in keys}
    scores = {k: _dot_nt(cg[k], bg[k]) for k in keys}
    xdt = {(i, g): lanes_of(rows_of(xs, i), g) * head_cols(rows_of(dt, i), g) for i, g in keys}
    decay = {(i, h): jnp.exp2(jnp.where(causal, acs[i][:, h:h + 1] - acs_t[i][h:h + 1, :], -jnp.inf))
             for i in range(G) for h in range(SSD_HEADS)}
    p_mat = {(i, g): jnp.concatenate([(scores[(i, g)] * decay[(i, g * hpg + k)]).astype(BF16) for k in range(hpg)],
                                     axis=1) for i, g in keys}
    y_in = {k: _dot(p_mat[k], _bd(xdt[k].astype(BF16))) for k in keys}
    st = {(i, g): _dot_tn((xdt[(i, g)] * head_cols(e_end[i], g)).astype(BF16), bg[(i, g)]) for i, g in keys}

    y_rows = []
    for i in range(G):
        ys = []
        for g in range(SSD_GROUPS):
            h_prev = h_scr[g * hpg:(g + 1) * hpg].reshape(hpg * SSD_HEAD_DIM, SSD_STATE)
            ys.append(y_in[(i, g)] + _dot_nt(cg[(i, g)], h_prev.astype(BF16)) * head_cols(e_acs[i], g))
            keep = jnp.where(row_lo, e_last[i][:, g * hpg:g * hpg + 1], e_last[i][:, g * hpg + 1:g * hpg + 2])
            h_scr[g * hpg:(g + 1) * hpg] = (h_prev * keep + st[(i, g)]).reshape(hpg, SSD_HEAD_DIM, SSD_STATE)
        y_rows.append(jnp.concatenate(ys, axis=-1))
    y = jnp.concatenate(y_rows, axis=0) + xs * dsk_ref[...]
    y = y * _silu(z_ref[...])
    y_ref[...] = _rms(y, ng_ref[...])

    @pl.when(c == pl.num_programs(1) - 1)
    def _():
        hout_ref[0] = h_scr[...]
        conv_ref[0] = xpad_scr[pad - halo:pad, :]


def _ssd(z, xbc, dtr, lp, *, batch, seq):
    chunk = SSD_CHUNK
    rows = chunk * SSD_GROUP
    nc = seq // rows
    rspec = lambda w: pl.BlockSpec((rows, w), lambda b, c: (b * nc + c, 0))
    consts = (lp["conv_w"], lp["conv_b"], lp["dt_bias"], lp["a_log"], lp["d_skip"], lp["ssd_norm"])
    return pl.pallas_call(
        functools.partial(_ssd_body, chunk=chunk, group=SSD_GROUP),
        grid=(batch, nc),
        in_specs=[rspec(GROUP_WIDTH), rspec(SSD_CONV_DIM), rspec(LANES)] + [_pspec(a) for a in consts],
        out_specs=[rspec(GROUP_WIDTH),
                   pl.BlockSpec((1, SSD_CONV - 1, SSD_CONV_DIM), lambda b, c: (b, 0, 0)),
                   pl.BlockSpec((1, SSD_HEADS, SSD_HEAD_DIM, SSD_STATE), lambda b, c: (b, 0, 0, 0))],
        out_shape=[jax.ShapeDtypeStruct((batch * seq, GROUP_WIDTH), F32),
                   jax.ShapeDtypeStruct((batch, SSD_CONV - 1, SSD_CONV_DIM), F32),
                   jax.ShapeDtypeStruct((batch, SSD_HEADS, SSD_HEAD_DIM, SSD_STATE), F32)],
        scratch_shapes=[pltpu.VMEM((SUBLANES + rows, SSD_CONV_DIM), F32),
                        pltpu.VMEM((SSD_HEADS, SSD_HEAD_DIM, SSD_STATE), F32)],
        compiler_params=_cparams("parallel", "arbitrary"),
        name="ssd",
    )(z, xbc, dtr, *[_parg(a) for a in consts])


def _ssd_step_body(z_ref, xbc_ref, dt_ref, conv0_ref, h0_ref, *rest, seq, batch, layer):
    hdone_ref, rest = (rest[0], rest[1:]) if layer else (None, rest)
    (cw_ref, cb_ref, dtb_ref, aneg_ref, dsk_ref, ng_ref, hexp_ref, y_ref, conv_ref, hout_ref,
     xs_scr, bm_scr, cm_scr, xdt_scr, dec_scr, y_scr) = rest
    T, B = seq, batch
    if layer:
        hout_ref[0:layer] = hdone_ref[...]
    GW = GROUP_WIDTH
    j = pl.program_id(0)
    tiles = SSD_STEP_TILES

    @pl.when(j == 0)
    def _():
        rows = [conv0_ref[i] for i in range(SSD_CONV - 1)]
        rows += [xbc_ref[t * B:(t + 1) * B, :] for t in range(T)]
        for t in range(T):
            conv = cb_ref[...] + cw_ref[0:1, :] * rows[t]
            for i in range(1, SSD_CONV):
                conv = conv + cw_ref[i:i + 1, :] * rows[t + i]
            conv = _silu(conv)
            xs = conv[:, 0:GW]
            xs_scr[t] = xs
            for g in range(SSD_GROUPS):
                bm_scr[t, g] = conv[:, GW + g * SSD_STATE:GW + (g + 1) * SSD_STATE].T
                cm_scr[t, g] = conv[:, 2 * GW + g * SSD_STATE:2 * GW + (g + 1) * SSD_STATE].T
            dt = _softplus(dt_ref[t * B:(t + 1) * B, :] + dtb_ref[...])
            dte = _dot(dt, hexp_ref[...], precision=HIGHEST)
            xdt_scr[t] = (xs * dte).T
            dec_scr[t] = jnp.exp(dte * aneg_ref[...]).T
        for i in range(SSD_CONV - 1):
            conv_ref[i] = rows[T + i]

    hp0 = j * tiles
    grp = hp0 // (SSD_HEAD_DIM * (SSD_HEADS // SSD_GROUPS))
    for q in range(tiles):
        hp = pl.ds(hp0 + q, 1)
        h = h0_ref[:, q, :].T
        for t in range(T):
            h = h * dec_scr[t, hp, :] + bm_scr[t, grp] * xdt_scr[t, hp, :]
            y_scr[t, hp, :] = jnp.sum(h * cm_scr[t, grp], axis=0, keepdims=True)
        hout_ref[layer, :, q, :] = h.T

    @pl.when(j == pl.num_programs(0) - 1)
    def _():
        for t in range(T):
            y = y_scr[t].T + xs_scr[t] * dsk_ref[...]
            y = y * _silu(z_ref[t * B:(t + 1) * B, :])
            y_ref[t * B:(t + 1) * B, :] = _rms(y, ng_ref[...])


def _layer_state_specs(layer, block, axis):
    idx = lambda first: (lambda j: (first,) + tuple(j if a == axis else 0 for a in range(len(block))))
    cur = pl.BlockSpec((None,) + block, idx(layer))
    prev = [pl.BlockSpec((layer,) + block, idx(0))] if layer else []
    out = pl.BlockSpec((layer + 1,) + block, idx(0))
    return cur, prev, out


def _ssd_step(z, xbc, dtr, conv_all, h_all, h_done, lp, *, batch, seq, layer):
    n = batch * seq
    srows = SSD_HEADS * SSD_HEAD_DIM
    consts = (lp["conv_w"], lp["conv_b"], lp["dt_bias"], lp["a_neg_exp"], lp["d_skip"], lp["ssd_norm"], lp["head_expand"])
    hspec, prev_specs, hout_spec = _layer_state_specs(layer, (batch, SSD_STEP_TILES, SSD_STATE), 1)
    prev_args = [h_done] if layer else []
    cshape = (SSD_CONV - 1, batch, SSD_CONV_DIM)
    return pl.pallas_call(
        functools.partial(_ssd_step_body, seq=seq, batch=batch, layer=layer),
        grid=(srows // SSD_STEP_TILES,),
        in_specs=[_full_spec((n, GROUP_WIDTH)), _full_spec((n, SSD_CONV_DIM)), _full_spec((n, LANES)),
                  pl.BlockSpec((None,) + cshape, lambda j: (layer, 0, 0, 0)), hspec] + prev_specs
                 + [_pspec(a) for a in consts],
        out_specs=[_full_spec((n, GROUP_WIDTH)), _full_spec(cshape), hout_spec],
        out_shape=[jax.ShapeDtypeStruct((n, GROUP_WIDTH), F32),
                   jax.ShapeDtypeStruct(cshape, F32),
                   jax.ShapeDtypeStruct((layer + 1, batch, srows, SSD_STATE), F32)],
        scratch_shapes=[pltpu.VMEM((seq, batch, GROUP_WIDTH), F32),
                        pltpu.VMEM((seq, SSD_GROUPS, SSD_STATE, batch), F32),
                        pltpu.VMEM((seq, SSD_GROUPS, SSD_STATE, batch), F32),
                        pltpu.VMEM((seq, GROUP_WIDTH, batch), F32),
                        pltpu.VMEM((seq, GROUP_WIDTH, batch), F32),
                        pltpu.VMEM((seq, GROUP_WIDTH, batch), F32)],
        compiler_params=_cparams("arbitrary"),
        name="ssd_step",
    )(z, xbc, dtr, conv_all, h_all.reshape(h_all.shape[0], batch, srows, SSD_STATE),
      *prev_args, *[_parg(a) for a in consts])


PAIR = 2 * RWKV_HEAD
RWKV_PAIRS = RWKV_HEADS // 2


def _bd(x):
    half = x.shape[1] // 2
    lane = lax.broadcasted_iota(jnp.int32, x.shape, 1)
    zero = jnp.zeros_like(x)
    return jnp.concatenate([jnp.where(lane < half, x, zero), jnp.where(lane >= half, x, zero)], axis=0)


def _half_sums(x, lo):
    s_lo = jnp.sum(jnp.where(lo, x, 0.0), axis=-1, keepdims=True)
    s_hi = jnp.sum(jnp.where(lo, 0.0, x), axis=-1, keepdims=True)
    return jnp.where(lo, s_lo, s_hi)


def _head_sum(x):
    lo = lax.broadcasted_iota(jnp.int32, (x.shape[0], PAIR), 1) < RWKV_HEAD
    return jnp.concatenate([_half_sums(x[:, p * PAIR:(p + 1) * PAIR], lo) for p in range(RWKV_PAIRS)], axis=-1)


def _rwkv_pointwise(u, prev, mu_ref, w0_ref, w2_ref, a0_ref, a2_ref, g2_ref, kk_ref, ka_ref):
    GW = GROUP_WIDTH
    xs = u + (prev - u) * mu_ref[...]
    r = xs[:, 0:GW]
    k = xs[:, GW:2 * GW]
    v = xs[:, 2 * GW:3 * GW]
    wd = xs[:, 3 * GW:3 * GW + 64]
    ad = xs[:, 3 * GW + 64:3 * GW + 128]
    gd = xs[:, 3 * GW + 128:3 * GW + 256]
    w_lin = w0_ref[...] + _dot(jnp.tanh(wd).astype(BF16), w2_ref[...])
    logdecay = -math.exp(-0.5) * _sigmoid(w_lin)
    a = _sigmoid(a0_ref[...] + _dot(ad.astype(BF16), a2_ref[...]))
    g = _dot(_sigmoid(gd).astype(BF16), g2_ref[...])
    kk = k * kk_ref[...]
    kk = kk * lax.rsqrt(jnp.maximum(_head_sum(kk * kk), 1e-24))
    k = k * (1.0 + (a - 1.0) * ka_ref[...])
    return r, k, v, logdecay, a, g, kk


def _rwkv_finish(y, r, k, v, g, rk_ref, lng_ref, lnb_ref):
    mean = _head_sum(y) * (1.0 / RWKV_HEAD)
    yc = y - mean
    var = _head_sum(yc * yc) * (1.0 / RWKV_HEAD)
    y = yc * lax.rsqrt(var + RWKV_LN_EPS) * lng_ref[...] + lnb_ref[...]
    bonus = _head_sum(r * k * rk_ref[...]) * v
    return (y + bonus) * g


def _rwkv_body(u_ref, mu_ref, w0_ref, w2_ref, a0_ref, a2_ref, g2_ref, kk_ref, ka_ref, rk_ref,
               lng_ref, lnb_ref, y_ref, shift_ref, sout_ref, upad_scr, s_scr, *, chunk, group):
    L, G = chunk, group
    GL = G * L
    c = pl.program_id(1)
    pad = SUBLANES

    @pl.when(c == 0)
    def _():
        upad_scr[0:pad, :] = jnp.zeros((pad, RWKV_PROJ), F32)
        s_scr[...] = jnp.zeros(s_scr.shape, F32)

    u = u_ref[...]
    upad_scr[pad:pad + GL, :] = u
    prev = pltpu.roll(upad_scr[...], 1, axis=0)[pad:pad + GL]
    upad_scr[pad - 1:pad, :] = u[GL - 1:GL, :]
    r, k, v, logdecay, a, g, kk = _rwkv_pointwise(u, prev, mu_ref, w0_ref, w2_ref, a0_ref, a2_ref, g2_ref,
                                                  kk_ref, ka_ref)

    tril = jnp.where(lax.broadcasted_iota(jnp.int32, (L, L), 0) >= lax.broadcasted_iota(jnp.int32, (L, L), 1),
                     1.0, 0.0).astype(F32)
    cl = jnp.concatenate([_dot(tril, logdecay[i * L:(i + 1) * L, :], precision=HIGHEST) for i in range(G)], axis=0)
    e_in = jnp.exp(cl)
    e_inv = jnp.exp(-cl)
    r_t = r * e_in
    r_tb = r_t.astype(BF16)
    a_tb = (-kk * jnp.exp(cl - logdecay)).astype(BF16)
    b_tb = (kk * a * e_inv).astype(BF16)
    k_tb = (k * e_inv).astype(BF16)
    vb = v.astype(BF16)

    row = lax.broadcasted_iota(jnp.int32, (L, PAIR), 0)
    colh = lax.broadcasted_iota(jnp.int32, (L, PAIR), 1) & (RWKV_HEAD - 1)
    strict = row > colh
    incl = row >= colh
    eye_pair = jnp.where(row == colh, 1.0, 0.0).astype(F32)
    lane_lo = lax.broadcasted_iota(jnp.int32, (RWKV_HEAD, PAIR), 1) < RWKV_HEAD
    same_head = (lax.broadcasted_iota(jnp.int32, (PAIR, PAIR), 0) < RWKV_HEAD) == \
                (lax.broadcasted_iota(jnp.int32, (PAIR, PAIR), 1) < RWKV_HEAD)

    streams = [(i, p) for i in range(G) for p in range(RWKV_PAIRS)]
    ns = len(streams)
    blk = lambda x, i, p: x[i * L:(i + 1) * L, p * PAIR:(p + 1) * PAIR]
    lhs = [jnp.concatenate([blk(a_tb, i, p), blk(r_tb, i, p)], axis=0) for i, p in streams]
    m_both = [_dot_nt(lhs[s], jnp.concatenate([_bd(blk(b_tb, i, p)), _bd(blk(k_tb, i, p))], axis=0))
              for s, (i, p) in enumerate(streams)]
    m_ab = [m[:, 0:PAIR] for m in m_both]
    m_ak = [m[:, PAIR:2 * PAIR] for m in m_both]
    n_ab = [jnp.where(strict, m[0:L], 0.0) for m in m_ab]
    m_rb = [jnp.where(incl, m[L:2 * L], 0.0).astype(BF16) for m in m_ab]
    n_ak = [jnp.where(strict, m[0:L], 0.0).astype(BF16) for m in m_ak]
    m_rk = [jnp.where(incl, m[L:2 * L], 0.0).astype(BF16) for m in m_ak]
    tinv = [eye_pair + n for n in n_ab]
    pwb = [n.astype(BF16) for n in n_ab]
    pw = [_dot(x, _bd(x)) for x in pwb]
    for _ in range(int(math.log2(L)) - 2):
        pwb = [x.astype(BF16) for x in pw]
        both = [_dot(jnp.concatenate([pwb[s], tinv[s].astype(BF16)], axis=0), _bd(pwb[s])) for s in range(ns)]
        pw = [x[0:L] for x in both]
        tinv = [tinv[s] + both[s][L:2 * L] for s in range(ns)]
    pwb = [x.astype(BF16) for x in pw]
    tinv = [tinv[s] + _dot(tinv[s].astype(BF16), _bd(pwb[s])) for s in range(ns)]
    tinvb = [x.astype(BF16) for x in tinv]
    nv_mv = [_dot(jnp.concatenate([n_ak[s], m_rk[s]], axis=0), _bd(blk(vb, i, p))) for s, (i, p) in enumerate(streams)]
    wu = [_dot(tinvb[s], jnp.concatenate([_bd(blk(a_tb, i, p)), _bd(nv_mv[s][0:L].astype(BF16))], axis=1))
          for s, (i, p) in enumerate(streams)]
    wub = [x.astype(BF16) for x in wu]
    qy = [_dot(m_rb[s], jnp.concatenate([_bd(wub[s][:, 0:PAIR]), _bd(wub[s][:, PAIR:2 * PAIR])], axis=1))
          for s in range(ns)]
    q = [(blk(r_t, i, p) + qy[s][:, 0:PAIR]).astype(BF16) for s, (i, p) in enumerate(streams)]
    y_loc = [qy[s][:, PAIR:2 * PAIR] + nv_mv[s][L:2 * L] for s in range(ns)]
    zeros_b = jnp.zeros((L, PAIR), BF16)
    mg = [_dot_tn(jnp.concatenate([wub[s], jnp.concatenate([zeros_b, blk(vb, i, p)], axis=1)], axis=0),
                  jnp.concatenate([blk(b_tb, i, p), blk(k_tb, i, p)], axis=0))
          for s, (i, p) in enumerate(streams)]
    p_end = [e_in[(i + 1) * L - 1:(i + 1) * L, p * PAIR:(p + 1) * PAIR] for i, p in streams]
    m_t = [(jnp.where(same_head, mg[s][0:PAIR], 0.0) * p_end[s]).astype(BF16) for s in range(ns)]
    g_t = [jnp.where(lane_lo, mg[s][PAIR:PAIR + RWKV_HEAD], mg[s][PAIR + RWKV_HEAD:2 * PAIR]) * p_end[s]
           for s in range(ns)]

    y_rows = []
    for i in range(G):
        y_pairs = []
        for p in range(RWKV_PAIRS):
            s = i * RWKV_PAIRS + p
            s0 = s_scr[p]
            s0b = s0.astype(BF16)
            y_pairs.append(_dot_nt(q[s], _bd(s0b)) + y_loc[s])
            s_scr[p] = s0 * p_end[s] + _dot(s0b, m_t[s]) + g_t[s]
        y_rows.append(jnp.concatenate(y_pairs, axis=-1))
    y = jnp.concatenate(y_rows, axis=0)
    y_ref[...] = _rwkv_finish(y, r, k, v, g, rk_ref, lng_ref, lnb_ref)

    @pl.when(c == pl.num_programs(1) - 1)
    def _():
        sout_ref[0] = s_scr[...]
        shift_ref[0] = upad_scr[pad - 1:pad, :]


_RWKV_PARAM_NAMES = ("mu", "w0", "w2", "a0", "a2", "g2", "k_k", "k_a", "r_k", "ln_g", "ln_b")


def _rwkv(u, p, *, batch, seq):
    rows = RWKV_CHUNK * RWKV_GROUP
    nc = seq // rows
    params = [p[n] for n in _RWKV_PARAM_NAMES]
    sspec = pl.BlockSpec((1, RWKV_PAIRS, RWKV_HEAD, PAIR), lambda b, c: (b, 0, 0, 0))
    y, shift, s_last = pl.pallas_call(
        functools.partial(_rwkv_body, chunk=RWKV_CHUNK, group=RWKV_GROUP),
        grid=(batch, nc),
        in_specs=[pl.BlockSpec((rows, RWKV_PROJ), lambda b, c: (b * nc + c, 0))] + [_pspec(a) for a in params],
        out_specs=[pl.BlockSpec((rows, GROUP_WIDTH), lambda b, c: (b * nc + c, 0)),
                   pl.BlockSpec((1, 1, RWKV_PROJ), lambda b, c: (b, 0, 0)), sspec],
        out_shape=[jax.ShapeDtypeStruct((batch * seq, GROUP_WIDTH), F32),
                   jax.ShapeDtypeStruct((batch, 1, RWKV_PROJ), F32),
                   jax.ShapeDtypeStruct((batch, RWKV_PAIRS, RWKV_HEAD, PAIR), F32)],
        scratch_shapes=[pltpu.VMEM((SUBLANES + rows, RWKV_PROJ), F32),
                        pltpu.VMEM((RWKV_PAIRS, RWKV_HEAD, PAIR), F32)],
        compiler_params=_cparams("parallel", "arbitrary"),
        name="rwkv",
    )(u, *[_parg(a) for a in params])
    s_last = s_last.reshape(batch, RWKV_PAIRS, RWKV_HEAD, 2, RWKV_HEAD).transpose(0, 1, 3, 2, 4).reshape(
        batch, RWKV_HEADS, RWKV_HEAD, RWKV_HEAD)
    return y, shift.reshape(batch, RWKV_PROJ), s_last


def _rwkv_step_body(u_ref, shift0_ref, s0_ref, *rest, seq, batch, layer):
    sdone_ref, rest = (rest[0], rest[1:]) if layer else (None, rest)
    (mu_ref, w0_ref, w2_ref, a0_ref, a2_ref, g2_ref, kk_ref, ka_ref, rk_ref, lng_ref, lnb_ref, y_ref, sout_ref,
     r_scr, w_scr, k_scr, b_scr, nkk_scr, v_scr, y_scr) = rest
    T, B = seq, batch
    j = pl.program_id(0)
    if layer:
        sout_ref[0:layer] = sdone_ref[...]
    tiles = RWKV_STEP_TILES

    def pointwise(t):
        u = u_ref[t * B:(t + 1) * B, :]
        prev = shift0_ref[...] if t == 0 else u_ref[(t - 1) * B:t * B, :]
        return _rwkv_pointwise(u, prev, mu_ref, w0_ref, w2_ref, a0_ref, a2_ref, g2_ref, kk_ref, ka_ref)

    @pl.when(j == 0)
    def _():
        for t in range(T):
            r, k, v, logdecay, a, _, kk = pointwise(t)
            r_scr[t] = r.T
            w_scr[t] = jnp.exp(logdecay).T
            k_scr[t] = k.T
            b_scr[t] = (kk * a).T
            nkk_scr[t] = (-kk).T
            v_scr[t] = v.T

    i0 = j * tiles
    keys = pl.ds(pl.multiple_of((i0 // RWKV_HEAD) * RWKV_HEAD, RWKV_HEAD), RWKV_HEAD)
    for q in range(tiles):
        vi = pl.ds(i0 + q, 1)
        s = s0_ref[q]
        for t in range(T):
            sa = jnp.sum(s * nkk_scr[t, keys, :], axis=0, keepdims=True)
            s = s * w_scr[t, keys, :] + k_scr[t, keys, :] * v_scr[t, vi, :] + b_scr[t, keys, :] * sa
            y_scr[t, vi, :] = jnp.sum(s * r_scr[t, keys, :], axis=0, keepdims=True)
        sout_ref[layer, q] = s

    @pl.when(j == pl.num_programs(0) - 1)
    def _():
        for t in range(T):
            r, k, v, _, _, g, _ = pointwise(t)
            y_ref[t * B:(t + 1) * B, :] = _rwkv_finish(y_scr[t].T, r, k, v, g, rk_ref, lng_ref, lnb_ref)


def _rwkv_step(u, shift0, s_all, s_done, p, *, batch, seq, layer):
    n = batch * seq
    srows = RWKV_HEADS * RWKV_HEAD
    params = [p[nm] for nm in _RWKV_PARAM_NAMES]
    sspec, prev_specs, sout_spec = _layer_state_specs(layer, (RWKV_STEP_TILES, RWKV_HEAD, batch), 0)
    prev_args = [s_done] if layer else []
    tposed = pltpu.VMEM((seq, GROUP_WIDTH, batch), F32)
    return pl.pallas_call(
        functools.partial(_rwkv_step_body, seq=seq, batch=batch, layer=layer),
        grid=(srows // RWKV_STEP_TILES,),
        in_specs=[_full_spec((n, RWKV_PROJ)), _full_spec((batch, RWKV_PROJ)), sspec] + prev_specs
                 + [_pspec(a) for a in params],
        out_specs=[_full_spec((n, GROUP_WIDTH)), sout_spec],
        out_shape=[jax.ShapeDtypeStruct((n, GROUP_WIDTH), F32),
                   jax.ShapeDtypeStruct((layer + 1, srows, RWKV_HEAD, batch), F32)],
        scratch_shapes=[tposed] * 7,
        compiler_params=_cparams("arbitrary"),
        name="rwkv_step",
    )(u, shift0, s_all, *prev_args, *[_parg(a) for a in params])


def _s5_body(u_ref, hre0_ref, him0_ref, are_ref, aim_ref, bmat_ref, cmat_ref, d_ref, gw_ref, gb_ref,
             y_ref, hre_ref, him_ref, hs_scr, tm_scr, *, steps, batch_major):
    c = pl.program_id(1)
    ns = S5_WIDTH
    bsub = hre_ref.shape[0]

    @pl.when(c == 0)
    def _():
        hre_ref[...] = hre0_ref[...]
        him_ref[...] = him0_ref[...]

    if batch_major:
        for b in range(bsub):
            tm_scr[:, b, :] = u_ref[b]
        u = tm_scr[...].reshape(steps * bsub, GROUP_WIDTH)
    else:
        u = u_ref[...].reshape(steps * bsub, GROUP_WIDTH)
    are = jnp.broadcast_to(are_ref[...], (bsub, ns))
    aim = jnp.broadcast_to(aim_ref[...], (bsub, ns))
    hre, him = hre_ref[...], him_ref[...]
    sub = min(S5_SUB, steps)
    rows = sub * bsub
    outs = []
    hs_scr[...] = _dot(u.astype(BF16), bmat_ref[...])
    for k in range(steps // sub):
        r0 = k * rows
        u_k = u[r0:r0 + rows]
        for t in range(sub):
            rs = slice(r0 + t * bsub, r0 + (t + 1) * bsub)
            hre, him = (are * hre - aim * him + hs_scr[rs, 0:ns], are * him + aim * hre + hs_scr[rs, ns:2 * ns])
            hs_scr[rs, 0:ns] = hre
            hs_scr[rs, ns:2 * ns] = him
        y = _dot(hs_scr[r0:r0 + rows, :].astype(BF16), cmat_ref[...]) + u_k * d_ref[...]
        y = _gelu_tanh(y)
        yy = _dot(y.astype(BF16), gw_ref[...]) + gb_ref[...]
        outs.append(yy[:, 0:GROUP_WIDTH] * _sigmoid(yy[:, GROUP_WIDTH:2 * GROUP_WIDTH]))
    hre_ref[...] = hre
    him_ref[...] = him
    out = jnp.concatenate(outs, axis=0).reshape(steps, bsub, GROUP_WIDTH)
    if batch_major:
        tm_scr[...] = out
        for b in range(bsub):
            y_ref[b] = tm_scr[:, b, :]
    else:
        y_ref[...] = out


def _time_specs(u, batch_major, chunk):
    if batch_major:
        batch, seq, _ = u.shape
        steps = min(chunk, seq)
        bsub = SUBLANES
        spec = pl.BlockSpec((bsub, steps, GROUP_WIDTH), lambda b, c: (b, c, 0))
    else:
        seq, batch, _ = u.shape
        steps = min(chunk, seq)
        bsub = min(batch, SUBLANES * max(1, chunk // steps))
        spec = pl.BlockSpec((steps, bsub, GROUP_WIDTH), lambda b, c: (c, b, 0))
    return batch, seq, steps, bsub, spec


def _s5(u, hre0, him0, lp, *, batch_major):
    batch, seq, steps, bsub, tspec = _time_specs(u, batch_major, TM_CHUNK)
    hspec =pl.BlockSpec((bsub, S5_WIDTH), lambda b, c: (b, 0))
    consts = (lp["s5_are"], lp["s5_aim"], lp["s5_bmat"], lp["s5_cmat"], lp["s5_d"], lp["s5_gw"], lp["s5_gb"])
    return pl.pallas_call(
        functools.partial(_s5_body, steps=steps, batch_major=batch_major),
        grid=(batch // bsub, seq // steps),
        in_specs=[tspec, hspec, hspec] + [_pspec(a) for a in consts],
        out_specs=[tspec, hspec, hspec],
        out_shape=[jax.ShapeDtypeStruct(u.shape, F32),
                   jax.ShapeDtypeStruct((batch, S5_WIDTH), F32),
                   jax.ShapeDtypeStruct((batch, S5_WIDTH), F32)],
        scratch_shapes=[pltpu.VMEM((steps * bsub, 2 * S5_WIDTH), F32),
                        pltpu.VMEM((steps, bsub, GROUP_WIDTH), F32)],
        compiler_params=_cparams("parallel", "arbitrary"),
        name="s5",
    )(u, hre0, him0, *[_parg(a) for a in consts])


def _pool_body(u_ref, buf0_ref, pw_ref, sc_ref, y_ref, buf_ref, f_scr, tm_scr, *, steps, pos0, batch_major):
    c = pl.program_id(1)
    bsub = f_scr.shape[1]
    GW = GROUP_WIDTH
    halo = POOL_BUF + 1

    @pl.when(c == 0)
    def _():
        f_scr[0] = jnp.zeros((bsub, GW), F32)
        f_scr[1:halo] = buf0_ref[...]

    if batch_major:
        for b in range(bsub):
            f_scr[halo:halo + steps, b, :] = u_ref[b]
    else:
        f_scr[halo:halo + steps] = u_ref[...]
    f = f_scr[...]
    u = f[halo:halo + steps]
    s2 = f[1:] + f[:-1]
    s4 = s2[2:] + s2[:-2]
    s8 = s4[4:] + s4[:-4]
    s16 = s8[8:] + s8[:-8]
    f_scr[0:halo] = f[steps:steps + halo]
    lane = lax.broadcasted_iota(jnp.int32, (steps, bsub, GW), 2)
    tpos = lax.broadcasted_iota(jnp.int32, (steps, bsub, GW), 0) + (pos0 + 1) + c * steps
    win = jnp.where(lane < POOL_CH, s2[halo - 1:halo - 1 + steps],
                    jnp.where(lane < 2 * POOL_CH, s4[halo - 3:halo - 3 + steps],
                              jnp.where(lane < 3 * POOL_CH, s8[halo - 7:halo - 7 + steps],
                                        s16[halo - 15:halo - 15 + steps])))
    wlen = jnp.where(lane < POOL_CH, POOL_WINDOWS[0],
                     jnp.where(lane < 2 * POOL_CH, POOL_WINDOWS[1],
                               jnp.where(lane < 3 * POOL_CH, POOL_WINDOWS[2], POOL_WINDOWS[3])))
    cnt = jnp.minimum(tpos, wlen).astype(F32)
    pooled = (win / cnt - u).reshape(steps * bsub, GW)
    y = (_dot(pooled.astype(BF16), pw_ref[...]) * sc_ref[...]).reshape(steps, bsub, GW)
    if batch_major:
        tm_scr[...] = y
        for b in range(bsub):
            y_ref[b] = tm_scr[:, b, :]
    else:
        y_ref[...] = y

    @pl.when(c == pl.num_programs(1) - 1)
    def _():
        buf_ref[...] = f_scr[1:halo]


def _pool(u, buf0, lp, *, pos0, batch_major, layer=None):
    batch, seq, steps, bsub, tspec = _time_specs(u, batch_major, POOL_CHUNK)
    bblock =(POOL_BUF, bsub, GROUP_WIDTH)
    bspec = pl.BlockSpec(bblock, lambda b, c: (0, b, 0))
    if layer is None:
        bspec_in = bspec
    else:
        bspec_in = pl.BlockSpec((None,) + bblock, lambda b, c: (layer, 0, b, 0))
    return pl.pallas_call(
        functools.partial(_pool_body, steps=steps, pos0=pos0, batch_major=batch_major),
        grid=(batch // bsub, seq // steps),
        in_specs=[tspec, bspec_in, _pspec(lp["pool_w"]), _pspec(lp["pool_scale"])],
        out_specs=[tspec, bspec],
        out_shape=[jax.ShapeDtypeStruct(u.shape, F32), jax.ShapeDtypeStruct((POOL_BUF, batch, GROUP_WIDTH), F32)],
        scratch_shapes=[pltpu.VMEM((POOL_BUF + 1 + steps, bsub, GROUP_WIDTH), F32),
                        pltpu.VMEM((steps, bsub, GROUP_WIDTH), F32)],
        compiler_params=_cparams("parallel", "arbitrary"),
        name="pool",
    )(u, buf0, _parg(lp["pool_w"]), _parg(lp["pool_scale"]))


def _block_diag(blocks):
    n, g, r, c = blocks.shape
    eye = jnp.eye(g, dtype=blocks.dtype)
    return (eye[None, :, None, :, None] * blocks[:, :, :, None, :]).reshape(n, g * r, g * c)


def _stacked_params(P):
    row = lambda a: a.reshape(a.shape[0], 1, -1)
    pad_lanes = lambda a: jnp.pad(a, ((0, 0), (0, LANES - a.shape[1])))
    bf = lambda a: a.astype(BF16)

    lam = lax.complex(P["s5_lam_re"], P["s5_lam_im"])
    a_bar = jnp.exp(lam * jnp.exp(P["s5_log_step"])[..., None])
    b_bar = ((a_bar - 1.0) / lam)[..., None] * lax.complex(P["s5_b_re"], P["s5_b_im"])
    b_t = jnp.swapaxes(b_bar, 2, 3)
    bmat = jnp.concatenate([_block_diag(jnp.real(b_t)), _block_diag(jnp.imag(b_t))], axis=2)
    c_t = jnp.swapaxes(lax.complex(P["s5_c_re"], P["s5_c_im"]), 2, 3)
    cmat = jnp.concatenate([_block_diag(jnp.real(c_t)), -_block_diag(jnp.imag(c_t))], axis=1)

    out = dict(
        norm_ffn1=row(P["norm_ffn1"]), ffn1_in=P["ffn1_in"], ffn1_out=P["ffn1_out"],
        norm_mix=row(P["norm_mix"]),
        w_in=jnp.transpose(P["w_in"], (2, 0, 1)),
        conv_w=P["ssd_conv_w"], conv_b=row(P["ssd_conv_b"]),
        dt_bias=row(pad_lanes(P["ssd_dt_bias"])), a_log=row(pad_lanes(P["ssd_a_log"])),
        a_neg_exp=row(jnp.repeat(-jnp.exp(P["ssd_a_log"]), SSD_HEAD_DIM, axis=1)),
        d_skip=row(jnp.repeat(P["ssd_d"], SSD_HEAD_DIM, axis=1)), ssd_norm=row(P["ssd_norm"]),
        s5_are=row(jnp.real(a_bar)), s5_aim=row(jnp.imag(a_bar)), s5_bmat=bf(bmat), s5_cmat=bf(cmat),
        s5_d=row(P["s5_d"]), s5_gw=bf(P["s5_glu_w"]), s5_gb=row(P["s5_glu_b"]),
        pool_w=bf(_block_diag(P["pool_w"])), pool_scale=row(P["pool_scale"]),
        w_out=bf(P["w_out"]),
        norm_ffn2=row(P["norm_ffn2"]), ffn2_in=P["ffn2_in"], ffn2_out=P["ffn2_out"],
    )
    for name in _RWKV_PARAM_NAMES:
        a = P["rwkv_" + name]
        out["rwkv_" + name] = bf(a) if name in ("w2", "a2", "g2") else row(a)
    return out


def _layer_params(stacked, l):
    lp = {k: _Layered((v, l)) for k, v in stacked.items()}
    lp["rwkv"] = {n: lp["rwkv_" + n] for n in _RWKV_PARAM_NAMES}
    lp["head_expand"] = jnp.pad(jnp.repeat(jnp.eye(SSD_HEADS, dtype=F32), SSD_HEAD_DIM, axis=1),
                                ((0, LANES - SSD_HEADS), (0, 0)))
    return lp


def _mixers_prompt(lp, proj, *, batch, seq):
    z, xbc, ur, us5, upool, dtr = proj
    y_ssd, conv_new, ssd_new = _ssd(z, xbc, dtr, lp, batch=batch, seq=seq)
    y_rwkv, shift_new, rwkv_new = _rwkv(ur, lp["rwkv"], batch=batch, seq=seq)
    zeros = jnp.zeros((batch, S5_WIDTH), F32)
    bm = lambda a: a.reshape(batch, seq, a.shape[-1])
    rows = lambda a: a.reshape(batch * seq, a.shape[-1])
    y_s5, s5re, s5im = _s5(bm(us5), zeros, zeros, lp, batch_major=True)
    y_pool, pool_new = _pool(bm(upool), jnp.zeros((POOL_BUF, batch, GROUP_WIDTH), F32), lp, pos0=0,
                             batch_major=True)
    ys = (y_ssd, y_rwkv, rows(y_s5), rows(y_pool))
    states = (conv_new, ssd_new, shift_new, rwkv_new, s5re.reshape(batch, S5_GROUPS, S5_STATE),
              s5im.reshape(batch, S5_GROUPS, S5_STATE), jnp.swapaxes(pool_new, 0, 1))
    return ys, states


def _mixers_decode(lp, proj, states, done, *, batch, seq, layer):
    z, xbc, ur, us5, upool, dtr = proj
    shift0, s5re0, s5im0 = (states[i][layer] for i in (2, 4, 5))
    ssd_done, rwkv_done = (done[1], done[3]) if layer else (None, None)
    y_ssd, conv_new, ssd_new = _ssd_step(z, xbc, dtr, states[0], states[1], ssd_done, lp, batch=batch, seq=seq,
                                         layer=layer)
    y_rwkv, rwkv_new = _rwkv_step(ur, shift0, states[3], rwkv_done, lp["rwkv"], batch=batch, seq=seq, layer=layer)
    shift_new = ur[(seq - 1) * batch:, :]
    tm = lambda a: a.reshape(seq, batch, a.shape[-1])
    y_s5, s5re, s5im = _s5(tm(us5), s5re0.reshape(batch, S5_WIDTH), s5im0.reshape(batch, S5_WIDTH), lp,
                           batch_major=False)
    y_pool, pool_new = _pool(tm(upool), states[6], lp, pos0=PAST_LEN, batch_major=False, layer=layer)
    rows = lambda a: a.reshape(seq * batch, a.shape[-1])
    ys = (y_ssd, y_rwkv, rows(y_s5), rows(y_pool))
    new_states = (jnp.swapaxes(conv_new, 0, 1), ssd_new, shift_new, rwkv_new,
                  s5re.reshape(batch, S5_GROUPS, S5_STATE), s5im.reshape(batch, S5_GROUPS, S5_STATE),
                  jnp.swapaxes(pool_new, 0, 1))
    return ys, new_states


_WIDTHS = (GROUP_WIDTH, SSD_CONV_DIM, RWKV_PROJ, GROUP_WIDTH, GROUP_WIDTH, LANES)


def _trunk(x_p, x_s, layer_params, norm_final, mixers_p, mixers_s):
    st_p, st_s = [], []
    mix_p, mix_s, lp = None, None, None
    for l, lp_next in enumerate(layer_params):
        if l > 0:
            x_s, wg, wu, wo = _ffn_cast(x_s, lp["norm_ffn2"], lp["ffn2_in"], lp["ffn2_out"], mix=mix_s, wmix=lp["w_out"])
            x_p = _ffn(x_p, lp["norm_ffn2"], wg, wu, wo, mix=mix_p, wmix=lp["w_out"])
        lp = lp_next
        x_s, wg, wu, wo = _ffn_cast(x_s, lp["norm_ffn1"], lp["ffn1_in"], lp["ffn1_out"])
        x_p = _ffn(x_p, lp["norm_ffn1"], wg, wu, wo)
        proj_s, w_all = _inproj_cast(x_s, lp["norm_mix"], lp["w_in"], _WIDTHS)
        mix_p, st = mixers_p(l, lp, _inproj(x_p, lp["norm_mix"], w_all, _WIDTHS), st_p[-1] if st_p else None)
        st_p.append(st)
        mix_s, st = mixers_s(l, lp, proj_s, st_s[-1] if st_s else None)
        st_s.append(st)
    x_s, wg, wu, wo = _ffn_cast(x_s, lp["norm_ffn2"], lp["ffn2_in"], lp["ffn2_out"], mix=mix_s, wmix=lp["w_out"],
                                gf=norm_final)
    x_p = _ffn(x_p, lp["norm_ffn2"], wg, wu, wo, mix=mix_p, wmix=lp["w_out"], gf=norm_final)
    return (x_p, x_s), (st_p, st_s)


def kernel(x_prompt, x_sample, state_ssd_conv, state_ssd, state_rwkv_shift, state_rwkv, state_s5_re, state_s5_im, state_pool, norm_ffn1, ffn1_in, ffn1_out, norm_mix, w_in, ssd_conv_w, ssd_conv_b, ssd_dt_bias, ssd_a_log, ssd_d, ssd_norm, rwkv_mu, rwkv_w0, rwkv_w2, rwkv_a0, rwkv_a2, rwkv_g2, rwkv_k_k, rwkv_k_a, rwkv_r_k, rwkv_ln_g, rwkv_ln_b, s5_lam_re, s5_lam_im, s5_log_step, s5_b_re, s5_b_im, s5_c_re, s5_c_im, s5_d, s5_glu_w, s5_glu_b, pool_w, pool_scale, w_out, norm_ffn2, ffn2_in, ffn2_out, norm_final):
    P = dict(norm_ffn1=norm_ffn1, ffn1_in=ffn1_in, ffn1_out=ffn1_out, norm_mix=norm_mix, w_in=w_in,
             ssd_conv_w=ssd_conv_w, ssd_conv_b=ssd_conv_b, ssd_dt_bias=ssd_dt_bias, ssd_a_log=ssd_a_log,
             ssd_d=ssd_d, ssd_norm=ssd_norm, rwkv_mu=rwkv_mu, rwkv_w0=rwkv_w0, rwkv_w2=rwkv_w2, rwkv_a0=rwkv_a0,
             rwkv_a2=rwkv_a2, rwkv_g2=rwkv_g2, rwkv_k_k=rwkv_k_k, rwkv_k_a=rwkv_k_a,
             rwkv_r_k=rwkv_r_k.reshape(rwkv_r_k.shape[0], -1), rwkv_ln_g=rwkv_ln_g, rwkv_ln_b=rwkv_ln_b,
             s5_lam_re=s5_lam_re, s5_lam_im=s5_lam_im, s5_log_step=s5_log_step, s5_b_re=s5_b_re, s5_b_im=s5_b_im,
             s5_c_re=s5_c_re, s5_c_im=s5_c_im, s5_d=s5_d, s5_glu_w=s5_glu_w, s5_glu_b=s5_glu_b, pool_w=pool_w,
             pool_scale=pool_scale, w_out=w_out, norm_ffn2=norm_ffn2, ffn2_in=ffn2_in, ffn2_out=ffn2_out)
    depth = norm_ffn1.shape[0]
    bp, tp, d = x_prompt.shape
    bs, ts, _ = x_sample.shape
    stacked = _stacked_params(P)
    layer_params = [_layer_params(stacked, l) for l in range(depth)]
    gf = norm_final.reshape(1, -1)
    sample_states = (state_ssd_conv, state_ssd, state_rwkv_shift, state_rwkv, state_s5_re, state_s5_im, state_pool)
    rwkv_rows = RWKV_HEADS * RWKV_HEAD
    decode_states = (jnp.swapaxes(state_ssd_conv, 1, 2), state_ssd, state_rwkv_shift,
                     jnp.transpose(state_rwkv, (0, 2, 3, 4, 1)).reshape(depth, rwkv_rows, RWKV_HEAD, bs),
                     state_s5_re, state_s5_im, jnp.swapaxes(state_pool, 1, 2))

    x_s = jnp.swapaxes(x_sample, 0, 1).reshape(ts * bs, d)
    (y_p, y_s), (st_p, st_s) = _trunk(
        x_prompt.reshape(bp * tp, d), x_s, layer_params, gf,
        lambda l, lp, proj, done: _mixers_prompt(lp, proj, batch=bp, seq=tp),
        lambda l, lp, proj, done: _mixers_decode(lp, proj, decode_states, done, batch=bs, seq=ts, layer=l))
    outs = [y_p.reshape(bp, tp, d), jnp.swapaxes(y_s.reshape(ts, bs, d), 0, 1)]
    for i, ref_state in enumerate(sample_states):
        outs.append(jnp.stack([st[i] for st in st_p]))
        if i == 1:
            outs.append(st_s[-1][i].reshape(ref_state.shape))
        elif i == 3:
            s_new = st_s[-1][i].reshape(depth, RWKV_HEADS, RWKV_HEAD, RWKV_HEAD, bs)
            outs.append(jnp.transpose(s_new, (0, 4, 1, 2, 3)))
        else:
            outs.append(jnp.stack([st[i] for st in st_s]))
    return tuple(outs)
```
